```python
import math
import jax
import jax.numpy as jnp
from jax import lax
import numpy as np

D_MODEL = 1024
BATCH = 8
SEQ = 4096
DEPTH = 1

A_HEADS = 8
A_HEAD_DIM = 64
A_WIDTH = A_HEADS * A_HEAD_DIM
MOBA_BLOCK = 256
MOBA_TOPK = 3
MOBA_Q_CHUNK = 32
REL_BUCKETS = 32
REL_MAX_DIST = 128
B_HEADS = 8
B_HEAD_DIM = 64
B_WIDTH = B_HEADS * B_HEAD_DIM
DECAY_LORA = 64
AAA_LORA = 64
GATE_LORA = 160
GN_EPS = 64e-5
COL_A = 3 * A_WIDTH
COL_B = 3 * B_WIDTH + DECAY_LORA + AAA_LORA + GATE_LORA
COL_G = 2 * D_MODEL
TOTAL_IN = COL_A + COL_B + COL_G
PEER_HEADS = 8
PEER_NKEYS = 128
PEER_EXPERTS = PEER_NKEYS * PEER_NKEYS
PEER_TOPK = 16
PEER_QDIM = 256
PEER_HALF = PEER_QDIM // 2
PEER_CHUNK = 128
RMS_EPS = 1e-6
NEG = -1e30

kernel_name = 'moba_rwkv7_peer_hybrid'


def rmsnorm(x, g):
    xf = x.astype(jnp.float32)
    y = xf * lax.rsqrt(jnp.mean(xf * xf, axis=-1, keepdims=True) + RMS_EPS)
    return (y * g.astype(jnp.float32)).astype(x.dtype)


def rel_bucket(dist):
    n = jnp.maximum(dist, 0)
    max_exact = REL_BUCKETS // 2
    nf = jnp.maximum(n, 1).astype(jnp.float32)
    large = max_exact + (jnp.log(nf / max_exact) / math.log(REL_MAX_DIST / max_exact)
                         * (REL_BUCKETS - max_exact)).astype(jnp.int32)
    large = jnp.minimum(large, REL_BUCKETS - 1)
    return jnp.where(n < max_exact, n, large)


def moba_attention(q, k, v, rel_bias):
    bsz, seq, heads, hd = q.shape
    n_blocks = -(-seq // MOBA_BLOCK)
    seq_pad = n_blocks * MOBA_BLOCK
    padw = ((0, 0), (0, seq_pad - seq), (0, 0), (0, 0))
    qh = jnp.pad(q, padw).transpose(0, 2, 1, 3)
    kh = jnp.pad(k, padw).transpose(0, 2, 1, 3)
    vh = jnp.pad(v, padw).transpose(0, 2, 1, 3)
    kb = kh.reshape(bsz, heads, n_blocks, MOBA_BLOCK, hd)
    vb = vh.reshape(bsz, heads, n_blocks, MOBA_BLOCK, hd)
    kbar = jnp.mean(kb.astype(jnp.float32), axis=3).astype(q.dtype)
    n_sel = min(MOBA_TOPK, n_blocks)
    scale = 1.0 / math.sqrt(hd)
    b_ix = jnp.arange(bsz)[:, None, None, None]
    h_ix = jnp.arange(heads)[None, :, None, None]
    blk_pos = jnp.arange(MOBA_BLOCK)

    def one_chunk(c):
        q0 = c * MOBA_Q_CHUNK
        blk = q0 // MOBA_BLOCK
        qc = lax.dynamic_slice_in_dim(qh, q0, MOBA_Q_CHUNK, axis=2)
        qpos = q0 + jnp.arange(MOBA_Q_CHUNK)
        k_own = lax.dynamic_slice_in_dim(kh, blk * MOBA_BLOCK, MOBA_BLOCK, axis=2)
        v_own = lax.dynamic_slice_in_dim(vh, blk * MOBA_BLOCK, MOBA_BLOCK, axis=2)
        dist_own = qpos[:, None] - (blk * MOBA_BLOCK + blk_pos)[None, :]
        s_own = (jnp.einsum('bhqd,bhkd->bhqk', qc, k_own).astype(jnp.float32) * scale
                 + rel_bias[:, rel_bucket(dist_own)].astype(jnp.float32)[None])
        s_own = jnp.where(dist_own >= 0, s_own, NEG)
        gate = jnp.einsum('bhqd,bhnd->bhqn', qc, kbar).astype(jnp.float32)
        gate = jnp.where(jnp.arange(n_blocks) < blk, gate, -jnp.inf)
        _, sel = lax.top_k(gate, n_sel)
        valid = sel < blk
        k_g = kb[b_ix, h_ix, sel]
        v_g = vb[b_ix, h_ix, sel]
        dist_sel = qpos[:, None, None] - (sel[..., None] * MOBA_BLOCK + blk_pos)
        bias_sel = rel_bias[h_ix[..., None], rel_bucket(dist_sel)].astype(jnp.float32)
        s_sel = jnp.einsum('bhqd,bhqnkd->bhqnk', qc, k_g).astype(jnp.float32) * scale + bias_sel
        s_sel = jnp.where(valid[..., None], s_sel, NEG)
        logits = jnp.concatenate(
            [s_own, s_sel.reshape(bsz, heads, MOBA_Q_CHUNK, n_sel * MOBA_BLOCK)], axis=-1)
        p = jax.nn.softmax(logits, axis=-1).astype(v.dtype)
        p_own = p[..., :MOBA_BLOCK]
        p_sel = p[..., MOBA_BLOCK:].reshape(bsz, heads, MOBA_Q_CHUNK, n_sel, MOBA_BLOCK)
        return (jnp.einsum('bhqk,bhkd->bhqd', p_own, v_own)
                + jnp.einsum('bhqnk,bhqnkd->bhqd', p_sel, v_g))

    out = lax.map(one_chunk, jnp.arange(seq_pad // MOBA_Q_CHUNK))
    out = out.transpose(1, 0, 3, 2, 4).reshape(bsz, seq_pad, heads * hd)
    return out[:, :seq]


def rwkv7_time_mix(pb, w0, w_lora_up, a0, a_lora_up, g_lora_up, k_k, k_a, r_k, lnx_g, lnx_b):
    f32 = jnp.float32
    bsz, seq, _ = pb.shape
    o1 = 3 * B_WIDTH
    r, k, v, xw, xa, xg = jnp.split(
        pb, [B_WIDTH, 2 * B_WIDTH, o1, o1 + DECAY_LORA, o1 + DECAY_LORA + AAA_LORA], axis=-1)
    w = -jax.nn.softplus(-(w0 + jnp.tanh(xw) @ w_lora_up).astype(f32)) - 0.5
    decay = jnp.exp(-jnp.exp(w))
    a = jax.nn.sigmoid((a0 + xa @ a_lora_up).astype(f32))
    g = (jax.nn.sigmoid(xg) @ g_lora_up).astype(f32)

    def heads(t):
        return t.astype(f32).reshape(bsz, seq, B_HEADS, B_HEAD_DIM)

    r, k, v, decay, a = heads(r), heads(k), heads(v), heads(decay), heads(a)
    kk = k * k_k.astype(f32).reshape(B_HEADS, B_HEAD_DIM)
    kk = kk / jnp.maximum(jnp.sqrt(jnp.sum(kk * kk, axis=-1, keepdims=True)), 1e-12)
    k = k * (1.0 + (a - 1.0) * k_a.astype(f32).reshape(B_HEADS, B_HEAD_DIM))

    def step(state, inp):
        r_t, w_t, k_t, v_t, kk_t, a_t = inp
        sk = jnp.einsum('bhij,bhj->bhi', state, kk_t)
        state = (state * w_t[:, :, None, :]
                 - sk[..., None] * (kk_t * a_t)[:, :, None, :]
                 + v_t[..., None] * k_t[:, :, None, :])
        return state, jnp.einsum('bhij,bhj->bhi', state, r_t)

    xs = tuple(jnp.moveaxis(t, 1, 0) for t in (r, decay, k, v, kk, a))
    s0 = jnp.zeros((bsz, B_HEADS, B_HEAD_DIM, B_HEAD_DIM), f32)
    _, o = lax.scan(step, s0, xs)
    o = jnp.moveaxis(o, 0, 1)
    mu = jnp.mean(o, axis=-1, keepdims=True)
    var = jnp.mean(jnp.square(o - mu), axis=-1, keepdims=True)
    o = ((o - mu) * lax.rsqrt(var + GN_EPS)).reshape(bsz, seq, B_WIDTH)
    o = o * lnx_g.astype(f32) + lnx_b.astype(f32)
    bonus = (jnp.sum(r * k * r_k.astype(f32), axis=-1, keepdims=True) * v).reshape(bsz, seq, B_WIDTH)
    return ((o + bonus) * g).astype(pb.dtype)


def hybrid_mixer(xn, w_in, rel_bias, rwkv_mu, w0, w_lora_up, a0, a_lora_up, g_lora_up,
                 k_k, k_a, r_k, lnx_g, lnx_b, w_proj_a, w_proj_b, w_out):
    bsz, seq, _ = xn.shape
    p = xn @ w_in
    pa, pb, pg = jnp.split(p, [COL_A, COL_A + COL_B], axis=-1)
    q, k, v = (t.reshape(bsz, seq, A_HEADS, A_HEAD_DIM) for t in jnp.split(pa, 3, axis=-1))
    o_a = moba_attention(q, k, v, rel_bias)
    pb_prev = jnp.pad(pb, ((0, 0), (1, 0), (0, 0)))[:, :seq]
    pb_mix = pb + rwkv_mu * (pb_prev - pb)
    o_b = rwkv7_time_mix(pb_mix, w0, w_lora_up, a0, a_lora_up, g_lora_up,
                         k_k, k_a, r_k, lnx_g, lnx_b)
    gate_a, gate_b = jnp.split(jax.nn.sigmoid(pg.astype(jnp.float32)), 2, axis=-1)
    y = (gate_a * (o_a @ w_proj_a).astype(jnp.float32)
         + gate_b * (o_b @ w_proj_b).astype(jnp.float32))
    return y.astype(xn.dtype) @ w_out


def peer_ffn(xn, peer_wq, peer_subkeys, peer_u, peer_v):
    f32 = jnp.float32
    bsz, seq, d = xn.shape
    n_tok = bsz * seq
    xt = xn.reshape(n_tok, d)
    q = (xt @ peer_wq).reshape(n_tok, PEER_HEADS, 2, PEER_HALF)
    s = jnp.einsum('thpc,hpnc->thpn', q, peer_subkeys).astype(f32)
    top_s, top_i = lax.top_k(s, PEER_TOPK)
    kk2 = PEER_TOPK * PEER_TOPK
    cand_s = (top_s[:, :, 0, :, None] + top_s[:, :, 1, None, :]).reshape(n_tok, PEER_HEADS, kk2)
    cand_i = (top_i[:, :, 0, :, None] * PEER_NKEYS + top_i[:, :, 1, None, :]).reshape(n_tok, PEER_HEADS, kk2)
    best_s, best_pos = lax.top_k(cand_s, PEER_TOPK)
    idx = jnp.take_along_axis(cand_i, best_pos, axis=-1)
    gates = jax.nn.softmax(best_s, axis=-1)
    n_ch = n_tok // PEER_CHUNK
    xr = xt.reshape(n_ch, PEER_CHUNK, d)
    ir = idx.reshape(n_ch, PEER_CHUNK, PEER_HEADS, PEER_TOPK)
    gr = gates.reshape(n_ch, PEER_CHUNK, PEER_HEADS, PEER_TOPK)

    def one_chunk(args):
        xc, ic, gc = args
        u_g = peer_u[ic]
        h = jax.nn.gelu(jnp.einsum('chkd,cd->chk', u_g, xc).astype(f32), approximate=False) * gc
        v_g = peer_v[ic]
        return jnp.einsum('chk,chkd->cd', h.astype(peer_v.dtype), v_g)

    out = lax.map(one_chunk, (xr, ir, gr))
    return out.reshape(bsz, seq, d).astype(xn.dtype)


def setup_inputs(seed: int = 0) -> dict:
    key = jax.random.key(seed)
    ks = jax.random.split(key, 26)
    f32 = jnp.float32
    L = DEPTH

    def nrm(k, shape, scale):
        return jax.random.normal(k, shape, f32) * scale

    ratio = jnp.arange(B_WIDTH, dtype=f32) / (B_WIDTH - 1)
    return {
        'x': nrm(ks[0], (BATCH, SEQ, D_MODEL), 1.0),
        'norm1_g': 1.0 + nrm(ks[1], (L, D_MODEL), 0.02),
        'w_in': nrm(ks[2], (L, D_MODEL, TOTAL_IN), D_MODEL ** -0.5),
        'rwkv_mu': jax.random.uniform(ks[3], (L, COL_B), f32),
        'w0': (-6.5 + 5.0 * ratio ** 0.85)[None, :] + nrm(ks[4], (L, B_WIDTH), 0.1),
        'w_lora_up': nrm(ks[5], (L, DECAY_LORA, B_WIDTH), 0.1),
        'a0': nrm(ks[6], (L, B_WIDTH), 0.1),
        'a_lora_up': nrm(ks[7], (L, AAA_LORA, B_WIDTH), AAA_LORA ** -0.5),
        'g_lora_up': nrm(ks[8], (L, GATE_LORA, B_WIDTH), GATE_LORA ** -0.5),
        'k_k': 0.85 + nrm(ks[9], (L, B_WIDTH), 0.05),
        'k_a': 1.0 + nrm(ks[10], (L, B_WIDTH), 0.05),
        'r_k': nrm(ks[11], (L, B_HEADS, B_HEAD_DIM), 0.1),
        'lnx_g': 1.0 + nrm(ks[12], (L, B_WIDTH), 0.02),
        'lnx_b': nrm(ks[13], (L, B_WIDTH), 0.02),
        'w_proj_a': nrm(ks[14], (L, A_WIDTH, D_MODEL), A_WIDTH ** -0.5),
        'w_proj_b': nrm(ks[15], (L, B_WIDTH, D_MODEL), B_WIDTH ** -0.5),
        'w_out': nrm(ks[16], (L, D_MODEL, D_MODEL), D_MODEL ** -0.5),
        'norm2_g': 1.0 + nrm(ks[17], (L, D_MODEL), 0.02),
        'peer_wq': nrm(ks[18], (L, D_MODEL, PEER_HEADS * PEER_QDIM), D_MODEL ** -0.5),
        'peer_subkeys': nrm(ks[19], (L, PEER_HEADS, 2, PEER_NKEYS, PEER_HALF), PEER_HALF ** -0.5),
        'peer_u': nrm(ks[20], (L, PEER_EXPERTS, D_MODEL), D_MODEL ** -0.5),
        'peer_v': nrm(ks[21], (L, PEER_EXPERTS, D_MODEL), 0.5),
        'rel_bias': nrm(ks[22], (A_HEADS, REL_BUCKETS), 0.5),
        'normf_g': 1.0 + nrm(ks[23], (D_MODEL,), 0.02),
    }


def reference(x, norm1_g, w_in, rwkv_mu, w0, w_lora_up, a0, a_lora_up, g_lora_up, k_k, k_a,
              r_k, lnx_g, lnx_b, w_proj_a, w_proj_b, w_out, norm2_g, peer_wq, peer_subkeys,
              peer_u, peer_v, rel_bias, normf_g):
    h = x
    for l in range(DEPTH):
        xn = rmsnorm(h, norm1_g[l])
        h = h + hybrid_mixer(xn, w_in[l], rel_bias, rwkv_mu[l], w0[l], w_lora_up[l], a0[l],
                             a_lora_up[l], g_lora_up[l], k_k[l], k_a[l], r_k[l], lnx_g[l],
                             lnx_b[l], w_proj_a[l], w_proj_b[l], w_out[l])
        xn = rmsnorm(h, norm2_g[l])
        h = h + peer_ffn(xn, peer_wq[l], peer_subkeys[l], peer_u[l], peer_v[l])
    return rmsnorm(h, normf_g)
```

```python
import functools
import math

import jax
import jax.numpy as jnp
from jax import lax
from jax.experimental import pallas as pl
from jax.experimental.pallas import tpu as pltpu

F32 = jnp.float32
BF16 = jnp.bfloat16
HI = lax.Precision.HIGHEST

LANES = 128
HEAD_DIM = 64
HEADS = 8
PAIRS = HEADS // 2
WIDTH = HEADS * HEAD_DIM
MOBA_BLOCK = 256
MOBA_TOPK = 3
REL_BUCKETS = 32
REL_MAX_DIST = 128
DECAY_LORA = 64
AAA_LORA = 64
GATE_LORA = 160
GN_EPS = 64e-5
RMS_EPS = 1e-6
NEG = -1e30
RWKV_CHUNK = 64
COL_A = 3 * WIDTH
COL_B_RAW = 3 * WIDTH + DECAY_LORA + AAA_LORA + GATE_LORA
COL_B = 4 * WIDTH
COL_G_OFF = COL_A + COL_B
VMEM_LIMIT = 56 * 1024 * 1024


def _cparams(sem):
    return pltpu.CompilerParams(dimension_semantics=sem, vmem_limit_bytes=VMEM_LIMIT)


def _norm_proj_kernel(x_ref, g_ref, w_ref, o_ref, xn_ref):
    @pl.when(pl.program_id(1) == 0)
    def _():
        x = x_ref[...]
        ms = jnp.mean(x * x, axis=-1, keepdims=True)
        xn_ref[...] = (x * lax.rsqrt(ms + RMS_EPS) * g_ref[...]).astype(xn_ref.dtype)

    o_ref[...] = jnp.dot(xn_ref[...], w_ref[...], preferred_element_type=F32).astype(o_ref.dtype)


def norm_proj(x2d, g, w, *, tm=512, tn=512, out_dtype=F32):
    t, d = x2d.shape
    n = w.shape[1]
    return pl.pallas_call(
        _norm_proj_kernel,
        grid=(t // tm, n // tn),
        in_specs=[
            pl.BlockSpec((tm, d), lambda i, j: (i, 0)),
            pl.BlockSpec((1, d), lambda i, j: (0, 0)),
            pl.BlockSpec((d, tn), lambda i, j: (0, j)),
        ],
        out_specs=pl.BlockSpec((tm, tn), lambda i, j: (i, j)),
        out_shape=jax.ShapeDtypeStruct((t, n), out_dtype),
        scratch_shapes=[pltpu.VMEM((tm, d), w.dtype)],
        compiler_params=_cparams(("parallel", "arbitrary")),
        name="norm_proj",
    )(x2d, g.reshape(1, d), w)


def _rel_bucket(dist):
    n = jnp.maximum(dist, 0)
    max_exact = REL_BUCKETS // 2
    nf = jnp.maximum(n, 1).astype(F32)
    large = max_exact + (jnp.log(nf / max_exact) / math.log(REL_MAX_DIST / max_exact)
                         * (REL_BUCKETS - max_exact)).astype(jnp.int32)
    large = jnp.minimum(large, REL_BUCKETS - 1)
    return jnp.where(n < max_exact, n, large)


def _moba_kernel(q_ref, k_ref, v_ref, bown_ref, bprev_ref, bfar_ref, o_ref,
                 kb_ref, vb_ref, kbar_ref, *, n_blocks):
    qb = pl.program_id(2)
    blk = MOBA_BLOCK
    scale = 1.0 / math.sqrt(HEAD_DIM)

    @pl.when(qb == 0)
    def _():
        kbar_ref[...] = jnp.zeros_like(kbar_ref)
        for n in range(n_blocks):
            kblk = k_ref[0, n * blk:(n + 1) * blk, :]
            kbar_ref[n:n + 1, :] = jnp.mean(kblk, axis=0, keepdims=True)
        kb_ref[...] = k_ref[0].astype(BF16)
        vb_ref[...] = v_ref[0].astype(BF16)

    q2 = q_ref[0]
    lane = lax.broadcasted_iota(jnp.int32, (blk, LANES), 1)
    row = lax.broadcasted_iota(jnp.int32, (blk, blk), 0)
    col = lax.broadcasted_iota(jnp.int32, (blk, blk), 1)
    own0 = pl.multiple_of(qb * blk, blk)
    k_own = kb_ref[pl.ds(own0, blk), :]
    v_own = vb_ref[pl.ds(own0, blk), :]
    prev0 = pl.multiple_of(jnp.maximum(qb - 1, 0) * blk, blk)
    k_prev = kb_ref[pl.ds(prev0, blk), :]
    v_prev = vb_ref[pl.ds(prev0, blk), :]
    nt = (((1,), (1,)), ((), ()))

    outs = []
    for hh in range(2):
        hmask = (lane >= hh * HEAD_DIM) & (lane < (hh + 1) * HEAD_DIM)
        qh = jnp.where(hmask, q2, 0.0)
        gate = lax.dot_general(qh, kbar_ref[...], nt, precision=HI, preferred_element_type=F32)
        g = jnp.where(lane < qb, gate, -jnp.inf)
        sel = []
        for _ in range(MOBA_TOPK):
            m = jnp.max(g, axis=1, keepdims=True)
            idx = jnp.min(jnp.where(g == m, lane, LANES), axis=1, keepdims=True)
            idx = jnp.where(m > -jnp.inf, idx, LANES)
            sel.append(idx)
            g = jnp.where(lane == idx, -jnp.inf, g)

        def picked(n):
            return (sel[0] == n) | (sel[1] == n) | (sel[2] == n)

        qs = (qh * scale).astype(BF16)
        s = lax.dot_general(qs, k_own, nt, preferred_element_type=F32) + bown_ref[hh]
        s = jnp.where(row >= col, s, NEG)
        m_i = jnp.max(s, axis=1, keepdims=True)
        p = jnp.exp(s - m_i)
        l_i = jnp.sum(p, axis=1, keepdims=True)
        acc = jnp.dot(p.astype(BF16), v_own, preferred_element_type=F32)

        def update(carry, s, vblk):
            m_i, l_i, acc = carry
            m_new = jnp.maximum(m_i, jnp.max(s, axis=1, keepdims=True))
            alpha = jnp.exp(m_i - m_new)
            p = jnp.exp(s - m_new)
            l_new = alpha * l_i + jnp.sum(p, axis=1, keepdims=True)
            acc_new = alpha * acc + jnp.dot(p.astype(BF16), vblk, preferred_element_type=F32)
            return m_new, l_new, acc_new

        s = lax.dot_general(qs, k_prev, nt, preferred_element_type=F32) + bprev_ref[hh]
        s = jnp.where(picked(qb - 1), s, NEG)
        carry = update((m_i, l_i, acc), s, v_prev)

        bfar = bfar_ref[hh, 0:1, 0:1]

        def body(n, carry):
            n0 = pl.multiple_of(n * blk, blk)
            kblk = kb_ref[pl.ds(n0, blk), :]
            vblk = vb_ref[pl.ds(n0, blk), :]
            s = lax.dot_general(qs, kblk, nt, preferred_element_type=F32) + bfar
            s = jnp.where(picked(n), s, NEG)
            return update(carry, s, vblk)

        m_i, l_i, acc = lax.fori_loop(0, jnp.maximum(qb - 1, 0), body, carry)
        outs.append(acc / l_i)

    o_ref[0] = jnp.where(lane < HEAD_DIM, outs[0], outs[1])


def moba_attention(p3d, rel_bias):
    bsz, seq, _ = p3d.shape
    blk = MOBA_BLOCK
    n_blocks = seq // blk
    pos = jnp.arange(blk)
    d_own = pos[:, None] - pos[None, :]
    bias_own = rel_bias[:, _rel_bucket(d_own)].astype(F32)
    bias_prev = rel_bias[:, _rel_bucket(d_own + blk)].astype(F32)
    bias_far = jnp.broadcast_to(rel_bias[:, REL_BUCKETS - 1].astype(F32)[:, None, None], (HEADS, 8, LANES))
    kern = functools.partial(_moba_kernel, n_blocks=n_blocks)
    return pl.pallas_call(
        kern,
        grid=(bsz, PAIRS, n_blocks),
        in_specs=[
            pl.BlockSpec((1, blk, LANES), lambda b, h, i: (b, i, h)),
            pl.BlockSpec((1, seq, LANES), lambda b, h, i: (b, 0, PAIRS + h)),
            pl.BlockSpec((1, seq, LANES), lambda b, h, i: (b, 0, 2 * PAIRS + h)),
            pl.BlockSpec((2, blk, blk), lambda b, h, i: (h, 0, 0)),
            pl.BlockSpec((2, blk, blk), lambda b, h, i: (h, 0, 0)),
            pl.BlockSpec((2, 8, LANES), lambda b, h, i: (h, 0, 0)),
        ],
        out_specs=pl.BlockSpec((1, blk, LANES), lambda b, h, i: (b, i, h)),
        out_shape=jax.ShapeDtypeStruct((bsz, seq, WIDTH), F32),
        scratch_shapes=[
            pltpu.VMEM((seq, LANES), BF16),
            pltpu.VMEM((seq, LANES), BF16),
            pltpu.VMEM((LANES, LANES), F32),
        ],
        compiler_params=_cparams(("parallel", "parallel", "arbitrary")),
        name="moba",
    )(p3d, p3d, p3d, bias_own, bias_prev, bias_far)


def _shifted(x, carry_row):
    rows = lax.broadcasted_iota(jnp.int32, x.shape, 0)
    return jnp.where(rows == 0, carry_row, pltpu.roll(x, 1, axis=0))


def _rwkv_prep_kernel(pr_ref, pk_ref, pv_ref, pl_ref, mu_ref, vec_ref, ww_ref, wa_ref, wg_ref,
                      bd_ref, tri_ref,
                      rt_ref, kt_ref, kd_ref, bd_out_ref, v_ref, g_ref, bonus_ref, pend_ref,
                      carry_ref, *, chunk):
    @pl.when(pl.program_id(1) == 0)
    def _():
        carry_ref[...] = jnp.zeros_like(carry_ref)

    def mix(ref, j):
        x = ref[0]
        mu = mu_ref[0:1, j * WIDTH:(j + 1) * WIDTH]
        prev = _shifted(x, carry_ref[0:1, j * WIDTH:(j + 1) * WIDTH])
        carry_ref[0:1, j * WIDTH:(j + 1) * WIDTH] = x[x.shape[0] - 1:, :]
        return x + mu * (prev - x)

    r = mix(pr_ref, 0)
    k = mix(pk_ref, 1)
    v = mix(pv_ref, 2)
    lo = mix(pl_ref, 3)
    w0, a0, k_k, k_a, r_k = (vec_ref[i:i + 1, :] for i in range(5))
    xwa = lo[:, 0:LANES]
    xg = lo[:, LANES:3 * LANES]
    lw = jnp.dot(jnp.tanh(xwa), ww_ref[...], precision=HI, preferred_element_type=F32)
    la = jnp.dot(xwa, wa_ref[...], precision=HI, preferred_element_type=F32)
    g = jnp.dot(jax.nn.sigmoid(xg), wg_ref[...], precision=HI, preferred_element_type=F32)
    z = -(w0 + lw)
    softplus = jnp.maximum(z, 0.0) + jnp.log(1.0 + jnp.exp(-jnp.abs(z)))
    logw = -jnp.exp(-softplus - 0.5)
    a = jax.nn.sigmoid(a0 + la)
    kk = k * k_k
    ss = jnp.dot(kk * kk, bd_ref[...], precision=HI, preferred_element_type=F32)
    kk = kk / jnp.maximum(jnp.sqrt(ss), 1e-12)
    k2 = k * (1.0 + (a - 1.0) * k_a)
    rk = jnp.dot(r * k2 * r_k, bd_ref[...], precision=HI, preferred_element_type=F32)
    cs = jnp.dot(tri_ref[...], logw, precision=HI, preferred_element_type=F32)
    e_pos = jnp.exp(cs)
    e_neg = jnp.exp(-cs)
    rt_ref[0] = r * e_pos
    kt_ref[0] = kk * jnp.exp(cs - logw)
    kd_ref[0] = k2 * e_neg
    bd_out_ref[0] = kk * a * e_neg
    v_ref[0] = v
    g_ref[0] = g
    bonus_ref[0] = rk * v
    ts = e_pos.shape[0]
    for c in range(ts // chunk):
        pend_ref[0, c:c + 1, :] = e_pos[(c + 1) * chunk - 1:(c + 1) * chunk, :]


def rwkv_prep(p3d, rwkv_mu, w0, w_lora_up, a0, a_lora_up, g_lora_up, k_k, k_a, r_k, *, ts=512):
    bsz, seq, _ = p3d.shape
    chunk = RWKV_CHUNK
    ts = min(ts, seq)
    mu = jnp.pad(rwkv_mu, (0, COL_B - COL_B_RAW)).reshape(1, COL_B)
    vec = jnp.stack([w0, a0, k_k, k_a, r_k.reshape(-1)] + [jnp.zeros_like(w0)] * 3).astype(F32)
    ww = jnp.zeros((LANES, WIDTH), F32).at[:DECAY_LORA].set(w_lora_up)
    wa = jnp.zeros((LANES, WIDTH), F32).at[DECAY_LORA:DECAY_LORA + AAA_LORA].set(a_lora_up)
    wg = jnp.zeros((2 * LANES, WIDTH), F32).at[:GATE_LORA].set(g_lora_up)
    hid = jnp.arange(WIDTH) // HEAD_DIM
    bd = (hid[:, None] == hid[None, :]).astype(F32)
    tix = jnp.arange(ts)
    tri = ((tix[:, None] // chunk == tix[None, :] // chunk) & (tix[None, :] <= tix[:, None])).astype(F32)
    c0 = COL_A // WIDTH
    big = jax.ShapeDtypeStruct((bsz, seq, WIDTH), F32)
    wspec = lambda shape: pl.BlockSpec(shape, lambda b, i: (0, 0))
    ospec = pl.BlockSpec((1, ts, WIDTH), lambda b, i: (b, i, 0))
    return pl.pallas_call(
        functools.partial(_rwkv_prep_kernel, chunk=chunk),
        grid=(bsz, seq // ts),
        in_specs=[
            pl.BlockSpec((1, ts, WIDTH), lambda b, i: (b, i, c0)),
            pl.BlockSpec((1, ts, WIDTH), lambda b, i: (b, i, c0 + 1)),
            pl.BlockSpec((1, ts, WIDTH), lambda b, i: (b, i, c0 + 2)),
            pl.BlockSpec((1, ts, WIDTH), lambda b, i: (b, i, c0 + 3)),
            wspec((1, COL_B)), wspec((8, WIDTH)), wspec((LANES, WIDTH)), wspec((LANES, WIDTH)),
            wspec((2 * LANES, WIDTH)), wspec((WIDTH, WIDTH)), wspec((ts, ts)),
        ],
        out_specs=[ospec] * 7 + [pl.BlockSpec((1, ts // chunk, WIDTH), lambda b, i: (b, i, 0))],
        out_shape=[big] * 7 + [jax.ShapeDtypeStruct((bsz, seq // chunk, WIDTH), F32)],
        scratch_shapes=[pltpu.VMEM((8, COL_B), F32)],
        compiler_params=_cparams(("parallel", "arbitrary")),
        name="rwkv_prep",
    )(p3d, p3d, p3d, p3d, mu, vec, ww, wa, wg, bd, tri)


def _rwkv_scan_kernel(rt_ref, kt_ref, kd_ref, bd_ref, v_ref, g_ref, bonus_ref, pend_ref, ln_ref, o_ref,
                      state_ref, *, chunk, prec):
    @pl.when(pl.program_id(1) == 0)
    def _():
        state_ref[...] = jnp.zeros_like(state_ref)

    c2 = 2 * chunk
    lane = lax.broadcasted_iota(jnp.int32, (chunk, LANES), 1)
    first = lane < HEAD_DIM
    row = lax.broadcasted_iota(jnp.int32, (c2, c2), 0)
    col = lax.broadcasted_iota(jnp.int32, (c2, c2), 1)
    eye = (row == col).astype(F32)
    hrow = lax.broadcasted_iota(jnp.int32, (LANES, LANES), 0) // HEAD_DIM
    hcol = lax.broadcasted_iota(jnp.int32, (LANES, LANES), 1) // HEAD_DIM
    head_mean = jnp.where(hrow == hcol, 1.0 / HEAD_DIM, 0.0).astype(F32)
    nt = (((1,), (1,)), ((), ()))
    tn = (((0,), (0,)), ((), ()))
    dot = functools.partial(jnp.dot, precision=prec, preferred_element_type=F32)
    dotg = functools.partial(lax.dot_general, precision=prec, preferred_element_type=F32)

    def stack(x):
        return jnp.concatenate([jnp.where(first, x, 0.0), jnp.where(first, 0.0, x)], axis=0)

    for hp in range(PAIRS):
        sl = slice(hp * LANES, (hp + 1) * LANES)
        rs, ks, kds, bs, vs = (stack(ref[0, :, sl]) for ref in (rt_ref, kt_ref, kd_ref, bd_ref, v_ref))
        pend = pend_ref[0, 0, 0:1, sl]
        big = dotg(jnp.concatenate([ks, rs], axis=0), jnp.concatenate([bs, kds], axis=0), nt)
        a_b = jnp.where(row > col, big[0:c2, 0:c2], 0.0)
        a_k = jnp.where(row > col, big[0:c2, c2:], 0.0)
        a_rb = jnp.where(row >= col, big[c2:, 0:c2], 0.0)
        a_rk = jnp.where(row >= col, big[c2:, c2:], 0.0)
        inv = eye - a_b
        pw = dot(a_b, a_b)
        n_sq = int(math.log2(chunk)) - 1
        for lvl in range(n_sq):
            inv = inv + dot(inv, pw)
            if lvl + 1 < n_sq:
                pw = dot(pw, pw)
        ht = state_ref[hp]
        rhs = dotg(ks, ht, nt) + dot(a_k, vs)
        us = dot(inv, rhs)
        os_ = dotg(rs, ht, nt) + dot(a_rk, vs) - dot(a_rb, us)
        o = os_[0:chunk] + os_[chunk:]
        state_ref[hp] = (ht + dotg(vs, kds, tn) - dotg(us, bs, tn)) * pend
        mu = jnp.dot(o, head_mean, precision=HI, preferred_element_type=F32)
        d = o - mu
        var = jnp.dot(d * d, head_mean, precision=HI, preferred_element_type=F32)
        on = d * lax.rsqrt(var + GN_EPS) * ln_ref[0:1, sl] + ln_ref[1:2, sl]
        o_ref[0, :, sl] = (on + bonus_ref[0, :, sl]) * g_ref[0, :, sl]


def rwkv_scan(rt, kt, kd, bd, v, g, bonus, pend, lnx_g, lnx_b, *, prec=HI):
    bsz, seq, _ = rt.shape
    chunk = RWKV_CHUNK
    n_chunks = seq // chunk
    ln = jnp.stack([lnx_g, lnx_b] + [jnp.zeros_like(lnx_g)] * 6).astype(F32)
    pend4 = pend.reshape(bsz, n_chunks, 1, WIDTH)
    spec = pl.BlockSpec((1, chunk, WIDTH), lambda b, c: (b, c, 0))
    return pl.pallas_call(
        functools.partial(_rwkv_scan_kernel, chunk=chunk, prec=prec),
        grid=(bsz, n_chunks),
        in_specs=[spec] * 7 + [
            pl.BlockSpec((1, 1, 1, WIDTH), lambda b, c: (b, c, 0, 0)),
            pl.BlockSpec((8, WIDTH), lambda b, c: (0, 0)),
        ],
        out_specs=spec,
        out_shape=jax.ShapeDtypeStruct((bsz, seq, WIDTH), F32),
        scratch_shapes=[pltpu.VMEM((PAIRS, LANES, LANES), F32)],
        compiler_params=_cparams(("parallel", "arbitrary")),
        name="rwkv_scan",
    )(rt, kt, kd, bd, v, g, bonus, pend4, ln)


def _merge_kernel(x_ref, oa_ref, ob_ref, ga_ref, gb_ref, wa_ref, wb_ref, wo_ref, g2_ref,
                  h_ref, xn_ref, acc_ref):
    j = pl.program_id(1)

    @pl.when(j == 0)
    def _():
        acc_ref[...] = x_ref[...]

    ya = jnp.dot(oa_ref[...].astype(BF16), wa_ref[...], preferred_element_type=F32)
    yb = jnp.dot(ob_ref[...].astype(BF16), wb_ref[...], preferred_element_type=F32)
    y = jax.nn.sigmoid(ga_ref[...]) * ya + jax.nn.sigmoid(gb_ref[...]) * yb
    acc_ref[...] += jnp.dot(y.astype(BF16), wo_ref[...], preferred_element_type=F32)

    @pl.when(j == pl.num_programs(1) - 1)
    def _():
        h = acc_ref[...]
        h_ref[...] = h
        ms = jnp.mean(h * h, axis=-1, keepdims=True)
        xn_ref[...] = h * lax.rsqrt(ms + RMS_EPS) * g2_ref[...]


def merge_out(x2d, oa, ob, p2d, w_proj_a, w_proj_b, w_out, norm2_g, *, tm=512):
    t, d = x2d.shape
    tn = WIDTH
    nj = d // tn
    g0 = COL_G_OFF // tn
    big = jax.ShapeDtypeStruct((t, d), F32)
    return pl.pallas_call(
        _merge_kernel,
        grid=(t // tm, nj),
        in_specs=[
            pl.BlockSpec((tm, d), lambda i, j: (i, 0)),
            pl.BlockSpec((tm, WIDTH), lambda i, j: (i, 0)),
            pl.BlockSpec((tm, WIDTH), lambda i, j: (i, 0)),
            pl.BlockSpec((tm, tn), lambda i, j: (i, g0 + j)),
            pl.BlockSpec((tm, tn), lambda i, j: (i, g0 + nj + j)),
            pl.BlockSpec((WIDTH, tn), lambda i, j: (0, j)),
            pl.BlockSpec((WIDTH, tn), lambda i, j: (0, j)),
            pl.BlockSpec((tn, d), lambda i, j: (j, 0)),
            pl.BlockSpec((1, d), lambda i, j: (0, 0)),
        ],
        out_specs=[pl.BlockSpec((tm, d), lambda i, j: (i, 0))] * 2,
        out_shape=[big, big],
        scratch_shapes=[pltpu.VMEM((tm, d), F32)],
        compiler_params=_cparams(("parallel", "arbitrary")),
        name="merge_out",
    )(x2d, oa, ob, p2d, p2d, w_proj_a.astype(BF16), w_proj_b.astype(BF16), w_out.astype(BF16),
      norm2_g.reshape(1, d))


PEER_HEADS = 8
PEER_NKEYS = 128
PEER_TOPK = 16
PEER_HALF = 128


def _topk_rows(s, payload, k):
    n = s.shape[0]
    rows = lax.broadcasted_iota(jnp.int32, s.shape, 0)
    vals, pays = [], []
    for _ in range(k):
        m = jnp.max(s, axis=0, keepdims=True)
        first = jnp.min(jnp.where(s == m, rows, n), axis=0, keepdims=True)
        hit = rows == first
        vals.append(m)
        pays.append(jnp.max(jnp.where(hit, payload, -1), axis=0, keepdims=True))
        s = jnp.where(hit, -jnp.inf, s)
    return jnp.concatenate(vals, axis=0), jnp.concatenate(pays, axis=0)


def _peer_route_kernel(xn_ref, wq_ref, sk_ref, idx_ref, gate_ref, *, prec):
    tt = xn_ref.shape[0]
    k = PEER_TOPK
    q = jnp.dot(xn_ref[...].astype(wq_ref.dtype), wq_ref[...], precision=prec, preferred_element_type=F32)
    key_iota = lax.broadcasted_iota(jnp.int32, (PEER_NKEYS, tt), 0)
    nt = (((1,), (1,)), ((), ()))
    idx_rows, gate_rows = [], []
    for h in range(PEER_HEADS):
        tops = []
        for p in range(2):
            c0 = (h * 2 + p) * PEER_HALF
            s = lax.dot_general(sk_ref[h, p].astype(wq_ref.dtype), q[:, c0:c0 + PEER_HALF].astype(wq_ref.dtype),
                                nt, precision=prec, preferred_element_type=F32)
            tops.append(_topk_rows(s, key_iota, k))
        (s0, i0), (s1, i1) = tops
        half = k // 2
        cs = [s0[0:1] + s1] + [s0[i:i + 1] + s1[0:half] for i in range(1, half)] + [s0[half:] + s1[0:1]]
        ci = [i0[0:1] * PEER_NKEYS + i1] + [i0[i:i + 1] * PEER_NKEYS + i1[0:half] for i in range(1, half)] \
            + [i0[half:] * PEER_NKEYS + i1[0:1]]
        best_s, best_i = _topk_rows(jnp.concatenate(cs, axis=0), jnp.concatenate(ci, axis=0), k)
        e = jnp.exp(best_s - best_s[0:1])
        gate_rows.append(e / jnp.sum(e, axis=0, keepdims=True))
        idx_rows.append(best_i)
    idx_ref[...] = jnp.concatenate(idx_rows, axis=0).T
    gate_ref[...] = jnp.concatenate(gate_rows, axis=0).T


def peer_route(xn2d, peer_wq, peer_subkeys, *, tt=256, prec=None, wdtype=BF16):
    t, d = xn2d.shape
    nq = peer_wq.shape[1]
    n_sel = PEER_HEADS * PEER_TOPK
    return pl.pallas_call(
        functools.partial(_peer_route_kernel, prec=prec),
        grid=(t // tt,),
        in_specs=[
            pl.BlockSpec((tt, d), lambda i: (i, 0)),
            pl.BlockSpec((d, nq), lambda i: (0, 0)),
            pl.BlockSpec((PEER_HEADS, 2, PEER_NKEYS, PEER_HALF), lambda i: (0, 0, 0, 0)),
        ],
        out_specs=[pl.BlockSpec((tt, n_sel), lambda i: (i, 0))] * 2,
        out_shape=[jax.ShapeDtypeStruct((t, n_sel), jnp.int32), jax.ShapeDtypeStruct((t, n_sel), F32)],
        compiler_params=_cparams(("parallel",)),
        name="peer_route",
    )(xn2d, peer_wq.astype(wdtype), peer_subkeys)


def _final_kernel(h_ref, y_ref, g_ref, o_ref):
    h = h_ref[...] + y_ref[...]
    ms = jnp.mean(h * h, axis=-1, keepdims=True)
    o_ref[...] = h * lax.rsqrt(ms + RMS_EPS) * g_ref[...]


def final_norm(h2d, y2d, g, *, tm=1024):
    t, d = h2d.shape
    spec = pl.BlockSpec((tm, d), lambda i: (i, 0))
    return pl.pallas_call(
        _final_kernel,
        grid=(t // tm,),
        in_specs=[spec, spec, pl.BlockSpec((1, d), lambda i: (0, 0))],
        out_specs=spec,
        out_shape=jax.ShapeDtypeStruct((t, d), F32),
        compiler_params=_cparams(("parallel",)),
        name="final_norm",
    )(h2d, y2d, g.reshape(1, d))


def _peer_experts_debug(xn2d, idx, gates, peer_u, peer_v):
    t, d = xn2d.shape
    ch = 128
    def one(args):
        xc, ic, gc = args
        h = jax.nn.gelu(jnp.einsum('ckd,cd->ck', peer_u[ic], xc), approximate=False) * gc
        return jnp.einsum('ck,ckd->cd', h, peer_v[ic])
    out = lax.map(one, (xn2d.reshape(t // ch, ch, d), idx.reshape(t // ch, ch, -1), gates.reshape(t // ch, ch, -1)))
    return out.reshape(t, d)


def kernel(x, norm1_g, w_in, rwkv_mu, w0, w_lora_up, a0, a_lora_up, g_lora_up, k_k, k_a, r_k, lnx_g, lnx_b,
           w_proj_a, w_proj_b, w_out, norm2_g, peer_wq, peer_subkeys, peer_u, peer_v, rel_bias, normf_g):
    bsz, seq, d = x.shape
    t = bsz * seq
    depth = norm1_g.shape[0]
    h2d = x.reshape(t, d)
    y2d = None
    for l in range(depth):
        if y2d is not None:
            h2d = h2d + y2d
        w_pad = jnp.concatenate([
            w_in[l][:, :COL_A + COL_B_RAW],
            jnp.zeros((d, COL_B - COL_B_RAW), w_in.dtype),
            w_in[l][:, COL_A + COL_B_RAW:]], axis=1).astype(BF16)
        p2d = norm_proj(h2d, norm1_g[l], w_pad)
        p3d = p2d.reshape(bsz, seq, -1)
        oa = moba_attention(p3d, rel_bias)
        prep = rwkv_prep(p3d, rwkv_mu[l], w0[l], w_lora_up[l], a0[l], a_lora_up[l], g_lora_up[l],
                         k_k[l], k_a[l], r_k[l])
        ob = rwkv_scan(*prep, lnx_g[l], lnx_b[l])
        h2d, xn2 = merge_out(h2d, oa.reshape(t, WIDTH), ob.reshape(t, WIDTH), p2d,
                             w_proj_a[l], w_proj_b[l], w_out[l], norm2_g[l])
        idx, gates = peer_route(xn2, peer_wq[l], peer_subkeys[l])
        y2d = _peer_experts_debug(xn2, idx, gates, peer_u[l], peer_v[l])
    return final_norm(h2d, y2d, normf_g).reshape(bsz, seq, d)
```

```python
import functools
import math

import jax
import jax.numpy as jnp
from jax import lax
from jax.experimental import pallas as pl
from jax.experimental.pallas import tpu as pltpu

F32 = jnp.float32
BF16 = jnp.bfloat16
HI = lax.Precision.HIGHEST

LANES = 128
HEAD_DIM = 64
HEADS = 8
PAIRS = HEADS // 2
WIDTH = HEADS * HEAD_DIM
MOBA_BLOCK = 256
MOBA_TOPK = 3
REL_BUCKETS = 32
REL_MAX_DIST = 128
DECAY_LORA = 64
AAA_LORA = 64
GATE_LORA = 160
GN_EPS = 64e-5
RMS_EPS = 1e-6
NEG = -1e30
RWKV_CHUNK = 64
COL_A = 3 * WIDTH
COL_B_RAW = 3 * WIDTH + DECAY_LORA + AAA_LORA + GATE_LORA
COL_B = 4 * WIDTH
COL_G_OFF = COL_A + COL_B
VMEM_LIMIT = 56 * 1024 * 1024


def _cparams(sem):
    return pltpu.CompilerParams(dimension_semantics=sem, vmem_limit_bytes=VMEM_LIMIT)


def _norm_proj_kernel(x_ref, g_ref, w_ref, o_ref, xn_ref):
    @pl.when(pl.program_id(1) == 0)
    def _():
        x = x_ref[...]
        ms = jnp.mean(x * x, axis=-1, keepdims=True)
        xn_ref[...] = (x * lax.rsqrt(ms + RMS_EPS) * g_ref[...]).astype(xn_ref.dtype)

    o_ref[...] = jnp.dot(xn_ref[...], w_ref[...], preferred_element_type=F32).astype(o_ref.dtype)


def norm_proj(x2d, g, w, *, tm=512, tn=512, out_dtype=F32):
    t, d = x2d.shape
    n = w.shape[1]
    return pl.pallas_call(
        _norm_proj_kernel,
        grid=(t // tm, n // tn),
        in_specs=[
            pl.BlockSpec((tm, d), lambda i, j: (i, 0)),
            pl.BlockSpec((1, d), lambda i, j: (0, 0)),
            pl.BlockSpec((d, tn), lambda i, j: (0, j)),
        ],
        out_specs=pl.BlockSpec((tm, tn), lambda i, j: (i, j)),
        out_shape=jax.ShapeDtypeStruct((t, n), out_dtype),
        scratch_shapes=[pltpu.VMEM((tm, d), w.dtype)],
        compiler_params=_cparams(("parallel", "arbitrary")),
        name="norm_proj",
    )(x2d, g.reshape(1, d), w)


def _rel_bucket(dist):
    n = jnp.maximum(dist, 0)
    max_exact = REL_BUCKETS // 2
    nf = jnp.maximum(n, 1).astype(F32)
    large = max_exact + (jnp.log(nf / max_exact) / math.log(REL_MAX_DIST / max_exact)
                         * (REL_BUCKETS - max_exact)).astype(jnp.int32)
    large = jnp.minimum(large, REL_BUCKETS - 1)
    return jnp.where(n < max_exact, n, large)


def _moba_kernel(q_ref, k_ref, v_ref, bown_ref, bprev_ref, bfar_ref, o_ref,
                 kb_ref, vb_ref, kbar_ref, *, n_blocks):
    qb = pl.program_id(2)
    blk = MOBA_BLOCK
    scale = 1.0 / math.sqrt(HEAD_DIM)

    @pl.when(qb == 0)
    def _():
        kbar_ref[...] = jnp.zeros_like(kbar_ref)
        for n in range(n_blocks):
            kblk = k_ref[0, n * blk:(n + 1) * blk, :]
            kbar_ref[n:n + 1, :] = jnp.mean(kblk, axis=0, keepdims=True)
        kb_ref[...] = k_ref[0].astype(BF16)
        vb_ref[...] = v_ref[0].astype(BF16)

    q2 = q_ref[0]
    lane = lax.broadcasted_iota(jnp.int32, (blk, LANES), 1)
    row = lax.broadcasted_iota(jnp.int32, (blk, blk), 0)
    col = lax.broadcasted_iota(jnp.int32, (blk, blk), 1)
    own0 = pl.multiple_of(qb * blk, blk)
    k_own = kb_ref[pl.ds(own0, blk), :]
    v_own = vb_ref[pl.ds(own0, blk), :]
    prev0 = pl.multiple_of(jnp.maximum(qb - 1, 0) * blk, blk)
    k_prev = kb_ref[pl.ds(prev0, blk), :]
    v_prev = vb_ref[pl.ds(prev0, blk), :]
    nt = (((1,), (1,)), ((), ()))

    outs = []
    for hh in range(2):
        hmask = (lane >= hh * HEAD_DIM) & (lane < (hh + 1) * HEAD_DIM)
        qh = jnp.where(hmask, q2, 0.0)
        gate = lax.dot_general(qh, kbar_ref[...], nt, precision=HI, preferred_element_type=F32)
        g = jnp.where(lane < qb, gate, -jnp.inf)
        sel = []
        for _ in range(MOBA_TOPK):
            m = jnp.max(g, axis=1, keepdims=True)
            idx = jnp.min(jnp.where(g == m, lane, LANES), axis=1, keepdims=True)
            idx = jnp.where(m > -jnp.inf, idx, LANES)
            sel.append(idx)
            g = jnp.where(lane == idx, -jnp.inf, g)

        def picked(n):
            return (sel[0] == n) | (sel[1] == n) | (sel[2] == n)

        qs = (qh * scale).astype(BF16)
        s = lax.dot_general(qs, k_own, nt, preferred_element_type=F32) + bown_ref[hh]
        s = jnp.where(row >= col, s, NEG)
        m_i = jnp.max(s, axis=1, keepdims=True)
        p = jnp.exp(s - m_i)
        l_i = jnp.sum(p, axis=1, keepdims=True)
        acc = jnp.dot(p.astype(BF16), v_own, preferred_element_type=F32)

        def update(carry, s, vblk):
            m_i, l_i, acc = carry
            m_new = jnp.maximum(m_i, jnp.max(s, axis=1, keepdims=True))
            alpha = jnp.exp(m_i - m_new)
            p = jnp.exp(s - m_new)
            l_new = alpha * l_i + jnp.sum(p, axis=1, keepdims=True)
            acc_new = alpha * acc + jnp.dot(p.astype(BF16), vblk, preferred_element_type=F32)
            return m_new, l_new, acc_new

        s = lax.dot_general(qs, k_prev, nt, preferred_element_type=F32) + bprev_ref[hh]
        s = jnp.where(picked(qb - 1), s, NEG)
        carry = update((m_i, l_i, acc), s, v_prev)

        bfar = bfar_ref[hh, 0:1, 0:1]

        def body(n, carry):
            n0 = pl.multiple_of(n * blk, blk)
            kblk = kb_ref[pl.ds(n0, blk), :]
            vblk = vb_ref[pl.ds(n0, blk), :]
            s = lax.dot_general(qs, kblk, nt, preferred_element_type=F32) + bfar
            s = jnp.where(picked(n), s, NEG)
            return update(carry, s, vblk)

        m_i, l_i, acc = lax.fori_loop(0, jnp.maximum(qb - 1, 0), body, carry)
        outs.append(acc / l_i)

    o_ref[0] = jnp.where(lane < HEAD_DIM, outs[0], outs[1])


def moba_attention(p3d, rel_bias):
    bsz, seq, _ = p3d.shape
    blk = MOBA_BLOCK
    n_blocks = seq // blk
    pos = jnp.arange(blk)
    d_own = pos[:, None] - pos[None, :]
    bias_own = rel_bias[:, _rel_bucket(d_own)].astype(F32)
    bias_prev = rel_bias[:, _rel_bucket(d_own + blk)].astype(F32)
    bias_far = jnp.broadcast_to(rel_bias[:, REL_BUCKETS - 1].astype(F32)[:, None, None], (HEADS, 8, LANES))
    kern = functools.partial(_moba_kernel, n_blocks=n_blocks)
    return pl.pallas_call(
        kern,
        grid=(bsz, PAIRS, n_blocks),
        in_specs=[
            pl.BlockSpec((1, blk, LANES), lambda b, h, i: (b, i, h)),
            pl.BlockSpec((1, seq, LANES), lambda b, h, i: (b, 0, PAIRS + h)),
            pl.BlockSpec((1, seq, LANES), lambda b, h, i: (b, 0, 2 * PAIRS + h)),
            pl.BlockSpec((2, blk, blk), lambda b, h, i: (h, 0, 0)),
            pl.BlockSpec((2, blk, blk), lambda b, h, i: (h, 0, 0)),
            pl.BlockSpec((2, 8, LANES), lambda b, h, i: (h, 0, 0)),
        ],
        out_specs=pl.BlockSpec((1, blk, LANES), lambda b, h, i: (b, i, h)),
        out_shape=jax.ShapeDtypeStruct((bsz, seq, WIDTH), F32),
        scratch_shapes=[
            pltpu.VMEM((seq, LANES), BF16),
            pltpu.VMEM((seq, LANES), BF16),
            pltpu.VMEM((LANES, LANES), F32),
        ],
        compiler_params=_cparams(("parallel", "parallel", "arbitrary")),
        name="moba",
    )(p3d, p3d, p3d, bias_own, bias_prev, bias_far)


def _shifted(x, carry_row):
    rows = lax.broadcasted_iota(jnp.int32, x.shape, 0)
    return jnp.where(rows == 0, carry_row, pltpu.roll(x, 1, axis=0))


def _rwkv_prep_kernel(pr_ref, pk_ref, pv_ref, pl_ref, mu_ref, vec_ref, ww_ref, wa_ref, wg_ref,
                      bd_ref, tri_ref,
                      rt_ref, kt_ref, kd_ref, bd_out_ref, v_ref, g_ref, bonus_ref, pend_ref,
                      carry_ref, *, chunk):
    @pl.when(pl.program_id(1) == 0)
    def _():
        carry_ref[...] = jnp.zeros_like(carry_ref)

    def mix(ref, j):
        x = ref[0]
        mu = mu_ref[0:1, j * WIDTH:(j + 1) * WIDTH]
        prev = _shifted(x, carry_ref[0:1, j * WIDTH:(j + 1) * WIDTH])
        carry_ref[0:1, j * WIDTH:(j + 1) * WIDTH] = x[x.shape[0] - 1:, :]
        return x + mu * (prev - x)

    r = mix(pr_ref, 0)
    k = mix(pk_ref, 1)
    v = mix(pv_ref, 2)
    lo = mix(pl_ref, 3)
    w0, a0, k_k, k_a, r_k = (vec_ref[i:i + 1, :] for i in range(5))
    xwa = lo[:, 0:LANES]
    xg = lo[:, LANES:3 * LANES]
    lw = jnp.dot(jnp.tanh(xwa), ww_ref[...], precision=HI, preferred_element_type=F32)
    la = jnp.dot(xwa, wa_ref[...], precision=HI, preferred_element_type=F32)
    g = jnp.dot(jax.nn.sigmoid(xg), wg_ref[...], precision=HI, preferred_element_type=F32)
    z = -(w0 + lw)
    softplus = jnp.maximum(z, 0.0) + jnp.log(1.0 + jnp.exp(-jnp.abs(z)))
    logw = -jnp.exp(-softplus - 0.5)
    a = jax.nn.sigmoid(a0 + la)
    kk = k * k_k
    ss = jnp.dot(kk * kk, bd_ref[...], precision=HI, preferred_element_type=F32)
    kk = kk / jnp.maximum(jnp.sqrt(ss), 1e-12)
    k2 = k * (1.0 + (a - 1.0) * k_a)
    rk = jnp.dot(r * k2 * r_k, bd_ref[...], precision=HI, preferred_element_type=F32)
    cs = jnp.dot(tri_ref[...], logw, precision=HI, preferred_element_type=F32)
    e_pos = jnp.exp(cs)
    e_neg = jnp.exp(-cs)
    rt_ref[0] = r * e_pos
    kt_ref[0] = kk * jnp.exp(cs - logw)
    kd_ref[0] = k2 * e_neg
    bd_out_ref[0] = kk * a * e_neg
    v_ref[0] = v
    g_ref[0] = g
    bonus_ref[0] = rk * v
    ts = e_pos.shape[0]
    for c in range(ts // chunk):
        pend_ref[0, c:c + 1, :] = e_pos[(c + 1) * chunk - 1:(c + 1) * chunk, :]


def rwkv_prep(p3d, rwkv_mu, w0, w_lora_up, a0, a_lora_up, g_lora_up, k_k, k_a, r_k, *, ts=512):
    bsz, seq, _ = p3d.shape
    chunk = RWKV_CHUNK
    ts = min(ts, seq)
    mu = jnp.pad(rwkv_mu, (0, COL_B - COL_B_RAW)).reshape(1, COL_B)
    vec = jnp.stack([w0, a0, k_k, k_a, r_k.reshape(-1)] + [jnp.zeros_like(w0)] * 3).astype(F32)
    ww = jnp.zeros((LANES, WIDTH), F32).at[:DECAY_LORA].set(w_lora_up)
    wa = jnp.zeros((LANES, WIDTH), F32).at[DECAY_LORA:DECAY_LORA + AAA_LORA].set(a_lora_up)
    wg = jnp.zeros((2 * LANES, WIDTH), F32).at[:GATE_LORA].set(g_lora_up)
    hid = jnp.arange(WIDTH) // HEAD_DIM
    bd = (hid[:, None] == hid[None, :]).astype(F32)
    tix = jnp.arange(ts)
    tri = ((tix[:, None] // chunk == tix[None, :] // chunk) & (tix[None, :] <= tix[:, None])).astype(F32)
    c0 = COL_A // WIDTH
    big = jax.ShapeDtypeStruct((bsz, seq, WIDTH), F32)
    wspec = lambda shape: pl.BlockSpec(shape, lambda b, i: (0, 0))
    ospec = pl.BlockSpec((1, ts, WIDTH), lambda b, i: (b, i, 0))
    return pl.pallas_call(
        functools.partial(_rwkv_prep_kernel, chunk=chunk),
        grid=(bsz, seq // ts),
        in_specs=[
            pl.BlockSpec((1, ts, WIDTH), lambda b, i: (b, i, c0)),
            pl.BlockSpec((1, ts, WIDTH), lambda b, i: (b, i, c0 + 1)),
            pl.BlockSpec((1, ts, WIDTH), lambda b, i: (b, i, c0 + 2)),
            pl.BlockSpec((1, ts, WIDTH), lambda b, i: (b, i, c0 + 3)),
            wspec((1, COL_B)), wspec((8, WIDTH)), wspec((LANES, WIDTH)), wspec((LANES, WIDTH)),
            wspec((2 * LANES, WIDTH)), wspec((WIDTH, WIDTH)), wspec((ts, ts)),
        ],
        out_specs=[ospec] * 7 + [pl.BlockSpec((1, ts // chunk, WIDTH), lambda b, i: (b, i, 0))],
        out_shape=[big] * 7 + [jax.ShapeDtypeStruct((bsz, seq // chunk, WIDTH), F32)],
        scratch_shapes=[pltpu.VMEM((8, COL_B), F32)],
        compiler_params=_cparams(("parallel", "arbitrary")),
        name="rwkv_prep",
    )(p3d, p3d, p3d, p3d, mu, vec, ww, wa, wg, bd, tri)


def _rwkv_scan_kernel(rt_ref, kt_ref, kd_ref, bd_ref, v_ref, g_ref, bonus_ref, pend_ref, ln_ref, o_ref,
                      state_ref, *, chunk, prec):
    @pl.when(pl.program_id(1) == 0)
    def _():
        state_ref[...] = jnp.zeros_like(state_ref)

    c2 = 2 * chunk
    lane = lax.broadcasted_iota(jnp.int32, (chunk, LANES), 1)
    first = lane < HEAD_DIM
    row = lax.broadcasted_iota(jnp.int32, (c2, c2), 0)
    col = lax.broadcasted_iota(jnp.int32, (c2, c2), 1)
    eye = (row == col).astype(F32)
    hrow = lax.broadcasted_iota(jnp.int32, (LANES, LANES), 0) // HEAD_DIM
    hcol = lax.broadcasted_iota(jnp.int32, (LANES, LANES), 1) // HEAD_DIM
    head_mean = jnp.where(hrow == hcol, 1.0 / HEAD_DIM, 0.0).astype(F32)
    nt = (((1,), (1,)), ((), ()))
    tn = (((0,), (0,)), ((), ()))
    dot = functools.partial(jnp.dot, precision=prec, preferred_element_type=F32)
    dotg = functools.partial(lax.dot_general, precision=prec, preferred_element_type=F32)

    def stack(x):
        return jnp.concatenate([jnp.where(first, x, 0.0), jnp.where(first, 0.0, x)], axis=0)

    for hp in range(PAIRS):
        sl = slice(hp * LANES, (hp + 1) * LANES)
        rs, ks, kds, bs, vs = (stack(ref[0, :, sl]) for ref in (rt_ref, kt_ref, kd_ref, bd_ref, v_ref))
        pend = pend_ref[0, 0, 0:1, sl]
        big = dotg(jnp.concatenate([ks, rs], axis=0), jnp.concatenate([bs, kds], axis=0), nt)
        a_b = jnp.where(row > col, big[0:c2, 0:c2], 0.0)
        a_k = jnp.where(row > col, big[0:c2, c2:], 0.0)
        a_rb = jnp.where(row >= col, big[c2:, 0:c2], 0.0)
        a_rk = jnp.where(row >= col, big[c2:, c2:], 0.0)
        inv = eye - a_b
        pw = dot(a_b, a_b)
        n_sq = int(math.log2(chunk)) - 1
        for lvl in range(n_sq):
            inv = inv + dot(inv, pw)
            if lvl + 1 < n_sq:
                pw = dot(pw, pw)
        ht = state_ref[hp]
        rhs = dotg(ks, ht, nt) + dot(a_k, vs)
        us = dot(inv, rhs)
        os_ = dotg(rs, ht, nt) + dot(a_rk, vs) - dot(a_rb, us)
        o = os_[0:chunk] + os_[chunk:]
        state_ref[hp] = (ht + dotg(vs, kds, tn) - dotg(us, bs, tn)) * pend
        mu = jnp.dot(o, head_mean, precision=HI, preferred_element_type=F32)
        d = o - mu
        var = jnp.dot(d * d, head_mean, precision=HI, preferred_element_type=F32)
        on = d * lax.rsqrt(var + GN_EPS) * ln_ref[0:1, sl] + ln_ref[1:2, sl]
        o_ref[0, :, sl] = (on + bonus_ref[0, :, sl]) * g_ref[0, :, sl]


def rwkv_scan(rt, kt, kd, bd, v, g, bonus, pend, lnx_g, lnx_b, *, prec=HI):
    bsz, seq, _ = rt.shape
    chunk = RWKV_CHUNK
    n_chunks = seq // chunk
    ln = jnp.stack([lnx_g, lnx_b] + [jnp.zeros_like(lnx_g)] * 6).astype(F32)
    pend4 = pend.reshape(bsz, n_chunks, 1, WIDTH)
    spec = pl.BlockSpec((1, chunk, WIDTH), lambda b, c: (b, c, 0))
    return pl.pallas_call(
        functools.partial(_rwkv_scan_kernel, chunk=chunk, prec=prec),
        grid=(bsz, n_chunks),
        in_specs=[spec] * 7 + [
            pl.BlockSpec((1, 1, 1, WIDTH), lambda b, c: (b, c, 0, 0)),
            pl.BlockSpec((8, WIDTH), lambda b, c: (0, 0)),
        ],
        out_specs=spec,
        out_shape=jax.ShapeDtypeStruct((bsz, seq, WIDTH), F32),
        scratch_shapes=[pltpu.VMEM((PAIRS, LANES, LANES), F32)],
        compiler_params=_cparams(("parallel", "arbitrary")),
        name="rwkv_scan",
    )(rt, kt, kd, bd, v, g, bonus, pend4, ln)


def _merge_kernel(x_ref, oa_ref, ob_ref, ga_ref, gb_ref, wa_ref, wb_ref, wo_ref, g2_ref,
                  h_ref, xn_ref, acc_ref):
    j = pl.program_id(1)

    @pl.when(j == 0)
    def _():
        acc_ref[...] = x_ref[...]

    ya = jnp.dot(oa_ref[...].astype(BF16), wa_ref[...], preferred_element_type=F32)
    yb = jnp.dot(ob_ref[...].astype(BF16), wb_ref[...], preferred_element_type=F32)
    y = jax.nn.sigmoid(ga_ref[...]) * ya + jax.nn.sigmoid(gb_ref[...]) * yb
    acc_ref[...] += jnp.dot(y.astype(BF16), wo_ref[...], preferred_element_type=F32)

    @pl.when(j == pl.num_programs(1) - 1)
    def _():
        h = acc_ref[...]
        h_ref[...] = h
        ms = jnp.mean(h * h, axis=-1, keepdims=True)
        xn_ref[...] = h * lax.rsqrt(ms + RMS_EPS) * g2_ref[...]


def merge_out(x2d, oa, ob, p2d, w_proj_a, w_proj_b, w_out, norm2_g, *, tm=512):
    t, d = x2d.shape
    tn = WIDTH
    nj = d // tn
    g0 = COL_G_OFF // tn
    big = jax.ShapeDtypeStruct((t, d), F32)
    return pl.pallas_call(
        _merge_kernel,
        grid=(t // tm, nj),
        in_specs=[
            pl.BlockSpec((tm, d), lambda i, j: (i, 0)),
            pl.BlockSpec((tm, WIDTH), lambda i, j: (i, 0)),
            pl.BlockSpec((tm, WIDTH), lambda i, j: (i, 0)),
            pl.BlockSpec((tm, tn), lambda i, j: (i, g0 + j)),
            pl.BlockSpec((tm, tn), lambda i, j: (i, g0 + nj + j)),
            pl.BlockSpec((WIDTH, tn), lambda i, j: (0, j)),
            pl.BlockSpec((WIDTH, tn), lambda i, j: (0, j)),
            pl.BlockSpec((tn, d), lambda i, j: (j, 0)),
            pl.BlockSpec((1, d), lambda i, j: (0, 0)),
        ],
        out_specs=[pl.BlockSpec((tm, d), lambda i, j: (i, 0))] * 2,
        out_shape=[big, big],
        scratch_shapes=[pltpu.VMEM((tm, d), F32)],
        compiler_params=_cparams(("parallel", "arbitrary")),
        name="merge_out",
    )(x2d, oa, ob, p2d, p2d, w_proj_a.astype(BF16), w_proj_b.astype(BF16), w_out.astype(BF16),
      norm2_g.reshape(1, d))


PEER_HEADS = 8
PEER_NKEYS = 128
PEER_TOPK = 16
PEER_HALF = 128


def _topk_rows(s, payload, k):
    n = s.shape[0]
    rows = lax.broadcasted_iota(jnp.int32, s.shape, 0)
    vals, pays = [], []
    for _ in range(k):
        m = jnp.max(s, axis=0, keepdims=True)
        first = jnp.min(jnp.where(s == m, rows, n), axis=0, keepdims=True)
        hit = rows == first
        vals.append(m)
        pays.append(jnp.max(jnp.where(hit, payload, -1), axis=0, keepdims=True))
        s = jnp.where(hit, -jnp.inf, s)
    return jnp.concatenate(vals, axis=0), jnp.concatenate(pays, axis=0)


def _peer_route_kernel(xn_ref, wq_ref, sk_ref, idx_ref, gate_ref, *, prec):
    tt = xn_ref.shape[0]
    k = PEER_TOPK
    q = jnp.dot(xn_ref[...].astype(wq_ref.dtype), wq_ref[...], precision=prec, preferred_element_type=F32)
    key_iota = lax.broadcasted_iota(jnp.int32, (PEER_NKEYS, tt), 0)
    nt = (((1,), (1,)), ((), ()))
    idx_rows, gate_rows = [], []
    for h in range(PEER_HEADS):
        tops = []
        for p in range(2):
            c0 = (h * 2 + p) * PEER_HALF
            s = lax.dot_general(sk_ref[h, p].astype(wq_ref.dtype), q[:, c0:c0 + PEER_HALF].astype(wq_ref.dtype),
                                nt, precision=prec, preferred_element_type=F32)
            tops.append(_topk_rows(s, key_iota, k))
        (s0, i0), (s1, i1) = tops
        half = k // 2
        cs = [s0[0:1] + s1] + [s0[i:i + 1] + s1[0:half] for i in range(1, half)] + [s0[half:] + s1[0:1]]
        ci = [i0[0:1] * PEER_NKEYS + i1] + [i0[i:i + 1] * PEER_NKEYS + i1[0:half] for i in range(1, half)] \
            + [i0[half:] * PEER_NKEYS + i1[0:1]]
        best_s, best_i = _topk_rows(jnp.concatenate(cs, axis=0), jnp.concatenate(ci, axis=0), k)
        e = jnp.exp(best_s - best_s[0:1])
        gate_rows.append(e / jnp.sum(e, axis=0, keepdims=True))
        idx_rows.append(best_i)
    idx_ref[...] = jnp.concatenate(idx_rows, axis=0).T
    gate_ref[...] = jnp.concatenate(gate_rows, axis=0).T


def peer_route(xn2d, peer_wq, peer_subkeys, *, tt=256, prec=None, wdtype=BF16):
    t, d = xn2d.shape
    nq = peer_wq.shape[1]
    n_sel = PEER_HEADS * PEER_TOPK
    return pl.pallas_call(
        functools.partial(_peer_route_kernel, prec=prec),
        grid=(t // tt,),
        in_specs=[
            pl.BlockSpec((tt, d), lambda i: (i, 0)),
            pl.BlockSpec((d, nq), lambda i: (0, 0)),
            pl.BlockSpec((PEER_HEADS, 2, PEER_NKEYS, PEER_HALF), lambda i: (0, 0, 0, 0)),
        ],
        out_specs=[pl.BlockSpec((tt, n_sel), lambda i: (i, 0))] * 2,
        out_shape=[jax.ShapeDtypeStruct((t, n_sel), jnp.int32), jax.ShapeDtypeStruct((t, n_sel), F32)],
        compiler_params=_cparams(("parallel",)),
        name="peer_route",
    )(xn2d, peer_wq.astype(wdtype), peer_subkeys)


def _final_kernel(h_ref, y_ref, g_ref, o_ref):
    h = h_ref[...] + y_ref[...]
    ms = jnp.mean(h * h, axis=-1, keepdims=True)
    o_ref[...] = h * lax.rsqrt(ms + RMS_EPS) * g_ref[...]


def final_norm(h2d, y2d, g, *, tm=1024):
    t, d = h2d.shape
    spec = pl.BlockSpec((tm, d), lambda i: (i, 0))
    return pl.pallas_call(
        _final_kernel,
        grid=(t // tm,),
        in_specs=[spec, spec, pl.BlockSpec((1, d), lambda i: (0, 0))],
        out_specs=spec,
        out_shape=jax.ShapeDtypeStruct((t, d), F32),
        compiler_params=_cparams(("parallel",)),
        name="final_norm",
    )(h2d, y2d, g.reshape(1, d))


SC_CORES = 2
SC_SUBCORES = 16
SC_LANES = 16
SC_WORKERS = SC_CORES * SC_SUBCORES
PEER_SEL = PEER_HEADS * PEER_TOPK
PEER_ROWS = 64
PEER_GROUP = 32


def _pack_rows(w):
    half = w.shape[1] // 2
    bits = lax.bitcast_convert_type(w.astype(BF16), jnp.uint16).astype(jnp.uint32)
    return lax.bitcast_convert_type(bits[:, :half] | (bits[:, half:] << 16), jnp.int32)


def _unpack_words(w):
    lo = lax.bitcast_convert_type(lax.shift_left(w, jnp.int32(16)), F32)
    hi = lax.bitcast_convert_type(lax.bitwise_and(w, jnp.int32(-65536)), F32)
    return lo, hi


def _sc_mesh():
    from jax.experimental.pallas import tpu_sc as plsc
    return plsc.VectorSubcoreMesh(core_axis_name="c", subcore_axis_name="s",
                                  num_cores=SC_CORES, num_subcores=SC_SUBCORES)


def _sc_loop(n, body, carry):
    from jax.experimental.pallas import tpu_sc as plsc
    return plsc.parallel_loop(0, n, carry=carry)(body)


def _worker_base(tokens_per_worker):
    return (lax.axis_index("s") * SC_CORES + lax.axis_index("c")) * tokens_per_worker


def peer_expert_dots(xn2d, idx2, u_packed):
    t, d = xn2d.shape
    half = d // 2
    n_chunks = half // SC_LANES
    tpw = t // SC_WORKERS
    grp = min(PEER_GROUP, tpw)
    rg_rows = 32

    def body(x_hbm, idx_hbm, u_hbm, out_hbm, idx_v, x_v, rows_v, ps_v, sem):
        base = _worker_base(tpw)

        def gather(i, h, b):
            return pltpu.make_async_copy(u_hbm.at[idx_v.at[2 * i + h]], rows_v.at[b], sem.at[b])

        def compute(i, h, b):
            for rg in range(PEER_ROWS // rg_rows):
                def cbody(c, accs):
                    c0 = pl.multiple_of(c * SC_LANES, SC_LANES)
                    xl = x_v[i, pl.ds(c0, SC_LANES)]
                    xh = x_v[i, pl.ds(half + c0, SC_LANES)]
                    new = []
                    for r in range(rg_rows):
                        lo, hi = _unpack_words(rows_v[b, rg * rg_rows + r, pl.ds(c0, SC_LANES)])
                        new.append(accs[r] + lo * xl + hi * xh)
                    return tuple(new)

                accs = _sc_loop(n_chunks, cbody, tuple(jnp.zeros((SC_LANES,), F32) for _ in range(rg_rows)))
                for r in range(rg_rows):
                    ps_v[pl.ds((h * PEER_ROWS + rg * rg_rows + r) * SC_LANES, SC_LANES)] = accs[r]

        @pl.loop(0, tpw // grp)
        def _(g):
            t0 = base + g * grp
            pltpu.sync_copy(idx_hbm.at[pl.ds(2 * t0, 2 * grp)], idx_v)
            pltpu.sync_copy(x_hbm.at[pl.ds(t0, grp)], x_v)
            gather(0, 0, 0).start()

            @pl.loop(0, grp)
            def _(i):
                gather(i, 1, 1).start()
                gather(i, 0, 0).wait()
                compute(i, 0, 0)

                @pl.when(i + 1 < grp)
                def _():
                    gather(i + 1, 0, 0).start()

                gather(i, 1, 1).wait()
                compute(i, 1, 1)
                pltpu.sync_copy(ps_v, out_hbm.at[t0 + i])

    return pl.kernel(
        body,
        out_type=jax.ShapeDtypeStruct((t, PEER_SEL * SC_LANES), F32),
        mesh=_sc_mesh(),
        scratch_types=[
            pltpu.VMEM((2 * grp, PEER_ROWS), jnp.int32),
            pltpu.VMEM((grp, d), F32),
            pltpu.VMEM((2, PEER_ROWS, half), jnp.int32),
            pltpu.VMEM((PEER_SEL * SC_LANES,), F32),
            pltpu.SemaphoreType.DMA((2,)),
        ],
        compiler_params=pltpu.CompilerParams(needs_layout_passes=False),
        name="peer_expert_dots",
    )(xn2d, idx2, u_packed)


def peer_expert_mix(hgx, idx2, v_packed):
    t = hgx.shape[0]
    half = v_packed.shape[1]
    d = 2 * half
    tpw = t // SC_WORKERS
    grp = min(PEER_GROUP // 2, tpw)
    n_parts = 2
    cpp = half // SC_LANES // n_parts

    def body(hg_hbm, idx_hbm, v_hbm, out_hbm, idx_v, hg_v, rows_v, o_v, sem):
        base = _worker_base(tpw)

        def gather(i, h, b):
            return pltpu.make_async_copy(v_hbm.at[idx_v.at[2 * i + h]], rows_v.at[b], sem.at[b])

        def compute(i, h, b):
            for part in range(n_parts):
                def rbody(r, accs):
                    s = hg_v[i, pl.ds(pl.multiple_of((h * PEER_ROWS + r) * SC_LANES, SC_LANES), SC_LANES)]
                    new = []
                    for c in range(cpp):
                        lo, hi = _unpack_words(rows_v[b, r, pl.ds((part * cpp + c) * SC_LANES, SC_LANES)])
                        new.append(accs[2 * c] + s * lo)
                        new.append(accs[2 * c + 1] + s * hi)
                    return tuple(new)

                accs = _sc_loop(PEER_ROWS, rbody, tuple(jnp.zeros((SC_LANES,), F32) for _ in range(2 * cpp)))
                for c in range(cpp):
                    lo_at = pl.ds((part * cpp + c) * SC_LANES, SC_LANES)
                    hi_at = pl.ds(half + (part * cpp + c) * SC_LANES, SC_LANES)
                    if h == 0:
                        o_v[lo_at] = accs[2 * c]
                        o_v[hi_at] = accs[2 * c + 1]
                    else:
                        o_v[lo_at] = o_v[lo_at] + accs[2 * c]
                        o_v[hi_at] = o_v[hi_at] + accs[2 * c + 1]

        @pl.loop(0, tpw // grp)
        def _(g):
            t0 = base + g * grp
            pltpu.sync_copy(idx_hbm.at[pl.ds(2 * t0, 2 * grp)], idx_v)
            pltpu.sync_copy(hg_hbm.at[pl.ds(t0, grp)], hg_v)
            gather(0, 0, 0).start()

            @pl.loop(0, grp)
            def _(i):
                gather(i, 1, 1).start()
                gather(i, 0, 0).wait()
                compute(i, 0, 0)

                @pl.when(i + 1 < grp)
                def _():
                    gather(i + 1, 0, 0).start()

                gather(i, 1, 1).wait()
                compute(i, 1, 1)
                pltpu.sync_copy(o_v, out_hbm.at[t0 + i])

    return pl.kernel(
        body,
        out_type=jax.ShapeDtypeStruct((t, d), F32),
        mesh=_sc_mesh(),
        scratch_types=[
            pltpu.VMEM((2 * grp, PEER_ROWS), jnp.int32),
            pltpu.VMEM((grp, PEER_SEL * SC_LANES), F32),
            pltpu.VMEM((2, PEER_ROWS, half), jnp.int32),
            pltpu.VMEM((d,), F32),
            pltpu.SemaphoreType.DMA((2,)),
        ],
        compiler_params=pltpu.CompilerParams(needs_layout_passes=False),
        name="peer_expert_mix",
    )(hgx, idx2, v_packed)


def _peer_act_kernel(ps_ref, gate_ref, sum_ref, o_ref):
    pre = jnp.dot(ps_ref[...], sum_ref[...], precision=HI, preferred_element_type=F32)
    hg = 0.5 * pre * (1.0 + lax.erf(pre * (1.0 / math.sqrt(2.0)))) * gate_ref[...]
    spread = (((1,), (1,)), ((), ()))
    o_ref[...] = lax.dot_general(hg, sum_ref[...], spread, precision=HI, preferred_element_type=F32)


def peer_act(ps, gates, *, tm=512):
    t, n = ps.shape
    lane_sum = (jnp.arange(n)[:, None] // SC_LANES == jnp.arange(PEER_SEL)[None, :]).astype(F32)
    return pl.pallas_call(
        _peer_act_kernel,
        grid=(t // tm,),
        in_specs=[
            pl.BlockSpec((tm, n), lambda i: (i, 0)),
            pl.BlockSpec((tm, PEER_SEL), lambda i: (i, 0)),
            pl.BlockSpec((n, PEER_SEL), lambda i: (0, 0)),
        ],
        out_specs=pl.BlockSpec((tm, n), lambda i: (i, 0)),
        out_shape=jax.ShapeDtypeStruct((t, n), F32),
        compiler_params=_cparams(("parallel",)),
        name="peer_act",
    )(ps, gates, lane_sum)


def peer_experts(xn2d, idx, gates, peer_u, peer_v):
    idx2 = idx.reshape(-1, PEER_ROWS)
    ps = peer_expert_dots(xn2d, idx2, _pack_rows(peer_u))
    hgx = peer_act(ps, gates)
    return peer_expert_mix(hgx, idx2, _pack_rows(peer_v))


def _peer_experts_debug(xn2d, idx, gates, peer_u, peer_v):
    t, d = xn2d.shape
    ch = 128
    def one(args):
        xc, ic, gc = args
        h = jax.nn.gelu(jnp.einsum('ckd,cd->ck', peer_u[ic], xc), approximate=False) * gc
        return jnp.einsum('ck,ckd->cd', h, peer_v[ic])
    out = lax.map(one, (xn2d.reshape(t // ch, ch, d), idx.reshape(t // ch, ch, -1), gates.reshape(t // ch, ch, -1)))
    return out.reshape(t, d)


def kernel(x, norm1_g, w_in, rwkv_mu, w0, w_lora_up, a0, a_lora_up, g_lora_up, k_k, k_a, r_k, lnx_g, lnx_b,
           w_proj_a, w_proj_b, w_out, norm2_g, peer_wq, peer_subkeys, peer_u, peer_v, rel_bias, normf_g):
    bsz, seq, d = x.shape
    t = bsz * seq
    depth = norm1_g.shape[0]
    h2d = x.reshape(t, d)
    y2d = None
    for l in range(depth):
        if y2d is not None:
            h2d = h2d + y2d
        w_pad = jnp.concatenate([
            w_in[l][:, :COL_A + COL_B_RAW],
            jnp.zeros((d, COL_B - COL_B_RAW), w_in.dtype),
            w_in[l][:, COL_A + COL_B_RAW:]], axis=1).astype(BF16)
        p2d = norm_proj(h2d, norm1_g[l], w_pad)
        p3d = p2d.reshape(bsz, seq, -1)
        oa = moba_attention(p3d, rel_bias)
        prep = rwkv_prep(p3d, rwkv_mu[l], w0[l], w_lora_up[l], a0[l], a_lora_up[l], g_lora_up[l],
                         k_k[l], k_a[l], r_k[l])
        ob = rwkv_scan(*prep, lnx_g[l], lnx_b[l])
        h2d, xn2 = merge_out(h2d, oa.reshape(t, WIDTH), ob.reshape(t, WIDTH), p2d,
                             w_proj_a[l], w_proj_b[l], w_out[l], norm2_g[l])
        idx, gates = peer_route(xn2, peer_wq[l], peer_subkeys[l])
        y2d = peer_experts(xn2, idx, gates, peer_u[l], peer_v[l])
    return final_norm(h2d, y2d, normf_g).reshape(bsz, seq, d)
```

```python
import functools
import math

import jax
import jax.numpy as jnp
from jax import lax
from jax.experimental import pallas as pl
from jax.experimental.pallas import tpu as pltpu

F32 = jnp.float32
BF16 = jnp.bfloat16
HI = lax.Precision.HIGHEST

LANES = 128
HEAD_DIM = 64
HEADS = 8
PAIRS = HEADS // 2
WIDTH = HEADS * HEAD_DIM
MOBA_BLOCK = 256
MOBA_TOPK = 3
REL_BUCKETS = 32
REL_MAX_DIST = 128
DECAY_LORA = 64
AAA_LORA = 64
GATE_LORA = 160
GN_EPS = 64e-5
RMS_EPS = 1e-6
NEG = -1e30
RWKV_CHUNK = 64
COL_A = 3 * WIDTH
COL_B_RAW = 3 * WIDTH + DECAY_LORA + AAA_LORA + GATE_LORA
COL_B = 4 * WIDTH
COL_G_OFF = COL_A + COL_B
VMEM_LIMIT = 56 * 1024 * 1024


def _cparams(sem):
    return pltpu.CompilerParams(dimension_semantics=sem, vmem_limit_bytes=VMEM_LIMIT)


def _norm_proj_kernel(x_ref, g_ref, w_ref, o_ref, xn_ref):
    @pl.when(pl.program_id(1) == 0)
    def _():
        x = x_ref[...]
        ms = jnp.mean(x * x, axis=-1, keepdims=True)
        xn_ref[...] = (x * lax.rsqrt(ms + RMS_EPS) * g_ref[...]).astype(xn_ref.dtype)

    o_ref[...] = jnp.dot(xn_ref[...], w_ref[...], preferred_element_type=F32).astype(o_ref.dtype)


def norm_proj(x2d, g, w, *, tm=512, tn=512, out_dtype=F32):
    t, d = x2d.shape
    n = w.shape[1]
    return pl.pallas_call(
        _norm_proj_kernel,
        grid=(t // tm, n // tn),
        in_specs=[
            pl.BlockSpec((tm, d), lambda i, j: (i, 0)),
            pl.BlockSpec((1, d), lambda i, j: (0, 0)),
            pl.BlockSpec((d, tn), lambda i, j: (0, j)),
        ],
        out_specs=pl.BlockSpec((tm, tn), lambda i, j: (i, j)),
        out_shape=jax.ShapeDtypeStruct((t, n), out_dtype),
        scratch_shapes=[pltpu.VMEM((tm, d), w.dtype)],
        compiler_params=_cparams(("parallel", "arbitrary")),
        name="norm_proj",
    )(x2d, g.reshape(1, d), w)


def _rel_bucket(dist):
    n = jnp.maximum(dist, 0)
    max_exact = REL_BUCKETS // 2
    nf = jnp.maximum(n, 1).astype(F32)
    large = max_exact + (jnp.log(nf / max_exact) / math.log(REL_MAX_DIST / max_exact)
                         * (REL_BUCKETS - max_exact)).astype(jnp.int32)
    large = jnp.minimum(large, REL_BUCKETS - 1)
    return jnp.where(n < max_exact, n, large)


def _moba_kernel(q_ref, k_ref, v_ref, bown_ref, bprev_ref, bfar_ref, o_ref,
                 kb_ref, vb_ref, kbar_ref, *, n_blocks):
    qb = pl.program_id(2)
    blk = MOBA_BLOCK
    scale = 1.0 / math.sqrt(HEAD_DIM)

    @pl.when(qb == 0)
    def _():
        kbar_ref[...] = jnp.zeros_like(kbar_ref)
        for n in range(n_blocks):
            kblk = k_ref[0, n * blk:(n + 1) * blk, :]
            kbar_ref[n:n + 1, :] = jnp.mean(kblk, axis=0, keepdims=True)
        kb_ref[...] = k_ref[0].astype(BF16)
        vb_ref[...] = v_ref[0].astype(BF16)

    q2 = q_ref[0]
    lane = lax.broadcasted_iota(jnp.int32, (blk, LANES), 1)
    row = lax.broadcasted_iota(jnp.int32, (blk, blk), 0)
    col = lax.broadcasted_iota(jnp.int32, (blk, blk), 1)
    own0 = pl.multiple_of(qb * blk, blk)
    k_own = kb_ref[pl.ds(own0, blk), :]
    v_own = vb_ref[pl.ds(own0, blk), :]
    prev0 = pl.multiple_of(jnp.maximum(qb - 1, 0) * blk, blk)
    k_prev = kb_ref[pl.ds(prev0, blk), :]
    v_prev = vb_ref[pl.ds(prev0, blk), :]
    nt = (((1,), (1,)), ((), ()))

    outs = []
    for hh in range(2):
        hmask = (lane >= hh * HEAD_DIM) & (lane < (hh + 1) * HEAD_DIM)
        qh = jnp.where(hmask, q2, 0.0)
        gate = lax.dot_general(qh, kbar_ref[...], nt, precision=HI, preferred_element_type=F32)
        g = jnp.where(lane < qb, gate, -jnp.inf)
        sel = []
        for _ in range(MOBA_TOPK):
            m = jnp.max(g, axis=1, keepdims=True)
            idx = jnp.min(jnp.where(g == m, lane, LANES), axis=1, keepdims=True)
            idx = jnp.where(m > -jnp.inf, idx, LANES)
            sel.append(idx)
            g = jnp.where(lane == idx, -jnp.inf, g)

        def picked(n):
            return (sel[0] == n) | (sel[1] == n) | (sel[2] == n)

        qs = (qh * scale).astype(BF16)
        s = lax.dot_general(qs, k_own, nt, preferred_element_type=F32) + bown_ref[hh]
        s = jnp.where(row >= col, s, NEG)
        m_i = jnp.max(s, axis=1, keepdims=True)
        p = jnp.exp(s - m_i)
        l_i = jnp.sum(p, axis=1, keepdims=True)
        acc = jnp.dot(p.astype(BF16), v_own, preferred_element_type=F32)

        def update(carry, s, vblk):
            m_i, l_i, acc = carry
            m_new = jnp.maximum(m_i, jnp.max(s, axis=1, keepdims=True))
            alpha = jnp.exp(m_i - m_new)
            p = jnp.exp(s - m_new)
            l_new = alpha * l_i + jnp.sum(p, axis=1, keepdims=True)
            acc_new = alpha * acc + jnp.dot(p.astype(BF16), vblk, preferred_element_type=F32)
            return m_new, l_new, acc_new

        s = lax.dot_general(qs, k_prev, nt, preferred_element_type=F32) + bprev_ref[hh]
        s = jnp.where(picked(qb - 1), s, NEG)
        carry = update((m_i, l_i, acc), s, v_prev)

        bfar = bfar_ref[hh, 0:1, 0:1]

        def body(n, carry):
            n0 = pl.multiple_of(n * blk, blk)
            kblk = kb_ref[pl.ds(n0, blk), :]
            vblk = vb_ref[pl.ds(n0, blk), :]
            s = lax.dot_general(qs, kblk, nt, preferred_element_type=F32) + bfar
            s = jnp.where(picked(n), s, NEG)
            return update(carry, s, vblk)

        m_i, l_i, acc = lax.fori_loop(0, jnp.maximum(qb - 1, 0), body, carry)
        outs.append(acc / l_i)

    o_ref[0] = jnp.where(lane < HEAD_DIM, outs[0], outs[1])


def moba_attention(p3d, rel_bias):
    bsz, seq, _ = p3d.shape
    blk = MOBA_BLOCK
    n_blocks = seq // blk
    span = 2 * blk
    by_dist = rel_bias[:, _rel_bucket(jnp.arange(span))].astype(F32)
    shift = jnp.arange(span)

    def toeplitz(c):
        k = jnp.where(shift < blk, shift, shift - span)
        s = by_dist[:, jnp.clip(c - k, 0, span - 1)]
        tiled = jnp.tile(s, (1, blk))[:, :blk * (span - 1)]
        return tiled.reshape(HEADS, blk, span - 1)[:, :, :blk]

    bias_own = toeplitz(0)
    bias_prev = toeplitz(blk)
    bias_far = jnp.broadcast_to(rel_bias[:, REL_BUCKETS - 1].astype(F32)[:, None, None], (HEADS, 8, LANES))
    kern = functools.partial(_moba_kernel, n_blocks=n_blocks)
    return pl.pallas_call(
        kern,
        grid=(bsz, PAIRS, n_blocks),
        in_specs=[
            pl.BlockSpec((1, blk, LANES), lambda b, h, i: (b, i, h)),
            pl.BlockSpec((1, seq, LANES), lambda b, h, i: (b, 0, PAIRS + h)),
            pl.BlockSpec((1, seq, LANES), lambda b, h, i: (b, 0, 2 * PAIRS + h)),
            pl.BlockSpec((2, blk, blk), lambda b, h, i: (h, 0, 0)),
            pl.BlockSpec((2, blk, blk), lambda b, h, i: (h, 0, 0)),
            pl.BlockSpec((2, 8, LANES), lambda b, h, i: (h, 0, 0)),
        ],
        out_specs=pl.BlockSpec((1, blk, LANES), lambda b, h, i: (b, i, h)),
        out_shape=jax.ShapeDtypeStruct((bsz, seq, WIDTH), F32),
        scratch_shapes=[
            pltpu.VMEM((seq, LANES), BF16),
            pltpu.VMEM((seq, LANES), BF16),
            pltpu.VMEM((LANES, LANES), F32),
        ],
        compiler_params=_cparams(("parallel", "parallel", "arbitrary")),
        name="moba",
    )(p3d, p3d, p3d, bias_own, bias_prev, bias_far)


def _shifted(x, carry_row):
    rows = lax.broadcasted_iota(jnp.int32, x.shape, 0)
    return jnp.where(rows == 0, carry_row, pltpu.roll(x, 1, axis=0))


def _rwkv_prep_kernel(pr_ref, pk_ref, pv_ref, pl_ref, mu_ref, vec_ref, ww_ref, wa_ref, wg_ref,
                      bd_ref, tri_ref,
                      rt_ref, kt_ref, kd_ref, bd_out_ref, v_ref, g_ref, bonus_ref, pend_ref,
                      carry_ref, *, chunk):
    @pl.when(pl.program_id(1) == 0)
    def _():
        carry_ref[...] = jnp.zeros_like(carry_ref)

    def mix(ref, j):
        x = ref[0]
        mu = mu_ref[0:1, j * WIDTH:(j + 1) * WIDTH]
        prev = _shifted(x, carry_ref[0:1, j * WIDTH:(j + 1) * WIDTH])
        carry_ref[0:1, j * WIDTH:(j + 1) * WIDTH] = x[x.shape[0] - 1:, :]
        return x + mu * (prev - x)

    r = mix(pr_ref, 0)
    k = mix(pk_ref, 1)
    v = mix(pv_ref, 2)
    lo = mix(pl_ref, 3)
    w0, a0, k_k, k_a, r_k = (vec_ref[i:i + 1, :] for i in range(5))
    xwa = lo[:, 0:LANES]
    xg = lo[:, LANES:3 * LANES]
    lw = jnp.dot(jnp.tanh(xwa), ww_ref[...], precision=HI, preferred_element_type=F32)
    la = jnp.dot(xwa, wa_ref[...], precision=HI, preferred_element_type=F32)
    g = jnp.dot(jax.nn.sigmoid(xg), wg_ref[...], precision=HI, preferred_element_type=F32)
    z = -(w0 + lw)
    softplus = jnp.maximum(z, 0.0) + jnp.log(1.0 + jnp.exp(-jnp.abs(z)))
    logw = -jnp.exp(-softplus - 0.5)
    a = jax.nn.sigmoid(a0 + la)
    kk = k * k_k
    ss = jnp.dot(kk * kk, bd_ref[...], precision=HI, preferred_element_type=F32)
    kk = kk / jnp.maximum(jnp.sqrt(ss), 1e-12)
    k2 = k * (1.0 + (a - 1.0) * k_a)
    rk = jnp.dot(r * k2 * r_k, bd_ref[...], precision=HI, preferred_element_type=F32)
    cs = jnp.dot(tri_ref[...], logw, precision=HI, preferred_element_type=F32)
    e_pos = jnp.exp(cs)
    e_neg = jnp.exp(-cs)
    rt_ref[0] = r * e_pos
    kt_ref[0] = kk * jnp.exp(cs - logw)
    kd_ref[0] = k2 * e_neg
    bd_out_ref[0] = kk * a * e_neg
    v_ref[0] = v
    g_ref[0] = g
    bonus_ref[0] = rk * v
    ts = e_pos.shape[0]
    for c in range(ts // chunk):
        pend_ref[0, c:c + 1, :] = e_pos[(c + 1) * chunk - 1:(c + 1) * chunk, :]


def rwkv_prep(p3d, rwkv_mu, w0, w_lora_up, a0, a_lora_up, g_lora_up, k_k, k_a, r_k, *, ts=512):
    bsz, seq, _ = p3d.shape
    chunk = RWKV_CHUNK
    ts = min(ts, seq)
    mu = jnp.pad(rwkv_mu, (0, COL_B - COL_B_RAW)).reshape(1, COL_B)
    vec = jnp.stack([w0, a0, k_k, k_a, r_k.reshape(-1)] + [jnp.zeros_like(w0)] * 3).astype(F32)
    ww = jnp.zeros((LANES, WIDTH), F32).at[:DECAY_LORA].set(w_lora_up)
    wa = jnp.zeros((LANES, WIDTH), F32).at[DECAY_LORA:DECAY_LORA + AAA_LORA].set(a_lora_up)
    wg = jnp.zeros((2 * LANES, WIDTH), F32).at[:GATE_LORA].set(g_lora_up)
    hid = jnp.arange(WIDTH) // HEAD_DIM
    bd = (hid[:, None] == hid[None, :]).astype(F32)
    tix = jnp.arange(ts)
    tri = ((tix[:, None] // chunk == tix[None, :] // chunk) & (tix[None, :] <= tix[:, None])).astype(F32)
    c0 = COL_A // WIDTH
    big = jax.ShapeDtypeStruct((bsz, seq, WIDTH), F32)
    wspec = lambda shape: pl.BlockSpec(shape, lambda b, i: (0, 0))
    ospec = pl.BlockSpec((1, ts, WIDTH), lambda b, i: (b, i, 0))
    return pl.pallas_call(
        functools.partial(_rwkv_prep_kernel, chunk=chunk),
        grid=(bsz, seq // ts),
        in_specs=[
            pl.BlockSpec((1, ts, WIDTH), lambda b, i: (b, i, c0)),
            pl.BlockSpec((1, ts, WIDTH), lambda b, i: (b, i, c0 + 1)),
            pl.BlockSpec((1, ts, WIDTH), lambda b, i: (b, i, c0 + 2)),
            pl.BlockSpec((1, ts, WIDTH), lambda b, i: (b, i, c0 + 3)),
            wspec((1, COL_B)), wspec((8, WIDTH)), wspec((LANES, WIDTH)), wspec((LANES, WIDTH)),
            wspec((2 * LANES, WIDTH)), wspec((WIDTH, WIDTH)), wspec((ts, ts)),
        ],
        out_specs=[ospec] * 7 + [pl.BlockSpec((1, ts // chunk, WIDTH), lambda b, i: (b, i, 0))],
        out_shape=[big] * 7 + [jax.ShapeDtypeStruct((bsz, seq // chunk, WIDTH), F32)],
        scratch_shapes=[pltpu.VMEM((8, COL_B), F32)],
        compiler_params=_cparams(("parallel", "arbitrary")),
        name="rwkv_prep",
    )(p3d, p3d, p3d, p3d, mu, vec, ww, wa, wg, bd, tri)


def _rwkv_scan_kernel(rt_ref, kt_ref, kd_ref, bd_ref, v_ref, g_ref, bonus_ref, pend_ref, ln_ref, o_ref,
                      state_ref, *, chunk, prec):
    @pl.when(pl.program_id(1) == 0)
    def _():
        state_ref[...] = jnp.zeros_like(state_ref)

    c2 = 2 * chunk
    lane = lax.broadcasted_iota(jnp.int32, (chunk, LANES), 1)
    first = lane < HEAD_DIM
    row = lax.broadcasted_iota(jnp.int32, (c2, c2), 0)
    col = lax.broadcasted_iota(jnp.int32, (c2, c2), 1)
    eye = (row == col).astype(F32)
    hrow = lax.broadcasted_iota(jnp.int32, (LANES, LANES), 0) // HEAD_DIM
    hcol = lax.broadcasted_iota(jnp.int32, (LANES, LANES), 1) // HEAD_DIM
    head_mean = jnp.where(hrow == hcol, 1.0 / HEAD_DIM, 0.0).astype(F32)
    nt = (((1,), (1,)), ((), ()))
    tn = (((0,), (0,)), ((), ()))
    dot = functools.partial(jnp.dot, precision=prec, preferred_element_type=F32)
    dotg = functools.partial(lax.dot_general, precision=prec, preferred_element_type=F32)

    def stack(x):
        return jnp.concatenate([jnp.where(first, x, 0.0), jnp.where(first, 0.0, x)], axis=0)

    for hp in range(PAIRS):
        sl = slice(hp * LANES, (hp + 1) * LANES)
        rs, ks, kds, bs, vs = (stack(ref[0, :, sl]) for ref in (rt_ref, kt_ref, kd_ref, bd_ref, v_ref))
        pend = pend_ref[0, 0, 0:1, sl]
        big = dotg(jnp.concatenate([ks, rs], axis=0), jnp.concatenate([bs, kds], axis=0), nt)
        a_b = jnp.where(row > col, big[0:c2, 0:c2], 0.0)
        a_k = jnp.where(row > col, big[0:c2, c2:], 0.0)
        a_rb = jnp.where(row >= col, big[c2:, 0:c2], 0.0)
        a_rk = jnp.where(row >= col, big[c2:, c2:], 0.0)
        inv = eye - a_b
        pw = dot(a_b, a_b)
        n_sq = int(math.log2(chunk)) - 1
        for lvl in range(n_sq):
            inv = inv + dot(inv, pw)
            if lvl + 1 < n_sq:
                pw = dot(pw, pw)
        ht = state_ref[hp]
        rhs = dotg(ks, ht, nt) + dot(a_k, vs)
        us = dot(inv, rhs)
        os_ = dotg(rs, ht, nt) + dot(a_rk, vs) - dot(a_rb, us)
        o = os_[0:chunk] + os_[chunk:]
        state_ref[hp] = (ht + dotg(vs, kds, tn) - dotg(us, bs, tn)) * pend
        mu = jnp.dot(o, head_mean, precision=HI, preferred_element_type=F32)
        d = o - mu
        var = jnp.dot(d * d, head_mean, precision=HI, preferred_element_type=F32)
        on = d * lax.rsqrt(var + GN_EPS) * ln_ref[0:1, sl] + ln_ref[1:2, sl]
        o_ref[0, :, sl] = (on + bonus_ref[0, :, sl]) * g_ref[0, :, sl]


def rwkv_scan(rt, kt, kd, bd, v, g, bonus, pend, lnx_g, lnx_b, *, prec=None):
    bsz, seq, _ = rt.shape
    chunk = RWKV_CHUNK
    n_chunks = seq // chunk
    ln = jnp.stack([lnx_g, lnx_b] + [jnp.zeros_like(lnx_g)] * 6).astype(F32)
    pend4 = pend.reshape(bsz, n_chunks, 1, WIDTH)
    spec = pl.BlockSpec((1, chunk, WIDTH), lambda b, c: (b, c, 0))
    return pl.pallas_call(
        functools.partial(_rwkv_scan_kernel, chunk=chunk, prec=prec),
        grid=(bsz, n_chunks),
        in_specs=[spec] * 7 + [
            pl.BlockSpec((1, 1, 1, WIDTH), lambda b, c: (b, c, 0, 0)),
            pl.BlockSpec((8, WIDTH), lambda b, c: (0, 0)),
        ],
        out_specs=spec,
        out_shape=jax.ShapeDtypeStruct((bsz, seq, WIDTH), F32),
        scratch_shapes=[pltpu.VMEM((PAIRS, LANES, LANES), F32)],
        compiler_params=_cparams(("parallel", "arbitrary")),
        name="rwkv_scan",
    )(rt, kt, kd, bd, v, g, bonus, pend4, ln)


def _merge_kernel(x_ref, oa_ref, ob_ref, ga_ref, gb_ref, wa_ref, wb_ref, wo_ref, g2_ref,
                  h_ref, xn_ref, acc_ref):
    j = pl.program_id(1)

    @pl.when(j == 0)
    def _():
        acc_ref[...] = x_ref[...]

    ya = jnp.dot(oa_ref[...].astype(BF16), wa_ref[...], preferred_element_type=F32)
    yb = jnp.dot(ob_ref[...].astype(BF16), wb_ref[...], preferred_element_type=F32)
    y = jax.nn.sigmoid(ga_ref[...]) * ya + jax.nn.sigmoid(gb_ref[...]) * yb
    acc_ref[...] += jnp.dot(y.astype(BF16), wo_ref[...], preferred_element_type=F32)

    @pl.when(j == pl.num_programs(1) - 1)
    def _():
        h = acc_ref[...]
        h_ref[...] = h
        ms = jnp.mean(h * h, axis=-1, keepdims=True)
        xn_ref[...] = h * lax.rsqrt(ms + RMS_EPS) * g2_ref[...]


def merge_out(x2d, oa, ob, p2d, w_proj_a, w_proj_b, w_out, norm2_g, *, tm=512):
    t, d = x2d.shape
    tn = WIDTH
    nj = d // tn
    g0 = COL_G_OFF // tn
    big = jax.ShapeDtypeStruct((t, d), F32)
    return pl.pallas_call(
        _merge_kernel,
        grid=(t // tm, nj),
        in_specs=[
            pl.BlockSpec((tm, d), lambda i, j: (i, 0)),
            pl.BlockSpec((tm, WIDTH), lambda i, j: (i, 0)),
            pl.BlockSpec((tm, WIDTH), lambda i, j: (i, 0)),
            pl.BlockSpec((tm, tn), lambda i, j: (i, g0 + j)),
            pl.BlockSpec((tm, tn), lambda i, j: (i, g0 + nj + j)),
            pl.BlockSpec((WIDTH, tn), lambda i, j: (0, j)),
            pl.BlockSpec((WIDTH, tn), lambda i, j: (0, j)),
            pl.BlockSpec((tn, d), lambda i, j: (j, 0)),
            pl.BlockSpec((1, d), lambda i, j: (0, 0)),
        ],
        out_specs=[pl.BlockSpec((tm, d), lambda i, j: (i, 0))] * 2,
        out_shape=[big, big],
        scratch_shapes=[pltpu.VMEM((tm, d), F32)],
        compiler_params=_cparams(("parallel", "arbitrary")),
        name="merge_out",
    )(x2d, oa, ob, p2d, p2d, w_proj_a.astype(BF16), w_proj_b.astype(BF16), w_out.astype(BF16),
      norm2_g.reshape(1, d))


PEER_HEADS = 8
PEER_NKEYS = 128
PEER_TOPK = 16
PEER_HALF = 128


def _topk_rows(s, payload, k):
    n = s.shape[0]
    rows = lax.broadcasted_iota(jnp.int32, s.shape, 0)
    vals, pays = [], []
    for _ in range(k):
        m = jnp.max(s, axis=0, keepdims=True)
        first = jnp.min(jnp.where(s == m, rows, n), axis=0, keepdims=True)
        hit = rows == first
        vals.append(m)
        pays.append(jnp.max(jnp.where(hit, payload, -1), axis=0, keepdims=True))
        s = jnp.where(hit, -jnp.inf, s)
    return jnp.concatenate(vals, axis=0), jnp.concatenate(pays, axis=0)


def _peer_route_kernel(xn_ref, wq_ref, sk_ref, idx_ref, gate_ref, *, prec):
    tt = xn_ref.shape[0]
    k = PEER_TOPK
    q = jnp.dot(xn_ref[...].astype(wq_ref.dtype), wq_ref[...], precision=prec, preferred_element_type=F32)
    key_iota = lax.broadcasted_iota(jnp.int32, (PEER_NKEYS, tt), 0)
    nt = (((1,), (1,)), ((), ()))
    idx_rows, gate_rows = [], []
    for h in range(PEER_HEADS):
        tops = []
        for p in range(2):
            c0 = (h * 2 + p) * PEER_HALF
            s = lax.dot_general(sk_ref[h, p].astype(wq_ref.dtype), q[:, c0:c0 + PEER_HALF].astype(wq_ref.dtype),
                                nt, precision=prec, preferred_element_type=F32)
            tops.append(_topk_rows(s, key_iota, k))
        (s0, i0), (s1, i1) = tops
        half = k // 2
        cs = [s0[0:1] + s1] + [s0[i:i + 1] + s1[0:half] for i in range(1, half)] + [s0[half:] + s1[0:1]]
        ci = [i0[0:1] * PEER_NKEYS + i1] + [i0[i:i + 1] * PEER_NKEYS + i1[0:half] for i in range(1, half)] \
            + [i0[half:] * PEER_NKEYS + i1[0:1]]
        best_s, best_i = _topk_rows(jnp.concatenate(cs, axis=0), jnp.concatenate(ci, axis=0), k)
        e = jnp.exp(best_s - best_s[0:1])
        gate_rows.append(e / jnp.sum(e, axis=0, keepdims=True))
        idx_rows.append(best_i)
    idx_ref[...] = jnp.concatenate(idx_rows, axis=0).T
    gate_ref[...] = jnp.concatenate(gate_rows, axis=0).T


def peer_route(xn2d, peer_wq, peer_subkeys, *, tt=256, prec=None, wdtype=BF16):
    t, d = xn2d.shape
    nq = peer_wq.shape[1]
    n_sel = PEER_HEADS * PEER_TOPK
    return pl.pallas_call(
        functools.partial(_peer_route_kernel, prec=prec),
        grid=(t // tt,),
        in_specs=[
            pl.BlockSpec((tt, d), lambda i: (i, 0)),
            pl.BlockSpec((d, nq), lambda i: (0, 0)),
            pl.BlockSpec((PEER_HEADS, 2, PEER_NKEYS, PEER_HALF), lambda i: (0, 0, 0, 0)),
        ],
        out_specs=[pl.BlockSpec((tt, n_sel), lambda i: (i, 0))] * 2,
        out_shape=[jax.ShapeDtypeStruct((t, n_sel), jnp.int32), jax.ShapeDtypeStruct((t, n_sel), F32)],
        compiler_params=_cparams(("parallel",)),
        name="peer_route",
    )(xn2d, peer_wq.astype(wdtype), peer_subkeys)


def _final_kernel(h_ref, y_ref, g_ref, o_ref):
    h = h_ref[...] + y_ref[...]
    ms = jnp.mean(h * h, axis=-1, keepdims=True)
    o_ref[...] = h * lax.rsqrt(ms + RMS_EPS) * g_ref[...]


def final_norm(h2d, y2d, g, *, tm=1024):
    t, d = h2d.shape
    spec = pl.BlockSpec((tm, d), lambda i: (i, 0))
    return pl.pallas_call(
        _final_kernel,
        grid=(t // tm,),
        in_specs=[spec, spec, pl.BlockSpec((1, d), lambda i: (0, 0))],
        out_specs=spec,
        out_shape=jax.ShapeDtypeStruct((t, d), F32),
        compiler_params=_cparams(("parallel",)),
        name="final_norm",
    )(h2d, y2d, g.reshape(1, d))


SC_CORES = 2
SC_SUBCORES = 16
SC_LANES = 16
SC_WORKERS = SC_CORES * SC_SUBCORES
PEER_SEL = PEER_HEADS * PEER_TOPK
PEER_ROWS = 64
PEER_GROUP = 32


def _pack_rows(w):
    half = w.shape[1] // 2
    bits = lax.bitcast_convert_type(w.astype(BF16), jnp.uint16).astype(jnp.uint32)
    return lax.bitcast_convert_type(bits[:, :half] | (bits[:, half:] << 16), jnp.int32)


def _unpack_words(w):
    lo = lax.bitcast_convert_type(lax.shift_left(w, jnp.int32(16)), F32)
    hi = lax.bitcast_convert_type(lax.bitwise_and(w, jnp.int32(-65536)), F32)
    return lo, hi


def _sc_mesh():
    from jax.experimental.pallas import tpu_sc as plsc
    return plsc.VectorSubcoreMesh(core_axis_name="c", subcore_axis_name="s",
                                  num_cores=SC_CORES, num_subcores=SC_SUBCORES)


def _sc_loop(n, body, carry):
    from jax.experimental.pallas import tpu_sc as plsc
    return plsc.parallel_loop(0, n, carry=carry)(body)


def _worker_base(tokens_per_worker):
    return (lax.axis_index("s") * SC_CORES + lax.axis_index("c")) * tokens_per_worker


def _gather_compute_loop(table_hbm, idx_v, rows_v, sem, stage_v, out_row, osem, grp, compute):
    def gather(j, b):
        return pltpu.make_async_copy(table_hbm.at[idx_v.at[j]], rows_v.at[b], sem.at[b])

    def put(i, slot):
        return pltpu.make_async_copy(stage_v.at[slot], out_row(i), osem.at[slot])

    gather(0, 0).start()

    @pl.loop(0, 2 * grp)
    def _(j):
        b = lax.bitwise_and(j, 1)
        i = lax.shift_right_logical(j, 1)
        slot = lax.bitwise_and(i, 1)

        @pl.when((b == 0) & (i >= 2))
        def _():
            put(i - 2, slot).wait()

        @pl.when(j + 1 < 2 * grp)
        def _():
            gather(j + 1, 1 - b).start()

        gather(j, b).wait()
        compute(i, b, b, slot)

        @pl.when(b == 1)
        def _():
            put(i, slot).start()

    put(grp - 2, 0).wait()
    put(grp - 1, 1).wait()


def peer_expert_dots(xn2d, idx2, u_packed):
    t, d = xn2d.shape
    half = d // 2
    n_chunks = half // SC_LANES
    tpw = t // SC_WORKERS
    grp = min(PEER_GROUP, tpw)
    rows_tog = 4
    n_acc = 2
    from jax.experimental.pallas import tpu_sc as plsc

    def body(x_hbm, idx_hbm, u_hbm, out_hbm, idx_v, x_v, rows_v, ps_v, sem, osem):
        base = _worker_base(tpw)

        def compute(i, h, b, slot):
            @pl.loop(0, PEER_ROWS // rows_tog)
            def _(rg):
                r0 = rg * rows_tog
                accs = [[None] * n_acc for _ in range(rows_tog)]
                for c in range(n_chunks):
                    xl = x_v[i, pl.ds(c * SC_LANES, SC_LANES)]
                    xh = x_v[i, pl.ds(half + c * SC_LANES, SC_LANES)]
                    for r in range(rows_tog):
                        lo, hi = _unpack_words(rows_v[b, r0 + r, pl.ds(c * SC_LANES, SC_LANES)])
                        term = lo * xl + hi * xh
                        k = c % n_acc
                        accs[r][k] = term if accs[r][k] is None else accs[r][k] + term
                for r in range(rows_tog):
                    at = pl.ds(pl.multiple_of((h * PEER_ROWS + r0 + r) * SC_LANES, SC_LANES), SC_LANES)
                    ps_v[slot, at] = accs[r][0] + accs[r][1]

        @pl.loop(0, tpw // grp)
        def _(g):
            t0 = base + g * grp
            pltpu.sync_copy(idx_hbm.at[pl.ds(2 * t0, 2 * grp)], idx_v)
            pltpu.sync_copy(x_hbm.at[pl.ds(t0, grp)], x_v)
            _gather_compute_loop(u_hbm, idx_v, rows_v, sem, ps_v, lambda i: out_hbm.at[t0 + i], osem, grp, compute)

    return pl.kernel(
        body,
        out_type=jax.ShapeDtypeStruct((t, PEER_SEL * SC_LANES), F32),
        mesh=_sc_mesh(),
        scratch_types=[
            pltpu.VMEM((2 * grp, PEER_ROWS), jnp.int32),
            pltpu.VMEM((grp, d), F32),
            pltpu.VMEM((2, PEER_ROWS, half), jnp.int32),
            pltpu.VMEM((2, PEER_SEL * SC_LANES), F32),
            pltpu.SemaphoreType.DMA((2,)),
            pltpu.SemaphoreType.DMA((2,)),
        ],
        compiler_params=pltpu.CompilerParams(needs_layout_passes=False),
        name="peer_expert_dots",
    )(xn2d, idx2, u_packed)


def peer_expert_mix(hgx, idx2, v_packed):
    t = hgx.shape[0]
    half = v_packed.shape[1]
    d = 2 * half
    tpw = t // SC_WORKERS
    grp = min(PEER_GROUP // 2, tpw)
    n_parts = 2
    cpp = half // SC_LANES // n_parts

    def body(hg_hbm, idx_hbm, v_hbm, out_hbm, idx_v, hg_v, rows_v, o_v2, sem, osem):
        base = _worker_base(tpw)

        def compute(i, h, b, slot):
            for part in range(n_parts):
                def rbody(r, accs):
                    s = hg_v[i, pl.ds(pl.multiple_of((h * PEER_ROWS + r) * SC_LANES, SC_LANES), SC_LANES)]
                    new = []
                    for c in range(cpp):
                        lo, hi = _unpack_words(rows_v[b, r, pl.ds((part * cpp + c) * SC_LANES, SC_LANES)])
                        new.append(accs[2 * c] + s * lo)
                        new.append(accs[2 * c + 1] + s * hi)
                    return tuple(new)

                accs = _sc_loop(PEER_ROWS, rbody, tuple(jnp.zeros((SC_LANES,), F32) for _ in range(2 * cpp)))
                def store(overwrite):
                    for c in range(cpp):
                        lo_at = pl.ds((part * cpp + c) * SC_LANES, SC_LANES)
                        hi_at = pl.ds(half + (part * cpp + c) * SC_LANES, SC_LANES)
                        if overwrite:
                            o_v2[slot, lo_at] = accs[2 * c]
                            o_v2[slot, hi_at] = accs[2 * c + 1]
                        else:
                            o_v2[slot, lo_at] = o_v2[slot, lo_at] + accs[2 * c]
                            o_v2[slot, hi_at] = o_v2[slot, hi_at] + accs[2 * c + 1]

                pl.when(h == 0)(functools.partial(store, True))
                pl.when(h != 0)(functools.partial(store, False))

        @pl.loop(0, tpw // grp)
        def _(g):
            t0 = base + g * grp
            pltpu.sync_copy(idx_hbm.at[pl.ds(2 * t0, 2 * grp)], idx_v)
            pltpu.sync_copy(hg_hbm.at[pl.ds(t0, grp)], hg_v)
            _gather_compute_loop(v_hbm, idx_v, rows_v, sem, o_v2, lambda i: out_hbm.at[t0 + i], osem, grp, compute)

    return pl.kernel(
        body,
        out_type=jax.ShapeDtypeStruct((t, d), F32),
        mesh=_sc_mesh(),
        scratch_types=[
            pltpu.VMEM((2 * grp, PEER_ROWS), jnp.int32),
            pltpu.VMEM((grp, PEER_SEL * SC_LANES), F32),
            pltpu.VMEM((2, PEER_ROWS, half), jnp.int32),
            pltpu.VMEM((2, d), F32),
            pltpu.SemaphoreType.DMA((2,)),
            pltpu.SemaphoreType.DMA((2,)),
        ],
        compiler_params=pltpu.CompilerParams(needs_layout_passes=False),
        name="peer_expert_mix",
    )(hgx, idx2, v_packed)


def _peer_act_kernel(ps_ref, gate_ref, sum_ref, o_ref):
    pre = jnp.dot(ps_ref[...], sum_ref[...], precision=HI, preferred_element_type=F32)
    hg = 0.5 * pre * (1.0 + lax.erf(pre * (1.0 / math.sqrt(2.0)))) * gate_ref[...]
    spread = (((1,), (1,)), ((), ()))
    o_ref[...] = lax.dot_general(hg, sum_ref[...], spread, precision=HI, preferred_element_type=F32)


def peer_act(ps, gates, *, tm=512):
    t, n = ps.shape
    lane_sum = (jnp.arange(n)[:, None] // SC_LANES == jnp.arange(PEER_SEL)[None, :]).astype(F32)
    return pl.pallas_call(
        _peer_act_kernel,
        grid=(t // tm,),
        in_specs=[
            pl.BlockSpec((tm, n), lambda i: (i, 0)),
            pl.BlockSpec((tm, PEER_SEL), lambda i: (i, 0)),
            pl.BlockSpec((n, PEER_SEL), lambda i: (0, 0)),
        ],
        out_specs=pl.BlockSpec((tm, n), lambda i: (i, 0)),
        out_shape=jax.ShapeDtypeStruct((t, n), F32),
        compiler_params=_cparams(("parallel",)),
        name="peer_act",
    )(ps, gates, lane_sum)


def peer_experts(xn2d, idx, gates, peer_u, peer_v):
    idx2 = idx.reshape(-1, PEER_ROWS)
    ps = peer_expert_dots(xn2d, idx2, _pack_rows(peer_u))
    hgx = peer_act(ps, gates)
    return peer_expert_mix(hgx, idx2, _pack_rows(peer_v))


def _peer_experts_debug(xn2d, idx, gates, peer_u, peer_v):
    t, d = xn2d.shape
    ch = 128
    def one(args):
        xc, ic, gc = args
        h = jax.nn.gelu(jnp.einsum('ckd,cd->ck', peer_u[ic], xc), approximate=False) * gc
        return jnp.einsum('ck,ckd->cd', h, peer_v[ic])
    out = lax.map(one, (xn2d.reshape(t // ch, ch, d), idx.reshape(t // ch, ch, -1), gates.reshape(t // ch, ch, -1)))
    return out.reshape(t, d)


def kernel(x, norm1_g, w_in, rwkv_mu, w0, w_lora_up, a0, a_lora_up, g_lora_up, k_k, k_a, r_k, lnx_g, lnx_b,
           w_proj_a, w_proj_b, w_out, norm2_g, peer_wq, peer_subkeys, peer_u, peer_v, rel_bias, normf_g):
    bsz, seq, d = x.shape
    t = bsz * seq
    depth = norm1_g.shape[0]
    h2d = x.reshape(t, d)
    y2d = None
    for l in range(depth):
        if y2d is not None:
            h2d = h2d + y2d
        w_pad = jnp.concatenate([
            w_in[l][:, :COL_A + COL_B_RAW],
            jnp.zeros((d, COL_B - COL_B_RAW), w_in.dtype),
            w_in[l][:, COL_A + COL_B_RAW:]], axis=1).astype(BF16)
        p2d = norm_proj(h2d, norm1_g[l], w_pad)
        p3d = p2d.reshape(bsz, seq, -1)
        oa = moba_attention(p3d, rel_bias)
        prep = rwkv_prep(p3d, rwkv_mu[l], w0[l], w_lora_up[l], a0[l], a_lora_up[l], g_lora_up[l],
                         k_k[l], k_a[l], r_k[l])
        ob = rwkv_scan(*prep, lnx_g[l], lnx_b[l])
        h2d, xn2 = merge_out(h2d, oa.reshape(t, WIDTH), ob.reshape(t, WIDTH), p2d,
                             w_proj_a[l], w_proj_b[l], w_out[l], norm2_g[l])
        idx, gates = peer_route(xn2, peer_wq[l], peer_subkeys[l])
        y2d = peer_experts(xn2, idx, gates, peer_u[l], peer_v[l])
    return final_norm(h2d, y2d, normf_g).reshape(bsz, seq, d)
```

```python
import functools
import math

import jax
import jax.numpy as jnp
from jax import lax
from jax.experimental import pallas as pl
from jax.experimental.pallas import tpu as pltpu

F32 = jnp.float32
BF16 = jnp.bfloat16
HI = lax.Precision.HIGHEST

LANES = 128
HEAD_DIM = 64
HEADS = 8
PAIRS = HEADS // 2
WIDTH = HEADS * HEAD_DIM
MOBA_BLOCK = 256
MOBA_TOPK = 3
REL_BUCKETS = 32
REL_MAX_DIST = 128
DECAY_LORA = 64
AAA_LORA = 64
GATE_LORA = 160
GN_EPS = 64e-5
RMS_EPS = 1e-6
NEG = -1e30
RWKV_CHUNK = 64
COL_A = 3 * WIDTH
COL_B_RAW = 3 * WIDTH + DECAY_LORA + AAA_LORA + GATE_LORA
COL_B = 4 * WIDTH
COL_G_OFF = COL_A + COL_B
VMEM_LIMIT = 56 * 1024 * 1024


def _cparams(sem):
    return pltpu.CompilerParams(dimension_semantics=sem, vmem_limit_bytes=VMEM_LIMIT)


def _norm_proj_kernel(x_ref, g_ref, w_ref, o_ref, xn_ref):
    @pl.when(pl.program_id(1) == 0)
    def _():
        x = x_ref[...]
        ms = jnp.mean(x * x, axis=-1, keepdims=True)
        xn_ref[...] = (x * lax.rsqrt(ms + RMS_EPS) * g_ref[...]).astype(xn_ref.dtype)

    o_ref[...] = jnp.dot(xn_ref[...], w_ref[...], preferred_element_type=F32).astype(o_ref.dtype)


def norm_proj(x2d, g, w, *, tm=512, tn=512, out_dtype=F32):
    t, d = x2d.shape
    n = w.shape[1]
    return pl.pallas_call(
        _norm_proj_kernel,
        grid=(t // tm, n // tn),
        in_specs=[
            pl.BlockSpec((tm, d), lambda i, j: (i, 0)),
            pl.BlockSpec((1, d), lambda i, j: (0, 0)),
            pl.BlockSpec((d, tn), lambda i, j: (0, j)),
        ],
        out_specs=pl.BlockSpec((tm, tn), lambda i, j: (i, j)),
        out_shape=jax.ShapeDtypeStruct((t, n), out_dtype),
        scratch_shapes=[pltpu.VMEM((tm, d), w.dtype)],
        compiler_params=_cparams(("parallel", "arbitrary")),
        name="norm_proj",
    )(x2d, g.reshape(1, d), w)


def _rel_bucket(dist):
    n = jnp.maximum(dist, 0)
    max_exact = REL_BUCKETS // 2
    nf = jnp.maximum(n, 1).astype(F32)
    large = max_exact + (jnp.log(nf / max_exact) / math.log(REL_MAX_DIST / max_exact)
                         * (REL_BUCKETS - max_exact)).astype(jnp.int32)
    large = jnp.minimum(large, REL_BUCKETS - 1)
    return jnp.where(n < max_exact, n, large)


def _moba_kernel(q_ref, k_ref, v_ref, bown_ref, bprev_ref, bfar_ref, o_ref,
                 kb_ref, vb_ref, kbar_ref, *, n_blocks):
    qb = pl.program_id(2)
    blk = MOBA_BLOCK
    scale = 1.0 / math.sqrt(HEAD_DIM)

    @pl.when(qb == 0)
    def _():
        kbar_ref[...] = jnp.zeros_like(kbar_ref)
        for n in range(n_blocks):
            kblk = k_ref[0, n * blk:(n + 1) * blk, :]
            kbar_ref[n:n + 1, :] = jnp.mean(kblk, axis=0, keepdims=True)
        kb_ref[...] = k_ref[0].astype(BF16)
        vb_ref[...] = v_ref[0].astype(BF16)

    q2 = q_ref[0]
    lane = lax.broadcasted_iota(jnp.int32, (blk, LANES), 1)
    row = lax.broadcasted_iota(jnp.int32, (blk, blk), 0)
    col = lax.broadcasted_iota(jnp.int32, (blk, blk), 1)
    own0 = pl.multiple_of(qb * blk, blk)
    k_own = kb_ref[pl.ds(own0, blk), :]
    v_own = vb_ref[pl.ds(own0, blk), :]
    prev0 = pl.multiple_of(jnp.maximum(qb - 1, 0) * blk, blk)
    k_prev = kb_ref[pl.ds(prev0, blk), :]
    v_prev = vb_ref[pl.ds(prev0, blk), :]
    nt = (((1,), (1,)), ((), ()))

    outs = []
    for hh in range(2):
        hmask = (lane >= hh * HEAD_DIM) & (lane < (hh + 1) * HEAD_DIM)
        qh = jnp.where(hmask, q2, 0.0)
        gate = lax.dot_general(qh, kbar_ref[...], nt, precision=HI, preferred_element_type=F32)
        g = jnp.where(lane < qb, gate, -jnp.inf)
        sel = []
        for _ in range(MOBA_TOPK):
            m = jnp.max(g, axis=1, keepdims=True)
            idx = jnp.min(jnp.where(g == m, lane, LANES), axis=1, keepdims=True)
            idx = jnp.where(m > -jnp.inf, idx, LANES)
            sel.append(idx)
            g = jnp.where(lane == idx, -jnp.inf, g)

        def picked(n):
            return (sel[0] == n) | (sel[1] == n) | (sel[2] == n)

        qs = (qh * scale).astype(BF16)
        s = lax.dot_general(qs, k_own, nt, preferred_element_type=F32) + bown_ref[hh]
        s = jnp.where(row >= col, s, NEG)
        m_i = jnp.max(s, axis=1, keepdims=True)
        p = jnp.exp(s - m_i)
        l_i = jnp.sum(p, axis=1, keepdims=True)
        acc = jnp.dot(p.astype(BF16), v_own, preferred_element_type=F32)

        def update(carry, s, vblk):
            m_i, l_i, acc = carry
            m_new = jnp.maximum(m_i, jnp.max(s, axis=1, keepdims=True))
            alpha = jnp.exp(m_i - m_new)
            p = jnp.exp(s - m_new)
            l_new = alpha * l_i + jnp.sum(p, axis=1, keepdims=True)
            acc_new = alpha * acc + jnp.dot(p.astype(BF16), vblk, preferred_element_type=F32)
            return m_new, l_new, acc_new

        s = lax.dot_general(qs, k_prev, nt, preferred_element_type=F32) + bprev_ref[hh]
        s = jnp.where(picked(qb - 1), s, NEG)
        carry = update((m_i, l_i, acc), s, v_prev)

        bfar = bfar_ref[hh, 0:1, 0:1]

        def body(n, carry):
            n0 = pl.multiple_of(n * blk, blk)
            kblk = kb_ref[pl.ds(n0, blk), :]
            vblk = vb_ref[pl.ds(n0, blk), :]
            s = lax.dot_general(qs, kblk, nt, preferred_element_type=F32) + bfar
            s = jnp.where(picked(n), s, NEG)
            return update(carry, s, vblk)

        m_i, l_i, acc = lax.fori_loop(0, jnp.maximum(qb - 1, 0), body, carry)
        outs.append(acc / l_i)

    o_ref[0] = jnp.where(lane < HEAD_DIM, outs[0], outs[1])


def moba_attention(p3d, rel_bias):
    bsz, seq, _ = p3d.shape
    blk = MOBA_BLOCK
    n_blocks = seq // blk
    span = 2 * blk
    by_dist = rel_bias[:, _rel_bucket(jnp.arange(span))].astype(F32)
    shift = jnp.arange(span)

    def toeplitz(c):
        k = jnp.where(shift < blk, shift, shift - span)
        s = by_dist[:, jnp.clip(c - k, 0, span - 1)]
        tiled = jnp.tile(s, (1, blk))[:, :blk * (span - 1)]
        return tiled.reshape(HEADS, blk, span - 1)[:, :, :blk]

    bias_own = toeplitz(0)
    bias_prev = toeplitz(blk)
    bias_far = jnp.broadcast_to(rel_bias[:, REL_BUCKETS - 1].astype(F32)[:, None, None], (HEADS, 8, LANES))
    kern = functools.partial(_moba_kernel, n_blocks=n_blocks)
    return pl.pallas_call(
        kern,
        grid=(bsz, PAIRS, n_blocks),
        in_specs=[
            pl.BlockSpec((1, blk, LANES), lambda b, h, i: (b, i, h)),
            pl.BlockSpec((1, seq, LANES), lambda b, h, i: (b, 0, PAIRS + h)),
            pl.BlockSpec((1, seq, LANES), lambda b, h, i: (b, 0, 2 * PAIRS + h)),
            pl.BlockSpec((2, blk, blk), lambda b, h, i: (h, 0, 0)),
            pl.BlockSpec((2, blk, blk), lambda b, h, i: (h, 0, 0)),
            pl.BlockSpec((2, 8, LANES), lambda b, h, i: (h, 0, 0)),
        ],
        out_specs=pl.BlockSpec((1, blk, LANES), lambda b, h, i: (b, i, h)),
        out_shape=jax.ShapeDtypeStruct((bsz, seq, WIDTH), F32),
        scratch_shapes=[
            pltpu.VMEM((seq, LANES), BF16),
            pltpu.VMEM((seq, LANES), BF16),
            pltpu.VMEM((LANES, LANES), F32),
        ],
        compiler_params=_cparams(("parallel", "parallel", "arbitrary")),
        name="moba",
    )(p3d, p3d, p3d, bias_own, bias_prev, bias_far)


def _shifted(x, carry_row):
    rows = lax.broadcasted_iota(jnp.int32, x.shape, 0)
    return jnp.where(rows == 0, carry_row, pltpu.roll(x, 1, axis=0))


def _rwkv_prep_kernel(pr_ref, pk_ref, pv_ref, pl_ref, mu_ref, vec_ref, ww_ref, wa_ref, wg_ref,
                      bd_ref, tri_ref,
                      rt_ref, kt_ref, kd_ref, bd_out_ref, v_ref, g_ref, bonus_ref, pend_ref,
                      carry_ref, *, chunk):
    @pl.when(pl.program_id(1) == 0)
    def _():
        carry_ref[...] = jnp.zeros_like(carry_ref)

    def mix(ref, j):
        x = ref[0]
        mu = mu_ref[0:1, j * WIDTH:(j + 1) * WIDTH]
        prev = _shifted(x, carry_ref[0:1, j * WIDTH:(j + 1) * WIDTH])
        carry_ref[0:1, j * WIDTH:(j + 1) * WIDTH] = x[x.shape[0] - 1:, :]
        return x + mu * (prev - x)

    r = mix(pr_ref, 0)
    k = mix(pk_ref, 1)
    v = mix(pv_ref, 2)
    lo = mix(pl_ref, 3)
    w0, a0, k_k, k_a, r_k = (vec_ref[i:i + 1, :] for i in range(5))
    xwa = lo[:, 0:LANES]
    xg = lo[:, LANES:3 * LANES]
    lw = jnp.dot(jnp.tanh(xwa), ww_ref[...], precision=HI, preferred_element_type=F32)
    la = jnp.dot(xwa, wa_ref[...], precision=HI, preferred_element_type=F32)
    g = jnp.dot(jax.nn.sigmoid(xg), wg_ref[...], precision=HI, preferred_element_type=F32)
    z = -(w0 + lw)
    softplus = jnp.maximum(z, 0.0) + jnp.log(1.0 + jnp.exp(-jnp.abs(z)))
    logw = -jnp.exp(-softplus - 0.5)
    a = jax.nn.sigmoid(a0 + la)
    kk = k * k_k
    ss = jnp.dot(kk * kk, bd_ref[...], precision=HI, preferred_element_type=F32)
    kk = kk / jnp.maximum(jnp.sqrt(ss), 1e-12)
    k2 = k * (1.0 + (a - 1.0) * k_a)
    rk = jnp.dot(r * k2 * r_k, bd_ref[...], precision=HI, preferred_element_type=F32)
    cs = jnp.dot(tri_ref[...], logw, precision=HI, preferred_element_type=F32)
    e_pos = jnp.exp(cs)
    e_neg = jnp.exp(-cs)
    rt_ref[0] = r * e_pos
    kt_ref[0] = kk * jnp.exp(cs - logw)
    kd_ref[0] = k2 * e_neg
    bd_out_ref[0] = kk * a * e_neg
    v_ref[0] = v
    g_ref[0] = g
    bonus_ref[0] = rk * v
    ts = e_pos.shape[0]
    for c in range(ts // chunk):
        pend_ref[0, c:c + 1, :] = e_pos[(c + 1) * chunk - 1:(c + 1) * chunk, :]


def rwkv_prep(p3d, rwkv_mu, w0, w_lora_up, a0, a_lora_up, g_lora_up, k_k, k_a, r_k, *, ts=512):
    bsz, seq, _ = p3d.shape
    chunk = RWKV_CHUNK
    ts = min(ts, seq)
    mu = jnp.pad(rwkv_mu, (0, COL_B - COL_B_RAW)).reshape(1, COL_B)
    vec = jnp.stack([w0, a0, k_k, k_a, r_k.reshape(-1)] + [jnp.zeros_like(w0)] * 3).astype(F32)
    ww = jnp.zeros((LANES, WIDTH), F32).at[:DECAY_LORA].set(w_lora_up)
    wa = jnp.zeros((LANES, WIDTH), F32).at[DECAY_LORA:DECAY_LORA + AAA_LORA].set(a_lora_up)
    wg = jnp.zeros((2 * LANES, WIDTH), F32).at[:GATE_LORA].set(g_lora_up)
    hid = jnp.arange(WIDTH) // HEAD_DIM
    bd = (hid[:, None] == hid[None, :]).astype(F32)
    tix = jnp.arange(ts)
    tri = ((tix[:, None] // chunk == tix[None, :] // chunk) & (tix[None, :] <= tix[:, None])).astype(F32)
    c0 = COL_A // WIDTH
    big = jax.ShapeDtypeStruct((bsz, seq, WIDTH), F32)
    wspec = lambda shape: pl.BlockSpec(shape, lambda b, i: (0, 0))
    ospec = pl.BlockSpec((1, ts, WIDTH), lambda b, i: (b, i, 0))
    return pl.pallas_call(
        functools.partial(_rwkv_prep_kernel, chunk=chunk),
        grid=(bsz, seq // ts),
        in_specs=[
            pl.BlockSpec((1, ts, WIDTH), lambda b, i: (b, i, c0)),
            pl.BlockSpec((1, ts, WIDTH), lambda b, i: (b, i, c0 + 1)),
            pl.BlockSpec((1, ts, WIDTH), lambda b, i: (b, i, c0 + 2)),
            pl.BlockSpec((1, ts, WIDTH), lambda b, i: (b, i, c0 + 3)),
            wspec((1, COL_B)), wspec((8, WIDTH)), wspec((LANES, WIDTH)), wspec((LANES, WIDTH)),
            wspec((2 * LANES, WIDTH)), wspec((WIDTH, WIDTH)), wspec((ts, ts)),
        ],
        out_specs=[ospec] * 7 + [pl.BlockSpec((1, ts // chunk, WIDTH), lambda b, i: (b, i, 0))],
        out_shape=[big] * 7 + [jax.ShapeDtypeStruct((bsz, seq // chunk, WIDTH), F32)],
        scratch_shapes=[pltpu.VMEM((8, COL_B), F32)],
        compiler_params=_cparams(("parallel", "arbitrary")),
        name="rwkv_prep",
    )(p3d, p3d, p3d, p3d, mu, vec, ww, wa, wg, bd, tri)


def _rwkv_scan_kernel(rt_ref, kt_ref, kd_ref, bd_ref, v_ref, g_ref, bonus_ref, pend_ref, ln_ref, o_ref,
                      state_ref, *, chunk, prec):
    @pl.when(pl.program_id(1) == 0)
    def _():
        state_ref[...] = jnp.zeros_like(state_ref)

    c2 = 2 * chunk
    lane = lax.broadcasted_iota(jnp.int32, (chunk, LANES), 1)
    first = lane < HEAD_DIM
    row = lax.broadcasted_iota(jnp.int32, (c2, c2), 0)
    col = lax.broadcasted_iota(jnp.int32, (c2, c2), 1)
    eye = (row == col).astype(F32)
    hrow = lax.broadcasted_iota(jnp.int32, (LANES, LANES), 0) // HEAD_DIM
    hcol = lax.broadcasted_iota(jnp.int32, (LANES, LANES), 1) // HEAD_DIM
    head_mean = jnp.where(hrow == hcol, 1.0 / HEAD_DIM, 0.0).astype(F32)
    nt = (((1,), (1,)), ((), ()))
    tn = (((0,), (0,)), ((), ()))
    dot = functools.partial(jnp.dot, precision=prec, preferred_element_type=F32)
    dotg = functools.partial(lax.dot_general, precision=prec, preferred_element_type=F32)

    def stack(x):
        return jnp.concatenate([jnp.where(first, x, 0.0), jnp.where(first, 0.0, x)], axis=0)

    for hp in range(PAIRS):
        sl = slice(hp * LANES, (hp + 1) * LANES)
        rs, ks, kds, bs, vs = (stack(ref[0, :, sl]) for ref in (rt_ref, kt_ref, kd_ref, bd_ref, v_ref))
        pend = pend_ref[0, 0, 0:1, sl]
        big = dotg(jnp.concatenate([ks, rs], axis=0), jnp.concatenate([bs, kds], axis=0), nt)
        a_b = jnp.where(row > col, big[0:c2, 0:c2], 0.0)
        a_k = jnp.where(row > col, big[0:c2, c2:], 0.0)
        a_rb = jnp.where(row >= col, big[c2:, 0:c2], 0.0)
        a_rk = jnp.where(row >= col, big[c2:, c2:], 0.0)
        inv = eye - a_b
        pw = dot(a_b, a_b)
        n_sq = int(math.log2(chunk)) - 1
        for lvl in range(n_sq):
            inv = inv + dot(inv, pw)
            if lvl + 1 < n_sq:
                pw = dot(pw, pw)
        ht = state_ref[hp]
        rhs = dotg(ks, ht, nt) + dot(a_k, vs)
        us = dot(inv, rhs)
        os_ = dotg(rs, ht, nt) + dot(a_rk, vs) - dot(a_rb, us)
        o = os_[0:chunk] + os_[chunk:]
        state_ref[hp] = (ht + dotg(vs, kds, tn) - dotg(us, bs, tn)) * pend
        mu = jnp.dot(o, head_mean, precision=HI, preferred_element_type=F32)
        d = o - mu
        var = jnp.dot(d * d, head_mean, precision=HI, preferred_element_type=F32)
        on = d * lax.rsqrt(var + GN_EPS) * ln_ref[0:1, sl] + ln_ref[1:2, sl]
        o_ref[0, :, sl] = (on + bonus_ref[0, :, sl]) * g_ref[0, :, sl]


def rwkv_scan(rt, kt, kd, bd, v, g, bonus, pend, lnx_g, lnx_b, *, prec=None):
    bsz, seq, _ = rt.shape
    chunk = RWKV_CHUNK
    n_chunks = seq // chunk
    ln = jnp.stack([lnx_g, lnx_b] + [jnp.zeros_like(lnx_g)] * 6).astype(F32)
    pend4 = pend.reshape(bsz, n_chunks, 1, WIDTH)
    spec = pl.BlockSpec((1, chunk, WIDTH), lambda b, c: (b, c, 0))
    return pl.pallas_call(
        functools.partial(_rwkv_scan_kernel, chunk=chunk, prec=prec),
        grid=(bsz, n_chunks),
        in_specs=[spec] * 7 + [
            pl.BlockSpec((1, 1, 1, WIDTH), lambda b, c: (b, c, 0, 0)),
            pl.BlockSpec((8, WIDTH), lambda b, c: (0, 0)),
        ],
        out_specs=spec,
        out_shape=jax.ShapeDtypeStruct((bsz, seq, WIDTH), F32),
        scratch_shapes=[pltpu.VMEM((PAIRS, LANES, LANES), F32)],
        compiler_params=_cparams(("parallel", "arbitrary")),
        name="rwkv_scan",
    )(rt, kt, kd, bd, v, g, bonus, pend4, ln)


def _merge_kernel(x_ref, oa_ref, ob_ref, ga_ref, gb_ref, wa_ref, wb_ref, wo_ref, g2_ref,
                  h_ref, xn_ref, acc_ref):
    j = pl.program_id(1)

    @pl.when(j == 0)
    def _():
        acc_ref[...] = x_ref[...]

    ya = jnp.dot(oa_ref[...].astype(BF16), wa_ref[...], preferred_element_type=F32)
    yb = jnp.dot(ob_ref[...].astype(BF16), wb_ref[...], preferred_element_type=F32)
    y = jax.nn.sigmoid(ga_ref[...]) * ya + jax.nn.sigmoid(gb_ref[...]) * yb
    acc_ref[...] += jnp.dot(y.astype(BF16), wo_ref[...], preferred_element_type=F32)

    @pl.when(j == pl.num_programs(1) - 1)
    def _():
        h = acc_ref[...]
        h_ref[...] = h
        ms = jnp.mean(h * h, axis=-1, keepdims=True)
        xn_ref[...] = h * lax.rsqrt(ms + RMS_EPS) * g2_ref[...]


def merge_out(x2d, oa, ob, p2d, w_proj_a, w_proj_b, w_out, norm2_g, *, tm=512):
    t, d = x2d.shape
    tn = WIDTH
    nj = d // tn
    g0 = COL_G_OFF // tn
    big = jax.ShapeDtypeStruct((t, d), F32)
    return pl.pallas_call(
        _merge_kernel,
        grid=(t // tm, nj),
        in_specs=[
            pl.BlockSpec((tm, d), lambda i, j: (i, 0)),
            pl.BlockSpec((tm, WIDTH), lambda i, j: (i, 0)),
            pl.BlockSpec((tm, WIDTH), lambda i, j: (i, 0)),
            pl.BlockSpec((tm, tn), lambda i, j: (i, g0 + j)),
            pl.BlockSpec((tm, tn), lambda i, j: (i, g0 + nj + j)),
            pl.BlockSpec((WIDTH, tn), lambda i, j: (0, j)),
            pl.BlockSpec((WIDTH, tn), lambda i, j: (0, j)),
            pl.BlockSpec((tn, d), lambda i, j: (j, 0)),
            pl.BlockSpec((1, d), lambda i, j: (0, 0)),
        ],
        out_specs=[pl.BlockSpec((tm, d), lambda i, j: (i, 0))] * 2,
        out_shape=[big, big],
        scratch_shapes=[pltpu.VMEM((tm, d), F32)],
        compiler_params=_cparams(("parallel", "arbitrary")),
        name="merge_out",
    )(x2d, oa, ob, p2d, p2d, w_proj_a.astype(BF16), w_proj_b.astype(BF16), w_out.astype(BF16),
      norm2_g.reshape(1, d))


PEER_HEADS = 8
PEER_NKEYS = 128
PEER_TOPK = 16
PEER_HALF = 128


def _topk_rows(s, payload, k):
    n = s.shape[0]
    rows = lax.broadcasted_iota(jnp.int32, s.shape, 0)
    vals, pays = [], []
    for _ in range(k):
        m = jnp.max(s, axis=0, keepdims=True)
        first = jnp.min(jnp.where(s == m, rows, n), axis=0, keepdims=True)
        hit = rows == first
        vals.append(m)
        pays.append(jnp.max(jnp.where(hit, payload, -1), axis=0, keepdims=True))
        s = jnp.where(hit, -jnp.inf, s)
    return jnp.concatenate(vals, axis=0), jnp.concatenate(pays, axis=0)


def _peer_route_kernel(xn_ref, wq_ref, sk_ref, idx_ref, gate_ref, *, prec):
    tt = xn_ref.shape[0]
    k = PEER_TOPK
    q = jnp.dot(xn_ref[...].astype(wq_ref.dtype), wq_ref[...], precision=prec, preferred_element_type=F32)
    key_iota = lax.broadcasted_iota(jnp.int32, (PEER_NKEYS, tt), 0)
    nt = (((1,), (1,)), ((), ()))
    idx_rows, gate_rows = [], []
    for h in range(PEER_HEADS):
        tops = []
        for p in range(2):
            c0 = (h * 2 + p) * PEER_HALF
            s = lax.dot_general(sk_ref[h, p].astype(wq_ref.dtype), q[:, c0:c0 + PEER_HALF].astype(wq_ref.dtype),
                                nt, precision=prec, preferred_element_type=F32)
            tops.append(_topk_rows(s, key_iota, k))
        (s0, i0), (s1, i1) = tops
        half = k // 2
        cs = [s0[0:1] + s1] + [s0[i:i + 1] + s1[0:half] for i in range(1, half)] + [s0[half:] + s1[0:1]]
        ci = [i0[0:1] * PEER_NKEYS + i1] + [i0[i:i + 1] * PEER_NKEYS + i1[0:half] for i in range(1, half)] \
            + [i0[half:] * PEER_NKEYS + i1[0:1]]
        best_s, best_i = _topk_rows(jnp.concatenate(cs, axis=0), jnp.concatenate(ci, axis=0), k)
        e = jnp.exp(best_s - best_s[0:1])
        gate_rows.append(e / jnp.sum(e, axis=0, keepdims=True))
        idx_rows.append(best_i)
    idx_ref[...] = jnp.concatenate(idx_rows, axis=0).T
    gate_ref[...] = jnp.concatenate(gate_rows, axis=0).T


def peer_route(xn2d, peer_wq, peer_subkeys, *, tt=256, prec=None, wdtype=BF16):
    t, d = xn2d.shape
    nq = peer_wq.shape[1]
    n_sel = PEER_HEADS * PEER_TOPK
    return pl.pallas_call(
        functools.partial(_peer_route_kernel, prec=prec),
        grid=(t // tt,),
        in_specs=[
            pl.BlockSpec((tt, d), lambda i: (i, 0)),
            pl.BlockSpec((d, nq), lambda i: (0, 0)),
            pl.BlockSpec((PEER_HEADS, 2, PEER_NKEYS, PEER_HALF), lambda i: (0, 0, 0, 0)),
        ],
        out_specs=[pl.BlockSpec((tt, n_sel), lambda i: (i, 0))] * 2,
        out_shape=[jax.ShapeDtypeStruct((t, n_sel), jnp.int32), jax.ShapeDtypeStruct((t, n_sel), F32)],
        compiler_params=_cparams(("parallel",)),
        name="peer_route",
    )(xn2d, peer_wq.astype(wdtype), peer_subkeys)


def _final_kernel(h_ref, y_ref, g_ref, o_ref):
    h = h_ref[...] + y_ref[...]
    ms = jnp.mean(h * h, axis=-1, keepdims=True)
    o_ref[...] = h * lax.rsqrt(ms + RMS_EPS) * g_ref[...]


def final_norm(h2d, y2d, g, *, tm=1024):
    t, d = h2d.shape
    spec = pl.BlockSpec((tm, d), lambda i: (i, 0))
    return pl.pallas_call(
        _final_kernel,
        grid=(t // tm,),
        in_specs=[spec, spec, pl.BlockSpec((1, d), lambda i: (0, 0))],
        out_specs=spec,
        out_shape=jax.ShapeDtypeStruct((t, d), F32),
        compiler_params=_cparams(("parallel",)),
        name="final_norm",
    )(h2d, y2d, g.reshape(1, d))


SC_CORES = 2
SC_SUBCORES = 16
SC_LANES = 16
SC_WORKERS = SC_CORES * SC_SUBCORES
PEER_SEL = PEER_HEADS * PEER_TOPK
PEER_ROWS = 64
PEER_GROUP = 32


def _pack_rows(w):
    half = w.shape[1] // 2
    bits = lax.bitcast_convert_type(w.astype(BF16), jnp.uint16).astype(jnp.uint32)
    return lax.bitcast_convert_type(bits[:, :half] | (bits[:, half:] << 16), jnp.int32)


def _unpack_words(w):
    lo = lax.bitcast_convert_type(lax.shift_left(w, jnp.int32(16)), F32)
    hi = lax.bitcast_convert_type(lax.bitwise_and(w, jnp.int32(-65536)), F32)
    return lo, hi


def _sc_mesh():
    from jax.experimental.pallas import tpu_sc as plsc
    return plsc.VectorSubcoreMesh(core_axis_name="c", subcore_axis_name="s",
                                  num_cores=SC_CORES, num_subcores=SC_SUBCORES)


def _sc_loop(n, body, carry):
    from jax.experimental.pallas import tpu_sc as plsc
    return plsc.parallel_loop(0, n, carry=carry)(body)


def _worker_base(tokens_per_worker):
    return (lax.axis_index("s") * SC_CORES + lax.axis_index("c")) * tokens_per_worker


def _gather_compute_loop(table_hbm, idx_v, rows_v, sem, stage_v, out_row, osem, grp, compute):
    def gather(j, b):
        return pltpu.make_async_copy(table_hbm.at[idx_v.at[j]], rows_v.at[b], sem.at[b])

    def put(i, slot):
        return pltpu.make_async_copy(stage_v.at[slot], out_row(i), osem.at[slot])

    gather(0, 0).start()

    @pl.loop(0, 2 * grp)
    def _(j):
        b = lax.bitwise_and(j, 1)
        i = lax.shift_right_logical(j, 1)
        slot = lax.bitwise_and(i, 1)

        @pl.when((b == 0) & (i >= 2))
        def _():
            put(i - 2, slot).wait()

        @pl.when(j + 1 < 2 * grp)
        def _():
            gather(j + 1, 1 - b).start()

        gather(j, b).wait()
        compute(i, b, b, slot)

        @pl.when(b == 1)
        def _():
            put(i, slot).start()

    put(grp - 2, 0).wait()
    put(grp - 1, 1).wait()


def peer_expert_dots(xn2d, idx2, u_packed):
    t, d = xn2d.shape
    half = d // 2
    n_chunks = half // SC_LANES
    tpw = t // SC_WORKERS
    grp = min(PEER_GROUP, tpw)
    rows_tog = 4
    n_acc = 2
    from jax.experimental.pallas import tpu_sc as plsc

    def body(x_hbm, idx_hbm, u_hbm, out_hbm, idx_v, x_v, rows_v, ps_v, sem, osem):
        base = _worker_base(tpw)

        def compute(i, h, b, slot):
            @pl.loop(0, PEER_ROWS // rows_tog)
            def _(rg):
                r0 = rg * rows_tog
                accs = [[None] * n_acc for _ in range(rows_tog)]
                for c in range(n_chunks):
                    xl = x_v[i, pl.ds(c * SC_LANES, SC_LANES)]
                    xh = x_v[i, pl.ds(half + c * SC_LANES, SC_LANES)]
                    for r in range(rows_tog):
                        lo, hi = _unpack_words(rows_v[b, r0 + r, pl.ds(c * SC_LANES, SC_LANES)])
                        term = lo * xl + hi * xh
                        k = c % n_acc
                        accs[r][k] = term if accs[r][k] is None else accs[r][k] + term
                for r in range(rows_tog):
                    at = pl.ds(pl.multiple_of((h * PEER_ROWS + r0 + r) * SC_LANES, SC_LANES), SC_LANES)
                    ps_v[slot, at] = accs[r][0] + accs[r][1]

        @pl.loop(0, tpw // grp)
        def _(g):
            t0 = base + g * grp
            pltpu.sync_copy(idx_hbm.at[pl.ds(2 * t0, 2 * grp)], idx_v)
            pltpu.sync_copy(x_hbm.at[pl.ds(t0, grp)], x_v)
            _gather_compute_loop(u_hbm, idx_v, rows_v, sem, ps_v, lambda i: out_hbm.at[t0 + i], osem, grp, compute)

    return pl.kernel(
        body,
        out_type=jax.ShapeDtypeStruct((t, PEER_SEL * SC_LANES), F32),
        mesh=_sc_mesh(),
        scratch_types=[
            pltpu.VMEM((2 * grp, PEER_ROWS), jnp.int32),
            pltpu.VMEM((grp, d), F32),
            pltpu.VMEM((2, PEER_ROWS, half), jnp.int32),
            pltpu.VMEM((2, PEER_SEL * SC_LANES), F32),
            pltpu.SemaphoreType.DMA((2,)),
            pltpu.SemaphoreType.DMA((2,)),
        ],
        compiler_params=pltpu.CompilerParams(needs_layout_passes=False),
        name="peer_expert_dots",
    )(xn2d, idx2, u_packed)


def peer_expert_mix(hgx, idx2, v_packed):
    t = hgx.shape[0]
    half = v_packed.shape[1]
    d = 2 * half
    tpw = t // SC_WORKERS
    grp = min(PEER_GROUP // 2, tpw)
    n_parts = 2
    cpp = half // SC_LANES // n_parts

    def body(hg_hbm, idx_hbm, v_hbm, out_hbm, idx_v, hg_v, rows_v, o_v2, sem, osem):
        base = _worker_base(tpw)

        def compute(i, h, b, slot):
            for part in range(n_parts):
                def rbody(r, accs):
                    s = hg_v[i, pl.ds(pl.multiple_of((h * PEER_ROWS + r) * SC_LANES, SC_LANES), SC_LANES)]
                    new = []
                    for c in range(cpp):
                        lo, hi = _unpack_words(rows_v[b, r, pl.ds((part * cpp + c) * SC_LANES, SC_LANES)])
                        new.append(accs[2 * c] + s * lo)
                        new.append(accs[2 * c + 1] + s * hi)
                    return tuple(new)

                accs = _sc_loop(PEER_ROWS, rbody, tuple(jnp.zeros((SC_LANES,), F32) for _ in range(2 * cpp)))
                def store(overwrite):
                    for c in range(cpp):
                        lo_at = pl.ds((part * cpp + c) * SC_LANES, SC_LANES)
                        hi_at = pl.ds(half + (part * cpp + c) * SC_LANES, SC_LANES)
                        if overwrite:
                            o_v2[slot, lo_at] = accs[2 * c]
                            o_v2[slot, hi_at] = accs[2 * c + 1]
                        else:
                            o_v2[slot, lo_at] = o_v2[slot, lo_at] + accs[2 * c]
                            o_v2[slot, hi_at] = o_v2[slot, hi_at] + accs[2 * c + 1]

                pl.when(h == 0)(functools.partial(store, True))
                pl.when(h != 0)(functools.partial(store, False))

        @pl.loop(0, tpw // grp)
        def _(g):
            t0 = base + g * grp
            pltpu.sync_copy(idx_hbm.at[pl.ds(2 * t0, 2 * grp)], idx_v)
            pltpu.sync_copy(hg_hbm.at[pl.ds(t0, grp)], hg_v)
            _gather_compute_loop(v_hbm, idx_v, rows_v, sem, o_v2, lambda i: out_hbm.at[t0 + i], osem, grp, compute)

    return pl.kernel(
        body,
        out_type=jax.ShapeDtypeStruct((t, d), F32),
        mesh=_sc_mesh(),
        scratch_types=[
            pltpu.VMEM((2 * grp, PEER_ROWS), jnp.int32),
            pltpu.VMEM((grp, PEER_SEL * SC_LANES), F32),
            pltpu.VMEM((2, PEER_ROWS, half), jnp.int32),
            pltpu.VMEM((2, d), F32),
            pltpu.SemaphoreType.DMA((2,)),
            pltpu.SemaphoreType.DMA((2,)),
        ],
        compiler_params=pltpu.CompilerParams(needs_layout_passes=False),
        name="peer_expert_mix",
    )(hgx, idx2, v_packed)


def _peer_act_kernel(ps_ref, gate_ref, sum_ref, o_ref):
    pre = jnp.dot(ps_ref[...], sum_ref[...], precision=HI, preferred_element_type=F32)
    hg = 0.5 * pre * (1.0 + lax.erf(pre * (1.0 / math.sqrt(2.0)))) * gate_ref[...]
    spread = (((1,), (1,)), ((), ()))
    o_ref[...] = lax.dot_general(hg, sum_ref[...], spread, precision=HI, preferred_element_type=F32)


def peer_act(ps, gates, *, tm=512):
    t, n = ps.shape
    lane_sum = (jnp.arange(n)[:, None] // SC_LANES == jnp.arange(PEER_SEL)[None, :]).astype(F32)
    return pl.pallas_call(
        _peer_act_kernel,
        grid=(t // tm,),
        in_specs=[
            pl.BlockSpec((tm, n), lambda i: (i, 0)),
            pl.BlockSpec((tm, PEER_SEL), lambda i: (i, 0)),
            pl.BlockSpec((n, PEER_SEL), lambda i: (0, 0)),
        ],
        out_specs=pl.BlockSpec((tm, n), lambda i: (i, 0)),
        out_shape=jax.ShapeDtypeStruct((t, n), F32),
        compiler_params=_cparams(("parallel",)),
        name="peer_act",
    )(ps, gates, lane_sum)


def peer_experts(xn2d, idx, gates, u_packed, v_packed):
    idx2 = idx.reshape(-1, PEER_ROWS)
    ps = peer_expert_dots(xn2d, idx2, u_packed)
    hgx = peer_act(ps, gates)
    return peer_expert_mix(hgx, idx2, v_packed)


BATCH_GROUPS = 4


def kernel(x, norm1_g, w_in, rwkv_mu, w0, w_lora_up, a0, a_lora_up, g_lora_up, k_k, k_a, r_k, lnx_g, lnx_b,
           w_proj_a, w_proj_b, w_out, norm2_g, peer_wq, peer_subkeys, peer_u, peer_v, rel_bias, normf_g):
    bsz, seq, d = x.shape
    depth = norm1_g.shape[0]
    groups = BATCH_GROUPS if bsz % BATCH_GROUPS == 0 else 1
    gb = bsz // groups
    tg = gb * seq
    hs = [x[g * gb:(g + 1) * gb].reshape(tg, d) for g in range(groups)]
    for l in range(depth):
        w_pad = jnp.concatenate([
            w_in[l][:, :COL_A + COL_B_RAW],
            jnp.zeros((d, COL_B - COL_B_RAW), w_in.dtype),
            w_in[l][:, COL_A + COL_B_RAW:]], axis=1).astype(BF16)
        u_packed = _pack_rows(peer_u[l])
        v_packed = _pack_rows(peer_v[l])
        last = l == depth - 1
        for g in range(groups):
            p2d = norm_proj(hs[g], norm1_g[l], w_pad)
            p3d = p2d.reshape(gb, seq, -1)
            oa = moba_attention(p3d, rel_bias)
            prep = rwkv_prep(p3d, rwkv_mu[l], w0[l], w_lora_up[l], a0[l], a_lora_up[l], g_lora_up[l],
                             k_k[l], k_a[l], r_k[l])
            ob = rwkv_scan(*prep, lnx_g[l], lnx_b[l])
            h2d, xn2 = merge_out(hs[g], oa.reshape(tg, WIDTH), ob.reshape(tg, WIDTH), p2d,
                                 w_proj_a[l], w_proj_b[l], w_out[l], norm2_g[l])
            idx, gates = peer_route(xn2, peer_wq[l], peer_subkeys[l])
            y2d = peer_experts(xn2, idx, gates, u_packed, v_packed)
            hs[g] = final_norm(h2d, y2d, normf_g) if last else h2d + y2d
    return jnp.concatenate(hs, axis=0).reshape(bsz, seq, d)
```

```python
import functools
import math

import jax
import jax.numpy as jnp
from jax import lax
from jax.experimental import pallas as pl
from jax.experimental.pallas import tpu as pltpu

F32 = jnp.float32
BF16 = jnp.bfloat16
HI = lax.Precision.HIGHEST

LANES = 128
HEAD_DIM = 64
HEADS = 8
PAIRS = HEADS // 2
WIDTH = HEADS * HEAD_DIM
MOBA_BLOCK = 256
MOBA_TOPK = 3
REL_BUCKETS = 32
REL_MAX_DIST = 128
DECAY_LORA = 64
AAA_LORA = 64
GATE_LORA = 160
GN_EPS = 64e-5
RMS_EPS = 1e-6
NEG = -1e30
RWKV_CHUNK = 64
COL_A = 3 * WIDTH
COL_B_RAW = 3 * WIDTH + DECAY_LORA + AAA_LORA + GATE_LORA
COL_B = 4 * WIDTH
COL_G_OFF = COL_A + COL_B
VMEM_LIMIT = 56 * 1024 * 1024


def _cparams(sem):
    return pltpu.CompilerParams(dimension_semantics=sem, vmem_limit_bytes=VMEM_LIMIT)


def _norm_proj_kernel(x_ref, g_ref, w_ref, o_ref, xn_ref):
    @pl.when(pl.program_id(1) == 0)
    def _():
        x = x_ref[...]
        ms = jnp.mean(x * x, axis=-1, keepdims=True)
        xn_ref[...] = (x * lax.rsqrt(ms + RMS_EPS) * g_ref[...]).astype(xn_ref.dtype)

    o_ref[...] = jnp.dot(xn_ref[...], w_ref[...], preferred_element_type=F32).astype(o_ref.dtype)


def norm_proj(x2d, g, w, *, tm=512, tn=512, out_dtype=F32):
    t, d = x2d.shape
    n = w.shape[1]
    return pl.pallas_call(
        _norm_proj_kernel,
        grid=(t // tm, n // tn),
        in_specs=[
            pl.BlockSpec((tm, d), lambda i, j: (i, 0)),
            pl.BlockSpec((1, d), lambda i, j: (0, 0)),
            pl.BlockSpec((d, tn), lambda i, j: (0, j)),
        ],
        out_specs=pl.BlockSpec((tm, tn), lambda i, j: (i, j)),
        out_shape=jax.ShapeDtypeStruct((t, n), out_dtype),
        scratch_shapes=[pltpu.VMEM((tm, d), w.dtype)],
        compiler_params=_cparams(("parallel", "arbitrary")),
        name="norm_proj",
    )(x2d, g.reshape(1, d), w)


def _rel_bucket(dist):
    n = jnp.maximum(dist, 0)
    max_exact = REL_BUCKETS // 2
    nf = jnp.maximum(n, 1).astype(F32)
    large = max_exact + (jnp.log(nf / max_exact) / math.log(REL_MAX_DIST / max_exact)
                         * (REL_BUCKETS - max_exact)).astype(jnp.int32)
    large = jnp.minimum(large, REL_BUCKETS - 1)
    return jnp.where(n < max_exact, n, large)


def _moba_kernel(q_ref, k_ref, v_ref, bown_ref, bprev_ref, bfar_ref, o_ref,
                 kb_ref, vb_ref, kbar_ref, *, n_blocks):
    qb = pl.program_id(2)
    blk = MOBA_BLOCK
    scale = 1.0 / math.sqrt(HEAD_DIM)

    @pl.when(qb == 0)
    def _():
        kbar_ref[...] = jnp.zeros_like(kbar_ref)
        for n in range(n_blocks):
            kblk = k_ref[0, n * blk:(n + 1) * blk, :]
            kbar_ref[n:n + 1, :] = jnp.mean(kblk, axis=0, keepdims=True)
        kb_ref[...] = k_ref[0].astype(BF16)
        vb_ref[...] = v_ref[0].astype(BF16)

    q2 = q_ref[0]
    lane = lax.broadcasted_iota(jnp.int32, (blk, LANES), 1)
    row = lax.broadcasted_iota(jnp.int32, (blk, blk), 0)
    col = lax.broadcasted_iota(jnp.int32, (blk, blk), 1)
    own0 = pl.multiple_of(qb * blk, blk)
    k_own = kb_ref[pl.ds(own0, blk), :]
    v_own = vb_ref[pl.ds(own0, blk), :]
    prev0 = pl.multiple_of(jnp.maximum(qb - 1, 0) * blk, blk)
    k_prev = kb_ref[pl.ds(prev0, blk), :]
    v_prev = vb_ref[pl.ds(prev0, blk), :]
    nt = (((1,), (1,)), ((), ()))

    outs = []
    for hh in range(2):
        hmask = (lane >= hh * HEAD_DIM) & (lane < (hh + 1) * HEAD_DIM)
        qh = jnp.where(hmask, q2, 0.0)
        gate = lax.dot_general(qh, kbar_ref[...], nt, precision=HI, preferred_element_type=F32)
        g = jnp.where(lane < qb, gate, -jnp.inf)
        sel = []
        for _ in range(MOBA_TOPK):
            m = jnp.max(g, axis=1, keepdims=True)
            idx = jnp.min(jnp.where(g == m, lane, LANES), axis=1, keepdims=True)
            idx = jnp.where(m > -jnp.inf, idx, LANES)
            sel.append(idx)
            g = jnp.where(lane == idx, -jnp.inf, g)

        def picked(n):
            return (sel[0] == n) | (sel[1] == n) | (sel[2] == n)

        qs = (qh * scale).astype(BF16)
        s = lax.dot_general(qs, k_own, nt, preferred_element_type=F32) + bown_ref[hh]
        s = jnp.where(row >= col, s, NEG)
        m_i = jnp.max(s, axis=1, keepdims=True)
        p = jnp.exp(s - m_i)
        l_i = jnp.sum(p, axis=1, keepdims=True)
        acc = jnp.dot(p.astype(BF16), v_own, preferred_element_type=F32)

        def update(carry, s, vblk):
            m_i, l_i, acc = carry
            m_new = jnp.maximum(m_i, jnp.max(s, axis=1, keepdims=True))
            alpha = jnp.exp(m_i - m_new)
            p = jnp.exp(s - m_new)
            l_new = alpha * l_i + jnp.sum(p, axis=1, keepdims=True)
            acc_new = alpha * acc + jnp.dot(p.astype(BF16), vblk, preferred_element_type=F32)
            return m_new, l_new, acc_new

        s = lax.dot_general(qs, k_prev, nt, preferred_element_type=F32) + bprev_ref[hh]
        s = jnp.where(picked(qb - 1), s, NEG)
        carry = update((m_i, l_i, acc), s, v_prev)

        bfar = bfar_ref[hh, 0:1, 0:1]

        def body(n, carry):
            n0 = pl.multiple_of(n * blk, blk)
            kblk = kb_ref[pl.ds(n0, blk), :]
            vblk = vb_ref[pl.ds(n0, blk), :]
            s = lax.dot_general(qs, kblk, nt, preferred_element_type=F32) + bfar
            s = jnp.where(picked(n), s, NEG)
            return update(carry, s, vblk)

        m_i, l_i, acc = lax.fori_loop(0, jnp.maximum(qb - 1, 0), body, carry)
        outs.append(acc / l_i)

    o_ref[0] = jnp.where(lane < HEAD_DIM, outs[0], outs[1])


def moba_attention(p3d, rel_bias):
    bsz, seq, _ = p3d.shape
    blk = MOBA_BLOCK
    n_blocks = seq // blk
    span = 2 * blk
    by_dist = rel_bias[:, _rel_bucket(jnp.arange(span))].astype(F32)
    shift = jnp.arange(span)

    def toeplitz(c):
        k = jnp.where(shift < blk, shift, shift - span)
        s = by_dist[:, jnp.clip(c - k, 0, span - 1)]
        tiled = jnp.tile(s, (1, blk))[:, :blk * (span - 1)]
        return tiled.reshape(HEADS, blk, span - 1)[:, :, :blk]

    bias_own = toeplitz(0)
    bias_prev = toeplitz(blk)
    bias_far = jnp.broadcast_to(rel_bias[:, REL_BUCKETS - 1].astype(F32)[:, None, None], (HEADS, 8, LANES))
    kern = functools.partial(_moba_kernel, n_blocks=n_blocks)
    return pl.pallas_call(
        kern,
        grid=(bsz, PAIRS, n_blocks),
        in_specs=[
            pl.BlockSpec((1, blk, LANES), lambda b, h, i: (b, i, h)),
            pl.BlockSpec((1, seq, LANES), lambda b, h, i: (b, 0, PAIRS + h)),
            pl.BlockSpec((1, seq, LANES), lambda b, h, i: (b, 0, 2 * PAIRS + h)),
            pl.BlockSpec((2, blk, blk), lambda b, h, i: (h, 0, 0)),
            pl.BlockSpec((2, blk, blk), lambda b, h, i: (h, 0, 0)),
            pl.BlockSpec((2, 8, LANES), lambda b, h, i: (h, 0, 0)),
        ],
        out_specs=pl.BlockSpec((1, blk, LANES), lambda b, h, i: (b, i, h)),
        out_shape=jax.ShapeDtypeStruct((bsz, seq, WIDTH), F32),
        scratch_shapes=[
            pltpu.VMEM((seq, LANES), BF16),
            pltpu.VMEM((seq, LANES), BF16),
            pltpu.VMEM((LANES, LANES), F32),
        ],
        compiler_params=_cparams(("parallel", "parallel", "arbitrary")),
        name="moba",
    )(p3d, p3d, p3d, bias_own, bias_prev, bias_far)


def _shifted(x, carry_row):
    rows = lax.broadcasted_iota(jnp.int32, x.shape, 0)
    return jnp.where(rows == 0, carry_row, pltpu.roll(x, 1, axis=0))


def _rwkv_prep_kernel(pr_ref, pk_ref, pv_ref, pl_ref, mu_ref, vec_ref, ww_ref, wa_ref, wg_ref,
                      bd_ref, tri_ref,
                      rt_ref, kt_ref, kd_ref, bd_out_ref, v_ref, g_ref, bonus_ref, pend_ref,
                      carry_ref, *, chunk):
    @pl.when(pl.program_id(1) == 0)
    def _():
        carry_ref[...] = jnp.zeros_like(carry_ref)

    def mix(ref, j):
        x = ref[0]
        mu = mu_ref[0:1, j * WIDTH:(j + 1) * WIDTH]
        prev = _shifted(x, carry_ref[0:1, j * WIDTH:(j + 1) * WIDTH])
        carry_ref[0:1, j * WIDTH:(j + 1) * WIDTH] = x[x.shape[0] - 1:, :]
        return x + mu * (prev - x)

    r = mix(pr_ref, 0)
    k = mix(pk_ref, 1)
    v = mix(pv_ref, 2)
    lo = mix(pl_ref, 3)
    w0, a0, k_k, k_a, r_k = (vec_ref[i:i + 1, :] for i in range(5))
    xwa = lo[:, 0:LANES]
    xg = lo[:, LANES:3 * LANES]
    lw = jnp.dot(jnp.tanh(xwa), ww_ref[...], precision=HI, preferred_element_type=F32)
    la = jnp.dot(xwa, wa_ref[...], precision=HI, preferred_element_type=F32)
    g = jnp.dot(jax.nn.sigmoid(xg), wg_ref[...], precision=HI, preferred_element_type=F32)
    z = -(w0 + lw)
    softplus = jnp.maximum(z, 0.0) + jnp.log(1.0 + jnp.exp(-jnp.abs(z)))
    logw = -jnp.exp(-softplus - 0.5)
    a = jax.nn.sigmoid(a0 + la)
    kk = k * k_k
    ss = jnp.dot(kk * kk, bd_ref[...], precision=HI, preferred_element_type=F32)
    kk = kk / jnp.maximum(jnp.sqrt(ss), 1e-12)
    k2 = k * (1.0 + (a - 1.0) * k_a)
    rk = jnp.dot(r * k2 * r_k, bd_ref[...], precision=HI, preferred_element_type=F32)
    cs = jnp.dot(tri_ref[...], logw, precision=HI, preferred_element_type=F32)
    e_pos = jnp.exp(cs)
    e_neg = jnp.exp(-cs)
    rt_ref[0] = r * e_pos
    kt_ref[0] = kk * jnp.exp(cs - logw)
    kd_ref[0] = k2 * e_neg
    bd_out_ref[0] = kk * a * e_neg
    v_ref[0] = v
    g_ref[0] = g
    bonus_ref[0] = rk * v
    ts = e_pos.shape[0]
    for c in range(ts // chunk):
        pend_ref[0, c:c + 1, :] = e_pos[(c + 1) * chunk - 1:(c + 1) * chunk, :]


def rwkv_prep(p3d, rwkv_mu, w0, w_lora_up, a0, a_lora_up, g_lora_up, k_k, k_a, r_k, *, ts=512):
    bsz, seq, _ = p3d.shape
    chunk = RWKV_CHUNK
    ts = min(ts, seq)
    mu = jnp.pad(rwkv_mu, (0, COL_B - COL_B_RAW)).reshape(1, COL_B)
    vec = jnp.stack([w0, a0, k_k, k_a, r_k.reshape(-1)] + [jnp.zeros_like(w0)] * 3).astype(F32)
    ww = jnp.zeros((LANES, WIDTH), F32).at[:DECAY_LORA].set(w_lora_up)
    wa = jnp.zeros((LANES, WIDTH), F32).at[DECAY_LORA:DECAY_LORA + AAA_LORA].set(a_lora_up)
    wg = jnp.zeros((2 * LANES, WIDTH), F32).at[:GATE_LORA].set(g_lora_up)
    hid = jnp.arange(WIDTH) // HEAD_DIM
    bd = (hid[:, None] == hid[None, :]).astype(F32)
    tix = jnp.arange(ts)
    tri = ((tix[:, None] // chunk == tix[None, :] // chunk) & (tix[None, :] <= tix[:, None])).astype(F32)
    c0 = COL_A // WIDTH
    big = jax.ShapeDtypeStruct((bsz, seq, WIDTH), F32)
    wspec = lambda shape: pl.BlockSpec(shape, lambda b, i: (0, 0))
    ospec = pl.BlockSpec((1, ts, WIDTH), lambda b, i: (b, i, 0))
    return pl.pallas_call(
        functools.partial(_rwkv_prep_kernel, chunk=chunk),
        grid=(bsz, seq // ts),
        in_specs=[
            pl.BlockSpec((1, ts, WIDTH), lambda b, i: (b, i, c0)),
            pl.BlockSpec((1, ts, WIDTH), lambda b, i: (b, i, c0 + 1)),
            pl.BlockSpec((1, ts, WIDTH), lambda b, i: (b, i, c0 + 2)),
            pl.BlockSpec((1, ts, WIDTH), lambda b, i: (b, i, c0 + 3)),
            wspec((1, COL_B)), wspec((8, WIDTH)), wspec((LANES, WIDTH)), wspec((LANES, WIDTH)),
            wspec((2 * LANES, WIDTH)), wspec((WIDTH, WIDTH)), wspec((ts, ts)),
        ],
        out_specs=[ospec] * 7 + [pl.BlockSpec((1, ts // chunk, WIDTH), lambda b, i: (b, i, 0))],
        out_shape=[big] * 7 + [jax.ShapeDtypeStruct((bsz, seq // chunk, WIDTH), F32)],
        scratch_shapes=[pltpu.VMEM((8, COL_B), F32)],
        compiler_params=_cparams(("parallel", "arbitrary")),
        name="rwkv_prep",
    )(p3d, p3d, p3d, p3d, mu, vec, ww, wa, wg, bd, tri)


def _rwkv_scan_kernel(rt_ref, kt_ref, kd_ref, bd_ref, v_ref, g_ref, bonus_ref, pend_ref, ln_ref, o_ref,
                      state_ref, *, chunk, prec):
    @pl.when(pl.program_id(1) == 0)
    def _():
        state_ref[...] = jnp.zeros_like(state_ref)

    c2 = 2 * chunk
    lane = lax.broadcasted_iota(jnp.int32, (chunk, LANES), 1)
    first = lane < HEAD_DIM
    row = lax.broadcasted_iota(jnp.int32, (c2, c2), 0)
    col = lax.broadcasted_iota(jnp.int32, (c2, c2), 1)
    eye = (row == col).astype(F32)
    hrow = lax.broadcasted_iota(jnp.int32, (LANES, LANES), 0) // HEAD_DIM
    hcol = lax.broadcasted_iota(jnp.int32, (LANES, LANES), 1) // HEAD_DIM
    head_mean = jnp.where(hrow == hcol, 1.0 / HEAD_DIM, 0.0).astype(F32)
    nt = (((1,), (1,)), ((), ()))
    tn = (((0,), (0,)), ((), ()))
    dot = functools.partial(jnp.dot, precision=prec, preferred_element_type=F32)
    dotg = functools.partial(lax.dot_general, precision=prec, preferred_element_type=F32)

    def stack(x):
        return jnp.concatenate([jnp.where(first, x, 0.0), jnp.where(first, 0.0, x)], axis=0)

    for hp in range(PAIRS):
        sl = slice(hp * LANES, (hp + 1) * LANES)
        rs, ks, kds, bs, vs = (stack(ref[0, :, sl]) for ref in (rt_ref, kt_ref, kd_ref, bd_ref, v_ref))
        pend = pend_ref[0, 0, 0:1, sl]
        big = dotg(jnp.concatenate([ks, rs], axis=0), jnp.concatenate([bs, kds], axis=0), nt)
        a_b = jnp.where(row > col, big[0:c2, 0:c2], 0.0)
        a_k = jnp.where(row > col, big[0:c2, c2:], 0.0)
        a_rb = jnp.where(row >= col, big[c2:, 0:c2], 0.0)
        a_rk = jnp.where(row >= col, big[c2:, c2:], 0.0)
        inv = eye - a_b
        pw = dot(a_b, a_b)
        n_sq = int(math.log2(chunk)) - 1
        for lvl in range(n_sq):
            inv = inv + dot(inv, pw)
            if lvl + 1 < n_sq:
                pw = dot(pw, pw)
        ht = state_ref[hp]
        rhs = dotg(ks, ht, nt) + dot(a_k, vs)
        us = dot(inv, rhs)
        os_ = dotg(rs, ht, nt) + dot(a_rk, vs) - dot(a_rb, us)
        o = os_[0:chunk] + os_[chunk:]
        state_ref[hp] = (ht + dotg(vs, kds, tn) - dotg(us, bs, tn)) * pend
        mu = jnp.dot(o, head_mean, precision=HI, preferred_element_type=F32)
        d = o - mu
        var = jnp.dot(d * d, head_mean, precision=HI, preferred_element_type=F32)
        on = d * lax.rsqrt(var + GN_EPS) * ln_ref[0:1, sl] + ln_ref[1:2, sl]
        o_ref[0, :, sl] = (on + bonus_ref[0, :, sl]) * g_ref[0, :, sl]


def rwkv_scan(rt, kt, kd, bd, v, g, bonus, pend, lnx_g, lnx_b, *, prec=None):
    bsz, seq, _ = rt.shape
    chunk = RWKV_CHUNK
    n_chunks = seq // chunk
    ln = jnp.stack([lnx_g, lnx_b] + [jnp.zeros_like(lnx_g)] * 6).astype(F32)
    pend4 = pend.reshape(bsz, n_chunks, 1, WIDTH)
    spec = pl.BlockSpec((1, chunk, WIDTH), lambda b, c: (b, c, 0))
    return pl.pallas_call(
        functools.partial(_rwkv_scan_kernel, chunk=chunk, prec=prec),
        grid=(bsz, n_chunks),
        in_specs=[spec] * 7 + [
            pl.BlockSpec((1, 1, 1, WIDTH), lambda b, c: (b, c, 0, 0)),
            pl.BlockSpec((8, WIDTH), lambda b, c: (0, 0)),
        ],
        out_specs=spec,
        out_shape=jax.ShapeDtypeStruct((bsz, seq, WIDTH), F32),
        scratch_shapes=[pltpu.VMEM((PAIRS, LANES, LANES), F32)],
        compiler_params=_cparams(("parallel", "arbitrary")),
        name="rwkv_scan",
    )(rt, kt, kd, bd, v, g, bonus, pend4, ln)


def _merge_kernel(x_ref, oa_ref, ob_ref, ga_ref, gb_ref, wa_ref, wb_ref, wo_ref, g2_ref,
                  h_ref, xn_ref, acc_ref):
    j = pl.program_id(1)

    @pl.when(j == 0)
    def _():
        acc_ref[...] = x_ref[...]

    ya = jnp.dot(oa_ref[...].astype(BF16), wa_ref[...], preferred_element_type=F32)
    yb = jnp.dot(ob_ref[...].astype(BF16), wb_ref[...], preferred_element_type=F32)
    y = jax.nn.sigmoid(ga_ref[...]) * ya + jax.nn.sigmoid(gb_ref[...]) * yb
    acc_ref[...] += jnp.dot(y.astype(BF16), wo_ref[...], preferred_element_type=F32)

    @pl.when(j == pl.num_programs(1) - 1)
    def _():
        h = acc_ref[...]
        h_ref[...] = h
        ms = jnp.mean(h * h, axis=-1, keepdims=True)
        xn_ref[...] = h * lax.rsqrt(ms + RMS_EPS) * g2_ref[...]


def merge_out(x2d, oa, ob, p2d, w_proj_a, w_proj_b, w_out, norm2_g, *, tm=512):
    t, d = x2d.shape
    tn = WIDTH
    nj = d // tn
    g0 = COL_G_OFF // tn
    big = jax.ShapeDtypeStruct((t, d), F32)
    return pl.pallas_call(
        _merge_kernel,
        grid=(t // tm, nj),
        in_specs=[
            pl.BlockSpec((tm, d), lambda i, j: (i, 0)),
            pl.BlockSpec((tm, WIDTH), lambda i, j: (i, 0)),
            pl.BlockSpec((tm, WIDTH), lambda i, j: (i, 0)),
            pl.BlockSpec((tm, tn), lambda i, j: (i, g0 + j)),
            pl.BlockSpec((tm, tn), lambda i, j: (i, g0 + nj + j)),
            pl.BlockSpec((WIDTH, tn), lambda i, j: (0, j)),
            pl.BlockSpec((WIDTH, tn), lambda i, j: (0, j)),
            pl.BlockSpec((tn, d), lambda i, j: (j, 0)),
            pl.BlockSpec((1, d), lambda i, j: (0, 0)),
        ],
        out_specs=[pl.BlockSpec((tm, d), lambda i, j: (i, 0))] * 2,
        out_shape=[big, big],
        scratch_shapes=[pltpu.VMEM((tm, d), F32)],
        compiler_params=_cparams(("parallel", "arbitrary")),
        name="merge_out",
    )(x2d, oa, ob, p2d, p2d, w_proj_a.astype(BF16), w_proj_b.astype(BF16), w_out.astype(BF16),
      norm2_g.reshape(1, d))


PEER_HEADS = 8
PEER_NKEYS = 128
PEER_TOPK = 16
PEER_HALF = 128


def _topk_rows(s, payload, k):
    n = s.shape[0]
    rows = lax.broadcasted_iota(jnp.int32, s.shape, 0)
    vals, pays = [], []
    for _ in range(k):
        m = jnp.max(s, axis=0, keepdims=True)
        first = jnp.min(jnp.where(s == m, rows, n), axis=0, keepdims=True)
        hit = rows == first
        vals.append(m)
        pays.append(jnp.max(jnp.where(hit, payload, -1), axis=0, keepdims=True))
        s = jnp.where(hit, -jnp.inf, s)
    return jnp.concatenate(vals, axis=0), jnp.concatenate(pays, axis=0)


def _peer_route_kernel(xn_ref, wq_ref, sk_ref, idx_ref, gate_ref, *, prec):
    tt = xn_ref.shape[0]
    k = PEER_TOPK
    q = jnp.dot(xn_ref[...].astype(wq_ref.dtype), wq_ref[...], precision=prec, preferred_element_type=F32)
    key_iota = lax.broadcasted_iota(jnp.int32, (PEER_NKEYS, tt), 0)
    nt = (((1,), (1,)), ((), ()))
    idx_rows, gate_rows = [], []
    for h in range(PEER_HEADS):
        tops = []
        for p in range(2):
            c0 = (h * 2 + p) * PEER_HALF
            s = lax.dot_general(sk_ref[h, p].astype(wq_ref.dtype), q[:, c0:c0 + PEER_HALF].astype(wq_ref.dtype),
                                nt, precision=prec, preferred_element_type=F32)
            tops.append(_topk_rows(s, key_iota, k))
        (s0, i0), (s1, i1) = tops
        half = k // 2
        cs = [s0[0:1] + s1] + [s0[i:i + 1] + s1[0:half] for i in range(1, half)] + [s0[half:] + s1[0:1]]
        ci = [i0[0:1] * PEER_NKEYS + i1] + [i0[i:i + 1] * PEER_NKEYS + i1[0:half] for i in range(1, half)] \
            + [i0[half:] * PEER_NKEYS + i1[0:1]]
        best_s, best_i = _topk_rows(jnp.concatenate(cs, axis=0), jnp.concatenate(ci, axis=0), k)
        e = jnp.exp(best_s - best_s[0:1])
        gate_rows.append(e / jnp.sum(e, axis=0, keepdims=True))
        idx_rows.append(best_i)
    idx_ref[...] = jnp.concatenate(idx_rows, axis=0).T
    gate_ref[...] = jnp.concatenate(gate_rows, axis=0).T


def peer_route(xn2d, peer_wq, peer_subkeys, *, tt=256, prec=None, wdtype=BF16):
    t, d = xn2d.shape
    nq = peer_wq.shape[1]
    n_sel = PEER_HEADS * PEER_TOPK
    return pl.pallas_call(
        functools.partial(_peer_route_kernel, prec=prec),
        grid=(t // tt,),
        in_specs=[
            pl.BlockSpec((tt, d), lambda i: (i, 0)),
            pl.BlockSpec((d, nq), lambda i: (0, 0)),
            pl.BlockSpec((PEER_HEADS, 2, PEER_NKEYS, PEER_HALF), lambda i: (0, 0, 0, 0)),
        ],
        out_specs=[pl.BlockSpec((tt, n_sel), lambda i: (i, 0))] * 2,
        out_shape=[jax.ShapeDtypeStruct((t, n_sel), jnp.int32), jax.ShapeDtypeStruct((t, n_sel), F32)],
        compiler_params=_cparams(("parallel",)),
        name="peer_route",
    )(xn2d, peer_wq.astype(wdtype), peer_subkeys)


def _final_kernel(h_ref, y_ref, g_ref, o_ref):
    h = h_ref[...] + y_ref[...]
    ms = jnp.mean(h * h, axis=-1, keepdims=True)
    o_ref[...] = h * lax.rsqrt(ms + RMS_EPS) * g_ref[...]


def final_norm(h2d, y2d, g, *, tm=1024):
    t, d = h2d.shape
    spec = pl.BlockSpec((tm, d), lambda i: (i, 0))
    return pl.pallas_call(
        _final_kernel,
        grid=(t // tm,),
        in_specs=[spec, spec, pl.BlockSpec((1, d), lambda i: (0, 0))],
        out_specs=spec,
        out_shape=jax.ShapeDtypeStruct((t, d), F32),
        compiler_params=_cparams(("parallel",)),
        name="final_norm",
    )(h2d, y2d, g.reshape(1, d))


SC_CORES = 2
SC_SUBCORES = 16
SC_LANES = 16
SC_WORKERS = SC_CORES * SC_SUBCORES
PEER_SEL = PEER_HEADS * PEER_TOPK
PEER_ROWS = 64
PEER_GROUP = 32


def _pack_rows(w):
    half = w.shape[1] // 2
    bits = lax.bitcast_convert_type(w.astype(BF16), jnp.uint16).astype(jnp.uint32)
    return lax.bitcast_convert_type(bits[:, :half] | (bits[:, half:] << 16), jnp.int32)


def _unpack_words(w):
    lo = lax.bitcast_convert_type(lax.shift_left(w, jnp.int32(16)), F32)
    hi = lax.bitcast_convert_type(lax.bitwise_and(w, jnp.int32(-65536)), F32)
    return lo, hi


def _sc_mesh():
    from jax.experimental.pallas import tpu_sc as plsc
    return plsc.VectorSubcoreMesh(core_axis_name="c", subcore_axis_name="s",
                                  num_cores=SC_CORES, num_subcores=SC_SUBCORES)


def _sc_loop(n, body, carry):
    from jax.experimental.pallas import tpu_sc as plsc
    return plsc.parallel_loop(0, n, carry=carry)(body)


def _worker_base(tokens_per_worker):
    return (lax.axis_index("s") * SC_CORES + lax.axis_index("c")) * tokens_per_worker


def _gather_compute_loop(table_hbm, idx_v, rows_v, sem, stage_v, out_row, osem, grp, compute):
    def gather(j, b):
        return pltpu.make_async_copy(table_hbm.at[idx_v.at[j]], rows_v.at[b], sem.at[b])

    def put(i, slot):
        return pltpu.make_async_copy(stage_v.at[slot], out_row(i), osem.at[slot])

    gather(0, 0).start()

    @pl.loop(0, 2 * grp)
    def _(j):
        b = lax.bitwise_and(j, 1)
        i = lax.shift_right_logical(j, 1)
        slot = lax.bitwise_and(i, 1)

        @pl.when((b == 0) & (i >= 2))
        def _():
            put(i - 2, slot).wait()

        @pl.when(j + 1 < 2 * grp)
        def _():
            gather(j + 1, 1 - b).start()

        gather(j, b).wait()
        compute(i, b, b, slot)

        @pl.when(b == 1)
        def _():
            put(i, slot).start()

    put(grp - 2, 0).wait()
    put(grp - 1, 1).wait()


def peer_expert_dots(xn2d, idx2, u_packed):
    t, d = xn2d.shape
    half = d // 2
    n_chunks = half // SC_LANES
    tpw = t // SC_WORKERS
    grp = min(PEER_GROUP, tpw)
    rows_tog = 4
    n_acc = 2
    from jax.experimental.pallas import tpu_sc as plsc

    def body(x_hbm, idx_hbm, u_hbm, out_hbm, idx_v, x_v, rows_v, ps_v, sem, osem):
        base = _worker_base(tpw)

        def compute(i, h, b, slot):
            @pl.loop(0, PEER_ROWS // rows_tog)
            def _(rg):
                r0 = rg * rows_tog
                accs = [[None] * n_acc for _ in range(rows_tog)]
                for c in range(n_chunks):
                    xl = x_v[i, pl.ds(c * SC_LANES, SC_LANES)]
                    xh = x_v[i, pl.ds(half + c * SC_LANES, SC_LANES)]
                    for r in range(rows_tog):
                        lo, hi = _unpack_words(rows_v[b, r0 + r, pl.ds(c * SC_LANES, SC_LANES)])
                        term = lo * xl + hi * xh
                        k = c % n_acc
                        accs[r][k] = term if accs[r][k] is None else accs[r][k] + term
                for r in range(rows_tog):
                    at = pl.ds(pl.multiple_of((h * PEER_ROWS + r0 + r) * SC_LANES, SC_LANES), SC_LANES)
                    ps_v[slot, at] = accs[r][0] + accs[r][1]

        @pl.loop(0, tpw // grp)
        def _(g):
            t0 = base + g * grp
            pltpu.sync_copy(idx_hbm.at[pl.ds(2 * t0, 2 * grp)], idx_v)
            pltpu.sync_copy(x_hbm.at[pl.ds(t0, grp)], x_v)
            _gather_compute_loop(u_hbm, idx_v, rows_v, sem, ps_v, lambda i: out_hbm.at[t0 + i], osem, grp, compute)

    return pl.kernel(
        body,
        out_type=jax.ShapeDtypeStruct((t, PEER_SEL * SC_LANES), F32),
        mesh=_sc_mesh(),
        scratch_types=[
            pltpu.VMEM((2 * grp, PEER_ROWS), jnp.int32),
            pltpu.VMEM((grp, d), F32),
            pltpu.VMEM((2, PEER_ROWS, half), jnp.int32),
            pltpu.VMEM((2, PEER_SEL * SC_LANES), F32),
            pltpu.SemaphoreType.DMA((2,)),
            pltpu.SemaphoreType.DMA((2,)),
        ],
        compiler_params=pltpu.CompilerParams(needs_layout_passes=False),
        name="peer_expert_dots",
    )(xn2d, idx2, u_packed)


def peer_expert_mix(hgx, idx2, v_packed):
    t = hgx.shape[0]
    half = v_packed.shape[1]
    d = 2 * half
    tpw = t // SC_WORKERS
    grp = min(PEER_GROUP // 2, tpw)
    n_parts = 2
    cpp = half // SC_LANES // n_parts

    def body(hg_hbm, idx_hbm, v_hbm, out_hbm, idx_v, hg_v, rows_v, o_v2, sem, osem):
        base = _worker_base(tpw)

        def compute(i, h, b, slot):
            for part in range(n_parts):
                def rbody(r, accs):
                    s = hg_v[i, pl.ds(pl.multiple_of((h * PEER_ROWS + r) * SC_LANES, SC_LANES), SC_LANES)]
                    new = []
                    for c in range(cpp):
                        lo, hi = _unpack_words(rows_v[b, r, pl.ds((part * cpp + c) * SC_LANES, SC_LANES)])
                        new.append(accs[2 * c] + s * lo)
                        new.append(accs[2 * c + 1] + s * hi)
                    return tuple(new)

                accs = _sc_loop(PEER_ROWS, rbody, tuple(jnp.zeros((SC_LANES,), F32) for _ in range(2 * cpp)))
                def store(overwrite):
                    for c in range(cpp):
                        lo_at = pl.ds((part * cpp + c) * SC_LANES, SC_LANES)
                        hi_at = pl.ds(half + (part * cpp + c) * SC_LANES, SC_LANES)
                        if overwrite:
                            o_v2[slot, lo_at] = accs[2 * c]
                            o_v2[slot, hi_at] = accs[2 * c + 1]
                        else:
                            o_v2[slot, lo_at] = o_v2[slot, lo_at] + accs[2 * c]
                            o_v2[slot, hi_at] = o_v2[slot, hi_at] + accs[2 * c + 1]

                pl.when(h == 0)(functools.partial(store, True))
                pl.when(h != 0)(functools.partial(store, False))

        @pl.loop(0, tpw // grp)
        def _(g):
            t0 = base + g * grp
            pltpu.sync_copy(idx_hbm.at[pl.ds(2 * t0, 2 * grp)], idx_v)
            pltpu.sync_copy(hg_hbm.at[pl.ds(t0, grp)], hg_v)
            _gather_compute_loop(v_hbm, idx_v, rows_v, sem, o_v2, lambda i: out_hbm.at[t0 + i], osem, grp, compute)

    return pl.kernel(
        body,
        out_type=jax.ShapeDtypeStruct((t, d), F32),
        mesh=_sc_mesh(),
        scratch_types=[
            pltpu.VMEM((2 * grp, PEER_ROWS), jnp.int32),
            pltpu.VMEM((grp, PEER_SEL * SC_LANES), F32),
            pltpu.VMEM((2, PEER_ROWS, half), jnp.int32),
            pltpu.VMEM((2, d), F32),
            pltpu.SemaphoreType.DMA((2,)),
            pltpu.SemaphoreType.DMA((2,)),
        ],
        compiler_params=pltpu.CompilerParams(needs_layout_passes=False),
        name="peer_expert_mix",
    )(hgx, idx2, v_packed)


def _peer_act_kernel(ps_ref, gate_ref, sum_ref, o_ref):
    pre = jnp.dot(ps_ref[...], sum_ref[...], precision=HI, preferred_element_type=F32)
    hg = 0.5 * pre * (1.0 + lax.erf(pre * (1.0 / math.sqrt(2.0)))) * gate_ref[...]
    spread = (((1,), (1,)), ((), ()))
    o_ref[...] = lax.dot_general(hg, sum_ref[...], spread, precision=HI, preferred_element_type=F32)


def peer_act(ps, gates, *, tm=512):
    t, n = ps.shape
    lane_sum = (jnp.arange(n)[:, None] // SC_LANES == jnp.arange(PEER_SEL)[None, :]).astype(F32)
    return pl.pallas_call(
        _peer_act_kernel,
        grid=(t // tm,),
        in_specs=[
            pl.BlockSpec((tm, n), lambda i: (i, 0)),
            pl.BlockSpec((tm, PEER_SEL), lambda i: (i, 0)),
            pl.BlockSpec((n, PEER_SEL), lambda i: (0, 0)),
        ],
        out_specs=pl.BlockSpec((tm, n), lambda i: (i, 0)),
        out_shape=jax.ShapeDtypeStruct((t, n), F32),
        compiler_params=_cparams(("parallel",)),
        name="peer_act",
    )(ps, gates, lane_sum)


BATCH_GROUPS = 8


def kernel(x, norm1_g, w_in, rwkv_mu, w0, w_lora_up, a0, a_lora_up, g_lora_up, k_k, k_a, r_k, lnx_g, lnx_b,
           w_proj_a, w_proj_b, w_out, norm2_g, peer_wq, peer_subkeys, peer_u, peer_v, rel_bias, normf_g):
    bsz, seq, d = x.shape
    depth = norm1_g.shape[0]
    groups = BATCH_GROUPS if bsz % BATCH_GROUPS == 0 else 1
    gb = bsz // groups
    tg = gb * seq
    hs = [x[g * gb:(g + 1) * gb].reshape(tg, d) for g in range(groups)]
    for l in range(depth):
        w_pad = jnp.concatenate([
            w_in[l][:, :COL_A + COL_B_RAW],
            jnp.zeros((d, COL_B - COL_B_RAW), w_in.dtype),
            w_in[l][:, COL_A + COL_B_RAW:]], axis=1).astype(BF16)
        u_packed = _pack_rows(peer_u[l])
        v_packed = _pack_rows(peer_v[l])
        last = l == depth - 1

        def finish(pending, tie=None):
            g, h2d, ps, gates, idx2 = pending
            hgx = peer_act(ps, gates)
            if tie is not None:
                tie, hgx = lax.optimization_barrier((tie, hgx))
            y2d = peer_expert_mix(hgx, idx2, v_packed)
            hs[g] = final_norm(h2d, y2d, normf_g) if last else h2d + y2d
            return tie

        pending = None
        for g in range(groups):
            p2d = norm_proj(hs[g], norm1_g[l], w_pad)
            p3d = p2d.reshape(gb, seq, -1)
            oa = moba_attention(p3d, rel_bias)
            if pending is not None:
                oa = finish(pending, oa)
            prep = rwkv_prep(p3d, rwkv_mu[l], w0[l], w_lora_up[l], a0[l], a_lora_up[l], g_lora_up[l],
                             k_k[l], k_a[l], r_k[l])
            ob = rwkv_scan(*prep, lnx_g[l], lnx_b[l])
            h2d, xn2 = merge_out(hs[g], oa.reshape(tg, WIDTH), ob.reshape(tg, WIDTH), p2d,
                                 w_proj_a[l], w_proj_b[l], w_out[l], norm2_g[l])
            idx, gates = peer_route(xn2, peer_wq[l], peer_subkeys[l])
            idx2 = idx.reshape(-1, PEER_ROWS)
            pending = (g, h2d, peer_expert_dots(xn2, idx2, u_packed), gates, idx2)
        finish(pending)
    return jnp.concatenate(hs, axis=0).reshape(bsz, seq, d)
```

```python
import functools
import math

import jax
import jax.numpy as jnp
from jax import lax
from jax.experimental import pallas as pl
from jax.experimental.pallas import tpu as pltpu

F32 = jnp.float32
BF16 = jnp.bfloat16
HI = lax.Precision.HIGHEST

LANES = 128
HEAD_DIM = 64
HEADS = 8
PAIRS = HEADS // 2
WIDTH = HEADS * HEAD_DIM
MOBA_BLOCK = 256
MOBA_TOPK = 3
REL_BUCKETS = 32
REL_MAX_DIST = 128
DECAY_LORA = 64
AAA_LORA = 64
GATE_LORA = 160
GN_EPS = 64e-5
RMS_EPS = 1e-6
NEG = -1e30
RWKV_CHUNK = 64
COL_A = 3 * WIDTH
COL_B_RAW = 3 * WIDTH + DECAY_LORA + AAA_LORA + GATE_LORA
COL_B = 4 * WIDTH
COL_G_OFF = COL_A + COL_B
VMEM_LIMIT = 56 * 1024 * 1024


def _cparams(sem):
    return pltpu.CompilerParams(dimension_semantics=sem, vmem_limit_bytes=VMEM_LIMIT)


def _norm_proj_kernel(x_ref, g_ref, w_ref, o_ref, xn_ref):
    @pl.when(pl.program_id(1) == 0)
    def _():
        x = x_ref[...]
        ms = jnp.mean(x * x, axis=-1, keepdims=True)
        xn_ref[...] = (x * lax.rsqrt(ms + RMS_EPS) * g_ref[...]).astype(xn_ref.dtype)

    o_ref[...] = jnp.dot(xn_ref[...], w_ref[...], preferred_element_type=F32).astype(o_ref.dtype)


def norm_proj(x2d, g, w, *, tm=512, tn=512, out_dtype=F32):
    t, d = x2d.shape
    n = w.shape[1]
    return pl.pallas_call(
        _norm_proj_kernel,
        grid=(t // tm, n // tn),
        in_specs=[
            pl.BlockSpec((tm, d), lambda i, j: (i, 0)),
            pl.BlockSpec((1, d), lambda i, j: (0, 0)),
            pl.BlockSpec((d, tn), lambda i, j: (0, j)),
        ],
        out_specs=pl.BlockSpec((tm, tn), lambda i, j: (i, j)),
        out_shape=jax.ShapeDtypeStruct((t, n), out_dtype),
        scratch_shapes=[pltpu.VMEM((tm, d), w.dtype)],
        compiler_params=_cparams(("parallel", "arbitrary")),
        name="norm_proj",
    )(x2d, g.reshape(1, d), w)


def _rel_bucket(dist):
    n = jnp.maximum(dist, 0)
    max_exact = REL_BUCKETS // 2
    nf = jnp.maximum(n, 1).astype(F32)
    large = max_exact + (jnp.log(nf / max_exact) / math.log(REL_MAX_DIST / max_exact)
                         * (REL_BUCKETS - max_exact)).astype(jnp.int32)
    large = jnp.minimum(large, REL_BUCKETS - 1)
    return jnp.where(n < max_exact, n, large)


def _moba_kernel(q_ref, k_ref, v_ref, bown_ref, bprev_ref, bfar_ref, o_ref,
                 kb_ref, vb_ref, kbar_ref, *, n_blocks):
    qb = pl.program_id(2)
    blk = MOBA_BLOCK
    scale = 1.0 / math.sqrt(HEAD_DIM)

    @pl.when(qb == 0)
    def _():
        kbar_ref[...] = jnp.zeros_like(kbar_ref)
        for n in range(n_blocks):
            kblk = k_ref[0, n * blk:(n + 1) * blk, :]
            kbar_ref[n:n + 1, :] = jnp.mean(kblk, axis=0, keepdims=True)
        kb_ref[...] = k_ref[0].astype(BF16)
        vb_ref[...] = v_ref[0].astype(BF16)

    q2 = q_ref[0]
    lane = lax.broadcasted_iota(jnp.int32, (blk, LANES), 1)
    row = lax.broadcasted_iota(jnp.int32, (blk, blk), 0)
    col = lax.broadcasted_iota(jnp.int32, (blk, blk), 1)
    own0 = pl.multiple_of(qb * blk, blk)
    k_own = kb_ref[pl.ds(own0, blk), :]
    v_own = vb_ref[pl.ds(own0, blk), :]
    prev0 = pl.multiple_of(jnp.maximum(qb - 1, 0) * blk, blk)
    k_prev = kb_ref[pl.ds(prev0, blk), :]
    v_prev = vb_ref[pl.ds(prev0, blk), :]
    nt = (((1,), (1,)), ((), ()))

    outs = []
    for hh in range(2):
        hmask = (lane >= hh * HEAD_DIM) & (lane < (hh + 1) * HEAD_DIM)
        qh = jnp.where(hmask, q2, 0.0)
        gate = lax.dot_general(qh, kbar_ref[...], nt, precision=HI, preferred_element_type=F32)
        g = jnp.where(lane < qb, gate, -jnp.inf)
        sel = []
        for _ in range(MOBA_TOPK):
            m = jnp.max(g, axis=1, keepdims=True)
            idx = jnp.min(jnp.where(g == m, lane, LANES), axis=1, keepdims=True)
            idx = jnp.where(m > -jnp.inf, idx, LANES)
            sel.append(idx)
            g = jnp.where(lane == idx, -jnp.inf, g)

        def picked(n):
            return (sel[0] == n) | (sel[1] == n) | (sel[2] == n)

        qs = (qh * scale).astype(BF16)
        s = lax.dot_general(qs, k_own, nt, preferred_element_type=F32) + bown_ref[hh]
        s = jnp.where(row >= col, s, NEG)
        m_i = jnp.max(s, axis=1, keepdims=True)
        p = jnp.exp(s - m_i)
        l_i = jnp.sum(p, axis=1, keepdims=True)
        acc = jnp.dot(p.astype(BF16), v_own, preferred_element_type=F32)

        def update(carry, s, vblk):
            m_i, l_i, acc = carry
            m_new = jnp.maximum(m_i, jnp.max(s, axis=1, keepdims=True))
            alpha = jnp.exp(m_i - m_new)
            p = jnp.exp(s - m_new)
            l_new = alpha * l_i + jnp.sum(p, axis=1, keepdims=True)
            acc_new = alpha * acc + jnp.dot(p.astype(BF16), vblk, preferred_element_type=F32)
            return m_new, l_new, acc_new

        s = lax.dot_general(qs, k_prev, nt, preferred_element_type=F32) + bprev_ref[hh]
        s = jnp.where(picked(qb - 1), s, NEG)
        carry = update((m_i, l_i, acc), s, v_prev)

        bfar = bfar_ref[hh, 0:1, 0:1]

        def body(n, carry):
            n0 = pl.multiple_of(n * blk, blk)
            kblk = kb_ref[pl.ds(n0, blk), :]
            vblk = vb_ref[pl.ds(n0, blk), :]
            s = lax.dot_general(qs, kblk, nt, preferred_element_type=F32) + bfar
            s = jnp.where(picked(n), s, NEG)
            return update(carry, s, vblk)

        m_i, l_i, acc = lax.fori_loop(0, jnp.maximum(qb - 1, 0), body, carry)
        outs.append(acc / l_i)

    o_ref[0] = jnp.where(lane < HEAD_DIM, outs[0], outs[1])


def moba_attention(p3d, rel_bias):
    bsz, seq, _ = p3d.shape
    blk = MOBA_BLOCK
    n_blocks = seq // blk
    span = 2 * blk
    by_dist = rel_bias[:, _rel_bucket(jnp.arange(span))].astype(F32)
    shift = jnp.arange(span)

    def toeplitz(c):
        k = jnp.where(shift < blk, shift, shift - span)
        s = by_dist[:, jnp.clip(c - k, 0, span - 1)]
        tiled = jnp.tile(s, (1, blk))[:, :blk * (span - 1)]
        return tiled.reshape(HEADS, blk, span - 1)[:, :, :blk]

    bias_own = toeplitz(0)
    bias_prev = toeplitz(blk)
    bias_far = jnp.broadcast_to(rel_bias[:, REL_BUCKETS - 1].astype(F32)[:, None, None], (HEADS, 8, LANES))
    kern = functools.partial(_moba_kernel, n_blocks=n_blocks)
    return pl.pallas_call(
        kern,
        grid=(bsz, PAIRS, n_blocks),
        in_specs=[
            pl.BlockSpec((1, blk, LANES), lambda b, h, i: (b, i, h)),
            pl.BlockSpec((1, seq, LANES), lambda b, h, i: (b, 0, PAIRS + h)),
            pl.BlockSpec((1, seq, LANES), lambda b, h, i: (b, 0, 2 * PAIRS + h)),
            pl.BlockSpec((2, blk, blk), lambda b, h, i: (h, 0, 0)),
            pl.BlockSpec((2, blk, blk), lambda b, h, i: (h, 0, 0)),
            pl.BlockSpec((2, 8, LANES), lambda b, h, i: (h, 0, 0)),
        ],
        out_specs=pl.BlockSpec((1, blk, LANES), lambda b, h, i: (b, i, h)),
        out_shape=jax.ShapeDtypeStruct((bsz, seq, WIDTH), F32),
        scratch_shapes=[
            pltpu.VMEM((seq, LANES), BF16),
            pltpu.VMEM((seq, LANES), BF16),
            pltpu.VMEM((LANES, LANES), F32),
        ],
        compiler_params=_cparams(("parallel", "parallel", "arbitrary")),
        name="moba",
    )(p3d, p3d, p3d, bias_own, bias_prev, bias_far)


def _shifted(x, carry_row):
    rows = lax.broadcasted_iota(jnp.int32, x.shape, 0)
    return jnp.where(rows == 0, carry_row, pltpu.roll(x, 1, axis=0))


def _rwkv_prep_kernel(pr_ref, pk_ref, pv_ref, pl_ref, mu_ref, vec_ref, ww_ref, wa_ref, wg_ref,
                      bd_ref, tri_ref,
                      rt_ref, kt_ref, kd_ref, bd_out_ref, v_ref, g_ref, bonus_ref, pend_ref,
                      carry_ref, *, chunk):
    @pl.when(pl.program_id(1) == 0)
    def _():
        carry_ref[...] = jnp.zeros_like(carry_ref)

    def mix(ref, j):
        x = ref[0]
        mu = mu_ref[0:1, j * WIDTH:(j + 1) * WIDTH]
        prev = _shifted(x, carry_ref[0:1, j * WIDTH:(j + 1) * WIDTH])
        carry_ref[0:1, j * WIDTH:(j + 1) * WIDTH] = x[x.shape[0] - 1:, :]
        return x + mu * (prev - x)

    r = mix(pr_ref, 0)
    k = mix(pk_ref, 1)
    v = mix(pv_ref, 2)
    lo = mix(pl_ref, 3)
    w0, a0, k_k, k_a, r_k = (vec_ref[i:i + 1, :] for i in range(5))
    xwa = lo[:, 0:LANES]
    xg = lo[:, LANES:3 * LANES]
    lw = jnp.dot(jnp.tanh(xwa), ww_ref[...], precision=HI, preferred_element_type=F32)
    la = jnp.dot(xwa, wa_ref[...], precision=HI, preferred_element_type=F32)
    g = jnp.dot(jax.nn.sigmoid(xg), wg_ref[...], precision=HI, preferred_element_type=F32)
    z = -(w0 + lw)
    softplus = jnp.maximum(z, 0.0) + jnp.log(1.0 + jnp.exp(-jnp.abs(z)))
    logw = -jnp.exp(-softplus - 0.5)
    a = jax.nn.sigmoid(a0 + la)
    kk = k * k_k
    ss = jnp.dot(kk * kk, bd_ref[...], precision=HI, preferred_element_type=F32)
    kk = kk / jnp.maximum(jnp.sqrt(ss), 1e-12)
    k2 = k * (1.0 + (a - 1.0) * k_a)
    rk = jnp.dot(r * k2 * r_k, bd_ref[...], precision=HI, preferred_element_type=F32)
    cs = jnp.dot(tri_ref[...], logw, precision=HI, preferred_element_type=F32)
    e_pos = jnp.exp(cs)
    e_neg = jnp.exp(-cs)
    rt_ref[0] = r * e_pos
    kt_ref[0] = kk * jnp.exp(cs - logw)
    kd_ref[0] = k2 * e_neg
    bd_out_ref[0] = kk * a * e_neg
    v_ref[0] = v
    g_ref[0] = g
    bonus_ref[0] = rk * v
    ts = e_pos.shape[0]
    for c in range(ts // chunk):
        pend_ref[0, c:c + 1, :] = e_pos[(c + 1) * chunk - 1:(c + 1) * chunk, :]


def rwkv_prep(p3d, rwkv_mu, w0, w_lora_up, a0, a_lora_up, g_lora_up, k_k, k_a, r_k, *, ts=512):
    bsz, seq, _ = p3d.shape
    chunk = RWKV_CHUNK
    ts = min(ts, seq)
    mu = jnp.pad(rwkv_mu, (0, COL_B - COL_B_RAW)).reshape(1, COL_B)
    vec = jnp.stack([w0, a0, k_k, k_a, r_k.reshape(-1)] + [jnp.zeros_like(w0)] * 3).astype(F32)
    ww = jnp.zeros((LANES, WIDTH), F32).at[:DECAY_LORA].set(w_lora_up)
    wa = jnp.zeros((LANES, WIDTH), F32).at[DECAY_LORA:DECAY_LORA + AAA_LORA].set(a_lora_up)
    wg = jnp.zeros((2 * LANES, WIDTH), F32).at[:GATE_LORA].set(g_lora_up)
    hid = jnp.arange(WIDTH) // HEAD_DIM
    bd = (hid[:, None] == hid[None, :]).astype(F32)
    tix = jnp.arange(ts)
    tri = ((tix[:, None] // chunk == tix[None, :] // chunk) & (tix[None, :] <= tix[:, None])).astype(F32)
    c0 = COL_A // WIDTH
    big = jax.ShapeDtypeStruct((bsz, seq, WIDTH), F32)
    wspec = lambda shape: pl.BlockSpec(shape, lambda b, i: (0, 0))
    ospec = pl.BlockSpec((1, ts, WIDTH), lambda b, i: (b, i, 0))
    return pl.pallas_call(
        functools.partial(_rwkv_prep_kernel, chunk=chunk),
        grid=(bsz, seq // ts),
        in_specs=[
            pl.BlockSpec((1, ts, WIDTH), lambda b, i: (b, i, c0)),
            pl.BlockSpec((1, ts, WIDTH), lambda b, i: (b, i, c0 + 1)),
            pl.BlockSpec((1, ts, WIDTH), lambda b, i: (b, i, c0 + 2)),
            pl.BlockSpec((1, ts, WIDTH), lambda b, i: (b, i, c0 + 3)),
            wspec((1, COL_B)), wspec((8, WIDTH)), wspec((LANES, WIDTH)), wspec((LANES, WIDTH)),
            wspec((2 * LANES, WIDTH)), wspec((WIDTH, WIDTH)), wspec((ts, ts)),
        ],
        out_specs=[ospec] * 7 + [pl.BlockSpec((1, ts // chunk, WIDTH), lambda b, i: (b, i, 0))],
        out_shape=[big] * 7 + [jax.ShapeDtypeStruct((bsz, seq // chunk, WIDTH), F32)],
        scratch_shapes=[pltpu.VMEM((8, COL_B), F32)],
        compiler_params=_cparams(("parallel", "arbitrary")),
        name="rwkv_prep",
    )(p3d, p3d, p3d, p3d, mu, vec, ww, wa, wg, bd, tri)


def _rwkv_scan_kernel(rt_ref, kt_ref, kd_ref, bd_ref, v_ref, g_ref, bonus_ref, pend_ref, ln_ref, o_ref,
                      state_ref, *, chunk, prec):
    @pl.when(pl.program_id(1) == 0)
    def _():
        state_ref[...] = jnp.zeros_like(state_ref)

    c2 = 2 * chunk
    lane = lax.broadcasted_iota(jnp.int32, (chunk, LANES), 1)
    first = lane < HEAD_DIM
    row = lax.broadcasted_iota(jnp.int32, (c2, c2), 0)
    col = lax.broadcasted_iota(jnp.int32, (c2, c2), 1)
    eye = (row == col).astype(F32)
    hrow = lax.broadcasted_iota(jnp.int32, (LANES, LANES), 0) // HEAD_DIM
    hcol = lax.broadcasted_iota(jnp.int32, (LANES, LANES), 1) // HEAD_DIM
    head_mean = jnp.where(hrow == hcol, 1.0 / HEAD_DIM, 0.0).astype(F32)
    nt = (((1,), (1,)), ((), ()))
    tn = (((0,), (0,)), ((), ()))
    dot = functools.partial(jnp.dot, precision=prec, preferred_element_type=F32)
    dotg = functools.partial(lax.dot_general, precision=prec, preferred_element_type=F32)

    def stack(x):
        return jnp.concatenate([jnp.where(first, x, 0.0), jnp.where(first, 0.0, x)], axis=0)

    for hp in range(PAIRS):
        sl = slice(hp * LANES, (hp + 1) * LANES)
        rs, ks, kds, bs, vs = (stack(ref[0, :, sl]) for ref in (rt_ref, kt_ref, kd_ref, bd_ref, v_ref))
        pend = pend_ref[0, 0, 0:1, sl]
        big = dotg(jnp.concatenate([ks, rs], axis=0), jnp.concatenate([bs, kds], axis=0), nt)
        a_b = jnp.where(row > col, big[0:c2, 0:c2], 0.0)
        a_k = jnp.where(row > col, big[0:c2, c2:], 0.0)
        a_rb = jnp.where(row >= col, big[c2:, 0:c2], 0.0)
        a_rk = jnp.where(row >= col, big[c2:, c2:], 0.0)
        inv = eye - a_b
        pw = dot(a_b, a_b)
        n_sq = int(math.log2(chunk)) - 1
        for lvl in range(n_sq):
            inv = inv + dot(inv, pw)
            if lvl + 1 < n_sq:
                pw = dot(pw, pw)
        ht = state_ref[hp]
        rhs = dotg(ks, ht, nt) + dot(a_k, vs)
        us = dot(inv, rhs)
        os_ = dotg(rs, ht, nt) + dot(a_rk, vs) - dot(a_rb, us)
        o = os_[0:chunk] + os_[chunk:]
        state_ref[hp] = (ht + dotg(vs, kds, tn) - dotg(us, bs, tn)) * pend
        mu = jnp.dot(o, head_mean, precision=HI, preferred_element_type=F32)
        d = o - mu
        var = jnp.dot(d * d, head_mean, precision=HI, preferred_element_type=F32)
        on = d * lax.rsqrt(var + GN_EPS) * ln_ref[0:1, sl] + ln_ref[1:2, sl]
        o_ref[0, :, sl] = (on + bonus_ref[0, :, sl]) * g_ref[0, :, sl]


def rwkv_scan(rt, kt, kd, bd, v, g, bonus, pend, lnx_g, lnx_b, *, prec=None):
    bsz, seq, _ = rt.shape
    chunk = RWKV_CHUNK
    n_chunks = seq // chunk
    ln = jnp.stack([lnx_g, lnx_b] + [jnp.zeros_like(lnx_g)] * 6).astype(F32)
    pend4 = pend.reshape(bsz, n_chunks, 1, WIDTH)
    spec = pl.BlockSpec((1, chunk, WIDTH), lambda b, c: (b, c, 0))
    return pl.pallas_call(
        functools.partial(_rwkv_scan_kernel, chunk=chunk, prec=prec),
        grid=(bsz, n_chunks),
        in_specs=[spec] * 7 + [
            pl.BlockSpec((1, 1, 1, WIDTH), lambda b, c: (b, c, 0, 0)),
            pl.BlockSpec((8, WIDTH), lambda b, c: (0, 0)),
        ],
        out_specs=spec,
        out_shape=jax.ShapeDtypeStruct((bsz, seq, WIDTH), F32),
        scratch_shapes=[pltpu.VMEM((PAIRS, LANES, LANES), F32)],
        compiler_params=_cparams(("parallel", "arbitrary")),
        name="rwkv_scan",
    )(rt, kt, kd, bd, v, g, bonus, pend4, ln)


def _merge_kernel(x_ref, oa_ref, ob_ref, ga_ref, gb_ref, wa_ref, wb_ref, wo_ref, g2_ref,
                  h_ref, xn_ref, acc_ref):
    j = pl.program_id(1)

    @pl.when(j == 0)
    def _():
        acc_ref[...] = x_ref[...]

    ya = jnp.dot(oa_ref[...].astype(BF16), wa_ref[...], preferred_element_type=F32)
    yb = jnp.dot(ob_ref[...].astype(BF16), wb_ref[...], preferred_element_type=F32)
    y = jax.nn.sigmoid(ga_ref[...]) * ya + jax.nn.sigmoid(gb_ref[...]) * yb
    acc_ref[...] += jnp.dot(y.astype(BF16), wo_ref[...], preferred_element_type=F32)

    @pl.when(j == pl.num_programs(1) - 1)
    def _():
        h = acc_ref[...]
        h_ref[...] = h
        ms = jnp.mean(h * h, axis=-1, keepdims=True)
        xn_ref[...] = h * lax.rsqrt(ms + RMS_EPS) * g2_ref[...]


def merge_out(x2d, oa, ob, p2d, w_proj_a, w_proj_b, w_out, norm2_g, *, tm=512):
    t, d = x2d.shape
    tn = WIDTH
    nj = d // tn
    g0 = COL_G_OFF // tn
    big = jax.ShapeDtypeStruct((t, d), F32)
    return pl.pallas_call(
        _merge_kernel,
        grid=(t // tm, nj),
        in_specs=[
            pl.BlockSpec((tm, d), lambda i, j: (i, 0)),
            pl.BlockSpec((tm, WIDTH), lambda i, j: (i, 0)),
            pl.BlockSpec((tm, WIDTH), lambda i, j: (i, 0)),
            pl.BlockSpec((tm, tn), lambda i, j: (i, g0 + j)),
            pl.BlockSpec((tm, tn), lambda i, j: (i, g0 + nj + j)),
            pl.BlockSpec((WIDTH, tn), lambda i, j: (0, j)),
            pl.BlockSpec((WIDTH, tn), lambda i, j: (0, j)),
            pl.BlockSpec((tn, d), lambda i, j: (j, 0)),
            pl.BlockSpec((1, d), lambda i, j: (0, 0)),
        ],
        out_specs=[pl.BlockSpec((tm, d), lambda i, j: (i, 0))] * 2,
        out_shape=[big, big],
        scratch_shapes=[pltpu.VMEM((tm, d), F32)],
        compiler_params=_cparams(("parallel", "arbitrary")),
        name="merge_out",
    )(x2d, oa, ob, p2d, p2d, w_proj_a.astype(BF16), w_proj_b.astype(BF16), w_out.astype(BF16),
      norm2_g.reshape(1, d))


PEER_HEADS = 8
PEER_NKEYS = 128
PEER_TOPK = 16
PEER_HALF = 128


def _topk_rows(s, payload, k):
    n = s.shape[0]
    rows = lax.broadcasted_iota(jnp.int32, s.shape, 0)
    vals, pays = [], []
    for _ in range(k):
        m = jnp.max(s, axis=0, keepdims=True)
        first = jnp.min(jnp.where(s == m, rows, n), axis=0, keepdims=True)
        hit = rows == first
        vals.append(m)
        pays.append(jnp.max(jnp.where(hit, payload, -1), axis=0, keepdims=True))
        s = jnp.where(hit, -jnp.inf, s)
    return jnp.concatenate(vals, axis=0), jnp.concatenate(pays, axis=0)


def _peer_route_kernel(xn_ref, wq_ref, sk_ref, idx_ref, gate_ref, *, prec):
    tt = xn_ref.shape[0]
    k = PEER_TOPK
    q = jnp.dot(xn_ref[...].astype(wq_ref.dtype), wq_ref[...], precision=prec, preferred_element_type=F32)
    key_iota = lax.broadcasted_iota(jnp.int32, (PEER_NKEYS, tt), 0)
    nt = (((1,), (1,)), ((), ()))
    idx_rows, gate_rows = [], []
    for h in range(PEER_HEADS):
        tops = []
        for p in range(2):
            c0 = (h * 2 + p) * PEER_HALF
            s = lax.dot_general(sk_ref[h, p].astype(wq_ref.dtype), q[:, c0:c0 + PEER_HALF].astype(wq_ref.dtype),
                                nt, precision=prec, preferred_element_type=F32)
            tops.append(_topk_rows(s, key_iota, k))
        (s0, i0), (s1, i1) = tops
        half = k // 2
        cs = [s0[0:1] + s1] + [s0[i:i + 1] + s1[0:half] for i in range(1, half)] + [s0[half:] + s1[0:1]]
        ci = [i0[0:1] * PEER_NKEYS + i1] + [i0[i:i + 1] * PEER_NKEYS + i1[0:half] for i in range(1, half)] \
            + [i0[half:] * PEER_NKEYS + i1[0:1]]
        best_s, best_i = _topk_rows(jnp.concatenate(cs, axis=0), jnp.concatenate(ci, axis=0), k)
        e = jnp.exp(best_s - best_s[0:1])
        gate_rows.append(e / jnp.sum(e, axis=0, keepdims=True))
        idx_rows.append(best_i)
    idx_ref[...] = jnp.concatenate(idx_rows, axis=0).T
    gate_ref[...] = jnp.concatenate(gate_rows, axis=0).T


def peer_route(xn2d, peer_wq, peer_subkeys, *, tt=256, prec=None, wdtype=BF16):
    t, d = xn2d.shape
    nq = peer_wq.shape[1]
    n_sel = PEER_HEADS * PEER_TOPK
    return pl.pallas_call(
        functools.partial(_peer_route_kernel, prec=prec),
        grid=(t // tt,),
        in_specs=[
            pl.BlockSpec((tt, d), lambda i: (i, 0)),
            pl.BlockSpec((d, nq), lambda i: (0, 0)),
            pl.BlockSpec((PEER_HEADS, 2, PEER_NKEYS, PEER_HALF), lambda i: (0, 0, 0, 0)),
        ],
        out_specs=[pl.BlockSpec((tt, n_sel), lambda i: (i, 0))] * 2,
        out_shape=[jax.ShapeDtypeStruct((t, n_sel), jnp.int32), jax.ShapeDtypeStruct((t, n_sel), F32)],
        compiler_params=_cparams(("parallel",)),
        name="peer_route",
    )(xn2d, peer_wq.astype(wdtype), peer_subkeys)


def _final_kernel(h_ref, y_ref, g_ref, o_ref):
    h = h_ref[...] + y_ref[...]
    ms = jnp.mean(h * h, axis=-1, keepdims=True)
    o_ref[...] = h * lax.rsqrt(ms + RMS_EPS) * g_ref[...]


def final_norm(h2d, y2d, g, *, tm=1024):
    t, d = h2d.shape
    spec = pl.BlockSpec((tm, d), lambda i: (i, 0))
    return pl.pallas_call(
        _final_kernel,
        grid=(t // tm,),
        in_specs=[spec, spec, pl.BlockSpec((1, d), lambda i: (0, 0))],
        out_specs=spec,
        out_shape=jax.ShapeDtypeStruct((t, d), F32),
        compiler_params=_cparams(("parallel",)),
        name="final_norm",
    )(h2d, y2d, g.reshape(1, d))


SC_CORES = 2
SC_SUBCORES = 16
SC_LANES = 16
SC_WORKERS = SC_CORES * SC_SUBCORES
PEER_SEL = PEER_HEADS * PEER_TOPK
PEER_ROWS = 64
PEER_GROUP = 32


def _pack_rows(w):
    half = w.shape[1] // 2
    bits = lax.bitcast_convert_type(w.astype(BF16), jnp.uint16).astype(jnp.uint32)
    return lax.bitcast_convert_type(bits[:, :half] | (bits[:, half:] << 16), jnp.int32)


def _unpack_words(w):
    lo = lax.bitcast_convert_type(lax.shift_left(w, jnp.int32(16)), F32)
    hi = lax.bitcast_convert_type(lax.bitwise_and(w, jnp.int32(-65536)), F32)
    return lo, hi


def _sc_mesh():
    from jax.experimental.pallas import tpu_sc as plsc
    return plsc.VectorSubcoreMesh(core_axis_name="c", subcore_axis_name="s",
                                  num_cores=SC_CORES, num_subcores=SC_SUBCORES)


def _sc_loop(n, body, carry):
    from jax.experimental.pallas import tpu_sc as plsc
    return plsc.parallel_loop(0, n, carry=carry)(body)


def _worker_base(tokens_per_worker):
    return (lax.axis_index("s") * SC_CORES + lax.axis_index("c")) * tokens_per_worker


def _gather_compute_loop(table_hbm, idx_v, rows_v, sem, stage_v, out_row, osem, grp, compute):
    def gather(j, b):
        return pltpu.make_async_copy(table_hbm.at[idx_v.at[j]], rows_v.at[b], sem.at[b])

    def put(i, slot):
        return pltpu.make_async_copy(stage_v.at[slot], out_row(i), osem.at[slot])

    gather(0, 0).start()

    @pl.loop(0, 2 * grp)
    def _(j):
        b = lax.bitwise_and(j, 1)
        i = lax.shift_right_logical(j, 1)
        slot = lax.bitwise_and(i, 1)

        @pl.when((b == 0) & (i >= 2))
        def _():
            put(i - 2, slot).wait()

        @pl.when(j + 1 < 2 * grp)
        def _():
            gather(j + 1, 1 - b).start()

        gather(j, b).wait()
        compute(i, b, b, slot)

        @pl.when(b == 1)
        def _():
            put(i, slot).start()

    put(grp - 2, 0).wait()
    put(grp - 1, 1).wait()


def peer_expert_dots(xn2d, idx2, u_packed):
    t, d = xn2d.shape
    half = d // 2
    n_chunks = half // SC_LANES
    tpw = t // SC_WORKERS
    grp = min(PEER_GROUP, tpw)
    rows_tog = 4
    n_acc = 2
    from jax.experimental.pallas import tpu_sc as plsc

    def body(x_hbm, idx_hbm, u_hbm, out_hbm, idx_v, x_v, rows_v, ps_v, sem, osem):
        base = _worker_base(tpw)

        def compute(i, h, b, slot):
            @pl.loop(0, PEER_ROWS // rows_tog)
            def _(rg):
                r0 = rg * rows_tog
                accs = [[None] * n_acc for _ in range(rows_tog)]
                for c in range(n_chunks):
                    xl = x_v[i, pl.ds(c * SC_LANES, SC_LANES)]
                    xh = x_v[i, pl.ds(half + c * SC_LANES, SC_LANES)]
                    for r in range(rows_tog):
                        lo, hi = _unpack_words(rows_v[b, r0 + r, pl.ds(c * SC_LANES, SC_LANES)])
                        term = lo * xl + hi * xh
                        k = c % n_acc
                        accs[r][k] = term if accs[r][k] is None else accs[r][k] + term
                for r in range(rows_tog):
                    at = pl.ds(pl.multiple_of((h * PEER_ROWS + r0 + r) * SC_LANES, SC_LANES), SC_LANES)
                    ps_v[slot, at] = accs[r][0] + accs[r][1]

        @pl.loop(0, tpw // grp)
        def _(g):
            t0 = base + g * grp
            pltpu.sync_copy(idx_hbm.at[pl.ds(2 * t0, 2 * grp)], idx_v)
            pltpu.sync_copy(x_hbm.at[pl.ds(t0, grp)], x_v)
            _gather_compute_loop(u_hbm, idx_v, rows_v, sem, ps_v, lambda i: out_hbm.at[t0 + i], osem, grp, compute)

    return pl.kernel(
        body,
        out_type=jax.ShapeDtypeStruct((t, PEER_SEL * SC_LANES), F32),
        mesh=_sc_mesh(),
        scratch_types=[
            pltpu.VMEM((2 * grp, PEER_ROWS), jnp.int32),
            pltpu.VMEM((grp, d), F32),
            pltpu.VMEM((2, PEER_ROWS, half), jnp.int32),
            pltpu.VMEM((2, PEER_SEL * SC_LANES), F32),
            pltpu.SemaphoreType.DMA((2,)),
            pltpu.SemaphoreType.DMA((2,)),
        ],
        compiler_params=pltpu.CompilerParams(needs_layout_passes=False),
        name="peer_expert_dots",
    )(xn2d, idx2, u_packed)


def peer_expert_mix(hgx, idx2, v_packed):
    t = hgx.shape[0]
    half = v_packed.shape[1]
    d = 2 * half
    tpw = t // SC_WORKERS
    grp = min(PEER_GROUP // 2, tpw)
    n_parts = 2
    cpp = half // SC_LANES // n_parts

    def body(hg_hbm, idx_hbm, v_hbm, out_hbm, idx_v, hg_v, rows_v, o_v2, sem, osem):
        base = _worker_base(tpw)

        def compute(i, h, b, slot):
            for part in range(n_parts):
                def rbody(r, accs):
                    s = hg_v[i, pl.ds(pl.multiple_of((h * PEER_ROWS + r) * SC_LANES, SC_LANES), SC_LANES)]
                    new = []
                    for c in range(cpp):
                        lo, hi = _unpack_words(rows_v[b, r, pl.ds((part * cpp + c) * SC_LANES, SC_LANES)])
                        new.append(accs[2 * c] + s * lo)
                        new.append(accs[2 * c + 1] + s * hi)
                    return tuple(new)

                accs = _sc_loop(PEER_ROWS, rbody, tuple(jnp.zeros((SC_LANES,), F32) for _ in range(2 * cpp)))
                def store(overwrite):
                    for c in range(cpp):
                        lo_at = pl.ds((part * cpp + c) * SC_LANES, SC_LANES)
                        hi_at = pl.ds(half + (part * cpp + c) * SC_LANES, SC_LANES)
                        if overwrite:
                            o_v2[slot, lo_at] = accs[2 * c]
                            o_v2[slot, hi_at] = accs[2 * c + 1]
                        else:
                            o_v2[slot, lo_at] = o_v2[slot, lo_at] + accs[2 * c]
                            o_v2[slot, hi_at] = o_v2[slot, hi_at] + accs[2 * c + 1]

                pl.when(h == 0)(functools.partial(store, True))
                pl.when(h != 0)(functools.partial(store, False))

        @pl.loop(0, tpw // grp)
        def _(g):
            t0 = base + g * grp
            pltpu.sync_copy(idx_hbm.at[pl.ds(2 * t0, 2 * grp)], idx_v)
            pltpu.sync_copy(hg_hbm.at[pl.ds(t0, grp)], hg_v)
            _gather_compute_loop(v_hbm, idx_v, rows_v, sem, o_v2, lambda i: out_hbm.at[t0 + i], osem, grp, compute)

    return pl.kernel(
        body,
        out_type=jax.ShapeDtypeStruct((t, d), F32),
        mesh=_sc_mesh(),
        scratch_types=[
            pltpu.VMEM((2 * grp, PEER_ROWS), jnp.int32),
            pltpu.VMEM((grp, PEER_SEL * SC_LANES), F32),
            pltpu.VMEM((2, PEER_ROWS, half), jnp.int32),
            pltpu.VMEM((2, d), F32),
            pltpu.SemaphoreType.DMA((2,)),
            pltpu.SemaphoreType.DMA((2,)),
        ],
        compiler_params=pltpu.CompilerParams(needs_layout_passes=False),
        name="peer_expert_mix",
    )(hgx, idx2, v_packed)


def _peer_act_kernel(ps_ref, gate_ref, sum_ref, o_ref):
    pre = jnp.dot(ps_ref[...], sum_ref[...], precision=HI, preferred_element_type=F32)
    hg = 0.5 * pre * (1.0 + lax.erf(pre * (1.0 / math.sqrt(2.0)))) * gate_ref[...]
    spread = (((1,), (1,)), ((), ()))
    o_ref[...] = lax.dot_general(hg, sum_ref[...], spread, precision=HI, preferred_element_type=F32)


def peer_act(ps, gates, *, tm=512):
    t, n = ps.shape
    lane_sum = (jnp.arange(n)[:, None] // SC_LANES == jnp.arange(PEER_SEL)[None, :]).astype(F32)
    return pl.pallas_call(
        _peer_act_kernel,
        grid=(t // tm,),
        in_specs=[
            pl.BlockSpec((tm, n), lambda i: (i, 0)),
            pl.BlockSpec((tm, PEER_SEL), lambda i: (i, 0)),
            pl.BlockSpec((n, PEER_SEL), lambda i: (0, 0)),
        ],
        out_specs=pl.BlockSpec((tm, n), lambda i: (i, 0)),
        out_shape=jax.ShapeDtypeStruct((t, n), F32),
        compiler_params=_cparams(("parallel",)),
        name="peer_act",
    )(ps, gates, lane_sum)


BATCH_GROUPS = 8


def kernel(x, norm1_g, w_in, rwkv_mu, w0, w_lora_up, a0, a_lora_up, g_lora_up, k_k, k_a, r_k, lnx_g, lnx_b,
           w_proj_a, w_proj_b, w_out, norm2_g, peer_wq, peer_subkeys, peer_u, peer_v, rel_bias, normf_g):
    bsz, seq, d = x.shape
    depth = norm1_g.shape[0]
    groups = BATCH_GROUPS if bsz % BATCH_GROUPS == 0 else 1
    gb = bsz // groups
    tg = gb * seq
    hs = [x[g * gb:(g + 1) * gb].reshape(tg, d) for g in range(groups)]
    for l in range(depth):
        w_pad = jnp.concatenate([
            w_in[l][:, :COL_A + COL_B_RAW],
            jnp.zeros((d, COL_B - COL_B_RAW), w_in.dtype),
            w_in[l][:, COL_A + COL_B_RAW:]], axis=1).astype(BF16)
        u_packed = _pack_rows(peer_u[l])
        v_packed = _pack_rows(peer_v[l])
        last = l == depth - 1

        def mix(pending, tie=None):
            g, h2d, ps, gates, idx2 = pending
            hgx = peer_act(ps, gates)
            if tie is not None:
                tie, hgx = lax.optimization_barrier((tie, hgx))
            return tie, (g, h2d, peer_expert_mix(hgx, idx2, v_packed))

        def close(mixed, tie=None):
            g, h2d, y2d = mixed
            out = final_norm(h2d, y2d, normf_g) if last else h2d + y2d
            if tie is not None:
                tie, out = lax.optimization_barrier((tie, out))
            hs[g] = out
            return tie

        pending = None
        for g in range(groups):
            p2d = norm_proj(hs[g], norm1_g[l], w_pad)
            p3d = p2d.reshape(gb, seq, -1)
            oa = moba_attention(p3d, rel_bias)
            mixed = None
            if pending is not None:
                oa, mixed = mix(pending, oa)
            prep = rwkv_prep(p3d, rwkv_mu[l], w0[l], w_lora_up[l], a0[l], a_lora_up[l], g_lora_up[l],
                             k_k[l], k_a[l], r_k[l])
            ob = rwkv_scan(*prep, lnx_g[l], lnx_b[l])
            h2d, xn2 = merge_out(hs[g], oa.reshape(tg, WIDTH), ob.reshape(tg, WIDTH), p2d,
                                 w_proj_a[l], w_proj_b[l], w_out[l], norm2_g[l])
            idx, gates = peer_route(xn2, peer_wq[l], peer_subkeys[l])
            if mixed is not None:
                idx = close(mixed, idx)
            idx2 = idx.reshape(-1, PEER_ROWS)
            pending = (g, h2d, peer_expert_dots(xn2, idx2, u_packed), gates, idx2)
        close(mix(pending)[1])
    return jnp.concatenate(hs, axis=0).reshape(bsz, seq, d)
```

```python
import functools
import math

import jax
import jax.numpy as jnp
from jax import lax
from jax.experimental import pallas as pl
from jax.experimental.pallas import tpu as pltpu

F32 = jnp.float32
BF16 = jnp.bfloat16
HI = lax.Precision.HIGHEST

LANES = 128
HEAD_DIM = 64
HEADS = 8
PAIRS = HEADS // 2
WIDTH = HEADS * HEAD_DIM
MOBA_BLOCK = 256
MOBA_TOPK = 3
MOBA_LO = 64
REL_BUCKETS = 32
REL_MAX_DIST = 128
DECAY_LORA = 64
AAA_LORA = 64
GATE_LORA = 160
GN_EPS = 64e-5
RMS_EPS = 1e-6
NEG = -1e30
RWKV_CHUNK = 64
COL_A = 3 * WIDTH
COL_B_RAW = 3 * WIDTH + DECAY_LORA + AAA_LORA + GATE_LORA
COL_B = 4 * WIDTH
COL_G_OFF = COL_A + COL_B
VMEM_LIMIT = 56 * 1024 * 1024


def _cparams(sem):
    return pltpu.CompilerParams(dimension_semantics=sem, vmem_limit_bytes=VMEM_LIMIT)


def _norm_proj_kernel(x_ref, g_ref, w_ref, o_ref, xn_ref):
    @pl.when(pl.program_id(1) == 0)
    def _():
        x = x_ref[...]
        ms = jnp.mean(x * x, axis=-1, keepdims=True)
        xn_ref[...] = (x * lax.rsqrt(ms + RMS_EPS) * g_ref[...]).astype(xn_ref.dtype)

    o_ref[...] = jnp.dot(xn_ref[...], w_ref[...], preferred_element_type=F32).astype(o_ref.dtype)


def norm_proj(x2d, g, w, *, tm=512, tn=512, out_dtype=F32):
    t, d = x2d.shape
    n = w.shape[1]
    return pl.pallas_call(
        _norm_proj_kernel,
        grid=(t // tm, n // tn),
        in_specs=[
            pl.BlockSpec((tm, d), lambda i, j: (i, 0)),
            pl.BlockSpec((1, d), lambda i, j: (0, 0)),
            pl.BlockSpec((d, tn), lambda i, j: (0, j)),
        ],
        out_specs=pl.BlockSpec((tm, tn), lambda i, j: (i, j)),
        out_shape=jax.ShapeDtypeStruct((t, n), out_dtype),
        scratch_shapes=[pltpu.VMEM((tm, d), w.dtype)],
        compiler_params=_cparams(("parallel", "arbitrary")),
        name="norm_proj",
    )(x2d, g.reshape(1, d), w)


def _rel_bucket(dist):
    n = jnp.maximum(dist, 0)
    max_exact = REL_BUCKETS // 2
    nf = jnp.maximum(n, 1).astype(F32)
    large = max_exact + (jnp.log(nf / max_exact) / math.log(REL_MAX_DIST / max_exact)
                         * (REL_BUCKETS - max_exact)).astype(jnp.int32)
    large = jnp.minimum(large, REL_BUCKETS - 1)
    return jnp.where(n < max_exact, n, large)


def _moba_kernel(q_ref, k_ref, v_ref, bown_ref, bprev_ref, bfar_ref, o_ref,
                 kb_ref, vb_ref, kbar_ref, *, n_blocks):
    qb = pl.program_id(2)
    blk = MOBA_BLOCK
    scale = 1.0 / math.sqrt(HEAD_DIM)

    rows2 = 2 * blk
    nt = (((1,), (1,)), ((), ()))

    @pl.when(qb == 0)
    def _():
        kbar_ref[...] = jnp.zeros_like(kbar_ref)
        lane_b = lax.broadcasted_iota(jnp.int32, (blk, LANES), 1)
        for n in range(n_blocks):
            kblk = k_ref[0, n * blk:(n + 1) * blk, :]
            kbar_ref[n:n + 1, :] = jnp.mean(kblk, axis=0, keepdims=True)
            kb_ref[n * blk:(n + 1) * blk, 0:LANES] = kblk.astype(BF16)
            kb_ref[n * blk:(n + 1) * blk, LANES:] = ((lane_b == n) | (lane_b == MOBA_LO + n)).astype(BF16)
        vb_ref[...] = v_ref[0].astype(BF16)

    q2 = q_ref[0]
    first = lax.broadcasted_iota(jnp.int32, (blk, LANES), 1) < HEAD_DIM
    qh = jnp.concatenate([jnp.where(first, q2, 0.0), jnp.where(first, 0.0, q2)], axis=0)
    lane = lax.broadcasted_iota(jnp.int32, (rows2, LANES), 1)
    rowi = lax.broadcasted_iota(jnp.int32, (rows2, LANES), 0)
    gate = lax.dot_general(qh.astype(BF16), kbar_ref[...].astype(BF16), nt, preferred_element_type=F32)
    g = jnp.where(lane < qb, gate, -jnp.inf)
    chosen = lane < 0
    lane_f = lane.astype(F32)
    for _ in range(MOBA_TOPK):
        m = jnp.max(g, axis=1, keepdims=True)
        idx = jnp.min(jnp.where(g == m, lane_f, float(LANES)), axis=1, keepdims=True)
        hit = (lane_f == idx) & (m > -jnp.inf)
        chosen = chosen | hit
        g = jnp.where(hit, -jnp.inf, g)
    nfar = qb - 1
    bfar = jnp.where(rowi < blk, bfar_ref[0, 0:1, 0:1], bfar_ref[1, 0:1, 0:1])
    bhi = bfar.astype(BF16).astype(F32)
    madd = jnp.where(lane < nfar, jnp.where(chosen, bhi, NEG),
                     jnp.where(lane == nfar, jnp.where(chosen, 0.0, NEG),
                               jnp.where((lane >= MOBA_LO) & (lane - MOBA_LO < nfar), bfar - bhi, 0.0)))
    q_aug = jnp.concatenate([(qh * scale).astype(BF16), madd.astype(BF16)], axis=1)

    prev0 = pl.multiple_of(jnp.maximum(nfar, 0) * blk, blk)
    own0 = pl.multiple_of(qb * blk, blk)
    s_prev = (lax.dot_general(q_aug, kb_ref[pl.ds(prev0, blk), :], nt, preferred_element_type=F32)
              + bprev_ref[...].reshape(rows2, blk) + jnp.where(qb > 0, 0.0, NEG))
    s_own = (lax.dot_general(q_aug, kb_ref[pl.ds(own0, blk), :], nt, preferred_element_type=F32)
             + bown_ref[...].reshape(rows2, blk))
    r = lax.broadcasted_iota(jnp.int32, (rows2, blk), 0)
    c = lax.broadcasted_iota(jnp.int32, (rows2, blk), 1)
    s_own = jnp.where(lax.bitwise_and(r, blk - 1) >= c, s_own, NEG)
    s = jnp.concatenate([s_prev, s_own], axis=1)
    m_i = jnp.max(s, axis=1, keepdims=True)
    p = jnp.exp(s - m_i)
    l_i = jnp.sum(p, axis=1, keepdims=True)
    v0 = jnp.concatenate([vb_ref[pl.ds(prev0, blk), :], vb_ref[pl.ds(own0, blk), :]], axis=0)
    acc = jnp.dot(p.astype(BF16), v0, preferred_element_type=F32)

    def body(it, carry):
        m_i, l_i, acc = carry
        k0 = pl.multiple_of(it * rows2, rows2)
        s = lax.dot_general(q_aug, kb_ref[pl.ds(k0, rows2), :], nt, preferred_element_type=F32)
        tail = jnp.where(2 * it + 1 < nfar, 0.0, NEG)
        s = jnp.concatenate([s[:, :blk], s[:, blk:] + tail], axis=1)
        m_new = jnp.maximum(m_i, jnp.max(s, axis=1, keepdims=True))
        alpha = jnp.exp(m_i - m_new)
        p = jnp.exp(s - m_new)
        l_new = alpha * l_i + jnp.sum(p, axis=1, keepdims=True)
        acc_new = alpha * acc + jnp.dot(p.astype(BF16), vb_ref[pl.ds(k0, rows2), :], preferred_element_type=F32)
        return m_new, l_new, acc_new

    m_i, l_i, acc = lax.fori_loop(0, (jnp.maximum(nfar, 0) + 1) // 2, body, (m_i, l_i, acc))
    out = acc / l_i
    o_ref[0] = jnp.where(first, out[:blk], out[blk:])


def moba_attention(p3d, rel_bias):
    bsz, seq, _ = p3d.shape
    blk = MOBA_BLOCK
    n_blocks = seq // blk
    span = 2 * blk
    by_dist = rel_bias[:, _rel_bucket(jnp.arange(span))].astype(F32)
    shift = jnp.arange(span)

    def toeplitz(c):
        k = jnp.where(shift < blk, shift, shift - span)
        s = by_dist[:, jnp.clip(c - k, 0, span - 1)]
        tiled = jnp.tile(s, (1, blk))[:, :blk * (span - 1)]
        return tiled.reshape(HEADS, blk, span - 1)[:, :, :blk]

    bias_own = toeplitz(0)
    bias_prev = toeplitz(blk)
    bias_far = jnp.broadcast_to(rel_bias[:, REL_BUCKETS - 1].astype(F32)[:, None, None], (HEADS, 8, LANES))
    kern = functools.partial(_moba_kernel, n_blocks=n_blocks)
    return pl.pallas_call(
        kern,
        grid=(bsz, PAIRS, n_blocks),
        in_specs=[
            pl.BlockSpec((1, blk, LANES), lambda b, h, i: (b, i, h)),
            pl.BlockSpec((1, seq, LANES), lambda b, h, i: (b, 0, PAIRS + h)),
            pl.BlockSpec((1, seq, LANES), lambda b, h, i: (b, 0, 2 * PAIRS + h)),
            pl.BlockSpec((2, blk, blk), lambda b, h, i: (h, 0, 0)),
            pl.BlockSpec((2, blk, blk), lambda b, h, i: (h, 0, 0)),
            pl.BlockSpec((2, 8, LANES), lambda b, h, i: (h, 0, 0)),
        ],
        out_specs=pl.BlockSpec((1, blk, LANES), lambda b, h, i: (b, i, h)),
        out_shape=jax.ShapeDtypeStruct((bsz, seq, WIDTH), F32),
        scratch_shapes=[
            pltpu.VMEM((seq, 2 * LANES), BF16),
            pltpu.VMEM((seq, LANES), BF16),
            pltpu.VMEM((LANES, LANES), F32),
        ],
        compiler_params=_cparams(("parallel", "parallel", "arbitrary")),
        name="moba",
    )(p3d, p3d, p3d, bias_own, bias_prev, bias_far)


def _shifted(x, carry_row):
    rows = lax.broadcasted_iota(jnp.int32, x.shape, 0)
    return jnp.where(rows == 0, carry_row, pltpu.roll(x, 1, axis=0))


def _rwkv_prep_kernel(pr_ref, pk_ref, pv_ref, pl_ref, mu_ref, vec_ref, ww_ref, wa_ref, wg_ref,
                      bd_ref, tri_ref,
                      rt_ref, kt_ref, kd_ref, bd_out_ref, v_ref, g_ref, bonus_ref, pend_ref,
                      carry_ref, *, chunk):
    @pl.when(pl.program_id(1) == 0)
    def _():
        carry_ref[...] = jnp.zeros_like(carry_ref)

    def mix(ref, j):
        x = ref[0]
        mu = mu_ref[0:1, j * WIDTH:(j + 1) * WIDTH]
        prev = _shifted(x, carry_ref[0:1, j * WIDTH:(j + 1) * WIDTH])
        carry_ref[0:1, j * WIDTH:(j + 1) * WIDTH] = x[x.shape[0] - 1:, :]
        return x + mu * (prev - x)

    r = mix(pr_ref, 0)
    k = mix(pk_ref, 1)
    v = mix(pv_ref, 2)
    lo = mix(pl_ref, 3)
    w0, a0, k_k, k_a, r_k = (vec_ref[i:i + 1, :] for i in range(5))
    xwa = lo[:, 0:LANES]
    xg = lo[:, LANES:3 * LANES]
    lw = jnp.dot(jnp.tanh(xwa), ww_ref[...], precision=HI, preferred_element_type=F32)
    la = jnp.dot(xwa, wa_ref[...], precision=HI, preferred_element_type=F32)
    g = jnp.dot(jax.nn.sigmoid(xg), wg_ref[...], precision=HI, preferred_element_type=F32)
    z = -(w0 + lw)
    softplus = jnp.maximum(z, 0.0) + jnp.log(1.0 + jnp.exp(-jnp.abs(z)))
    logw = -jnp.exp(-softplus - 0.5)
    a = jax.nn.sigmoid(a0 + la)
    kk = k * k_k
    ss = jnp.dot(kk * kk, bd_ref[...], precision=HI, preferred_element_type=F32)
    kk = kk / jnp.maximum(jnp.sqrt(ss), 1e-12)
    k2 = k * (1.0 + (a - 1.0) * k_a)
    rk = jnp.dot(r * k2 * r_k, bd_ref[...], precision=HI, preferred_element_type=F32)
    cs = jnp.dot(tri_ref[...], logw, precision=HI, preferred_element_type=F32)
    e_pos = jnp.exp(cs)
    e_neg = jnp.exp(-cs)
    rt_ref[0] = r * e_pos
    kt_ref[0] = kk * jnp.exp(cs - logw)
    kd_ref[0] = k2 * e_neg
    bd_out_ref[0] = kk * a * e_neg
    v_ref[0] = v
    g_ref[0] = g
    bonus_ref[0] = rk * v
    ts = e_pos.shape[0]
    for c in range(ts // chunk):
        pend_ref[0, c:c + 1, :] = e_pos[(c + 1) * chunk - 1:(c + 1) * chunk, :]


def rwkv_prep(p3d, rwkv_mu, w0, w_lora_up, a0, a_lora_up, g_lora_up, k_k, k_a, r_k, *, ts=512):
    bsz, seq, _ = p3d.shape
    chunk = RWKV_CHUNK
    ts = min(ts, seq)
    mu = jnp.pad(rwkv_mu, (0, COL_B - COL_B_RAW)).reshape(1, COL_B)
    vec = jnp.stack([w0, a0, k_k, k_a, r_k.reshape(-1)] + [jnp.zeros_like(w0)] * 3).astype(F32)
    ww = jnp.zeros((LANES, WIDTH), F32).at[:DECAY_LORA].set(w_lora_up)
    wa = jnp.zeros((LANES, WIDTH), F32).at[DECAY_LORA:DECAY_LORA + AAA_LORA].set(a_lora_up)
    wg = jnp.zeros((2 * LANES, WIDTH), F32).at[:GATE_LORA].set(g_lora_up)
    hid = jnp.arange(WIDTH) // HEAD_DIM
    bd = (hid[:, None] == hid[None, :]).astype(F32)
    tix = jnp.arange(ts)
    tri = ((tix[:, None] // chunk == tix[None, :] // chunk) & (tix[None, :] <= tix[:, None])).astype(F32)
    c0 = COL_A // WIDTH
    big = jax.ShapeDtypeStruct((bsz, seq, WIDTH), F32)
    wspec = lambda shape: pl.BlockSpec(shape, lambda b, i: (0, 0))
    ospec = pl.BlockSpec((1, ts, WIDTH), lambda b, i: (b, i, 0))
    return pl.pallas_call(
        functools.partial(_rwkv_prep_kernel, chunk=chunk),
        grid=(bsz, seq // ts),
        in_specs=[
            pl.BlockSpec((1, ts, WIDTH), lambda b, i: (b, i, c0)),
            pl.BlockSpec((1, ts, WIDTH), lambda b, i: (b, i, c0 + 1)),
            pl.BlockSpec((1, ts, WIDTH), lambda b, i: (b, i, c0 + 2)),
            pl.BlockSpec((1, ts, WIDTH), lambda b, i: (b, i, c0 + 3)),
            wspec((1, COL_B)), wspec((8, WIDTH)), wspec((LANES, WIDTH)), wspec((LANES, WIDTH)),
            wspec((2 * LANES, WIDTH)), wspec((WIDTH, WIDTH)), wspec((ts, ts)),
        ],
        out_specs=[ospec] * 7 + [pl.BlockSpec((1, ts // chunk, WIDTH), lambda b, i: (b, i, 0))],
        out_shape=[big] * 7 + [jax.ShapeDtypeStruct((bsz, seq // chunk, WIDTH), F32)],
        scratch_shapes=[pltpu.VMEM((8, COL_B), F32)],
        compiler_params=_cparams(("parallel", "arbitrary")),
        name="rwkv_prep",
    )(p3d, p3d, p3d, p3d, mu, vec, ww, wa, wg, bd, tri)


def _rwkv_scan_kernel(rt_ref, kt_ref, kd_ref, bd_ref, v_ref, g_ref, bonus_ref, pend_ref, ln_ref, o_ref,
                      state_ref, *, chunk, prec):
    @pl.when(pl.program_id(1) == 0)
    def _():
        state_ref[...] = jnp.zeros_like(state_ref)

    c2 = 2 * chunk
    lane = lax.broadcasted_iota(jnp.int32, (chunk, LANES), 1)
    first = lane < HEAD_DIM
    row = lax.broadcasted_iota(jnp.int32, (c2, c2), 0)
    col = lax.broadcasted_iota(jnp.int32, (c2, c2), 1)
    eye = (row == col).astype(F32)
    hrow = lax.broadcasted_iota(jnp.int32, (LANES, LANES), 0) // HEAD_DIM
    hcol = lax.broadcasted_iota(jnp.int32, (LANES, LANES), 1) // HEAD_DIM
    head_mean = jnp.where(hrow == hcol, 1.0 / HEAD_DIM, 0.0).astype(F32)
    nt = (((1,), (1,)), ((), ()))
    tn = (((0,), (0,)), ((), ()))
    dot = functools.partial(jnp.dot, precision=prec, preferred_element_type=F32)
    dotg = functools.partial(lax.dot_general, precision=prec, preferred_element_type=F32)

    def stack(x):
        return jnp.concatenate([jnp.where(first, x, 0.0), jnp.where(first, 0.0, x)], axis=0)

    pairs = range(PAIRS)
    sls = [slice(hp * LANES, (hp + 1) * LANES) for hp in pairs]
    rs, ks, kds, bs, vs = ([stack(ref[0, :, sl]) for sl in sls] for ref in (rt_ref, kt_ref, kd_ref, bd_ref, v_ref))
    hts = [state_ref[hp] for hp in pairs]
    big = [dotg(jnp.concatenate([ks[hp], rs[hp]], axis=0), jnp.concatenate([bs[hp], kds[hp]], axis=0), nt)
           for hp in pairs]
    a_b = [jnp.where(row > col, big[hp][0:c2, 0:c2], 0.0) for hp in pairs]
    a_k = [jnp.where(row > col, big[hp][0:c2, c2:], 0.0) for hp in pairs]
    a_rb = [jnp.where(row >= col, big[hp][c2:, 0:c2], 0.0) for hp in pairs]
    a_rk = [jnp.where(row >= col, big[hp][c2:, c2:], 0.0) for hp in pairs]
    kh = [dotg(jnp.concatenate([ks[hp], rs[hp]], axis=0), hts[hp], nt) for hp in pairs]
    av = [dot(jnp.concatenate([a_k[hp], a_rk[hp]], axis=0), vs[hp]) for hp in pairs]
    vk = [dotg(vs[hp], kds[hp], tn) for hp in pairs]
    inv = [eye - a_b[hp] for hp in pairs]
    pw = [dot(a_b[hp], a_b[hp]) for hp in pairs]
    n_sq = int(math.log2(chunk)) - 1
    for lvl in range(n_sq):
        if lvl + 1 < n_sq:
            both = [dot(jnp.concatenate([inv[hp], pw[hp]], axis=0), pw[hp]) for hp in pairs]
            inv = [inv[hp] + both[hp][0:c2] for hp in pairs]
            pw = [both[hp][c2:] for hp in pairs]
        else:
            inv = [inv[hp] + dot(inv[hp], pw[hp]) for hp in pairs]
    us = [dot(inv[hp], kh[hp][0:c2] + av[hp][0:c2]) for hp in pairs]
    ub = [dotg(us[hp], bs[hp], tn) for hp in pairs]
    au = [dot(a_rb[hp], us[hp]) for hp in pairs]
    for hp in pairs:
        sl = sls[hp]
        pend = pend_ref[0, 0, 0:1, sl]
        state_ref[hp] = (hts[hp] + vk[hp] - ub[hp]) * pend
        os_ = kh[hp][c2:] + av[hp][c2:] - au[hp]
        o = os_[0:chunk] + os_[chunk:]
        mu = jnp.dot(o, head_mean, precision=HI, preferred_element_type=F32)
        d = o - mu
        var = jnp.dot(d * d, head_mean, precision=HI, preferred_element_type=F32)
        on = d * lax.rsqrt(var + GN_EPS) * ln_ref[0:1, sl] + ln_ref[1:2, sl]
        o_ref[0, :, sl] = (on + bonus_ref[0, :, sl]) * g_ref[0, :, sl]


def rwkv_scan(rt, kt, kd, bd, v, g, bonus, pend, lnx_g, lnx_b, *, prec=None):
    bsz, seq, _ = rt.shape
    chunk = RWKV_CHUNK
    n_chunks = seq // chunk
    ln = jnp.stack([lnx_g, lnx_b] + [jnp.zeros_like(lnx_g)] * 6).astype(F32)
    pend4 = pend.reshape(bsz, n_chunks, 1, WIDTH)
    spec = pl.BlockSpec((1, chunk, WIDTH), lambda b, c: (b, c, 0))
    return pl.pallas_call(
        functools.partial(_rwkv_scan_kernel, chunk=chunk, prec=prec),
        grid=(bsz, n_chunks),
        in_specs=[spec] * 7 + [
            pl.BlockSpec((1, 1, 1, WIDTH), lambda b, c: (b, c, 0, 0)),
            pl.BlockSpec((8, WIDTH), lambda b, c: (0, 0)),
        ],
        out_specs=spec,
        out_shape=jax.ShapeDtypeStruct((bsz, seq, WIDTH), F32),
        scratch_shapes=[pltpu.VMEM((PAIRS, LANES, LANES), F32)],
        compiler_params=_cparams(("parallel", "arbitrary")),
        name="rwkv_scan",
    )(rt, kt, kd, bd, v, g, bonus, pend4, ln)


def _merge_kernel(x_ref, oa_ref, ob_ref, ga_ref, gb_ref, wa_ref, wb_ref, wo_ref, g2_ref,
                  h_ref, xn_ref, acc_ref):
    j = pl.program_id(1)

    @pl.when(j == 0)
    def _():
        acc_ref[...] = x_ref[...]

    ya = jnp.dot(oa_ref[...].astype(BF16), wa_ref[...], preferred_element_type=F32)
    yb = jnp.dot(ob_ref[...].astype(BF16), wb_ref[...], preferred_element_type=F32)
    y = jax.nn.sigmoid(ga_ref[...]) * ya + jax.nn.sigmoid(gb_ref[...]) * yb
    acc_ref[...] += jnp.dot(y.astype(BF16), wo_ref[...], preferred_element_type=F32)

    @pl.when(j == pl.num_programs(1) - 1)
    def _():
        h = acc_ref[...]
        h_ref[...] = h
        ms = jnp.mean(h * h, axis=-1, keepdims=True)
        xn_ref[...] = h * lax.rsqrt(ms + RMS_EPS) * g2_ref[...]


def merge_out(x2d, oa, ob, p2d, w_proj_a, w_proj_b, w_out, norm2_g, *, tm=512):
    t, d = x2d.shape
    tn = WIDTH
    nj = d // tn
    g0 = COL_G_OFF // tn
    big = jax.ShapeDtypeStruct((t, d), F32)
    return pl.pallas_call(
        _merge_kernel,
        grid=(t // tm, nj),
        in_specs=[
            pl.BlockSpec((tm, d), lambda i, j: (i, 0)),
            pl.BlockSpec((tm, WIDTH), lambda i, j: (i, 0)),
            pl.BlockSpec((tm, WIDTH), lambda i, j: (i, 0)),
            pl.BlockSpec((tm, tn), lambda i, j: (i, g0 + j)),
            pl.BlockSpec((tm, tn), lambda i, j: (i, g0 + nj + j)),
            pl.BlockSpec((WIDTH, tn), lambda i, j: (0, j)),
            pl.BlockSpec((WIDTH, tn), lambda i, j: (0, j)),
            pl.BlockSpec((tn, d), lambda i, j: (j, 0)),
            pl.BlockSpec((1, d), lambda i, j: (0, 0)),
        ],
        out_specs=[pl.BlockSpec((tm, d), lambda i, j: (i, 0))] * 2,
        out_shape=[big, big],
        scratch_shapes=[pltpu.VMEM((tm, d), F32)],
        compiler_params=_cparams(("parallel", "arbitrary")),
        name="merge_out",
    )(x2d, oa, ob, p2d, p2d, w_proj_a.astype(BF16), w_proj_b.astype(BF16), w_out.astype(BF16),
      norm2_g.reshape(1, d))


PEER_HEADS = 8
PEER_NKEYS = 128
PEER_TOPK = 16
PEER_HALF = 128


def _topk_rows(s, k):
    n = s.shape[0]
    rows = lax.broadcasted_iota(jnp.int32, s.shape, 0).astype(F32)
    vals, ids = [], []
    for _ in range(k):
        m = jnp.max(s, axis=0, keepdims=True)
        first = jnp.min(jnp.where(s == m, rows, float(n)), axis=0, keepdims=True)
        vals.append(m)
        ids.append(first)
        s = jnp.where(rows == first, -jnp.inf, s)
    return jnp.concatenate(vals, axis=0), jnp.concatenate(ids, axis=0)


def _take_rows(table, ids):
    rows = lax.broadcasted_iota(jnp.int32, table.shape, 0).astype(F32)
    return jnp.sum(jnp.where(rows == ids, table, 0.0), axis=0, keepdims=True)


def _peer_route_kernel(xn_ref, wq_ref, sk_ref, idx_ref, gate_ref, *, prec):
    tt = xn_ref.shape[0]
    k = PEER_TOPK
    q = jnp.dot(xn_ref[...].astype(wq_ref.dtype), wq_ref[...], precision=prec, preferred_element_type=F32)
    nt = (((1,), (1,)), ((), ()))
    idx_rows, gate_rows = [], []
    half = k // 2
    for h in range(PEER_HEADS):
        tops = []
        for p in range(2):
            c0 = (h * 2 + p) * PEER_HALF
            s = lax.dot_general(sk_ref[h, p].astype(wq_ref.dtype), q[:, c0:c0 + PEER_HALF].astype(wq_ref.dtype),
                                nt, precision=prec, preferred_element_type=F32)
            tops.append(_topk_rows(s, k))
        (s0, i0), (s1, i1) = tops
        cs = [s0[0:1] + s1] + [s0[i:i + 1] + s1[0:half] for i in range(1, half)] + [s0[half:] + s1[0:1]]
        best_s, pos = _topk_rows(jnp.concatenate(cs, axis=0), k)
        mid = jnp.floor((pos - k) * (1.0 / half))
        end_mid = float(k + (half - 1) * half)
        i_rank = jnp.where(pos < k, 0.0, jnp.where(pos < end_mid, 1.0 + mid, pos - (end_mid - half)))
        j_rank = jnp.where(pos < k, pos, jnp.where(pos < end_mid, (pos - k) - half * mid, 0.0))
        ids = [_take_rows(i0, i_rank[n:n + 1]) * PEER_NKEYS + _take_rows(i1, j_rank[n:n + 1]) for n in range(k)]
        e = jnp.exp(best_s - best_s[0:1])
        gate_rows.append(e / jnp.sum(e, axis=0, keepdims=True))
        idx_rows.append(jnp.concatenate(ids, axis=0).astype(jnp.int32))
    idx_ref[...] = jnp.concatenate(idx_rows, axis=0).T
    gate_ref[...] = jnp.concatenate(gate_rows, axis=0).T


def peer_route(xn2d, peer_wq, peer_subkeys, *, tt=256, prec=None, wdtype=BF16):
    t, d = xn2d.shape
    nq = peer_wq.shape[1]
    n_sel = PEER_HEADS * PEER_TOPK
    return pl.pallas_call(
        functools.partial(_peer_route_kernel, prec=prec),
        grid=(t // tt,),
        in_specs=[
            pl.BlockSpec((tt, d), lambda i: (i, 0)),
            pl.BlockSpec((d, nq), lambda i: (0, 0)),
            pl.BlockSpec((PEER_HEADS, 2, PEER_NKEYS, PEER_HALF), lambda i: (0, 0, 0, 0)),
        ],
        out_specs=[pl.BlockSpec((tt, n_sel), lambda i: (i, 0))] * 2,
        out_shape=[jax.ShapeDtypeStruct((t, n_sel), jnp.int32), jax.ShapeDtypeStruct((t, n_sel), F32)],
        compiler_params=_cparams(("parallel",)),
        name="peer_route",
    )(xn2d, peer_wq.astype(wdtype), peer_subkeys)


def _final_kernel(h_ref, y_ref, g_ref, o_ref):
    h = h_ref[...] + y_ref[...]
    ms = jnp.mean(h * h, axis=-1, keepdims=True)
    o_ref[...] = h * lax.rsqrt(ms + RMS_EPS) * g_ref[...]


def final_norm(h2d, y2d, g, *, tm=1024):
    t, d = h2d.shape
    spec = pl.BlockSpec((tm, d), lambda i: (i, 0))
    return pl.pallas_call(
        _final_kernel,
        grid=(t // tm,),
        in_specs=[spec, spec, pl.BlockSpec((1, d), lambda i: (0, 0))],
        out_specs=spec,
        out_shape=jax.ShapeDtypeStruct((t, d), F32),
        compiler_params=_cparams(("parallel",)),
        name="final_norm",
    )(h2d, y2d, g.reshape(1, d))


SC_CORES = 2
SC_SUBCORES = 16
SC_LANES = 16
SC_WORKERS = SC_CORES * SC_SUBCORES
PEER_SEL = PEER_HEADS * PEER_TOPK
PEER_ROWS = 64
PEER_GROUP = 32


def _pack_rows(w):
    half = w.shape[1] // 2
    bits = lax.bitcast_convert_type(w.astype(BF16), jnp.uint16).astype(jnp.uint32)
    return lax.bitcast_convert_type(bits[:, :half] | (bits[:, half:] << 16), jnp.int32)


def _unpack_words(w):
    lo = lax.bitcast_convert_type(lax.shift_left(w, jnp.int32(16)), F32)
    hi = lax.bitcast_convert_type(lax.bitwise_and(w, jnp.int32(-65536)), F32)
    return lo, hi


def _sc_mesh():
    from jax.experimental.pallas import tpu_sc as plsc
    return plsc.VectorSubcoreMesh(core_axis_name="c", subcore_axis_name="s",
                                  num_cores=SC_CORES, num_subcores=SC_SUBCORES)


def _sc_loop(n, body, carry):
    from jax.experimental.pallas import tpu_sc as plsc
    return plsc.parallel_loop(0, n, carry=carry)(body)


def _worker_base(tokens_per_worker):
    return (lax.axis_index("s") * SC_CORES + lax.axis_index("c")) * tokens_per_worker


def _gather_compute_loop(table_hbm, idx_v, rows_v, sem, stage_v, out_row, osem, grp, compute):
    def gather(j, b):
        return pltpu.make_async_copy(table_hbm.at[idx_v.at[j]], rows_v.at[b], sem.at[b])

    def put(i, slot):
        return pltpu.make_async_copy(stage_v.at[slot], out_row(i), osem.at[slot])

    gather(0, 0).start()

    @pl.loop(0, 2 * grp)
    def _(j):
        b = lax.bitwise_and(j, 1)
        i = lax.shift_right_logical(j, 1)
        slot = lax.bitwise_and(i, 1)

        @pl.when((b == 0) & (i >= 2))
        def _():
            put(i - 2, slot).wait()

        @pl.when(j + 1 < 2 * grp)
        def _():
            gather(j + 1, 1 - b).start()

        gather(j, b).wait()
        compute(i, b, b, slot)

        @pl.when(b == 1)
        def _():
            put(i, slot).start()

    put(grp - 2, 0).wait()
    put(grp - 1, 1).wait()


def peer_expert_dots(xn2d, idx2, u_packed):
    t, d = xn2d.shape
    half = d // 2
    n_chunks = half // SC_LANES
    tpw = t // SC_WORKERS
    grp = min(PEER_GROUP, tpw)
    rows_tog = 4
    n_acc = 2
    from jax.experimental.pallas import tpu_sc as plsc

    def body(x_hbm, idx_hbm, u_hbm, out_hbm, idx_v, x_v, rows_v, ps_v, sem, osem):
        base = _worker_base(tpw)

        def compute(i, h, b, slot):
            @pl.loop(0, PEER_ROWS // rows_tog)
            def _(rg):
                r0 = rg * rows_tog
                accs = [[None] * n_acc for _ in range(rows_tog)]
                for c in range(n_chunks):
                    xl = x_v[i, pl.ds(c * SC_LANES, SC_LANES)]
                    xh = x_v[i, pl.ds(half + c * SC_LANES, SC_LANES)]
                    for r in range(rows_tog):
                        lo, hi = _unpack_words(rows_v[b, r0 + r, pl.ds(c * SC_LANES, SC_LANES)])
                        term = lo * xl + hi * xh
                        k = c % n_acc
                        accs[r][k] = term if accs[r][k] is None else accs[r][k] + term
                for r in range(rows_tog):
                    at = pl.ds(pl.multiple_of((h * PEER_ROWS + r0 + r) * SC_LANES, SC_LANES), SC_LANES)
                    ps_v[slot, at] = accs[r][0] + accs[r][1]

        @pl.loop(0, tpw // grp)
        def _(g):
            t0 = base + g * grp
            pltpu.sync_copy(idx_hbm.at[pl.ds(2 * t0, 2 * grp)], idx_v)
            pltpu.sync_copy(x_hbm.at[pl.ds(t0, grp)], x_v)
            _gather_compute_loop(u_hbm, idx_v, rows_v, sem, ps_v, lambda i: out_hbm.at[t0 + i], osem, grp, compute)

    return pl.kernel(
        body,
        out_type=jax.ShapeDtypeStruct((t, PEER_SEL * SC_LANES), F32),
        mesh=_sc_mesh(),
        scratch_types=[
            pltpu.VMEM((2 * grp, PEER_ROWS), jnp.int32),
            pltpu.VMEM((grp, d), F32),
            pltpu.VMEM((2, PEER_ROWS, half), jnp.int32),
            pltpu.VMEM((2, PEER_SEL * SC_LANES), F32),
            pltpu.SemaphoreType.DMA((2,)),
            pltpu.SemaphoreType.DMA((2,)),
        ],
        compiler_params=pltpu.CompilerParams(needs_layout_passes=False),
        name="peer_expert_dots",
    )(xn2d, idx2, u_packed)


def peer_expert_mix(hgx, idx2, v_packed):
    t = hgx.shape[0]
    half = v_packed.shape[1]
    d = 2 * half
    tpw = t // SC_WORKERS
    grp = min(PEER_GROUP // 2, tpw)
    n_parts = 2
    cpp = half // SC_LANES // n_parts

    def body(hg_hbm, idx_hbm, v_hbm, out_hbm, idx_v, hg_v, rows_v, o_v2, sem, osem):
        base = _worker_base(tpw)

        def compute(i, h, b, slot):
            for part in range(n_parts):
                def rbody(r, accs):
                    s = hg_v[i, pl.ds(pl.multiple_of((h * PEER_ROWS + r) * SC_LANES, SC_LANES), SC_LANES)]
                    new = []
                    for c in range(cpp):
                        lo, hi = _unpack_words(rows_v[b, r, pl.ds((part * cpp + c) * SC_LANES, SC_LANES)])
                        new.append(accs[2 * c] + s * lo)
                        new.append(accs[2 * c + 1] + s * hi)
                    return tuple(new)

                accs = _sc_loop(PEER_ROWS, rbody, tuple(jnp.zeros((SC_LANES,), F32) for _ in range(2 * cpp)))
                def store(overwrite):
                    for c in range(cpp):
                        lo_at = pl.ds((part * cpp + c) * SC_LANES, SC_LANES)
                        hi_at = pl.ds(half + (part * cpp + c) * SC_LANES, SC_LANES)
                        if overwrite:
                            o_v2[slot, lo_at] = accs[2 * c]
                            o_v2[slot, hi_at] = accs[2 * c + 1]
                        else:
                            o_v2[slot, lo_at] = o_v2[slot, lo_at] + accs[2 * c]
                            o_v2[slot, hi_at] = o_v2[slot, hi_at] + accs[2 * c + 1]

                pl.when(h == 0)(functools.partial(store, True))
                pl.when(h != 0)(functools.partial(store, False))

        @pl.loop(0, tpw // grp)
        def _(g):
            t0 = base + g * grp
            pltpu.sync_copy(idx_hbm.at[pl.ds(2 * t0, 2 * grp)], idx_v)
            pltpu.sync_copy(hg_hbm.at[pl.ds(t0, grp)], hg_v)
            _gather_compute_loop(v_hbm, idx_v, rows_v, sem, o_v2, lambda i: out_hbm.at[t0 + i], osem, grp, compute)

    return pl.kernel(
        body,
        out_type=jax.ShapeDtypeStruct((t, d), F32),
        mesh=_sc_mesh(),
        scratch_types=[
            pltpu.VMEM((2 * grp, PEER_ROWS), jnp.int32),
            pltpu.VMEM((grp, PEER_SEL * SC_LANES), F32),
            pltpu.VMEM((2, PEER_ROWS, half), jnp.int32),
            pltpu.VMEM((2, d), F32),
            pltpu.SemaphoreType.DMA((2,)),
            pltpu.SemaphoreType.DMA((2,)),
        ],
        compiler_params=pltpu.CompilerParams(needs_layout_passes=False),
        name="peer_expert_mix",
    )(hgx, idx2, v_packed)


def _peer_act_kernel(ps_ref, gate_ref, sum_ref, o_ref):
    pre = jnp.dot(ps_ref[...], sum_ref[...], precision=HI, preferred_element_type=F32)
    hg = 0.5 * pre * (1.0 + lax.erf(pre * (1.0 / math.sqrt(2.0)))) * gate_ref[...]
    spread = (((1,), (1,)), ((), ()))
    o_ref[...] = lax.dot_general(hg, sum_ref[...], spread, precision=HI, preferred_element_type=F32)


def peer_act(ps, gates, *, tm=512):
    t, n = ps.shape
    lane_sum = (jnp.arange(n)[:, None] // SC_LANES == jnp.arange(PEER_SEL)[None, :]).astype(F32)
    return pl.pallas_call(
        _peer_act_kernel,
        grid=(t // tm,),
        in_specs=[
            pl.BlockSpec((tm, n), lambda i: (i, 0)),
            pl.BlockSpec((tm, PEER_SEL), lambda i: (i, 0)),
            pl.BlockSpec((n, PEER_SEL), lambda i: (0, 0)),
        ],
        out_specs=pl.BlockSpec((tm, n), lambda i: (i, 0)),
        out_shape=jax.ShapeDtypeStruct((t, n), F32),
        compiler_params=_cparams(("parallel",)),
        name="peer_act",
    )(ps, gates, lane_sum)


BATCH_GROUPS = 8


def kernel(x, norm1_g, w_in, rwkv_mu, w0, w_lora_up, a0, a_lora_up, g_lora_up, k_k, k_a, r_k, lnx_g, lnx_b,
           w_proj_a, w_proj_b, w_out, norm2_g, peer_wq, peer_subkeys, peer_u, peer_v, rel_bias, normf_g):
    bsz, seq, d = x.shape
    depth = norm1_g.shape[0]
    groups = BATCH_GROUPS if bsz % BATCH_GROUPS == 0 else 1
    gb = bsz // groups
    tg = gb * seq
    hs = [x[g * gb:(g + 1) * gb].reshape(tg, d) for g in range(groups)]
    for l in range(depth):
        w_pad = jnp.concatenate([
            w_in[l][:, :COL_A + COL_B_RAW],
            jnp.zeros((d, COL_B - COL_B_RAW), w_in.dtype),
            w_in[l][:, COL_A + COL_B_RAW:]], axis=1).astype(BF16)
        u_packed = _pack_rows(peer_u[l])
        v_packed = _pack_rows(peer_v[l])
        last = l == depth - 1

        def mix(pending, tie=None):
            g, h2d, ps, gates, idx2 = pending
            hgx = peer_act(ps, gates)
            if tie is not None:
                tie, hgx = lax.optimization_barrier((tie, hgx))
            return tie, (g, h2d, peer_expert_mix(hgx, idx2, v_packed))

        def close(mixed, tie=None):
            g, h2d, y2d = mixed
            out = final_norm(h2d, y2d, normf_g) if last else h2d + y2d
            if tie is not None:
                tie, out = lax.optimization_barrier((tie, out))
            hs[g] = out
            return tie

        pending = None
        for g in range(groups):
            p2d = norm_proj(hs[g], norm1_g[l], w_pad)
            p3d = p2d.reshape(gb, seq, -1)
            oa = moba_attention(p3d, rel_bias)
            mixed = None
            if pending is not None:
                oa, mixed = mix(pending, oa)
            prep = rwkv_prep(p3d, rwkv_mu[l], w0[l], w_lora_up[l], a0[l], a_lora_up[l], g_lora_up[l],
                             k_k[l], k_a[l], r_k[l])
            ob = rwkv_scan(*prep, lnx_g[l], lnx_b[l])
            h2d, xn2 = merge_out(hs[g], oa.reshape(tg, WIDTH), ob.reshape(tg, WIDTH), p2d,
                                 w_proj_a[l], w_proj_b[l], w_out[l], norm2_g[l])
            idx, gates = peer_route(xn2, peer_wq[l], peer_subkeys[l])
            if mixed is not None:
                idx = close(mixed, idx)
            idx2 = idx.reshape(-1, PEER_ROWS)
            pending = (g, h2d, peer_expert_dots(xn2, idx2, u_packed), gates, idx2)
        close(mix(pending)[1])
    return jnp.concatenate(hs, axis=0).reshape(bsz, seq, d)
```

```python
import functools
import math

import jax
import jax.numpy as jnp
from jax import lax
from jax.experimental import pallas as pl
from jax.experimental.pallas import tpu as pltpu

F32 = jnp.float32
BF16 = jnp.bfloat16
HI = lax.Precision.HIGHEST

LANES = 128
HEAD_DIM = 64
HEADS = 8
PAIRS = HEADS // 2
WIDTH = HEADS * HEAD_DIM
MOBA_BLOCK = 256
MOBA_TOPK = 3
MOBA_LO = 64
REL_BUCKETS = 32
REL_MAX_DIST = 128
DECAY_LORA = 64
AAA_LORA = 64
GATE_LORA = 160
GN_EPS = 64e-5
RMS_EPS = 1e-6
NEG = -1e30
RWKV_CHUNK = 64
COL_A = 3 * WIDTH
COL_B_RAW = 3 * WIDTH + DECAY_LORA + AAA_LORA + GATE_LORA
COL_B = 4 * WIDTH
COL_G_OFF = COL_A + COL_B
VMEM_LIMIT = 56 * 1024 * 1024


def _cparams(sem):
    return pltpu.CompilerParams(dimension_semantics=sem, vmem_limit_bytes=VMEM_LIMIT)


def _norm_proj_kernel(x_ref, g_ref, w_ref, o_ref, xn_ref):
    @pl.when(pl.program_id(1) == 0)
    def _():
        x = x_ref[...]
        ms = jnp.mean(x * x, axis=-1, keepdims=True)
        xn_ref[...] = (x * lax.rsqrt(ms + RMS_EPS) * g_ref[...]).astype(xn_ref.dtype)

    o_ref[...] = jnp.dot(xn_ref[...], w_ref[...], preferred_element_type=F32).astype(o_ref.dtype)


def norm_proj(x2d, g, w, *, tm=512, tn=512, out_dtype=F32):
    t, d = x2d.shape
    n = w.shape[1]
    return pl.pallas_call(
        _norm_proj_kernel,
        grid=(t // tm, n // tn),
        in_specs=[
            pl.BlockSpec((tm, d), lambda i, j: (i, 0)),
            pl.BlockSpec((1, d), lambda i, j: (0, 0)),
            pl.BlockSpec((d, tn), lambda i, j: (0, j)),
        ],
        out_specs=pl.BlockSpec((tm, tn), lambda i, j: (i, j)),
        out_shape=jax.ShapeDtypeStruct((t, n), out_dtype),
        scratch_shapes=[pltpu.VMEM((tm, d), w.dtype)],
        compiler_params=_cparams(("parallel", "arbitrary")),
        name="norm_proj",
    )(x2d, g.reshape(1, d), w)


def _rel_bucket(dist):
    n = jnp.maximum(dist, 0)
    max_exact = REL_BUCKETS // 2
    nf = jnp.maximum(n, 1).astype(F32)
    large = max_exact + (jnp.log(nf / max_exact) / math.log(REL_MAX_DIST / max_exact)
                         * (REL_BUCKETS - max_exact)).astype(jnp.int32)
    large = jnp.minimum(large, REL_BUCKETS - 1)
    return jnp.where(n < max_exact, n, large)


def _moba_kernel(q_ref, k_ref, v_ref, bown_ref, bprev_ref, bfar_ref, o_ref,
                 kb_ref, vb_ref, kbar_ref, *, n_blocks):
    qb = pl.program_id(2)
    blk = MOBA_BLOCK
    scale = 1.0 / math.sqrt(HEAD_DIM)

    rows2 = 2 * blk
    nt = (((1,), (1,)), ((), ()))

    @pl.when(qb == 0)
    def _():
        kbar_ref[...] = jnp.zeros_like(kbar_ref)
        lane_b = lax.broadcasted_iota(jnp.int32, (blk, LANES), 1)
        for n in range(n_blocks):
            kblk = k_ref[0, n * blk:(n + 1) * blk, :]
            kbar_ref[n:n + 1, :] = jnp.mean(kblk, axis=0, keepdims=True)
            kb_ref[n * blk:(n + 1) * blk, 0:LANES] = kblk.astype(BF16)
            kb_ref[n * blk:(n + 1) * blk, LANES:] = ((lane_b == n) | (lane_b == MOBA_LO + n)).astype(BF16)
        vb_ref[...] = v_ref[0].astype(BF16)

    q2 = q_ref[0]
    first = lax.broadcasted_iota(jnp.int32, (blk, LANES), 1) < HEAD_DIM
    qh = jnp.concatenate([jnp.where(first, q2, 0.0), jnp.where(first, 0.0, q2)], axis=0)
    lane = lax.broadcasted_iota(jnp.int32, (rows2, LANES), 1)
    rowi = lax.broadcasted_iota(jnp.int32, (rows2, LANES), 0)
    gate = lax.dot_general(qh.astype(BF16), kbar_ref[...].astype(BF16), nt, preferred_element_type=F32)
    g = jnp.where(lane < qb, gate, -jnp.inf)
    chosen = lane < 0
    lane_f = lane.astype(F32)
    for _ in range(MOBA_TOPK):
        m = jnp.max(g, axis=1, keepdims=True)
        idx = jnp.min(jnp.where(g == m, lane_f, float(LANES)), axis=1, keepdims=True)
        hit = (lane_f == idx) & (m > -jnp.inf)
        chosen = chosen | hit
        g = jnp.where(hit, -jnp.inf, g)
    nfar = qb - 1
    bfar = jnp.where(rowi < blk, bfar_ref[0, 0:1, 0:1], bfar_ref[1, 0:1, 0:1])
    bhi = bfar.astype(BF16).astype(F32)
    madd = jnp.where(lane < nfar, jnp.where(chosen, bhi, NEG),
                     jnp.where(lane == nfar, jnp.where(chosen, 0.0, NEG),
                               jnp.where((lane >= MOBA_LO) & (lane - MOBA_LO < nfar), bfar - bhi, 0.0)))
    q_aug = jnp.concatenate([(qh * scale).astype(BF16), madd.astype(BF16)], axis=1)

    prev0 = pl.multiple_of(jnp.maximum(nfar, 0) * blk, blk)
    own0 = pl.multiple_of(qb * blk, blk)
    s_prev = (lax.dot_general(q_aug, kb_ref[pl.ds(prev0, blk), :], nt, preferred_element_type=F32)
              + bprev_ref[...].reshape(rows2, blk) + jnp.where(qb > 0, 0.0, NEG))
    s_own = (lax.dot_general(q_aug, kb_ref[pl.ds(own0, blk), :], nt, preferred_element_type=F32)
             + bown_ref[...].reshape(rows2, blk))
    r = lax.broadcasted_iota(jnp.int32, (rows2, blk), 0)
    c = lax.broadcasted_iota(jnp.int32, (rows2, blk), 1)
    s_own = jnp.where(lax.bitwise_and(r, blk - 1) >= c, s_own, NEG)
    s = jnp.concatenate([s_prev, s_own], axis=1)
    m_i = jnp.max(s, axis=1, keepdims=True)
    p = jnp.exp(s - m_i)
    l_i = jnp.sum(p, axis=1, keepdims=True)
    v0 = jnp.concatenate([vb_ref[pl.ds(prev0, blk), :], vb_ref[pl.ds(own0, blk), :]], axis=0)
    acc = jnp.dot(p.astype(BF16), v0, preferred_element_type=F32)

    def body(it, carry):
        m_i, l_i, acc = carry
        k0 = pl.multiple_of(it * rows2, rows2)
        s = lax.dot_general(q_aug, kb_ref[pl.ds(k0, rows2), :], nt, preferred_element_type=F32)
        tail = jnp.where(2 * it + 1 < nfar, 0.0, NEG)
        s = jnp.concatenate([s[:, :blk], s[:, blk:] + tail], axis=1)
        m_new = jnp.maximum(m_i, jnp.max(s, axis=1, keepdims=True))
        alpha = jnp.exp(m_i - m_new)
        p = jnp.exp(s - m_new)
        l_new = alpha * l_i + jnp.sum(p, axis=1, keepdims=True)
        acc_new = alpha * acc + jnp.dot(p.astype(BF16), vb_ref[pl.ds(k0, rows2), :], preferred_element_type=F32)
        return m_new, l_new, acc_new

    m_i, l_i, acc = lax.fori_loop(0, (jnp.maximum(nfar, 0) + 1) // 2, body, (m_i, l_i, acc))
    out = acc / l_i
    o_ref[0] = jnp.where(first, out[:blk], out[blk:])


def moba_attention(p3d, rel_bias):
    bsz, seq, _ = p3d.shape
    blk = MOBA_BLOCK
    n_blocks = seq // blk
    span = 2 * blk
    by_dist = rel_bias[:, _rel_bucket(jnp.arange(span))].astype(F32)
    shift = jnp.arange(span)

    def toeplitz(c):
        k = jnp.where(shift < blk, shift, shift - span)
        s = by_dist[:, jnp.clip(c - k, 0, span - 1)]
        tiled = jnp.tile(s, (1, blk))[:, :blk * (span - 1)]
        return tiled.reshape(HEADS, blk, span - 1)[:, :, :blk]

    bias_own = toeplitz(0)
    bias_prev = toeplitz(blk)
    bias_far = jnp.broadcast_to(rel_bias[:, REL_BUCKETS - 1].astype(F32)[:, None, None], (HEADS, 8, LANES))
    kern = functools.partial(_moba_kernel, n_blocks=n_blocks)
    return pl.pallas_call(
        kern,
        grid=(bsz, PAIRS, n_blocks),
        in_specs=[
            pl.BlockSpec((1, blk, LANES), lambda b, h, i: (b, i, h)),
            pl.BlockSpec((1, seq, LANES), lambda b, h, i: (b, 0, PAIRS + h)),
            pl.BlockSpec((1, seq, LANES), lambda b, h, i: (b, 0, 2 * PAIRS + h)),
            pl.BlockSpec((2, blk, blk), lambda b, h, i: (h, 0, 0)),
            pl.BlockSpec((2, blk, blk), lambda b, h, i: (h, 0, 0)),
            pl.BlockSpec((2, 8, LANES), lambda b, h, i: (h, 0, 0)),
        ],
        out_specs=pl.BlockSpec((1, blk, LANES), lambda b, h, i: (b, i, h)),
        out_shape=jax.ShapeDtypeStruct((bsz, seq, WIDTH), F32),
        scratch_shapes=[
            pltpu.VMEM((seq, 2 * LANES), BF16),
            pltpu.VMEM((seq, LANES), BF16),
            pltpu.VMEM((LANES, LANES), F32),
        ],
        compiler_params=_cparams(("parallel", "parallel", "arbitrary")),
        name="moba",
    )(p3d, p3d, p3d, bias_own, bias_prev, bias_far)


def _shifted(x, carry_row):
    rows = lax.broadcasted_iota(jnp.int32, x.shape, 0)
    return jnp.where(rows == 0, carry_row, pltpu.roll(x, 1, axis=0))


def _rwkv_prep_kernel(pr_ref, pk_ref, pv_ref, pl_ref, mu_ref, vec_ref, ww_ref, wa_ref, wg_ref,
                      bd_ref, tri_ref,
                      rt_ref, kt_ref, kd_ref, bd_out_ref, v_ref, g_ref, bonus_ref, pend_ref,
                      carry_ref, *, chunk):
    @pl.when(pl.program_id(1) == 0)
    def _():
        carry_ref[...] = jnp.zeros_like(carry_ref)

    def mix(ref, j):
        x = ref[0]
        mu = mu_ref[0:1, j * WIDTH:(j + 1) * WIDTH]
        prev = _shifted(x, carry_ref[0:1, j * WIDTH:(j + 1) * WIDTH])
        carry_ref[0:1, j * WIDTH:(j + 1) * WIDTH] = x[x.shape[0] - 1:, :]
        return x + mu * (prev - x)

    r = mix(pr_ref, 0)
    k = mix(pk_ref, 1)
    v = mix(pv_ref, 2)
    lo = mix(pl_ref, 3)
    w0, a0, k_k, k_a, r_k = (vec_ref[i:i + 1, :] for i in range(5))
    xwa = lo[:, 0:LANES]
    xg = lo[:, LANES:3 * LANES]
    lw = jnp.dot(jnp.tanh(xwa), ww_ref[...], precision=HI, preferred_element_type=F32)
    la = jnp.dot(xwa, wa_ref[...], precision=HI, preferred_element_type=F32)
    g = jnp.dot(jax.nn.sigmoid(xg), wg_ref[...], precision=HI, preferred_element_type=F32)
    z = -(w0 + lw)
    softplus = jnp.maximum(z, 0.0) + jnp.log(1.0 + jnp.exp(-jnp.abs(z)))
    logw = -jnp.exp(-softplus - 0.5)
    a = jax.nn.sigmoid(a0 + la)
    kk = k * k_k
    ss = jnp.dot(kk * kk, bd_ref[...], precision=HI, preferred_element_type=F32)
    kk = kk / jnp.maximum(jnp.sqrt(ss), 1e-12)
    k2 = k * (1.0 + (a - 1.0) * k_a)
    rk = jnp.dot(r * k2 * r_k, bd_ref[...], precision=HI, preferred_element_type=F32)
    cs = jnp.dot(tri_ref[...], logw, precision=HI, preferred_element_type=F32)
    e_pos = jnp.exp(cs)
    e_neg = jnp.exp(-cs)
    rt_ref[0] = r * e_pos
    kt_ref[0] = kk * jnp.exp(cs - logw)
    kd_ref[0] = k2 * e_neg
    bd_out_ref[0] = kk * a * e_neg
    v_ref[0] = v
    g_ref[0] = g
    bonus_ref[0] = rk * v
    ts = e_pos.shape[0]
    for c in range(ts // chunk):
        pend_ref[0, c:c + 1, :] = e_pos[(c + 1) * chunk - 1:(c + 1) * chunk, :]


def rwkv_prep(p3d, rwkv_mu, w0, w_lora_up, a0, a_lora_up, g_lora_up, k_k, k_a, r_k, *, ts=512):
    bsz, seq, _ = p3d.shape
    chunk = RWKV_CHUNK
    ts = min(ts, seq)
    mu = jnp.pad(rwkv_mu, (0, COL_B - COL_B_RAW)).reshape(1, COL_B)
    vec = jnp.stack([w0, a0, k_k, k_a, r_k.reshape(-1)] + [jnp.zeros_like(w0)] * 3).astype(F32)
    ww = jnp.zeros((LANES, WIDTH), F32).at[:DECAY_LORA].set(w_lora_up)
    wa = jnp.zeros((LANES, WIDTH), F32).at[DECAY_LORA:DECAY_LORA + AAA_LORA].set(a_lora_up)
    wg = jnp.zeros((2 * LANES, WIDTH), F32).at[:GATE_LORA].set(g_lora_up)
    hid = jnp.arange(WIDTH) // HEAD_DIM
    bd = (hid[:, None] == hid[None, :]).astype(F32)
    tix = jnp.arange(ts)
    tri = ((tix[:, None] // chunk == tix[None, :] // chunk) & (tix[None, :] <= tix[:, None])).astype(F32)
    c0 = COL_A // WIDTH
    big = jax.ShapeDtypeStruct((bsz, seq, WIDTH), F32)
    wspec = lambda shape: pl.BlockSpec(shape, lambda b, i: (0, 0))
    ospec = pl.BlockSpec((1, ts, WIDTH), lambda b, i: (b, i, 0))
    return pl.pallas_call(
        functools.partial(_rwkv_prep_kernel, chunk=chunk),
        grid=(bsz, seq // ts),
        in_specs=[
            pl.BlockSpec((1, ts, WIDTH), lambda b, i: (b, i, c0)),
            pl.BlockSpec((1, ts, WIDTH), lambda b, i: (b, i, c0 + 1)),
            pl.BlockSpec((1, ts, WIDTH), lambda b, i: (b, i, c0 + 2)),
            pl.BlockSpec((1, ts, WIDTH), lambda b, i: (b, i, c0 + 3)),
            wspec((1, COL_B)), wspec((8, WIDTH)), wspec((LANES, WIDTH)), wspec((LANES, WIDTH)),
            wspec((2 * LANES, WIDTH)), wspec((WIDTH, WIDTH)), wspec((ts, ts)),
        ],
        out_specs=[ospec] * 7 + [pl.BlockSpec((1, ts // chunk, WIDTH), lambda b, i: (b, i, 0))],
        out_shape=[big] * 7 + [jax.ShapeDtypeStruct((bsz, seq // chunk, WIDTH), F32)],
        scratch_shapes=[pltpu.VMEM((8, COL_B), F32)],
        compiler_params=_cparams(("parallel", "arbitrary")),
        name="rwkv_prep",
    )(p3d, p3d, p3d, p3d, mu, vec, ww, wa, wg, bd, tri)


def _rwkv_scan_kernel(rt_ref, kt_ref, kd_ref, bd_ref, v_ref, g_ref, bonus_ref, pend_ref, ln_ref, o_ref,
                      state_ref, *, chunk, prec):
    @pl.when(pl.program_id(1) == 0)
    def _():
        state_ref[...] = jnp.zeros_like(state_ref)

    c2 = 2 * chunk
    lane = lax.broadcasted_iota(jnp.int32, (chunk, LANES), 1)
    first = lane < HEAD_DIM
    row = lax.broadcasted_iota(jnp.int32, (c2, c2), 0)
    col = lax.broadcasted_iota(jnp.int32, (c2, c2), 1)
    eye = (row == col).astype(F32)
    hrow = lax.broadcasted_iota(jnp.int32, (LANES, LANES), 0) // HEAD_DIM
    hcol = lax.broadcasted_iota(jnp.int32, (LANES, LANES), 1) // HEAD_DIM
    head_mean = jnp.where(hrow == hcol, 1.0 / HEAD_DIM, 0.0).astype(F32)
    nt = (((1,), (1,)), ((), ()))
    tn = (((0,), (0,)), ((), ()))
    dot = functools.partial(jnp.dot, precision=prec, preferred_element_type=F32)
    dotg = functools.partial(lax.dot_general, precision=prec, preferred_element_type=F32)

    def stack(x):
        return jnp.concatenate([jnp.where(first, x, 0.0), jnp.where(first, 0.0, x)], axis=0)

    pairs = range(PAIRS)
    sls = [slice(hp * LANES, (hp + 1) * LANES) for hp in pairs]
    rs, ks, kds, bs, vs = ([stack(ref[0, :, sl]) for sl in sls] for ref in (rt_ref, kt_ref, kd_ref, bd_ref, v_ref))
    hts = [state_ref[hp] for hp in pairs]
    big = [dotg(jnp.concatenate([ks[hp], rs[hp]], axis=0), jnp.concatenate([bs[hp], kds[hp]], axis=0), nt)
           for hp in pairs]
    a_b = [jnp.where(row > col, big[hp][0:c2, 0:c2], 0.0) for hp in pairs]
    a_k = [jnp.where(row > col, big[hp][0:c2, c2:], 0.0) for hp in pairs]
    a_rb = [jnp.where(row >= col, big[hp][c2:, 0:c2], 0.0) for hp in pairs]
    a_rk = [jnp.where(row >= col, big[hp][c2:, c2:], 0.0) for hp in pairs]
    kh = [dotg(jnp.concatenate([ks[hp], rs[hp]], axis=0), hts[hp], nt) for hp in pairs]
    av = [dot(jnp.concatenate([a_k[hp], a_rk[hp]], axis=0), vs[hp]) for hp in pairs]
    vk = [dotg(vs[hp], kds[hp], tn) for hp in pairs]
    inv = [eye - a_b[hp] for hp in pairs]
    pw = [dot(a_b[hp], a_b[hp]) for hp in pairs]
    n_sq = int(math.log2(chunk)) - 1
    for lvl in range(n_sq):
        if lvl + 1 < n_sq:
            both = [dot(jnp.concatenate([inv[hp], pw[hp]], axis=0), pw[hp]) for hp in pairs]
            inv = [inv[hp] + both[hp][0:c2] for hp in pairs]
            pw = [both[hp][c2:] for hp in pairs]
        else:
            inv = [inv[hp] + dot(inv[hp], pw[hp]) for hp in pairs]
    us = [dot(inv[hp], kh[hp][0:c2] + av[hp][0:c2]) for hp in pairs]
    ub = [dotg(us[hp], bs[hp], tn) for hp in pairs]
    au = [dot(a_rb[hp], us[hp]) for hp in pairs]
    for hp in pairs:
        sl = sls[hp]
        pend = pend_ref[0, 0, 0:1, sl]
        state_ref[hp] = (hts[hp] + vk[hp] - ub[hp]) * pend
        os_ = kh[hp][c2:] + av[hp][c2:] - au[hp]
        o = os_[0:chunk] + os_[chunk:]
        mu = jnp.dot(o, head_mean, precision=HI, preferred_element_type=F32)
        d = o - mu
        var = jnp.dot(d * d, head_mean, precision=HI, preferred_element_type=F32)
        on = d * lax.rsqrt(var + GN_EPS) * ln_ref[0:1, sl] + ln_ref[1:2, sl]
        o_ref[0, :, sl] = (on + bonus_ref[0, :, sl]) * g_ref[0, :, sl]


def rwkv_scan(rt, kt, kd, bd, v, g, bonus, pend, lnx_g, lnx_b, *, prec=None):
    bsz, seq, _ = rt.shape
    chunk = RWKV_CHUNK
    n_chunks = seq // chunk
    ln = jnp.stack([lnx_g, lnx_b] + [jnp.zeros_like(lnx_g)] * 6).astype(F32)
    pend4 = pend.reshape(bsz, n_chunks, 1, WIDTH)
    spec = pl.BlockSpec((1, chunk, WIDTH), lambda b, c: (b, c, 0))
    return pl.pallas_call(
        functools.partial(_rwkv_scan_kernel, chunk=chunk, prec=prec),
        grid=(bsz, n_chunks),
        in_specs=[spec] * 7 + [
            pl.BlockSpec((1, 1, 1, WIDTH), lambda b, c: (b, c, 0, 0)),
            pl.BlockSpec((8, WIDTH), lambda b, c: (0, 0)),
        ],
        out_specs=spec,
        out_shape=jax.ShapeDtypeStruct((bsz, seq, WIDTH), F32),
        scratch_shapes=[pltpu.VMEM((PAIRS, LANES, LANES), F32)],
        compiler_params=_cparams(("parallel", "arbitrary")),
        name="rwkv_scan",
    )(rt, kt, kd, bd, v, g, bonus, pend4, ln)


def _merge_kernel(x_ref, oa_ref, ob_ref, ga_ref, gb_ref, wa_ref, wb_ref, wo_ref, g2_ref,
                  h_ref, xn_ref, acc_ref):
    j = pl.program_id(1)

    @pl.when(j == 0)
    def _():
        acc_ref[...] = x_ref[...]

    ya = jnp.dot(oa_ref[...].astype(BF16), wa_ref[...], preferred_element_type=F32)
    yb = jnp.dot(ob_ref[...].astype(BF16), wb_ref[...], preferred_element_type=F32)
    y = jax.nn.sigmoid(ga_ref[...]) * ya + jax.nn.sigmoid(gb_ref[...]) * yb
    acc_ref[...] += jnp.dot(y.astype(BF16), wo_ref[...], preferred_element_type=F32)

    @pl.when(j == pl.num_programs(1) - 1)
    def _():
        h = acc_ref[...]
        h_ref[...] = h
        ms = jnp.mean(h * h, axis=-1, keepdims=True)
        xn_ref[...] = h * lax.rsqrt(ms + RMS_EPS) * g2_ref[...]


def merge_out(x2d, oa, ob, p2d, w_proj_a, w_proj_b, w_out, norm2_g, *, tm=512):
    t, d = x2d.shape
    tn = WIDTH
    nj = d // tn
    g0 = COL_G_OFF // tn
    big = jax.ShapeDtypeStruct((t, d), F32)
    return pl.pallas_call(
        _merge_kernel,
        grid=(t // tm, nj),
        in_specs=[
            pl.BlockSpec((tm, d), lambda i, j: (i, 0)),
            pl.BlockSpec((tm, WIDTH), lambda i, j: (i, 0)),
            pl.BlockSpec((tm, WIDTH), lambda i, j: (i, 0)),
            pl.BlockSpec((tm, tn), lambda i, j: (i, g0 + j)),
            pl.BlockSpec((tm, tn), lambda i, j: (i, g0 + nj + j)),
            pl.BlockSpec((WIDTH, tn), lambda i, j: (0, j)),
            pl.BlockSpec((WIDTH, tn), lambda i, j: (0, j)),
            pl.BlockSpec((tn, d), lambda i, j: (j, 0)),
            pl.BlockSpec((1, d), lambda i, j: (0, 0)),
        ],
        out_specs=[pl.BlockSpec((tm, d), lambda i, j: (i, 0))] * 2,
        out_shape=[big, big],
        scratch_shapes=[pltpu.VMEM((tm, d), F32)],
        compiler_params=_cparams(("parallel", "arbitrary")),
        name="merge_out",
    )(x2d, oa, ob, p2d, p2d, w_proj_a.astype(BF16), w_proj_b.astype(BF16), w_out.astype(BF16),
      norm2_g.reshape(1, d))


PEER_HEADS = 8
PEER_NKEYS = 128
PEER_TOPK = 16
PEER_HALF = 128


def _topk_rows(s, k):
    n = s.shape[0]
    rows = lax.broadcasted_iota(jnp.int32, s.shape, 0).astype(F32)
    vals, ids = [], []
    for _ in range(k):
        m = jnp.max(s, axis=0, keepdims=True)
        first = jnp.min(jnp.where(s == m, rows, float(n)), axis=0, keepdims=True)
        vals.append(m)
        ids.append(first)
        s = jnp.where(rows == first, -jnp.inf, s)
    return jnp.concatenate(vals, axis=0), jnp.concatenate(ids, axis=0)


def _take_rows(table, ids):
    rows = lax.broadcasted_iota(jnp.int32, table.shape, 0).astype(F32)
    return jnp.sum(jnp.where(rows == ids, table, 0.0), axis=0, keepdims=True)


def _peer_route_kernel(xn_ref, wq_ref, sk_ref, idx_ref, gate_ref, *, prec):
    tt = xn_ref.shape[0]
    k = PEER_TOPK
    q = jnp.dot(xn_ref[...].astype(wq_ref.dtype), wq_ref[...], precision=prec, preferred_element_type=F32)
    nt = (((1,), (1,)), ((), ()))
    idx_rows, gate_rows = [], []
    half = k // 2
    for h in range(PEER_HEADS):
        tops = []
        for p in range(2):
            c0 = (h * 2 + p) * PEER_HALF
            s = lax.dot_general(sk_ref[h, p].astype(wq_ref.dtype), q[:, c0:c0 + PEER_HALF].astype(wq_ref.dtype),
                                nt, precision=prec, preferred_element_type=F32)
            tops.append(_topk_rows(s, k))
        (s0, i0), (s1, i1) = tops
        cs = [s0[0:1] + s1] + [s0[i:i + 1] + s1[0:half] for i in range(1, half)] + [s0[half:] + s1[0:1]]
        best_s, pos = _topk_rows(jnp.concatenate(cs, axis=0), k)
        mid = jnp.floor((pos - k) * (1.0 / half))
        end_mid = float(k + (half - 1) * half)
        i_rank = jnp.where(pos < k, 0.0, jnp.where(pos < end_mid, 1.0 + mid, pos - (end_mid - half)))
        j_rank = jnp.where(pos < k, pos, jnp.where(pos < end_mid, (pos - k) - half * mid, 0.0))
        ids = [_take_rows(i0, i_rank[n:n + 1]) * PEER_NKEYS + _take_rows(i1, j_rank[n:n + 1]) for n in range(k)]
        e = jnp.exp(best_s - best_s[0:1])
        gate_rows.append(e / jnp.sum(e, axis=0, keepdims=True))
        idx_rows.append(jnp.concatenate(ids, axis=0).astype(jnp.int32))
    idx_ref[...] = jnp.concatenate(idx_rows, axis=0).T
    gate_ref[...] = jnp.concatenate(gate_rows, axis=0).T


def peer_route(xn2d, peer_wq, peer_subkeys, *, tt=256, prec=None, wdtype=BF16):
    t, d = xn2d.shape
    nq = peer_wq.shape[1]
    n_sel = PEER_HEADS * PEER_TOPK
    return pl.pallas_call(
        functools.partial(_peer_route_kernel, prec=prec),
        grid=(t // tt,),
        in_specs=[
            pl.BlockSpec((tt, d), lambda i: (i, 0)),
            pl.BlockSpec((d, nq), lambda i: (0, 0)),
            pl.BlockSpec((PEER_HEADS, 2, PEER_NKEYS, PEER_HALF), lambda i: (0, 0, 0, 0)),
        ],
        out_specs=[pl.BlockSpec((tt, n_sel), lambda i: (i, 0))] * 2,
        out_shape=[jax.ShapeDtypeStruct((t, n_sel), jnp.int32), jax.ShapeDtypeStruct((t, n_sel), F32)],
        compiler_params=_cparams(("parallel",)),
        name="peer_route",
    )(xn2d, peer_wq.astype(wdtype), peer_subkeys)


def _final_kernel(h_ref, y_ref, g_ref, o_ref):
    h = h_ref[...] + y_ref[...]
    ms = jnp.mean(h * h, axis=-1, keepdims=True)
    o_ref[...] = h * lax.rsqrt(ms + RMS_EPS) * g_ref[...]


def final_norm(h2d, y2d, g, *, tm=1024):
    t, d = h2d.shape
    spec = pl.BlockSpec((tm, d), lambda i: (i, 0))
    return pl.pallas_call(
        _final_kernel,
        grid=(t // tm,),
        in_specs=[spec, spec, pl.BlockSpec((1, d), lambda i: (0, 0))],
        out_specs=spec,
        out_shape=jax.ShapeDtypeStruct((t, d), F32),
        compiler_params=_cparams(("parallel",)),
        name="final_norm",
    )(h2d, y2d, g.reshape(1, d))


SC_CORES = 2
SC_SUBCORES = 16
SC_LANES = 16
SC_WORKERS = SC_CORES * SC_SUBCORES
PEER_SEL = PEER_HEADS * PEER_TOPK
PEER_ROWS = 64
PEER_GROUP = 32


def _pack_rows(w):
    half = w.shape[1] // 2
    bits = lax.bitcast_convert_type(w.astype(BF16), jnp.uint16).astype(jnp.uint32)
    return lax.bitcast_convert_type(bits[:, :half] | (bits[:, half:] << 16), jnp.int32)


def _unpack_words(w):
    lo = lax.bitcast_convert_type(lax.shift_left(w, jnp.int32(16)), F32)
    hi = lax.bitcast_convert_type(lax.bitwise_and(w, jnp.int32(-65536)), F32)
    return lo, hi


def _packed_products(a_words, b_words):
    from jax.experimental.pallas import tpu_sc as plsc
    prod = plsc.bitcast(a_words, BF16) * plsc.bitcast(b_words, BF16)
    return _unpack_words(plsc.bitcast(prod, jnp.int32))


def _sc_mesh():
    from jax.experimental.pallas import tpu_sc as plsc
    return plsc.VectorSubcoreMesh(core_axis_name="c", subcore_axis_name="s",
                                  num_cores=SC_CORES, num_subcores=SC_SUBCORES)


def _sc_loop(n, body, carry):
    from jax.experimental.pallas import tpu_sc as plsc
    return plsc.parallel_loop(0, n, carry=carry)(body)


def _worker_base(tokens_per_worker):
    return (lax.axis_index("s") * SC_CORES + lax.axis_index("c")) * tokens_per_worker


def _gather_compute_loop(table_hbm, idx_v, rows_v, sem, stage_v, out_row, osem, grp, compute):
    def gather(j, b):
        return pltpu.make_async_copy(table_hbm.at[idx_v.at[j]], rows_v.at[b], sem.at[b])

    def put(i, slot):
        return pltpu.make_async_copy(stage_v.at[slot], out_row(i), osem.at[slot])

    gather(0, 0).start()

    @pl.loop(0, 2 * grp)
    def _(j):
        b = lax.bitwise_and(j, 1)
        i = lax.shift_right_logical(j, 1)
        slot = lax.bitwise_and(i, 1)

        @pl.when((b == 0) & (i >= 2))
        def _():
            put(i - 2, slot).wait()

        @pl.when(j + 1 < 2 * grp)
        def _():
            gather(j + 1, 1 - b).start()

        gather(j, b).wait()
        compute(i, b, b, slot)

        @pl.when(b == 1)
        def _():
            put(i, slot).start()

    put(grp - 2, 0).wait()
    put(grp - 1, 1).wait()


def peer_expert_dots(x_packed, idx2, u_packed):
    t, half = x_packed.shape
    n_chunks = half // SC_LANES
    tpw = t // SC_WORKERS
    grp = min(PEER_GROUP, tpw)
    rows_tog = 4

    def body(x_hbm, idx_hbm, u_hbm, out_hbm, idx_v, x_v, rows_v, ps_v, sem, osem):
        base = _worker_base(tpw)

        def compute(i, h, b, slot):
            @pl.loop(0, PEER_ROWS // rows_tog)
            def _(rg):
                r0 = rg * rows_tog
                accs = [[None, None] for _ in range(rows_tog)]
                for c in range(n_chunks):
                    at = pl.ds(c * SC_LANES, SC_LANES)
                    xw = x_v[i, at]
                    for r in range(rows_tog):
                        for k, term in enumerate(_packed_products(rows_v[b, r0 + r, at], xw)):
                            accs[r][k] = term if accs[r][k] is None else accs[r][k] + term
                for r in range(rows_tog):
                    at = pl.ds(pl.multiple_of((h * PEER_ROWS + r0 + r) * SC_LANES, SC_LANES), SC_LANES)
                    ps_v[slot, at] = accs[r][0] + accs[r][1]

        @pl.loop(0, tpw // grp)
        def _(g):
            t0 = base + g * grp
            pltpu.sync_copy(idx_hbm.at[pl.ds(2 * t0, 2 * grp)], idx_v)
            pltpu.sync_copy(x_hbm.at[pl.ds(t0, grp)], x_v)
            _gather_compute_loop(u_hbm, idx_v, rows_v, sem, ps_v, lambda i: out_hbm.at[t0 + i], osem, grp, compute)

    return pl.kernel(
        body,
        out_type=jax.ShapeDtypeStruct((t, PEER_SEL * SC_LANES), F32),
        mesh=_sc_mesh(),
        scratch_types=[
            pltpu.VMEM((2 * grp, PEER_ROWS), jnp.int32),
            pltpu.VMEM((grp, half), jnp.int32),
            pltpu.VMEM((2, PEER_ROWS, half), jnp.int32),
            pltpu.VMEM((2, PEER_SEL * SC_LANES), F32),
            pltpu.SemaphoreType.DMA((2,)),
            pltpu.SemaphoreType.DMA((2,)),
        ],
        compiler_params=pltpu.CompilerParams(needs_layout_passes=False),
        name="peer_expert_dots",
    )(x_packed, idx2, u_packed)


def peer_expert_mix(hgx, idx2, v_packed):
    t = hgx.shape[0]
    half = v_packed.shape[1]
    d = 2 * half
    tpw = t // SC_WORKERS
    grp = min(PEER_GROUP // 2, tpw)
    n_parts = 2
    cpp = half // SC_LANES // n_parts

    def body(hg_hbm, idx_hbm, v_hbm, out_hbm, idx_v, hg_v, rows_v, o_v2, sem, osem):
        base = _worker_base(tpw)

        def compute(i, h, b, slot):
            for part in range(n_parts):
                def rbody(r, accs):
                    s = hg_v[i, pl.ds(pl.multiple_of((h * PEER_ROWS + r) * SC_LANES, SC_LANES), SC_LANES)]
                    new = []
                    for c in range(cpp):
                        lo, hi = _packed_products(rows_v[b, r, pl.ds((part * cpp + c) * SC_LANES, SC_LANES)], s)
                        new.append(accs[2 * c] + lo)
                        new.append(accs[2 * c + 1] + hi)
                    return tuple(new)

                accs = _sc_loop(PEER_ROWS, rbody, tuple(jnp.zeros((SC_LANES,), F32) for _ in range(2 * cpp)))
                def store(overwrite):
                    for c in range(cpp):
                        lo_at = pl.ds((part * cpp + c) * SC_LANES, SC_LANES)
                        hi_at = pl.ds(half + (part * cpp + c) * SC_LANES, SC_LANES)
                        if overwrite:
                            o_v2[slot, lo_at] = accs[2 * c]
                            o_v2[slot, hi_at] = accs[2 * c + 1]
                        else:
                            o_v2[slot, lo_at] = o_v2[slot, lo_at] + accs[2 * c]
                            o_v2[slot, hi_at] = o_v2[slot, hi_at] + accs[2 * c + 1]

                pl.when(h == 0)(functools.partial(store, True))
                pl.when(h != 0)(functools.partial(store, False))

        @pl.loop(0, tpw // grp)
        def _(g):
            t0 = base + g * grp
            pltpu.sync_copy(idx_hbm.at[pl.ds(2 * t0, 2 * grp)], idx_v)
            pltpu.sync_copy(hg_hbm.at[pl.ds(t0, grp)], hg_v)
            _gather_compute_loop(v_hbm, idx_v, rows_v, sem, o_v2, lambda i: out_hbm.at[t0 + i], osem, grp, compute)

    return pl.kernel(
        body,
        out_type=jax.ShapeDtypeStruct((t, d), F32),
        mesh=_sc_mesh(),
        scratch_types=[
            pltpu.VMEM((2 * grp, PEER_ROWS), jnp.int32),
            pltpu.VMEM((grp, PEER_SEL * SC_LANES), jnp.int32),
            pltpu.VMEM((2, PEER_ROWS, half), jnp.int32),
            pltpu.VMEM((2, d), F32),
            pltpu.SemaphoreType.DMA((2,)),
            pltpu.SemaphoreType.DMA((2,)),
        ],
        compiler_params=pltpu.CompilerParams(needs_layout_passes=False),
        name="peer_expert_mix",
    )(hgx, idx2, v_packed)


def _peer_act_kernel(ps_ref, gate_ref, sum_ref, o_ref):
    pre = jnp.dot(ps_ref[...], sum_ref[...], precision=HI, preferred_element_type=F32)
    hg = 0.5 * pre * (1.0 + lax.erf(pre * (1.0 / math.sqrt(2.0)))) * gate_ref[...]
    spread = (((1,), (1,)), ((), ()))
    hgx = lax.dot_general(hg, sum_ref[...], spread, precision=HI, preferred_element_type=F32)
    bits = lax.bitcast_convert_type(hgx.astype(BF16).astype(F32), jnp.int32)
    o_ref[...] = lax.bitwise_or(bits, lax.shift_right_logical(bits, jnp.int32(16)))


def peer_act(ps, gates, *, tm=512):
    t, n = ps.shape
    lane_sum = (jnp.arange(n)[:, None] // SC_LANES == jnp.arange(PEER_SEL)[None, :]).astype(F32)
    return pl.pallas_call(
        _peer_act_kernel,
        grid=(t // tm,),
        in_specs=[
            pl.BlockSpec((tm, n), lambda i: (i, 0)),
            pl.BlockSpec((tm, PEER_SEL), lambda i: (i, 0)),
            pl.BlockSpec((n, PEER_SEL), lambda i: (0, 0)),
        ],
        out_specs=pl.BlockSpec((tm, n), lambda i: (i, 0)),
        out_shape=jax.ShapeDtypeStruct((t, n), jnp.int32),
        compiler_params=_cparams(("parallel",)),
        name="peer_act",
    )(ps, gates, lane_sum)


BATCH_GROUPS = 8


def kernel(x, norm1_g, w_in, rwkv_mu, w0, w_lora_up, a0, a_lora_up, g_lora_up, k_k, k_a, r_k, lnx_g, lnx_b,
           w_proj_a, w_proj_b, w_out, norm2_g, peer_wq, peer_subkeys, peer_u, peer_v, rel_bias, normf_g):
    bsz, seq, d = x.shape
    depth = norm1_g.shape[0]
    groups = BATCH_GROUPS if bsz % BATCH_GROUPS == 0 else 1
    gb = bsz // groups
    tg = gb * seq
    hs = [x[g * gb:(g + 1) * gb].reshape(tg, d) for g in range(groups)]
    for l in range(depth):
        w_pad = jnp.concatenate([
            w_in[l][:, :COL_A + COL_B_RAW],
            jnp.zeros((d, COL_B - COL_B_RAW), w_in.dtype),
            w_in[l][:, COL_A + COL_B_RAW:]], axis=1).astype(BF16)
        u_packed = _pack_rows(peer_u[l])
        v_packed = _pack_rows(peer_v[l])
        last = l == depth - 1

        def mix(pending, tie=None):
            g, h2d, ps, gates, idx2 = pending
            hgx = peer_act(ps, gates)
            if tie is not None:
                tie, hgx = lax.optimization_barrier((tie, hgx))
            return tie, (g, h2d, peer_expert_mix(hgx, idx2, v_packed))

        def close(mixed, tie=None):
            g, h2d, y2d = mixed
            out = final_norm(h2d, y2d, normf_g) if last else h2d + y2d
            if tie is not None:
                tie, out = lax.optimization_barrier((tie, out))
            hs[g] = out
            return tie

        pending = None
        for g in range(groups):
            p2d = norm_proj(hs[g], norm1_g[l], w_pad)
            p3d = p2d.reshape(gb, seq, -1)
            oa = moba_attention(p3d, rel_bias)
            mixed = None
            if pending is not None:
                oa, mixed = mix(pending, oa)
            prep = rwkv_prep(p3d, rwkv_mu[l], w0[l], w_lora_up[l], a0[l], a_lora_up[l], g_lora_up[l],
                             k_k[l], k_a[l], r_k[l])
            ob = rwkv_scan(*prep, lnx_g[l], lnx_b[l])
            h2d, xn2 = merge_out(hs[g], oa.reshape(tg, WIDTH), ob.reshape(tg, WIDTH), p2d,
                                 w_proj_a[l], w_proj_b[l], w_out[l], norm2_g[l])
            idx, gates = peer_route(xn2, peer_wq[l], peer_subkeys[l])
            if mixed is not None:
                idx = close(mixed, idx)
            idx2 = idx.reshape(-1, PEER_ROWS)
            pending = (g, h2d, peer_expert_dots(_pack_rows(xn2), idx2, u_packed), gates, idx2)
        close(mix(pending)[1])
    return jnp.concatenate(hs, axis=0).reshape(bsz, seq, d)
```

```python
import functools
import math

import jax
import jax.numpy as jnp
from jax import lax
from jax.experimental import pallas as pl
from jax.experimental.pallas import tpu as pltpu

F32 = jnp.float32
BF16 = jnp.bfloat16
HI = lax.Precision.HIGHEST

LANES = 128
HEAD_DIM = 64
HEADS = 8
PAIRS = HEADS // 2
WIDTH = HEADS * HEAD_DIM
MOBA_BLOCK = 256
MOBA_TOPK = 3
MOBA_LO = 64
REL_BUCKETS = 32
REL_MAX_DIST = 128
DECAY_LORA = 64
AAA_LORA = 64
GATE_LORA = 160
GN_EPS = 64e-5
RMS_EPS = 1e-6
NEG = -1e30
RWKV_CHUNK = 64
COL_A = 3 * WIDTH
COL_B_RAW = 3 * WIDTH + DECAY_LORA + AAA_LORA + GATE_LORA
COL_B = 4 * WIDTH
COL_G_OFF = COL_A + COL_B
VMEM_LIMIT = 56 * 1024 * 1024


def _cparams(sem):
    return pltpu.CompilerParams(dimension_semantics=sem, vmem_limit_bytes=VMEM_LIMIT)


def _norm_proj_kernel(x_ref, g_ref, w_ref, o_ref, xn_ref):
    @pl.when(pl.program_id(1) == 0)
    def _():
        x = x_ref[...]
        ms = jnp.mean(x * x, axis=-1, keepdims=True)
        xn_ref[...] = (x * lax.rsqrt(ms + RMS_EPS) * g_ref[...]).astype(xn_ref.dtype)

    o_ref[...] = jnp.dot(xn_ref[...], w_ref[...], preferred_element_type=F32).astype(o_ref.dtype)


def norm_proj(x2d, g, w, *, tm=512, tn=512, out_dtype=F32):
    t, d = x2d.shape
    n = w.shape[1]
    return pl.pallas_call(
        _norm_proj_kernel,
        grid=(t // tm, n // tn),
        in_specs=[
            pl.BlockSpec((tm, d), lambda i, j: (i, 0)),
            pl.BlockSpec((1, d), lambda i, j: (0, 0)),
            pl.BlockSpec((d, tn), lambda i, j: (0, j)),
        ],
        out_specs=pl.BlockSpec((tm, tn), lambda i, j: (i, j)),
        out_shape=jax.ShapeDtypeStruct((t, n), out_dtype),
        scratch_shapes=[pltpu.VMEM((tm, d), w.dtype)],
        compiler_params=_cparams(("parallel", "arbitrary")),
        name="norm_proj",
    )(x2d, g.reshape(1, d), w)


def _rel_bucket(dist):
    n = jnp.maximum(dist, 0)
    max_exact = REL_BUCKETS // 2
    nf = jnp.maximum(n, 1).astype(F32)
    large = max_exact + (jnp.log(nf / max_exact) / math.log(REL_MAX_DIST / max_exact)
                         * (REL_BUCKETS - max_exact)).astype(jnp.int32)
    large = jnp.minimum(large, REL_BUCKETS - 1)
    return jnp.where(n < max_exact, n, large)


def _moba_kernel(q_ref, k_ref, v_ref, bown_ref, bprev_ref, bfar_ref, o_ref,
                 kb_ref, vb_ref, kbar_ref, *, n_blocks):
    qb = pl.program_id(2)
    blk = MOBA_BLOCK
    scale = 1.0 / math.sqrt(HEAD_DIM)

    rows2 = 2 * blk
    nt = (((1,), (1,)), ((), ()))

    @pl.when(qb == 0)
    def _():
        kbar_ref[...] = jnp.zeros_like(kbar_ref)
        lane_b = lax.broadcasted_iota(jnp.int32, (blk, LANES), 1)
        for n in range(n_blocks):
            kblk = k_ref[0, n * blk:(n + 1) * blk, :]
            kbar_ref[n:n + 1, :] = jnp.mean(kblk, axis=0, keepdims=True)
            kb_ref[n * blk:(n + 1) * blk, 0:LANES] = kblk.astype(BF16)
            kb_ref[n * blk:(n + 1) * blk, LANES:] = ((lane_b == n) | (lane_b == MOBA_LO + n)).astype(BF16)
        vb_ref[...] = v_ref[0].astype(BF16)

    q2 = q_ref[0]
    first = lax.broadcasted_iota(jnp.int32, (blk, LANES), 1) < HEAD_DIM
    qh = jnp.concatenate([jnp.where(first, q2, 0.0), jnp.where(first, 0.0, q2)], axis=0)
    lane = lax.broadcasted_iota(jnp.int32, (rows2, LANES), 1)
    rowi = lax.broadcasted_iota(jnp.int32, (rows2, LANES), 0)
    gate = lax.dot_general(qh.astype(BF16), kbar_ref[...].astype(BF16), nt, preferred_element_type=F32)
    g = jnp.where(lane < qb, gate, -jnp.inf)
    chosen = lane < 0
    lane_f = lane.astype(F32)
    for _ in range(MOBA_TOPK):
        m = jnp.max(g, axis=1, keepdims=True)
        idx = jnp.min(jnp.where(g == m, lane_f, float(LANES)), axis=1, keepdims=True)
        hit = (lane_f == idx) & (m > -jnp.inf)
        chosen = chosen | hit
        g = jnp.where(hit, -jnp.inf, g)
    nfar = qb - 1
    bfar = jnp.where(rowi < blk, bfar_ref[0, 0:1, 0:1], bfar_ref[1, 0:1, 0:1])
    bhi = bfar.astype(BF16).astype(F32)
    madd = jnp.where(lane < nfar, jnp.where(chosen, bhi, NEG),
                     jnp.where(lane == nfar, jnp.where(chosen, 0.0, NEG),
                               jnp.where((lane >= MOBA_LO) & (lane - MOBA_LO < nfar), bfar - bhi, 0.0)))
    q_aug = jnp.concatenate([(qh * scale).astype(BF16), madd.astype(BF16)], axis=1)

    prev0 = pl.multiple_of(jnp.maximum(nfar, 0) * blk, blk)
    own0 = pl.multiple_of(qb * blk, blk)
    s_prev = (lax.dot_general(q_aug, kb_ref[pl.ds(prev0, blk), :], nt, preferred_element_type=F32)
              + bprev_ref[...].reshape(rows2, blk) + jnp.where(qb > 0, 0.0, NEG))
    s_own = (lax.dot_general(q_aug, kb_ref[pl.ds(own0, blk), :], nt, preferred_element_type=F32)
             + bown_ref[...].reshape(rows2, blk))
    r = lax.broadcasted_iota(jnp.int32, (rows2, blk), 0)
    c = lax.broadcasted_iota(jnp.int32, (rows2, blk), 1)
    s_own = jnp.where(lax.bitwise_and(r, blk - 1) >= c, s_own, NEG)
    s = jnp.concatenate([s_prev, s_own], axis=1)
    m_i = jnp.max(s, axis=1, keepdims=True)
    p = jnp.exp(s - m_i)
    l_i = jnp.sum(p, axis=1, keepdims=True)
    v0 = jnp.concatenate([vb_ref[pl.ds(prev0, blk), :], vb_ref[pl.ds(own0, blk), :]], axis=0)
    acc = jnp.dot(p.astype(BF16), v0, preferred_element_type=F32)

    def body(it, carry):
        m_i, l_i, acc = carry
        k0 = pl.multiple_of(it * rows2, rows2)
        s = lax.dot_general(q_aug, kb_ref[pl.ds(k0, rows2), :], nt, preferred_element_type=F32)
        tail = jnp.where(2 * it + 1 < nfar, 0.0, NEG)
        s = jnp.concatenate([s[:, :blk], s[:, blk:] + tail], axis=1)
        m_new = jnp.maximum(m_i, jnp.max(s, axis=1, keepdims=True))
        alpha = jnp.exp(m_i - m_new)
        p = jnp.exp(s - m_new)
        l_new = alpha * l_i + jnp.sum(p, axis=1, keepdims=True)
        acc_new = alpha * acc + jnp.dot(p.astype(BF16), vb_ref[pl.ds(k0, rows2), :], preferred_element_type=F32)
        return m_new, l_new, acc_new

    m_i, l_i, acc = lax.fori_loop(0, (jnp.maximum(nfar, 0) + 1) // 2, body, (m_i, l_i, acc))
    out = acc / l_i
    o_ref[0] = jnp.where(first, out[:blk], out[blk:])


def moba_attention(p3d, rel_bias):
    bsz, seq, _ = p3d.shape
    blk = MOBA_BLOCK
    n_blocks = seq // blk
    span = 2 * blk
    by_dist = rel_bias[:, _rel_bucket(jnp.arange(span))].astype(F32)
    shift = jnp.arange(span)

    def toeplitz(c):
        k = jnp.where(shift < blk, shift, shift - span)
        s = by_dist[:, jnp.clip(c - k, 0, span - 1)]
        tiled = jnp.tile(s, (1, blk))[:, :blk * (span - 1)]
        return tiled.reshape(HEADS, blk, span - 1)[:, :, :blk]

    bias_own = toeplitz(0)
    bias_prev = toeplitz(blk)
    bias_far = jnp.broadcast_to(rel_bias[:, REL_BUCKETS - 1].astype(F32)[:, None, None], (HEADS, 8, LANES))
    kern = functools.partial(_moba_kernel, n_blocks=n_blocks)
    return pl.pallas_call(
        kern,
        grid=(bsz, PAIRS, n_blocks),
        in_specs=[
            pl.BlockSpec((1, blk, LANES), lambda b, h, i: (b, i, h)),
            pl.BlockSpec((1, seq, LANES), lambda b, h, i: (b, 0, PAIRS + h)),
            pl.BlockSpec((1, seq, LANES), lambda b, h, i: (b, 0, 2 * PAIRS + h)),
            pl.BlockSpec((2, blk, blk), lambda b, h, i: (h, 0, 0)),
            pl.BlockSpec((2, blk, blk), lambda b, h, i: (h, 0, 0)),
            pl.BlockSpec((2, 8, LANES), lambda b, h, i: (h, 0, 0)),
        ],
        out_specs=pl.BlockSpec((1, blk, LANES), lambda b, h, i: (b, i, h)),
        out_shape=jax.ShapeDtypeStruct((bsz, seq, WIDTH), F32),
        scratch_shapes=[
            pltpu.VMEM((seq, 2 * LANES), BF16),
            pltpu.VMEM((seq, LANES), BF16),
            pltpu.VMEM((LANES, LANES), F32),
        ],
        compiler_params=_cparams(("parallel", "parallel", "arbitrary")),
        name="moba",
    )(p3d, p3d, p3d, bias_own, bias_prev, bias_far)


def _shifted(x, carry_row):
    rows = lax.broadcasted_iota(jnp.int32, x.shape, 0)
    return jnp.where(rows == 0, carry_row, pltpu.roll(x, 1, axis=0))


def _rwkv_prep_kernel(pr_ref, pk_ref, pv_ref, pl_ref, mu_ref, vec_ref, ww_ref, wa_ref, wg_ref,
                      bd_ref, tri_ref,
                      rt_ref, kt_ref, kd_ref, bd_out_ref, v_ref, g_ref, bonus_ref, pend_ref,
                      carry_ref, *, chunk):
    @pl.when(pl.program_id(1) == 0)
    def _():
        carry_ref[...] = jnp.zeros_like(carry_ref)

    def mix(ref, j):
        x = ref[0]
        mu = mu_ref[0:1, j * WIDTH:(j + 1) * WIDTH]
        prev = _shifted(x, carry_ref[0:1, j * WIDTH:(j + 1) * WIDTH])
        carry_ref[0:1, j * WIDTH:(j + 1) * WIDTH] = x[x.shape[0] - 1:, :]
        return x + mu * (prev - x)

    r = mix(pr_ref, 0)
    k = mix(pk_ref, 1)
    v = mix(pv_ref, 2)
    lo = mix(pl_ref, 3)
    w0, a0, k_k, k_a, r_k = (vec_ref[i:i + 1, :] for i in range(5))
    xwa = lo[:, 0:LANES]
    xg = lo[:, LANES:3 * LANES]
    lw = jnp.dot(jnp.tanh(xwa), ww_ref[...], precision=HI, preferred_element_type=F32)
    la = jnp.dot(xwa, wa_ref[...], precision=HI, preferred_element_type=F32)
    g = jnp.dot(jax.nn.sigmoid(xg), wg_ref[...], precision=HI, preferred_element_type=F32)
    z = -(w0 + lw)
    softplus = jnp.maximum(z, 0.0) + jnp.log(1.0 + jnp.exp(-jnp.abs(z)))
    logw = -jnp.exp(-softplus - 0.5)
    a = jax.nn.sigmoid(a0 + la)
    kk = k * k_k
    ss = jnp.dot(kk * kk, bd_ref[...], precision=HI, preferred_element_type=F32)
    kk = kk / jnp.maximum(jnp.sqrt(ss), 1e-12)
    k2 = k * (1.0 + (a - 1.0) * k_a)
    rk = jnp.dot(r * k2 * r_k, bd_ref[...], precision=HI, preferred_element_type=F32)
    cs = jnp.dot(tri_ref[...], logw, precision=HI, preferred_element_type=F32)
    e_pos = jnp.exp(cs)
    e_neg = jnp.exp(-cs)
    rt_ref[0] = r * e_pos
    kt_ref[0] = kk * jnp.exp(cs - logw)
    kd_ref[0] = k2 * e_neg
    bd_out_ref[0] = kk * a * e_neg
    v_ref[0] = v
    g_ref[0] = g
    bonus_ref[0] = rk * v
    ts = e_pos.shape[0]
    for c in range(ts // chunk):
        pend_ref[0, c:c + 1, :] = e_pos[(c + 1) * chunk - 1:(c + 1) * chunk, :]


def rwkv_prep(p3d, rwkv_mu, w0, w_lora_up, a0, a_lora_up, g_lora_up, k_k, k_a, r_k, *, ts=512):
    bsz, seq, _ = p3d.shape
    chunk = RWKV_CHUNK
    ts = min(ts, seq)
    mu = jnp.pad(rwkv_mu, (0, COL_B - COL_B_RAW)).reshape(1, COL_B)
    vec = jnp.stack([w0, a0, k_k, k_a, r_k.reshape(-1)] + [jnp.zeros_like(w0)] * 3).astype(F32)
    ww = jnp.zeros((LANES, WIDTH), F32).at[:DECAY_LORA].set(w_lora_up)
    wa = jnp.zeros((LANES, WIDTH), F32).at[DECAY_LORA:DECAY_LORA + AAA_LORA].set(a_lora_up)
    wg = jnp.zeros((2 * LANES, WIDTH), F32).at[:GATE_LORA].set(g_lora_up)
    hid = jnp.arange(WIDTH) // HEAD_DIM
    bd = (hid[:, None] == hid[None, :]).astype(F32)
    tix = jnp.arange(ts)
    tri = ((tix[:, None] // chunk == tix[None, :] // chunk) & (tix[None, :] <= tix[:, None])).astype(F32)
    c0 = COL_A // WIDTH
    big = jax.ShapeDtypeStruct((bsz, seq, WIDTH), F32)
    wspec = lambda shape: pl.BlockSpec(shape, lambda b, i: (0, 0))
    ospec = pl.BlockSpec((1, ts, WIDTH), lambda b, i: (b, i, 0))
    return pl.pallas_call(
        functools.partial(_rwkv_prep_kernel, chunk=chunk),
        grid=(bsz, seq // ts),
        in_specs=[
            pl.BlockSpec((1, ts, WIDTH), lambda b, i: (b, i, c0)),
            pl.BlockSpec((1, ts, WIDTH), lambda b, i: (b, i, c0 + 1)),
            pl.BlockSpec((1, ts, WIDTH), lambda b, i: (b, i, c0 + 2)),
            pl.BlockSpec((1, ts, WIDTH), lambda b, i: (b, i, c0 + 3)),
            wspec((1, COL_B)), wspec((8, WIDTH)), wspec((LANES, WIDTH)), wspec((LANES, WIDTH)),
            wspec((2 * LANES, WIDTH)), wspec((WIDTH, WIDTH)), wspec((ts, ts)),
        ],
        out_specs=[ospec] * 7 + [pl.BlockSpec((1, ts // chunk, WIDTH), lambda b, i: (b, i, 0))],
        out_shape=[big] * 7 + [jax.ShapeDtypeStruct((bsz, seq // chunk, WIDTH), F32)],
        scratch_shapes=[pltpu.VMEM((8, COL_B), F32)],
        compiler_params=_cparams(("parallel", "arbitrary")),
        name="rwkv_prep",
    )(p3d, p3d, p3d, p3d, mu, vec, ww, wa, wg, bd, tri)


def _rwkv_scan_kernel(rt_ref, kt_ref, kd_ref, bd_ref, v_ref, g_ref, bonus_ref, pend_ref, ln_ref, o_ref,
                      state_ref, *, chunk, prec):
    @pl.when(pl.program_id(1) == 0)
    def _():
        state_ref[...] = jnp.zeros_like(state_ref)

    c2 = 2 * chunk
    lane = lax.broadcasted_iota(jnp.int32, (chunk, LANES), 1)
    first = lane < HEAD_DIM
    row = lax.broadcasted_iota(jnp.int32, (c2, c2), 0)
    col = lax.broadcasted_iota(jnp.int32, (c2, c2), 1)
    eye = (row == col).astype(F32)
    hrow = lax.broadcasted_iota(jnp.int32, (LANES, LANES), 0) // HEAD_DIM
    hcol = lax.broadcasted_iota(jnp.int32, (LANES, LANES), 1) // HEAD_DIM
    head_mean = jnp.where(hrow == hcol, 1.0 / HEAD_DIM, 0.0).astype(F32)
    nt = (((1,), (1,)), ((), ()))
    tn = (((0,), (0,)), ((), ()))
    dot = functools.partial(jnp.dot, precision=prec, preferred_element_type=F32)
    dotg = functools.partial(lax.dot_general, precision=prec, preferred_element_type=F32)

    def stack(x):
        return jnp.concatenate([jnp.where(first, x, 0.0), jnp.where(first, 0.0, x)], axis=0)

    pairs = range(PAIRS)
    sls = [slice(hp * LANES, (hp + 1) * LANES) for hp in pairs]
    rs, ks, kds, bs, vs = ([stack(ref[0, :, sl]) for sl in sls] for ref in (rt_ref, kt_ref, kd_ref, bd_ref, v_ref))
    hts = [state_ref[hp] for hp in pairs]
    big = [dotg(jnp.concatenate([ks[hp], rs[hp]], axis=0), jnp.concatenate([bs[hp], kds[hp]], axis=0), nt)
           for hp in pairs]
    a_b = [jnp.where(row > col, big[hp][0:c2, 0:c2], 0.0) for hp in pairs]
    a_k = [jnp.where(row > col, big[hp][0:c2, c2:], 0.0) for hp in pairs]
    a_rb = [jnp.where(row >= col, big[hp][c2:, 0:c2], 0.0) for hp in pairs]
    a_rk = [jnp.where(row >= col, big[hp][c2:, c2:], 0.0) for hp in pairs]
    kh = [dotg(jnp.concatenate([ks[hp], rs[hp]], axis=0), hts[hp], nt) for hp in pairs]
    av = [dot(jnp.concatenate([a_k[hp], a_rk[hp]], axis=0), vs[hp]) for hp in pairs]
    vk = [dotg(vs[hp], kds[hp], tn) for hp in pairs]
    inv = [eye - a_b[hp] for hp in pairs]
    pw = [dot(a_b[hp], a_b[hp]) for hp in pairs]
    n_sq = int(math.log2(chunk)) - 1
    for lvl in range(n_sq):
        if lvl + 1 < n_sq:
            both = [dot(jnp.concatenate([inv[hp], pw[hp]], axis=0), pw[hp]) for hp in pairs]
            inv = [inv[hp] + both[hp][0:c2] for hp in pairs]
            pw = [both[hp][c2:] for hp in pairs]
        else:
            inv = [inv[hp] + dot(inv[hp], pw[hp]) for hp in pairs]
    us = [dot(inv[hp], kh[hp][0:c2] + av[hp][0:c2]) for hp in pairs]
    ub = [dotg(us[hp], bs[hp], tn) for hp in pairs]
    au = [dot(a_rb[hp], us[hp]) for hp in pairs]
    for hp in pairs:
        sl = sls[hp]
        pend = pend_ref[0, 0, 0:1, sl]
        state_ref[hp] = (hts[hp] + vk[hp] - ub[hp]) * pend
        os_ = kh[hp][c2:] + av[hp][c2:] - au[hp]
        o = os_[0:chunk] + os_[chunk:]
        mu = jnp.dot(o, head_mean, precision=HI, preferred_element_type=F32)
        d = o - mu
        var = jnp.dot(d * d, head_mean, precision=HI, preferred_element_type=F32)
        on = d * lax.rsqrt(var + GN_EPS) * ln_ref[0:1, sl] + ln_ref[1:2, sl]
        o_ref[0, :, sl] = (on + bonus_ref[0, :, sl]) * g_ref[0, :, sl]


def rwkv_scan(rt, kt, kd, bd, v, g, bonus, pend, lnx_g, lnx_b, *, prec=None):
    bsz, seq, _ = rt.shape
    chunk = RWKV_CHUNK
    n_chunks = seq // chunk
    ln = jnp.stack([lnx_g, lnx_b] + [jnp.zeros_like(lnx_g)] * 6).astype(F32)
    pend4 = pend.reshape(bsz, n_chunks, 1, WIDTH)
    spec = pl.BlockSpec((1, chunk, WIDTH), lambda b, c: (b, c, 0))
    return pl.pallas_call(
        functools.partial(_rwkv_scan_kernel, chunk=chunk, prec=prec),
        grid=(bsz, n_chunks),
        in_specs=[spec] * 7 + [
            pl.BlockSpec((1, 1, 1, WIDTH), lambda b, c: (b, c, 0, 0)),
            pl.BlockSpec((8, WIDTH), lambda b, c: (0, 0)),
        ],
        out_specs=spec,
        out_shape=jax.ShapeDtypeStruct((bsz, seq, WIDTH), F32),
        scratch_shapes=[pltpu.VMEM((PAIRS, LANES, LANES), F32)],
        compiler_params=_cparams(("parallel", "arbitrary")),
        name="rwkv_scan",
    )(rt, kt, kd, bd, v, g, bonus, pend4, ln)


def _merge_kernel(x_ref, oa_ref, ob_ref, ga_ref, gb_ref, wa_ref, wb_ref, wo_ref, g2_ref,
                  h_ref, xn_ref, acc_ref):
    j = pl.program_id(1)

    @pl.when(j == 0)
    def _():
        acc_ref[...] = x_ref[...]

    ya = jnp.dot(oa_ref[...].astype(BF16), wa_ref[...], preferred_element_type=F32)
    yb = jnp.dot(ob_ref[...].astype(BF16), wb_ref[...], preferred_element_type=F32)
    y = jax.nn.sigmoid(ga_ref[...]) * ya + jax.nn.sigmoid(gb_ref[...]) * yb
    acc_ref[...] += jnp.dot(y.astype(BF16), wo_ref[...], preferred_element_type=F32)

    @pl.when(j == pl.num_programs(1) - 1)
    def _():
        h = acc_ref[...]
        h_ref[...] = h
        ms = jnp.mean(h * h, axis=-1, keepdims=True)
        xn_ref[...] = h * lax.rsqrt(ms + RMS_EPS) * g2_ref[...]


def merge_out(x2d, oa, ob, p2d, w_proj_a, w_proj_b, w_out, norm2_g, *, tm=512):
    t, d = x2d.shape
    tn = WIDTH
    nj = d // tn
    g0 = COL_G_OFF // tn
    big = jax.ShapeDtypeStruct((t, d), F32)
    return pl.pallas_call(
        _merge_kernel,
        grid=(t // tm, nj),
        in_specs=[
            pl.BlockSpec((tm, d), lambda i, j: (i, 0)),
            pl.BlockSpec((tm, WIDTH), lambda i, j: (i, 0)),
            pl.BlockSpec((tm, WIDTH), lambda i, j: (i, 0)),
            pl.BlockSpec((tm, tn), lambda i, j: (i, g0 + j)),
            pl.BlockSpec((tm, tn), lambda i, j: (i, g0 + nj + j)),
            pl.BlockSpec((WIDTH, tn), lambda i, j: (0, j)),
            pl.BlockSpec((WIDTH, tn), lambda i, j: (0, j)),
            pl.BlockSpec((tn, d), lambda i, j: (j, 0)),
            pl.BlockSpec((1, d), lambda i, j: (0, 0)),
        ],
        out_specs=[pl.BlockSpec((tm, d), lambda i, j: (i, 0))] * 2,
        out_shape=[big, big],
        scratch_shapes=[pltpu.VMEM((tm, d), F32)],
        compiler_params=_cparams(("parallel", "arbitrary")),
        name="merge_out",
    )(x2d, oa, ob, p2d, p2d, w_proj_a.astype(BF16), w_proj_b.astype(BF16), w_out.astype(BF16),
      norm2_g.reshape(1, d))


PEER_HEADS = 8
PEER_NKEYS = 128
PEER_TOPK = 16
PEER_HALF = 128


def _topk_rows(s, k):
    n = s.shape[0]
    rows = lax.broadcasted_iota(jnp.int32, s.shape, 0).astype(F32)
    vals, ids = [], []
    for _ in range(k):
        m = jnp.max(s, axis=0, keepdims=True)
        first = jnp.min(jnp.where(s == m, rows, float(n)), axis=0, keepdims=True)
        vals.append(m)
        ids.append(first)
        s = jnp.where(rows == first, -jnp.inf, s)
    return jnp.concatenate(vals, axis=0), jnp.concatenate(ids, axis=0)


def _take_rows(table, ids):
    rows = lax.broadcasted_iota(jnp.int32, table.shape, 0).astype(F32)
    return jnp.sum(jnp.where(rows == ids, table, 0.0), axis=0, keepdims=True)


def _peer_route_kernel(xn_ref, wq_ref, sk_ref, idx_ref, gate_ref, *, prec):
    tt = xn_ref.shape[0]
    k = PEER_TOPK
    q = jnp.dot(xn_ref[...].astype(wq_ref.dtype), wq_ref[...], precision=prec, preferred_element_type=F32)
    nt = (((1,), (1,)), ((), ()))
    idx_rows, gate_rows = [], []
    half = k // 2
    for h in range(PEER_HEADS):
        tops = []
        for p in range(2):
            c0 = (h * 2 + p) * PEER_HALF
            s = lax.dot_general(sk_ref[h, p].astype(wq_ref.dtype), q[:, c0:c0 + PEER_HALF].astype(wq_ref.dtype),
                                nt, precision=prec, preferred_element_type=F32)
            tops.append(_topk_rows(s, k))
        (s0, i0), (s1, i1) = tops
        cs = [s0[0:1] + s1] + [s0[i:i + 1] + s1[0:half] for i in range(1, half)] + [s0[half:] + s1[0:1]]
        best_s, pos = _topk_rows(jnp.concatenate(cs, axis=0), k)
        mid = jnp.floor((pos - k) * (1.0 / half))
        end_mid = float(k + (half - 1) * half)
        i_rank = jnp.where(pos < k, 0.0, jnp.where(pos < end_mid, 1.0 + mid, pos - (end_mid - half)))
        j_rank = jnp.where(pos < k, pos, jnp.where(pos < end_mid, (pos - k) - half * mid, 0.0))
        ids = [_take_rows(i0, i_rank[n:n + 1]) * PEER_NKEYS + _take_rows(i1, j_rank[n:n + 1]) for n in range(k)]
        e = jnp.exp(best_s - best_s[0:1])
        gate_rows.append(e / jnp.sum(e, axis=0, keepdims=True))
        idx_rows.append(jnp.concatenate(ids, axis=0).astype(jnp.int32))
    idx_ref[...] = jnp.concatenate(idx_rows, axis=0).T
    gate_ref[...] = jnp.concatenate(gate_rows, axis=0).T


def peer_route(xn2d, peer_wq, peer_subkeys, *, tt=256, prec=None, wdtype=BF16):
    t, d = xn2d.shape
    nq = peer_wq.shape[1]
    n_sel = PEER_HEADS * PEER_TOPK
    return pl.pallas_call(
        functools.partial(_peer_route_kernel, prec=prec),
        grid=(t // tt,),
        in_specs=[
            pl.BlockSpec((tt, d), lambda i: (i, 0)),
            pl.BlockSpec((d, nq), lambda i: (0, 0)),
            pl.BlockSpec((PEER_HEADS, 2, PEER_NKEYS, PEER_HALF), lambda i: (0, 0, 0, 0)),
        ],
        out_specs=[pl.BlockSpec((tt, n_sel), lambda i: (i, 0))] * 2,
        out_shape=[jax.ShapeDtypeStruct((t, n_sel), jnp.int32), jax.ShapeDtypeStruct((t, n_sel), F32)],
        compiler_params=_cparams(("parallel",)),
        name="peer_route",
    )(xn2d, peer_wq.astype(wdtype), peer_subkeys)


def _final_kernel(h_ref, y_ref, g_ref, o_ref):
    h = h_ref[...] + y_ref[...]
    ms = jnp.mean(h * h, axis=-1, keepdims=True)
    o_ref[...] = h * lax.rsqrt(ms + RMS_EPS) * g_ref[...]


def final_norm(h2d, y2d, g, *, tm=1024):
    t, d = h2d.shape
    spec = pl.BlockSpec((tm, d), lambda i: (i, 0))
    return pl.pallas_call(
        _final_kernel,
        grid=(t // tm,),
        in_specs=[spec, spec, pl.BlockSpec((1, d), lambda i: (0, 0))],
        out_specs=spec,
        out_shape=jax.ShapeDtypeStruct((t, d), F32),
        compiler_params=_cparams(("parallel",)),
        name="final_norm",
    )(h2d, y2d, g.reshape(1, d))


SC_CORES = 2
SC_SUBCORES = 16
SC_LANES = 16
SC_WORKERS = SC_CORES * SC_SUBCORES
PEER_SEL = PEER_HEADS * PEER_TOPK
PEER_ROWS = 64
PEER_GROUP = 32
PEER_BF16_RUN = 4


def _pack_rows(w):
    half = w.shape[1] // 2
    bits = lax.bitcast_convert_type(w.astype(BF16), jnp.uint16).astype(jnp.uint32)
    return lax.bitcast_convert_type(bits[:, :half] | (bits[:, half:] << 16), jnp.int32)


def _unpack_words(w):
    lo = lax.bitcast_convert_type(lax.shift_left(w, jnp.int32(16)), F32)
    hi = lax.bitcast_convert_type(lax.bitwise_and(w, jnp.int32(-65536)), F32)
    return lo, hi


def _packed_dot(a_words, b_words):
    from jax.experimental.pallas import tpu_sc as plsc
    prods = [plsc.bitcast(a, BF16) * plsc.bitcast(b, BF16) for a, b in zip(a_words, b_words)]
    while len(prods) > 1:
        prods = [prods[k] + prods[k + 1] for k in range(0, len(prods), 2)]
    return _unpack_words(plsc.bitcast(prods[0], jnp.int32))


def _sc_mesh():
    from jax.experimental.pallas import tpu_sc as plsc
    return plsc.VectorSubcoreMesh(core_axis_name="c", subcore_axis_name="s",
                                  num_cores=SC_CORES, num_subcores=SC_SUBCORES)


def _sc_loop(n, body, carry):
    from jax.experimental.pallas import tpu_sc as plsc
    return plsc.parallel_loop(0, n, carry=carry)(body)


def _worker_base(tokens_per_worker):
    return (lax.axis_index("s") * SC_CORES + lax.axis_index("c")) * tokens_per_worker


def _gather_compute_loop(table_hbm, idx_v, rows_v, sem, stage_v, out_row, osem, grp, compute):
    def gather(j, b):
        return pltpu.make_async_copy(table_hbm.at[idx_v.at[j]], rows_v.at[b], sem.at[b])

    def put(i, slot):
        return pltpu.make_async_copy(stage_v.at[slot], out_row(i), osem.at[slot])

    gather(0, 0).start()

    @pl.loop(0, 2 * grp)
    def _(j):
        b = lax.bitwise_and(j, 1)
        i = lax.shift_right_logical(j, 1)
        slot = lax.bitwise_and(i, 1)

        @pl.when((b == 0) & (i >= 2))
        def _():
            put(i - 2, slot).wait()

        @pl.when(j + 1 < 2 * grp)
        def _():
            gather(j + 1, 1 - b).start()

        gather(j, b).wait()
        compute(i, b, b, slot)

        @pl.when(b == 1)
        def _():
            put(i, slot).start()

    put(grp - 2, 0).wait()
    put(grp - 1, 1).wait()


def peer_expert_dots(x_packed, idx2, u_packed):
    t, half = x_packed.shape
    n_chunks = half // SC_LANES
    tpw = t // SC_WORKERS
    grp = min(PEER_GROUP, tpw)
    rows_tog = 4

    def body(x_hbm, idx_hbm, u_hbm, out_hbm, idx_v, x_v, rows_v, ps_v, sem, osem):
        base = _worker_base(tpw)

        def compute(i, h, b, slot):
            @pl.loop(0, PEER_ROWS // rows_tog)
            def _(rg):
                r0 = rg * rows_tog
                accs = [[None, None] for _ in range(rows_tog)]
                for c0 in range(0, n_chunks, PEER_BF16_RUN):
                    ats = [pl.ds((c0 + k) * SC_LANES, SC_LANES) for k in range(PEER_BF16_RUN)]
                    xw = [x_v[i, at] for at in ats]
                    for r in range(rows_tog):
                        terms = _packed_dot([rows_v[b, r0 + r, at] for at in ats], xw)
                        for k, term in enumerate(terms):
                            accs[r][k] = term if accs[r][k] is None else accs[r][k] + term
                for r in range(rows_tog):
                    at = pl.ds(pl.multiple_of((h * PEER_ROWS + r0 + r) * SC_LANES, SC_LANES), SC_LANES)
                    ps_v[slot, at] = accs[r][0] + accs[r][1]

        @pl.loop(0, tpw // grp)
        def _(g):
            t0 = base + g * grp
            pltpu.sync_copy(idx_hbm.at[pl.ds(2 * t0, 2 * grp)], idx_v)
            pltpu.sync_copy(x_hbm.at[pl.ds(t0, grp)], x_v)
            _gather_compute_loop(u_hbm, idx_v, rows_v, sem, ps_v, lambda i: out_hbm.at[t0 + i], osem, grp, compute)

    return pl.kernel(
        body,
        out_type=jax.ShapeDtypeStruct((t, PEER_SEL * SC_LANES), F32),
        mesh=_sc_mesh(),
        scratch_types=[
            pltpu.VMEM((2 * grp, PEER_ROWS), jnp.int32),
            pltpu.VMEM((grp, half), jnp.int32),
            pltpu.VMEM((2, PEER_ROWS, half), jnp.int32),
            pltpu.VMEM((2, PEER_SEL * SC_LANES), F32),
            pltpu.SemaphoreType.DMA((2,)),
            pltpu.SemaphoreType.DMA((2,)),
        ],
        compiler_params=pltpu.CompilerParams(needs_layout_passes=False),
        name="peer_expert_dots",
    )(x_packed, idx2, u_packed)


def peer_expert_mix(hgx, idx2, v_packed):
    t = hgx.shape[0]
    half = v_packed.shape[1]
    d = 2 * half
    tpw = t // SC_WORKERS
    grp = min(PEER_GROUP // 2, tpw)
    n_parts = 2
    cpp = half // SC_LANES // n_parts

    def body(hg_hbm, idx_hbm, v_hbm, out_hbm, idx_v, hg_v, rows_v, o_v2, sem, osem):
        base = _worker_base(tpw)

        def compute(i, h, b, slot):
            for part in range(n_parts):
                def rbody(rq, accs):
                    r0 = rq * PEER_BF16_RUN
                    s = [hg_v[i, pl.ds(pl.multiple_of((h * PEER_ROWS + r0 + k) * SC_LANES, SC_LANES), SC_LANES)]
                         for k in range(PEER_BF16_RUN)]
                    new = []
                    for c in range(cpp):
                        at = pl.ds((part * cpp + c) * SC_LANES, SC_LANES)
                        lo, hi = _packed_dot([rows_v[b, r0 + k, at] for k in range(PEER_BF16_RUN)], s)
                        new.append(accs[2 * c] + lo)
                        new.append(accs[2 * c + 1] + hi)
                    return tuple(new)

                accs = _sc_loop(PEER_ROWS // PEER_BF16_RUN, rbody,
                                tuple(jnp.zeros((SC_LANES,), F32) for _ in range(2 * cpp)))
                def store(overwrite):
                    for c in range(cpp):
                        lo_at = pl.ds((part * cpp + c) * SC_LANES, SC_LANES)
                        hi_at = pl.ds(half + (part * cpp + c) * SC_LANES, SC_LANES)
                        if overwrite:
                            o_v2[slot, lo_at] = accs[2 * c]
                            o_v2[slot, hi_at] = accs[2 * c + 1]
                        else:
                            o_v2[slot, lo_at] = o_v2[slot, lo_at] + accs[2 * c]
                            o_v2[slot, hi_at] = o_v2[slot, hi_at] + accs[2 * c + 1]

                pl.when(h == 0)(functools.partial(store, True))
                pl.when(h != 0)(functools.partial(store, False))

        @pl.loop(0, tpw // grp)
        def _(g):
            t0 = base + g * grp
            pltpu.sync_copy(idx_hbm.at[pl.ds(2 * t0, 2 * grp)], idx_v)
            pltpu.sync_copy(hg_hbm.at[pl.ds(t0, grp)], hg_v)
            _gather_compute_loop(v_hbm, idx_v, rows_v, sem, o_v2, lambda i: out_hbm.at[t0 + i], osem, grp, compute)

    return pl.kernel(
        body,
        out_type=jax.ShapeDtypeStruct((t, d), F32),
        mesh=_sc_mesh(),
        scratch_types=[
            pltpu.VMEM((2 * grp, PEER_ROWS), jnp.int32),
            pltpu.VMEM((grp, PEER_SEL * SC_LANES), jnp.int32),
            pltpu.VMEM((2, PEER_ROWS, half), jnp.int32),
            pltpu.VMEM((2, d), F32),
            pltpu.SemaphoreType.DMA((2,)),
            pltpu.SemaphoreType.DMA((2,)),
        ],
        compiler_params=pltpu.CompilerParams(needs_layout_passes=False),
        name="peer_expert_mix",
    )(hgx, idx2, v_packed)


def _peer_act_kernel(ps_ref, gate_ref, sum_ref, o_ref):
    pre = jnp.dot(ps_ref[...], sum_ref[...], precision=HI, preferred_element_type=F32)
    hg = 0.5 * pre * (1.0 + lax.erf(pre * (1.0 / math.sqrt(2.0)))) * gate_ref[...]
    spread = (((1,), (1,)), ((), ()))
    hgx = lax.dot_general(hg, sum_ref[...], spread, precision=HI, preferred_element_type=F32)
    bits = lax.bitcast_convert_type(hgx.astype(BF16).astype(F32), jnp.int32)
    o_ref[...] = lax.bitwise_or(bits, lax.shift_right_logical(bits, jnp.int32(16)))


def peer_act(ps, gates, *, tm=512):
    t, n = ps.shape
    lane_sum = (jnp.arange(n)[:, None] // SC_LANES == jnp.arange(PEER_SEL)[None, :]).astype(F32)
    return pl.pallas_call(
        _peer_act_kernel,
        grid=(t // tm,),
        in_specs=[
            pl.BlockSpec((tm, n), lambda i: (i, 0)),
            pl.BlockSpec((tm, PEER_SEL), lambda i: (i, 0)),
            pl.BlockSpec((n, PEER_SEL), lambda i: (0, 0)),
        ],
        out_specs=pl.BlockSpec((tm, n), lambda i: (i, 0)),
        out_shape=jax.ShapeDtypeStruct((t, n), jnp.int32),
        compiler_params=_cparams(("parallel",)),
        name="peer_act",
    )(ps, gates, lane_sum)


BATCH_GROUPS = 8


def kernel(x, norm1_g, w_in, rwkv_mu, w0, w_lora_up, a0, a_lora_up, g_lora_up, k_k, k_a, r_k, lnx_g, lnx_b,
           w_proj_a, w_proj_b, w_out, norm2_g, peer_wq, peer_subkeys, peer_u, peer_v, rel_bias, normf_g):
    bsz, seq, d = x.shape
    depth = norm1_g.shape[0]
    groups = BATCH_GROUPS if bsz % BATCH_GROUPS == 0 else 1
    gb = bsz // groups
    tg = gb * seq
    hs = [x[g * gb:(g + 1) * gb].reshape(tg, d) for g in range(groups)]
    for l in range(depth):
        w_pad = jnp.concatenate([
            w_in[l][:, :COL_A + COL_B_RAW],
            jnp.zeros((d, COL_B - COL_B_RAW), w_in.dtype),
            w_in[l][:, COL_A + COL_B_RAW:]], axis=1).astype(BF16)
        u_packed = _pack_rows(peer_u[l])
        v_packed = _pack_rows(peer_v[l])
        last = l == depth - 1

        def mix(pending, tie=None):
            g, h2d, ps, gates, idx2 = pending
            hgx = peer_act(ps, gates)
            if tie is not None:
                tie, hgx = lax.optimization_barrier((tie, hgx))
            return tie, (g, h2d, peer_expert_mix(hgx, idx2, v_packed))

        def close(mixed, tie=None):
            g, h2d, y2d = mixed
            out = final_norm(h2d, y2d, normf_g) if last else h2d + y2d
            if tie is not None:
                tie, out = lax.optimization_barrier((tie, out))
            hs[g] = out
            return tie

        pending = None
        for g in range(groups):
            p2d = norm_proj(hs[g], norm1_g[l], w_pad)
            p3d = p2d.reshape(gb, seq, -1)
            oa = moba_attention(p3d, rel_bias)
            mixed = None
            if pending is not None:
                oa, mixed = mix(pending, oa)
            prep = rwkv_prep(p3d, rwkv_mu[l], w0[l], w_lora_up[l], a0[l], a_lora_up[l], g_lora_up[l],
                             k_k[l], k_a[l], r_k[l])
            ob = rwkv_scan(*prep, lnx_g[l], lnx_b[l])
            h2d, xn2 = merge_out(hs[g], oa.reshape(tg, WIDTH), ob.reshape(tg, WIDTH), p2d,
                                 w_proj_a[l], w_proj_b[l], w_out[l], norm2_g[l])
            idx, gates = peer_route(xn2, peer_wq[l], peer_subkeys[l])
            if mixed is not None:
                idx = close(mixed, idx)
            idx2 = idx.reshape(-1, PEER_ROWS)
            pending = (g, h2d, peer_expert_dots(_pack_rows(xn2), idx2, u_packed), gates, idx2)
        close(mix(pending)[1])
    return jnp.concatenate(hs, axis=0).reshape(bsz, seq, d)
```

```python
import functools
import math

import jax
import jax.numpy as jnp
from jax import lax
from jax.experimental import pallas as pl
from jax.experimental.pallas import tpu as pltpu

F32 = jnp.float32
BF16 = jnp.bfloat16
HI = lax.Precision.HIGHEST

LANES = 128
HEAD_DIM = 64
HEADS = 8
PAIRS = HEADS // 2
WIDTH = HEADS * HEAD_DIM
MOBA_BLOCK = 256
MOBA_TOPK = 3
MOBA_LO = 64
REL_BUCKETS = 32
REL_MAX_DIST = 128
DECAY_LORA = 64
AAA_LORA = 64
GATE_LORA = 160
GN_EPS = 64e-5
RMS_EPS = 1e-6
NEG = -1e30
RWKV_CHUNK = 64
COL_A = 3 * WIDTH
COL_B_RAW = 3 * WIDTH + DECAY_LORA + AAA_LORA + GATE_LORA
COL_B = 4 * WIDTH
COL_G_OFF = COL_A + COL_B
VMEM_LIMIT = 56 * 1024 * 1024


def _cparams(sem):
    return pltpu.CompilerParams(dimension_semantics=sem, vmem_limit_bytes=VMEM_LIMIT)


def _norm_proj_kernel(x_ref, g_ref, w_ref, o_ref, xn_ref):
    @pl.when(pl.program_id(1) == 0)
    def _():
        x = x_ref[...]
        ms = jnp.mean(x * x, axis=-1, keepdims=True)
        xn_ref[...] = (x * lax.rsqrt(ms + RMS_EPS) * g_ref[...]).astype(xn_ref.dtype)

    o_ref[...] = jnp.dot(xn_ref[...], w_ref[...], preferred_element_type=F32).astype(o_ref.dtype)


def norm_proj(x2d, g, w, *, tm=512, tn=512, out_dtype=F32):
    t, d = x2d.shape
    n = w.shape[1]
    return pl.pallas_call(
        _norm_proj_kernel,
        grid=(t // tm, n // tn),
        in_specs=[
            pl.BlockSpec((tm, d), lambda i, j: (i, 0)),
            pl.BlockSpec((1, d), lambda i, j: (0, 0)),
            pl.BlockSpec((d, tn), lambda i, j: (0, j)),
        ],
        out_specs=pl.BlockSpec((tm, tn), lambda i, j: (i, j)),
        out_shape=jax.ShapeDtypeStruct((t, n), out_dtype),
        scratch_shapes=[pltpu.VMEM((tm, d), w.dtype)],
        compiler_params=_cparams(("parallel", "arbitrary")),
        name="norm_proj",
    )(x2d, g.reshape(1, d), w)


def _rel_bucket(dist):
    n = jnp.maximum(dist, 0)
    max_exact = REL_BUCKETS // 2
    nf = jnp.maximum(n, 1).astype(F32)
    large = max_exact + (jnp.log(nf / max_exact) / math.log(REL_MAX_DIST / max_exact)
                         * (REL_BUCKETS - max_exact)).astype(jnp.int32)
    large = jnp.minimum(large, REL_BUCKETS - 1)
    return jnp.where(n < max_exact, n, large)


def _moba_kernel(q_ref, k_ref, v_ref, bown_ref, bprev_ref, bfar_ref, o_ref,
                 kb_ref, vb_ref, kbar_ref, *, n_blocks):
    qb = pl.program_id(2)
    blk = MOBA_BLOCK
    scale = 1.0 / math.sqrt(HEAD_DIM)

    rows2 = 2 * blk
    nt = (((1,), (1,)), ((), ()))

    @pl.when(qb == 0)
    def _():
        kbar_ref[...] = jnp.zeros_like(kbar_ref)
        lane_b = lax.broadcasted_iota(jnp.int32, (blk, LANES), 1)
        for n in range(n_blocks):
            kblk = k_ref[0, n * blk:(n + 1) * blk, :]
            kbar_ref[n:n + 1, :] = jnp.mean(kblk, axis=0, keepdims=True)
            kb_ref[n * blk:(n + 1) * blk, 0:LANES] = kblk.astype(BF16)
            kb_ref[n * blk:(n + 1) * blk, LANES:] = ((lane_b == n) | (lane_b == MOBA_LO + n)).astype(BF16)
        vb_ref[...] = v_ref[0].astype(BF16)

    q2 = q_ref[0]
    first = lax.broadcasted_iota(jnp.int32, (blk, LANES), 1) < HEAD_DIM
    qh = jnp.concatenate([jnp.where(first, q2, 0.0), jnp.where(first, 0.0, q2)], axis=0)
    lane = lax.broadcasted_iota(jnp.int32, (rows2, LANES), 1)
    rowi = lax.broadcasted_iota(jnp.int32, (rows2, LANES), 0)
    gate = lax.dot_general(qh.astype(BF16), kbar_ref[...].astype(BF16), nt, preferred_element_type=F32)
    g = jnp.where(lane < qb, gate, -jnp.inf)
    chosen = lane < 0
    lane_f = lane.astype(F32)
    for _ in range(MOBA_TOPK):
        m = jnp.max(g, axis=1, keepdims=True)
        idx = jnp.min(jnp.where(g == m, lane_f, float(LANES)), axis=1, keepdims=True)
        hit = (lane_f == idx) & (m > -jnp.inf)
        chosen = chosen | hit
        g = jnp.where(hit, -jnp.inf, g)
    nfar = qb - 1
    bfar = jnp.where(rowi < blk, bfar_ref[0, 0:1, 0:1], bfar_ref[1, 0:1, 0:1])
    bhi = bfar.astype(BF16).astype(F32)
    madd = jnp.where(lane < nfar, jnp.where(chosen, bhi, NEG),
                     jnp.where(lane == nfar, jnp.where(chosen, 0.0, NEG),
                               jnp.where((lane >= MOBA_LO) & (lane - MOBA_LO < nfar), bfar - bhi, 0.0)))
    q_aug = jnp.concatenate([(qh * scale).astype(BF16), madd.astype(BF16)], axis=1)

    prev0 = pl.multiple_of(jnp.maximum(nfar, 0) * blk, blk)
    own0 = pl.multiple_of(qb * blk, blk)
    s_prev = (lax.dot_general(q_aug, kb_ref[pl.ds(prev0, blk), :], nt, preferred_element_type=F32)
              + bprev_ref[...].reshape(rows2, blk) + jnp.where(qb > 0, 0.0, NEG))
    s_own = (lax.dot_general(q_aug, kb_ref[pl.ds(own0, blk), :], nt, preferred_element_type=F32)
             + bown_ref[...].reshape(rows2, blk))
    r = lax.broadcasted_iota(jnp.int32, (rows2, blk), 0)
    c = lax.broadcasted_iota(jnp.int32, (rows2, blk), 1)
    s_own = jnp.where(lax.bitwise_and(r, blk - 1) >= c, s_own, NEG)
    s = jnp.concatenate([s_prev, s_own], axis=1)
    m_i = jnp.max(s, axis=1, keepdims=True)
    p = jnp.exp(s - m_i)
    l_i = jnp.sum(p, axis=1, keepdims=True)
    v0 = jnp.concatenate([vb_ref[pl.ds(prev0, blk), :], vb_ref[pl.ds(own0, blk), :]], axis=0)
    acc = jnp.dot(p.astype(BF16), v0, preferred_element_type=F32)

    def body(it, carry):
        m_i, l_i, acc = carry
        k0 = pl.multiple_of(it * rows2, rows2)
        s = lax.dot_general(q_aug, kb_ref[pl.ds(k0, rows2), :], nt, preferred_element_type=F32)
        tail = jnp.where(2 * it + 1 < nfar, 0.0, NEG)
        s = jnp.concatenate([s[:, :blk], s[:, blk:] + tail], axis=1)
        m_new = jnp.maximum(m_i, jnp.max(s, axis=1, keepdims=True))
        alpha = jnp.exp(m_i - m_new)
        p = jnp.exp(s - m_new)
        l_new = alpha * l_i + jnp.sum(p, axis=1, keepdims=True)
        acc_new = alpha * acc + jnp.dot(p.astype(BF16), vb_ref[pl.ds(k0, rows2), :], preferred_element_type=F32)
        return m_new, l_new, acc_new

    m_i, l_i, acc = lax.fori_loop(0, (jnp.maximum(nfar, 0) + 1) // 2, body, (m_i, l_i, acc))
    out = acc / l_i
    o_ref[0] = jnp.where(first, out[:blk], out[blk:])


def moba_attention(p3d, rel_bias):
    bsz, seq, _ = p3d.shape
    blk = MOBA_BLOCK
    n_blocks = seq // blk
    span = 2 * blk
    by_dist = rel_bias[:, _rel_bucket(jnp.arange(span))].astype(F32)
    shift = jnp.arange(span)

    def toeplitz(c):
        k = jnp.where(shift < blk, shift, shift - span)
        s = by_dist[:, jnp.clip(c - k, 0, span - 1)]
        tiled = jnp.tile(s, (1, blk))[:, :blk * (span - 1)]
        return tiled.reshape(HEADS, blk, span - 1)[:, :, :blk]

    bias_own = toeplitz(0)
    bias_prev = toeplitz(blk)
    bias_far = jnp.broadcast_to(rel_bias[:, REL_BUCKETS - 1].astype(F32)[:, None, None], (HEADS, 8, LANES))
    kern = functools.partial(_moba_kernel, n_blocks=n_blocks)
    return pl.pallas_call(
        kern,
        grid=(bsz, PAIRS, n_blocks),
        in_specs=[
            pl.BlockSpec((1, blk, LANES), lambda b, h, i: (b, i, h)),
            pl.BlockSpec((1, seq, LANES), lambda b, h, i: (b, 0, PAIRS + h)),
            pl.BlockSpec((1, seq, LANES), lambda b, h, i: (b, 0, 2 * PAIRS + h)),
            pl.BlockSpec((2, blk, blk), lambda b, h, i: (h, 0, 0)),
            pl.BlockSpec((2, blk, blk), lambda b, h, i: (h, 0, 0)),
            pl.BlockSpec((2, 8, LANES), lambda b, h, i: (h, 0, 0)),
        ],
        out_specs=pl.BlockSpec((1, blk, LANES), lambda b, h, i: (b, i, h)),
        out_shape=jax.ShapeDtypeStruct((bsz, seq, WIDTH), F32),
        scratch_shapes=[
            pltpu.VMEM((seq, 2 * LANES), BF16),
            pltpu.VMEM((seq, LANES), BF16),
            pltpu.VMEM((LANES, LANES), F32),
        ],
        compiler_params=_cparams(("parallel", "parallel", "arbitrary")),
        name="moba",
    )(p3d, p3d, p3d, bias_own, bias_prev, bias_far)


def _shifted(x, carry_row):
    rows = lax.broadcasted_iota(jnp.int32, x.shape, 0)
    return jnp.where(rows == 0, carry_row, pltpu.roll(x, 1, axis=0))


def _rwkv_prep_kernel(pr_ref, pk_ref, pv_ref, pl_ref, mu_ref, vec_ref, ww_ref, wa_ref, wg_ref,
                      bd_ref, tri_ref,
                      rt_ref, kt_ref, kd_ref, bd_out_ref, v_ref, g_ref, bonus_ref, pend_ref,
                      carry_ref, *, chunk):
    @pl.when(pl.program_id(1) == 0)
    def _():
        carry_ref[...] = jnp.zeros_like(carry_ref)

    def mix(ref, j):
        x = ref[0]
        mu = mu_ref[0:1, j * WIDTH:(j + 1) * WIDTH]
        prev = _shifted(x, carry_ref[0:1, j * WIDTH:(j + 1) * WIDTH])
        carry_ref[0:1, j * WIDTH:(j + 1) * WIDTH] = x[x.shape[0] - 1:, :]
        return x + mu * (prev - x)

    r = mix(pr_ref, 0)
    k = mix(pk_ref, 1)
    v = mix(pv_ref, 2)
    lo = mix(pl_ref, 3)
    w0, a0, k_k, k_a, r_k = (vec_ref[i:i + 1, :] for i in range(5))
    xwa = lo[:, 0:LANES]
    xg = lo[:, LANES:3 * LANES]
    lw = jnp.dot(jnp.tanh(xwa), ww_ref[...], precision=HI, preferred_element_type=F32)
    la = jnp.dot(xwa, wa_ref[...], precision=HI, preferred_element_type=F32)
    g = jnp.dot(jax.nn.sigmoid(xg), wg_ref[...], precision=HI, preferred_element_type=F32)
    z = -(w0 + lw)
    softplus = jnp.maximum(z, 0.0) + jnp.log(1.0 + jnp.exp(-jnp.abs(z)))
    logw = -jnp.exp(-softplus - 0.5)
    a = jax.nn.sigmoid(a0 + la)
    kk = k * k_k
    ss = jnp.dot(kk * kk, bd_ref[...], precision=HI, preferred_element_type=F32)
    kk = kk / jnp.maximum(jnp.sqrt(ss), 1e-12)
    k2 = k * (1.0 + (a - 1.0) * k_a)
    rk = jnp.dot(r * k2 * r_k, bd_ref[...], precision=HI, preferred_element_type=F32)
    cs = jnp.dot(tri_ref[...], logw, precision=HI, preferred_element_type=F32)
    e_pos = jnp.exp(cs)
    e_neg = jnp.exp(-cs)
    rt_ref[0] = r * e_pos
    kt_ref[0] = kk * jnp.exp(cs - logw)
    kd_ref[0] = k2 * e_neg
    bd_out_ref[0] = kk * a * e_neg
    v_ref[0] = v
    g_ref[0] = g
    bonus_ref[0] = rk * v
    ts = e_pos.shape[0]
    for c in range(ts // chunk):
        pend_ref[0, c:c + 1, :] = e_pos[(c + 1) * chunk - 1:(c + 1) * chunk, :]


def rwkv_prep(p3d, rwkv_mu, w0, w_lora_up, a0, a_lora_up, g_lora_up, k_k, k_a, r_k, *, ts=512):
    bsz, seq, _ = p3d.shape
    chunk = RWKV_CHUNK
    ts = min(ts, seq)
    mu = jnp.pad(rwkv_mu, (0, COL_B - COL_B_RAW)).reshape(1, COL_B)
    vec = jnp.stack([w0, a0, k_k, k_a, r_k.reshape(-1)] + [jnp.zeros_like(w0)] * 3).astype(F32)
    ww = jnp.zeros((LANES, WIDTH), F32).at[:DECAY_LORA].set(w_lora_up)
    wa = jnp.zeros((LANES, WIDTH), F32).at[DECAY_LORA:DECAY_LORA + AAA_LORA].set(a_lora_up)
    wg = jnp.zeros((2 * LANES, WIDTH), F32).at[:GATE_LORA].set(g_lora_up)
    hid = jnp.arange(WIDTH) // HEAD_DIM
    bd = (hid[:, None] == hid[None, :]).astype(F32)
    tix = jnp.arange(ts)
    tri = ((tix[:, None] // chunk == tix[None, :] // chunk) & (tix[None, :] <= tix[:, None])).astype(F32)
    c0 = COL_A // WIDTH
    big = jax.ShapeDtypeStruct((bsz, seq, WIDTH), F32)
    wspec = lambda shape: pl.BlockSpec(shape, lambda b, i: (0, 0))
    ospec = pl.BlockSpec((1, ts, WIDTH), lambda b, i: (b, i, 0))
    return pl.pallas_call(
        functools.partial(_rwkv_prep_kernel, chunk=chunk),
        grid=(bsz, seq // ts),
        in_specs=[
            pl.BlockSpec((1, ts, WIDTH), lambda b, i: (b, i, c0)),
            pl.BlockSpec((1, ts, WIDTH), lambda b, i: (b, i, c0 + 1)),
            pl.BlockSpec((1, ts, WIDTH), lambda b, i: (b, i, c0 + 2)),
            pl.BlockSpec((1, ts, WIDTH), lambda b, i: (b, i, c0 + 3)),
            wspec((1, COL_B)), wspec((8, WIDTH)), wspec((LANES, WIDTH)), wspec((LANES, WIDTH)),
            wspec((2 * LANES, WIDTH)), wspec((WIDTH, WIDTH)), wspec((ts, ts)),
        ],
        out_specs=[ospec] * 7 + [pl.BlockSpec((1, ts // chunk, WIDTH), lambda b, i: (b, i, 0))],
        out_shape=[big] * 7 + [jax.ShapeDtypeStruct((bsz, seq // chunk, WIDTH), F32)],
        scratch_shapes=[pltpu.VMEM((8, COL_B), F32)],
        compiler_params=_cparams(("parallel", "arbitrary")),
        name="rwkv_prep",
    )(p3d, p3d, p3d, p3d, mu, vec, ww, wa, wg, bd, tri)


def _rwkv_scan_kernel(rt_ref, kt_ref, kd_ref, bd_ref, v_ref, g_ref, bonus_ref, pend_ref, ln_ref, o_ref,
                      state_ref, *, chunk, prec):
    @pl.when(pl.program_id(1) == 0)
    def _():
        state_ref[...] = jnp.zeros_like(state_ref)

    c2 = 2 * chunk
    lane = lax.broadcasted_iota(jnp.int32, (chunk, LANES), 1)
    first = lane < HEAD_DIM
    row = lax.broadcasted_iota(jnp.int32, (c2, c2), 0)
    col = lax.broadcasted_iota(jnp.int32, (c2, c2), 1)
    eye = (row == col).astype(F32)
    hrow = lax.broadcasted_iota(jnp.int32, (LANES, LANES), 0) // HEAD_DIM
    hcol = lax.broadcasted_iota(jnp.int32, (LANES, LANES), 1) // HEAD_DIM
    head_mean = jnp.where(hrow == hcol, 1.0 / HEAD_DIM, 0.0).astype(F32)
    nt = (((1,), (1,)), ((), ()))
    tn = (((0,), (0,)), ((), ()))
    dot = functools.partial(jnp.dot, precision=prec, preferred_element_type=F32)
    dotg = functools.partial(lax.dot_general, precision=prec, preferred_element_type=F32)

    def stack(x):
        return jnp.concatenate([jnp.where(first, x, 0.0), jnp.where(first, 0.0, x)], axis=0)

    pairs = range(PAIRS)
    sls = [slice(hp * LANES, (hp + 1) * LANES) for hp in pairs]
    rs, ks, kds, bs, vs = ([stack(ref[0, :, sl]) for sl in sls] for ref in (rt_ref, kt_ref, kd_ref, bd_ref, v_ref))
    hts = [state_ref[hp] for hp in pairs]
    big = [dotg(jnp.concatenate([ks[hp], rs[hp]], axis=0), jnp.concatenate([bs[hp], kds[hp]], axis=0), nt)
           for hp in pairs]
    a_b = [jnp.where(row > col, big[hp][0:c2, 0:c2], 0.0) for hp in pairs]
    a_k = [jnp.where(row > col, big[hp][0:c2, c2:], 0.0) for hp in pairs]
    a_rb = [jnp.where(row >= col, big[hp][c2:, 0:c2], 0.0) for hp in pairs]
    a_rk = [jnp.where(row >= col, big[hp][c2:, c2:], 0.0) for hp in pairs]
    kh = [dotg(jnp.concatenate([ks[hp], rs[hp]], axis=0), hts[hp], nt) for hp in pairs]
    av = [dot(jnp.concatenate([a_k[hp], a_rk[hp]], axis=0), vs[hp]) for hp in pairs]
    vk = [dotg(vs[hp], kds[hp], tn) for hp in pairs]
    inv = [eye - a_b[hp] for hp in pairs]
    pw = [dot(a_b[hp], a_b[hp]) for hp in pairs]
    n_sq = int(math.log2(chunk)) - 1
    for lvl in range(n_sq):
        if lvl + 1 < n_sq:
            both = [dot(jnp.concatenate([inv[hp], pw[hp]], axis=0), pw[hp]) for hp in pairs]
            inv = [inv[hp] + both[hp][0:c2] for hp in pairs]
            pw = [both[hp][c2:] for hp in pairs]
        else:
            inv = [inv[hp] + dot(inv[hp], pw[hp]) for hp in pairs]
    us = [dot(inv[hp], kh[hp][0:c2] + av[hp][0:c2]) for hp in pairs]
    ub = [dotg(us[hp], bs[hp], tn) for hp in pairs]
    au = [dot(a_rb[hp], us[hp]) for hp in pairs]
    for hp in pairs:
        sl = sls[hp]
        pend = pend_ref[0, 0, 0:1, sl]
        state_ref[hp] = (hts[hp] + vk[hp] - ub[hp]) * pend
        os_ = kh[hp][c2:] + av[hp][c2:] - au[hp]
        o = os_[0:chunk] + os_[chunk:]
        mu = jnp.dot(o, head_mean, precision=HI, preferred_element_type=F32)
        d = o - mu
        var = jnp.dot(d * d, head_mean, precision=HI, preferred_element_type=F32)
        on = d * lax.rsqrt(var + GN_EPS) * ln_ref[0:1, sl] + ln_ref[1:2, sl]
        o_ref[0, :, sl] = (on + bonus_ref[0, :, sl]) * g_ref[0, :, sl]


def rwkv_scan(rt, kt, kd, bd, v, g, bonus, pend, lnx_g, lnx_b, *, prec=None):
    bsz, seq, _ = rt.shape
    chunk = RWKV_CHUNK
    n_chunks = seq // chunk
    ln = jnp.stack([lnx_g, lnx_b] + [jnp.zeros_like(lnx_g)] * 6).astype(F32)
    pend4 = pend.reshape(bsz, n_chunks, 1, WIDTH)
    spec = pl.BlockSpec((1, chunk, WIDTH), lambda b, c: (b, c, 0))
    return pl.pallas_call(
        functools.partial(_rwkv_scan_kernel, chunk=chunk, prec=prec),
        grid=(bsz, n_chunks),
        in_specs=[spec] * 7 + [
            pl.BlockSpec((1, 1, 1, WIDTH), lambda b, c: (b, c, 0, 0)),
            pl.BlockSpec((8, WIDTH), lambda b, c: (0, 0)),
        ],
        out_specs=spec,
        out_shape=jax.ShapeDtypeStruct((bsz, seq, WIDTH), F32),
        scratch_shapes=[pltpu.VMEM((PAIRS, LANES, LANES), F32)],
        compiler_params=_cparams(("parallel", "arbitrary")),
        name="rwkv_scan",
    )(rt, kt, kd, bd, v, g, bonus, pend4, ln)


def _merge_kernel(x_ref, oa_ref, ob_ref, ga_ref, gb_ref, wa_ref, wb_ref, wo_ref, g2_ref,
                  h_ref, xn_ref, acc_ref):
    j = pl.program_id(1)

    @pl.when(j == 0)
    def _():
        acc_ref[...] = x_ref[...]

    ya = jnp.dot(oa_ref[...].astype(BF16), wa_ref[...], preferred_element_type=F32)
    yb = jnp.dot(ob_ref[...].astype(BF16), wb_ref[...], preferred_element_type=F32)
    y = jax.nn.sigmoid(ga_ref[...]) * ya + jax.nn.sigmoid(gb_ref[...]) * yb
    acc_ref[...] += jnp.dot(y.astype(BF16), wo_ref[...], preferred_element_type=F32)

    @pl.when(j == pl.num_programs(1) - 1)
    def _():
        h = acc_ref[...]
        h_ref[...] = h
        ms = jnp.mean(h * h, axis=-1, keepdims=True)
        xn_ref[...] = h * lax.rsqrt(ms + RMS_EPS) * g2_ref[...]


def merge_out(x2d, oa, ob, p2d, w_proj_a, w_proj_b, w_out, norm2_g, *, tm=512):
    t, d = x2d.shape
    tn = WIDTH
    nj = d // tn
    g0 = COL_G_OFF // tn
    big = jax.ShapeDtypeStruct((t, d), F32)
    return pl.pallas_call(
        _merge_kernel,
        grid=(t // tm, nj),
        in_specs=[
            pl.BlockSpec((tm, d), lambda i, j: (i, 0)),
            pl.BlockSpec((tm, WIDTH), lambda i, j: (i, 0)),
            pl.BlockSpec((tm, WIDTH), lambda i, j: (i, 0)),
            pl.BlockSpec((tm, tn), lambda i, j: (i, g0 + j)),
            pl.BlockSpec((tm, tn), lambda i, j: (i, g0 + nj + j)),
            pl.BlockSpec((WIDTH, tn), lambda i, j: (0, j)),
            pl.BlockSpec((WIDTH, tn), lambda i, j: (0, j)),
            pl.BlockSpec((tn, d), lambda i, j: (j, 0)),
            pl.BlockSpec((1, d), lambda i, j: (0, 0)),
        ],
        out_specs=[pl.BlockSpec((tm, d), lambda i, j: (i, 0))] * 2,
        out_shape=[big, big],
        scratch_shapes=[pltpu.VMEM((tm, d), F32)],
        compiler_params=_cparams(("parallel", "arbitrary")),
        name="merge_out",
    )(x2d, oa, ob, p2d, p2d, w_proj_a.astype(BF16), w_proj_b.astype(BF16), w_out.astype(BF16),
      norm2_g.reshape(1, d))


PEER_HEADS = 8
PEER_NKEYS = 128
PEER_TOPK = 16
PEER_HALF = 128


def _topk_rows(s, k):
    n = s.shape[0]
    rows = lax.broadcasted_iota(jnp.int32, s.shape, 0).astype(F32)
    vals, ids = [], []
    for _ in range(k):
        m = jnp.max(s, axis=0, keepdims=True)
        first = jnp.min(jnp.where(s == m, rows, float(n)), axis=0, keepdims=True)
        vals.append(m)
        ids.append(first)
        s = jnp.where(rows == first, -jnp.inf, s)
    return jnp.concatenate(vals, axis=0), jnp.concatenate(ids, axis=0)


def _take_rows(table, ids):
    rows = lax.broadcasted_iota(jnp.int32, table.shape, 0).astype(F32)
    return jnp.sum(jnp.where(rows == ids, table, 0.0), axis=0, keepdims=True)


def _peer_route_kernel(xn_ref, wq_ref, sk_ref, idx_ref, gate_ref, *, prec):
    tt = xn_ref.shape[0]
    k = PEER_TOPK
    q = jnp.dot(xn_ref[...].astype(wq_ref.dtype), wq_ref[...], precision=prec, preferred_element_type=F32)
    nt = (((1,), (1,)), ((), ()))
    idx_rows, gate_rows = [], []
    half = k // 2
    for h in range(PEER_HEADS):
        tops = []
        for p in range(2):
            c0 = (h * 2 + p) * PEER_HALF
            s = lax.dot_general(sk_ref[h, p].astype(wq_ref.dtype), q[:, c0:c0 + PEER_HALF].astype(wq_ref.dtype),
                                nt, precision=prec, preferred_element_type=F32)
            tops.append(_topk_rows(s, k))
        (s0, i0), (s1, i1) = tops
        cs = [s0[0:1] + s1] + [s0[i:i + 1] + s1[0:half] for i in range(1, half)] + [s0[half:] + s1[0:1]]
        best_s, pos = _topk_rows(jnp.concatenate(cs, axis=0), k)
        mid = jnp.floor((pos - k) * (1.0 / half))
        end_mid = float(k + (half - 1) * half)
        i_rank = jnp.where(pos < k, 0.0, jnp.where(pos < end_mid, 1.0 + mid, pos - (end_mid - half)))
        j_rank = jnp.where(pos < k, pos, jnp.where(pos < end_mid, (pos - k) - half * mid, 0.0))
        ids = [_take_rows(i0, i_rank[n:n + 1]) * PEER_NKEYS + _take_rows(i1, j_rank[n:n + 1]) for n in range(k)]
        e = jnp.exp(best_s - best_s[0:1])
        gate_rows.append(e / jnp.sum(e, axis=0, keepdims=True))
        idx_rows.append(jnp.concatenate(ids, axis=0).astype(jnp.int32))
    idx_ref[...] = jnp.concatenate(idx_rows, axis=0).T
    gate_ref[...] = jnp.concatenate(gate_rows, axis=0).T


def peer_route(xn2d, peer_wq, peer_subkeys, *, tt=256, prec=None, wdtype=BF16):
    t, d = xn2d.shape
    nq = peer_wq.shape[1]
    n_sel = PEER_HEADS * PEER_TOPK
    return pl.pallas_call(
        functools.partial(_peer_route_kernel, prec=prec),
        grid=(t // tt,),
        in_specs=[
            pl.BlockSpec((tt, d), lambda i: (i, 0)),
            pl.BlockSpec((d, nq), lambda i: (0, 0)),
            pl.BlockSpec((PEER_HEADS, 2, PEER_NKEYS, PEER_HALF), lambda i: (0, 0, 0, 0)),
        ],
        out_specs=[pl.BlockSpec((tt, n_sel), lambda i: (i, 0))] * 2,
        out_shape=[jax.ShapeDtypeStruct((t, n_sel), jnp.int32), jax.ShapeDtypeStruct((t, n_sel), F32)],
        compiler_params=_cparams(("parallel",)),
        name="peer_route",
    )(xn2d, peer_wq.astype(wdtype), peer_subkeys)


def _final_kernel(h_ref, y_ref, g_ref, o_ref):
    h = h_ref[...] + y_ref[...]
    ms = jnp.mean(h * h, axis=-1, keepdims=True)
    o_ref[...] = h * lax.rsqrt(ms + RMS_EPS) * g_ref[...]


def final_norm(h2d, y2d, g, *, tm=1024):
    t, d = h2d.shape
    spec = pl.BlockSpec((tm, d), lambda i: (i, 0))
    return pl.pallas_call(
        _final_kernel,
        grid=(t // tm,),
        in_specs=[spec, spec, pl.BlockSpec((1, d), lambda i: (0, 0))],
        out_specs=spec,
        out_shape=jax.ShapeDtypeStruct((t, d), F32),
        compiler_params=_cparams(("parallel",)),
        name="final_norm",
    )(h2d, y2d, g.reshape(1, d))


SC_CORES = 2
SC_SUBCORES = 16
SC_LANES = 16
SC_WORKERS = SC_CORES * SC_SUBCORES
PEER_SEL = PEER_HEADS * PEER_TOPK
PEER_ROWS = 32
PEER_PARTS = PEER_SEL // PEER_ROWS
PEER_NBUF = 4
PEER_GROUP = 32
PEER_BF16_RUN = 4


def _pack_rows(w):
    half = w.shape[1] // 2
    bits = lax.bitcast_convert_type(w.astype(BF16), jnp.uint16).astype(jnp.uint32)
    return lax.bitcast_convert_type(bits[:, :half] | (bits[:, half:] << 16), jnp.int32)


def _unpack_words(w):
    lo = lax.bitcast_convert_type(lax.shift_left(w, jnp.int32(16)), F32)
    hi = lax.bitcast_convert_type(lax.bitwise_and(w, jnp.int32(-65536)), F32)
    return lo, hi


def _packed_dot(a_words, b_words):
    from jax.experimental.pallas import tpu_sc as plsc
    prods = [plsc.bitcast(a, BF16) * plsc.bitcast(b, BF16) for a, b in zip(a_words, b_words)]
    while len(prods) > 1:
        prods = [prods[k] + prods[k + 1] for k in range(0, len(prods), 2)]
    return _unpack_words(plsc.bitcast(prods[0], jnp.int32))


def _sc_mesh():
    from jax.experimental.pallas import tpu_sc as plsc
    return plsc.VectorSubcoreMesh(core_axis_name="c", subcore_axis_name="s",
                                  num_cores=SC_CORES, num_subcores=SC_SUBCORES)


def _sc_loop(n, body, carry):
    from jax.experimental.pallas import tpu_sc as plsc
    return plsc.parallel_loop(0, n, carry=carry)(body)


def _worker_base(tokens_per_worker):
    return (lax.axis_index("s") * SC_CORES + lax.axis_index("c")) * tokens_per_worker


def _gather_compute_loop(table_hbm, idx_v, rows_v, sem, stage_v, out_row, osem, grp, compute):
    n_gathers = PEER_PARTS * grp
    ahead = PEER_NBUF - 1

    def gather(j, b):
        return pltpu.make_async_copy(table_hbm.at[idx_v.at[j]], rows_v.at[b], sem.at[b])

    def put(i, slot):
        return pltpu.make_async_copy(stage_v.at[slot], out_row(i), osem.at[slot])

    for j in range(ahead):
        gather(j, j).start()

    @pl.loop(0, n_gathers)
    def _(j):
        b = lax.bitwise_and(j, PEER_NBUF - 1)
        h = lax.bitwise_and(j, PEER_PARTS - 1)
        i = lax.shift_right_logical(j, PEER_PARTS.bit_length() - 1)
        slot = lax.bitwise_and(i, 1)

        @pl.when((h == 0) & (i >= 2))
        def _():
            put(i - 2, slot).wait()

        @pl.when(j + ahead < n_gathers)
        def _():
            gather(j + ahead, lax.bitwise_and(j + ahead, PEER_NBUF - 1)).start()

        gather(j, b).wait()
        compute(i, h, b, slot)

        @pl.when(h == PEER_PARTS - 1)
        def _():
            put(i, slot).start()

    put(grp - 2, 0).wait()
    put(grp - 1, 1).wait()


def peer_expert_dots(x_packed, idx2, u_packed):
    t, half = x_packed.shape
    n_chunks = half // SC_LANES
    tpw = t // SC_WORKERS
    grp = min(PEER_GROUP, tpw)
    rows_tog = 4

    def body(x_hbm, idx_hbm, u_hbm, out_hbm, idx_v, x_v, rows_v, ps_v, sem, osem):
        base = _worker_base(tpw)

        def compute(i, h, b, slot):
            @pl.loop(0, PEER_ROWS // rows_tog)
            def _(rg):
                r0 = rg * rows_tog
                accs = [[None, None] for _ in range(rows_tog)]
                for c0 in range(0, n_chunks, PEER_BF16_RUN):
                    ats = [pl.ds((c0 + k) * SC_LANES, SC_LANES) for k in range(PEER_BF16_RUN)]
                    xw = [x_v[i, at] for at in ats]
                    for r in range(rows_tog):
                        terms = _packed_dot([rows_v[b, r0 + r, at] for at in ats], xw)
                        for k, term in enumerate(terms):
                            accs[r][k] = term if accs[r][k] is None else accs[r][k] + term
                for r in range(rows_tog):
                    at = pl.ds(pl.multiple_of((h * PEER_ROWS + r0 + r) * SC_LANES, SC_LANES), SC_LANES)
                    ps_v[slot, at] = accs[r][0] + accs[r][1]

        @pl.loop(0, tpw // grp)
        def _(g):
            t0 = base + g * grp
            pltpu.sync_copy(idx_hbm.at[pl.ds(PEER_PARTS * t0, PEER_PARTS * grp)], idx_v)
            pltpu.sync_copy(x_hbm.at[pl.ds(t0, grp)], x_v)
            _gather_compute_loop(u_hbm, idx_v, rows_v, sem, ps_v, lambda i: out_hbm.at[t0 + i], osem, grp, compute)

    return pl.kernel(
        body,
        out_type=jax.ShapeDtypeStruct((t, PEER_SEL * SC_LANES), F32),
        mesh=_sc_mesh(),
        scratch_types=[
            pltpu.VMEM((PEER_PARTS * grp, PEER_ROWS), jnp.int32),
            pltpu.VMEM((grp, half), jnp.int32),
            pltpu.VMEM((PEER_NBUF, PEER_ROWS, half), jnp.int32),
            pltpu.VMEM((2, PEER_SEL * SC_LANES), F32),
            pltpu.SemaphoreType.DMA((PEER_NBUF,)),
            pltpu.SemaphoreType.DMA((2,)),
        ],
        compiler_params=pltpu.CompilerParams(needs_layout_passes=False),
        name="peer_expert_dots",
    )(x_packed, idx2, u_packed)


def peer_expert_mix(hgx, idx2, v_packed):
    t = hgx.shape[0]
    half = v_packed.shape[1]
    d = 2 * half
    tpw = t // SC_WORKERS
    grp = min(PEER_GROUP // 2, tpw)
    n_parts = 2
    cpp = half // SC_LANES // n_parts

    def body(hg_hbm, idx_hbm, v_hbm, out_hbm, idx_v, hg_v, rows_v, o_v2, sem, osem):
        base = _worker_base(tpw)

        def compute(i, h, b, slot):
            for part in range(n_parts):
                def rbody(rq, accs):
                    r0 = rq * PEER_BF16_RUN
                    s = [hg_v[i, pl.ds(pl.multiple_of((h * PEER_ROWS + r0 + k) * SC_LANES, SC_LANES), SC_LANES)]
                         for k in range(PEER_BF16_RUN)]
                    new = []
                    for c in range(cpp):
                        at = pl.ds((part * cpp + c) * SC_LANES, SC_LANES)
                        lo, hi = _packed_dot([rows_v[b, r0 + k, at] for k in range(PEER_BF16_RUN)], s)
                        new.append(accs[2 * c] + lo)
                        new.append(accs[2 * c + 1] + hi)
                    return tuple(new)

                accs = _sc_loop(PEER_ROWS // PEER_BF16_RUN, rbody,
                                tuple(jnp.zeros((SC_LANES,), F32) for _ in range(2 * cpp)))
                def store(overwrite):
                    for c in range(cpp):
                        lo_at = pl.ds((part * cpp + c) * SC_LANES, SC_LANES)
                        hi_at = pl.ds(half + (part * cpp + c) * SC_LANES, SC_LANES)
                        if overwrite:
                            o_v2[slot, lo_at] = accs[2 * c]
                            o_v2[slot, hi_at] = accs[2 * c + 1]
                        else:
                            o_v2[slot, lo_at] = o_v2[slot, lo_at] + accs[2 * c]
                            o_v2[slot, hi_at] = o_v2[slot, hi_at] + accs[2 * c + 1]

                pl.when(h == 0)(functools.partial(store, True))
                pl.when(h != 0)(functools.partial(store, False))

        @pl.loop(0, tpw // grp)
        def _(g):
            t0 = base + g * grp
            pltpu.sync_copy(idx_hbm.at[pl.ds(PEER_PARTS * t0, PEER_PARTS * grp)], idx_v)
            pltpu.sync_copy(hg_hbm.at[pl.ds(t0, grp)], hg_v)
            _gather_compute_loop(v_hbm, idx_v, rows_v, sem, o_v2, lambda i: out_hbm.at[t0 + i], osem, grp, compute)

    return pl.kernel(
        body,
        out_type=jax.ShapeDtypeStruct((t, d), F32),
        mesh=_sc_mesh(),
        scratch_types=[
            pltpu.VMEM((PEER_PARTS * grp, PEER_ROWS), jnp.int32),
            pltpu.VMEM((grp, PEER_SEL * SC_LANES), jnp.int32),
            pltpu.VMEM((PEER_NBUF, PEER_ROWS, half), jnp.int32),
            pltpu.VMEM((2, d), F32),
            pltpu.SemaphoreType.DMA((PEER_NBUF,)),
            pltpu.SemaphoreType.DMA((2,)),
        ],
        compiler_params=pltpu.CompilerParams(needs_layout_passes=False),
        name="peer_expert_mix",
    )(hgx, idx2, v_packed)


def _peer_act_kernel(ps_ref, gate_ref, sum_ref, o_ref):
    ps = ps_ref[...]
    sel = sum_ref[...]
    hi = ps.astype(BF16)
    rest = ps - hi.astype(F32)
    mid = rest.astype(BF16)
    lo = (rest - mid.astype(F32)).astype(BF16)
    pre = (jnp.dot(hi, sel, preferred_element_type=F32) + jnp.dot(mid, sel, preferred_element_type=F32)
           + jnp.dot(lo, sel, preferred_element_type=F32))
    hg = 0.5 * pre * (1.0 + lax.erf(pre * (1.0 / math.sqrt(2.0)))) * gate_ref[...]
    spread = (((1,), (1,)), ((), ()))
    hgx = lax.dot_general(hg.astype(BF16), sel, spread, preferred_element_type=F32)
    bits = lax.bitcast_convert_type(hgx, jnp.int32)
    o_ref[...] = lax.bitwise_or(bits, lax.shift_right_logical(bits, jnp.int32(16)))


def peer_act(ps, gates, *, tm=512):
    t, n = ps.shape
    lane_sum = (jnp.arange(n)[:, None] // SC_LANES == jnp.arange(PEER_SEL)[None, :]).astype(BF16)
    return pl.pallas_call(
        _peer_act_kernel,
        grid=(t // tm,),
        in_specs=[
            pl.BlockSpec((tm, n), lambda i: (i, 0)),
            pl.BlockSpec((tm, PEER_SEL), lambda i: (i, 0)),
            pl.BlockSpec((n, PEER_SEL), lambda i: (0, 0)),
        ],
        out_specs=pl.BlockSpec((tm, n), lambda i: (i, 0)),
        out_shape=jax.ShapeDtypeStruct((t, n), jnp.int32),
        compiler_params=_cparams(("parallel",)),
        name="peer_act",
    )(ps, gates, lane_sum)


BATCH_GROUPS = 8


def kernel(x, norm1_g, w_in, rwkv_mu, w0, w_lora_up, a0, a_lora_up, g_lora_up, k_k, k_a, r_k, lnx_g, lnx_b,
           w_proj_a, w_proj_b, w_out, norm2_g, peer_wq, peer_subkeys, peer_u, peer_v, rel_bias, normf_g):
    bsz, seq, d = x.shape
    depth = norm1_g.shape[0]
    groups = BATCH_GROUPS if bsz % BATCH_GROUPS == 0 else 1
    gb = bsz // groups
    tg = gb * seq
    hs = [x[g * gb:(g + 1) * gb].reshape(tg, d) for g in range(groups)]
    for l in range(depth):
        w_pad = jnp.concatenate([
            w_in[l][:, :COL_A + COL_B_RAW],
            jnp.zeros((d, COL_B - COL_B_RAW), w_in.dtype),
            w_in[l][:, COL_A + COL_B_RAW:]], axis=1).astype(BF16)
        u_packed = _pack_rows(peer_u[l])
        v_packed = _pack_rows(peer_v[l])
        last = l == depth - 1

        def mix(pending, tie=None):
            g, h2d, ps, gates, idx2 = pending
            hgx = peer_act(ps, gates)
            if tie is not None:
                tie, hgx = lax.optimization_barrier((tie, hgx))
            return tie, (g, h2d, peer_expert_mix(hgx, idx2, v_packed))

        def close(mixed, tie=None):
            g, h2d, y2d = mixed
            out = final_norm(h2d, y2d, normf_g) if last else h2d + y2d
            if tie is not None:
                tie, out = lax.optimization_barrier((tie, out))
            hs[g] = out
            return tie

        pending = None
        for g in range(groups):
            p2d = norm_proj(hs[g], norm1_g[l], w_pad)
            p3d = p2d.reshape(gb, seq, -1)
            oa = moba_attention(p3d, rel_bias)
            mixed = None
            if pending is not None:
                oa, mixed = mix(pending, oa)
            prep = rwkv_prep(p3d, rwkv_mu[l], w0[l], w_lora_up[l], a0[l], a_lora_up[l], g_lora_up[l],
                             k_k[l], k_a[l], r_k[l])
            ob = rwkv_scan(*prep, lnx_g[l], lnx_b[l])
            h2d, xn2 = merge_out(hs[g], oa.reshape(tg, WIDTH), ob.reshape(tg, WIDTH), p2d,
                                 w_proj_a[l], w_proj_b[l], w_out[l], norm2_g[l])
            idx, gates = peer_route(xn2, peer_wq[l], peer_subkeys[l])
            if mixed is not None:
                idx = close(mixed, idx)
            idx2 = idx.reshape(-1, PEER_ROWS)
            pending = (g, h2d, peer_expert_dots(_pack_rows(xn2), idx2, u_packed), gates, idx2)
        close(mix(pending)[1])
    return jnp.concatenate(hs, axis=0).reshape(bsz, seq, d)
```

```python
import functools
import math

import jax
import jax.numpy as jnp
from jax import lax
from jax.experimental import pallas as pl
from jax.experimental.pallas import tpu as pltpu

F32 = jnp.float32
BF16 = jnp.bfloat16
HI = lax.Precision.HIGHEST

LANES = 128
HEAD_DIM = 64
HEADS = 8
PAIRS = HEADS // 2
WIDTH = HEADS * HEAD_DIM
MOBA_BLOCK = 256
MOBA_TOPK = 3
MOBA_LO = 64
REL_BUCKETS = 32
REL_MAX_DIST = 128
DECAY_LORA = 64
AAA_LORA = 64
GATE_LORA = 160
GN_EPS = 64e-5
RMS_EPS = 1e-6
NEG = -1e30
RWKV_CHUNK = 64
COL_A = 3 * WIDTH
COL_B_RAW = 3 * WIDTH + DECAY_LORA + AAA_LORA + GATE_LORA
COL_B = 4 * WIDTH
COL_G_OFF = COL_A + COL_B
VMEM_LIMIT = 56 * 1024 * 1024


def _cparams(sem):
    return pltpu.CompilerParams(dimension_semantics=sem, vmem_limit_bytes=VMEM_LIMIT)


def _norm_proj_kernel(x_ref, g_ref, w_ref, o_ref, xn_ref):
    @pl.when(pl.program_id(1) == 0)
    def _():
        x = x_ref[...]
        ms = jnp.mean(x * x, axis=-1, keepdims=True)
        xn_ref[...] = (x * lax.rsqrt(ms + RMS_EPS) * g_ref[...]).astype(xn_ref.dtype)

    o_ref[...] = jnp.dot(xn_ref[...], w_ref[...], preferred_element_type=F32).astype(o_ref.dtype)


def norm_proj(x2d, g, w, *, tm=512, tn=512, out_dtype=F32):
    t, d = x2d.shape
    n = w.shape[1]
    return pl.pallas_call(
        _norm_proj_kernel,
        grid=(t // tm, n // tn),
        in_specs=[
            pl.BlockSpec((tm, d), lambda i, j: (i, 0)),
            pl.BlockSpec((1, d), lambda i, j: (0, 0)),
            pl.BlockSpec((d, tn), lambda i, j: (0, j)),
        ],
        out_specs=pl.BlockSpec((tm, tn), lambda i, j: (i, j)),
        out_shape=jax.ShapeDtypeStruct((t, n), out_dtype),
        scratch_shapes=[pltpu.VMEM((tm, d), w.dtype)],
        compiler_params=_cparams(("parallel", "arbitrary")),
        name="norm_proj",
    )(x2d, g.reshape(1, d), w)


def _rel_bucket(dist):
    n = jnp.maximum(dist, 0)
    max_exact = REL_BUCKETS // 2
    nf = jnp.maximum(n, 1).astype(F32)
    large = max_exact + (jnp.log(nf / max_exact) / math.log(REL_MAX_DIST / max_exact)
                         * (REL_BUCKETS - max_exact)).astype(jnp.int32)
    large = jnp.minimum(large, REL_BUCKETS - 1)
    return jnp.where(n < max_exact, n, large)


def _moba_kernel(q_ref, k_ref, v_ref, bown_ref, bprev_ref, bfar_ref, o_ref,
                 kb_ref, vb_ref, kbar_ref, *, n_blocks):
    qb = pl.program_id(2)
    blk = MOBA_BLOCK
    scale = 1.0 / math.sqrt(HEAD_DIM)

    rows2 = 2 * blk
    nt = (((1,), (1,)), ((), ()))

    @pl.when(qb == 0)
    def _():
        kbar_ref[...] = jnp.zeros_like(kbar_ref)
        lane_b = lax.broadcasted_iota(jnp.int32, (blk, LANES), 1)
        for n in range(n_blocks):
            kblk = k_ref[0, n * blk:(n + 1) * blk, :]
            kbar_ref[n:n + 1, :] = jnp.mean(kblk, axis=0, keepdims=True)
            kb_ref[n * blk:(n + 1) * blk, 0:LANES] = kblk.astype(BF16)
            kb_ref[n * blk:(n + 1) * blk, LANES:] = ((lane_b == n) | (lane_b == MOBA_LO + n)).astype(BF16)
        vb_ref[...] = v_ref[0].astype(BF16)

    q2 = q_ref[0]
    first = lax.broadcasted_iota(jnp.int32, (blk, LANES), 1) < HEAD_DIM
    qh = jnp.concatenate([jnp.where(first, q2, 0.0), jnp.where(first, 0.0, q2)], axis=0)
    lane = lax.broadcasted_iota(jnp.int32, (rows2, LANES), 1)
    rowi = lax.broadcasted_iota(jnp.int32, (rows2, LANES), 0)
    gate = lax.dot_general(qh.astype(BF16), kbar_ref[...].astype(BF16), nt, preferred_element_type=F32)
    g = jnp.where(lane < qb, gate, -jnp.inf)
    chosen = lane < 0
    lane_f = lane.astype(F32)
    for _ in range(MOBA_TOPK):
        m = jnp.max(g, axis=1, keepdims=True)
        idx = jnp.min(jnp.where(g == m, lane_f, float(LANES)), axis=1, keepdims=True)
        hit = (lane_f == idx) & (m > -jnp.inf)
        chosen = chosen | hit
        g = jnp.where(hit, -jnp.inf, g)
    nfar = qb - 1
    bfar = jnp.where(rowi < blk, bfar_ref[0, 0:1, 0:1], bfar_ref[1, 0:1, 0:1])
    bhi = bfar.astype(BF16).astype(F32)
    madd = jnp.where(lane < nfar, jnp.where(chosen, bhi, NEG),
                     jnp.where(lane == nfar, jnp.where(chosen, 0.0, NEG),
                               jnp.where((lane >= MOBA_LO) & (lane - MOBA_LO < nfar), bfar - bhi, 0.0)))
    q_aug = jnp.concatenate([(qh * scale).astype(BF16), madd.astype(BF16)], axis=1)

    prev0 = pl.multiple_of(jnp.maximum(nfar, 0) * blk, blk)
    own0 = pl.multiple_of(qb * blk, blk)
    s_prev = (lax.dot_general(q_aug, kb_ref[pl.ds(prev0, blk), :], nt, preferred_element_type=F32)
              + bprev_ref[...].reshape(rows2, blk) + jnp.where(qb > 0, 0.0, NEG))
    s_own = (lax.dot_general(q_aug, kb_ref[pl.ds(own0, blk), :], nt, preferred_element_type=F32)
             + bown_ref[...].reshape(rows2, blk))
    r = lax.broadcasted_iota(jnp.int32, (rows2, blk), 0)
    c = lax.broadcasted_iota(jnp.int32, (rows2, blk), 1)
    s_own = jnp.where(lax.bitwise_and(r, blk - 1) >= c, s_own, NEG)
    s = jnp.concatenate([s_prev, s_own], axis=1)
    m_i = jnp.max(s, axis=1, keepdims=True)
    p = jnp.exp(s - m_i)
    l_i = jnp.sum(p, axis=1, keepdims=True)
    v0 = jnp.concatenate([vb_ref[pl.ds(prev0, blk), :], vb_ref[pl.ds(own0, blk), :]], axis=0)
    acc = jnp.dot(p.astype(BF16), v0, preferred_element_type=F32)

    def body(it, carry):
        m_i, l_i, acc = carry
        k0 = pl.multiple_of(it * rows2, rows2)
        s = lax.dot_general(q_aug, kb_ref[pl.ds(k0, rows2), :], nt, preferred_element_type=F32)
        tail = jnp.where(2 * it + 1 < nfar, 0.0, NEG)
        s = jnp.concatenate([s[:, :blk], s[:, blk:] + tail], axis=1)
        m_new = jnp.maximum(m_i, jnp.max(s, axis=1, keepdims=True))
        alpha = jnp.exp(m_i - m_new)
        p = jnp.exp(s - m_new)
        l_new = alpha * l_i + jnp.sum(p, axis=1, keepdims=True)
        acc_new = alpha * acc + jnp.dot(p.astype(BF16), vb_ref[pl.ds(k0, rows2), :], preferred_element_type=F32)
        return m_new, l_new, acc_new

    m_i, l_i, acc = lax.fori_loop(0, (jnp.maximum(nfar, 0) + 1) // 2, body, (m_i, l_i, acc))
    out = acc / l_i
    o_ref[0] = jnp.where(first, out[:blk], out[blk:])


def moba_attention(p3d, rel_bias):
    bsz, seq, _ = p3d.shape
    blk = MOBA_BLOCK
    n_blocks = seq // blk
    span = 2 * blk
    by_dist = rel_bias[:, _rel_bucket(jnp.arange(span))].astype(F32)
    shift = jnp.arange(span)

    def toeplitz(c):
        k = jnp.where(shift < blk, shift, shift - span)
        s = by_dist[:, jnp.clip(c - k, 0, span - 1)]
        tiled = jnp.tile(s, (1, blk))[:, :blk * (span - 1)]
        return tiled.reshape(HEADS, blk, span - 1)[:, :, :blk]

    bias_own = toeplitz(0)
    bias_prev = toeplitz(blk)
    bias_far = jnp.broadcast_to(rel_bias[:, REL_BUCKETS - 1].astype(F32)[:, None, None], (HEADS, 8, LANES))
    kern = functools.partial(_moba_kernel, n_blocks=n_blocks)
    return pl.pallas_call(
        kern,
        grid=(bsz, PAIRS, n_blocks),
        in_specs=[
            pl.BlockSpec((1, blk, LANES), lambda b, h, i: (b, i, h)),
            pl.BlockSpec((1, seq, LANES), lambda b, h, i: (b, 0, PAIRS + h)),
            pl.BlockSpec((1, seq, LANES), lambda b, h, i: (b, 0, 2 * PAIRS + h)),
            pl.BlockSpec((2, blk, blk), lambda b, h, i: (h, 0, 0)),
            pl.BlockSpec((2, blk, blk), lambda b, h, i: (h, 0, 0)),
            pl.BlockSpec((2, 8, LANES), lambda b, h, i: (h, 0, 0)),
        ],
        out_specs=pl.BlockSpec((1, blk, LANES), lambda b, h, i: (b, i, h)),
        out_shape=jax.ShapeDtypeStruct((bsz, seq, WIDTH), F32),
        scratch_shapes=[
            pltpu.VMEM((seq, 2 * LANES), BF16),
            pltpu.VMEM((seq, LANES), BF16),
            pltpu.VMEM((LANES, LANES), F32),
        ],
        compiler_params=_cparams(("parallel", "parallel", "arbitrary")),
        name="moba",
    )(p3d, p3d, p3d, bias_own, bias_prev, bias_far)


def _shifted(x, carry_row):
    rows = lax.broadcasted_iota(jnp.int32, x.shape, 0)
    return jnp.where(rows == 0, carry_row, pltpu.roll(x, 1, axis=0))


def _rwkv_prep_kernel(pr_ref, pk_ref, pv_ref, pl_ref, mu_ref, vec_ref, ww_ref, wa_ref, wg_ref,
                      bd_ref, tri_ref,
                      rt_ref, kt_ref, kd_ref, bd_out_ref, v_ref, g_ref, bonus_ref, pend_ref,
                      carry_ref, *, chunk):
    @pl.when(pl.program_id(1) == 0)
    def _():
        carry_ref[...] = jnp.zeros_like(carry_ref)

    def mix(ref, j):
        x = ref[0]
        mu = mu_ref[0:1, j * WIDTH:(j + 1) * WIDTH]
        prev = _shifted(x, carry_ref[0:1, j * WIDTH:(j + 1) * WIDTH])
        carry_ref[0:1, j * WIDTH:(j + 1) * WIDTH] = x[x.shape[0] - 1:, :]
        return x + mu * (prev - x)

    r = mix(pr_ref, 0)
    k = mix(pk_ref, 1)
    v = mix(pv_ref, 2)
    lo = mix(pl_ref, 3)
    w0, a0, k_k, k_a, r_k = (vec_ref[i:i + 1, :] for i in range(5))
    xwa = lo[:, 0:LANES]
    xg = lo[:, LANES:3 * LANES]
    lw = jnp.dot(jnp.tanh(xwa), ww_ref[...], precision=HI, preferred_element_type=F32)
    la = jnp.dot(xwa, wa_ref[...], precision=HI, preferred_element_type=F32)
    g = jnp.dot(jax.nn.sigmoid(xg), wg_ref[...], precision=HI, preferred_element_type=F32)
    z = -(w0 + lw)
    softplus = jnp.maximum(z, 0.0) + jnp.log(1.0 + jnp.exp(-jnp.abs(z)))
    logw = -jnp.exp(-softplus - 0.5)
    a = jax.nn.sigmoid(a0 + la)
    kk = k * k_k
    ss = jnp.dot(kk * kk, bd_ref[...], precision=HI, preferred_element_type=F32)
    kk = kk / jnp.maximum(jnp.sqrt(ss), 1e-12)
    k2 = k * (1.0 + (a - 1.0) * k_a)
    rk = jnp.dot(r * k2 * r_k, bd_ref[...], precision=HI, preferred_element_type=F32)
    cs = jnp.dot(tri_ref[...], logw, precision=HI, preferred_element_type=F32)
    e_pos = jnp.exp(cs)
    e_neg = jnp.exp(-cs)
    rt_ref[0] = r * e_pos
    kt_ref[0] = kk * jnp.exp(cs - logw)
    kd_ref[0] = k2 * e_neg
    bd_out_ref[0] = kk * a * e_neg
    v_ref[0] = v
    g_ref[0] = g
    bonus_ref[0] = rk * v
    ts = e_pos.shape[0]
    for c in range(ts // chunk):
        pend_ref[0, c:c + 1, :] = e_pos[(c + 1) * chunk - 1:(c + 1) * chunk, :]


def rwkv_prep(p3d, rwkv_mu, w0, w_lora_up, a0, a_lora_up, g_lora_up, k_k, k_a, r_k, *, ts=512):
    bsz, seq, _ = p3d.shape
    chunk = RWKV_CHUNK
    ts = min(ts, seq)
    mu = jnp.pad(rwkv_mu, (0, COL_B - COL_B_RAW)).reshape(1, COL_B)
    vec = jnp.stack([w0, a0, k_k, k_a, r_k.reshape(-1)] + [jnp.zeros_like(w0)] * 3).astype(F32)
    ww = jnp.zeros((LANES, WIDTH), F32).at[:DECAY_LORA].set(w_lora_up)
    wa = jnp.zeros((LANES, WIDTH), F32).at[DECAY_LORA:DECAY_LORA + AAA_LORA].set(a_lora_up)
    wg = jnp.zeros((2 * LANES, WIDTH), F32).at[:GATE_LORA].set(g_lora_up)
    hid = jnp.arange(WIDTH) // HEAD_DIM
    bd = (hid[:, None] == hid[None, :]).astype(F32)
    tix = jnp.arange(ts)
    tri = ((tix[:, None] // chunk == tix[None, :] // chunk) & (tix[None, :] <= tix[:, None])).astype(F32)
    c0 = COL_A // WIDTH
    big = jax.ShapeDtypeStruct((bsz, seq, WIDTH), F32)
    wspec = lambda shape: pl.BlockSpec(shape, lambda b, i: (0, 0))
    ospec = pl.BlockSpec((1, ts, WIDTH), lambda b, i: (b, i, 0))
    return pl.pallas_call(
        functools.partial(_rwkv_prep_kernel, chunk=chunk),
        grid=(bsz, seq // ts),
        in_specs=[
            pl.BlockSpec((1, ts, WIDTH), lambda b, i: (b, i, c0)),
            pl.BlockSpec((1, ts, WIDTH), lambda b, i: (b, i, c0 + 1)),
            pl.BlockSpec((1, ts, WIDTH), lambda b, i: (b, i, c0 + 2)),
            pl.BlockSpec((1, ts, WIDTH), lambda b, i: (b, i, c0 + 3)),
            wspec((1, COL_B)), wspec((8, WIDTH)), wspec((LANES, WIDTH)), wspec((LANES, WIDTH)),
            wspec((2 * LANES, WIDTH)), wspec((WIDTH, WIDTH)), wspec((ts, ts)),
        ],
        out_specs=[ospec] * 7 + [pl.BlockSpec((1, ts // chunk, WIDTH), lambda b, i: (b, i, 0))],
        out_shape=[big] * 7 + [jax.ShapeDtypeStruct((bsz, seq // chunk, WIDTH), F32)],
        scratch_shapes=[pltpu.VMEM((8, COL_B), F32)],
        compiler_params=_cparams(("parallel", "arbitrary")),
        name="rwkv_prep",
    )(p3d, p3d, p3d, p3d, mu, vec, ww, wa, wg, bd, tri)


def _rwkv_scan_kernel(rt_ref, kt_ref, kd_ref, bd_ref, v_ref, g_ref, bonus_ref, pend_ref, ln_ref, o_ref,
                      state_ref, *, chunk, prec):
    @pl.when(pl.program_id(1) == 0)
    def _():
        state_ref[...] = jnp.zeros_like(state_ref)

    c2 = 2 * chunk
    lane = lax.broadcasted_iota(jnp.int32, (chunk, LANES), 1)
    first = lane < HEAD_DIM
    row = lax.broadcasted_iota(jnp.int32, (c2, c2), 0)
    col = lax.broadcasted_iota(jnp.int32, (c2, c2), 1)
    eye = (row == col).astype(F32)
    hrow = lax.broadcasted_iota(jnp.int32, (LANES, LANES), 0) // HEAD_DIM
    hcol = lax.broadcasted_iota(jnp.int32, (LANES, LANES), 1) // HEAD_DIM
    head_mean = jnp.where(hrow == hcol, 1.0 / HEAD_DIM, 0.0).astype(F32)
    nt = (((1,), (1,)), ((), ()))
    tn = (((0,), (0,)), ((), ()))
    dot = functools.partial(jnp.dot, precision=prec, preferred_element_type=F32)
    dotg = functools.partial(lax.dot_general, precision=prec, preferred_element_type=F32)

    def stack(x):
        return jnp.concatenate([jnp.where(first, x, 0.0), jnp.where(first, 0.0, x)], axis=0)

    pairs = range(PAIRS)
    sls = [slice(hp * LANES, (hp + 1) * LANES) for hp in pairs]
    rs, ks, kds, bs, vs = ([stack(ref[0, :, sl]) for sl in sls] for ref in (rt_ref, kt_ref, kd_ref, bd_ref, v_ref))
    hts = [state_ref[hp] for hp in pairs]
    big = [dotg(jnp.concatenate([ks[hp], rs[hp]], axis=0), jnp.concatenate([bs[hp], kds[hp]], axis=0), nt)
           for hp in pairs]
    a_b = [jnp.where(row > col, big[hp][0:c2, 0:c2], 0.0) for hp in pairs]
    a_k = [jnp.where(row > col, big[hp][0:c2, c2:], 0.0) for hp in pairs]
    a_rb = [jnp.where(row >= col, big[hp][c2:, 0:c2], 0.0) for hp in pairs]
    a_rk = [jnp.where(row >= col, big[hp][c2:, c2:], 0.0) for hp in pairs]
    kh = [dotg(jnp.concatenate([ks[hp], rs[hp]], axis=0), hts[hp], nt) for hp in pairs]
    av = [dot(jnp.concatenate([a_k[hp], a_rk[hp]], axis=0), vs[hp]) for hp in pairs]
    vk = [dotg(vs[hp], kds[hp], tn) for hp in pairs]
    inv = [eye - a_b[hp] for hp in pairs]
    pw = [dot(a_b[hp], a_b[hp]) for hp in pairs]
    n_sq = int(math.log2(chunk)) - 1
    for lvl in range(n_sq):
        if lvl + 1 < n_sq:
            both = [dot(jnp.concatenate([inv[hp], pw[hp]], axis=0), pw[hp]) for hp in pairs]
            inv = [inv[hp] + both[hp][0:c2] for hp in pairs]
            pw = [both[hp][c2:] for hp in pairs]
        else:
            inv = [inv[hp] + dot(inv[hp], pw[hp]) for hp in pairs]
    us = [dot(inv[hp], kh[hp][0:c2] + av[hp][0:c2]) for hp in pairs]
    ub = [dotg(us[hp], bs[hp], tn) for hp in pairs]
    au = [dot(a_rb[hp], us[hp]) for hp in pairs]
    for hp in pairs:
        sl = sls[hp]
        pend = pend_ref[0, 0, 0:1, sl]
        state_ref[hp] = (hts[hp] + vk[hp] - ub[hp]) * pend
        os_ = kh[hp][c2:] + av[hp][c2:] - au[hp]
        o = os_[0:chunk] + os_[chunk:]
        mu = jnp.dot(o, head_mean, precision=HI, preferred_element_type=F32)
        d = o - mu
        var = jnp.dot(d * d, head_mean, precision=HI, preferred_element_type=F32)
        on = d * lax.rsqrt(var + GN_EPS) * ln_ref[0:1, sl] + ln_ref[1:2, sl]
        o_ref[0, :, sl] = (on + bonus_ref[0, :, sl]) * g_ref[0, :, sl]


def rwkv_scan(rt, kt, kd, bd, v, g, bonus, pend, lnx_g, lnx_b, *, prec=None):
    bsz, seq, _ = rt.shape
    chunk = RWKV_CHUNK
    n_chunks = seq // chunk
    ln = jnp.stack([lnx_g, lnx_b] + [jnp.zeros_like(lnx_g)] * 6).astype(F32)
    pend4 = pend.reshape(bsz, n_chunks, 1, WIDTH)
    spec = pl.BlockSpec((1, chunk, WIDTH), lambda b, c: (b, c, 0))
    return pl.pallas_call(
        functools.partial(_rwkv_scan_kernel, chunk=chunk, prec=prec),
        grid=(bsz, n_chunks),
        in_specs=[spec] * 7 + [
            pl.BlockSpec((1, 1, 1, WIDTH), lambda b, c: (b, c, 0, 0)),
            pl.BlockSpec((8, WIDTH), lambda b, c: (0, 0)),
        ],
        out_specs=spec,
        out_shape=jax.ShapeDtypeStruct((bsz, seq, WIDTH), F32),
        scratch_shapes=[pltpu.VMEM((PAIRS, LANES, LANES), F32)],
        compiler_params=_cparams(("parallel", "arbitrary")),
        name="rwkv_scan",
    )(rt, kt, kd, bd, v, g, bonus, pend4, ln)


def _merge_kernel(x_ref, oa_ref, ob_ref, ga_ref, gb_ref, wa_ref, wb_ref, wo_ref, g2_ref,
                  h_ref, xn_ref, acc_ref):
    j = pl.program_id(1)

    @pl.when(j == 0)
    def _():
        acc_ref[...] = x_ref[...]

    ya = jnp.dot(oa_ref[...].astype(BF16), wa_ref[...], preferred_element_type=F32)
    yb = jnp.dot(ob_ref[...].astype(BF16), wb_ref[...], preferred_element_type=F32)
    y = jax.nn.sigmoid(ga_ref[...]) * ya + jax.nn.sigmoid(gb_ref[...]) * yb
    acc_ref[...] += jnp.dot(y.astype(BF16), wo_ref[...], preferred_element_type=F32)

    @pl.when(j == pl.num_programs(1) - 1)
    def _():
        h = acc_ref[...]
        h_ref[...] = h
        ms = jnp.mean(h * h, axis=-1, keepdims=True)
        xn_ref[...] = _pack_halves(h * lax.rsqrt(ms + RMS_EPS) * g2_ref[...])


def _pack_halves(x):
    half = x.shape[1] // 2
    lo = lax.bitcast_convert_type(x[:, :half].astype(BF16).astype(F32), jnp.int32)
    hi = lax.bitcast_convert_type(x[:, half:].astype(BF16).astype(F32), jnp.int32)
    return lax.bitwise_or(lax.shift_right_logical(lo, jnp.int32(16)), hi)


def _unpack_halves(words):
    lo, hi = _unpack_words(words)
    return jnp.concatenate([lo, hi], axis=1)


def merge_out(x2d, oa, ob, p2d, w_proj_a, w_proj_b, w_out, norm2_g, *, tm=512):
    t, d = x2d.shape
    tn = WIDTH
    nj = d // tn
    g0 = COL_G_OFF // tn
    return pl.pallas_call(
        _merge_kernel,
        grid=(t // tm, nj),
        in_specs=[
            pl.BlockSpec((tm, d), lambda i, j: (i, 0)),
            pl.BlockSpec((tm, WIDTH), lambda i, j: (i, 0)),
            pl.BlockSpec((tm, WIDTH), lambda i, j: (i, 0)),
            pl.BlockSpec((tm, tn), lambda i, j: (i, g0 + j)),
            pl.BlockSpec((tm, tn), lambda i, j: (i, g0 + nj + j)),
            pl.BlockSpec((WIDTH, tn), lambda i, j: (0, j)),
            pl.BlockSpec((WIDTH, tn), lambda i, j: (0, j)),
            pl.BlockSpec((tn, d), lambda i, j: (j, 0)),
            pl.BlockSpec((1, d), lambda i, j: (0, 0)),
        ],
        out_specs=[pl.BlockSpec((tm, d), lambda i, j: (i, 0)), pl.BlockSpec((tm, d // 2), lambda i, j: (i, 0))],
        out_shape=[jax.ShapeDtypeStruct((t, d), F32), jax.ShapeDtypeStruct((t, d // 2), jnp.int32)],
        scratch_shapes=[pltpu.VMEM((tm, d), F32)],
        compiler_params=_cparams(("parallel", "arbitrary")),
        name="merge_out",
    )(x2d, oa, ob, p2d, p2d, w_proj_a.astype(BF16), w_proj_b.astype(BF16), w_out.astype(BF16),
      norm2_g.reshape(1, d))


PEER_HEADS = 8
PEER_NKEYS = 128
PEER_TOPK = 16
PEER_HALF = 128


def _topk_rows(s, k):
    n = s.shape[0]
    rows = lax.broadcasted_iota(jnp.int32, s.shape, 0).astype(F32)
    vals, ids = [], []
    for _ in range(k):
        m = jnp.max(s, axis=0, keepdims=True)
        first = jnp.min(jnp.where(s == m, rows, float(n)), axis=0, keepdims=True)
        vals.append(m)
        ids.append(first)
        s = jnp.where(rows == first, -jnp.inf, s)
    return jnp.concatenate(vals, axis=0), jnp.concatenate(ids, axis=0)


def _take_rows(table, ids):
    rows = lax.broadcasted_iota(jnp.int32, table.shape, 0).astype(F32)
    return jnp.sum(jnp.where(rows == ids, table, 0.0), axis=0, keepdims=True)


def _peer_route_kernel(xn_ref, wq_ref, sk_ref, idx_ref, gate_ref, *, prec):
    tt = xn_ref.shape[0]
    k = PEER_TOPK
    xn = _unpack_halves(xn_ref[...]) if xn_ref.dtype == jnp.int32 else xn_ref[...]
    q = jnp.dot(xn.astype(wq_ref.dtype), wq_ref[...], precision=prec, preferred_element_type=F32)
    nt = (((1,), (1,)), ((), ()))
    idx_rows, gate_rows = [], []
    half = k // 2
    for h in range(PEER_HEADS):
        tops = []
        for p in range(2):
            c0 = (h * 2 + p) * PEER_HALF
            s = lax.dot_general(sk_ref[h, p].astype(wq_ref.dtype), q[:, c0:c0 + PEER_HALF].astype(wq_ref.dtype),
                                nt, precision=prec, preferred_element_type=F32)
            tops.append(_topk_rows(s, k))
        (s0, i0), (s1, i1) = tops
        cs = [s0[0:1] + s1] + [s0[i:i + 1] + s1[0:half] for i in range(1, half)] + [s0[half:] + s1[0:1]]
        best_s, pos = _topk_rows(jnp.concatenate(cs, axis=0), k)
        mid = jnp.floor((pos - k) * (1.0 / half))
        end_mid = float(k + (half - 1) * half)
        i_rank = jnp.where(pos < k, 0.0, jnp.where(pos < end_mid, 1.0 + mid, pos - (end_mid - half)))
        j_rank = jnp.where(pos < k, pos, jnp.where(pos < end_mid, (pos - k) - half * mid, 0.0))
        ids = [_take_rows(i0, i_rank[n:n + 1]) * PEER_NKEYS + _take_rows(i1, j_rank[n:n + 1]) for n in range(k)]
        e = jnp.exp(best_s - best_s[0:1])
        gate_rows.append(e / jnp.sum(e, axis=0, keepdims=True))
        idx_rows.append(jnp.concatenate(ids, axis=0).astype(jnp.int32))
    idx_ref[...] = jnp.concatenate(idx_rows, axis=0).T
    gate_ref[...] = jnp.concatenate(gate_rows, axis=0).T


def peer_route(xn2d, peer_wq, peer_subkeys, *, tt=256, prec=None, wdtype=BF16):
    t, dx = xn2d.shape
    d, nq = peer_wq.shape
    n_sel = PEER_HEADS * PEER_TOPK
    return pl.pallas_call(
        functools.partial(_peer_route_kernel, prec=prec),
        grid=(t // tt,),
        in_specs=[
            pl.BlockSpec((tt, dx), lambda i: (i, 0)),
            pl.BlockSpec((d, nq), lambda i: (0, 0)),
            pl.BlockSpec((PEER_HEADS, 2, PEER_NKEYS, PEER_HALF), lambda i: (0, 0, 0, 0)),
        ],
        out_specs=[pl.BlockSpec((tt, n_sel), lambda i: (i, 0))] * 2,
        out_shape=[jax.ShapeDtypeStruct((t, n_sel), jnp.int32), jax.ShapeDtypeStruct((t, n_sel), F32)],
        compiler_params=_cparams(("parallel",)),
        name="peer_route",
    )(xn2d, peer_wq.astype(wdtype), peer_subkeys)


def _final_kernel(h_ref, y_ref, g_ref, o_ref):
    h = h_ref[...] + y_ref[...]
    ms = jnp.mean(h * h, axis=-1, keepdims=True)
    o_ref[...] = h * lax.rsqrt(ms + RMS_EPS) * g_ref[...]


def final_norm(h2d, y2d, g, *, tm=1024):
    t, d = h2d.shape
    spec = pl.BlockSpec((tm, d), lambda i: (i, 0))
    return pl.pallas_call(
        _final_kernel,
        grid=(t // tm,),
        in_specs=[spec, spec, pl.BlockSpec((1, d), lambda i: (0, 0))],
        out_specs=spec,
        out_shape=jax.ShapeDtypeStruct((t, d), F32),
        compiler_params=_cparams(("parallel",)),
        name="final_norm",
    )(h2d, y2d, g.reshape(1, d))


SC_CORES = 2
SC_SUBCORES = 16
SC_LANES = 16
SC_WORKERS = SC_CORES * SC_SUBCORES
PEER_SEL = PEER_HEADS * PEER_TOPK
PEER_ROWS = 32
PEER_PARTS = PEER_SEL // PEER_ROWS
PEER_NBUF = 4
PEER_GROUP = 32
PEER_BF16_RUN = 4


def _pack_rows(w):
    half = w.shape[1] // 2
    bits = lax.bitcast_convert_type(w.astype(BF16), jnp.uint16).astype(jnp.uint32)
    return lax.bitcast_convert_type(bits[:, :half] | (bits[:, half:] << 16), jnp.int32)


def _unpack_words(w):
    lo = lax.bitcast_convert_type(lax.shift_left(w, jnp.int32(16)), F32)
    hi = lax.bitcast_convert_type(lax.bitwise_and(w, jnp.int32(-65536)), F32)
    return lo, hi


def _packed_dot(a_words, b_words):
    from jax.experimental.pallas import tpu_sc as plsc
    prods = [plsc.bitcast(a, BF16) * plsc.bitcast(b, BF16) for a, b in zip(a_words, b_words)]
    while len(prods) > 1:
        prods = [prods[k] + prods[k + 1] for k in range(0, len(prods), 2)]
    return _unpack_words(plsc.bitcast(prods[0], jnp.int32))


def _sc_mesh():
    from jax.experimental.pallas import tpu_sc as plsc
    return plsc.VectorSubcoreMesh(core_axis_name="c", subcore_axis_name="s",
                                  num_cores=SC_CORES, num_subcores=SC_SUBCORES)


def _sc_loop(n, body, carry):
    from jax.experimental.pallas import tpu_sc as plsc
    return plsc.parallel_loop(0, n, carry=carry)(body)


def _worker_base(tokens_per_worker):
    return (lax.axis_index("s") * SC_CORES + lax.axis_index("c")) * tokens_per_worker


def _gather_compute_loop(table_hbm, idx_v, rows_v, sem, stage_v, out_row, osem, grp, compute):
    n_gathers = PEER_PARTS * grp
    ahead = PEER_NBUF - 1

    def gather(j, b):
        i = j // PEER_PARTS if isinstance(j, int) else lax.shift_right_logical(j, PEER_PARTS.bit_length() - 1)
        h = j % PEER_PARTS if isinstance(j, int) else lax.bitwise_and(j, PEER_PARTS - 1)
        ids = idx_v.at[i, pl.ds(pl.multiple_of(h * PEER_ROWS, PEER_ROWS), PEER_ROWS)]
        return pltpu.make_async_copy(table_hbm.at[ids], rows_v.at[b], sem.at[b])

    def put(i, slot):
        return pltpu.make_async_copy(stage_v.at[slot], out_row(i), osem.at[slot])

    for j in range(ahead):
        gather(j, j).start()

    @pl.loop(0, n_gathers)
    def _(j):
        b = lax.bitwise_and(j, PEER_NBUF - 1)
        h = lax.bitwise_and(j, PEER_PARTS - 1)
        i = lax.shift_right_logical(j, PEER_PARTS.bit_length() - 1)
        slot = lax.bitwise_and(i, 1)

        @pl.when((h == 0) & (i >= 2))
        def _():
            put(i - 2, slot).wait()

        @pl.when(j + ahead < n_gathers)
        def _():
            gather(j + ahead, lax.bitwise_and(j + ahead, PEER_NBUF - 1)).start()

        gather(j, b).wait()
        compute(i, h, b, slot)

        @pl.when(h == PEER_PARTS - 1)
        def _():
            put(i, slot).start()

    put(grp - 2, 0).wait()
    put(grp - 1, 1).wait()


def peer_expert_dots(x_packed, idx, u_packed):
    t, half = x_packed.shape
    n_chunks = half // SC_LANES
    tpw = t // SC_WORKERS
    grp = min(PEER_GROUP, tpw)
    rows_tog = 4

    def body(x_hbm, idx_hbm, u_hbm, out_hbm, idx_v, x_v, rows_v, ps_v, sem, osem):
        base = _worker_base(tpw)

        def compute(i, h, b, slot):
            @pl.loop(0, PEER_ROWS // rows_tog)
            def _(rg):
                r0 = rg * rows_tog
                accs = [[None, None] for _ in range(rows_tog)]
                for c0 in range(0, n_chunks, PEER_BF16_RUN):
                    ats = [pl.ds((c0 + k) * SC_LANES, SC_LANES) for k in range(PEER_BF16_RUN)]
                    xw = [x_v[i, at] for at in ats]
                    for r in range(rows_tog):
                        terms = _packed_dot([rows_v[b, r0 + r, at] for at in ats], xw)
                        for k, term in enumerate(terms):
                            accs[r][k] = term if accs[r][k] is None else accs[r][k] + term
                for r in range(rows_tog):
                    at = pl.ds(pl.multiple_of((h * PEER_ROWS + r0 + r) * SC_LANES, SC_LANES), SC_LANES)
                    ps_v[slot, at] = accs[r][0] + accs[r][1]

        @pl.loop(0, tpw // grp)
        def _(g):
            t0 = base + g * grp
            pltpu.sync_copy(idx_hbm.at[pl.ds(t0, grp)], idx_v)
            pltpu.sync_copy(x_hbm.at[pl.ds(t0, grp)], x_v)
            _gather_compute_loop(u_hbm, idx_v, rows_v, sem, ps_v, lambda i: out_hbm.at[t0 + i], osem, grp, compute)

    return pl.kernel(
        body,
        out_type=jax.ShapeDtypeStruct((t, PEER_SEL * SC_LANES), F32),
        mesh=_sc_mesh(),
        scratch_types=[
            pltpu.VMEM((grp, PEER_SEL), jnp.int32),
            pltpu.VMEM((grp, half), jnp.int32),
            pltpu.VMEM((PEER_NBUF, PEER_ROWS, half), jnp.int32),
            pltpu.VMEM((2, PEER_SEL * SC_LANES), F32),
            pltpu.SemaphoreType.DMA((PEER_NBUF,)),
            pltpu.SemaphoreType.DMA((2,)),
        ],
        compiler_params=pltpu.CompilerParams(needs_layout_passes=False),
        name="peer_expert_dots",
    )(x_packed, idx, u_packed)


def peer_expert_mix(hgx, idx, v_packed):
    t = hgx.shape[0]
    half = v_packed.shape[1]
    d = 2 * half
    tpw = t // SC_WORKERS
    grp = min(PEER_GROUP // 2, tpw)
    n_parts = 2
    cpp = half // SC_LANES // n_parts

    def body(hg_hbm, idx_hbm, v_hbm, out_hbm, idx_v, hg_v, rows_v, o_v2, sem, osem):
        base = _worker_base(tpw)

        def compute(i, h, b, slot):
            for part in range(n_parts):
                def rbody(rq, accs):
                    r0 = rq * PEER_BF16_RUN
                    s = [hg_v[i, pl.ds(pl.multiple_of((h * PEER_ROWS + r0 + k) * SC_LANES, SC_LANES), SC_LANES)]
                         for k in range(PEER_BF16_RUN)]
                    new = []
                    for c in range(cpp):
                        at = pl.ds((part * cpp + c) * SC_LANES, SC_LANES)
                        lo, hi = _packed_dot([rows_v[b, r0 + k, at] for k in range(PEER_BF16_RUN)], s)
                        new.append(accs[2 * c] + lo)
                        new.append(accs[2 * c + 1] + hi)
                    return tuple(new)

                accs = _sc_loop(PEER_ROWS // PEER_BF16_RUN, rbody,
                                tuple(jnp.zeros((SC_LANES,), F32) for _ in range(2 * cpp)))
                def store(overwrite):
                    for c in range(cpp):
                        lo_at = pl.ds((part * cpp + c) * SC_LANES, SC_LANES)
                        hi_at = pl.ds(half + (part * cpp + c) * SC_LANES, SC_LANES)
                        if overwrite:
                            o_v2[slot, lo_at] = accs[2 * c]
                            o_v2[slot, hi_at] = accs[2 * c + 1]
                        else:
                            o_v2[slot, lo_at] = o_v2[slot, lo_at] + accs[2 * c]
                            o_v2[slot, hi_at] = o_v2[slot, hi_at] + accs[2 * c + 1]

                pl.when(h == 0)(functools.partial(store, True))
                pl.when(h != 0)(functools.partial(store, False))

        @pl.loop(0, tpw // grp)
        def _(g):
            t0 = base + g * grp
            pltpu.sync_copy(idx_hbm.at[pl.ds(t0, grp)], idx_v)
            pltpu.sync_copy(hg_hbm.at[pl.ds(t0, grp)], hg_v)
            _gather_compute_loop(v_hbm, idx_v, rows_v, sem, o_v2, lambda i: out_hbm.at[t0 + i], osem, grp, compute)

    return pl.kernel(
        body,
        out_type=jax.ShapeDtypeStruct((t, d), F32),
        mesh=_sc_mesh(),
        scratch_types=[
            pltpu.VMEM((grp, PEER_SEL), jnp.int32),
            pltpu.VMEM((grp, PEER_SEL * SC_LANES), jnp.int32),
            pltpu.VMEM((PEER_NBUF, PEER_ROWS, half), jnp.int32),
            pltpu.VMEM((2, d), F32),
            pltpu.SemaphoreType.DMA((PEER_NBUF,)),
            pltpu.SemaphoreType.DMA((2,)),
        ],
        compiler_params=pltpu.CompilerParams(needs_layout_passes=False),
        name="peer_expert_mix",
    )(hgx, idx, v_packed)


def _peer_act_kernel(ps_ref, gate_ref, sum_ref, o_ref):
    ps = ps_ref[...]
    sel = sum_ref[...]
    hi = ps.astype(BF16)
    rest = ps - hi.astype(F32)
    mid = rest.astype(BF16)
    lo = (rest - mid.astype(F32)).astype(BF16)
    pre = (jnp.dot(hi, sel, preferred_element_type=F32) + jnp.dot(mid, sel, preferred_element_type=F32)
           + jnp.dot(lo, sel, preferred_element_type=F32))
    hg = 0.5 * pre * (1.0 + lax.erf(pre * (1.0 / math.sqrt(2.0)))) * gate_ref[...]
    spread = (((1,), (1,)), ((), ()))
    hgx = lax.dot_general(hg.astype(BF16), sel, spread, preferred_element_type=F32)
    bits = lax.bitcast_convert_type(hgx, jnp.int32)
    o_ref[...] = lax.bitwise_or(bits, lax.shift_right_logical(bits, jnp.int32(16)))


def peer_act(ps, gates, *, tm=512):
    t, n = ps.shape
    lane_sum = (jnp.arange(n)[:, None] // SC_LANES == jnp.arange(PEER_SEL)[None, :]).astype(BF16)
    return pl.pallas_call(
        _peer_act_kernel,
        grid=(t // tm,),
        in_specs=[
            pl.BlockSpec((tm, n), lambda i: (i, 0)),
            pl.BlockSpec((tm, PEER_SEL), lambda i: (i, 0)),
            pl.BlockSpec((n, PEER_SEL), lambda i: (0, 0)),
        ],
        out_specs=pl.BlockSpec((tm, n), lambda i: (i, 0)),
        out_shape=jax.ShapeDtypeStruct((t, n), jnp.int32),
        compiler_params=_cparams(("parallel",)),
        name="peer_act",
    )(ps, gates, lane_sum)


BATCH_GROUPS = 8


def kernel(x, norm1_g, w_in, rwkv_mu, w0, w_lora_up, a0, a_lora_up, g_lora_up, k_k, k_a, r_k, lnx_g, lnx_b,
           w_proj_a, w_proj_b, w_out, norm2_g, peer_wq, peer_subkeys, peer_u, peer_v, rel_bias, normf_g):
    bsz, seq, d = x.shape
    depth = norm1_g.shape[0]
    groups = BATCH_GROUPS if bsz % BATCH_GROUPS == 0 else 1
    gb = bsz // groups
    tg = gb * seq
    hs = [x[g * gb:(g + 1) * gb].reshape(tg, d) for g in range(groups)]
    for l in range(depth):
        w_pad = jnp.concatenate([
            w_in[l][:, :COL_A + COL_B_RAW],
            jnp.zeros((d, COL_B - COL_B_RAW), w_in.dtype),
            w_in[l][:, COL_A + COL_B_RAW:]], axis=1).astype(BF16)
        u_packed = _pack_rows(peer_u[l])
        v_packed = _pack_rows(peer_v[l])
        last = l == depth - 1

        def mix(pending, tie=None):
            g, h2d, ps, gates, idx = pending
            hgx = peer_act(ps, gates)
            if tie is not None:
                tie, hgx = lax.optimization_barrier((tie, hgx))
            return tie, (g, h2d, peer_expert_mix(hgx, idx, v_packed))

        def close(mixed, tie=None):
            g, h2d, y2d = mixed
            out = final_norm(h2d, y2d, normf_g) if last else h2d + y2d
            if tie is not None:
                tie, out = lax.optimization_barrier((tie, out))
            hs[g] = out
            return tie

        pending = None
        for g in range(groups):
            p2d = norm_proj(hs[g], norm1_g[l], w_pad)
            p3d = p2d.reshape(gb, seq, -1)
            oa = moba_attention(p3d, rel_bias)
            mixed = None
            if pending is not None:
                oa, mixed = mix(pending, oa)
            prep = rwkv_prep(p3d, rwkv_mu[l], w0[l], w_lora_up[l], a0[l], a_lora_up[l], g_lora_up[l],
                             k_k[l], k_a[l], r_k[l])
            ob = rwkv_scan(*prep, lnx_g[l], lnx_b[l])
            h2d, xn2 = merge_out(hs[g], oa.reshape(tg, WIDTH), ob.reshape(tg, WIDTH), p2d,
                                 w_proj_a[l], w_proj_b[l], w_out[l], norm2_g[l])
            idx, gates = peer_route(xn2, peer_wq[l], peer_subkeys[l])
            if mixed is not None:
                idx = close(mixed, idx)
            pending = (g, h2d, peer_expert_dots(xn2, idx, u_packed), gates, idx)
        close(mix(pending)[1])
    return jnp.concatenate(hs, axis=0).reshape(bsz, seq, d)
```

```python
import functools
import math

import jax
import jax.numpy as jnp
from jax import lax
from jax.experimental import pallas as pl
from jax.experimental.pallas import tpu as pltpu

F32 = jnp.float32
BF16 = jnp.bfloat16
HI = lax.Precision.HIGHEST

LANES = 128
HEAD_DIM = 64
HEADS = 8
PAIRS = HEADS // 2
WIDTH = HEADS * HEAD_DIM
MOBA_BLOCK = 256
MOBA_TOPK = 3
MOBA_LO = 64
REL_BUCKETS = 32
REL_MAX_DIST = 128
DECAY_LORA = 64
AAA_LORA = 64
GATE_LORA = 160
GN_EPS = 64e-5
RMS_EPS = 1e-6
NEG = -1e30
RWKV_CHUNK = 64
COL_A = 3 * WIDTH
COL_B_RAW = 3 * WIDTH + DECAY_LORA + AAA_LORA + GATE_LORA
COL_B = 4 * WIDTH
COL_G_OFF = COL_A + COL_B
VMEM_LIMIT = 56 * 1024 * 1024


def _cparams(sem):
    return pltpu.CompilerParams(dimension_semantics=sem, vmem_limit_bytes=VMEM_LIMIT)


def _norm_proj_kernel(x_ref, g_ref, w_ref, o_ref, xn_ref):
    @pl.when(pl.program_id(1) == 0)
    def _():
        x = x_ref[...]
        ms = jnp.mean(x * x, axis=-1, keepdims=True)
        xn_ref[...] = (x * lax.rsqrt(ms + RMS_EPS) * g_ref[...]).astype(xn_ref.dtype)

    o_ref[...] = jnp.dot(xn_ref[...], w_ref[...], preferred_element_type=F32).astype(o_ref.dtype)


def norm_proj(x2d, g, w, *, tm=512, tn=512, out_dtype=F32):
    t, d = x2d.shape
    n = w.shape[1]
    return pl.pallas_call(
        _norm_proj_kernel,
        grid=(t // tm, n // tn),
        in_specs=[
            pl.BlockSpec((tm, d), lambda i, j: (i, 0)),
            pl.BlockSpec((1, d), lambda i, j: (0, 0)),
            pl.BlockSpec((d, tn), lambda i, j: (0, j)),
        ],
        out_specs=pl.BlockSpec((tm, tn), lambda i, j: (i, j)),
        out_shape=jax.ShapeDtypeStruct((t, n), out_dtype),
        scratch_shapes=[pltpu.VMEM((tm, d), w.dtype)],
        compiler_params=_cparams(("parallel", "arbitrary")),
        name="norm_proj",
    )(x2d, g.reshape(1, d), w)


def _rel_bucket(dist):
    n = jnp.maximum(dist, 0)
    max_exact = REL_BUCKETS // 2
    nf = jnp.maximum(n, 1).astype(F32)
    large = max_exact + (jnp.log(nf / max_exact) / math.log(REL_MAX_DIST / max_exact)
                         * (REL_BUCKETS - max_exact)).astype(jnp.int32)
    large = jnp.minimum(large, REL_BUCKETS - 1)
    return jnp.where(n < max_exact, n, large)


def _moba_kernel(q_ref, k_ref, v_ref, bown_ref, bprev_ref, bfar_ref, o_ref,
                 kb_ref, vb_ref, kbar_ref, *, n_blocks):
    qb = pl.program_id(2)
    blk = MOBA_BLOCK
    scale = 1.0 / math.sqrt(HEAD_DIM)

    rows2 = 2 * blk
    nt = (((1,), (1,)), ((), ()))

    @pl.when(qb == 0)
    def _():
        kbar_ref[...] = jnp.zeros_like(kbar_ref)
        lane_b = lax.broadcasted_iota(jnp.int32, (blk, LANES), 1)
        for n in range(n_blocks):
            kblk = k_ref[0, n * blk:(n + 1) * blk, :]
            kbar_ref[n:n + 1, :] = jnp.mean(kblk, axis=0, keepdims=True)
            kb_ref[n * blk:(n + 1) * blk, 0:LANES] = kblk.astype(BF16)
            kb_ref[n * blk:(n + 1) * blk, LANES:] = ((lane_b == n) | (lane_b == MOBA_LO + n)).astype(BF16)
        vb_ref[...] = v_ref[0].astype(BF16)

    q2 = q_ref[0]
    first = lax.broadcasted_iota(jnp.int32, (blk, LANES), 1) < HEAD_DIM
    qh = jnp.concatenate([jnp.where(first, q2, 0.0), jnp.where(first, 0.0, q2)], axis=0)
    lane = lax.broadcasted_iota(jnp.int32, (rows2, LANES), 1)
    rowi = lax.broadcasted_iota(jnp.int32, (rows2, LANES), 0)
    gate = lax.dot_general(qh.astype(BF16), kbar_ref[...].astype(BF16), nt, preferred_element_type=F32)
    g = jnp.where(lane < qb, gate, -jnp.inf)
    chosen = lane < 0
    lane_f = lane.astype(F32)
    for _ in range(MOBA_TOPK):
        m = jnp.max(g, axis=1, keepdims=True)
        idx = jnp.min(jnp.where(g == m, lane_f, float(LANES)), axis=1, keepdims=True)
        hit = (lane_f == idx) & (m > -jnp.inf)
        chosen = chosen | hit
        g = jnp.where(hit, -jnp.inf, g)
    nfar = qb - 1
    bfar = jnp.where(rowi < blk, bfar_ref[0, 0:1, 0:1], bfar_ref[1, 0:1, 0:1])
    bhi = bfar.astype(BF16).astype(F32)
    madd = jnp.where(lane < nfar, jnp.where(chosen, bhi, NEG),
                     jnp.where(lane == nfar, jnp.where(chosen, 0.0, NEG),
                               jnp.where((lane >= MOBA_LO) & (lane - MOBA_LO < nfar), bfar - bhi, 0.0)))
    q_aug = jnp.concatenate([(qh * scale).astype(BF16), madd.astype(BF16)], axis=1)

    prev0 = pl.multiple_of(jnp.maximum(nfar, 0) * blk, blk)
    own0 = pl.multiple_of(qb * blk, blk)
    s_prev = (lax.dot_general(q_aug, kb_ref[pl.ds(prev0, blk), :], nt, preferred_element_type=F32)
              + bprev_ref[...].reshape(rows2, blk) + jnp.where(qb > 0, 0.0, NEG))
    s_own = (lax.dot_general(q_aug, kb_ref[pl.ds(own0, blk), :], nt, preferred_element_type=F32)
             + bown_ref[...].reshape(rows2, blk))
    r = lax.broadcasted_iota(jnp.int32, (rows2, blk), 0)
    c = lax.broadcasted_iota(jnp.int32, (rows2, blk), 1)
    s_own = jnp.where(lax.bitwise_and(r, blk - 1) >= c, s_own, NEG)
    s = jnp.concatenate([s_prev, s_own], axis=1)
    m_i = jnp.max(s, axis=1, keepdims=True)
    p = jnp.exp(s - m_i)
    l_i = jnp.sum(p, axis=1, keepdims=True)
    v0 = jnp.concatenate([vb_ref[pl.ds(prev0, blk), :], vb_ref[pl.ds(own0, blk), :]], axis=0)
    acc = jnp.dot(p.astype(BF16), v0, preferred_element_type=F32)

    def body(it, carry):
        m_i, l_i, acc = carry
        k0 = pl.multiple_of(it * rows2, rows2)
        s = lax.dot_general(q_aug, kb_ref[pl.ds(k0, rows2), :], nt, preferred_element_type=F32)
        tail = jnp.where(2 * it + 1 < nfar, 0.0, NEG)
        s = jnp.concatenate([s[:, :blk], s[:, blk:] + tail], axis=1)
        m_new = jnp.maximum(m_i, jnp.max(s, axis=1, keepdims=True))
        alpha = jnp.exp(m_i - m_new)
        p = jnp.exp(s - m_new)
        l_new = alpha * l_i + jnp.sum(p, axis=1, keepdims=True)
        acc_new = alpha * acc + jnp.dot(p.astype(BF16), vb_ref[pl.ds(k0, rows2), :], preferred_element_type=F32)
        return m_new, l_new, acc_new

    m_i, l_i, acc = lax.fori_loop(0, (jnp.maximum(nfar, 0) + 1) // 2, body, (m_i, l_i, acc))
    out = acc / l_i
    o_ref[0] = jnp.where(first, out[:blk], out[blk:])


def moba_attention(p3d, rel_bias):
    bsz, seq, _ = p3d.shape
    blk = MOBA_BLOCK
    n_blocks = seq // blk
    span = 2 * blk
    by_dist = rel_bias[:, _rel_bucket(jnp.arange(span))].astype(F32)
    shift = jnp.arange(span)

    def toeplitz(c):
        k = jnp.where(shift < blk, shift, shift - span)
        s = by_dist[:, jnp.clip(c - k, 0, span - 1)]
        tiled = jnp.tile(s, (1, blk))[:, :blk * (span - 1)]
        return tiled.reshape(HEADS, blk, span - 1)[:, :, :blk]

    bias_own = toeplitz(0)
    bias_prev = toeplitz(blk)
    bias_far = jnp.broadcast_to(rel_bias[:, REL_BUCKETS - 1].astype(F32)[:, None, None], (HEADS, 8, LANES))
    kern = functools.partial(_moba_kernel, n_blocks=n_blocks)
    return pl.pallas_call(
        kern,
        grid=(bsz, PAIRS, n_blocks),
        in_specs=[
            pl.BlockSpec((1, blk, LANES), lambda b, h, i: (b, i, h)),
            pl.BlockSpec((1, seq, LANES), lambda b, h, i: (b, 0, PAIRS + h)),
            pl.BlockSpec((1, seq, LANES), lambda b, h, i: (b, 0, 2 * PAIRS + h)),
            pl.BlockSpec((2, blk, blk), lambda b, h, i: (h, 0, 0)),
            pl.BlockSpec((2, blk, blk), lambda b, h, i: (h, 0, 0)),
            pl.BlockSpec((2, 8, LANES), lambda b, h, i: (h, 0, 0)),
        ],
        out_specs=pl.BlockSpec((1, blk, LANES), lambda b, h, i: (b, i, h)),
        out_shape=jax.ShapeDtypeStruct((bsz, seq, WIDTH), F32),
        scratch_shapes=[
            pltpu.VMEM((seq, 2 * LANES), BF16),
            pltpu.VMEM((seq, LANES), BF16),
            pltpu.VMEM((LANES, LANES), F32),
        ],
        compiler_params=_cparams(("parallel", "parallel", "arbitrary")),
        name="moba",
    )(p3d, p3d, p3d, bias_own, bias_prev, bias_far)


def _shifted(x, carry_row):
    rows = lax.broadcasted_iota(jnp.int32, x.shape, 0)
    return jnp.where(rows == 0, carry_row, pltpu.roll(x, 1, axis=0))


def _rwkv_prep_kernel(pr_ref, pk_ref, pv_ref, pl_ref, mu_ref, vec_ref, ww_ref, wa_ref, wg_ref,
                      bd_ref, tri_ref,
                      rt_ref, kt_ref, kd_ref, bd_out_ref, v_ref, g_ref, bonus_ref, pend_ref,
                      carry_ref, *, chunk):
    @pl.when(pl.program_id(1) == 0)
    def _():
        carry_ref[...] = jnp.zeros_like(carry_ref)

    def mix(ref, j):
        x = ref[0]
        mu = mu_ref[0:1, j * WIDTH:(j + 1) * WIDTH]
        prev = _shifted(x, carry_ref[0:1, j * WIDTH:(j + 1) * WIDTH])
        carry_ref[0:1, j * WIDTH:(j + 1) * WIDTH] = x[x.shape[0] - 1:, :]
        return x + mu * (prev - x)

    r = mix(pr_ref, 0)
    k = mix(pk_ref, 1)
    v = mix(pv_ref, 2)
    lo = mix(pl_ref, 3)
    w0, a0, k_k, k_a, r_k = (vec_ref[i:i + 1, :] for i in range(5))
    xwa = lo[:, 0:LANES]
    xg = lo[:, LANES:3 * LANES]
    lw = jnp.dot(jnp.tanh(xwa), ww_ref[...], precision=HI, preferred_element_type=F32)
    la = jnp.dot(xwa, wa_ref[...], precision=HI, preferred_element_type=F32)
    g = jnp.dot(jax.nn.sigmoid(xg), wg_ref[...], precision=HI, preferred_element_type=F32)
    z = -(w0 + lw)
    softplus = jnp.maximum(z, 0.0) + jnp.log(1.0 + jnp.exp(-jnp.abs(z)))
    logw = -jnp.exp(-softplus - 0.5)
    a = jax.nn.sigmoid(a0 + la)
    kk = k * k_k
    ss = jnp.dot(kk * kk, bd_ref[...], precision=HI, preferred_element_type=F32)
    kk = kk / jnp.maximum(jnp.sqrt(ss), 1e-12)
    k2 = k * (1.0 + (a - 1.0) * k_a)
    rk = jnp.dot(r * k2 * r_k, bd_ref[...], precision=HI, preferred_element_type=F32)
    cs = jnp.dot(tri_ref[...], logw, precision=HI, preferred_element_type=F32)
    e_pos = jnp.exp(cs)
    e_neg = jnp.exp(-cs)
    rt_ref[0] = r * e_pos
    kt_ref[0] = kk * jnp.exp(cs - logw)
    kd_ref[0] = k2 * e_neg
    bd_out_ref[0] = kk * a * e_neg
    v_ref[0] = v
    g_ref[0] = g
    bonus_ref[0] = rk * v
    ts = e_pos.shape[0]
    for c in range(ts // chunk):
        pend_ref[0, c:c + 1, :] = e_pos[(c + 1) * chunk - 1:(c + 1) * chunk, :]


def rwkv_prep(p3d, rwkv_mu, w0, w_lora_up, a0, a_lora_up, g_lora_up, k_k, k_a, r_k, *, ts=512):
    bsz, seq, _ = p3d.shape
    chunk = RWKV_CHUNK
    ts = min(ts, seq)
    mu = jnp.pad(rwkv_mu, (0, COL_B - COL_B_RAW)).reshape(1, COL_B)
    vec = jnp.stack([w0, a0, k_k, k_a, r_k.reshape(-1)] + [jnp.zeros_like(w0)] * 3).astype(F32)
    ww = jnp.zeros((LANES, WIDTH), F32).at[:DECAY_LORA].set(w_lora_up)
    wa = jnp.zeros((LANES, WIDTH), F32).at[DECAY_LORA:DECAY_LORA + AAA_LORA].set(a_lora_up)
    wg = jnp.zeros((2 * LANES, WIDTH), F32).at[:GATE_LORA].set(g_lora_up)
    hid = jnp.arange(WIDTH) // HEAD_DIM
    bd = (hid[:, None] == hid[None, :]).astype(F32)
    tix = jnp.arange(ts)
    tri = ((tix[:, None] // chunk == tix[None, :] // chunk) & (tix[None, :] <= tix[:, None])).astype(F32)
    c0 = COL_A // WIDTH
    big = jax.ShapeDtypeStruct((bsz, seq, WIDTH), F32)
    wspec = lambda shape: pl.BlockSpec(shape, lambda b, i: (0, 0))
    ospec = pl.BlockSpec((1, ts, WIDTH), lambda b, i: (b, i, 0))
    return pl.pallas_call(
        functools.partial(_rwkv_prep_kernel, chunk=chunk),
        grid=(bsz, seq // ts),
        in_specs=[
            pl.BlockSpec((1, ts, WIDTH), lambda b, i: (b, i, c0)),
            pl.BlockSpec((1, ts, WIDTH), lambda b, i: (b, i, c0 + 1)),
            pl.BlockSpec((1, ts, WIDTH), lambda b, i: (b, i, c0 + 2)),
            pl.BlockSpec((1, ts, WIDTH), lambda b, i: (b, i, c0 + 3)),
            wspec((1, COL_B)), wspec((8, WIDTH)), wspec((LANES, WIDTH)), wspec((LANES, WIDTH)),
            wspec((2 * LANES, WIDTH)), wspec((WIDTH, WIDTH)), wspec((ts, ts)),
        ],
        out_specs=[ospec] * 7 + [pl.BlockSpec((1, ts // chunk, WIDTH), lambda b, i: (b, i, 0))],
        out_shape=[big] * 7 + [jax.ShapeDtypeStruct((bsz, seq // chunk, WIDTH), F32)],
        scratch_shapes=[pltpu.VMEM((8, COL_B), F32)],
        compiler_params=_cparams(("parallel", "arbitrary")),
        name="rwkv_prep",
    )(p3d, p3d, p3d, p3d, mu, vec, ww, wa, wg, bd, tri)


def _rwkv_scan_kernel(rt_ref, kt_ref, kd_ref, bd_ref, v_ref, g_ref, bonus_ref, pend_ref, ln_ref, o_ref,
                      state_ref, *, chunk, prec):
    @pl.when(pl.program_id(1) == 0)
    def _():
        state_ref[...] = jnp.zeros_like(state_ref)

    c2 = 2 * chunk
    lane = lax.broadcasted_iota(jnp.int32, (chunk, LANES), 1)
    first = lane < HEAD_DIM
    row = lax.broadcasted_iota(jnp.int32, (c2, c2), 0)
    col = lax.broadcasted_iota(jnp.int32, (c2, c2), 1)
    eye = (row == col).astype(F32)
    hrow = lax.broadcasted_iota(jnp.int32, (LANES, LANES), 0) // HEAD_DIM
    hcol = lax.broadcasted_iota(jnp.int32, (LANES, LANES), 1) // HEAD_DIM
    head_mean = jnp.where(hrow == hcol, 1.0 / HEAD_DIM, 0.0).astype(F32)
    nt = (((1,), (1,)), ((), ()))
    tn = (((0,), (0,)), ((), ()))
    dot = functools.partial(jnp.dot, precision=prec, preferred_element_type=F32)
    dotg = functools.partial(lax.dot_general, precision=prec, preferred_element_type=F32)

    def stack(x):
        return jnp.concatenate([jnp.where(first, x, 0.0), jnp.where(first, 0.0, x)], axis=0)

    pairs = range(PAIRS)
    sls = [slice(hp * LANES, (hp + 1) * LANES) for hp in pairs]
    rs, ks, kds, bs, vs = ([stack(ref[0, :, sl]) for sl in sls] for ref in (rt_ref, kt_ref, kd_ref, bd_ref, v_ref))
    hts = [state_ref[hp] for hp in pairs]
    big = [dotg(jnp.concatenate([ks[hp], rs[hp]], axis=0), jnp.concatenate([bs[hp], kds[hp]], axis=0), nt)
           for hp in pairs]
    a_b = [jnp.where(row > col, big[hp][0:c2, 0:c2], 0.0) for hp in pairs]
    a_k = [jnp.where(row > col, big[hp][0:c2, c2:], 0.0) for hp in pairs]
    a_rb = [jnp.where(row >= col, big[hp][c2:, 0:c2], 0.0) for hp in pairs]
    a_rk = [jnp.where(row >= col, big[hp][c2:, c2:], 0.0) for hp in pairs]
    kh = [dotg(jnp.concatenate([ks[hp], rs[hp]], axis=0), hts[hp], nt) for hp in pairs]
    av = [dot(jnp.concatenate([a_k[hp], a_rk[hp]], axis=0), vs[hp]) for hp in pairs]
    vk = [dotg(vs[hp], kds[hp], tn) for hp in pairs]
    inv = [eye - a_b[hp] for hp in pairs]
    pw = [dot(a_b[hp], a_b[hp]) for hp in pairs]
    n_sq = int(math.log2(chunk)) - 1
    for lvl in range(n_sq):
        if lvl + 1 < n_sq:
            both = [dot(jnp.concatenate([inv[hp], pw[hp]], axis=0), pw[hp]) for hp in pairs]
            inv = [inv[hp] + both[hp][0:c2] for hp in pairs]
            pw = [both[hp][c2:] for hp in pairs]
        else:
            inv = [inv[hp] + dot(inv[hp], pw[hp]) for hp in pairs]
    us = [dot(inv[hp], kh[hp][0:c2] + av[hp][0:c2]) for hp in pairs]
    ub = [dotg(us[hp], bs[hp], tn) for hp in pairs]
    au = [dot(a_rb[hp], us[hp]) for hp in pairs]
    for hp in pairs:
        sl = sls[hp]
        pend = pend_ref[0, 0, 0:1, sl]
        state_ref[hp] = (hts[hp] + vk[hp] - ub[hp]) * pend
        os_ = kh[hp][c2:] + av[hp][c2:] - au[hp]
        o = os_[0:chunk] + os_[chunk:]
        mu = jnp.dot(o, head_mean, precision=HI, preferred_element_type=F32)
        d = o - mu
        var = jnp.dot(d * d, head_mean, precision=HI, preferred_element_type=F32)
        on = d * lax.rsqrt(var + GN_EPS) * ln_ref[0:1, sl] + ln_ref[1:2, sl]
        o_ref[0, :, sl] = (on + bonus_ref[0, :, sl]) * g_ref[0, :, sl]


def rwkv_scan(rt, kt, kd, bd, v, g, bonus, pend, lnx_g, lnx_b, *, prec=None):
    bsz, seq, _ = rt.shape
    chunk = RWKV_CHUNK
    n_chunks = seq // chunk
    ln = jnp.stack([lnx_g, lnx_b] + [jnp.zeros_like(lnx_g)] * 6).astype(F32)
    pend4 = pend.reshape(bsz, n_chunks, 1, WIDTH)
    spec = pl.BlockSpec((1, chunk, WIDTH), lambda b, c: (b, c, 0))
    return pl.pallas_call(
        functools.partial(_rwkv_scan_kernel, chunk=chunk, prec=prec),
        grid=(bsz, n_chunks),
        in_specs=[spec] * 7 + [
            pl.BlockSpec((1, 1, 1, WIDTH), lambda b, c: (b, c, 0, 0)),
            pl.BlockSpec((8, WIDTH), lambda b, c: (0, 0)),
        ],
        out_specs=spec,
        out_shape=jax.ShapeDtypeStruct((bsz, seq, WIDTH), F32),
        scratch_shapes=[pltpu.VMEM((PAIRS, LANES, LANES), F32)],
        compiler_params=_cparams(("parallel", "arbitrary")),
        name="rwkv_scan",
    )(rt, kt, kd, bd, v, g, bonus, pend4, ln)


def _merge_kernel(x_ref, oa_ref, ob_ref, ga_ref, gb_ref, wa_ref, wb_ref, wo_ref, g2_ref,
                  h_ref, xn_ref, acc_ref):
    j = pl.program_id(1)

    @pl.when(j == 0)
    def _():
        acc_ref[...] = x_ref[...]

    ya = jnp.dot(oa_ref[...].astype(BF16), wa_ref[...], preferred_element_type=F32)
    yb = jnp.dot(ob_ref[...].astype(BF16), wb_ref[...], preferred_element_type=F32)
    y = jax.nn.sigmoid(ga_ref[...]) * ya + jax.nn.sigmoid(gb_ref[...]) * yb
    acc_ref[...] += jnp.dot(y.astype(BF16), wo_ref[...], preferred_element_type=F32)

    @pl.when(j == pl.num_programs(1) - 1)
    def _():
        h = acc_ref[...]
        h_ref[...] = h
        ms = jnp.mean(h * h, axis=-1, keepdims=True)
        xn_ref[...] = _pack_halves(h * lax.rsqrt(ms + RMS_EPS) * g2_ref[...])


def _pack_halves(x):
    half = x.shape[1] // 2
    lo = lax.bitcast_convert_type(x[:, :half].astype(BF16).astype(F32), jnp.int32)
    hi = lax.bitcast_convert_type(x[:, half:].astype(BF16).astype(F32), jnp.int32)
    return lax.bitwise_or(lax.shift_right_logical(lo, jnp.int32(16)), hi)


def _unpack_halves(words):
    lo, hi = _unpack_words(words)
    return jnp.concatenate([lo, hi], axis=1)


def merge_out(x2d, oa, ob, p2d, w_proj_a, w_proj_b, w_out, norm2_g, *, tm=512):
    t, d = x2d.shape
    tn = WIDTH
    nj = d // tn
    g0 = COL_G_OFF // tn
    return pl.pallas_call(
        _merge_kernel,
        grid=(t // tm, nj),
        in_specs=[
            pl.BlockSpec((tm, d), lambda i, j: (i, 0)),
            pl.BlockSpec((tm, WIDTH), lambda i, j: (i, 0)),
            pl.BlockSpec((tm, WIDTH), lambda i, j: (i, 0)),
            pl.BlockSpec((tm, tn), lambda i, j: (i, g0 + j)),
            pl.BlockSpec((tm, tn), lambda i, j: (i, g0 + nj + j)),
            pl.BlockSpec((WIDTH, tn), lambda i, j: (0, j)),
            pl.BlockSpec((WIDTH, tn), lambda i, j: (0, j)),
            pl.BlockSpec((tn, d), lambda i, j: (j, 0)),
            pl.BlockSpec((1, d), lambda i, j: (0, 0)),
        ],
        out_specs=[pl.BlockSpec((tm, d), lambda i, j: (i, 0)), pl.BlockSpec((tm, d // 2), lambda i, j: (i, 0))],
        out_shape=[jax.ShapeDtypeStruct((t, d), F32), jax.ShapeDtypeStruct((t, d // 2), jnp.int32)],
        scratch_shapes=[pltpu.VMEM((tm, d), F32)],
        compiler_params=_cparams(("parallel", "arbitrary")),
        name="merge_out",
    )(x2d, oa, ob, p2d, p2d, w_proj_a.astype(BF16), w_proj_b.astype(BF16), w_out.astype(BF16),
      norm2_g.reshape(1, d))


PEER_HEADS = 8
PEER_NKEYS = 128
PEER_TOPK = 16
PEER_HALF = 128


def _topk_rows(s, k):
    n = s.shape[0]
    rows = lax.broadcasted_iota(jnp.int32, s.shape, 0).astype(F32)
    vals, ids = [], []
    for _ in range(k):
        m = jnp.max(s, axis=0, keepdims=True)
        first = jnp.min(jnp.where(s == m, rows, float(n)), axis=0, keepdims=True)
        vals.append(m)
        ids.append(first)
        s = jnp.where(rows == first, -jnp.inf, s)
    return jnp.concatenate(vals, axis=0), jnp.concatenate(ids, axis=0)


def _take_rows(table, ids):
    rows = lax.broadcasted_iota(jnp.int32, table.shape, 0).astype(F32)
    return jnp.sum(jnp.where(rows == ids, table, 0.0), axis=0, keepdims=True)


def _peer_route_kernel(xn_ref, wq_ref, sk_ref, idx_ref, gate_ref, *, prec):
    tt = xn_ref.shape[0]
    k = PEER_TOPK
    xn = _unpack_halves(xn_ref[...]) if xn_ref.dtype == jnp.int32 else xn_ref[...]
    q = jnp.dot(xn.astype(wq_ref.dtype), wq_ref[...], precision=prec, preferred_element_type=F32)
    nt = (((1,), (1,)), ((), ()))
    idx_rows, gate_rows = [], []
    half = k // 2
    for h in range(PEER_HEADS):
        tops = []
        for p in range(2):
            c0 = (h * 2 + p) * PEER_HALF
            s = lax.dot_general(sk_ref[h, p].astype(wq_ref.dtype), q[:, c0:c0 + PEER_HALF].astype(wq_ref.dtype),
                                nt, precision=prec, preferred_element_type=F32)
            tops.append(_topk_rows(s, k))
        (s0, i0), (s1, i1) = tops
        cs = [s0[0:1] + s1] + [s0[i:i + 1] + s1[0:half] for i in range(1, half)] + [s0[half:] + s1[0:1]]
        best_s, pos = _topk_rows(jnp.concatenate(cs, axis=0), k)
        mid = jnp.floor((pos - k) * (1.0 / half))
        end_mid = float(k + (half - 1) * half)
        i_rank = jnp.where(pos < k, 0.0, jnp.where(pos < end_mid, 1.0 + mid, pos - (end_mid - half)))
        j_rank = jnp.where(pos < k, pos, jnp.where(pos < end_mid, (pos - k) - half * mid, 0.0))
        ids = [_take_rows(i0, i_rank[n:n + 1]) * PEER_NKEYS + _take_rows(i1, j_rank[n:n + 1]) for n in range(k)]
        e = jnp.exp(best_s - best_s[0:1])
        gate_rows.append(e / jnp.sum(e, axis=0, keepdims=True))
        idx_rows.append(jnp.concatenate(ids, axis=0).astype(jnp.int32))
    idx_ref[...] = jnp.concatenate(idx_rows, axis=0).T
    gate_ref[...] = jnp.concatenate(gate_rows, axis=0).T


def peer_route(xn2d, peer_wq, peer_subkeys, *, tt=256, prec=None, wdtype=BF16):
    t, dx = xn2d.shape
    d, nq = peer_wq.shape
    n_sel = PEER_HEADS * PEER_TOPK
    return pl.pallas_call(
        functools.partial(_peer_route_kernel, prec=prec),
        grid=(t // tt,),
        in_specs=[
            pl.BlockSpec((tt, dx), lambda i: (i, 0)),
            pl.BlockSpec((d, nq), lambda i: (0, 0)),
            pl.BlockSpec((PEER_HEADS, 2, PEER_NKEYS, PEER_HALF), lambda i: (0, 0, 0, 0)),
        ],
        out_specs=[pl.BlockSpec((tt, n_sel), lambda i: (i, 0))] * 2,
        out_shape=[jax.ShapeDtypeStruct((t, n_sel), jnp.int32), jax.ShapeDtypeStruct((t, n_sel), F32)],
        compiler_params=_cparams(("parallel",)),
        name="peer_route",
    )(xn2d, peer_wq.astype(wdtype), peer_subkeys)


def _final_kernel(h_ref, y_ref, g_ref, o_ref):
    h = h_ref[...] + y_ref[...]
    ms = jnp.mean(h * h, axis=-1, keepdims=True)
    o_ref[...] = h * lax.rsqrt(ms + RMS_EPS) * g_ref[...]


def final_norm(h2d, y2d, g, *, tm=1024):
    t, d = h2d.shape
    spec = pl.BlockSpec((tm, d), lambda i: (i, 0))
    return pl.pallas_call(
        _final_kernel,
        grid=(t // tm,),
        in_specs=[spec, spec, pl.BlockSpec((1, d), lambda i: (0, 0))],
        out_specs=spec,
        out_shape=jax.ShapeDtypeStruct((t, d), F32),
        compiler_params=_cparams(("parallel",)),
        name="final_norm",
    )(h2d, y2d, g.reshape(1, d))


SC_CORES = 2
SC_SUBCORES = 16
SC_LANES = 16
SC_WORKERS = SC_CORES * SC_SUBCORES
PEER_SEL = PEER_HEADS * PEER_TOPK
PEER_ROWS = 32
PEER_PARTS = PEER_SEL // PEER_ROWS
PEER_NBUF = 4
PEER_GROUP = 32
PEER_BF16_RUN = 4


def _pack_rows(w):
    half = w.shape[1] // 2
    bits = lax.bitcast_convert_type(w.astype(BF16), jnp.uint16).astype(jnp.uint32)
    return lax.bitcast_convert_type(bits[:, :half] | (bits[:, half:] << 16), jnp.int32)


def _unpack_words(w):
    lo = lax.bitcast_convert_type(lax.shift_left(w, jnp.int32(16)), F32)
    hi = lax.bitcast_convert_type(lax.bitwise_and(w, jnp.int32(-65536)), F32)
    return lo, hi


def _packed_dot(a_words, b_words):
    from jax.experimental.pallas import tpu_sc as plsc
    prods = [plsc.bitcast(a, BF16) * plsc.bitcast(b, BF16) for a, b in zip(a_words, b_words)]
    while len(prods) > 1:
        prods = [prods[k] + prods[k + 1] for k in range(0, len(prods), 2)]
    return _unpack_words(plsc.bitcast(prods[0], jnp.int32))


def _sc_mesh():
    from jax.experimental.pallas import tpu_sc as plsc
    return plsc.VectorSubcoreMesh(core_axis_name="c", subcore_axis_name="s",
                                  num_cores=SC_CORES, num_subcores=SC_SUBCORES)


def _sc_loop(n, body, carry):
    from jax.experimental.pallas import tpu_sc as plsc
    return plsc.parallel_loop(0, n, carry=carry)(body)


def _worker_base(tokens_per_worker):
    return (lax.axis_index("s") * SC_CORES + lax.axis_index("c")) * tokens_per_worker


def _gather_compute_loop(table_hbm, idx_v, rows_v, sem, stage_v, out_row, osem, grp, compute):
    n_gathers = PEER_PARTS * grp
    ahead = PEER_NBUF - 1

    def gather(j, b):
        i = j // PEER_PARTS if isinstance(j, int) else lax.shift_right_logical(j, PEER_PARTS.bit_length() - 1)
        h = j % PEER_PARTS if isinstance(j, int) else lax.bitwise_and(j, PEER_PARTS - 1)
        ids = idx_v.at[i, pl.ds(pl.multiple_of(h * PEER_ROWS, PEER_ROWS), PEER_ROWS)]
        return pltpu.make_async_copy(table_hbm.at[ids], rows_v.at[b], sem.at[b])

    def put(i, slot):
        return pltpu.make_async_copy(stage_v.at[slot], out_row(i), osem.at[slot])

    for j in range(ahead):
        gather(j, j).start()

    @pl.loop(0, n_gathers)
    def _(j):
        b = lax.bitwise_and(j, PEER_NBUF - 1)
        h = lax.bitwise_and(j, PEER_PARTS - 1)
        i = lax.shift_right_logical(j, PEER_PARTS.bit_length() - 1)
        slot = lax.bitwise_and(i, 1)

        @pl.when((h == 0) & (i >= 2))
        def _():
            put(i - 2, slot).wait()

        @pl.when(j + ahead < n_gathers)
        def _():
            gather(j + ahead, lax.bitwise_and(j + ahead, PEER_NBUF - 1)).start()

        gather(j, b).wait()
        compute(i, h, b, slot)

        @pl.when(h == PEER_PARTS - 1)
        def _():
            put(i, slot).start()

    put(grp - 2, 0).wait()
    put(grp - 1, 1).wait()


def peer_expert_dots(x_packed, idx, u_packed):
    t, half = x_packed.shape
    n_chunks = half // SC_LANES
    tpw = t // SC_WORKERS
    grp = min(PEER_GROUP, tpw)
    rows_tog = 4

    def body(x_hbm, idx_hbm, u_hbm, out_hbm, idx_v, x_v, rows_v, ps_v, sem, osem):
        base = _worker_base(tpw)

        def compute(i, h, b, slot):
            @pl.loop(0, PEER_ROWS // rows_tog)
            def _(rg):
                r0 = rg * rows_tog
                accs = [[None, None] for _ in range(rows_tog)]
                for c0 in range(0, n_chunks, PEER_BF16_RUN):
                    ats = [pl.ds((c0 + k) * SC_LANES, SC_LANES) for k in range(PEER_BF16_RUN)]
                    xw = [x_v[i, at] for at in ats]
                    for r in range(rows_tog):
                        terms = _packed_dot([rows_v[b, r0 + r, at] for at in ats], xw)
                        for k, term in enumerate(terms):
                            accs[r][k] = term if accs[r][k] is None else accs[r][k] + term
                for r in range(rows_tog):
                    at = pl.ds(pl.multiple_of((h * PEER_ROWS + r0 + r) * SC_LANES, SC_LANES), SC_LANES)
                    ps_v[slot, at] = accs[r][0] + accs[r][1]

        @pl.loop(0, tpw // grp)
        def _(g):
            t0 = base + g * grp
            pltpu.sync_copy(idx_hbm.at[pl.ds(t0, grp)], idx_v)
            pltpu.sync_copy(x_hbm.at[pl.ds(t0, grp)], x_v)
            _gather_compute_loop(u_hbm, idx_v, rows_v, sem, ps_v, lambda i: out_hbm.at[t0 + i], osem, grp, compute)

    return pl.kernel(
        body,
        out_type=jax.ShapeDtypeStruct((t, PEER_SEL * SC_LANES), F32),
        mesh=_sc_mesh(),
        scratch_types=[
            pltpu.VMEM((grp, PEER_SEL), jnp.int32),
            pltpu.VMEM((grp, half), jnp.int32),
            pltpu.VMEM((PEER_NBUF, PEER_ROWS, half), jnp.int32),
            pltpu.VMEM((2, PEER_SEL * SC_LANES), F32),
            pltpu.SemaphoreType.DMA((PEER_NBUF,)),
            pltpu.SemaphoreType.DMA((2,)),
        ],
        compiler_params=pltpu.CompilerParams(needs_layout_passes=False),
        name="peer_expert_dots",
    )(x_packed, idx, u_packed)


def peer_expert_mix(hgx, idx, v_packed):
    t = hgx.shape[0]
    half = v_packed.shape[1]
    d = 2 * half
    tpw = t // SC_WORKERS
    grp = min(PEER_GROUP // 2, tpw)
    n_parts = 2
    cpp = half // SC_LANES // n_parts

    def body(hg_hbm, idx_hbm, v_hbm, out_hbm, idx_v, hg_v, rows_v, o_v2, sem, osem):
        base = _worker_base(tpw)

        def compute(i, h, b, slot):
            for part in range(n_parts):
                def rbody(rq, accs):
                    r0 = rq * PEER_BF16_RUN
                    s = [hg_v[i, pl.ds(pl.multiple_of((h * PEER_ROWS + r0 + k) * SC_LANES, SC_LANES), SC_LANES)]
                         for k in range(PEER_BF16_RUN)]
                    new = []
                    for c in range(cpp):
                        at = pl.ds((part * cpp + c) * SC_LANES, SC_LANES)
                        lo, hi = _packed_dot([rows_v[b, r0 + k, at] for k in range(PEER_BF16_RUN)], s)
                        new.append(accs[2 * c] + lo)
                        new.append(accs[2 * c + 1] + hi)
                    return tuple(new)

                accs = _sc_loop(PEER_ROWS // PEER_BF16_RUN, rbody,
                                tuple(jnp.zeros((SC_LANES,), F32) for _ in range(2 * cpp)))
                def store(overwrite):
                    for c in range(cpp):
                        lo_at = pl.ds((part * cpp + c) * SC_LANES, SC_LANES)
                        hi_at = pl.ds(half + (part * cpp + c) * SC_LANES, SC_LANES)
                        if overwrite:
                            o_v2[slot, lo_at] = accs[2 * c]
                            o_v2[slot, hi_at] = accs[2 * c + 1]
                        else:
                            o_v2[slot, lo_at] = o_v2[slot, lo_at] + accs[2 * c]
                            o_v2[slot, hi_at] = o_v2[slot, hi_at] + accs[2 * c + 1]

                pl.when(h == 0)(functools.partial(store, True))
                pl.when(h != 0)(functools.partial(store, False))

        @pl.loop(0, tpw // grp)
        def _(g):
            t0 = base + g * grp
            pltpu.sync_copy(idx_hbm.at[pl.ds(t0, grp)], idx_v)
            pltpu.sync_copy(hg_hbm.at[pl.ds(t0, grp)], hg_v)
            _gather_compute_loop(v_hbm, idx_v, rows_v, sem, o_v2, lambda i: out_hbm.at[t0 + i], osem, grp, compute)

    return pl.kernel(
        body,
        out_type=jax.ShapeDtypeStruct((t, d), F32),
        mesh=_sc_mesh(),
        scratch_types=[
            pltpu.VMEM((grp, PEER_SEL), jnp.int32),
            pltpu.VMEM((grp, PEER_SEL * SC_LANES), jnp.int32),
            pltpu.VMEM((PEER_NBUF, PEER_ROWS, half), jnp.int32),
            pltpu.VMEM((2, d), F32),
            pltpu.SemaphoreType.DMA((PEER_NBUF,)),
            pltpu.SemaphoreType.DMA((2,)),
        ],
        compiler_params=pltpu.CompilerParams(needs_layout_passes=False),
        name="peer_expert_mix",
    )(hgx, idx, v_packed)


def _peer_act_kernel(ps_ref, gate_ref, sum_ref, o_ref):
    ps = ps_ref[...]
    sel = sum_ref[...]
    hi = ps.astype(BF16)
    rest = ps - hi.astype(F32)
    mid = rest.astype(BF16)
    lo = (rest - mid.astype(F32)).astype(BF16)
    pre = (jnp.dot(hi, sel, preferred_element_type=F32) + jnp.dot(mid, sel, preferred_element_type=F32)
           + jnp.dot(lo, sel, preferred_element_type=F32))
    hg = 0.5 * pre * (1.0 + lax.erf(pre * (1.0 / math.sqrt(2.0)))) * gate_ref[...]
    spread = (((1,), (1,)), ((), ()))
    hgx = lax.dot_general(hg.astype(BF16), sel, spread, preferred_element_type=F32)
    bits = lax.bitcast_convert_type(hgx, jnp.int32)
    o_ref[...] = lax.bitwise_or(bits, lax.shift_right_logical(bits, jnp.int32(16)))


def peer_act(ps, gates, *, tm=512):
    t, n = ps.shape
    lane_sum = (jnp.arange(n)[:, None] // SC_LANES == jnp.arange(PEER_SEL)[None, :]).astype(BF16)
    return pl.pallas_call(
        _peer_act_kernel,
        grid=(t // tm,),
        in_specs=[
            pl.BlockSpec((tm, n), lambda i: (i, 0)),
            pl.BlockSpec((tm, PEER_SEL), lambda i: (i, 0)),
            pl.BlockSpec((n, PEER_SEL), lambda i: (0, 0)),
        ],
        out_specs=pl.BlockSpec((tm, n), lambda i: (i, 0)),
        out_shape=jax.ShapeDtypeStruct((t, n), jnp.int32),
        compiler_params=_cparams(("parallel",)),
        name="peer_act",
    )(ps, gates, lane_sum)


BATCH_GROUPS = 8


def kernel(x, norm1_g, w_in, rwkv_mu, w0, w_lora_up, a0, a_lora_up, g_lora_up, k_k, k_a, r_k, lnx_g, lnx_b,
           w_proj_a, w_proj_b, w_out, norm2_g, peer_wq, peer_subkeys, peer_u, peer_v, rel_bias, normf_g):
    bsz, seq, d = x.shape
    depth = norm1_g.shape[0]
    groups = BATCH_GROUPS if bsz % BATCH_GROUPS == 0 else 1
    gb = bsz // groups
    tg = gb * seq
    hs = [x[g * gb:(g + 1) * gb].reshape(tg, d) for g in range(groups)]
    for l in range(depth):
        w_pad = jnp.concatenate([
            w_in[l][:, :COL_A + COL_B_RAW],
            jnp.zeros((d, COL_B - COL_B_RAW), w_in.dtype),
            w_in[l][:, COL_A + COL_B_RAW:]], axis=1).astype(BF16)
        u_packed = _pack_rows(peer_u[l])
        v_packed = _pack_rows(peer_v[l])
        last = l == depth - 1

        def mix(pending, tie=None):
            g, h2d, ps, gates, idx = pending
            hgx = peer_act(ps, gates)
            if tie is not None:
                tie, hgx = lax.optimization_barrier((tie, hgx))
            return tie, (g, h2d, peer_expert_mix(hgx, idx, v_packed))

        def close(mixed):
            g, h2d, y2d = mixed
            hs[g] = final_norm(h2d, y2d, normf_g) if last else h2d + y2d

        pending = closing = None
        for g in range(groups):
            p2d = norm_proj(hs[g], norm1_g[l], w_pad)
            p3d = p2d.reshape(gb, seq, -1)
            oa = moba_attention(p3d, rel_bias)
            mixed = None
            if pending is not None:
                oa, mixed = mix(pending, oa)
            prep = rwkv_prep(p3d, rwkv_mu[l], w0[l], w_lora_up[l], a0[l], a_lora_up[l], g_lora_up[l],
                             k_k[l], k_a[l], r_k[l])
            if closing is not None:
                first, y2d = lax.optimization_barrier((prep[0], closing[2]))
                prep = (first,) + tuple(prep[1:])
                close(closing[:2] + (y2d,))
            ob = rwkv_scan(*prep, lnx_g[l], lnx_b[l])
            h2d, xn2 = merge_out(hs[g], oa.reshape(tg, WIDTH), ob.reshape(tg, WIDTH), p2d,
                                 w_proj_a[l], w_proj_b[l], w_out[l], norm2_g[l])
            idx, gates = peer_route(xn2, peer_wq[l], peer_subkeys[l])
            closing = None
            if mixed is not None:
                idx, y2d = lax.optimization_barrier((idx, mixed[2]))
                closing = mixed[:2] + (y2d,)
            pending = (g, h2d, peer_expert_dots(xn2, idx, u_packed), gates, idx)
        if closing is not None:
            close(closing)
        close(mix(pending)[1])
    return jnp.concatenate(hs, axis=0).reshape(bsz, seq, d)
```

```python
import functools
import math

import jax
import jax.numpy as jnp
from jax import lax
from jax.experimental import pallas as pl
from jax.experimental.pallas import tpu as pltpu

F32 = jnp.float32
BF16 = jnp.bfloat16
HI = lax.Precision.HIGHEST

LANES = 128
HEAD_DIM = 64
HEADS = 8
PAIRS = HEADS // 2
WIDTH = HEADS * HEAD_DIM
MOBA_BLOCK = 256
MOBA_TOPK = 3
MOBA_LO = 64
REL_BUCKETS = 32
REL_MAX_DIST = 128
DECAY_LORA = 64
AAA_LORA = 64
GATE_LORA = 160
GN_EPS = 64e-5
RMS_EPS = 1e-6
NEG = -1e30
RWKV_CHUNK = 64
COL_A = 3 * WIDTH
COL_B_RAW = 3 * WIDTH + DECAY_LORA + AAA_LORA + GATE_LORA
COL_B = 4 * WIDTH
COL_G_OFF = COL_A + COL_B
VMEM_LIMIT = 56 * 1024 * 1024


def _cparams(sem):
    return pltpu.CompilerParams(dimension_semantics=sem, vmem_limit_bytes=VMEM_LIMIT)


def _norm_proj_kernel(x_ref, g_ref, w_ref, o_ref, xn_ref):
    @pl.when(pl.program_id(1) == 0)
    def _():
        x = x_ref[...]
        ms = jnp.mean(x * x, axis=-1, keepdims=True)
        xn_ref[...] = (x * lax.rsqrt(ms + RMS_EPS) * g_ref[...]).astype(xn_ref.dtype)

    o_ref[...] = jnp.dot(xn_ref[...], w_ref[...], preferred_element_type=F32).astype(o_ref.dtype)


def norm_proj(x2d, g, w, *, row0=0, rows=None, tm=512, tn=512, out_dtype=F32):
    d = x2d.shape[1]
    t = x2d.shape[0] if rows is None else rows
    n = w.shape[1]
    r0 = row0 // tm
    return pl.pallas_call(
        _norm_proj_kernel,
        grid=(t // tm, n // tn),
        in_specs=[
            pl.BlockSpec((tm, d), lambda i, j: (r0 + i, 0)),
            pl.BlockSpec((1, d), lambda i, j: (0, 0)),
            pl.BlockSpec((d, tn), lambda i, j: (0, j)),
        ],
        out_specs=pl.BlockSpec((tm, tn), lambda i, j: (i, j)),
        out_shape=jax.ShapeDtypeStruct((t, n), out_dtype),
        scratch_shapes=[pltpu.VMEM((tm, d), w.dtype)],
        compiler_params=_cparams(("parallel", "arbitrary")),
        name="norm_proj",
    )(x2d, g.reshape(1, d), w)


def _rel_bucket(dist):
    n = jnp.maximum(dist, 0)
    max_exact = REL_BUCKETS // 2
    nf = jnp.maximum(n, 1).astype(F32)
    large = max_exact + (jnp.log(nf / max_exact) / math.log(REL_MAX_DIST / max_exact)
                         * (REL_BUCKETS - max_exact)).astype(jnp.int32)
    large = jnp.minimum(large, REL_BUCKETS - 1)
    return jnp.where(n < max_exact, n, large)


def _moba_kernel(q_ref, k_ref, v_ref, bown_ref, bprev_ref, bfar_ref, o_ref,
                 kb_ref, vb_ref, kbar_ref, *, n_blocks):
    qb = pl.program_id(2)
    blk = MOBA_BLOCK
    scale = 1.0 / math.sqrt(HEAD_DIM)

    rows2 = 2 * blk
    nt = (((1,), (1,)), ((), ()))

    @pl.when(qb == 0)
    def _():
        kbar_ref[...] = jnp.zeros_like(kbar_ref)
        lane_b = lax.broadcasted_iota(jnp.int32, (blk, LANES), 1)
        for n in range(n_blocks):
            kblk = k_ref[0, n * blk:(n + 1) * blk, :]
            kbar_ref[n:n + 1, :] = jnp.mean(kblk, axis=0, keepdims=True)
            kb_ref[n * blk:(n + 1) * blk, 0:LANES] = kblk.astype(BF16)
            kb_ref[n * blk:(n + 1) * blk, LANES:] = ((lane_b == n) | (lane_b == MOBA_LO + n)).astype(BF16)
        vb_ref[...] = v_ref[0].astype(BF16)

    q2 = q_ref[0]
    first = lax.broadcasted_iota(jnp.int32, (blk, LANES), 1) < HEAD_DIM
    qh = jnp.concatenate([jnp.where(first, q2, 0.0), jnp.where(first, 0.0, q2)], axis=0)
    lane = lax.broadcasted_iota(jnp.int32, (rows2, LANES), 1)
    rowi = lax.broadcasted_iota(jnp.int32, (rows2, LANES), 0)
    gate = lax.dot_general(qh.astype(BF16), kbar_ref[...].astype(BF16), nt, preferred_element_type=F32)
    g = jnp.where(lane < qb, gate, -jnp.inf)
    chosen = lane < 0
    lane_f = lane.astype(F32)
    for _ in range(MOBA_TOPK):
        m = jnp.max(g, axis=1, keepdims=True)
        idx = jnp.min(jnp.where(g == m, lane_f, float(LANES)), axis=1, keepdims=True)
        hit = (lane_f == idx) & (m > -jnp.inf)
        chosen = chosen | hit
        g = jnp.where(hit, -jnp.inf, g)
    nfar = qb - 1
    bfar = jnp.where(rowi < blk, bfar_ref[0, 0:1, 0:1], bfar_ref[1, 0:1, 0:1])
    bhi = bfar.astype(BF16).astype(F32)
    madd = jnp.where(lane < nfar, jnp.where(chosen, bhi, NEG),
                     jnp.where(lane == nfar, jnp.where(chosen, 0.0, NEG),
                               jnp.where((lane >= MOBA_LO) & (lane - MOBA_LO < nfar), bfar - bhi, 0.0)))
    q_aug = jnp.concatenate([(qh * scale).astype(BF16), madd.astype(BF16)], axis=1)

    prev0 = pl.multiple_of(jnp.maximum(nfar, 0) * blk, blk)
    own0 = pl.multiple_of(qb * blk, blk)
    s_prev = (lax.dot_general(q_aug, kb_ref[pl.ds(prev0, blk), :], nt, preferred_element_type=F32)
              + bprev_ref[...].reshape(rows2, blk) + jnp.where(qb > 0, 0.0, NEG))
    s_own = (lax.dot_general(q_aug, kb_ref[pl.ds(own0, blk), :], nt, preferred_element_type=F32)
             + bown_ref[...].reshape(rows2, blk))
    r = lax.broadcasted_iota(jnp.int32, (rows2, blk), 0)
    c = lax.broadcasted_iota(jnp.int32, (rows2, blk), 1)
    s_own = jnp.where(lax.bitwise_and(r, blk - 1) >= c, s_own, NEG)
    s = jnp.concatenate([s_prev, s_own], axis=1)
    m_i = jnp.max(s, axis=1, keepdims=True)
    p = jnp.exp(s - m_i)
    l_i = jnp.sum(p, axis=1, keepdims=True)
    v0 = jnp.concatenate([vb_ref[pl.ds(prev0, blk), :], vb_ref[pl.ds(own0, blk), :]], axis=0)
    acc = jnp.dot(p.astype(BF16), v0, preferred_element_type=F32)

    def body(it, carry):
        m_i, l_i, acc = carry
        k0 = pl.multiple_of(it * rows2, rows2)
        s = lax.dot_general(q_aug, kb_ref[pl.ds(k0, rows2), :], nt, preferred_element_type=F32)
        tail = jnp.where(2 * it + 1 < nfar, 0.0, NEG)
        s = jnp.concatenate([s[:, :blk], s[:, blk:] + tail], axis=1)
        m_new = jnp.maximum(m_i, jnp.max(s, axis=1, keepdims=True))
        alpha = jnp.exp(m_i - m_new)
        p = jnp.exp(s - m_new)
        l_new = alpha * l_i + jnp.sum(p, axis=1, keepdims=True)
        acc_new = alpha * acc + jnp.dot(p.astype(BF16), vb_ref[pl.ds(k0, rows2), :], preferred_element_type=F32)
        return m_new, l_new, acc_new

    m_i, l_i, acc = lax.fori_loop(0, (jnp.maximum(nfar, 0) + 1) // 2, body, (m_i, l_i, acc))
    out = acc / l_i
    o_ref[0] = jnp.where(first, out[:blk], out[blk:])


def moba_attention(p3d, rel_bias):
    bsz, seq, _ = p3d.shape
    blk = MOBA_BLOCK
    n_blocks = seq // blk
    span = 2 * blk
    by_dist = rel_bias[:, _rel_bucket(jnp.arange(span))].astype(F32)
    shift = jnp.arange(span)

    def toeplitz(c):
        k = jnp.where(shift < blk, shift, shift - span)
        s = by_dist[:, jnp.clip(c - k, 0, span - 1)]
        tiled = jnp.tile(s, (1, blk))[:, :blk * (span - 1)]
        return tiled.reshape(HEADS, blk, span - 1)[:, :, :blk]

    bias_own = toeplitz(0)
    bias_prev = toeplitz(blk)
    bias_far = jnp.broadcast_to(rel_bias[:, REL_BUCKETS - 1].astype(F32)[:, None, None], (HEADS, 8, LANES))
    kern = functools.partial(_moba_kernel, n_blocks=n_blocks)
    return pl.pallas_call(
        kern,
        grid=(bsz, PAIRS, n_blocks),
        in_specs=[
            pl.BlockSpec((1, blk, LANES), lambda b, h, i: (b, i, h)),
            pl.BlockSpec((1, seq, LANES), lambda b, h, i: (b, 0, PAIRS + h)),
            pl.BlockSpec((1, seq, LANES), lambda b, h, i: (b, 0, 2 * PAIRS + h)),
            pl.BlockSpec((2, blk, blk), lambda b, h, i: (h, 0, 0)),
            pl.BlockSpec((2, blk, blk), lambda b, h, i: (h, 0, 0)),
            pl.BlockSpec((2, 8, LANES), lambda b, h, i: (h, 0, 0)),
        ],
        out_specs=pl.BlockSpec((1, blk, LANES), lambda b, h, i: (b, i, h)),
        out_shape=jax.ShapeDtypeStruct((bsz, seq, WIDTH), F32),
        scratch_shapes=[
            pltpu.VMEM((seq, 2 * LANES), BF16),
            pltpu.VMEM((seq, LANES), BF16),
            pltpu.VMEM((LANES, LANES), F32),
        ],
        compiler_params=_cparams(("parallel", "parallel", "arbitrary")),
        name="moba",
    )(p3d, p3d, p3d, bias_own, bias_prev, bias_far)


def _shifted(x, carry_row):
    rows = lax.broadcasted_iota(jnp.int32, x.shape, 0)
    return jnp.where(rows == 0, carry_row, pltpu.roll(x, 1, axis=0))


def _rwkv_prep_kernel(pr_ref, pk_ref, pv_ref, pl_ref, mu_ref, vec_ref, ww_ref, wa_ref, wg_ref,
                      bd_ref, tri_ref,
                      rt_ref, kt_ref, kd_ref, bd_out_ref, v_ref, g_ref, bonus_ref, pend_ref,
                      carry_ref, *, chunk):
    @pl.when(pl.program_id(1) == 0)
    def _():
        carry_ref[...] = jnp.zeros_like(carry_ref)

    def mix(ref, j):
        x = ref[0]
        mu = mu_ref[0:1, j * WIDTH:(j + 1) * WIDTH]
        prev = _shifted(x, carry_ref[0:1, j * WIDTH:(j + 1) * WIDTH])
        carry_ref[0:1, j * WIDTH:(j + 1) * WIDTH] = x[x.shape[0] - 1:, :]
        return x + mu * (prev - x)

    r = mix(pr_ref, 0)
    k = mix(pk_ref, 1)
    v = mix(pv_ref, 2)
    lo = mix(pl_ref, 3)
    w0, a0, k_k, k_a, r_k = (vec_ref[i:i + 1, :] for i in range(5))
    xwa = lo[:, 0:LANES]
    xg = lo[:, LANES:3 * LANES]
    lw = jnp.dot(jnp.tanh(xwa), ww_ref[...], precision=HI, preferred_element_type=F32)
    la = jnp.dot(xwa, wa_ref[...], precision=HI, preferred_element_type=F32)
    g = jnp.dot(jax.nn.sigmoid(xg), wg_ref[...], precision=HI, preferred_element_type=F32)
    z = -(w0 + lw)
    softplus = jnp.maximum(z, 0.0) + jnp.log(1.0 + jnp.exp(-jnp.abs(z)))
    logw = -jnp.exp(-softplus - 0.5)
    a = jax.nn.sigmoid(a0 + la)
    kk = k * k_k
    ss = jnp.dot(kk * kk, bd_ref[...], precision=HI, preferred_element_type=F32)
    kk = kk / jnp.maximum(jnp.sqrt(ss), 1e-12)
    k2 = k * (1.0 + (a - 1.0) * k_a)
    rk = jnp.dot(r * k2 * r_k, bd_ref[...], precision=HI, preferred_element_type=F32)
    cs = jnp.dot(tri_ref[...], logw, precision=HI, preferred_element_type=F32)
    e_pos = jnp.exp(cs)
    e_neg = jnp.exp(-cs)
    rt_ref[0] = r * e_pos
    kt_ref[0] = kk * jnp.exp(cs - logw)
    kd_ref[0] = k2 * e_neg
    bd_out_ref[0] = kk * a * e_neg
    v_ref[0] = v
    g_ref[0] = g
    bonus_ref[0] = rk * v
    ts = e_pos.shape[0]
    for c in range(ts // chunk):
        pend_ref[0, c:c + 1, :] = e_pos[(c + 1) * chunk - 1:(c + 1) * chunk, :]


def rwkv_prep(p3d, rwkv_mu, w0, w_lora_up, a0, a_lora_up, g_lora_up, k_k, k_a, r_k, *, ts=512):
    bsz, seq, _ = p3d.shape
    chunk = RWKV_CHUNK
    ts = min(ts, seq)
    mu = jnp.pad(rwkv_mu, (0, COL_B - COL_B_RAW)).reshape(1, COL_B)
    vec = jnp.stack([w0, a0, k_k, k_a, r_k.reshape(-1)] + [jnp.zeros_like(w0)] * 3).astype(F32)
    ww = jnp.zeros((LANES, WIDTH), F32).at[:DECAY_LORA].set(w_lora_up)
    wa = jnp.zeros((LANES, WIDTH), F32).at[DECAY_LORA:DECAY_LORA + AAA_LORA].set(a_lora_up)
    wg = jnp.zeros((2 * LANES, WIDTH), F32).at[:GATE_LORA].set(g_lora_up)
    hid = jnp.arange(WIDTH) // HEAD_DIM
    bd = (hid[:, None] == hid[None, :]).astype(F32)
    tix = jnp.arange(ts)
    tri = ((tix[:, None] // chunk == tix[None, :] // chunk) & (tix[None, :] <= tix[:, None])).astype(F32)
    c0 = COL_A // WIDTH
    big = jax.ShapeDtypeStruct((bsz, seq, WIDTH), F32)
    wspec = lambda shape: pl.BlockSpec(shape, lambda b, i: (0, 0))
    ospec = pl.BlockSpec((1, ts, WIDTH), lambda b, i: (b, i, 0))
    return pl.pallas_call(
        functools.partial(_rwkv_prep_kernel, chunk=chunk),
        grid=(bsz, seq // ts),
        in_specs=[
            pl.BlockSpec((1, ts, WIDTH), lambda b, i: (b, i, c0)),
            pl.BlockSpec((1, ts, WIDTH), lambda b, i: (b, i, c0 + 1)),
            pl.BlockSpec((1, ts, WIDTH), lambda b, i: (b, i, c0 + 2)),
            pl.BlockSpec((1, ts, WIDTH), lambda b, i: (b, i, c0 + 3)),
            wspec((1, COL_B)), wspec((8, WIDTH)), wspec((LANES, WIDTH)), wspec((LANES, WIDTH)),
            wspec((2 * LANES, WIDTH)), wspec((WIDTH, WIDTH)), wspec((ts, ts)),
        ],
        out_specs=[ospec] * 7 + [pl.BlockSpec((1, ts // chunk, WIDTH), lambda b, i: (b, i, 0))],
        out_shape=[big] * 7 + [jax.ShapeDtypeStruct((bsz, seq // chunk, WIDTH), F32)],
        scratch_shapes=[pltpu.VMEM((8, COL_B), F32)],
        compiler_params=_cparams(("parallel", "arbitrary")),
        name="rwkv_prep",
    )(p3d, p3d, p3d, p3d, mu, vec, ww, wa, wg, bd, tri)


def _rwkv_scan_kernel(rt_ref, kt_ref, kd_ref, bd_ref, v_ref, g_ref, bonus_ref, pend_ref, ln_ref, o_ref,
                      state_ref, *, chunk, prec):
    @pl.when(pl.program_id(1) == 0)
    def _():
        state_ref[...] = jnp.zeros_like(state_ref)

    c2 = 2 * chunk
    lane = lax.broadcasted_iota(jnp.int32, (chunk, LANES), 1)
    first = lane < HEAD_DIM
    row = lax.broadcasted_iota(jnp.int32, (c2, c2), 0)
    col = lax.broadcasted_iota(jnp.int32, (c2, c2), 1)
    eye = (row == col).astype(F32)
    hrow = lax.broadcasted_iota(jnp.int32, (LANES, LANES), 0) // HEAD_DIM
    hcol = lax.broadcasted_iota(jnp.int32, (LANES, LANES), 1) // HEAD_DIM
    head_mean = jnp.where(hrow == hcol, 1.0 / HEAD_DIM, 0.0).astype(F32)
    nt = (((1,), (1,)), ((), ()))
    tn = (((0,), (0,)), ((), ()))
    dot = functools.partial(jnp.dot, precision=prec, preferred_element_type=F32)
    dotg = functools.partial(lax.dot_general, precision=prec, preferred_element_type=F32)

    def stack(x):
        return jnp.concatenate([jnp.where(first, x, 0.0), jnp.where(first, 0.0, x)], axis=0)

    pairs = range(PAIRS)
    sls = [slice(hp * LANES, (hp + 1) * LANES) for hp in pairs]
    rs, ks, kds, bs, vs = ([stack(ref[0, :, sl]) for sl in sls] for ref in (rt_ref, kt_ref, kd_ref, bd_ref, v_ref))
    hts = [state_ref[hp] for hp in pairs]
    big = [dotg(jnp.concatenate([ks[hp], rs[hp]], axis=0), jnp.concatenate([bs[hp], kds[hp]], axis=0), nt)
           for hp in pairs]
    a_b = [jnp.where(row > col, big[hp][0:c2, 0:c2], 0.0) for hp in pairs]
    a_k = [jnp.where(row > col, big[hp][0:c2, c2:], 0.0) for hp in pairs]
    a_rb = [jnp.where(row >= col, big[hp][c2:, 0:c2], 0.0) for hp in pairs]
    a_rk = [jnp.where(row >= col, big[hp][c2:, c2:], 0.0) for hp in pairs]
    kh = [dotg(jnp.concatenate([ks[hp], rs[hp]], axis=0), hts[hp], nt) for hp in pairs]
    av = [dot(jnp.concatenate([a_k[hp], a_rk[hp]], axis=0), vs[hp]) for hp in pairs]
    vk = [dotg(vs[hp], kds[hp], tn) for hp in pairs]
    inv = [eye - a_b[hp] for hp in pairs]
    pw = [dot(a_b[hp], a_b[hp]) for hp in pairs]
    n_sq = int(math.log2(chunk)) - 1
    for lvl in range(n_sq):
        if lvl + 1 < n_sq:
            both = [dot(jnp.concatenate([inv[hp], pw[hp]], axis=0), pw[hp]) for hp in pairs]
            inv = [inv[hp] + both[hp][0:c2] for hp in pairs]
            pw = [both[hp][c2:] for hp in pairs]
        else:
            inv = [inv[hp] + dot(inv[hp], pw[hp]) for hp in pairs]
    us = [dot(inv[hp], kh[hp][0:c2] + av[hp][0:c2]) for hp in pairs]
    ub = [dotg(us[hp], bs[hp], tn) for hp in pairs]
    au = [dot(a_rb[hp], us[hp]) for hp in pairs]
    for hp in pairs:
        sl = sls[hp]
        pend = pend_ref[0, 0, 0:1, sl]
        state_ref[hp] = (hts[hp] + vk[hp] - ub[hp]) * pend
        os_ = kh[hp][c2:] + av[hp][c2:] - au[hp]
        o = os_[0:chunk] + os_[chunk:]
        mu = jnp.dot(o, head_mean, precision=HI, preferred_element_type=F32)
        d = o - mu
        var = jnp.dot(d * d, head_mean, precision=HI, preferred_element_type=F32)
        on = d * lax.rsqrt(var + GN_EPS) * ln_ref[0:1, sl] + ln_ref[1:2, sl]
        o_ref[0, :, sl] = (on + bonus_ref[0, :, sl]) * g_ref[0, :, sl]


def rwkv_scan(rt, kt, kd, bd, v, g, bonus, pend, lnx_g, lnx_b, *, prec=None):
    bsz, seq, _ = rt.shape
    chunk = RWKV_CHUNK
    n_chunks = seq // chunk
    ln = jnp.stack([lnx_g, lnx_b] + [jnp.zeros_like(lnx_g)] * 6).astype(F32)
    pend4 = pend.reshape(bsz, n_chunks, 1, WIDTH)
    spec = pl.BlockSpec((1, chunk, WIDTH), lambda b, c: (b, c, 0))
    return pl.pallas_call(
        functools.partial(_rwkv_scan_kernel, chunk=chunk, prec=prec),
        grid=(bsz, n_chunks),
        in_specs=[spec] * 7 + [
            pl.BlockSpec((1, 1, 1, WIDTH), lambda b, c: (b, c, 0, 0)),
            pl.BlockSpec((8, WIDTH), lambda b, c: (0, 0)),
        ],
        out_specs=spec,
        out_shape=jax.ShapeDtypeStruct((bsz, seq, WIDTH), F32),
        scratch_shapes=[pltpu.VMEM((PAIRS, LANES, LANES), F32)],
        compiler_params=_cparams(("parallel", "arbitrary")),
        name="rwkv_scan",
    )(rt, kt, kd, bd, v, g, bonus, pend4, ln)


def _merge_kernel(x_ref, oa_ref, ob_ref, ga_ref, gb_ref, wa_ref, wb_ref, wo_ref, g2_ref,
                  h_ref, xn_ref, acc_ref):
    j = pl.program_id(1)

    @pl.when(j == 0)
    def _():
        acc_ref[...] = x_ref[...]

    ya = jnp.dot(oa_ref[...].astype(BF16), wa_ref[...], preferred_element_type=F32)
    yb = jnp.dot(ob_ref[...].astype(BF16), wb_ref[...], preferred_element_type=F32)
    y = jax.nn.sigmoid(ga_ref[...]) * ya + jax.nn.sigmoid(gb_ref[...]) * yb
    acc_ref[...] += jnp.dot(y.astype(BF16), wo_ref[...], preferred_element_type=F32)

    @pl.when(j == pl.num_programs(1) - 1)
    def _():
        h = acc_ref[...]
        h_ref[...] = h
        ms = jnp.mean(h * h, axis=-1, keepdims=True)
        xn_ref[...] = _pack_halves(h * lax.rsqrt(ms + RMS_EPS) * g2_ref[...])


def _pack_halves(x):
    half = x.shape[1] // 2
    lo = lax.bitcast_convert_type(x[:, :half].astype(BF16).astype(F32), jnp.int32)
    hi = lax.bitcast_convert_type(x[:, half:].astype(BF16).astype(F32), jnp.int32)
    return lax.bitwise_or(lax.shift_right_logical(lo, jnp.int32(16)), hi)


def _unpack_halves(words):
    lo, hi = _unpack_words(words)
    return jnp.concatenate([lo, hi], axis=1)


def merge_out(x2d, oa, ob, p2d, w_proj_a, w_proj_b, w_out, norm2_g, *, row0=0, tm=512):
    t, d = oa.shape[0], x2d.shape[1]
    r0 = row0 // tm
    tn = WIDTH
    nj = d // tn
    g0 = COL_G_OFF // tn
    return pl.pallas_call(
        _merge_kernel,
        grid=(t // tm, nj),
        in_specs=[
            pl.BlockSpec((tm, d), lambda i, j: (r0 + i, 0)),
            pl.BlockSpec((tm, WIDTH), lambda i, j: (i, 0)),
            pl.BlockSpec((tm, WIDTH), lambda i, j: (i, 0)),
            pl.BlockSpec((tm, tn), lambda i, j: (i, g0 + j)),
            pl.BlockSpec((tm, tn), lambda i, j: (i, g0 + nj + j)),
            pl.BlockSpec((WIDTH, tn), lambda i, j: (0, j)),
            pl.BlockSpec((WIDTH, tn), lambda i, j: (0, j)),
            pl.BlockSpec((tn, d), lambda i, j: (j, 0)),
            pl.BlockSpec((1, d), lambda i, j: (0, 0)),
        ],
        out_specs=[pl.BlockSpec((tm, d), lambda i, j: (i, 0)), pl.BlockSpec((tm, d // 2), lambda i, j: (i, 0))],
        out_shape=[jax.ShapeDtypeStruct((t, d), F32), jax.ShapeDtypeStruct((t, d // 2), jnp.int32)],
        scratch_shapes=[pltpu.VMEM((tm, d), F32)],
        compiler_params=_cparams(("parallel", "arbitrary")),
        name="merge_out",
    )(x2d, oa, ob, p2d, p2d, w_proj_a.astype(BF16), w_proj_b.astype(BF16), w_out.astype(BF16),
      norm2_g.reshape(1, d))


PEER_HEADS = 8
PEER_NKEYS = 128
PEER_TOPK = 16
PEER_HALF = 128


def _topk_rows(s, k):
    n = s.shape[0]
    rows = lax.broadcasted_iota(jnp.int32, s.shape, 0).astype(F32)
    vals, ids = [], []
    for _ in range(k):
        m = jnp.max(s, axis=0, keepdims=True)
        first = jnp.min(jnp.where(s == m, rows, float(n)), axis=0, keepdims=True)
        vals.append(m)
        ids.append(first)
        s = jnp.where(rows == first, -jnp.inf, s)
    return jnp.concatenate(vals, axis=0), jnp.concatenate(ids, axis=0)


def _take_rows(table, ids):
    rows = lax.broadcasted_iota(jnp.int32, table.shape, 0).astype(F32)
    return jnp.sum(jnp.where(rows == ids, table, 0.0), axis=0, keepdims=True)


def _peer_route_kernel(xn_ref, wq_ref, sk_ref, idx_ref, gate_ref, *, prec):
    tt = xn_ref.shape[0]
    k = PEER_TOPK
    xn = _unpack_halves(xn_ref[...]) if xn_ref.dtype == jnp.int32 else xn_ref[...]
    q = jnp.dot(xn.astype(wq_ref.dtype), wq_ref[...], precision=prec, preferred_element_type=F32)
    nt = (((1,), (1,)), ((), ()))
    idx_rows, gate_rows = [], []
    half = k // 2
    for h in range(PEER_HEADS):
        tops = []
        for p in range(2):
            c0 = (h * 2 + p) * PEER_HALF
            s = lax.dot_general(sk_ref[h, p].astype(wq_ref.dtype), q[:, c0:c0 + PEER_HALF].astype(wq_ref.dtype),
                                nt, precision=prec, preferred_element_type=F32)
            tops.append(_topk_rows(s, k))
        (s0, i0), (s1, i1) = tops
        cs = [s0[0:1] + s1] + [s0[i:i + 1] + s1[0:half] for i in range(1, half)] + [s0[half:] + s1[0:1]]
        best_s, pos = _topk_rows(jnp.concatenate(cs, axis=0), k)
        mid = jnp.floor((pos - k) * (1.0 / half))
        end_mid = float(k + (half - 1) * half)
        i_rank = jnp.where(pos < k, 0.0, jnp.where(pos < end_mid, 1.0 + mid, pos - (end_mid - half)))
        j_rank = jnp.where(pos < k, pos, jnp.where(pos < end_mid, (pos - k) - half * mid, 0.0))
        ids = [_take_rows(i0, i_rank[n:n + 1]) * PEER_NKEYS + _take_rows(i1, j_rank[n:n + 1]) for n in range(k)]
        e = jnp.exp(best_s - best_s[0:1])
        gate_rows.append(e / jnp.sum(e, axis=0, keepdims=True))
        idx_rows.append(jnp.concatenate(ids, axis=0).astype(jnp.int32))
    idx_ref[...] = jnp.concatenate(idx_rows, axis=0).T
    gate_ref[...] = jnp.concatenate(gate_rows, axis=0).T


def peer_route(xn2d, peer_wq, peer_subkeys, *, tt=256, prec=None, wdtype=BF16):
    t, dx = xn2d.shape
    d, nq = peer_wq.shape
    n_sel = PEER_HEADS * PEER_TOPK
    return pl.pallas_call(
        functools.partial(_peer_route_kernel, prec=prec),
        grid=(t // tt,),
        in_specs=[
            pl.BlockSpec((tt, dx), lambda i: (i, 0)),
            pl.BlockSpec((d, nq), lambda i: (0, 0)),
            pl.BlockSpec((PEER_HEADS, 2, PEER_NKEYS, PEER_HALF), lambda i: (0, 0, 0, 0)),
        ],
        out_specs=[pl.BlockSpec((tt, n_sel), lambda i: (i, 0))] * 2,
        out_shape=[jax.ShapeDtypeStruct((t, n_sel), jnp.int32), jax.ShapeDtypeStruct((t, n_sel), F32)],
        compiler_params=_cparams(("parallel",)),
        name="peer_route",
    )(xn2d, peer_wq.astype(wdtype), peer_subkeys)


def _final_kernel(h_ref, y_ref, g_ref, *rest):
    o_ref = rest[-1]
    h = h_ref[...] + y_ref[...]
    ms = jnp.mean(h * h, axis=-1, keepdims=True)
    o_ref[...] = h * lax.rsqrt(ms + RMS_EPS) * g_ref[...]


def final_norm(h2d, y2d, g, *, out=None, row0=0, total_rows=None, tm=1024):
    t, d = h2d.shape
    total = t if total_rows is None else total_rows
    r0 = row0 // tm
    spec = pl.BlockSpec((tm, d), lambda i: (i, 0))
    in_specs = [spec, spec, pl.BlockSpec((1, d), lambda i: (0, 0))]
    args = [h2d, y2d, g.reshape(1, d)]
    aliases = {}
    if out is not None:
        in_specs.append(pl.BlockSpec(memory_space=pl.ANY))
        args.append(out)
        aliases = {3: 0}
    return pl.pallas_call(
        _final_kernel,
        grid=(t // tm,),
        in_specs=in_specs,
        out_specs=pl.BlockSpec((tm, d), lambda i: (r0 + i, 0)),
        out_shape=jax.ShapeDtypeStruct((total, d), F32),
        input_output_aliases=aliases,
        compiler_params=_cparams(("parallel",)),
        name="final_norm",
    )(*args)


SC_CORES = 2
SC_SUBCORES = 16
SC_LANES = 16
SC_WORKERS = SC_CORES * SC_SUBCORES
PEER_SEL = PEER_HEADS * PEER_TOPK
PEER_ROWS = 32
PEER_PARTS = PEER_SEL // PEER_ROWS
PEER_NBUF = 4
PEER_GROUP = 32
PEER_BF16_RUN = 4


def _pack_rows(w):
    half = w.shape[1] // 2
    bits = lax.bitcast_convert_type(w.astype(BF16), jnp.uint16).astype(jnp.uint32)
    return lax.bitcast_convert_type(bits[:, :half] | (bits[:, half:] << 16), jnp.int32)


def _unpack_words(w):
    lo = lax.bitcast_convert_type(lax.shift_left(w, jnp.int32(16)), F32)
    hi = lax.bitcast_convert_type(lax.bitwise_and(w, jnp.int32(-65536)), F32)
    return lo, hi


def _packed_dot(a_words, b_words):
    from jax.experimental.pallas import tpu_sc as plsc
    prods = [plsc.bitcast(a, BF16) * plsc.bitcast(b, BF16) for a, b in zip(a_words, b_words)]
    while len(prods) > 1:
        prods = [prods[k] + prods[k + 1] for k in range(0, len(prods), 2)]
    return _unpack_words(plsc.bitcast(prods[0], jnp.int32))


def _sc_mesh():
    from jax.experimental.pallas import tpu_sc as plsc
    return plsc.VectorSubcoreMesh(core_axis_name="c", subcore_axis_name="s",
                                  num_cores=SC_CORES, num_subcores=SC_SUBCORES)


def _sc_loop(n, body, carry):
    from jax.experimental.pallas import tpu_sc as plsc
    return plsc.parallel_loop(0, n, carry=carry)(body)


def _worker_base(tokens_per_worker):
    return (lax.axis_index("s") * SC_CORES + lax.axis_index("c")) * tokens_per_worker


def _gather_compute_loop(table_hbm, idx_v, rows_v, sem, stage_v, out_row, osem, grp, compute):
    n_gathers = PEER_PARTS * grp
    ahead = PEER_NBUF - 1

    def gather(j, b):
        i = j // PEER_PARTS if isinstance(j, int) else lax.shift_right_logical(j, PEER_PARTS.bit_length() - 1)
        h = j % PEER_PARTS if isinstance(j, int) else lax.bitwise_and(j, PEER_PARTS - 1)
        ids = idx_v.at[i, pl.ds(pl.multiple_of(h * PEER_ROWS, PEER_ROWS), PEER_ROWS)]
        return pltpu.make_async_copy(table_hbm.at[ids], rows_v.at[b], sem.at[b])

    def put(i, slot):
        return pltpu.make_async_copy(stage_v.at[slot], out_row(i), osem.at[slot])

    for j in range(ahead):
        gather(j, j).start()

    @pl.loop(0, n_gathers)
    def _(j):
        b = lax.bitwise_and(j, PEER_NBUF - 1)
        h = lax.bitwise_and(j, PEER_PARTS - 1)
        i = lax.shift_right_logical(j, PEER_PARTS.bit_length() - 1)
        slot = lax.bitwise_and(i, 1)

        @pl.when((h == 0) & (i >= 2))
        def _():
            put(i - 2, slot).wait()

        @pl.when(j + ahead < n_gathers)
        def _():
            gather(j + ahead, lax.bitwise_and(j + ahead, PEER_NBUF - 1)).start()

        gather(j, b).wait()
        compute(i, h, b, slot)

        @pl.when(h == PEER_PARTS - 1)
        def _():
            put(i, slot).start()

    put(grp - 2, 0).wait()
    put(grp - 1, 1).wait()


def peer_expert_dots(x_packed, idx, u_packed):
    t, half = x_packed.shape
    n_chunks = half // SC_LANES
    tpw = t // SC_WORKERS
    grp = min(PEER_GROUP, tpw)
    rows_tog = 4

    def body(x_hbm, idx_hbm, u_hbm, out_hbm, idx_v, x_v, rows_v, ps_v, sem, osem):
        base = _worker_base(tpw)

        def compute(i, h, b, slot):
            @pl.loop(0, PEER_ROWS // rows_tog)
            def _(rg):
                r0 = rg * rows_tog
                accs = [[None, None] for _ in range(rows_tog)]
                for c0 in range(0, n_chunks, PEER_BF16_RUN):
                    ats = [pl.ds((c0 + k) * SC_LANES, SC_LANES) for k in range(PEER_BF16_RUN)]
                    xw = [x_v[i, at] for at in ats]
                    for r in range(rows_tog):
                        terms = _packed_dot([rows_v[b, r0 + r, at] for at in ats], xw)
                        for k, term in enumerate(terms):
                            accs[r][k] = term if accs[r][k] is None else accs[r][k] + term
                for r in range(rows_tog):
                    at = pl.ds(pl.multiple_of((h * PEER_ROWS + r0 + r) * SC_LANES, SC_LANES), SC_LANES)
                    ps_v[slot, at] = accs[r][0] + accs[r][1]

        @pl.loop(0, tpw // grp)
        def _(g):
            t0 = base + g * grp
            pltpu.sync_copy(idx_hbm.at[pl.ds(t0, grp)], idx_v)
            pltpu.sync_copy(x_hbm.at[pl.ds(t0, grp)], x_v)
            _gather_compute_loop(u_hbm, idx_v, rows_v, sem, ps_v, lambda i: out_hbm.at[t0 + i], osem, grp, compute)

    return pl.kernel(
        body,
        out_type=jax.ShapeDtypeStruct((t, PEER_SEL * SC_LANES), F32),
        mesh=_sc_mesh(),
        scratch_types=[
            pltpu.VMEM((grp, PEER_SEL), jnp.int32),
            pltpu.VMEM((grp, half), jnp.int32),
            pltpu.VMEM((PEER_NBUF, PEER_ROWS, half), jnp.int32),
            pltpu.VMEM((2, PEER_SEL * SC_LANES), F32),
            pltpu.SemaphoreType.DMA((PEER_NBUF,)),
            pltpu.SemaphoreType.DMA((2,)),
        ],
        compiler_params=pltpu.CompilerParams(needs_layout_passes=False),
        name="peer_expert_dots",
    )(x_packed, idx, u_packed)


def peer_expert_mix(hgx, idx, v_packed):
    t = hgx.shape[0]
    half = v_packed.shape[1]
    d = 2 * half
    tpw = t // SC_WORKERS
    grp = min(PEER_GROUP // 2, tpw)
    n_parts = 2
    cpp = half // SC_LANES // n_parts

    def body(hg_hbm, idx_hbm, v_hbm, out_hbm, idx_v, hg_v, rows_v, o_v2, sem, osem):
        base = _worker_base(tpw)

        def compute(i, h, b, slot):
            for part in range(n_parts):
                def rbody(rq, accs):
                    r0 = rq * PEER_BF16_RUN
                    s = [hg_v[i, pl.ds(pl.multiple_of((h * PEER_ROWS + r0 + k) * SC_LANES, SC_LANES), SC_LANES)]
                         for k in range(PEER_BF16_RUN)]
                    new = []
                    for c in range(cpp):
                        at = pl.ds((part * cpp + c) * SC_LANES, SC_LANES)
                        lo, hi = _packed_dot([rows_v[b, r0 + k, at] for k in range(PEER_BF16_RUN)], s)
                        new.append(accs[2 * c] + lo)
                        new.append(accs[2 * c + 1] + hi)
                    return tuple(new)

                accs = _sc_loop(PEER_ROWS // PEER_BF16_RUN, rbody,
                                tuple(jnp.zeros((SC_LANES,), F32) for _ in range(2 * cpp)))
                def store(overwrite):
                    for c in range(cpp):
                        lo_at = pl.ds((part * cpp + c) * SC_LANES, SC_LANES)
                        hi_at = pl.ds(half + (part * cpp + c) * SC_LANES, SC_LANES)
                        if overwrite:
                            o_v2[slot, lo_at] = accs[2 * c]
                            o_v2[slot, hi_at] = accs[2 * c + 1]
                        else:
                            o_v2[slot, lo_at] = o_v2[slot, lo_at] + accs[2 * c]
                            o_v2[slot, hi_at] = o_v2[slot, hi_at] + accs[2 * c + 1]

                pl.when(h == 0)(functools.partial(store, True))
                pl.when(h != 0)(functools.partial(store, False))

        @pl.loop(0, tpw // grp)
        def _(g):
            t0 = base + g * grp
            pltpu.sync_copy(idx_hbm.at[pl.ds(t0, grp)], idx_v)
            pltpu.sync_copy(hg_hbm.at[pl.ds(t0, grp)], hg_v)
            _gather_compute_loop(v_hbm, idx_v, rows_v, sem, o_v2, lambda i: out_hbm.at[t0 + i], osem, grp, compute)

    return pl.kernel(
        body,
        out_type=jax.ShapeDtypeStruct((t, d), F32),
        mesh=_sc_mesh(),
        scratch_types=[
            pltpu.VMEM((grp, PEER_SEL), jnp.int32),
            pltpu.VMEM((grp, PEER_SEL * SC_LANES), jnp.int32),
            pltpu.VMEM((PEER_NBUF, PEER_ROWS, half), jnp.int32),
            pltpu.VMEM((2, d), F32),
            pltpu.SemaphoreType.DMA((PEER_NBUF,)),
            pltpu.SemaphoreType.DMA((2,)),
        ],
        compiler_params=pltpu.CompilerParams(needs_layout_passes=False),
        name="peer_expert_mix",
    )(hgx, idx, v_packed)


def _peer_act_kernel(ps_ref, gate_ref, sum_ref, o_ref):
    ps = ps_ref[...]
    sel = sum_ref[...]
    hi = ps.astype(BF16)
    rest = ps - hi.astype(F32)
    mid = rest.astype(BF16)
    lo = (rest - mid.astype(F32)).astype(BF16)
    pre = (jnp.dot(hi, sel, preferred_element_type=F32) + jnp.dot(mid, sel, preferred_element_type=F32)
           + jnp.dot(lo, sel, preferred_element_type=F32))
    hg = 0.5 * pre * (1.0 + lax.erf(pre * (1.0 / math.sqrt(2.0)))) * gate_ref[...]
    spread = (((1,), (1,)), ((), ()))
    hgx = lax.dot_general(hg.astype(BF16), sel, spread, preferred_element_type=F32)
    bits = lax.bitcast_convert_type(hgx, jnp.int32)
    o_ref[...] = lax.bitwise_or(bits, lax.shift_right_logical(bits, jnp.int32(16)))


def peer_act(ps, gates, *, tm=512):
    t, n = ps.shape
    lane_sum = (jnp.arange(n)[:, None] // SC_LANES == jnp.arange(PEER_SEL)[None, :]).astype(BF16)
    return pl.pallas_call(
        _peer_act_kernel,
        grid=(t // tm,),
        in_specs=[
            pl.BlockSpec((tm, n), lambda i: (i, 0)),
            pl.BlockSpec((tm, PEER_SEL), lambda i: (i, 0)),
            pl.BlockSpec((n, PEER_SEL), lambda i: (0, 0)),
        ],
        out_specs=pl.BlockSpec((tm, n), lambda i: (i, 0)),
        out_shape=jax.ShapeDtypeStruct((t, n), jnp.int32),
        compiler_params=_cparams(("parallel",)),
        name="peer_act",
    )(ps, gates, lane_sum)


BATCH_GROUPS = 8


def kernel(x, norm1_g, w_in, rwkv_mu, w0, w_lora_up, a0, a_lora_up, g_lora_up, k_k, k_a, r_k, lnx_g, lnx_b,
           w_proj_a, w_proj_b, w_out, norm2_g, peer_wq, peer_subkeys, peer_u, peer_v, rel_bias, normf_g):
    bsz, seq, d = x.shape
    depth = norm1_g.shape[0]
    groups = BATCH_GROUPS if bsz % BATCH_GROUPS == 0 else 1
    gb = bsz // groups
    tg = gb * seq
    t = bsz * seq
    src = x.reshape(t, d)
    for l in range(depth):
        w_pad = jnp.concatenate([
            w_in[l][:, :COL_A + COL_B_RAW],
            jnp.zeros((d, COL_B - COL_B_RAW), w_in.dtype),
            w_in[l][:, COL_A + COL_B_RAW:]], axis=1).astype(BF16)
        u_packed = _pack_rows(peer_u[l])
        v_packed = _pack_rows(peer_v[l])
        last = l == depth - 1

        def mix(pending, tie=None):
            g, h2d, ps, gates, idx = pending
            hgx = peer_act(ps, gates)
            if tie is not None:
                tie, hgx = lax.optimization_barrier((tie, hgx))
            return tie, (g, h2d, peer_expert_mix(hgx, idx, v_packed))

        outs = [None] * groups

        def close(mixed):
            g, h2d, y2d = mixed
            if last:
                done = [o for o in outs if o is not None]
                outs[g] = final_norm(h2d, y2d, normf_g, out=done[-1] if done else None, row0=g * tg, total_rows=t)
            else:
                outs[g] = h2d + y2d

        pending = closing = None
        for g in range(groups):
            p2d = norm_proj(src, norm1_g[l], w_pad, row0=g * tg, rows=tg)
            p3d = p2d.reshape(gb, seq, -1)
            oa = moba_attention(p3d, rel_bias)
            mixed = None
            if pending is not None:
                oa, mixed = mix(pending, oa)
            prep = rwkv_prep(p3d, rwkv_mu[l], w0[l], w_lora_up[l], a0[l], a_lora_up[l], g_lora_up[l],
                             k_k[l], k_a[l], r_k[l])
            if closing is not None:
                first, y2d = lax.optimization_barrier((prep[0], closing[2]))
                prep = (first,) + tuple(prep[1:])
                close(closing[:2] + (y2d,))
            ob = rwkv_scan(*prep, lnx_g[l], lnx_b[l])
            h2d, xn2 = merge_out(src, oa.reshape(tg, WIDTH), ob.reshape(tg, WIDTH), p2d,
                                 w_proj_a[l], w_proj_b[l], w_out[l], norm2_g[l], row0=g * tg)
            idx, gates = peer_route(xn2, peer_wq[l], peer_subkeys[l])
            closing = None
            if mixed is not None:
                idx, y2d = lax.optimization_barrier((idx, mixed[2]))
                closing = mixed[:2] + (y2d,)
            pending = (g, h2d, peer_expert_dots(xn2, idx, u_packed), gates, idx)
        if closing is not None:
            close(closing)
        close(mix(pending)[1])
        src = outs[-1] if last else jnp.concatenate(outs, axis=0)
    return src.reshape(bsz, seq, d)
```

```python
import functools
import math

import jax
import jax.numpy as jnp
from jax import lax
from jax.experimental import pallas as pl
from jax.experimental.pallas import tpu as pltpu

F32 = jnp.float32
BF16 = jnp.bfloat16
HI = lax.Precision.HIGHEST

LANES = 128
HEAD_DIM = 64
HEADS = 8
PAIRS = HEADS // 2
WIDTH = HEADS * HEAD_DIM
MOBA_BLOCK = 256
MOBA_TOPK = 3
MOBA_LO = 64
REL_BUCKETS = 32
REL_MAX_DIST = 128
DECAY_LORA = 64
AAA_LORA = 64
GATE_LORA = 160
GN_EPS = 64e-5
RMS_EPS = 1e-6
NEG = -1e30
RWKV_CHUNK = 64
COL_A = 3 * WIDTH
COL_B_RAW = 3 * WIDTH + DECAY_LORA + AAA_LORA + GATE_LORA
COL_B = 4 * WIDTH
COL_G_OFF = COL_A + COL_B
VMEM_LIMIT = 56 * 1024 * 1024


def _cparams(sem):
    return pltpu.CompilerParams(dimension_semantics=sem, vmem_limit_bytes=VMEM_LIMIT)


def _norm_proj_kernel(x_ref, g_ref, w_ref, o_ref, xn_ref):
    @pl.when(pl.program_id(1) == 0)
    def _():
        x = x_ref[...]
        ms = jnp.mean(x * x, axis=-1, keepdims=True)
        xn_ref[...] = (x * lax.rsqrt(ms + RMS_EPS) * g_ref[...]).astype(xn_ref.dtype)

    o_ref[...] = jnp.dot(xn_ref[...], w_ref[...], preferred_element_type=F32).astype(o_ref.dtype)


def norm_proj(x2d, g, w, *, row0=0, rows=None, tm=512, tn=512, out_dtype=F32):
    d = x2d.shape[1]
    t = x2d.shape[0] if rows is None else rows
    n = w.shape[1]
    r0 = row0 // tm
    return pl.pallas_call(
        _norm_proj_kernel,
        grid=(t // tm, n // tn),
        in_specs=[
            pl.BlockSpec((tm, d), lambda i, j: (r0 + i, 0)),
            pl.BlockSpec((1, d), lambda i, j: (0, 0)),
            pl.BlockSpec((d, tn), lambda i, j: (0, j)),
        ],
        out_specs=pl.BlockSpec((tm, tn), lambda i, j: (i, j)),
        out_shape=jax.ShapeDtypeStruct((t, n), out_dtype),
        scratch_shapes=[pltpu.VMEM((tm, d), w.dtype)],
        compiler_params=_cparams(("parallel", "arbitrary")),
        name="norm_proj",
    )(x2d, g.reshape(1, d), w)


def _rel_bucket(dist):
    n = jnp.maximum(dist, 0)
    max_exact = REL_BUCKETS // 2
    nf = jnp.maximum(n, 1).astype(F32)
    large = max_exact + (jnp.log(nf / max_exact) / math.log(REL_MAX_DIST / max_exact)
                         * (REL_BUCKETS - max_exact)).astype(jnp.int32)
    large = jnp.minimum(large, REL_BUCKETS - 1)
    return jnp.where(n < max_exact, n, large)


def _moba_kernel(q_ref, k_ref, v_ref, bown_ref, bprev_ref, bfar_ref, o_ref,
                 kb_ref, vb_ref, kbar_ref, *, n_blocks, q0):
    qb = pl.program_id(2) + q0
    blk = MOBA_BLOCK
    scale = 1.0 / math.sqrt(HEAD_DIM)

    rows2 = 2 * blk
    nt = (((1,), (1,)), ((), ()))

    @pl.when(pl.program_id(2) == 0)
    def _():
        kbar_ref[...] = jnp.zeros_like(kbar_ref)
        lane_b = lax.broadcasted_iota(jnp.int32, (blk, LANES), 1)
        for n in range(n_blocks):
            kblk = k_ref[0, n * blk:(n + 1) * blk, :]
            kbar_ref[n:n + 1, :] = jnp.mean(kblk, axis=0, keepdims=True)
            kb_ref[n * blk:(n + 1) * blk, 0:LANES] = kblk.astype(BF16)
            kb_ref[n * blk:(n + 1) * blk, LANES:] = ((lane_b == n) | (lane_b == MOBA_LO + n)).astype(BF16)
        vb_ref[...] = v_ref[0].astype(BF16)

    q2 = q_ref[0]
    first = lax.broadcasted_iota(jnp.int32, (blk, LANES), 1) < HEAD_DIM
    qh = jnp.concatenate([jnp.where(first, q2, 0.0), jnp.where(first, 0.0, q2)], axis=0)
    lane = lax.broadcasted_iota(jnp.int32, (rows2, LANES), 1)
    rowi = lax.broadcasted_iota(jnp.int32, (rows2, LANES), 0)
    gate = lax.dot_general(qh.astype(BF16), kbar_ref[...].astype(BF16), nt, preferred_element_type=F32)
    g = jnp.where(lane < qb, gate, -jnp.inf)
    chosen = lane < 0
    lane_f = lane.astype(F32)
    for _ in range(MOBA_TOPK):
        m = jnp.max(g, axis=1, keepdims=True)
        idx = jnp.min(jnp.where(g == m, lane_f, float(LANES)), axis=1, keepdims=True)
        hit = (lane_f == idx) & (m > -jnp.inf)
        chosen = chosen | hit
        g = jnp.where(hit, -jnp.inf, g)
    nfar = qb - 1
    bfar = jnp.where(rowi < blk, bfar_ref[0, 0:1, 0:1], bfar_ref[1, 0:1, 0:1])
    bhi = bfar.astype(BF16).astype(F32)
    madd = jnp.where(lane < nfar, jnp.where(chosen, bhi, NEG),
                     jnp.where(lane == nfar, jnp.where(chosen, 0.0, NEG),
                               jnp.where((lane >= MOBA_LO) & (lane - MOBA_LO < nfar), bfar - bhi, 0.0)))
    q_aug = jnp.concatenate([(qh * scale).astype(BF16), madd.astype(BF16)], axis=1)

    prev0 = pl.multiple_of(jnp.maximum(nfar, 0) * blk, blk)
    own0 = pl.multiple_of(qb * blk, blk)
    s_prev = (lax.dot_general(q_aug, kb_ref[pl.ds(prev0, blk), :], nt, preferred_element_type=F32)
              + bprev_ref[...].reshape(rows2, blk) + jnp.where(qb > 0, 0.0, NEG))
    s_own = (lax.dot_general(q_aug, kb_ref[pl.ds(own0, blk), :], nt, preferred_element_type=F32)
             + bown_ref[...].reshape(rows2, blk))
    r = lax.broadcasted_iota(jnp.int32, (rows2, blk), 0)
    c = lax.broadcasted_iota(jnp.int32, (rows2, blk), 1)
    s_own = jnp.where(lax.bitwise_and(r, blk - 1) >= c, s_own, NEG)
    s = jnp.concatenate([s_prev, s_own], axis=1)
    m_i = jnp.max(s, axis=1, keepdims=True)
    p = jnp.exp(s - m_i)
    l_i = jnp.sum(p, axis=1, keepdims=True)
    v0 = jnp.concatenate([vb_ref[pl.ds(prev0, blk), :], vb_ref[pl.ds(own0, blk), :]], axis=0)
    acc = jnp.dot(p.astype(BF16), v0, preferred_element_type=F32)

    def body(it, carry):
        m_i, l_i, acc = carry
        k0 = pl.multiple_of(it * rows2, rows2)
        s = lax.dot_general(q_aug, kb_ref[pl.ds(k0, rows2), :], nt, preferred_element_type=F32)
        tail = jnp.where(2 * it + 1 < nfar, 0.0, NEG)
        s = jnp.concatenate([s[:, :blk], s[:, blk:] + tail], axis=1)
        m_new = jnp.maximum(m_i, jnp.max(s, axis=1, keepdims=True))
        alpha = jnp.exp(m_i - m_new)
        p = jnp.exp(s - m_new)
        l_new = alpha * l_i + jnp.sum(p, axis=1, keepdims=True)
        acc_new = alpha * acc + jnp.dot(p.astype(BF16), vb_ref[pl.ds(k0, rows2), :], preferred_element_type=F32)
        return m_new, l_new, acc_new

    m_i, l_i, acc = lax.fori_loop(0, (jnp.maximum(nfar, 0) + 1) // 2, body, (m_i, l_i, acc))
    out = acc / l_i
    o_ref[0] = jnp.where(first, out[:blk], out[blk:])


def moba_attention(p3d, rel_bias, *, q0=0, nq=None):
    bsz, seq, _ = p3d.shape
    blk = MOBA_BLOCK
    n_blocks = seq // blk
    nq = n_blocks - q0 if nq is None else nq
    assert n_blocks <= MOBA_LO and seq % blk == 0
    span = 2 * blk
    by_dist = rel_bias[:, _rel_bucket(jnp.arange(span))].astype(F32)
    shift = jnp.arange(span)

    def toeplitz(c):
        k = jnp.where(shift < blk, shift, shift - span)
        s = by_dist[:, jnp.clip(c - k, 0, span - 1)]
        tiled = jnp.tile(s, (1, blk))[:, :blk * (span - 1)]
        return tiled.reshape(HEADS, blk, span - 1)[:, :, :blk]

    bias_own = toeplitz(0)
    bias_prev = toeplitz(blk)
    bias_far = jnp.broadcast_to(rel_bias[:, REL_BUCKETS - 1].astype(F32)[:, None, None], (HEADS, 8, LANES))
    kern = functools.partial(_moba_kernel, n_blocks=n_blocks, q0=q0)
    return pl.pallas_call(
        kern,
        grid=(bsz, PAIRS, nq),
        in_specs=[
            pl.BlockSpec((1, blk, LANES), lambda b, h, i: (b, q0 + i, h)),
            pl.BlockSpec((1, seq, LANES), lambda b, h, i: (b, 0, PAIRS + h)),
            pl.BlockSpec((1, seq, LANES), lambda b, h, i: (b, 0, 2 * PAIRS + h)),
            pl.BlockSpec((2, blk, blk), lambda b, h, i: (h, 0, 0)),
            pl.BlockSpec((2, blk, blk), lambda b, h, i: (h, 0, 0)),
            pl.BlockSpec((2, 8, LANES), lambda b, h, i: (h, 0, 0)),
        ],
        out_specs=pl.BlockSpec((1, blk, LANES), lambda b, h, i: (b, i, h)),
        out_shape=jax.ShapeDtypeStruct((bsz, nq * blk, WIDTH), F32),
        scratch_shapes=[
            pltpu.VMEM((seq, 2 * LANES), BF16),
            pltpu.VMEM((seq, LANES), BF16),
            pltpu.VMEM((LANES, LANES), F32),
        ],
        compiler_params=_cparams(("parallel", "parallel", "arbitrary")),
        name="moba",
    )(p3d, p3d, p3d, bias_own, bias_prev, bias_far)


def _shifted(x, carry_row):
    rows = lax.broadcasted_iota(jnp.int32, x.shape, 0)
    return jnp.where(rows == 0, carry_row, pltpu.roll(x, 1, axis=0))


def _rwkv_prep_kernel(pr_ref, pk_ref, pv_ref, pl_ref, mu_ref, vec_ref, ww_ref, wa_ref, wg_ref,
                      bd_ref, tri_ref,
                      rt_ref, kt_ref, kd_ref, bd_out_ref, v_ref, g_ref, bonus_ref, pend_ref,
                      carry_ref, *, chunk):
    @pl.when(pl.program_id(1) == 0)
    def _():
        carry_ref[...] = jnp.zeros_like(carry_ref)

    def mix(ref, j):
        x = ref[0]
        mu = mu_ref[0:1, j * WIDTH:(j + 1) * WIDTH]
        prev = _shifted(x, carry_ref[0:1, j * WIDTH:(j + 1) * WIDTH])
        carry_ref[0:1, j * WIDTH:(j + 1) * WIDTH] = x[x.shape[0] - 1:, :]
        return x + mu * (prev - x)

    r = mix(pr_ref, 0)
    k = mix(pk_ref, 1)
    v = mix(pv_ref, 2)
    lo = mix(pl_ref, 3)
    w0, a0, k_k, k_a, r_k = (vec_ref[i:i + 1, :] for i in range(5))
    xwa = lo[:, 0:LANES]
    xg = lo[:, LANES:3 * LANES]
    lw = jnp.dot(jnp.tanh(xwa), ww_ref[...], precision=HI, preferred_element_type=F32)
    la = jnp.dot(xwa, wa_ref[...], precision=HI, preferred_element_type=F32)
    g = jnp.dot(jax.nn.sigmoid(xg), wg_ref[...], precision=HI, preferred_element_type=F32)
    z = -(w0 + lw)
    softplus = jnp.maximum(z, 0.0) + jnp.log(1.0 + jnp.exp(-jnp.abs(z)))
    logw = -jnp.exp(-softplus - 0.5)
    a = jax.nn.sigmoid(a0 + la)
    kk = k * k_k
    ss = jnp.dot(kk * kk, bd_ref[...], precision=HI, preferred_element_type=F32)
    kk = kk / jnp.maximum(jnp.sqrt(ss), 1e-12)
    k2 = k * (1.0 + (a - 1.0) * k_a)
    rk = jnp.dot(r * k2 * r_k, bd_ref[...], precision=HI, preferred_element_type=F32)
    cs = jnp.dot(tri_ref[...], logw, precision=HI, preferred_element_type=F32)
    e_pos = jnp.exp(cs)
    e_neg = jnp.exp(-cs)
    rt_ref[0] = r * e_pos
    kt_ref[0] = kk * jnp.exp(cs - logw)
    kd_ref[0] = k2 * e_neg
    bd_out_ref[0] = kk * a * e_neg
    v_ref[0] = v
    g_ref[0] = g
    bonus_ref[0] = rk * v
    ts = e_pos.shape[0]
    for c in range(ts // chunk):
        pend_ref[0, c:c + 1, :] = e_pos[(c + 1) * chunk - 1:(c + 1) * chunk, :]


def rwkv_prep(p3d, rwkv_mu, w0, w_lora_up, a0, a_lora_up, g_lora_up, k_k, k_a, r_k, *, ts=512):
    bsz, seq, _ = p3d.shape
    chunk = RWKV_CHUNK
    ts = min(ts, seq)
    mu = jnp.pad(rwkv_mu, (0, COL_B - COL_B_RAW)).reshape(1, COL_B)
    vec = jnp.stack([w0, a0, k_k, k_a, r_k.reshape(-1)] + [jnp.zeros_like(w0)] * 3).astype(F32)
    ww = jnp.zeros((LANES, WIDTH), F32).at[:DECAY_LORA].set(w_lora_up)
    wa = jnp.zeros((LANES, WIDTH), F32).at[DECAY_LORA:DECAY_LORA + AAA_LORA].set(a_lora_up)
    wg = jnp.zeros((2 * LANES, WIDTH), F32).at[:GATE_LORA].set(g_lora_up)
    hid = jnp.arange(WIDTH) // HEAD_DIM
    bd = (hid[:, None] == hid[None, :]).astype(F32)
    tix = jnp.arange(ts)
    tri = ((tix[:, None] // chunk == tix[None, :] // chunk) & (tix[None, :] <= tix[:, None])).astype(F32)
    c0 = COL_A // WIDTH
    big = jax.ShapeDtypeStruct((bsz, seq, WIDTH), F32)
    wspec = lambda shape: pl.BlockSpec(shape, lambda b, i: (0, 0))
    ospec = pl.BlockSpec((1, ts, WIDTH), lambda b, i: (b, i, 0))
    return pl.pallas_call(
        functools.partial(_rwkv_prep_kernel, chunk=chunk),
        grid=(bsz, seq // ts),
        in_specs=[
            pl.BlockSpec((1, ts, WIDTH), lambda b, i: (b, i, c0)),
            pl.BlockSpec((1, ts, WIDTH), lambda b, i: (b, i, c0 + 1)),
            pl.BlockSpec((1, ts, WIDTH), lambda b, i: (b, i, c0 + 2)),
            pl.BlockSpec((1, ts, WIDTH), lambda b, i: (b, i, c0 + 3)),
            wspec((1, COL_B)), wspec((8, WIDTH)), wspec((LANES, WIDTH)), wspec((LANES, WIDTH)),
            wspec((2 * LANES, WIDTH)), wspec((WIDTH, WIDTH)), wspec((ts, ts)),
        ],
        out_specs=[ospec] * 7 + [pl.BlockSpec((1, ts // chunk, WIDTH), lambda b, i: (b, i, 0))],
        out_shape=[big] * 7 + [jax.ShapeDtypeStruct((bsz, seq // chunk, WIDTH), F32)],
        scratch_shapes=[pltpu.VMEM((8, COL_B), F32)],
        compiler_params=_cparams(("parallel", "arbitrary")),
        name="rwkv_prep",
    )(p3d, p3d, p3d, p3d, mu, vec, ww, wa, wg, bd, tri)


def _rwkv_scan_kernel(rt_ref, kt_ref, kd_ref, bd_ref, v_ref, g_ref, bonus_ref, pend_ref, ln_ref, sin_ref,
                      o_ref, state_ref, *, chunk, prec):
    @pl.when(pl.program_id(1) == 0)
    def _():
        state_ref[...] = sin_ref[...]

    c2 = 2 * chunk
    lane = lax.broadcasted_iota(jnp.int32, (chunk, LANES), 1)
    first = lane < HEAD_DIM
    row = lax.broadcasted_iota(jnp.int32, (c2, c2), 0)
    col = lax.broadcasted_iota(jnp.int32, (c2, c2), 1)
    eye = (row == col).astype(F32)
    hrow = lax.broadcasted_iota(jnp.int32, (LANES, LANES), 0) // HEAD_DIM
    hcol = lax.broadcasted_iota(jnp.int32, (LANES, LANES), 1) // HEAD_DIM
    head_mean = jnp.where(hrow == hcol, 1.0 / HEAD_DIM, 0.0).astype(F32)
    nt = (((1,), (1,)), ((), ()))
    tn = (((0,), (0,)), ((), ()))
    dot = functools.partial(jnp.dot, precision=prec, preferred_element_type=F32)
    dotg = functools.partial(lax.dot_general, precision=prec, preferred_element_type=F32)

    def stack(x):
        return jnp.concatenate([jnp.where(first, x, 0.0), jnp.where(first, 0.0, x)], axis=0)

    pairs = range(PAIRS)
    sls = [slice(hp * LANES, (hp + 1) * LANES) for hp in pairs]
    rs, ks, kds, bs, vs = ([stack(ref[0, :, sl]) for sl in sls] for ref in (rt_ref, kt_ref, kd_ref, bd_ref, v_ref))
    hts = [state_ref[0, hp] for hp in pairs]
    big = [dotg(jnp.concatenate([ks[hp], rs[hp]], axis=0), jnp.concatenate([bs[hp], kds[hp]], axis=0), nt)
           for hp in pairs]
    a_b = [jnp.where(row > col, big[hp][0:c2, 0:c2], 0.0) for hp in pairs]
    a_k = [jnp.where(row > col, big[hp][0:c2, c2:], 0.0) for hp in pairs]
    a_rb = [jnp.where(row >= col, big[hp][c2:, 0:c2], 0.0) for hp in pairs]
    a_rk = [jnp.where(row >= col, big[hp][c2:, c2:], 0.0) for hp in pairs]
    kh = [dotg(jnp.concatenate([ks[hp], rs[hp]], axis=0), hts[hp], nt) for hp in pairs]
    av = [dot(jnp.concatenate([a_k[hp], a_rk[hp]], axis=0), vs[hp]) for hp in pairs]
    vk = [dotg(vs[hp], kds[hp], tn) for hp in pairs]
    inv = [eye - a_b[hp] for hp in pairs]
    pw = [dot(a_b[hp], a_b[hp]) for hp in pairs]
    n_sq = int(math.log2(chunk)) - 1
    for lvl in range(n_sq):
        if lvl + 1 < n_sq:
            both = [dot(jnp.concatenate([inv[hp], pw[hp]], axis=0), pw[hp]) for hp in pairs]
            inv = [inv[hp] + both[hp][0:c2] for hp in pairs]
            pw = [both[hp][c2:] for hp in pairs]
        else:
            inv = [inv[hp] + dot(inv[hp], pw[hp]) for hp in pairs]
    us = [dot(inv[hp], kh[hp][0:c2] + av[hp][0:c2]) for hp in pairs]
    ub = [dotg(us[hp], bs[hp], tn) for hp in pairs]
    au = [dot(a_rb[hp], us[hp]) for hp in pairs]
    for hp in pairs:
        sl = sls[hp]
        pend = pend_ref[0, 0, 0:1, sl]
        state_ref[0, hp] = (hts[hp] + vk[hp] - ub[hp]) * pend
        os_ = kh[hp][c2:] + av[hp][c2:] - au[hp]
        o = os_[0:chunk] + os_[chunk:]
        mu = jnp.dot(o, head_mean, precision=HI, preferred_element_type=F32)
        d = o - mu
        var = jnp.dot(d * d, head_mean, precision=HI, preferred_element_type=F32)
        on = d * lax.rsqrt(var + GN_EPS) * ln_ref[0:1, sl] + ln_ref[1:2, sl]
        o_ref[0, :, sl] = (on + bonus_ref[0, :, sl]) * g_ref[0, :, sl]


def rwkv_scan(rt, kt, kd, bd, v, g, bonus, pend, lnx_g, lnx_b, *, state=None, c0=0, nc=None, prec=None):
    bsz, seq, _ = rt.shape
    chunk = RWKV_CHUNK
    n_chunks = seq // chunk
    nc = n_chunks - c0 if nc is None else nc
    ln = jnp.stack([lnx_g, lnx_b] + [jnp.zeros_like(lnx_g)] * 6).astype(F32)
    pend4 = pend.reshape(bsz, n_chunks, 1, WIDTH)
    if state is None:
        state = jnp.zeros((bsz, PAIRS, LANES, LANES), F32)
    spec = pl.BlockSpec((1, chunk, WIDTH), lambda b, c: (b, c0 + c, 0))
    sspec = pl.BlockSpec((1, PAIRS, LANES, LANES), lambda b, c: (b, 0, 0, 0))
    return pl.pallas_call(
        functools.partial(_rwkv_scan_kernel, chunk=chunk, prec=prec),
        grid=(bsz, nc),
        in_specs=[spec] * 7 + [
            pl.BlockSpec((1, 1, 1, WIDTH), lambda b, c: (b, c0 + c, 0, 0)),
            pl.BlockSpec((8, WIDTH), lambda b, c: (0, 0)),
            sspec,
        ],
        out_specs=[pl.BlockSpec((1, chunk, WIDTH), lambda b, c: (b, c, 0)), sspec],
        out_shape=[jax.ShapeDtypeStruct((bsz, nc * chunk, WIDTH), F32),
                   jax.ShapeDtypeStruct((bsz, PAIRS, LANES, LANES), F32)],
        compiler_params=_cparams(("parallel", "arbitrary")),
        name="rwkv_scan",
    )(rt, kt, kd, bd, v, g, bonus, pend4, ln, state)


def _merge_kernel(x_ref, oa_ref, ob_ref, ga_ref, gb_ref, wa_ref, wb_ref, wo_ref, g2_ref,
                  h_ref, xn_ref, acc_ref):
    j = pl.program_id(1)

    @pl.when(j == 0)
    def _():
        acc_ref[...] = x_ref[...]

    ya = jnp.dot(oa_ref[...].astype(BF16), wa_ref[...], preferred_element_type=F32)
    yb = jnp.dot(ob_ref[...].astype(BF16), wb_ref[...], preferred_element_type=F32)
    y = jax.nn.sigmoid(ga_ref[...]) * ya + jax.nn.sigmoid(gb_ref[...]) * yb
    acc_ref[...] += jnp.dot(y.astype(BF16), wo_ref[...], preferred_element_type=F32)

    @pl.when(j == pl.num_programs(1) - 1)
    def _():
        h = acc_ref[...]
        h_ref[...] = h
        ms = jnp.mean(h * h, axis=-1, keepdims=True)
        xn_ref[...] = _pack_halves(h * lax.rsqrt(ms + RMS_EPS) * g2_ref[...])


def _pack_halves(x):
    half = x.shape[1] // 2
    lo = lax.bitcast_convert_type(x[:, :half].astype(BF16).astype(F32), jnp.int32)
    hi = lax.bitcast_convert_type(x[:, half:].astype(BF16).astype(F32), jnp.int32)
    return lax.bitwise_or(lax.shift_right_logical(lo, jnp.int32(16)), hi)


def _unpack_halves(words):
    lo, hi = _unpack_words(words)
    return jnp.concatenate([lo, hi], axis=1)


def merge_out(x2d, oa, ob, p2d, w_proj_a, w_proj_b, w_out, norm2_g, *, row0=0, prow0=0, tm=512):
    t, d = oa.shape[0], x2d.shape[1]
    r0 = row0 // tm
    p0 = prow0 // tm
    tn = WIDTH
    nj = d // tn
    g0 = COL_G_OFF // tn
    return pl.pallas_call(
        _merge_kernel,
        grid=(t // tm, nj),
        in_specs=[
            pl.BlockSpec((tm, d), lambda i, j: (r0 + i, 0)),
            pl.BlockSpec((tm, WIDTH), lambda i, j: (i, 0)),
            pl.BlockSpec((tm, WIDTH), lambda i, j: (i, 0)),
            pl.BlockSpec((tm, tn), lambda i, j: (p0 + i, g0 + j)),
            pl.BlockSpec((tm, tn), lambda i, j: (p0 + i, g0 + nj + j)),
            pl.BlockSpec((WIDTH, tn), lambda i, j: (0, j)),
            pl.BlockSpec((WIDTH, tn), lambda i, j: (0, j)),
            pl.BlockSpec((tn, d), lambda i, j: (j, 0)),
            pl.BlockSpec((1, d), lambda i, j: (0, 0)),
        ],
        out_specs=[pl.BlockSpec((tm, d), lambda i, j: (i, 0)), pl.BlockSpec((tm, d // 2), lambda i, j: (i, 0))],
        out_shape=[jax.ShapeDtypeStruct((t, d), F32), jax.ShapeDtypeStruct((t, d // 2), jnp.int32)],
        scratch_shapes=[pltpu.VMEM((tm, d), F32)],
        compiler_params=_cparams(("parallel", "arbitrary")),
        name="merge_out",
    )(x2d, oa, ob, p2d, p2d, w_proj_a.astype(BF16), w_proj_b.astype(BF16), w_out.astype(BF16),
      norm2_g.reshape(1, d))


PEER_HEADS = 8
PEER_NKEYS = 128
PEER_TOPK = 16
PEER_HALF = 128


def _topk_rows(s, k):
    n = s.shape[0]
    rows = lax.broadcasted_iota(jnp.int32, s.shape, 0).astype(F32)
    vals, ids = [], []
    for _ in range(k):
        m = jnp.max(s, axis=0, keepdims=True)
        first = jnp.min(jnp.where(s == m, rows, float(n)), axis=0, keepdims=True)
        vals.append(m)
        ids.append(first)
        s = jnp.where(rows == first, -jnp.inf, s)
    return jnp.concatenate(vals, axis=0), jnp.concatenate(ids, axis=0)


def _take_rows(table, ids):
    rows = lax.broadcasted_iota(jnp.int32, table.shape, 0).astype(F32)
    return jnp.sum(jnp.where(rows == ids, table, 0.0), axis=0, keepdims=True)


def _peer_route_kernel(xn_ref, wq_ref, sk_ref, idx_ref, gate_ref, *, prec):
    tt = xn_ref.shape[0]
    k = PEER_TOPK
    xn = _unpack_halves(xn_ref[...]) if xn_ref.dtype == jnp.int32 else xn_ref[...]
    q = jnp.dot(xn.astype(wq_ref.dtype), wq_ref[...], precision=prec, preferred_element_type=F32)
    nt = (((1,), (1,)), ((), ()))
    idx_rows, gate_rows = [], []
    half = k // 2
    for h in range(PEER_HEADS):
        tops = []
        for p in range(2):
            c0 = (h * 2 + p) * PEER_HALF
            s = lax.dot_general(sk_ref[h, p].astype(wq_ref.dtype), q[:, c0:c0 + PEER_HALF].astype(wq_ref.dtype),
                                nt, precision=prec, preferred_element_type=F32)
            tops.append(_topk_rows(s, k))
        (s0, i0), (s1, i1) = tops
        cs = [s0[0:1] + s1] + [s0[i:i + 1] + s1[0:half] for i in range(1, half)] + [s0[half:] + s1[0:1]]
        best_s, pos = _topk_rows(jnp.concatenate(cs, axis=0), k)
        mid = jnp.floor((pos - k) * (1.0 / half))
        end_mid = float(k + (half - 1) * half)
        i_rank = jnp.where(pos < k, 0.0, jnp.where(pos < end_mid, 1.0 + mid, pos - (end_mid - half)))
        j_rank = jnp.where(pos < k, pos, jnp.where(pos < end_mid, (pos - k) - half * mid, 0.0))
        ids = [_take_rows(i0, i_rank[n:n + 1]) * PEER_NKEYS + _take_rows(i1, j_rank[n:n + 1]) for n in range(k)]
        e = jnp.exp(best_s - best_s[0:1])
        gate_rows.append(e / jnp.sum(e, axis=0, keepdims=True))
        idx_rows.append(jnp.concatenate(ids, axis=0).astype(jnp.int32))
    idx_ref[...] = jnp.concatenate(idx_rows, axis=0).T
    gate_ref[...] = jnp.concatenate(gate_rows, axis=0).T


def peer_route(xn2d, peer_wq, peer_subkeys, *, tt=256, prec=None, wdtype=BF16):
    t, dx = xn2d.shape
    d, nq = peer_wq.shape
    n_sel = PEER_HEADS * PEER_TOPK
    return pl.pallas_call(
        functools.partial(_peer_route_kernel, prec=prec),
        grid=(t // tt,),
        in_specs=[
            pl.BlockSpec((tt, dx), lambda i: (i, 0)),
            pl.BlockSpec((d, nq), lambda i: (0, 0)),
            pl.BlockSpec((PEER_HEADS, 2, PEER_NKEYS, PEER_HALF), lambda i: (0, 0, 0, 0)),
        ],
        out_specs=[pl.BlockSpec((tt, n_sel), lambda i: (i, 0))] * 2,
        out_shape=[jax.ShapeDtypeStruct((t, n_sel), jnp.int32), jax.ShapeDtypeStruct((t, n_sel), F32)],
        compiler_params=_cparams(("parallel",)),
        name="peer_route",
    )(xn2d, peer_wq.astype(wdtype), peer_subkeys)


def _final_kernel(h_ref, y_ref, g_ref, *rest):
    o_ref = rest[-1]
    h = h_ref[...] + y_ref[...]
    ms = jnp.mean(h * h, axis=-1, keepdims=True)
    o_ref[...] = h * lax.rsqrt(ms + RMS_EPS) * g_ref[...]


def final_norm(h2d, y2d, g, *, out=None, row0=0, total_rows=None, tm=1024):
    t, d = h2d.shape
    total = t if total_rows is None else total_rows
    r0 = row0 // tm
    spec = pl.BlockSpec((tm, d), lambda i: (i, 0))
    in_specs = [spec, spec, pl.BlockSpec((1, d), lambda i: (0, 0))]
    args = [h2d, y2d, g.reshape(1, d)]
    aliases = {}
    if out is not None:
        in_specs.append(pl.BlockSpec(memory_space=pl.ANY))
        args.append(out)
        aliases = {3: 0}
    return pl.pallas_call(
        _final_kernel,
        grid=(t // tm,),
        in_specs=in_specs,
        out_specs=pl.BlockSpec((tm, d), lambda i: (r0 + i, 0)),
        out_shape=jax.ShapeDtypeStruct((total, d), F32),
        input_output_aliases=aliases,
        compiler_params=_cparams(("parallel",)),
        name="final_norm",
    )(*args)


SC_CORES = 2
SC_SUBCORES = 16
SC_LANES = 16
SC_WORKERS = SC_CORES * SC_SUBCORES
PEER_SEL = PEER_HEADS * PEER_TOPK
PEER_ROWS = 32
PEER_PARTS = PEER_SEL // PEER_ROWS
PEER_NBUF = 4
PEER_GROUP = 32
PEER_BF16_RUN = 4


def _pack_rows(w):
    half = w.shape[1] // 2
    bits = lax.bitcast_convert_type(w.astype(BF16), jnp.uint16).astype(jnp.uint32)
    return lax.bitcast_convert_type(bits[:, :half] | (bits[:, half:] << 16), jnp.int32)


def _unpack_words(w):
    lo = lax.bitcast_convert_type(lax.shift_left(w, jnp.int32(16)), F32)
    hi = lax.bitcast_convert_type(lax.bitwise_and(w, jnp.int32(-65536)), F32)
    return lo, hi


def _packed_dot(a_words, b_words):
    from jax.experimental.pallas import tpu_sc as plsc
    prods = [plsc.bitcast(a, BF16) * plsc.bitcast(b, BF16) for a, b in zip(a_words, b_words)]
    while len(prods) > 1:
        prods = [prods[k] + prods[k + 1] for k in range(0, len(prods), 2)]
    return _unpack_words(plsc.bitcast(prods[0], jnp.int32))


def _sc_mesh():
    from jax.experimental.pallas import tpu_sc as plsc
    return plsc.VectorSubcoreMesh(core_axis_name="c", subcore_axis_name="s",
                                  num_cores=SC_CORES, num_subcores=SC_SUBCORES)


def _sc_loop(n, body, carry):
    from jax.experimental.pallas import tpu_sc as plsc
    return plsc.parallel_loop(0, n, carry=carry)(body)


def _worker_base(tokens_per_worker):
    return (lax.axis_index("s") * SC_CORES + lax.axis_index("c")) * tokens_per_worker


def _gather_compute_loop(table_hbm, idx_v, rows_v, sem, stage_v, out_row, osem, grp, compute):
    n_gathers = PEER_PARTS * grp
    ahead = PEER_NBUF - 1

    def gather(j, b):
        i = j // PEER_PARTS if isinstance(j, int) else lax.shift_right_logical(j, PEER_PARTS.bit_length() - 1)
        h = j % PEER_PARTS if isinstance(j, int) else lax.bitwise_and(j, PEER_PARTS - 1)
        ids = idx_v.at[i, pl.ds(pl.multiple_of(h * PEER_ROWS, PEER_ROWS), PEER_ROWS)]
        return pltpu.make_async_copy(table_hbm.at[ids], rows_v.at[b], sem.at[b])

    def put(i, slot):
        return pltpu.make_async_copy(stage_v.at[slot], out_row(i), osem.at[slot])

    for j in range(ahead):
        gather(j, j).start()

    @pl.loop(0, n_gathers)
    def _(j):
        b = lax.bitwise_and(j, PEER_NBUF - 1)
        h = lax.bitwise_and(j, PEER_PARTS - 1)
        i = lax.shift_right_logical(j, PEER_PARTS.bit_length() - 1)
        slot = lax.bitwise_and(i, 1)

        @pl.when((h == 0) & (i >= 2))
        def _():
            put(i - 2, slot).wait()

        @pl.when(j + ahead < n_gathers)
        def _():
            gather(j + ahead, lax.bitwise_and(j + ahead, PEER_NBUF - 1)).start()

        gather(j, b).wait()
        compute(i, h, b, slot)

        @pl.when(h == PEER_PARTS - 1)
        def _():
            put(i, slot).start()

    put(grp - 2, 0).wait()
    put(grp - 1, 1).wait()


def peer_expert_dots(x_packed, idx, u_packed):
    t, half = x_packed.shape
    n_chunks = half // SC_LANES
    tpw = t // SC_WORKERS
    grp = min(PEER_GROUP, tpw)
    rows_tog = 4

    def body(x_hbm, idx_hbm, u_hbm, out_hbm, idx_v, x_v, rows_v, ps_v, sem, osem):
        base = _worker_base(tpw)

        def compute(i, h, b, slot):
            @pl.loop(0, PEER_ROWS // rows_tog)
            def _(rg):
                r0 = rg * rows_tog
                accs = [[None, None] for _ in range(rows_tog)]
                for c0 in range(0, n_chunks, PEER_BF16_RUN):
                    ats = [pl.ds((c0 + k) * SC_LANES, SC_LANES) for k in range(PEER_BF16_RUN)]
                    xw = [x_v[i, at] for at in ats]
                    for r in range(rows_tog):
                        terms = _packed_dot([rows_v[b, r0 + r, at] for at in ats], xw)
                        for k, term in enumerate(terms):
                            accs[r][k] = term if accs[r][k] is None else accs[r][k] + term
                for r in range(rows_tog):
                    at = pl.ds(pl.multiple_of((h * PEER_ROWS + r0 + r) * SC_LANES, SC_LANES), SC_LANES)
                    ps_v[slot, at] = accs[r][0] + accs[r][1]

        @pl.loop(0, tpw // grp)
        def _(g):
            t0 = base + g * grp
            pltpu.sync_copy(idx_hbm.at[pl.ds(t0, grp)], idx_v)
            pltpu.sync_copy(x_hbm.at[pl.ds(t0, grp)], x_v)
            _gather_compute_loop(u_hbm, idx_v, rows_v, sem, ps_v, lambda i: out_hbm.at[t0 + i], osem, grp, compute)

    return pl.kernel(
        body,
        out_type=jax.ShapeDtypeStruct((t, PEER_SEL * SC_LANES), F32),
        mesh=_sc_mesh(),
        scratch_types=[
            pltpu.VMEM((grp, PEER_SEL), jnp.int32),
            pltpu.VMEM((grp, half), jnp.int32),
            pltpu.VMEM((PEER_NBUF, PEER_ROWS, half), jnp.int32),
            pltpu.VMEM((2, PEER_SEL * SC_LANES), F32),
            pltpu.SemaphoreType.DMA((PEER_NBUF,)),
            pltpu.SemaphoreType.DMA((2,)),
        ],
        compiler_params=pltpu.CompilerParams(needs_layout_passes=False),
        name="peer_expert_dots",
    )(x_packed, idx, u_packed)


def peer_expert_mix(hgx, idx, v_packed):
    t = hgx.shape[0]
    half = v_packed.shape[1]
    d = 2 * half
    tpw = t // SC_WORKERS
    grp = min(PEER_GROUP // 2, tpw)
    n_parts = 2
    cpp = half // SC_LANES // n_parts

    def body(hg_hbm, idx_hbm, v_hbm, out_hbm, idx_v, hg_v, rows_v, o_v2, sem, osem):
        base = _worker_base(tpw)

        def compute(i, h, b, slot):
            for part in range(n_parts):
                def rbody(rq, accs):
                    r0 = rq * PEER_BF16_RUN
                    s = [hg_v[i, pl.ds(pl.multiple_of((h * PEER_ROWS + r0 + k) * SC_LANES, SC_LANES), SC_LANES)]
                         for k in range(PEER_BF16_RUN)]
                    new = []
                    for c in range(cpp):
                        at = pl.ds((part * cpp + c) * SC_LANES, SC_LANES)
                        lo, hi = _packed_dot([rows_v[b, r0 + k, at] for k in range(PEER_BF16_RUN)], s)
                        new.append(accs[2 * c] + lo)
                        new.append(accs[2 * c + 1] + hi)
                    return tuple(new)

                accs = _sc_loop(PEER_ROWS // PEER_BF16_RUN, rbody,
                                tuple(jnp.zeros((SC_LANES,), F32) for _ in range(2 * cpp)))
                def store(overwrite):
                    for c in range(cpp):
                        lo_at = pl.ds((part * cpp + c) * SC_LANES, SC_LANES)
                        hi_at = pl.ds(half + (part * cpp + c) * SC_LANES, SC_LANES)
                        if overwrite:
                            o_v2[slot, lo_at] = accs[2 * c]
                            o_v2[slot, hi_at] = accs[2 * c + 1]
                        else:
                            o_v2[slot, lo_at] = o_v2[slot, lo_at] + accs[2 * c]
                            o_v2[slot, hi_at] = o_v2[slot, hi_at] + accs[2 * c + 1]

                pl.when(h == 0)(functools.partial(store, True))
                pl.when(h != 0)(functools.partial(store, False))

        @pl.loop(0, tpw // grp)
        def _(g):
            t0 = base + g * grp
            pltpu.sync_copy(idx_hbm.at[pl.ds(t0, grp)], idx_v)
            pltpu.sync_copy(hg_hbm.at[pl.ds(t0, grp)], hg_v)
            _gather_compute_loop(v_hbm, idx_v, rows_v, sem, o_v2, lambda i: out_hbm.at[t0 + i], osem, grp, compute)

    return pl.kernel(
        body,
        out_type=jax.ShapeDtypeStruct((t, d), F32),
        mesh=_sc_mesh(),
        scratch_types=[
            pltpu.VMEM((grp, PEER_SEL), jnp.int32),
            pltpu.VMEM((grp, PEER_SEL * SC_LANES), jnp.int32),
            pltpu.VMEM((PEER_NBUF, PEER_ROWS, half), jnp.int32),
            pltpu.VMEM((2, d), F32),
            pltpu.SemaphoreType.DMA((PEER_NBUF,)),
            pltpu.SemaphoreType.DMA((2,)),
        ],
        compiler_params=pltpu.CompilerParams(needs_layout_passes=False),
        name="peer_expert_mix",
    )(hgx, idx, v_packed)


def _peer_act_kernel(ps_ref, gate_ref, sum_ref, o_ref):
    ps = ps_ref[...]
    sel = sum_ref[...]
    hi = ps.astype(BF16)
    rest = ps - hi.astype(F32)
    mid = rest.astype(BF16)
    lo = (rest - mid.astype(F32)).astype(BF16)
    pre = (jnp.dot(hi, sel, preferred_element_type=F32) + jnp.dot(mid, sel, preferred_element_type=F32)
           + jnp.dot(lo, sel, preferred_element_type=F32))
    hg = 0.5 * pre * (1.0 + lax.erf(pre * (1.0 / math.sqrt(2.0)))) * gate_ref[...]
    spread = (((1,), (1,)), ((), ()))
    hgx = lax.dot_general(hg.astype(BF16), sel, spread, preferred_element_type=F32)
    bits = lax.bitcast_convert_type(hgx, jnp.int32)
    o_ref[...] = lax.bitwise_or(bits, lax.shift_right_logical(bits, jnp.int32(16)))


def peer_act(ps, gates, *, tm=512):
    t, n = ps.shape
    lane_sum = (jnp.arange(n)[:, None] // SC_LANES == jnp.arange(PEER_SEL)[None, :]).astype(BF16)
    return pl.pallas_call(
        _peer_act_kernel,
        grid=(t // tm,),
        in_specs=[
            pl.BlockSpec((tm, n), lambda i: (i, 0)),
            pl.BlockSpec((tm, PEER_SEL), lambda i: (i, 0)),
            pl.BlockSpec((n, PEER_SEL), lambda i: (0, 0)),
        ],
        out_specs=pl.BlockSpec((tm, n), lambda i: (i, 0)),
        out_shape=jax.ShapeDtypeStruct((t, n), jnp.int32),
        compiler_params=_cparams(("parallel",)),
        name="peer_act",
    )(ps, gates, lane_sum)


BATCH_GROUPS = 8


def kernel(x, norm1_g, w_in, rwkv_mu, w0, w_lora_up, a0, a_lora_up, g_lora_up, k_k, k_a, r_k, lnx_g, lnx_b,
           w_proj_a, w_proj_b, w_out, norm2_g, peer_wq, peer_subkeys, peer_u, peer_v, rel_bias, normf_g):
    bsz, seq, d = x.shape
    depth = norm1_g.shape[0]
    groups = BATCH_GROUPS if bsz % BATCH_GROUPS == 0 else 1
    gb = bsz // groups
    tg = gb * seq
    t = bsz * seq
    src = x.reshape(t, d)
    for l in range(depth):
        w_pad = jnp.concatenate([
            w_in[l][:, :COL_A + COL_B_RAW],
            jnp.zeros((d, COL_B - COL_B_RAW), w_in.dtype),
            w_in[l][:, COL_A + COL_B_RAW:]], axis=1).astype(BF16)
        u_packed = _pack_rows(peer_u[l])
        tables = {}
        last = l == depth - 1

        def mix(pending, tie=None):
            row0, h2d, ps, gates, idx = pending
            hgx = peer_act(ps, gates)
            if tie is not None:
                tie, hgx = lax.optimization_barrier((tie, hgx))
            if "v" not in tables:
                tables["v"] = _pack_rows(peer_v[l])
            return tie, (row0, h2d, peer_expert_mix(hgx, idx, tables["v"]))

        outs = []

        def close(mixed):
            row0, h2d, y2d = mixed
            if last:
                outs.append(final_norm(h2d, y2d, normf_g, out=outs[-1] if outs else None, row0=row0, total_rows=t))
            else:
                outs.append(h2d + y2d)

        halves = gb == 1 and seq % (2 * MOBA_BLOCK) == 0 and (seq // 2) % (SC_WORKERS * PEER_GROUP) == 0

        pending = closing = None
        for g in range(groups):
            p2d = norm_proj(src, norm1_g[l], w_pad, row0=g * tg, rows=tg)
            if g == 1:
                p2d, v_src = lax.optimization_barrier((p2d, peer_v[l]))
                tables["v"] = _pack_rows(v_src)
            p3d = p2d.reshape(gb, seq, -1)
            prep = state = None
            for s0, sn in ([(0, seq // 2), (seq // 2, seq // 2)] if halves and g == 0 else [(0, seq)]):
                oa = moba_attention(p3d, rel_bias, q0=s0 // MOBA_BLOCK, nq=sn // MOBA_BLOCK)
                mixed = None
                if pending is not None:
                    oa, mixed = mix(pending, oa)
                if closing is not None:
                    oa, y2d = lax.optimization_barrier((oa, closing[2]))
                    close(closing[:2] + (y2d,))
                    closing = None
                if prep is None:
                    prep = rwkv_prep(p3d, rwkv_mu[l], w0[l], w_lora_up[l], a0[l], a_lora_up[l], g_lora_up[l],
                                     k_k[l], k_a[l], r_k[l])
                ob, state = rwkv_scan(*prep, lnx_g[l], lnx_b[l], state=state,
                                      c0=s0 // RWKV_CHUNK, nc=sn // RWKV_CHUNK)
                nt = gb * sn
                h2d, xn2 = merge_out(src, oa.reshape(nt, WIDTH), ob.reshape(nt, WIDTH), p2d, w_proj_a[l], w_proj_b[l],
                                     w_out[l], norm2_g[l], row0=g * tg + s0, prow0=s0)
                idx, gates = peer_route(xn2, peer_wq[l], peer_subkeys[l])
                if mixed is not None:
                    idx, y2d = lax.optimization_barrier((idx, mixed[2]))
                    closing = mixed[:2] + (y2d,)
                pending = (g * tg + s0, h2d, peer_expert_dots(xn2, idx, u_packed), gates, idx)
        if closing is not None:
            close(closing)
        close(mix(pending)[1])
        src = outs[-1] if last else jnp.concatenate(outs, axis=0)
    return src.reshape(bsz, seq, d)
```

```python
import functools
import math

import jax
import jax.numpy as jnp
from jax import lax
from jax.experimental import pallas as pl
from jax.experimental.pallas import tpu as pltpu

F32 = jnp.float32
BF16 = jnp.bfloat16
HI = lax.Precision.HIGHEST

LANES = 128
HEAD_DIM = 64
HEADS = 8
PAIRS = HEADS // 2
WIDTH = HEADS * HEAD_DIM
MOBA_BLOCK = 256
MOBA_TOPK = 3
MOBA_LO = 64
REL_BUCKETS = 32
REL_MAX_DIST = 128
DECAY_LORA = 64
AAA_LORA = 64
GATE_LORA = 160
GN_EPS = 64e-5
RMS_EPS = 1e-6
NEG = -1e30
RWKV_CHUNK = 64
COL_A = 3 * WIDTH
COL_B_RAW = 3 * WIDTH + DECAY_LORA + AAA_LORA + GATE_LORA
COL_B = 4 * WIDTH
COL_G_OFF = COL_A + COL_B
VMEM_LIMIT = 56 * 1024 * 1024


def _cparams(sem):
    return pltpu.CompilerParams(dimension_semantics=sem, vmem_limit_bytes=VMEM_LIMIT)


def _norm_proj_kernel(x_ref, g_ref, w_ref, o_ref, xn_ref):
    @pl.when(pl.program_id(1) == 0)
    def _():
        x = x_ref[...]
        ms = jnp.mean(x * x, axis=-1, keepdims=True)
        xn_ref[...] = (x * lax.rsqrt(ms + RMS_EPS) * g_ref[...]).astype(xn_ref.dtype)

    o_ref[...] = jnp.dot(xn_ref[...], w_ref[...], preferred_element_type=F32).astype(o_ref.dtype)


def norm_proj(x2d, g, w, *, row0=0, rows=None, tm=512, tn=512, out_dtype=F32):
    d = x2d.shape[1]
    t = x2d.shape[0] if rows is None else rows
    n = w.shape[1]
    r0 = row0 // tm
    return pl.pallas_call(
        _norm_proj_kernel,
        grid=(t // tm, n // tn),
        in_specs=[
            pl.BlockSpec((tm, d), lambda i, j: (r0 + i, 0)),
            pl.BlockSpec((1, d), lambda i, j: (0, 0)),
            pl.BlockSpec((d, tn), lambda i, j: (0, j)),
        ],
        out_specs=pl.BlockSpec((tm, tn), lambda i, j: (i, j)),
        out_shape=jax.ShapeDtypeStruct((t, n), out_dtype),
        scratch_shapes=[pltpu.VMEM((tm, d), w.dtype)],
        compiler_params=_cparams(("parallel", "arbitrary")),
        name="norm_proj",
    )(x2d, g.reshape(1, d), w)


def _rel_bucket(dist):
    n = jnp.maximum(dist, 0)
    max_exact = REL_BUCKETS // 2
    nf = jnp.maximum(n, 1).astype(F32)
    large = max_exact + (jnp.log(nf / max_exact) / math.log(REL_MAX_DIST / max_exact)
                         * (REL_BUCKETS - max_exact)).astype(jnp.int32)
    large = jnp.minimum(large, REL_BUCKETS - 1)
    return jnp.where(n < max_exact, n, large)


def _moba_kernel(q_ref, k_ref, v_ref, bown_ref, bprev_ref, bfar_ref, o_ref,
                 kb_ref, vb_ref, kbar_ref, *, n_blocks, q0):
    qb = pl.program_id(2) + q0
    blk = MOBA_BLOCK
    scale = 1.0 / math.sqrt(HEAD_DIM)

    rows2 = 2 * blk
    nt = (((1,), (1,)), ((), ()))

    @pl.when(pl.program_id(2) == 0)
    def _():
        kbar_ref[...] = jnp.zeros_like(kbar_ref)
        lane_b = lax.broadcasted_iota(jnp.int32, (blk, LANES), 1)
        for n in range(n_blocks):
            kblk = k_ref[0, n * blk:(n + 1) * blk, :]
            kbar_ref[n:n + 1, :] = jnp.mean(kblk, axis=0, keepdims=True)
            kb_ref[n * blk:(n + 1) * blk, 0:LANES] = kblk.astype(BF16)
            kb_ref[n * blk:(n + 1) * blk, LANES:] = ((lane_b == n) | (lane_b == MOBA_LO + n)).astype(BF16)
        vb_ref[...] = v_ref[0].astype(BF16)

    q2 = q_ref[0]
    first = lax.broadcasted_iota(jnp.int32, (blk, LANES), 1) < HEAD_DIM
    qh = jnp.concatenate([jnp.where(first, q2, 0.0), jnp.where(first, 0.0, q2)], axis=0)
    lane = lax.broadcasted_iota(jnp.int32, (rows2, LANES), 1)
    rowi = lax.broadcasted_iota(jnp.int32, (rows2, LANES), 0)
    gate = lax.dot_general(qh.astype(BF16), kbar_ref[...].astype(BF16), nt, preferred_element_type=F32)
    g = jnp.where(lane < qb, gate, -jnp.inf)
    chosen = lane < 0
    lane_f = lane.astype(F32)
    for _ in range(MOBA_TOPK):
        m = jnp.max(g, axis=1, keepdims=True)
        idx = jnp.min(jnp.where(g == m, lane_f, float(LANES)), axis=1, keepdims=True)
        hit = (lane_f == idx) & (m > -jnp.inf)
        chosen = chosen | hit
        g = jnp.where(hit, -jnp.inf, g)
    nfar = qb - 1
    bfar = jnp.where(rowi < blk, bfar_ref[0, 0:1, 0:1], bfar_ref[1, 0:1, 0:1])
    bhi = bfar.astype(BF16).astype(F32)
    madd = jnp.where(lane < nfar, jnp.where(chosen, bhi, NEG),
                     jnp.where(lane == nfar, jnp.where(chosen, 0.0, NEG),
                               jnp.where((lane >= MOBA_LO) & (lane - MOBA_LO < nfar), bfar - bhi, 0.0)))
    q_aug = jnp.concatenate([(qh * scale).astype(BF16), madd.astype(BF16)], axis=1)

    prev0 = pl.multiple_of(jnp.maximum(nfar, 0) * blk, blk)
    own0 = pl.multiple_of(qb * blk, blk)
    s_prev = (lax.dot_general(q_aug, kb_ref[pl.ds(prev0, blk), :], nt, preferred_element_type=F32)
              + bprev_ref[...].reshape(rows2, blk) + jnp.where(qb > 0, 0.0, NEG))
    s_own = (lax.dot_general(q_aug, kb_ref[pl.ds(own0, blk), :], nt, preferred_element_type=F32)
             + bown_ref[...].reshape(rows2, blk))
    r = lax.broadcasted_iota(jnp.int32, (rows2, blk), 0)
    c = lax.broadcasted_iota(jnp.int32, (rows2, blk), 1)
    s_own = jnp.where(lax.bitwise_and(r, blk - 1) >= c, s_own, NEG)
    s = jnp.concatenate([s_prev, s_own], axis=1)
    m_i = jnp.max(s, axis=1, keepdims=True)
    p = jnp.exp(s - m_i)
    l_i = jnp.sum(p, axis=1, keepdims=True)
    v0 = jnp.concatenate([vb_ref[pl.ds(prev0, blk), :], vb_ref[pl.ds(own0, blk), :]], axis=0)
    acc = jnp.dot(p.astype(BF16), v0, preferred_element_type=F32)

    def body(it, carry):
        m_i, l_i, acc = carry
        k0 = pl.multiple_of(it * rows2, rows2)
        s = lax.dot_general(q_aug, kb_ref[pl.ds(k0, rows2), :], nt, preferred_element_type=F32)
        tail = jnp.where(2 * it + 1 < nfar, 0.0, NEG)
        s = jnp.concatenate([s[:, :blk], s[:, blk:] + tail], axis=1)
        m_new = jnp.maximum(m_i, jnp.max(s, axis=1, keepdims=True))
        alpha = jnp.exp(m_i - m_new)
        p = jnp.exp(s - m_new)
        l_new = alpha * l_i + jnp.sum(p, axis=1, keepdims=True)
        acc_new = alpha * acc + jnp.dot(p.astype(BF16), vb_ref[pl.ds(k0, rows2), :], preferred_element_type=F32)
        return m_new, l_new, acc_new

    m_i, l_i, acc = lax.fori_loop(0, (jnp.maximum(nfar, 0) + 1) // 2, body, (m_i, l_i, acc))
    out = acc / l_i
    o_ref[0] = jnp.where(first, out[:blk], out[blk:])


def moba_attention(p3d, rel_bias, *, q0=0, nq=None):
    bsz, seq, _ = p3d.shape
    blk = MOBA_BLOCK
    n_blocks = seq // blk
    nq = n_blocks - q0 if nq is None else nq
    assert n_blocks <= MOBA_LO and seq % blk == 0
    span = 2 * blk
    by_dist = rel_bias[:, _rel_bucket(jnp.arange(span))].astype(F32)
    shift = jnp.arange(span)

    def toeplitz(c):
        k = jnp.where(shift < blk, shift, shift - span)
        s = by_dist[:, jnp.clip(c - k, 0, span - 1)]
        tiled = jnp.tile(s, (1, blk))[:, :blk * (span - 1)]
        return tiled.reshape(HEADS, blk, span - 1)[:, :, :blk]

    bias_own = toeplitz(0)
    bias_prev = toeplitz(blk)
    bias_far = jnp.broadcast_to(rel_bias[:, REL_BUCKETS - 1].astype(F32)[:, None, None], (HEADS, 8, LANES))
    kern = functools.partial(_moba_kernel, n_blocks=n_blocks, q0=q0)
    return pl.pallas_call(
        kern,
        grid=(bsz, PAIRS, nq),
        in_specs=[
            pl.BlockSpec((1, blk, LANES), lambda b, h, i: (b, q0 + i, h)),
            pl.BlockSpec((1, seq, LANES), lambda b, h, i: (b, 0, PAIRS + h)),
            pl.BlockSpec((1, seq, LANES), lambda b, h, i: (b, 0, 2 * PAIRS + h)),
            pl.BlockSpec((2, blk, blk), lambda b, h, i: (h, 0, 0)),
            pl.BlockSpec((2, blk, blk), lambda b, h, i: (h, 0, 0)),
            pl.BlockSpec((2, 8, LANES), lambda b, h, i: (h, 0, 0)),
        ],
        out_specs=pl.BlockSpec((1, blk, LANES), lambda b, h, i: (b, i, h)),
        out_shape=jax.ShapeDtypeStruct((bsz, nq * blk, WIDTH), F32),
        scratch_shapes=[
            pltpu.VMEM((seq, 2 * LANES), BF16),
            pltpu.VMEM((seq, LANES), BF16),
            pltpu.VMEM((LANES, LANES), F32),
        ],
        compiler_params=_cparams(("parallel", "parallel", "arbitrary")),
        name="moba",
    )(p3d, p3d, p3d, bias_own, bias_prev, bias_far)


def _shifted(x, carry_row):
    rows = lax.broadcasted_iota(jnp.int32, x.shape, 0)
    return jnp.where(rows == 0, carry_row, pltpu.roll(x, 1, axis=0))


def _rwkv_prep_kernel(pr_ref, pk_ref, pv_ref, pl_ref, mu_ref, vec_ref, ww_ref, wa_ref, wg_ref,
                      bd_ref, tri_ref,
                      rt_ref, kt_ref, kd_ref, bd_out_ref, v_ref, g_ref, bonus_ref, pend_ref,
                      carry_ref, *, chunk):
    @pl.when(pl.program_id(1) == 0)
    def _():
        carry_ref[...] = jnp.zeros_like(carry_ref)

    def mix(ref, j):
        x = ref[0]
        mu = mu_ref[0:1, j * WIDTH:(j + 1) * WIDTH]
        prev = _shifted(x, carry_ref[0:1, j * WIDTH:(j + 1) * WIDTH])
        carry_ref[0:1, j * WIDTH:(j + 1) * WIDTH] = x[x.shape[0] - 1:, :]
        return x + mu * (prev - x)

    r = mix(pr_ref, 0)
    k = mix(pk_ref, 1)
    v = mix(pv_ref, 2)
    lo = mix(pl_ref, 3)
    w0, a0, k_k, k_a, r_k = (vec_ref[i:i + 1, :] for i in range(5))
    xwa = lo[:, 0:LANES]
    xg = lo[:, LANES:3 * LANES]
    lw = jnp.dot(jnp.tanh(xwa), ww_ref[...], precision=HI, preferred_element_type=F32)
    la = jnp.dot(xwa, wa_ref[...], precision=HI, preferred_element_type=F32)
    g = jnp.dot(jax.nn.sigmoid(xg), wg_ref[...], precision=HI, preferred_element_type=F32)
    z = -(w0 + lw)
    softplus = jnp.maximum(z, 0.0) + jnp.log(1.0 + jnp.exp(-jnp.abs(z)))
    logw = -jnp.exp(-softplus - 0.5)
    a = jax.nn.sigmoid(a0 + la)
    kk = k * k_k
    ss = jnp.dot(kk * kk, bd_ref[...], precision=HI, preferred_element_type=F32)
    kk = kk / jnp.maximum(jnp.sqrt(ss), 1e-12)
    k2 = k * (1.0 + (a - 1.0) * k_a)
    rk = jnp.dot(r * k2 * r_k, bd_ref[...], precision=HI, preferred_element_type=F32)
    cs = jnp.dot(tri_ref[...], logw, precision=HI, preferred_element_type=F32)
    e_pos = jnp.exp(cs)
    e_neg = jnp.exp(-cs)
    rt_ref[0] = r * e_pos
    kt_ref[0] = kk * jnp.exp(cs - logw)
    kd_ref[0] = k2 * e_neg
    bd_out_ref[0] = kk * a * e_neg
    v_ref[0] = v
    g_ref[0] = g
    bonus_ref[0] = rk * v
    ts = e_pos.shape[0]
    for c in range(ts // chunk):
        pend_ref[0, c:c + 1, :] = e_pos[(c + 1) * chunk - 1:(c + 1) * chunk, :]


def rwkv_prep(p3d, rwkv_mu, w0, w_lora_up, a0, a_lora_up, g_lora_up, k_k, k_a, r_k, *, ts=512):
    bsz, seq, _ = p3d.shape
    chunk = RWKV_CHUNK
    ts = min(ts, seq)
    mu = jnp.pad(rwkv_mu, (0, COL_B - COL_B_RAW)).reshape(1, COL_B)
    vec = jnp.stack([w0, a0, k_k, k_a, r_k.reshape(-1)] + [jnp.zeros_like(w0)] * 3).astype(F32)
    ww = jnp.zeros((LANES, WIDTH), F32).at[:DECAY_LORA].set(w_lora_up)
    wa = jnp.zeros((LANES, WIDTH), F32).at[DECAY_LORA:DECAY_LORA + AAA_LORA].set(a_lora_up)
    wg = jnp.zeros((2 * LANES, WIDTH), F32).at[:GATE_LORA].set(g_lora_up)
    hid = jnp.arange(WIDTH) // HEAD_DIM
    bd = (hid[:, None] == hid[None, :]).astype(F32)
    tix = jnp.arange(ts)
    tri = ((tix[:, None] // chunk == tix[None, :] // chunk) & (tix[None, :] <= tix[:, None])).astype(F32)
    c0 = COL_A // WIDTH
    big = jax.ShapeDtypeStruct((bsz, seq, WIDTH), F32)
    wspec = lambda shape: pl.BlockSpec(shape, lambda b, i: (0, 0))
    ospec = pl.BlockSpec((1, ts, WIDTH), lambda b, i: (b, i, 0))
    return pl.pallas_call(
        functools.partial(_rwkv_prep_kernel, chunk=chunk),
        grid=(bsz, seq // ts),
        in_specs=[
            pl.BlockSpec((1, ts, WIDTH), lambda b, i: (b, i, c0)),
            pl.BlockSpec((1, ts, WIDTH), lambda b, i: (b, i, c0 + 1)),
            pl.BlockSpec((1, ts, WIDTH), lambda b, i: (b, i, c0 + 2)),
            pl.BlockSpec((1, ts, WIDTH), lambda b, i: (b, i, c0 + 3)),
            wspec((1, COL_B)), wspec((8, WIDTH)), wspec((LANES, WIDTH)), wspec((LANES, WIDTH)),
            wspec((2 * LANES, WIDTH)), wspec((WIDTH, WIDTH)), wspec((ts, ts)),
        ],
        out_specs=[ospec] * 7 + [pl.BlockSpec((1, ts // chunk, WIDTH), lambda b, i: (b, i, 0))],
        out_shape=[big] * 7 + [jax.ShapeDtypeStruct((bsz, seq // chunk, WIDTH), F32)],
        scratch_shapes=[pltpu.VMEM((8, COL_B), F32)],
        compiler_params=_cparams(("parallel", "arbitrary")),
        name="rwkv_prep",
    )(p3d, p3d, p3d, p3d, mu, vec, ww, wa, wg, bd, tri)


def _rwkv_scan_kernel(rt_ref, kt_ref, kd_ref, bd_ref, v_ref, g_ref, bonus_ref, pend_ref, ln_ref, sin_ref,
                      o_ref, state_ref, *, chunk, prec):
    @pl.when(pl.program_id(1) == 0)
    def _():
        state_ref[...] = sin_ref[...]

    c2 = 2 * chunk
    lane = lax.broadcasted_iota(jnp.int32, (chunk, LANES), 1)
    first = lane < HEAD_DIM
    row = lax.broadcasted_iota(jnp.int32, (c2, c2), 0)
    col = lax.broadcasted_iota(jnp.int32, (c2, c2), 1)
    eye = (row == col).astype(F32)
    hrow = lax.broadcasted_iota(jnp.int32, (LANES, LANES), 0) // HEAD_DIM
    hcol = lax.broadcasted_iota(jnp.int32, (LANES, LANES), 1) // HEAD_DIM
    head_mean = jnp.where(hrow == hcol, 1.0 / HEAD_DIM, 0.0).astype(F32)
    nt = (((1,), (1,)), ((), ()))
    tn = (((0,), (0,)), ((), ()))
    dot = functools.partial(jnp.dot, precision=prec, preferred_element_type=F32)
    dotg = functools.partial(lax.dot_general, precision=prec, preferred_element_type=F32)

    def stack(x):
        return jnp.concatenate([jnp.where(first, x, 0.0), jnp.where(first, 0.0, x)], axis=0)

    pairs = range(PAIRS)
    sls = [slice(hp * LANES, (hp + 1) * LANES) for hp in pairs]
    rs, ks, kds, bs, vs = ([stack(ref[0, :, sl]) for sl in sls] for ref in (rt_ref, kt_ref, kd_ref, bd_ref, v_ref))
    hts = [state_ref[0, hp] for hp in pairs]
    big = [dotg(jnp.concatenate([ks[hp], rs[hp]], axis=0), jnp.concatenate([bs[hp], kds[hp]], axis=0), nt)
           for hp in pairs]
    a_b = [jnp.where(row > col, big[hp][0:c2, 0:c2], 0.0) for hp in pairs]
    a_k = [jnp.where(row > col, big[hp][0:c2, c2:], 0.0) for hp in pairs]
    a_rb = [jnp.where(row >= col, big[hp][c2:, 0:c2], 0.0) for hp in pairs]
    a_rk = [jnp.where(row >= col, big[hp][c2:, c2:], 0.0) for hp in pairs]
    kh = [dotg(jnp.concatenate([ks[hp], rs[hp]], axis=0), hts[hp], nt) for hp in pairs]
    av = [dot(jnp.concatenate([a_k[hp], a_rk[hp]], axis=0), vs[hp]) for hp in pairs]
    vk = [dotg(vs[hp], kds[hp], tn) for hp in pairs]
    inv = [eye - a_b[hp] for hp in pairs]
    pw = [dot(a_b[hp], a_b[hp]) for hp in pairs]
    n_sq = int(math.log2(chunk)) - 1
    for lvl in range(n_sq):
        if lvl + 1 < n_sq:
            both = [dot(jnp.concatenate([inv[hp], pw[hp]], axis=0), pw[hp]) for hp in pairs]
            inv = [inv[hp] + both[hp][0:c2] for hp in pairs]
            pw = [both[hp][c2:] for hp in pairs]
        else:
            inv = [inv[hp] + dot(inv[hp], pw[hp]) for hp in pairs]
    us = [dot(inv[hp], kh[hp][0:c2] + av[hp][0:c2]) for hp in pairs]
    ub = [dotg(us[hp], bs[hp], tn) for hp in pairs]
    au = [dot(a_rb[hp], us[hp]) for hp in pairs]
    for hp in pairs:
        sl = sls[hp]
        pend = pend_ref[0, 0, 0:1, sl]
        state_ref[0, hp] = (hts[hp] + vk[hp] - ub[hp]) * pend
        os_ = kh[hp][c2:] + av[hp][c2:] - au[hp]
        o = os_[0:chunk] + os_[chunk:]
        mu = jnp.dot(o, head_mean, precision=HI, preferred_element_type=F32)
        d = o - mu
        var = jnp.dot(d * d, head_mean, precision=HI, preferred_element_type=F32)
        on = d * lax.rsqrt(var + GN_EPS) * ln_ref[0:1, sl] + ln_ref[1:2, sl]
        o_ref[0, :, sl] = (on + bonus_ref[0, :, sl]) * g_ref[0, :, sl]


def rwkv_scan(rt, kt, kd, bd, v, g, bonus, pend, lnx_g, lnx_b, *, state=None, c0=0, nc=None, prec=None):
    bsz, seq, _ = rt.shape
    chunk = RWKV_CHUNK
    n_chunks = seq // chunk
    nc = n_chunks - c0 if nc is None else nc
    ln = jnp.stack([lnx_g, lnx_b] + [jnp.zeros_like(lnx_g)] * 6).astype(F32)
    pend4 = pend.reshape(bsz, n_chunks, 1, WIDTH)
    if state is None:
        state = jnp.zeros((bsz, PAIRS, LANES, LANES), F32)
    spec = pl.BlockSpec((1, chunk, WIDTH), lambda b, c: (b, c0 + c, 0))
    sspec = pl.BlockSpec((1, PAIRS, LANES, LANES), lambda b, c: (b, 0, 0, 0))
    return pl.pallas_call(
        functools.partial(_rwkv_scan_kernel, chunk=chunk, prec=prec),
        grid=(bsz, nc),
        in_specs=[spec] * 7 + [
            pl.BlockSpec((1, 1, 1, WIDTH), lambda b, c: (b, c0 + c, 0, 0)),
            pl.BlockSpec((8, WIDTH), lambda b, c: (0, 0)),
            sspec,
        ],
        out_specs=[pl.BlockSpec((1, chunk, WIDTH), lambda b, c: (b, c, 0)), sspec],
        out_shape=[jax.ShapeDtypeStruct((bsz, nc * chunk, WIDTH), F32),
                   jax.ShapeDtypeStruct((bsz, PAIRS, LANES, LANES), F32)],
        compiler_params=_cparams(("parallel", "arbitrary")),
        name="rwkv_scan",
    )(rt, kt, kd, bd, v, g, bonus, pend4, ln, state)


def _merge_kernel(x_ref, oa_ref, ob_ref, ga_ref, gb_ref, wa_ref, wb_ref, wo_ref, g2_ref,
                  h_ref, xn_ref, acc_ref):
    j = pl.program_id(1)

    @pl.when(j == 0)
    def _():
        acc_ref[...] = x_ref[...]

    ya = jnp.dot(oa_ref[...].astype(BF16), wa_ref[...], preferred_element_type=F32)
    yb = jnp.dot(ob_ref[...].astype(BF16), wb_ref[...], preferred_element_type=F32)
    y = jax.nn.sigmoid(ga_ref[...]) * ya + jax.nn.sigmoid(gb_ref[...]) * yb
    acc_ref[...] += jnp.dot(y.astype(BF16), wo_ref[...], preferred_element_type=F32)

    @pl.when(j == pl.num_programs(1) - 1)
    def _():
        h = acc_ref[...]
        h_ref[...] = h
        ms = jnp.mean(h * h, axis=-1, keepdims=True)
        xn_ref[...] = _pack_halves(h * lax.rsqrt(ms + RMS_EPS) * g2_ref[...])


def _pack_halves(x):
    half = x.shape[1] // 2
    lo = lax.bitcast_convert_type(x[:, :half].astype(BF16).astype(F32), jnp.int32)
    hi = lax.bitcast_convert_type(x[:, half:].astype(BF16).astype(F32), jnp.int32)
    return lax.bitwise_or(lax.shift_right_logical(lo, jnp.int32(16)), hi)


def _unpack_halves(words):
    lo, hi = _unpack_words(words)
    return jnp.concatenate([lo, hi], axis=1)


def merge_out(x2d, oa, ob, p2d, w_proj_a, w_proj_b, w_out, norm2_g, *, row0=0, prow0=0, tm=512):
    t, d = oa.shape[0], x2d.shape[1]
    r0 = row0 // tm
    p0 = prow0 // tm
    tn = WIDTH
    nj = d // tn
    g0 = COL_G_OFF // tn
    return pl.pallas_call(
        _merge_kernel,
        grid=(t // tm, nj),
        in_specs=[
            pl.BlockSpec((tm, d), lambda i, j: (r0 + i, 0)),
            pl.BlockSpec((tm, WIDTH), lambda i, j: (i, 0)),
            pl.BlockSpec((tm, WIDTH), lambda i, j: (i, 0)),
            pl.BlockSpec((tm, tn), lambda i, j: (p0 + i, g0 + j)),
            pl.BlockSpec((tm, tn), lambda i, j: (p0 + i, g0 + nj + j)),
            pl.BlockSpec((WIDTH, tn), lambda i, j: (0, j)),
            pl.BlockSpec((WIDTH, tn), lambda i, j: (0, j)),
            pl.BlockSpec((tn, d), lambda i, j: (j, 0)),
            pl.BlockSpec((1, d), lambda i, j: (0, 0)),
        ],
        out_specs=[pl.BlockSpec((tm, d), lambda i, j: (i, 0)), pl.BlockSpec((tm, d // 2), lambda i, j: (i, 0))],
        out_shape=[jax.ShapeDtypeStruct((t, d), F32), jax.ShapeDtypeStruct((t, d // 2), jnp.int32)],
        scratch_shapes=[pltpu.VMEM((tm, d), F32)],
        compiler_params=_cparams(("parallel", "arbitrary")),
        name="merge_out",
    )(x2d, oa, ob, p2d, p2d, w_proj_a.astype(BF16), w_proj_b.astype(BF16), w_out.astype(BF16),
      norm2_g.reshape(1, d))


PEER_HEADS = 8
PEER_NKEYS = 128
PEER_TOPK = 16
PEER_HALF = 128


def _topk_rows(s, k):
    n = s.shape[0]
    rows = lax.broadcasted_iota(jnp.int32, s.shape, 0).astype(F32)
    vals, ids = [], []
    for _ in range(k):
        m = jnp.max(s, axis=0, keepdims=True)
        first = jnp.min(jnp.where(s == m, rows, float(n)), axis=0, keepdims=True)
        vals.append(m)
        ids.append(first)
        s = jnp.where(rows == first, -jnp.inf, s)
    return jnp.concatenate(vals, axis=0), jnp.concatenate(ids, axis=0)


def _take_rows(table, ids):
    rows = lax.broadcasted_iota(jnp.int32, table.shape, 0).astype(F32)
    return jnp.sum(jnp.where(rows == ids, table, 0.0), axis=0, keepdims=True)


def _peer_route_kernel(xn_ref, wq_ref, sk_ref, idx_ref, gate_ref, *, prec):
    tt = xn_ref.shape[0]
    k = PEER_TOPK
    xn = _unpack_halves(xn_ref[...]) if xn_ref.dtype == jnp.int32 else xn_ref[...]
    q = jnp.dot(xn.astype(wq_ref.dtype), wq_ref[...], precision=prec, preferred_element_type=F32)
    nt = (((1,), (1,)), ((), ()))
    idx_rows, gate_rows = [], []
    half = k // 2
    for h in range(PEER_HEADS):
        tops = []
        for p in range(2):
            c0 = (h * 2 + p) * PEER_HALF
            s = lax.dot_general(sk_ref[h, p].astype(wq_ref.dtype), q[:, c0:c0 + PEER_HALF].astype(wq_ref.dtype),
                                nt, precision=prec, preferred_element_type=F32)
            tops.append(_topk_rows(s, k))
        (s0, i0), (s1, i1) = tops
        cs = [s0[0:1] + s1] + [s0[i:i + 1] + s1[0:half] for i in range(1, half)] + [s0[half:] + s1[0:1]]
        best_s, pos = _topk_rows(jnp.concatenate(cs, axis=0), k)
        mid = jnp.floor((pos - k) * (1.0 / half))
        end_mid = float(k + (half - 1) * half)
        i_rank = jnp.where(pos < k, 0.0, jnp.where(pos < end_mid, 1.0 + mid, pos - (end_mid - half)))
        j_rank = jnp.where(pos < k, pos, jnp.where(pos < end_mid, (pos - k) - half * mid, 0.0))
        ids = [_take_rows(i0, i_rank[n:n + 1]) * PEER_NKEYS + _take_rows(i1, j_rank[n:n + 1]) for n in range(k)]
        e = jnp.exp(best_s - best_s[0:1])
        gate_rows.append(e / jnp.sum(e, axis=0, keepdims=True))
        idx_rows.append(jnp.concatenate(ids, axis=0).astype(jnp.int32))
    idx_ref[...] = jnp.concatenate(idx_rows, axis=0).T
    gate_ref[...] = jnp.concatenate(gate_rows, axis=0).T


def peer_route(xn2d, peer_wq, peer_subkeys, *, tt=256, prec=None, wdtype=BF16):
    t, dx = xn2d.shape
    d, nq = peer_wq.shape
    n_sel = PEER_HEADS * PEER_TOPK
    return pl.pallas_call(
        functools.partial(_peer_route_kernel, prec=prec),
        grid=(t // tt,),
        in_specs=[
            pl.BlockSpec((tt, dx), lambda i: (i, 0)),
            pl.BlockSpec((d, nq), lambda i: (0, 0)),
            pl.BlockSpec((PEER_HEADS, 2, PEER_NKEYS, PEER_HALF), lambda i: (0, 0, 0, 0)),
        ],
        out_specs=[pl.BlockSpec((tt, n_sel), lambda i: (i, 0))] * 2,
        out_shape=[jax.ShapeDtypeStruct((t, n_sel), jnp.int32), jax.ShapeDtypeStruct((t, n_sel), F32)],
        compiler_params=_cparams(("parallel",)),
        name="peer_route",
    )(xn2d, peer_wq.astype(wdtype), peer_subkeys)


def _final_kernel(h_ref, y_ref, g_ref, *rest):
    o_ref = rest[-1]
    h = h_ref[...] + y_ref[...]
    ms = jnp.mean(h * h, axis=-1, keepdims=True)
    o_ref[...] = h * lax.rsqrt(ms + RMS_EPS) * g_ref[...]


def final_norm(h2d, y2d, g, *, out=None, row0=0, total_rows=None, tm=1024):
    t, d = h2d.shape
    total = t if total_rows is None else total_rows
    r0 = row0 // tm
    spec = pl.BlockSpec((tm, d), lambda i: (i, 0))
    in_specs = [spec, spec, pl.BlockSpec((1, d), lambda i: (0, 0))]
    args = [h2d, y2d, g.reshape(1, d)]
    aliases = {}
    if out is not None:
        in_specs.append(pl.BlockSpec(memory_space=pl.ANY))
        args.append(out)
        aliases = {3: 0}
    return pl.pallas_call(
        _final_kernel,
        grid=(t // tm,),
        in_specs=in_specs,
        out_specs=pl.BlockSpec((tm, d), lambda i: (r0 + i, 0)),
        out_shape=jax.ShapeDtypeStruct((total, d), F32),
        input_output_aliases=aliases,
        compiler_params=_cparams(("parallel",)),
        name="final_norm",
    )(*args)


SC_CORES = 2
SC_SUBCORES = 16
SC_LANES = 16
SC_WORKERS = SC_CORES * SC_SUBCORES
PEER_SEL = PEER_HEADS * PEER_TOPK
PEER_ROWS = 32
PEER_PARTS = PEER_SEL // PEER_ROWS
PEER_NBUF = 4
PEER_GROUP = 32
PEER_BF16_RUN = 4


def _pack_rows(w):
    half = w.shape[1] // 2
    bits = lax.bitcast_convert_type(w.astype(BF16), jnp.uint16).astype(jnp.uint32)
    return lax.bitcast_convert_type(bits[:, :half] | (bits[:, half:] << 16), jnp.int32)


def _unpack_words(w):
    lo = lax.bitcast_convert_type(lax.shift_left(w, jnp.int32(16)), F32)
    hi = lax.bitcast_convert_type(lax.bitwise_and(w, jnp.int32(-65536)), F32)
    return lo, hi


def _packed_dot(a_words, b_words):
    from jax.experimental.pallas import tpu_sc as plsc
    prods = [plsc.bitcast(a, BF16) * plsc.bitcast(b, BF16) for a, b in zip(a_words, b_words)]
    while len(prods) > 1:
        prods = [prods[k] + prods[k + 1] for k in range(0, len(prods), 2)]
    return _unpack_words(plsc.bitcast(prods[0], jnp.int32))


def _sc_mesh():
    from jax.experimental.pallas import tpu_sc as plsc
    return plsc.VectorSubcoreMesh(core_axis_name="c", subcore_axis_name="s",
                                  num_cores=SC_CORES, num_subcores=SC_SUBCORES)


def _sc_loop(n, body, carry):
    from jax.experimental.pallas import tpu_sc as plsc
    return plsc.parallel_loop(0, n, carry=carry)(body)


def _worker_base(tokens_per_worker):
    return (lax.axis_index("s") * SC_CORES + lax.axis_index("c")) * tokens_per_worker


def _gather_compute_loop(table_hbm, idx_v, rows_v, sem, stage_v, out_row, osem, grp, compute):
    n_gathers = PEER_PARTS * grp
    ahead = PEER_NBUF - 1

    def gather(j, b):
        i = j // PEER_PARTS if isinstance(j, int) else lax.shift_right_logical(j, PEER_PARTS.bit_length() - 1)
        h = j % PEER_PARTS if isinstance(j, int) else lax.bitwise_and(j, PEER_PARTS - 1)
        ids = idx_v.at[i, pl.ds(pl.multiple_of(h * PEER_ROWS, PEER_ROWS), PEER_ROWS)]
        return pltpu.make_async_copy(table_hbm.at[ids], rows_v.at[b], sem.at[b])

    def put(i, slot):
        return pltpu.make_async_copy(stage_v.at[slot], out_row(i), osem.at[slot])

    for j in range(ahead):
        gather(j, j).start()

    @pl.loop(0, n_gathers)
    def _(j):
        b = lax.bitwise_and(j, PEER_NBUF - 1)
        h = lax.bitwise_and(j, PEER_PARTS - 1)
        i = lax.shift_right_logical(j, PEER_PARTS.bit_length() - 1)
        slot = lax.bitwise_and(i, 1)

        @pl.when((h == 0) & (i >= 2))
        def _():
            put(i - 2, slot).wait()

        @pl.when(j + ahead < n_gathers)
        def _():
            gather(j + ahead, lax.bitwise_and(j + ahead, PEER_NBUF - 1)).start()

        gather(j, b).wait()
        compute(i, h, b, slot)

        @pl.when(h == PEER_PARTS - 1)
        def _():
            put(i, slot).start()

    put(grp - 2, 0).wait()
    put(grp - 1, 1).wait()


def peer_expert_dots(x_packed, idx, u_packed):
    t, half = x_packed.shape
    n_chunks = half // SC_LANES
    tpw = t // SC_WORKERS
    grp = min(PEER_GROUP, tpw)
    rows_tog = 4

    def body(x_hbm, idx_hbm, u_hbm, out_hbm, idx_v, x_v, rows_v, ps_v, sem, osem):
        base = _worker_base(tpw)

        def compute(i, h, b, slot):
            @pl.loop(0, PEER_ROWS // rows_tog)
            def _(rg):
                r0 = rg * rows_tog
                accs = [[None, None] for _ in range(rows_tog)]
                for c0 in range(0, n_chunks, PEER_BF16_RUN):
                    ats = [pl.ds((c0 + k) * SC_LANES, SC_LANES) for k in range(PEER_BF16_RUN)]
                    xw = [x_v[i, at] for at in ats]
                    for r in range(rows_tog):
                        terms = _packed_dot([rows_v[b, r0 + r, at] for at in ats], xw)
                        for k, term in enumerate(terms):
                            accs[r][k] = term if accs[r][k] is None else accs[r][k] + term
                for r in range(rows_tog):
                    at = pl.ds(pl.multiple_of((h * PEER_ROWS + r0 + r) * SC_LANES, SC_LANES), SC_LANES)
                    ps_v[slot, at] = accs[r][0] + accs[r][1]

        @pl.loop(0, tpw // grp)
        def _(g):
            t0 = base + g * grp
            pltpu.sync_copy(idx_hbm.at[pl.ds(t0, grp)], idx_v)
            pltpu.sync_copy(x_hbm.at[pl.ds(t0, grp)], x_v)
            _gather_compute_loop(u_hbm, idx_v, rows_v, sem, ps_v, lambda i: out_hbm.at[t0 + i], osem, grp, compute)

    return pl.kernel(
        body,
        out_type=jax.ShapeDtypeStruct((t, PEER_SEL * SC_LANES), F32),
        mesh=_sc_mesh(),
        scratch_types=[
            pltpu.VMEM((grp, PEER_SEL), jnp.int32),
            pltpu.VMEM((grp, half), jnp.int32),
            pltpu.VMEM((PEER_NBUF, PEER_ROWS, half), jnp.int32),
            pltpu.VMEM((2, PEER_SEL * SC_LANES), F32),
            pltpu.SemaphoreType.DMA((PEER_NBUF,)),
            pltpu.SemaphoreType.DMA((2,)),
        ],
        compiler_params=pltpu.CompilerParams(needs_layout_passes=False),
        name="peer_expert_dots",
    )(x_packed, idx, u_packed)


def peer_expert_mix(hgx, idx, v_packed):
    t = hgx.shape[0]
    half = v_packed.shape[1]
    d = 2 * half
    tpw = t // SC_WORKERS
    grp = min(PEER_GROUP // 2, tpw)
    n_parts = 2
    cpp = half // SC_LANES // n_parts

    def body(hg_hbm, idx_hbm, v_hbm, out_hbm, idx_v, hg_v, rows_v, o_v2, sem, osem):
        base = _worker_base(tpw)

        def compute(i, h, b, slot):
            for part in range(n_parts):
                def rbody(rq, accs):
                    r0 = rq * PEER_BF16_RUN
                    s = [hg_v[i, pl.ds(pl.multiple_of((h * PEER_ROWS + r0 + k) * SC_LANES, SC_LANES), SC_LANES)]
                         for k in range(PEER_BF16_RUN)]
                    new = []
                    for c in range(cpp):
                        at = pl.ds((part * cpp + c) * SC_LANES, SC_LANES)
                        lo, hi = _packed_dot([rows_v[b, r0 + k, at] for k in range(PEER_BF16_RUN)], s)
                        new.append(accs[2 * c] + lo)
                        new.append(accs[2 * c + 1] + hi)
                    return tuple(new)

                accs = _sc_loop(PEER_ROWS // PEER_BF16_RUN, rbody,
                                tuple(jnp.zeros((SC_LANES,), F32) for _ in range(2 * cpp)))
                def store(overwrite):
                    for c in range(cpp):
                        lo_at = pl.ds((part * cpp + c) * SC_LANES, SC_LANES)
                        hi_at = pl.ds(half + (part * cpp + c) * SC_LANES, SC_LANES)
                        if overwrite:
                            o_v2[slot, lo_at] = accs[2 * c]
                            o_v2[slot, hi_at] = accs[2 * c + 1]
                        else:
                            o_v2[slot, lo_at] = o_v2[slot, lo_at] + accs[2 * c]
                            o_v2[slot, hi_at] = o_v2[slot, hi_at] + accs[2 * c + 1]

                pl.when(h == 0)(functools.partial(store, True))
                pl.when(h != 0)(functools.partial(store, False))

        @pl.loop(0, tpw // grp)
        def _(g):
            t0 = base + g * grp
            pltpu.sync_copy(idx_hbm.at[pl.ds(t0, grp)], idx_v)
            pltpu.sync_copy(hg_hbm.at[pl.ds(t0, grp)], hg_v)
            _gather_compute_loop(v_hbm, idx_v, rows_v, sem, o_v2, lambda i: out_hbm.at[t0 + i], osem, grp, compute)

    return pl.kernel(
        body,
        out_type=jax.ShapeDtypeStruct((t, d), F32),
        mesh=_sc_mesh(),
        scratch_types=[
            pltpu.VMEM((grp, PEER_SEL), jnp.int32),
            pltpu.VMEM((grp, PEER_SEL * SC_LANES), jnp.int32),
            pltpu.VMEM((PEER_NBUF, PEER_ROWS, half), jnp.int32),
            pltpu.VMEM((2, d), F32),
            pltpu.SemaphoreType.DMA((PEER_NBUF,)),
            pltpu.SemaphoreType.DMA((2,)),
        ],
        compiler_params=pltpu.CompilerParams(needs_layout_passes=False),
        name="peer_expert_mix",
    )(hgx, idx, v_packed)


def _peer_act_kernel(ps_ref, gate_ref, sum_ref, o_ref):
    ps = ps_ref[...]
    sel = sum_ref[...]
    hi = ps.astype(BF16)
    rest = ps - hi.astype(F32)
    mid = rest.astype(BF16)
    lo = (rest - mid.astype(F32)).astype(BF16)
    pre = (jnp.dot(hi, sel, preferred_element_type=F32) + jnp.dot(mid, sel, preferred_element_type=F32)
           + jnp.dot(lo, sel, preferred_element_type=F32))
    hg = 0.5 * pre * (1.0 + lax.erf(pre * (1.0 / math.sqrt(2.0)))) * gate_ref[...]
    spread = (((1,), (1,)), ((), ()))
    hgx = lax.dot_general(hg.astype(BF16), sel, spread, preferred_element_type=F32)
    bits = lax.bitcast_convert_type(hgx, jnp.int32)
    o_ref[...] = lax.bitwise_or(bits, lax.shift_right_logical(bits, jnp.int32(16)))


def peer_act(ps, gates, *, tm=512):
    t, n = ps.shape
    lane_sum = (jnp.arange(n)[:, None] // SC_LANES == jnp.arange(PEER_SEL)[None, :]).astype(BF16)
    return pl.pallas_call(
        _peer_act_kernel,
        grid=(t // tm,),
        in_specs=[
            pl.BlockSpec((tm, n), lambda i: (i, 0)),
            pl.BlockSpec((tm, PEER_SEL), lambda i: (i, 0)),
            pl.BlockSpec((n, PEER_SEL), lambda i: (0, 0)),
        ],
        out_specs=pl.BlockSpec((tm, n), lambda i: (i, 0)),
        out_shape=jax.ShapeDtypeStruct((t, n), jnp.int32),
        compiler_params=_cparams(("parallel",)),
        name="peer_act",
    )(ps, gates, lane_sum)


BATCH_GROUPS = 8


def kernel(x, norm1_g, w_in, rwkv_mu, w0, w_lora_up, a0, a_lora_up, g_lora_up, k_k, k_a, r_k, lnx_g, lnx_b,
           w_proj_a, w_proj_b, w_out, norm2_g, peer_wq, peer_subkeys, peer_u, peer_v, rel_bias, normf_g):
    bsz, seq, d = x.shape
    depth = norm1_g.shape[0]
    groups = BATCH_GROUPS if bsz % BATCH_GROUPS == 0 else 1
    gb = bsz // groups
    tg = gb * seq
    t = bsz * seq
    src = x.reshape(t, d)
    for l in range(depth):
        w_pad = jnp.concatenate([
            w_in[l][:, :COL_A + COL_B_RAW],
            jnp.zeros((d, COL_B - COL_B_RAW), w_in.dtype),
            w_in[l][:, COL_A + COL_B_RAW:]], axis=1).astype(BF16)
        u_packed = _pack_rows(peer_u[l])
        tables = {}
        last = l == depth - 1

        def mix(pending, tie=None):
            row0, h2d, ps, gates, idx = pending
            hgx = peer_act(ps, gates)
            if "v" not in tables:
                v_src = peer_v[l]
                if tie is not None:
                    tie, v_src = lax.optimization_barrier((tie, v_src))
                tables["v"] = _pack_rows(v_src)
            if tie is not None:
                tie, hgx = lax.optimization_barrier((tie, hgx))
            return tie, (row0, h2d, peer_expert_mix(hgx, idx, tables["v"]))

        outs = []

        def close(mixed):
            row0, h2d, y2d = mixed
            if last:
                outs.append(final_norm(h2d, y2d, normf_g, out=outs[-1] if outs else None, row0=row0, total_rows=t))
            else:
                outs.append(h2d + y2d)

        halves = gb == 1 and seq % (2 * MOBA_BLOCK) == 0 and (seq // 2) % (SC_WORKERS * PEER_GROUP) == 0

        pending = closing = None
        for g in range(groups):
            p2d = norm_proj(src, norm1_g[l], w_pad, row0=g * tg, rows=tg)
            p3d = p2d.reshape(gb, seq, -1)
            prep = state = None
            for s0, sn in ([(0, seq // 2), (seq // 2, seq // 2)] if halves and g == 0 else [(0, seq)]):
                oa = moba_attention(p3d, rel_bias, q0=s0 // MOBA_BLOCK, nq=sn // MOBA_BLOCK)
                mixed = None
                if pending is not None:
                    oa, mixed = mix(pending, oa)
                if closing is not None:
                    oa, y2d = lax.optimization_barrier((oa, closing[2]))
                    close(closing[:2] + (y2d,))
                    closing = None
                if prep is None:
                    prep = rwkv_prep(p3d, rwkv_mu[l], w0[l], w_lora_up[l], a0[l], a_lora_up[l], g_lora_up[l],
                                     k_k[l], k_a[l], r_k[l])
                ob, state = rwkv_scan(*prep, lnx_g[l], lnx_b[l], state=state,
                                      c0=s0 // RWKV_CHUNK, nc=sn // RWKV_CHUNK)
                nt = gb * sn
                h2d, xn2 = merge_out(src, oa.reshape(nt, WIDTH), ob.reshape(nt, WIDTH), p2d, w_proj_a[l], w_proj_b[l],
                                     w_out[l], norm2_g[l], row0=g * tg + s0, prow0=s0)
                idx, gates = peer_route(xn2, peer_wq[l], peer_subkeys[l])
                if mixed is not None:
                    idx, y2d = lax.optimization_barrier((idx, mixed[2]))
                    closing = mixed[:2] + (y2d,)
                pending = (g * tg + s0, h2d, peer_expert_dots(xn2, idx, u_packed), gates, idx)
        if closing is not None:
            close(closing)
        close(mix(pending)[1])
        src = outs[-1] if last else jnp.concatenate(outs, axis=0)
    return src.reshape(bsz, seq, d)
```

```python
import functools
import math

import jax
import jax.numpy as jnp
from jax import lax
from jax.experimental import pallas as pl
from jax.experimental.pallas import tpu as pltpu

F32 = jnp.float32
BF16 = jnp.bfloat16
HI = lax.Precision.HIGHEST

LANES = 128
HEAD_DIM = 64
HEADS = 8
PAIRS = HEADS // 2
WIDTH = HEADS * HEAD_DIM
MOBA_BLOCK = 256
MOBA_TOPK = 3
MOBA_LO = 64
REL_BUCKETS = 32
REL_MAX_DIST = 128
DECAY_LORA = 64
AAA_LORA = 64
GATE_LORA = 160
GN_EPS = 64e-5
RMS_EPS = 1e-6
NEG = -1e30
RWKV_CHUNK = 64
COL_A = 3 * WIDTH
COL_B_RAW = 3 * WIDTH + DECAY_LORA + AAA_LORA + GATE_LORA
COL_B = 4 * WIDTH
COL_G_OFF = COL_A + COL_B
VMEM_LIMIT = 56 * 1024 * 1024


def _cparams(sem):
    return pltpu.CompilerParams(dimension_semantics=sem, vmem_limit_bytes=VMEM_LIMIT)


def _norm_proj_kernel(x_ref, g_ref, w_ref, o_ref, xn_ref):
    @pl.when(pl.program_id(1) == 0)
    def _():
        x = x_ref[...]
        ms = jnp.mean(x * x, axis=-1, keepdims=True)
        xn_ref[...] = (x * lax.rsqrt(ms + RMS_EPS) * g_ref[...]).astype(xn_ref.dtype)

    o_ref[...] = jnp.dot(xn_ref[...], w_ref[...], preferred_element_type=F32).astype(o_ref.dtype)


def norm_proj(x2d, g, w, *, row0=0, rows=None, tm=512, tn=512, out_dtype=F32):
    d = x2d.shape[1]
    t = x2d.shape[0] if rows is None else rows
    n = w.shape[1]
    r0 = row0 // tm
    return pl.pallas_call(
        _norm_proj_kernel,
        grid=(t // tm, n // tn),
        in_specs=[
            pl.BlockSpec((tm, d), lambda i, j: (r0 + i, 0)),
            pl.BlockSpec((1, d), lambda i, j: (0, 0)),
            pl.BlockSpec((d, tn), lambda i, j: (0, j)),
        ],
        out_specs=pl.BlockSpec((tm, tn), lambda i, j: (i, j)),
        out_shape=jax.ShapeDtypeStruct((t, n), out_dtype),
        scratch_shapes=[pltpu.VMEM((tm, d), w.dtype)],
        compiler_params=_cparams(("parallel", "arbitrary")),
        name="norm_proj",
    )(x2d, g.reshape(1, d), w)


def _rel_bucket(dist):
    n = jnp.maximum(dist, 0)
    max_exact = REL_BUCKETS // 2
    nf = jnp.maximum(n, 1).astype(F32)
    large = max_exact + (jnp.log(nf / max_exact) / math.log(REL_MAX_DIST / max_exact)
                         * (REL_BUCKETS - max_exact)).astype(jnp.int32)
    large = jnp.minimum(large, REL_BUCKETS - 1)
    return jnp.where(n < max_exact, n, large)


def _moba_kernel(q_ref, k_ref, v_ref, bown_ref, bprev_ref, bfar_ref, o_ref,
                 kb_ref, vb_ref, kbar_ref, *, n_blocks, q0):
    qb = pl.program_id(2) + q0
    blk = MOBA_BLOCK
    scale = 1.0 / math.sqrt(HEAD_DIM)

    rows2 = 2 * blk
    nt = (((1,), (1,)), ((), ()))

    @pl.when(pl.program_id(2) == 0)
    def _():
        kbar_ref[...] = jnp.zeros_like(kbar_ref)
        lane_b = lax.broadcasted_iota(jnp.int32, (blk, LANES), 1)
        for n in range(n_blocks):
            kblk = k_ref[0, n * blk:(n + 1) * blk, :]
            kbar_ref[n:n + 1, :] = jnp.mean(kblk, axis=0, keepdims=True)
            kb_ref[n * blk:(n + 1) * blk, 0:LANES] = kblk.astype(BF16)
            kb_ref[n * blk:(n + 1) * blk, LANES:] = ((lane_b == n) | (lane_b == MOBA_LO + n)).astype(BF16)
        vb_ref[...] = v_ref[0].astype(BF16)

    q2 = q_ref[0]
    first = lax.broadcasted_iota(jnp.int32, (blk, LANES), 1) < HEAD_DIM
    qh = jnp.concatenate([jnp.where(first, q2, 0.0), jnp.where(first, 0.0, q2)], axis=0)
    lane = lax.broadcasted_iota(jnp.int32, (rows2, LANES), 1)
    rowi = lax.broadcasted_iota(jnp.int32, (rows2, LANES), 0)
    gate = lax.dot_general(qh.astype(BF16), kbar_ref[...].astype(BF16), nt, preferred_element_type=F32)
    g = jnp.where(lane < qb, gate, -jnp.inf)
    chosen = lane < 0
    lane_f = lane.astype(F32)
    for _ in range(MOBA_TOPK):
        m = jnp.max(g, axis=1, keepdims=True)
        idx = jnp.min(jnp.where(g == m, lane_f, float(LANES)), axis=1, keepdims=True)
        hit = (lane_f == idx) & (m > -jnp.inf)
        chosen = chosen | hit
        g = jnp.where(hit, -jnp.inf, g)
    nfar = qb - 1
    bfar = jnp.where(rowi < blk, bfar_ref[0, 0:1, 0:1], bfar_ref[1, 0:1, 0:1])
    bhi = bfar.astype(BF16).astype(F32)
    madd = jnp.where(lane < nfar, jnp.where(chosen, bhi, NEG),
                     jnp.where(lane == nfar, jnp.where(chosen, 0.0, NEG),
                               jnp.where((lane >= MOBA_LO) & (lane - MOBA_LO < nfar), bfar - bhi, 0.0)))
    q_aug = jnp.concatenate([(qh * scale).astype(BF16), madd.astype(BF16)], axis=1)

    prev0 = pl.multiple_of(jnp.maximum(nfar, 0) * blk, blk)
    own0 = pl.multiple_of(qb * blk, blk)
    s_prev = (lax.dot_general(q_aug, kb_ref[pl.ds(prev0, blk), :], nt, preferred_element_type=F32)
              + bprev_ref[...].reshape(rows2, blk) + jnp.where(qb > 0, 0.0, NEG))
    s_own = (lax.dot_general(q_aug, kb_ref[pl.ds(own0, blk), :], nt, preferred_element_type=F32)
             + bown_ref[...].reshape(rows2, blk))
    r = lax.broadcasted_iota(jnp.int32, (rows2, blk), 0)
    c = lax.broadcasted_iota(jnp.int32, (rows2, blk), 1)
    s_own = jnp.where(lax.bitwise_and(r, blk - 1) >= c, s_own, NEG)
    s = jnp.concatenate([s_prev, s_own], axis=1)
    m_i = jnp.max(s, axis=1, keepdims=True)
    p = jnp.exp(s - m_i)
    l_i = jnp.sum(p, axis=1, keepdims=True)
    v0 = jnp.concatenate([vb_ref[pl.ds(prev0, blk), :], vb_ref[pl.ds(own0, blk), :]], axis=0)
    acc = jnp.dot(p.astype(BF16), v0, preferred_element_type=F32)

    def body(it, carry):
        m_i, l_i, acc = carry
        k0 = pl.multiple_of(it * rows2, rows2)
        s = lax.dot_general(q_aug, kb_ref[pl.ds(k0, rows2), :], nt, preferred_element_type=F32)
        tail = jnp.where(2 * it + 1 < nfar, 0.0, NEG)
        s = jnp.concatenate([s[:, :blk], s[:, blk:] + tail], axis=1)
        m_new = jnp.maximum(m_i, jnp.max(s, axis=1, keepdims=True))
        alpha = jnp.exp(m_i - m_new)
        p = jnp.exp(s - m_new)
        l_new = alpha * l_i + jnp.sum(p, axis=1, keepdims=True)
        acc_new = alpha * acc + jnp.dot(p.astype(BF16), vb_ref[pl.ds(k0, rows2), :], preferred_element_type=F32)
        return m_new, l_new, acc_new

    m_i, l_i, acc = lax.fori_loop(0, (jnp.maximum(nfar, 0) + 1) // 2, body, (m_i, l_i, acc))
    out = acc / l_i
    o_ref[0] = jnp.where(first, out[:blk], out[blk:])


def moba_attention(p3d, rel_bias, *, q0=0, nq=None):
    bsz, seq, _ = p3d.shape
    blk = MOBA_BLOCK
    n_blocks = seq // blk
    nq = n_blocks - q0 if nq is None else nq
    assert n_blocks <= MOBA_LO and seq % blk == 0
    span = 2 * blk
    by_dist = rel_bias[:, _rel_bucket(jnp.arange(span))].astype(F32)
    shift = jnp.arange(span)

    def toeplitz(c):
        k = jnp.where(shift < blk, shift, shift - span)
        s = by_dist[:, jnp.clip(c - k, 0, span - 1)]
        tiled = jnp.tile(s, (1, blk))[:, :blk * (span - 1)]
        return tiled.reshape(HEADS, blk, span - 1)[:, :, :blk]

    bias_own = toeplitz(0)
    bias_prev = toeplitz(blk)
    bias_far = jnp.broadcast_to(rel_bias[:, REL_BUCKETS - 1].astype(F32)[:, None, None], (HEADS, 8, LANES))
    kern = functools.partial(_moba_kernel, n_blocks=n_blocks, q0=q0)
    return pl.pallas_call(
        kern,
        grid=(bsz, PAIRS, nq),
        in_specs=[
            pl.BlockSpec((1, blk, LANES), lambda b, h, i: (b, q0 + i, h)),
            pl.BlockSpec((1, seq, LANES), lambda b, h, i: (b, 0, PAIRS + h)),
            pl.BlockSpec((1, seq, LANES), lambda b, h, i: (b, 0, 2 * PAIRS + h)),
            pl.BlockSpec((2, blk, blk), lambda b, h, i: (h, 0, 0)),
            pl.BlockSpec((2, blk, blk), lambda b, h, i: (h, 0, 0)),
            pl.BlockSpec((2, 8, LANES), lambda b, h, i: (h, 0, 0)),
        ],
        out_specs=pl.BlockSpec((1, blk, LANES), lambda b, h, i: (b, i, h)),
        out_shape=jax.ShapeDtypeStruct((bsz, nq * blk, WIDTH), F32),
        scratch_shapes=[
            pltpu.VMEM((seq, 2 * LANES), BF16),
            pltpu.VMEM((seq, LANES), BF16),
            pltpu.VMEM((LANES, LANES), F32),
        ],
        compiler_params=_cparams(("parallel", "parallel", "arbitrary")),
        name="moba",
    )(p3d, p3d, p3d, bias_own, bias_prev, bias_far)


def _shifted(x, carry_row):
    rows = lax.broadcasted_iota(jnp.int32, x.shape, 0)
    return jnp.where(rows == 0, carry_row, pltpu.roll(x, 1, axis=0))


def _rwkv_prep_kernel(pr_ref, pk_ref, pv_ref, pl_ref, mu_ref, vec_ref, ww_ref, wa_ref, wg_ref,
                      bd_ref, tri_ref,
                      rt_ref, kt_ref, kd_ref, bd_out_ref, v_ref, g_ref, bonus_ref, pend_ref,
                      carry_ref, *, chunk):
    @pl.when(pl.program_id(1) == 0)
    def _():
        carry_ref[...] = jnp.zeros_like(carry_ref)

    def mix(ref, j):
        x = ref[0]
        mu = mu_ref[0:1, j * WIDTH:(j + 1) * WIDTH]
        prev = _shifted(x, carry_ref[0:1, j * WIDTH:(j + 1) * WIDTH])
        carry_ref[0:1, j * WIDTH:(j + 1) * WIDTH] = x[x.shape[0] - 1:, :]
        return x + mu * (prev - x)

    r = mix(pr_ref, 0)
    k = mix(pk_ref, 1)
    v = mix(pv_ref, 2)
    lo = mix(pl_ref, 3)
    w0, a0, k_k, k_a, r_k = (vec_ref[i:i + 1, :] for i in range(5))
    xwa = lo[:, 0:LANES]
    xg = lo[:, LANES:3 * LANES]
    lw = jnp.dot(jnp.tanh(xwa), ww_ref[...], precision=HI, preferred_element_type=F32)
    la = jnp.dot(xwa, wa_ref[...], precision=HI, preferred_element_type=F32)
    g = jnp.dot(jax.nn.sigmoid(xg), wg_ref[...], precision=HI, preferred_element_type=F32)
    z = -(w0 + lw)
    softplus = jnp.maximum(z, 0.0) + jnp.log(1.0 + jnp.exp(-jnp.abs(z)))
    logw = -jnp.exp(-softplus - 0.5)
    a = jax.nn.sigmoid(a0 + la)
    kk = k * k_k
    ss = jnp.dot(kk * kk, bd_ref[...], precision=HI, preferred_element_type=F32)
    kk = kk / jnp.maximum(jnp.sqrt(ss), 1e-12)
    k2 = k * (1.0 + (a - 1.0) * k_a)
    rk = jnp.dot(r * k2 * r_k, bd_ref[...], precision=HI, preferred_element_type=F32)
    cs = jnp.dot(tri_ref[...], logw, precision=HI, preferred_element_type=F32)
    e_pos = jnp.exp(cs)
    e_neg = jnp.exp(-cs)
    rt_ref[0] = r * e_pos
    kt_ref[0] = kk * jnp.exp(cs - logw)
    kd_ref[0] = k2 * e_neg
    bd_out_ref[0] = kk * a * e_neg
    v_ref[0] = v
    g_ref[0] = g
    bonus_ref[0] = rk * v
    ts = e_pos.shape[0]
    for c in range(ts // chunk):
        pend_ref[0, c:c + 1, :] = e_pos[(c + 1) * chunk - 1:(c + 1) * chunk, :]


def rwkv_prep(p3d, rwkv_mu, w0, w_lora_up, a0, a_lora_up, g_lora_up, k_k, k_a, r_k, *, ts=512):
    bsz, seq, _ = p3d.shape
    chunk = RWKV_CHUNK
    ts = min(ts, seq)
    mu = jnp.pad(rwkv_mu, (0, COL_B - COL_B_RAW)).reshape(1, COL_B)
    vec = jnp.stack([w0, a0, k_k, k_a, r_k.reshape(-1)] + [jnp.zeros_like(w0)] * 3).astype(F32)
    ww = jnp.zeros((LANES, WIDTH), F32).at[:DECAY_LORA].set(w_lora_up)
    wa = jnp.zeros((LANES, WIDTH), F32).at[DECAY_LORA:DECAY_LORA + AAA_LORA].set(a_lora_up)
    wg = jnp.zeros((2 * LANES, WIDTH), F32).at[:GATE_LORA].set(g_lora_up)
    hid = jnp.arange(WIDTH) // HEAD_DIM
    bd = (hid[:, None] == hid[None, :]).astype(F32)
    tix = jnp.arange(ts)
    tri = ((tix[:, None] // chunk == tix[None, :] // chunk) & (tix[None, :] <= tix[:, None])).astype(F32)
    c0 = COL_A // WIDTH
    big = jax.ShapeDtypeStruct((bsz, seq, WIDTH), F32)
    wspec = lambda shape: pl.BlockSpec(shape, lambda b, i: (0, 0))
    ospec = pl.BlockSpec((1, ts, WIDTH), lambda b, i: (b, i, 0))
    return pl.pallas_call(
        functools.partial(_rwkv_prep_kernel, chunk=chunk),
        grid=(bsz, seq // ts),
        in_specs=[
            pl.BlockSpec((1, ts, WIDTH), lambda b, i: (b, i, c0)),
            pl.BlockSpec((1, ts, WIDTH), lambda b, i: (b, i, c0 + 1)),
            pl.BlockSpec((1, ts, WIDTH), lambda b, i: (b, i, c0 + 2)),
            pl.BlockSpec((1, ts, WIDTH), lambda b, i: (b, i, c0 + 3)),
            wspec((1, COL_B)), wspec((8, WIDTH)), wspec((LANES, WIDTH)), wspec((LANES, WIDTH)),
            wspec((2 * LANES, WIDTH)), wspec((WIDTH, WIDTH)), wspec((ts, ts)),
        ],
        out_specs=[ospec] * 7 + [pl.BlockSpec((1, ts // chunk, WIDTH), lambda b, i: (b, i, 0))],
        out_shape=[big] * 7 + [jax.ShapeDtypeStruct((bsz, seq // chunk, WIDTH), F32)],
        scratch_shapes=[pltpu.VMEM((8, COL_B), F32)],
        compiler_params=_cparams(("parallel", "arbitrary")),
        name="rwkv_prep",
    )(p3d, p3d, p3d, p3d, mu, vec, ww, wa, wg, bd, tri)


def _rwkv_scan_kernel(rt_ref, kt_ref, kd_ref, bd_ref, v_ref, g_ref, bonus_ref, pend_ref, ln_ref, sin_ref,
                      o_ref, state_ref, *, chunk, prec):
    @pl.when(pl.program_id(1) == 0)
    def _():
        state_ref[...] = sin_ref[...]

    c2 = 2 * chunk
    lane = lax.broadcasted_iota(jnp.int32, (chunk, LANES), 1)
    first = lane < HEAD_DIM
    row = lax.broadcasted_iota(jnp.int32, (c2, c2), 0)
    col = lax.broadcasted_iota(jnp.int32, (c2, c2), 1)
    eye = (row == col).astype(F32)
    hrow = lax.broadcasted_iota(jnp.int32, (LANES, LANES), 0) // HEAD_DIM
    hcol = lax.broadcasted_iota(jnp.int32, (LANES, LANES), 1) // HEAD_DIM
    head_mean = jnp.where(hrow == hcol, 1.0 / HEAD_DIM, 0.0).astype(F32)
    nt = (((1,), (1,)), ((), ()))
    tn = (((0,), (0,)), ((), ()))
    dot = functools.partial(jnp.dot, precision=prec, preferred_element_type=F32)
    dotg = functools.partial(lax.dot_general, precision=prec, preferred_element_type=F32)

    def stack(x):
        return jnp.concatenate([jnp.where(first, x, 0.0), jnp.where(first, 0.0, x)], axis=0)

    pairs = range(PAIRS)
    sls = [slice(hp * LANES, (hp + 1) * LANES) for hp in pairs]
    rs, ks, kds, bs, vs = ([stack(ref[0, :, sl]) for sl in sls] for ref in (rt_ref, kt_ref, kd_ref, bd_ref, v_ref))
    hts = [state_ref[0, hp] for hp in pairs]
    big = [dotg(jnp.concatenate([ks[hp], rs[hp]], axis=0), jnp.concatenate([bs[hp], kds[hp]], axis=0), nt)
           for hp in pairs]
    a_b = [jnp.where(row > col, big[hp][0:c2, 0:c2], 0.0) for hp in pairs]
    a_k = [jnp.where(row > col, big[hp][0:c2, c2:], 0.0) for hp in pairs]
    a_rb = [jnp.where(row >= col, big[hp][c2:, 0:c2], 0.0) for hp in pairs]
    a_rk = [jnp.where(row >= col, big[hp][c2:, c2:], 0.0) for hp in pairs]
    kh = [dotg(jnp.concatenate([ks[hp], rs[hp]], axis=0), hts[hp], nt) for hp in pairs]
    av = [dot(jnp.concatenate([a_k[hp], a_rk[hp]], axis=0), vs[hp]) for hp in pairs]
    vk = [dotg(vs[hp], kds[hp], tn) for hp in pairs]
    inv = [eye - a_b[hp] for hp in pairs]
    pw = [dot(a_b[hp], a_b[hp]) for hp in pairs]
    n_sq = int(math.log2(chunk)) - 1
    for lvl in range(n_sq):
        if lvl + 1 < n_sq:
            both = [dot(jnp.concatenate([inv[hp], pw[hp]], axis=0), pw[hp]) for hp in pairs]
            inv = [inv[hp] + both[hp][0:c2] for hp in pairs]
            pw = [both[hp][c2:] for hp in pairs]
        else:
            inv = [inv[hp] + dot(inv[hp], pw[hp]) for hp in pairs]
    us = [dot(inv[hp], kh[hp][0:c2] + av[hp][0:c2]) for hp in pairs]
    ub = [dotg(us[hp], bs[hp], tn) for hp in pairs]
    au = [dot(a_rb[hp], us[hp]) for hp in pairs]
    for hp in pairs:
        sl = sls[hp]
        pend = pend_ref[0, 0, 0:1, sl]
        state_ref[0, hp] = (hts[hp] + vk[hp] - ub[hp]) * pend
        os_ = kh[hp][c2:] + av[hp][c2:] - au[hp]
        o = os_[0:chunk] + os_[chunk:]
        mu = jnp.dot(o, head_mean, precision=HI, preferred_element_type=F32)
        d = o - mu
        var = jnp.dot(d * d, head_mean, precision=HI, preferred_element_type=F32)
        on = d * lax.rsqrt(var + GN_EPS) * ln_ref[0:1, sl] + ln_ref[1:2, sl]
        o_ref[0, :, sl] = (on + bonus_ref[0, :, sl]) * g_ref[0, :, sl]


def rwkv_scan(rt, kt, kd, bd, v, g, bonus, pend, lnx_g, lnx_b, *, state=None, c0=0, nc=None, prec=None):
    bsz, seq, _ = rt.shape
    chunk = RWKV_CHUNK
    n_chunks = seq // chunk
    nc = n_chunks - c0 if nc is None else nc
    ln = jnp.stack([lnx_g, lnx_b] + [jnp.zeros_like(lnx_g)] * 6).astype(F32)
    pend4 = pend.reshape(bsz, n_chunks, 1, WIDTH)
    if state is None:
        state = jnp.zeros((bsz, PAIRS, LANES, LANES), F32)
    spec = pl.BlockSpec((1, chunk, WIDTH), lambda b, c: (b, c0 + c, 0))
    sspec = pl.BlockSpec((1, PAIRS, LANES, LANES), lambda b, c: (b, 0, 0, 0))
    return pl.pallas_call(
        functools.partial(_rwkv_scan_kernel, chunk=chunk, prec=prec),
        grid=(bsz, nc),
        in_specs=[spec] * 7 + [
            pl.BlockSpec((1, 1, 1, WIDTH), lambda b, c: (b, c0 + c, 0, 0)),
            pl.BlockSpec((8, WIDTH), lambda b, c: (0, 0)),
            sspec,
        ],
        out_specs=[pl.BlockSpec((1, chunk, WIDTH), lambda b, c: (b, c, 0)), sspec],
        out_shape=[jax.ShapeDtypeStruct((bsz, nc * chunk, WIDTH), F32),
                   jax.ShapeDtypeStruct((bsz, PAIRS, LANES, LANES), F32)],
        compiler_params=_cparams(("parallel", "arbitrary")),
        name="rwkv_scan",
    )(rt, kt, kd, bd, v, g, bonus, pend4, ln, state)


def _merge_kernel(x_ref, oa_ref, ob_ref, ga_ref, gb_ref, wa_ref, wb_ref, wo_ref, g2_ref,
                  h_ref, xn_ref, acc_ref):
    j = pl.program_id(1)

    @pl.when(j == 0)
    def _():
        acc_ref[...] = x_ref[...]

    ya = jnp.dot(oa_ref[...].astype(BF16), wa_ref[...], preferred_element_type=F32)
    yb = jnp.dot(ob_ref[...].astype(BF16), wb_ref[...], preferred_element_type=F32)
    y = jax.nn.sigmoid(ga_ref[...]) * ya + jax.nn.sigmoid(gb_ref[...]) * yb
    acc_ref[...] += jnp.dot(y.astype(BF16), wo_ref[...], preferred_element_type=F32)

    @pl.when(j == pl.num_programs(1) - 1)
    def _():
        h = acc_ref[...]
        h_ref[...] = h
        ms = jnp.mean(h * h, axis=-1, keepdims=True)
        xn_ref[...] = _pack_halves(h * lax.rsqrt(ms + RMS_EPS) * g2_ref[...])


def _pack_halves(x):
    half = x.shape[1] // 2
    lo = lax.bitcast_convert_type(x[:, :half].astype(BF16).astype(F32), jnp.int32)
    hi = lax.bitcast_convert_type(x[:, half:].astype(BF16).astype(F32), jnp.int32)
    return lax.bitwise_or(lax.shift_right_logical(lo, jnp.int32(16)), hi)


def _unpack_halves(words):
    lo, hi = _unpack_words(words)
    return jnp.concatenate([lo, hi], axis=1)


def merge_out(x2d, oa, ob, p2d, w_proj_a, w_proj_b, w_out, norm2_g, *, row0=0, prow0=0, tm=512):
    t, d = oa.shape[0], x2d.shape[1]
    r0 = row0 // tm
    p0 = prow0 // tm
    tn = WIDTH
    nj = d // tn
    g0 = COL_G_OFF // tn
    return pl.pallas_call(
        _merge_kernel,
        grid=(t // tm, nj),
        in_specs=[
            pl.BlockSpec((tm, d), lambda i, j: (r0 + i, 0)),
            pl.BlockSpec((tm, WIDTH), lambda i, j: (i, 0)),
            pl.BlockSpec((tm, WIDTH), lambda i, j: (i, 0)),
            pl.BlockSpec((tm, tn), lambda i, j: (p0 + i, g0 + j)),
            pl.BlockSpec((tm, tn), lambda i, j: (p0 + i, g0 + nj + j)),
            pl.BlockSpec((WIDTH, tn), lambda i, j: (0, j)),
            pl.BlockSpec((WIDTH, tn), lambda i, j: (0, j)),
            pl.BlockSpec((tn, d), lambda i, j: (j, 0)),
            pl.BlockSpec((1, d), lambda i, j: (0, 0)),
        ],
        out_specs=[pl.BlockSpec((tm, d), lambda i, j: (i, 0)), pl.BlockSpec((tm, d // 2), lambda i, j: (i, 0))],
        out_shape=[jax.ShapeDtypeStruct((t, d), F32), jax.ShapeDtypeStruct((t, d // 2), jnp.int32)],
        scratch_shapes=[pltpu.VMEM((tm, d), F32)],
        compiler_params=_cparams(("parallel", "arbitrary")),
        name="merge_out",
    )(x2d, oa, ob, p2d, p2d, w_proj_a.astype(BF16), w_proj_b.astype(BF16), w_out.astype(BF16),
      norm2_g.reshape(1, d))


PEER_HEADS = 8
PEER_NKEYS = 128
PEER_TOPK = 16
PEER_HALF = 128


def _topk_rows(s, k):
    n = s.shape[0]
    rows = lax.broadcasted_iota(jnp.int32, s.shape, 0).astype(F32)
    vals, ids = [], []
    for _ in range(k):
        m = jnp.max(s, axis=0, keepdims=True)
        first = jnp.min(jnp.where(s == m, rows, float(n)), axis=0, keepdims=True)
        vals.append(m)
        ids.append(first)
        s = jnp.where(rows == first, -jnp.inf, s)
    return jnp.concatenate(vals, axis=0), jnp.concatenate(ids, axis=0)


def _take_rows(table, ids):
    rows = lax.broadcasted_iota(jnp.int32, table.shape, 0).astype(F32)
    return jnp.sum(jnp.where(rows == ids, table, 0.0), axis=0, keepdims=True)


def _peer_route_kernel(xn_ref, wq_ref, sk_ref, idx_ref, gate_ref, *, prec):
    tt = xn_ref.shape[0]
    k = PEER_TOPK
    xn = _unpack_halves(xn_ref[...]) if xn_ref.dtype == jnp.int32 else xn_ref[...]
    q = jnp.dot(xn.astype(wq_ref.dtype), wq_ref[...], precision=prec, preferred_element_type=F32)
    nt = (((1,), (1,)), ((), ()))
    idx_rows, gate_rows = [], []
    half = k // 2
    for h in range(PEER_HEADS):
        tops = []
        for p in range(2):
            c0 = (h * 2 + p) * PEER_HALF
            s = lax.dot_general(sk_ref[h, p].astype(wq_ref.dtype), q[:, c0:c0 + PEER_HALF].astype(wq_ref.dtype),
                                nt, precision=prec, preferred_element_type=F32)
            tops.append(_topk_rows(s, k))
        (s0, i0), (s1, i1) = tops
        cs = [s0[0:1] + s1] + [s0[i:i + 1] + s1[0:half] for i in range(1, half)] + [s0[half:] + s1[0:1]]
        best_s, pos = _topk_rows(jnp.concatenate(cs, axis=0), k)
        mid = jnp.floor((pos - k) * (1.0 / half))
        end_mid = float(k + (half - 1) * half)
        i_rank = jnp.where(pos < k, 0.0, jnp.where(pos < end_mid, 1.0 + mid, pos - (end_mid - half)))
        j_rank = jnp.where(pos < k, pos, jnp.where(pos < end_mid, (pos - k) - half * mid, 0.0))
        ids = [_take_rows(i0, i_rank[n:n + 1]) * PEER_NKEYS + _take_rows(i1, j_rank[n:n + 1]) for n in range(k)]
        e = jnp.exp(best_s - best_s[0:1])
        gate_rows.append(e / jnp.sum(e, axis=0, keepdims=True))
        idx_rows.append(jnp.concatenate(ids, axis=0).astype(jnp.int32))
    idx_ref[...] = jnp.concatenate(idx_rows, axis=0).T
    gate_ref[...] = jnp.concatenate(gate_rows, axis=0).T


def peer_route(xn2d, peer_wq, peer_subkeys, *, tt=256, prec=None, wdtype=BF16):
    t, dx = xn2d.shape
    d, nq = peer_wq.shape
    n_sel = PEER_HEADS * PEER_TOPK
    return pl.pallas_call(
        functools.partial(_peer_route_kernel, prec=prec),
        grid=(t // tt,),
        in_specs=[
            pl.BlockSpec((tt, dx), lambda i: (i, 0)),
            pl.BlockSpec((d, nq), lambda i: (0, 0)),
            pl.BlockSpec((PEER_HEADS, 2, PEER_NKEYS, PEER_HALF), lambda i: (0, 0, 0, 0)),
        ],
        out_specs=[pl.BlockSpec((tt, n_sel), lambda i: (i, 0))] * 2,
        out_shape=[jax.ShapeDtypeStruct((t, n_sel), jnp.int32), jax.ShapeDtypeStruct((t, n_sel), F32)],
        compiler_params=_cparams(("parallel",)),
        name="peer_route",
    )(xn2d, peer_wq.astype(wdtype), peer_subkeys)


def _final_kernel(h_ref, y_ref, g_ref, *rest):
    o_ref = rest[-1]
    h = h_ref[...] + y_ref[...]
    ms = jnp.mean(h * h, axis=-1, keepdims=True)
    o_ref[...] = h * lax.rsqrt(ms + RMS_EPS) * g_ref[...]


def final_norm(h2d, y2d, g, *, out=None, row0=0, total_rows=None, tm=1024):
    t, d = h2d.shape
    total = t if total_rows is None else total_rows
    r0 = row0 // tm
    spec = pl.BlockSpec((tm, d), lambda i: (i, 0))
    in_specs = [spec, spec, pl.BlockSpec((1, d), lambda i: (0, 0))]
    args = [h2d, y2d, g.reshape(1, d)]
    aliases = {}
    if out is not None:
        in_specs.append(pl.BlockSpec(memory_space=pl.ANY))
        args.append(out)
        aliases = {3: 0}
    return pl.pallas_call(
        _final_kernel,
        grid=(t // tm,),
        in_specs=in_specs,
        out_specs=pl.BlockSpec((tm, d), lambda i: (r0 + i, 0)),
        out_shape=jax.ShapeDtypeStruct((total, d), F32),
        input_output_aliases=aliases,
        compiler_params=_cparams(("parallel",)),
        name="final_norm",
    )(*args)


SC_CORES = 2
SC_SUBCORES = 16
SC_LANES = 16
SC_WORKERS = SC_CORES * SC_SUBCORES
PEER_SEL = PEER_HEADS * PEER_TOPK
PEER_ROWS = 32
PEER_PARTS = PEER_SEL // PEER_ROWS
PEER_NBUF = 4
PEER_GROUP = 32
PEER_BF16_RUN = 4


def _pack_rows(w):
    half = w.shape[1] // 2
    bits = lax.bitcast_convert_type(w.astype(BF16), jnp.uint16).astype(jnp.uint32)
    return lax.bitcast_convert_type(bits[:, :half] | (bits[:, half:] << 16), jnp.int32)


def _unpack_words(w):
    lo = lax.bitcast_convert_type(lax.shift_left(w, jnp.int32(16)), F32)
    hi = lax.bitcast_convert_type(lax.bitwise_and(w, jnp.int32(-65536)), F32)
    return lo, hi


def _packed_dot(a_words, b_words):
    from jax.experimental.pallas import tpu_sc as plsc
    prods = [plsc.bitcast(a, BF16) * plsc.bitcast(b, BF16) for a, b in zip(a_words, b_words)]
    while len(prods) > 1:
        prods = [prods[k] + prods[k + 1] for k in range(0, len(prods), 2)]
    return _unpack_words(plsc.bitcast(prods[0], jnp.int32))


def _sc_mesh():
    from jax.experimental.pallas import tpu_sc as plsc
    return plsc.VectorSubcoreMesh(core_axis_name="c", subcore_axis_name="s",
                                  num_cores=SC_CORES, num_subcores=SC_SUBCORES)


def _sc_loop(n, body, carry):
    from jax.experimental.pallas import tpu_sc as plsc
    return plsc.parallel_loop(0, n, carry=carry)(body)


def _worker_base(tokens_per_worker):
    return (lax.axis_index("s") * SC_CORES + lax.axis_index("c")) * tokens_per_worker


def _gather_compute_loop(table_hbm, idx_v, rows_v, sem, stage_v, out_row, osem, grp, compute):
    n_gathers = PEER_PARTS * grp
    ahead = PEER_NBUF - 1

    def gather(j, b):
        i = j // PEER_PARTS if isinstance(j, int) else lax.shift_right_logical(j, PEER_PARTS.bit_length() - 1)
        h = j % PEER_PARTS if isinstance(j, int) else lax.bitwise_and(j, PEER_PARTS - 1)
        ids = idx_v.at[i, pl.ds(pl.multiple_of(h * PEER_ROWS, PEER_ROWS), PEER_ROWS)]
        return pltpu.make_async_copy(table_hbm.at[ids], rows_v.at[b], sem.at[b])

    def put(i, slot):
        return pltpu.make_async_copy(stage_v.at[slot], out_row(i), osem.at[slot])

    for j in range(ahead):
        gather(j, j).start()

    @pl.loop(0, n_gathers)
    def _(j):
        b = lax.bitwise_and(j, PEER_NBUF - 1)
        h = lax.bitwise_and(j, PEER_PARTS - 1)
        i = lax.shift_right_logical(j, PEER_PARTS.bit_length() - 1)
        slot = lax.bitwise_and(i, 1)

        @pl.when((h == 0) & (i >= 2))
        def _():
            put(i - 2, slot).wait()

        @pl.when(j + ahead < n_gathers)
        def _():
            gather(j + ahead, lax.bitwise_and(j + ahead, PEER_NBUF - 1)).start()

        gather(j, b).wait()
        compute(i, h, b, slot)

        @pl.when(h == PEER_PARTS - 1)
        def _():
            put(i, slot).start()

    put(grp - 2, 0).wait()
    put(grp - 1, 1).wait()


def peer_expert_dots(x_packed, idx, u_packed):
    t, half = x_packed.shape
    n_chunks = half // SC_LANES
    tpw = t // SC_WORKERS
    grp = min(PEER_GROUP, tpw)
    rows_tog = 4

    from jax.experimental.pallas import tpu_sc as plsc

    def body(x_hbm, idx_hbm, u_hbm, out_hbm, idx_v, x_v, rows_v, ps_v, part_v, sem, osem):
        base = _worker_base(tpw)

        def compute(i, h, b, slot):
            @pl.loop(0, PEER_ROWS // rows_tog)
            def _(rg):
                r0 = rg * rows_tog
                accs = [[None, None] for _ in range(rows_tog)]
                for c0 in range(0, n_chunks, PEER_BF16_RUN):
                    ats = [pl.ds((c0 + k) * SC_LANES, SC_LANES) for k in range(PEER_BF16_RUN)]
                    xw = [x_v[i, at] for at in ats]
                    for r in range(rows_tog):
                        terms = _packed_dot([rows_v[b, r0 + r, at] for at in ats], xw)
                        for k, term in enumerate(terms):
                            accs[r][k] = term if accs[r][k] is None else accs[r][k] + term
                for r in range(rows_tog):
                    part_v[r0 + r, :] = accs[r][0] + accs[r][1]

            lanes = lax.broadcasted_iota(jnp.int32, (SC_LANES,), 0)
            for r16 in range(0, PEER_ROWS, SC_LANES):
                tot = None
                for k in range(SC_LANES):
                    col = plsc.load_gather(part_v, [lanes + r16, jnp.full((SC_LANES,), k, jnp.int32)])
                    tot = col if tot is None else tot + col
                ps_v[slot, pl.ds(pl.multiple_of(h * PEER_ROWS + r16, SC_LANES), SC_LANES)] = tot

        @pl.loop(0, tpw // grp)
        def _(g):
            t0 = base + g * grp
            pltpu.sync_copy(idx_hbm.at[pl.ds(t0, grp)], idx_v)
            pltpu.sync_copy(x_hbm.at[pl.ds(t0, grp)], x_v)
            _gather_compute_loop(u_hbm, idx_v, rows_v, sem, ps_v, lambda i: out_hbm.at[t0 + i], osem, grp, compute)

    return pl.kernel(
        body,
        out_type=jax.ShapeDtypeStruct((t, PEER_SEL), F32),
        mesh=_sc_mesh(),
        scratch_types=[
            pltpu.VMEM((grp, PEER_SEL), jnp.int32),
            pltpu.VMEM((grp, half), jnp.int32),
            pltpu.VMEM((PEER_NBUF, PEER_ROWS, half), jnp.int32),
            pltpu.VMEM((2, PEER_SEL), F32),
            pltpu.VMEM((PEER_ROWS, SC_LANES), F32),
            pltpu.SemaphoreType.DMA((PEER_NBUF,)),
            pltpu.SemaphoreType.DMA((2,)),
        ],
        compiler_params=pltpu.CompilerParams(needs_layout_passes=False),
        name="peer_expert_dots",
    )(x_packed, idx, u_packed)


def peer_expert_mix(hgw, idx, v_packed):
    t = hgw.shape[0]
    half = v_packed.shape[1]
    d = 2 * half
    tpw = t // SC_WORKERS
    grp = min(PEER_GROUP, tpw)
    n_parts = 2
    cpp = half // SC_LANES // n_parts
    from jax.experimental.pallas import tpu_sc as plsc

    def body(hg_hbm, idx_hbm, v_hbm, out_hbm, idx_v, hg_v, rows_v, o_v2, sem, osem):
        base = _worker_base(tpw)

        def compute(i, h, b, slot):
            token = jnp.full((SC_LANES,), i, jnp.int32)
            for part in range(n_parts):
                def rbody(rq, accs):
                    r0 = rq * PEER_BF16_RUN
                    s = [plsc.load_gather(hg_v, [token, jnp.full((SC_LANES,), h * PEER_ROWS + r0 + k, jnp.int32)])
                         for k in range(PEER_BF16_RUN)]
                    new = []
                    for c in range(cpp):
                        at = pl.ds((part * cpp + c) * SC_LANES, SC_LANES)
                        lo, hi = _packed_dot([rows_v[b, r0 + k, at] for k in range(PEER_BF16_RUN)], s)
                        new.append(accs[2 * c] + lo)
                        new.append(accs[2 * c + 1] + hi)
                    return tuple(new)

                accs = _sc_loop(PEER_ROWS // PEER_BF16_RUN, rbody,
                                tuple(jnp.zeros((SC_LANES,), F32) for _ in range(2 * cpp)))
                def store(overwrite):
                    for c in range(cpp):
                        lo_at = pl.ds((part * cpp + c) * SC_LANES, SC_LANES)
                        hi_at = pl.ds(half + (part * cpp + c) * SC_LANES, SC_LANES)
                        if overwrite:
                            o_v2[slot, lo_at] = accs[2 * c]
                            o_v2[slot, hi_at] = accs[2 * c + 1]
                        else:
                            o_v2[slot, lo_at] = o_v2[slot, lo_at] + accs[2 * c]
                            o_v2[slot, hi_at] = o_v2[slot, hi_at] + accs[2 * c + 1]

                pl.when(h == 0)(functools.partial(store, True))
                pl.when(h != 0)(functools.partial(store, False))

        @pl.loop(0, tpw // grp)
        def _(g):
            t0 = base + g * grp
            pltpu.sync_copy(idx_hbm.at[pl.ds(t0, grp)], idx_v)
            pltpu.sync_copy(hg_hbm.at[pl.ds(t0, grp)], hg_v)
            _gather_compute_loop(v_hbm, idx_v, rows_v, sem, o_v2, lambda i: out_hbm.at[t0 + i], osem, grp, compute)

    return pl.kernel(
        body,
        out_type=jax.ShapeDtypeStruct((t, d), F32),
        mesh=_sc_mesh(),
        scratch_types=[
            pltpu.VMEM((grp, PEER_SEL), jnp.int32),
            pltpu.VMEM((grp, PEER_SEL), jnp.int32),
            pltpu.VMEM((PEER_NBUF, PEER_ROWS, half), jnp.int32),
            pltpu.VMEM((2, d), F32),
            pltpu.SemaphoreType.DMA((PEER_NBUF,)),
            pltpu.SemaphoreType.DMA((2,)),
        ],
        compiler_params=pltpu.CompilerParams(needs_layout_passes=False),
        name="peer_expert_mix",
    )(hgw, idx, v_packed)


def _peer_act_kernel(pre_ref, gate_ref, o_ref):
    pre = pre_ref[...]
    hg = 0.5 * pre * (1.0 + lax.erf(pre * (1.0 / math.sqrt(2.0)))) * gate_ref[...]
    bits = lax.bitcast_convert_type(hg.astype(BF16).astype(F32), jnp.int32)
    o_ref[...] = lax.bitwise_or(bits, lax.shift_right_logical(bits, jnp.int32(16)))


def peer_act(pre, gates, *, tm=1024):
    t, n = pre.shape
    spec = pl.BlockSpec((tm, n), lambda i: (i, 0))
    return pl.pallas_call(
        _peer_act_kernel,
        grid=(t // tm,),
        in_specs=[spec, spec],
        out_specs=spec,
        out_shape=jax.ShapeDtypeStruct((t, n), jnp.int32),
        compiler_params=_cparams(("parallel",)),
        name="peer_act",
    )(pre, gates)


BATCH_GROUPS = 8


def kernel(x, norm1_g, w_in, rwkv_mu, w0, w_lora_up, a0, a_lora_up, g_lora_up, k_k, k_a, r_k, lnx_g, lnx_b,
           w_proj_a, w_proj_b, w_out, norm2_g, peer_wq, peer_subkeys, peer_u, peer_v, rel_bias, normf_g):
    bsz, seq, d = x.shape
    depth = norm1_g.shape[0]
    groups = BATCH_GROUPS if bsz % BATCH_GROUPS == 0 else 1
    gb = bsz // groups
    tg = gb * seq
    t = bsz * seq
    src = x.reshape(t, d)
    for l in range(depth):
        w_pad = jnp.concatenate([
            w_in[l][:, :COL_A + COL_B_RAW],
            jnp.zeros((d, COL_B - COL_B_RAW), w_in.dtype),
            w_in[l][:, COL_A + COL_B_RAW:]], axis=1).astype(BF16)
        u_packed = _pack_rows(peer_u[l])
        tables = {}
        last = l == depth - 1

        def mix(pending, tie=None):
            row0, h2d, ps, gates, idx = pending
            hgx = peer_act(ps, gates)
            if "v" not in tables:
                v_src = peer_v[l]
                if tie is not None:
                    tie, v_src = lax.optimization_barrier((tie, v_src))
                tables["v"] = _pack_rows(v_src)
            if tie is not None:
                tie, hgx = lax.optimization_barrier((tie, hgx))
            return tie, (row0, h2d, peer_expert_mix(hgx, idx, tables["v"]))

        outs = []

        def close(mixed):
            row0, h2d, y2d = mixed
            if last:
                outs.append(final_norm(h2d, y2d, normf_g, out=outs[-1] if outs else None, row0=row0, total_rows=t))
            else:
                outs.append(h2d + y2d)

        halves = gb == 1 and seq % (2 * MOBA_BLOCK) == 0 and (seq // 2) % (SC_WORKERS * PEER_GROUP) == 0

        pending = closing = None
        for g in range(groups):
            p2d = norm_proj(src, norm1_g[l], w_pad, row0=g * tg, rows=tg)
            p3d = p2d.reshape(gb, seq, -1)
            prep = state = None
            for s0, sn in ([(0, seq // 2), (seq // 2, seq // 2)] if halves and g == 0 else [(0, seq)]):
                oa = moba_attention(p3d, rel_bias, q0=s0 // MOBA_BLOCK, nq=sn // MOBA_BLOCK)
                mixed = None
                if pending is not None:
                    oa, mixed = mix(pending, oa)
                if closing is not None:
                    oa, y2d = lax.optimization_barrier((oa, closing[2]))
                    close(closing[:2] + (y2d,))
                    closing = None
                if prep is None:
                    prep = rwkv_prep(p3d, rwkv_mu[l], w0[l], w_lora_up[l], a0[l], a_lora_up[l], g_lora_up[l],
                                     k_k[l], k_a[l], r_k[l])
                ob, state = rwkv_scan(*prep, lnx_g[l], lnx_b[l], state=state,
                                      c0=s0 // RWKV_CHUNK, nc=sn // RWKV_CHUNK)
                nt = gb * sn
                h2d, xn2 = merge_out(src, oa.reshape(nt, WIDTH), ob.reshape(nt, WIDTH), p2d, w_proj_a[l], w_proj_b[l],
                                     w_out[l], norm2_g[l], row0=g * tg + s0, prow0=s0)
                idx, gates = peer_route(xn2, peer_wq[l], peer_subkeys[l])
                if mixed is not None:
                    idx, y2d = lax.optimization_barrier((idx, mixed[2]))
                    closing = mixed[:2] + (y2d,)
                pending = (g * tg + s0, h2d, peer_expert_dots(xn2, idx, u_packed), gates, idx)
        if closing is not None:
            close(closing)
        close(mix(pending)[1])
        src = outs[-1] if last else jnp.concatenate(outs, axis=0)
    return src.reshape(bsz, seq, d)
```

```python
import functools
import math

import jax
import jax.numpy as jnp
from jax import lax
from jax.experimental import pallas as pl
from jax.experimental.pallas import tpu as pltpu

F32 = jnp.float32
BF16 = jnp.bfloat16
HI = lax.Precision.HIGHEST

LANES = 128
HEAD_DIM = 64
HEADS = 8
PAIRS = HEADS // 2
WIDTH = HEADS * HEAD_DIM
MOBA_BLOCK = 256
MOBA_TOPK = 3
MOBA_LO = 64
REL_BUCKETS = 32
REL_MAX_DIST = 128
DECAY_LORA = 64
AAA_LORA = 64
GATE_LORA = 160
GN_EPS = 64e-5
RMS_EPS = 1e-6
NEG = -1e30
RWKV_CHUNK = 64
COL_A = 3 * WIDTH
COL_B_RAW = 3 * WIDTH + DECAY_LORA + AAA_LORA + GATE_LORA
COL_B = 4 * WIDTH
COL_G_OFF = COL_A + COL_B
VMEM_LIMIT = 56 * 1024 * 1024


def _cparams(sem):
    return pltpu.CompilerParams(dimension_semantics=sem, vmem_limit_bytes=VMEM_LIMIT)


def _norm_proj_kernel(x_ref, g_ref, w_ref, o_ref, xn_ref):
    @pl.when(pl.program_id(1) == 0)
    def _():
        x = x_ref[...]
        ms = jnp.mean(x * x, axis=-1, keepdims=True)
        xn_ref[...] = (x * lax.rsqrt(ms + RMS_EPS) * g_ref[...]).astype(xn_ref.dtype)

    o_ref[...] = jnp.dot(xn_ref[...], w_ref[...], preferred_element_type=F32).astype(o_ref.dtype)


def norm_proj(x2d, g, w, *, row0=0, rows=None, tm=512, tn=512, out_dtype=F32):
    d = x2d.shape[1]
    t = x2d.shape[0] if rows is None else rows
    n = w.shape[1]
    r0 = row0 // tm
    return pl.pallas_call(
        _norm_proj_kernel,
        grid=(t // tm, n // tn),
        in_specs=[
            pl.BlockSpec((tm, d), lambda i, j: (r0 + i, 0)),
            pl.BlockSpec((1, d), lambda i, j: (0, 0)),
            pl.BlockSpec((d, tn), lambda i, j: (0, j)),
        ],
        out_specs=pl.BlockSpec((tm, tn), lambda i, j: (i, j)),
        out_shape=jax.ShapeDtypeStruct((t, n), out_dtype),
        scratch_shapes=[pltpu.VMEM((tm, d), w.dtype)],
        compiler_params=_cparams(("parallel", "arbitrary")),
        name="norm_proj",
    )(x2d, g.reshape(1, d), w)


def _rel_bucket(dist):
    n = jnp.maximum(dist, 0)
    max_exact = REL_BUCKETS // 2
    nf = jnp.maximum(n, 1).astype(F32)
    large = max_exact + (jnp.log(nf / max_exact) / math.log(REL_MAX_DIST / max_exact)
                         * (REL_BUCKETS - max_exact)).astype(jnp.int32)
    large = jnp.minimum(large, REL_BUCKETS - 1)
    return jnp.where(n < max_exact, n, large)


def _moba_kernel(q_ref, k_ref, v_ref, bown_ref, bprev_ref, bfar_ref, o_ref,
                 kb_ref, vb_ref, kbar_ref, *, n_blocks, q0):
    qb = pl.program_id(2) + q0
    blk = MOBA_BLOCK
    scale = 1.0 / math.sqrt(HEAD_DIM)

    rows2 = 2 * blk
    nt = (((1,), (1,)), ((), ()))

    @pl.when(pl.program_id(2) == 0)
    def _():
        kbar_ref[...] = jnp.zeros_like(kbar_ref)
        lane_b = lax.broadcasted_iota(jnp.int32, (blk, LANES), 1)
        for n in range(n_blocks):
            kblk = k_ref[0, n * blk:(n + 1) * blk, :]
            kbar_ref[n:n + 1, :] = jnp.mean(kblk, axis=0, keepdims=True)
            kb_ref[n * blk:(n + 1) * blk, 0:LANES] = kblk.astype(BF16)
            kb_ref[n * blk:(n + 1) * blk, LANES:] = ((lane_b == n) | (lane_b == MOBA_LO + n)).astype(BF16)
        vb_ref[...] = v_ref[0].astype(BF16)

    q2 = q_ref[0]
    first = lax.broadcasted_iota(jnp.int32, (blk, LANES), 1) < HEAD_DIM
    qh = jnp.concatenate([jnp.where(first, q2, 0.0), jnp.where(first, 0.0, q2)], axis=0)
    lane = lax.broadcasted_iota(jnp.int32, (rows2, LANES), 1)
    rowi = lax.broadcasted_iota(jnp.int32, (rows2, LANES), 0)
    gate = lax.dot_general(qh.astype(BF16), kbar_ref[...].astype(BF16), nt, preferred_element_type=F32)
    g = jnp.where(lane < qb, gate, -jnp.inf)
    chosen = lane < 0
    lane_f = lane.astype(F32)
    for _ in range(MOBA_TOPK):
        m = jnp.max(g, axis=1, keepdims=True)
        idx = jnp.min(jnp.where(g == m, lane_f, float(LANES)), axis=1, keepdims=True)
        hit = (lane_f == idx) & (m > -jnp.inf)
        chosen = chosen | hit
        g = jnp.where(hit, -jnp.inf, g)
    nfar = qb - 1
    bfar = jnp.where(rowi < blk, bfar_ref[0, 0:1, 0:1], bfar_ref[1, 0:1, 0:1])
    bhi = bfar.astype(BF16).astype(F32)
    madd = jnp.where(lane < nfar, jnp.where(chosen, bhi, NEG),
                     jnp.where(lane == nfar, jnp.where(chosen, 0.0, NEG),
                               jnp.where((lane >= MOBA_LO) & (lane - MOBA_LO < nfar), bfar - bhi, 0.0)))
    q_aug = jnp.concatenate([(qh * scale).astype(BF16), madd.astype(BF16)], axis=1)

    prev0 = pl.multiple_of(jnp.maximum(nfar, 0) * blk, blk)
    own0 = pl.multiple_of(qb * blk, blk)
    s_prev = (lax.dot_general(q_aug, kb_ref[pl.ds(prev0, blk), :], nt, preferred_element_type=F32)
              + bprev_ref[...].reshape(rows2, blk) + jnp.where(qb > 0, 0.0, NEG))
    s_own = (lax.dot_general(q_aug, kb_ref[pl.ds(own0, blk), :], nt, preferred_element_type=F32)
             + bown_ref[...].reshape(rows2, blk))
    r = lax.broadcasted_iota(jnp.int32, (rows2, blk), 0)
    c = lax.broadcasted_iota(jnp.int32, (rows2, blk), 1)
    s_own = jnp.where(lax.bitwise_and(r, blk - 1) >= c, s_own, NEG)
    s = jnp.concatenate([s_prev, s_own], axis=1)
    m_i = jnp.max(s, axis=1, keepdims=True)
    p = jnp.exp(s - m_i)
    l_i = jnp.sum(p, axis=1, keepdims=True)
    v0 = jnp.concatenate([vb_ref[pl.ds(prev0, blk), :], vb_ref[pl.ds(own0, blk), :]], axis=0)
    acc = jnp.dot(p.astype(BF16), v0, preferred_element_type=F32)

    def body(it, carry):
        m_i, l_i, acc = carry
        k0 = pl.multiple_of(it * rows2, rows2)
        s = lax.dot_general(q_aug, kb_ref[pl.ds(k0, rows2), :], nt, preferred_element_type=F32)
        tail = jnp.where(2 * it + 1 < nfar, 0.0, NEG)
        s = jnp.concatenate([s[:, :blk], s[:, blk:] + tail], axis=1)
        m_new = jnp.maximum(m_i, jnp.max(s, axis=1, keepdims=True))
        alpha = jnp.exp(m_i - m_new)
        p = jnp.exp(s - m_new)
        l_new = alpha * l_i + jnp.sum(p, axis=1, keepdims=True)
        acc_new = alpha * acc + jnp.dot(p.astype(BF16), vb_ref[pl.ds(k0, rows2), :], preferred_element_type=F32)
        return m_new, l_new, acc_new

    m_i, l_i, acc = lax.fori_loop(0, (jnp.maximum(nfar, 0) + 1) // 2, body, (m_i, l_i, acc))
    out = acc / l_i
    o_ref[0] = jnp.where(first, out[:blk], out[blk:])


def moba_attention(p3d, rel_bias, *, q0=0, nq=None):
    bsz, seq, _ = p3d.shape
    blk = MOBA_BLOCK
    n_blocks = seq // blk
    nq = n_blocks - q0 if nq is None else nq
    assert n_blocks <= MOBA_LO and seq % blk == 0
    span = 2 * blk
    by_dist = rel_bias[:, _rel_bucket(jnp.arange(span))].astype(F32)
    shift = jnp.arange(span)

    def toeplitz(c):
        k = jnp.where(shift < blk, shift, shift - span)
        s = by_dist[:, jnp.clip(c - k, 0, span - 1)]
        tiled = jnp.tile(s, (1, blk))[:, :blk * (span - 1)]
        return tiled.reshape(HEADS, blk, span - 1)[:, :, :blk]

    bias_own = toeplitz(0)
    bias_prev = toeplitz(blk)
    bias_far = jnp.broadcast_to(rel_bias[:, REL_BUCKETS - 1].astype(F32)[:, None, None], (HEADS, 8, LANES))
    kern = functools.partial(_moba_kernel, n_blocks=n_blocks, q0=q0)
    return pl.pallas_call(
        kern,
        grid=(bsz, PAIRS, nq),
        in_specs=[
            pl.BlockSpec((1, blk, LANES), lambda b, h, i: (b, q0 + i, h)),
            pl.BlockSpec((1, seq, LANES), lambda b, h, i: (b, 0, PAIRS + h)),
            pl.BlockSpec((1, seq, LANES), lambda b, h, i: (b, 0, 2 * PAIRS + h)),
            pl.BlockSpec((2, blk, blk), lambda b, h, i: (h, 0, 0)),
            pl.BlockSpec((2, blk, blk), lambda b, h, i: (h, 0, 0)),
            pl.BlockSpec((2, 8, LANES), lambda b, h, i: (h, 0, 0)),
        ],
        out_specs=pl.BlockSpec((1, blk, LANES), lambda b, h, i: (b, i, h)),
        out_shape=jax.ShapeDtypeStruct((bsz, nq * blk, WIDTH), F32),
        scratch_shapes=[
            pltpu.VMEM((seq, 2 * LANES), BF16),
            pltpu.VMEM((seq, LANES), BF16),
            pltpu.VMEM((LANES, LANES), F32),
        ],
        compiler_params=_cparams(("parallel", "parallel", "arbitrary")),
        name="moba",
    )(p3d, p3d, p3d, bias_own, bias_prev, bias_far)


def _shifted(x, carry_row):
    rows = lax.broadcasted_iota(jnp.int32, x.shape, 0)
    return jnp.where(rows == 0, carry_row, pltpu.roll(x, 1, axis=0))


def _rwkv_prep_kernel(pr_ref, pk_ref, pv_ref, pl_ref, mu_ref, vec_ref, ww_ref, wa_ref, wg_ref,
                      bd_ref, tri_ref,
                      rt_ref, kt_ref, kd_ref, bd_out_ref, v_ref, g_ref, bonus_ref, pend_ref,
                      carry_ref, *, chunk):
    @pl.when(pl.program_id(1) == 0)
    def _():
        carry_ref[...] = jnp.zeros_like(carry_ref)

    def mix(ref, j):
        x = ref[0]
        mu = mu_ref[0:1, j * WIDTH:(j + 1) * WIDTH]
        prev = _shifted(x, carry_ref[0:1, j * WIDTH:(j + 1) * WIDTH])
        carry_ref[0:1, j * WIDTH:(j + 1) * WIDTH] = x[x.shape[0] - 1:, :]
        return x + mu * (prev - x)

    r = mix(pr_ref, 0)
    k = mix(pk_ref, 1)
    v = mix(pv_ref, 2)
    lo = mix(pl_ref, 3)
    w0, a0, k_k, k_a, r_k = (vec_ref[i:i + 1, :] for i in range(5))
    xwa = lo[:, 0:LANES]
    xg = lo[:, LANES:3 * LANES]
    lw = jnp.dot(jnp.tanh(xwa), ww_ref[...], precision=HI, preferred_element_type=F32)
    la = jnp.dot(xwa, wa_ref[...], precision=HI, preferred_element_type=F32)
    g = jnp.dot(jax.nn.sigmoid(xg), wg_ref[...], precision=HI, preferred_element_type=F32)
    z = -(w0 + lw)
    softplus = jnp.maximum(z, 0.0) + jnp.log(1.0 + jnp.exp(-jnp.abs(z)))
    logw = -jnp.exp(-softplus - 0.5)
    a = jax.nn.sigmoid(a0 + la)
    kk = k * k_k
    ss = jnp.dot(kk * kk, bd_ref[...], precision=HI, preferred_element_type=F32)
    kk = kk / jnp.maximum(jnp.sqrt(ss), 1e-12)
    k2 = k * (1.0 + (a - 1.0) * k_a)
    rk = jnp.dot(r * k2 * r_k, bd_ref[...], precision=HI, preferred_element_type=F32)
    cs = jnp.dot(tri_ref[...], logw, precision=HI, preferred_element_type=F32)
    e_pos = jnp.exp(cs)
    e_neg = jnp.exp(-cs)
    rt_ref[0] = r * e_pos
    kt_ref[0] = kk * jnp.exp(cs - logw)
    kd_ref[0] = k2 * e_neg
    bd_out_ref[0] = kk * a * e_neg
    v_ref[0] = v
    g_ref[0] = g
    bonus_ref[0] = rk * v
    ts = e_pos.shape[0]
    for c in range(ts // chunk):
        pend_ref[0, c:c + 1, :] = e_pos[(c + 1) * chunk - 1:(c + 1) * chunk, :]


def rwkv_prep(p3d, rwkv_mu, w0, w_lora_up, a0, a_lora_up, g_lora_up, k_k, k_a, r_k, *, ts=512):
    bsz, seq, _ = p3d.shape
    chunk = RWKV_CHUNK
    ts = min(ts, seq)
    mu = jnp.pad(rwkv_mu, (0, COL_B - COL_B_RAW)).reshape(1, COL_B)
    vec = jnp.stack([w0, a0, k_k, k_a, r_k.reshape(-1)] + [jnp.zeros_like(w0)] * 3).astype(F32)
    ww = jnp.zeros((LANES, WIDTH), F32).at[:DECAY_LORA].set(w_lora_up)
    wa = jnp.zeros((LANES, WIDTH), F32).at[DECAY_LORA:DECAY_LORA + AAA_LORA].set(a_lora_up)
    wg = jnp.zeros((2 * LANES, WIDTH), F32).at[:GATE_LORA].set(g_lora_up)
    hid = jnp.arange(WIDTH) // HEAD_DIM
    bd = (hid[:, None] == hid[None, :]).astype(F32)
    tix = jnp.arange(ts)
    tri = ((tix[:, None] // chunk == tix[None, :] // chunk) & (tix[None, :] <= tix[:, None])).astype(F32)
    c0 = COL_A // WIDTH
    big = jax.ShapeDtypeStruct((bsz, seq, WIDTH), F32)
    wspec = lambda shape: pl.BlockSpec(shape, lambda b, i: (0, 0))
    ospec = pl.BlockSpec((1, ts, WIDTH), lambda b, i: (b, i, 0))
    return pl.pallas_call(
        functools.partial(_rwkv_prep_kernel, chunk=chunk),
        grid=(bsz, seq // ts),
        in_specs=[
            pl.BlockSpec((1, ts, WIDTH), lambda b, i: (b, i, c0)),
            pl.BlockSpec((1, ts, WIDTH), lambda b, i: (b, i, c0 + 1)),
            pl.BlockSpec((1, ts, WIDTH), lambda b, i: (b, i, c0 + 2)),
            pl.BlockSpec((1, ts, WIDTH), lambda b, i: (b, i, c0 + 3)),
            wspec((1, COL_B)), wspec((8, WIDTH)), wspec((LANES, WIDTH)), wspec((LANES, WIDTH)),
            wspec((2 * LANES, WIDTH)), wspec((WIDTH, WIDTH)), wspec((ts, ts)),
        ],
        out_specs=[ospec] * 7 + [pl.BlockSpec((1, ts // chunk, WIDTH), lambda b, i: (b, i, 0))],
        out_shape=[big] * 7 + [jax.ShapeDtypeStruct((bsz, seq // chunk, WIDTH), F32)],
        scratch_shapes=[pltpu.VMEM((8, COL_B), F32)],
        compiler_params=_cparams(("parallel", "arbitrary")),
        name="rwkv_prep",
    )(p3d, p3d, p3d, p3d, mu, vec, ww, wa, wg, bd, tri)


def _rwkv_scan_kernel(rt_ref, kt_ref, kd_ref, bd_ref, v_ref, g_ref, bonus_ref, pend_ref, ln_ref, sin_ref,
                      o_ref, state_ref, *, chunk, prec):
    @pl.when(pl.program_id(1) == 0)
    def _():
        state_ref[...] = sin_ref[...]

    c2 = 2 * chunk
    lane = lax.broadcasted_iota(jnp.int32, (chunk, LANES), 1)
    first = lane < HEAD_DIM
    row = lax.broadcasted_iota(jnp.int32, (c2, c2), 0)
    col = lax.broadcasted_iota(jnp.int32, (c2, c2), 1)
    eye = (row == col).astype(F32)
    hrow = lax.broadcasted_iota(jnp.int32, (LANES, LANES), 0) // HEAD_DIM
    hcol = lax.broadcasted_iota(jnp.int32, (LANES, LANES), 1) // HEAD_DIM
    head_mean = jnp.where(hrow == hcol, 1.0 / HEAD_DIM, 0.0).astype(F32)
    nt = (((1,), (1,)), ((), ()))
    tn = (((0,), (0,)), ((), ()))
    dot = functools.partial(jnp.dot, precision=prec, preferred_element_type=F32)
    dotg = functools.partial(lax.dot_general, precision=prec, preferred_element_type=F32)

    def stack(x):
        return jnp.concatenate([jnp.where(first, x, 0.0), jnp.where(first, 0.0, x)], axis=0)

    pairs = range(PAIRS)
    sls = [slice(hp * LANES, (hp + 1) * LANES) for hp in pairs]
    rs, ks, kds, bs, vs = ([stack(ref[0, :, sl]) for sl in sls] for ref in (rt_ref, kt_ref, kd_ref, bd_ref, v_ref))
    hts = [state_ref[0, hp] for hp in pairs]
    big = [dotg(jnp.concatenate([ks[hp], rs[hp]], axis=0), jnp.concatenate([bs[hp], kds[hp]], axis=0), nt)
           for hp in pairs]
    a_b = [jnp.where(row > col, big[hp][0:c2, 0:c2], 0.0) for hp in pairs]
    a_k = [jnp.where(row > col, big[hp][0:c2, c2:], 0.0) for hp in pairs]
    a_rb = [jnp.where(row >= col, big[hp][c2:, 0:c2], 0.0) for hp in pairs]
    a_rk = [jnp.where(row >= col, big[hp][c2:, c2:], 0.0) for hp in pairs]
    kh = [dotg(jnp.concatenate([ks[hp], rs[hp]], axis=0), hts[hp], nt) for hp in pairs]
    av = [dot(jnp.concatenate([a_k[hp], a_rk[hp]], axis=0), vs[hp]) for hp in pairs]
    vk = [dotg(vs[hp], kds[hp], tn) for hp in pairs]
    inv = [eye - a_b[hp] for hp in pairs]
    pw = [dot(a_b[hp], a_b[hp]) for hp in pairs]
    n_sq = int(math.log2(chunk)) - 1
    for lvl in range(n_sq):
        if lvl + 1 < n_sq:
            both = [dot(jnp.concatenate([inv[hp], pw[hp]], axis=0), pw[hp]) for hp in pairs]
            inv = [inv[hp] + both[hp][0:c2] for hp in pairs]
            pw = [both[hp][c2:] for hp in pairs]
        else:
            inv = [inv[hp] + dot(inv[hp], pw[hp]) for hp in pairs]
    us = [dot(inv[hp], kh[hp][0:c2] + av[hp][0:c2]) for hp in pairs]
    ub = [dotg(us[hp], bs[hp], tn) for hp in pairs]
    au = [dot(a_rb[hp], us[hp]) for hp in pairs]
    for hp in pairs:
        sl = sls[hp]
        pend = pend_ref[0, 0, 0:1, sl]
        state_ref[0, hp] = (hts[hp] + vk[hp] - ub[hp]) * pend
        os_ = kh[hp][c2:] + av[hp][c2:] - au[hp]
        o = os_[0:chunk] + os_[chunk:]
        mu = jnp.dot(o, head_mean, precision=HI, preferred_element_type=F32)
        d = o - mu
        var = jnp.dot(d * d, head_mean, precision=HI, preferred_element_type=F32)
        on = d * lax.rsqrt(var + GN_EPS) * ln_ref[0:1, sl] + ln_ref[1:2, sl]
        o_ref[0, :, sl] = (on + bonus_ref[0, :, sl]) * g_ref[0, :, sl]


def rwkv_scan(rt, kt, kd, bd, v, g, bonus, pend, lnx_g, lnx_b, *, state=None, c0=0, nc=None, prec=None):
    bsz, seq, _ = rt.shape
    chunk = RWKV_CHUNK
    n_chunks = seq // chunk
    nc = n_chunks - c0 if nc is None else nc
    ln = jnp.stack([lnx_g, lnx_b] + [jnp.zeros_like(lnx_g)] * 6).astype(F32)
    pend4 = pend.reshape(bsz, n_chunks, 1, WIDTH)
    if state is None:
        state = jnp.zeros((bsz, PAIRS, LANES, LANES), F32)
    spec = pl.BlockSpec((1, chunk, WIDTH), lambda b, c: (b, c0 + c, 0))
    sspec = pl.BlockSpec((1, PAIRS, LANES, LANES), lambda b, c: (b, 0, 0, 0))
    return pl.pallas_call(
        functools.partial(_rwkv_scan_kernel, chunk=chunk, prec=prec),
        grid=(bsz, nc),
        in_specs=[spec] * 7 + [
            pl.BlockSpec((1, 1, 1, WIDTH), lambda b, c: (b, c0 + c, 0, 0)),
            pl.BlockSpec((8, WIDTH), lambda b, c: (0, 0)),
            sspec,
        ],
        out_specs=[pl.BlockSpec((1, chunk, WIDTH), lambda b, c: (b, c, 0)), sspec],
        out_shape=[jax.ShapeDtypeStruct((bsz, nc * chunk, WIDTH), F32),
                   jax.ShapeDtypeStruct((bsz, PAIRS, LANES, LANES), F32)],
        compiler_params=_cparams(("parallel", "arbitrary")),
        name="rwkv_scan",
    )(rt, kt, kd, bd, v, g, bonus, pend4, ln, state)


def _merge_kernel(x_ref, oa_ref, ob_ref, ga_ref, gb_ref, wa_ref, wb_ref, wo_ref, g2_ref,
                  h_ref, xn_ref, acc_ref):
    j = pl.program_id(1)

    @pl.when(j == 0)
    def _():
        acc_ref[...] = x_ref[...]

    ya = jnp.dot(oa_ref[...].astype(BF16), wa_ref[...], preferred_element_type=F32)
    yb = jnp.dot(ob_ref[...].astype(BF16), wb_ref[...], preferred_element_type=F32)
    y = jax.nn.sigmoid(ga_ref[...]) * ya + jax.nn.sigmoid(gb_ref[...]) * yb
    acc_ref[...] += jnp.dot(y.astype(BF16), wo_ref[...], preferred_element_type=F32)

    @pl.when(j == pl.num_programs(1) - 1)
    def _():
        h = acc_ref[...]
        h_ref[...] = h
        ms = jnp.mean(h * h, axis=-1, keepdims=True)
        xn_ref[...] = _pack_halves(h * lax.rsqrt(ms + RMS_EPS) * g2_ref[...])


def _pack_halves(x):
    half = x.shape[1] // 2
    lo = lax.bitcast_convert_type(x[:, :half].astype(BF16).astype(F32), jnp.int32)
    hi = lax.bitcast_convert_type(x[:, half:].astype(BF16).astype(F32), jnp.int32)
    return lax.bitwise_or(lax.shift_right_logical(lo, jnp.int32(16)), hi)


def _unpack_halves(words):
    lo, hi = _unpack_words(words)
    return jnp.concatenate([lo, hi], axis=1)


def merge_out(x2d, oa, ob, p2d, w_proj_a, w_proj_b, w_out, norm2_g, *, row0=0, prow0=0, tm=512):
    t, d = oa.shape[0], x2d.shape[1]
    r0 = row0 // tm
    p0 = prow0 // tm
    tn = WIDTH
    nj = d // tn
    g0 = COL_G_OFF // tn
    return pl.pallas_call(
        _merge_kernel,
        grid=(t // tm, nj),
        in_specs=[
            pl.BlockSpec((tm, d), lambda i, j: (r0 + i, 0)),
            pl.BlockSpec((tm, WIDTH), lambda i, j: (i, 0)),
            pl.BlockSpec((tm, WIDTH), lambda i, j: (i, 0)),
            pl.BlockSpec((tm, tn), lambda i, j: (p0 + i, g0 + j)),
            pl.BlockSpec((tm, tn), lambda i, j: (p0 + i, g0 + nj + j)),
            pl.BlockSpec((WIDTH, tn), lambda i, j: (0, j)),
            pl.BlockSpec((WIDTH, tn), lambda i, j: (0, j)),
            pl.BlockSpec((tn, d), lambda i, j: (j, 0)),
            pl.BlockSpec((1, d), lambda i, j: (0, 0)),
        ],
        out_specs=[pl.BlockSpec((tm, d), lambda i, j: (i, 0)), pl.BlockSpec((tm, d // 2), lambda i, j: (i, 0))],
        out_shape=[jax.ShapeDtypeStruct((t, d), F32), jax.ShapeDtypeStruct((t, d // 2), jnp.int32)],
        scratch_shapes=[pltpu.VMEM((tm, d), F32)],
        compiler_params=_cparams(("parallel", "arbitrary")),
        name="merge_out",
    )(x2d, oa, ob, p2d, p2d, w_proj_a.astype(BF16), w_proj_b.astype(BF16), w_out.astype(BF16),
      norm2_g.reshape(1, d))


PEER_HEADS = 8
PEER_NKEYS = 128
PEER_TOPK = 16
PEER_HALF = 128


def _topk_rows(s, k):
    n = s.shape[0]
    rows = lax.broadcasted_iota(jnp.int32, s.shape, 0).astype(F32)
    vals, ids = [], []
    for _ in range(k):
        m = jnp.max(s, axis=0, keepdims=True)
        first = jnp.min(jnp.where(s == m, rows, float(n)), axis=0, keepdims=True)
        vals.append(m)
        ids.append(first)
        s = jnp.where(rows == first, -jnp.inf, s)
    return jnp.concatenate(vals, axis=0), jnp.concatenate(ids, axis=0)


def _take_rows(table, ids):
    rows = lax.broadcasted_iota(jnp.int32, table.shape, 0).astype(F32)
    return jnp.sum(jnp.where(rows == ids, table, 0.0), axis=0, keepdims=True)


def _peer_route_kernel(xn_ref, wq_ref, sk_ref, idx_ref, gate_ref, *, prec):
    tt = xn_ref.shape[0]
    k = PEER_TOPK
    xn = _unpack_halves(xn_ref[...]) if xn_ref.dtype == jnp.int32 else xn_ref[...]
    q = jnp.dot(xn.astype(wq_ref.dtype), wq_ref[...], precision=prec, preferred_element_type=F32)
    nt = (((1,), (1,)), ((), ()))
    idx_rows, gate_rows = [], []
    half = k // 2
    for h in range(PEER_HEADS):
        tops = []
        for p in range(2):
            c0 = (h * 2 + p) * PEER_HALF
            s = lax.dot_general(sk_ref[h, p].astype(wq_ref.dtype), q[:, c0:c0 + PEER_HALF].astype(wq_ref.dtype),
                                nt, precision=prec, preferred_element_type=F32)
            tops.append(_topk_rows(s, k))
        (s0, i0), (s1, i1) = tops
        cs = [s0[0:1] + s1] + [s0[i:i + 1] + s1[0:half] for i in range(1, half)] + [s0[half:] + s1[0:1]]
        best_s, pos = _topk_rows(jnp.concatenate(cs, axis=0), k)
        mid = jnp.floor((pos - k) * (1.0 / half))
        end_mid = float(k + (half - 1) * half)
        i_rank = jnp.where(pos < k, 0.0, jnp.where(pos < end_mid, 1.0 + mid, pos - (end_mid - half)))
        j_rank = jnp.where(pos < k, pos, jnp.where(pos < end_mid, (pos - k) - half * mid, 0.0))
        ids = [_take_rows(i0, i_rank[n:n + 1]) * PEER_NKEYS + _take_rows(i1, j_rank[n:n + 1]) for n in range(k)]
        e = jnp.exp(best_s - best_s[0:1])
        gate_rows.append(e / jnp.sum(e, axis=0, keepdims=True))
        idx_rows.append(jnp.concatenate(ids, axis=0).astype(jnp.int32))
    idx_ref[...] = jnp.concatenate(idx_rows, axis=0).T
    gate_ref[...] = jnp.concatenate(gate_rows, axis=0).T


def peer_route(xn2d, peer_wq, peer_subkeys, *, tt=256, prec=None, wdtype=BF16):
    t, dx = xn2d.shape
    d, nq = peer_wq.shape
    n_sel = PEER_HEADS * PEER_TOPK
    return pl.pallas_call(
        functools.partial(_peer_route_kernel, prec=prec),
        grid=(t // tt,),
        in_specs=[
            pl.BlockSpec((tt, dx), lambda i: (i, 0)),
            pl.BlockSpec((d, nq), lambda i: (0, 0)),
            pl.BlockSpec((PEER_HEADS, 2, PEER_NKEYS, PEER_HALF), lambda i: (0, 0, 0, 0)),
        ],
        out_specs=[pl.BlockSpec((tt, n_sel), lambda i: (i, 0))] * 2,
        out_shape=[jax.ShapeDtypeStruct((t, n_sel), jnp.int32), jax.ShapeDtypeStruct((t, n_sel), F32)],
        compiler_params=_cparams(("parallel",)),
        name="peer_route",
    )(xn2d, peer_wq.astype(wdtype), peer_subkeys)


def _final_kernel(h_ref, y_ref, g_ref, *rest):
    o_ref = rest[-1]
    h = h_ref[...] + y_ref[...]
    ms = jnp.mean(h * h, axis=-1, keepdims=True)
    o_ref[...] = h * lax.rsqrt(ms + RMS_EPS) * g_ref[...]


def final_norm(h2d, y2d, g, *, out=None, row0=0, total_rows=None, tm=1024):
    t, d = h2d.shape
    total = t if total_rows is None else total_rows
    r0 = row0 // tm
    spec = pl.BlockSpec((tm, d), lambda i: (i, 0))
    in_specs = [spec, spec, pl.BlockSpec((1, d), lambda i: (0, 0))]
    args = [h2d, y2d, g.reshape(1, d)]
    aliases = {}
    if out is not None:
        in_specs.append(pl.BlockSpec(memory_space=pl.ANY))
        args.append(out)
        aliases = {3: 0}
    return pl.pallas_call(
        _final_kernel,
        grid=(t // tm,),
        in_specs=in_specs,
        out_specs=pl.BlockSpec((tm, d), lambda i: (r0 + i, 0)),
        out_shape=jax.ShapeDtypeStruct((total, d), F32),
        input_output_aliases=aliases,
        compiler_params=_cparams(("parallel",)),
        name="final_norm",
    )(*args)


SC_CORES = 2
SC_SUBCORES = 16
SC_LANES = 16
SC_WORKERS = SC_CORES * SC_SUBCORES
PEER_SEL = PEER_HEADS * PEER_TOPK
PEER_ROWS = 32
PEER_PARTS = PEER_SEL // PEER_ROWS
PEER_NBUF = 4
PEER_GROUP = 32
PEER_BF16_RUN = 4


def _pack_rows(w):
    half = w.shape[1] // 2
    bits = lax.bitcast_convert_type(w.astype(BF16), jnp.uint16).astype(jnp.uint32)
    return lax.bitcast_convert_type(bits[:, :half] | (bits[:, half:] << 16), jnp.int32)


def _unpack_words(w):
    lo = lax.bitcast_convert_type(lax.shift_left(w, jnp.int32(16)), F32)
    hi = lax.bitcast_convert_type(lax.bitwise_and(w, jnp.int32(-65536)), F32)
    return lo, hi


def _packed_dot(a_words, b_words):
    from jax.experimental.pallas import tpu_sc as plsc
    prods = [plsc.bitcast(a, BF16) * plsc.bitcast(b, BF16) for a, b in zip(a_words, b_words)]
    while len(prods) > 1:
        prods = [prods[k] + prods[k + 1] for k in range(0, len(prods), 2)]
    return _unpack_words(plsc.bitcast(prods[0], jnp.int32))


def _sc_mesh():
    from jax.experimental.pallas import tpu_sc as plsc
    return plsc.VectorSubcoreMesh(core_axis_name="c", subcore_axis_name="s",
                                  num_cores=SC_CORES, num_subcores=SC_SUBCORES)


def _sc_loop(n, body, carry):
    from jax.experimental.pallas import tpu_sc as plsc
    return plsc.parallel_loop(0, n, carry=carry)(body)


def _worker_base(tokens_per_worker):
    return (lax.axis_index("s") * SC_CORES + lax.axis_index("c")) * tokens_per_worker


def _gather_compute_loop(table_hbm, idx_v, rows_v, sem, stage_v, out_row, osem, grp, compute):
    n_gathers = PEER_PARTS * grp
    ahead = PEER_NBUF - 1

    def gather(j, b):
        i = j // PEER_PARTS if isinstance(j, int) else lax.shift_right_logical(j, PEER_PARTS.bit_length() - 1)
        h = j % PEER_PARTS if isinstance(j, int) else lax.bitwise_and(j, PEER_PARTS - 1)
        ids = idx_v.at[i, pl.ds(pl.multiple_of(h * PEER_ROWS, PEER_ROWS), PEER_ROWS)]
        return pltpu.make_async_copy(table_hbm.at[ids], rows_v.at[b], sem.at[b])

    def put(i, slot):
        return pltpu.make_async_copy(stage_v.at[slot], out_row(i), osem.at[slot])

    for j in range(ahead):
        gather(j, j).start()

    @pl.loop(0, n_gathers)
    def _(j):
        b = lax.bitwise_and(j, PEER_NBUF - 1)
        h = lax.bitwise_and(j, PEER_PARTS - 1)
        i = lax.shift_right_logical(j, PEER_PARTS.bit_length() - 1)
        slot = lax.bitwise_and(i, 1)

        @pl.when((h == 0) & (i >= 2))
        def _():
            put(i - 2, slot).wait()

        @pl.when(j + ahead < n_gathers)
        def _():
            gather(j + ahead, lax.bitwise_and(j + ahead, PEER_NBUF - 1)).start()

        gather(j, b).wait()
        compute(i, h, b, slot)

        @pl.when(h == PEER_PARTS - 1)
        def _():
            put(i, slot).start()

    put(grp - 2, 0).wait()
    put(grp - 1, 1).wait()


def peer_expert_dots(x_packed, idx, u_packed):
    t, half = x_packed.shape
    n_chunks = half // SC_LANES
    tpw = t // SC_WORKERS
    grp = min(PEER_GROUP, tpw)
    rows_tog = 4

    from jax.experimental.pallas import tpu_sc as plsc

    def body(x_hbm, idx_hbm, u_hbm, out_hbm, idx_v, x_v, rows_v, ps_v, part_v, sem, osem):
        base = _worker_base(tpw)

        def compute(i, h, b, slot):
            @pl.loop(0, PEER_ROWS // rows_tog)
            def _(rg):
                r0 = rg * rows_tog
                accs = [[None, None] for _ in range(rows_tog)]
                for c0 in range(0, n_chunks, PEER_BF16_RUN):
                    ats = [pl.ds((c0 + k) * SC_LANES, SC_LANES) for k in range(PEER_BF16_RUN)]
                    xw = [x_v[i, at] for at in ats]
                    for r in range(rows_tog):
                        terms = _packed_dot([rows_v[b, r0 + r, at] for at in ats], xw)
                        for k, term in enumerate(terms):
                            accs[r][k] = term if accs[r][k] is None else accs[r][k] + term
                for r in range(rows_tog):
                    part_v[h * PEER_ROWS + r0 + r, :] = accs[r][0] + accs[r][1]

            @pl.when(h == PEER_PARTS - 1)
            def _():
                lanes = lax.broadcasted_iota(jnp.int32, (SC_LANES,), 0)
                tots = [None] * (PEER_SEL // SC_LANES)
                for k in range(SC_LANES):
                    col_k = jnp.full((SC_LANES,), k, jnp.int32)
                    for n in range(len(tots)):
                        col = plsc.load_gather(part_v, [lanes + n * SC_LANES, col_k])
                        tots[n] = col if tots[n] is None else tots[n] + col
                for n, tot in enumerate(tots):
                    ps_v[slot, pl.ds(n * SC_LANES, SC_LANES)] = tot

        @pl.loop(0, tpw // grp)
        def _(g):
            t0 = base + g * grp
            pltpu.sync_copy(idx_hbm.at[pl.ds(t0, grp)], idx_v)
            pltpu.sync_copy(x_hbm.at[pl.ds(t0, grp)], x_v)
            _gather_compute_loop(u_hbm, idx_v, rows_v, sem, ps_v, lambda i: out_hbm.at[t0 + i], osem, grp, compute)

    return pl.kernel(
        body,
        out_type=jax.ShapeDtypeStruct((t, PEER_SEL), F32),
        mesh=_sc_mesh(),
        scratch_types=[
            pltpu.VMEM((grp, PEER_SEL), jnp.int32),
            pltpu.VMEM((grp, half), jnp.int32),
            pltpu.VMEM((PEER_NBUF, PEER_ROWS, half), jnp.int32),
            pltpu.VMEM((2, PEER_SEL), F32),
            pltpu.VMEM((PEER_SEL, SC_LANES), F32),
            pltpu.SemaphoreType.DMA((PEER_NBUF,)),
            pltpu.SemaphoreType.DMA((2,)),
        ],
        compiler_params=pltpu.CompilerParams(needs_layout_passes=False),
        name="peer_expert_dots",
    )(x_packed, idx, u_packed)


def peer_expert_mix(hgw, idx, v_packed):
    t = hgw.shape[0]
    half = v_packed.shape[1]
    d = 2 * half
    tpw = t // SC_WORKERS
    grp = min(PEER_GROUP, tpw)
    n_parts = 2
    cpp = half // SC_LANES // n_parts
    from jax.experimental.pallas import tpu_sc as plsc

    def body(hg_hbm, idx_hbm, v_hbm, out_hbm, idx_v, hg_v, rows_v, o_v2, sem, osem):
        base = _worker_base(tpw)

        def compute(i, h, b, slot):
            token = jnp.full((SC_LANES,), i, jnp.int32)
            for part in range(n_parts):
                def rbody(rq, accs):
                    r0 = rq * PEER_BF16_RUN
                    s = [plsc.load_gather(hg_v, [token, jnp.full((SC_LANES,), h * PEER_ROWS + r0 + k, jnp.int32)])
                         for k in range(PEER_BF16_RUN)]
                    new = []
                    for c in range(cpp):
                        at = pl.ds((part * cpp + c) * SC_LANES, SC_LANES)
                        lo, hi = _packed_dot([rows_v[b, r0 + k, at] for k in range(PEER_BF16_RUN)], s)
                        new.append(accs[2 * c] + lo)
                        new.append(accs[2 * c + 1] + hi)
                    return tuple(new)

                accs = _sc_loop(PEER_ROWS // PEER_BF16_RUN, rbody,
                                tuple(jnp.zeros((SC_LANES,), F32) for _ in range(2 * cpp)))
                def store(overwrite):
                    for c in range(cpp):
                        lo_at = pl.ds((part * cpp + c) * SC_LANES, SC_LANES)
                        hi_at = pl.ds(half + (part * cpp + c) * SC_LANES, SC_LANES)
                        if overwrite:
                            o_v2[slot, lo_at] = accs[2 * c]
                            o_v2[slot, hi_at] = accs[2 * c + 1]
                        else:
                            o_v2[slot, lo_at] = o_v2[slot, lo_at] + accs[2 * c]
                            o_v2[slot, hi_at] = o_v2[slot, hi_at] + accs[2 * c + 1]

                pl.when(h == 0)(functools.partial(store, True))
                pl.when(h != 0)(functools.partial(store, False))

        @pl.loop(0, tpw // grp)
        def _(g):
            t0 = base + g * grp
            pltpu.sync_copy(idx_hbm.at[pl.ds(t0, grp)], idx_v)
            pltpu.sync_copy(hg_hbm.at[pl.ds(t0, grp)], hg_v)
            _gather_compute_loop(v_hbm, idx_v, rows_v, sem, o_v2, lambda i: out_hbm.at[t0 + i], osem, grp, compute)

    return pl.kernel(
        body,
        out_type=jax.ShapeDtypeStruct((t, d), F32),
        mesh=_sc_mesh(),
        scratch_types=[
            pltpu.VMEM((grp, PEER_SEL), jnp.int32),
            pltpu.VMEM((grp, PEER_SEL), jnp.int32),
            pltpu.VMEM((PEER_NBUF, PEER_ROWS, half), jnp.int32),
            pltpu.VMEM((2, d), F32),
            pltpu.SemaphoreType.DMA((PEER_NBUF,)),
            pltpu.SemaphoreType.DMA((2,)),
        ],
        compiler_params=pltpu.CompilerParams(needs_layout_passes=False),
        name="peer_expert_mix",
    )(hgw, idx, v_packed)


def _peer_act_kernel(pre_ref, gate_ref, o_ref):
    pre = pre_ref[...]
    hg = 0.5 * pre * (1.0 + lax.erf(pre * (1.0 / math.sqrt(2.0)))) * gate_ref[...]
    bits = lax.bitcast_convert_type(hg.astype(BF16).astype(F32), jnp.int32)
    o_ref[...] = lax.bitwise_or(bits, lax.shift_right_logical(bits, jnp.int32(16)))


def peer_act(pre, gates, *, tm=1024):
    t, n = pre.shape
    spec = pl.BlockSpec((tm, n), lambda i: (i, 0))
    return pl.pallas_call(
        _peer_act_kernel,
        grid=(t // tm,),
        in_specs=[spec, spec],
        out_specs=spec,
        out_shape=jax.ShapeDtypeStruct((t, n), jnp.int32),
        compiler_params=_cparams(("parallel",)),
        name="peer_act",
    )(pre, gates)


BATCH_GROUPS = 8


def kernel(x, norm1_g, w_in, rwkv_mu, w0, w_lora_up, a0, a_lora_up, g_lora_up, k_k, k_a, r_k, lnx_g, lnx_b,
           w_proj_a, w_proj_b, w_out, norm2_g, peer_wq, peer_subkeys, peer_u, peer_v, rel_bias, normf_g):
    bsz, seq, d = x.shape
    depth = norm1_g.shape[0]
    groups = BATCH_GROUPS if bsz % BATCH_GROUPS == 0 else 1
    gb = bsz // groups
    tg = gb * seq
    t = bsz * seq
    src = x.reshape(t, d)
    for l in range(depth):
        w_pad = jnp.concatenate([
            w_in[l][:, :COL_A + COL_B_RAW],
            jnp.zeros((d, COL_B - COL_B_RAW), w_in.dtype),
            w_in[l][:, COL_A + COL_B_RAW:]], axis=1).astype(BF16)
        u_packed = _pack_rows(peer_u[l])
        tables = {}
        last = l == depth - 1

        def mix(pending, tie=None):
            row0, h2d, ps, gates, idx = pending
            hgx = peer_act(ps, gates)
            if "v" not in tables:
                v_src = peer_v[l]
                if tie is not None:
                    tie, v_src = lax.optimization_barrier((tie, v_src))
                tables["v"] = _pack_rows(v_src)
            if tie is not None:
                tie, hgx = lax.optimization_barrier((tie, hgx))
            return tie, (row0, h2d, peer_expert_mix(hgx, idx, tables["v"]))

        outs = []

        def close(mixed):
            row0, h2d, y2d = mixed
            if last:
                outs.append(final_norm(h2d, y2d, normf_g, out=outs[-1] if outs else None, row0=row0, total_rows=t))
            else:
                outs.append(h2d + y2d)

        halves = gb == 1 and seq % (2 * MOBA_BLOCK) == 0 and (seq // 2) % (SC_WORKERS * PEER_GROUP) == 0

        pending = closing = None
        for g in range(groups):
            p2d = norm_proj(src, norm1_g[l], w_pad, row0=g * tg, rows=tg)
            p3d = p2d.reshape(gb, seq, -1)
            prep = state = None
            for s0, sn in ([(0, seq // 2), (seq // 2, seq // 2)] if halves and g == 0 else [(0, seq)]):
                oa = moba_attention(p3d, rel_bias, q0=s0 // MOBA_BLOCK, nq=sn // MOBA_BLOCK)
                mixed = None
                if pending is not None:
                    oa, mixed = mix(pending, oa)
                if closing is not None:
                    oa, y2d = lax.optimization_barrier((oa, closing[2]))
                    close(closing[:2] + (y2d,))
                    closing = None
                if prep is None:
                    prep = rwkv_prep(p3d, rwkv_mu[l], w0[l], w_lora_up[l], a0[l], a_lora_up[l], g_lora_up[l],
                                     k_k[l], k_a[l], r_k[l])
                ob, state = rwkv_scan(*prep, lnx_g[l], lnx_b[l], state=state,
                                      c0=s0 // RWKV_CHUNK, nc=sn // RWKV_CHUNK)
                nt = gb * sn
                h2d, xn2 = merge_out(src, oa.reshape(nt, WIDTH), ob.reshape(nt, WIDTH), p2d, w_proj_a[l], w_proj_b[l],
                                     w_out[l], norm2_g[l], row0=g * tg + s0, prow0=s0)
                idx, gates = peer_route(xn2, peer_wq[l], peer_subkeys[l])
                if mixed is not None:
                    idx, y2d = lax.optimization_barrier((idx, mixed[2]))
                    closing = mixed[:2] + (y2d,)
                pending = (g * tg + s0, h2d, peer_expert_dots(xn2, idx, u_packed), gates, idx)
        if closing is not None:
            close(closing)
        close(mix(pending)[1])
        src = outs[-1] if last else jnp.concatenate(outs, axis=0)
    return src.reshape(bsz, seq, d)
```

```python
import functools
import math

import jax
import jax.numpy as jnp
from jax import lax
from jax.experimental import pallas as pl
from jax.experimental.pallas import tpu as pltpu

F32 = jnp.float32
BF16 = jnp.bfloat16
HI = lax.Precision.HIGHEST

LANES = 128
HEAD_DIM = 64
HEADS = 8
PAIRS = HEADS // 2
WIDTH = HEADS * HEAD_DIM
MOBA_BLOCK = 256
MOBA_TOPK = 3
MOBA_LO = 64
REL_BUCKETS = 32
REL_MAX_DIST = 128
DECAY_LORA = 64
AAA_LORA = 64
GATE_LORA = 160
GN_EPS = 64e-5
RMS_EPS = 1e-6
NEG = -1e30
RWKV_CHUNK = 64
COL_A = 3 * WIDTH
COL_B_RAW = 3 * WIDTH + DECAY_LORA + AAA_LORA + GATE_LORA
COL_B = 4 * WIDTH
COL_G_OFF = COL_A + COL_B
VMEM_LIMIT = 56 * 1024 * 1024


def _cparams(sem):
    return pltpu.CompilerParams(dimension_semantics=sem, vmem_limit_bytes=VMEM_LIMIT)


def _norm_proj_kernel(x_ref, g_ref, w_ref, o_ref, xn_ref):
    @pl.when(pl.program_id(1) == 0)
    def _():
        x = x_ref[...]
        ms = jnp.mean(x * x, axis=-1, keepdims=True)
        xn_ref[...] = (x * lax.rsqrt(ms + RMS_EPS) * g_ref[...]).astype(xn_ref.dtype)

    o_ref[...] = jnp.dot(xn_ref[...], w_ref[...], preferred_element_type=F32).astype(o_ref.dtype)


def norm_proj(x2d, g, w, *, row0=0, rows=None, tm=512, tn=512, out_dtype=F32):
    d = x2d.shape[1]
    t = x2d.shape[0] if rows is None else rows
    n = w.shape[1]
    r0 = row0 // tm
    return pl.pallas_call(
        _norm_proj_kernel,
        grid=(t // tm, n // tn),
        in_specs=[
            pl.BlockSpec((tm, d), lambda i, j: (r0 + i, 0)),
            pl.BlockSpec((1, d), lambda i, j: (0, 0)),
            pl.BlockSpec((d, tn), lambda i, j: (0, j)),
        ],
        out_specs=pl.BlockSpec((tm, tn), lambda i, j: (i, j)),
        out_shape=jax.ShapeDtypeStruct((t, n), out_dtype),
        scratch_shapes=[pltpu.VMEM((tm, d), w.dtype)],
        compiler_params=_cparams(("parallel", "arbitrary")),
        name="norm_proj",
    )(x2d, g.reshape(1, d), w)


def _rel_bucket(dist):
    n = jnp.maximum(dist, 0)
    max_exact = REL_BUCKETS // 2
    nf = jnp.maximum(n, 1).astype(F32)
    large = max_exact + (jnp.log(nf / max_exact) / math.log(REL_MAX_DIST / max_exact)
                         * (REL_BUCKETS - max_exact)).astype(jnp.int32)
    large = jnp.minimum(large, REL_BUCKETS - 1)
    return jnp.where(n < max_exact, n, large)


def _moba_kernel(q_ref, k_ref, v_ref, bown_ref, bprev_ref, bfar_ref, o_ref,
                 kb_ref, vb_ref, kbar_ref, *, n_blocks, q0):
    qb = pl.program_id(2) + q0
    blk = MOBA_BLOCK
    scale = 1.0 / math.sqrt(HEAD_DIM)

    rows2 = 2 * blk
    nt = (((1,), (1,)), ((), ()))

    @pl.when(pl.program_id(2) == 0)
    def _():
        kbar_ref[...] = jnp.zeros_like(kbar_ref)
        lane_b = lax.broadcasted_iota(jnp.int32, (blk, LANES), 1)
        for n in range(n_blocks):
            kblk = k_ref[0, n * blk:(n + 1) * blk, :]
            kbar_ref[n:n + 1, :] = jnp.mean(kblk, axis=0, keepdims=True)
            kb_ref[n * blk:(n + 1) * blk, 0:LANES] = kblk.astype(BF16)
            kb_ref[n * blk:(n + 1) * blk, LANES:] = ((lane_b == n) | (lane_b == MOBA_LO + n)).astype(BF16)
        vb_ref[...] = v_ref[0].astype(BF16)

    q2 = q_ref[0]
    first = lax.broadcasted_iota(jnp.int32, (blk, LANES), 1) < HEAD_DIM
    qh = jnp.concatenate([jnp.where(first, q2, 0.0), jnp.where(first, 0.0, q2)], axis=0)
    lane = lax.broadcasted_iota(jnp.int32, (rows2, LANES), 1)
    rowi = lax.broadcasted_iota(jnp.int32, (rows2, LANES), 0)
    gate = lax.dot_general(qh.astype(BF16), kbar_ref[...].astype(BF16), nt, preferred_element_type=F32)
    g = jnp.where(lane < qb, gate, -jnp.inf)
    chosen = lane < 0
    lane_f = lane.astype(F32)
    for _ in range(MOBA_TOPK):
        m = jnp.max(g, axis=1, keepdims=True)
        idx = jnp.min(jnp.where(g == m, lane_f, float(LANES)), axis=1, keepdims=True)
        hit = (lane_f == idx) & (m > -jnp.inf)
        chosen = chosen | hit
        g = jnp.where(hit, -jnp.inf, g)
    nfar = qb - 1
    bfar = jnp.where(rowi < blk, bfar_ref[0, 0:1, 0:1], bfar_ref[1, 0:1, 0:1])
    bhi = bfar.astype(BF16).astype(F32)
    madd = jnp.where(lane < nfar, jnp.where(chosen, bhi, NEG),
                     jnp.where(lane == nfar, jnp.where(chosen, 0.0, NEG),
                               jnp.where((lane >= MOBA_LO) & (lane - MOBA_LO < nfar), bfar - bhi, 0.0)))
    q_aug = jnp.concatenate([(qh * scale).astype(BF16), madd.astype(BF16)], axis=1)

    prev0 = pl.multiple_of(jnp.maximum(nfar, 0) * blk, blk)
    own0 = pl.multiple_of(qb * blk, blk)
    s_prev = (lax.dot_general(q_aug, kb_ref[pl.ds(prev0, blk), :], nt, preferred_element_type=F32)
              + bprev_ref[...].reshape(rows2, blk) + jnp.where(qb > 0, 0.0, NEG))
    s_own = (lax.dot_general(q_aug, kb_ref[pl.ds(own0, blk), :], nt, preferred_element_type=F32)
             + bown_ref[...].reshape(rows2, blk))
    r = lax.broadcasted_iota(jnp.int32, (rows2, blk), 0)
    c = lax.broadcasted_iota(jnp.int32, (rows2, blk), 1)
    s_own = jnp.where(lax.bitwise_and(r, blk - 1) >= c, s_own, NEG)
    s = jnp.concatenate([s_prev, s_own], axis=1)
    m_i = jnp.max(s, axis=1, keepdims=True)
    p = jnp.exp(s - m_i)
    l_i = jnp.sum(p, axis=1, keepdims=True)
    v0 = jnp.concatenate([vb_ref[pl.ds(prev0, blk), :], vb_ref[pl.ds(own0, blk), :]], axis=0)
    acc = jnp.dot(p.astype(BF16), v0, preferred_element_type=F32)

    def body(it, carry):
        m_i, l_i, acc = carry
        k0 = pl.multiple_of(it * rows2, rows2)
        s = lax.dot_general(q_aug, kb_ref[pl.ds(k0, rows2), :], nt, preferred_element_type=F32)
        tail = jnp.where(2 * it + 1 < nfar, 0.0, NEG)
        s = jnp.concatenate([s[:, :blk], s[:, blk:] + tail], axis=1)
        m_new = jnp.maximum(m_i, jnp.max(s, axis=1, keepdims=True))
        alpha = jnp.exp(m_i - m_new)
        p = jnp.exp(s - m_new)
        l_new = alpha * l_i + jnp.sum(p, axis=1, keepdims=True)
        acc_new = alpha * acc + jnp.dot(p.astype(BF16), vb_ref[pl.ds(k0, rows2), :], preferred_element_type=F32)
        return m_new, l_new, acc_new

    m_i, l_i, acc = lax.fori_loop(0, (jnp.maximum(nfar, 0) + 1) // 2, body, (m_i, l_i, acc))
    out = acc / l_i
    o_ref[0] = jnp.where(first, out[:blk], out[blk:])


def moba_attention(p3d, rel_bias, *, q0=0, nq=None):
    bsz, seq, _ = p3d.shape
    blk = MOBA_BLOCK
    n_blocks = seq // blk
    nq = n_blocks - q0 if nq is None else nq
    assert n_blocks <= MOBA_LO and seq % blk == 0
    span = 2 * blk
    by_dist = rel_bias[:, _rel_bucket(jnp.arange(span))].astype(F32)
    shift = jnp.arange(span)

    def toeplitz(c):
        k = jnp.where(shift < blk, shift, shift - span)
        s = by_dist[:, jnp.clip(c - k, 0, span - 1)]
        tiled = jnp.tile(s, (1, blk))[:, :blk * (span - 1)]
        return tiled.reshape(HEADS, blk, span - 1)[:, :, :blk]

    bias_own = toeplitz(0)
    bias_prev = toeplitz(blk)
    bias_far = jnp.broadcast_to(rel_bias[:, REL_BUCKETS - 1].astype(F32)[:, None, None], (HEADS, 8, LANES))
    kern = functools.partial(_moba_kernel, n_blocks=n_blocks, q0=q0)
    return pl.pallas_call(
        kern,
        grid=(bsz, PAIRS, nq),
        in_specs=[
            pl.BlockSpec((1, blk, LANES), lambda b, h, i: (b, q0 + i, h)),
            pl.BlockSpec((1, seq, LANES), lambda b, h, i: (b, 0, PAIRS + h)),
            pl.BlockSpec((1, seq, LANES), lambda b, h, i: (b, 0, 2 * PAIRS + h)),
            pl.BlockSpec((2, blk, blk), lambda b, h, i: (h, 0, 0)),
            pl.BlockSpec((2, blk, blk), lambda b, h, i: (h, 0, 0)),
            pl.BlockSpec((2, 8, LANES), lambda b, h, i: (h, 0, 0)),
        ],
        out_specs=pl.BlockSpec((1, blk, LANES), lambda b, h, i: (b, i, h)),
        out_shape=jax.ShapeDtypeStruct((bsz, nq * blk, WIDTH), F32),
        scratch_shapes=[
            pltpu.VMEM((seq, 2 * LANES), BF16),
            pltpu.VMEM((seq, LANES), BF16),
            pltpu.VMEM((LANES, LANES), F32),
        ],
        compiler_params=_cparams(("parallel", "parallel", "arbitrary")),
        name="moba",
    )(p3d, p3d, p3d, bias_own, bias_prev, bias_far)


def _shifted(x, carry_row):
    rows = lax.broadcasted_iota(jnp.int32, x.shape, 0)
    return jnp.where(rows == 0, carry_row, pltpu.roll(x, 1, axis=0))


def _rwkv_prep_kernel(pr_ref, pk_ref, pv_ref, pl_ref, mu_ref, vec_ref, ww_ref, wa_ref, wg_ref,
                      bd_ref, tri_ref,
                      rt_ref, kt_ref, kd_ref, bd_out_ref, v_ref, g_ref, bonus_ref, pend_ref,
                      carry_ref, *, chunk):
    @pl.when(pl.program_id(1) == 0)
    def _():
        carry_ref[...] = jnp.zeros_like(carry_ref)

    def mix(ref, j):
        x = ref[0]
        mu = mu_ref[0:1, j * WIDTH:(j + 1) * WIDTH]
        prev = _shifted(x, carry_ref[0:1, j * WIDTH:(j + 1) * WIDTH])
        carry_ref[0:1, j * WIDTH:(j + 1) * WIDTH] = x[x.shape[0] - 1:, :]
        return x + mu * (prev - x)

    r = mix(pr_ref, 0)
    k = mix(pk_ref, 1)
    v = mix(pv_ref, 2)
    lo = mix(pl_ref, 3)
    w0, a0, k_k, k_a, r_k = (vec_ref[i:i + 1, :] for i in range(5))
    xwa = lo[:, 0:LANES]
    xg = lo[:, LANES:3 * LANES]
    lw = jnp.dot(jnp.tanh(xwa), ww_ref[...], precision=HI, preferred_element_type=F32)
    la = jnp.dot(xwa, wa_ref[...], precision=HI, preferred_element_type=F32)
    g = jnp.dot(jax.nn.sigmoid(xg), wg_ref[...], precision=HI, preferred_element_type=F32)
    z = -(w0 + lw)
    softplus = jnp.maximum(z, 0.0) + jnp.log(1.0 + jnp.exp(-jnp.abs(z)))
    logw = -jnp.exp(-softplus - 0.5)
    a = jax.nn.sigmoid(a0 + la)
    kk = k * k_k
    ss = jnp.dot(kk * kk, bd_ref[...], precision=HI, preferred_element_type=F32)
    kk = kk / jnp.maximum(jnp.sqrt(ss), 1e-12)
    k2 = k * (1.0 + (a - 1.0) * k_a)
    rk = jnp.dot(r * k2 * r_k, bd_ref[...], precision=HI, preferred_element_type=F32)
    cs = jnp.dot(tri_ref[...], logw, precision=HI, preferred_element_type=F32)
    e_pos = jnp.exp(cs)
    e_neg = jnp.exp(-cs)
    rt_ref[0] = r * e_pos
    kt_ref[0] = kk * jnp.exp(cs - logw)
    kd_ref[0] = k2 * e_neg
    bd_out_ref[0] = kk * a * e_neg
    v_ref[0] = v
    g_ref[0] = g
    bonus_ref[0] = rk * v
    ts = e_pos.shape[0]
    for c in range(ts // chunk):
        pend_ref[0, c:c + 1, :] = e_pos[(c + 1) * chunk - 1:(c + 1) * chunk, :]


def rwkv_prep(p3d, rwkv_mu, w0, w_lora_up, a0, a_lora_up, g_lora_up, k_k, k_a, r_k, *, ts=512):
    bsz, seq, _ = p3d.shape
    chunk = RWKV_CHUNK
    ts = min(ts, seq)
    mu = jnp.pad(rwkv_mu, (0, COL_B - COL_B_RAW)).reshape(1, COL_B)
    vec = jnp.stack([w0, a0, k_k, k_a, r_k.reshape(-1)] + [jnp.zeros_like(w0)] * 3).astype(F32)
    ww = jnp.zeros((LANES, WIDTH), F32).at[:DECAY_LORA].set(w_lora_up)
    wa = jnp.zeros((LANES, WIDTH), F32).at[DECAY_LORA:DECAY_LORA + AAA_LORA].set(a_lora_up)
    wg = jnp.zeros((2 * LANES, WIDTH), F32).at[:GATE_LORA].set(g_lora_up)
    hid = jnp.arange(WIDTH) // HEAD_DIM
    bd = (hid[:, None] == hid[None, :]).astype(F32)
    tix = jnp.arange(ts)
    tri = ((tix[:, None] // chunk == tix[None, :] // chunk) & (tix[None, :] <= tix[:, None])).astype(F32)
    c0 = COL_A // WIDTH
    big = jax.ShapeDtypeStruct((bsz, seq, WIDTH), F32)
    wspec = lambda shape: pl.BlockSpec(shape, lambda b, i: (0, 0))
    ospec = pl.BlockSpec((1, ts, WIDTH), lambda b, i: (b, i, 0))
    return pl.pallas_call(
        functools.partial(_rwkv_prep_kernel, chunk=chunk),
        grid=(bsz, seq // ts),
        in_specs=[
            pl.BlockSpec((1, ts, WIDTH), lambda b, i: (b, i, c0)),
            pl.BlockSpec((1, ts, WIDTH), lambda b, i: (b, i, c0 + 1)),
            pl.BlockSpec((1, ts, WIDTH), lambda b, i: (b, i, c0 + 2)),
            pl.BlockSpec((1, ts, WIDTH), lambda b, i: (b, i, c0 + 3)),
            wspec((1, COL_B)), wspec((8, WIDTH)), wspec((LANES, WIDTH)), wspec((LANES, WIDTH)),
            wspec((2 * LANES, WIDTH)), wspec((WIDTH, WIDTH)), wspec((ts, ts)),
        ],
        out_specs=[ospec] * 7 + [pl.BlockSpec((1, ts // chunk, WIDTH), lambda b, i: (b, i, 0))],
        out_shape=[big] * 7 + [jax.ShapeDtypeStruct((bsz, seq // chunk, WIDTH), F32)],
        scratch_shapes=[pltpu.VMEM((8, COL_B), F32)],
        compiler_params=_cparams(("parallel", "arbitrary")),
        name="rwkv_prep",
    )(p3d, p3d, p3d, p3d, mu, vec, ww, wa, wg, bd, tri)


def _rwkv_scan_kernel(rt_ref, kt_ref, kd_ref, bd_ref, v_ref, g_ref, bonus_ref, pend_ref, ln_ref, sin_ref,
                      o_ref, state_ref, *, chunk, prec):
    @pl.when(pl.program_id(1) == 0)
    def _():
        state_ref[...] = sin_ref[...]

    c2 = 2 * chunk
    lane = lax.broadcasted_iota(jnp.int32, (chunk, LANES), 1)
    first = lane < HEAD_DIM
    row = lax.broadcasted_iota(jnp.int32, (c2, c2), 0)
    col = lax.broadcasted_iota(jnp.int32, (c2, c2), 1)
    eye = (row == col).astype(F32)
    hrow = lax.broadcasted_iota(jnp.int32, (LANES, LANES), 0) // HEAD_DIM
    hcol = lax.broadcasted_iota(jnp.int32, (LANES, LANES), 1) // HEAD_DIM
    head_mean = jnp.where(hrow == hcol, 1.0 / HEAD_DIM, 0.0).astype(F32)
    nt = (((1,), (1,)), ((), ()))
    tn = (((0,), (0,)), ((), ()))
    dot = functools.partial(jnp.dot, precision=prec, preferred_element_type=F32)
    dotg = functools.partial(lax.dot_general, precision=prec, preferred_element_type=F32)

    def stack(x):
        return jnp.concatenate([jnp.where(first, x, 0.0), jnp.where(first, 0.0, x)], axis=0)

    pairs = range(PAIRS)
    sls = [slice(hp * LANES, (hp + 1) * LANES) for hp in pairs]
    rs, ks, kds, bs, vs = ([stack(ref[0, :, sl]) for sl in sls] for ref in (rt_ref, kt_ref, kd_ref, bd_ref, v_ref))
    hts = [state_ref[0, hp] for hp in pairs]
    big = [dotg(jnp.concatenate([ks[hp], rs[hp]], axis=0), jnp.concatenate([bs[hp], kds[hp]], axis=0), nt)
           for hp in pairs]
    a_b = [jnp.where(row > col, big[hp][0:c2, 0:c2], 0.0) for hp in pairs]
    a_k = [jnp.where(row > col, big[hp][0:c2, c2:], 0.0) for hp in pairs]
    a_rb = [jnp.where(row >= col, big[hp][c2:, 0:c2], 0.0) for hp in pairs]
    a_rk = [jnp.where(row >= col, big[hp][c2:, c2:], 0.0) for hp in pairs]
    kh = [dotg(jnp.concatenate([ks[hp], rs[hp]], axis=0), hts[hp], nt) for hp in pairs]
    av = [dot(jnp.concatenate([a_k[hp], a_rk[hp]], axis=0), vs[hp]) for hp in pairs]
    vk = [dotg(vs[hp], kds[hp], tn) for hp in pairs]
    inv = [eye - a_b[hp] for hp in pairs]
    pw = [dot(a_b[hp], a_b[hp]) for hp in pairs]
    n_sq = int(math.log2(chunk)) - 1
    for lvl in range(n_sq):
        if lvl + 1 < n_sq:
            both = [dot(jnp.concatenate([inv[hp], pw[hp]], axis=0), pw[hp]) for hp in pairs]
            inv = [inv[hp] + both[hp][0:c2] for hp in pairs]
            pw = [both[hp][c2:] for hp in pairs]
        else:
            inv = [inv[hp] + dot(inv[hp], pw[hp]) for hp in pairs]
    us = [dot(inv[hp], kh[hp][0:c2] + av[hp][0:c2]) for hp in pairs]
    ub = [dotg(us[hp], bs[hp], tn) for hp in pairs]
    au = [dot(a_rb[hp], us[hp]) for hp in pairs]
    for hp in pairs:
        sl = sls[hp]
        pend = pend_ref[0, 0, 0:1, sl]
        state_ref[0, hp] = (hts[hp] + vk[hp] - ub[hp]) * pend
        os_ = kh[hp][c2:] + av[hp][c2:] - au[hp]
        o = os_[0:chunk] + os_[chunk:]
        mu = jnp.dot(o, head_mean, precision=HI, preferred_element_type=F32)
        d = o - mu
        var = jnp.dot(d * d, head_mean, precision=HI, preferred_element_type=F32)
        on = d * lax.rsqrt(var + GN_EPS) * ln_ref[0:1, sl] + ln_ref[1:2, sl]
        o_ref[0, :, sl] = (on + bonus_ref[0, :, sl]) * g_ref[0, :, sl]


def rwkv_scan(rt, kt, kd, bd, v, g, bonus, pend, lnx_g, lnx_b, *, state=None, c0=0, nc=None, prec=None):
    bsz, seq, _ = rt.shape
    chunk = RWKV_CHUNK
    n_chunks = seq // chunk
    nc = n_chunks - c0 if nc is None else nc
    ln = jnp.stack([lnx_g, lnx_b] + [jnp.zeros_like(lnx_g)] * 6).astype(F32)
    pend4 = pend.reshape(bsz, n_chunks, 1, WIDTH)
    if state is None:
        state = jnp.zeros((bsz, PAIRS, LANES, LANES), F32)
    spec = pl.BlockSpec((1, chunk, WIDTH), lambda b, c: (b, c0 + c, 0))
    sspec = pl.BlockSpec((1, PAIRS, LANES, LANES), lambda b, c: (b, 0, 0, 0))
    return pl.pallas_call(
        functools.partial(_rwkv_scan_kernel, chunk=chunk, prec=prec),
        grid=(bsz, nc),
        in_specs=[spec] * 7 + [
            pl.BlockSpec((1, 1, 1, WIDTH), lambda b, c: (b, c0 + c, 0, 0)),
            pl.BlockSpec((8, WIDTH), lambda b, c: (0, 0)),
            sspec,
        ],
        out_specs=[pl.BlockSpec((1, chunk, WIDTH), lambda b, c: (b, c, 0)), sspec],
        out_shape=[jax.ShapeDtypeStruct((bsz, nc * chunk, WIDTH), F32),
                   jax.ShapeDtypeStruct((bsz, PAIRS, LANES, LANES), F32)],
        compiler_params=_cparams(("parallel", "arbitrary")),
        name="rwkv_scan",
    )(rt, kt, kd, bd, v, g, bonus, pend4, ln, state)


def _merge_kernel(x_ref, oa_ref, ob_ref, ga_ref, gb_ref, wa_ref, wb_ref, wo_ref, g2_ref,
                  h_ref, xn_ref, acc_ref):
    j = pl.program_id(1)

    @pl.when(j == 0)
    def _():
        acc_ref[...] = x_ref[...]

    ya = jnp.dot(oa_ref[...].astype(BF16), wa_ref[...], preferred_element_type=F32)
    yb = jnp.dot(ob_ref[...].astype(BF16), wb_ref[...], preferred_element_type=F32)
    y = jax.nn.sigmoid(ga_ref[...]) * ya + jax.nn.sigmoid(gb_ref[...]) * yb
    acc_ref[...] += jnp.dot(y.astype(BF16), wo_ref[...], preferred_element_type=F32)

    @pl.when(j == pl.num_programs(1) - 1)
    def _():
        h = acc_ref[...]
        h_ref[...] = h
        ms = jnp.mean(h * h, axis=-1, keepdims=True)
        xn_ref[...] = _pack_halves(h * lax.rsqrt(ms + RMS_EPS) * g2_ref[...])


def _pack_halves(x):
    half = x.shape[1] // 2
    lo = lax.bitcast_convert_type(x[:, :half].astype(BF16).astype(F32), jnp.int32)
    hi = lax.bitcast_convert_type(x[:, half:].astype(BF16).astype(F32), jnp.int32)
    return lax.bitwise_or(lax.shift_right_logical(lo, jnp.int32(16)), hi)


def _unpack_halves(words):
    lo, hi = _unpack_words(words)
    return jnp.concatenate([lo, hi], axis=1)


def merge_out(x2d, oa, ob, p2d, w_proj_a, w_proj_b, w_out, norm2_g, *, row0=0, prow0=0, tm=512):
    t, d = oa.shape[0], x2d.shape[1]
    r0 = row0 // tm
    p0 = prow0 // tm
    tn = WIDTH
    nj = d // tn
    g0 = COL_G_OFF // tn
    return pl.pallas_call(
        _merge_kernel,
        grid=(t // tm, nj),
        in_specs=[
            pl.BlockSpec((tm, d), lambda i, j: (r0 + i, 0)),
            pl.BlockSpec((tm, WIDTH), lambda i, j: (i, 0)),
            pl.BlockSpec((tm, WIDTH), lambda i, j: (i, 0)),
            pl.BlockSpec((tm, tn), lambda i, j: (p0 + i, g0 + j)),
            pl.BlockSpec((tm, tn), lambda i, j: (p0 + i, g0 + nj + j)),
            pl.BlockSpec((WIDTH, tn), lambda i, j: (0, j)),
            pl.BlockSpec((WIDTH, tn), lambda i, j: (0, j)),
            pl.BlockSpec((tn, d), lambda i, j: (j, 0)),
            pl.BlockSpec((1, d), lambda i, j: (0, 0)),
        ],
        out_specs=[pl.BlockSpec((tm, d), lambda i, j: (i, 0)), pl.BlockSpec((tm, d // 2), lambda i, j: (i, 0))],
        out_shape=[jax.ShapeDtypeStruct((t, d), F32), jax.ShapeDtypeStruct((t, d // 2), jnp.int32)],
        scratch_shapes=[pltpu.VMEM((tm, d), F32)],
        compiler_params=_cparams(("parallel", "arbitrary")),
        name="merge_out",
    )(x2d, oa, ob, p2d, p2d, w_proj_a.astype(BF16), w_proj_b.astype(BF16), w_out.astype(BF16),
      norm2_g.reshape(1, d))


PEER_HEADS = 8
PEER_NKEYS = 128
PEER_TOPK = 16
PEER_HALF = 128


def _topk_rows(s, k):
    n = s.shape[0]
    rows = lax.broadcasted_iota(jnp.int32, s.shape, 0).astype(F32)
    vals, ids = [], []
    for _ in range(k):
        m = jnp.max(s, axis=0, keepdims=True)
        first = jnp.min(jnp.where(s == m, rows, float(n)), axis=0, keepdims=True)
        vals.append(m)
        ids.append(first)
        s = jnp.where(rows == first, -jnp.inf, s)
    return jnp.concatenate(vals, axis=0), jnp.concatenate(ids, axis=0)


def _take_rows(table, ids):
    rows = lax.broadcasted_iota(jnp.int32, table.shape, 0).astype(F32)
    return jnp.sum(jnp.where(rows == ids, table, 0.0), axis=0, keepdims=True)


def _peer_route_kernel(xn_ref, wq_ref, sk_ref, idx_ref, gate_ref, *, prec):
    tt = xn_ref.shape[0]
    k = PEER_TOPK
    xn = _unpack_halves(xn_ref[...]) if xn_ref.dtype == jnp.int32 else xn_ref[...]
    q = jnp.dot(xn.astype(wq_ref.dtype), wq_ref[...], precision=prec, preferred_element_type=F32)
    nt = (((1,), (1,)), ((), ()))
    idx_rows, gate_rows = [], []
    half = k // 2
    for h in range(PEER_HEADS):
        tops = []
        for p in range(2):
            c0 = (h * 2 + p) * PEER_HALF
            s = lax.dot_general(sk_ref[h, p].astype(wq_ref.dtype), q[:, c0:c0 + PEER_HALF].astype(wq_ref.dtype),
                                nt, precision=prec, preferred_element_type=F32)
            tops.append(_topk_rows(s, k))
        (s0, i0), (s1, i1) = tops
        cs = [s0[0:1] + s1] + [s0[i:i + 1] + s1[0:half] for i in range(1, half)] + [s0[half:] + s1[0:1]]
        best_s, pos = _topk_rows(jnp.concatenate(cs, axis=0), k)
        mid = jnp.floor((pos - k) * (1.0 / half))
        end_mid = float(k + (half - 1) * half)
        i_rank = jnp.where(pos < k, 0.0, jnp.where(pos < end_mid, 1.0 + mid, pos - (end_mid - half)))
        j_rank = jnp.where(pos < k, pos, jnp.where(pos < end_mid, (pos - k) - half * mid, 0.0))
        ids = [_take_rows(i0, i_rank[n:n + 1]) * PEER_NKEYS + _take_rows(i1, j_rank[n:n + 1]) for n in range(k)]
        e = jnp.exp(best_s - best_s[0:1])
        gate_rows.append(e / jnp.sum(e, axis=0, keepdims=True))
        idx_rows.append(jnp.concatenate(ids, axis=0).astype(jnp.int32))
    idx_ref[...] = jnp.concatenate(idx_rows, axis=0).T
    gate_ref[...] = jnp.concatenate(gate_rows, axis=0).T


def peer_route(xn2d, peer_wq, peer_subkeys, *, tt=256, prec=None, wdtype=BF16):
    t, dx = xn2d.shape
    d, nq = peer_wq.shape
    n_sel = PEER_HEADS * PEER_TOPK
    return pl.pallas_call(
        functools.partial(_peer_route_kernel, prec=prec),
        grid=(t // tt,),
        in_specs=[
            pl.BlockSpec((tt, dx), lambda i: (i, 0)),
            pl.BlockSpec((d, nq), lambda i: (0, 0)),
            pl.BlockSpec((PEER_HEADS, 2, PEER_NKEYS, PEER_HALF), lambda i: (0, 0, 0, 0)),
        ],
        out_specs=[pl.BlockSpec((tt, n_sel), lambda i: (i, 0))] * 2,
        out_shape=[jax.ShapeDtypeStruct((t, n_sel), jnp.int32), jax.ShapeDtypeStruct((t, n_sel), F32)],
        compiler_params=_cparams(("parallel",)),
        name="peer_route",
    )(xn2d, peer_wq.astype(wdtype), peer_subkeys)


def _final_kernel(h_ref, y_ref, g_ref, *rest):
    o_ref = rest[-1]
    h = h_ref[...] + y_ref[...]
    ms = jnp.mean(h * h, axis=-1, keepdims=True)
    o_ref[...] = h * lax.rsqrt(ms + RMS_EPS) * g_ref[...]


def final_norm(h2d, y2d, g, *, out=None, row0=0, total_rows=None, tm=1024):
    t, d = h2d.shape
    total = t if total_rows is None else total_rows
    r0 = row0 // tm
    spec = pl.BlockSpec((tm, d), lambda i: (i, 0))
    in_specs = [spec, spec, pl.BlockSpec((1, d), lambda i: (0, 0))]
    args = [h2d, y2d, g.reshape(1, d)]
    aliases = {}
    if out is not None:
        in_specs.append(pl.BlockSpec(memory_space=pl.ANY))
        args.append(out)
        aliases = {3: 0}
    return pl.pallas_call(
        _final_kernel,
        grid=(t // tm,),
        in_specs=in_specs,
        out_specs=pl.BlockSpec((tm, d), lambda i: (r0 + i, 0)),
        out_shape=jax.ShapeDtypeStruct((total, d), F32),
        input_output_aliases=aliases,
        compiler_params=_cparams(("parallel",)),
        name="final_norm",
    )(*args)


SC_CORES = 2
SC_SUBCORES = 16
SC_LANES = 16
SC_WORKERS = SC_CORES * SC_SUBCORES
PEER_SEL = PEER_HEADS * PEER_TOPK
PEER_ROWS = 32
PEER_PARTS = PEER_SEL // PEER_ROWS
PEER_NBUF = 4
PEER_GROUP = 32
PEER_BF16_RUN = 4


def _pack_rows(w):
    half = w.shape[1] // 2
    bits = lax.bitcast_convert_type(w.astype(BF16), jnp.uint16).astype(jnp.uint32)
    return lax.bitcast_convert_type(bits[:, :half] | (bits[:, half:] << 16), jnp.int32)


def _unpack_words(w):
    lo = lax.bitcast_convert_type(lax.shift_left(w, jnp.int32(16)), F32)
    hi = lax.bitcast_convert_type(lax.bitwise_and(w, jnp.int32(-65536)), F32)
    return lo, hi


def _packed_dot(a_words, b_words):
    from jax.experimental.pallas import tpu_sc as plsc
    prods = [plsc.bitcast(a, BF16) * plsc.bitcast(b, BF16) for a, b in zip(a_words, b_words)]
    while len(prods) > 1:
        prods = [prods[k] + prods[k + 1] for k in range(0, len(prods), 2)]
    return _unpack_words(plsc.bitcast(prods[0], jnp.int32))


def _sc_mesh():
    from jax.experimental.pallas import tpu_sc as plsc
    return plsc.VectorSubcoreMesh(core_axis_name="c", subcore_axis_name="s",
                                  num_cores=SC_CORES, num_subcores=SC_SUBCORES)


def _sc_loop(n, body, carry):
    from jax.experimental.pallas import tpu_sc as plsc
    return plsc.parallel_loop(0, n, carry=carry)(body)


def _worker_base(tokens_per_worker):
    return (lax.axis_index("s") * SC_CORES + lax.axis_index("c")) * tokens_per_worker


def _gather_compute_loop(table_hbm, idx_v, rows_v, sem, stage_v, out_row, osem, grp, compute):
    n_gathers = PEER_PARTS * grp
    ahead = PEER_NBUF - 1

    def gather(j, b):
        i = j // PEER_PARTS if isinstance(j, int) else lax.shift_right_logical(j, PEER_PARTS.bit_length() - 1)
        h = j % PEER_PARTS if isinstance(j, int) else lax.bitwise_and(j, PEER_PARTS - 1)
        ids = idx_v.at[i, pl.ds(pl.multiple_of(h * PEER_ROWS, PEER_ROWS), PEER_ROWS)]
        return pltpu.make_async_copy(table_hbm.at[ids], rows_v.at[b], sem.at[b])

    def put(i, slot):
        return pltpu.make_async_copy(stage_v.at[slot], out_row(i), osem.at[slot])

    for j in range(ahead):
        gather(j, j).start()

    @pl.loop(0, n_gathers)
    def _(j):
        b = lax.bitwise_and(j, PEER_NBUF - 1)
        h = lax.bitwise_and(j, PEER_PARTS - 1)
        i = lax.shift_right_logical(j, PEER_PARTS.bit_length() - 1)
        slot = lax.bitwise_and(i, 1)

        @pl.when((h == 0) & (i >= 2))
        def _():
            put(i - 2, slot).wait()

        @pl.when(j + ahead < n_gathers)
        def _():
            gather(j + ahead, lax.bitwise_and(j + ahead, PEER_NBUF - 1)).start()

        gather(j, b).wait()
        compute(i, h, b, slot)

        @pl.when(h == PEER_PARTS - 1)
        def _():
            put(i, slot).start()

    put(grp - 2, 0).wait()
    put(grp - 1, 1).wait()


def peer_expert_dots(x_packed, idx, u_packed):
    t, half = x_packed.shape
    n_chunks = half // SC_LANES
    tpw = t // SC_WORKERS
    grp = min(PEER_GROUP, tpw)
    rows_tog = 4

    def body(x_hbm, idx_hbm, u_hbm, out_hbm, idx_v, x_v, rows_v, ps_v, sem, osem):
        base = _worker_base(tpw)

        def compute(i, h, b, slot):
            @pl.loop(0, PEER_ROWS // rows_tog)
            def _(rg):
                r0 = rg * rows_tog
                accs = [[None, None] for _ in range(rows_tog)]
                for c0 in range(0, n_chunks, PEER_BF16_RUN):
                    ats = [pl.ds((c0 + k) * SC_LANES, SC_LANES) for k in range(PEER_BF16_RUN)]
                    xw = [x_v[i, at] for at in ats]
                    for r in range(rows_tog):
                        terms = _packed_dot([rows_v[b, r0 + r, at] for at in ats], xw)
                        for k, term in enumerate(terms):
                            accs[r][k] = term if accs[r][k] is None else accs[r][k] + term
                for r in range(rows_tog):
                    at = pl.ds(pl.multiple_of((h * PEER_ROWS + r0 + r) * SC_LANES, SC_LANES), SC_LANES)
                    ps_v[slot, at] = accs[r][0] + accs[r][1]

        @pl.loop(0, tpw // grp)
        def _(g):
            t0 = base + g * grp
            pltpu.sync_copy(idx_hbm.at[pl.ds(t0, grp)], idx_v)
            pltpu.sync_copy(x_hbm.at[pl.ds(t0, grp)], x_v)
            _gather_compute_loop(u_hbm, idx_v, rows_v, sem, ps_v, lambda i: out_hbm.at[t0 + i], osem, grp, compute)

    return pl.kernel(
        body,
        out_type=jax.ShapeDtypeStruct((t, PEER_SEL * SC_LANES), F32),
        mesh=_sc_mesh(),
        scratch_types=[
            pltpu.VMEM((grp, PEER_SEL), jnp.int32),
            pltpu.VMEM((grp, half), jnp.int32),
            pltpu.VMEM((PEER_NBUF, PEER_ROWS, half), jnp.int32),
            pltpu.VMEM((2, PEER_SEL * SC_LANES), F32),
            pltpu.SemaphoreType.DMA((PEER_NBUF,)),
            pltpu.SemaphoreType.DMA((2,)),
        ],
        compiler_params=pltpu.CompilerParams(needs_layout_passes=False),
        name="peer_expert_dots",
    )(x_packed, idx, u_packed)


def peer_expert_mix(hgw, idx, v_packed):
    t = hgw.shape[0]
    half = v_packed.shape[1]
    d = 2 * half
    tpw = t // SC_WORKERS
    grp = min(PEER_GROUP, tpw)
    n_parts = 2
    cpp = half // SC_LANES // n_parts
    from jax.experimental.pallas import tpu_sc as plsc

    def body(hg_hbm, idx_hbm, v_hbm, out_hbm, idx_v, hg_v, rows_v, o_v2, sem, osem):
        base = _worker_base(tpw)

        def compute(i, h, b, slot):
            token = jnp.full((SC_LANES,), i, jnp.int32)
            for part in range(n_parts):
                def rbody(rq, accs):
                    r0 = rq * PEER_BF16_RUN
                    s = [plsc.load_gather(hg_v, [token, jnp.full((SC_LANES,), h * PEER_ROWS + r0 + k, jnp.int32)])
                         for k in range(PEER_BF16_RUN)]
                    new = []
                    for c in range(cpp):
                        at = pl.ds((part * cpp + c) * SC_LANES, SC_LANES)
                        lo, hi = _packed_dot([rows_v[b, r0 + k, at] for k in range(PEER_BF16_RUN)], s)
                        new.append(accs[2 * c] + lo)
                        new.append(accs[2 * c + 1] + hi)
                    return tuple(new)

                accs = _sc_loop(PEER_ROWS // PEER_BF16_RUN, rbody,
                                tuple(jnp.zeros((SC_LANES,), F32) for _ in range(2 * cpp)))
                def store(overwrite):
                    for c in range(cpp):
                        lo_at = pl.ds((part * cpp + c) * SC_LANES, SC_LANES)
                        hi_at = pl.ds(half + (part * cpp + c) * SC_LANES, SC_LANES)
                        if overwrite:
                            o_v2[slot, lo_at] = accs[2 * c]
                            o_v2[slot, hi_at] = accs[2 * c + 1]
                        else:
                            o_v2[slot, lo_at] = o_v2[slot, lo_at] + accs[2 * c]
                            o_v2[slot, hi_at] = o_v2[slot, hi_at] + accs[2 * c + 1]

                pl.when(h == 0)(functools.partial(store, True))
                pl.when(h != 0)(functools.partial(store, False))

        @pl.loop(0, tpw // grp)
        def _(g):
            t0 = base + g * grp
            pltpu.sync_copy(idx_hbm.at[pl.ds(t0, grp)], idx_v)
            pltpu.sync_copy(hg_hbm.at[pl.ds(t0, grp)], hg_v)
            _gather_compute_loop(v_hbm, idx_v, rows_v, sem, o_v2, lambda i: out_hbm.at[t0 + i], osem, grp, compute)

    return pl.kernel(
        body,
        out_type=jax.ShapeDtypeStruct((t, d), F32),
        mesh=_sc_mesh(),
        scratch_types=[
            pltpu.VMEM((grp, PEER_SEL), jnp.int32),
            pltpu.VMEM((grp, PEER_SEL), jnp.int32),
            pltpu.VMEM((PEER_NBUF, PEER_ROWS, half), jnp.int32),
            pltpu.VMEM((2, d), F32),
            pltpu.SemaphoreType.DMA((PEER_NBUF,)),
            pltpu.SemaphoreType.DMA((2,)),
        ],
        compiler_params=pltpu.CompilerParams(needs_layout_passes=False),
        name="peer_expert_mix",
    )(hgw, idx, v_packed)


def _peer_act_kernel(ps_ref, gate_ref, sum_ref, o_ref):
    ps = ps_ref[...]
    sel = sum_ref[...]
    hi = ps.astype(BF16)
    rest = ps - hi.astype(F32)
    mid = rest.astype(BF16)
    lo = (rest - mid.astype(F32)).astype(BF16)
    pre = (jnp.dot(hi, sel, preferred_element_type=F32) + jnp.dot(mid, sel, preferred_element_type=F32)
           + jnp.dot(lo, sel, preferred_element_type=F32))
    hg = 0.5 * pre * (1.0 + lax.erf(pre * (1.0 / math.sqrt(2.0)))) * gate_ref[...]
    bits = lax.bitcast_convert_type(hg.astype(BF16).astype(F32), jnp.int32)
    o_ref[...] = lax.bitwise_or(bits, lax.shift_right_logical(bits, jnp.int32(16)))


def peer_act(ps, gates, *, tm=512):
    t, n = ps.shape
    lane_sum = (jnp.arange(n)[:, None] // SC_LANES == jnp.arange(PEER_SEL)[None, :]).astype(BF16)
    return pl.pallas_call(
        _peer_act_kernel,
        grid=(t // tm,),
        in_specs=[
            pl.BlockSpec((tm, n), lambda i: (i, 0)),
            pl.BlockSpec((tm, PEER_SEL), lambda i: (i, 0)),
            pl.BlockSpec((n, PEER_SEL), lambda i: (0, 0)),
        ],
        out_specs=pl.BlockSpec((tm, PEER_SEL), lambda i: (i, 0)),
        out_shape=jax.ShapeDtypeStruct((t, PEER_SEL), jnp.int32),
        compiler_params=_cparams(("parallel",)),
        name="peer_act",
    )(ps, gates, lane_sum)


BATCH_GROUPS = 8


def kernel(x, norm1_g, w_in, rwkv_mu, w0, w_lora_up, a0, a_lora_up, g_lora_up, k_k, k_a, r_k, lnx_g, lnx_b,
           w_proj_a, w_proj_b, w_out, norm2_g, peer_wq, peer_subkeys, peer_u, peer_v, rel_bias, normf_g):
    bsz, seq, d = x.shape
    depth = norm1_g.shape[0]
    groups = BATCH_GROUPS if bsz % BATCH_GROUPS == 0 else 1
    gb = bsz // groups
    tg = gb * seq
    t = bsz * seq
    src = x.reshape(t, d)
    for l in range(depth):
        w_pad = jnp.concatenate([
            w_in[l][:, :COL_A + COL_B_RAW],
            jnp.zeros((d, COL_B - COL_B_RAW), w_in.dtype),
            w_in[l][:, COL_A + COL_B_RAW:]], axis=1).astype(BF16)
        u_packed = _pack_rows(peer_u[l])
        tables = {}
        last = l == depth - 1

        def mix(pending, tie=None):
            row0, h2d, ps, gates, idx = pending
            hgx = peer_act(ps, gates)
            if "v" not in tables:
                v_src = peer_v[l]
                if tie is not None:
                    tie, v_src = lax.optimization_barrier((tie, v_src))
                tables["v"] = _pack_rows(v_src)
            if tie is not None:
                tie, hgx = lax.optimization_barrier((tie, hgx))
            return tie, (row0, h2d, peer_expert_mix(hgx, idx, tables["v"]))

        outs = []

        def close(mixed):
            row0, h2d, y2d = mixed
            if last:
                outs.append(final_norm(h2d, y2d, normf_g, out=outs[-1] if outs else None, row0=row0, total_rows=t))
            else:
                outs.append(h2d + y2d)

        halves = gb == 1 and seq % (2 * MOBA_BLOCK) == 0 and (seq // 2) % (SC_WORKERS * PEER_GROUP) == 0

        pending = closing = None
        for g in range(groups):
            p2d = norm_proj(src, norm1_g[l], w_pad, row0=g * tg, rows=tg)
            p3d = p2d.reshape(gb, seq, -1)
            prep = state = None
            for s0, sn in ([(0, seq // 2), (seq // 2, seq // 2)] if halves and g == 0 else [(0, seq)]):
                oa = moba_attention(p3d, rel_bias, q0=s0 // MOBA_BLOCK, nq=sn // MOBA_BLOCK)
                mixed = None
                if pending is not None:
                    oa, mixed = mix(pending, oa)
                if closing is not None:
                    oa, y2d = lax.optimization_barrier((oa, closing[2]))
                    close(closing[:2] + (y2d,))
                    closing = None
                if prep is None:
                    prep = rwkv_prep(p3d, rwkv_mu[l], w0[l], w_lora_up[l], a0[l], a_lora_up[l], g_lora_up[l],
                                     k_k[l], k_a[l], r_k[l])
                ob, state = rwkv_scan(*prep, lnx_g[l], lnx_b[l], state=state,
                                      c0=s0 // RWKV_CHUNK, nc=sn // RWKV_CHUNK)
                nt = gb * sn
                h2d, xn2 = merge_out(src, oa.reshape(nt, WIDTH), ob.reshape(nt, WIDTH), p2d, w_proj_a[l], w_proj_b[l],
                                     w_out[l], norm2_g[l], row0=g * tg + s0, prow0=s0)
                idx, gates = peer_route(xn2, peer_wq[l], peer_subkeys[l])
                if mixed is not None:
                    idx, y2d = lax.optimization_barrier((idx, mixed[2]))
                    closing = mixed[:2] + (y2d,)
                pending = (g * tg + s0, h2d, peer_expert_dots(xn2, idx, u_packed), gates, idx)
        if closing is not None:
            close(closing)
        close(mix(pending)[1])
        src = outs[-1] if last else jnp.concatenate(outs, axis=0)
    return src.reshape(bsz, seq, d)
```

```python
import functools
import math

import jax
import jax.numpy as jnp
from jax import lax
from jax.experimental import pallas as pl
from jax.experimental.pallas import tpu as pltpu

F32 = jnp.float32
BF16 = jnp.bfloat16
HI = lax.Precision.HIGHEST

LANES = 128
HEAD_DIM = 64
HEADS = 8
PAIRS = HEADS // 2
WIDTH = HEADS * HEAD_DIM
MOBA_BLOCK = 256
MOBA_TOPK = 3
MOBA_LO = 64
REL_BUCKETS = 32
REL_MAX_DIST = 128
DECAY_LORA = 64
AAA_LORA = 64
GATE_LORA = 160
GN_EPS = 64e-5
RMS_EPS = 1e-6
NEG = -1e30
RWKV_CHUNK = 64
COL_A = 3 * WIDTH
COL_B_RAW = 3 * WIDTH + DECAY_LORA + AAA_LORA + GATE_LORA
COL_B = 4 * WIDTH
COL_G_OFF = COL_A + COL_B
VMEM_LIMIT = 56 * 1024 * 1024


def _cparams(sem):
    return pltpu.CompilerParams(dimension_semantics=sem, vmem_limit_bytes=VMEM_LIMIT)


def _norm_proj_kernel(x_ref, g_ref, w_ref, o_ref, xn_ref):
    @pl.when(pl.program_id(1) == 0)
    def _():
        x = x_ref[...]
        ms = jnp.mean(x * x, axis=-1, keepdims=True)
        xn_ref[...] = (x * lax.rsqrt(ms + RMS_EPS) * g_ref[...]).astype(xn_ref.dtype)

    o_ref[...] = jnp.dot(xn_ref[...], w_ref[...], preferred_element_type=F32).astype(o_ref.dtype)


def norm_proj(x2d, g, w, *, row0=0, rows=None, tm=512, tn=512, out_dtype=F32):
    d = x2d.shape[1]
    t = x2d.shape[0] if rows is None else rows
    n = w.shape[1]
    r0 = row0 // tm
    return pl.pallas_call(
        _norm_proj_kernel,
        grid=(t // tm, n // tn),
        in_specs=[
            pl.BlockSpec((tm, d), lambda i, j: (r0 + i, 0)),
            pl.BlockSpec((1, d), lambda i, j: (0, 0)),
            pl.BlockSpec((d, tn), lambda i, j: (0, j)),
        ],
        out_specs=pl.BlockSpec((tm, tn), lambda i, j: (i, j)),
        out_shape=jax.ShapeDtypeStruct((t, n), out_dtype),
        scratch_shapes=[pltpu.VMEM((tm, d), w.dtype)],
        compiler_params=_cparams(("parallel", "arbitrary")),
        name="norm_proj",
    )(x2d, g.reshape(1, d), w)


def _rel_bucket(dist):
    n = jnp.maximum(dist, 0)
    max_exact = REL_BUCKETS // 2
    nf = jnp.maximum(n, 1).astype(F32)
    large = max_exact + (jnp.log(nf / max_exact) / math.log(REL_MAX_DIST / max_exact)
                         * (REL_BUCKETS - max_exact)).astype(jnp.int32)
    large = jnp.minimum(large, REL_BUCKETS - 1)
    return jnp.where(n < max_exact, n, large)


def _moba_kernel(q_ref, k_ref, v_ref, bown_ref, bprev_ref, bfar_ref, o_ref,
                 kb_ref, vb_ref, kbar_ref, *, n_blocks, q0):
    qb = pl.program_id(2) + q0
    blk = MOBA_BLOCK
    scale = 1.0 / math.sqrt(HEAD_DIM)

    rows2 = 2 * blk
    nt = (((1,), (1,)), ((), ()))

    @pl.when(pl.program_id(2) == 0)
    def _():
        kbar_ref[...] = jnp.zeros_like(kbar_ref)
        lane_b = lax.broadcasted_iota(jnp.int32, (blk, LANES), 1)
        for n in range(n_blocks):
            kblk = k_ref[0, n * blk:(n + 1) * blk, :]
            kbar_ref[n:n + 1, :] = jnp.mean(kblk, axis=0, keepdims=True)
            kb_ref[n * blk:(n + 1) * blk, 0:LANES] = kblk.astype(BF16)
            kb_ref[n * blk:(n + 1) * blk, LANES:] = ((lane_b == n) | (lane_b == MOBA_LO + n)).astype(BF16)
        vb_ref[...] = v_ref[0].astype(BF16)

    q2 = q_ref[0]
    first = lax.broadcasted_iota(jnp.int32, (blk, LANES), 1) < HEAD_DIM
    qh = jnp.concatenate([jnp.where(first, q2, 0.0), jnp.where(first, 0.0, q2)], axis=0)
    lane = lax.broadcasted_iota(jnp.int32, (rows2, LANES), 1)
    rowi = lax.broadcasted_iota(jnp.int32, (rows2, LANES), 0)
    gate = lax.dot_general(qh.astype(BF16), kbar_ref[...].astype(BF16), nt, preferred_element_type=F32)
    g = jnp.where(lane < qb, gate, -jnp.inf)
    chosen = lane < 0
    lane_f = lane.astype(F32)
    for _ in range(MOBA_TOPK):
        m = jnp.max(g, axis=1, keepdims=True)
        idx = jnp.min(jnp.where(g == m, lane_f, float(LANES)), axis=1, keepdims=True)
        hit = (lane_f == idx) & (m > -jnp.inf)
        chosen = chosen | hit
        g = jnp.where(hit, -jnp.inf, g)
    nfar = qb - 1
    bfar = jnp.where(rowi < blk, bfar_ref[0, 0:1, 0:1], bfar_ref[1, 0:1, 0:1])
    bhi = bfar.astype(BF16).astype(F32)
    madd = jnp.where(lane < nfar, jnp.where(chosen, bhi, NEG),
                     jnp.where(lane == nfar, jnp.where(chosen, 0.0, NEG),
                               jnp.where((lane >= MOBA_LO) & (lane - MOBA_LO < nfar), bfar - bhi, 0.0)))
    q_aug = jnp.concatenate([(qh * scale).astype(BF16), madd.astype(BF16)], axis=1)

    prev0 = pl.multiple_of(jnp.maximum(nfar, 0) * blk, blk)
    own0 = pl.multiple_of(qb * blk, blk)
    s_prev = (lax.dot_general(q_aug, kb_ref[pl.ds(prev0, blk), :], nt, preferred_element_type=F32)
              + bprev_ref[...].reshape(rows2, blk) + jnp.where(qb > 0, 0.0, NEG))
    s_own = (lax.dot_general(q_aug, kb_ref[pl.ds(own0, blk), :], nt, preferred_element_type=F32)
             + bown_ref[...].reshape(rows2, blk))
    r = lax.broadcasted_iota(jnp.int32, (rows2, blk), 0)
    c = lax.broadcasted_iota(jnp.int32, (rows2, blk), 1)
    s_own = jnp.where(lax.bitwise_and(r, blk - 1) >= c, s_own, NEG)
    s = jnp.concatenate([s_prev, s_own], axis=1)
    m_i = jnp.max(s, axis=1, keepdims=True)
    p = jnp.exp(s - m_i)
    l_i = jnp.sum(p, axis=1, keepdims=True)
    v0 = jnp.concatenate([vb_ref[pl.ds(prev0, blk), :], vb_ref[pl.ds(own0, blk), :]], axis=0)
    acc = jnp.dot(p.astype(BF16), v0, preferred_element_type=F32)

    def body(it, carry):
        m_i, l_i, acc = carry
        k0 = pl.multiple_of(it * rows2, rows2)
        s = lax.dot_general(q_aug, kb_ref[pl.ds(k0, rows2), :], nt, preferred_element_type=F32)
        tail = jnp.where(2 * it + 1 < nfar, 0.0, NEG)
        s = jnp.concatenate([s[:, :blk], s[:, blk:] + tail], axis=1)
        m_new = jnp.maximum(m_i, jnp.max(s, axis=1, keepdims=True))
        alpha = jnp.exp(m_i - m_new)
        p = jnp.exp(s - m_new)
        l_new = alpha * l_i + jnp.sum(p, axis=1, keepdims=True)
        acc_new = alpha * acc + jnp.dot(p.astype(BF16), vb_ref[pl.ds(k0, rows2), :], preferred_element_type=F32)
        return m_new, l_new, acc_new

    m_i, l_i, acc = lax.fori_loop(0, (jnp.maximum(nfar, 0) + 1) // 2, body, (m_i, l_i, acc))
    out = acc / l_i
    o_ref[0] = jnp.where(first, out[:blk], out[blk:])


def moba_attention(p3d, rel_bias, *, q0=0, nq=None):
    bsz, seq, _ = p3d.shape
    blk = MOBA_BLOCK
    n_blocks = seq // blk
    nq = n_blocks - q0 if nq is None else nq
    assert n_blocks <= MOBA_LO and seq % blk == 0
    span = 2 * blk
    by_dist = rel_bias[:, _rel_bucket(jnp.arange(span))].astype(F32)
    shift = jnp.arange(span)

    def toeplitz(c):
        k = jnp.where(shift < blk, shift, shift - span)
        s = by_dist[:, jnp.clip(c - k, 0, span - 1)]
        tiled = jnp.tile(s, (1, blk))[:, :blk * (span - 1)]
        return tiled.reshape(HEADS, blk, span - 1)[:, :, :blk]

    bias_own = toeplitz(0)
    bias_prev = toeplitz(blk)
    bias_far = jnp.broadcast_to(rel_bias[:, REL_BUCKETS - 1].astype(F32)[:, None, None], (HEADS, 8, LANES))
    kern = functools.partial(_moba_kernel, n_blocks=n_blocks, q0=q0)
    return pl.pallas_call(
        kern,
        grid=(bsz, PAIRS, nq),
        in_specs=[
            pl.BlockSpec((1, blk, LANES), lambda b, h, i: (b, q0 + i, h)),
            pl.BlockSpec((1, seq, LANES), lambda b, h, i: (b, 0, PAIRS + h)),
            pl.BlockSpec((1, seq, LANES), lambda b, h, i: (b, 0, 2 * PAIRS + h)),
            pl.BlockSpec((2, blk, blk), lambda b, h, i: (h, 0, 0)),
            pl.BlockSpec((2, blk, blk), lambda b, h, i: (h, 0, 0)),
            pl.BlockSpec((2, 8, LANES), lambda b, h, i: (h, 0, 0)),
        ],
        out_specs=pl.BlockSpec((1, blk, LANES), lambda b, h, i: (b, i, h)),
        out_shape=jax.ShapeDtypeStruct((bsz, nq * blk, WIDTH), F32),
        scratch_shapes=[
            pltpu.VMEM((seq, 2 * LANES), BF16),
            pltpu.VMEM((seq, LANES), BF16),
            pltpu.VMEM((LANES, LANES), F32),
        ],
        compiler_params=_cparams(("parallel", "parallel", "arbitrary")),
        name="moba",
    )(p3d, p3d, p3d, bias_own, bias_prev, bias_far)


def _shifted(x, carry_row):
    rows = lax.broadcasted_iota(jnp.int32, x.shape, 0)
    return jnp.where(rows == 0, carry_row, pltpu.roll(x, 1, axis=0))


def _rwkv_prep_kernel(pr_ref, pk_ref, pv_ref, pl_ref, mu_ref, vec_ref, ww_ref, wa_ref, wg_ref,
                      bd_ref, tri_ref,
                      rt_ref, kt_ref, kd_ref, bd_out_ref, v_ref, g_ref, bonus_ref, pend_ref,
                      carry_ref, *, chunk):
    @pl.when(pl.program_id(1) == 0)
    def _():
        carry_ref[...] = jnp.zeros_like(carry_ref)

    def mix(ref, j):
        x = ref[0]
        mu = mu_ref[0:1, j * WIDTH:(j + 1) * WIDTH]
        prev = _shifted(x, carry_ref[0:1, j * WIDTH:(j + 1) * WIDTH])
        carry_ref[0:1, j * WIDTH:(j + 1) * WIDTH] = x[x.shape[0] - 1:, :]
        return x + mu * (prev - x)

    r = mix(pr_ref, 0)
    k = mix(pk_ref, 1)
    v = mix(pv_ref, 2)
    lo = mix(pl_ref, 3)
    w0, a0, k_k, k_a, r_k = (vec_ref[i:i + 1, :] for i in range(5))
    xwa = lo[:, 0:LANES]
    xg = lo[:, LANES:3 * LANES]
    lw = jnp.dot(jnp.tanh(xwa), ww_ref[...], precision=HI, preferred_element_type=F32)
    la = jnp.dot(xwa, wa_ref[...], precision=HI, preferred_element_type=F32)
    g = jnp.dot(jax.nn.sigmoid(xg), wg_ref[...], precision=HI, preferred_element_type=F32)
    z = -(w0 + lw)
    softplus = jnp.maximum(z, 0.0) + jnp.log(1.0 + jnp.exp(-jnp.abs(z)))
    logw = -jnp.exp(-softplus - 0.5)
    a = jax.nn.sigmoid(a0 + la)
    kk = k * k_k
    ss = jnp.dot(kk * kk, bd_ref[...], precision=HI, preferred_element_type=F32)
    kk = kk / jnp.maximum(jnp.sqrt(ss), 1e-12)
    k2 = k * (1.0 + (a - 1.0) * k_a)
    rk = jnp.dot(r * k2 * r_k, bd_ref[...], precision=HI, preferred_element_type=F32)
    cs = jnp.dot(tri_ref[...], logw, precision=HI, preferred_element_type=F32)
    e_pos = jnp.exp(cs)
    e_neg = jnp.exp(-cs)
    rt_ref[0] = r * e_pos
    kt_ref[0] = kk * jnp.exp(cs - logw)
    kd_ref[0] = k2 * e_neg
    bd_out_ref[0] = kk * a * e_neg
    v_ref[0] = v
    g_ref[0] = g
    bonus_ref[0] = rk * v
    ts = e_pos.shape[0]
    for c in range(ts // chunk):
        pend_ref[0, c:c + 1, :] = e_pos[(c + 1) * chunk - 1:(c + 1) * chunk, :]


def rwkv_prep(p3d, rwkv_mu, w0, w_lora_up, a0, a_lora_up, g_lora_up, k_k, k_a, r_k, *, ts=512):
    bsz, seq, _ = p3d.shape
    chunk = RWKV_CHUNK
    ts = min(ts, seq)
    mu = jnp.pad(rwkv_mu, (0, COL_B - COL_B_RAW)).reshape(1, COL_B)
    vec = jnp.stack([w0, a0, k_k, k_a, r_k.reshape(-1)] + [jnp.zeros_like(w0)] * 3).astype(F32)
    ww = jnp.zeros((LANES, WIDTH), F32).at[:DECAY_LORA].set(w_lora_up)
    wa = jnp.zeros((LANES, WIDTH), F32).at[DECAY_LORA:DECAY_LORA + AAA_LORA].set(a_lora_up)
    wg = jnp.zeros((2 * LANES, WIDTH), F32).at[:GATE_LORA].set(g_lora_up)
    hid = jnp.arange(WIDTH) // HEAD_DIM
    bd = (hid[:, None] == hid[None, :]).astype(F32)
    tix = jnp.arange(ts)
    tri = ((tix[:, None] // chunk == tix[None, :] // chunk) & (tix[None, :] <= tix[:, None])).astype(F32)
    c0 = COL_A // WIDTH
    big = jax.ShapeDtypeStruct((bsz, seq, WIDTH), F32)
    wspec = lambda shape: pl.BlockSpec(shape, lambda b, i: (0, 0))
    ospec = pl.BlockSpec((1, ts, WIDTH), lambda b, i: (b, i, 0))
    return pl.pallas_call(
        functools.partial(_rwkv_prep_kernel, chunk=chunk),
        grid=(bsz, seq // ts),
        in_specs=[
            pl.BlockSpec((1, ts, WIDTH), lambda b, i: (b, i, c0)),
            pl.BlockSpec((1, ts, WIDTH), lambda b, i: (b, i, c0 + 1)),
            pl.BlockSpec((1, ts, WIDTH), lambda b, i: (b, i, c0 + 2)),
            pl.BlockSpec((1, ts, WIDTH), lambda b, i: (b, i, c0 + 3)),
            wspec((1, COL_B)), wspec((8, WIDTH)), wspec((LANES, WIDTH)), wspec((LANES, WIDTH)),
            wspec((2 * LANES, WIDTH)), wspec((WIDTH, WIDTH)), wspec((ts, ts)),
        ],
        out_specs=[ospec] * 7 + [pl.BlockSpec((1, ts // chunk, WIDTH), lambda b, i: (b, i, 0))],
        out_shape=[big] * 7 + [jax.ShapeDtypeStruct((bsz, seq // chunk, WIDTH), F32)],
        scratch_shapes=[pltpu.VMEM((8, COL_B), F32)],
        compiler_params=_cparams(("parallel", "arbitrary")),
        name="rwkv_prep",
    )(p3d, p3d, p3d, p3d, mu, vec, ww, wa, wg, bd, tri)


def _rwkv_scan_kernel(rt_ref, kt_ref, kd_ref, bd_ref, v_ref, g_ref, bonus_ref, pend_ref, ln_ref, sin_ref,
                      o_ref, state_ref, *, chunk, prec):
    @pl.when(pl.program_id(1) == 0)
    def _():
        state_ref[...] = sin_ref[...]

    c2 = 2 * chunk
    lane = lax.broadcasted_iota(jnp.int32, (chunk, LANES), 1)
    first = lane < HEAD_DIM
    row = lax.broadcasted_iota(jnp.int32, (c2, c2), 0)
    col = lax.broadcasted_iota(jnp.int32, (c2, c2), 1)
    eye = (row == col).astype(F32)
    hrow = lax.broadcasted_iota(jnp.int32, (LANES, LANES), 0) // HEAD_DIM
    hcol = lax.broadcasted_iota(jnp.int32, (LANES, LANES), 1) // HEAD_DIM
    head_mean = jnp.where(hrow == hcol, 1.0 / HEAD_DIM, 0.0).astype(F32)
    nt = (((1,), (1,)), ((), ()))
    tn = (((0,), (0,)), ((), ()))
    dot = functools.partial(jnp.dot, precision=prec, preferred_element_type=F32)
    dotg = functools.partial(lax.dot_general, precision=prec, preferred_element_type=F32)

    def stack(x):
        return jnp.concatenate([jnp.where(first, x, 0.0), jnp.where(first, 0.0, x)], axis=0)

    pairs = range(PAIRS)
    sls = [slice(hp * LANES, (hp + 1) * LANES) for hp in pairs]
    rs, ks, kds, bs, vs = ([stack(ref[0, :, sl]) for sl in sls] for ref in (rt_ref, kt_ref, kd_ref, bd_ref, v_ref))
    hts = [state_ref[0, hp] for hp in pairs]
    big = [dotg(jnp.concatenate([ks[hp], rs[hp]], axis=0), jnp.concatenate([bs[hp], kds[hp]], axis=0), nt)
           for hp in pairs]
    a_b = [jnp.where(row > col, big[hp][0:c2, 0:c2], 0.0) for hp in pairs]
    a_k = [jnp.where(row > col, big[hp][0:c2, c2:], 0.0) for hp in pairs]
    a_rb = [jnp.where(row >= col, big[hp][c2:, 0:c2], 0.0) for hp in pairs]
    a_rk = [jnp.where(row >= col, big[hp][c2:, c2:], 0.0) for hp in pairs]
    kh = [dotg(jnp.concatenate([ks[hp], rs[hp]], axis=0), hts[hp], nt) for hp in pairs]
    av = [dot(jnp.concatenate([a_k[hp], a_rk[hp]], axis=0), vs[hp]) for hp in pairs]
    vk = [dotg(vs[hp], kds[hp], tn) for hp in pairs]
    inv = [eye - a_b[hp] for hp in pairs]
    pw = [dot(a_b[hp], a_b[hp]) for hp in pairs]
    n_sq = int(math.log2(chunk)) - 1
    for lvl in range(n_sq):
        if lvl + 1 < n_sq:
            both = [dot(jnp.concatenate([inv[hp], pw[hp]], axis=0), pw[hp]) for hp in pairs]
            inv = [inv[hp] + both[hp][0:c2] for hp in pairs]
            pw = [both[hp][c2:] for hp in pairs]
        else:
            inv = [inv[hp] + dot(inv[hp], pw[hp]) for hp in pairs]
    us = [dot(inv[hp], kh[hp][0:c2] + av[hp][0:c2]) for hp in pairs]
    ub = [dotg(us[hp], bs[hp], tn) for hp in pairs]
    au = [dot(a_rb[hp], us[hp]) for hp in pairs]
    for hp in pairs:
        sl = sls[hp]
        pend = pend_ref[0, 0, 0:1, sl]
        state_ref[0, hp] = (hts[hp] + vk[hp] - ub[hp]) * pend
        os_ = kh[hp][c2:] + av[hp][c2:] - au[hp]
        o = os_[0:chunk] + os_[chunk:]
        mu = jnp.dot(o, head_mean, precision=HI, preferred_element_type=F32)
        d = o - mu
        var = jnp.dot(d * d, head_mean, precision=HI, preferred_element_type=F32)
        on = d * lax.rsqrt(var + GN_EPS) * ln_ref[0:1, sl] + ln_ref[1:2, sl]
        o_ref[0, :, sl] = (on + bonus_ref[0, :, sl]) * g_ref[0, :, sl]


def rwkv_scan(rt, kt, kd, bd, v, g, bonus, pend, lnx_g, lnx_b, *, state=None, c0=0, nc=None, prec=None):
    bsz, seq, _ = rt.shape
    chunk = RWKV_CHUNK
    n_chunks = seq // chunk
    nc = n_chunks - c0 if nc is None else nc
    ln = jnp.stack([lnx_g, lnx_b] + [jnp.zeros_like(lnx_g)] * 6).astype(F32)
    pend4 = pend.reshape(bsz, n_chunks, 1, WIDTH)
    if state is None:
        state = jnp.zeros((bsz, PAIRS, LANES, LANES), F32)
    spec = pl.BlockSpec((1, chunk, WIDTH), lambda b, c: (b, c0 + c, 0))
    sspec = pl.BlockSpec((1, PAIRS, LANES, LANES), lambda b, c: (b, 0, 0, 0))
    return pl.pallas_call(
        functools.partial(_rwkv_scan_kernel, chunk=chunk, prec=prec),
        grid=(bsz, nc),
        in_specs=[spec] * 7 + [
            pl.BlockSpec((1, 1, 1, WIDTH), lambda b, c: (b, c0 + c, 0, 0)),
            pl.BlockSpec((8, WIDTH), lambda b, c: (0, 0)),
            sspec,
        ],
        out_specs=[pl.BlockSpec((1, chunk, WIDTH), lambda b, c: (b, c, 0)), sspec],
        out_shape=[jax.ShapeDtypeStruct((bsz, nc * chunk, WIDTH), F32),
                   jax.ShapeDtypeStruct((bsz, PAIRS, LANES, LANES), F32)],
        compiler_params=_cparams(("parallel", "arbitrary")),
        name="rwkv_scan",
    )(rt, kt, kd, bd, v, g, bonus, pend4, ln, state)


def _merge_kernel(x_ref, oa_ref, ob_ref, ga_ref, gb_ref, wa_ref, wb_ref, wo_ref, g2_ref,
                  h_ref, xn_ref, acc_ref):
    j = pl.program_id(1)

    @pl.when(j == 0)
    def _():
        acc_ref[...] = x_ref[...]

    ya = jnp.dot(oa_ref[...].astype(BF16), wa_ref[...], preferred_element_type=F32)
    yb = jnp.dot(ob_ref[...].astype(BF16), wb_ref[...], preferred_element_type=F32)
    y = jax.nn.sigmoid(ga_ref[...]) * ya + jax.nn.sigmoid(gb_ref[...]) * yb
    acc_ref[...] += jnp.dot(y.astype(BF16), wo_ref[...], preferred_element_type=F32)

    @pl.when(j == pl.num_programs(1) - 1)
    def _():
        h = acc_ref[...]
        h_ref[...] = h
        ms = jnp.mean(h * h, axis=-1, keepdims=True)
        xn_ref[...] = _pack_halves(h * lax.rsqrt(ms + RMS_EPS) * g2_ref[...])


def _pack_halves(x):
    half = x.shape[1] // 2
    lo = lax.bitcast_convert_type(x[:, :half].astype(BF16).astype(F32), jnp.int32)
    hi = lax.bitcast_convert_type(x[:, half:].astype(BF16).astype(F32), jnp.int32)
    return lax.bitwise_or(lax.shift_right_logical(lo, jnp.int32(16)), hi)


def _unpack_halves(words):
    lo, hi = _unpack_words(words)
    return jnp.concatenate([lo, hi], axis=1)


def merge_out(x2d, oa, ob, p2d, w_proj_a, w_proj_b, w_out, norm2_g, *, row0=0, prow0=0, tm=512):
    t, d = oa.shape[0], x2d.shape[1]
    r0 = row0 // tm
    p0 = prow0 // tm
    tn = WIDTH
    nj = d // tn
    g0 = COL_G_OFF // tn
    return pl.pallas_call(
        _merge_kernel,
        grid=(t // tm, nj),
        in_specs=[
            pl.BlockSpec((tm, d), lambda i, j: (r0 + i, 0)),
            pl.BlockSpec((tm, WIDTH), lambda i, j: (i, 0)),
            pl.BlockSpec((tm, WIDTH), lambda i, j: (i, 0)),
            pl.BlockSpec((tm, tn), lambda i, j: (p0 + i, g0 + j)),
            pl.BlockSpec((tm, tn), lambda i, j: (p0 + i, g0 + nj + j)),
            pl.BlockSpec((WIDTH, tn), lambda i, j: (0, j)),
            pl.BlockSpec((WIDTH, tn), lambda i, j: (0, j)),
            pl.BlockSpec((tn, d), lambda i, j: (j, 0)),
            pl.BlockSpec((1, d), lambda i, j: (0, 0)),
        ],
        out_specs=[pl.BlockSpec((tm, d), lambda i, j: (i, 0)), pl.BlockSpec((tm, d // 2), lambda i, j: (i, 0))],
        out_shape=[jax.ShapeDtypeStruct((t, d), F32), jax.ShapeDtypeStruct((t, d // 2), jnp.int32)],
        scratch_shapes=[pltpu.VMEM((tm, d), F32)],
        compiler_params=_cparams(("parallel", "arbitrary")),
        name="merge_out",
    )(x2d, oa, ob, p2d, p2d, w_proj_a.astype(BF16), w_proj_b.astype(BF16), w_out.astype(BF16),
      norm2_g.reshape(1, d))


PEER_HEADS = 8
PEER_NKEYS = 128
PEER_TOPK = 16
PEER_HALF = 128


def _topk_rows(s, k):
    n = s.shape[0]
    rows = lax.broadcasted_iota(jnp.int32, s.shape, 0).astype(F32)
    vals, ids = [], []
    for _ in range(k):
        m = jnp.max(s, axis=0, keepdims=True)
        first = jnp.min(jnp.where(s == m, rows, float(n)), axis=0, keepdims=True)
        vals.append(m)
        ids.append(first)
        s = jnp.where(rows == first, -jnp.inf, s)
    return jnp.concatenate(vals, axis=0), jnp.concatenate(ids, axis=0)


def _take_rows(table, ids):
    rows = lax.broadcasted_iota(jnp.int32, table.shape, 0).astype(F32)
    return jnp.sum(jnp.where(rows == ids, table, 0.0), axis=0, keepdims=True)


def _peer_route_kernel(xn_ref, wq_ref, sk_ref, idx_ref, gate_ref, *, prec):
    tt = xn_ref.shape[0]
    k = PEER_TOPK
    xn = _unpack_halves(xn_ref[...]) if xn_ref.dtype == jnp.int32 else xn_ref[...]
    q = jnp.dot(xn.astype(wq_ref.dtype), wq_ref[...], precision=prec, preferred_element_type=F32)
    nt = (((1,), (1,)), ((), ()))
    idx_rows, gate_rows = [], []
    half = k // 2
    for h in range(PEER_HEADS):
        tops = []
        for p in range(2):
            c0 = (h * 2 + p) * PEER_HALF
            s = lax.dot_general(sk_ref[h, p].astype(wq_ref.dtype), q[:, c0:c0 + PEER_HALF].astype(wq_ref.dtype),
                                nt, precision=prec, preferred_element_type=F32)
            tops.append(_topk_rows(s, k))
        (s0, i0), (s1, i1) = tops
        cs = [s0[0:1] + s1] + [s0[i:i + 1] + s1[0:half] for i in range(1, half)] + [s0[half:] + s1[0:1]]
        best_s, pos = _topk_rows(jnp.concatenate(cs, axis=0), k)
        mid = jnp.floor((pos - k) * (1.0 / half))
        end_mid = float(k + (half - 1) * half)
        i_rank = jnp.where(pos < k, 0.0, jnp.where(pos < end_mid, 1.0 + mid, pos - (end_mid - half)))
        j_rank = jnp.where(pos < k, pos, jnp.where(pos < end_mid, (pos - k) - half * mid, 0.0))
        ids = [_take_rows(i0, i_rank[n:n + 1]) * PEER_NKEYS + _take_rows(i1, j_rank[n:n + 1]) for n in range(k)]
        e = jnp.exp(best_s - best_s[0:1])
        gate_rows.append(e / jnp.sum(e, axis=0, keepdims=True))
        idx_rows.append(jnp.concatenate(ids, axis=0).astype(jnp.int32))
    idx_ref[...] = jnp.concatenate(idx_rows, axis=0).T
    gate_ref[...] = jnp.concatenate(gate_rows, axis=0).T


def peer_route(xn2d, peer_wq, peer_subkeys, *, tt=256, prec=None, wdtype=BF16):
    t, dx = xn2d.shape
    d, nq = peer_wq.shape
    n_sel = PEER_HEADS * PEER_TOPK
    return pl.pallas_call(
        functools.partial(_peer_route_kernel, prec=prec),
        grid=(t // tt,),
        in_specs=[
            pl.BlockSpec((tt, dx), lambda i: (i, 0)),
            pl.BlockSpec((d, nq), lambda i: (0, 0)),
            pl.BlockSpec((PEER_HEADS, 2, PEER_NKEYS, PEER_HALF), lambda i: (0, 0, 0, 0)),
        ],
        out_specs=[pl.BlockSpec((tt, n_sel), lambda i: (i, 0))] * 2,
        out_shape=[jax.ShapeDtypeStruct((t, n_sel), jnp.int32), jax.ShapeDtypeStruct((t, n_sel), F32)],
        compiler_params=_cparams(("parallel",)),
        name="peer_route",
    )(xn2d, peer_wq.astype(wdtype), peer_subkeys)


def _final_kernel(h_ref, y_ref, g_ref, *rest):
    o_ref = rest[-1]
    h = h_ref[...] + y_ref[...]
    ms = jnp.mean(h * h, axis=-1, keepdims=True)
    o_ref[...] = h * lax.rsqrt(ms + RMS_EPS) * g_ref[...]


def final_norm(h2d, y2d, g, *, out=None, row0=0, total_rows=None, tm=1024):
    t, d = h2d.shape
    total = t if total_rows is None else total_rows
    r0 = row0 // tm
    spec = pl.BlockSpec((tm, d), lambda i: (i, 0))
    in_specs = [spec, spec, pl.BlockSpec((1, d), lambda i: (0, 0))]
    args = [h2d, y2d, g.reshape(1, d)]
    aliases = {}
    if out is not None:
        in_specs.append(pl.BlockSpec(memory_space=pl.ANY))
        args.append(out)
        aliases = {3: 0}
    return pl.pallas_call(
        _final_kernel,
        grid=(t // tm,),
        in_specs=in_specs,
        out_specs=pl.BlockSpec((tm, d), lambda i: (r0 + i, 0)),
        out_shape=jax.ShapeDtypeStruct((total, d), F32),
        input_output_aliases=aliases,
        compiler_params=_cparams(("parallel",)),
        name="final_norm",
    )(*args)


SC_CORES = 2
SC_SUBCORES = 16
SC_LANES = 16
SC_WORKERS = SC_CORES * SC_SUBCORES
PEER_SEL = PEER_HEADS * PEER_TOPK
PEER_ROWS = 32
PEER_PARTS = PEER_SEL // PEER_ROWS
PEER_NBUF = 4
PEER_GROUP = 32
PEER_BF16_RUN = 4


def _pack_rows(w):
    half = w.shape[1] // 2
    bits = lax.bitcast_convert_type(w.astype(BF16), jnp.uint16).astype(jnp.uint32)
    return lax.bitcast_convert_type(bits[:, :half] | (bits[:, half:] << 16), jnp.int32)


def _unpack_words(w):
    lo = lax.bitcast_convert_type(lax.shift_left(w, jnp.int32(16)), F32)
    hi = lax.bitcast_convert_type(lax.bitwise_and(w, jnp.int32(-65536)), F32)
    return lo, hi


def _packed_dot(a_words, b_words):
    from jax.experimental.pallas import tpu_sc as plsc
    prods = [plsc.bitcast(a, BF16) * plsc.bitcast(b, BF16) for a, b in zip(a_words, b_words)]
    while len(prods) > 1:
        prods = [prods[k] + prods[k + 1] for k in range(0, len(prods), 2)]
    return _unpack_words(plsc.bitcast(prods[0], jnp.int32))


def _sc_mesh():
    from jax.experimental.pallas import tpu_sc as plsc
    return plsc.VectorSubcoreMesh(core_axis_name="c", subcore_axis_name="s",
                                  num_cores=SC_CORES, num_subcores=SC_SUBCORES)


def _sc_loop(n, body, carry):
    from jax.experimental.pallas import tpu_sc as plsc
    return plsc.parallel_loop(0, n, carry=carry)(body)


def _worker_base(tokens_per_worker):
    return (lax.axis_index("s") * SC_CORES + lax.axis_index("c")) * tokens_per_worker


def _gather_compute_loop(table_hbm, idx_v, rows_v, sem, stage_v, out_row, osem, grp, compute):
    n_gathers = PEER_PARTS * grp
    ahead = PEER_NBUF - 1

    def gather(j, b):
        i = j // PEER_PARTS if isinstance(j, int) else lax.shift_right_logical(j, PEER_PARTS.bit_length() - 1)
        h = j % PEER_PARTS if isinstance(j, int) else lax.bitwise_and(j, PEER_PARTS - 1)
        ids = idx_v.at[i, pl.ds(pl.multiple_of(h * PEER_ROWS, PEER_ROWS), PEER_ROWS)]
        return pltpu.make_async_copy(table_hbm.at[ids], rows_v.at[b], sem.at[b])

    def put(i, slot):
        return pltpu.make_async_copy(stage_v.at[slot], out_row(i), osem.at[slot])

    for j in range(ahead):
        gather(j, j).start()

    @pl.loop(0, n_gathers)
    def _(j):
        b = lax.bitwise_and(j, PEER_NBUF - 1)
        h = lax.bitwise_and(j, PEER_PARTS - 1)
        i = lax.shift_right_logical(j, PEER_PARTS.bit_length() - 1)
        slot = lax.bitwise_and(i, 1)

        @pl.when((h == 0) & (i >= 2))
        def _():
            put(i - 2, slot).wait()

        @pl.when(j + ahead < n_gathers)
        def _():
            gather(j + ahead, lax.bitwise_and(j + ahead, PEER_NBUF - 1)).start()

        gather(j, b).wait()
        compute(i, h, b, slot)

        @pl.when(h == PEER_PARTS - 1)
        def _():
            put(i, slot).start()

    put(grp - 2, 0).wait()
    put(grp - 1, 1).wait()


def peer_expert_dots(x_packed, idx, u_packed):
    t, half = x_packed.shape
    n_chunks = half // SC_LANES
    tpw = t // SC_WORKERS
    grp = min(PEER_GROUP, tpw)
    rows_tog = 4

    def body(x_hbm, idx_hbm, u_hbm, out_hbm, idx_v, x_v, rows_v, ps_v, sem, osem):
        base = _worker_base(tpw)

        def compute(i, h, b, slot):
            @pl.loop(0, PEER_ROWS // rows_tog)
            def _(rg):
                r0 = rg * rows_tog
                accs = [[None, None] for _ in range(rows_tog)]
                for c0 in range(0, n_chunks, PEER_BF16_RUN):
                    ats = [pl.ds((c0 + k) * SC_LANES, SC_LANES) for k in range(PEER_BF16_RUN)]
                    xw = [x_v[i, at] for at in ats]
                    for r in range(rows_tog):
                        terms = _packed_dot([rows_v[b, r0 + r, at] for at in ats], xw)
                        for k, term in enumerate(terms):
                            accs[r][k] = term if accs[r][k] is None else accs[r][k] + term
                for r in range(rows_tog):
                    at = pl.ds(pl.multiple_of((h * PEER_ROWS + r0 + r) * SC_LANES, SC_LANES), SC_LANES)
                    ps_v[slot, at] = accs[r][0] + accs[r][1]

        @pl.loop(0, tpw // grp)
        def _(g):
            t0 = base + g * grp
            pltpu.sync_copy(idx_hbm.at[pl.ds(t0, grp)], idx_v)
            pltpu.sync_copy(x_hbm.at[pl.ds(t0, grp)], x_v)
            _gather_compute_loop(u_hbm, idx_v, rows_v, sem, ps_v, lambda i: out_hbm.at[t0 + i], osem, grp, compute)

    return pl.kernel(
        body,
        out_type=jax.ShapeDtypeStruct((t, PEER_SEL * SC_LANES), F32),
        mesh=_sc_mesh(),
        scratch_types=[
            pltpu.VMEM((grp, PEER_SEL), jnp.int32),
            pltpu.VMEM((grp, half), jnp.int32),
            pltpu.VMEM((PEER_NBUF, PEER_ROWS, half), jnp.int32),
            pltpu.VMEM((2, PEER_SEL * SC_LANES), F32),
            pltpu.SemaphoreType.DMA((PEER_NBUF,)),
            pltpu.SemaphoreType.DMA((2,)),
        ],
        compiler_params=pltpu.CompilerParams(needs_layout_passes=False),
        name="peer_expert_dots",
    )(x_packed, idx, u_packed)


def peer_expert_mix(hgw, idx, v_packed):
    t = hgw.shape[0]
    half = v_packed.shape[1]
    d = 2 * half
    tpw = t // SC_WORKERS
    grp = min(PEER_GROUP, tpw)
    n_parts = 2
    cpp = half // SC_LANES // n_parts
    from jax.experimental.pallas import tpu_sc as plsc

    def body(hg_hbm, idx_hbm, v_hbm, out_hbm, idx_v, hg_v, rows_v, o_v2, sem, osem):
        base = _worker_base(tpw)

        def compute(i, h, b, slot):
            token = jnp.full((SC_LANES,), i, jnp.int32)
            for part in range(n_parts):
                def rbody(rq, accs):
                    r0 = rq * PEER_BF16_RUN
                    s = [plsc.load_gather(hg_v, [token, jnp.full((SC_LANES,), h * PEER_ROWS + r0 + k, jnp.int32)])
                         for k in range(PEER_BF16_RUN)]
                    new = []
                    for c in range(cpp):
                        at = pl.ds((part * cpp + c) * SC_LANES, SC_LANES)
                        lo, hi = _packed_dot([rows_v[b, r0 + k, at] for k in range(PEER_BF16_RUN)], s)
                        new.append(accs[2 * c] + lo)
                        new.append(accs[2 * c + 1] + hi)
                    return tuple(new)

                accs = _sc_loop(PEER_ROWS // PEER_BF16_RUN, rbody,
                                tuple(jnp.zeros((SC_LANES,), F32) for _ in range(2 * cpp)))
                def store(overwrite):
                    for c in range(cpp):
                        lo_at = pl.ds((part * cpp + c) * SC_LANES, SC_LANES)
                        hi_at = pl.ds(half + (part * cpp + c) * SC_LANES, SC_LANES)
                        if overwrite:
                            o_v2[slot, lo_at] = accs[2 * c]
                            o_v2[slot, hi_at] = accs[2 * c + 1]
                        else:
                            o_v2[slot, lo_at] = o_v2[slot, lo_at] + accs[2 * c]
                            o_v2[slot, hi_at] = o_v2[slot, hi_at] + accs[2 * c + 1]

                pl.when(h == 0)(functools.partial(store, True))
                pl.when(h != 0)(functools.partial(store, False))

        @pl.loop(0, tpw // grp)
        def _(g):
            t0 = base + g * grp
            pltpu.sync_copy(idx_hbm.at[pl.ds(t0, grp)], idx_v)
            pltpu.sync_copy(hg_hbm.at[pl.ds(t0, grp)], hg_v)
            _gather_compute_loop(v_hbm, idx_v, rows_v, sem, o_v2, lambda i: out_hbm.at[t0 + i], osem, grp, compute)

    return pl.kernel(
        body,
        out_type=jax.ShapeDtypeStruct((t, d), F32),
        mesh=_sc_mesh(),
        scratch_types=[
            pltpu.VMEM((grp, PEER_SEL), jnp.int32),
            pltpu.VMEM((grp, PEER_SEL), jnp.int32),
            pltpu.VMEM((PEER_NBUF, PEER_ROWS, half), jnp.int32),
            pltpu.VMEM((2, d), F32),
            pltpu.SemaphoreType.DMA((PEER_NBUF,)),
            pltpu.SemaphoreType.DMA((2,)),
        ],
        compiler_params=pltpu.CompilerParams(needs_layout_passes=False),
        name="peer_expert_mix",
    )(hgw, idx, v_packed)


def _peer_act_kernel(ps_ref, gate_ref, sum_ref, o_ref):
    ps = ps_ref[...]
    sel = sum_ref[...]
    hi = ps.astype(BF16)
    rest = ps - hi.astype(F32)
    mid = rest.astype(BF16)
    lo = (rest - mid.astype(F32)).astype(BF16)
    pre = (jnp.dot(hi, sel, preferred_element_type=F32) + jnp.dot(mid, sel, preferred_element_type=F32)
           + jnp.dot(lo, sel, preferred_element_type=F32))
    hg = 0.5 * pre * (1.0 + lax.erf(pre * (1.0 / math.sqrt(2.0)))) * gate_ref[...]
    bits = lax.bitcast_convert_type(hg.astype(BF16).astype(F32), jnp.int32)
    o_ref[...] = lax.bitwise_or(bits, lax.shift_right_logical(bits, jnp.int32(16)))


def peer_act(ps, gates, *, tm=512):
    t, n = ps.shape
    lane_sum = (jnp.arange(n)[:, None] // SC_LANES == jnp.arange(PEER_SEL)[None, :]).astype(BF16)
    return pl.pallas_call(
        _peer_act_kernel,
        grid=(t // tm,),
        in_specs=[
            pl.BlockSpec((tm, n), lambda i: (i, 0)),
            pl.BlockSpec((tm, PEER_SEL), lambda i: (i, 0)),
            pl.BlockSpec((n, PEER_SEL), lambda i: (0, 0)),
        ],
        out_specs=pl.BlockSpec((tm, PEER_SEL), lambda i: (i, 0)),
        out_shape=jax.ShapeDtypeStruct((t, PEER_SEL), jnp.int32),
        compiler_params=_cparams(("parallel",)),
        name="peer_act",
    )(ps, gates, lane_sum)


BATCH_GROUPS = 8


def kernel(x, norm1_g, w_in, rwkv_mu, w0, w_lora_up, a0, a_lora_up, g_lora_up, k_k, k_a, r_k, lnx_g, lnx_b,
           w_proj_a, w_proj_b, w_out, norm2_g, peer_wq, peer_subkeys, peer_u, peer_v, rel_bias, normf_g):
    bsz, seq, d = x.shape
    depth = norm1_g.shape[0]
    groups = BATCH_GROUPS if bsz % BATCH_GROUPS == 0 else 1
    gb = bsz // groups
    tg = gb * seq
    t = bsz * seq
    src = x.reshape(t, d)
    for l in range(depth):
        w_pad = jnp.concatenate([
            w_in[l][:, :COL_A + COL_B_RAW],
            jnp.zeros((d, COL_B - COL_B_RAW), w_in.dtype),
            w_in[l][:, COL_A + COL_B_RAW:]], axis=1).astype(BF16)
        u_packed = _pack_rows(peer_u[l])
        tables = {}
        last = l == depth - 1

        def mix(pending, tie=None):
            row0, h2d, ps, gates, idx = pending
            hgx = peer_act(ps, gates)
            if "v" not in tables:
                v_src = peer_v[l]
                if tie is not None:
                    tie, v_src = lax.optimization_barrier((tie, v_src))
                tables["v"] = _pack_rows(v_src)
            if tie is not None:
                tie, hgx = lax.optimization_barrier((tie, hgx))
            return tie, (row0, h2d, peer_expert_mix(hgx, idx, tables["v"]))

        outs = []

        def close(mixed):
            row0, h2d, y2d = mixed
            if last:
                outs.append(final_norm(h2d, y2d, normf_g, out=outs[-1] if outs else None, row0=row0, total_rows=t))
            else:
                outs.append(h2d + y2d)

        halves = gb == 1 and seq % (2 * MOBA_BLOCK) == 0 and (seq // 2) % (SC_WORKERS * PEER_GROUP) == 0

        pending = closing = None
        for g in range(groups):
            p2d = norm_proj(src, norm1_g[l], w_pad, row0=g * tg, rows=tg)
            p3d = p2d.reshape(gb, seq, -1)
            prep = state = None
            for s0, sn in ([(0, seq // 2), (seq // 2, seq // 2)] if halves and g == 0 else [(0, seq)]):
                oa = moba_attention(p3d, rel_bias, q0=s0 // MOBA_BLOCK, nq=sn // MOBA_BLOCK)
                if prep is None:
                    prep = tuple(rwkv_prep(p3d, rwkv_mu[l], w0[l], w_lora_up[l], a0[l], a_lora_up[l], g_lora_up[l],
                                           k_k[l], k_a[l], r_k[l]))
                mixed = None
                if pending is not None:
                    (oa, prep), mixed = mix(pending, (oa, prep))
                if closing is not None:
                    oa, y2d = lax.optimization_barrier((oa, closing[2]))
                    close(closing[:2] + (y2d,))
                    closing = None
                ob, state = rwkv_scan(*prep, lnx_g[l], lnx_b[l], state=state,
                                      c0=s0 // RWKV_CHUNK, nc=sn // RWKV_CHUNK)
                nt = gb * sn
                h2d, xn2 = merge_out(src, oa.reshape(nt, WIDTH), ob.reshape(nt, WIDTH), p2d, w_proj_a[l], w_proj_b[l],
                                     w_out[l], norm2_g[l], row0=g * tg + s0, prow0=s0)
                idx, gates = peer_route(xn2, peer_wq[l], peer_subkeys[l])
                if mixed is not None:
                    idx, y2d = lax.optimization_barrier((idx, mixed[2]))
                    closing = mixed[:2] + (y2d,)
                pending = (g * tg + s0, h2d, peer_expert_dots(xn2, idx, u_packed), gates, idx)
        if closing is not None:
            close(closing)
        close(mix(pending)[1])
        src = outs[-1] if last else jnp.concatenate(outs, axis=0)
    return src.reshape(bsz, seq, d)
```

```python
import functools
import math

import jax
import jax.numpy as jnp
from jax import lax
from jax.experimental import pallas as pl
from jax.experimental.pallas import tpu as pltpu

F32 = jnp.float32
BF16 = jnp.bfloat16
HI = lax.Precision.HIGHEST

LANES = 128
HEAD_DIM = 64
HEADS = 8
PAIRS = HEADS // 2
WIDTH = HEADS * HEAD_DIM
MOBA_BLOCK = 256
MOBA_TOPK = 3
MOBA_LO = 64
REL_BUCKETS = 32
REL_MAX_DIST = 128
DECAY_LORA = 64
AAA_LORA = 64
GATE_LORA = 160
GN_EPS = 64e-5
RMS_EPS = 1e-6
NEG = -1e30
RWKV_CHUNK = 64
COL_A = 3 * WIDTH
COL_B_RAW = 3 * WIDTH + DECAY_LORA + AAA_LORA + GATE_LORA
COL_B = 4 * WIDTH
COL_G_OFF = COL_A + COL_B
VMEM_LIMIT = 56 * 1024 * 1024


def _cparams(sem):
    return pltpu.CompilerParams(dimension_semantics=sem, vmem_limit_bytes=VMEM_LIMIT)


def _norm_proj_kernel(x_ref, g_ref, w_ref, o_ref, xn_ref):
    @pl.when(pl.program_id(1) == 0)
    def _():
        x = x_ref[...]
        ms = jnp.mean(x * x, axis=-1, keepdims=True)
        xn_ref[...] = (x * lax.rsqrt(ms + RMS_EPS) * g_ref[...]).astype(xn_ref.dtype)

    o_ref[...] = jnp.dot(xn_ref[...], w_ref[...], preferred_element_type=F32).astype(o_ref.dtype)


def norm_proj(x2d, g, w, *, row0=0, rows=None, tm=512, tn=512, out_dtype=F32):
    d = x2d.shape[1]
    t = x2d.shape[0] if rows is None else rows
    n = w.shape[1]
    r0 = row0 // tm
    return pl.pallas_call(
        _norm_proj_kernel,
        grid=(t // tm, n // tn),
        in_specs=[
            pl.BlockSpec((tm, d), lambda i, j: (r0 + i, 0)),
            pl.BlockSpec((1, d), lambda i, j: (0, 0)),
            pl.BlockSpec((d, tn), lambda i, j: (0, j)),
        ],
        out_specs=pl.BlockSpec((tm, tn), lambda i, j: (i, j)),
        out_shape=jax.ShapeDtypeStruct((t, n), out_dtype),
        scratch_shapes=[pltpu.VMEM((tm, d), w.dtype)],
        compiler_params=_cparams(("parallel", "arbitrary")),
        name="norm_proj",
    )(x2d, g.reshape(1, d), w)


def _rel_bucket(dist):
    n = jnp.maximum(dist, 0)
    max_exact = REL_BUCKETS // 2
    nf = jnp.maximum(n, 1).astype(F32)
    large = max_exact + (jnp.log(nf / max_exact) / math.log(REL_MAX_DIST / max_exact)
                         * (REL_BUCKETS - max_exact)).astype(jnp.int32)
    large = jnp.minimum(large, REL_BUCKETS - 1)
    return jnp.where(n < max_exact, n, large)


def _moba_kernel(q_ref, k_ref, v_ref, bown_ref, bprev_ref, bfar_ref, o_ref,
                 kb_ref, vb_ref, kbar_ref, *, n_blocks, q0):
    qb = pl.program_id(2) + q0
    blk = MOBA_BLOCK
    scale = 1.0 / math.sqrt(HEAD_DIM)

    rows2 = 2 * blk
    nt = (((1,), (1,)), ((), ()))

    @pl.when(pl.program_id(2) == 0)
    def _():
        kbar_ref[...] = jnp.zeros_like(kbar_ref)
        lane_b = lax.broadcasted_iota(jnp.int32, (blk, LANES), 1)
        for n in range(n_blocks):
            kblk = k_ref[0, n * blk:(n + 1) * blk, :]
            kbar_ref[n:n + 1, :] = jnp.mean(kblk, axis=0, keepdims=True)
            kb_ref[n * blk:(n + 1) * blk, 0:LANES] = kblk.astype(BF16)
            kb_ref[n * blk:(n + 1) * blk, LANES:] = ((lane_b == n) | (lane_b == MOBA_LO + n)).astype(BF16)
        vb_ref[...] = v_ref[0].astype(BF16)

    q2 = q_ref[0]
    first = lax.broadcasted_iota(jnp.int32, (blk, LANES), 1) < HEAD_DIM
    qh = jnp.concatenate([jnp.where(first, q2, 0.0), jnp.where(first, 0.0, q2)], axis=0)
    lane = lax.broadcasted_iota(jnp.int32, (rows2, LANES), 1)
    rowi = lax.broadcasted_iota(jnp.int32, (rows2, LANES), 0)
    gate = lax.dot_general(qh.astype(BF16), kbar_ref[...].astype(BF16), nt, preferred_element_type=F32)
    g = jnp.where(lane < qb, gate, -jnp.inf)
    chosen = lane < 0
    lane_f = lane.astype(F32)
    for _ in range(MOBA_TOPK):
        m = jnp.max(g, axis=1, keepdims=True)
        idx = jnp.min(jnp.where(g == m, lane_f, float(LANES)), axis=1, keepdims=True)
        hit = (lane_f == idx) & (m > -jnp.inf)
        chosen = chosen | hit
        g = jnp.where(hit, -jnp.inf, g)
    nfar = qb - 1
    bfar = jnp.where(rowi < blk, bfar_ref[0, 0:1, 0:1], bfar_ref[1, 0:1, 0:1])
    bhi = bfar.astype(BF16).astype(F32)
    madd = jnp.where(lane < nfar, jnp.where(chosen, bhi, NEG),
                     jnp.where(lane == nfar, jnp.where(chosen, 0.0, NEG),
                               jnp.where((lane >= MOBA_LO) & (lane - MOBA_LO < nfar), bfar - bhi, 0.0)))
    q_aug = jnp.concatenate([(qh * scale).astype(BF16), madd.astype(BF16)], axis=1)

    prev0 = pl.multiple_of(jnp.maximum(nfar, 0) * blk, blk)
    own0 = pl.multiple_of(qb * blk, blk)
    s_prev = (lax.dot_general(q_aug, kb_ref[pl.ds(prev0, blk), :], nt, preferred_element_type=F32)
              + bprev_ref[...].reshape(rows2, blk) + jnp.where(qb > 0, 0.0, NEG))
    s_own = (lax.dot_general(q_aug, kb_ref[pl.ds(own0, blk), :], nt, preferred_element_type=F32)
             + bown_ref[...].reshape(rows2, blk))
    r = lax.broadcasted_iota(jnp.int32, (rows2, blk), 0)
    c = lax.broadcasted_iota(jnp.int32, (rows2, blk), 1)
    s_own = jnp.where(lax.bitwise_and(r, blk - 1) >= c, s_own, NEG)
    s = jnp.concatenate([s_prev, s_own], axis=1)
    m_i = jnp.max(s, axis=1, keepdims=True)
    p = jnp.exp(s - m_i)
    l_i = jnp.sum(p, axis=1, keepdims=True)
    v0 = jnp.concatenate([vb_ref[pl.ds(prev0, blk), :], vb_ref[pl.ds(own0, blk), :]], axis=0)
    acc = jnp.dot(p.astype(BF16), v0, preferred_element_type=F32)

    def body(it, carry):
        m_i, l_i, acc = carry
        k0 = pl.multiple_of(it * rows2, rows2)
        s = lax.dot_general(q_aug, kb_ref[pl.ds(k0, rows2), :], nt, preferred_element_type=F32)
        tail = jnp.where(2 * it + 1 < nfar, 0.0, NEG)
        s = jnp.concatenate([s[:, :blk], s[:, blk:] + tail], axis=1)
        m_new = jnp.maximum(m_i, jnp.max(s, axis=1, keepdims=True))
        alpha = jnp.exp(m_i - m_new)
        p = jnp.exp(s - m_new)
        l_new = alpha * l_i + jnp.sum(p, axis=1, keepdims=True)
        acc_new = alpha * acc + jnp.dot(p.astype(BF16), vb_ref[pl.ds(k0, rows2), :], preferred_element_type=F32)
        return m_new, l_new, acc_new

    m_i, l_i, acc = lax.fori_loop(0, (jnp.maximum(nfar, 0) + 1) // 2, body, (m_i, l_i, acc))
    out = acc / l_i
    o_ref[0] = jnp.where(first, out[:blk], out[blk:])


def moba_attention(p3d, rel_bias, *, q0=0, nq=None):
    bsz, seq, _ = p3d.shape
    blk = MOBA_BLOCK
    n_blocks = seq // blk
    nq = n_blocks - q0 if nq is None else nq
    assert n_blocks <= MOBA_LO and seq % blk == 0
    span = 2 * blk
    by_dist = rel_bias[:, _rel_bucket(jnp.arange(span))].astype(F32)
    shift = jnp.arange(span)

    def toeplitz(c):
        k = jnp.where(shift < blk, shift, shift - span)
        s = by_dist[:, jnp.clip(c - k, 0, span - 1)]
        tiled = jnp.tile(s, (1, blk))[:, :blk * (span - 1)]
        return tiled.reshape(HEADS, blk, span - 1)[:, :, :blk]

    bias_own = toeplitz(0)
    bias_prev = toeplitz(blk)
    bias_far = jnp.broadcast_to(rel_bias[:, REL_BUCKETS - 1].astype(F32)[:, None, None], (HEADS, 8, LANES))
    kern = functools.partial(_moba_kernel, n_blocks=n_blocks, q0=q0)
    return pl.pallas_call(
        kern,
        grid=(bsz, PAIRS, nq),
        in_specs=[
            pl.BlockSpec((1, blk, LANES), lambda b, h, i: (b, q0 + i, h)),
            pl.BlockSpec((1, seq, LANES), lambda b, h, i: (b, 0, PAIRS + h)),
            pl.BlockSpec((1, seq, LANES), lambda b, h, i: (b, 0, 2 * PAIRS + h)),
            pl.BlockSpec((2, blk, blk), lambda b, h, i: (h, 0, 0)),
            pl.BlockSpec((2, blk, blk), lambda b, h, i: (h, 0, 0)),
            pl.BlockSpec((2, 8, LANES), lambda b, h, i: (h, 0, 0)),
        ],
        out_specs=pl.BlockSpec((1, blk, LANES), lambda b, h, i: (b, i, h)),
        out_shape=jax.ShapeDtypeStruct((bsz, nq * blk, WIDTH), F32),
        scratch_shapes=[
            pltpu.VMEM((seq, 2 * LANES), BF16),
            pltpu.VMEM((seq, LANES), BF16),
            pltpu.VMEM((LANES, LANES), F32),
        ],
        compiler_params=_cparams(("parallel", "parallel", "arbitrary")),
        name="moba",
    )(p3d, p3d, p3d, bias_own, bias_prev, bias_far)


def _shifted(x, carry_row):
    rows = lax.broadcasted_iota(jnp.int32, x.shape, 0)
    return jnp.where(rows == 0, carry_row, pltpu.roll(x, 1, axis=0))


def _rwkv_prep_kernel(pr_ref, pk_ref, pv_ref, pl_ref, mu_ref, vec_ref, ww_ref, wa_ref, wg_ref,
                      bd_ref, tri_ref,
                      rt_ref, kt_ref, kd_ref, bd_out_ref, v_ref, g_ref, bonus_ref, pend_ref,
                      carry_ref, *, chunk):
    @pl.when(pl.program_id(1) == 0)
    def _():
        carry_ref[...] = jnp.zeros_like(carry_ref)

    def mix(ref, j):
        x = ref[0]
        mu = mu_ref[0:1, j * WIDTH:(j + 1) * WIDTH]
        prev = _shifted(x, carry_ref[0:1, j * WIDTH:(j + 1) * WIDTH])
        carry_ref[0:1, j * WIDTH:(j + 1) * WIDTH] = x[x.shape[0] - 1:, :]
        return x + mu * (prev - x)

    r = mix(pr_ref, 0)
    k = mix(pk_ref, 1)
    v = mix(pv_ref, 2)
    lo = mix(pl_ref, 3)
    w0, a0, k_k, k_a, r_k = (vec_ref[i:i + 1, :] for i in range(5))
    xwa = lo[:, 0:LANES]
    xg = lo[:, LANES:3 * LANES]
    lw = jnp.dot(jnp.tanh(xwa), ww_ref[...], precision=HI, preferred_element_type=F32)
    la = jnp.dot(xwa, wa_ref[...], precision=HI, preferred_element_type=F32)
    g = jnp.dot(jax.nn.sigmoid(xg), wg_ref[...], precision=HI, preferred_element_type=F32)
    z = -(w0 + lw)
    softplus = jnp.maximum(z, 0.0) + jnp.log(1.0 + jnp.exp(-jnp.abs(z)))
    logw = -jnp.exp(-softplus - 0.5)
    a = jax.nn.sigmoid(a0 + la)
    kk = k * k_k
    ss = jnp.dot(kk * kk, bd_ref[...], precision=HI, preferred_element_type=F32)
    kk = kk / jnp.maximum(jnp.sqrt(ss), 1e-12)
    k2 = k * (1.0 + (a - 1.0) * k_a)
    rk = jnp.dot(r * k2 * r_k, bd_ref[...], precision=HI, preferred_element_type=F32)
    cs = jnp.dot(tri_ref[...], logw, precision=HI, preferred_element_type=F32)
    e_pos = jnp.exp(cs)
    e_neg = jnp.exp(-cs)
    rt_ref[0] = r * e_pos
    kt_ref[0] = kk * jnp.exp(cs - logw)
    kd_ref[0] = k2 * e_neg
    bd_out_ref[0] = kk * a * e_neg
    v_ref[0] = v
    g_ref[0] = g
    bonus_ref[0] = rk * v
    ts = e_pos.shape[0]
    for c in range(ts // chunk):
        pend_ref[0, c:c + 1, :] = e_pos[(c + 1) * chunk - 1:(c + 1) * chunk, :]


def rwkv_prep(p3d, rwkv_mu, w0, w_lora_up, a0, a_lora_up, g_lora_up, k_k, k_a, r_k, *, ts=512):
    bsz, seq, _ = p3d.shape
    chunk = RWKV_CHUNK
    ts = min(ts, seq)
    mu = jnp.pad(rwkv_mu, (0, COL_B - COL_B_RAW)).reshape(1, COL_B)
    vec = jnp.stack([w0, a0, k_k, k_a, r_k.reshape(-1)] + [jnp.zeros_like(w0)] * 3).astype(F32)
    ww = jnp.zeros((LANES, WIDTH), F32).at[:DECAY_LORA].set(w_lora_up)
    wa = jnp.zeros((LANES, WIDTH), F32).at[DECAY_LORA:DECAY_LORA + AAA_LORA].set(a_lora_up)
    wg = jnp.zeros((2 * LANES, WIDTH), F32).at[:GATE_LORA].set(g_lora_up)
    hid = jnp.arange(WIDTH) // HEAD_DIM
    bd = (hid[:, None] == hid[None, :]).astype(F32)
    tix = jnp.arange(ts)
    tri = ((tix[:, None] // chunk == tix[None, :] // chunk) & (tix[None, :] <= tix[:, None])).astype(F32)
    c0 = COL_A // WIDTH
    big = jax.ShapeDtypeStruct((bsz, seq, WIDTH), F32)
    wspec = lambda shape: pl.BlockSpec(shape, lambda b, i: (0, 0))
    ospec = pl.BlockSpec((1, ts, WIDTH), lambda b, i: (b, i, 0))
    return pl.pallas_call(
        functools.partial(_rwkv_prep_kernel, chunk=chunk),
        grid=(bsz, seq // ts),
        in_specs=[
            pl.BlockSpec((1, ts, WIDTH), lambda b, i: (b, i, c0)),
            pl.BlockSpec((1, ts, WIDTH), lambda b, i: (b, i, c0 + 1)),
            pl.BlockSpec((1, ts, WIDTH), lambda b, i: (b, i, c0 + 2)),
            pl.BlockSpec((1, ts, WIDTH), lambda b, i: (b, i, c0 + 3)),
            wspec((1, COL_B)), wspec((8, WIDTH)), wspec((LANES, WIDTH)), wspec((LANES, WIDTH)),
            wspec((2 * LANES, WIDTH)), wspec((WIDTH, WIDTH)), wspec((ts, ts)),
        ],
        out_specs=[ospec] * 7 + [pl.BlockSpec((1, ts // chunk, WIDTH), lambda b, i: (b, i, 0))],
        out_shape=[big] * 7 + [jax.ShapeDtypeStruct((bsz, seq // chunk, WIDTH), F32)],
        scratch_shapes=[pltpu.VMEM((8, COL_B), F32)],
        compiler_params=_cparams(("parallel", "arbitrary")),
        name="rwkv_prep",
    )(p3d, p3d, p3d, p3d, mu, vec, ww, wa, wg, bd, tri)


def _rwkv_scan_kernel(rt_ref, kt_ref, kd_ref, bd_ref, v_ref, g_ref, bonus_ref, pend_ref, ln_ref, sin_ref,
                      o_ref, state_ref, *, chunk, prec):
    @pl.when(pl.program_id(1) == 0)
    def _():
        state_ref[...] = sin_ref[...]

    c2 = 2 * chunk
    lane = lax.broadcasted_iota(jnp.int32, (chunk, LANES), 1)
    first = lane < HEAD_DIM
    row = lax.broadcasted_iota(jnp.int32, (c2, c2), 0)
    col = lax.broadcasted_iota(jnp.int32, (c2, c2), 1)
    eye = (row == col).astype(F32)
    hrow = lax.broadcasted_iota(jnp.int32, (LANES, LANES), 0) // HEAD_DIM
    hcol = lax.broadcasted_iota(jnp.int32, (LANES, LANES), 1) // HEAD_DIM
    head_mean = jnp.where(hrow == hcol, 1.0 / HEAD_DIM, 0.0).astype(F32)
    nt = (((1,), (1,)), ((), ()))
    tn = (((0,), (0,)), ((), ()))
    dot = functools.partial(jnp.dot, precision=prec, preferred_element_type=F32)
    dotg = functools.partial(lax.dot_general, precision=prec, preferred_element_type=F32)

    def stack(x):
        return jnp.concatenate([jnp.where(first, x, 0.0), jnp.where(first, 0.0, x)], axis=0)

    pairs = range(PAIRS)
    sls = [slice(hp * LANES, (hp + 1) * LANES) for hp in pairs]
    rs, ks, kds, bs, vs = ([stack(ref[0, :, sl]) for sl in sls] for ref in (rt_ref, kt_ref, kd_ref, bd_ref, v_ref))
    hts = [state_ref[0, hp] for hp in pairs]
    big = [dotg(jnp.concatenate([ks[hp], rs[hp]], axis=0), jnp.concatenate([bs[hp], kds[hp]], axis=0), nt)
           for hp in pairs]
    a_b = [jnp.where(row > col, big[hp][0:c2, 0:c2], 0.0) for hp in pairs]
    a_k = [jnp.where(row > col, big[hp][0:c2, c2:], 0.0) for hp in pairs]
    a_rb = [jnp.where(row >= col, big[hp][c2:, 0:c2], 0.0) for hp in pairs]
    a_rk = [jnp.where(row >= col, big[hp][c2:, c2:], 0.0) for hp in pairs]
    kh = [dotg(jnp.concatenate([ks[hp], rs[hp]], axis=0), hts[hp], nt) for hp in pairs]
    av = [dot(jnp.concatenate([a_k[hp], a_rk[hp]], axis=0), vs[hp]) for hp in pairs]
    vk = [dotg(vs[hp], kds[hp], tn) for hp in pairs]
    inv = [eye - a_b[hp] for hp in pairs]
    pw = [dot(a_b[hp], a_b[hp]) for hp in pairs]
    n_sq = int(math.log2(chunk)) - 1
    for lvl in range(n_sq):
        if lvl + 1 < n_sq:
            both = [dot(jnp.concatenate([inv[hp], pw[hp]], axis=0), pw[hp]) for hp in pairs]
            inv = [inv[hp] + both[hp][0:c2] for hp in pairs]
            pw = [both[hp][c2:] for hp in pairs]
        else:
            inv = [inv[hp] + dot(inv[hp], pw[hp]) for hp in pairs]
    us = [dot(inv[hp], kh[hp][0:c2] + av[hp][0:c2]) for hp in pairs]
    ub = [dotg(us[hp], bs[hp], tn) for hp in pairs]
    au = [dot(a_rb[hp], us[hp]) for hp in pairs]
    for hp in pairs:
        sl = sls[hp]
        pend = pend_ref[0, 0, 0:1, sl]
        state_ref[0, hp] = (hts[hp] + vk[hp] - ub[hp]) * pend
        os_ = kh[hp][c2:] + av[hp][c2:] - au[hp]
        o = os_[0:chunk] + os_[chunk:]
        mu = jnp.dot(o, head_mean, precision=HI, preferred_element_type=F32)
        d = o - mu
        var = jnp.dot(d * d, head_mean, precision=HI, preferred_element_type=F32)
        on = d * lax.rsqrt(var + GN_EPS) * ln_ref[0:1, sl] + ln_ref[1:2, sl]
        o_ref[0, :, sl] = (on + bonus_ref[0, :, sl]) * g_ref[0, :, sl]


def rwkv_scan(rt, kt, kd, bd, v, g, bonus, pend, lnx_g, lnx_b, *, state=None, c0=0, nc=None, prec=None):
    bsz, seq, _ = rt.shape
    chunk = RWKV_CHUNK
    n_chunks = seq // chunk
    nc = n_chunks - c0 if nc is None else nc
    ln = jnp.stack([lnx_g, lnx_b] + [jnp.zeros_like(lnx_g)] * 6).astype(F32)
    pend4 = pend.reshape(bsz, n_chunks, 1, WIDTH)
    if state is None:
        state = jnp.zeros((bsz, PAIRS, LANES, LANES), F32)
    spec = pl.BlockSpec((1, chunk, WIDTH), lambda b, c: (b, c0 + c, 0))
    sspec = pl.BlockSpec((1, PAIRS, LANES, LANES), lambda b, c: (b, 0, 0, 0))
    return pl.pallas_call(
        functools.partial(_rwkv_scan_kernel, chunk=chunk, prec=prec),
        grid=(bsz, nc),
        in_specs=[spec] * 7 + [
            pl.BlockSpec((1, 1, 1, WIDTH), lambda b, c: (b, c0 + c, 0, 0)),
            pl.BlockSpec((8, WIDTH), lambda b, c: (0, 0)),
            sspec,
        ],
        out_specs=[pl.BlockSpec((1, chunk, WIDTH), lambda b, c: (b, c, 0)), sspec],
        out_shape=[jax.ShapeDtypeStruct((bsz, nc * chunk, WIDTH), F32),
                   jax.ShapeDtypeStruct((bsz, PAIRS, LANES, LANES), F32)],
        compiler_params=_cparams(("parallel", "arbitrary")),
        name="rwkv_scan",
    )(rt, kt, kd, bd, v, g, bonus, pend4, ln, state)


def _merge_kernel(x_ref, oa_ref, ob_ref, ga_ref, gb_ref, wa_ref, wb_ref, wo_ref, g2_ref,
                  h_ref, xn_ref, acc_ref):
    j = pl.program_id(1)

    @pl.when(j == 0)
    def _():
        acc_ref[...] = x_ref[...]

    ya = jnp.dot(oa_ref[...].astype(BF16), wa_ref[...], preferred_element_type=F32)
    yb = jnp.dot(ob_ref[...].astype(BF16), wb_ref[...], preferred_element_type=F32)
    y = jax.nn.sigmoid(ga_ref[...]) * ya + jax.nn.sigmoid(gb_ref[...]) * yb
    acc_ref[...] += jnp.dot(y.astype(BF16), wo_ref[...], preferred_element_type=F32)

    @pl.when(j == pl.num_programs(1) - 1)
    def _():
        h = acc_ref[...]
        h_ref[...] = h
        ms = jnp.mean(h * h, axis=-1, keepdims=True)
        xn_ref[...] = _pack_halves(h * lax.rsqrt(ms + RMS_EPS) * g2_ref[...])


def _pack_halves(x):
    half = x.shape[1] // 2
    lo = lax.bitcast_convert_type(x[:, :half].astype(BF16).astype(F32), jnp.int32)
    hi = lax.bitcast_convert_type(x[:, half:].astype(BF16).astype(F32), jnp.int32)
    return lax.bitwise_or(lax.shift_right_logical(lo, jnp.int32(16)), hi)


def _unpack_halves(words):
    lo, hi = _unpack_words(words)
    return jnp.concatenate([lo, hi], axis=1)


def merge_out(x2d, oa, ob, p2d, w_proj_a, w_proj_b, w_out, norm2_g, *, row0=0, prow0=0, tm=512):
    t, d = oa.shape[0], x2d.shape[1]
    r0 = row0 // tm
    p0 = prow0 // tm
    tn = WIDTH
    nj = d // tn
    g0 = COL_G_OFF // tn
    return pl.pallas_call(
        _merge_kernel,
        grid=(t // tm, nj),
        in_specs=[
            pl.BlockSpec((tm, d), lambda i, j: (r0 + i, 0)),
            pl.BlockSpec((tm, WIDTH), lambda i, j: (i, 0)),
            pl.BlockSpec((tm, WIDTH), lambda i, j: (i, 0)),
            pl.BlockSpec((tm, tn), lambda i, j: (p0 + i, g0 + j)),
            pl.BlockSpec((tm, tn), lambda i, j: (p0 + i, g0 + nj + j)),
            pl.BlockSpec((WIDTH, tn), lambda i, j: (0, j)),
            pl.BlockSpec((WIDTH, tn), lambda i, j: (0, j)),
            pl.BlockSpec((tn, d), lambda i, j: (j, 0)),
            pl.BlockSpec((1, d), lambda i, j: (0, 0)),
        ],
        out_specs=[pl.BlockSpec((tm, d), lambda i, j: (i, 0)), pl.BlockSpec((tm, d // 2), lambda i, j: (i, 0))],
        out_shape=[jax.ShapeDtypeStruct((t, d), F32), jax.ShapeDtypeStruct((t, d // 2), jnp.int32)],
        scratch_shapes=[pltpu.VMEM((tm, d), F32)],
        compiler_params=_cparams(("parallel", "arbitrary")),
        name="merge_out",
    )(x2d, oa, ob, p2d, p2d, w_proj_a.astype(BF16), w_proj_b.astype(BF16), w_out.astype(BF16),
      norm2_g.reshape(1, d))


PEER_HEADS = 8
PEER_NKEYS = 128
PEER_TOPK = 16
PEER_HALF = 128


def _topk_rows(s, k):
    n = s.shape[0]
    rows = lax.broadcasted_iota(jnp.int32, s.shape, 0).astype(F32)
    vals, ids = [], []
    for _ in range(k):
        m = jnp.max(s, axis=0, keepdims=True)
        first = jnp.min(jnp.where(s == m, rows, float(n)), axis=0, keepdims=True)
        vals.append(m)
        ids.append(first)
        s = jnp.where(rows == first, -jnp.inf, s)
    return jnp.concatenate(vals, axis=0), jnp.concatenate(ids, axis=0)


def _take_rows(table, ids):
    rows = lax.broadcasted_iota(jnp.int32, table.shape, 0).astype(F32)
    return jnp.sum(jnp.where(rows == ids, table, 0.0), axis=0, keepdims=True)


def _peer_route_kernel(xn_ref, wq_ref, sk_ref, idx_ref, gate_ref, *, prec):
    tt = xn_ref.shape[0]
    k = PEER_TOPK
    xn = _unpack_halves(xn_ref[...]) if xn_ref.dtype == jnp.int32 else xn_ref[...]
    q = jnp.dot(xn.astype(wq_ref.dtype), wq_ref[...], precision=prec, preferred_element_type=F32)
    nt = (((1,), (1,)), ((), ()))
    idx_rows, gate_rows = [], []
    half = k // 2
    for h in range(PEER_HEADS):
        tops = []
        for p in range(2):
            c0 = (h * 2 + p) * PEER_HALF
            s = lax.dot_general(sk_ref[h, p].astype(wq_ref.dtype), q[:, c0:c0 + PEER_HALF].astype(wq_ref.dtype),
                                nt, precision=prec, preferred_element_type=F32)
            tops.append(_topk_rows(s, k))
        (s0, i0), (s1, i1) = tops
        cs = [s0[0:1] + s1] + [s0[i:i + 1] + s1[0:half] for i in range(1, half)] + [s0[half:] + s1[0:1]]
        best_s, pos = _topk_rows(jnp.concatenate(cs, axis=0), k)
        mid = jnp.floor((pos - k) * (1.0 / half))
        end_mid = float(k + (half - 1) * half)
        i_rank = jnp.where(pos < k, 0.0, jnp.where(pos < end_mid, 1.0 + mid, pos - (end_mid - half)))
        j_rank = jnp.where(pos < k, pos, jnp.where(pos < end_mid, (pos - k) - half * mid, 0.0))
        ids = [_take_rows(i0, i_rank[n:n + 1]) * PEER_NKEYS + _take_rows(i1, j_rank[n:n + 1]) for n in range(k)]
        e = jnp.exp(best_s - best_s[0:1])
        gate_rows.append(e / jnp.sum(e, axis=0, keepdims=True))
        idx_rows.append(jnp.concatenate(ids, axis=0).astype(jnp.int32))
    idx_ref[...] = jnp.concatenate(idx_rows, axis=0).T
    gate_ref[...] = jnp.concatenate(gate_rows, axis=0).T


def peer_route(xn2d, peer_wq, peer_subkeys, *, tt=256, prec=None, wdtype=BF16):
    t, dx = xn2d.shape
    d, nq = peer_wq.shape
    n_sel = PEER_HEADS * PEER_TOPK
    return pl.pallas_call(
        functools.partial(_peer_route_kernel, prec=prec),
        grid=(t // tt,),
        in_specs=[
            pl.BlockSpec((tt, dx), lambda i: (i, 0)),
            pl.BlockSpec((d, nq), lambda i: (0, 0)),
            pl.BlockSpec((PEER_HEADS, 2, PEER_NKEYS, PEER_HALF), lambda i: (0, 0, 0, 0)),
        ],
        out_specs=[pl.BlockSpec((tt, n_sel), lambda i: (i, 0))] * 2,
        out_shape=[jax.ShapeDtypeStruct((t, n_sel), jnp.int32), jax.ShapeDtypeStruct((t, n_sel), F32)],
        compiler_params=_cparams(("parallel",)),
        name="peer_route",
    )(xn2d, peer_wq.astype(wdtype), peer_subkeys)


def _final_kernel(h_ref, y_ref, g_ref, *rest):
    o_ref = rest[-1]
    h = h_ref[...] + y_ref[...]
    ms = jnp.mean(h * h, axis=-1, keepdims=True)
    o_ref[...] = h * lax.rsqrt(ms + RMS_EPS) * g_ref[...]


def final_norm(h2d, y2d, g, *, out=None, row0=0, total_rows=None, tm=1024):
    t, d = h2d.shape
    total = t if total_rows is None else total_rows
    r0 = row0 // tm
    spec = pl.BlockSpec((tm, d), lambda i: (i, 0))
    in_specs = [spec, spec, pl.BlockSpec((1, d), lambda i: (0, 0))]
    args = [h2d, y2d, g.reshape(1, d)]
    aliases = {}
    if out is not None:
        in_specs.append(pl.BlockSpec(memory_space=pl.ANY))
        args.append(out)
        aliases = {3: 0}
    return pl.pallas_call(
        _final_kernel,
        grid=(t // tm,),
        in_specs=in_specs,
        out_specs=pl.BlockSpec((tm, d), lambda i: (r0 + i, 0)),
        out_shape=jax.ShapeDtypeStruct((total, d), F32),
        input_output_aliases=aliases,
        compiler_params=_cparams(("parallel",)),
        name="final_norm",
    )(*args)


SC_CORES = 2
SC_SUBCORES = 16
SC_LANES = 16
SC_WORKERS = SC_CORES * SC_SUBCORES
PEER_SEL = PEER_HEADS * PEER_TOPK
PEER_ROWS = 32
PEER_PARTS = PEER_SEL // PEER_ROWS
PEER_NBUF = 4
PEER_GROUP = 32
PEER_BF16_RUN = 4


def _pack_rows(w):
    half = w.shape[1] // 2
    bits = lax.bitcast_convert_type(w.astype(BF16), jnp.uint16).astype(jnp.uint32)
    return lax.bitcast_convert_type(bits[:, :half] | (bits[:, half:] << 16), jnp.int32)


def _unpack_words(w):
    lo = lax.bitcast_convert_type(lax.shift_left(w, jnp.int32(16)), F32)
    hi = lax.bitcast_convert_type(lax.bitwise_and(w, jnp.int32(-65536)), F32)
    return lo, hi


def _packed_dot(a_words, b_words):
    from jax.experimental.pallas import tpu_sc as plsc
    prods = [plsc.bitcast(a, BF16) * plsc.bitcast(b, BF16) for a, b in zip(a_words, b_words)]
    while len(prods) > 1:
        prods = [prods[k] + prods[k + 1] for k in range(0, len(prods), 2)]
    return _unpack_words(plsc.bitcast(prods[0], jnp.int32))


def _sc_mesh():
    from jax.experimental.pallas import tpu_sc as plsc
    return plsc.VectorSubcoreMesh(core_axis_name="c", subcore_axis_name="s",
                                  num_cores=SC_CORES, num_subcores=SC_SUBCORES)


def _sc_loop(n, body, carry):
    from jax.experimental.pallas import tpu_sc as plsc
    return plsc.parallel_loop(0, n, carry=carry)(body)


def _worker_base(tokens_per_worker):
    return (lax.axis_index("s") * SC_CORES + lax.axis_index("c")) * tokens_per_worker


def _gather_compute_loop(table_hbm, idx_v, rows_v, sem, stage_v, out_row, osem, grp, compute):
    n_gathers = PEER_PARTS * grp
    ahead = PEER_NBUF - 1

    def gather(j, b):
        i = j // PEER_PARTS if isinstance(j, int) else lax.shift_right_logical(j, PEER_PARTS.bit_length() - 1)
        h = j % PEER_PARTS if isinstance(j, int) else lax.bitwise_and(j, PEER_PARTS - 1)
        ids = idx_v.at[i, pl.ds(pl.multiple_of(h * PEER_ROWS, PEER_ROWS), PEER_ROWS)]
        return pltpu.make_async_copy(table_hbm.at[ids], rows_v.at[b], sem.at[b])

    def put(i, slot):
        return pltpu.make_async_copy(stage_v.at[slot], out_row(i), osem.at[slot])

    for j in range(ahead):
        gather(j, j).start()

    @pl.loop(0, n_gathers)
    def _(j):
        b = lax.bitwise_and(j, PEER_NBUF - 1)
        h = lax.bitwise_and(j, PEER_PARTS - 1)
        i = lax.shift_right_logical(j, PEER_PARTS.bit_length() - 1)
        slot = lax.bitwise_and(i, 1)

        @pl.when((h == 0) & (i >= 2))
        def _():
            put(i - 2, slot).wait()

        @pl.when(j + ahead < n_gathers)
        def _():
            gather(j + ahead, lax.bitwise_and(j + ahead, PEER_NBUF - 1)).start()

        gather(j, b).wait()
        compute(i, h, b, slot)

        @pl.when(h == PEER_PARTS - 1)
        def _():
            put(i, slot).start()

    put(grp - 2, 0).wait()
    put(grp - 1, 1).wait()


def peer_expert_dots(x_packed, idx, u_packed):
    t, half = x_packed.shape
    n_chunks = half // SC_LANES
    tpw = t // SC_WORKERS
    grp = min(PEER_GROUP, tpw)
    rows_tog = 4

    def body(x_hbm, idx_hbm, u_hbm, out_hbm, idx_v, x_v, rows_v, ps_v, sem, osem):
        base = _worker_base(tpw)

        def compute(i, h, b, slot):
            @pl.loop(0, PEER_ROWS // rows_tog)
            def _(rg):
                r0 = rg * rows_tog
                accs = [[None, None] for _ in range(rows_tog)]
                for c0 in range(0, n_chunks, PEER_BF16_RUN):
                    ats = [pl.ds((c0 + k) * SC_LANES, SC_LANES) for k in range(PEER_BF16_RUN)]
                    xw = [x_v[i, at] for at in ats]
                    for r in range(rows_tog):
                        terms = _packed_dot([rows_v[b, r0 + r, at] for at in ats], xw)
                        for k, term in enumerate(terms):
                            accs[r][k] = term if accs[r][k] is None else accs[r][k] + term
                for r in range(rows_tog):
                    at = pl.ds(pl.multiple_of((h * PEER_ROWS + r0 + r) * SC_LANES, SC_LANES), SC_LANES)
                    ps_v[slot, at] = accs[r][0] + accs[r][1]

        @pl.loop(0, tpw // grp)
        def _(g):
            t0 = base + g * grp
            pltpu.sync_copy(idx_hbm.at[pl.ds(t0, grp)], idx_v)
            pltpu.sync_copy(x_hbm.at[pl.ds(t0, grp)], x_v)
            _gather_compute_loop(u_hbm, idx_v, rows_v, sem, ps_v, lambda i: out_hbm.at[t0 + i], osem, grp, compute)

    return pl.kernel(
        body,
        out_type=jax.ShapeDtypeStruct((t, PEER_SEL * SC_LANES), F32),
        mesh=_sc_mesh(),
        scratch_types=[
            pltpu.VMEM((grp, PEER_SEL), jnp.int32),
            pltpu.VMEM((grp, half), jnp.int32),
            pltpu.VMEM((PEER_NBUF, PEER_ROWS, half), jnp.int32),
            pltpu.VMEM((2, PEER_SEL * SC_LANES), F32),
            pltpu.SemaphoreType.DMA((PEER_NBUF,)),
            pltpu.SemaphoreType.DMA((2,)),
        ],
        compiler_params=pltpu.CompilerParams(needs_layout_passes=False),
        name="peer_expert_dots",
    )(x_packed, idx, u_packed)


def peer_expert_mix(hgw, idx, v_packed):
    t = hgw.shape[0]
    half = v_packed.shape[1]
    d = 2 * half
    tpw = t // SC_WORKERS
    grp = min(PEER_GROUP, tpw)
    n_parts = 2
    cpp = half // SC_LANES // n_parts
    from jax.experimental.pallas import tpu_sc as plsc

    def body(hg_hbm, idx_hbm, v_hbm, out_hbm, idx_v, hg_v, rows_v, o_v2, sem, osem):
        base = _worker_base(tpw)

        def compute(i, h, b, slot):
            token = jnp.full((SC_LANES,), i, jnp.int32)
            for part in range(n_parts):
                def rbody(rq, accs):
                    r0 = rq * PEER_BF16_RUN
                    s = [plsc.load_gather(hg_v, [token, jnp.full((SC_LANES,), h * PEER_ROWS + r0 + k, jnp.int32)])
                         for k in range(PEER_BF16_RUN)]
                    new = []
                    for c in range(cpp):
                        at = pl.ds((part * cpp + c) * SC_LANES, SC_LANES)
                        lo, hi = _packed_dot([rows_v[b, r0 + k, at] for k in range(PEER_BF16_RUN)], s)
                        new.append(accs[2 * c] + lo)
                        new.append(accs[2 * c + 1] + hi)
                    return tuple(new)

                accs = _sc_loop(PEER_ROWS // PEER_BF16_RUN, rbody,
                                tuple(jnp.zeros((SC_LANES,), F32) for _ in range(2 * cpp)))
                def store(overwrite):
                    for c in range(cpp):
                        lo_at = pl.ds((part * cpp + c) * SC_LANES, SC_LANES)
                        hi_at = pl.ds(half + (part * cpp + c) * SC_LANES, SC_LANES)
                        if overwrite:
                            o_v2[slot, lo_at] = accs[2 * c]
                            o_v2[slot, hi_at] = accs[2 * c + 1]
                        else:
                            o_v2[slot, lo_at] = o_v2[slot, lo_at] + accs[2 * c]
                            o_v2[slot, hi_at] = o_v2[slot, hi_at] + accs[2 * c + 1]

                pl.when(h == 0)(functools.partial(store, True))
                pl.when(h != 0)(functools.partial(store, False))

        @pl.loop(0, tpw // grp)
        def _(g):
            t0 = base + g * grp
            pltpu.sync_copy(idx_hbm.at[pl.ds(t0, grp)], idx_v)
            pltpu.sync_copy(hg_hbm.at[pl.ds(t0, grp)], hg_v)
            _gather_compute_loop(v_hbm, idx_v, rows_v, sem, o_v2, lambda i: out_hbm.at[t0 + i], osem, grp, compute)

    return pl.kernel(
        body,
        out_type=jax.ShapeDtypeStruct((t, d), F32),
        mesh=_sc_mesh(),
        scratch_types=[
            pltpu.VMEM((grp, PEER_SEL), jnp.int32),
            pltpu.VMEM((grp, PEER_SEL), jnp.int32),
            pltpu.VMEM((PEER_NBUF, PEER_ROWS, half), jnp.int32),
            pltpu.VMEM((2, d), F32),
            pltpu.SemaphoreType.DMA((PEER_NBUF,)),
            pltpu.SemaphoreType.DMA((2,)),
        ],
        compiler_params=pltpu.CompilerParams(needs_layout_passes=False),
        name="peer_expert_mix",
    )(hgw, idx, v_packed)


def _peer_act_kernel(ps_ref, gate_ref, sum_ref, o_ref):
    ps = ps_ref[...]
    sel = sum_ref[...]
    hi = ps.astype(BF16)
    rest = ps - hi.astype(F32)
    mid = rest.astype(BF16)
    lo = (rest - mid.astype(F32)).astype(BF16)
    pre = (jnp.dot(hi, sel, preferred_element_type=F32) + jnp.dot(mid, sel, preferred_element_type=F32)
           + jnp.dot(lo, sel, preferred_element_type=F32))
    hg = 0.5 * pre * (1.0 + lax.erf(pre * (1.0 / math.sqrt(2.0)))) * gate_ref[...]
    bits = lax.bitcast_convert_type(hg.astype(BF16).astype(F32), jnp.int32)
    o_ref[...] = lax.bitwise_or(bits, lax.shift_right_logical(bits, jnp.int32(16)))


def peer_act(ps, gates, *, tm=512):
    t, n = ps.shape
    lane_sum = (jnp.arange(n)[:, None] // SC_LANES == jnp.arange(PEER_SEL)[None, :]).astype(BF16)
    return pl.pallas_call(
        _peer_act_kernel,
        grid=(t // tm,),
        in_specs=[
            pl.BlockSpec((tm, n), lambda i: (i, 0)),
            pl.BlockSpec((tm, PEER_SEL), lambda i: (i, 0)),
            pl.BlockSpec((n, PEER_SEL), lambda i: (0, 0)),
        ],
        out_specs=pl.BlockSpec((tm, PEER_SEL), lambda i: (i, 0)),
        out_shape=jax.ShapeDtypeStruct((t, PEER_SEL), jnp.int32),
        compiler_params=_cparams(("parallel",)),
        name="peer_act",
    )(ps, gates, lane_sum)


BATCH_GROUPS = 8


def kernel(x, norm1_g, w_in, rwkv_mu, w0, w_lora_up, a0, a_lora_up, g_lora_up, k_k, k_a, r_k, lnx_g, lnx_b,
           w_proj_a, w_proj_b, w_out, norm2_g, peer_wq, peer_subkeys, peer_u, peer_v, rel_bias, normf_g):
    bsz, seq, d = x.shape
    depth = norm1_g.shape[0]
    groups = BATCH_GROUPS if bsz % BATCH_GROUPS == 0 else 1
    gb = bsz // groups
    tg = gb * seq
    t = bsz * seq
    src = x.reshape(t, d)
    for l in range(depth):
        w_pad = jnp.concatenate([
            w_in[l][:, :COL_A + COL_B_RAW],
            jnp.zeros((d, COL_B - COL_B_RAW), w_in.dtype),
            w_in[l][:, COL_A + COL_B_RAW:]], axis=1).astype(BF16)
        u_packed = _pack_rows(peer_u[l])
        tables = {"v": _pack_rows(peer_v[l])}
        last = l == depth - 1

        def mix(pending, tie=None):
            row0, h2d, ps, gates, idx = pending
            hgx = peer_act(ps, gates)
            if tie is not None:
                tie, hgx = lax.optimization_barrier((tie, hgx))
            return tie, (row0, h2d, peer_expert_mix(hgx, idx, tables["v"]))

        outs = []

        def close(mixed):
            row0, h2d, y2d = mixed
            if last:
                outs.append(final_norm(h2d, y2d, normf_g, out=outs[-1] if outs else None, row0=row0, total_rows=t))
            else:
                outs.append(h2d + y2d)

        halves = gb == 1 and seq % (2 * MOBA_BLOCK) == 0 and (seq // 2) % (SC_WORKERS * PEER_GROUP) == 0

        pending = closing = None
        for g in range(groups):
            p2d = norm_proj(src, norm1_g[l], w_pad, row0=g * tg, rows=tg)
            p3d = p2d.reshape(gb, seq, -1)
            prep = state = None
            for s0, sn in ([(0, seq // 2), (seq // 2, seq // 2)] if halves and g == 0 else [(0, seq)]):
                oa = moba_attention(p3d, rel_bias, q0=s0 // MOBA_BLOCK, nq=sn // MOBA_BLOCK)
                if prep is None:
                    prep = tuple(rwkv_prep(p3d, rwkv_mu[l], w0[l], w_lora_up[l], a0[l], a_lora_up[l], g_lora_up[l],
                                           k_k[l], k_a[l], r_k[l]))
                mixed = None
                if pending is not None:
                    (oa, prep), mixed = mix(pending, (oa, prep))
                if closing is not None:
                    oa, y2d = lax.optimization_barrier((oa, closing[2]))
                    close(closing[:2] + (y2d,))
                    closing = None
                ob, state = rwkv_scan(*prep, lnx_g[l], lnx_b[l], state=state,
                                      c0=s0 // RWKV_CHUNK, nc=sn // RWKV_CHUNK)
                nt = gb * sn
                h2d, xn2 = merge_out(src, oa.reshape(nt, WIDTH), ob.reshape(nt, WIDTH), p2d, w_proj_a[l], w_proj_b[l],
                                     w_out[l], norm2_g[l], row0=g * tg + s0, prow0=s0)
                idx, gates = peer_route(xn2, peer_wq[l], peer_subkeys[l])
                if mixed is not None:
                    idx, y2d = lax.optimization_barrier((idx, mixed[2]))
                    closing = mixed[:2] + (y2d,)
                pending = (g * tg + s0, h2d, peer_expert_dots(xn2, idx, u_packed), gates, idx)
        if closing is not None:
            close(closing)
        close(mix(pending)[1])
        src = outs[-1] if last else jnp.concatenate(outs, axis=0)
    return src.reshape(bsz, seq, d)
```

```python
import functools
import math

import jax
import jax.numpy as jnp
from jax import lax
from jax.experimental import pallas as pl
from jax.experimental.pallas import tpu as pltpu

F32 = jnp.float32
BF16 = jnp.bfloat16
HI = lax.Precision.HIGHEST

LANES = 128
HEAD_DIM = 64
HEADS = 8
PAIRS = HEADS // 2
WIDTH = HEADS * HEAD_DIM
MOBA_BLOCK = 256
MOBA_TOPK = 3
MOBA_LO = 64
REL_BUCKETS = 32
REL_MAX_DIST = 128
DECAY_LORA = 64
AAA_LORA = 64
GATE_LORA = 160
GN_EPS = 64e-5
RMS_EPS = 1e-6
NEG = -1e30
RWKV_CHUNK = 64
COL_A = 3 * WIDTH
COL_B_RAW = 3 * WIDTH + DECAY_LORA + AAA_LORA + GATE_LORA
COL_B = 4 * WIDTH
COL_G_OFF = COL_A + COL_B
VMEM_LIMIT = 56 * 1024 * 1024


def _cparams(sem):
    return pltpu.CompilerParams(dimension_semantics=sem, vmem_limit_bytes=VMEM_LIMIT)


def _norm_proj_kernel(x_ref, g_ref, w_ref, o_ref, xn_ref):
    @pl.when(pl.program_id(1) == 0)
    def _():
        x = x_ref[...]
        ms = jnp.mean(x * x, axis=-1, keepdims=True)
        xn_ref[...] = (x * lax.rsqrt(ms + RMS_EPS) * g_ref[...]).astype(xn_ref.dtype)

    o_ref[...] = jnp.dot(xn_ref[...], w_ref[...], preferred_element_type=F32).astype(o_ref.dtype)


def norm_proj(x2d, g, w, *, row0=0, rows=None, tm=512, tn=512, out_dtype=F32):
    d = x2d.shape[1]
    t = x2d.shape[0] if rows is None else rows
    n = w.shape[1]
    r0 = row0 // tm
    return pl.pallas_call(
        _norm_proj_kernel,
        grid=(t // tm, n // tn),
        in_specs=[
            pl.BlockSpec((tm, d), lambda i, j: (r0 + i, 0)),
            pl.BlockSpec((1, d), lambda i, j: (0, 0)),
            pl.BlockSpec((d, tn), lambda i, j: (0, j)),
        ],
        out_specs=pl.BlockSpec((tm, tn), lambda i, j: (i, j)),
        out_shape=jax.ShapeDtypeStruct((t, n), out_dtype),
        scratch_shapes=[pltpu.VMEM((tm, d), w.dtype)],
        compiler_params=_cparams(("parallel", "arbitrary")),
        name="norm_proj",
    )(x2d, g.reshape(1, d), w)


def _rel_bucket(dist):
    n = jnp.maximum(dist, 0)
    max_exact = REL_BUCKETS // 2
    nf = jnp.maximum(n, 1).astype(F32)
    large = max_exact + (jnp.log(nf / max_exact) / math.log(REL_MAX_DIST / max_exact)
                         * (REL_BUCKETS - max_exact)).astype(jnp.int32)
    large = jnp.minimum(large, REL_BUCKETS - 1)
    return jnp.where(n < max_exact, n, large)


def _moba_kernel(q_ref, k_ref, v_ref, bown_ref, bprev_ref, bfar_ref, o_ref,
                 kb_ref, vb_ref, kbar_ref, *, n_blocks, q0):
    qb = pl.program_id(2) + q0
    blk = MOBA_BLOCK
    scale = 1.0 / math.sqrt(HEAD_DIM)

    rows2 = 2 * blk
    nt = (((1,), (1,)), ((), ()))

    @pl.when(pl.program_id(2) == 0)
    def _():
        kbar_ref[...] = jnp.zeros_like(kbar_ref)
        lane_b = lax.broadcasted_iota(jnp.int32, (blk, LANES), 1)
        for n in range(n_blocks):
            kblk = k_ref[0, n * blk:(n + 1) * blk, :]
            kbar_ref[n:n + 1, :] = jnp.mean(kblk, axis=0, keepdims=True)
            kb_ref[n * blk:(n + 1) * blk, 0:LANES] = kblk.astype(BF16)
            kb_ref[n * blk:(n + 1) * blk, LANES:] = ((lane_b == n) | (lane_b == MOBA_LO + n)).astype(BF16)
        vb_ref[...] = v_ref[0].astype(BF16)

    q2 = q_ref[0]
    first = lax.broadcasted_iota(jnp.int32, (blk, LANES), 1) < HEAD_DIM
    qh = jnp.concatenate([jnp.where(first, q2, 0.0), jnp.where(first, 0.0, q2)], axis=0)
    lane = lax.broadcasted_iota(jnp.int32, (rows2, LANES), 1)
    rowi = lax.broadcasted_iota(jnp.int32, (rows2, LANES), 0)
    gate = lax.dot_general(qh.astype(BF16), kbar_ref[...].astype(BF16), nt, preferred_element_type=F32)
    g = jnp.where(lane < qb, gate, -jnp.inf)
    chosen = lane < 0
    lane_f = lane.astype(F32)
    for _ in range(MOBA_TOPK):
        m = jnp.max(g, axis=1, keepdims=True)
        idx = jnp.min(jnp.where(g == m, lane_f, float(LANES)), axis=1, keepdims=True)
        hit = (lane_f == idx) & (m > -jnp.inf)
        chosen = chosen | hit
        g = jnp.where(hit, -jnp.inf, g)
    nfar = qb - 1
    bfar = jnp.where(rowi < blk, bfar_ref[0, 0:1, 0:1], bfar_ref[1, 0:1, 0:1])
    bhi = bfar.astype(BF16).astype(F32)
    madd = jnp.where(lane < nfar, jnp.where(chosen, bhi, NEG),
                     jnp.where(lane == nfar, jnp.where(chosen, 0.0, NEG),
                               jnp.where((lane >= MOBA_LO) & (lane - MOBA_LO < nfar), bfar - bhi, 0.0)))
    q_aug = jnp.concatenate([(qh * scale).astype(BF16), madd.astype(BF16)], axis=1)

    prev0 = pl.multiple_of(jnp.maximum(nfar, 0) * blk, blk)
    own0 = pl.multiple_of(qb * blk, blk)
    s_prev = (lax.dot_general(q_aug, kb_ref[pl.ds(prev0, blk), :], nt, preferred_element_type=F32)
              + bprev_ref[...].reshape(rows2, blk) + jnp.where(qb > 0, 0.0, NEG))
    s_own = (lax.dot_general(q_aug, kb_ref[pl.ds(own0, blk), :], nt, preferred_element_type=F32)
             + bown_ref[...].reshape(rows2, blk))
    r = lax.broadcasted_iota(jnp.int32, (rows2, blk), 0)
    c = lax.broadcasted_iota(jnp.int32, (rows2, blk), 1)
    s_own = jnp.where(lax.bitwise_and(r, blk - 1) >= c, s_own, NEG)
    s = jnp.concatenate([s_prev, s_own], axis=1)
    m_i = jnp.max(s, axis=1, keepdims=True)
    p = jnp.exp(s - m_i)
    l_i = jnp.sum(p, axis=1, keepdims=True)
    v0 = jnp.concatenate([vb_ref[pl.ds(prev0, blk), :], vb_ref[pl.ds(own0, blk), :]], axis=0)
    acc = jnp.dot(p.astype(BF16), v0, preferred_element_type=F32)

    def body(it, carry):
        m_i, l_i, acc = carry
        k0 = pl.multiple_of(it * rows2, rows2)
        s = lax.dot_general(q_aug, kb_ref[pl.ds(k0, rows2), :], nt, preferred_element_type=F32)
        tail = jnp.where(2 * it + 1 < nfar, 0.0, NEG)
        s = jnp.concatenate([s[:, :blk], s[:, blk:] + tail], axis=1)
        m_new = jnp.maximum(m_i, jnp.max(s, axis=1, keepdims=True))
        alpha = jnp.exp(m_i - m_new)
        p = jnp.exp(s - m_new)
        l_new = alpha * l_i + jnp.sum(p, axis=1, keepdims=True)
        acc_new = alpha * acc + jnp.dot(p.astype(BF16), vb_ref[pl.ds(k0, rows2), :], preferred_element_type=F32)
        return m_new, l_new, acc_new

    m_i, l_i, acc = lax.fori_loop(0, (jnp.maximum(nfar, 0) + 1) // 2, body, (m_i, l_i, acc))
    out = acc / l_i
    o_ref[0] = jnp.where(first, out[:blk], out[blk:])


def moba_attention(p3d, rel_bias, *, q0=0, nq=None):
    bsz, seq, _ = p3d.shape
    blk = MOBA_BLOCK
    n_blocks = seq // blk
    nq = n_blocks - q0 if nq is None else nq
    assert n_blocks <= MOBA_LO and seq % blk == 0
    span = 2 * blk
    by_dist = rel_bias[:, _rel_bucket(jnp.arange(span))].astype(F32)
    shift = jnp.arange(span)

    def toeplitz(c):
        k = jnp.where(shift < blk, shift, shift - span)
        s = by_dist[:, jnp.clip(c - k, 0, span - 1)]
        tiled = jnp.tile(s, (1, blk))[:, :blk * (span - 1)]
        return tiled.reshape(HEADS, blk, span - 1)[:, :, :blk]

    bias_own = toeplitz(0)
    bias_prev = toeplitz(blk)
    bias_far = jnp.broadcast_to(rel_bias[:, REL_BUCKETS - 1].astype(F32)[:, None, None], (HEADS, 8, LANES))
    kern = functools.partial(_moba_kernel, n_blocks=n_blocks, q0=q0)
    return pl.pallas_call(
        kern,
        grid=(bsz, PAIRS, nq),
        in_specs=[
            pl.BlockSpec((1, blk, LANES), lambda b, h, i: (b, q0 + i, h)),
            pl.BlockSpec((1, seq, LANES), lambda b, h, i: (b, 0, PAIRS + h)),
            pl.BlockSpec((1, seq, LANES), lambda b, h, i: (b, 0, 2 * PAIRS + h)),
            pl.BlockSpec((2, blk, blk), lambda b, h, i: (h, 0, 0)),
            pl.BlockSpec((2, blk, blk), lambda b, h, i: (h, 0, 0)),
            pl.BlockSpec((2, 8, LANES), lambda b, h, i: (h, 0, 0)),
        ],
        out_specs=pl.BlockSpec((1, blk, LANES), lambda b, h, i: (b, i, h)),
        out_shape=jax.ShapeDtypeStruct((bsz, nq * blk, WIDTH), F32),
        scratch_shapes=[
            pltpu.VMEM((seq, 2 * LANES), BF16),
            pltpu.VMEM((seq, LANES), BF16),
            pltpu.VMEM((LANES, LANES), F32),
        ],
        compiler_params=_cparams(("parallel", "parallel", "arbitrary")),
        name="moba",
    )(p3d, p3d, p3d, bias_own, bias_prev, bias_far)


def _shifted(x, carry_row):
    rows = lax.broadcasted_iota(jnp.int32, x.shape, 0)
    return jnp.where(rows == 0, carry_row, pltpu.roll(x, 1, axis=0))


def _rwkv_prep_kernel(pr_ref, pk_ref, pv_ref, pl_ref, mu_ref, vec_ref, ww_ref, wa_ref, wg_ref,
                      bd_ref, tri_ref,
                      rt_ref, kt_ref, kd_ref, bd_out_ref, v_ref, g_ref, bonus_ref, pend_ref,
                      carry_ref, *, chunk):
    @pl.when(pl.program_id(1) == 0)
    def _():
        carry_ref[...] = jnp.zeros_like(carry_ref)

    def mix(ref, j):
        x = ref[0]
        mu = mu_ref[0:1, j * WIDTH:(j + 1) * WIDTH]
        prev = _shifted(x, carry_ref[0:1, j * WIDTH:(j + 1) * WIDTH])
        carry_ref[0:1, j * WIDTH:(j + 1) * WIDTH] = x[x.shape[0] - 1:, :]
        return x + mu * (prev - x)

    r = mix(pr_ref, 0)
    k = mix(pk_ref, 1)
    v = mix(pv_ref, 2)
    lo = mix(pl_ref, 3)
    w0, a0, k_k, k_a, r_k = (vec_ref[i:i + 1, :] for i in range(5))
    xwa = lo[:, 0:LANES]
    xg = lo[:, LANES:3 * LANES]
    lw = jnp.dot(jnp.tanh(xwa), ww_ref[...], precision=HI, preferred_element_type=F32)
    la = jnp.dot(xwa, wa_ref[...], precision=HI, preferred_element_type=F32)
    g = jnp.dot(jax.nn.sigmoid(xg), wg_ref[...], precision=HI, preferred_element_type=F32)
    z = -(w0 + lw)
    softplus = jnp.maximum(z, 0.0) + jnp.log(1.0 + jnp.exp(-jnp.abs(z)))
    logw = -jnp.exp(-softplus - 0.5)
    a = jax.nn.sigmoid(a0 + la)
    kk = k * k_k
    ss = jnp.dot(kk * kk, bd_ref[...], precision=HI, preferred_element_type=F32)
    kk = kk / jnp.maximum(jnp.sqrt(ss), 1e-12)
    k2 = k * (1.0 + (a - 1.0) * k_a)
    rk = jnp.dot(r * k2 * r_k, bd_ref[...], precision=HI, preferred_element_type=F32)
    cs = jnp.dot(tri_ref[...], logw, precision=HI, preferred_element_type=F32)
    e_pos = jnp.exp(cs)
    e_neg = jnp.exp(-cs)
    rt_ref[0] = r * e_pos
    kt_ref[0] = kk * jnp.exp(cs - logw)
    kd_ref[0] = k2 * e_neg
    bd_out_ref[0] = kk * a * e_neg
    v_ref[0] = v
    g_ref[0] = g
    bonus_ref[0] = rk * v
    ts = e_pos.shape[0]
    for c in range(ts // chunk):
        pend_ref[0, c:c + 1, :] = e_pos[(c + 1) * chunk - 1:(c + 1) * chunk, :]


def rwkv_prep(p3d, rwkv_mu, w0, w_lora_up, a0, a_lora_up, g_lora_up, k_k, k_a, r_k, *, ts=512):
    bsz, seq, _ = p3d.shape
    chunk = RWKV_CHUNK
    ts = min(ts, seq)
    mu = jnp.pad(rwkv_mu, (0, COL_B - COL_B_RAW)).reshape(1, COL_B)
    vec = jnp.stack([w0, a0, k_k, k_a, r_k.reshape(-1)] + [jnp.zeros_like(w0)] * 3).astype(F32)
    ww = jnp.zeros((LANES, WIDTH), F32).at[:DECAY_LORA].set(w_lora_up)
    wa = jnp.zeros((LANES, WIDTH), F32).at[DECAY_LORA:DECAY_LORA + AAA_LORA].set(a_lora_up)
    wg = jnp.zeros((2 * LANES, WIDTH), F32).at[:GATE_LORA].set(g_lora_up)
    hid = jnp.arange(WIDTH) // HEAD_DIM
    bd = (hid[:, None] == hid[None, :]).astype(F32)
    tix = jnp.arange(ts)
    tri = ((tix[:, None] // chunk == tix[None, :] // chunk) & (tix[None, :] <= tix[:, None])).astype(F32)
    c0 = COL_A // WIDTH
    big = jax.ShapeDtypeStruct((bsz, seq, WIDTH), F32)
    wspec = lambda shape: pl.BlockSpec(shape, lambda b, i: (0, 0))
    ospec = pl.BlockSpec((1, ts, WIDTH), lambda b, i: (b, i, 0))
    return pl.pallas_call(
        functools.partial(_rwkv_prep_kernel, chunk=chunk),
        grid=(bsz, seq // ts),
        in_specs=[
            pl.BlockSpec((1, ts, WIDTH), lambda b, i: (b, i, c0)),
            pl.BlockSpec((1, ts, WIDTH), lambda b, i: (b, i, c0 + 1)),
            pl.BlockSpec((1, ts, WIDTH), lambda b, i: (b, i, c0 + 2)),
            pl.BlockSpec((1, ts, WIDTH), lambda b, i: (b, i, c0 + 3)),
            wspec((1, COL_B)), wspec((8, WIDTH)), wspec((LANES, WIDTH)), wspec((LANES, WIDTH)),
            wspec((2 * LANES, WIDTH)), wspec((WIDTH, WIDTH)), wspec((ts, ts)),
        ],
        out_specs=[ospec] * 7 + [pl.BlockSpec((1, ts // chunk, WIDTH), lambda b, i: (b, i, 0))],
        out_shape=[big] * 7 + [jax.ShapeDtypeStruct((bsz, seq // chunk, WIDTH), F32)],
        scratch_shapes=[pltpu.VMEM((8, COL_B), F32)],
        compiler_params=_cparams(("parallel", "arbitrary")),
        name="rwkv_prep",
    )(p3d, p3d, p3d, p3d, mu, vec, ww, wa, wg, bd, tri)


def _rwkv_scan_kernel(rt_ref, kt_ref, kd_ref, bd_ref, v_ref, g_ref, bonus_ref, pend_ref, ln_ref, sin_ref,
                      o_ref, state_ref, *, chunk, prec):
    @pl.when(pl.program_id(1) == 0)
    def _():
        state_ref[...] = sin_ref[...]

    c2 = 2 * chunk
    lane = lax.broadcasted_iota(jnp.int32, (chunk, LANES), 1)
    first = lane < HEAD_DIM
    row = lax.broadcasted_iota(jnp.int32, (c2, c2), 0)
    col = lax.broadcasted_iota(jnp.int32, (c2, c2), 1)
    eye = (row == col).astype(F32)
    hrow = lax.broadcasted_iota(jnp.int32, (LANES, LANES), 0) // HEAD_DIM
    hcol = lax.broadcasted_iota(jnp.int32, (LANES, LANES), 1) // HEAD_DIM
    head_mean = jnp.where(hrow == hcol, 1.0 / HEAD_DIM, 0.0).astype(F32)
    nt = (((1,), (1,)), ((), ()))
    tn = (((0,), (0,)), ((), ()))
    dot = functools.partial(jnp.dot, precision=prec, preferred_element_type=F32)
    dotg = functools.partial(lax.dot_general, precision=prec, preferred_element_type=F32)

    def stack(x):
        return jnp.concatenate([jnp.where(first, x, 0.0), jnp.where(first, 0.0, x)], axis=0)

    pairs = range(PAIRS)
    sls = [slice(hp * LANES, (hp + 1) * LANES) for hp in pairs]
    rs, ks, kds, bs, vs = ([stack(ref[0, :, sl]) for sl in sls] for ref in (rt_ref, kt_ref, kd_ref, bd_ref, v_ref))
    hts = [state_ref[0, hp] for hp in pairs]
    big = [dotg(jnp.concatenate([ks[hp], rs[hp]], axis=0), jnp.concatenate([bs[hp], kds[hp]], axis=0), nt)
           for hp in pairs]
    a_b = [jnp.where(row > col, big[hp][0:c2, 0:c2], 0.0) for hp in pairs]
    a_k = [jnp.where(row > col, big[hp][0:c2, c2:], 0.0) for hp in pairs]
    a_rb = [jnp.where(row >= col, big[hp][c2:, 0:c2], 0.0) for hp in pairs]
    a_rk = [jnp.where(row >= col, big[hp][c2:, c2:], 0.0) for hp in pairs]
    kh = [dotg(jnp.concatenate([ks[hp], rs[hp]], axis=0), hts[hp], nt) for hp in pairs]
    av = [dot(jnp.concatenate([a_k[hp], a_rk[hp]], axis=0), vs[hp]) for hp in pairs]
    vk = [dotg(vs[hp], kds[hp], tn) for hp in pairs]
    inv = [eye - a_b[hp] for hp in pairs]
    pw = [dot(a_b[hp], a_b[hp]) for hp in pairs]
    n_sq = int(math.log2(chunk)) - 1
    for lvl in range(n_sq):
        if lvl + 1 < n_sq:
            both = [dot(jnp.concatenate([inv[hp], pw[hp]], axis=0), pw[hp]) for hp in pairs]
            inv = [inv[hp] + both[hp][0:c2] for hp in pairs]
            pw = [both[hp][c2:] for hp in pairs]
        else:
            inv = [inv[hp] + dot(inv[hp], pw[hp]) for hp in pairs]
    us = [dot(inv[hp], kh[hp][0:c2] + av[hp][0:c2]) for hp in pairs]
    ub = [dotg(us[hp], bs[hp], tn) for hp in pairs]
    au = [dot(a_rb[hp], us[hp]) for hp in pairs]
    for hp in pairs:
        sl = sls[hp]
        pend = pend_ref[0, 0, 0:1, sl]
        state_ref[0, hp] = (hts[hp] + vk[hp] - ub[hp]) * pend
        os_ = kh[hp][c2:] + av[hp][c2:] - au[hp]
        o = os_[0:chunk] + os_[chunk:]
        mu = jnp.dot(o, head_mean, precision=HI, preferred_element_type=F32)
        d = o - mu
        var = jnp.dot(d * d, head_mean, precision=HI, preferred_element_type=F32)
        on = d * lax.rsqrt(var + GN_EPS) * ln_ref[0:1, sl] + ln_ref[1:2, sl]
        o_ref[0, :, sl] = (on + bonus_ref[0, :, sl]) * g_ref[0, :, sl]


def rwkv_scan(rt, kt, kd, bd, v, g, bonus, pend, lnx_g, lnx_b, *, state=None, c0=0, nc=None, prec=None):
    bsz, seq, _ = rt.shape
    chunk = RWKV_CHUNK
    n_chunks = seq // chunk
    nc = n_chunks - c0 if nc is None else nc
    ln = jnp.stack([lnx_g, lnx_b] + [jnp.zeros_like(lnx_g)] * 6).astype(F32)
    pend4 = pend.reshape(bsz, n_chunks, 1, WIDTH)
    if state is None:
        state = jnp.zeros((bsz, PAIRS, LANES, LANES), F32)
    spec = pl.BlockSpec((1, chunk, WIDTH), lambda b, c: (b, c0 + c, 0))
    sspec = pl.BlockSpec((1, PAIRS, LANES, LANES), lambda b, c: (b, 0, 0, 0))
    return pl.pallas_call(
        functools.partial(_rwkv_scan_kernel, chunk=chunk, prec=prec),
        grid=(bsz, nc),
        in_specs=[spec] * 7 + [
            pl.BlockSpec((1, 1, 1, WIDTH), lambda b, c: (b, c0 + c, 0, 0)),
            pl.BlockSpec((8, WIDTH), lambda b, c: (0, 0)),
            sspec,
        ],
        out_specs=[pl.BlockSpec((1, chunk, WIDTH), lambda b, c: (b, c, 0)), sspec],
        out_shape=[jax.ShapeDtypeStruct((bsz, nc * chunk, WIDTH), F32),
                   jax.ShapeDtypeStruct((bsz, PAIRS, LANES, LANES), F32)],
        compiler_params=_cparams(("parallel", "arbitrary")),
        name="rwkv_scan",
    )(rt, kt, kd, bd, v, g, bonus, pend4, ln, state)


def _merge_kernel(x_ref, oa_ref, ob_ref, ga_ref, gb_ref, wa_ref, wb_ref, wo_ref, g2_ref,
                  h_ref, xn_ref, acc_ref):
    j = pl.program_id(1)

    @pl.when(j == 0)
    def _():
        acc_ref[...] = x_ref[...]

    ya = jnp.dot(oa_ref[...].astype(BF16), wa_ref[...], preferred_element_type=F32)
    yb = jnp.dot(ob_ref[...].astype(BF16), wb_ref[...], preferred_element_type=F32)
    y = jax.nn.sigmoid(ga_ref[...]) * ya + jax.nn.sigmoid(gb_ref[...]) * yb
    acc_ref[...] += jnp.dot(y.astype(BF16), wo_ref[...], preferred_element_type=F32)

    @pl.when(j == pl.num_programs(1) - 1)
    def _():
        h = acc_ref[...]
        h_ref[...] = h
        ms = jnp.mean(h * h, axis=-1, keepdims=True)
        xn_ref[...] = _pack_halves(h * lax.rsqrt(ms + RMS_EPS) * g2_ref[...])


def _pack_halves(x):
    half = x.shape[1] // 2
    lo = lax.bitcast_convert_type(x[:, :half].astype(BF16).astype(F32), jnp.int32)
    hi = lax.bitcast_convert_type(x[:, half:].astype(BF16).astype(F32), jnp.int32)
    return lax.bitwise_or(lax.shift_right_logical(lo, jnp.int32(16)), hi)


def _unpack_halves(words):
    lo, hi = _unpack_words(words)
    return jnp.concatenate([lo, hi], axis=1)


def merge_out(x2d, oa, ob, p2d, w_proj_a, w_proj_b, w_out, norm2_g, *, row0=0, prow0=0, tm=512):
    t, d = oa.shape[0], x2d.shape[1]
    r0 = row0 // tm
    p0 = prow0 // tm
    tn = WIDTH
    nj = d // tn
    g0 = COL_G_OFF // tn
    return pl.pallas_call(
        _merge_kernel,
        grid=(t // tm, nj),
        in_specs=[
            pl.BlockSpec((tm, d), lambda i, j: (r0 + i, 0)),
            pl.BlockSpec((tm, WIDTH), lambda i, j: (i, 0)),
            pl.BlockSpec((tm, WIDTH), lambda i, j: (i, 0)),
            pl.BlockSpec((tm, tn), lambda i, j: (p0 + i, g0 + j)),
            pl.BlockSpec((tm, tn), lambda i, j: (p0 + i, g0 + nj + j)),
            pl.BlockSpec((WIDTH, tn), lambda i, j: (0, j)),
            pl.BlockSpec((WIDTH, tn), lambda i, j: (0, j)),
            pl.BlockSpec((tn, d), lambda i, j: (j, 0)),
            pl.BlockSpec((1, d), lambda i, j: (0, 0)),
        ],
        out_specs=[pl.BlockSpec((tm, d), lambda i, j: (i, 0)), pl.BlockSpec((tm, d // 2), lambda i, j: (i, 0))],
        out_shape=[jax.ShapeDtypeStruct((t, d), F32), jax.ShapeDtypeStruct((t, d // 2), jnp.int32)],
        scratch_shapes=[pltpu.VMEM((tm, d), F32)],
        compiler_params=_cparams(("parallel", "arbitrary")),
        name="merge_out",
    )(x2d, oa, ob, p2d, p2d, w_proj_a.astype(BF16), w_proj_b.astype(BF16), w_out.astype(BF16),
      norm2_g.reshape(1, d))


PEER_HEADS = 8
PEER_NKEYS = 128
PEER_TOPK = 16
PEER_HALF = 128


def _topk_rows(s, k):
    n = s.shape[0]
    rows = lax.broadcasted_iota(jnp.int32, s.shape, 0).astype(F32)
    vals, ids = [], []
    for _ in range(k):
        m = jnp.max(s, axis=0, keepdims=True)
        first = jnp.min(jnp.where(s == m, rows, float(n)), axis=0, keepdims=True)
        vals.append(m)
        ids.append(first)
        s = jnp.where(rows == first, -jnp.inf, s)
    return jnp.concatenate(vals, axis=0), jnp.concatenate(ids, axis=0)


def _take_rows(table, ids):
    rows = lax.broadcasted_iota(jnp.int32, table.shape, 0).astype(F32)
    return jnp.sum(jnp.where(rows == ids, table, 0.0), axis=0, keepdims=True)


def _peer_route_kernel(xn_ref, wq_ref, sk_ref, idx_ref, gate_ref, *, prec):
    tt = xn_ref.shape[0]
    k = PEER_TOPK
    xn = _unpack_halves(xn_ref[...]) if xn_ref.dtype == jnp.int32 else xn_ref[...]
    q = jnp.dot(xn.astype(wq_ref.dtype), wq_ref[...], precision=prec, preferred_element_type=F32)
    nt = (((1,), (1,)), ((), ()))
    idx_rows, gate_rows = [], []
    half = k // 2
    for h in range(PEER_HEADS):
        tops = []
        for p in range(2):
            c0 = (h * 2 + p) * PEER_HALF
            s = lax.dot_general(sk_ref[h, p].astype(wq_ref.dtype), q[:, c0:c0 + PEER_HALF].astype(wq_ref.dtype),
                                nt, precision=prec, preferred_element_type=F32)
            tops.append(_topk_rows(s, k))
        (s0, i0), (s1, i1) = tops
        cs = [s0[0:1] + s1] + [s0[i:i + 1] + s1[0:half] for i in range(1, half)] + [s0[half:] + s1[0:1]]
        best_s, pos = _topk_rows(jnp.concatenate(cs, axis=0), k)
        mid = jnp.floor((pos - k) * (1.0 / half))
        end_mid = float(k + (half - 1) * half)
        i_rank = jnp.where(pos < k, 0.0, jnp.where(pos < end_mid, 1.0 + mid, pos - (end_mid - half)))
        j_rank = jnp.where(pos < k, pos, jnp.where(pos < end_mid, (pos - k) - half * mid, 0.0))
        ids = [_take_rows(i0, i_rank[n:n + 1]) * PEER_NKEYS + _take_rows(i1, j_rank[n:n + 1]) for n in range(k)]
        e = jnp.exp(best_s - best_s[0:1])
        gate_rows.append(e / jnp.sum(e, axis=0, keepdims=True))
        idx_rows.append(jnp.concatenate(ids, axis=0).astype(jnp.int32))
    idx_ref[...] = jnp.concatenate(idx_rows, axis=0).T
    gate_ref[...] = jnp.concatenate(gate_rows, axis=0).T


def peer_route(xn2d, peer_wq, peer_subkeys, *, tt=256, prec=None, wdtype=BF16):
    t, dx = xn2d.shape
    d, nq = peer_wq.shape
    n_sel = PEER_HEADS * PEER_TOPK
    return pl.pallas_call(
        functools.partial(_peer_route_kernel, prec=prec),
        grid=(t // tt,),
        in_specs=[
            pl.BlockSpec((tt, dx), lambda i: (i, 0)),
            pl.BlockSpec((d, nq), lambda i: (0, 0)),
            pl.BlockSpec((PEER_HEADS, 2, PEER_NKEYS, PEER_HALF), lambda i: (0, 0, 0, 0)),
        ],
        out_specs=[pl.BlockSpec((tt, n_sel), lambda i: (i, 0))] * 2,
        out_shape=[jax.ShapeDtypeStruct((t, n_sel), jnp.int32), jax.ShapeDtypeStruct((t, n_sel), F32)],
        compiler_params=_cparams(("parallel",)),
        name="peer_route",
    )(xn2d, peer_wq.astype(wdtype), peer_subkeys)


def _final_kernel(h_ref, y_ref, g_ref, *rest):
    o_ref = rest[-1]
    h = h_ref[...] + y_ref[...]
    ms = jnp.mean(h * h, axis=-1, keepdims=True)
    o_ref[...] = h * lax.rsqrt(ms + RMS_EPS) * g_ref[...]


def final_norm(h2d, y2d, g, *, out=None, row0=0, total_rows=None, tm=1024):
    t, d = h2d.shape
    total = t if total_rows is None else total_rows
    r0 = row0 // tm
    spec = pl.BlockSpec((tm, d), lambda i: (i, 0))
    in_specs = [spec, spec, pl.BlockSpec((1, d), lambda i: (0, 0))]
    args = [h2d, y2d, g.reshape(1, d)]
    aliases = {}
    if out is not None:
        in_specs.append(pl.BlockSpec(memory_space=pl.ANY))
        args.append(out)
        aliases = {3: 0}
    return pl.pallas_call(
        _final_kernel,
        grid=(t // tm,),
        in_specs=in_specs,
        out_specs=pl.BlockSpec((tm, d), lambda i: (r0 + i, 0)),
        out_shape=jax.ShapeDtypeStruct((total, d), F32),
        input_output_aliases=aliases,
        compiler_params=_cparams(("parallel",)),
        name="final_norm",
    )(*args)


SC_CORES = 2
SC_SUBCORES = 16
SC_LANES = 16
SC_WORKERS = SC_CORES * SC_SUBCORES
PEER_SEL = PEER_HEADS * PEER_TOPK
PEER_ROWS = 32
PEER_PARTS = PEER_SEL // PEER_ROWS
PEER_NBUF = 4
PEER_GROUP = 32
PEER_BF16_RUN = 4


def _pack_rows(w):
    half = w.shape[1] // 2
    bits = lax.bitcast_convert_type(w.astype(BF16), jnp.uint16).astype(jnp.uint32)
    return lax.bitcast_convert_type(bits[:, :half] | (bits[:, half:] << 16), jnp.int32)


def _unpack_words(w):
    lo = lax.bitcast_convert_type(lax.shift_left(w, jnp.int32(16)), F32)
    hi = lax.bitcast_convert_type(lax.bitwise_and(w, jnp.int32(-65536)), F32)
    return lo, hi


def _packed_dot(a_words, b_words):
    from jax.experimental.pallas import tpu_sc as plsc
    prods = [plsc.bitcast(a, BF16) * plsc.bitcast(b, BF16) for a, b in zip(a_words, b_words)]
    while len(prods) > 1:
        prods = [prods[k] + prods[k + 1] for k in range(0, len(prods), 2)]
    return _unpack_words(plsc.bitcast(prods[0], jnp.int32))


def _sc_mesh():
    from jax.experimental.pallas import tpu_sc as plsc
    return plsc.VectorSubcoreMesh(core_axis_name="c", subcore_axis_name="s",
                                  num_cores=SC_CORES, num_subcores=SC_SUBCORES)


def _sc_loop(n, body, carry):
    from jax.experimental.pallas import tpu_sc as plsc
    return plsc.parallel_loop(0, n, carry=carry)(body)


def _worker_base(tokens_per_worker):
    return (lax.axis_index("s") * SC_CORES + lax.axis_index("c")) * tokens_per_worker


def _gather_compute_loop(table_hbm, idx_v, rows_v, sem, stage_v, out_row, osem, grp, compute):
    n_gathers = PEER_PARTS * grp
    ahead = PEER_NBUF - 1

    def gather(j, b):
        i = j // PEER_PARTS if isinstance(j, int) else lax.shift_right_logical(j, PEER_PARTS.bit_length() - 1)
        h = j % PEER_PARTS if isinstance(j, int) else lax.bitwise_and(j, PEER_PARTS - 1)
        ids = idx_v.at[i, pl.ds(pl.multiple_of(h * PEER_ROWS, PEER_ROWS), PEER_ROWS)]
        return pltpu.make_async_copy(table_hbm.at[ids], rows_v.at[b], sem.at[b])

    def put(i, slot):
        return pltpu.make_async_copy(stage_v.at[slot], out_row(i), osem.at[slot])

    for j in range(ahead):
        gather(j, j).start()

    @pl.loop(0, n_gathers)
    def _(j):
        b = lax.bitwise_and(j, PEER_NBUF - 1)
        h = lax.bitwise_and(j, PEER_PARTS - 1)
        i = lax.shift_right_logical(j, PEER_PARTS.bit_length() - 1)
        slot = lax.bitwise_and(i, 1)

        @pl.when((h == 0) & (i >= 2))
        def _():
            put(i - 2, slot).wait()

        @pl.when(j + ahead < n_gathers)
        def _():
            gather(j + ahead, lax.bitwise_and(j + ahead, PEER_NBUF - 1)).start()

        gather(j, b).wait()
        compute(i, h, b, slot)

        @pl.when(h == PEER_PARTS - 1)
        def _():
            put(i, slot).start()

    put(grp - 2, 0).wait()
    put(grp - 1, 1).wait()


def peer_expert_dots(x_packed, idx, u_packed):
    t, half = x_packed.shape
    n_chunks = half // SC_LANES
    tpw = t // SC_WORKERS
    grp = min(PEER_GROUP, tpw)
    rows_tog = 8

    def body(x_hbm, idx_hbm, u_hbm, out_hbm, idx_v, x_v, rows_v, ps_v, sem, osem):
        base = _worker_base(tpw)

        def compute(i, h, b, slot):
            @pl.loop(0, PEER_ROWS // rows_tog)
            def _(rg):
                r0 = rg * rows_tog
                accs = [[None, None] for _ in range(rows_tog)]
                for c0 in range(0, n_chunks, PEER_BF16_RUN):
                    ats = [pl.ds((c0 + k) * SC_LANES, SC_LANES) for k in range(PEER_BF16_RUN)]
                    xw = [x_v[i, at] for at in ats]
                    for r in range(rows_tog):
                        terms = _packed_dot([rows_v[b, r0 + r, at] for at in ats], xw)
                        for k, term in enumerate(terms):
                            accs[r][k] = term if accs[r][k] is None else accs[r][k] + term
                for r in range(rows_tog):
                    at = pl.ds(pl.multiple_of((h * PEER_ROWS + r0 + r) * SC_LANES, SC_LANES), SC_LANES)
                    ps_v[slot, at] = accs[r][0] + accs[r][1]

        @pl.loop(0, tpw // grp)
        def _(g):
            t0 = base + g * grp
            pltpu.sync_copy(idx_hbm.at[pl.ds(t0, grp)], idx_v)
            pltpu.sync_copy(x_hbm.at[pl.ds(t0, grp)], x_v)
            _gather_compute_loop(u_hbm, idx_v, rows_v, sem, ps_v, lambda i: out_hbm.at[t0 + i], osem, grp, compute)

    return pl.kernel(
        body,
        out_type=jax.ShapeDtypeStruct((t, PEER_SEL * SC_LANES), F32),
        mesh=_sc_mesh(),
        scratch_types=[
            pltpu.VMEM((grp, PEER_SEL), jnp.int32),
            pltpu.VMEM((grp, half), jnp.int32),
            pltpu.VMEM((PEER_NBUF, PEER_ROWS, half), jnp.int32),
            pltpu.VMEM((2, PEER_SEL * SC_LANES), F32),
            pltpu.SemaphoreType.DMA((PEER_NBUF,)),
            pltpu.SemaphoreType.DMA((2,)),
        ],
        compiler_params=pltpu.CompilerParams(needs_layout_passes=False),
        name="peer_expert_dots",
    )(x_packed, idx, u_packed)


def peer_expert_mix(hgw, idx, v_packed):
    t = hgw.shape[0]
    half = v_packed.shape[1]
    d = 2 * half
    tpw = t // SC_WORKERS
    grp = min(PEER_GROUP, tpw)
    n_parts = 2
    cpp = half // SC_LANES // n_parts
    from jax.experimental.pallas import tpu_sc as plsc

    def body(hg_hbm, idx_hbm, v_hbm, out_hbm, idx_v, hg_v, rows_v, o_v2, sem, osem):
        base = _worker_base(tpw)

        def compute(i, h, b, slot):
            token = jnp.full((SC_LANES,), i, jnp.int32)
            for part in range(n_parts):
                def rbody(rq, accs):
                    r0 = rq * PEER_BF16_RUN
                    s = [plsc.load_gather(hg_v, [token, jnp.full((SC_LANES,), h * PEER_ROWS + r0 + k, jnp.int32)])
                         for k in range(PEER_BF16_RUN)]
                    new = []
                    for c in range(cpp):
                        at = pl.ds((part * cpp + c) * SC_LANES, SC_LANES)
                        lo, hi = _packed_dot([rows_v[b, r0 + k, at] for k in range(PEER_BF16_RUN)], s)
                        new.append(accs[2 * c] + lo)
                        new.append(accs[2 * c + 1] + hi)
                    return tuple(new)

                accs = _sc_loop(PEER_ROWS // PEER_BF16_RUN, rbody,
                                tuple(jnp.zeros((SC_LANES,), F32) for _ in range(2 * cpp)))
                def store(overwrite):
                    for c in range(cpp):
                        lo_at = pl.ds((part * cpp + c) * SC_LANES, SC_LANES)
                        hi_at = pl.ds(half + (part * cpp + c) * SC_LANES, SC_LANES)
                        if overwrite:
                            o_v2[slot, lo_at] = accs[2 * c]
                            o_v2[slot, hi_at] = accs[2 * c + 1]
                        else:
                            o_v2[slot, lo_at] = o_v2[slot, lo_at] + accs[2 * c]
                            o_v2[slot, hi_at] = o_v2[slot, hi_at] + accs[2 * c + 1]

                pl.when(h == 0)(functools.partial(store, True))
                pl.when(h != 0)(functools.partial(store, False))

        @pl.loop(0, tpw // grp)
        def _(g):
            t0 = base + g * grp
            pltpu.sync_copy(idx_hbm.at[pl.ds(t0, grp)], idx_v)
            pltpu.sync_copy(hg_hbm.at[pl.ds(t0, grp)], hg_v)
            _gather_compute_loop(v_hbm, idx_v, rows_v, sem, o_v2, lambda i: out_hbm.at[t0 + i], osem, grp, compute)

    return pl.kernel(
        body,
        out_type=jax.ShapeDtypeStruct((t, d), F32),
        mesh=_sc_mesh(),
        scratch_types=[
            pltpu.VMEM((grp, PEER_SEL), jnp.int32),
            pltpu.VMEM((grp, PEER_SEL), jnp.int32),
            pltpu.VMEM((PEER_NBUF, PEER_ROWS, half), jnp.int32),
            pltpu.VMEM((2, d), F32),
            pltpu.SemaphoreType.DMA((PEER_NBUF,)),
            pltpu.SemaphoreType.DMA((2,)),
        ],
        compiler_params=pltpu.CompilerParams(needs_layout_passes=False),
        name="peer_expert_mix",
    )(hgw, idx, v_packed)


def _peer_act_kernel(ps_ref, gate_ref, sum_ref, o_ref):
    ps = ps_ref[...]
    sel = sum_ref[...]
    hi = ps.astype(BF16)
    rest = ps - hi.astype(F32)
    mid = rest.astype(BF16)
    lo = (rest - mid.astype(F32)).astype(BF16)
    pre = (jnp.dot(hi, sel, preferred_element_type=F32) + jnp.dot(mid, sel, preferred_element_type=F32)
           + jnp.dot(lo, sel, preferred_element_type=F32))
    hg = 0.5 * pre * (1.0 + lax.erf(pre * (1.0 / math.sqrt(2.0)))) * gate_ref[...]
    bits = lax.bitcast_convert_type(hg.astype(BF16).astype(F32), jnp.int32)
    o_ref[...] = lax.bitwise_or(bits, lax.shift_right_logical(bits, jnp.int32(16)))


def peer_act(ps, gates, *, tm=512):
    t, n = ps.shape
    lane_sum = (jnp.arange(n)[:, None] // SC_LANES == jnp.arange(PEER_SEL)[None, :]).astype(BF16)
    return pl.pallas_call(
        _peer_act_kernel,
        grid=(t // tm,),
        in_specs=[
            pl.BlockSpec((tm, n), lambda i: (i, 0)),
            pl.BlockSpec((tm, PEER_SEL), lambda i: (i, 0)),
            pl.BlockSpec((n, PEER_SEL), lambda i: (0, 0)),
        ],
        out_specs=pl.BlockSpec((tm, PEER_SEL), lambda i: (i, 0)),
        out_shape=jax.ShapeDtypeStruct((t, PEER_SEL), jnp.int32),
        compiler_params=_cparams(("parallel",)),
        name="peer_act",
    )(ps, gates, lane_sum)


BATCH_GROUPS = 8


def kernel(x, norm1_g, w_in, rwkv_mu, w0, w_lora_up, a0, a_lora_up, g_lora_up, k_k, k_a, r_k, lnx_g, lnx_b,
           w_proj_a, w_proj_b, w_out, norm2_g, peer_wq, peer_subkeys, peer_u, peer_v, rel_bias, normf_g):
    bsz, seq, d = x.shape
    depth = norm1_g.shape[0]
    groups = BATCH_GROUPS if bsz % BATCH_GROUPS == 0 else 1
    gb = bsz // groups
    tg = gb * seq
    t = bsz * seq
    src = x.reshape(t, d)
    for l in range(depth):
        w_pad = jnp.concatenate([
            w_in[l][:, :COL_A + COL_B_RAW],
            jnp.zeros((d, COL_B - COL_B_RAW), w_in.dtype),
            w_in[l][:, COL_A + COL_B_RAW:]], axis=1).astype(BF16)
        u_packed = _pack_rows(peer_u[l])
        tables = {"v": _pack_rows(peer_v[l])}
        last = l == depth - 1

        def mix(pending, tie=None):
            row0, h2d, ps, gates, idx = pending
            hgx = peer_act(ps, gates)
            if tie is not None:
                tie, hgx = lax.optimization_barrier((tie, hgx))
            return tie, (row0, h2d, peer_expert_mix(hgx, idx, tables["v"]))

        outs = []

        def close(mixed):
            row0, h2d, y2d = mixed
            if last:
                outs.append(final_norm(h2d, y2d, normf_g, out=outs[-1] if outs else None, row0=row0, total_rows=t))
            else:
                outs.append(h2d + y2d)

        halves = gb == 1 and seq % (2 * MOBA_BLOCK) == 0 and (seq // 2) % (SC_WORKERS * PEER_GROUP) == 0

        pending = closing = None
        for g in range(groups):
            p2d = norm_proj(src, norm1_g[l], w_pad, row0=g * tg, rows=tg)
            p3d = p2d.reshape(gb, seq, -1)
            prep = state = None
            for s0, sn in ([(0, seq // 2), (seq // 2, seq // 2)] if halves and g == 0 else [(0, seq)]):
                oa = moba_attention(p3d, rel_bias, q0=s0 // MOBA_BLOCK, nq=sn // MOBA_BLOCK)
                if prep is None:
                    prep = tuple(rwkv_prep(p3d, rwkv_mu[l], w0[l], w_lora_up[l], a0[l], a_lora_up[l], g_lora_up[l],
                                           k_k[l], k_a[l], r_k[l]))
                mixed = None
                if pending is not None:
                    (oa, prep), mixed = mix(pending, (oa, prep))
                if closing is not None:
                    oa, y2d = lax.optimization_barrier((oa, closing[2]))
                    close(closing[:2] + (y2d,))
                    closing = None
                ob, state = rwkv_scan(*prep, lnx_g[l], lnx_b[l], state=state,
                                      c0=s0 // RWKV_CHUNK, nc=sn // RWKV_CHUNK)
                nt = gb * sn
                h2d, xn2 = merge_out(src, oa.reshape(nt, WIDTH), ob.reshape(nt, WIDTH), p2d, w_proj_a[l], w_proj_b[l],
                                     w_out[l], norm2_g[l], row0=g * tg + s0, prow0=s0)
                idx, gates = peer_route(xn2, peer_wq[l], peer_subkeys[l])
                if mixed is not None:
                    idx, y2d = lax.optimization_barrier((idx, mixed[2]))
                    closing = mixed[:2] + (y2d,)
                pending = (g * tg + s0, h2d, peer_expert_dots(xn2, idx, u_packed), gates, idx)
        if closing is not None:
            close(closing)
        close(mix(pending)[1])
        src = outs[-1] if last else jnp.concatenate(outs, axis=0)
    return src.reshape(bsz, seq, d)
```

```python
import functools
import math

import jax
import jax.numpy as jnp
from jax import lax
from jax.experimental import pallas as pl
from jax.experimental.pallas import tpu as pltpu

F32 = jnp.float32
BF16 = jnp.bfloat16
HI = lax.Precision.HIGHEST

LANES = 128
HEAD_DIM = 64
HEADS = 8
PAIRS = HEADS // 2
WIDTH = HEADS * HEAD_DIM
MOBA_BLOCK = 256
MOBA_TOPK = 3
MOBA_LO = 64
REL_BUCKETS = 32
REL_MAX_DIST = 128
DECAY_LORA = 64
AAA_LORA = 64
GATE_LORA = 160
GN_EPS = 64e-5
RMS_EPS = 1e-6
NEG = -1e30
RWKV_CHUNK = 64
COL_A = 3 * WIDTH
COL_B_RAW = 3 * WIDTH + DECAY_LORA + AAA_LORA + GATE_LORA
COL_B = 4 * WIDTH
COL_G_OFF = COL_A + COL_B
VMEM_LIMIT = 56 * 1024 * 1024


def _cparams(sem):
    return pltpu.CompilerParams(dimension_semantics=sem, vmem_limit_bytes=VMEM_LIMIT)


def _norm_proj_kernel(x_ref, g_ref, w_ref, o_ref, xn_ref):
    @pl.when(pl.program_id(1) == 0)
    def _():
        x = x_ref[...]
        ms = jnp.mean(x * x, axis=-1, keepdims=True)
        xn_ref[...] = (x * lax.rsqrt(ms + RMS_EPS) * g_ref[...]).astype(xn_ref.dtype)

    o_ref[...] = jnp.dot(xn_ref[...], w_ref[...], preferred_element_type=F32).astype(o_ref.dtype)


def norm_proj(x2d, g, w, *, row0=0, rows=None, tm=1024, tn=512, out_dtype=F32):
    d = x2d.shape[1]
    t = x2d.shape[0] if rows is None else rows
    n = w.shape[1]
    r0 = row0 // tm
    return pl.pallas_call(
        _norm_proj_kernel,
        grid=(t // tm, n // tn),
        in_specs=[
            pl.BlockSpec((tm, d), lambda i, j: (r0 + i, 0)),
            pl.BlockSpec((1, d), lambda i, j: (0, 0)),
            pl.BlockSpec((d, tn), lambda i, j: (0, j)),
        ],
        out_specs=pl.BlockSpec((tm, tn), lambda i, j: (i, j)),
        out_shape=jax.ShapeDtypeStruct((t, n), out_dtype),
        scratch_shapes=[pltpu.VMEM((tm, d), w.dtype)],
        compiler_params=_cparams(("parallel", "arbitrary")),
        name="norm_proj",
    )(x2d, g.reshape(1, d), w)


def _rel_bucket(dist):
    n = jnp.maximum(dist, 0)
    max_exact = REL_BUCKETS // 2
    nf = jnp.maximum(n, 1).astype(F32)
    large = max_exact + (jnp.log(nf / max_exact) / math.log(REL_MAX_DIST / max_exact)
                         * (REL_BUCKETS - max_exact)).astype(jnp.int32)
    large = jnp.minimum(large, REL_BUCKETS - 1)
    return jnp.where(n < max_exact, n, large)


def _moba_kernel(q_ref, k_ref, v_ref, bown_ref, bprev_ref, bfar_ref, o_ref,
                 kb_ref, vb_ref, kbar_ref, *, n_blocks, q0):
    qb = pl.program_id(2) + q0
    blk = MOBA_BLOCK
    scale = 1.0 / math.sqrt(HEAD_DIM)

    rows2 = 2 * blk
    nt = (((1,), (1,)), ((), ()))

    @pl.when(pl.program_id(2) == 0)
    def _():
        kbar_ref[...] = jnp.zeros_like(kbar_ref)
        lane_b = lax.broadcasted_iota(jnp.int32, (blk, LANES), 1)
        for n in range(n_blocks):
            kblk = k_ref[0, n * blk:(n + 1) * blk, :]
            kbar_ref[n:n + 1, :] = jnp.mean(kblk, axis=0, keepdims=True)
            kb_ref[n * blk:(n + 1) * blk, 0:LANES] = kblk.astype(BF16)
            kb_ref[n * blk:(n + 1) * blk, LANES:] = ((lane_b == n) | (lane_b == MOBA_LO + n)).astype(BF16)
        vb_ref[...] = v_ref[0].astype(BF16)

    q2 = q_ref[0]
    first = lax.broadcasted_iota(jnp.int32, (blk, LANES), 1) < HEAD_DIM
    qh = jnp.concatenate([jnp.where(first, q2, 0.0), jnp.where(first, 0.0, q2)], axis=0)
    lane = lax.broadcasted_iota(jnp.int32, (rows2, LANES), 1)
    rowi = lax.broadcasted_iota(jnp.int32, (rows2, LANES), 0)
    gate = lax.dot_general(qh.astype(BF16), kbar_ref[...].astype(BF16), nt, preferred_element_type=F32)
    g = jnp.where(lane < qb, gate, -jnp.inf)
    chosen = lane < 0
    lane_f = lane.astype(F32)
    for _ in range(MOBA_TOPK):
        m = jnp.max(g, axis=1, keepdims=True)
        idx = jnp.min(jnp.where(g == m, lane_f, float(LANES)), axis=1, keepdims=True)
        hit = (lane_f == idx) & (m > -jnp.inf)
        chosen = chosen | hit
        g = jnp.where(hit, -jnp.inf, g)
    nfar = qb - 1
    bfar = jnp.where(rowi < blk, bfar_ref[0, 0:1, 0:1], bfar_ref[1, 0:1, 0:1])
    bhi = bfar.astype(BF16).astype(F32)
    madd = jnp.where(lane < nfar, jnp.where(chosen, bhi, NEG),
                     jnp.where(lane == nfar, jnp.where(chosen, 0.0, NEG),
                               jnp.where((lane >= MOBA_LO) & (lane - MOBA_LO < nfar), bfar - bhi, 0.0)))
    q_aug = jnp.concatenate([(qh * scale).astype(BF16), madd.astype(BF16)], axis=1)

    prev0 = pl.multiple_of(jnp.maximum(nfar, 0) * blk, blk)
    own0 = pl.multiple_of(qb * blk, blk)
    s_prev = (lax.dot_general(q_aug, kb_ref[pl.ds(prev0, blk), :], nt, preferred_element_type=F32)
              + bprev_ref[...].reshape(rows2, blk) + jnp.where(qb > 0, 0.0, NEG))
    s_own = (lax.dot_general(q_aug, kb_ref[pl.ds(own0, blk), :], nt, preferred_element_type=F32)
             + bown_ref[...].reshape(rows2, blk))
    r = lax.broadcasted_iota(jnp.int32, (rows2, blk), 0)
    c = lax.broadcasted_iota(jnp.int32, (rows2, blk), 1)
    s_own = jnp.where(lax.bitwise_and(r, blk - 1) >= c, s_own, NEG)
    s = jnp.concatenate([s_prev, s_own], axis=1)
    m_i = jnp.max(s, axis=1, keepdims=True)
    p = jnp.exp(s - m_i)
    l_i = jnp.sum(p, axis=1, keepdims=True)
    v0 = jnp.concatenate([vb_ref[pl.ds(prev0, blk), :], vb_ref[pl.ds(own0, blk), :]], axis=0)
    acc = jnp.dot(p.astype(BF16), v0, preferred_element_type=F32)

    def body(it, carry):
        m_i, l_i, acc = carry
        k0 = pl.multiple_of(it * rows2, rows2)
        s = lax.dot_general(q_aug, kb_ref[pl.ds(k0, rows2), :], nt, preferred_element_type=F32)
        tail = jnp.where(2 * it + 1 < nfar, 0.0, NEG)
        s = jnp.concatenate([s[:, :blk], s[:, blk:] + tail], axis=1)
        m_new = jnp.maximum(m_i, jnp.max(s, axis=1, keepdims=True))
        alpha = jnp.exp(m_i - m_new)
        p = jnp.exp(s - m_new)
        l_new = alpha * l_i + jnp.sum(p, axis=1, keepdims=True)
        acc_new = alpha * acc + jnp.dot(p.astype(BF16), vb_ref[pl.ds(k0, rows2), :], preferred_element_type=F32)
        return m_new, l_new, acc_new

    m_i, l_i, acc = lax.fori_loop(0, (jnp.maximum(nfar, 0) + 1) // 2, body, (m_i, l_i, acc))
    out = acc / l_i
    o_ref[0] = jnp.where(first, out[:blk], out[blk:])


def moba_attention(p3d, rel_bias, *, q0=0, nq=None):
    bsz, seq, _ = p3d.shape
    blk = MOBA_BLOCK
    n_blocks = seq // blk
    nq = n_blocks - q0 if nq is None else nq
    assert n_blocks <= MOBA_LO and seq % blk == 0
    span = 2 * blk
    by_dist = rel_bias[:, _rel_bucket(jnp.arange(span))].astype(F32)
    shift = jnp.arange(span)

    def toeplitz(c):
        k = jnp.where(shift < blk, shift, shift - span)
        s = by_dist[:, jnp.clip(c - k, 0, span - 1)]
        tiled = jnp.tile(s, (1, blk))[:, :blk * (span - 1)]
        return tiled.reshape(HEADS, blk, span - 1)[:, :, :blk]

    bias_own = toeplitz(0)
    bias_prev = toeplitz(blk)
    bias_far = jnp.broadcast_to(rel_bias[:, REL_BUCKETS - 1].astype(F32)[:, None, None], (HEADS, 8, LANES))
    kern = functools.partial(_moba_kernel, n_blocks=n_blocks, q0=q0)
    return pl.pallas_call(
        kern,
        grid=(bsz, PAIRS, nq),
        in_specs=[
            pl.BlockSpec((1, blk, LANES), lambda b, h, i: (b, q0 + i, h)),
            pl.BlockSpec((1, seq, LANES), lambda b, h, i: (b, 0, PAIRS + h)),
            pl.BlockSpec((1, seq, LANES), lambda b, h, i: (b, 0, 2 * PAIRS + h)),
            pl.BlockSpec((2, blk, blk), lambda b, h, i: (h, 0, 0)),
            pl.BlockSpec((2, blk, blk), lambda b, h, i: (h, 0, 0)),
            pl.BlockSpec((2, 8, LANES), lambda b, h, i: (h, 0, 0)),
        ],
        out_specs=pl.BlockSpec((1, blk, LANES), lambda b, h, i: (b, i, h)),
        out_shape=jax.ShapeDtypeStruct((bsz, nq * blk, WIDTH), F32),
        scratch_shapes=[
            pltpu.VMEM((seq, 2 * LANES), BF16),
            pltpu.VMEM((seq, LANES), BF16),
            pltpu.VMEM((LANES, LANES), F32),
        ],
        compiler_params=_cparams(("parallel", "parallel", "arbitrary")),
        name="moba",
    )(p3d, p3d, p3d, bias_own, bias_prev, bias_far)


def _shifted(x, carry_row):
    rows = lax.broadcasted_iota(jnp.int32, x.shape, 0)
    return jnp.where(rows == 0, carry_row, pltpu.roll(x, 1, axis=0))


def _rwkv_prep_kernel(pr_ref, pk_ref, pv_ref, pl_ref, mu_ref, vec_ref, ww_ref, wa_ref, wg_ref,
                      bd_ref, tri_ref,
                      rt_ref, kt_ref, kd_ref, bd_out_ref, v_ref, g_ref, bonus_ref, pend_ref,
                      carry_ref, *, chunk):
    @pl.when(pl.program_id(1) == 0)
    def _():
        carry_ref[...] = jnp.zeros_like(carry_ref)

    def mix(ref, j):
        x = ref[0]
        mu = mu_ref[0:1, j * WIDTH:(j + 1) * WIDTH]
        prev = _shifted(x, carry_ref[0:1, j * WIDTH:(j + 1) * WIDTH])
        carry_ref[0:1, j * WIDTH:(j + 1) * WIDTH] = x[x.shape[0] - 1:, :]
        return x + mu * (prev - x)

    r = mix(pr_ref, 0)
    k = mix(pk_ref, 1)
    v = mix(pv_ref, 2)
    lo = mix(pl_ref, 3)
    w0, a0, k_k, k_a, r_k = (vec_ref[i:i + 1, :] for i in range(5))
    xwa = lo[:, 0:LANES]
    xg = lo[:, LANES:3 * LANES]
    lw = jnp.dot(jnp.tanh(xwa), ww_ref[...], precision=HI, preferred_element_type=F32)
    la = jnp.dot(xwa, wa_ref[...], precision=HI, preferred_element_type=F32)
    g = jnp.dot(jax.nn.sigmoid(xg), wg_ref[...], precision=HI, preferred_element_type=F32)
    z = -(w0 + lw)
    softplus = jnp.maximum(z, 0.0) + jnp.log(1.0 + jnp.exp(-jnp.abs(z)))
    logw = -jnp.exp(-softplus - 0.5)
    a = jax.nn.sigmoid(a0 + la)
    kk = k * k_k
    ss = jnp.dot(kk * kk, bd_ref[...], precision=HI, preferred_element_type=F32)
    kk = kk / jnp.maximum(jnp.sqrt(ss), 1e-12)
    k2 = k * (1.0 + (a - 1.0) * k_a)
    rk = jnp.dot(r * k2 * r_k, bd_ref[...], precision=HI, preferred_element_type=F32)
    cs = jnp.dot(tri_ref[...], logw, precision=HI, preferred_element_type=F32)
    e_pos = jnp.exp(cs)
    e_neg = jnp.exp(-cs)
    rt_ref[0] = (r * e_pos).astype(rt_ref.dtype)
    kt_ref[0] = (kk * jnp.exp(cs - logw)).astype(kt_ref.dtype)
    kd_ref[0] = (k2 * e_neg).astype(kd_ref.dtype)
    bd_out_ref[0] = (kk * a * e_neg).astype(bd_out_ref.dtype)
    v_ref[0] = v.astype(v_ref.dtype)
    g_ref[0] = g
    bonus_ref[0] = rk * v
    ts = e_pos.shape[0]
    for c in range(ts // chunk):
        pend_ref[0, c:c + 1, :] = e_pos[(c + 1) * chunk - 1:(c + 1) * chunk, :]


def rwkv_prep(p3d, rwkv_mu, w0, w_lora_up, a0, a_lora_up, g_lora_up, k_k, k_a, r_k, *, ts=512):
    bsz, seq, _ = p3d.shape
    chunk = RWKV_CHUNK
    ts = min(ts, seq)
    mu = jnp.pad(rwkv_mu, (0, COL_B - COL_B_RAW)).reshape(1, COL_B)
    vec = jnp.stack([w0, a0, k_k, k_a, r_k.reshape(-1)] + [jnp.zeros_like(w0)] * 3).astype(F32)
    ww = jnp.zeros((LANES, WIDTH), F32).at[:DECAY_LORA].set(w_lora_up)
    wa = jnp.zeros((LANES, WIDTH), F32).at[DECAY_LORA:DECAY_LORA + AAA_LORA].set(a_lora_up)
    wg = jnp.zeros((2 * LANES, WIDTH), F32).at[:GATE_LORA].set(g_lora_up)
    hid = jnp.arange(WIDTH) // HEAD_DIM
    bd = (hid[:, None] == hid[None, :]).astype(F32)
    tix = jnp.arange(ts)
    tri = ((tix[:, None] // chunk == tix[None, :] // chunk) & (tix[None, :] <= tix[:, None])).astype(F32)
    c0 = COL_A // WIDTH
    big = jax.ShapeDtypeStruct((bsz, seq, WIDTH), F32)
    wspec = lambda shape: pl.BlockSpec(shape, lambda b, i: (0, 0))
    ospec = pl.BlockSpec((1, ts, WIDTH), lambda b, i: (b, i, 0))
    return pl.pallas_call(
        functools.partial(_rwkv_prep_kernel, chunk=chunk),
        grid=(bsz, seq // ts),
        in_specs=[
            pl.BlockSpec((1, ts, WIDTH), lambda b, i: (b, i, c0)),
            pl.BlockSpec((1, ts, WIDTH), lambda b, i: (b, i, c0 + 1)),
            pl.BlockSpec((1, ts, WIDTH), lambda b, i: (b, i, c0 + 2)),
            pl.BlockSpec((1, ts, WIDTH), lambda b, i: (b, i, c0 + 3)),
            wspec((1, COL_B)), wspec((8, WIDTH)), wspec((LANES, WIDTH)), wspec((LANES, WIDTH)),
            wspec((2 * LANES, WIDTH)), wspec((WIDTH, WIDTH)), wspec((ts, ts)),
        ],
        out_specs=[ospec] * 7 + [pl.BlockSpec((1, ts // chunk, WIDTH), lambda b, i: (b, i, 0))],
        out_shape=[jax.ShapeDtypeStruct((bsz, seq, WIDTH), BF16)] * 5 + [big] * 2
        + [jax.ShapeDtypeStruct((bsz, seq // chunk, WIDTH), F32)],
        scratch_shapes=[pltpu.VMEM((8, COL_B), F32)],
        compiler_params=_cparams(("parallel", "arbitrary")),
        name="rwkv_prep",
    )(p3d, p3d, p3d, p3d, mu, vec, ww, wa, wg, bd, tri)


def _rwkv_scan_kernel(rt_ref, kt_ref, kd_ref, bd_ref, v_ref, g_ref, bonus_ref, pend_ref, ln_ref, sin_ref,
                      o_ref, state_ref, *, chunk, prec):
    @pl.when(pl.program_id(1) == 0)
    def _():
        state_ref[...] = sin_ref[...]

    c2 = 2 * chunk
    lane = lax.broadcasted_iota(jnp.int32, (chunk, LANES), 1)
    first = lane < HEAD_DIM
    row = lax.broadcasted_iota(jnp.int32, (c2, c2), 0)
    col = lax.broadcasted_iota(jnp.int32, (c2, c2), 1)
    eye = (row == col).astype(F32)
    hrow = lax.broadcasted_iota(jnp.int32, (LANES, LANES), 0) // HEAD_DIM
    hcol = lax.broadcasted_iota(jnp.int32, (LANES, LANES), 1) // HEAD_DIM
    head_mean = jnp.where(hrow == hcol, 1.0 / HEAD_DIM, 0.0).astype(F32)
    nt = (((1,), (1,)), ((), ()))
    tn = (((0,), (0,)), ((), ()))
    dot = functools.partial(jnp.dot, precision=prec, preferred_element_type=F32)
    dotg = functools.partial(lax.dot_general, precision=prec, preferred_element_type=F32)

    def stack(x):
        return jnp.concatenate([jnp.where(first, x, 0.0), jnp.where(first, 0.0, x)], axis=0)

    pairs = range(PAIRS)
    sls = [slice(hp * LANES, (hp + 1) * LANES) for hp in pairs]
    rs, ks, kds, bs, vs = ([stack(ref[0, :, sl].astype(F32)) for sl in sls]
                           for ref in (rt_ref, kt_ref, kd_ref, bd_ref, v_ref))
    hts = [state_ref[0, hp] for hp in pairs]
    big = [dotg(jnp.concatenate([ks[hp], rs[hp]], axis=0), jnp.concatenate([bs[hp], kds[hp]], axis=0), nt)
           for hp in pairs]
    a_b = [jnp.where(row > col, big[hp][0:c2, 0:c2], 0.0) for hp in pairs]
    a_k = [jnp.where(row > col, big[hp][0:c2, c2:], 0.0) for hp in pairs]
    a_rb = [jnp.where(row >= col, big[hp][c2:, 0:c2], 0.0) for hp in pairs]
    a_rk = [jnp.where(row >= col, big[hp][c2:, c2:], 0.0) for hp in pairs]
    kh = [dotg(jnp.concatenate([ks[hp], rs[hp]], axis=0), hts[hp], nt) for hp in pairs]
    av = [dot(jnp.concatenate([a_k[hp], a_rk[hp]], axis=0), vs[hp]) for hp in pairs]
    vk = [dotg(vs[hp], kds[hp], tn) for hp in pairs]
    inv = [eye - a_b[hp] for hp in pairs]
    pw = [dot(a_b[hp], a_b[hp]) for hp in pairs]
    n_sq = int(math.log2(chunk)) - 1
    for lvl in range(n_sq):
        if lvl + 1 < n_sq:
            both = [dot(jnp.concatenate([inv[hp], pw[hp]], axis=0), pw[hp]) for hp in pairs]
            inv = [inv[hp] + both[hp][0:c2] for hp in pairs]
            pw = [both[hp][c2:] for hp in pairs]
        else:
            inv = [inv[hp] + dot(inv[hp], pw[hp]) for hp in pairs]
    us = [dot(inv[hp], kh[hp][0:c2] + av[hp][0:c2]) for hp in pairs]
    ub = [dotg(us[hp], bs[hp], tn) for hp in pairs]
    au = [dot(a_rb[hp], us[hp]) for hp in pairs]
    for hp in pairs:
        sl = sls[hp]
        pend = pend_ref[0, 0, 0:1, sl]
        state_ref[0, hp] = (hts[hp] + vk[hp] - ub[hp]) * pend
        os_ = kh[hp][c2:] + av[hp][c2:] - au[hp]
        o = os_[0:chunk] + os_[chunk:]
        mu = jnp.dot(o, head_mean, precision=HI, preferred_element_type=F32)
        d = o - mu
        var = jnp.dot(d * d, head_mean, precision=HI, preferred_element_type=F32)
        on = d * lax.rsqrt(var + GN_EPS) * ln_ref[0:1, sl] + ln_ref[1:2, sl]
        o_ref[0, :, sl] = (on + bonus_ref[0, :, sl]) * g_ref[0, :, sl]


def rwkv_scan(rt, kt, kd, bd, v, g, bonus, pend, lnx_g, lnx_b, *, state=None, c0=0, nc=None, prec=None):
    bsz, seq, _ = rt.shape
    chunk = RWKV_CHUNK
    n_chunks = seq // chunk
    nc = n_chunks - c0 if nc is None else nc
    ln = jnp.stack([lnx_g, lnx_b] + [jnp.zeros_like(lnx_g)] * 6).astype(F32)
    pend4 = pend.reshape(bsz, n_chunks, 1, WIDTH)
    if state is None:
        state = jnp.zeros((bsz, PAIRS, LANES, LANES), F32)
    spec = pl.BlockSpec((1, chunk, WIDTH), lambda b, c: (b, c0 + c, 0))
    sspec = pl.BlockSpec((1, PAIRS, LANES, LANES), lambda b, c: (b, 0, 0, 0))
    return pl.pallas_call(
        functools.partial(_rwkv_scan_kernel, chunk=chunk, prec=prec),
        grid=(bsz, nc),
        in_specs=[spec] * 7 + [
            pl.BlockSpec((1, 1, 1, WIDTH), lambda b, c: (b, c0 + c, 0, 0)),
            pl.BlockSpec((8, WIDTH), lambda b, c: (0, 0)),
            sspec,
        ],
        out_specs=[pl.BlockSpec((1, chunk, WIDTH), lambda b, c: (b, c, 0)), sspec],
        out_shape=[jax.ShapeDtypeStruct((bsz, nc * chunk, WIDTH), F32),
                   jax.ShapeDtypeStruct((bsz, PAIRS, LANES, LANES), F32)],
        compiler_params=_cparams(("parallel", "arbitrary")),
        name="rwkv_scan",
    )(rt, kt, kd, bd, v, g, bonus, pend4, ln, state)


def _merge_kernel(x_ref, oa_ref, ob_ref, ga_ref, gb_ref, wa_ref, wb_ref, wo_ref, g2_ref,
                  h_ref, xn_ref, acc_ref):
    j = pl.program_id(1)

    @pl.when(j == 0)
    def _():
        acc_ref[...] = x_ref[...]

    ya = jnp.dot(oa_ref[...].astype(BF16), wa_ref[...], preferred_element_type=F32)
    yb = jnp.dot(ob_ref[...].astype(BF16), wb_ref[...], preferred_element_type=F32)
    y = jax.nn.sigmoid(ga_ref[...]) * ya + jax.nn.sigmoid(gb_ref[...]) * yb
    acc_ref[...] += jnp.dot(y.astype(BF16), wo_ref[...], preferred_element_type=F32)

    @pl.when(j == pl.num_programs(1) - 1)
    def _():
        h = acc_ref[...]
        h_ref[...] = h
        ms = jnp.mean(h * h, axis=-1, keepdims=True)
        xn_ref[...] = _pack_halves(h * lax.rsqrt(ms + RMS_EPS) * g2_ref[...])


def _pack_halves(x):
    half = x.shape[1] // 2
    lo = lax.bitcast_convert_type(x[:, :half].astype(BF16).astype(F32), jnp.int32)
    hi = lax.bitcast_convert_type(x[:, half:].astype(BF16).astype(F32), jnp.int32)
    return lax.bitwise_or(lax.shift_right_logical(lo, jnp.int32(16)), hi)


def _unpack_halves(words):
    lo, hi = _unpack_words(words)
    return jnp.concatenate([lo, hi], axis=1)


def merge_out(x2d, oa, ob, p2d, w_proj_a, w_proj_b, w_out, norm2_g, *, row0=0, prow0=0, tm=512):
    t, d = oa.shape[0], x2d.shape[1]
    r0 = row0 // tm
    p0 = prow0 // tm
    tn = WIDTH
    nj = d // tn
    g0 = COL_G_OFF // tn
    return pl.pallas_call(
        _merge_kernel,
        grid=(t // tm, nj),
        in_specs=[
            pl.BlockSpec((tm, d), lambda i, j: (r0 + i, 0)),
            pl.BlockSpec((tm, WIDTH), lambda i, j: (i, 0)),
            pl.BlockSpec((tm, WIDTH), lambda i, j: (i, 0)),
            pl.BlockSpec((tm, tn), lambda i, j: (p0 + i, g0 + j)),
            pl.BlockSpec((tm, tn), lambda i, j: (p0 + i, g0 + nj + j)),
            pl.BlockSpec((WIDTH, tn), lambda i, j: (0, j)),
            pl.BlockSpec((WIDTH, tn), lambda i, j: (0, j)),
            pl.BlockSpec((tn, d), lambda i, j: (j, 0)),
            pl.BlockSpec((1, d), lambda i, j: (0, 0)),
        ],
        out_specs=[pl.BlockSpec((tm, d), lambda i, j: (i, 0)), pl.BlockSpec((tm, d // 2), lambda i, j: (i, 0))],
        out_shape=[jax.ShapeDtypeStruct((t, d), F32), jax.ShapeDtypeStruct((t, d // 2), jnp.int32)],
        scratch_shapes=[pltpu.VMEM((tm, d), F32)],
        compiler_params=_cparams(("parallel", "arbitrary")),
        name="merge_out",
    )(x2d, oa, ob, p2d, p2d, w_proj_a.astype(BF16), w_proj_b.astype(BF16), w_out.astype(BF16),
      norm2_g.reshape(1, d))


PEER_HEADS = 8
PEER_NKEYS = 128
PEER_TOPK = 16
PEER_HALF = 128


def _topk_rows(s, k):
    n = s.shape[0]
    rows = lax.broadcasted_iota(jnp.int32, s.shape, 0).astype(F32)
    vals, ids = [], []
    for _ in range(k):
        m = jnp.max(s, axis=0, keepdims=True)
        first = jnp.min(jnp.where(s == m, rows, float(n)), axis=0, keepdims=True)
        vals.append(m)
        ids.append(first)
        s = jnp.where(rows == first, -jnp.inf, s)
    return jnp.concatenate(vals, axis=0), jnp.concatenate(ids, axis=0)


def _take_rows(table, ids):
    rows = lax.broadcasted_iota(jnp.int32, table.shape, 0).astype(F32)
    return jnp.sum(jnp.where(rows == ids, table, 0.0), axis=0, keepdims=True)


def _peer_route_kernel(xn_ref, wq_ref, sk_ref, idx_ref, gate_ref, *, prec):
    tt = xn_ref.shape[0]
    k = PEER_TOPK
    xn = _unpack_halves(xn_ref[...]) if xn_ref.dtype == jnp.int32 else xn_ref[...]
    q = jnp.dot(xn.astype(wq_ref.dtype), wq_ref[...], precision=prec, preferred_element_type=F32)
    nt = (((1,), (1,)), ((), ()))
    idx_rows, gate_rows = [], []
    half = k // 2
    for h in range(PEER_HEADS):
        tops = []
        for p in range(2):
            c0 = (h * 2 + p) * PEER_HALF
            s = lax.dot_general(sk_ref[h, p].astype(wq_ref.dtype), q[:, c0:c0 + PEER_HALF].astype(wq_ref.dtype),
                                nt, precision=prec, preferred_element_type=F32)
            tops.append(_topk_rows(s, k))
        (s0, i0), (s1, i1) = tops
        cs = [s0[0:1] + s1] + [s0[i:i + 1] + s1[0:half] for i in range(1, half)] + [s0[half:] + s1[0:1]]
        best_s, pos = _topk_rows(jnp.concatenate(cs, axis=0), k)
        mid = jnp.floor((pos - k) * (1.0 / half))
        end_mid = float(k + (half - 1) * half)
        i_rank = jnp.where(pos < k, 0.0, jnp.where(pos < end_mid, 1.0 + mid, pos - (end_mid - half)))
        j_rank = jnp.where(pos < k, pos, jnp.where(pos < end_mid, (pos - k) - half * mid, 0.0))
        ids = [_take_rows(i0, i_rank[n:n + 1]) * PEER_NKEYS + _take_rows(i1, j_rank[n:n + 1]) for n in range(k)]
        e = jnp.exp(best_s - best_s[0:1])
        gate_rows.append(e / jnp.sum(e, axis=0, keepdims=True))
        idx_rows.append(jnp.concatenate(ids, axis=0).astype(jnp.int32))
    idx_ref[...] = jnp.concatenate(idx_rows, axis=0).T
    gate_ref[...] = jnp.concatenate(gate_rows, axis=0).T


def peer_route(xn2d, peer_wq, peer_subkeys, *, tt=256, prec=None, wdtype=BF16):
    t, dx = xn2d.shape
    d, nq = peer_wq.shape
    n_sel = PEER_HEADS * PEER_TOPK
    return pl.pallas_call(
        functools.partial(_peer_route_kernel, prec=prec),
        grid=(t // tt,),
        in_specs=[
            pl.BlockSpec((tt, dx), lambda i: (i, 0)),
            pl.BlockSpec((d, nq), lambda i: (0, 0)),
            pl.BlockSpec((PEER_HEADS, 2, PEER_NKEYS, PEER_HALF), lambda i: (0, 0, 0, 0)),
        ],
        out_specs=[pl.BlockSpec((tt, n_sel), lambda i: (i, 0))] * 2,
        out_shape=[jax.ShapeDtypeStruct((t, n_sel), jnp.int32), jax.ShapeDtypeStruct((t, n_sel), F32)],
        compiler_params=_cparams(("parallel",)),
        name="peer_route",
    )(xn2d, peer_wq.astype(wdtype), peer_subkeys)


def _final_kernel(h_ref, y_ref, g_ref, *rest):
    o_ref = rest[-1]
    h = h_ref[...] + y_ref[...]
    ms = jnp.mean(h * h, axis=-1, keepdims=True)
    o_ref[...] = h * lax.rsqrt(ms + RMS_EPS) * g_ref[...]


def final_norm(h2d, y2d, g, *, out=None, row0=0, total_rows=None, tm=1024):
    t, d = h2d.shape
    total = t if total_rows is None else total_rows
    r0 = row0 // tm
    spec = pl.BlockSpec((tm, d), lambda i: (i, 0))
    in_specs = [spec, spec, pl.BlockSpec((1, d), lambda i: (0, 0))]
    args = [h2d, y2d, g.reshape(1, d)]
    aliases = {}
    if out is not None:
        in_specs.append(pl.BlockSpec(memory_space=pl.ANY))
        args.append(out)
        aliases = {3: 0}
    return pl.pallas_call(
        _final_kernel,
        grid=(t // tm,),
        in_specs=in_specs,
        out_specs=pl.BlockSpec((tm, d), lambda i: (r0 + i, 0)),
        out_shape=jax.ShapeDtypeStruct((total, d), F32),
        input_output_aliases=aliases,
        compiler_params=_cparams(("parallel",)),
        name="final_norm",
    )(*args)


SC_CORES = 2
SC_SUBCORES = 16
SC_LANES = 16
SC_WORKERS = SC_CORES * SC_SUBCORES
PEER_SEL = PEER_HEADS * PEER_TOPK
PEER_ROWS = 32
PEER_PARTS = PEER_SEL // PEER_ROWS
PEER_NBUF = 4
PEER_GROUP = 32
PEER_BF16_RUN = 4


def _pack_rows(w):
    half = w.shape[1] // 2
    bits = lax.bitcast_convert_type(w.astype(BF16), jnp.uint16).astype(jnp.uint32)
    return lax.bitcast_convert_type(bits[:, :half] | (bits[:, half:] << 16), jnp.int32)


def _unpack_words(w):
    lo = lax.bitcast_convert_type(lax.shift_left(w, jnp.int32(16)), F32)
    hi = lax.bitcast_convert_type(lax.bitwise_and(w, jnp.int32(-65536)), F32)
    return lo, hi


def _packed_dot(a_words, b_words):
    from jax.experimental.pallas import tpu_sc as plsc
    prods = [plsc.bitcast(a, BF16) * plsc.bitcast(b, BF16) for a, b in zip(a_words, b_words)]
    while len(prods) > 1:
        prods = [prods[k] + prods[k + 1] for k in range(0, len(prods), 2)]
    return _unpack_words(plsc.bitcast(prods[0], jnp.int32))


def _sc_mesh():
    from jax.experimental.pallas import tpu_sc as plsc
    return plsc.VectorSubcoreMesh(core_axis_name="c", subcore_axis_name="s",
                                  num_cores=SC_CORES, num_subcores=SC_SUBCORES)


def _sc_loop(n, body, carry):
    from jax.experimental.pallas import tpu_sc as plsc
    return plsc.parallel_loop(0, n, carry=carry)(body)


def _worker_base(tokens_per_worker):
    return (lax.axis_index("s") * SC_CORES + lax.axis_index("c")) * tokens_per_worker


def _gather_compute_loop(table_hbm, idx_v, rows_v, sem, stage_v, out_row, osem, grp, compute):
    n_gathers = PEER_PARTS * grp
    ahead = PEER_NBUF - 1

    def gather(j, b):
        i = j // PEER_PARTS if isinstance(j, int) else lax.shift_right_logical(j, PEER_PARTS.bit_length() - 1)
        h = j % PEER_PARTS if isinstance(j, int) else lax.bitwise_and(j, PEER_PARTS - 1)
        ids = idx_v.at[i, pl.ds(pl.multiple_of(h * PEER_ROWS, PEER_ROWS), PEER_ROWS)]
        return pltpu.make_async_copy(table_hbm.at[ids], rows_v.at[b], sem.at[b])

    def put(i, slot):
        return pltpu.make_async_copy(stage_v.at[slot], out_row(i), osem.at[slot])

    for j in range(ahead):
        gather(j, j).start()

    @pl.loop(0, n_gathers)
    def _(j):
        b = lax.bitwise_and(j, PEER_NBUF - 1)
        h = lax.bitwise_and(j, PEER_PARTS - 1)
        i = lax.shift_right_logical(j, PEER_PARTS.bit_length() - 1)
        slot = lax.bitwise_and(i, 1)

        @pl.when((h == 0) & (i >= 2))
        def _():
            put(i - 2, slot).wait()

        @pl.when(j + ahead < n_gathers)
        def _():
            gather(j + ahead, lax.bitwise_and(j + ahead, PEER_NBUF - 1)).start()

        gather(j, b).wait()
        compute(i, h, b, slot)

        @pl.when(h == PEER_PARTS - 1)
        def _():
            put(i, slot).start()

    put(grp - 2, 0).wait()
    put(grp - 1, 1).wait()


def peer_expert_dots(x_packed, idx, u_packed):
    t, half = x_packed.shape
    n_chunks = half // SC_LANES
    tpw = t // SC_WORKERS
    grp = min(PEER_GROUP, tpw)
    rows_tog = 8

    def body(x_hbm, idx_hbm, u_hbm, out_hbm, idx_v, x_v, rows_v, ps_v, sem, osem):
        base = _worker_base(tpw)

        def compute(i, h, b, slot):
            @pl.loop(0, PEER_ROWS // rows_tog)
            def _(rg):
                r0 = rg * rows_tog
                accs = [[None, None] for _ in range(rows_tog)]
                for c0 in range(0, n_chunks, PEER_BF16_RUN):
                    ats = [pl.ds((c0 + k) * SC_LANES, SC_LANES) for k in range(PEER_BF16_RUN)]
                    xw = [x_v[i, at] for at in ats]
                    for r in range(rows_tog):
                        terms = _packed_dot([rows_v[b, r0 + r, at] for at in ats], xw)
                        for k, term in enumerate(terms):
                            accs[r][k] = term if accs[r][k] is None else accs[r][k] + term
                for r in range(rows_tog):
                    at = pl.ds(pl.multiple_of((h * PEER_ROWS + r0 + r) * SC_LANES, SC_LANES), SC_LANES)
                    ps_v[slot, at] = accs[r][0] + accs[r][1]

        @pl.loop(0, tpw // grp)
        def _(g):
            t0 = base + g * grp
            pltpu.sync_copy(idx_hbm.at[pl.ds(t0, grp)], idx_v)
            pltpu.sync_copy(x_hbm.at[pl.ds(t0, grp)], x_v)
            _gather_compute_loop(u_hbm, idx_v, rows_v, sem, ps_v, lambda i: out_hbm.at[t0 + i], osem, grp, compute)

    return pl.kernel(
        body,
        out_type=jax.ShapeDtypeStruct((t, PEER_SEL * SC_LANES), F32),
        mesh=_sc_mesh(),
        scratch_types=[
            pltpu.VMEM((grp, PEER_SEL), jnp.int32),
            pltpu.VMEM((grp, half), jnp.int32),
            pltpu.VMEM((PEER_NBUF, PEER_ROWS, half), jnp.int32),
            pltpu.VMEM((2, PEER_SEL * SC_LANES), F32),
            pltpu.SemaphoreType.DMA((PEER_NBUF,)),
            pltpu.SemaphoreType.DMA((2,)),
        ],
        compiler_params=pltpu.CompilerParams(needs_layout_passes=False),
        name="peer_expert_dots",
    )(x_packed, idx, u_packed)


def peer_expert_mix(hgw, idx, v_packed):
    t = hgw.shape[0]
    half = v_packed.shape[1]
    d = 2 * half
    tpw = t // SC_WORKERS
    grp = min(PEER_GROUP, tpw)
    n_parts = 2
    cpp = half // SC_LANES // n_parts
    from jax.experimental.pallas import tpu_sc as plsc

    def body(hg_hbm, idx_hbm, v_hbm, out_hbm, idx_v, hg_v, rows_v, o_v2, sem, osem):
        base = _worker_base(tpw)

        def compute(i, h, b, slot):
            token = jnp.full((SC_LANES,), i, jnp.int32)
            for part in range(n_parts):
                def rbody(rq, accs):
                    r0 = rq * PEER_BF16_RUN
                    s = [plsc.load_gather(hg_v, [token, jnp.full((SC_LANES,), h * PEER_ROWS + r0 + k, jnp.int32)])
                         for k in range(PEER_BF16_RUN)]
                    new = []
                    for c in range(cpp):
                        at = pl.ds((part * cpp + c) * SC_LANES, SC_LANES)
                        lo, hi = _packed_dot([rows_v[b, r0 + k, at] for k in range(PEER_BF16_RUN)], s)
                        new.append(accs[2 * c] + lo)
                        new.append(accs[2 * c + 1] + hi)
                    return tuple(new)

                accs = _sc_loop(PEER_ROWS // PEER_BF16_RUN, rbody,
                                tuple(jnp.zeros((SC_LANES,), F32) for _ in range(2 * cpp)))
                def store(overwrite):
                    for c in range(cpp):
                        lo_at = pl.ds((part * cpp + c) * SC_LANES, SC_LANES)
                        hi_at = pl.ds(half + (part * cpp + c) * SC_LANES, SC_LANES)
                        if overwrite:
                            o_v2[slot, lo_at] = accs[2 * c]
                            o_v2[slot, hi_at] = accs[2 * c + 1]
                        else:
                            o_v2[slot, lo_at] = o_v2[slot, lo_at] + accs[2 * c]
                            o_v2[slot, hi_at] = o_v2[slot, hi_at] + accs[2 * c + 1]

                pl.when(h == 0)(functools.partial(store, True))
                pl.when(h != 0)(functools.partial(store, False))

        @pl.loop(0, tpw // grp)
        def _(g):
            t0 = base + g * grp
            pltpu.sync_copy(idx_hbm.at[pl.ds(t0, grp)], idx_v)
            pltpu.sync_copy(hg_hbm.at[pl.ds(t0, grp)], hg_v)
            _gather_compute_loop(v_hbm, idx_v, rows_v, sem, o_v2, lambda i: out_hbm.at[t0 + i], osem, grp, compute)

    return pl.kernel(
        body,
        out_type=jax.ShapeDtypeStruct((t, d), F32),
        mesh=_sc_mesh(),
        scratch_types=[
            pltpu.VMEM((grp, PEER_SEL), jnp.int32),
            pltpu.VMEM((grp, PEER_SEL), jnp.int32),
            pltpu.VMEM((PEER_NBUF, PEER_ROWS, half), jnp.int32),
            pltpu.VMEM((2, d), F32),
            pltpu.SemaphoreType.DMA((PEER_NBUF,)),
            pltpu.SemaphoreType.DMA((2,)),
        ],
        compiler_params=pltpu.CompilerParams(needs_layout_passes=False),
        name="peer_expert_mix",
    )(hgw, idx, v_packed)


def _peer_act_kernel(ps_ref, gate_ref, sum_ref, o_ref):
    ps = ps_ref[...]
    sel = sum_ref[...]
    hi = ps.astype(BF16)
    rest = ps - hi.astype(F32)
    mid = rest.astype(BF16)
    lo = (rest - mid.astype(F32)).astype(BF16)
    pre = (jnp.dot(hi, sel, preferred_element_type=F32) + jnp.dot(mid, sel, preferred_element_type=F32)
           + jnp.dot(lo, sel, preferred_element_type=F32))
    hg = 0.5 * pre * (1.0 + lax.erf(pre * (1.0 / math.sqrt(2.0)))) * gate_ref[...]
    bits = lax.bitcast_convert_type(hg.astype(BF16).astype(F32), jnp.int32)
    o_ref[...] = lax.bitwise_or(bits, lax.shift_right_logical(bits, jnp.int32(16)))


def peer_act(ps, gates, *, tm=512):
    t, n = ps.shape
    lane_sum = (jnp.arange(n)[:, None] // SC_LANES == jnp.arange(PEER_SEL)[None, :]).astype(BF16)
    return pl.pallas_call(
        _peer_act_kernel,
        grid=(t // tm,),
        in_specs=[
            pl.BlockSpec((tm, n), lambda i: (i, 0)),
            pl.BlockSpec((tm, PEER_SEL), lambda i: (i, 0)),
            pl.BlockSpec((n, PEER_SEL), lambda i: (0, 0)),
        ],
        out_specs=pl.BlockSpec((tm, PEER_SEL), lambda i: (i, 0)),
        out_shape=jax.ShapeDtypeStruct((t, PEER_SEL), jnp.int32),
        compiler_params=_cparams(("parallel",)),
        name="peer_act",
    )(ps, gates, lane_sum)


BATCH_GROUPS = 8


def kernel(x, norm1_g, w_in, rwkv_mu, w0, w_lora_up, a0, a_lora_up, g_lora_up, k_k, k_a, r_k, lnx_g, lnx_b,
           w_proj_a, w_proj_b, w_out, norm2_g, peer_wq, peer_subkeys, peer_u, peer_v, rel_bias, normf_g):
    bsz, seq, d = x.shape
    depth = norm1_g.shape[0]
    groups = BATCH_GROUPS if bsz % BATCH_GROUPS == 0 else 1
    gb = bsz // groups
    tg = gb * seq
    t = bsz * seq
    src = x.reshape(t, d)
    for l in range(depth):
        w_pad = jnp.concatenate([
            w_in[l][:, :COL_A + COL_B_RAW],
            jnp.zeros((d, COL_B - COL_B_RAW), w_in.dtype),
            w_in[l][:, COL_A + COL_B_RAW:]], axis=1).astype(BF16)
        u_packed = _pack_rows(peer_u[l])
        tables = {"v": _pack_rows(peer_v[l])}
        last = l == depth - 1

        def mix(pending, tie=None):
            row0, h2d, ps, gates, idx = pending
            hgx = peer_act(ps, gates)
            if tie is not None:
                tie, hgx = lax.optimization_barrier((tie, hgx))
            return tie, (row0, h2d, peer_expert_mix(hgx, idx, tables["v"]))

        outs = []

        def close(mixed):
            row0, h2d, y2d = mixed
            if last:
                outs.append(final_norm(h2d, y2d, normf_g, out=outs[-1] if outs else None, row0=row0, total_rows=t))
            else:
                outs.append(h2d + y2d)

        halves = gb == 1 and seq % (2 * MOBA_BLOCK) == 0 and (seq // 2) % (SC_WORKERS * PEER_GROUP) == 0

        pending = closing = None
        for g in range(groups):
            p2d = norm_proj(src, norm1_g[l], w_pad, row0=g * tg, rows=tg)
            p3d = p2d.reshape(gb, seq, -1)
            prep = state = None
            for s0, sn in ([(0, seq // 2), (seq // 2, seq // 2)] if halves and g == 0 else [(0, seq)]):
                oa = moba_attention(p3d, rel_bias, q0=s0 // MOBA_BLOCK, nq=sn // MOBA_BLOCK)
                if prep is None:
                    prep = tuple(rwkv_prep(p3d, rwkv_mu[l], w0[l], w_lora_up[l], a0[l], a_lora_up[l], g_lora_up[l],
                                           k_k[l], k_a[l], r_k[l]))
                mixed = None
                if pending is not None:
                    (oa, prep), mixed = mix(pending, (oa, prep))
                if closing is not None:
                    oa, y2d = lax.optimization_barrier((oa, closing[2]))
                    close(closing[:2] + (y2d,))
                    closing = None
                ob, state = rwkv_scan(*prep, lnx_g[l], lnx_b[l], state=state,
                                      c0=s0 // RWKV_CHUNK, nc=sn // RWKV_CHUNK)
                nt = gb * sn
                h2d, xn2 = merge_out(src, oa.reshape(nt, WIDTH), ob.reshape(nt, WIDTH), p2d, w_proj_a[l], w_proj_b[l],
                                     w_out[l], norm2_g[l], row0=g * tg + s0, prow0=s0)
                idx, gates = peer_route(xn2, peer_wq[l], peer_subkeys[l])
                if mixed is not None:
                    idx, y2d = lax.optimization_barrier((idx, mixed[2]))
                    closing = mixed[:2] + (y2d,)
                pending = (g * tg + s0, h2d, peer_expert_dots(xn2, idx, u_packed), gates, idx)
        if closing is not None:
            close(closing)
        close(mix(pending)[1])
        src = outs[-1] if last else jnp.concatenate(outs, axis=0)
    return src.reshape(bsz, seq, d)
```

```python
import functools
import math

import jax
import jax.numpy as jnp
from jax import lax
from jax.experimental import pallas as pl
from jax.experimental.pallas import tpu as pltpu

F32 = jnp.float32
BF16 = jnp.bfloat16
HI = lax.Precision.HIGHEST

LANES = 128
HEAD_DIM = 64
HEADS = 8
PAIRS = HEADS // 2
WIDTH = HEADS * HEAD_DIM
MOBA_BLOCK = 256
MOBA_TOPK = 3
MOBA_LO = 64
REL_BUCKETS = 32
REL_MAX_DIST = 128
DECAY_LORA = 64
AAA_LORA = 64
GATE_LORA = 160
GN_EPS = 64e-5
RMS_EPS = 1e-6
NEG = -1e30
RWKV_CHUNK = 64
COL_A = 3 * WIDTH
COL_B_RAW = 3 * WIDTH + DECAY_LORA + AAA_LORA + GATE_LORA
COL_B = 4 * WIDTH
COL_G_OFF = COL_A + COL_B
VMEM_LIMIT = 56 * 1024 * 1024


def _cparams(sem):
    return pltpu.CompilerParams(dimension_semantics=sem, vmem_limit_bytes=VMEM_LIMIT)


def _norm_proj_kernel(x_ref, g_ref, w_ref, o_ref, xn_ref):
    @pl.when(pl.program_id(1) == 0)
    def _():
        x = x_ref[...]
        ms = jnp.mean(x * x, axis=-1, keepdims=True)
        xn_ref[...] = (x * lax.rsqrt(ms + RMS_EPS) * g_ref[...]).astype(xn_ref.dtype)

    o_ref[...] = jnp.dot(xn_ref[...], w_ref[...], preferred_element_type=F32).astype(o_ref.dtype)


def norm_proj(x2d, g, w, *, row0=0, rows=None, tm=1024, tn=512, out_dtype=F32):
    d = x2d.shape[1]
    t = x2d.shape[0] if rows is None else rows
    n = w.shape[1]
    r0 = row0 // tm
    return pl.pallas_call(
        _norm_proj_kernel,
        grid=(t // tm, n // tn),
        in_specs=[
            pl.BlockSpec((tm, d), lambda i, j: (r0 + i, 0)),
            pl.BlockSpec((1, d), lambda i, j: (0, 0)),
            pl.BlockSpec((d, tn), lambda i, j: (0, j)),
        ],
        out_specs=pl.BlockSpec((tm, tn), lambda i, j: (i, j)),
        out_shape=jax.ShapeDtypeStruct((t, n), out_dtype),
        scratch_shapes=[pltpu.VMEM((tm, d), w.dtype)],
        compiler_params=_cparams(("parallel", "arbitrary")),
        name="norm_proj",
    )(x2d, g.reshape(1, d), w)


def _rel_bucket(dist):
    n = jnp.maximum(dist, 0)
    max_exact = REL_BUCKETS // 2
    nf = jnp.maximum(n, 1).astype(F32)
    large = max_exact + (jnp.log(nf / max_exact) / math.log(REL_MAX_DIST / max_exact)
                         * (REL_BUCKETS - max_exact)).astype(jnp.int32)
    large = jnp.minimum(large, REL_BUCKETS - 1)
    return jnp.where(n < max_exact, n, large)


def _moba_kernel(q_ref, k_ref, v_ref, bown_ref, bprev_ref, bfar_ref, o_ref,
                 kb_ref, vb_ref, kbar_ref, *, n_blocks, q0):
    qb = pl.program_id(2) + q0
    blk = MOBA_BLOCK
    scale = 1.0 / math.sqrt(HEAD_DIM)

    rows2 = 2 * blk
    nt = (((1,), (1,)), ((), ()))

    @pl.when(pl.program_id(2) == 0)
    def _():
        kbar_ref[...] = jnp.zeros_like(kbar_ref)
        lane_b = lax.broadcasted_iota(jnp.int32, (blk, LANES), 1)
        for n in range(n_blocks):
            kblk = k_ref[0, n * blk:(n + 1) * blk, :]
            kbar_ref[n:n + 1, :] = jnp.mean(kblk, axis=0, keepdims=True)
            kb_ref[n * blk:(n + 1) * blk, 0:LANES] = kblk.astype(BF16)
            kb_ref[n * blk:(n + 1) * blk, LANES:] = ((lane_b == n) | (lane_b == MOBA_LO + n)).astype(BF16)
        vb_ref[...] = v_ref[0].astype(BF16)

    q2 = q_ref[0]
    first = lax.broadcasted_iota(jnp.int32, (blk, LANES), 1) < HEAD_DIM
    qh = jnp.concatenate([jnp.where(first, q2, 0.0), jnp.where(first, 0.0, q2)], axis=0)
    lane = lax.broadcasted_iota(jnp.int32, (rows2, LANES), 1)
    rowi = lax.broadcasted_iota(jnp.int32, (rows2, LANES), 0)
    gate = lax.dot_general(qh.astype(BF16), kbar_ref[...].astype(BF16), nt, preferred_element_type=F32)
    g = jnp.where(lane < qb, gate, -jnp.inf)
    chosen = lane < 0
    lane_f = lane.astype(F32)
    for _ in range(MOBA_TOPK):
        m = jnp.max(g, axis=1, keepdims=True)
        idx = jnp.min(jnp.where(g == m, lane_f, float(LANES)), axis=1, keepdims=True)
        hit = (lane_f == idx) & (m > -jnp.inf)
        chosen = chosen | hit
        g = jnp.where(hit, -jnp.inf, g)
    nfar = qb - 1
    bfar = jnp.where(rowi < blk, bfar_ref[0, 0:1, 0:1], bfar_ref[1, 0:1, 0:1])
    bhi = bfar.astype(BF16).astype(F32)
    madd = jnp.where(lane < nfar, jnp.where(chosen, bhi, NEG),
                     jnp.where(lane == nfar, jnp.where(chosen, 0.0, NEG),
                               jnp.where((lane >= MOBA_LO) & (lane - MOBA_LO < nfar), bfar - bhi, 0.0)))
    q_aug = jnp.concatenate([(qh * scale).astype(BF16), madd.astype(BF16)], axis=1)

    prev0 = pl.multiple_of(jnp.maximum(nfar, 0) * blk, blk)
    own0 = pl.multiple_of(qb * blk, blk)
    s_prev = (lax.dot_general(q_aug, kb_ref[pl.ds(prev0, blk), :], nt, preferred_element_type=F32)
              + bprev_ref[...].reshape(rows2, blk) + jnp.where(qb > 0, 0.0, NEG))
    s_own = (lax.dot_general(q_aug, kb_ref[pl.ds(own0, blk), :], nt, preferred_element_type=F32)
             + bown_ref[...].reshape(rows2, blk))
    r = lax.broadcasted_iota(jnp.int32, (rows2, blk), 0)
    c = lax.broadcasted_iota(jnp.int32, (rows2, blk), 1)
    s_own = jnp.where(lax.bitwise_and(r, blk - 1) >= c, s_own, NEG)
    s = jnp.concatenate([s_prev, s_own], axis=1)
    m_i = jnp.max(s, axis=1, keepdims=True)
    p = jnp.exp(s - m_i)
    l_i = jnp.sum(p, axis=1, keepdims=True)
    v0 = jnp.concatenate([vb_ref[pl.ds(prev0, blk), :], vb_ref[pl.ds(own0, blk), :]], axis=0)
    acc = jnp.dot(p.astype(BF16), v0, preferred_element_type=F32)

    def body(it, carry):
        m_i, l_i, acc = carry
        k0 = pl.multiple_of(it * rows2, rows2)
        s = lax.dot_general(q_aug, kb_ref[pl.ds(k0, rows2), :], nt, preferred_element_type=F32)
        tail = jnp.where(2 * it + 1 < nfar, 0.0, NEG)
        s = jnp.concatenate([s[:, :blk], s[:, blk:] + tail], axis=1)
        m_new = jnp.maximum(m_i, jnp.max(s, axis=1, keepdims=True))
        alpha = jnp.exp(m_i - m_new)
        p = jnp.exp(s - m_new)
        l_new = alpha * l_i + jnp.sum(p, axis=1, keepdims=True)
        acc_new = alpha * acc + jnp.dot(p.astype(BF16), vb_ref[pl.ds(k0, rows2), :], preferred_element_type=F32)
        return m_new, l_new, acc_new

    m_i, l_i, acc = lax.fori_loop(0, (jnp.maximum(nfar, 0) + 1) // 2, body, (m_i, l_i, acc))
    out = acc / l_i
    o_ref[0] = jnp.where(first, out[:blk], out[blk:]).astype(o_ref.dtype)


def moba_attention(p3d, rel_bias, *, q0=0, nq=None):
    bsz, seq, _ = p3d.shape
    blk = MOBA_BLOCK
    n_blocks = seq // blk
    nq = n_blocks - q0 if nq is None else nq
    assert n_blocks <= MOBA_LO and seq % blk == 0
    span = 2 * blk
    by_dist = rel_bias[:, _rel_bucket(jnp.arange(span))].astype(F32)
    shift = jnp.arange(span)

    def toeplitz(c):
        k = jnp.where(shift < blk, shift, shift - span)
        s = by_dist[:, jnp.clip(c - k, 0, span - 1)]
        tiled = jnp.tile(s, (1, blk))[:, :blk * (span - 1)]
        return tiled.reshape(HEADS, blk, span - 1)[:, :, :blk]

    bias_own = toeplitz(0)
    bias_prev = toeplitz(blk)
    bias_far = jnp.broadcast_to(rel_bias[:, REL_BUCKETS - 1].astype(F32)[:, None, None], (HEADS, 8, LANES))
    kern = functools.partial(_moba_kernel, n_blocks=n_blocks, q0=q0)
    return pl.pallas_call(
        kern,
        grid=(bsz, PAIRS, nq),
        in_specs=[
            pl.BlockSpec((1, blk, LANES), lambda b, h, i: (b, q0 + i, h)),
            pl.BlockSpec((1, seq, LANES), lambda b, h, i: (b, 0, PAIRS + h)),
            pl.BlockSpec((1, seq, LANES), lambda b, h, i: (b, 0, 2 * PAIRS + h)),
            pl.BlockSpec((2, blk, blk), lambda b, h, i: (h, 0, 0)),
            pl.BlockSpec((2, blk, blk), lambda b, h, i: (h, 0, 0)),
            pl.BlockSpec((2, 8, LANES), lambda b, h, i: (h, 0, 0)),
        ],
        out_specs=pl.BlockSpec((1, blk, LANES), lambda b, h, i: (b, i, h)),
        out_shape=jax.ShapeDtypeStruct((bsz, nq * blk, WIDTH), BF16),
        scratch_shapes=[
            pltpu.VMEM((seq, 2 * LANES), BF16),
            pltpu.VMEM((seq, LANES), BF16),
            pltpu.VMEM((LANES, LANES), F32),
        ],
        compiler_params=_cparams(("parallel", "parallel", "arbitrary")),
        name="moba",
    )(p3d, p3d, p3d, bias_own, bias_prev, bias_far)


def _shifted(x, carry_row):
    rows = lax.broadcasted_iota(jnp.int32, x.shape, 0)
    return jnp.where(rows == 0, carry_row, pltpu.roll(x, 1, axis=0))


def _rwkv_prep_kernel(pr_ref, pk_ref, pv_ref, pl_ref, mu_ref, vec_ref, ww_ref, wa_ref, wg_ref,
                      bd_ref, tri_ref,
                      rt_ref, kt_ref, kd_ref, bd_out_ref, v_ref, g_ref, bonus_ref, pend_ref,
                      carry_ref, *, chunk):
    @pl.when(pl.program_id(1) == 0)
    def _():
        carry_ref[...] = jnp.zeros_like(carry_ref)

    def mix(ref, j):
        x = ref[0]
        mu = mu_ref[0:1, j * WIDTH:(j + 1) * WIDTH]
        prev = _shifted(x, carry_ref[0:1, j * WIDTH:(j + 1) * WIDTH])
        carry_ref[0:1, j * WIDTH:(j + 1) * WIDTH] = x[x.shape[0] - 1:, :]
        return x + mu * (prev - x)

    r = mix(pr_ref, 0)
    k = mix(pk_ref, 1)
    v = mix(pv_ref, 2)
    lo = mix(pl_ref, 3)
    w0, a0, k_k, k_a, r_k = (vec_ref[i:i + 1, :] for i in range(5))
    xwa = lo[:, 0:LANES]
    xg = lo[:, LANES:3 * LANES]
    lw = jnp.dot(jnp.tanh(xwa), ww_ref[...], precision=HI, preferred_element_type=F32)
    la = jnp.dot(xwa, wa_ref[...], precision=HI, preferred_element_type=F32)
    g = jnp.dot(jax.nn.sigmoid(xg), wg_ref[...], precision=HI, preferred_element_type=F32)
    z = -(w0 + lw)
    softplus = jnp.maximum(z, 0.0) + jnp.log(1.0 + jnp.exp(-jnp.abs(z)))
    logw = -jnp.exp(-softplus - 0.5)
    a = jax.nn.sigmoid(a0 + la)
    kk = k * k_k
    ss = jnp.dot(kk * kk, bd_ref[...], precision=HI, preferred_element_type=F32)
    kk = kk / jnp.maximum(jnp.sqrt(ss), 1e-12)
    k2 = k * (1.0 + (a - 1.0) * k_a)
    rk = jnp.dot(r * k2 * r_k, bd_ref[...], precision=HI, preferred_element_type=F32)
    cs = jnp.dot(tri_ref[...], logw, precision=HI, preferred_element_type=F32)
    e_pos = jnp.exp(cs)
    e_neg = jnp.exp(-cs)
    rt_ref[0] = (r * e_pos).astype(rt_ref.dtype)
    kt_ref[0] = (kk * jnp.exp(cs - logw)).astype(kt_ref.dtype)
    kd_ref[0] = (k2 * e_neg).astype(kd_ref.dtype)
    bd_out_ref[0] = (kk * a * e_neg).astype(bd_out_ref.dtype)
    v_ref[0] = v.astype(v_ref.dtype)
    g_ref[0] = g
    bonus_ref[0] = rk * v
    ts = e_pos.shape[0]
    for c in range(ts // chunk):
        pend_ref[0, c:c + 1, :] = e_pos[(c + 1) * chunk - 1:(c + 1) * chunk, :]


def rwkv_prep(p3d, rwkv_mu, w0, w_lora_up, a0, a_lora_up, g_lora_up, k_k, k_a, r_k, *, ts=512):
    bsz, seq, _ = p3d.shape
    chunk = RWKV_CHUNK
    ts = min(ts, seq)
    mu = jnp.pad(rwkv_mu, (0, COL_B - COL_B_RAW)).reshape(1, COL_B)
    vec = jnp.stack([w0, a0, k_k, k_a, r_k.reshape(-1)] + [jnp.zeros_like(w0)] * 3).astype(F32)
    ww = jnp.zeros((LANES, WIDTH), F32).at[:DECAY_LORA].set(w_lora_up)
    wa = jnp.zeros((LANES, WIDTH), F32).at[DECAY_LORA:DECAY_LORA + AAA_LORA].set(a_lora_up)
    wg = jnp.zeros((2 * LANES, WIDTH), F32).at[:GATE_LORA].set(g_lora_up)
    hid = jnp.arange(WIDTH) // HEAD_DIM
    bd = (hid[:, None] == hid[None, :]).astype(F32)
    tix = jnp.arange(ts)
    tri = ((tix[:, None] // chunk == tix[None, :] // chunk) & (tix[None, :] <= tix[:, None])).astype(F32)
    c0 = COL_A // WIDTH
    big = jax.ShapeDtypeStruct((bsz, seq, WIDTH), F32)
    wspec = lambda shape: pl.BlockSpec(shape, lambda b, i: (0, 0))
    ospec = pl.BlockSpec((1, ts, WIDTH), lambda b, i: (b, i, 0))
    return pl.pallas_call(
        functools.partial(_rwkv_prep_kernel, chunk=chunk),
        grid=(bsz, seq // ts),
        in_specs=[
            pl.BlockSpec((1, ts, WIDTH), lambda b, i: (b, i, c0)),
            pl.BlockSpec((1, ts, WIDTH), lambda b, i: (b, i, c0 + 1)),
            pl.BlockSpec((1, ts, WIDTH), lambda b, i: (b, i, c0 + 2)),
            pl.BlockSpec((1, ts, WIDTH), lambda b, i: (b, i, c0 + 3)),
            wspec((1, COL_B)), wspec((8, WIDTH)), wspec((LANES, WIDTH)), wspec((LANES, WIDTH)),
            wspec((2 * LANES, WIDTH)), wspec((WIDTH, WIDTH)), wspec((ts, ts)),
        ],
        out_specs=[ospec] * 7 + [pl.BlockSpec((1, ts // chunk, WIDTH), lambda b, i: (b, i, 0))],
        out_shape=[jax.ShapeDtypeStruct((bsz, seq, WIDTH), BF16)] * 5 + [big] * 2
        + [jax.ShapeDtypeStruct((bsz, seq // chunk, WIDTH), F32)],
        scratch_shapes=[pltpu.VMEM((8, COL_B), F32)],
        compiler_params=_cparams(("parallel", "arbitrary")),
        name="rwkv_prep",
    )(p3d, p3d, p3d, p3d, mu, vec, ww, wa, wg, bd, tri)


def _rwkv_scan_kernel(rt_ref, kt_ref, kd_ref, bd_ref, v_ref, g_ref, bonus_ref, pend_ref, ln_ref, sin_ref,
                      o_ref, state_ref, *, chunk, prec):
    @pl.when(pl.program_id(1) == 0)
    def _():
        state_ref[...] = sin_ref[...]

    c2 = 2 * chunk
    lane = lax.broadcasted_iota(jnp.int32, (chunk, LANES), 1)
    first = lane < HEAD_DIM
    row = lax.broadcasted_iota(jnp.int32, (c2, c2), 0)
    col = lax.broadcasted_iota(jnp.int32, (c2, c2), 1)
    eye = (row == col).astype(F32)
    hrow = lax.broadcasted_iota(jnp.int32, (LANES, LANES), 0) // HEAD_DIM
    hcol = lax.broadcasted_iota(jnp.int32, (LANES, LANES), 1) // HEAD_DIM
    head_mean = jnp.where(hrow == hcol, 1.0 / HEAD_DIM, 0.0).astype(F32)
    nt = (((1,), (1,)), ((), ()))
    tn = (((0,), (0,)), ((), ()))
    dot = functools.partial(jnp.dot, precision=prec, preferred_element_type=F32)
    dotg = functools.partial(lax.dot_general, precision=prec, preferred_element_type=F32)

    def stack(x):
        return jnp.concatenate([jnp.where(first, x, 0.0), jnp.where(first, 0.0, x)], axis=0)

    pairs = range(PAIRS)
    sls = [slice(hp * LANES, (hp + 1) * LANES) for hp in pairs]
    rs, ks, kds, bs, vs = ([stack(ref[0, :, sl].astype(F32)) for sl in sls]
                           for ref in (rt_ref, kt_ref, kd_ref, bd_ref, v_ref))
    hts = [state_ref[0, hp] for hp in pairs]
    big = [dotg(jnp.concatenate([ks[hp], rs[hp]], axis=0), jnp.concatenate([bs[hp], kds[hp]], axis=0), nt)
           for hp in pairs]
    a_b = [jnp.where(row > col, big[hp][0:c2, 0:c2], 0.0) for hp in pairs]
    a_k = [jnp.where(row > col, big[hp][0:c2, c2:], 0.0) for hp in pairs]
    a_rb = [jnp.where(row >= col, big[hp][c2:, 0:c2], 0.0) for hp in pairs]
    a_rk = [jnp.where(row >= col, big[hp][c2:, c2:], 0.0) for hp in pairs]
    kh = [dotg(jnp.concatenate([ks[hp], rs[hp]], axis=0), hts[hp], nt) for hp in pairs]
    av = [dot(jnp.concatenate([a_k[hp], a_rk[hp]], axis=0), vs[hp]) for hp in pairs]
    vk = [dotg(vs[hp], kds[hp], tn) for hp in pairs]
    inv = [eye - a_b[hp] for hp in pairs]
    pw = [dot(a_b[hp], a_b[hp]) for hp in pairs]
    n_sq = int(math.log2(chunk)) - 1
    for lvl in range(n_sq):
        if lvl + 1 < n_sq:
            both = [dot(jnp.concatenate([inv[hp], pw[hp]], axis=0), pw[hp]) for hp in pairs]
            inv = [inv[hp] + both[hp][0:c2] for hp in pairs]
            pw = [both[hp][c2:] for hp in pairs]
        else:
            inv = [inv[hp] + dot(inv[hp], pw[hp]) for hp in pairs]
    us = [dot(inv[hp], kh[hp][0:c2] + av[hp][0:c2]) for hp in pairs]
    ub = [dotg(us[hp], bs[hp], tn) for hp in pairs]
    au = [dot(a_rb[hp], us[hp]) for hp in pairs]
    for hp in pairs:
        sl = sls[hp]
        pend = pend_ref[0, 0, 0:1, sl]
        state_ref[0, hp] = (hts[hp] + vk[hp] - ub[hp]) * pend
        os_ = kh[hp][c2:] + av[hp][c2:] - au[hp]
        o = os_[0:chunk] + os_[chunk:]
        mu = jnp.dot(o, head_mean, precision=HI, preferred_element_type=F32)
        d = o - mu
        var = jnp.dot(d * d, head_mean, precision=HI, preferred_element_type=F32)
        on = d * lax.rsqrt(var + GN_EPS) * ln_ref[0:1, sl] + ln_ref[1:2, sl]
        o_ref[0, :, sl] = ((on + bonus_ref[0, :, sl]) * g_ref[0, :, sl]).astype(o_ref.dtype)


def rwkv_scan(rt, kt, kd, bd, v, g, bonus, pend, lnx_g, lnx_b, *, state=None, c0=0, nc=None, prec=None):
    bsz, seq, _ = rt.shape
    chunk = RWKV_CHUNK
    n_chunks = seq // chunk
    nc = n_chunks - c0 if nc is None else nc
    ln = jnp.stack([lnx_g, lnx_b] + [jnp.zeros_like(lnx_g)] * 6).astype(F32)
    pend4 = pend.reshape(bsz, n_chunks, 1, WIDTH)
    if state is None:
        state = jnp.zeros((bsz, PAIRS, LANES, LANES), F32)
    spec = pl.BlockSpec((1, chunk, WIDTH), lambda b, c: (b, c0 + c, 0))
    sspec = pl.BlockSpec((1, PAIRS, LANES, LANES), lambda b, c: (b, 0, 0, 0))
    return pl.pallas_call(
        functools.partial(_rwkv_scan_kernel, chunk=chunk, prec=prec),
        grid=(bsz, nc),
        in_specs=[spec] * 7 + [
            pl.BlockSpec((1, 1, 1, WIDTH), lambda b, c: (b, c0 + c, 0, 0)),
            pl.BlockSpec((8, WIDTH), lambda b, c: (0, 0)),
            sspec,
        ],
        out_specs=[pl.BlockSpec((1, chunk, WIDTH), lambda b, c: (b, c, 0)), sspec],
        out_shape=[jax.ShapeDtypeStruct((bsz, nc * chunk, WIDTH), BF16),
                   jax.ShapeDtypeStruct((bsz, PAIRS, LANES, LANES), F32)],
        compiler_params=_cparams(("parallel", "arbitrary")),
        name="rwkv_scan",
    )(rt, kt, kd, bd, v, g, bonus, pend4, ln, state)


def _merge_kernel(x_ref, oa_ref, ob_ref, ga_ref, gb_ref, wa_ref, wb_ref, wo_ref, g2_ref,
                  h_ref, xn_ref, acc_ref):
    j = pl.program_id(1)

    @pl.when(j == 0)
    def _():
        acc_ref[...] = x_ref[...]

    ya = jnp.dot(oa_ref[...].astype(BF16), wa_ref[...], preferred_element_type=F32)
    yb = jnp.dot(ob_ref[...].astype(BF16), wb_ref[...], preferred_element_type=F32)
    y = jax.nn.sigmoid(ga_ref[...]) * ya + jax.nn.sigmoid(gb_ref[...]) * yb
    acc_ref[...] += jnp.dot(y.astype(BF16), wo_ref[...], preferred_element_type=F32)

    @pl.when(j == pl.num_programs(1) - 1)
    def _():
        h = acc_ref[...]
        h_ref[...] = h
        ms = jnp.mean(h * h, axis=-1, keepdims=True)
        xn_ref[...] = _pack_halves(h * lax.rsqrt(ms + RMS_EPS) * g2_ref[...])


def _pack_halves(x):
    half = x.shape[1] // 2
    lo = lax.bitcast_convert_type(x[:, :half].astype(BF16).astype(F32), jnp.int32)
    hi = lax.bitcast_convert_type(x[:, half:].astype(BF16).astype(F32), jnp.int32)
    return lax.bitwise_or(lax.shift_right_logical(lo, jnp.int32(16)), hi)


def _unpack_halves(words):
    lo, hi = _unpack_words(words)
    return jnp.concatenate([lo, hi], axis=1)


def merge_out(x2d, oa, ob, p2d, w_proj_a, w_proj_b, w_out, norm2_g, *, row0=0, prow0=0, tm=512):
    t, d = oa.shape[0], x2d.shape[1]
    r0 = row0 // tm
    p0 = prow0 // tm
    tn = WIDTH
    nj = d // tn
    g0 = COL_G_OFF // tn
    return pl.pallas_call(
        _merge_kernel,
        grid=(t // tm, nj),
        in_specs=[
            pl.BlockSpec((tm, d), lambda i, j: (r0 + i, 0)),
            pl.BlockSpec((tm, WIDTH), lambda i, j: (i, 0)),
            pl.BlockSpec((tm, WIDTH), lambda i, j: (i, 0)),
            pl.BlockSpec((tm, tn), lambda i, j: (p0 + i, g0 + j)),
            pl.BlockSpec((tm, tn), lambda i, j: (p0 + i, g0 + nj + j)),
            pl.BlockSpec((WIDTH, tn), lambda i, j: (0, j)),
            pl.BlockSpec((WIDTH, tn), lambda i, j: (0, j)),
            pl.BlockSpec((tn, d), lambda i, j: (j, 0)),
            pl.BlockSpec((1, d), lambda i, j: (0, 0)),
        ],
        out_specs=[pl.BlockSpec((tm, d), lambda i, j: (i, 0)), pl.BlockSpec((tm, d // 2), lambda i, j: (i, 0))],
        out_shape=[jax.ShapeDtypeStruct((t, d), F32), jax.ShapeDtypeStruct((t, d // 2), jnp.int32)],
        scratch_shapes=[pltpu.VMEM((tm, d), F32)],
        compiler_params=_cparams(("parallel", "arbitrary")),
        name="merge_out",
    )(x2d, oa, ob, p2d, p2d, w_proj_a.astype(BF16), w_proj_b.astype(BF16), w_out.astype(BF16),
      norm2_g.reshape(1, d))


PEER_HEADS = 8
PEER_NKEYS = 128
PEER_TOPK = 16
PEER_HALF = 128


def _topk_rows(s, k):
    n = s.shape[0]
    rows = lax.broadcasted_iota(jnp.int32, s.shape, 0).astype(F32)
    vals, ids = [], []
    for _ in range(k):
        m = jnp.max(s, axis=0, keepdims=True)
        first = jnp.min(jnp.where(s == m, rows, float(n)), axis=0, keepdims=True)
        vals.append(m)
        ids.append(first)
        s = jnp.where(rows == first, -jnp.inf, s)
    return jnp.concatenate(vals, axis=0), jnp.concatenate(ids, axis=0)


def _take_rows(table, ids):
    rows = lax.broadcasted_iota(jnp.int32, table.shape, 0).astype(F32)
    return jnp.sum(jnp.where(rows == ids, table, 0.0), axis=0, keepdims=True)


def _peer_route_kernel(xn_ref, wq_ref, sk_ref, idx_ref, gate_ref, *, prec):
    tt = xn_ref.shape[0]
    k = PEER_TOPK
    xn = _unpack_halves(xn_ref[...]) if xn_ref.dtype == jnp.int32 else xn_ref[...]
    q = jnp.dot(xn.astype(wq_ref.dtype), wq_ref[...], precision=prec, preferred_element_type=F32)
    nt = (((1,), (1,)), ((), ()))
    idx_rows, gate_rows = [], []
    half = k // 2
    for h in range(PEER_HEADS):
        tops = []
        for p in range(2):
            c0 = (h * 2 + p) * PEER_HALF
            s = lax.dot_general(sk_ref[h, p].astype(wq_ref.dtype), q[:, c0:c0 + PEER_HALF].astype(wq_ref.dtype),
                                nt, precision=prec, preferred_element_type=F32)
            tops.append(_topk_rows(s, k))
        (s0, i0), (s1, i1) = tops
        cs = [s0[0:1] + s1] + [s0[i:i + 1] + s1[0:half] for i in range(1, half)] + [s0[half:] + s1[0:1]]
        best_s, pos = _topk_rows(jnp.concatenate(cs, axis=0), k)
        mid = jnp.floor((pos - k) * (1.0 / half))
        end_mid = float(k + (half - 1) * half)
        i_rank = jnp.where(pos < k, 0.0, jnp.where(pos < end_mid, 1.0 + mid, pos - (end_mid - half)))
        j_rank = jnp.where(pos < k, pos, jnp.where(pos < end_mid, (pos - k) - half * mid, 0.0))
        ids = [_take_rows(i0, i_rank[n:n + 1]) * PEER_NKEYS + _take_rows(i1, j_rank[n:n + 1]) for n in range(k)]
        e = jnp.exp(best_s - best_s[0:1])
        gate_rows.append(e / jnp.sum(e, axis=0, keepdims=True))
        idx_rows.append(jnp.concatenate(ids, axis=0).astype(jnp.int32))
    idx_ref[...] = jnp.concatenate(idx_rows, axis=0).T
    gate_ref[...] = jnp.concatenate(gate_rows, axis=0).T


def peer_route(xn2d, peer_wq, peer_subkeys, *, tt=256, prec=None, wdtype=BF16):
    t, dx = xn2d.shape
    d, nq = peer_wq.shape
    n_sel = PEER_HEADS * PEER_TOPK
    return pl.pallas_call(
        functools.partial(_peer_route_kernel, prec=prec),
        grid=(t // tt,),
        in_specs=[
            pl.BlockSpec((tt, dx), lambda i: (i, 0)),
            pl.BlockSpec((d, nq), lambda i: (0, 0)),
            pl.BlockSpec((PEER_HEADS, 2, PEER_NKEYS, PEER_HALF), lambda i: (0, 0, 0, 0)),
        ],
        out_specs=[pl.BlockSpec((tt, n_sel), lambda i: (i, 0))] * 2,
        out_shape=[jax.ShapeDtypeStruct((t, n_sel), jnp.int32), jax.ShapeDtypeStruct((t, n_sel), F32)],
        compiler_params=_cparams(("parallel",)),
        name="peer_route",
    )(xn2d, peer_wq.astype(wdtype), peer_subkeys)


def _final_kernel(h_ref, y_ref, g_ref, *rest):
    o_ref = rest[-1]
    h = h_ref[...] + y_ref[...]
    ms = jnp.mean(h * h, axis=-1, keepdims=True)
    o_ref[...] = h * lax.rsqrt(ms + RMS_EPS) * g_ref[...]


def final_norm(h2d, y2d, g, *, out=None, row0=0, total_rows=None, tm=1024):
    t, d = h2d.shape
    total = t if total_rows is None else total_rows
    r0 = row0 // tm
    spec = pl.BlockSpec((tm, d), lambda i: (i, 0))
    in_specs = [spec, spec, pl.BlockSpec((1, d), lambda i: (0, 0))]
    args = [h2d, y2d, g.reshape(1, d)]
    aliases = {}
    if out is not None:
        in_specs.append(pl.BlockSpec(memory_space=pl.ANY))
        args.append(out)
        aliases = {3: 0}
    return pl.pallas_call(
        _final_kernel,
        grid=(t // tm,),
        in_specs=in_specs,
        out_specs=pl.BlockSpec((tm, d), lambda i: (r0 + i, 0)),
        out_shape=jax.ShapeDtypeStruct((total, d), F32),
        input_output_aliases=aliases,
        compiler_params=_cparams(("parallel",)),
        name="final_norm",
    )(*args)


SC_CORES = 2
SC_SUBCORES = 16
SC_LANES = 16
SC_WORKERS = SC_CORES * SC_SUBCORES
PEER_SEL = PEER_HEADS * PEER_TOPK
PEER_ROWS = 32
PEER_PARTS = PEER_SEL // PEER_ROWS
PEER_NBUF = 4
PEER_GROUP = 32
PEER_BF16_RUN = 4


def _pack_rows(w):
    half = w.shape[1] // 2
    bits = lax.bitcast_convert_type(w.astype(BF16), jnp.uint16).astype(jnp.uint32)
    return lax.bitcast_convert_type(bits[:, :half] | (bits[:, half:] << 16), jnp.int32)


def _unpack_words(w):
    lo = lax.bitcast_convert_type(lax.shift_left(w, jnp.int32(16)), F32)
    hi = lax.bitcast_convert_type(lax.bitwise_and(w, jnp.int32(-65536)), F32)
    return lo, hi


def _packed_dot(a_words, b_words):
    from jax.experimental.pallas import tpu_sc as plsc
    prods = [plsc.bitcast(a, BF16) * plsc.bitcast(b, BF16) for a, b in zip(a_words, b_words)]
    while len(prods) > 1:
        prods = [prods[k] + prods[k + 1] for k in range(0, len(prods), 2)]
    return _unpack_words(plsc.bitcast(prods[0], jnp.int32))


def _sc_mesh():
    from jax.experimental.pallas import tpu_sc as plsc
    return plsc.VectorSubcoreMesh(core_axis_name="c", subcore_axis_name="s",
                                  num_cores=SC_CORES, num_subcores=SC_SUBCORES)


def _sc_loop(n, body, carry):
    from jax.experimental.pallas import tpu_sc as plsc
    return plsc.parallel_loop(0, n, carry=carry)(body)


def _worker_base(tokens_per_worker):
    return (lax.axis_index("s") * SC_CORES + lax.axis_index("c")) * tokens_per_worker


def _gather_compute_loop(table_hbm, idx_v, rows_v, sem, stage_v, out_row, osem, grp, compute):
    n_gathers = PEER_PARTS * grp
    ahead = PEER_NBUF - 1

    def gather(j, b):
        i = j // PEER_PARTS if isinstance(j, int) else lax.shift_right_logical(j, PEER_PARTS.bit_length() - 1)
        h = j % PEER_PARTS if isinstance(j, int) else lax.bitwise_and(j, PEER_PARTS - 1)
        ids = idx_v.at[i, pl.ds(pl.multiple_of(h * PEER_ROWS, PEER_ROWS), PEER_ROWS)]
        return pltpu.make_async_copy(table_hbm.at[ids], rows_v.at[b], sem.at[b])

    def put(i, slot):
        return pltpu.make_async_copy(stage_v.at[slot], out_row(i), osem.at[slot])

    for j in range(ahead):
        gather(j, j).start()

    @pl.loop(0, n_gathers)
    def _(j):
        b = lax.bitwise_and(j, PEER_NBUF - 1)
        h = lax.bitwise_and(j, PEER_PARTS - 1)
        i = lax.shift_right_logical(j, PEER_PARTS.bit_length() - 1)
        slot = lax.bitwise_and(i, 1)

        @pl.when((h == 0) & (i >= 2))
        def _():
            put(i - 2, slot).wait()

        @pl.when(j + ahead < n_gathers)
        def _():
            gather(j + ahead, lax.bitwise_and(j + ahead, PEER_NBUF - 1)).start()

        gather(j, b).wait()
        compute(i, h, b, slot)

        @pl.when(h == PEER_PARTS - 1)
        def _():
            put(i, slot).start()

    put(grp - 2, 0).wait()
    put(grp - 1, 1).wait()


def peer_expert_dots(x_packed, idx, u_packed):
    t, half = x_packed.shape
    n_chunks = half // SC_LANES
    tpw = t // SC_WORKERS
    grp = min(PEER_GROUP, tpw)
    rows_tog = 8

    def body(x_hbm, idx_hbm, u_hbm, out_hbm, idx_v, x_v, rows_v, ps_v, sem, osem):
        base = _worker_base(tpw)

        def compute(i, h, b, slot):
            @pl.loop(0, PEER_ROWS // rows_tog)
            def _(rg):
                r0 = rg * rows_tog
                accs = [[None, None] for _ in range(rows_tog)]
                for c0 in range(0, n_chunks, PEER_BF16_RUN):
                    ats = [pl.ds((c0 + k) * SC_LANES, SC_LANES) for k in range(PEER_BF16_RUN)]
                    xw = [x_v[i, at] for at in ats]
                    for r in range(rows_tog):
                        terms = _packed_dot([rows_v[b, r0 + r, at] for at in ats], xw)
                        for k, term in enumerate(terms):
                            accs[r][k] = term if accs[r][k] is None else accs[r][k] + term
                for r in range(rows_tog):
                    at = pl.ds(pl.multiple_of((h * PEER_ROWS + r0 + r) * SC_LANES, SC_LANES), SC_LANES)
                    ps_v[slot, at] = accs[r][0] + accs[r][1]

        @pl.loop(0, tpw // grp)
        def _(g):
            t0 = base + g * grp
            pltpu.sync_copy(idx_hbm.at[pl.ds(t0, grp)], idx_v)
            pltpu.sync_copy(x_hbm.at[pl.ds(t0, grp)], x_v)
            _gather_compute_loop(u_hbm, idx_v, rows_v, sem, ps_v, lambda i: out_hbm.at[t0 + i], osem, grp, compute)

    return pl.kernel(
        body,
        out_type=jax.ShapeDtypeStruct((t, PEER_SEL * SC_LANES), F32),
        mesh=_sc_mesh(),
        scratch_types=[
            pltpu.VMEM((grp, PEER_SEL), jnp.int32),
            pltpu.VMEM((grp, half), jnp.int32),
            pltpu.VMEM((PEER_NBUF, PEER_ROWS, half), jnp.int32),
            pltpu.VMEM((2, PEER_SEL * SC_LANES), F32),
            pltpu.SemaphoreType.DMA((PEER_NBUF,)),
            pltpu.SemaphoreType.DMA((2,)),
        ],
        compiler_params=pltpu.CompilerParams(needs_layout_passes=False),
        name="peer_expert_dots",
    )(x_packed, idx, u_packed)


def peer_expert_mix(hgw, idx, v_packed):
    t = hgw.shape[0]
    half = v_packed.shape[1]
    d = 2 * half
    tpw = t // SC_WORKERS
    grp = min(PEER_GROUP, tpw)
    n_parts = 2
    cpp = half // SC_LANES // n_parts
    from jax.experimental.pallas import tpu_sc as plsc

    def body(hg_hbm, idx_hbm, v_hbm, out_hbm, idx_v, hg_v, rows_v, o_v2, sem, osem):
        base = _worker_base(tpw)

        def compute(i, h, b, slot):
            token = jnp.full((SC_LANES,), i, jnp.int32)
            for part in range(n_parts):
                def rbody(rq, accs):
                    r0 = rq * PEER_BF16_RUN
                    s = [plsc.load_gather(hg_v, [token, jnp.full((SC_LANES,), h * PEER_ROWS + r0 + k, jnp.int32)])
                         for k in range(PEER_BF16_RUN)]
                    new = []
                    for c in range(cpp):
                        at = pl.ds((part * cpp + c) * SC_LANES, SC_LANES)
                        lo, hi = _packed_dot([rows_v[b, r0 + k, at] for k in range(PEER_BF16_RUN)], s)
                        new.append(accs[2 * c] + lo)
                        new.append(accs[2 * c + 1] + hi)
                    return tuple(new)

                accs = _sc_loop(PEER_ROWS // PEER_BF16_RUN, rbody,
                                tuple(jnp.zeros((SC_LANES,), F32) for _ in range(2 * cpp)))
                def store(overwrite):
                    for c in range(cpp):
                        lo_at = pl.ds((part * cpp + c) * SC_LANES, SC_LANES)
                        hi_at = pl.ds(half + (part * cpp + c) * SC_LANES, SC_LANES)
                        if overwrite:
                            o_v2[slot, lo_at] = accs[2 * c]
                            o_v2[slot, hi_at] = accs[2 * c + 1]
                        else:
                            o_v2[slot, lo_at] = o_v2[slot, lo_at] + accs[2 * c]
                            o_v2[slot, hi_at] = o_v2[slot, hi_at] + accs[2 * c + 1]

                pl.when(h == 0)(functools.partial(store, True))
                pl.when(h != 0)(functools.partial(store, False))

        @pl.loop(0, tpw // grp)
        def _(g):
            t0 = base + g * grp
            pltpu.sync_copy(idx_hbm.at[pl.ds(t0, grp)], idx_v)
            pltpu.sync_copy(hg_hbm.at[pl.ds(t0, grp)], hg_v)
            _gather_compute_loop(v_hbm, idx_v, rows_v, sem, o_v2, lambda i: out_hbm.at[t0 + i], osem, grp, compute)

    return pl.kernel(
        body,
        out_type=jax.ShapeDtypeStruct((t, d), F32),
        mesh=_sc_mesh(),
        scratch_types=[
            pltpu.VMEM((grp, PEER_SEL), jnp.int32),
            pltpu.VMEM((grp, PEER_SEL), jnp.int32),
            pltpu.VMEM((PEER_NBUF, PEER_ROWS, half), jnp.int32),
            pltpu.VMEM((2, d), F32),
            pltpu.SemaphoreType.DMA((PEER_NBUF,)),
            pltpu.SemaphoreType.DMA((2,)),
        ],
        compiler_params=pltpu.CompilerParams(needs_layout_passes=False),
        name="peer_expert_mix",
    )(hgw, idx, v_packed)


def _peer_act_kernel(ps_ref, gate_ref, sum_ref, o_ref):
    ps = ps_ref[...]
    sel = sum_ref[...]
    hi = ps.astype(BF16)
    rest = ps - hi.astype(F32)
    mid = rest.astype(BF16)
    lo = (rest - mid.astype(F32)).astype(BF16)
    pre = (jnp.dot(hi, sel, preferred_element_type=F32) + jnp.dot(mid, sel, preferred_element_type=F32)
           + jnp.dot(lo, sel, preferred_element_type=F32))
    hg = 0.5 * pre * (1.0 + lax.erf(pre * (1.0 / math.sqrt(2.0)))) * gate_ref[...]
    bits = lax.bitcast_convert_type(hg.astype(BF16).astype(F32), jnp.int32)
    o_ref[...] = lax.bitwise_or(bits, lax.shift_right_logical(bits, jnp.int32(16)))


def peer_act(ps, gates, *, tm=512):
    t, n = ps.shape
    lane_sum = (jnp.arange(n)[:, None] // SC_LANES == jnp.arange(PEER_SEL)[None, :]).astype(BF16)
    return pl.pallas_call(
        _peer_act_kernel,
        grid=(t // tm,),
        in_specs=[
            pl.BlockSpec((tm, n), lambda i: (i, 0)),
            pl.BlockSpec((tm, PEER_SEL), lambda i: (i, 0)),
            pl.BlockSpec((n, PEER_SEL), lambda i: (0, 0)),
        ],
        out_specs=pl.BlockSpec((tm, PEER_SEL), lambda i: (i, 0)),
        out_shape=jax.ShapeDtypeStruct((t, PEER_SEL), jnp.int32),
        compiler_params=_cparams(("parallel",)),
        name="peer_act",
    )(ps, gates, lane_sum)


BATCH_GROUPS = 8


def kernel(x, norm1_g, w_in, rwkv_mu, w0, w_lora_up, a0, a_lora_up, g_lora_up, k_k, k_a, r_k, lnx_g, lnx_b,
           w_proj_a, w_proj_b, w_out, norm2_g, peer_wq, peer_subkeys, peer_u, peer_v, rel_bias, normf_g):
    bsz, seq, d = x.shape
    depth = norm1_g.shape[0]
    groups = BATCH_GROUPS if bsz % BATCH_GROUPS == 0 else 1
    gb = bsz // groups
    tg = gb * seq
    t = bsz * seq
    src = x.reshape(t, d)
    for l in range(depth):
        w_pad = jnp.concatenate([
            w_in[l][:, :COL_A + COL_B_RAW],
            jnp.zeros((d, COL_B - COL_B_RAW), w_in.dtype),
            w_in[l][:, COL_A + COL_B_RAW:]], axis=1).astype(BF16)
        u_packed = _pack_rows(peer_u[l])
        tables = {"v": _pack_rows(peer_v[l])}
        last = l == depth - 1

        def mix(pending, tie=None):
            row0, h2d, ps, gates, idx = pending
            hgx = peer_act(ps, gates)
            if tie is not None:
                tie, hgx = lax.optimization_barrier((tie, hgx))
            return tie, (row0, h2d, peer_expert_mix(hgx, idx, tables["v"]))

        outs = []

        def close(mixed):
            row0, h2d, y2d = mixed
            if last:
                outs.append(final_norm(h2d, y2d, normf_g, out=outs[-1] if outs else None, row0=row0, total_rows=t))
            else:
                outs.append(h2d + y2d)

        halves = gb == 1 and seq % (2 * MOBA_BLOCK) == 0 and (seq // 2) % (SC_WORKERS * PEER_GROUP) == 0

        pending = closing = None
        for g in range(groups):
            p2d = norm_proj(src, norm1_g[l], w_pad, row0=g * tg, rows=tg)
            p3d = p2d.reshape(gb, seq, -1)
            prep = state = None
            for s0, sn in ([(0, seq // 2), (seq // 2, seq // 2)] if halves and g == 0 else [(0, seq)]):
                oa = moba_attention(p3d, rel_bias, q0=s0 // MOBA_BLOCK, nq=sn // MOBA_BLOCK)
                if prep is None:
                    prep = tuple(rwkv_prep(p3d, rwkv_mu[l], w0[l], w_lora_up[l], a0[l], a_lora_up[l], g_lora_up[l],
                                           k_k[l], k_a[l], r_k[l]))
                mixed = None
                if pending is not None:
                    (oa, prep), mixed = mix(pending, (oa, prep))
                if closing is not None:
                    oa, y2d = lax.optimization_barrier((oa, closing[2]))
                    close(closing[:2] + (y2d,))
                    closing = None
                ob, state = rwkv_scan(*prep, lnx_g[l], lnx_b[l], state=state,
                                      c0=s0 // RWKV_CHUNK, nc=sn // RWKV_CHUNK)
                nt = gb * sn
                h2d, xn2 = merge_out(src, oa.reshape(nt, WIDTH), ob.reshape(nt, WIDTH), p2d, w_proj_a[l], w_proj_b[l],
                                     w_out[l], norm2_g[l], row0=g * tg + s0, prow0=s0)
                idx, gates = peer_route(xn2, peer_wq[l], peer_subkeys[l])
                if mixed is not None:
                    idx, y2d = lax.optimization_barrier((idx, mixed[2]))
                    closing = mixed[:2] + (y2d,)
                pending = (g * tg + s0, h2d, peer_expert_dots(xn2, idx, u_packed), gates, idx)
        if closing is not None:
            close(closing)
        close(mix(pending)[1])
        src = outs[-1] if last else jnp.concatenate(outs, axis=0)
    return src.reshape(bsz, seq, d)
```

```python
import functools
import math

import jax
import jax.numpy as jnp
from jax import lax
from jax.experimental import pallas as pl
from jax.experimental.pallas import tpu as pltpu

F32 = jnp.float32
BF16 = jnp.bfloat16
HI = lax.Precision.HIGHEST

LANES = 128
HEAD_DIM = 64
HEADS = 8
PAIRS = HEADS // 2
WIDTH = HEADS * HEAD_DIM
MOBA_BLOCK = 256
MOBA_TOPK = 3
MOBA_LO = 64
REL_BUCKETS = 32
REL_MAX_DIST = 128
DECAY_LORA = 64
AAA_LORA = 64
GATE_LORA = 160
GN_EPS = 64e-5
RMS_EPS = 1e-6
NEG = -1e30
RWKV_CHUNK = 64
COL_A = 3 * WIDTH
COL_B_RAW = 3 * WIDTH + DECAY_LORA + AAA_LORA + GATE_LORA
COL_B = 4 * WIDTH
COL_G_OFF = COL_A + COL_B
VMEM_LIMIT = 56 * 1024 * 1024


def _cparams(sem):
    return pltpu.CompilerParams(dimension_semantics=sem, vmem_limit_bytes=VMEM_LIMIT)


def _norm_proj_kernel(x_ref, g_ref, w_ref, o_ref, xn_ref):
    @pl.when(pl.program_id(1) == 0)
    def _():
        x = x_ref[...]
        ms = jnp.mean(x * x, axis=-1, keepdims=True)
        xn_ref[...] = (x * lax.rsqrt(ms + RMS_EPS) * g_ref[...]).astype(xn_ref.dtype)

    o_ref[...] = jnp.dot(xn_ref[...], w_ref[...], preferred_element_type=F32).astype(o_ref.dtype)


def norm_proj(x2d, g, w, *, row0=0, rows=None, tm=1024, tn=512, out_dtype=F32):
    d = x2d.shape[1]
    t = x2d.shape[0] if rows is None else rows
    n = w.shape[1]
    r0 = row0 // tm
    return pl.pallas_call(
        _norm_proj_kernel,
        grid=(t // tm, n // tn),
        in_specs=[
            pl.BlockSpec((tm, d), lambda i, j: (r0 + i, 0)),
            pl.BlockSpec((1, d), lambda i, j: (0, 0)),
            pl.BlockSpec((d, tn), lambda i, j: (0, j)),
        ],
        out_specs=pl.BlockSpec((tm, tn), lambda i, j: (i, j)),
        out_shape=jax.ShapeDtypeStruct((t, n), out_dtype),
        scratch_shapes=[pltpu.VMEM((tm, d), w.dtype)],
        compiler_params=_cparams(("parallel", "arbitrary")),
        name="norm_proj",
    )(x2d, g.reshape(1, d), w)


def _rel_bucket(dist):
    n = jnp.maximum(dist, 0)
    max_exact = REL_BUCKETS // 2
    nf = jnp.maximum(n, 1).astype(F32)
    large = max_exact + (jnp.log(nf / max_exact) / math.log(REL_MAX_DIST / max_exact)
                         * (REL_BUCKETS - max_exact)).astype(jnp.int32)
    large = jnp.minimum(large, REL_BUCKETS - 1)
    return jnp.where(n < max_exact, n, large)


def _moba_kernel(q_ref, k_ref, v_ref, bown_ref, bprev_ref, bfar_ref, o_ref,
                 kb_ref, vb_ref, kbar_ref, *, n_blocks, q0):
    qb = pl.program_id(2) + q0
    blk = MOBA_BLOCK
    scale = 1.0 / math.sqrt(HEAD_DIM)

    rows2 = 2 * blk
    nt = (((1,), (1,)), ((), ()))

    @pl.when(pl.program_id(2) == 0)
    def _():
        kbar_ref[...] = jnp.zeros_like(kbar_ref)
        lane_b = lax.broadcasted_iota(jnp.int32, (blk, LANES), 1)
        for n in range(n_blocks):
            kblk = k_ref[0, n * blk:(n + 1) * blk, :]
            kbar_ref[n:n + 1, :] = jnp.mean(kblk, axis=0, keepdims=True)
            kb_ref[n * blk:(n + 1) * blk, 0:LANES] = kblk.astype(BF16)
            kb_ref[n * blk:(n + 1) * blk, LANES:] = ((lane_b == n) | (lane_b == MOBA_LO + n)).astype(BF16)
        vb_ref[...] = v_ref[0].astype(BF16)

    q2 = q_ref[0]
    first = lax.broadcasted_iota(jnp.int32, (blk, LANES), 1) < HEAD_DIM
    qh = jnp.concatenate([jnp.where(first, q2, 0.0), jnp.where(first, 0.0, q2)], axis=0)
    lane = lax.broadcasted_iota(jnp.int32, (rows2, LANES), 1)
    rowi = lax.broadcasted_iota(jnp.int32, (rows2, LANES), 0)
    gate = lax.dot_general(qh.astype(BF16), kbar_ref[...].astype(BF16), nt, preferred_element_type=F32)
    g = jnp.where(lane < qb, gate, -jnp.inf)
    chosen = lane < 0
    lane_f = lane.astype(F32)
    for _ in range(MOBA_TOPK):
        m = jnp.max(g, axis=1, keepdims=True)
        idx = jnp.min(jnp.where(g == m, lane_f, float(LANES)), axis=1, keepdims=True)
        hit = (lane_f == idx) & (m > -jnp.inf)
        chosen = chosen | hit
        g = jnp.where(hit, -jnp.inf, g)
    nfar = qb - 1
    bfar = jnp.where(rowi < blk, bfar_ref[0, 0:1, 0:1], bfar_ref[1, 0:1, 0:1])
    bhi = bfar.astype(BF16).astype(F32)
    madd = jnp.where(lane < nfar, jnp.where(chosen, bhi, NEG),
                     jnp.where(lane == nfar, jnp.where(chosen, 0.0, NEG),
                               jnp.where((lane >= MOBA_LO) & (lane - MOBA_LO < nfar), bfar - bhi, 0.0)))
    q_aug = jnp.concatenate([(qh * scale).astype(BF16), madd.astype(BF16)], axis=1)

    prev0 = pl.multiple_of(jnp.maximum(nfar, 0) * blk, blk)
    own0 = pl.multiple_of(qb * blk, blk)
    s_prev = (lax.dot_general(q_aug, kb_ref[pl.ds(prev0, blk), :], nt, preferred_element_type=F32)
              + bprev_ref[...].reshape(rows2, blk) + jnp.where(qb > 0, 0.0, NEG))
    s_own = (lax.dot_general(q_aug, kb_ref[pl.ds(own0, blk), :], nt, preferred_element_type=F32)
             + bown_ref[...].reshape(rows2, blk))
    r = lax.broadcasted_iota(jnp.int32, (rows2, blk), 0)
    c = lax.broadcasted_iota(jnp.int32, (rows2, blk), 1)
    s_own = jnp.where(lax.bitwise_and(r, blk - 1) >= c, s_own, NEG)
    s = jnp.concatenate([s_prev, s_own], axis=1)
    m_i = jnp.max(s, axis=1, keepdims=True)
    p = jnp.exp(s - m_i)
    l_i = jnp.sum(p, axis=1, keepdims=True)
    v0 = jnp.concatenate([vb_ref[pl.ds(prev0, blk), :], vb_ref[pl.ds(own0, blk), :]], axis=0)
    acc = jnp.dot(p.astype(BF16), v0, preferred_element_type=F32)

    def body(it, carry):
        m_i, l_i, acc = carry
        k0 = pl.multiple_of(it * rows2, rows2)
        s = lax.dot_general(q_aug, kb_ref[pl.ds(k0, rows2), :], nt, preferred_element_type=F32)
        tail = jnp.where(2 * it + 1 < nfar, 0.0, NEG)
        s = jnp.concatenate([s[:, :blk], s[:, blk:] + tail], axis=1)
        m_new = jnp.maximum(m_i, jnp.max(s, axis=1, keepdims=True))
        alpha = jnp.exp(m_i - m_new)
        p = jnp.exp(s - m_new)
        l_new = alpha * l_i + jnp.sum(p, axis=1, keepdims=True)
        acc_new = alpha * acc + jnp.dot(p.astype(BF16), vb_ref[pl.ds(k0, rows2), :], preferred_element_type=F32)
        return m_new, l_new, acc_new

    m_i, l_i, acc = lax.fori_loop(0, (jnp.maximum(nfar, 0) + 1) // 2, body, (m_i, l_i, acc))
    out = acc / l_i
    o_ref[0] = jnp.where(first, out[:blk], out[blk:]).astype(o_ref.dtype)


def moba_attention(p3d, rel_bias, *, q0=0, nq=None):
    bsz, seq, _ = p3d.shape
    blk = MOBA_BLOCK
    n_blocks = seq // blk
    nq = n_blocks - q0 if nq is None else nq
    assert n_blocks <= MOBA_LO and seq % blk == 0
    span = 2 * blk
    by_dist = rel_bias[:, _rel_bucket(jnp.arange(span))].astype(F32)
    shift = jnp.arange(span)

    def toeplitz(c):
        k = jnp.where(shift < blk, shift, shift - span)
        s = by_dist[:, jnp.clip(c - k, 0, span - 1)]
        tiled = jnp.tile(s, (1, blk))[:, :blk * (span - 1)]
        return tiled.reshape(HEADS, blk, span - 1)[:, :, :blk]

    bias_own = toeplitz(0)
    bias_prev = toeplitz(blk)
    bias_far = jnp.broadcast_to(rel_bias[:, REL_BUCKETS - 1].astype(F32)[:, None, None], (HEADS, 8, LANES))
    kern = functools.partial(_moba_kernel, n_blocks=n_blocks, q0=q0)
    return pl.pallas_call(
        kern,
        grid=(bsz, PAIRS, nq),
        in_specs=[
            pl.BlockSpec((1, blk, LANES), lambda b, h, i: (b, q0 + i, h)),
            pl.BlockSpec((1, seq, LANES), lambda b, h, i: (b, 0, PAIRS + h)),
            pl.BlockSpec((1, seq, LANES), lambda b, h, i: (b, 0, 2 * PAIRS + h)),
            pl.BlockSpec((2, blk, blk), lambda b, h, i: (h, 0, 0)),
            pl.BlockSpec((2, blk, blk), lambda b, h, i: (h, 0, 0)),
            pl.BlockSpec((2, 8, LANES), lambda b, h, i: (h, 0, 0)),
        ],
        out_specs=pl.BlockSpec((1, blk, LANES), lambda b, h, i: (b, i, h)),
        out_shape=jax.ShapeDtypeStruct((bsz, nq * blk, WIDTH), BF16),
        scratch_shapes=[
            pltpu.VMEM((seq, 2 * LANES), BF16),
            pltpu.VMEM((seq, LANES), BF16),
            pltpu.VMEM((LANES, LANES), F32),
        ],
        compiler_params=_cparams(("parallel", "parallel", "arbitrary")),
        name="moba",
    )(p3d, p3d, p3d, bias_own, bias_prev, bias_far)


def _shifted(x, carry_row):
    rows = lax.broadcasted_iota(jnp.int32, x.shape, 0)
    return jnp.where(rows == 0, carry_row, pltpu.roll(x, 1, axis=0))


def _rwkv_prep_kernel(pr_ref, pk_ref, pv_ref, pl_ref, mu_ref, vec_ref, ww_ref, wa_ref, wg_ref,
                      bd_ref, tri_ref,
                      rt_ref, kt_ref, kd_ref, bd_out_ref, v_ref, g_ref, bonus_ref, pend_ref,
                      carry_ref, *, chunk):
    @pl.when(pl.program_id(1) == 0)
    def _():
        carry_ref[...] = jnp.zeros_like(carry_ref)

    def mix(ref, j):
        x = ref[0]
        mu = mu_ref[0:1, j * WIDTH:(j + 1) * WIDTH]
        prev = _shifted(x, carry_ref[0:1, j * WIDTH:(j + 1) * WIDTH])
        carry_ref[0:1, j * WIDTH:(j + 1) * WIDTH] = x[x.shape[0] - 1:, :]
        return x + mu * (prev - x)

    r = mix(pr_ref, 0)
    k = mix(pk_ref, 1)
    v = mix(pv_ref, 2)
    lo = mix(pl_ref, 3)
    w0, a0, k_k, k_a, r_k = (vec_ref[i:i + 1, :] for i in range(5))
    xwa = lo[:, 0:LANES]
    xg = lo[:, LANES:3 * LANES]
    lw = jnp.dot(jnp.tanh(xwa), ww_ref[...], precision=HI, preferred_element_type=F32)
    la = jnp.dot(xwa, wa_ref[...], precision=HI, preferred_element_type=F32)
    g = jnp.dot(jax.nn.sigmoid(xg), wg_ref[...], precision=HI, preferred_element_type=F32)
    z = -(w0 + lw)
    softplus = jnp.maximum(z, 0.0) + jnp.log(1.0 + jnp.exp(-jnp.abs(z)))
    logw = -jnp.exp(-softplus - 0.5)
    a = jax.nn.sigmoid(a0 + la)
    kk = k * k_k
    ss = jnp.dot(kk * kk, bd_ref[...], precision=HI, preferred_element_type=F32)
    kk = kk / jnp.maximum(jnp.sqrt(ss), 1e-12)
    k2 = k * (1.0 + (a - 1.0) * k_a)
    rk = jnp.dot(r * k2 * r_k, bd_ref[...], precision=HI, preferred_element_type=F32)
    cs = jnp.dot(tri_ref[...], logw, precision=HI, preferred_element_type=F32)
    e_pos = jnp.exp(cs)
    e_neg = jnp.exp(-cs)
    rt_ref[0] = (r * e_pos).astype(rt_ref.dtype)
    kt_ref[0] = (kk * jnp.exp(cs - logw)).astype(kt_ref.dtype)
    kd_ref[0] = (k2 * e_neg).astype(kd_ref.dtype)
    bd_out_ref[0] = (kk * a * e_neg).astype(bd_out_ref.dtype)
    v_ref[0] = v.astype(v_ref.dtype)
    g_ref[0] = g
    bonus_ref[0] = rk * v
    ts = e_pos.shape[0]
    for c in range(ts // chunk):
        pend_ref[0, c:c + 1, :] = e_pos[(c + 1) * chunk - 1:(c + 1) * chunk, :]


def rwkv_prep(p3d, rwkv_mu, w0, w_lora_up, a0, a_lora_up, g_lora_up, k_k, k_a, r_k, *, ts=512):
    bsz, seq, _ = p3d.shape
    chunk = RWKV_CHUNK
    ts = min(ts, seq)
    mu = jnp.pad(rwkv_mu, (0, COL_B - COL_B_RAW)).reshape(1, COL_B)
    vec = jnp.stack([w0, a0, k_k, k_a, r_k.reshape(-1)] + [jnp.zeros_like(w0)] * 3).astype(F32)
    ww = jnp.zeros((LANES, WIDTH), F32).at[:DECAY_LORA].set(w_lora_up)
    wa = jnp.zeros((LANES, WIDTH), F32).at[DECAY_LORA:DECAY_LORA + AAA_LORA].set(a_lora_up)
    wg = jnp.zeros((2 * LANES, WIDTH), F32).at[:GATE_LORA].set(g_lora_up)
    hid = jnp.arange(WIDTH) // HEAD_DIM
    bd = (hid[:, None] == hid[None, :]).astype(F32)
    tix = jnp.arange(ts)
    tri = ((tix[:, None] // chunk == tix[None, :] // chunk) & (tix[None, :] <= tix[:, None])).astype(F32)
    c0 = COL_A // WIDTH
    big = jax.ShapeDtypeStruct((bsz, seq, WIDTH), F32)
    wspec = lambda shape: pl.BlockSpec(shape, lambda b, i: (0, 0))
    ospec = pl.BlockSpec((1, ts, WIDTH), lambda b, i: (b, i, 0))
    return pl.pallas_call(
        functools.partial(_rwkv_prep_kernel, chunk=chunk),
        grid=(bsz, seq // ts),
        in_specs=[
            pl.BlockSpec((1, ts, WIDTH), lambda b, i: (b, i, c0)),
            pl.BlockSpec((1, ts, WIDTH), lambda b, i: (b, i, c0 + 1)),
            pl.BlockSpec((1, ts, WIDTH), lambda b, i: (b, i, c0 + 2)),
            pl.BlockSpec((1, ts, WIDTH), lambda b, i: (b, i, c0 + 3)),
            wspec((1, COL_B)), wspec((8, WIDTH)), wspec((LANES, WIDTH)), wspec((LANES, WIDTH)),
            wspec((2 * LANES, WIDTH)), wspec((WIDTH, WIDTH)), wspec((ts, ts)),
        ],
        out_specs=[ospec] * 7 + [pl.BlockSpec((1, ts // chunk, WIDTH), lambda b, i: (b, i, 0))],
        out_shape=[jax.ShapeDtypeStruct((bsz, seq, WIDTH), BF16)] * 5 + [big] * 2
        + [jax.ShapeDtypeStruct((bsz, seq // chunk, WIDTH), F32)],
        scratch_shapes=[pltpu.VMEM((8, COL_B), F32)],
        compiler_params=_cparams(("parallel", "arbitrary")),
        name="rwkv_prep",
    )(p3d, p3d, p3d, p3d, mu, vec, ww, wa, wg, bd, tri)


def _rwkv_scan_kernel(rt_ref, kt_ref, kd_ref, bd_ref, v_ref, g_ref, bonus_ref, pend_ref, ln_ref, sin_ref,
                      o_ref, state_ref, *, chunk, prec):
    @pl.when(pl.program_id(1) == 0)
    def _():
        state_ref[...] = sin_ref[...]

    c2 = 2 * chunk
    lane = lax.broadcasted_iota(jnp.int32, (chunk, LANES), 1)
    first = lane < HEAD_DIM
    row = lax.broadcasted_iota(jnp.int32, (c2, c2), 0)
    col = lax.broadcasted_iota(jnp.int32, (c2, c2), 1)
    eye = (row == col).astype(F32)
    hrow = lax.broadcasted_iota(jnp.int32, (LANES, LANES), 0) // HEAD_DIM
    hcol = lax.broadcasted_iota(jnp.int32, (LANES, LANES), 1) // HEAD_DIM
    head_mean = jnp.where(hrow == hcol, 1.0 / HEAD_DIM, 0.0).astype(F32)
    nt = (((1,), (1,)), ((), ()))
    tn = (((0,), (0,)), ((), ()))
    dot = functools.partial(jnp.dot, precision=prec, preferred_element_type=F32)
    dotg = functools.partial(lax.dot_general, precision=prec, preferred_element_type=F32)

    def stack(x):
        return jnp.concatenate([jnp.where(first, x, 0.0), jnp.where(first, 0.0, x)], axis=0)

    pairs = range(PAIRS)
    sls = [slice(hp * LANES, (hp + 1) * LANES) for hp in pairs]
    rs, ks, kds, bs, vs = ([stack(ref[0, :, sl].astype(F32)) for sl in sls]
                           for ref in (rt_ref, kt_ref, kd_ref, bd_ref, v_ref))
    hts = [state_ref[0, hp] for hp in pairs]
    big = [dotg(jnp.concatenate([ks[hp], rs[hp]], axis=0), jnp.concatenate([bs[hp], kds[hp]], axis=0), nt)
           for hp in pairs]
    a_b = [jnp.where(row > col, big[hp][0:c2, 0:c2], 0.0) for hp in pairs]
    a_k = [jnp.where(row > col, big[hp][0:c2, c2:], 0.0) for hp in pairs]
    a_rb = [jnp.where(row >= col, big[hp][c2:, 0:c2], 0.0) for hp in pairs]
    a_rk = [jnp.where(row >= col, big[hp][c2:, c2:], 0.0) for hp in pairs]
    kh = [dotg(jnp.concatenate([ks[hp], rs[hp]], axis=0), hts[hp], nt) for hp in pairs]
    av = [dot(jnp.concatenate([a_k[hp], a_rk[hp]], axis=0), vs[hp]) for hp in pairs]
    vk = [dotg(vs[hp], kds[hp], tn) for hp in pairs]
    inv = [eye - a_b[hp] for hp in pairs]
    pw = [dot(a_b[hp], a_b[hp]) for hp in pairs]
    n_sq = int(math.log2(chunk)) - 1
    for lvl in range(n_sq):
        if lvl + 1 < n_sq:
            both = [dot(jnp.concatenate([inv[hp], pw[hp]], axis=0), pw[hp]) for hp in pairs]
            inv = [inv[hp] + both[hp][0:c2] for hp in pairs]
            pw = [both[hp][c2:] for hp in pairs]
        else:
            inv = [inv[hp] + dot(inv[hp], pw[hp]) for hp in pairs]
    us = [dot(inv[hp], kh[hp][0:c2] + av[hp][0:c2]) for hp in pairs]
    ub = [dotg(us[hp], bs[hp], tn) for hp in pairs]
    au = [dot(a_rb[hp], us[hp]) for hp in pairs]
    for hp in pairs:
        sl = sls[hp]
        pend = pend_ref[0, 0, 0:1, sl]
        state_ref[0, hp] = (hts[hp] + vk[hp] - ub[hp]) * pend
        os_ = kh[hp][c2:] + av[hp][c2:] - au[hp]
        o = os_[0:chunk] + os_[chunk:]
        mu = jnp.dot(o, head_mean, precision=HI, preferred_element_type=F32)
        d = o - mu
        var = jnp.dot(d * d, head_mean, precision=HI, preferred_element_type=F32)
        on = d * lax.rsqrt(var + GN_EPS) * ln_ref[0:1, sl] + ln_ref[1:2, sl]
        o_ref[0, :, sl] = ((on + bonus_ref[0, :, sl]) * g_ref[0, :, sl]).astype(o_ref.dtype)


def rwkv_scan(rt, kt, kd, bd, v, g, bonus, pend, lnx_g, lnx_b, *, state=None, c0=0, nc=None, prec=None):
    bsz, seq, _ = rt.shape
    chunk = RWKV_CHUNK
    n_chunks = seq // chunk
    nc = n_chunks - c0 if nc is None else nc
    ln = jnp.stack([lnx_g, lnx_b] + [jnp.zeros_like(lnx_g)] * 6).astype(F32)
    pend4 = pend.reshape(bsz, n_chunks, 1, WIDTH)
    if state is None:
        state = jnp.zeros((bsz, PAIRS, LANES, LANES), F32)
    spec = pl.BlockSpec((1, chunk, WIDTH), lambda b, c: (b, c0 + c, 0))
    sspec = pl.BlockSpec((1, PAIRS, LANES, LANES), lambda b, c: (b, 0, 0, 0))
    return pl.pallas_call(
        functools.partial(_rwkv_scan_kernel, chunk=chunk, prec=prec),
        grid=(bsz, nc),
        in_specs=[spec] * 7 + [
            pl.BlockSpec((1, 1, 1, WIDTH), lambda b, c: (b, c0 + c, 0, 0)),
            pl.BlockSpec((8, WIDTH), lambda b, c: (0, 0)),
            sspec,
        ],
        out_specs=[pl.BlockSpec((1, chunk, WIDTH), lambda b, c: (b, c, 0)), sspec],
        out_shape=[jax.ShapeDtypeStruct((bsz, nc * chunk, WIDTH), BF16),
                   jax.ShapeDtypeStruct((bsz, PAIRS, LANES, LANES), F32)],
        compiler_params=_cparams(("parallel", "arbitrary")),
        name="rwkv_scan",
    )(rt, kt, kd, bd, v, g, bonus, pend4, ln, state)


def _merge_kernel(x_ref, oa_ref, ob_ref, ga_ref, gb_ref, wa_ref, wb_ref, wo_ref, g2_ref,
                  h_ref, xn_ref, acc_ref):
    j = pl.program_id(1)

    @pl.when(j == 0)
    def _():
        acc_ref[...] = x_ref[...]

    ya = jnp.dot(oa_ref[...].astype(BF16), wa_ref[...], preferred_element_type=F32)
    yb = jnp.dot(ob_ref[...].astype(BF16), wb_ref[...], preferred_element_type=F32)
    y = jax.nn.sigmoid(ga_ref[...]) * ya + jax.nn.sigmoid(gb_ref[...]) * yb
    acc_ref[...] += jnp.dot(y.astype(BF16), wo_ref[...], preferred_element_type=F32)

    @pl.when(j == pl.num_programs(1) - 1)
    def _():
        h = acc_ref[...]
        h_ref[...] = h
        ms = jnp.mean(h * h, axis=-1, keepdims=True)
        xn_ref[...] = _pack_halves(h * lax.rsqrt(ms + RMS_EPS) * g2_ref[...])


def _pack_halves(x):
    half = x.shape[1] // 2
    lo = lax.bitcast_convert_type(x[:, :half].astype(BF16).astype(F32), jnp.int32)
    hi = lax.bitcast_convert_type(x[:, half:].astype(BF16).astype(F32), jnp.int32)
    return lax.bitwise_or(lax.shift_right_logical(lo, jnp.int32(16)), hi)


def _unpack_halves(words):
    lo, hi = _unpack_words(words)
    return jnp.concatenate([lo, hi], axis=1)


def merge_out(x2d, oa, ob, p2d, w_proj_a, w_proj_b, w_out, norm2_g, *, row0=0, prow0=0, tm=512):
    t, d = oa.shape[0], x2d.shape[1]
    r0 = row0 // tm
    p0 = prow0 // tm
    tn = WIDTH
    nj = d // tn
    g0 = COL_G_OFF // tn
    return pl.pallas_call(
        _merge_kernel,
        grid=(t // tm, nj),
        in_specs=[
            pl.BlockSpec((tm, d), lambda i, j: (r0 + i, 0)),
            pl.BlockSpec((tm, WIDTH), lambda i, j: (i, 0)),
            pl.BlockSpec((tm, WIDTH), lambda i, j: (i, 0)),
            pl.BlockSpec((tm, tn), lambda i, j: (p0 + i, g0 + j)),
            pl.BlockSpec((tm, tn), lambda i, j: (p0 + i, g0 + nj + j)),
            pl.BlockSpec((WIDTH, tn), lambda i, j: (0, j)),
            pl.BlockSpec((WIDTH, tn), lambda i, j: (0, j)),
            pl.BlockSpec((tn, d), lambda i, j: (j, 0)),
            pl.BlockSpec((1, d), lambda i, j: (0, 0)),
        ],
        out_specs=[pl.BlockSpec((tm, d), lambda i, j: (i, 0)), pl.BlockSpec((tm, d // 2), lambda i, j: (i, 0))],
        out_shape=[jax.ShapeDtypeStruct((t, d), F32), jax.ShapeDtypeStruct((t, d // 2), jnp.int32)],
        scratch_shapes=[pltpu.VMEM((tm, d), F32)],
        compiler_params=_cparams(("parallel", "arbitrary")),
        name="merge_out",
    )(x2d, oa, ob, p2d, p2d, w_proj_a.astype(BF16), w_proj_b.astype(BF16), w_out.astype(BF16),
      norm2_g.reshape(1, d))


PEER_HEADS = 8
PEER_NKEYS = 128
PEER_TOPK = 16
PEER_HALF = 128


def _topk_rows(s, k):
    n = s.shape[0]
    rows = lax.broadcasted_iota(jnp.int32, s.shape, 0).astype(F32)
    vals, ids = [], []
    for _ in range(k):
        m = jnp.max(s, axis=0, keepdims=True)
        first = jnp.min(jnp.where(s == m, rows, float(n)), axis=0, keepdims=True)
        vals.append(m)
        ids.append(first)
        s = jnp.where(rows == first, -jnp.inf, s)
    return jnp.concatenate(vals, axis=0), jnp.concatenate(ids, axis=0)


def _take_rows(table, ids):
    rows = lax.broadcasted_iota(jnp.int32, table.shape, 0).astype(F32)
    return jnp.sum(jnp.where(rows == ids, table, 0.0), axis=0, keepdims=True)


def _peer_route_kernel(xn_ref, wq_ref, sk_ref, idx_ref, gate_ref, *, prec):
    tt = xn_ref.shape[0]
    k = PEER_TOPK
    xn = _unpack_halves(xn_ref[...]) if xn_ref.dtype == jnp.int32 else xn_ref[...]
    q = jnp.dot(xn.astype(wq_ref.dtype), wq_ref[...], precision=prec, preferred_element_type=F32)
    nt = (((1,), (1,)), ((), ()))
    idx_rows, gate_rows = [], []
    half = k // 2
    for h in range(PEER_HEADS):
        tops = []
        for p in range(2):
            c0 = (h * 2 + p) * PEER_HALF
            s = lax.dot_general(sk_ref[h, p].astype(wq_ref.dtype), q[:, c0:c0 + PEER_HALF].astype(wq_ref.dtype),
                                nt, precision=prec, preferred_element_type=F32)
            tops.append(_topk_rows(s, k))
        (s0, i0), (s1, i1) = tops
        cs = [s0[0:1] + s1] + [s0[i:i + 1] + s1[0:half] for i in range(1, half)] + [s0[half:] + s1[0:1]]
        best_s, pos = _topk_rows(jnp.concatenate(cs, axis=0), k)
        mid = jnp.floor((pos - k) * (1.0 / half))
        end_mid = float(k + (half - 1) * half)
        i_rank = jnp.where(pos < k, 0.0, jnp.where(pos < end_mid, 1.0 + mid, pos - (end_mid - half)))
        j_rank = jnp.where(pos < k, pos, jnp.where(pos < end_mid, (pos - k) - half * mid, 0.0))
        ids = [_take_rows(i0, i_rank[n:n + 1]) * PEER_NKEYS + _take_rows(i1, j_rank[n:n + 1]) for n in range(k)]
        e = jnp.exp(best_s - best_s[0:1])
        gate_rows.append(e / jnp.sum(e, axis=0, keepdims=True))
        idx_rows.append(jnp.concatenate(ids, axis=0).astype(jnp.int32))
    idx_ref[...] = jnp.concatenate(idx_rows, axis=0).T
    gate_ref[...] = jnp.concatenate(gate_rows, axis=0).T


def peer_route(xn2d, peer_wq, peer_subkeys, *, tt=256, prec=None, wdtype=BF16):
    t, dx = xn2d.shape
    d, nq = peer_wq.shape
    n_sel = PEER_HEADS * PEER_TOPK
    return pl.pallas_call(
        functools.partial(_peer_route_kernel, prec=prec),
        grid=(t // tt,),
        in_specs=[
            pl.BlockSpec((tt, dx), lambda i: (i, 0)),
            pl.BlockSpec((d, nq), lambda i: (0, 0)),
            pl.BlockSpec((PEER_HEADS, 2, PEER_NKEYS, PEER_HALF), lambda i: (0, 0, 0, 0)),
        ],
        out_specs=[pl.BlockSpec((tt, n_sel), lambda i: (i, 0))] * 2,
        out_shape=[jax.ShapeDtypeStruct((t, n_sel), jnp.int32), jax.ShapeDtypeStruct((t, n_sel), F32)],
        compiler_params=_cparams(("parallel",)),
        name="peer_route",
    )(xn2d, peer_wq.astype(wdtype), peer_subkeys)


def _final_kernel(h_ref, y_ref, g_ref, *rest):
    o_ref = rest[-1]
    h = h_ref[...] + y_ref[...]
    ms = jnp.mean(h * h, axis=-1, keepdims=True)
    o_ref[...] = h * lax.rsqrt(ms + RMS_EPS) * g_ref[...]


def final_norm(h2d, y2d, g, *, out=None, row0=0, total_rows=None, tm=1024):
    t, d = h2d.shape
    total = t if total_rows is None else total_rows
    r0 = row0 // tm
    spec = pl.BlockSpec((tm, d), lambda i: (i, 0))
    in_specs = [spec, spec, pl.BlockSpec((1, d), lambda i: (0, 0))]
    args = [h2d, y2d, g.reshape(1, d)]
    aliases = {}
    if out is not None:
        in_specs.append(pl.BlockSpec(memory_space=pl.ANY))
        args.append(out)
        aliases = {3: 0}
    return pl.pallas_call(
        _final_kernel,
        grid=(t // tm,),
        in_specs=in_specs,
        out_specs=pl.BlockSpec((tm, d), lambda i: (r0 + i, 0)),
        out_shape=jax.ShapeDtypeStruct((total, d), F32),
        input_output_aliases=aliases,
        compiler_params=_cparams(("parallel",)),
        name="final_norm",
    )(*args)


SC_CORES = 2
SC_SUBCORES = 16
SC_LANES = 16
SC_WORKERS = SC_CORES * SC_SUBCORES
PEER_SEL = PEER_HEADS * PEER_TOPK
PEER_ROWS = 32
PEER_PARTS = PEER_SEL // PEER_ROWS
PEER_NBUF = 4
PEER_GROUP = 32
PEER_BF16_RUN = 4


def _pack_rows(w):
    half = w.shape[1] // 2
    rounded = lax.reduce_precision(w, exponent_bits=8, mantissa_bits=7)
    bits = lax.bitcast_convert_type(rounded, jnp.uint32)
    return lax.bitcast_convert_type((bits[:, :half] >> 16) | (bits[:, half:] & jnp.uint32(0xFFFF0000)), jnp.int32)


def _unpack_words(w):
    lo = lax.bitcast_convert_type(lax.shift_left(w, jnp.int32(16)), F32)
    hi = lax.bitcast_convert_type(lax.bitwise_and(w, jnp.int32(-65536)), F32)
    return lo, hi


def _packed_dot(a_words, b_words):
    from jax.experimental.pallas import tpu_sc as plsc
    prods = [plsc.bitcast(a, BF16) * plsc.bitcast(b, BF16) for a, b in zip(a_words, b_words)]
    while len(prods) > 1:
        prods = [prods[k] + prods[k + 1] for k in range(0, len(prods), 2)]
    return _unpack_words(plsc.bitcast(prods[0], jnp.int32))


def _sc_mesh():
    from jax.experimental.pallas import tpu_sc as plsc
    return plsc.VectorSubcoreMesh(core_axis_name="c", subcore_axis_name="s",
                                  num_cores=SC_CORES, num_subcores=SC_SUBCORES)


def _sc_loop(n, body, carry):
    from jax.experimental.pallas import tpu_sc as plsc
    return plsc.parallel_loop(0, n, carry=carry)(body)


def _worker_base(tokens_per_worker):
    return (lax.axis_index("s") * SC_CORES + lax.axis_index("c")) * tokens_per_worker


def _gather_compute_loop(table_hbm, idx_v, rows_v, sem, stage_v, out_row, osem, grp, compute):
    n_gathers = PEER_PARTS * grp
    ahead = PEER_NBUF - 1

    def gather(j, b):
        i = j // PEER_PARTS if isinstance(j, int) else lax.shift_right_logical(j, PEER_PARTS.bit_length() - 1)
        h = j % PEER_PARTS if isinstance(j, int) else lax.bitwise_and(j, PEER_PARTS - 1)
        ids = idx_v.at[i, pl.ds(pl.multiple_of(h * PEER_ROWS, PEER_ROWS), PEER_ROWS)]
        return pltpu.make_async_copy(table_hbm.at[ids], rows_v.at[b], sem.at[b])

    def put(i, slot):
        return pltpu.make_async_copy(stage_v.at[slot], out_row(i), osem.at[slot])

    for j in range(ahead):
        gather(j, j).start()

    @pl.loop(0, n_gathers)
    def _(j):
        b = lax.bitwise_and(j, PEER_NBUF - 1)
        h = lax.bitwise_and(j, PEER_PARTS - 1)
        i = lax.shift_right_logical(j, PEER_PARTS.bit_length() - 1)
        slot = lax.bitwise_and(i, 1)

        @pl.when((h == 0) & (i >= 2))
        def _():
            put(i - 2, slot).wait()

        @pl.when(j + ahead < n_gathers)
        def _():
            gather(j + ahead, lax.bitwise_and(j + ahead, PEER_NBUF - 1)).start()

        gather(j, b).wait()
        compute(i, h, b, slot)

        @pl.when(h == PEER_PARTS - 1)
        def _():
            put(i, slot).start()

    put(grp - 2, 0).wait()
    put(grp - 1, 1).wait()


def peer_expert_dots(x_packed, idx, u_packed):
    t, half = x_packed.shape
    n_chunks = half // SC_LANES
    tpw = t // SC_WORKERS
    grp = min(PEER_GROUP, tpw)
    rows_tog = 8

    def body(x_hbm, idx_hbm, u_hbm, out_hbm, idx_v, x_v, rows_v, ps_v, sem, osem):
        base = _worker_base(tpw)

        def compute(i, h, b, slot):
            @pl.loop(0, PEER_ROWS // rows_tog)
            def _(rg):
                r0 = rg * rows_tog
                accs = [[None, None] for _ in range(rows_tog)]
                for c0 in range(0, n_chunks, PEER_BF16_RUN):
                    ats = [pl.ds((c0 + k) * SC_LANES, SC_LANES) for k in range(PEER_BF16_RUN)]
                    xw = [x_v[i, at] for at in ats]
                    for r in range(rows_tog):
                        terms = _packed_dot([rows_v[b, r0 + r, at] for at in ats], xw)
                        for k, term in enumerate(terms):
                            accs[r][k] = term if accs[r][k] is None else accs[r][k] + term
                for r in range(rows_tog):
                    at = pl.ds(pl.multiple_of((h * PEER_ROWS + r0 + r) * SC_LANES, SC_LANES), SC_LANES)
                    ps_v[slot, at] = accs[r][0] + accs[r][1]

        @pl.loop(0, tpw // grp)
        def _(g):
            t0 = base + g * grp
            pltpu.sync_copy(idx_hbm.at[pl.ds(t0, grp)], idx_v)
            pltpu.sync_copy(x_hbm.at[pl.ds(t0, grp)], x_v)
            _gather_compute_loop(u_hbm, idx_v, rows_v, sem, ps_v, lambda i: out_hbm.at[t0 + i], osem, grp, compute)

    return pl.kernel(
        body,
        out_type=jax.ShapeDtypeStruct((t, PEER_SEL * SC_LANES), F32),
        mesh=_sc_mesh(),
        scratch_types=[
            pltpu.VMEM((grp, PEER_SEL), jnp.int32),
            pltpu.VMEM((grp, half), jnp.int32),
            pltpu.VMEM((PEER_NBUF, PEER_ROWS, half), jnp.int32),
            pltpu.VMEM((2, PEER_SEL * SC_LANES), F32),
            pltpu.SemaphoreType.DMA((PEER_NBUF,)),
            pltpu.SemaphoreType.DMA((2,)),
        ],
        compiler_params=pltpu.CompilerParams(needs_layout_passes=False),
        name="peer_expert_dots",
    )(x_packed, idx, u_packed)


def peer_expert_mix(hgw, idx, v_packed):
    t = hgw.shape[0]
    half = v_packed.shape[1]
    d = 2 * half
    tpw = t // SC_WORKERS
    grp = min(PEER_GROUP, tpw)
    n_parts = 2
    cpp = half // SC_LANES // n_parts
    from jax.experimental.pallas import tpu_sc as plsc

    def body(hg_hbm, idx_hbm, v_hbm, out_hbm, idx_v, hg_v, rows_v, o_v2, sem, osem):
        base = _worker_base(tpw)

        def compute(i, h, b, slot):
            token = jnp.full((SC_LANES,), i, jnp.int32)
            for part in range(n_parts):
                def rbody(rq, accs):
                    r0 = rq * PEER_BF16_RUN
                    s = [plsc.load_gather(hg_v, [token, jnp.full((SC_LANES,), h * PEER_ROWS + r0 + k, jnp.int32)])
                         for k in range(PEER_BF16_RUN)]
                    new = []
                    for c in range(cpp):
                        at = pl.ds((part * cpp + c) * SC_LANES, SC_LANES)
                        lo, hi = _packed_dot([rows_v[b, r0 + k, at] for k in range(PEER_BF16_RUN)], s)
                        new.append(accs[2 * c] + lo)
                        new.append(accs[2 * c + 1] + hi)
                    return tuple(new)

                accs = _sc_loop(PEER_ROWS // PEER_BF16_RUN, rbody,
                                tuple(jnp.zeros((SC_LANES,), F32) for _ in range(2 * cpp)))
                def store(overwrite):
                    for c in range(cpp):
                        lo_at = pl.ds((part * cpp + c) * SC_LANES, SC_LANES)
                        hi_at = pl.ds(half + (part * cpp + c) * SC_LANES, SC_LANES)
                        if overwrite:
                            o_v2[slot, lo_at] = accs[2 * c]
                            o_v2[slot, hi_at] = accs[2 * c + 1]
                        else:
                            o_v2[slot, lo_at] = o_v2[slot, lo_at] + accs[2 * c]
                            o_v2[slot, hi_at] = o_v2[slot, hi_at] + accs[2 * c + 1]

                pl.when(h == 0)(functools.partial(store, True))
                pl.when(h != 0)(functools.partial(store, False))

        @pl.loop(0, tpw // grp)
        def _(g):
            t0 = base + g * grp
            pltpu.sync_copy(idx_hbm.at[pl.ds(t0, grp)], idx_v)
            pltpu.sync_copy(hg_hbm.at[pl.ds(t0, grp)], hg_v)
            _gather_compute_loop(v_hbm, idx_v, rows_v, sem, o_v2, lambda i: out_hbm.at[t0 + i], osem, grp, compute)

    return pl.kernel(
        body,
        out_type=jax.ShapeDtypeStruct((t, d), F32),
        mesh=_sc_mesh(),
        scratch_types=[
            pltpu.VMEM((grp, PEER_SEL), jnp.int32),
            pltpu.VMEM((grp, PEER_SEL), jnp.int32),
            pltpu.VMEM((PEER_NBUF, PEER_ROWS, half), jnp.int32),
            pltpu.VMEM((2, d), F32),
            pltpu.SemaphoreType.DMA((PEER_NBUF,)),
            pltpu.SemaphoreType.DMA((2,)),
        ],
        compiler_params=pltpu.CompilerParams(needs_layout_passes=False),
        name="peer_expert_mix",
    )(hgw, idx, v_packed)


def _peer_act_kernel(ps_ref, gate_ref, sum_ref, o_ref):
    ps = ps_ref[...]
    sel = sum_ref[...]
    hi = ps.astype(BF16)
    rest = ps - hi.astype(F32)
    mid = rest.astype(BF16)
    lo = (rest - mid.astype(F32)).astype(BF16)
    pre = (jnp.dot(hi, sel, preferred_element_type=F32) + jnp.dot(mid, sel, preferred_element_type=F32)
           + jnp.dot(lo, sel, preferred_element_type=F32))
    hg = 0.5 * pre * (1.0 + lax.erf(pre * (1.0 / math.sqrt(2.0)))) * gate_ref[...]
    bits = lax.bitcast_convert_type(hg.astype(BF16).astype(F32), jnp.int32)
    o_ref[...] = lax.bitwise_or(bits, lax.shift_right_logical(bits, jnp.int32(16)))


def peer_act(ps, gates, *, tm=512):
    t, n = ps.shape
    lane_sum = (jnp.arange(n)[:, None] // SC_LANES == jnp.arange(PEER_SEL)[None, :]).astype(BF16)
    return pl.pallas_call(
        _peer_act_kernel,
        grid=(t // tm,),
        in_specs=[
            pl.BlockSpec((tm, n), lambda i: (i, 0)),
            pl.BlockSpec((tm, PEER_SEL), lambda i: (i, 0)),
            pl.BlockSpec((n, PEER_SEL), lambda i: (0, 0)),
        ],
        out_specs=pl.BlockSpec((tm, PEER_SEL), lambda i: (i, 0)),
        out_shape=jax.ShapeDtypeStruct((t, PEER_SEL), jnp.int32),
        compiler_params=_cparams(("parallel",)),
        name="peer_act",
    )(ps, gates, lane_sum)


BATCH_GROUPS = 8


def kernel(x, norm1_g, w_in, rwkv_mu, w0, w_lora_up, a0, a_lora_up, g_lora_up, k_k, k_a, r_k, lnx_g, lnx_b,
           w_proj_a, w_proj_b, w_out, norm2_g, peer_wq, peer_subkeys, peer_u, peer_v, rel_bias, normf_g):
    bsz, seq, d = x.shape
    depth = norm1_g.shape[0]
    groups = BATCH_GROUPS if bsz % BATCH_GROUPS == 0 else 1
    gb = bsz // groups
    tg = gb * seq
    t = bsz * seq
    src = x.reshape(t, d)
    for l in range(depth):
        w_pad = jnp.concatenate([
            w_in[l][:, :COL_A + COL_B_RAW],
            jnp.zeros((d, COL_B - COL_B_RAW), w_in.dtype),
            w_in[l][:, COL_A + COL_B_RAW:]], axis=1).astype(BF16)
        u_packed = _pack_rows(peer_u[l])
        tables = {"v": _pack_rows(peer_v[l])}
        last = l == depth - 1

        def mix(pending, tie=None):
            row0, h2d, ps, gates, idx = pending
            hgx = peer_act(ps, gates)
            if tie is not None:
                tie, hgx = lax.optimization_barrier((tie, hgx))
            return tie, (row0, h2d, peer_expert_mix(hgx, idx, tables["v"]))

        outs = []

        def close(mixed):
            row0, h2d, y2d = mixed
            if last:
                outs.append(final_norm(h2d, y2d, normf_g, out=outs[-1] if outs else None, row0=row0, total_rows=t))
            else:
                outs.append(h2d + y2d)

        halves = gb == 1 and seq % (2 * MOBA_BLOCK) == 0 and (seq // 2) % (SC_WORKERS * PEER_GROUP) == 0

        pending = closing = None
        for g in range(groups):
            p2d = norm_proj(src, norm1_g[l], w_pad, row0=g * tg, rows=tg)
            p3d = p2d.reshape(gb, seq, -1)
            prep = state = None
            for s0, sn in ([(0, seq // 2), (seq // 2, seq // 2)] if halves and g == 0 else [(0, seq)]):
                oa = moba_attention(p3d, rel_bias, q0=s0 // MOBA_BLOCK, nq=sn // MOBA_BLOCK)
                if prep is None:
                    prep = tuple(rwkv_prep(p3d, rwkv_mu[l], w0[l], w_lora_up[l], a0[l], a_lora_up[l], g_lora_up[l],
                                           k_k[l], k_a[l], r_k[l]))
                mixed = None
                if pending is not None:
                    (oa, prep), mixed = mix(pending, (oa, prep))
                if closing is not None:
                    oa, y2d = lax.optimization_barrier((oa, closing[2]))
                    close(closing[:2] + (y2d,))
                    closing = None
                ob, state = rwkv_scan(*prep, lnx_g[l], lnx_b[l], state=state,
                                      c0=s0 // RWKV_CHUNK, nc=sn // RWKV_CHUNK)
                nt = gb * sn
                h2d, xn2 = merge_out(src, oa.reshape(nt, WIDTH), ob.reshape(nt, WIDTH), p2d, w_proj_a[l], w_proj_b[l],
                                     w_out[l], norm2_g[l], row0=g * tg + s0, prow0=s0)
                idx, gates = peer_route(xn2, peer_wq[l], peer_subkeys[l])
                if mixed is not None:
                    idx, y2d = lax.optimization_barrier((idx, mixed[2]))
                    closing = mixed[:2] + (y2d,)
                pending = (g * tg + s0, h2d, peer_expert_dots(xn2, idx, u_packed), gates, idx)
        if closing is not None:
            close(closing)
        close(mix(pending)[1])
        src = outs[-1] if last else jnp.concatenate(outs, axis=0)
    return src.reshape(bsz, seq, d)
```

```python
import functools
import math

import jax
import jax.numpy as jnp
from jax import lax
from jax.experimental import pallas as pl
from jax.experimental.pallas import tpu as pltpu

F32 = jnp.float32
BF16 = jnp.bfloat16
HI = lax.Precision.HIGHEST

LANES = 128
HEAD_DIM = 64
HEADS = 8
PAIRS = HEADS // 2
WIDTH = HEADS * HEAD_DIM
MOBA_BLOCK = 256
MOBA_TOPK = 3
MOBA_LO = 64
REL_BUCKETS = 32
REL_MAX_DIST = 128
DECAY_LORA = 64
AAA_LORA = 64
GATE_LORA = 160
GN_EPS = 64e-5
RMS_EPS = 1e-6
NEG = -1e30
RWKV_CHUNK = 64
COL_A = 3 * WIDTH
COL_B_RAW = 3 * WIDTH + DECAY_LORA + AAA_LORA + GATE_LORA
COL_B = 4 * WIDTH
COL_G_OFF = COL_A + COL_B
VMEM_LIMIT = 56 * 1024 * 1024


def _cparams(sem):
    return pltpu.CompilerParams(dimension_semantics=sem, vmem_limit_bytes=VMEM_LIMIT)


def _norm_proj_kernel(x_ref, g_ref, w_ref, o_ref, xn_ref):
    @pl.when(pl.program_id(1) == 0)
    def _():
        x = x_ref[...]
        ms = jnp.mean(x * x, axis=-1, keepdims=True)
        xn_ref[...] = (x * lax.rsqrt(ms + RMS_EPS) * g_ref[...]).astype(xn_ref.dtype)

    o_ref[...] = jnp.dot(xn_ref[...], w_ref[...], preferred_element_type=F32).astype(o_ref.dtype)


def norm_proj(x2d, g, w, *, row0=0, rows=None, tm=1024, tn=512, out_dtype=F32):
    d = x2d.shape[1]
    t = x2d.shape[0] if rows is None else rows
    n = w.shape[1]
    r0 = row0 // tm
    return pl.pallas_call(
        _norm_proj_kernel,
        grid=(t // tm, n // tn),
        in_specs=[
            pl.BlockSpec((tm, d), lambda i, j: (r0 + i, 0)),
            pl.BlockSpec((1, d), lambda i, j: (0, 0)),
            pl.BlockSpec((d, tn), lambda i, j: (0, j)),
        ],
        out_specs=pl.BlockSpec((tm, tn), lambda i, j: (i, j)),
        out_shape=jax.ShapeDtypeStruct((t, n), out_dtype),
        scratch_shapes=[pltpu.VMEM((tm, d), w.dtype)],
        compiler_params=_cparams(("parallel", "arbitrary")),
        name="norm_proj",
    )(x2d, g.reshape(1, d), w)


def _rel_bucket(dist):
    n = jnp.maximum(dist, 0)
    max_exact = REL_BUCKETS // 2
    nf = jnp.maximum(n, 1).astype(F32)
    large = max_exact + (jnp.log(nf / max_exact) / math.log(REL_MAX_DIST / max_exact)
                         * (REL_BUCKETS - max_exact)).astype(jnp.int32)
    large = jnp.minimum(large, REL_BUCKETS - 1)
    return jnp.where(n < max_exact, n, large)


def _moba_kernel(q_ref, k_ref, v_ref, bown_ref, bprev_ref, bfar_ref, o_ref,
                 kb_ref, vb_ref, kbar_ref, *, n_blocks, q0):
    qb = pl.program_id(2) + q0
    blk = MOBA_BLOCK
    scale = 1.0 / math.sqrt(HEAD_DIM)

    rows2 = 2 * blk
    nt = (((1,), (1,)), ((), ()))

    @pl.when(pl.program_id(2) == 0)
    def _():
        kbar_ref[...] = jnp.zeros_like(kbar_ref)
        lane_b = lax.broadcasted_iota(jnp.int32, (blk, LANES), 1)
        for n in range(n_blocks):
            kblk = k_ref[0, n * blk:(n + 1) * blk, :]
            kbar_ref[n:n + 1, :] = jnp.mean(kblk, axis=0, keepdims=True)
            kb_ref[n * blk:(n + 1) * blk, 0:LANES] = kblk.astype(BF16)
            kb_ref[n * blk:(n + 1) * blk, LANES:] = ((lane_b == n) | (lane_b == MOBA_LO + n)).astype(BF16)
        vb_ref[...] = v_ref[0].astype(BF16)

    q2 = q_ref[0]
    first = lax.broadcasted_iota(jnp.int32, (blk, LANES), 1) < HEAD_DIM
    qh = jnp.concatenate([jnp.where(first, q2, 0.0), jnp.where(first, 0.0, q2)], axis=0)
    lane = lax.broadcasted_iota(jnp.int32, (rows2, LANES), 1)
    rowi = lax.broadcasted_iota(jnp.int32, (rows2, LANES), 0)
    gate = lax.dot_general(qh.astype(BF16), kbar_ref[...].astype(BF16), nt, preferred_element_type=F32)
    g = jnp.where(lane < qb, gate, -jnp.inf)
    chosen = lane < 0
    lane_f = lane.astype(F32)
    for _ in range(MOBA_TOPK):
        m = jnp.max(g, axis=1, keepdims=True)
        idx = jnp.min(jnp.where(g == m, lane_f, float(LANES)), axis=1, keepdims=True)
        hit = (lane_f == idx) & (m > -jnp.inf)
        chosen = chosen | hit
        g = jnp.where(hit, -jnp.inf, g)
    nfar = qb - 1
    bfar = jnp.where(rowi < blk, bfar_ref[0, 0:1, 0:1], bfar_ref[1, 0:1, 0:1])
    bhi = bfar.astype(BF16).astype(F32)
    madd = jnp.where(lane < nfar, jnp.where(chosen, bhi, NEG),
                     jnp.where(lane == nfar, jnp.where(chosen, 0.0, NEG),
                               jnp.where((lane >= MOBA_LO) & (lane - MOBA_LO < nfar), bfar - bhi, 0.0)))
    q_aug = jnp.concatenate([(qh * scale).astype(BF16), madd.astype(BF16)], axis=1)

    prev0 = pl.multiple_of(jnp.maximum(nfar, 0) * blk, blk)
    own0 = pl.multiple_of(qb * blk, blk)
    s_prev = (lax.dot_general(q_aug, kb_ref[pl.ds(prev0, blk), :], nt, preferred_element_type=F32)
              + bprev_ref[...].reshape(rows2, blk) + jnp.where(qb > 0, 0.0, NEG))
    s_own = (lax.dot_general(q_aug, kb_ref[pl.ds(own0, blk), :], nt, preferred_element_type=F32)
             + bown_ref[...].reshape(rows2, blk))
    r = lax.broadcasted_iota(jnp.int32, (rows2, blk), 0)
    c = lax.broadcasted_iota(jnp.int32, (rows2, blk), 1)
    s_own = jnp.where(lax.bitwise_and(r, blk - 1) >= c, s_own, NEG)
    s = jnp.concatenate([s_prev, s_own], axis=1)
    m_i = jnp.max(s, axis=1, keepdims=True)
    p = jnp.exp(s - m_i)
    l_i = jnp.sum(p, axis=1, keepdims=True)
    v0 = jnp.concatenate([vb_ref[pl.ds(prev0, blk), :], vb_ref[pl.ds(own0, blk), :]], axis=0)
    acc = jnp.dot(p.astype(BF16), v0, preferred_element_type=F32)

    def body(it, carry):
        m_i, l_i, acc = carry
        k0 = pl.multiple_of(it * rows2, rows2)
        s = lax.dot_general(q_aug, kb_ref[pl.ds(k0, rows2), :], nt, preferred_element_type=F32)
        tail = jnp.where(2 * it + 1 < nfar, 0.0, NEG)
        s = jnp.concatenate([s[:, :blk], s[:, blk:] + tail], axis=1)
        m_new = jnp.maximum(m_i, jnp.max(s, axis=1, keepdims=True))
        alpha = jnp.exp(m_i - m_new)
        p = jnp.exp(s - m_new)
        l_new = alpha * l_i + jnp.sum(p, axis=1, keepdims=True)
        acc_new = alpha * acc + jnp.dot(p.astype(BF16), vb_ref[pl.ds(k0, rows2), :], preferred_element_type=F32)
        return m_new, l_new, acc_new

    m_i, l_i, acc = lax.fori_loop(0, (jnp.maximum(nfar, 0) + 1) // 2, body, (m_i, l_i, acc))
    out = acc / l_i
    o_ref[0] = jnp.where(first, out[:blk], out[blk:]).astype(o_ref.dtype)


def moba_attention(p3d, rel_bias, *, q0=0, nq=None):
    bsz, seq, _ = p3d.shape
    blk = MOBA_BLOCK
    n_blocks = seq // blk
    nq = n_blocks - q0 if nq is None else nq
    assert n_blocks <= MOBA_LO and seq % blk == 0
    span = 2 * blk
    by_dist = rel_bias[:, _rel_bucket(jnp.arange(span))].astype(F32)
    shift = jnp.arange(span)

    def toeplitz(c):
        k = jnp.where(shift < blk, shift, shift - span)
        s = by_dist[:, jnp.clip(c - k, 0, span - 1)]
        tiled = jnp.tile(s, (1, blk))[:, :blk * (span - 1)]
        return tiled.reshape(HEADS, blk, span - 1)[:, :, :blk]

    bias_own = toeplitz(0)
    bias_prev = toeplitz(blk)
    bias_far = jnp.broadcast_to(rel_bias[:, REL_BUCKETS - 1].astype(F32)[:, None, None], (HEADS, 8, LANES))
    kern = functools.partial(_moba_kernel, n_blocks=n_blocks, q0=q0)
    return pl.pallas_call(
        kern,
        grid=(bsz, PAIRS, nq),
        in_specs=[
            pl.BlockSpec((1, blk, LANES), lambda b, h, i: (b, q0 + i, h)),
            pl.BlockSpec((1, seq, LANES), lambda b, h, i: (b, 0, PAIRS + h)),
            pl.BlockSpec((1, seq, LANES), lambda b, h, i: (b, 0, 2 * PAIRS + h)),
            pl.BlockSpec((2, blk, blk), lambda b, h, i: (h, 0, 0)),
            pl.BlockSpec((2, blk, blk), lambda b, h, i: (h, 0, 0)),
            pl.BlockSpec((2, 8, LANES), lambda b, h, i: (h, 0, 0)),
        ],
        out_specs=pl.BlockSpec((1, blk, LANES), lambda b, h, i: (b, i, h)),
        out_shape=jax.ShapeDtypeStruct((bsz, nq * blk, WIDTH), BF16),
        scratch_shapes=[
            pltpu.VMEM((seq, 2 * LANES), BF16),
            pltpu.VMEM((seq, LANES), BF16),
            pltpu.VMEM((LANES, LANES), F32),
        ],
        compiler_params=_cparams(("parallel", "parallel", "arbitrary")),
        name="moba",
    )(p3d, p3d, p3d, bias_own, bias_prev, bias_far)


def _shifted(x, carry_row):
    rows = lax.broadcasted_iota(jnp.int32, x.shape, 0)
    return jnp.where(rows == 0, carry_row, pltpu.roll(x, 1, axis=0))


def _rwkv_prep_kernel(pr_ref, pk_ref, pv_ref, pl_ref, mu_ref, vec_ref, ww_ref, wa_ref, wg_ref,
                      bd_ref, tri_ref,
                      rt_ref, kt_ref, kd_ref, bd_out_ref, v_ref, g_ref, bonus_ref, pend_ref,
                      carry_ref, *, chunk):
    @pl.when(pl.program_id(1) == 0)
    def _():
        carry_ref[...] = jnp.zeros_like(carry_ref)

    def mix(ref, j):
        x = ref[0]
        mu = mu_ref[0:1, j * WIDTH:(j + 1) * WIDTH]
        prev = _shifted(x, carry_ref[0:1, j * WIDTH:(j + 1) * WIDTH])
        carry_ref[0:1, j * WIDTH:(j + 1) * WIDTH] = x[x.shape[0] - 1:, :]
        return x + mu * (prev - x)

    r = mix(pr_ref, 0)
    k = mix(pk_ref, 1)
    v = mix(pv_ref, 2)
    lo = mix(pl_ref, 3)
    w0, a0, k_k, k_a, r_k = (vec_ref[i:i + 1, :] for i in range(5))
    xwa = lo[:, 0:LANES]
    xg = lo[:, LANES:3 * LANES]
    lw = jnp.dot(jnp.tanh(xwa), ww_ref[...], precision=HI, preferred_element_type=F32)
    la = jnp.dot(xwa, wa_ref[...], precision=HI, preferred_element_type=F32)
    g = jnp.dot(jax.nn.sigmoid(xg), wg_ref[...], precision=HI, preferred_element_type=F32)
    z = -(w0 + lw)
    softplus = jnp.maximum(z, 0.0) + jnp.log(1.0 + jnp.exp(-jnp.abs(z)))
    logw = -jnp.exp(-softplus - 0.5)
    a = jax.nn.sigmoid(a0 + la)
    kk = k * k_k
    ss = jnp.dot(kk * kk, bd_ref[...], precision=HI, preferred_element_type=F32)
    kk = kk / jnp.maximum(jnp.sqrt(ss), 1e-12)
    k2 = k * (1.0 + (a - 1.0) * k_a)
    rk = jnp.dot(r * k2 * r_k, bd_ref[...], precision=HI, preferred_element_type=F32)
    cs = jnp.dot(tri_ref[...], logw, precision=HI, preferred_element_type=F32)
    e_pos = jnp.exp(cs)
    e_neg = jnp.exp(-cs)
    rt_ref[0] = (r * e_pos).astype(rt_ref.dtype)
    kt_ref[0] = (kk * jnp.exp(cs - logw)).astype(kt_ref.dtype)
    kd_ref[0] = (k2 * e_neg).astype(kd_ref.dtype)
    bd_out_ref[0] = (kk * a * e_neg).astype(bd_out_ref.dtype)
    v_ref[0] = v.astype(v_ref.dtype)
    g_ref[0] = g
    bonus_ref[0] = rk * v
    ts = e_pos.shape[0]
    for c in range(ts // chunk):
        pend_ref[0, c:c + 1, :] = e_pos[(c + 1) * chunk - 1:(c + 1) * chunk, :]


def rwkv_prep(p3d, rwkv_mu, w0, w_lora_up, a0, a_lora_up, g_lora_up, k_k, k_a, r_k, *, ts=512):
    bsz, seq, _ = p3d.shape
    chunk = RWKV_CHUNK
    ts = min(ts, seq)
    mu = jnp.pad(rwkv_mu, (0, COL_B - COL_B_RAW)).reshape(1, COL_B)
    vec = jnp.stack([w0, a0, k_k, k_a, r_k.reshape(-1)] + [jnp.zeros_like(w0)] * 3).astype(F32)
    ww = jnp.zeros((LANES, WIDTH), F32).at[:DECAY_LORA].set(w_lora_up)
    wa = jnp.zeros((LANES, WIDTH), F32).at[DECAY_LORA:DECAY_LORA + AAA_LORA].set(a_lora_up)
    wg = jnp.zeros((2 * LANES, WIDTH), F32).at[:GATE_LORA].set(g_lora_up)
    hid = jnp.arange(WIDTH) // HEAD_DIM
    bd = (hid[:, None] == hid[None, :]).astype(F32)
    tix = jnp.arange(ts)
    tri = ((tix[:, None] // chunk == tix[None, :] // chunk) & (tix[None, :] <= tix[:, None])).astype(F32)
    c0 = COL_A // WIDTH
    big = jax.ShapeDtypeStruct((bsz, seq, WIDTH), F32)
    wspec = lambda shape: pl.BlockSpec(shape, lambda b, i: (0, 0))
    ospec = pl.BlockSpec((1, ts, WIDTH), lambda b, i: (b, i, 0))
    return pl.pallas_call(
        functools.partial(_rwkv_prep_kernel, chunk=chunk),
        grid=(bsz, seq // ts),
        in_specs=[
            pl.BlockSpec((1, ts, WIDTH), lambda b, i: (b, i, c0)),
            pl.BlockSpec((1, ts, WIDTH), lambda b, i: (b, i, c0 + 1)),
            pl.BlockSpec((1, ts, WIDTH), lambda b, i: (b, i, c0 + 2)),
            pl.BlockSpec((1, ts, WIDTH), lambda b, i: (b, i, c0 + 3)),
            wspec((1, COL_B)), wspec((8, WIDTH)), wspec((LANES, WIDTH)), wspec((LANES, WIDTH)),
            wspec((2 * LANES, WIDTH)), wspec((WIDTH, WIDTH)), wspec((ts, ts)),
        ],
        out_specs=[ospec] * 7 + [pl.BlockSpec((1, ts // chunk, WIDTH), lambda b, i: (b, i, 0))],
        out_shape=[jax.ShapeDtypeStruct((bsz, seq, WIDTH), BF16)] * 5 + [big] * 2
        + [jax.ShapeDtypeStruct((bsz, seq // chunk, WIDTH), F32)],
        scratch_shapes=[pltpu.VMEM((8, COL_B), F32)],
        compiler_params=_cparams(("parallel", "arbitrary")),
        name="rwkv_prep",
    )(p3d, p3d, p3d, p3d, mu, vec, ww, wa, wg, bd, tri)


def _rwkv_scan_kernel(rt_ref, kt_ref, kd_ref, bd_ref, v_ref, g_ref, bonus_ref, pend_ref, ln_ref, sin_ref,
                      o_ref, state_ref, *, chunk, prec):
    @pl.when(pl.program_id(1) == 0)
    def _():
        state_ref[...] = sin_ref[...]

    c2 = 2 * chunk
    lane = lax.broadcasted_iota(jnp.int32, (chunk, LANES), 1)
    first = lane < HEAD_DIM
    row = lax.broadcasted_iota(jnp.int32, (c2, c2), 0)
    col = lax.broadcasted_iota(jnp.int32, (c2, c2), 1)
    eye = (row == col).astype(F32)
    hrow = lax.broadcasted_iota(jnp.int32, (LANES, LANES), 0) // HEAD_DIM
    hcol = lax.broadcasted_iota(jnp.int32, (LANES, LANES), 1) // HEAD_DIM
    head_mean = jnp.where(hrow == hcol, 1.0 / HEAD_DIM, 0.0).astype(F32)
    nt = (((1,), (1,)), ((), ()))
    tn = (((0,), (0,)), ((), ()))
    dot = functools.partial(jnp.dot, precision=prec, preferred_element_type=F32)
    dotg = functools.partial(lax.dot_general, precision=prec, preferred_element_type=F32)

    def stack(x):
        return jnp.concatenate([jnp.where(first, x, 0.0), jnp.where(first, 0.0, x)], axis=0)

    pairs = range(PAIRS)
    sls = [slice(hp * LANES, (hp + 1) * LANES) for hp in pairs]
    rs, ks, kds, bs, vs = ([stack(ref[0, :, sl].astype(F32)) for sl in sls]
                           for ref in (rt_ref, kt_ref, kd_ref, bd_ref, v_ref))
    hts = [state_ref[0, hp] for hp in pairs]
    big = [dotg(jnp.concatenate([ks[hp], rs[hp]], axis=0), jnp.concatenate([bs[hp], kds[hp]], axis=0), nt)
           for hp in pairs]
    a_b = [jnp.where(row > col, big[hp][0:c2, 0:c2], 0.0) for hp in pairs]
    a_k = [jnp.where(row > col, big[hp][0:c2, c2:], 0.0) for hp in pairs]
    a_rb = [jnp.where(row >= col, big[hp][c2:, 0:c2], 0.0) for hp in pairs]
    a_rk = [jnp.where(row >= col, big[hp][c2:, c2:], 0.0) for hp in pairs]
    kh = [dotg(jnp.concatenate([ks[hp], rs[hp]], axis=0), hts[hp], nt) for hp in pairs]
    av = [dot(jnp.concatenate([a_k[hp], a_rk[hp]], axis=0), vs[hp]) for hp in pairs]
    vk = [dotg(vs[hp], kds[hp], tn) for hp in pairs]
    inv = [eye - a_b[hp] for hp in pairs]
    pw = [dot(a_b[hp], a_b[hp]) for hp in pairs]
    n_sq = int(math.log2(chunk)) - 1
    for lvl in range(n_sq):
        if lvl + 1 < n_sq:
            both = [dot(jnp.concatenate([inv[hp], pw[hp]], axis=0), pw[hp]) for hp in pairs]
            inv = [inv[hp] + both[hp][0:c2] for hp in pairs]
            pw = [both[hp][c2:] for hp in pairs]
        else:
            inv = [inv[hp] + dot(inv[hp], pw[hp]) for hp in pairs]
    us = [dot(inv[hp], kh[hp][0:c2] + av[hp][0:c2]) for hp in pairs]
    ub = [dotg(us[hp], bs[hp], tn) for hp in pairs]
    au = [dot(a_rb[hp], us[hp]) for hp in pairs]
    for hp in pairs:
        sl = sls[hp]
        pend = pend_ref[0, 0, 0:1, sl]
        state_ref[0, hp] = (hts[hp] + vk[hp] - ub[hp]) * pend
        os_ = kh[hp][c2:] + av[hp][c2:] - au[hp]
        o = os_[0:chunk] + os_[chunk:]
        mu = jnp.dot(o, head_mean, precision=HI, preferred_element_type=F32)
        d = o - mu
        var = jnp.dot(d * d, head_mean, precision=HI, preferred_element_type=F32)
        on = d * lax.rsqrt(var + GN_EPS) * ln_ref[0:1, sl] + ln_ref[1:2, sl]
        o_ref[0, :, sl] = ((on + bonus_ref[0, :, sl]) * g_ref[0, :, sl]).astype(o_ref.dtype)


def rwkv_scan(rt, kt, kd, bd, v, g, bonus, pend, lnx_g, lnx_b, *, state=None, c0=0, nc=None, prec=None):
    bsz, seq, _ = rt.shape
    chunk = RWKV_CHUNK
    n_chunks = seq // chunk
    nc = n_chunks - c0 if nc is None else nc
    ln = jnp.stack([lnx_g, lnx_b] + [jnp.zeros_like(lnx_g)] * 6).astype(F32)
    pend4 = pend.reshape(bsz, n_chunks, 1, WIDTH)
    if state is None:
        state = jnp.zeros((bsz, PAIRS, LANES, LANES), F32)
    spec = pl.BlockSpec((1, chunk, WIDTH), lambda b, c: (b, c0 + c, 0))
    sspec = pl.BlockSpec((1, PAIRS, LANES, LANES), lambda b, c: (b, 0, 0, 0))
    return pl.pallas_call(
        functools.partial(_rwkv_scan_kernel, chunk=chunk, prec=prec),
        grid=(bsz, nc),
        in_specs=[spec] * 7 + [
            pl.BlockSpec((1, 1, 1, WIDTH), lambda b, c: (b, c0 + c, 0, 0)),
            pl.BlockSpec((8, WIDTH), lambda b, c: (0, 0)),
            sspec,
        ],
        out_specs=[pl.BlockSpec((1, chunk, WIDTH), lambda b, c: (b, c, 0)), sspec],
        out_shape=[jax.ShapeDtypeStruct((bsz, nc * chunk, WIDTH), BF16),
                   jax.ShapeDtypeStruct((bsz, PAIRS, LANES, LANES), F32)],
        compiler_params=_cparams(("parallel", "arbitrary")),
        name="rwkv_scan",
    )(rt, kt, kd, bd, v, g, bonus, pend4, ln, state)


def _merge_kernel(x_ref, oa_ref, ob_ref, ga_ref, gb_ref, wa_ref, wb_ref, wo_ref, g2_ref,
                  h_ref, xn_ref, acc_ref):
    j = pl.program_id(1)

    @pl.when(j == 0)
    def _():
        acc_ref[...] = x_ref[...]

    ya = jnp.dot(oa_ref[...].astype(BF16), wa_ref[...], preferred_element_type=F32)
    yb = jnp.dot(ob_ref[...].astype(BF16), wb_ref[...], preferred_element_type=F32)
    y = jax.nn.sigmoid(ga_ref[...]) * ya + jax.nn.sigmoid(gb_ref[...]) * yb
    acc_ref[...] += jnp.dot(y.astype(BF16), wo_ref[...], preferred_element_type=F32)

    @pl.when(j == pl.num_programs(1) - 1)
    def _():
        h = acc_ref[...]
        h_ref[...] = h
        ms = jnp.mean(h * h, axis=-1, keepdims=True)
        xn_ref[...] = _pack_halves(h * lax.rsqrt(ms + RMS_EPS) * g2_ref[...])


def _pack_halves(x):
    half = x.shape[1] // 2
    lo = lax.bitcast_convert_type(x[:, :half].astype(BF16).astype(F32), jnp.int32)
    hi = lax.bitcast_convert_type(x[:, half:].astype(BF16).astype(F32), jnp.int32)
    return lax.bitwise_or(lax.shift_right_logical(lo, jnp.int32(16)), hi)


def _unpack_halves(words):
    lo, hi = _unpack_words(words)
    return jnp.concatenate([lo, hi], axis=1)


def merge_out(x2d, oa, ob, p2d, w_proj_a, w_proj_b, w_out, norm2_g, *, row0=0, prow0=0, tm=512):
    t, d = oa.shape[0], x2d.shape[1]
    r0 = row0 // tm
    p0 = prow0 // tm
    tn = WIDTH
    nj = d // tn
    g0 = COL_G_OFF // tn
    return pl.pallas_call(
        _merge_kernel,
        grid=(t // tm, nj),
        in_specs=[
            pl.BlockSpec((tm, d), lambda i, j: (r0 + i, 0)),
            pl.BlockSpec((tm, WIDTH), lambda i, j: (i, 0)),
            pl.BlockSpec((tm, WIDTH), lambda i, j: (i, 0)),
            pl.BlockSpec((tm, tn), lambda i, j: (p0 + i, g0 + j)),
            pl.BlockSpec((tm, tn), lambda i, j: (p0 + i, g0 + nj + j)),
            pl.BlockSpec((WIDTH, tn), lambda i, j: (0, j)),
            pl.BlockSpec((WIDTH, tn), lambda i, j: (0, j)),
            pl.BlockSpec((tn, d), lambda i, j: (j, 0)),
            pl.BlockSpec((1, d), lambda i, j: (0, 0)),
        ],
        out_specs=[pl.BlockSpec((tm, d), lambda i, j: (i, 0)), pl.BlockSpec((tm, d // 2), lambda i, j: (i, 0))],
        out_shape=[jax.ShapeDtypeStruct((t, d), F32), jax.ShapeDtypeStruct((t, d // 2), jnp.int32)],
        scratch_shapes=[pltpu.VMEM((tm, d), F32)],
        compiler_params=_cparams(("parallel", "arbitrary")),
        name="merge_out",
    )(x2d, oa, ob, p2d, p2d, w_proj_a.astype(BF16), w_proj_b.astype(BF16), w_out.astype(BF16),
      norm2_g.reshape(1, d))


PEER_HEADS = 8
PEER_NKEYS = 128
PEER_TOPK = 16
PEER_HALF = 128


def _topk_rows(s, k):
    n = s.shape[0]
    rows = lax.broadcasted_iota(jnp.int32, s.shape, 0).astype(F32)
    vals, ids = [], []
    for _ in range(k):
        m = jnp.max(s, axis=0, keepdims=True)
        first = jnp.min(jnp.where(s == m, rows, float(n)), axis=0, keepdims=True)
        vals.append(m)
        ids.append(first)
        s = jnp.where(rows == first, -jnp.inf, s)
    return jnp.concatenate(vals, axis=0), jnp.concatenate(ids, axis=0)


def _take_rows(table, ids):
    rows = lax.broadcasted_iota(jnp.int32, table.shape, 0).astype(F32)
    return jnp.sum(jnp.where(rows == ids, table, 0.0), axis=0, keepdims=True)


def _peer_route_kernel(xn_ref, wq_ref, sk_ref, idx_ref, gate_ref, *, prec):
    tt = xn_ref.shape[0]
    k = PEER_TOPK
    xn = _unpack_halves(xn_ref[...]) if xn_ref.dtype == jnp.int32 else xn_ref[...]
    q = jnp.dot(xn.astype(wq_ref.dtype), wq_ref[...], precision=prec, preferred_element_type=F32)
    nt = (((1,), (1,)), ((), ()))
    idx_rows, gate_rows = [], []
    half = k // 2
    for h in range(PEER_HEADS):
        tops = []
        for p in range(2):
            c0 = (h * 2 + p) * PEER_HALF
            s = lax.dot_general(sk_ref[h, p].astype(wq_ref.dtype), q[:, c0:c0 + PEER_HALF].astype(wq_ref.dtype),
                                nt, precision=prec, preferred_element_type=F32)
            tops.append(_topk_rows(s, k))
        (s0, i0), (s1, i1) = tops
        cs = [s0[0:1] + s1] + [s0[i:i + 1] + s1[0:half] for i in range(1, half)] + [s0[half:] + s1[0:1]]
        best_s, pos = _topk_rows(jnp.concatenate(cs, axis=0), k)
        mid = jnp.floor((pos - k) * (1.0 / half))
        end_mid = float(k + (half - 1) * half)
        i_rank = jnp.where(pos < k, 0.0, jnp.where(pos < end_mid, 1.0 + mid, pos - (end_mid - half)))
        j_rank = jnp.where(pos < k, pos, jnp.where(pos < end_mid, (pos - k) - half * mid, 0.0))
        ids = [_take_rows(i0, i_rank[n:n + 1]) * PEER_NKEYS + _take_rows(i1, j_rank[n:n + 1]) for n in range(k)]
        e = jnp.exp(best_s - best_s[0:1])
        gate_rows.append(e / jnp.sum(e, axis=0, keepdims=True))
        idx_rows.append(jnp.concatenate(ids, axis=0).astype(jnp.int32))
    idx_ref[...] = jnp.concatenate(idx_rows, axis=0).T
    gate_ref[...] = jnp.concatenate(gate_rows, axis=0).T


def peer_route(xn2d, peer_wq, peer_subkeys, *, tt=256, prec=None, wdtype=BF16):
    t, dx = xn2d.shape
    d, nq = peer_wq.shape
    n_sel = PEER_HEADS * PEER_TOPK
    return pl.pallas_call(
        functools.partial(_peer_route_kernel, prec=prec),
        grid=(t // tt,),
        in_specs=[
            pl.BlockSpec((tt, dx), lambda i: (i, 0)),
            pl.BlockSpec((d, nq), lambda i: (0, 0)),
            pl.BlockSpec((PEER_HEADS, 2, PEER_NKEYS, PEER_HALF), lambda i: (0, 0, 0, 0)),
        ],
        out_specs=[pl.BlockSpec((tt, n_sel), lambda i: (i, 0))] * 2,
        out_shape=[jax.ShapeDtypeStruct((t, n_sel), jnp.int32), jax.ShapeDtypeStruct((t, n_sel), F32)],
        compiler_params=_cparams(("parallel",)),
        name="peer_route",
    )(xn2d, peer_wq.astype(wdtype), peer_subkeys)


def _final_kernel(h_ref, y_ref, g_ref, *rest):
    o_ref = rest[-1]
    h = h_ref[...] + y_ref[...]
    ms = jnp.mean(h * h, axis=-1, keepdims=True)
    o_ref[...] = h * lax.rsqrt(ms + RMS_EPS) * g_ref[...]


def final_norm(h2d, y2d, g, *, out=None, row0=0, total_rows=None, tm=1024):
    t, d = h2d.shape
    total = t if total_rows is None else total_rows
    r0 = row0 // tm
    spec = pl.BlockSpec((tm, d), lambda i: (i, 0))
    in_specs = [spec, spec, pl.BlockSpec((1, d), lambda i: (0, 0))]
    args = [h2d, y2d, g.reshape(1, d)]
    aliases = {}
    if out is not None:
        in_specs.append(pl.BlockSpec(memory_space=pl.ANY))
        args.append(out)
        aliases = {3: 0}
    return pl.pallas_call(
        _final_kernel,
        grid=(t // tm,),
        in_specs=in_specs,
        out_specs=pl.BlockSpec((tm, d), lambda i: (r0 + i, 0)),
        out_shape=jax.ShapeDtypeStruct((total, d), F32),
        input_output_aliases=aliases,
        compiler_params=_cparams(("parallel",)),
        name="final_norm",
    )(*args)


SC_CORES = 2
SC_SUBCORES = 16
SC_LANES = 16
SC_WORKERS = SC_CORES * SC_SUBCORES
PEER_SEL = PEER_HEADS * PEER_TOPK
PEER_ROWS = 32
PEER_PARTS = PEER_SEL // PEER_ROWS
PEER_NBUF = 4
PEER_GROUP = 32
PEER_BF16_RUN = 4


def _pack_rows_kernel(w_ref, o_ref):
    o_ref[...] = _pack_halves(w_ref[...])


def _pack_rows(w, *, tr=1024):
    e, d = w.shape
    return pl.pallas_call(
        _pack_rows_kernel,
        grid=(e // tr,),
        in_specs=[pl.BlockSpec((tr, d), lambda i: (i, 0))],
        out_specs=pl.BlockSpec((tr, d // 2), lambda i: (i, 0)),
        out_shape=jax.ShapeDtypeStruct((e, d // 2), jnp.int32),
        compiler_params=_cparams(("parallel",)),
        name="pack_rows",
    )(w)


def _unpack_words(w):
    lo = lax.bitcast_convert_type(lax.shift_left(w, jnp.int32(16)), F32)
    hi = lax.bitcast_convert_type(lax.bitwise_and(w, jnp.int32(-65536)), F32)
    return lo, hi


def _packed_dot(a_words, b_words):
    from jax.experimental.pallas import tpu_sc as plsc
    prods = [plsc.bitcast(a, BF16) * plsc.bitcast(b, BF16) for a, b in zip(a_words, b_words)]
    while len(prods) > 1:
        prods = [prods[k] + prods[k + 1] for k in range(0, len(prods), 2)]
    return _unpack_words(plsc.bitcast(prods[0], jnp.int32))


def _sc_mesh():
    from jax.experimental.pallas import tpu_sc as plsc
    return plsc.VectorSubcoreMesh(core_axis_name="c", subcore_axis_name="s",
                                  num_cores=SC_CORES, num_subcores=SC_SUBCORES)


def _sc_loop(n, body, carry):
    from jax.experimental.pallas import tpu_sc as plsc
    return plsc.parallel_loop(0, n, carry=carry)(body)


def _worker_base(tokens_per_worker):
    return (lax.axis_index("s") * SC_CORES + lax.axis_index("c")) * tokens_per_worker


def _gather_compute_loop(table_hbm, idx_v, rows_v, sem, stage_v, out_row, osem, grp, compute):
    n_gathers = PEER_PARTS * grp
    ahead = PEER_NBUF - 1

    def gather(j, b):
        i = j // PEER_PARTS if isinstance(j, int) else lax.shift_right_logical(j, PEER_PARTS.bit_length() - 1)
        h = j % PEER_PARTS if isinstance(j, int) else lax.bitwise_and(j, PEER_PARTS - 1)
        ids = idx_v.at[i, pl.ds(pl.multiple_of(h * PEER_ROWS, PEER_ROWS), PEER_ROWS)]
        return pltpu.make_async_copy(table_hbm.at[ids], rows_v.at[b], sem.at[b])

    def put(i, slot):
        return pltpu.make_async_copy(stage_v.at[slot], out_row(i), osem.at[slot])

    for j in range(ahead):
        gather(j, j).start()

    @pl.loop(0, n_gathers)
    def _(j):
        b = lax.bitwise_and(j, PEER_NBUF - 1)
        h = lax.bitwise_and(j, PEER_PARTS - 1)
        i = lax.shift_right_logical(j, PEER_PARTS.bit_length() - 1)
        slot = lax.bitwise_and(i, 1)

        @pl.when((h == 0) & (i >= 2))
        def _():
            put(i - 2, slot).wait()

        @pl.when(j + ahead < n_gathers)
        def _():
            gather(j + ahead, lax.bitwise_and(j + ahead, PEER_NBUF - 1)).start()

        gather(j, b).wait()
        compute(i, h, b, slot)

        @pl.when(h == PEER_PARTS - 1)
        def _():
            put(i, slot).start()

    put(grp - 2, 0).wait()
    put(grp - 1, 1).wait()


def peer_expert_dots(x_packed, idx, u_packed):
    t, half = x_packed.shape
    n_chunks = half // SC_LANES
    tpw = t // SC_WORKERS
    grp = min(PEER_GROUP, tpw)
    rows_tog = 8

    def body(x_hbm, idx_hbm, u_hbm, out_hbm, idx_v, x_v, rows_v, ps_v, sem, osem):
        base = _worker_base(tpw)

        def compute(i, h, b, slot):
            @pl.loop(0, PEER_ROWS // rows_tog)
            def _(rg):
                r0 = rg * rows_tog
                accs = [[None, None] for _ in range(rows_tog)]
                for c0 in range(0, n_chunks, PEER_BF16_RUN):
                    ats = [pl.ds((c0 + k) * SC_LANES, SC_LANES) for k in range(PEER_BF16_RUN)]
                    xw = [x_v[i, at] for at in ats]
                    for r in range(rows_tog):
                        terms = _packed_dot([rows_v[b, r0 + r, at] for at in ats], xw)
                        for k, term in enumerate(terms):
                            accs[r][k] = term if accs[r][k] is None else accs[r][k] + term
                for r in range(rows_tog):
                    at = pl.ds(pl.multiple_of((h * PEER_ROWS + r0 + r) * SC_LANES, SC_LANES), SC_LANES)
                    ps_v[slot, at] = accs[r][0] + accs[r][1]

        @pl.loop(0, tpw // grp)
        def _(g):
            t0 = base + g * grp
            pltpu.sync_copy(idx_hbm.at[pl.ds(t0, grp)], idx_v)
            pltpu.sync_copy(x_hbm.at[pl.ds(t0, grp)], x_v)
            _gather_compute_loop(u_hbm, idx_v, rows_v, sem, ps_v, lambda i: out_hbm.at[t0 + i], osem, grp, compute)

    return pl.kernel(
        body,
        out_type=jax.ShapeDtypeStruct((t, PEER_SEL * SC_LANES), F32),
        mesh=_sc_mesh(),
        scratch_types=[
            pltpu.VMEM((grp, PEER_SEL), jnp.int32),
            pltpu.VMEM((grp, half), jnp.int32),
            pltpu.VMEM((PEER_NBUF, PEER_ROWS, half), jnp.int32),
            pltpu.VMEM((2, PEER_SEL * SC_LANES), F32),
            pltpu.SemaphoreType.DMA((PEER_NBUF,)),
            pltpu.SemaphoreType.DMA((2,)),
        ],
        compiler_params=pltpu.CompilerParams(needs_layout_passes=False),
        name="peer_expert_dots",
    )(x_packed, idx, u_packed)


def peer_expert_mix(hgw, idx, v_packed):
    t = hgw.shape[0]
    half = v_packed.shape[1]
    d = 2 * half
    tpw = t // SC_WORKERS
    grp = min(PEER_GROUP, tpw)
    n_parts = 2
    cpp = half // SC_LANES // n_parts
    from jax.experimental.pallas import tpu_sc as plsc

    def body(hg_hbm, idx_hbm, v_hbm, out_hbm, idx_v, hg_v, rows_v, o_v2, sem, osem):
        base = _worker_base(tpw)

        def compute(i, h, b, slot):
            token = jnp.full((SC_LANES,), i, jnp.int32)
            for part in range(n_parts):
                def rbody(rq, accs):
                    r0 = rq * PEER_BF16_RUN
                    s = [plsc.load_gather(hg_v, [token, jnp.full((SC_LANES,), h * PEER_ROWS + r0 + k, jnp.int32)])
                         for k in range(PEER_BF16_RUN)]
                    new = []
                    for c in range(cpp):
                        at = pl.ds((part * cpp + c) * SC_LANES, SC_LANES)
                        lo, hi = _packed_dot([rows_v[b, r0 + k, at] for k in range(PEER_BF16_RUN)], s)
                        new.append(accs[2 * c] + lo)
                        new.append(accs[2 * c + 1] + hi)
                    return tuple(new)

                accs = _sc_loop(PEER_ROWS // PEER_BF16_RUN, rbody,
                                tuple(jnp.zeros((SC_LANES,), F32) for _ in range(2 * cpp)))
                def store(overwrite):
                    for c in range(cpp):
                        lo_at = pl.ds((part * cpp + c) * SC_LANES, SC_LANES)
                        hi_at = pl.ds(half + (part * cpp + c) * SC_LANES, SC_LANES)
                        if overwrite:
                            o_v2[slot, lo_at] = accs[2 * c]
                            o_v2[slot, hi_at] = accs[2 * c + 1]
                        else:
                            o_v2[slot, lo_at] = o_v2[slot, lo_at] + accs[2 * c]
                            o_v2[slot, hi_at] = o_v2[slot, hi_at] + accs[2 * c + 1]

                pl.when(h == 0)(functools.partial(store, True))
                pl.when(h != 0)(functools.partial(store, False))

        @pl.loop(0, tpw // grp)
        def _(g):
            t0 = base + g * grp
            pltpu.sync_copy(idx_hbm.at[pl.ds(t0, grp)], idx_v)
            pltpu.sync_copy(hg_hbm.at[pl.ds(t0, grp)], hg_v)
            _gather_compute_loop(v_hbm, idx_v, rows_v, sem, o_v2, lambda i: out_hbm.at[t0 + i], osem, grp, compute)

    return pl.kernel(
        body,
        out_type=jax.ShapeDtypeStruct((t, d), F32),
        mesh=_sc_mesh(),
        scratch_types=[
            pltpu.VMEM((grp, PEER_SEL), jnp.int32),
            pltpu.VMEM((grp, PEER_SEL), jnp.int32),
            pltpu.VMEM((PEER_NBUF, PEER_ROWS, half), jnp.int32),
            pltpu.VMEM((2, d), F32),
            pltpu.SemaphoreType.DMA((PEER_NBUF,)),
            pltpu.SemaphoreType.DMA((2,)),
        ],
        compiler_params=pltpu.CompilerParams(needs_layout_passes=False),
        name="peer_expert_mix",
    )(hgw, idx, v_packed)


def _peer_act_kernel(ps_ref, gate_ref, sum_ref, o_ref):
    ps = ps_ref[...]
    sel = sum_ref[...]
    hi = ps.astype(BF16)
    rest = ps - hi.astype(F32)
    mid = rest.astype(BF16)
    lo = (rest - mid.astype(F32)).astype(BF16)
    pre = (jnp.dot(hi, sel, preferred_element_type=F32) + jnp.dot(mid, sel, preferred_element_type=F32)
           + jnp.dot(lo, sel, preferred_element_type=F32))
    hg = 0.5 * pre * (1.0 + lax.erf(pre * (1.0 / math.sqrt(2.0)))) * gate_ref[...]
    bits = lax.bitcast_convert_type(hg.astype(BF16).astype(F32), jnp.int32)
    o_ref[...] = lax.bitwise_or(bits, lax.shift_right_logical(bits, jnp.int32(16)))


def peer_act(ps, gates, *, tm=512):
    t, n = ps.shape
    lane_sum = (jnp.arange(n)[:, None] // SC_LANES == jnp.arange(PEER_SEL)[None, :]).astype(BF16)
    return pl.pallas_call(
        _peer_act_kernel,
        grid=(t // tm,),
        in_specs=[
            pl.BlockSpec((tm, n), lambda i: (i, 0)),
            pl.BlockSpec((tm, PEER_SEL), lambda i: (i, 0)),
            pl.BlockSpec((n, PEER_SEL), lambda i: (0, 0)),
        ],
        out_specs=pl.BlockSpec((tm, PEER_SEL), lambda i: (i, 0)),
        out_shape=jax.ShapeDtypeStruct((t, PEER_SEL), jnp.int32),
        compiler_params=_cparams(("parallel",)),
        name="peer_act",
    )(ps, gates, lane_sum)


BATCH_GROUPS = 8


def kernel(x, norm1_g, w_in, rwkv_mu, w0, w_lora_up, a0, a_lora_up, g_lora_up, k_k, k_a, r_k, lnx_g, lnx_b,
           w_proj_a, w_proj_b, w_out, norm2_g, peer_wq, peer_subkeys, peer_u, peer_v, rel_bias, normf_g):
    bsz, seq, d = x.shape
    depth = norm1_g.shape[0]
    groups = BATCH_GROUPS if bsz % BATCH_GROUPS == 0 else 1
    gb = bsz // groups
    tg = gb * seq
    t = bsz * seq
    src = x.reshape(t, d)
    for l in range(depth):
        w_pad = jnp.concatenate([
            w_in[l][:, :COL_A + COL_B_RAW],
            jnp.zeros((d, COL_B - COL_B_RAW), w_in.dtype),
            w_in[l][:, COL_A + COL_B_RAW:]], axis=1).astype(BF16)
        u_packed = _pack_rows(peer_u[l])
        tables = {"v": _pack_rows(peer_v[l])}
        last = l == depth - 1

        def mix(pending, tie=None):
            row0, h2d, ps, gates, idx = pending
            hgx = peer_act(ps, gates)
            if tie is not None:
                tie, hgx = lax.optimization_barrier((tie, hgx))
            return tie, (row0, h2d, peer_expert_mix(hgx, idx, tables["v"]))

        outs = []

        def close(mixed):
            row0, h2d, y2d = mixed
            if last:
                outs.append(final_norm(h2d, y2d, normf_g, out=outs[-1] if outs else None, row0=row0, total_rows=t))
            else:
                outs.append(h2d + y2d)

        halves = gb == 1 and seq % (2 * MOBA_BLOCK) == 0 and (seq // 2) % (SC_WORKERS * PEER_GROUP) == 0

        pending = closing = None
        for g in range(groups):
            p2d = norm_proj(src, norm1_g[l], w_pad, row0=g * tg, rows=tg)
            p3d = p2d.reshape(gb, seq, -1)
            prep = state = None
            for s0, sn in ([(0, seq // 2), (seq // 2, seq // 2)] if halves and g == 0 else [(0, seq)]):
                oa = moba_attention(p3d, rel_bias, q0=s0 // MOBA_BLOCK, nq=sn // MOBA_BLOCK)
                if prep is None:
                    prep = tuple(rwkv_prep(p3d, rwkv_mu[l], w0[l], w_lora_up[l], a0[l], a_lora_up[l], g_lora_up[l],
                                           k_k[l], k_a[l], r_k[l]))
                mixed = None
                if pending is not None:
                    (oa, prep), mixed = mix(pending, (oa, prep))
                if closing is not None:
                    oa, y2d = lax.optimization_barrier((oa, closing[2]))
                    close(closing[:2] + (y2d,))
                    closing = None
                ob, state = rwkv_scan(*prep, lnx_g[l], lnx_b[l], state=state,
                                      c0=s0 // RWKV_CHUNK, nc=sn // RWKV_CHUNK)
                nt = gb * sn
                h2d, xn2 = merge_out(src, oa.reshape(nt, WIDTH), ob.reshape(nt, WIDTH), p2d, w_proj_a[l], w_proj_b[l],
                                     w_out[l], norm2_g[l], row0=g * tg + s0, prow0=s0)
                idx, gates = peer_route(xn2, peer_wq[l], peer_subkeys[l])
                if mixed is not None:
                    idx, y2d = lax.optimization_barrier((idx, mixed[2]))
                    closing = mixed[:2] + (y2d,)
                pending = (g * tg + s0, h2d, peer_expert_dots(xn2, idx, u_packed), gates, idx)
        if closing is not None:
            close(closing)
        close(mix(pending)[1])
        src = outs[-1] if last else jnp.concatenate(outs, axis=0)
    return src.reshape(bsz, seq, d)
```

```python
import functools
import math

import jax
import jax.numpy as jnp
from jax import lax
from jax.experimental import pallas as pl
from jax.experimental.pallas import tpu as pltpu

F32 = jnp.float32
BF16 = jnp.bfloat16
HI = lax.Precision.HIGHEST

LANES = 128
HEAD_DIM = 64
HEADS = 8
PAIRS = HEADS // 2
WIDTH = HEADS * HEAD_DIM
MOBA_BLOCK = 256
MOBA_TOPK = 3
MOBA_LO = 64
REL_BUCKETS = 32
REL_MAX_DIST = 128
DECAY_LORA = 64
AAA_LORA = 64
GATE_LORA = 160
GN_EPS = 64e-5
RMS_EPS = 1e-6
NEG = -1e30
RWKV_CHUNK = 64
COL_A = 3 * WIDTH
COL_B_RAW = 3 * WIDTH + DECAY_LORA + AAA_LORA + GATE_LORA
COL_B = 4 * WIDTH
COL_G_OFF = COL_A + COL_B
VMEM_LIMIT = 56 * 1024 * 1024


def _cparams(sem):
    return pltpu.CompilerParams(dimension_semantics=sem, vmem_limit_bytes=VMEM_LIMIT)


def _norm_proj_kernel(x_ref, g_ref, w_ref, o_ref, xn_ref):
    @pl.when(pl.program_id(1) == 0)
    def _():
        x = x_ref[...]
        ms = jnp.mean(x * x, axis=-1, keepdims=True)
        xn_ref[...] = (x * lax.rsqrt(ms + RMS_EPS) * g_ref[...]).astype(xn_ref.dtype)

    o_ref[...] = jnp.dot(xn_ref[...], w_ref[...], preferred_element_type=F32).astype(o_ref.dtype)


def norm_proj(x2d, g, w, *, row0=0, rows=None, tm=1024, tn=512, out_dtype=F32):
    d = x2d.shape[1]
    t = x2d.shape[0] if rows is None else rows
    n = w.shape[1]
    r0 = row0 // tm
    return pl.pallas_call(
        _norm_proj_kernel,
        grid=(t // tm, n // tn),
        in_specs=[
            pl.BlockSpec((tm, d), lambda i, j: (r0 + i, 0)),
            pl.BlockSpec((1, d), lambda i, j: (0, 0)),
            pl.BlockSpec((d, tn), lambda i, j: (0, j)),
        ],
        out_specs=pl.BlockSpec((tm, tn), lambda i, j: (i, j)),
        out_shape=jax.ShapeDtypeStruct((t, n), out_dtype),
        scratch_shapes=[pltpu.VMEM((tm, d), w.dtype)],
        compiler_params=_cparams(("parallel", "arbitrary")),
        name="norm_proj",
    )(x2d, g.reshape(1, d), w)


def _rel_bucket(dist):
    n = jnp.maximum(dist, 0)
    max_exact = REL_BUCKETS // 2
    nf = jnp.maximum(n, 1).astype(F32)
    large = max_exact + (jnp.log(nf / max_exact) / math.log(REL_MAX_DIST / max_exact)
                         * (REL_BUCKETS - max_exact)).astype(jnp.int32)
    large = jnp.minimum(large, REL_BUCKETS - 1)
    return jnp.where(n < max_exact, n, large)


def _moba_kernel(q_ref, k_ref, v_ref, bown_ref, bprev_ref, bfar_ref, o_ref,
                 kb_ref, vb_ref, kbar_ref, *, n_blocks):
    qb = pl.program_id(2)
    blk = MOBA_BLOCK
    scale = 1.0 / math.sqrt(HEAD_DIM)

    rows2 = 2 * blk
    nt = (((1,), (1,)), ((), ()))

    @pl.when(qb == 0)
    def _():
        kbar_ref[...] = jnp.zeros_like(kbar_ref)
        lane_b = lax.broadcasted_iota(jnp.int32, (blk, LANES), 1)
        for n in range(n_blocks):
            kblk = k_ref[0, n * blk:(n + 1) * blk, :]
            kbar_ref[n:n + 1, :] = jnp.mean(kblk, axis=0, keepdims=True)
            kb_ref[n * blk:(n + 1) * blk, 0:LANES] = kblk.astype(BF16)
            kb_ref[n * blk:(n + 1) * blk, LANES:] = ((lane_b == n) | (lane_b == MOBA_LO + n)).astype(BF16)
        vb_ref[...] = v_ref[0].astype(BF16)

    q2 = q_ref[0]
    first = lax.broadcasted_iota(jnp.int32, (blk, LANES), 1) < HEAD_DIM
    qh = jnp.concatenate([jnp.where(first, q2, 0.0), jnp.where(first, 0.0, q2)], axis=0)
    lane = lax.broadcasted_iota(jnp.int32, (rows2, LANES), 1)
    rowi = lax.broadcasted_iota(jnp.int32, (rows2, LANES), 0)
    gate = lax.dot_general(qh.astype(BF16), kbar_ref[...].astype(BF16), nt, preferred_element_type=F32)
    g = jnp.where(lane < qb, gate, -jnp.inf)
    chosen = lane < 0
    lane_f = lane.astype(F32)
    for _ in range(MOBA_TOPK):
        m = jnp.max(g, axis=1, keepdims=True)
        idx = jnp.min(jnp.where(g == m, lane_f, float(LANES)), axis=1, keepdims=True)
        hit = (lane_f == idx) & (m > -jnp.inf)
        chosen = chosen | hit
        g = jnp.where(hit, -jnp.inf, g)
    nfar = qb - 1
    bfar = jnp.where(rowi < blk, bfar_ref[0, 0:1, 0:1], bfar_ref[1, 0:1, 0:1])
    bhi = bfar.astype(BF16).astype(F32)
    madd = jnp.where(lane < nfar, jnp.where(chosen, bhi, NEG),
                     jnp.where(lane == nfar, jnp.where(chosen, 0.0, NEG),
                               jnp.where((lane >= MOBA_LO) & (lane - MOBA_LO < nfar), bfar - bhi, 0.0)))
    q_aug = jnp.concatenate([(qh * scale).astype(BF16), madd.astype(BF16)], axis=1)

    prev0 = pl.multiple_of(jnp.maximum(nfar, 0) * blk, blk)
    own0 = pl.multiple_of(qb * blk, blk)
    s_prev = (lax.dot_general(q_aug, kb_ref[pl.ds(prev0, blk), :], nt, preferred_element_type=F32)
              + bprev_ref[...].reshape(rows2, blk) + jnp.where(qb > 0, 0.0, NEG))
    s_own = (lax.dot_general(q_aug, kb_ref[pl.ds(own0, blk), :], nt, preferred_element_type=F32)
             + bown_ref[...].reshape(rows2, blk))
    r = lax.broadcasted_iota(jnp.int32, (rows2, blk), 0)
    c = lax.broadcasted_iota(jnp.int32, (rows2, blk), 1)
    s_own = jnp.where(lax.bitwise_and(r, blk - 1) >= c, s_own, NEG)
    s = jnp.concatenate([s_prev, s_own], axis=1)
    m_i = jnp.max(s, axis=1, keepdims=True)
    p = jnp.exp(s - m_i)
    l_i = jnp.sum(p, axis=1, keepdims=True)
    v0 = jnp.concatenate([vb_ref[pl.ds(prev0, blk), :], vb_ref[pl.ds(own0, blk), :]], axis=0)
    acc = jnp.dot(p.astype(BF16), v0, preferred_element_type=F32)

    def body(it, carry):
        m_i, l_i, acc = carry
        k0 = pl.multiple_of(it * rows2, rows2)
        s = lax.dot_general(q_aug, kb_ref[pl.ds(k0, rows2), :], nt, preferred_element_type=F32)
        tail = jnp.where(2 * it + 1 < nfar, 0.0, NEG)
        s = jnp.concatenate([s[:, :blk], s[:, blk:] + tail], axis=1)
        m_new = jnp.maximum(m_i, jnp.max(s, axis=1, keepdims=True))
        alpha = jnp.exp(m_i - m_new)
        p = jnp.exp(s - m_new)
        l_new = alpha * l_i + jnp.sum(p, axis=1, keepdims=True)
        acc_new = alpha * acc + jnp.dot(p.astype(BF16), vb_ref[pl.ds(k0, rows2), :], preferred_element_type=F32)
        return m_new, l_new, acc_new

    m_i, l_i, acc = lax.fori_loop(0, (jnp.maximum(nfar, 0) + 1) // 2, body, (m_i, l_i, acc))
    out = acc / l_i
    o_ref[0] = jnp.where(first, out[:blk], out[blk:]).astype(o_ref.dtype)


def moba_attention(p3d, rel_bias):
    bsz, seq, _ = p3d.shape
    blk = MOBA_BLOCK
    n_blocks = seq // blk
    assert n_blocks <= MOBA_LO and seq % blk == 0
    span = 2 * blk
    by_dist = rel_bias[:, _rel_bucket(jnp.arange(span))].astype(F32)
    shift = jnp.arange(span)

    def toeplitz(c):
        k = jnp.where(shift < blk, shift, shift - span)
        s = by_dist[:, jnp.clip(c - k, 0, span - 1)]
        tiled = jnp.tile(s, (1, blk))[:, :blk * (span - 1)]
        return tiled.reshape(HEADS, blk, span - 1)[:, :, :blk]

    bias_own = toeplitz(0)
    bias_prev = toeplitz(blk)
    bias_far = jnp.broadcast_to(rel_bias[:, REL_BUCKETS - 1].astype(F32)[:, None, None], (HEADS, 8, LANES))
    kern = functools.partial(_moba_kernel, n_blocks=n_blocks)
    return pl.pallas_call(
        kern,
        grid=(bsz, PAIRS, n_blocks),
        in_specs=[
            pl.BlockSpec((1, blk, LANES), lambda b, h, i: (b, i, h)),
            pl.BlockSpec((1, seq, LANES), lambda b, h, i: (b, 0, PAIRS + h)),
            pl.BlockSpec((1, seq, LANES), lambda b, h, i: (b, 0, 2 * PAIRS + h)),
            pl.BlockSpec((2, blk, blk), lambda b, h, i: (h, 0, 0)),
            pl.BlockSpec((2, blk, blk), lambda b, h, i: (h, 0, 0)),
            pl.BlockSpec((2, 8, LANES), lambda b, h, i: (h, 0, 0)),
        ],
        out_specs=pl.BlockSpec((1, blk, LANES), lambda b, h, i: (b, i, h)),
        out_shape=jax.ShapeDtypeStruct((bsz, seq, WIDTH), BF16),
        scratch_shapes=[
            pltpu.VMEM((seq, 2 * LANES), BF16),
            pltpu.VMEM((seq, LANES), BF16),
            pltpu.VMEM((LANES, LANES), F32),
        ],
        compiler_params=_cparams(("parallel", "parallel", "arbitrary")),
        name="moba",
    )(p3d, p3d, p3d, bias_own, bias_prev, bias_far)


def _shifted(x, carry_row):
    rows = lax.broadcasted_iota(jnp.int32, x.shape, 0)
    return jnp.where(rows == 0, carry_row, pltpu.roll(x, 1, axis=0))


def _rwkv_prep_kernel(pr_ref, pk_ref, pv_ref, pl_ref, mu_ref, vec_ref, ww_ref, wa_ref, wg_ref,
                      bd_ref, tri_ref,
                      rt_ref, kt_ref, kd_ref, bd_out_ref, v_ref, g_ref, bonus_ref, pend_ref,
                      carry_ref, *, chunk):
    @pl.when(pl.program_id(1) == 0)
    def _():
        carry_ref[...] = jnp.zeros_like(carry_ref)

    def mix(ref, j):
        x = ref[0]
        mu = mu_ref[0:1, j * WIDTH:(j + 1) * WIDTH]
        prev = _shifted(x, carry_ref[0:1, j * WIDTH:(j + 1) * WIDTH])
        carry_ref[0:1, j * WIDTH:(j + 1) * WIDTH] = x[x.shape[0] - 1:, :]
        return x + mu * (prev - x)

    r = mix(pr_ref, 0)
    k = mix(pk_ref, 1)
    v = mix(pv_ref, 2)
    lo = mix(pl_ref, 3)
    w0, a0, k_k, k_a, r_k = (vec_ref[i:i + 1, :] for i in range(5))
    xwa = lo[:, 0:LANES]
    xg = lo[:, LANES:3 * LANES]
    lw = jnp.dot(jnp.tanh(xwa), ww_ref[...], precision=HI, preferred_element_type=F32)
    la = jnp.dot(xwa, wa_ref[...], precision=HI, preferred_element_type=F32)
    g = jnp.dot(jax.nn.sigmoid(xg), wg_ref[...], precision=HI, preferred_element_type=F32)
    z = -(w0 + lw)
    softplus = jnp.maximum(z, 0.0) + jnp.log(1.0 + jnp.exp(-jnp.abs(z)))
    logw = -jnp.exp(-softplus - 0.5)
    a = jax.nn.sigmoid(a0 + la)
    kk = k * k_k
    ss = jnp.dot(kk * kk, bd_ref[...], precision=HI, preferred_element_type=F32)
    kk = kk / jnp.maximum(jnp.sqrt(ss), 1e-12)
    k2 = k * (1.0 + (a - 1.0) * k_a)
    rk = jnp.dot(r * k2 * r_k, bd_ref[...], precision=HI, preferred_element_type=F32)
    cs = jnp.dot(tri_ref[...], logw, precision=HI, preferred_element_type=F32)
    e_pos = jnp.exp(cs)
    e_neg = jnp.exp(-cs)
    rt_ref[0] = (r * e_pos).astype(rt_ref.dtype)
    kt_ref[0] = (kk * jnp.exp(cs - logw)).astype(kt_ref.dtype)
    kd_ref[0] = (k2 * e_neg).astype(kd_ref.dtype)
    bd_out_ref[0] = (kk * a * e_neg).astype(bd_out_ref.dtype)
    v_ref[0] = v.astype(v_ref.dtype)
    g_ref[0] = g
    bonus_ref[0] = rk * v
    ts = e_pos.shape[0]
    for c in range(ts // chunk):
        pend_ref[0, c:c + 1, :] = e_pos[(c + 1) * chunk - 1:(c + 1) * chunk, :]


def rwkv_prep(p3d, rwkv_mu, w0, w_lora_up, a0, a_lora_up, g_lora_up, k_k, k_a, r_k, *, ts=512):
    bsz, seq, _ = p3d.shape
    chunk = RWKV_CHUNK
    ts = min(ts, seq)
    mu = jnp.pad(rwkv_mu, (0, COL_B - COL_B_RAW)).reshape(1, COL_B)
    vec = jnp.stack([w0, a0, k_k, k_a, r_k.reshape(-1)] + [jnp.zeros_like(w0)] * 3).astype(F32)
    ww = jnp.zeros((LANES, WIDTH), F32).at[:DECAY_LORA].set(w_lora_up)
    wa = jnp.zeros((LANES, WIDTH), F32).at[DECAY_LORA:DECAY_LORA + AAA_LORA].set(a_lora_up)
    wg = jnp.zeros((2 * LANES, WIDTH), F32).at[:GATE_LORA].set(g_lora_up)
    hid = jnp.arange(WIDTH) // HEAD_DIM
    bd = (hid[:, None] == hid[None, :]).astype(F32)
    tix = jnp.arange(ts)
    tri = ((tix[:, None] // chunk == tix[None, :] // chunk) & (tix[None, :] <= tix[:, None])).astype(F32)
    c0 = COL_A // WIDTH
    big = jax.ShapeDtypeStruct((bsz, seq, WIDTH), F32)
    wspec = lambda shape: pl.BlockSpec(shape, lambda b, i: (0, 0))
    ospec = pl.BlockSpec((1, ts, WIDTH), lambda b, i: (b, i, 0))
    return pl.pallas_call(
        functools.partial(_rwkv_prep_kernel, chunk=chunk),
        grid=(bsz, seq // ts),
        in_specs=[
            pl.BlockSpec((1, ts, WIDTH), lambda b, i: (b, i, c0)),
            pl.BlockSpec((1, ts, WIDTH), lambda b, i: (b, i, c0 + 1)),
            pl.BlockSpec((1, ts, WIDTH), lambda b, i: (b, i, c0 + 2)),
            pl.BlockSpec((1, ts, WIDTH), lambda b, i: (b, i, c0 + 3)),
            wspec((1, COL_B)), wspec((8, WIDTH)), wspec((LANES, WIDTH)), wspec((LANES, WIDTH)),
            wspec((2 * LANES, WIDTH)), wspec((WIDTH, WIDTH)), wspec((ts, ts)),
        ],
        out_specs=[ospec] * 7 + [pl.BlockSpec((1, ts // chunk, WIDTH), lambda b, i: (b, i, 0))],
        out_shape=[jax.ShapeDtypeStruct((bsz, seq, WIDTH), BF16)] * 5 + [big] * 2
        + [jax.ShapeDtypeStruct((bsz, seq // chunk, WIDTH), F32)],
        scratch_shapes=[pltpu.VMEM((8, COL_B), F32)],
        compiler_params=_cparams(("parallel", "arbitrary")),
        name="rwkv_prep",
    )(p3d, p3d, p3d, p3d, mu, vec, ww, wa, wg, bd, tri)


def _rwkv_scan_kernel(rt_ref, kt_ref, kd_ref, bd_ref, v_ref, g_ref, bonus_ref, pend_ref, ln_ref, o_ref,
                      state_ref, *, chunk, prec):
    @pl.when(pl.program_id(1) == 0)
    def _():
        state_ref[...] = jnp.zeros_like(state_ref)

    c2 = 2 * chunk
    lane = lax.broadcasted_iota(jnp.int32, (chunk, LANES), 1)
    first = lane < HEAD_DIM
    row = lax.broadcasted_iota(jnp.int32, (c2, c2), 0)
    col = lax.broadcasted_iota(jnp.int32, (c2, c2), 1)
    eye = (row == col).astype(F32)
    hrow = lax.broadcasted_iota(jnp.int32, (LANES, LANES), 0) // HEAD_DIM
    hcol = lax.broadcasted_iota(jnp.int32, (LANES, LANES), 1) // HEAD_DIM
    head_mean = jnp.where(hrow == hcol, 1.0 / HEAD_DIM, 0.0).astype(F32)
    nt = (((1,), (1,)), ((), ()))
    tn = (((0,), (0,)), ((), ()))
    dot = functools.partial(jnp.dot, precision=prec, preferred_element_type=F32)
    dotg = functools.partial(lax.dot_general, precision=prec, preferred_element_type=F32)

    def stack(x):
        return jnp.concatenate([jnp.where(first, x, 0.0), jnp.where(first, 0.0, x)], axis=0)

    pairs = range(PAIRS)
    sls = [slice(hp * LANES, (hp + 1) * LANES) for hp in pairs]
    rs, ks, kds, bs, vs = ([stack(ref[0, :, sl].astype(F32)) for sl in sls]
                           for ref in (rt_ref, kt_ref, kd_ref, bd_ref, v_ref))
    hts = [state_ref[0, hp] for hp in pairs]
    big = [dotg(jnp.concatenate([ks[hp], rs[hp]], axis=0), jnp.concatenate([bs[hp], kds[hp]], axis=0), nt)
           for hp in pairs]
    a_b = [jnp.where(row > col, big[hp][0:c2, 0:c2], 0.0) for hp in pairs]
    a_k = [jnp.where(row > col, big[hp][0:c2, c2:], 0.0) for hp in pairs]
    a_rb = [jnp.where(row >= col, big[hp][c2:, 0:c2], 0.0) for hp in pairs]
    a_rk = [jnp.where(row >= col, big[hp][c2:, c2:], 0.0) for hp in pairs]
    kh = [dotg(jnp.concatenate([ks[hp], rs[hp]], axis=0), hts[hp], nt) for hp in pairs]
    av = [dot(jnp.concatenate([a_k[hp], a_rk[hp]], axis=0), vs[hp]) for hp in pairs]
    vk = [dotg(vs[hp], kds[hp], tn) for hp in pairs]
    inv = [eye - a_b[hp] for hp in pairs]
    pw = [dot(a_b[hp], a_b[hp]) for hp in pairs]
    n_sq = int(math.log2(chunk)) - 1
    for lvl in range(n_sq):
        if lvl + 1 < n_sq:
            both = [dot(jnp.concatenate([inv[hp], pw[hp]], axis=0), pw[hp]) for hp in pairs]
            inv = [inv[hp] + both[hp][0:c2] for hp in pairs]
            pw = [both[hp][c2:] for hp in pairs]
        else:
            inv = [inv[hp] + dot(inv[hp], pw[hp]) for hp in pairs]
    us = [dot(inv[hp], kh[hp][0:c2] + av[hp][0:c2]) for hp in pairs]
    ub = [dotg(us[hp], bs[hp], tn) for hp in pairs]
    au = [dot(a_rb[hp], us[hp]) for hp in pairs]
    for hp in pairs:
        sl = sls[hp]
        pend = pend_ref[0, 0, 0:1, sl]
        state_ref[0, hp] = (hts[hp] + vk[hp] - ub[hp]) * pend
        os_ = kh[hp][c2:] + av[hp][c2:] - au[hp]
        o = os_[0:chunk] + os_[chunk:]
        mu = jnp.dot(o, head_mean, precision=HI, preferred_element_type=F32)
        d = o - mu
        var = jnp.dot(d * d, head_mean, precision=HI, preferred_element_type=F32)
        on = d * lax.rsqrt(var + GN_EPS) * ln_ref[0:1, sl] + ln_ref[1:2, sl]
        o_ref[0, :, sl] = ((on + bonus_ref[0, :, sl]) * g_ref[0, :, sl]).astype(o_ref.dtype)


def rwkv_scan(rt, kt, kd, bd, v, g, bonus, pend, lnx_g, lnx_b, *, prec=None):
    bsz, seq, _ = rt.shape
    chunk = RWKV_CHUNK
    n_chunks = seq // chunk
    ln = jnp.stack([lnx_g, lnx_b] + [jnp.zeros_like(lnx_g)] * 6).astype(F32)
    pend4 = pend.reshape(bsz, n_chunks, 1, WIDTH)
    spec = pl.BlockSpec((1, chunk, WIDTH), lambda b, c: (b, c, 0))
    return pl.pallas_call(
        functools.partial(_rwkv_scan_kernel, chunk=chunk, prec=prec),
        grid=(bsz, n_chunks),
        in_specs=[spec] * 7 + [
            pl.BlockSpec((1, 1, 1, WIDTH), lambda b, c: (b, c, 0, 0)),
            pl.BlockSpec((8, WIDTH), lambda b, c: (0, 0)),
        ],
        out_specs=spec,
        out_shape=jax.ShapeDtypeStruct((bsz, seq, WIDTH), BF16),
        scratch_shapes=[pltpu.VMEM((1, PAIRS, LANES, LANES), F32)],
        compiler_params=_cparams(("parallel", "arbitrary")),
        name="rwkv_scan",
    )(rt, kt, kd, bd, v, g, bonus, pend4, ln)


def _merge_kernel(x_ref, oa_ref, ob_ref, ga_ref, gb_ref, wa_ref, wb_ref, wo_ref, g2_ref,
                  h_ref, xn_ref, acc_ref):
    j = pl.program_id(1)

    @pl.when(j == 0)
    def _():
        acc_ref[...] = x_ref[...]

    ya = jnp.dot(oa_ref[...].astype(BF16), wa_ref[...], preferred_element_type=F32)
    yb = jnp.dot(ob_ref[...].astype(BF16), wb_ref[...], preferred_element_type=F32)
    y = jax.nn.sigmoid(ga_ref[...]) * ya + jax.nn.sigmoid(gb_ref[...]) * yb
    acc_ref[...] += jnp.dot(y.astype(BF16), wo_ref[...], preferred_element_type=F32)

    @pl.when(j == pl.num_programs(1) - 1)
    def _():
        h = acc_ref[...]
        h_ref[...] = h
        ms = jnp.mean(h * h, axis=-1, keepdims=True)
        xn_ref[...] = _pack_halves(h * lax.rsqrt(ms + RMS_EPS) * g2_ref[...])


def _pack_halves(x):
    half = x.shape[1] // 2
    lo = lax.bitcast_convert_type(x[:, :half].astype(BF16).astype(F32), jnp.int32)
    hi = lax.bitcast_convert_type(x[:, half:].astype(BF16).astype(F32), jnp.int32)
    return lax.bitwise_or(lax.shift_right_logical(lo, jnp.int32(16)), hi)


def _unpack_halves(words):
    lo, hi = _unpack_words(words)
    return jnp.concatenate([lo, hi], axis=1)


def merge_out(x2d, oa, ob, p2d, w_proj_a, w_proj_b, w_out, norm2_g, *, row0=0, tm=512):
    t, d = oa.shape[0], x2d.shape[1]
    r0 = row0 // tm
    tn = WIDTH
    nj = d // tn
    g0 = COL_G_OFF // tn
    return pl.pallas_call(
        _merge_kernel,
        grid=(t // tm, nj),
        in_specs=[
            pl.BlockSpec((tm, d), lambda i, j: (r0 + i, 0)),
            pl.BlockSpec((tm, WIDTH), lambda i, j: (i, 0)),
            pl.BlockSpec((tm, WIDTH), lambda i, j: (i, 0)),
            pl.BlockSpec((tm, tn), lambda i, j: (i, g0 + j)),
            pl.BlockSpec((tm, tn), lambda i, j: (i, g0 + nj + j)),
            pl.BlockSpec((WIDTH, tn), lambda i, j: (0, j)),
            pl.BlockSpec((WIDTH, tn), lambda i, j: (0, j)),
            pl.BlockSpec((tn, d), lambda i, j: (j, 0)),
            pl.BlockSpec((1, d), lambda i, j: (0, 0)),
        ],
        out_specs=[pl.BlockSpec((tm, d), lambda i, j: (i, 0)), pl.BlockSpec((tm, d // 2), lambda i, j: (i, 0))],
        out_shape=[jax.ShapeDtypeStruct((t, d), F32), jax.ShapeDtypeStruct((t, d // 2), jnp.int32)],
        scratch_shapes=[pltpu.VMEM((tm, d), F32)],
        compiler_params=_cparams(("parallel", "arbitrary")),
        name="merge_out",
    )(x2d, oa, ob, p2d, p2d, w_proj_a.astype(BF16), w_proj_b.astype(BF16), w_out.astype(BF16),
      norm2_g.reshape(1, d))


PEER_HEADS = 8
PEER_NKEYS = 128
PEER_TOPK = 16
PEER_HALF = 128


def _topk_rows(s, k):
    n = s.shape[0]
    rows = lax.broadcasted_iota(jnp.int32, s.shape, 0).astype(F32)
    vals, ids = [], []
    for _ in range(k):
        m = jnp.max(s, axis=0, keepdims=True)
        first = jnp.min(jnp.where(s == m, rows, float(n)), axis=0, keepdims=True)
        vals.append(m)
        ids.append(first)
        s = jnp.where(rows == first, -jnp.inf, s)
    return jnp.concatenate(vals, axis=0), jnp.concatenate(ids, axis=0)


def _take_rows(table, ids):
    rows = lax.broadcasted_iota(jnp.int32, table.shape, 0).astype(F32)
    return jnp.sum(jnp.where(rows == ids, table, 0.0), axis=0, keepdims=True)


def _peer_route_kernel(xn_ref, wq_ref, sk_ref, idx_ref, gate_ref, *, prec):
    tt = xn_ref.shape[0]
    k = PEER_TOPK
    xn = _unpack_halves(xn_ref[...]) if xn_ref.dtype == jnp.int32 else xn_ref[...]
    q = jnp.dot(xn.astype(wq_ref.dtype), wq_ref[...], precision=prec, preferred_element_type=F32)
    nt = (((1,), (1,)), ((), ()))
    idx_rows, gate_rows = [], []
    half = k // 2
    for h in range(PEER_HEADS):
        tops = []
        for p in range(2):
            c0 = (h * 2 + p) * PEER_HALF
            s = lax.dot_general(sk_ref[h, p].astype(wq_ref.dtype), q[:, c0:c0 + PEER_HALF].astype(wq_ref.dtype),
                                nt, precision=prec, preferred_element_type=F32)
            tops.append(_topk_rows(s, k))
        (s0, i0), (s1, i1) = tops
        cs = [s0[0:1] + s1] + [s0[i:i + 1] + s1[0:half] for i in range(1, half)] + [s0[half:] + s1[0:1]]
        best_s, pos = _topk_rows(jnp.concatenate(cs, axis=0), k)
        mid = jnp.floor((pos - k) * (1.0 / half))
        end_mid = float(k + (half - 1) * half)
        i_rank = jnp.where(pos < k, 0.0, jnp.where(pos < end_mid, 1.0 + mid, pos - (end_mid - half)))
        j_rank = jnp.where(pos < k, pos, jnp.where(pos < end_mid, (pos - k) - half * mid, 0.0))
        ids = [_take_rows(i0, i_rank[n:n + 1]) * PEER_NKEYS + _take_rows(i1, j_rank[n:n + 1]) for n in range(k)]
        e = jnp.exp(best_s - best_s[0:1])
        gate_rows.append(e / jnp.sum(e, axis=0, keepdims=True))
        idx_rows.append(jnp.concatenate(ids, axis=0).astype(jnp.int32))
    idx_ref[...] = jnp.concatenate(idx_rows, axis=0).T
    gate_ref[...] = jnp.concatenate(gate_rows, axis=0).T


def peer_route(xn2d, peer_wq, peer_subkeys, *, tt=256, prec=None, wdtype=BF16):
    t, dx = xn2d.shape
    d, nq = peer_wq.shape
    n_sel = PEER_HEADS * PEER_TOPK
    return pl.pallas_call(
        functools.partial(_peer_route_kernel, prec=prec),
        grid=(t // tt,),
        in_specs=[
            pl.BlockSpec((tt, dx), lambda i: (i, 0)),
            pl.BlockSpec((d, nq), lambda i: (0, 0)),
            pl.BlockSpec((PEER_HEADS, 2, PEER_NKEYS, PEER_HALF), lambda i: (0, 0, 0, 0)),
        ],
        out_specs=[pl.BlockSpec((tt, n_sel), lambda i: (i, 0))] * 2,
        out_shape=[jax.ShapeDtypeStruct((t, n_sel), jnp.int32), jax.ShapeDtypeStruct((t, n_sel), F32)],
        compiler_params=_cparams(("parallel",)),
        name="peer_route",
    )(xn2d, peer_wq.astype(wdtype), peer_subkeys)


def _final_kernel(h_ref, y_ref, g_ref, *rest):
    o_ref = rest[-1]
    h = h_ref[...] + y_ref[...]
    ms = jnp.mean(h * h, axis=-1, keepdims=True)
    o_ref[...] = h * lax.rsqrt(ms + RMS_EPS) * g_ref[...]


def final_norm(h2d, y2d, g, *, out=None, row0=0, total_rows=None, tm=1024):
    t, d = h2d.shape
    total = t if total_rows is None else total_rows
    r0 = row0 // tm
    spec = pl.BlockSpec((tm, d), lambda i: (i, 0))
    in_specs = [spec, spec, pl.BlockSpec((1, d), lambda i: (0, 0))]
    args = [h2d, y2d, g.reshape(1, d)]
    aliases = {}
    if out is not None:
        in_specs.append(pl.BlockSpec(memory_space=pl.ANY))
        args.append(out)
        aliases = {3: 0}
    return pl.pallas_call(
        _final_kernel,
        grid=(t // tm,),
        in_specs=in_specs,
        out_specs=pl.BlockSpec((tm, d), lambda i: (r0 + i, 0)),
        out_shape=jax.ShapeDtypeStruct((total, d), F32),
        input_output_aliases=aliases,
        compiler_params=_cparams(("parallel",)),
        name="final_norm",
    )(*args)


SC_CORES = 2
SC_SUBCORES = 16
SC_LANES = 16
SC_WORKERS = SC_CORES * SC_SUBCORES
PEER_SEL = PEER_HEADS * PEER_TOPK
PEER_ROWS = 32
PEER_PARTS = PEER_SEL // PEER_ROWS
PEER_NBUF = 4
PEER_GROUP = 32
PEER_BF16_RUN = 4


def _pack_rows_kernel(w_ref, o_ref):
    o_ref[...] = _pack_halves(w_ref[...])


def _pack_rows(w, *, tr=1024):
    e, d = w.shape
    return pl.pallas_call(
        _pack_rows_kernel,
        grid=(e // tr,),
        in_specs=[pl.BlockSpec((tr, d), lambda i: (i, 0))],
        out_specs=pl.BlockSpec((tr, d // 2), lambda i: (i, 0)),
        out_shape=jax.ShapeDtypeStruct((e, d // 2), jnp.int32),
        compiler_params=_cparams(("parallel",)),
        name="pack_rows",
    )(w)


def _unpack_words(w):
    lo = lax.bitcast_convert_type(lax.shift_left(w, jnp.int32(16)), F32)
    hi = lax.bitcast_convert_type(lax.bitwise_and(w, jnp.int32(-65536)), F32)
    return lo, hi


def _packed_dot(a_words, b_words):
    from jax.experimental.pallas import tpu_sc as plsc
    prods = [plsc.bitcast(a, BF16) * plsc.bitcast(b, BF16) for a, b in zip(a_words, b_words)]
    while len(prods) > 1:
        prods = [prods[k] + prods[k + 1] for k in range(0, len(prods), 2)]
    return _unpack_words(plsc.bitcast(prods[0], jnp.int32))


def _sc_mesh():
    from jax.experimental.pallas import tpu_sc as plsc
    return plsc.VectorSubcoreMesh(core_axis_name="c", subcore_axis_name="s",
                                  num_cores=SC_CORES, num_subcores=SC_SUBCORES)


def _sc_loop(n, body, carry):
    from jax.experimental.pallas import tpu_sc as plsc
    return plsc.parallel_loop(0, n, carry=carry)(body)


def _worker_base(tokens_per_worker):
    return (lax.axis_index("s") * SC_CORES + lax.axis_index("c")) * tokens_per_worker


def _gather_compute_loop(table_hbm, idx_v, rows_v, sem, stage_v, out_row, osem, grp, compute):
    n_gathers = PEER_PARTS * grp
    ahead = PEER_NBUF - 1

    def gather(j, b):
        i = j // PEER_PARTS if isinstance(j, int) else lax.shift_right_logical(j, PEER_PARTS.bit_length() - 1)
        h = j % PEER_PARTS if isinstance(j, int) else lax.bitwise_and(j, PEER_PARTS - 1)
        ids = idx_v.at[i, pl.ds(pl.multiple_of(h * PEER_ROWS, PEER_ROWS), PEER_ROWS)]
        return pltpu.make_async_copy(table_hbm.at[ids], rows_v.at[b], sem.at[b])

    def put(i, slot):
        return pltpu.make_async_copy(stage_v.at[slot], out_row(i), osem.at[slot])

    for j in range(ahead):
        gather(j, j).start()

    @pl.loop(0, n_gathers)
    def _(j):
        b = lax.bitwise_and(j, PEER_NBUF - 1)
        h = lax.bitwise_and(j, PEER_PARTS - 1)
        i = lax.shift_right_logical(j, PEER_PARTS.bit_length() - 1)
        slot = lax.bitwise_and(i, 1)

        @pl.when((h == 0) & (i >= 2))
        def _():
            put(i - 2, slot).wait()

        @pl.when(j + ahead < n_gathers)
        def _():
            gather(j + ahead, lax.bitwise_and(j + ahead, PEER_NBUF - 1)).start()

        gather(j, b).wait()
        compute(i, h, b, slot)

        @pl.when(h == PEER_PARTS - 1)
        def _():
            put(i, slot).start()

    put(grp - 2, 0).wait()
    put(grp - 1, 1).wait()


def peer_expert_dots(x_packed, idx, u_packed):
    t, half = x_packed.shape
    n_chunks = half // SC_LANES
    tpw = t // SC_WORKERS
    grp = min(PEER_GROUP, tpw)
    rows_tog = 8

    def body(x_hbm, idx_hbm, u_hbm, out_hbm, idx_v, x_v, rows_v, ps_v, sem, osem):
        base = _worker_base(tpw)

        def compute(i, h, b, slot):
            @pl.loop(0, PEER_ROWS // rows_tog)
            def _(rg):
                r0 = rg * rows_tog
                accs = [[None, None] for _ in range(rows_tog)]
                for c0 in range(0, n_chunks, PEER_BF16_RUN):
                    ats = [pl.ds((c0 + k) * SC_LANES, SC_LANES) for k in range(PEER_BF16_RUN)]
                    xw = [x_v[i, at] for at in ats]
                    for r in range(rows_tog):
                        terms = _packed_dot([rows_v[b, r0 + r, at] for at in ats], xw)
                        for k, term in enumerate(terms):
                            accs[r][k] = term if accs[r][k] is None else accs[r][k] + term
                for r in range(rows_tog):
                    at = pl.ds(pl.multiple_of((h * PEER_ROWS + r0 + r) * SC_LANES, SC_LANES), SC_LANES)
                    ps_v[slot, at] = accs[r][0] + accs[r][1]

        @pl.loop(0, tpw // grp)
        def _(g):
            t0 = base + g * grp
            pltpu.sync_copy(idx_hbm.at[pl.ds(t0, grp)], idx_v)
            pltpu.sync_copy(x_hbm.at[pl.ds(t0, grp)], x_v)
            _gather_compute_loop(u_hbm, idx_v, rows_v, sem, ps_v, lambda i: out_hbm.at[t0 + i], osem, grp, compute)

    return pl.kernel(
        body,
        out_type=jax.ShapeDtypeStruct((t, PEER_SEL * SC_LANES), F32),
        mesh=_sc_mesh(),
        scratch_types=[
            pltpu.VMEM((grp, PEER_SEL), jnp.int32),
            pltpu.VMEM((grp, half), jnp.int32),
            pltpu.VMEM((PEER_NBUF, PEER_ROWS, half), jnp.int32),
            pltpu.VMEM((2, PEER_SEL * SC_LANES), F32),
            pltpu.SemaphoreType.DMA((PEER_NBUF,)),
            pltpu.SemaphoreType.DMA((2,)),
        ],
        compiler_params=pltpu.CompilerParams(needs_layout_passes=False),
        name="peer_expert_dots",
    )(x_packed, idx, u_packed)


def peer_expert_mix(hgw, idx, v_packed):
    t = hgw.shape[0]
    half = v_packed.shape[1]
    d = 2 * half
    tpw = t // SC_WORKERS
    grp = min(PEER_GROUP, tpw)
    n_parts = 2
    cpp = half // SC_LANES // n_parts
    from jax.experimental.pallas import tpu_sc as plsc

    def body(hg_hbm, idx_hbm, v_hbm, out_hbm, idx_v, hg_v, rows_v, o_v2, sem, osem):
        base = _worker_base(tpw)

        def compute(i, h, b, slot):
            token = jnp.full((SC_LANES,), i, jnp.int32)
            for part in range(n_parts):
                def rbody(rq, accs):
                    r0 = rq * PEER_BF16_RUN
                    s = [plsc.load_gather(hg_v, [token, jnp.full((SC_LANES,), h * PEER_ROWS + r0 + k, jnp.int32)])
                         for k in range(PEER_BF16_RUN)]
                    new = []
                    for c in range(cpp):
                        at = pl.ds((part * cpp + c) * SC_LANES, SC_LANES)
                        lo, hi = _packed_dot([rows_v[b, r0 + k, at] for k in range(PEER_BF16_RUN)], s)
                        new.append(accs[2 * c] + lo)
                        new.append(accs[2 * c + 1] + hi)
                    return tuple(new)

                accs = _sc_loop(PEER_ROWS // PEER_BF16_RUN, rbody,
                                tuple(jnp.zeros((SC_LANES,), F32) for _ in range(2 * cpp)))
                def store(overwrite):
                    for c in range(cpp):
                        lo_at = pl.ds((part * cpp + c) * SC_LANES, SC_LANES)
                        hi_at = pl.ds(half + (part * cpp + c) * SC_LANES, SC_LANES)
                        if overwrite:
                            o_v2[slot, lo_at] = accs[2 * c]
                            o_v2[slot, hi_at] = accs[2 * c + 1]
                        else:
                            o_v2[slot, lo_at] = o_v2[slot, lo_at] + accs[2 * c]
                            o_v2[slot, hi_at] = o_v2[slot, hi_at] + accs[2 * c + 1]

                pl.when(h == 0)(functools.partial(store, True))
                pl.when(h != 0)(functools.partial(store, False))

        @pl.loop(0, tpw // grp)
        def _(g):
            t0 = base + g * grp
            pltpu.sync_copy(idx_hbm.at[pl.ds(t0, grp)], idx_v)
            pltpu.sync_copy(hg_hbm.at[pl.ds(t0, grp)], hg_v)
            _gather_compute_loop(v_hbm, idx_v, rows_v, sem, o_v2, lambda i: out_hbm.at[t0 + i], osem, grp, compute)

    return pl.kernel(
        body,
        out_type=jax.ShapeDtypeStruct((t, d), F32),
        mesh=_sc_mesh(),
        scratch_types=[
            pltpu.VMEM((grp, PEER_SEL), jnp.int32),
            pltpu.VMEM((grp, PEER_SEL), jnp.int32),
            pltpu.VMEM((PEER_NBUF, PEER_ROWS, half), jnp.int32),
            pltpu.VMEM((2, d), F32),
            pltpu.SemaphoreType.DMA((PEER_NBUF,)),
            pltpu.SemaphoreType.DMA((2,)),
        ],
        compiler_params=pltpu.CompilerParams(needs_layout_passes=False),
        name="peer_expert_mix",
    )(hgw, idx, v_packed)


def _peer_act_kernel(ps_ref, gate_ref, sum_ref, o_ref):
    ps = ps_ref[...]
    sel = sum_ref[...]
    hi = ps.astype(BF16)
    rest = ps - hi.astype(F32)
    mid = rest.astype(BF16)
    lo = (rest - mid.astype(F32)).astype(BF16)
    pre = (jnp.dot(hi, sel, preferred_element_type=F32) + jnp.dot(mid, sel, preferred_element_type=F32)
           + jnp.dot(lo, sel, preferred_element_type=F32))
    hg = 0.5 * pre * (1.0 + lax.erf(pre * (1.0 / math.sqrt(2.0)))) * gate_ref[...]
    bits = lax.bitcast_convert_type(hg.astype(BF16).astype(F32), jnp.int32)
    o_ref[...] = lax.bitwise_or(bits, lax.shift_right_logical(bits, jnp.int32(16)))


def peer_act(ps, gates, *, tm=512):
    t, n = ps.shape
    lane_sum = (jnp.arange(n)[:, None] // SC_LANES == jnp.arange(PEER_SEL)[None, :]).astype(BF16)
    return pl.pallas_call(
        _peer_act_kernel,
        grid=(t // tm,),
        in_specs=[
            pl.BlockSpec((tm, n), lambda i: (i, 0)),
            pl.BlockSpec((tm, PEER_SEL), lambda i: (i, 0)),
            pl.BlockSpec((n, PEER_SEL), lambda i: (0, 0)),
        ],
        out_specs=pl.BlockSpec((tm, PEER_SEL), lambda i: (i, 0)),
        out_shape=jax.ShapeDtypeStruct((t, PEER_SEL), jnp.int32),
        compiler_params=_cparams(("parallel",)),
        name="peer_act",
    )(ps, gates, lane_sum)


BATCH_GROUPS = 8


def kernel(x, norm1_g, w_in, rwkv_mu, w0, w_lora_up, a0, a_lora_up, g_lora_up, k_k, k_a, r_k, lnx_g, lnx_b,
           w_proj_a, w_proj_b, w_out, norm2_g, peer_wq, peer_subkeys, peer_u, peer_v, rel_bias, normf_g):
    bsz, seq, d = x.shape
    depth = norm1_g.shape[0]
    groups = BATCH_GROUPS if bsz % BATCH_GROUPS == 0 else 1
    gb = bsz // groups
    tg = gb * seq
    t = bsz * seq
    src = x.reshape(t, d)
    for l in range(depth):
        w_pad = jnp.concatenate([
            w_in[l][:, :COL_A + COL_B_RAW],
            jnp.zeros((d, COL_B - COL_B_RAW), w_in.dtype),
            w_in[l][:, COL_A + COL_B_RAW:]], axis=1).astype(BF16)
        u_packed = _pack_rows(peer_u[l])
        v_packed = _pack_rows(peer_v[l])
        last = l == depth - 1

        def mix(pending, tie=None):
            row0, h2d, ps, gates, idx = pending
            hgw = peer_act(ps, gates)
            if tie is not None:
                tie, hgw = lax.optimization_barrier((tie, hgw))
            return tie, (row0, h2d, peer_expert_mix(hgw, idx, v_packed))

        outs = []

        def close(mixed):
            row0, h2d, y2d = mixed
            if last:
                outs.append(final_norm(h2d, y2d, normf_g, out=outs[-1] if outs else None, row0=row0, total_rows=t))
            else:
                outs.append(h2d + y2d)

        pending = closing = None
        for g in range(groups):
            p2d = norm_proj(src, norm1_g[l], w_pad, row0=g * tg, rows=tg)
            p3d = p2d.reshape(gb, seq, -1)
            oa = moba_attention(p3d, rel_bias)
            prep = tuple(rwkv_prep(p3d, rwkv_mu[l], w0[l], w_lora_up[l], a0[l], a_lora_up[l], g_lora_up[l],
                                   k_k[l], k_a[l], r_k[l]))
            mixed = None
            if pending is not None:
                (oa, prep), mixed = mix(pending, (oa, prep))
            if closing is not None:
                oa, y2d = lax.optimization_barrier((oa, closing[2]))
                close(closing[:2] + (y2d,))
                closing = None
            ob = rwkv_scan(*prep, lnx_g[l], lnx_b[l])
            h2d, xn2 = merge_out(src, oa.reshape(tg, WIDTH), ob.reshape(tg, WIDTH), p2d, w_proj_a[l], w_proj_b[l],
                                 w_out[l], norm2_g[l], row0=g * tg)
            idx, gates = peer_route(xn2, peer_wq[l], peer_subkeys[l])
            if mixed is not None:
                idx, y2d = lax.optimization_barrier((idx, mixed[2]))
                closing = mixed[:2] + (y2d,)
            pending = (g * tg, h2d, peer_expert_dots(xn2, idx, u_packed), gates, idx)
        if closing is not None:
            close(closing)
        close(mix(pending)[1])
        src = outs[-1] if last else jnp.concatenate(outs, axis=0)
    return src.reshape(bsz, seq, d)
```

```python
import functools
import math

import jax
import jax.numpy as jnp
from jax import lax
from jax.experimental import pallas as pl
from jax.experimental.pallas import tpu as pltpu

F32 = jnp.float32
BF16 = jnp.bfloat16
HI = lax.Precision.HIGHEST

LANES = 128
HEAD_DIM = 64
HEADS = 8
PAIRS = HEADS // 2
WIDTH = HEADS * HEAD_DIM
MOBA_BLOCK = 256
MOBA_TOPK = 3
MOBA_LO = 64
REL_BUCKETS = 32
REL_MAX_DIST = 128
DECAY_LORA = 64
AAA_LORA = 64
GATE_LORA = 160
GN_EPS = 64e-5
RMS_EPS = 1e-6
NEG = -1e30
RWKV_CHUNK = 64
COL_A = 3 * WIDTH
COL_B_RAW = 3 * WIDTH + DECAY_LORA + AAA_LORA + GATE_LORA
COL_B = 4 * WIDTH
COL_G_OFF = COL_A + COL_B
VMEM_LIMIT = 56 * 1024 * 1024


def _cparams(sem):
    return pltpu.CompilerParams(dimension_semantics=sem, vmem_limit_bytes=VMEM_LIMIT)


def _norm_proj_kernel(x_ref, g_ref, w_ref, pa_ref, pb_ref, pg_ref, xn_ref, *, ja, jb):
    j = pl.program_id(1)

    @pl.when(j == 0)
    def _():
        x = x_ref[...]
        ms = jnp.mean(x * x, axis=-1, keepdims=True)
        xn_ref[...] = (x * lax.rsqrt(ms + RMS_EPS) * g_ref[...]).astype(xn_ref.dtype)

    res = jnp.dot(xn_ref[...], w_ref[...], preferred_element_type=F32)

    @pl.when(j < ja)
    def _():
        pa_ref[...] = res.astype(pa_ref.dtype)

    @pl.when((j >= ja) & (j < jb))
    def _():
        pb_ref[...] = res

    @pl.when(j >= jb)
    def _():
        pg_ref[...] = res.astype(pg_ref.dtype)


def norm_proj(x2d, g, w, *, row0=0, rows=None, tm=1024, tn=512):
    d = x2d.shape[1]
    t = x2d.shape[0] if rows is None else rows
    n = w.shape[1]
    r0 = row0 // tm
    ja, jb, jn = COL_A // tn, COL_G_OFF // tn, n // tn
    return pl.pallas_call(
        functools.partial(_norm_proj_kernel, ja=ja, jb=jb),
        grid=(t // tm, jn),
        in_specs=[
            pl.BlockSpec((tm, d), lambda i, j: (r0 + i, 0)),
            pl.BlockSpec((1, d), lambda i, j: (0, 0)),
            pl.BlockSpec((d, tn), lambda i, j: (0, j)),
        ],
        out_specs=[
            pl.BlockSpec((tm, tn), lambda i, j: (i, jnp.minimum(j, ja - 1))),
            pl.BlockSpec((tm, tn), lambda i, j: (i, jnp.clip(j - ja, 0, jb - ja - 1))),
            pl.BlockSpec((tm, tn), lambda i, j: (i, jnp.maximum(j - jb, 0))),
        ],
        out_shape=[jax.ShapeDtypeStruct((t, COL_A), BF16), jax.ShapeDtypeStruct((t, COL_B), F32),
                   jax.ShapeDtypeStruct((t, n - COL_G_OFF), BF16)],
        scratch_shapes=[pltpu.VMEM((tm, d), w.dtype)],
        compiler_params=_cparams(("parallel", "arbitrary")),
        name="norm_proj",
    )(x2d, g.reshape(1, d), w)


def _rel_bucket(dist):
    n = jnp.maximum(dist, 0)
    max_exact = REL_BUCKETS // 2
    nf = jnp.maximum(n, 1).astype(F32)
    large = max_exact + (jnp.log(nf / max_exact) / math.log(REL_MAX_DIST / max_exact)
                         * (REL_BUCKETS - max_exact)).astype(jnp.int32)
    large = jnp.minimum(large, REL_BUCKETS - 1)
    return jnp.where(n < max_exact, n, large)


def _moba_kernel(q_ref, k_ref, v_ref, bown_ref, bprev_ref, bfar_ref, o_ref,
                 kb_ref, vb_ref, kbar_ref, *, n_blocks):
    qb = pl.program_id(2)
    blk = MOBA_BLOCK
    scale = 1.0 / math.sqrt(HEAD_DIM)

    rows2 = 2 * blk
    nt = (((1,), (1,)), ((), ()))

    @pl.when(qb == 0)
    def _():
        kbar_ref[...] = jnp.zeros_like(kbar_ref)
        lane_b = lax.broadcasted_iota(jnp.int32, (blk, LANES), 1)
        for n in range(n_blocks):
            kblk = k_ref[0, n * blk:(n + 1) * blk, :]
            kbar_ref[n:n + 1, :] = jnp.mean(kblk.astype(F32), axis=0, keepdims=True)
            kb_ref[n * blk:(n + 1) * blk, 0:LANES] = kblk.astype(BF16)
            kb_ref[n * blk:(n + 1) * blk, LANES:] = ((lane_b == n) | (lane_b == MOBA_LO + n)).astype(BF16)
        vb_ref[...] = v_ref[0].astype(BF16)

    q2 = q_ref[0].astype(F32)
    first = lax.broadcasted_iota(jnp.int32, (blk, LANES), 1) < HEAD_DIM
    qh = jnp.concatenate([jnp.where(first, q2, 0.0), jnp.where(first, 0.0, q2)], axis=0)
    lane = lax.broadcasted_iota(jnp.int32, (rows2, LANES), 1)
    rowi = lax.broadcasted_iota(jnp.int32, (rows2, LANES), 0)
    gate = lax.dot_general(qh.astype(BF16), kbar_ref[...].astype(BF16), nt, preferred_element_type=F32)
    g = jnp.where(lane < qb, gate, -jnp.inf)
    chosen = lane < 0
    lane_f = lane.astype(F32)
    for _ in range(MOBA_TOPK):
        m = jnp.max(g, axis=1, keepdims=True)
        idx = jnp.min(jnp.where(g == m, lane_f, float(LANES)), axis=1, keepdims=True)
        hit = (lane_f == idx) & (m > -jnp.inf)
        chosen = chosen | hit
        g = jnp.where(hit, -jnp.inf, g)
    nfar = qb - 1
    bfar = jnp.where(rowi < blk, bfar_ref[0, 0:1, 0:1], bfar_ref[1, 0:1, 0:1])
    bhi = bfar.astype(BF16).astype(F32)
    madd = jnp.where(lane < nfar, jnp.where(chosen, bhi, NEG),
                     jnp.where(lane == nfar, jnp.where(chosen, 0.0, NEG),
                               jnp.where((lane >= MOBA_LO) & (lane - MOBA_LO < nfar), bfar - bhi, 0.0)))
    q_aug = jnp.concatenate([(qh * scale).astype(BF16), madd.astype(BF16)], axis=1)

    prev0 = pl.multiple_of(jnp.maximum(nfar, 0) * blk, blk)
    own0 = pl.multiple_of(qb * blk, blk)
    s_prev = (lax.dot_general(q_aug, kb_ref[pl.ds(prev0, blk), :], nt, preferred_element_type=F32)
              + bprev_ref[...].reshape(rows2, blk) + jnp.where(qb > 0, 0.0, NEG))
    s_own = (lax.dot_general(q_aug, kb_ref[pl.ds(own0, blk), :], nt, preferred_element_type=F32)
             + bown_ref[...].reshape(rows2, blk))
    r = lax.broadcasted_iota(jnp.int32, (rows2, blk), 0)
    c = lax.broadcasted_iota(jnp.int32, (rows2, blk), 1)
    s_own = jnp.where(lax.bitwise_and(r, blk - 1) >= c, s_own, NEG)
    s = jnp.concatenate([s_prev, s_own], axis=1)
    m_i = jnp.max(s, axis=1, keepdims=True)
    p = jnp.exp(s - m_i)
    l_i = jnp.sum(p, axis=1, keepdims=True)
    v0 = jnp.concatenate([vb_ref[pl.ds(prev0, blk), :], vb_ref[pl.ds(own0, blk), :]], axis=0)
    acc = jnp.dot(p.astype(BF16), v0, preferred_element_type=F32)

    def body(it, carry):
        m_i, l_i, acc = carry
        k0 = pl.multiple_of(it * rows2, rows2)
        s = lax.dot_general(q_aug, kb_ref[pl.ds(k0, rows2), :], nt, preferred_element_type=F32)
        tail = jnp.where(2 * it + 1 < nfar, 0.0, NEG)
        s = jnp.concatenate([s[:, :blk], s[:, blk:] + tail], axis=1)
        m_new = jnp.maximum(m_i, jnp.max(s, axis=1, keepdims=True))
        alpha = jnp.exp(m_i - m_new)
        p = jnp.exp(s - m_new)
        l_new = alpha * l_i + jnp.sum(p, axis=1, keepdims=True)
        acc_new = alpha * acc + jnp.dot(p.astype(BF16), vb_ref[pl.ds(k0, rows2), :], preferred_element_type=F32)
        return m_new, l_new, acc_new

    m_i, l_i, acc = lax.fori_loop(0, (jnp.maximum(nfar, 0) + 1) // 2, body, (m_i, l_i, acc))
    out = acc / l_i
    o_ref[0] = jnp.where(first, out[:blk], out[blk:]).astype(o_ref.dtype)


def moba_attention(p3d, rel_bias):
    bsz, seq, _ = p3d.shape
    blk = MOBA_BLOCK
    n_blocks = seq // blk
    assert n_blocks <= MOBA_LO and seq % blk == 0
    span = 2 * blk
    by_dist = rel_bias[:, _rel_bucket(jnp.arange(span))].astype(F32)
    shift = jnp.arange(span)

    def toeplitz(c):
        k = jnp.where(shift < blk, shift, shift - span)
        s = by_dist[:, jnp.clip(c - k, 0, span - 1)]
        tiled = jnp.tile(s, (1, blk))[:, :blk * (span - 1)]
        return tiled.reshape(HEADS, blk, span - 1)[:, :, :blk]

    bias_own = toeplitz(0)
    bias_prev = toeplitz(blk)
    bias_far = jnp.broadcast_to(rel_bias[:, REL_BUCKETS - 1].astype(F32)[:, None, None], (HEADS, 8, LANES))
    kern = functools.partial(_moba_kernel, n_blocks=n_blocks)
    return pl.pallas_call(
        kern,
        grid=(bsz, PAIRS, n_blocks),
        in_specs=[
            pl.BlockSpec((1, blk, LANES), lambda b, h, i: (b, i, h)),
            pl.BlockSpec((1, seq, LANES), lambda b, h, i: (b, 0, PAIRS + h)),
            pl.BlockSpec((1, seq, LANES), lambda b, h, i: (b, 0, 2 * PAIRS + h)),
            pl.BlockSpec((2, blk, blk), lambda b, h, i: (h, 0, 0)),
            pl.BlockSpec((2, blk, blk), lambda b, h, i: (h, 0, 0)),
            pl.BlockSpec((2, 8, LANES), lambda b, h, i: (h, 0, 0)),
        ],
        out_specs=pl.BlockSpec((1, blk, LANES), lambda b, h, i: (b, i, h)),
        out_shape=jax.ShapeDtypeStruct((bsz, seq, WIDTH), BF16),
        scratch_shapes=[
            pltpu.VMEM((seq, 2 * LANES), BF16),
            pltpu.VMEM((seq, LANES), BF16),
            pltpu.VMEM((LANES, LANES), F32),
        ],
        compiler_params=_cparams(("parallel", "parallel", "arbitrary")),
        name="moba",
    )(p3d, p3d, p3d, bias_own, bias_prev, bias_far)


def _shifted(x, carry_row):
    rows = lax.broadcasted_iota(jnp.int32, x.shape, 0)
    return jnp.where(rows == 0, carry_row, pltpu.roll(x, 1, axis=0))


def _rwkv_prep_kernel(pr_ref, pk_ref, pv_ref, pl_ref, mu_ref, vec_ref, ww_ref, wa_ref, wg_ref,
                      bd_ref, tri_ref,
                      rt_ref, kt_ref, kd_ref, bd_out_ref, v_ref, g_ref, bonus_ref, pend_ref,
                      carry_ref, *, chunk):
    @pl.when(pl.program_id(1) == 0)
    def _():
        carry_ref[...] = jnp.zeros_like(carry_ref)

    def mix(ref, j):
        x = ref[0]
        mu = mu_ref[0:1, j * WIDTH:(j + 1) * WIDTH]
        prev = _shifted(x, carry_ref[0:1, j * WIDTH:(j + 1) * WIDTH])
        carry_ref[0:1, j * WIDTH:(j + 1) * WIDTH] = x[x.shape[0] - 1:, :]
        return x + mu * (prev - x)

    r = mix(pr_ref, 0)
    k = mix(pk_ref, 1)
    v = mix(pv_ref, 2)
    lo = mix(pl_ref, 3)
    w0, a0, k_k, k_a, r_k = (vec_ref[i:i + 1, :] for i in range(5))
    xwa = lo[:, 0:LANES]
    xg = lo[:, LANES:3 * LANES]
    lw = jnp.dot(jnp.tanh(xwa), ww_ref[...], precision=HI, preferred_element_type=F32)
    la = jnp.dot(xwa, wa_ref[...], precision=HI, preferred_element_type=F32)
    g = jnp.dot(jax.nn.sigmoid(xg), wg_ref[...], precision=HI, preferred_element_type=F32)
    z = -(w0 + lw)
    softplus = jnp.maximum(z, 0.0) + jnp.log(1.0 + jnp.exp(-jnp.abs(z)))
    logw = -jnp.exp(-softplus - 0.5)
    a = jax.nn.sigmoid(a0 + la)
    kk = k * k_k
    ss = jnp.dot(kk * kk, bd_ref[...], precision=HI, preferred_element_type=F32)
    kk = kk / jnp.maximum(jnp.sqrt(ss), 1e-12)
    k2 = k * (1.0 + (a - 1.0) * k_a)
    rk = jnp.dot(r * k2 * r_k, bd_ref[...], precision=HI, preferred_element_type=F32)
    cs = jnp.dot(tri_ref[...], logw, precision=HI, preferred_element_type=F32)
    e_pos = jnp.exp(cs)
    e_neg = jnp.exp(-cs)
    rt_ref[0] = (r * e_pos).astype(rt_ref.dtype)
    kt_ref[0] = (kk * jnp.exp(cs - logw)).astype(kt_ref.dtype)
    kd_ref[0] = (k2 * e_neg).astype(kd_ref.dtype)
    bd_out_ref[0] = (kk * a * e_neg).astype(bd_out_ref.dtype)
    v_ref[0] = v.astype(v_ref.dtype)
    g_ref[0] = g
    bonus_ref[0] = rk * v
    ts = e_pos.shape[0]
    for c in range(ts // chunk):
        pend_ref[0, c:c + 1, :] = e_pos[(c + 1) * chunk - 1:(c + 1) * chunk, :]


def rwkv_prep(p3d, rwkv_mu, w0, w_lora_up, a0, a_lora_up, g_lora_up, k_k, k_a, r_k, *, ts=512):
    bsz, seq, _ = p3d.shape
    chunk = RWKV_CHUNK
    ts = min(ts, seq)
    mu = jnp.pad(rwkv_mu, (0, COL_B - COL_B_RAW)).reshape(1, COL_B)
    vec = jnp.stack([w0, a0, k_k, k_a, r_k.reshape(-1)] + [jnp.zeros_like(w0)] * 3).astype(F32)
    ww = jnp.zeros((LANES, WIDTH), F32).at[:DECAY_LORA].set(w_lora_up)
    wa = jnp.zeros((LANES, WIDTH), F32).at[DECAY_LORA:DECAY_LORA + AAA_LORA].set(a_lora_up)
    wg = jnp.zeros((2 * LANES, WIDTH), F32).at[:GATE_LORA].set(g_lora_up)
    hid = jnp.arange(WIDTH) // HEAD_DIM
    bd = (hid[:, None] == hid[None, :]).astype(F32)
    tix = jnp.arange(ts)
    tri = ((tix[:, None] // chunk == tix[None, :] // chunk) & (tix[None, :] <= tix[:, None])).astype(F32)
    c0 = 0
    big = jax.ShapeDtypeStruct((bsz, seq, WIDTH), F32)
    wspec = lambda shape: pl.BlockSpec(shape, lambda b, i: (0, 0))
    ospec = pl.BlockSpec((1, ts, WIDTH), lambda b, i: (b, i, 0))
    return pl.pallas_call(
        functools.partial(_rwkv_prep_kernel, chunk=chunk),
        grid=(bsz, seq // ts),
        in_specs=[
            pl.BlockSpec((1, ts, WIDTH), lambda b, i: (b, i, c0)),
            pl.BlockSpec((1, ts, WIDTH), lambda b, i: (b, i, c0 + 1)),
            pl.BlockSpec((1, ts, WIDTH), lambda b, i: (b, i, c0 + 2)),
            pl.BlockSpec((1, ts, WIDTH), lambda b, i: (b, i, c0 + 3)),
            wspec((1, COL_B)), wspec((8, WIDTH)), wspec((LANES, WIDTH)), wspec((LANES, WIDTH)),
            wspec((2 * LANES, WIDTH)), wspec((WIDTH, WIDTH)), wspec((ts, ts)),
        ],
        out_specs=[ospec] * 7 + [pl.BlockSpec((1, ts // chunk, WIDTH), lambda b, i: (b, i, 0))],
        out_shape=[jax.ShapeDtypeStruct((bsz, seq, WIDTH), BF16)] * 5 + [big] * 2
        + [jax.ShapeDtypeStruct((bsz, seq // chunk, WIDTH), F32)],
        scratch_shapes=[pltpu.VMEM((8, COL_B), F32)],
        compiler_params=_cparams(("parallel", "arbitrary")),
        name="rwkv_prep",
    )(p3d, p3d, p3d, p3d, mu, vec, ww, wa, wg, bd, tri)


def _rwkv_scan_kernel(rt_ref, kt_ref, kd_ref, bd_ref, v_ref, g_ref, bonus_ref, pend_ref, ln_ref, o_ref,
                      state_ref, *, chunk, prec):
    @pl.when(pl.program_id(1) == 0)
    def _():
        state_ref[...] = jnp.zeros_like(state_ref)

    c2 = 2 * chunk
    lane = lax.broadcasted_iota(jnp.int32, (chunk, LANES), 1)
    first = lane < HEAD_DIM
    row = lax.broadcasted_iota(jnp.int32, (c2, c2), 0)
    col = lax.broadcasted_iota(jnp.int32, (c2, c2), 1)
    eye = (row == col).astype(F32)
    hrow = lax.broadcasted_iota(jnp.int32, (LANES, LANES), 0) // HEAD_DIM
    hcol = lax.broadcasted_iota(jnp.int32, (LANES, LANES), 1) // HEAD_DIM
    head_mean = jnp.where(hrow == hcol, 1.0 / HEAD_DIM, 0.0).astype(F32)
    nt = (((1,), (1,)), ((), ()))
    tn = (((0,), (0,)), ((), ()))
    dot = functools.partial(jnp.dot, precision=prec, preferred_element_type=F32)
    dotg = functools.partial(lax.dot_general, precision=prec, preferred_element_type=F32)

    def stack(x):
        return jnp.concatenate([jnp.where(first, x, 0.0), jnp.where(first, 0.0, x)], axis=0)

    pairs = range(PAIRS)
    sls = [slice(hp * LANES, (hp + 1) * LANES) for hp in pairs]
    rs, ks, kds, bs, vs = ([stack(ref[0, :, sl].astype(F32)) for sl in sls]
                           for ref in (rt_ref, kt_ref, kd_ref, bd_ref, v_ref))
    hts = [state_ref[0, hp] for hp in pairs]
    big = [dotg(jnp.concatenate([ks[hp], rs[hp]], axis=0), jnp.concatenate([bs[hp], kds[hp]], axis=0), nt)
           for hp in pairs]
    a_b = [jnp.where(row > col, big[hp][0:c2, 0:c2], 0.0) for hp in pairs]
    a_k = [jnp.where(row > col, big[hp][0:c2, c2:], 0.0) for hp in pairs]
    a_rb = [jnp.where(row >= col, big[hp][c2:, 0:c2], 0.0) for hp in pairs]
    a_rk = [jnp.where(row >= col, big[hp][c2:, c2:], 0.0) for hp in pairs]
    kh = [dotg(jnp.concatenate([ks[hp], rs[hp]], axis=0), hts[hp], nt) for hp in pairs]
    av = [dot(jnp.concatenate([a_k[hp], a_rk[hp]], axis=0), vs[hp]) for hp in pairs]
    vk = [dotg(vs[hp], kds[hp], tn) for hp in pairs]
    inv = [eye - a_b[hp] for hp in pairs]
    pw = [dot(a_b[hp], a_b[hp]) for hp in pairs]
    n_sq = int(math.log2(chunk)) - 1
    for lvl in range(n_sq):
        if lvl + 1 < n_sq:
            both = [dot(jnp.concatenate([inv[hp], pw[hp]], axis=0), pw[hp]) for hp in pairs]
            inv = [inv[hp] + both[hp][0:c2] for hp in pairs]
            pw = [both[hp][c2:] for hp in pairs]
        else:
            inv = [inv[hp] + dot(inv[hp], pw[hp]) for hp in pairs]
    us = [dot(inv[hp], kh[hp][0:c2] + av[hp][0:c2]) for hp in pairs]
    ub = [dotg(us[hp], bs[hp], tn) for hp in pairs]
    au = [dot(a_rb[hp], us[hp]) for hp in pairs]
    for hp in pairs:
        sl = sls[hp]
        pend = pend_ref[0, 0, 0:1, sl]
        state_ref[0, hp] = (hts[hp] + vk[hp] - ub[hp]) * pend
        os_ = kh[hp][c2:] + av[hp][c2:] - au[hp]
        o = os_[0:chunk] + os_[chunk:]
        mu = jnp.dot(o, head_mean, precision=HI, preferred_element_type=F32)
        d = o - mu
        var = jnp.dot(d * d, head_mean, precision=HI, preferred_element_type=F32)
        on = d * lax.rsqrt(var + GN_EPS) * ln_ref[0:1, sl] + ln_ref[1:2, sl]
        o_ref[0, :, sl] = ((on + bonus_ref[0, :, sl]) * g_ref[0, :, sl]).astype(o_ref.dtype)


def rwkv_scan(rt, kt, kd, bd, v, g, bonus, pend, lnx_g, lnx_b, *, prec=None):
    bsz, seq, _ = rt.shape
    chunk = RWKV_CHUNK
    n_chunks = seq // chunk
    ln = jnp.stack([lnx_g, lnx_b] + [jnp.zeros_like(lnx_g)] * 6).astype(F32)
    pend4 = pend.reshape(bsz, n_chunks, 1, WIDTH)
    spec = pl.BlockSpec((1, chunk, WIDTH), lambda b, c: (b, c, 0))
    return pl.pallas_call(
        functools.partial(_rwkv_scan_kernel, chunk=chunk, prec=prec),
        grid=(bsz, n_chunks),
        in_specs=[spec] * 7 + [
            pl.BlockSpec((1, 1, 1, WIDTH), lambda b, c: (b, c, 0, 0)),
            pl.BlockSpec((8, WIDTH), lambda b, c: (0, 0)),
        ],
        out_specs=spec,
        out_shape=jax.ShapeDtypeStruct((bsz, seq, WIDTH), BF16),
        scratch_shapes=[pltpu.VMEM((1, PAIRS, LANES, LANES), F32)],
        compiler_params=_cparams(("parallel", "arbitrary")),
        name="rwkv_scan",
    )(rt, kt, kd, bd, v, g, bonus, pend4, ln)


def _merge_kernel(x_ref, oa_ref, ob_ref, ga_ref, gb_ref, wa_ref, wb_ref, wo_ref, g2_ref,
                  h_ref, xn_ref, acc_ref):
    j = pl.program_id(1)

    @pl.when(j == 0)
    def _():
        acc_ref[...] = x_ref[...]

    ya = jnp.dot(oa_ref[...].astype(BF16), wa_ref[...], preferred_element_type=F32)
    yb = jnp.dot(ob_ref[...].astype(BF16), wb_ref[...], preferred_element_type=F32)
    y = jax.nn.sigmoid(ga_ref[...].astype(F32)) * ya + jax.nn.sigmoid(gb_ref[...].astype(F32)) * yb
    acc_ref[...] += jnp.dot(y.astype(BF16), wo_ref[...], preferred_element_type=F32)

    @pl.when(j == pl.num_programs(1) - 1)
    def _():
        h = acc_ref[...]
        h_ref[...] = h
        ms = jnp.mean(h * h, axis=-1, keepdims=True)
        xn_ref[...] = _pack_halves(h * lax.rsqrt(ms + RMS_EPS) * g2_ref[...])


def _pack_halves(x):
    half = x.shape[1] // 2
    lo = lax.bitcast_convert_type(x[:, :half].astype(BF16).astype(F32), jnp.int32)
    hi = lax.bitcast_convert_type(x[:, half:].astype(BF16).astype(F32), jnp.int32)
    return lax.bitwise_or(lax.shift_right_logical(lo, jnp.int32(16)), hi)


def _unpack_halves(words):
    lo, hi = _unpack_words(words)
    return jnp.concatenate([lo, hi], axis=1)


def merge_out(x2d, oa, ob, p2d, w_proj_a, w_proj_b, w_out, norm2_g, *, row0=0, tm=512):
    t, d = oa.shape[0], x2d.shape[1]
    r0 = row0 // tm
    tn = WIDTH
    nj = d // tn
    g0 = 0
    return pl.pallas_call(
        _merge_kernel,
        grid=(t // tm, nj),
        in_specs=[
            pl.BlockSpec((tm, d), lambda i, j: (r0 + i, 0)),
            pl.BlockSpec((tm, WIDTH), lambda i, j: (i, 0)),
            pl.BlockSpec((tm, WIDTH), lambda i, j: (i, 0)),
            pl.BlockSpec((tm, tn), lambda i, j: (i, g0 + j)),
            pl.BlockSpec((tm, tn), lambda i, j: (i, g0 + nj + j)),
            pl.BlockSpec((WIDTH, tn), lambda i, j: (0, j)),
            pl.BlockSpec((WIDTH, tn), lambda i, j: (0, j)),
            pl.BlockSpec((tn, d), lambda i, j: (j, 0)),
            pl.BlockSpec((1, d), lambda i, j: (0, 0)),
        ],
        out_specs=[pl.BlockSpec((tm, d), lambda i, j: (i, 0)), pl.BlockSpec((tm, d // 2), lambda i, j: (i, 0))],
        out_shape=[jax.ShapeDtypeStruct((t, d), F32), jax.ShapeDtypeStruct((t, d // 2), jnp.int32)],
        scratch_shapes=[pltpu.VMEM((tm, d), F32)],
        compiler_params=_cparams(("parallel", "arbitrary")),
        name="merge_out",
    )(x2d, oa, ob, p2d, p2d, w_proj_a.astype(BF16), w_proj_b.astype(BF16), w_out.astype(BF16),
      norm2_g.reshape(1, d))


PEER_HEADS = 8
PEER_NKEYS = 128
PEER_TOPK = 16
PEER_HALF = 128


def _topk_rows(s, k):
    n = s.shape[0]
    rows = lax.broadcasted_iota(jnp.int32, s.shape, 0).astype(F32)
    vals, ids = [], []
    for _ in range(k):
        m = jnp.max(s, axis=0, keepdims=True)
        first = jnp.min(jnp.where(s == m, rows, float(n)), axis=0, keepdims=True)
        vals.append(m)
        ids.append(first)
        s = jnp.where(rows == first, -jnp.inf, s)
    return jnp.concatenate(vals, axis=0), jnp.concatenate(ids, axis=0)


def _take_rows(table, ids):
    rows = lax.broadcasted_iota(jnp.int32, table.shape, 0).astype(F32)
    return jnp.sum(jnp.where(rows == ids, table, 0.0), axis=0, keepdims=True)


def _peer_route_kernel(xn_ref, wq_ref, sk_ref, idx_ref, gate_ref, *, prec):
    tt = xn_ref.shape[0]
    k = PEER_TOPK
    xn = _unpack_halves(xn_ref[...]) if xn_ref.dtype == jnp.int32 else xn_ref[...]
    q = jnp.dot(xn.astype(wq_ref.dtype), wq_ref[...], precision=prec, preferred_element_type=F32)
    nt = (((1,), (1,)), ((), ()))
    idx_rows, gate_rows = [], []
    half = k // 2
    for h in range(PEER_HEADS):
        tops = []
        for p in range(2):
            c0 = (h * 2 + p) * PEER_HALF
            s = lax.dot_general(sk_ref[h, p].astype(wq_ref.dtype), q[:, c0:c0 + PEER_HALF].astype(wq_ref.dtype),
                                nt, precision=prec, preferred_element_type=F32)
            tops.append(_topk_rows(s, k))
        (s0, i0), (s1, i1) = tops
        cs = [s0[0:1] + s1] + [s0[i:i + 1] + s1[0:half] for i in range(1, half)] + [s0[half:] + s1[0:1]]
        best_s, pos = _topk_rows(jnp.concatenate(cs, axis=0), k)
        mid = jnp.floor((pos - k) * (1.0 / half))
        end_mid = float(k + (half - 1) * half)
        i_rank = jnp.where(pos < k, 0.0, jnp.where(pos < end_mid, 1.0 + mid, pos - (end_mid - half)))
        j_rank = jnp.where(pos < k, pos, jnp.where(pos < end_mid, (pos - k) - half * mid, 0.0))
        ids = [_take_rows(i0, i_rank[n:n + 1]) * PEER_NKEYS + _take_rows(i1, j_rank[n:n + 1]) for n in range(k)]
        e = jnp.exp(best_s - best_s[0:1])
        gate_rows.append(e / jnp.sum(e, axis=0, keepdims=True))
        idx_rows.append(jnp.concatenate(ids, axis=0).astype(jnp.int32))
    idx_ref[...] = jnp.concatenate(idx_rows, axis=0).T
    gate_ref[...] = jnp.concatenate(gate_rows, axis=0).T


def peer_route(xn2d, peer_wq, peer_subkeys, *, tt=256, prec=None, wdtype=BF16):
    t, dx = xn2d.shape
    d, nq = peer_wq.shape
    n_sel = PEER_HEADS * PEER_TOPK
    return pl.pallas_call(
        functools.partial(_peer_route_kernel, prec=prec),
        grid=(t // tt,),
        in_specs=[
            pl.BlockSpec((tt, dx), lambda i: (i, 0)),
            pl.BlockSpec((d, nq), lambda i: (0, 0)),
            pl.BlockSpec((PEER_HEADS, 2, PEER_NKEYS, PEER_HALF), lambda i: (0, 0, 0, 0)),
        ],
        out_specs=[pl.BlockSpec((tt, n_sel), lambda i: (i, 0))] * 2,
        out_shape=[jax.ShapeDtypeStruct((t, n_sel), jnp.int32), jax.ShapeDtypeStruct((t, n_sel), F32)],
        compiler_params=_cparams(("parallel",)),
        name="peer_route",
    )(xn2d, peer_wq.astype(wdtype), peer_subkeys)


def _final_kernel(h_ref, y_ref, g_ref, *rest):
    o_ref = rest[-1]
    h = h_ref[...] + y_ref[...]
    ms = jnp.mean(h * h, axis=-1, keepdims=True)
    o_ref[...] = h * lax.rsqrt(ms + RMS_EPS) * g_ref[...]


def final_norm(h2d, y2d, g, *, out=None, row0=0, total_rows=None, tm=1024):
    t, d = h2d.shape
    total = t if total_rows is None else total_rows
    r0 = row0 // tm
    spec = pl.BlockSpec((tm, d), lambda i: (i, 0))
    in_specs = [spec, spec, pl.BlockSpec((1, d), lambda i: (0, 0))]
    args = [h2d, y2d, g.reshape(1, d)]
    aliases = {}
    if out is not None:
        in_specs.append(pl.BlockSpec(memory_space=pl.ANY))
        args.append(out)
        aliases = {3: 0}
    return pl.pallas_call(
        _final_kernel,
        grid=(t // tm,),
        in_specs=in_specs,
        out_specs=pl.BlockSpec((tm, d), lambda i: (r0 + i, 0)),
        out_shape=jax.ShapeDtypeStruct((total, d), F32),
        input_output_aliases=aliases,
        compiler_params=_cparams(("parallel",)),
        name="final_norm",
    )(*args)


SC_CORES = 2
SC_SUBCORES = 16
SC_LANES = 16
SC_WORKERS = SC_CORES * SC_SUBCORES
PEER_SEL = PEER_HEADS * PEER_TOPK
PEER_ROWS = 32
PEER_PARTS = PEER_SEL // PEER_ROWS
PEER_NBUF = 4
PEER_GROUP = 32
PEER_BF16_RUN = 4


def _pack_rows_kernel(w_ref, o_ref):
    o_ref[...] = _pack_halves(w_ref[...])


def _pack_rows(w, *, tr=1024):
    e, d = w.shape
    return pl.pallas_call(
        _pack_rows_kernel,
        grid=(e // tr,),
        in_specs=[pl.BlockSpec((tr, d), lambda i: (i, 0))],
        out_specs=pl.BlockSpec((tr, d // 2), lambda i: (i, 0)),
        out_shape=jax.ShapeDtypeStruct((e, d // 2), jnp.int32),
        compiler_params=_cparams(("parallel",)),
        name="pack_rows",
    )(w)


def _unpack_words(w):
    lo = lax.bitcast_convert_type(lax.shift_left(w, jnp.int32(16)), F32)
    hi = lax.bitcast_convert_type(lax.bitwise_and(w, jnp.int32(-65536)), F32)
    return lo, hi


def _packed_dot(a_words, b_words):
    from jax.experimental.pallas import tpu_sc as plsc
    prods = [plsc.bitcast(a, BF16) * plsc.bitcast(b, BF16) for a, b in zip(a_words, b_words)]
    while len(prods) > 1:
        prods = [prods[k] + prods[k + 1] for k in range(0, len(prods), 2)]
    return _unpack_words(plsc.bitcast(prods[0], jnp.int32))


def _sc_mesh():
    from jax.experimental.pallas import tpu_sc as plsc
    return plsc.VectorSubcoreMesh(core_axis_name="c", subcore_axis_name="s",
                                  num_cores=SC_CORES, num_subcores=SC_SUBCORES)


def _sc_loop(n, body, carry):
    from jax.experimental.pallas import tpu_sc as plsc
    return plsc.parallel_loop(0, n, carry=carry)(body)


def _worker_base(tokens_per_worker):
    return (lax.axis_index("s") * SC_CORES + lax.axis_index("c")) * tokens_per_worker


def _gather_compute_loop(table_hbm, idx_v, rows_v, sem, stage_v, out_row, osem, grp, compute):
    n_gathers = PEER_PARTS * grp
    ahead = PEER_NBUF - 1

    def gather(j, b):
        i = j // PEER_PARTS if isinstance(j, int) else lax.shift_right_logical(j, PEER_PARTS.bit_length() - 1)
        h = j % PEER_PARTS if isinstance(j, int) else lax.bitwise_and(j, PEER_PARTS - 1)
        ids = idx_v.at[i, pl.ds(pl.multiple_of(h * PEER_ROWS, PEER_ROWS), PEER_ROWS)]
        return pltpu.make_async_copy(table_hbm.at[ids], rows_v.at[b], sem.at[b])

    def put(i, slot):
        return pltpu.make_async_copy(stage_v.at[slot], out_row(i), osem.at[slot])

    for j in range(ahead):
        gather(j, j).start()

    @pl.loop(0, n_gathers)
    def _(j):
        b = lax.bitwise_and(j, PEER_NBUF - 1)
        h = lax.bitwise_and(j, PEER_PARTS - 1)
        i = lax.shift_right_logical(j, PEER_PARTS.bit_length() - 1)
        slot = lax.bitwise_and(i, 1)

        @pl.when((h == 0) & (i >= 2))
        def _():
            put(i - 2, slot).wait()

        @pl.when(j + ahead < n_gathers)
        def _():
            gather(j + ahead, lax.bitwise_and(j + ahead, PEER_NBUF - 1)).start()

        gather(j, b).wait()
        compute(i, h, b, slot)

        @pl.when(h == PEER_PARTS - 1)
        def _():
            put(i, slot).start()

    put(grp - 2, 0).wait()
    put(grp - 1, 1).wait()


def peer_expert_dots(x_packed, idx, u_packed):
    t, half = x_packed.shape
    n_chunks = half // SC_LANES
    tpw = t // SC_WORKERS
    grp = min(PEER_GROUP, tpw)
    rows_tog = 8

    def body(x_hbm, idx_hbm, u_hbm, out_hbm, idx_v, x_v, rows_v, ps_v, sem, osem):
        base = _worker_base(tpw)

        def compute(i, h, b, slot):
            @pl.loop(0, PEER_ROWS // rows_tog)
            def _(rg):
                r0 = rg * rows_tog
                accs = [[None, None] for _ in range(rows_tog)]
                for c0 in range(0, n_chunks, PEER_BF16_RUN):
                    ats = [pl.ds((c0 + k) * SC_LANES, SC_LANES) for k in range(PEER_BF16_RUN)]
                    xw = [x_v[i, at] for at in ats]
                    for r in range(rows_tog):
                        terms = _packed_dot([rows_v[b, r0 + r, at] for at in ats], xw)
                        for k, term in enumerate(terms):
                            accs[r][k] = term if accs[r][k] is None else accs[r][k] + term
                for r in range(rows_tog):
                    at = pl.ds(pl.multiple_of((h * PEER_ROWS + r0 + r) * SC_LANES, SC_LANES), SC_LANES)
                    ps_v[slot, at] = accs[r][0] + accs[r][1]

        @pl.loop(0, tpw // grp)
        def _(g):
            t0 = base + g * grp
            pltpu.sync_copy(idx_hbm.at[pl.ds(t0, grp)], idx_v)
            pltpu.sync_copy(x_hbm.at[pl.ds(t0, grp)], x_v)
            _gather_compute_loop(u_hbm, idx_v, rows_v, sem, ps_v, lambda i: out_hbm.at[t0 + i], osem, grp, compute)

    return pl.kernel(
        body,
        out_type=jax.ShapeDtypeStruct((t, PEER_SEL * SC_LANES), F32),
        mesh=_sc_mesh(),
        scratch_types=[
            pltpu.VMEM((grp, PEER_SEL), jnp.int32),
            pltpu.VMEM((grp, half), jnp.int32),
            pltpu.VMEM((PEER_NBUF, PEER_ROWS, half), jnp.int32),
            pltpu.VMEM((2, PEER_SEL * SC_LANES), F32),
            pltpu.SemaphoreType.DMA((PEER_NBUF,)),
            pltpu.SemaphoreType.DMA((2,)),
        ],
        compiler_params=pltpu.CompilerParams(needs_layout_passes=False),
        name="peer_expert_dots",
    )(x_packed, idx, u_packed)


def peer_expert_mix(hgw, idx, v_packed):
    t = hgw.shape[0]
    half = v_packed.shape[1]
    d = 2 * half
    tpw = t // SC_WORKERS
    grp = min(PEER_GROUP, tpw)
    n_parts = 2
    cpp = half // SC_LANES // n_parts
    from jax.experimental.pallas import tpu_sc as plsc

    def body(hg_hbm, idx_hbm, v_hbm, out_hbm, idx_v, hg_v, rows_v, o_v2, sem, osem):
        base = _worker_base(tpw)

        def compute(i, h, b, slot):
            token = jnp.full((SC_LANES,), i, jnp.int32)
            for part in range(n_parts):
                def rbody(rq, accs):
                    r0 = rq * PEER_BF16_RUN
                    s = [plsc.load_gather(hg_v, [token, jnp.full((SC_LANES,), h * PEER_ROWS + r0 + k, jnp.int32)])
                         for k in range(PEER_BF16_RUN)]
                    new = []
                    for c in range(cpp):
                        at = pl.ds((part * cpp + c) * SC_LANES, SC_LANES)
                        lo, hi = _packed_dot([rows_v[b, r0 + k, at] for k in range(PEER_BF16_RUN)], s)
                        new.append(accs[2 * c] + lo)
                        new.append(accs[2 * c + 1] + hi)
                    return tuple(new)

                accs = _sc_loop(PEER_ROWS // PEER_BF16_RUN, rbody,
                                tuple(jnp.zeros((SC_LANES,), F32) for _ in range(2 * cpp)))
                def store(overwrite):
                    for c in range(cpp):
                        lo_at = pl.ds((part * cpp + c) * SC_LANES, SC_LANES)
                        hi_at = pl.ds(half + (part * cpp + c) * SC_LANES, SC_LANES)
                        if overwrite:
                            o_v2[slot, lo_at] = accs[2 * c]
                            o_v2[slot, hi_at] = accs[2 * c + 1]
                        else:
                            o_v2[slot, lo_at] = o_v2[slot, lo_at] + accs[2 * c]
                            o_v2[slot, hi_at] = o_v2[slot, hi_at] + accs[2 * c + 1]

                pl.when(h == 0)(functools.partial(store, True))
                pl.when(h != 0)(functools.partial(store, False))

        @pl.loop(0, tpw // grp)
        def _(g):
            t0 = base + g * grp
            pltpu.sync_copy(idx_hbm.at[pl.ds(t0, grp)], idx_v)
            pltpu.sync_copy(hg_hbm.at[pl.ds(t0, grp)], hg_v)
            _gather_compute_loop(v_hbm, idx_v, rows_v, sem, o_v2, lambda i: out_hbm.at[t0 + i], osem, grp, compute)

    return pl.kernel(
        body,
        out_type=jax.ShapeDtypeStruct((t, d), F32),
        mesh=_sc_mesh(),
        scratch_types=[
            pltpu.VMEM((grp, PEER_SEL), jnp.int32),
            pltpu.VMEM((grp, PEER_SEL), jnp.int32),
            pltpu.VMEM((PEER_NBUF, PEER_ROWS, half), jnp.int32),
            pltpu.VMEM((2, d), F32),
            pltpu.SemaphoreType.DMA((PEER_NBUF,)),
            pltpu.SemaphoreType.DMA((2,)),
        ],
        compiler_params=pltpu.CompilerParams(needs_layout_passes=False),
        name="peer_expert_mix",
    )(hgw, idx, v_packed)


def _peer_act_kernel(ps_ref, gate_ref, sum_ref, o_ref):
    ps = ps_ref[...]
    sel = sum_ref[...]
    hi = ps.astype(BF16)
    rest = ps - hi.astype(F32)
    mid = rest.astype(BF16)
    lo = (rest - mid.astype(F32)).astype(BF16)
    pre = (jnp.dot(hi, sel, preferred_element_type=F32) + jnp.dot(mid, sel, preferred_element_type=F32)
           + jnp.dot(lo, sel, preferred_element_type=F32))
    hg = 0.5 * pre * (1.0 + lax.erf(pre * (1.0 / math.sqrt(2.0)))) * gate_ref[...]
    bits = lax.bitcast_convert_type(hg.astype(BF16).astype(F32), jnp.int32)
    o_ref[...] = lax.bitwise_or(bits, lax.shift_right_logical(bits, jnp.int32(16)))


def peer_act(ps, gates, *, tm=512):
    t, n = ps.shape
    lane_sum = (jnp.arange(n)[:, None] // SC_LANES == jnp.arange(PEER_SEL)[None, :]).astype(BF16)
    return pl.pallas_call(
        _peer_act_kernel,
        grid=(t // tm,),
        in_specs=[
            pl.BlockSpec((tm, n), lambda i: (i, 0)),
            pl.BlockSpec((tm, PEER_SEL), lambda i: (i, 0)),
            pl.BlockSpec((n, PEER_SEL), lambda i: (0, 0)),
        ],
        out_specs=pl.BlockSpec((tm, PEER_SEL), lambda i: (i, 0)),
        out_shape=jax.ShapeDtypeStruct((t, PEER_SEL), jnp.int32),
        compiler_params=_cparams(("parallel",)),
        name="peer_act",
    )(ps, gates, lane_sum)


BATCH_GROUPS = 8


def kernel(x, norm1_g, w_in, rwkv_mu, w0, w_lora_up, a0, a_lora_up, g_lora_up, k_k, k_a, r_k, lnx_g, lnx_b,
           w_proj_a, w_proj_b, w_out, norm2_g, peer_wq, peer_subkeys, peer_u, peer_v, rel_bias, normf_g):
    bsz, seq, d = x.shape
    depth = norm1_g.shape[0]
    groups = BATCH_GROUPS if bsz % BATCH_GROUPS == 0 else 1
    gb = bsz // groups
    tg = gb * seq
    t = bsz * seq
    src = x.reshape(t, d)
    for l in range(depth):
        w_pad = jnp.concatenate([
            w_in[l][:, :COL_A + COL_B_RAW],
            jnp.zeros((d, COL_B - COL_B_RAW), w_in.dtype),
            w_in[l][:, COL_A + COL_B_RAW:]], axis=1).astype(BF16)
        u_packed = _pack_rows(peer_u[l])
        v_packed = _pack_rows(peer_v[l])
        last = l == depth - 1

        def mix(pending, tie=None):
            row0, h2d, ps, gates, idx = pending
            hgw = peer_act(ps, gates)
            if tie is not None:
                tie, hgw = lax.optimization_barrier((tie, hgw))
            return tie, (row0, h2d, peer_expert_mix(hgw, idx, v_packed))

        outs = []

        def close(mixed):
            row0, h2d, y2d = mixed
            if last:
                outs.append(final_norm(h2d, y2d, normf_g, out=outs[-1] if outs else None, row0=row0, total_rows=t))
            else:
                outs.append(h2d + y2d)

        pending = closing = None
        for g in range(groups):
            pa, pb, pg = norm_proj(src, norm1_g[l], w_pad, row0=g * tg, rows=tg)
            oa = moba_attention(pa.reshape(gb, seq, -1), rel_bias)
            prep = tuple(rwkv_prep(pb.reshape(gb, seq, -1), rwkv_mu[l], w0[l], w_lora_up[l], a0[l], a_lora_up[l], g_lora_up[l],
                                   k_k[l], k_a[l], r_k[l]))
            mixed = None
            if pending is not None:
                (oa, prep), mixed = mix(pending, (oa, prep))
            if closing is not None:
                oa, y2d = lax.optimization_barrier((oa, closing[2]))
                close(closing[:2] + (y2d,))
                closing = None
            ob = rwkv_scan(*prep, lnx_g[l], lnx_b[l])
            h2d, xn2 = merge_out(src, oa.reshape(tg, WIDTH), ob.reshape(tg, WIDTH), pg, w_proj_a[l], w_proj_b[l],
                                 w_out[l], norm2_g[l], row0=g * tg)
            idx, gates = peer_route(xn2, peer_wq[l], peer_subkeys[l])
            if mixed is not None:
                idx, y2d = lax.optimization_barrier((idx, mixed[2]))
                closing = mixed[:2] + (y2d,)
            pending = (g * tg, h2d, peer_expert_dots(xn2, idx, u_packed), gates, idx)
        if closing is not None:
            close(closing)
        close(mix(pending)[1])
        src = outs[-1] if last else jnp.concatenate(outs, axis=0)
    return src.reshape(bsz, seq, d)
```

```python
import functools
import math

import jax
import jax.numpy as jnp
from jax import lax
from jax.experimental import pallas as pl
from jax.experimental.pallas import tpu as pltpu

F32 = jnp.float32
BF16 = jnp.bfloat16
HI = lax.Precision.HIGHEST

LANES = 128
HEAD_DIM = 64
HEADS = 8
PAIRS = HEADS // 2
WIDTH = HEADS * HEAD_DIM
MOBA_BLOCK = 256
MOBA_TOPK = 3
MOBA_LO = 64
REL_BUCKETS = 32
REL_MAX_DIST = 128
DECAY_LORA = 64
AAA_LORA = 64
GATE_LORA = 160
GN_EPS = 64e-5
RMS_EPS = 1e-6
NEG = -1e30
RWKV_CHUNK = 64
COL_A = 3 * WIDTH
COL_B_RAW = 3 * WIDTH + DECAY_LORA + AAA_LORA + GATE_LORA
COL_B = 4 * WIDTH
COL_G_OFF = COL_A + COL_B
VMEM_LIMIT = 56 * 1024 * 1024


def _cparams(sem):
    return pltpu.CompilerParams(dimension_semantics=sem, vmem_limit_bytes=VMEM_LIMIT)


def _norm_proj_kernel(x_ref, g_ref, w_ref, pa_ref, pb_ref, pg_ref, xn_ref, *, ja, jb):
    j = pl.program_id(1)

    @pl.when(j == 0)
    def _():
        x = x_ref[...]
        ms = jnp.mean(x * x, axis=-1, keepdims=True)
        xn_ref[...] = (x * lax.rsqrt(ms + RMS_EPS) * g_ref[...]).astype(xn_ref.dtype)

    res = jnp.dot(xn_ref[...], w_ref[...], preferred_element_type=F32)

    @pl.when(j < ja)
    def _():
        pa_ref[...] = res.astype(pa_ref.dtype)

    @pl.when((j >= ja) & (j < jb))
    def _():
        pb_ref[...] = res

    @pl.when(j >= jb)
    def _():
        pg_ref[...] = res.astype(pg_ref.dtype)


def norm_proj(x2d, g, w, *, row0=0, rows=None, tm=2048, tn=512):
    d = x2d.shape[1]
    t = x2d.shape[0] if rows is None else rows
    n = w.shape[1]
    r0 = row0 // tm
    ja, jb, jn = COL_A // tn, COL_G_OFF // tn, n // tn
    return pl.pallas_call(
        functools.partial(_norm_proj_kernel, ja=ja, jb=jb),
        grid=(t // tm, jn),
        in_specs=[
            pl.BlockSpec((tm, d), lambda i, j: (r0 + i, 0)),
            pl.BlockSpec((1, d), lambda i, j: (0, 0)),
            pl.BlockSpec((d, tn), lambda i, j: (0, j)),
        ],
        out_specs=[
            pl.BlockSpec((tm, tn), lambda i, j: (i, jnp.minimum(j, ja - 1))),
            pl.BlockSpec((tm, tn), lambda i, j: (i, jnp.clip(j - ja, 0, jb - ja - 1))),
            pl.BlockSpec((tm, tn), lambda i, j: (i, jnp.maximum(j - jb, 0))),
        ],
        out_shape=[jax.ShapeDtypeStruct((t, COL_A), BF16), jax.ShapeDtypeStruct((t, COL_B), F32),
                   jax.ShapeDtypeStruct((t, n - COL_G_OFF), BF16)],
        scratch_shapes=[pltpu.VMEM((tm, d), w.dtype)],
        compiler_params=_cparams(("parallel", "arbitrary")),
        name="norm_proj",
    )(x2d, g.reshape(1, d), w)


def _rel_bucket(dist):
    n = jnp.maximum(dist, 0)
    max_exact = REL_BUCKETS // 2
    nf = jnp.maximum(n, 1).astype(F32)
    large = max_exact + (jnp.log(nf / max_exact) / math.log(REL_MAX_DIST / max_exact)
                         * (REL_BUCKETS - max_exact)).astype(jnp.int32)
    large = jnp.minimum(large, REL_BUCKETS - 1)
    return jnp.where(n < max_exact, n, large)


def _moba_kernel(q_ref, k_ref, v_ref, bown_ref, bprev_ref, bfar_ref, o_ref,
                 kb_ref, vb_ref, kbar_ref, *, n_blocks):
    qb = pl.program_id(2)
    blk = MOBA_BLOCK
    scale = 1.0 / math.sqrt(HEAD_DIM)

    rows2 = 2 * blk
    nt = (((1,), (1,)), ((), ()))

    @pl.when(qb == 0)
    def _():
        kbar_ref[...] = jnp.zeros_like(kbar_ref)
        lane_b = lax.broadcasted_iota(jnp.int32, (blk, LANES), 1)
        for n in range(n_blocks):
            kblk = k_ref[0, n * blk:(n + 1) * blk, :]
            kbar_ref[n:n + 1, :] = jnp.mean(kblk.astype(F32), axis=0, keepdims=True)
            kb_ref[n * blk:(n + 1) * blk, 0:LANES] = kblk.astype(BF16)
            kb_ref[n * blk:(n + 1) * blk, LANES:] = ((lane_b == n) | (lane_b == MOBA_LO + n)).astype(BF16)
        vb_ref[...] = v_ref[0].astype(BF16)

    q2 = q_ref[0].astype(F32)
    first = lax.broadcasted_iota(jnp.int32, (blk, LANES), 1) < HEAD_DIM
    qh = jnp.concatenate([jnp.where(first, q2, 0.0), jnp.where(first, 0.0, q2)], axis=0)
    lane = lax.broadcasted_iota(jnp.int32, (rows2, LANES), 1)
    rowi = lax.broadcasted_iota(jnp.int32, (rows2, LANES), 0)
    gate = lax.dot_general(qh.astype(BF16), kbar_ref[...].astype(BF16), nt, preferred_element_type=F32)
    g = jnp.where(lane < qb, gate, -jnp.inf)
    chosen = lane < 0
    lane_f = lane.astype(F32)
    for _ in range(MOBA_TOPK):
        m = jnp.max(g, axis=1, keepdims=True)
        idx = jnp.min(jnp.where(g == m, lane_f, float(LANES)), axis=1, keepdims=True)
        hit = (lane_f == idx) & (m > -jnp.inf)
        chosen = chosen | hit
        g = jnp.where(hit, -jnp.inf, g)
    nfar = qb - 1
    bfar = jnp.where(rowi < blk, bfar_ref[0, 0:1, 0:1], bfar_ref[1, 0:1, 0:1])
    bhi = bfar.astype(BF16).astype(F32)
    madd = jnp.where(lane < nfar, jnp.where(chosen, bhi, NEG),
                     jnp.where(lane == nfar, jnp.where(chosen, 0.0, NEG),
                               jnp.where((lane >= MOBA_LO) & (lane - MOBA_LO < nfar), bfar - bhi, 0.0)))
    q_aug = jnp.concatenate([(qh * scale).astype(BF16), madd.astype(BF16)], axis=1)

    prev0 = pl.multiple_of(jnp.maximum(nfar, 0) * blk, blk)
    own0 = pl.multiple_of(qb * blk, blk)
    s_prev = (lax.dot_general(q_aug, kb_ref[pl.ds(prev0, blk), :], nt, preferred_element_type=F32)
              + bprev_ref[...].reshape(rows2, blk) + jnp.where(qb > 0, 0.0, NEG))
    s_own = (lax.dot_general(q_aug, kb_ref[pl.ds(own0, blk), :], nt, preferred_element_type=F32)
             + bown_ref[...].reshape(rows2, blk))
    r = lax.broadcasted_iota(jnp.int32, (rows2, blk), 0)
    c = lax.broadcasted_iota(jnp.int32, (rows2, blk), 1)
    s_own = jnp.where(lax.bitwise_and(r, blk - 1) >= c, s_own, NEG)
    s = jnp.concatenate([s_prev, s_own], axis=1)
    m_i = jnp.max(s, axis=1, keepdims=True)
    p = jnp.exp(s - m_i)
    l_i = jnp.sum(p, axis=1, keepdims=True)
    v0 = jnp.concatenate([vb_ref[pl.ds(prev0, blk), :], vb_ref[pl.ds(own0, blk), :]], axis=0)
    acc = jnp.dot(p.astype(BF16), v0, preferred_element_type=F32)

    def body(it, carry):
        m_i, l_i, acc = carry
        k0 = pl.multiple_of(it * rows2, rows2)
        s = lax.dot_general(q_aug, kb_ref[pl.ds(k0, rows2), :], nt, preferred_element_type=F32)
        tail = jnp.where(2 * it + 1 < nfar, 0.0, NEG)
        s = jnp.concatenate([s[:, :blk], s[:, blk:] + tail], axis=1)
        m_new = jnp.maximum(m_i, jnp.max(s, axis=1, keepdims=True))
        alpha = jnp.exp(m_i - m_new)
        p = jnp.exp(s - m_new)
        l_new = alpha * l_i + jnp.sum(p, axis=1, keepdims=True)
        acc_new = alpha * acc + jnp.dot(p.astype(BF16), vb_ref[pl.ds(k0, rows2), :], preferred_element_type=F32)
        return m_new, l_new, acc_new

    m_i, l_i, acc = lax.fori_loop(0, (jnp.maximum(nfar, 0) + 1) // 2, body, (m_i, l_i, acc))
    out = acc / l_i
    o_ref[0] = jnp.where(first, out[:blk], out[blk:]).astype(o_ref.dtype)


def moba_attention(p3d, rel_bias):
    bsz, seq, _ = p3d.shape
    blk = MOBA_BLOCK
    n_blocks = seq // blk
    assert n_blocks <= MOBA_LO and seq % blk == 0
    span = 2 * blk
    by_dist = rel_bias[:, _rel_bucket(jnp.arange(span))].astype(F32)
    shift = jnp.arange(span)

    def toeplitz(c):
        k = jnp.where(shift < blk, shift, shift - span)
        s = by_dist[:, jnp.clip(c - k, 0, span - 1)]
        tiled = jnp.tile(s, (1, blk))[:, :blk * (span - 1)]
        return tiled.reshape(HEADS, blk, span - 1)[:, :, :blk]

    bias_own = toeplitz(0)
    bias_prev = toeplitz(blk)
    bias_far = jnp.broadcast_to(rel_bias[:, REL_BUCKETS - 1].astype(F32)[:, None, None], (HEADS, 8, LANES))
    kern = functools.partial(_moba_kernel, n_blocks=n_blocks)
    return pl.pallas_call(
        kern,
        grid=(bsz, PAIRS, n_blocks),
        in_specs=[
            pl.BlockSpec((1, blk, LANES), lambda b, h, i: (b, i, h)),
            pl.BlockSpec((1, seq, LANES), lambda b, h, i: (b, 0, PAIRS + h)),
            pl.BlockSpec((1, seq, LANES), lambda b, h, i: (b, 0, 2 * PAIRS + h)),
            pl.BlockSpec((2, blk, blk), lambda b, h, i: (h, 0, 0)),
            pl.BlockSpec((2, blk, blk), lambda b, h, i: (h, 0, 0)),
            pl.BlockSpec((2, 8, LANES), lambda b, h, i: (h, 0, 0)),
        ],
        out_specs=pl.BlockSpec((1, blk, LANES), lambda b, h, i: (b, i, h)),
        out_shape=jax.ShapeDtypeStruct((bsz, seq, WIDTH), BF16),
        scratch_shapes=[
            pltpu.VMEM((seq, 2 * LANES), BF16),
            pltpu.VMEM((seq, LANES), BF16),
            pltpu.VMEM((LANES, LANES), F32),
        ],
        compiler_params=_cparams(("parallel", "parallel", "arbitrary")),
        name="moba",
    )(p3d, p3d, p3d, bias_own, bias_prev, bias_far)


def _shifted(x, carry_row):
    rows = lax.broadcasted_iota(jnp.int32, x.shape, 0)
    return jnp.where(rows == 0, carry_row, pltpu.roll(x, 1, axis=0))


def _rwkv_prep_kernel(pr_ref, pk_ref, pv_ref, pl_ref, mu_ref, vec_ref, ww_ref, wa_ref, wg_ref,
                      bd_ref, tri_ref,
                      rt_ref, kt_ref, kd_ref, bd_out_ref, v_ref, g_ref, bonus_ref, pend_ref,
                      carry_ref, *, chunk):
    @pl.when(pl.program_id(1) == 0)
    def _():
        carry_ref[...] = jnp.zeros_like(carry_ref)

    def mix(ref, j):
        x = ref[0]
        mu = mu_ref[0:1, j * WIDTH:(j + 1) * WIDTH]
        prev = _shifted(x, carry_ref[0:1, j * WIDTH:(j + 1) * WIDTH])
        carry_ref[0:1, j * WIDTH:(j + 1) * WIDTH] = x[x.shape[0] - 1:, :]
        return x + mu * (prev - x)

    r = mix(pr_ref, 0)
    k = mix(pk_ref, 1)
    v = mix(pv_ref, 2)
    lo = mix(pl_ref, 3)
    w0, a0, k_k, k_a, r_k = (vec_ref[i:i + 1, :] for i in range(5))
    xwa = lo[:, 0:LANES]
    xg = lo[:, LANES:3 * LANES]
    lw = jnp.dot(jnp.tanh(xwa), ww_ref[...], precision=HI, preferred_element_type=F32)
    la = jnp.dot(xwa, wa_ref[...], precision=HI, preferred_element_type=F32)
    g = jnp.dot(jax.nn.sigmoid(xg), wg_ref[...], precision=HI, preferred_element_type=F32)
    z = -(w0 + lw)
    softplus = jnp.maximum(z, 0.0) + jnp.log(1.0 + jnp.exp(-jnp.abs(z)))
    logw = -jnp.exp(-softplus - 0.5)
    a = jax.nn.sigmoid(a0 + la)
    kk = k * k_k
    ss = jnp.dot(kk * kk, bd_ref[...], precision=HI, preferred_element_type=F32)
    kk = kk / jnp.maximum(jnp.sqrt(ss), 1e-12)
    k2 = k * (1.0 + (a - 1.0) * k_a)
    rk = jnp.dot(r * k2 * r_k, bd_ref[...], precision=HI, preferred_element_type=F32)
    cs = jnp.dot(tri_ref[...], logw, precision=HI, preferred_element_type=F32)
    e_pos = jnp.exp(cs)
    e_neg = jnp.exp(-cs)
    rt_ref[0] = (r * e_pos).astype(rt_ref.dtype)
    kt_ref[0] = (kk * jnp.exp(cs - logw)).astype(kt_ref.dtype)
    kd_ref[0] = (k2 * e_neg).astype(kd_ref.dtype)
    bd_out_ref[0] = (kk * a * e_neg).astype(bd_out_ref.dtype)
    v_ref[0] = v.astype(v_ref.dtype)
    g_ref[0] = g
    bonus_ref[0] = rk * v
    ts = e_pos.shape[0]
    for c in range(ts // chunk):
        pend_ref[0, c:c + 1, :] = e_pos[(c + 1) * chunk - 1:(c + 1) * chunk, :]


def rwkv_prep(p3d, rwkv_mu, w0, w_lora_up, a0, a_lora_up, g_lora_up, k_k, k_a, r_k, *, ts=512):
    bsz, seq, _ = p3d.shape
    chunk = RWKV_CHUNK
    ts = min(ts, seq)
    mu = jnp.pad(rwkv_mu, (0, COL_B - COL_B_RAW)).reshape(1, COL_B)
    vec = jnp.stack([w0, a0, k_k, k_a, r_k.reshape(-1)] + [jnp.zeros_like(w0)] * 3).astype(F32)
    ww = jnp.zeros((LANES, WIDTH), F32).at[:DECAY_LORA].set(w_lora_up)
    wa = jnp.zeros((LANES, WIDTH), F32).at[DECAY_LORA:DECAY_LORA + AAA_LORA].set(a_lora_up)
    wg = jnp.zeros((2 * LANES, WIDTH), F32).at[:GATE_LORA].set(g_lora_up)
    hid = jnp.arange(WIDTH) // HEAD_DIM
    bd = (hid[:, None] == hid[None, :]).astype(F32)
    tix = jnp.arange(ts)
    tri = ((tix[:, None] // chunk == tix[None, :] // chunk) & (tix[None, :] <= tix[:, None])).astype(F32)
    c0 = 0
    big = jax.ShapeDtypeStruct((bsz, seq, WIDTH), F32)
    wspec = lambda shape: pl.BlockSpec(shape, lambda b, i: (0, 0))
    ospec = pl.BlockSpec((1, ts, WIDTH), lambda b, i: (b, i, 0))
    return pl.pallas_call(
        functools.partial(_rwkv_prep_kernel, chunk=chunk),
        grid=(bsz, seq // ts),
        in_specs=[
            pl.BlockSpec((1, ts, WIDTH), lambda b, i: (b, i, c0)),
            pl.BlockSpec((1, ts, WIDTH), lambda b, i: (b, i, c0 + 1)),
            pl.BlockSpec((1, ts, WIDTH), lambda b, i: (b, i, c0 + 2)),
            pl.BlockSpec((1, ts, WIDTH), lambda b, i: (b, i, c0 + 3)),
            wspec((1, COL_B)), wspec((8, WIDTH)), wspec((LANES, WIDTH)), wspec((LANES, WIDTH)),
            wspec((2 * LANES, WIDTH)), wspec((WIDTH, WIDTH)), wspec((ts, ts)),
        ],
        out_specs=[ospec] * 7 + [pl.BlockSpec((1, ts // chunk, WIDTH), lambda b, i: (b, i, 0))],
        out_shape=[jax.ShapeDtypeStruct((bsz, seq, WIDTH), BF16)] * 5 + [big] * 2
        + [jax.ShapeDtypeStruct((bsz, seq // chunk, WIDTH), F32)],
        scratch_shapes=[pltpu.VMEM((8, COL_B), F32)],
        compiler_params=_cparams(("parallel", "arbitrary")),
        name="rwkv_prep",
    )(p3d, p3d, p3d, p3d, mu, vec, ww, wa, wg, bd, tri)


def _rwkv_scan_kernel(rt_ref, kt_ref, kd_ref, bd_ref, v_ref, g_ref, bonus_ref, pend_ref, ln_ref, o_ref,
                      state_ref, *, chunk, prec):
    @pl.when(pl.program_id(1) == 0)
    def _():
        state_ref[...] = jnp.zeros_like(state_ref)

    c2 = 2 * chunk
    lane = lax.broadcasted_iota(jnp.int32, (chunk, LANES), 1)
    first = lane < HEAD_DIM
    row = lax.broadcasted_iota(jnp.int32, (c2, c2), 0)
    col = lax.broadcasted_iota(jnp.int32, (c2, c2), 1)
    eye = (row == col).astype(F32)
    hrow = lax.broadcasted_iota(jnp.int32, (LANES, LANES), 0) // HEAD_DIM
    hcol = lax.broadcasted_iota(jnp.int32, (LANES, LANES), 1) // HEAD_DIM
    head_mean = jnp.where(hrow == hcol, 1.0 / HEAD_DIM, 0.0).astype(F32)
    nt = (((1,), (1,)), ((), ()))
    tn = (((0,), (0,)), ((), ()))
    dot = functools.partial(jnp.dot, precision=prec, preferred_element_type=F32)
    dotg = functools.partial(lax.dot_general, precision=prec, preferred_element_type=F32)

    def stack(x):
        return jnp.concatenate([jnp.where(first, x, 0.0), jnp.where(first, 0.0, x)], axis=0)

    pairs = range(PAIRS)
    sls = [slice(hp * LANES, (hp + 1) * LANES) for hp in pairs]
    rs, ks, kds, bs, vs = ([stack(ref[0, :, sl].astype(F32)) for sl in sls]
                           for ref in (rt_ref, kt_ref, kd_ref, bd_ref, v_ref))
    hts = [state_ref[0, hp] for hp in pairs]
    big = [dotg(jnp.concatenate([ks[hp], rs[hp]], axis=0), jnp.concatenate([bs[hp], kds[hp]], axis=0), nt)
           for hp in pairs]
    a_b = [jnp.where(row > col, big[hp][0:c2, 0:c2], 0.0) for hp in pairs]
    a_k = [jnp.where(row > col, big[hp][0:c2, c2:], 0.0) for hp in pairs]
    a_rb = [jnp.where(row >= col, big[hp][c2:, 0:c2], 0.0) for hp in pairs]
    a_rk = [jnp.where(row >= col, big[hp][c2:, c2:], 0.0) for hp in pairs]
    kh = [dotg(jnp.concatenate([ks[hp], rs[hp]], axis=0), hts[hp], nt) for hp in pairs]
    av = [dot(jnp.concatenate([a_k[hp], a_rk[hp]], axis=0), vs[hp]) for hp in pairs]
    vk = [dotg(vs[hp], kds[hp], tn) for hp in pairs]
    inv = [eye - a_b[hp] for hp in pairs]
    pw = [dot(a_b[hp], a_b[hp]) for hp in pairs]
    n_sq = int(math.log2(chunk)) - 1
    for lvl in range(n_sq):
        if lvl + 1 < n_sq:
            both = [dot(jnp.concatenate([inv[hp], pw[hp]], axis=0), pw[hp]) for hp in pairs]
            inv = [inv[hp] + both[hp][0:c2] for hp in pairs]
            pw = [both[hp][c2:] for hp in pairs]
        else:
            inv = [inv[hp] + dot(inv[hp], pw[hp]) for hp in pairs]
    us = [dot(inv[hp], kh[hp][0:c2] + av[hp][0:c2]) for hp in pairs]
    ub = [dotg(us[hp], bs[hp], tn) for hp in pairs]
    au = [dot(a_rb[hp], us[hp]) for hp in pairs]
    for hp in pairs:
        sl = sls[hp]
        pend = pend_ref[0, 0, 0:1, sl]
        state_ref[0, hp] = (hts[hp] + vk[hp] - ub[hp]) * pend
        os_ = kh[hp][c2:] + av[hp][c2:] - au[hp]
        o = os_[0:chunk] + os_[chunk:]
        mu = jnp.dot(o, head_mean, precision=HI, preferred_element_type=F32)
        d = o - mu
        var = jnp.dot(d * d, head_mean, precision=HI, preferred_element_type=F32)
        on = d * lax.rsqrt(var + GN_EPS) * ln_ref[0:1, sl] + ln_ref[1:2, sl]
        o_ref[0, :, sl] = ((on + bonus_ref[0, :, sl]) * g_ref[0, :, sl]).astype(o_ref.dtype)


def rwkv_scan(rt, kt, kd, bd, v, g, bonus, pend, lnx_g, lnx_b, *, prec=None):
    bsz, seq, _ = rt.shape
    chunk = RWKV_CHUNK
    n_chunks = seq // chunk
    ln = jnp.stack([lnx_g, lnx_b] + [jnp.zeros_like(lnx_g)] * 6).astype(F32)
    pend4 = pend.reshape(bsz, n_chunks, 1, WIDTH)
    spec = pl.BlockSpec((1, chunk, WIDTH), lambda b, c: (b, c, 0))
    return pl.pallas_call(
        functools.partial(_rwkv_scan_kernel, chunk=chunk, prec=prec),
        grid=(bsz, n_chunks),
        in_specs=[spec] * 7 + [
            pl.BlockSpec((1, 1, 1, WIDTH), lambda b, c: (b, c, 0, 0)),
            pl.BlockSpec((8, WIDTH), lambda b, c: (0, 0)),
        ],
        out_specs=spec,
        out_shape=jax.ShapeDtypeStruct((bsz, seq, WIDTH), BF16),
        scratch_shapes=[pltpu.VMEM((1, PAIRS, LANES, LANES), F32)],
        compiler_params=_cparams(("parallel", "arbitrary")),
        name="rwkv_scan",
    )(rt, kt, kd, bd, v, g, bonus, pend4, ln)


def _merge_kernel(x_ref, oa_ref, ob_ref, ga_ref, gb_ref, wa_ref, wb_ref, wo_ref, g2_ref,
                  h_ref, xn_ref, acc_ref):
    j = pl.program_id(1)

    @pl.when(j == 0)
    def _():
        acc_ref[...] = x_ref[...]

    ya = jnp.dot(oa_ref[...].astype(BF16), wa_ref[...], preferred_element_type=F32)
    yb = jnp.dot(ob_ref[...].astype(BF16), wb_ref[...], preferred_element_type=F32)
    y = jax.nn.sigmoid(ga_ref[...].astype(F32)) * ya + jax.nn.sigmoid(gb_ref[...].astype(F32)) * yb
    acc_ref[...] += jnp.dot(y.astype(BF16), wo_ref[...], preferred_element_type=F32)

    @pl.when(j == pl.num_programs(1) - 1)
    def _():
        h = acc_ref[...]
        h_ref[...] = h
        ms = jnp.mean(h * h, axis=-1, keepdims=True)
        xn_ref[...] = _pack_halves(h * lax.rsqrt(ms + RMS_EPS) * g2_ref[...])


def _pack_halves(x):
    half = x.shape[1] // 2
    lo = lax.bitcast_convert_type(x[:, :half].astype(BF16).astype(F32), jnp.int32)
    hi = lax.bitcast_convert_type(x[:, half:].astype(BF16).astype(F32), jnp.int32)
    return lax.bitwise_or(lax.shift_right_logical(lo, jnp.int32(16)), hi)


def _unpack_halves(words):
    lo, hi = _unpack_words(words)
    return jnp.concatenate([lo, hi], axis=1)


def merge_out(x2d, oa, ob, p2d, w_proj_a, w_proj_b, w_out, norm2_g, *, row0=0, tm=512):
    t, d = oa.shape[0], x2d.shape[1]
    r0 = row0 // tm
    tn = WIDTH
    nj = d // tn
    g0 = 0
    return pl.pallas_call(
        _merge_kernel,
        grid=(t // tm, nj),
        in_specs=[
            pl.BlockSpec((tm, d), lambda i, j: (r0 + i, 0)),
            pl.BlockSpec((tm, WIDTH), lambda i, j: (i, 0)),
            pl.BlockSpec((tm, WIDTH), lambda i, j: (i, 0)),
            pl.BlockSpec((tm, tn), lambda i, j: (i, g0 + j)),
            pl.BlockSpec((tm, tn), lambda i, j: (i, g0 + nj + j)),
            pl.BlockSpec((WIDTH, tn), lambda i, j: (0, j)),
            pl.BlockSpec((WIDTH, tn), lambda i, j: (0, j)),
            pl.BlockSpec((tn, d), lambda i, j: (j, 0)),
            pl.BlockSpec((1, d), lambda i, j: (0, 0)),
        ],
        out_specs=[pl.BlockSpec((tm, d), lambda i, j: (i, 0)), pl.BlockSpec((tm, d // 2), lambda i, j: (i, 0))],
        out_shape=[jax.ShapeDtypeStruct((t, d), F32), jax.ShapeDtypeStruct((t, d // 2), jnp.int32)],
        scratch_shapes=[pltpu.VMEM((tm, d), F32)],
        compiler_params=_cparams(("parallel", "arbitrary")),
        name="merge_out",
    )(x2d, oa, ob, p2d, p2d, w_proj_a.astype(BF16), w_proj_b.astype(BF16), w_out.astype(BF16),
      norm2_g.reshape(1, d))


PEER_HEADS = 8
PEER_NKEYS = 128
PEER_TOPK = 16
PEER_HALF = 128


def _topk_rows(s, k):
    n = s.shape[0]
    rows = lax.broadcasted_iota(jnp.int32, s.shape, 0).astype(F32)
    vals, ids = [], []
    for _ in range(k):
        m = jnp.max(s, axis=0, keepdims=True)
        first = jnp.min(jnp.where(s == m, rows, float(n)), axis=0, keepdims=True)
        vals.append(m)
        ids.append(first)
        s = jnp.where(rows == first, -jnp.inf, s)
    return jnp.concatenate(vals, axis=0), jnp.concatenate(ids, axis=0)


def _take_rows(table, ids):
    rows = lax.broadcasted_iota(jnp.int32, table.shape, 0).astype(F32)
    return jnp.sum(jnp.where(rows == ids, table, 0.0), axis=0, keepdims=True)


def _peer_route_kernel(xn_ref, wq_ref, sk_ref, idx_ref, gate_ref, *, prec):
    tt = xn_ref.shape[0]
    k = PEER_TOPK
    xn = _unpack_halves(xn_ref[...]) if xn_ref.dtype == jnp.int32 else xn_ref[...]
    q = jnp.dot(xn.astype(wq_ref.dtype), wq_ref[...], precision=prec, preferred_element_type=F32)
    nt = (((1,), (1,)), ((), ()))
    idx_rows, gate_rows = [], []
    half = k // 2
    for h in range(PEER_HEADS):
        tops = []
        for p in range(2):
            c0 = (h * 2 + p) * PEER_HALF
            s = lax.dot_general(sk_ref[h, p].astype(wq_ref.dtype), q[:, c0:c0 + PEER_HALF].astype(wq_ref.dtype),
                                nt, precision=prec, preferred_element_type=F32)
            tops.append(_topk_rows(s, k))
        (s0, i0), (s1, i1) = tops
        cs = [s0[0:1] + s1] + [s0[i:i + 1] + s1[0:half] for i in range(1, half)] + [s0[half:] + s1[0:1]]
        best_s, pos = _topk_rows(jnp.concatenate(cs, axis=0), k)
        mid = jnp.floor((pos - k) * (1.0 / half))
        end_mid = float(k + (half - 1) * half)
        i_rank = jnp.where(pos < k, 0.0, jnp.where(pos < end_mid, 1.0 + mid, pos - (end_mid - half)))
        j_rank = jnp.where(pos < k, pos, jnp.where(pos < end_mid, (pos - k) - half * mid, 0.0))
        ids = [_take_rows(i0, i_rank[n:n + 1]) * PEER_NKEYS + _take_rows(i1, j_rank[n:n + 1]) for n in range(k)]
        e = jnp.exp(best_s - best_s[0:1])
        gate_rows.append(e / jnp.sum(e, axis=0, keepdims=True))
        idx_rows.append(jnp.concatenate(ids, axis=0).astype(jnp.int32))
    idx_ref[...] = jnp.concatenate(idx_rows, axis=0).T
    gate_ref[...] = jnp.concatenate(gate_rows, axis=0).T


def peer_route(xn2d, peer_wq, peer_subkeys, *, tt=256, prec=None, wdtype=BF16):
    t, dx = xn2d.shape
    d, nq = peer_wq.shape
    n_sel = PEER_HEADS * PEER_TOPK
    return pl.pallas_call(
        functools.partial(_peer_route_kernel, prec=prec),
        grid=(t // tt,),
        in_specs=[
            pl.BlockSpec((tt, dx), lambda i: (i, 0)),
            pl.BlockSpec((d, nq), lambda i: (0, 0)),
            pl.BlockSpec((PEER_HEADS, 2, PEER_NKEYS, PEER_HALF), lambda i: (0, 0, 0, 0)),
        ],
        out_specs=[pl.BlockSpec((tt, n_sel), lambda i: (i, 0))] * 2,
        out_shape=[jax.ShapeDtypeStruct((t, n_sel), jnp.int32), jax.ShapeDtypeStruct((t, n_sel), F32)],
        compiler_params=_cparams(("parallel",)),
        name="peer_route",
    )(xn2d, peer_wq.astype(wdtype), peer_subkeys)


def _final_kernel(h_ref, y_ref, g_ref, *rest):
    o_ref = rest[-1]
    h = h_ref[...] + y_ref[...]
    ms = jnp.mean(h * h, axis=-1, keepdims=True)
    o_ref[...] = h * lax.rsqrt(ms + RMS_EPS) * g_ref[...]


def final_norm(h2d, y2d, g, *, out=None, row0=0, total_rows=None, tm=1024):
    t, d = h2d.shape
    total = t if total_rows is None else total_rows
    r0 = row0 // tm
    spec = pl.BlockSpec((tm, d), lambda i: (i, 0))
    in_specs = [spec, spec, pl.BlockSpec((1, d), lambda i: (0, 0))]
    args = [h2d, y2d, g.reshape(1, d)]
    aliases = {}
    if out is not None:
        in_specs.append(pl.BlockSpec(memory_space=pl.ANY))
        args.append(out)
        aliases = {3: 0}
    return pl.pallas_call(
        _final_kernel,
        grid=(t // tm,),
        in_specs=in_specs,
        out_specs=pl.BlockSpec((tm, d), lambda i: (r0 + i, 0)),
        out_shape=jax.ShapeDtypeStruct((total, d), F32),
        input_output_aliases=aliases,
        compiler_params=_cparams(("parallel",)),
        name="final_norm",
    )(*args)


SC_CORES = 2
SC_SUBCORES = 16
SC_LANES = 16
SC_WORKERS = SC_CORES * SC_SUBCORES
PEER_SEL = PEER_HEADS * PEER_TOPK
PEER_ROWS = 32
PEER_PARTS = PEER_SEL // PEER_ROWS
PEER_NBUF = 4
PEER_GROUP = 32
PEER_BF16_RUN = 4


def _pack_rows_kernel(w_ref, o_ref):
    o_ref[...] = _pack_halves(w_ref[...])


def _pack_rows(w, *, tr=1024):
    e, d = w.shape
    return pl.pallas_call(
        _pack_rows_kernel,
        grid=(e // tr,),
        in_specs=[pl.BlockSpec((tr, d), lambda i: (i, 0))],
        out_specs=pl.BlockSpec((tr, d // 2), lambda i: (i, 0)),
        out_shape=jax.ShapeDtypeStruct((e, d // 2), jnp.int32),
        compiler_params=_cparams(("parallel",)),
        name="pack_rows",
    )(w)


def _unpack_words(w):
    lo = lax.bitcast_convert_type(lax.shift_left(w, jnp.int32(16)), F32)
    hi = lax.bitcast_convert_type(lax.bitwise_and(w, jnp.int32(-65536)), F32)
    return lo, hi


def _packed_dot(a_words, b_words):
    from jax.experimental.pallas import tpu_sc as plsc
    prods = [plsc.bitcast(a, BF16) * plsc.bitcast(b, BF16) for a, b in zip(a_words, b_words)]
    while len(prods) > 1:
        prods = [prods[k] + prods[k + 1] for k in range(0, len(prods), 2)]
    return _unpack_words(plsc.bitcast(prods[0], jnp.int32))


def _sc_mesh():
    from jax.experimental.pallas import tpu_sc as plsc
    return plsc.VectorSubcoreMesh(core_axis_name="c", subcore_axis_name="s",
                                  num_cores=SC_CORES, num_subcores=SC_SUBCORES)


def _sc_loop(n, body, carry):
    from jax.experimental.pallas import tpu_sc as plsc
    return plsc.parallel_loop(0, n, carry=carry)(body)


def _worker_base(tokens_per_worker):
    return (lax.axis_index("s") * SC_CORES + lax.axis_index("c")) * tokens_per_worker


def _gather_compute_loop(table_hbm, idx_v, rows_v, sem, stage_v, out_row, osem, grp, compute):
    n_gathers = PEER_PARTS * grp
    ahead = PEER_NBUF - 1

    def gather(j, b):
        i = j // PEER_PARTS if isinstance(j, int) else lax.shift_right_logical(j, PEER_PARTS.bit_length() - 1)
        h = j % PEER_PARTS if isinstance(j, int) else lax.bitwise_and(j, PEER_PARTS - 1)
        ids = idx_v.at[i, pl.ds(pl.multiple_of(h * PEER_ROWS, PEER_ROWS), PEER_ROWS)]
        return pltpu.make_async_copy(table_hbm.at[ids], rows_v.at[b], sem.at[b])

    def put(i, slot):
        return pltpu.make_async_copy(stage_v.at[slot], out_row(i), osem.at[slot])

    for j in range(ahead):
        gather(j, j).start()

    @pl.loop(0, n_gathers)
    def _(j):
        b = lax.bitwise_and(j, PEER_NBUF - 1)
        h = lax.bitwise_and(j, PEER_PARTS - 1)
        i = lax.shift_right_logical(j, PEER_PARTS.bit_length() - 1)
        slot = lax.bitwise_and(i, 1)

        @pl.when((h == 0) & (i >= 2))
        def _():
            put(i - 2, slot).wait()

        @pl.when(j + ahead < n_gathers)
        def _():
            gather(j + ahead, lax.bitwise_and(j + ahead, PEER_NBUF - 1)).start()

        gather(j, b).wait()
        compute(i, h, b, slot)

        @pl.when(h == PEER_PARTS - 1)
        def _():
            put(i, slot).start()

    put(grp - 2, 0).wait()
    put(grp - 1, 1).wait()


def peer_expert_dots(x_packed, idx, u_packed):
    t, half = x_packed.shape
    n_chunks = half // SC_LANES
    tpw = t // SC_WORKERS
    grp = min(PEER_GROUP, tpw)
    rows_tog = 8

    def body(x_hbm, idx_hbm, u_hbm, out_hbm, idx_v, x_v, rows_v, ps_v, sem, osem):
        base = _worker_base(tpw)

        def compute(i, h, b, slot):
            @pl.loop(0, PEER_ROWS // rows_tog)
            def _(rg):
                r0 = rg * rows_tog
                accs = [[None, None] for _ in range(rows_tog)]
                for c0 in range(0, n_chunks, PEER_BF16_RUN):
                    ats = [pl.ds((c0 + k) * SC_LANES, SC_LANES) for k in range(PEER_BF16_RUN)]
                    xw = [x_v[i, at] for at in ats]
                    for r in range(rows_tog):
                        terms = _packed_dot([rows_v[b, r0 + r, at] for at in ats], xw)
                        for k, term in enumerate(terms):
                            accs[r][k] = term if accs[r][k] is None else accs[r][k] + term
                for r in range(rows_tog):
                    at = pl.ds(pl.multiple_of((h * PEER_ROWS + r0 + r) * SC_LANES, SC_LANES), SC_LANES)
                    ps_v[slot, at] = accs[r][0] + accs[r][1]

        @pl.loop(0, tpw // grp)
        def _(g):
            t0 = base + g * grp
            pltpu.sync_copy(idx_hbm.at[pl.ds(t0, grp)], idx_v)
            pltpu.sync_copy(x_hbm.at[pl.ds(t0, grp)], x_v)
            _gather_compute_loop(u_hbm, idx_v, rows_v, sem, ps_v, lambda i: out_hbm.at[t0 + i], osem, grp, compute)

    return pl.kernel(
        body,
        out_type=jax.ShapeDtypeStruct((t, PEER_SEL * SC_LANES), F32),
        mesh=_sc_mesh(),
        scratch_types=[
            pltpu.VMEM((grp, PEER_SEL), jnp.int32),
            pltpu.VMEM((grp, half), jnp.int32),
            pltpu.VMEM((PEER_NBUF, PEER_ROWS, half), jnp.int32),
            pltpu.VMEM((2, PEER_SEL * SC_LANES), F32),
            pltpu.SemaphoreType.DMA((PEER_NBUF,)),
            pltpu.SemaphoreType.DMA((2,)),
        ],
        compiler_params=pltpu.CompilerParams(needs_layout_passes=False),
        name="peer_expert_dots",
    )(x_packed, idx, u_packed)


def peer_expert_mix(hgw, idx, v_packed):
    t = hgw.shape[0]
    half = v_packed.shape[1]
    d = 2 * half
    tpw = t // SC_WORKERS
    grp = min(PEER_GROUP, tpw)
    n_parts = 2
    cpp = half // SC_LANES // n_parts
    from jax.experimental.pallas import tpu_sc as plsc

    def body(hg_hbm, idx_hbm, v_hbm, out_hbm, idx_v, hg_v, rows_v, o_v2, sem, osem):
        base = _worker_base(tpw)

        def compute(i, h, b, slot):
            token = jnp.full((SC_LANES,), i, jnp.int32)
            for part in range(n_parts):
                def rbody(rq, accs):
                    r0 = rq * PEER_BF16_RUN
                    s = [plsc.load_gather(hg_v, [token, jnp.full((SC_LANES,), h * PEER_ROWS + r0 + k, jnp.int32)])
                         for k in range(PEER_BF16_RUN)]
                    new = []
                    for c in range(cpp):
                        at = pl.ds((part * cpp + c) * SC_LANES, SC_LANES)
                        lo, hi = _packed_dot([rows_v[b, r0 + k, at] for k in range(PEER_BF16_RUN)], s)
                        new.append(accs[2 * c] + lo)
                        new.append(accs[2 * c + 1] + hi)
                    return tuple(new)

                accs = _sc_loop(PEER_ROWS // PEER_BF16_RUN, rbody,
                                tuple(jnp.zeros((SC_LANES,), F32) for _ in range(2 * cpp)))
                def store(overwrite):
                    for c in range(cpp):
                        lo_at = pl.ds((part * cpp + c) * SC_LANES, SC_LANES)
                        hi_at = pl.ds(half + (part * cpp + c) * SC_LANES, SC_LANES)
                        if overwrite:
                            o_v2[slot, lo_at] = accs[2 * c]
                            o_v2[slot, hi_at] = accs[2 * c + 1]
                        else:
                            o_v2[slot, lo_at] = o_v2[slot, lo_at] + accs[2 * c]
                            o_v2[slot, hi_at] = o_v2[slot, hi_at] + accs[2 * c + 1]

                pl.when(h == 0)(functools.partial(store, True))
                pl.when(h != 0)(functools.partial(store, False))

        @pl.loop(0, tpw // grp)
        def _(g):
            t0 = base + g * grp
            pltpu.sync_copy(idx_hbm.at[pl.ds(t0, grp)], idx_v)
            pltpu.sync_copy(hg_hbm.at[pl.ds(t0, grp)], hg_v)
            _gather_compute_loop(v_hbm, idx_v, rows_v, sem, o_v2, lambda i: out_hbm.at[t0 + i], osem, grp, compute)

    return pl.kernel(
        body,
        out_type=jax.ShapeDtypeStruct((t, d), F32),
        mesh=_sc_mesh(),
        scratch_types=[
            pltpu.VMEM((grp, PEER_SEL), jnp.int32),
            pltpu.VMEM((grp, PEER_SEL), jnp.int32),
            pltpu.VMEM((PEER_NBUF, PEER_ROWS, half), jnp.int32),
            pltpu.VMEM((2, d), F32),
            pltpu.SemaphoreType.DMA((PEER_NBUF,)),
            pltpu.SemaphoreType.DMA((2,)),
        ],
        compiler_params=pltpu.CompilerParams(needs_layout_passes=False),
        name="peer_expert_mix",
    )(hgw, idx, v_packed)


def _peer_act_kernel(ps_ref, gate_ref, sum_ref, o_ref):
    ps = ps_ref[...]
    sel = sum_ref[...]
    hi = ps.astype(BF16)
    rest = ps - hi.astype(F32)
    mid = rest.astype(BF16)
    lo = (rest - mid.astype(F32)).astype(BF16)
    pre = (jnp.dot(hi, sel, preferred_element_type=F32) + jnp.dot(mid, sel, preferred_element_type=F32)
           + jnp.dot(lo, sel, preferred_element_type=F32))
    hg = 0.5 * pre * (1.0 + lax.erf(pre * (1.0 / math.sqrt(2.0)))) * gate_ref[...]
    bits = lax.bitcast_convert_type(hg.astype(BF16).astype(F32), jnp.int32)
    o_ref[...] = lax.bitwise_or(bits, lax.shift_right_logical(bits, jnp.int32(16)))


def peer_act(ps, gates, *, tm=512):
    t, n = ps.shape
    lane_sum = (jnp.arange(n)[:, None] // SC_LANES == jnp.arange(PEER_SEL)[None, :]).astype(BF16)
    return pl.pallas_call(
        _peer_act_kernel,
        grid=(t // tm,),
        in_specs=[
            pl.BlockSpec((tm, n), lambda i: (i, 0)),
            pl.BlockSpec((tm, PEER_SEL), lambda i: (i, 0)),
            pl.BlockSpec((n, PEER_SEL), lambda i: (0, 0)),
        ],
        out_specs=pl.BlockSpec((tm, PEER_SEL), lambda i: (i, 0)),
        out_shape=jax.ShapeDtypeStruct((t, PEER_SEL), jnp.int32),
        compiler_params=_cparams(("parallel",)),
        name="peer_act",
    )(ps, gates, lane_sum)


BATCH_GROUPS = 8


def kernel(x, norm1_g, w_in, rwkv_mu, w0, w_lora_up, a0, a_lora_up, g_lora_up, k_k, k_a, r_k, lnx_g, lnx_b,
           w_proj_a, w_proj_b, w_out, norm2_g, peer_wq, peer_subkeys, peer_u, peer_v, rel_bias, normf_g):
    bsz, seq, d = x.shape
    depth = norm1_g.shape[0]
    groups = BATCH_GROUPS if bsz % BATCH_GROUPS == 0 else 1
    gb = bsz // groups
    tg = gb * seq
    t = bsz * seq
    src = x.reshape(t, d)
    for l in range(depth):
        w_pad = jnp.concatenate([
            w_in[l][:, :COL_A + COL_B_RAW],
            jnp.zeros((d, COL_B - COL_B_RAW), w_in.dtype),
            w_in[l][:, COL_A + COL_B_RAW:]], axis=1).astype(BF16)
        u_packed = _pack_rows(peer_u[l])
        v_packed = _pack_rows(peer_v[l])
        last = l == depth - 1

        def mix(pending, tie=None):
            row0, h2d, ps, gates, idx = pending
            hgw = peer_act(ps, gates)
            if tie is not None:
                tie, hgw = lax.optimization_barrier((tie, hgw))
            return tie, (row0, h2d, peer_expert_mix(hgw, idx, v_packed))

        outs = []

        def close(mixed):
            row0, h2d, y2d = mixed
            if last:
                outs.append(final_norm(h2d, y2d, normf_g, out=outs[-1] if outs else None, row0=row0, total_rows=t))
            else:
                outs.append(h2d + y2d)

        pending = closing = None
        for g in range(groups):
            pa, pb, pg = norm_proj(src, norm1_g[l], w_pad, row0=g * tg, rows=tg)
            oa = moba_attention(pa.reshape(gb, seq, -1), rel_bias)
            prep = tuple(rwkv_prep(pb.reshape(gb, seq, -1), rwkv_mu[l], w0[l], w_lora_up[l], a0[l], a_lora_up[l], g_lora_up[l],
                                   k_k[l], k_a[l], r_k[l]))
            mixed = None
            if pending is not None:
                (oa, prep), mixed = mix(pending, (oa, prep))
            if closing is not None:
                oa, y2d = lax.optimization_barrier((oa, closing[2]))
                close(closing[:2] + (y2d,))
                closing = None
            ob = rwkv_scan(*prep, lnx_g[l], lnx_b[l])
            h2d, xn2 = merge_out(src, oa.reshape(tg, WIDTH), ob.reshape(tg, WIDTH), pg, w_proj_a[l], w_proj_b[l],
                                 w_out[l], norm2_g[l], row0=g * tg)
            idx, gates = peer_route(xn2, peer_wq[l], peer_subkeys[l])
            if mixed is not None:
                idx, y2d = lax.optimization_barrier((idx, mixed[2]))
                closing = mixed[:2] + (y2d,)
            pending = (g * tg, h2d, peer_expert_dots(xn2, idx, u_packed), gates, idx)
        if closing is not None:
            close(closing)
        close(mix(pending)[1])
        src = outs[-1] if last else jnp.concatenate(outs, axis=0)
    return src.reshape(bsz, seq, d)
```

```python
import functools
import math

import jax
import jax.numpy as jnp
from jax import lax
from jax.experimental import pallas as pl
from jax.experimental.pallas import tpu as pltpu

F32 = jnp.float32
BF16 = jnp.bfloat16
HI = lax.Precision.HIGHEST

LANES = 128
HEAD_DIM = 64
HEADS = 8
PAIRS = HEADS // 2
WIDTH = HEADS * HEAD_DIM
MOBA_BLOCK = 256
MOBA_TOPK = 3
MOBA_LO = 64
REL_BUCKETS = 32
REL_MAX_DIST = 128
DECAY_LORA = 64
AAA_LORA = 64
GATE_LORA = 160
GN_EPS = 64e-5
RMS_EPS = 1e-6
NEG = -1e30
RWKV_CHUNK = 64
RWKV_CHUNKS_PER_STEP = 4
COL_A = 3 * WIDTH
COL_B_RAW = 3 * WIDTH + DECAY_LORA + AAA_LORA + GATE_LORA
COL_B = 4 * WIDTH
COL_G_OFF = COL_A + COL_B
VMEM_LIMIT = 56 * 1024 * 1024


def _cparams(sem):
    return pltpu.CompilerParams(dimension_semantics=sem, vmem_limit_bytes=VMEM_LIMIT)


def _norm_proj_kernel(x_ref, g_ref, w_ref, pa_ref, pb_ref, pg_ref, xn_ref, *, ja, jb):
    j = pl.program_id(1)

    @pl.when(j == 0)
    def _():
        x = x_ref[...]
        ms = jnp.mean(x * x, axis=-1, keepdims=True)
        xn_ref[...] = (x * lax.rsqrt(ms + RMS_EPS) * g_ref[...]).astype(xn_ref.dtype)

    res = jnp.dot(xn_ref[...], w_ref[...], preferred_element_type=F32)

    @pl.when(j < ja)
    def _():
        pa_ref[...] = res.astype(pa_ref.dtype)

    @pl.when((j >= ja) & (j < jb))
    def _():
        pb_ref[...] = res

    @pl.when(j >= jb)
    def _():
        pg_ref[...] = res.astype(pg_ref.dtype)


def norm_proj(x2d, g, w, *, row0=0, rows=None, tm=2048, tn=512):
    d = x2d.shape[1]
    t = x2d.shape[0] if rows is None else rows
    n = w.shape[1]
    r0 = row0 // tm
    ja, jb, jn = COL_A // tn, COL_G_OFF // tn, n // tn
    return pl.pallas_call(
        functools.partial(_norm_proj_kernel, ja=ja, jb=jb),
        grid=(t // tm, jn),
        in_specs=[
            pl.BlockSpec((tm, d), lambda i, j: (r0 + i, 0)),
            pl.BlockSpec((1, d), lambda i, j: (0, 0)),
            pl.BlockSpec((d, tn), lambda i, j: (0, j)),
        ],
        out_specs=[
            pl.BlockSpec((tm, tn), lambda i, j: (i, jnp.minimum(j, ja - 1))),
            pl.BlockSpec((tm, tn), lambda i, j: (i, jnp.clip(j - ja, 0, jb - ja - 1))),
            pl.BlockSpec((tm, tn), lambda i, j: (i, jnp.maximum(j - jb, 0))),
        ],
        out_shape=[jax.ShapeDtypeStruct((t, COL_A), BF16), jax.ShapeDtypeStruct((t, COL_B), F32),
                   jax.ShapeDtypeStruct((t, n - COL_G_OFF), BF16)],
        scratch_shapes=[pltpu.VMEM((tm, d), w.dtype)],
        compiler_params=_cparams(("parallel", "arbitrary")),
        name="norm_proj",
    )(x2d, g.reshape(1, d), w)


def _rel_bucket(dist):
    n = jnp.maximum(dist, 0)
    max_exact = REL_BUCKETS // 2
    nf = jnp.maximum(n, 1).astype(F32)
    large = max_exact + (jnp.log(nf / max_exact) / math.log(REL_MAX_DIST / max_exact)
                         * (REL_BUCKETS - max_exact)).astype(jnp.int32)
    large = jnp.minimum(large, REL_BUCKETS - 1)
    return jnp.where(n < max_exact, n, large)


def _moba_kernel(q_ref, k_ref, v_ref, bown_ref, bprev_ref, bfar_ref, o_ref,
                 kb_ref, vb_ref, kbar_ref, *, n_blocks):
    qb = pl.program_id(2)
    blk = MOBA_BLOCK
    scale = 1.0 / math.sqrt(HEAD_DIM)

    rows2 = 2 * blk
    nt = (((1,), (1,)), ((), ()))

    @pl.when(qb == 0)
    def _():
        kbar_ref[...] = jnp.zeros_like(kbar_ref)
        lane_b = lax.broadcasted_iota(jnp.int32, (blk, LANES), 1)
        for n in range(n_blocks):
            kblk = k_ref[0, n * blk:(n + 1) * blk, :]
            kbar_ref[n:n + 1, :] = jnp.mean(kblk.astype(F32), axis=0, keepdims=True)
            kb_ref[n * blk:(n + 1) * blk, 0:LANES] = kblk.astype(BF16)
            kb_ref[n * blk:(n + 1) * blk, LANES:] = ((lane_b == n) | (lane_b == MOBA_LO + n)).astype(BF16)
        vb_ref[...] = v_ref[0].astype(BF16)

    q2 = q_ref[0].astype(F32)
    first = lax.broadcasted_iota(jnp.int32, (blk, LANES), 1) < HEAD_DIM
    qh = jnp.concatenate([jnp.where(first, q2, 0.0), jnp.where(first, 0.0, q2)], axis=0)
    lane = lax.broadcasted_iota(jnp.int32, (rows2, LANES), 1)
    rowi = lax.broadcasted_iota(jnp.int32, (rows2, LANES), 0)
    gate = lax.dot_general(qh.astype(BF16), kbar_ref[...].astype(BF16), nt, preferred_element_type=F32)
    g = jnp.where(lane < qb, gate, -jnp.inf)
    chosen = lane < 0
    lane_f = lane.astype(F32)
    for _ in range(MOBA_TOPK):
        m = jnp.max(g, axis=1, keepdims=True)
        idx = jnp.min(jnp.where(g == m, lane_f, float(LANES)), axis=1, keepdims=True)
        hit = (lane_f == idx) & (m > -jnp.inf)
        chosen = chosen | hit
        g = jnp.where(hit, -jnp.inf, g)
    nfar = qb - 1
    bfar = jnp.where(rowi < blk, bfar_ref[0, 0:1, 0:1], bfar_ref[1, 0:1, 0:1])
    bhi = bfar.astype(BF16).astype(F32)
    madd = jnp.where(lane < nfar, jnp.where(chosen, bhi, NEG),
                     jnp.where(lane == nfar, jnp.where(chosen, 0.0, NEG),
                               jnp.where((lane >= MOBA_LO) & (lane - MOBA_LO < nfar), bfar - bhi, 0.0)))
    q_aug = jnp.concatenate([(qh * scale).astype(BF16), madd.astype(BF16)], axis=1)

    prev0 = pl.multiple_of(jnp.maximum(nfar, 0) * blk, blk)
    own0 = pl.multiple_of(qb * blk, blk)
    s_prev = (lax.dot_general(q_aug, kb_ref[pl.ds(prev0, blk), :], nt, preferred_element_type=F32)
              + bprev_ref[...].reshape(rows2, blk) + jnp.where(qb > 0, 0.0, NEG))
    s_own = (lax.dot_general(q_aug, kb_ref[pl.ds(own0, blk), :], nt, preferred_element_type=F32)
             + bown_ref[...].reshape(rows2, blk))
    r = lax.broadcasted_iota(jnp.int32, (rows2, blk), 0)
    c = lax.broadcasted_iota(jnp.int32, (rows2, blk), 1)
    s_own = jnp.where(lax.bitwise_and(r, blk - 1) >= c, s_own, NEG)
    s = jnp.concatenate([s_prev, s_own], axis=1)
    m_i = jnp.max(s, axis=1, keepdims=True)
    p = jnp.exp(s - m_i)
    l_i = jnp.sum(p, axis=1, keepdims=True)
    v0 = jnp.concatenate([vb_ref[pl.ds(prev0, blk), :], vb_ref[pl.ds(own0, blk), :]], axis=0)
    acc = jnp.dot(p.astype(BF16), v0, preferred_element_type=F32)

    def body(it, carry):
        m_i, l_i, acc = carry
        k0 = pl.multiple_of(it * rows2, rows2)
        s = lax.dot_general(q_aug, kb_ref[pl.ds(k0, rows2), :], nt, preferred_element_type=F32)
        tail = jnp.where(2 * it + 1 < nfar, 0.0, NEG)
        s = jnp.concatenate([s[:, :blk], s[:, blk:] + tail], axis=1)
        m_new = jnp.maximum(m_i, jnp.max(s, axis=1, keepdims=True))
        alpha = jnp.exp(m_i - m_new)
        p = jnp.exp(s - m_new)
        l_new = alpha * l_i + jnp.sum(p, axis=1, keepdims=True)
        acc_new = alpha * acc + jnp.dot(p.astype(BF16), vb_ref[pl.ds(k0, rows2), :], preferred_element_type=F32)
        return m_new, l_new, acc_new

    m_i, l_i, acc = lax.fori_loop(0, (jnp.maximum(nfar, 0) + 1) // 2, body, (m_i, l_i, acc))
    out = acc / l_i
    o_ref[0] = jnp.where(first, out[:blk], out[blk:]).astype(o_ref.dtype)


def moba_attention(p3d, rel_bias):
    bsz, seq, _ = p3d.shape
    blk = MOBA_BLOCK
    n_blocks = seq // blk
    assert n_blocks <= MOBA_LO and seq % blk == 0
    span = 2 * blk
    by_dist = rel_bias[:, _rel_bucket(jnp.arange(span))].astype(F32)
    shift = jnp.arange(span)

    def toeplitz(c):
        k = jnp.where(shift < blk, shift, shift - span)
        s = by_dist[:, jnp.clip(c - k, 0, span - 1)]
        tiled = jnp.tile(s, (1, blk))[:, :blk * (span - 1)]
        return tiled.reshape(HEADS, blk, span - 1)[:, :, :blk]

    bias_own = toeplitz(0)
    bias_prev = toeplitz(blk)
    bias_far = jnp.broadcast_to(rel_bias[:, REL_BUCKETS - 1].astype(F32)[:, None, None], (HEADS, 8, LANES))
    kern = functools.partial(_moba_kernel, n_blocks=n_blocks)
    return pl.pallas_call(
        kern,
        grid=(bsz, PAIRS, n_blocks),
        in_specs=[
            pl.BlockSpec((1, blk, LANES), lambda b, h, i: (b, i, h)),
            pl.BlockSpec((1, seq, LANES), lambda b, h, i: (b, 0, PAIRS + h)),
            pl.BlockSpec((1, seq, LANES), lambda b, h, i: (b, 0, 2 * PAIRS + h)),
            pl.BlockSpec((2, blk, blk), lambda b, h, i: (h, 0, 0)),
            pl.BlockSpec((2, blk, blk), lambda b, h, i: (h, 0, 0)),
            pl.BlockSpec((2, 8, LANES), lambda b, h, i: (h, 0, 0)),
        ],
        out_specs=pl.BlockSpec((1, blk, LANES), lambda b, h, i: (b, i, h)),
        out_shape=jax.ShapeDtypeStruct((bsz, seq, WIDTH), BF16),
        scratch_shapes=[
            pltpu.VMEM((seq, 2 * LANES), BF16),
            pltpu.VMEM((seq, LANES), BF16),
            pltpu.VMEM((LANES, LANES), F32),
        ],
        compiler_params=_cparams(("parallel", "parallel", "arbitrary")),
        name="moba",
    )(p3d, p3d, p3d, bias_own, bias_prev, bias_far)


def _shifted(x, carry_row):
    rows = lax.broadcasted_iota(jnp.int32, x.shape, 0)
    return jnp.where(rows == 0, carry_row, pltpu.roll(x, 1, axis=0))


def _rwkv_prep_kernel(pr_ref, pk_ref, pv_ref, pl_ref, mu_ref, vec_ref, ww_ref, wa_ref, wg_ref,
                      bd_ref, tri_ref,
                      rt_ref, kt_ref, kd_ref, bd_out_ref, v_ref, g_ref, bonus_ref, pend_ref,
                      carry_ref, *, chunk):
    @pl.when(pl.program_id(1) == 0)
    def _():
        carry_ref[...] = jnp.zeros_like(carry_ref)

    def mix(ref, j):
        x = ref[0]
        mu = mu_ref[0:1, j * WIDTH:(j + 1) * WIDTH]
        prev = _shifted(x, carry_ref[0:1, j * WIDTH:(j + 1) * WIDTH])
        carry_ref[0:1, j * WIDTH:(j + 1) * WIDTH] = x[x.shape[0] - 1:, :]
        return x + mu * (prev - x)

    r = mix(pr_ref, 0)
    k = mix(pk_ref, 1)
    v = mix(pv_ref, 2)
    lo = mix(pl_ref, 3)
    w0, a0, k_k, k_a, r_k = (vec_ref[i:i + 1, :] for i in range(5))
    xwa = lo[:, 0:LANES]
    xg = lo[:, LANES:3 * LANES]
    lw = jnp.dot(jnp.tanh(xwa), ww_ref[...], precision=HI, preferred_element_type=F32)
    la = jnp.dot(xwa, wa_ref[...], precision=HI, preferred_element_type=F32)
    g = jnp.dot(jax.nn.sigmoid(xg), wg_ref[...], precision=HI, preferred_element_type=F32)
    z = -(w0 + lw)
    softplus = jnp.maximum(z, 0.0) + jnp.log(1.0 + jnp.exp(-jnp.abs(z)))
    logw = -jnp.exp(-softplus - 0.5)
    a = jax.nn.sigmoid(a0 + la)
    kk = k * k_k
    ss = jnp.dot(kk * kk, bd_ref[...], precision=HI, preferred_element_type=F32)
    kk = kk / jnp.maximum(jnp.sqrt(ss), 1e-12)
    k2 = k * (1.0 + (a - 1.0) * k_a)
    rk = jnp.dot(r * k2 * r_k, bd_ref[...], precision=HI, preferred_element_type=F32)
    cs = jnp.dot(tri_ref[...], logw, precision=HI, preferred_element_type=F32)
    e_pos = jnp.exp(cs)
    e_neg = jnp.exp(-cs)
    rt_ref[0] = (r * e_pos).astype(rt_ref.dtype)
    kt_ref[0] = (kk * jnp.exp(cs - logw)).astype(kt_ref.dtype)
    kd_ref[0] = (k2 * e_neg).astype(kd_ref.dtype)
    bd_out_ref[0] = (kk * a * e_neg).astype(bd_out_ref.dtype)
    v_ref[0] = v.astype(v_ref.dtype)
    g_ref[0] = g
    bonus_ref[0] = rk * v
    ts = e_pos.shape[0]
    for c in range(ts // chunk):
        pend_ref[0, c:c + 1, :] = e_pos[(c + 1) * chunk - 1:(c + 1) * chunk, :]


def rwkv_prep(p3d, rwkv_mu, w0, w_lora_up, a0, a_lora_up, g_lora_up, k_k, k_a, r_k, *, ts=512):
    bsz, seq, _ = p3d.shape
    chunk = RWKV_CHUNK
    ts = min(ts, seq)
    mu = jnp.pad(rwkv_mu, (0, COL_B - COL_B_RAW)).reshape(1, COL_B)
    vec = jnp.stack([w0, a0, k_k, k_a, r_k.reshape(-1)] + [jnp.zeros_like(w0)] * 3).astype(F32)
    ww = jnp.zeros((LANES, WIDTH), F32).at[:DECAY_LORA].set(w_lora_up)
    wa = jnp.zeros((LANES, WIDTH), F32).at[DECAY_LORA:DECAY_LORA + AAA_LORA].set(a_lora_up)
    wg = jnp.zeros((2 * LANES, WIDTH), F32).at[:GATE_LORA].set(g_lora_up)
    hid = jnp.arange(WIDTH) // HEAD_DIM
    bd = (hid[:, None] == hid[None, :]).astype(F32)
    tix = jnp.arange(ts)
    tri = ((tix[:, None] // chunk == tix[None, :] // chunk) & (tix[None, :] <= tix[:, None])).astype(F32)
    c0 = 0
    big = jax.ShapeDtypeStruct((bsz, seq, WIDTH), F32)
    wspec = lambda shape: pl.BlockSpec(shape, lambda b, i: (0, 0))
    ospec = pl.BlockSpec((1, ts, WIDTH), lambda b, i: (b, i, 0))
    return pl.pallas_call(
        functools.partial(_rwkv_prep_kernel, chunk=chunk),
        grid=(bsz, seq // ts),
        in_specs=[
            pl.BlockSpec((1, ts, WIDTH), lambda b, i: (b, i, c0)),
            pl.BlockSpec((1, ts, WIDTH), lambda b, i: (b, i, c0 + 1)),
            pl.BlockSpec((1, ts, WIDTH), lambda b, i: (b, i, c0 + 2)),
            pl.BlockSpec((1, ts, WIDTH), lambda b, i: (b, i, c0 + 3)),
            wspec((1, COL_B)), wspec((8, WIDTH)), wspec((LANES, WIDTH)), wspec((LANES, WIDTH)),
            wspec((2 * LANES, WIDTH)), wspec((WIDTH, WIDTH)), wspec((ts, ts)),
        ],
        out_specs=[ospec] * 7 + [pl.BlockSpec((1, ts // chunk, WIDTH), lambda b, i: (b, i, 0))],
        out_shape=[jax.ShapeDtypeStruct((bsz, seq, WIDTH), BF16)] * 5 + [big] * 2
        + [jax.ShapeDtypeStruct((bsz, seq // chunk, WIDTH), F32)],
        scratch_shapes=[pltpu.VMEM((8, COL_B), F32)],
        compiler_params=_cparams(("parallel", "arbitrary")),
        name="rwkv_prep",
    )(p3d, p3d, p3d, p3d, mu, vec, ww, wa, wg, bd, tri)


def _rwkv_scan_kernel(rt_ref, kt_ref, kd_ref, bd_ref, v_ref, g_ref, bonus_ref, pend_ref, ln_ref, o_ref,
                      state_ref, *, chunk, cps, prec):
    @pl.when(pl.program_id(1) == 0)
    def _():
        state_ref[...] = jnp.zeros_like(state_ref)

    c2 = 2 * chunk
    lane = lax.broadcasted_iota(jnp.int32, (chunk, LANES), 1)
    first = lane < HEAD_DIM
    row = lax.broadcasted_iota(jnp.int32, (c2, c2), 0)
    col = lax.broadcasted_iota(jnp.int32, (c2, c2), 1)
    eye = (row == col).astype(F32)
    hrow = lax.broadcasted_iota(jnp.int32, (LANES, LANES), 0) // HEAD_DIM
    hcol = lax.broadcasted_iota(jnp.int32, (LANES, LANES), 1) // HEAD_DIM
    head_mean = jnp.where(hrow == hcol, 1.0 / HEAD_DIM, 0.0).astype(F32)
    nt = (((1,), (1,)), ((), ()))
    tn = (((0,), (0,)), ((), ()))
    dot = functools.partial(jnp.dot, precision=prec, preferred_element_type=F32)
    dotg = functools.partial(lax.dot_general, precision=prec, preferred_element_type=F32)

    def stack(x):
        return jnp.concatenate([jnp.where(first, x, 0.0), jnp.where(first, 0.0, x)], axis=0)

    pairs = range(PAIRS)
    units = [(j, hp) for j in range(cps) for hp in pairs]
    sls = [slice(hp * LANES, (hp + 1) * LANES) for hp in pairs]
    rows_of = [slice(j * chunk, (j + 1) * chunk) for j in range(cps)]
    rs, ks, kds, bs, vs = ({(j, hp): stack(ref[0, rows_of[j], sls[hp]].astype(F32)) for j, hp in units}
                           for ref in (rt_ref, kt_ref, kd_ref, bd_ref, v_ref))
    big = {u: dotg(jnp.concatenate([ks[u], rs[u]], axis=0), jnp.concatenate([bs[u], kds[u]], axis=0), nt)
           for u in units}
    a_b = {u: jnp.where(row > col, big[u][0:c2, 0:c2], 0.0) for u in units}
    a_k = {u: jnp.where(row > col, big[u][0:c2, c2:], 0.0) for u in units}
    a_rb = {u: jnp.where(row >= col, big[u][c2:, 0:c2], 0.0) for u in units}
    a_rk = {u: jnp.where(row >= col, big[u][c2:, c2:], 0.0) for u in units}
    av = {u: dot(jnp.concatenate([a_k[u], a_rk[u]], axis=0), vs[u]) for u in units}
    vk = {u: dotg(vs[u], kds[u], tn) for u in units}
    inv = {u: eye - a_b[u] for u in units}
    pw = {u: dot(a_b[u], a_b[u]) for u in units}
    n_sq = int(math.log2(chunk)) - 1
    for lvl in range(n_sq):
        if lvl + 1 < n_sq:
            both = {u: dot(jnp.concatenate([inv[u], pw[u]], axis=0), pw[u]) for u in units}
            inv = {u: inv[u] + both[u][0:c2] for u in units}
            pw = {u: both[u][c2:] for u in units}
        else:
            inv = {u: inv[u] + dot(inv[u], pw[u]) for u in units}
    hts = [state_ref[0, hp] for hp in pairs]
    for j in range(cps):
        kh = [dotg(jnp.concatenate([ks[j, hp], rs[j, hp]], axis=0), hts[hp], nt) for hp in pairs]
        us = [dot(inv[j, hp], kh[hp][0:c2] + av[j, hp][0:c2]) for hp in pairs]
        ub = [dotg(us[hp], bs[j, hp], tn) for hp in pairs]
        au = [dot(a_rb[j, hp], us[hp]) for hp in pairs]
        for hp in pairs:
            sl = sls[hp]
            pend = pend_ref[0, j, 0:1, sl]
            hts[hp] = (hts[hp] + vk[j, hp] - ub[hp]) * pend
            os_ = kh[hp][c2:] + av[j, hp][c2:] - au[hp]
            o = os_[0:chunk] + os_[chunk:]
            mu = jnp.dot(o, head_mean, precision=HI, preferred_element_type=F32)
            d = o - mu
            var = jnp.dot(d * d, head_mean, precision=HI, preferred_element_type=F32)
            on = d * lax.rsqrt(var + GN_EPS) * ln_ref[0:1, sl] + ln_ref[1:2, sl]
            o_ref[0, rows_of[j], sl] = ((on + bonus_ref[0, rows_of[j], sl]) * g_ref[0, rows_of[j], sl]
                                        ).astype(o_ref.dtype)
    for hp in pairs:
        state_ref[0, hp] = hts[hp]


def rwkv_scan(rt, kt, kd, bd, v, g, bonus, pend, lnx_g, lnx_b, *, prec=None):
    bsz, seq, _ = rt.shape
    chunk = RWKV_CHUNK
    n_chunks = seq // chunk
    ln = jnp.stack([lnx_g, lnx_b] + [jnp.zeros_like(lnx_g)] * 6).astype(F32)
    pend4 = pend.reshape(bsz, n_chunks, 1, WIDTH)
    cps = RWKV_CHUNKS_PER_STEP if n_chunks % RWKV_CHUNKS_PER_STEP == 0 else 1
    spec = pl.BlockSpec((1, cps * chunk, WIDTH), lambda b, c: (b, c, 0))
    return pl.pallas_call(
        functools.partial(_rwkv_scan_kernel, chunk=chunk, cps=cps, prec=prec),
        grid=(bsz, n_chunks // cps),
        in_specs=[spec] * 7 + [
            pl.BlockSpec((1, cps, 1, WIDTH), lambda b, c: (b, c, 0, 0)),
            pl.BlockSpec((8, WIDTH), lambda b, c: (0, 0)),
        ],
        out_specs=spec,
        out_shape=jax.ShapeDtypeStruct((bsz, seq, WIDTH), BF16),
        scratch_shapes=[pltpu.VMEM((1, PAIRS, LANES, LANES), F32)],
        compiler_params=_cparams(("parallel", "arbitrary")),
        name="rwkv_scan",
    )(rt, kt, kd, bd, v, g, bonus, pend4, ln)


def _merge_kernel(x_ref, oa_ref, ob_ref, ga_ref, gb_ref, wa_ref, wb_ref, wo_ref, g2_ref,
                  h_ref, xn_ref, acc_ref):
    j = pl.program_id(1)

    @pl.when(j == 0)
    def _():
        acc_ref[...] = x_ref[...]

    ya = jnp.dot(oa_ref[...].astype(BF16), wa_ref[...], preferred_element_type=F32)
    yb = jnp.dot(ob_ref[...].astype(BF16), wb_ref[...], preferred_element_type=F32)
    y = jax.nn.sigmoid(ga_ref[...].astype(F32)) * ya + jax.nn.sigmoid(gb_ref[...].astype(F32)) * yb
    acc_ref[...] += jnp.dot(y.astype(BF16), wo_ref[...], preferred_element_type=F32)

    @pl.when(j == pl.num_programs(1) - 1)
    def _():
        h = acc_ref[...]
        h_ref[...] = h
        ms = jnp.mean(h * h, axis=-1, keepdims=True)
        xn_ref[...] = _pack_halves(h * lax.rsqrt(ms + RMS_EPS) * g2_ref[...])


def _pack_halves(x):
    half = x.shape[1] // 2
    lo = lax.bitcast_convert_type(x[:, :half].astype(BF16).astype(F32), jnp.int32)
    hi = lax.bitcast_convert_type(x[:, half:].astype(BF16).astype(F32), jnp.int32)
    return lax.bitwise_or(lax.shift_right_logical(lo, jnp.int32(16)), hi)


def _unpack_halves(words):
    lo, hi = _unpack_words(words)
    return jnp.concatenate([lo, hi], axis=1)


def merge_out(x2d, oa, ob, p2d, w_proj_a, w_proj_b, w_out, norm2_g, *, row0=0, tm=512):
    t, d = oa.shape[0], x2d.shape[1]
    r0 = row0 // tm
    tn = WIDTH
    nj = d // tn
    g0 = 0
    return pl.pallas_call(
        _merge_kernel,
        grid=(t // tm, nj),
        in_specs=[
            pl.BlockSpec((tm, d), lambda i, j: (r0 + i, 0)),
            pl.BlockSpec((tm, WIDTH), lambda i, j: (i, 0)),
            pl.BlockSpec((tm, WIDTH), lambda i, j: (i, 0)),
            pl.BlockSpec((tm, tn), lambda i, j: (i, g0 + j)),
            pl.BlockSpec((tm, tn), lambda i, j: (i, g0 + nj + j)),
            pl.BlockSpec((WIDTH, tn), lambda i, j: (0, j)),
            pl.BlockSpec((WIDTH, tn), lambda i, j: (0, j)),
            pl.BlockSpec((tn, d), lambda i, j: (j, 0)),
            pl.BlockSpec((1, d), lambda i, j: (0, 0)),
        ],
        out_specs=[pl.BlockSpec((tm, d), lambda i, j: (i, 0)), pl.BlockSpec((tm, d // 2), lambda i, j: (i, 0))],
        out_shape=[jax.ShapeDtypeStruct((t, d), F32), jax.ShapeDtypeStruct((t, d // 2), jnp.int32)],
        scratch_shapes=[pltpu.VMEM((tm, d), F32)],
        compiler_params=_cparams(("parallel", "arbitrary")),
        name="merge_out",
    )(x2d, oa, ob, p2d, p2d, w_proj_a.astype(BF16), w_proj_b.astype(BF16), w_out.astype(BF16),
      norm2_g.reshape(1, d))


PEER_HEADS = 8
PEER_NKEYS = 128
PEER_TOPK = 16
PEER_HALF = 128


def _topk_rows(s, k):
    n = s.shape[0]
    rows = lax.broadcasted_iota(jnp.int32, s.shape, 0).astype(F32)
    vals, ids = [], []
    for _ in range(k):
        m = jnp.max(s, axis=0, keepdims=True)
        first = jnp.min(jnp.where(s == m, rows, float(n)), axis=0, keepdims=True)
        vals.append(m)
        ids.append(first)
        s = jnp.where(rows == first, -jnp.inf, s)
    return jnp.concatenate(vals, axis=0), jnp.concatenate(ids, axis=0)


def _take_rows(table, ids):
    rows = lax.broadcasted_iota(jnp.int32, table.shape, 0).astype(F32)
    return jnp.sum(jnp.where(rows == ids, table, 0.0), axis=0, keepdims=True)


def _peer_route_kernel(xn_ref, wq_ref, sk_ref, idx_ref, gate_ref, *, prec):
    tt = xn_ref.shape[0]
    k = PEER_TOPK
    xn = _unpack_halves(xn_ref[...]) if xn_ref.dtype == jnp.int32 else xn_ref[...]
    q = jnp.dot(xn.astype(wq_ref.dtype), wq_ref[...], precision=prec, preferred_element_type=F32)
    nt = (((1,), (1,)), ((), ()))
    idx_rows, gate_rows = [], []
    half = k // 2
    for h in range(PEER_HEADS):
        tops = []
        for p in range(2):
            c0 = (h * 2 + p) * PEER_HALF
            s = lax.dot_general(sk_ref[h, p].astype(wq_ref.dtype), q[:, c0:c0 + PEER_HALF].astype(wq_ref.dtype),
                                nt, precision=prec, preferred_element_type=F32)
            tops.append(_topk_rows(s, k))
        (s0, i0), (s1, i1) = tops
        cs = [s0[0:1] + s1] + [s0[i:i + 1] + s1[0:half] for i in range(1, half)] + [s0[half:] + s1[0:1]]
        best_s, pos = _topk_rows(jnp.concatenate(cs, axis=0), k)
        mid = jnp.floor((pos - k) * (1.0 / half))
        end_mid = float(k + (half - 1) * half)
        i_rank = jnp.where(pos < k, 0.0, jnp.where(pos < end_mid, 1.0 + mid, pos - (end_mid - half)))
        j_rank = jnp.where(pos < k, pos, jnp.where(pos < end_mid, (pos - k) - half * mid, 0.0))
        ids = [_take_rows(i0, i_rank[n:n + 1]) * PEER_NKEYS + _take_rows(i1, j_rank[n:n + 1]) for n in range(k)]
        e = jnp.exp(best_s - best_s[0:1])
        gate_rows.append(e / jnp.sum(e, axis=0, keepdims=True))
        idx_rows.append(jnp.concatenate(ids, axis=0).astype(jnp.int32))
    idx_ref[...] = jnp.concatenate(idx_rows, axis=0).T
    gate_ref[...] = jnp.concatenate(gate_rows, axis=0).T


def peer_route(xn2d, peer_wq, peer_subkeys, *, tt=256, prec=None, wdtype=BF16):
    t, dx = xn2d.shape
    d, nq = peer_wq.shape
    n_sel = PEER_HEADS * PEER_TOPK
    return pl.pallas_call(
        functools.partial(_peer_route_kernel, prec=prec),
        grid=(t // tt,),
        in_specs=[
            pl.BlockSpec((tt, dx), lambda i: (i, 0)),
            pl.BlockSpec((d, nq), lambda i: (0, 0)),
            pl.BlockSpec((PEER_HEADS, 2, PEER_NKEYS, PEER_HALF), lambda i: (0, 0, 0, 0)),
        ],
        out_specs=[pl.BlockSpec((tt, n_sel), lambda i: (i, 0))] * 2,
        out_shape=[jax.ShapeDtypeStruct((t, n_sel), jnp.int32), jax.ShapeDtypeStruct((t, n_sel), F32)],
        compiler_params=_cparams(("parallel",)),
        name="peer_route",
    )(xn2d, peer_wq.astype(wdtype), peer_subkeys)


def _final_kernel(h_ref, y_ref, g_ref, *rest):
    o_ref = rest[-1]
    h = h_ref[...] + y_ref[...]
    ms = jnp.mean(h * h, axis=-1, keepdims=True)
    o_ref[...] = h * lax.rsqrt(ms + RMS_EPS) * g_ref[...]


def final_norm(h2d, y2d, g, *, out=None, row0=0, total_rows=None, tm=1024):
    t, d = h2d.shape
    total = t if total_rows is None else total_rows
    r0 = row0 // tm
    spec = pl.BlockSpec((tm, d), lambda i: (i, 0))
    in_specs = [spec, spec, pl.BlockSpec((1, d), lambda i: (0, 0))]
    args = [h2d, y2d, g.reshape(1, d)]
    aliases = {}
    if out is not None:
        in_specs.append(pl.BlockSpec(memory_space=pl.ANY))
        args.append(out)
        aliases = {3: 0}
    return pl.pallas_call(
        _final_kernel,
        grid=(t // tm,),
        in_specs=in_specs,
        out_specs=pl.BlockSpec((tm, d), lambda i: (r0 + i, 0)),
        out_shape=jax.ShapeDtypeStruct((total, d), F32),
        input_output_aliases=aliases,
        compiler_params=_cparams(("parallel",)),
        name="final_norm",
    )(*args)


SC_CORES = 2
SC_SUBCORES = 16
SC_LANES = 16
SC_WORKERS = SC_CORES * SC_SUBCORES
PEER_SEL = PEER_HEADS * PEER_TOPK
PEER_ROWS = 32
PEER_PARTS = PEER_SEL // PEER_ROWS
PEER_NBUF = 4
PEER_GROUP = 32
PEER_BF16_RUN = 4


def _pack_rows_kernel(w_ref, o_ref):
    o_ref[...] = _pack_halves(w_ref[...])


def _pack_rows(w, *, tr=1024):
    e, d = w.shape
    return pl.pallas_call(
        _pack_rows_kernel,
        grid=(e // tr,),
        in_specs=[pl.BlockSpec((tr, d), lambda i: (i, 0))],
        out_specs=pl.BlockSpec((tr, d // 2), lambda i: (i, 0)),
        out_shape=jax.ShapeDtypeStruct((e, d // 2), jnp.int32),
        compiler_params=_cparams(("parallel",)),
        name="pack_rows",
    )(w)


def _unpack_words(w):
    lo = lax.bitcast_convert_type(lax.shift_left(w, jnp.int32(16)), F32)
    hi = lax.bitcast_convert_type(lax.bitwise_and(w, jnp.int32(-65536)), F32)
    return lo, hi


def _packed_dot(a_words, b_words):
    from jax.experimental.pallas import tpu_sc as plsc
    prods = [plsc.bitcast(a, BF16) * plsc.bitcast(b, BF16) for a, b in zip(a_words, b_words)]
    while len(prods) > 1:
        prods = [prods[k] + prods[k + 1] for k in range(0, len(prods), 2)]
    return _unpack_words(plsc.bitcast(prods[0], jnp.int32))


def _sc_mesh():
    from jax.experimental.pallas import tpu_sc as plsc
    return plsc.VectorSubcoreMesh(core_axis_name="c", subcore_axis_name="s",
                                  num_cores=SC_CORES, num_subcores=SC_SUBCORES)


def _sc_loop(n, body, carry):
    from jax.experimental.pallas import tpu_sc as plsc
    return plsc.parallel_loop(0, n, carry=carry)(body)


def _worker_base(tokens_per_worker):
    return (lax.axis_index("s") * SC_CORES + lax.axis_index("c")) * tokens_per_worker


def _gather_compute_loop(table_hbm, idx_v, rows_v, sem, stage_v, out_row, osem, grp, compute):
    n_gathers = PEER_PARTS * grp
    ahead = PEER_NBUF - 1

    def gather(j, b):
        i = j // PEER_PARTS if isinstance(j, int) else lax.shift_right_logical(j, PEER_PARTS.bit_length() - 1)
        h = j % PEER_PARTS if isinstance(j, int) else lax.bitwise_and(j, PEER_PARTS - 1)
        ids = idx_v.at[i, pl.ds(pl.multiple_of(h * PEER_ROWS, PEER_ROWS), PEER_ROWS)]
        return pltpu.make_async_copy(table_hbm.at[ids], rows_v.at[b], sem.at[b])

    def put(i, slot):
        return pltpu.make_async_copy(stage_v.at[slot], out_row(i), osem.at[slot])

    for j in range(ahead):
        gather(j, j).start()

    @pl.loop(0, n_gathers)
    def _(j):
        b = lax.bitwise_and(j, PEER_NBUF - 1)
        h = lax.bitwise_and(j, PEER_PARTS - 1)
        i = lax.shift_right_logical(j, PEER_PARTS.bit_length() - 1)
        slot = lax.bitwise_and(i, 1)

        @pl.when((h == 0) & (i >= 2))
        def _():
            put(i - 2, slot).wait()

        @pl.when(j + ahead < n_gathers)
        def _():
            gather(j + ahead, lax.bitwise_and(j + ahead, PEER_NBUF - 1)).start()

        gather(j, b).wait()
        compute(i, h, b, slot)

        @pl.when(h == PEER_PARTS - 1)
        def _():
            put(i, slot).start()

    put(grp - 2, 0).wait()
    put(grp - 1, 1).wait()


def peer_expert_dots(x_packed, idx, u_packed):
    t, half = x_packed.shape
    n_chunks = half // SC_LANES
    tpw = t // SC_WORKERS
    grp = min(PEER_GROUP, tpw)
    rows_tog = 8

    def body(x_hbm, idx_hbm, u_hbm, out_hbm, idx_v, x_v, rows_v, ps_v, sem, osem):
        base = _worker_base(tpw)

        def compute(i, h, b, slot):
            @pl.loop(0, PEER_ROWS // rows_tog)
            def _(rg):
                r0 = rg * rows_tog
                accs = [[None, None] for _ in range(rows_tog)]
                for c0 in range(0, n_chunks, PEER_BF16_RUN):
                    ats = [pl.ds((c0 + k) * SC_LANES, SC_LANES) for k in range(PEER_BF16_RUN)]
                    xw = [x_v[i, at] for at in ats]
                    for r in range(rows_tog):
                        terms = _packed_dot([rows_v[b, r0 + r, at] for at in ats], xw)
                        for k, term in enumerate(terms):
                            accs[r][k] = term if accs[r][k] is None else accs[r][k] + term
                for r in range(rows_tog):
                    at = pl.ds(pl.multiple_of((h * PEER_ROWS + r0 + r) * SC_LANES, SC_LANES), SC_LANES)
                    ps_v[slot, at] = accs[r][0] + accs[r][1]

        @pl.loop(0, tpw // grp)
        def _(g):
            t0 = base + g * grp
            pltpu.sync_copy(idx_hbm.at[pl.ds(t0, grp)], idx_v)
            pltpu.sync_copy(x_hbm.at[pl.ds(t0, grp)], x_v)
            _gather_compute_loop(u_hbm, idx_v, rows_v, sem, ps_v, lambda i: out_hbm.at[t0 + i], osem, grp, compute)

    return pl.kernel(
        body,
        out_type=jax.ShapeDtypeStruct((t, PEER_SEL * SC_LANES), F32),
        mesh=_sc_mesh(),
        scratch_types=[
            pltpu.VMEM((grp, PEER_SEL), jnp.int32),
            pltpu.VMEM((grp, half), jnp.int32),
            pltpu.VMEM((PEER_NBUF, PEER_ROWS, half), jnp.int32),
            pltpu.VMEM((2, PEER_SEL * SC_LANES), F32),
            pltpu.SemaphoreType.DMA((PEER_NBUF,)),
            pltpu.SemaphoreType.DMA((2,)),
        ],
        compiler_params=pltpu.CompilerParams(needs_layout_passes=False),
        name="peer_expert_dots",
    )(x_packed, idx, u_packed)


def peer_expert_mix(hgw, idx, v_packed):
    t = hgw.shape[0]
    half = v_packed.shape[1]
    d = 2 * half
    tpw = t // SC_WORKERS
    grp = min(PEER_GROUP, tpw)
    n_parts = 2
    cpp = half // SC_LANES // n_parts
    from jax.experimental.pallas import tpu_sc as plsc

    def body(hg_hbm, idx_hbm, v_hbm, out_hbm, idx_v, hg_v, rows_v, o_v2, sem, osem):
        base = _worker_base(tpw)

        def compute(i, h, b, slot):
            token = jnp.full((SC_LANES,), i, jnp.int32)
            for part in range(n_parts):
                def rbody(rq, accs):
                    r0 = rq * PEER_BF16_RUN
                    s = [plsc.load_gather(hg_v, [token, jnp.full((SC_LANES,), h * PEER_ROWS + r0 + k, jnp.int32)])
                         for k in range(PEER_BF16_RUN)]
                    new = []
                    for c in range(cpp):
                        at = pl.ds((part * cpp + c) * SC_LANES, SC_LANES)
                        lo, hi = _packed_dot([rows_v[b, r0 + k, at] for k in range(PEER_BF16_RUN)], s)
                        new.append(accs[2 * c] + lo)
                        new.append(accs[2 * c + 1] + hi)
                    return tuple(new)

                accs = _sc_loop(PEER_ROWS // PEER_BF16_RUN, rbody,
                                tuple(jnp.zeros((SC_LANES,), F32) for _ in range(2 * cpp)))
                def store(overwrite):
                    for c in range(cpp):
                        lo_at = pl.ds((part * cpp + c) * SC_LANES, SC_LANES)
                        hi_at = pl.ds(half + (part * cpp + c) * SC_LANES, SC_LANES)
                        if overwrite:
                            o_v2[slot, lo_at] = accs[2 * c]
                            o_v2[slot, hi_at] = accs[2 * c + 1]
                        else:
                            o_v2[slot, lo_at] = o_v2[slot, lo_at] + accs[2 * c]
                            o_v2[slot, hi_at] = o_v2[slot, hi_at] + accs[2 * c + 1]

                pl.when(h == 0)(functools.partial(store, True))
                pl.when(h != 0)(functools.partial(store, False))

        @pl.loop(0, tpw // grp)
        def _(g):
            t0 = base + g * grp
            pltpu.sync_copy(idx_hbm.at[pl.ds(t0, grp)], idx_v)
            pltpu.sync_copy(hg_hbm.at[pl.ds(t0, grp)], hg_v)
            _gather_compute_loop(v_hbm, idx_v, rows_v, sem, o_v2, lambda i: out_hbm.at[t0 + i], osem, grp, compute)

    return pl.kernel(
        body,
        out_type=jax.ShapeDtypeStruct((t, d), F32),
        mesh=_sc_mesh(),
        scratch_types=[
            pltpu.VMEM((grp, PEER_SEL), jnp.int32),
            pltpu.VMEM((grp, PEER_SEL), jnp.int32),
            pltpu.VMEM((PEER_NBUF, PEER_ROWS, half), jnp.int32),
            pltpu.VMEM((2, d), F32),
            pltpu.SemaphoreType.DMA((PEER_NBUF,)),
            pltpu.SemaphoreType.DMA((2,)),
        ],
        compiler_params=pltpu.CompilerParams(needs_layout_passes=False),
        name="peer_expert_mix",
    )(hgw, idx, v_packed)


def _peer_act_kernel(ps_ref, gate_ref, sum_ref, o_ref):
    ps = ps_ref[...]
    sel = sum_ref[...]
    hi = ps.astype(BF16)
    rest = ps - hi.astype(F32)
    mid = rest.astype(BF16)
    lo = (rest - mid.astype(F32)).astype(BF16)
    pre = (jnp.dot(hi, sel, preferred_element_type=F32) + jnp.dot(mid, sel, preferred_element_type=F32)
           + jnp.dot(lo, sel, preferred_element_type=F32))
    hg = 0.5 * pre * (1.0 + lax.erf(pre * (1.0 / math.sqrt(2.0)))) * gate_ref[...]
    bits = lax.bitcast_convert_type(hg.astype(BF16).astype(F32), jnp.int32)
    o_ref[...] = lax.bitwise_or(bits, lax.shift_right_logical(bits, jnp.int32(16)))


def peer_act(ps, gates, *, tm=512):
    t, n = ps.shape
    lane_sum = (jnp.arange(n)[:, None] // SC_LANES == jnp.arange(PEER_SEL)[None, :]).astype(BF16)
    return pl.pallas_call(
        _peer_act_kernel,
        grid=(t // tm,),
        in_specs=[
            pl.BlockSpec((tm, n), lambda i: (i, 0)),
            pl.BlockSpec((tm, PEER_SEL), lambda i: (i, 0)),
            pl.BlockSpec((n, PEER_SEL), lambda i: (0, 0)),
        ],
        out_specs=pl.BlockSpec((tm, PEER_SEL), lambda i: (i, 0)),
        out_shape=jax.ShapeDtypeStruct((t, PEER_SEL), jnp.int32),
        compiler_params=_cparams(("parallel",)),
        name="peer_act",
    )(ps, gates, lane_sum)


BATCH_GROUPS = 8


def kernel(x, norm1_g, w_in, rwkv_mu, w0, w_lora_up, a0, a_lora_up, g_lora_up, k_k, k_a, r_k, lnx_g, lnx_b,
           w_proj_a, w_proj_b, w_out, norm2_g, peer_wq, peer_subkeys, peer_u, peer_v, rel_bias, normf_g):
    bsz, seq, d = x.shape
    depth = norm1_g.shape[0]
    groups = BATCH_GROUPS if bsz % BATCH_GROUPS == 0 else 1
    gb = bsz // groups
    tg = gb * seq
    t = bsz * seq
    src = x.reshape(t, d)
    for l in range(depth):
        w_pad = jnp.concatenate([
            w_in[l][:, :COL_A + COL_B_RAW],
            jnp.zeros((d, COL_B - COL_B_RAW), w_in.dtype),
            w_in[l][:, COL_A + COL_B_RAW:]], axis=1).astype(BF16)
        u_packed = _pack_rows(peer_u[l])
        v_packed = _pack_rows(peer_v[l])
        last = l == depth - 1

        def mix(pending, tie=None):
            row0, h2d, ps, gates, idx = pending
            hgw = peer_act(ps, gates)
            if tie is not None:
                tie, hgw = lax.optimization_barrier((tie, hgw))
            return tie, (row0, h2d, peer_expert_mix(hgw, idx, v_packed))

        outs = []

        def close(mixed):
            row0, h2d, y2d = mixed
            if last:
                outs.append(final_norm(h2d, y2d, normf_g, out=outs[-1] if outs else None, row0=row0, total_rows=t))
            else:
                outs.append(h2d + y2d)

        pending = closing = None
        for g in range(groups):
            pa, pb, pg = norm_proj(src, norm1_g[l], w_pad, row0=g * tg, rows=tg)
            oa = moba_attention(pa.reshape(gb, seq, -1), rel_bias)
            prep = tuple(rwkv_prep(pb.reshape(gb, seq, -1), rwkv_mu[l], w0[l], w_lora_up[l], a0[l], a_lora_up[l], g_lora_up[l],
                                   k_k[l], k_a[l], r_k[l]))
            mixed = None
            if pending is not None:
                (oa, prep), mixed = mix(pending, (oa, prep))
            if closing is not None:
                oa, y2d = lax.optimization_barrier((oa, closing[2]))
                close(closing[:2] + (y2d,))
                closing = None
            ob = rwkv_scan(*prep, lnx_g[l], lnx_b[l])
            h2d, xn2 = merge_out(src, oa.reshape(tg, WIDTH), ob.reshape(tg, WIDTH), pg, w_proj_a[l], w_proj_b[l],
                                 w_out[l], norm2_g[l], row0=g * tg)
            idx, gates = peer_route(xn2, peer_wq[l], peer_subkeys[l])
            if mixed is not None:
                idx, y2d = lax.optimization_barrier((idx, mixed[2]))
                closing = mixed[:2] + (y2d,)
            pending = (g * tg, h2d, peer_expert_dots(xn2, idx, u_packed), gates, idx)
        if closing is not None:
            close(closing)
        close(mix(pending)[1])
        src = outs[-1] if last else jnp.concatenate(outs, axis=0)
    return src.reshape(bsz, seq, d)
```

```python
import functools
import math

import jax
import jax.numpy as jnp
from jax import lax
from jax.experimental import pallas as pl
from jax.experimental.pallas import tpu as pltpu

F32 = jnp.float32
BF16 = jnp.bfloat16
HI = lax.Precision.HIGHEST

LANES = 128
HEAD_DIM = 64
HEADS = 8
PAIRS = HEADS // 2
WIDTH = HEADS * HEAD_DIM
MOBA_BLOCK = 256
MOBA_TOPK = 3
MOBA_LO = 64
REL_BUCKETS = 32
REL_MAX_DIST = 128
DECAY_LORA = 64
AAA_LORA = 64
GATE_LORA = 160
GN_EPS = 64e-5
RMS_EPS = 1e-6
NEG = -1e30
RWKV_CHUNK = 64
RWKV_CHUNKS_PER_STEP = 4
COL_A = 3 * WIDTH
COL_B_RAW = 3 * WIDTH + DECAY_LORA + AAA_LORA + GATE_LORA
COL_B = 4 * WIDTH
COL_G_OFF = COL_A + COL_B
VMEM_LIMIT = 56 * 1024 * 1024


def _cparams(sem):
    return pltpu.CompilerParams(dimension_semantics=sem, vmem_limit_bytes=VMEM_LIMIT)


def _norm_proj_kernel(x_ref, g_ref, w_ref, pa_ref, pb_ref, pg_ref, xn_ref, *, ja, jb):
    j = pl.program_id(1)

    @pl.when(j == 0)
    def _():
        x = x_ref[...]
        ms = jnp.mean(x * x, axis=-1, keepdims=True)
        xn_ref[...] = (x * lax.rsqrt(ms + RMS_EPS) * g_ref[...]).astype(xn_ref.dtype)

    res = jnp.dot(xn_ref[...], w_ref[...], preferred_element_type=F32)

    @pl.when(j < ja)
    def _():
        pa_ref[...] = res.astype(pa_ref.dtype)

    @pl.when((j >= ja) & (j < jb))
    def _():
        pb_ref[...] = res

    @pl.when(j >= jb)
    def _():
        pg_ref[...] = res.astype(pg_ref.dtype)


def norm_proj(x2d, g, w, *, row0=0, rows=None, tm=2048, tn=512):
    d = x2d.shape[1]
    t = x2d.shape[0] if rows is None else rows
    n = w.shape[1]
    r0 = row0 // tm
    ja, jb, jn = COL_A // tn, COL_G_OFF // tn, n // tn
    return pl.pallas_call(
        functools.partial(_norm_proj_kernel, ja=ja, jb=jb),
        grid=(t // tm, jn),
        in_specs=[
            pl.BlockSpec((tm, d), lambda i, j: (r0 + i, 0)),
            pl.BlockSpec((1, d), lambda i, j: (0, 0)),
            pl.BlockSpec((d, tn), lambda i, j: (0, j)),
        ],
        out_specs=[
            pl.BlockSpec((tm, tn), lambda i, j: (i, jnp.minimum(j, ja - 1))),
            pl.BlockSpec((tm, tn), lambda i, j: (i, jnp.clip(j - ja, 0, jb - ja - 1))),
            pl.BlockSpec((tm, tn), lambda i, j: (i, jnp.maximum(j - jb, 0))),
        ],
        out_shape=[jax.ShapeDtypeStruct((t, COL_A), BF16), jax.ShapeDtypeStruct((t, COL_B), F32),
                   jax.ShapeDtypeStruct((t, n - COL_G_OFF), BF16)],
        scratch_shapes=[pltpu.VMEM((tm, d), w.dtype)],
        compiler_params=_cparams(("parallel", "arbitrary")),
        name="norm_proj",
    )(x2d, g.reshape(1, d), w)


def _rel_bucket(dist):
    n = jnp.maximum(dist, 0)
    max_exact = REL_BUCKETS // 2
    nf = jnp.maximum(n, 1).astype(F32)
    large = max_exact + (jnp.log(nf / max_exact) / math.log(REL_MAX_DIST / max_exact)
                         * (REL_BUCKETS - max_exact)).astype(jnp.int32)
    large = jnp.minimum(large, REL_BUCKETS - 1)
    return jnp.where(n < max_exact, n, large)


def _moba_kernel(q_ref, k_ref, v_ref, bown_ref, bprev_ref, bfar_ref, o_ref,
                 kb_ref, vb_ref, kbar_ref, *, n_blocks):
    qb = pl.program_id(2)
    blk = MOBA_BLOCK
    scale = 1.0 / math.sqrt(HEAD_DIM)

    rows2 = 2 * blk
    nt = (((1,), (1,)), ((), ()))

    @pl.when(qb == 0)
    def _():
        kbar_ref[...] = jnp.zeros_like(kbar_ref)
        lane_b = lax.broadcasted_iota(jnp.int32, (blk, LANES), 1)
        for n in range(n_blocks):
            kblk = k_ref[0, n * blk:(n + 1) * blk, :]
            kbar_ref[n:n + 1, :] = jnp.mean(kblk.astype(F32), axis=0, keepdims=True)
            kb_ref[n * blk:(n + 1) * blk, 0:LANES] = kblk.astype(BF16)
            kb_ref[n * blk:(n + 1) * blk, LANES:] = ((lane_b == n) | (lane_b == MOBA_LO + n)).astype(BF16)
        vb_ref[...] = v_ref[0].astype(BF16)

    q2 = q_ref[0].astype(F32)
    first = lax.broadcasted_iota(jnp.int32, (blk, LANES), 1) < HEAD_DIM
    qh = jnp.concatenate([jnp.where(first, q2, 0.0), jnp.where(first, 0.0, q2)], axis=0)
    lane = lax.broadcasted_iota(jnp.int32, (rows2, LANES), 1)
    rowi = lax.broadcasted_iota(jnp.int32, (rows2, LANES), 0)
    gate = lax.dot_general(qh.astype(BF16), kbar_ref[...].astype(BF16), nt, preferred_element_type=F32)
    g = jnp.where(lane < qb, gate, -jnp.inf)
    chosen = lane < 0
    lane_f = lane.astype(F32)
    for _ in range(MOBA_TOPK):
        m = jnp.max(g, axis=1, keepdims=True)
        idx = jnp.min(jnp.where(g == m, lane_f, float(LANES)), axis=1, keepdims=True)
        hit = (lane_f == idx) & (m > -jnp.inf)
        chosen = chosen | hit
        g = jnp.where(hit, -jnp.inf, g)
    nfar = qb - 1
    bfar = jnp.where(rowi < blk, bfar_ref[0, 0:1, 0:1], bfar_ref[1, 0:1, 0:1])
    bhi = bfar.astype(BF16).astype(F32)
    madd = jnp.where(lane < nfar, jnp.where(chosen, bhi, NEG),
                     jnp.where(lane == nfar, jnp.where(chosen, 0.0, NEG),
                               jnp.where((lane >= MOBA_LO) & (lane - MOBA_LO < nfar), bfar - bhi, 0.0)))
    q_aug = jnp.concatenate([(qh * scale).astype(BF16), madd.astype(BF16)], axis=1)

    prev0 = pl.multiple_of(jnp.maximum(nfar, 0) * blk, blk)
    own0 = pl.multiple_of(qb * blk, blk)
    s_prev = (lax.dot_general(q_aug, kb_ref[pl.ds(prev0, blk), :], nt, preferred_element_type=F32)
              + bprev_ref[...].reshape(rows2, blk) + jnp.where(qb > 0, 0.0, NEG))
    s_own = (lax.dot_general(q_aug, kb_ref[pl.ds(own0, blk), :], nt, preferred_element_type=F32)
             + bown_ref[...].reshape(rows2, blk))
    r = lax.broadcasted_iota(jnp.int32, (rows2, blk), 0)
    c = lax.broadcasted_iota(jnp.int32, (rows2, blk), 1)
    s_own = jnp.where(lax.bitwise_and(r, blk - 1) >= c, s_own, NEG)
    s = jnp.concatenate([s_prev, s_own], axis=1)
    m_i = jnp.max(s, axis=1, keepdims=True)
    p = jnp.exp(s - m_i)
    l_i = jnp.sum(p, axis=1, keepdims=True)
    v0 = jnp.concatenate([vb_ref[pl.ds(prev0, blk), :], vb_ref[pl.ds(own0, blk), :]], axis=0)
    acc = jnp.dot(p.astype(BF16), v0, preferred_element_type=F32)

    def body(it, carry):
        m_i, l_i, acc = carry
        k0 = pl.multiple_of(it * rows2, rows2)
        s = lax.dot_general(q_aug, kb_ref[pl.ds(k0, rows2), :], nt, preferred_element_type=F32)
        tail = jnp.where(2 * it + 1 < nfar, 0.0, NEG)
        s = jnp.concatenate([s[:, :blk], s[:, blk:] + tail], axis=1)
        m_new = jnp.maximum(m_i, jnp.max(s, axis=1, keepdims=True))
        alpha = jnp.exp(m_i - m_new)
        p = jnp.exp(s - m_new)
        l_new = alpha * l_i + jnp.sum(p, axis=1, keepdims=True)
        acc_new = alpha * acc + jnp.dot(p.astype(BF16), vb_ref[pl.ds(k0, rows2), :], preferred_element_type=F32)
        return m_new, l_new, acc_new

    m_i, l_i, acc = lax.fori_loop(0, (jnp.maximum(nfar, 0) + 1) // 2, body, (m_i, l_i, acc))
    out = acc / l_i
    o_ref[0] = jnp.where(first, out[:blk], out[blk:]).astype(o_ref.dtype)


def moba_attention(p3d, rel_bias):
    bsz, seq, _ = p3d.shape
    blk = MOBA_BLOCK
    n_blocks = seq // blk
    assert n_blocks <= MOBA_LO and seq % blk == 0
    span = 2 * blk
    by_dist = rel_bias[:, _rel_bucket(jnp.arange(span))].astype(F32)
    shift = jnp.arange(span)

    def toeplitz(c):
        k = jnp.where(shift < blk, shift, shift - span)
        s = by_dist[:, jnp.clip(c - k, 0, span - 1)]
        tiled = jnp.tile(s, (1, blk))[:, :blk * (span - 1)]
        return tiled.reshape(HEADS, blk, span - 1)[:, :, :blk]

    bias_own = toeplitz(0)
    bias_prev = toeplitz(blk)
    bias_far = jnp.broadcast_to(rel_bias[:, REL_BUCKETS - 1].astype(F32)[:, None, None], (HEADS, 8, LANES))
    kern = functools.partial(_moba_kernel, n_blocks=n_blocks)
    return pl.pallas_call(
        kern,
        grid=(bsz, PAIRS, n_blocks),
        in_specs=[
            pl.BlockSpec((1, blk, LANES), lambda b, h, i: (b, i, h)),
            pl.BlockSpec((1, seq, LANES), lambda b, h, i: (b, 0, PAIRS + h)),
            pl.BlockSpec((1, seq, LANES), lambda b, h, i: (b, 0, 2 * PAIRS + h)),
            pl.BlockSpec((2, blk, blk), lambda b, h, i: (h, 0, 0)),
            pl.BlockSpec((2, blk, blk), lambda b, h, i: (h, 0, 0)),
            pl.BlockSpec((2, 8, LANES), lambda b, h, i: (h, 0, 0)),
        ],
        out_specs=pl.BlockSpec((1, blk, LANES), lambda b, h, i: (b, i, h)),
        out_shape=jax.ShapeDtypeStruct((bsz, seq, WIDTH), BF16),
        scratch_shapes=[
            pltpu.VMEM((seq, 2 * LANES), BF16),
            pltpu.VMEM((seq, LANES), BF16),
            pltpu.VMEM((LANES, LANES), F32),
        ],
        compiler_params=_cparams(("parallel", "parallel", "arbitrary")),
        name="moba",
    )(p3d, p3d, p3d, bias_own, bias_prev, bias_far)


def _shifted(x, carry_row):
    rows = lax.broadcasted_iota(jnp.int32, x.shape, 0)
    return jnp.where(rows == 0, carry_row, pltpu.roll(x, 1, axis=0))


def _rwkv_prep_kernel(pr_ref, pk_ref, pv_ref, pl_ref, mu_ref, vec_ref, ww_ref, wa_ref, wg_ref,
                      bd_ref, tri_ref,
                      rt_ref, kt_ref, kd_ref, bd_out_ref, v_ref, g_ref, bonus_ref, pend_ref,
                      carry_ref, *, chunk):
    @pl.when(pl.program_id(1) == 0)
    def _():
        carry_ref[...] = jnp.zeros_like(carry_ref)

    def mix(ref, j):
        x = ref[0]
        mu = mu_ref[0:1, j * WIDTH:(j + 1) * WIDTH]
        prev = _shifted(x, carry_ref[0:1, j * WIDTH:(j + 1) * WIDTH])
        carry_ref[0:1, j * WIDTH:(j + 1) * WIDTH] = x[x.shape[0] - 1:, :]
        return x + mu * (prev - x)

    r = mix(pr_ref, 0)
    k = mix(pk_ref, 1)
    v = mix(pv_ref, 2)
    lo = mix(pl_ref, 3)
    w0, a0, k_k, k_a, r_k = (vec_ref[i:i + 1, :] for i in range(5))
    xwa = lo[:, 0:LANES]
    xg = lo[:, LANES:3 * LANES]
    lw = jnp.dot(jnp.tanh(xwa), ww_ref[...], precision=HI, preferred_element_type=F32)
    la = jnp.dot(xwa, wa_ref[...], precision=HI, preferred_element_type=F32)
    g = jnp.dot(jax.nn.sigmoid(xg), wg_ref[...], precision=HI, preferred_element_type=F32)
    z = -(w0 + lw)
    softplus = jnp.maximum(z, 0.0) + jnp.log(1.0 + jnp.exp(-jnp.abs(z)))
    logw = -jnp.exp(-softplus - 0.5)
    a = jax.nn.sigmoid(a0 + la)
    kk = k * k_k
    ss = jnp.dot(kk * kk, bd_ref[...], precision=HI, preferred_element_type=F32)
    kk = kk / jnp.maximum(jnp.sqrt(ss), 1e-12)
    k2 = k * (1.0 + (a - 1.0) * k_a)
    rk = jnp.dot(r * k2 * r_k, bd_ref[...], precision=HI, preferred_element_type=F32)
    cs = jnp.dot(tri_ref[...], logw, precision=HI, preferred_element_type=F32)
    e_pos = jnp.exp(cs)
    e_neg = jnp.exp(-cs)
    rt_ref[0] = (r * e_pos).astype(rt_ref.dtype)
    kt_ref[0] = (kk * jnp.exp(cs - logw)).astype(kt_ref.dtype)
    kd_ref[0] = (k2 * e_neg).astype(kd_ref.dtype)
    bd_out_ref[0] = (kk * a * e_neg).astype(bd_out_ref.dtype)
    v_ref[0] = v.astype(v_ref.dtype)
    g_ref[0] = g
    bonus_ref[0] = rk * v
    ts = e_pos.shape[0]
    for c in range(ts // chunk):
        pend_ref[0, c:c + 1, :] = e_pos[(c + 1) * chunk - 1:(c + 1) * chunk, :]


def rwkv_prep(p3d, rwkv_mu, w0, w_lora_up, a0, a_lora_up, g_lora_up, k_k, k_a, r_k, *, ts=512):
    bsz, seq, _ = p3d.shape
    chunk = RWKV_CHUNK
    ts = min(ts, seq)
    mu = jnp.pad(rwkv_mu, (0, COL_B - COL_B_RAW)).reshape(1, COL_B)
    vec = jnp.stack([w0, a0, k_k, k_a, r_k.reshape(-1)] + [jnp.zeros_like(w0)] * 3).astype(F32)
    ww = jnp.zeros((LANES, WIDTH), F32).at[:DECAY_LORA].set(w_lora_up)
    wa = jnp.zeros((LANES, WIDTH), F32).at[DECAY_LORA:DECAY_LORA + AAA_LORA].set(a_lora_up)
    wg = jnp.zeros((2 * LANES, WIDTH), F32).at[:GATE_LORA].set(g_lora_up)
    hid = jnp.arange(WIDTH) // HEAD_DIM
    bd = (hid[:, None] == hid[None, :]).astype(F32)
    tix = jnp.arange(ts)
    tri = ((tix[:, None] // chunk == tix[None, :] // chunk) & (tix[None, :] <= tix[:, None])).astype(F32)
    c0 = 0
    big = jax.ShapeDtypeStruct((bsz, seq, WIDTH), F32)
    wspec = lambda shape: pl.BlockSpec(shape, lambda b, i: (0, 0))
    ospec = pl.BlockSpec((1, ts, WIDTH), lambda b, i: (b, i, 0))
    return pl.pallas_call(
        functools.partial(_rwkv_prep_kernel, chunk=chunk),
        grid=(bsz, seq // ts),
        in_specs=[
            pl.BlockSpec((1, ts, WIDTH), lambda b, i: (b, i, c0)),
            pl.BlockSpec((1, ts, WIDTH), lambda b, i: (b, i, c0 + 1)),
            pl.BlockSpec((1, ts, WIDTH), lambda b, i: (b, i, c0 + 2)),
            pl.BlockSpec((1, ts, WIDTH), lambda b, i: (b, i, c0 + 3)),
            wspec((1, COL_B)), wspec((8, WIDTH)), wspec((LANES, WIDTH)), wspec((LANES, WIDTH)),
            wspec((2 * LANES, WIDTH)), wspec((WIDTH, WIDTH)), wspec((ts, ts)),
        ],
        out_specs=[ospec] * 7 + [pl.BlockSpec((1, ts // chunk, WIDTH), lambda b, i: (b, i, 0))],
        out_shape=[jax.ShapeDtypeStruct((bsz, seq, WIDTH), BF16)] * 5 + [big] * 2
        + [jax.ShapeDtypeStruct((bsz, seq // chunk, WIDTH), F32)],
        scratch_shapes=[pltpu.VMEM((8, COL_B), F32)],
        compiler_params=_cparams(("parallel", "arbitrary")),
        name="rwkv_prep",
    )(p3d, p3d, p3d, p3d, mu, vec, ww, wa, wg, bd, tri)


def _rwkv_scan_kernel(rt_ref, kt_ref, kd_ref, bd_ref, v_ref, g_ref, bonus_ref, pend_ref, ln_ref, o_ref,
                      state_ref, *, chunk, cps, prec):
    @pl.when(pl.program_id(1) == 0)
    def _():
        state_ref[...] = jnp.zeros_like(state_ref)

    c2 = 2 * chunk
    lane = lax.broadcasted_iota(jnp.int32, (chunk, LANES), 1)
    first = lane < HEAD_DIM
    row = lax.broadcasted_iota(jnp.int32, (c2, c2), 0)
    col = lax.broadcasted_iota(jnp.int32, (c2, c2), 1)
    eye = (row == col).astype(F32)
    hrow = lax.broadcasted_iota(jnp.int32, (LANES, LANES), 0) // HEAD_DIM
    hcol = lax.broadcasted_iota(jnp.int32, (LANES, LANES), 1) // HEAD_DIM
    head_mean = jnp.where(hrow == hcol, 1.0 / HEAD_DIM, 0.0).astype(F32)
    nt = (((1,), (1,)), ((), ()))
    tn = (((0,), (0,)), ((), ()))
    dot = functools.partial(jnp.dot, precision=prec, preferred_element_type=F32)
    dotg = functools.partial(lax.dot_general, precision=prec, preferred_element_type=F32)

    def stack(x):
        return jnp.concatenate([jnp.where(first, x, 0.0), jnp.where(first, 0.0, x)], axis=0)

    pairs = range(PAIRS)
    units = [(j, hp) for j in range(cps) for hp in pairs]
    sls = [slice(hp * LANES, (hp + 1) * LANES) for hp in pairs]
    rows_of = [slice(j * chunk, (j + 1) * chunk) for j in range(cps)]
    rs, ks, kds, bs, vs = ({(j, hp): stack(ref[0, rows_of[j], sls[hp]].astype(F32)) for j, hp in units}
                           for ref in (rt_ref, kt_ref, kd_ref, bd_ref, v_ref))
    big = {u: dotg(jnp.concatenate([ks[u], rs[u]], axis=0), jnp.concatenate([bs[u], kds[u]], axis=0), nt)
           for u in units}
    a_b = {u: jnp.where(row > col, big[u][0:c2, 0:c2], 0.0) for u in units}
    a_k = {u: jnp.where(row > col, big[u][0:c2, c2:], 0.0) for u in units}
    a_rb = {u: jnp.where(row >= col, big[u][c2:, 0:c2], 0.0) for u in units}
    a_rk = {u: jnp.where(row >= col, big[u][c2:, c2:], 0.0) for u in units}
    av = {u: dot(jnp.concatenate([a_k[u], a_rk[u]], axis=0), vs[u]) for u in units}
    vk = {u: dotg(vs[u], kds[u], tn) for u in units}
    inv = {u: eye - a_b[u] for u in units}
    pw = {u: dot(a_b[u], a_b[u]) for u in units}
    n_sq = int(math.log2(chunk)) - 1
    for lvl in range(n_sq):
        if lvl + 1 < n_sq:
            both = {u: dot(jnp.concatenate([inv[u], pw[u]], axis=0), pw[u]) for u in units}
            inv = {u: inv[u] + both[u][0:c2] for u in units}
            pw = {u: both[u][c2:] for u in units}
        else:
            inv = {u: inv[u] + dot(inv[u], pw[u]) for u in units}
    hts = [state_ref[0, hp] for hp in pairs]
    for j in range(cps):
        kh = [dotg(jnp.concatenate([ks[j, hp], rs[j, hp]], axis=0), hts[hp], nt) for hp in pairs]
        us = [dot(inv[j, hp], kh[hp][0:c2] + av[j, hp][0:c2]) for hp in pairs]
        ub = [dotg(us[hp], bs[j, hp], tn) for hp in pairs]
        au = [dot(a_rb[j, hp], us[hp]) for hp in pairs]
        for hp in pairs:
            sl = sls[hp]
            pend = pend_ref[0, j, 0:1, sl]
            hts[hp] = (hts[hp] + vk[j, hp] - ub[hp]) * pend
            os_ = kh[hp][c2:] + av[j, hp][c2:] - au[hp]
            o = os_[0:chunk] + os_[chunk:]
            mu = jnp.dot(o, head_mean, precision=HI, preferred_element_type=F32)
            d = o - mu
            var = jnp.dot(d * d, head_mean, precision=HI, preferred_element_type=F32)
            on = d * lax.rsqrt(var + GN_EPS) * ln_ref[0:1, sl] + ln_ref[1:2, sl]
            o_ref[0, rows_of[j], sl] = ((on + bonus_ref[0, rows_of[j], sl]) * g_ref[0, rows_of[j], sl]
                                        ).astype(o_ref.dtype)
    for hp in pairs:
        state_ref[0, hp] = hts[hp]


def rwkv_scan(rt, kt, kd, bd, v, g, bonus, pend, lnx_g, lnx_b, *, prec=None):
    bsz, seq, _ = rt.shape
    chunk = RWKV_CHUNK
    n_chunks = seq // chunk
    ln = jnp.stack([lnx_g, lnx_b] + [jnp.zeros_like(lnx_g)] * 6).astype(F32)
    pend4 = pend.reshape(bsz, n_chunks, 1, WIDTH)
    cps = RWKV_CHUNKS_PER_STEP if n_chunks % RWKV_CHUNKS_PER_STEP == 0 else 1
    spec = pl.BlockSpec((1, cps * chunk, WIDTH), lambda b, c: (b, c, 0))
    return pl.pallas_call(
        functools.partial(_rwkv_scan_kernel, chunk=chunk, cps=cps, prec=prec),
        grid=(bsz, n_chunks // cps),
        in_specs=[spec] * 7 + [
            pl.BlockSpec((1, cps, 1, WIDTH), lambda b, c: (b, c, 0, 0)),
            pl.BlockSpec((8, WIDTH), lambda b, c: (0, 0)),
        ],
        out_specs=spec,
        out_shape=jax.ShapeDtypeStruct((bsz, seq, WIDTH), BF16),
        scratch_shapes=[pltpu.VMEM((1, PAIRS, LANES, LANES), F32)],
        compiler_params=_cparams(("parallel", "arbitrary")),
        name="rwkv_scan",
    )(rt, kt, kd, bd, v, g, bonus, pend4, ln)


def _merge_kernel(x_ref, oa_ref, ob_ref, ga_ref, gb_ref, wa_ref, wb_ref, wo_ref, g2_ref,
                  h_ref, xn_ref, acc_ref):
    j = pl.program_id(1)

    @pl.when(j == 0)
    def _():
        acc_ref[...] = x_ref[...]

    ya = jnp.dot(oa_ref[...].astype(BF16), wa_ref[...], preferred_element_type=F32)
    yb = jnp.dot(ob_ref[...].astype(BF16), wb_ref[...], preferred_element_type=F32)
    y = jax.nn.sigmoid(ga_ref[...].astype(F32)) * ya + jax.nn.sigmoid(gb_ref[...].astype(F32)) * yb
    acc_ref[...] += jnp.dot(y.astype(BF16), wo_ref[...], preferred_element_type=F32)

    @pl.when(j == pl.num_programs(1) - 1)
    def _():
        h = acc_ref[...]
        h_ref[...] = h
        ms = jnp.mean(h * h, axis=-1, keepdims=True)
        xn_ref[...] = _pack_halves(h * lax.rsqrt(ms + RMS_EPS) * g2_ref[...])


def _pack_halves(x):
    half = x.shape[1] // 2
    lo = lax.bitcast_convert_type(x[:, :half].astype(BF16).astype(F32), jnp.int32)
    hi = lax.bitcast_convert_type(x[:, half:].astype(BF16).astype(F32), jnp.int32)
    return lax.bitwise_or(lax.shift_right_logical(lo, jnp.int32(16)), hi)


def _unpack_halves(words):
    lo, hi = _unpack_words(words)
    return jnp.concatenate([lo, hi], axis=1)


def merge_out(x2d, oa, ob, p2d, w_proj_a, w_proj_b, w_out, norm2_g, *, row0=0, tm=512):
    t, d = oa.shape[0], x2d.shape[1]
    r0 = row0 // tm
    tn = WIDTH
    nj = d // tn
    g0 = 0
    return pl.pallas_call(
        _merge_kernel,
        grid=(t // tm, nj),
        in_specs=[
            pl.BlockSpec((tm, d), lambda i, j: (r0 + i, 0)),
            pl.BlockSpec((tm, WIDTH), lambda i, j: (i, 0)),
            pl.BlockSpec((tm, WIDTH), lambda i, j: (i, 0)),
            pl.BlockSpec((tm, tn), lambda i, j: (i, g0 + j)),
            pl.BlockSpec((tm, tn), lambda i, j: (i, g0 + nj + j)),
            pl.BlockSpec((WIDTH, tn), lambda i, j: (0, j)),
            pl.BlockSpec((WIDTH, tn), lambda i, j: (0, j)),
            pl.BlockSpec((tn, d), lambda i, j: (j, 0)),
            pl.BlockSpec((1, d), lambda i, j: (0, 0)),
        ],
        out_specs=[pl.BlockSpec((tm, d), lambda i, j: (i, 0)), pl.BlockSpec((tm, d // 2), lambda i, j: (i, 0))],
        out_shape=[jax.ShapeDtypeStruct((t, d), F32), jax.ShapeDtypeStruct((t, d // 2), jnp.int32)],
        scratch_shapes=[pltpu.VMEM((tm, d), F32)],
        compiler_params=_cparams(("parallel", "arbitrary")),
        name="merge_out",
    )(x2d, oa, ob, p2d, p2d, w_proj_a.astype(BF16), w_proj_b.astype(BF16), w_out.astype(BF16),
      norm2_g.reshape(1, d))


PEER_HEADS = 8
PEER_NKEYS = 128
PEER_TOPK = 16
PEER_HALF = 128


def _topk_rows(s, k):
    n = s.shape[0]
    rows = lax.broadcasted_iota(jnp.int32, s.shape, 0).astype(F32)
    vals, ids = [], []
    for _ in range(k):
        m = jnp.max(s, axis=0, keepdims=True)
        first = jnp.min(jnp.where(s == m, rows, float(n)), axis=0, keepdims=True)
        vals.append(m)
        ids.append(first)
        s = jnp.where(rows == first, -jnp.inf, s)
    return jnp.concatenate(vals, axis=0), jnp.concatenate(ids, axis=0)


def _take_rows(table, ids):
    rows = lax.broadcasted_iota(jnp.int32, table.shape, 0).astype(F32)
    return jnp.sum(jnp.where(rows == ids, table, 0.0), axis=0, keepdims=True)


def _peer_route_kernel(xn_ref, wq_ref, sk_ref, idx_ref, gate_ref, *, prec):
    tt = xn_ref.shape[0]
    k = PEER_TOPK
    xn = _unpack_halves(xn_ref[...]) if xn_ref.dtype == jnp.int32 else xn_ref[...]
    q = jnp.dot(xn.astype(wq_ref.dtype), wq_ref[...], precision=prec, preferred_element_type=F32)
    nt = (((1,), (1,)), ((), ()))
    idx_rows, gate_rows = [], []
    half = k // 2
    for h in range(PEER_HEADS):
        tops = []
        for p in range(2):
            c0 = (h * 2 + p) * PEER_HALF
            s = lax.dot_general(sk_ref[h, p].astype(wq_ref.dtype), q[:, c0:c0 + PEER_HALF].astype(wq_ref.dtype),
                                nt, precision=prec, preferred_element_type=F32)
            tops.append(_topk_rows(s, k))
        (s0, i0), (s1, i1) = tops
        cs = [s0[0:1] + s1] + [s0[i:i + 1] + s1[0:half] for i in range(1, half)] + [s0[half:] + s1[0:1]]
        best_s, pos = _topk_rows(jnp.concatenate(cs, axis=0), k)
        mid = jnp.floor((pos - k) * (1.0 / half))
        end_mid = float(k + (half - 1) * half)
        i_rank = jnp.where(pos < k, 0.0, jnp.where(pos < end_mid, 1.0 + mid, pos - (end_mid - half)))
        j_rank = jnp.where(pos < k, pos, jnp.where(pos < end_mid, (pos - k) - half * mid, 0.0))
        ids = [_take_rows(i0, i_rank[n:n + 1]) * PEER_NKEYS + _take_rows(i1, j_rank[n:n + 1]) for n in range(k)]
        e = jnp.exp(best_s - best_s[0:1])
        gate_rows.append(e / jnp.sum(e, axis=0, keepdims=True))
        idx_rows.append(jnp.concatenate(ids, axis=0).astype(jnp.int32))
    idx_ref[...] = jnp.concatenate(idx_rows, axis=0).T
    gate_ref[...] = jnp.concatenate(gate_rows, axis=0).T


def peer_route(xn2d, peer_wq, peer_subkeys, *, tt=256, prec=None, wdtype=BF16):
    t, dx = xn2d.shape
    d, nq = peer_wq.shape
    n_sel = PEER_HEADS * PEER_TOPK
    return pl.pallas_call(
        functools.partial(_peer_route_kernel, prec=prec),
        grid=(t // tt,),
        in_specs=[
            pl.BlockSpec((tt, dx), lambda i: (i, 0)),
            pl.BlockSpec((d, nq), lambda i: (0, 0)),
            pl.BlockSpec((PEER_HEADS, 2, PEER_NKEYS, PEER_HALF), lambda i: (0, 0, 0, 0)),
        ],
        out_specs=[pl.BlockSpec((tt, n_sel), lambda i: (i, 0))] * 2,
        out_shape=[jax.ShapeDtypeStruct((t, n_sel), jnp.int32), jax.ShapeDtypeStruct((t, n_sel), F32)],
        compiler_params=_cparams(("parallel",)),
        name="peer_route",
    )(xn2d, peer_wq.astype(wdtype), peer_subkeys)


def _final_kernel(h_ref, y_ref, g_ref, *rest):
    o_ref = rest[-1]
    h = h_ref[...] + y_ref[...]
    ms = jnp.mean(h * h, axis=-1, keepdims=True)
    o_ref[...] = h * lax.rsqrt(ms + RMS_EPS) * g_ref[...]


def final_norm(h2d, y2d, g, *, out=None, row0=0, total_rows=None, tm=1024):
    t, d = h2d.shape
    total = t if total_rows is None else total_rows
    r0 = row0 // tm
    spec = pl.BlockSpec((tm, d), lambda i: (i, 0))
    in_specs = [spec, spec, pl.BlockSpec((1, d), lambda i: (0, 0))]
    args = [h2d, y2d, g.reshape(1, d)]
    aliases = {}
    if out is not None:
        in_specs.append(pl.BlockSpec(memory_space=pl.ANY))
        args.append(out)
        aliases = {3: 0}
    return pl.pallas_call(
        _final_kernel,
        grid=(t // tm,),
        in_specs=in_specs,
        out_specs=pl.BlockSpec((tm, d), lambda i: (r0 + i, 0)),
        out_shape=jax.ShapeDtypeStruct((total, d), F32),
        input_output_aliases=aliases,
        compiler_params=_cparams(("parallel",)),
        name="final_norm",
    )(*args)


SC_CORES = 2
SC_SUBCORES = 16
SC_LANES = 16
SC_WORKERS = SC_CORES * SC_SUBCORES
PEER_SEL = PEER_HEADS * PEER_TOPK
PEER_ROWS = 32
PEER_PARTS = PEER_SEL // PEER_ROWS
PEER_NBUF = 4
PEER_GROUP = 64
PEER_BF16_RUN = 4


def _pack_rows_kernel(w_ref, o_ref):
    o_ref[...] = _pack_halves(w_ref[...])


def _pack_rows(w, *, tr=1024):
    e, d = w.shape
    return pl.pallas_call(
        _pack_rows_kernel,
        grid=(e // tr,),
        in_specs=[pl.BlockSpec((tr, d), lambda i: (i, 0))],
        out_specs=pl.BlockSpec((tr, d // 2), lambda i: (i, 0)),
        out_shape=jax.ShapeDtypeStruct((e, d // 2), jnp.int32),
        compiler_params=_cparams(("parallel",)),
        name="pack_rows",
    )(w)


def _unpack_words(w):
    lo = lax.bitcast_convert_type(lax.shift_left(w, jnp.int32(16)), F32)
    hi = lax.bitcast_convert_type(lax.bitwise_and(w, jnp.int32(-65536)), F32)
    return lo, hi


def _packed_dot(a_words, b_words):
    from jax.experimental.pallas import tpu_sc as plsc
    prods = [plsc.bitcast(a, BF16) * plsc.bitcast(b, BF16) for a, b in zip(a_words, b_words)]
    while len(prods) > 1:
        prods = [prods[k] + prods[k + 1] for k in range(0, len(prods), 2)]
    return _unpack_words(plsc.bitcast(prods[0], jnp.int32))


def _sc_mesh():
    from jax.experimental.pallas import tpu_sc as plsc
    return plsc.VectorSubcoreMesh(core_axis_name="c", subcore_axis_name="s",
                                  num_cores=SC_CORES, num_subcores=SC_SUBCORES)


def _sc_loop(n, body, carry):
    from jax.experimental.pallas import tpu_sc as plsc
    return plsc.parallel_loop(0, n, carry=carry)(body)


def _worker_base(tokens_per_worker):
    return (lax.axis_index("s") * SC_CORES + lax.axis_index("c")) * tokens_per_worker


def _gather_compute_loop(table_hbm, idx_v, rows_v, sem, stage_v, out_row, osem, grp, compute):
    n_gathers = PEER_PARTS * grp
    ahead = PEER_NBUF - 1

    def gather(j, b):
        i = j // PEER_PARTS if isinstance(j, int) else lax.shift_right_logical(j, PEER_PARTS.bit_length() - 1)
        h = j % PEER_PARTS if isinstance(j, int) else lax.bitwise_and(j, PEER_PARTS - 1)
        ids = idx_v.at[i, pl.ds(pl.multiple_of(h * PEER_ROWS, PEER_ROWS), PEER_ROWS)]
        return pltpu.make_async_copy(table_hbm.at[ids], rows_v.at[b], sem.at[b])

    def put(i, slot):
        return pltpu.make_async_copy(stage_v.at[slot], out_row(i), osem.at[slot])

    for j in range(ahead):
        gather(j, j).start()

    @pl.loop(0, n_gathers)
    def _(j):
        b = lax.bitwise_and(j, PEER_NBUF - 1)
        h = lax.bitwise_and(j, PEER_PARTS - 1)
        i = lax.shift_right_logical(j, PEER_PARTS.bit_length() - 1)
        slot = lax.bitwise_and(i, 1)

        @pl.when((h == 0) & (i >= 2))
        def _():
            put(i - 2, slot).wait()

        @pl.when(j + ahead < n_gathers)
        def _():
            gather(j + ahead, lax.bitwise_and(j + ahead, PEER_NBUF - 1)).start()

        gather(j, b).wait()
        compute(i, h, b, slot)

        @pl.when(h == PEER_PARTS - 1)
        def _():
            put(i, slot).start()

    put(grp - 2, 0).wait()
    put(grp - 1, 1).wait()


def peer_expert_dots(x_packed, idx, u_packed):
    t, half = x_packed.shape
    n_chunks = half // SC_LANES
    tpw = t // SC_WORKERS
    grp = min(PEER_GROUP, tpw)
    assert t % SC_WORKERS == 0 and tpw % grp == 0 and grp % 2 == 0 and idx.shape == (t, PEER_SEL)
    rows_tog = 8

    def body(x_hbm, idx_hbm, u_hbm, out_hbm, idx_v, x_v, rows_v, ps_v, sem, osem):
        base = _worker_base(tpw)

        def compute(i, h, b, slot):
            @pl.loop(0, PEER_ROWS // rows_tog)
            def _(rg):
                r0 = rg * rows_tog
                accs = [[None, None] for _ in range(rows_tog)]
                for c0 in range(0, n_chunks, PEER_BF16_RUN):
                    ats = [pl.ds((c0 + k) * SC_LANES, SC_LANES) for k in range(PEER_BF16_RUN)]
                    xw = [x_v[i, at] for at in ats]
                    for r in range(rows_tog):
                        terms = _packed_dot([rows_v[b, r0 + r, at] for at in ats], xw)
                        for k, term in enumerate(terms):
                            accs[r][k] = term if accs[r][k] is None else accs[r][k] + term
                for r in range(rows_tog):
                    at = pl.ds(pl.multiple_of((h * PEER_ROWS + r0 + r) * SC_LANES, SC_LANES), SC_LANES)
                    ps_v[slot, at] = accs[r][0] + accs[r][1]

        @pl.loop(0, tpw // grp)
        def _(g):
            t0 = base + g * grp
            pltpu.sync_copy(idx_hbm.at[pl.ds(t0, grp)], idx_v)
            pltpu.sync_copy(x_hbm.at[pl.ds(t0, grp)], x_v)
            _gather_compute_loop(u_hbm, idx_v, rows_v, sem, ps_v, lambda i: out_hbm.at[t0 + i], osem, grp, compute)

    return pl.kernel(
        body,
        out_type=jax.ShapeDtypeStruct((t, PEER_SEL * SC_LANES), F32),
        mesh=_sc_mesh(),
        scratch_types=[
            pltpu.VMEM((grp, PEER_SEL), jnp.int32),
            pltpu.VMEM((grp, half), jnp.int32),
            pltpu.VMEM((PEER_NBUF, PEER_ROWS, half), jnp.int32),
            pltpu.VMEM((2, PEER_SEL * SC_LANES), F32),
            pltpu.SemaphoreType.DMA((PEER_NBUF,)),
            pltpu.SemaphoreType.DMA((2,)),
        ],
        compiler_params=pltpu.CompilerParams(needs_layout_passes=False),
        name="peer_expert_dots",
    )(x_packed, idx, u_packed)


def peer_expert_mix(hgw, idx, v_packed):
    t = hgw.shape[0]
    half = v_packed.shape[1]
    d = 2 * half
    tpw = t // SC_WORKERS
    grp = min(2 * PEER_GROUP, tpw)
    assert t % SC_WORKERS == 0 and tpw % grp == 0 and grp % 2 == 0 and idx.shape == (t, PEER_SEL)
    n_parts = 2
    cpp = half // SC_LANES // n_parts
    from jax.experimental.pallas import tpu_sc as plsc

    def body(hg_hbm, idx_hbm, v_hbm, out_hbm, idx_v, hg_v, rows_v, o_v2, sem, osem):
        base = _worker_base(tpw)

        def compute(i, h, b, slot):
            token = jnp.full((SC_LANES,), i, jnp.int32)
            for part in range(n_parts):
                def rbody(rq, accs):
                    r0 = rq * PEER_BF16_RUN
                    s = [plsc.load_gather(hg_v, [token, jnp.full((SC_LANES,), h * PEER_ROWS + r0 + k, jnp.int32)])
                         for k in range(PEER_BF16_RUN)]
                    new = []
                    for c in range(cpp):
                        at = pl.ds((part * cpp + c) * SC_LANES, SC_LANES)
                        lo, hi = _packed_dot([rows_v[b, r0 + k, at] for k in range(PEER_BF16_RUN)], s)
                        new.append(accs[2 * c] + lo)
                        new.append(accs[2 * c + 1] + hi)
                    return tuple(new)

                accs = _sc_loop(PEER_ROWS // PEER_BF16_RUN, rbody,
                                tuple(jnp.zeros((SC_LANES,), F32) for _ in range(2 * cpp)))
                def store(overwrite):
                    for c in range(cpp):
                        lo_at = pl.ds((part * cpp + c) * SC_LANES, SC_LANES)
                        hi_at = pl.ds(half + (part * cpp + c) * SC_LANES, SC_LANES)
                        if overwrite:
                            o_v2[slot, lo_at] = accs[2 * c]
                            o_v2[slot, hi_at] = accs[2 * c + 1]
                        else:
                            o_v2[slot, lo_at] = o_v2[slot, lo_at] + accs[2 * c]
                            o_v2[slot, hi_at] = o_v2[slot, hi_at] + accs[2 * c + 1]

                pl.when(h == 0)(functools.partial(store, True))
                pl.when(h != 0)(functools.partial(store, False))

        @pl.loop(0, tpw // grp)
        def _(g):
            t0 = base + g * grp
            pltpu.sync_copy(idx_hbm.at[pl.ds(t0, grp)], idx_v)
            pltpu.sync_copy(hg_hbm.at[pl.ds(t0, grp)], hg_v)
            _gather_compute_loop(v_hbm, idx_v, rows_v, sem, o_v2, lambda i: out_hbm.at[t0 + i], osem, grp, compute)

    return pl.kernel(
        body,
        out_type=jax.ShapeDtypeStruct((t, d), F32),
        mesh=_sc_mesh(),
        scratch_types=[
            pltpu.VMEM((grp, PEER_SEL), jnp.int32),
            pltpu.VMEM((grp, PEER_SEL), jnp.int32),
            pltpu.VMEM((PEER_NBUF, PEER_ROWS, half), jnp.int32),
            pltpu.VMEM((2, d), F32),
            pltpu.SemaphoreType.DMA((PEER_NBUF,)),
            pltpu.SemaphoreType.DMA((2,)),
        ],
        compiler_params=pltpu.CompilerParams(needs_layout_passes=False),
        name="peer_expert_mix",
    )(hgw, idx, v_packed)


def _peer_act_kernel(ps_ref, gate_ref, sum_ref, o_ref):
    ps = ps_ref[...]
    sel = sum_ref[...]
    hi = ps.astype(BF16)
    rest = ps - hi.astype(F32)
    mid = rest.astype(BF16)
    lo = (rest - mid.astype(F32)).astype(BF16)
    pre = (jnp.dot(hi, sel, preferred_element_type=F32) + jnp.dot(mid, sel, preferred_element_type=F32)
           + jnp.dot(lo, sel, preferred_element_type=F32))
    hg = 0.5 * pre * (1.0 + lax.erf(pre * (1.0 / math.sqrt(2.0)))) * gate_ref[...]
    bits = lax.bitcast_convert_type(hg.astype(BF16).astype(F32), jnp.int32)
    o_ref[...] = lax.bitwise_or(bits, lax.shift_right_logical(bits, jnp.int32(16)))


def peer_act(ps, gates, *, tm=512):
    t, n = ps.shape
    lane_sum = (jnp.arange(n)[:, None] // SC_LANES == jnp.arange(PEER_SEL)[None, :]).astype(BF16)
    return pl.pallas_call(
        _peer_act_kernel,
        grid=(t // tm,),
        in_specs=[
            pl.BlockSpec((tm, n), lambda i: (i, 0)),
            pl.BlockSpec((tm, PEER_SEL), lambda i: (i, 0)),
            pl.BlockSpec((n, PEER_SEL), lambda i: (0, 0)),
        ],
        out_specs=pl.BlockSpec((tm, PEER_SEL), lambda i: (i, 0)),
        out_shape=jax.ShapeDtypeStruct((t, PEER_SEL), jnp.int32),
        compiler_params=_cparams(("parallel",)),
        name="peer_act",
    )(ps, gates, lane_sum)


BATCH_GROUPS = 8


def kernel(x, norm1_g, w_in, rwkv_mu, w0, w_lora_up, a0, a_lora_up, g_lora_up, k_k, k_a, r_k, lnx_g, lnx_b,
           w_proj_a, w_proj_b, w_out, norm2_g, peer_wq, peer_subkeys, peer_u, peer_v, rel_bias, normf_g):
    bsz, seq, d = x.shape
    depth = norm1_g.shape[0]
    groups = BATCH_GROUPS if bsz % BATCH_GROUPS == 0 else 1
    gb = bsz // groups
    tg = gb * seq
    t = bsz * seq
    src = x.reshape(t, d)
    for l in range(depth):
        w_pad = jnp.concatenate([
            w_in[l][:, :COL_A + COL_B_RAW],
            jnp.zeros((d, COL_B - COL_B_RAW), w_in.dtype),
            w_in[l][:, COL_A + COL_B_RAW:]], axis=1).astype(BF16)
        u_packed = _pack_rows(peer_u[l])
        v_packed = _pack_rows(peer_v[l])
        last = l == depth - 1

        def mix(pending, tie=None):
            row0, h2d, ps, gates, idx = pending
            hgw = peer_act(ps, gates)
            if tie is not None:
                tie, hgw = lax.optimization_barrier((tie, hgw))
            return tie, (row0, h2d, peer_expert_mix(hgw, idx, v_packed))

        outs = []

        def close(mixed):
            row0, h2d, y2d = mixed
            if last:
                outs.append(final_norm(h2d, y2d, normf_g, out=outs[-1] if outs else None, row0=row0, total_rows=t))
            else:
                outs.append(h2d + y2d)

        pending = closing = None
        for g in range(groups):
            pa, pb, pg = norm_proj(src, norm1_g[l], w_pad, row0=g * tg, rows=tg)
            oa = moba_attention(pa.reshape(gb, seq, -1), rel_bias)
            prep = tuple(rwkv_prep(pb.reshape(gb, seq, -1), rwkv_mu[l], w0[l], w_lora_up[l], a0[l], a_lora_up[l], g_lora_up[l],
                                   k_k[l], k_a[l], r_k[l]))
            mixed = None
            if pending is not None:
                (oa, prep), mixed = mix(pending, (oa, prep))
            if closing is not None:
                oa, y2d = lax.optimization_barrier((oa, closing[2]))
                close(closing[:2] + (y2d,))
                closing = None
            ob = rwkv_scan(*prep, lnx_g[l], lnx_b[l])
            h2d, xn2 = merge_out(src, oa.reshape(tg, WIDTH), ob.reshape(tg, WIDTH), pg, w_proj_a[l], w_proj_b[l],
                                 w_out[l], norm2_g[l], row0=g * tg)
            idx, gates = peer_route(xn2, peer_wq[l], peer_subkeys[l])
            if mixed is not None:
                idx, y2d = lax.optimization_barrier((idx, mixed[2]))
                closing = mixed[:2] + (y2d,)
            pending = (g * tg, h2d, peer_expert_dots(xn2, idx, u_packed), gates, idx)
        if closing is not None:
            close(closing)
        close(mix(pending)[1])
        src = outs[-1] if last else jnp.concatenate(outs, axis=0)
    return src.reshape(bsz, seq, d)
```

```python
import functools
import math

import jax
import jax.numpy as jnp
from jax import lax
from jax.experimental import pallas as pl
from jax.experimental.pallas import tpu as pltpu

F32 = jnp.float32
BF16 = jnp.bfloat16
HI = lax.Precision.HIGHEST

LANES = 128
HEAD_DIM = 64
HEADS = 8
PAIRS = HEADS // 2
WIDTH = HEADS * HEAD_DIM
MOBA_BLOCK = 256
MOBA_TOPK = 3
MOBA_LO = 64
REL_BUCKETS = 32
REL_MAX_DIST = 128
DECAY_LORA = 64
AAA_LORA = 64
GATE_LORA = 160
GN_EPS = 64e-5
RMS_EPS = 1e-6
NEG = -1e30
RWKV_CHUNK = 64
RWKV_CHUNKS_PER_STEP = 4
COL_A = 3 * WIDTH
COL_B_RAW = 3 * WIDTH + DECAY_LORA + AAA_LORA + GATE_LORA
COL_B = 4 * WIDTH
COL_G_OFF = COL_A + COL_B
VMEM_LIMIT = 56 * 1024 * 1024


def _cparams(sem):
    return pltpu.CompilerParams(dimension_semantics=sem, vmem_limit_bytes=VMEM_LIMIT)


def _norm_proj_kernel(x_ref, g_ref, w_ref, pa_ref, pb_ref, pg_ref, xn_ref, *, ja, jb):
    j = pl.program_id(1)

    @pl.when(j == 0)
    def _():
        x = x_ref[...]
        ms = jnp.mean(x * x, axis=-1, keepdims=True)
        xn_ref[...] = (x * lax.rsqrt(ms + RMS_EPS) * g_ref[...]).astype(xn_ref.dtype)

    res = jnp.dot(xn_ref[...], w_ref[...], preferred_element_type=F32)

    @pl.when(j < ja)
    def _():
        pa_ref[...] = res.astype(pa_ref.dtype)

    @pl.when((j >= ja) & (j < jb))
    def _():
        pb_ref[...] = res

    @pl.when(j >= jb)
    def _():
        pg_ref[...] = res.astype(pg_ref.dtype)


def norm_proj(x2d, g, w, *, row0=0, rows=None, tm=2048, tn=512):
    d = x2d.shape[1]
    t = x2d.shape[0] if rows is None else rows
    n = w.shape[1]
    r0 = row0 // tm
    ja, jb, jn = COL_A // tn, COL_G_OFF // tn, n // tn
    return pl.pallas_call(
        functools.partial(_norm_proj_kernel, ja=ja, jb=jb),
        grid=(t // tm, jn),
        in_specs=[
            pl.BlockSpec((tm, d), lambda i, j: (r0 + i, 0)),
            pl.BlockSpec((1, d), lambda i, j: (0, 0)),
            pl.BlockSpec((d, tn), lambda i, j: (0, j)),
        ],
        out_specs=[
            pl.BlockSpec((tm, tn), lambda i, j: (i, jnp.minimum(j, ja - 1))),
            pl.BlockSpec((tm, tn), lambda i, j: (i, jnp.clip(j - ja, 0, jb - ja - 1))),
            pl.BlockSpec((tm, tn), lambda i, j: (i, jnp.maximum(j - jb, 0))),
        ],
        out_shape=[jax.ShapeDtypeStruct((t, COL_A), BF16), jax.ShapeDtypeStruct((t, COL_B), F32),
                   jax.ShapeDtypeStruct((t, n - COL_G_OFF), BF16)],
        scratch_shapes=[pltpu.VMEM((tm, d), w.dtype)],
        compiler_params=_cparams(("parallel", "arbitrary")),
        name="norm_proj",
    )(x2d, g.reshape(1, d), w)


def _rel_bucket(dist):
    n = jnp.maximum(dist, 0)
    max_exact = REL_BUCKETS // 2
    nf = jnp.maximum(n, 1).astype(F32)
    large = max_exact + (jnp.log(nf / max_exact) / math.log(REL_MAX_DIST / max_exact)
                         * (REL_BUCKETS - max_exact)).astype(jnp.int32)
    large = jnp.minimum(large, REL_BUCKETS - 1)
    return jnp.where(n < max_exact, n, large)


def _moba_kernel(q_ref, k_ref, v_ref, bown_ref, bprev_ref, bfar_ref, o_ref,
                 kb_ref, vb_ref, kbar_ref, *, n_blocks):
    qb = pl.program_id(2)
    blk = MOBA_BLOCK
    scale = 1.0 / math.sqrt(HEAD_DIM)

    rows2 = 2 * blk
    nt = (((1,), (1,)), ((), ()))

    @pl.when(qb == 0)
    def _():
        kbar_ref[...] = jnp.zeros_like(kbar_ref)
        lane_b = lax.broadcasted_iota(jnp.int32, (blk, LANES), 1)
        for n in range(n_blocks):
            kblk = k_ref[0, n * blk:(n + 1) * blk, :]
            kbar_ref[n:n + 1, :] = jnp.mean(kblk.astype(F32), axis=0, keepdims=True)
            kb_ref[n * blk:(n + 1) * blk, 0:LANES] = kblk.astype(BF16)
            kb_ref[n * blk:(n + 1) * blk, LANES:] = ((lane_b == n) | (lane_b == MOBA_LO + n)).astype(BF16)
        vb_ref[...] = v_ref[0].astype(BF16)

    q2 = q_ref[0].astype(F32)
    first = lax.broadcasted_iota(jnp.int32, (blk, LANES), 1) < HEAD_DIM
    qh = jnp.concatenate([jnp.where(first, q2, 0.0), jnp.where(first, 0.0, q2)], axis=0)
    lane = lax.broadcasted_iota(jnp.int32, (rows2, LANES), 1)
    rowi = lax.broadcasted_iota(jnp.int32, (rows2, LANES), 0)
    gate = lax.dot_general(qh.astype(BF16), kbar_ref[...].astype(BF16), nt, preferred_element_type=F32)
    g = jnp.where(lane < qb, gate, -jnp.inf)
    chosen = lane < 0
    lane_f = lane.astype(F32)
    for _ in range(MOBA_TOPK):
        m = jnp.max(g, axis=1, keepdims=True)
        idx = jnp.min(jnp.where(g == m, lane_f, float(LANES)), axis=1, keepdims=True)
        hit = (lane_f == idx) & (m > -jnp.inf)
        chosen = chosen | hit
        g = jnp.where(hit, -jnp.inf, g)
    nfar = qb - 1
    bfar = jnp.where(rowi < blk, bfar_ref[0, 0:1, 0:1], bfar_ref[1, 0:1, 0:1])
    bhi = bfar.astype(BF16).astype(F32)
    madd = jnp.where(lane < nfar, jnp.where(chosen, bhi, NEG),
                     jnp.where(lane == nfar, jnp.where(chosen, 0.0, NEG),
                               jnp.where((lane >= MOBA_LO) & (lane - MOBA_LO < nfar), bfar - bhi, 0.0)))
    q_aug = jnp.concatenate([(qh * scale).astype(BF16), madd.astype(BF16)], axis=1)

    prev0 = pl.multiple_of(jnp.maximum(nfar, 0) * blk, blk)
    own0 = pl.multiple_of(qb * blk, blk)
    s_prev = (lax.dot_general(q_aug, kb_ref[pl.ds(prev0, blk), :], nt, preferred_element_type=F32)
              + bprev_ref[...].reshape(rows2, blk) + jnp.where(qb > 0, 0.0, NEG))
    s_own = (lax.dot_general(q_aug, kb_ref[pl.ds(own0, blk), :], nt, preferred_element_type=F32)
             + bown_ref[...].reshape(rows2, blk))
    r = lax.broadcasted_iota(jnp.int32, (rows2, blk), 0)
    c = lax.broadcasted_iota(jnp.int32, (rows2, blk), 1)
    s_own = jnp.where(lax.bitwise_and(r, blk - 1) >= c, s_own, NEG)
    s = jnp.concatenate([s_prev, s_own], axis=1)
    m_i = jnp.max(s, axis=1, keepdims=True)
    p = jnp.exp(s - m_i)
    l_i = jnp.sum(p, axis=1, keepdims=True)
    v0 = jnp.concatenate([vb_ref[pl.ds(prev0, blk), :], vb_ref[pl.ds(own0, blk), :]], axis=0)
    acc = jnp.dot(p.astype(BF16), v0, preferred_element_type=F32)

    def body(it, carry):
        m_i, l_i, acc = carry
        k0 = pl.multiple_of(it * rows2, rows2)
        s = lax.dot_general(q_aug, kb_ref[pl.ds(k0, rows2), :], nt, preferred_element_type=F32)
        tail = jnp.where(2 * it + 1 < nfar, 0.0, NEG)
        s = jnp.concatenate([s[:, :blk], s[:, blk:] + tail], axis=1)
        m_new = jnp.maximum(m_i, jnp.max(s, axis=1, keepdims=True))
        alpha = jnp.exp(m_i - m_new)
        p = jnp.exp(s - m_new)
        l_new = alpha * l_i + jnp.sum(p, axis=1, keepdims=True)
        acc_new = alpha * acc + jnp.dot(p.astype(BF16), vb_ref[pl.ds(k0, rows2), :], preferred_element_type=F32)
        return m_new, l_new, acc_new

    m_i, l_i, acc = lax.fori_loop(0, (jnp.maximum(nfar, 0) + 1) // 2, body, (m_i, l_i, acc))
    out = acc / l_i
    o_ref[0] = jnp.where(first, out[:blk], out[blk:]).astype(o_ref.dtype)


def moba_attention(p3d, rel_bias):
    bsz, seq, _ = p3d.shape
    blk = MOBA_BLOCK
    n_blocks = seq // blk
    assert n_blocks <= MOBA_LO and seq % blk == 0
    span = 2 * blk
    by_dist = rel_bias[:, _rel_bucket(jnp.arange(span))].astype(F32)
    shift = jnp.arange(span)

    def toeplitz(c):
        k = jnp.where(shift < blk, shift, shift - span)
        s = by_dist[:, jnp.clip(c - k, 0, span - 1)]
        tiled = jnp.tile(s, (1, blk))[:, :blk * (span - 1)]
        return tiled.reshape(HEADS, blk, span - 1)[:, :, :blk]

    bias_own = toeplitz(0)
    bias_prev = toeplitz(blk)
    bias_far = jnp.broadcast_to(rel_bias[:, REL_BUCKETS - 1].astype(F32)[:, None, None], (HEADS, 8, LANES))
    kern = functools.partial(_moba_kernel, n_blocks=n_blocks)
    return pl.pallas_call(
        kern,
        grid=(bsz, PAIRS, n_blocks),
        in_specs=[
            pl.BlockSpec((1, blk, LANES), lambda b, h, i: (b, i, h)),
            pl.BlockSpec((1, seq, LANES), lambda b, h, i: (b, 0, PAIRS + h)),
            pl.BlockSpec((1, seq, LANES), lambda b, h, i: (b, 0, 2 * PAIRS + h)),
            pl.BlockSpec((2, blk, blk), lambda b, h, i: (h, 0, 0)),
            pl.BlockSpec((2, blk, blk), lambda b, h, i: (h, 0, 0)),
            pl.BlockSpec((2, 8, LANES), lambda b, h, i: (h, 0, 0)),
        ],
        out_specs=pl.BlockSpec((1, blk, LANES), lambda b, h, i: (b, i, h)),
        out_shape=jax.ShapeDtypeStruct((bsz, seq, WIDTH), BF16),
        scratch_shapes=[
            pltpu.VMEM((seq, 2 * LANES), BF16),
            pltpu.VMEM((seq, LANES), BF16),
            pltpu.VMEM((LANES, LANES), F32),
        ],
        compiler_params=_cparams(("parallel", "parallel", "arbitrary")),
        name="moba",
    )(p3d, p3d, p3d, bias_own, bias_prev, bias_far)


def _shifted(x, carry_row):
    rows = lax.broadcasted_iota(jnp.int32, x.shape, 0)
    return jnp.where(rows == 0, carry_row, pltpu.roll(x, 1, axis=0))


def _rwkv_prep_kernel(pr_ref, pk_ref, pv_ref, pl_ref, mu_ref, vec_ref, ww_ref, wa_ref, wg_ref,
                      bd_ref, tri_ref,
                      rt_ref, kt_ref, kd_ref, bd_out_ref, v_ref, g_ref, bonus_ref, pend_ref,
                      carry_ref, *, chunk):
    @pl.when(pl.program_id(1) == 0)
    def _():
        carry_ref[...] = jnp.zeros_like(carry_ref)

    def mix(ref, j):
        x = ref[0]
        mu = mu_ref[0:1, j * WIDTH:(j + 1) * WIDTH]
        prev = _shifted(x, carry_ref[0:1, j * WIDTH:(j + 1) * WIDTH])
        carry_ref[0:1, j * WIDTH:(j + 1) * WIDTH] = x[x.shape[0] - 1:, :]
        return x + mu * (prev - x)

    r = mix(pr_ref, 0)
    k = mix(pk_ref, 1)
    v = mix(pv_ref, 2)
    lo = mix(pl_ref, 3)
    w0, a0, k_k, k_a, r_k = (vec_ref[i:i + 1, :] for i in range(5))
    xwa = lo[:, 0:LANES]
    xg = lo[:, LANES:3 * LANES]
    lw = jnp.dot(jnp.tanh(xwa), ww_ref[...], precision=HI, preferred_element_type=F32)
    la = jnp.dot(xwa, wa_ref[...], precision=HI, preferred_element_type=F32)
    g = jnp.dot(jax.nn.sigmoid(xg), wg_ref[...], precision=HI, preferred_element_type=F32)
    z = -(w0 + lw)
    softplus = jnp.maximum(z, 0.0) + jnp.log(1.0 + jnp.exp(-jnp.abs(z)))
    logw = -jnp.exp(-softplus - 0.5)
    a = jax.nn.sigmoid(a0 + la)
    kk = k * k_k
    ss = jnp.dot(kk * kk, bd_ref[...], precision=HI, preferred_element_type=F32)
    kk = kk / jnp.maximum(jnp.sqrt(ss), 1e-12)
    k2 = k * (1.0 + (a - 1.0) * k_a)
    rk = jnp.dot(r * k2 * r_k, bd_ref[...], precision=HI, preferred_element_type=F32)
    cs = jnp.dot(tri_ref[...], logw, precision=HI, preferred_element_type=F32)
    e_pos = jnp.exp(cs)
    e_neg = jnp.exp(-cs)
    rt_ref[0] = (r * e_pos).astype(rt_ref.dtype)
    kt_ref[0] = (kk * jnp.exp(cs - logw)).astype(kt_ref.dtype)
    kd_ref[0] = (k2 * e_neg).astype(kd_ref.dtype)
    bd_out_ref[0] = (kk * a * e_neg).astype(bd_out_ref.dtype)
    v_ref[0] = v.astype(v_ref.dtype)
    g_ref[0] = g
    bonus_ref[0] = rk * v
    ts = e_pos.shape[0]
    for c in range(ts // chunk):
        pend_ref[0, c:c + 1, :] = e_pos[(c + 1) * chunk - 1:(c + 1) * chunk, :]


def rwkv_prep(p3d, rwkv_mu, w0, w_lora_up, a0, a_lora_up, g_lora_up, k_k, k_a, r_k, *, ts=512):
    bsz, seq, _ = p3d.shape
    chunk = RWKV_CHUNK
    ts = min(ts, seq)
    mu = jnp.pad(rwkv_mu, (0, COL_B - COL_B_RAW)).reshape(1, COL_B)
    vec = jnp.stack([w0, a0, k_k, k_a, r_k.reshape(-1)] + [jnp.zeros_like(w0)] * 3).astype(F32)
    ww = jnp.zeros((LANES, WIDTH), F32).at[:DECAY_LORA].set(w_lora_up)
    wa = jnp.zeros((LANES, WIDTH), F32).at[DECAY_LORA:DECAY_LORA + AAA_LORA].set(a_lora_up)
    wg = jnp.zeros((2 * LANES, WIDTH), F32).at[:GATE_LORA].set(g_lora_up)
    hid = jnp.arange(WIDTH) // HEAD_DIM
    bd = (hid[:, None] == hid[None, :]).astype(F32)
    tix = jnp.arange(ts)
    tri = ((tix[:, None] // chunk == tix[None, :] // chunk) & (tix[None, :] <= tix[:, None])).astype(F32)
    c0 = 0
    big = jax.ShapeDtypeStruct((bsz, seq, WIDTH), F32)
    wspec = lambda shape: pl.BlockSpec(shape, lambda b, i: (0, 0))
    ospec = pl.BlockSpec((1, ts, WIDTH), lambda b, i: (b, i, 0))
    return pl.pallas_call(
        functools.partial(_rwkv_prep_kernel, chunk=chunk),
        grid=(bsz, seq // ts),
        in_specs=[
            pl.BlockSpec((1, ts, WIDTH), lambda b, i: (b, i, c0)),
            pl.BlockSpec((1, ts, WIDTH), lambda b, i: (b, i, c0 + 1)),
            pl.BlockSpec((1, ts, WIDTH), lambda b, i: (b, i, c0 + 2)),
            pl.BlockSpec((1, ts, WIDTH), lambda b, i: (b, i, c0 + 3)),
            wspec((1, COL_B)), wspec((8, WIDTH)), wspec((LANES, WIDTH)), wspec((LANES, WIDTH)),
            wspec((2 * LANES, WIDTH)), wspec((WIDTH, WIDTH)), wspec((ts, ts)),
        ],
        out_specs=[ospec] * 7 + [pl.BlockSpec((1, ts // chunk, WIDTH), lambda b, i: (b, i, 0))],
        out_shape=[jax.ShapeDtypeStruct((bsz, seq, WIDTH), BF16)] * 5 + [big] * 2
        + [jax.ShapeDtypeStruct((bsz, seq // chunk, WIDTH), F32)],
        scratch_shapes=[pltpu.VMEM((8, COL_B), F32)],
        compiler_params=_cparams(("parallel", "arbitrary")),
        name="rwkv_prep",
    )(p3d, p3d, p3d, p3d, mu, vec, ww, wa, wg, bd, tri)


def _rwkv_scan_kernel(rt_ref, kt_ref, kd_ref, bd_ref, v_ref, g_ref, bonus_ref, pend_ref, ln_ref, o_ref,
                      state_ref, *, chunk, cps, prec):
    @pl.when(pl.program_id(1) == 0)
    def _():
        state_ref[...] = jnp.zeros_like(state_ref)

    c2 = 2 * chunk
    lane = lax.broadcasted_iota(jnp.int32, (chunk, LANES), 1)
    first = lane < HEAD_DIM
    row = lax.broadcasted_iota(jnp.int32, (c2, c2), 0)
    col = lax.broadcasted_iota(jnp.int32, (c2, c2), 1)
    eye = (row == col).astype(F32)
    hrow = lax.broadcasted_iota(jnp.int32, (LANES, LANES), 0) // HEAD_DIM
    hcol = lax.broadcasted_iota(jnp.int32, (LANES, LANES), 1) // HEAD_DIM
    head_mean = jnp.where(hrow == hcol, 1.0 / HEAD_DIM, 0.0).astype(F32)
    nt = (((1,), (1,)), ((), ()))
    tn = (((0,), (0,)), ((), ()))
    dot = functools.partial(jnp.dot, precision=prec, preferred_element_type=F32)
    dotg = functools.partial(lax.dot_general, precision=prec, preferred_element_type=F32)

    def stack(x):
        return jnp.concatenate([jnp.where(first, x, 0.0), jnp.where(first, 0.0, x)], axis=0)

    pairs = range(PAIRS)
    units = [(j, hp) for j in range(cps) for hp in pairs]
    sls = [slice(hp * LANES, (hp + 1) * LANES) for hp in pairs]
    rows_of = [slice(j * chunk, (j + 1) * chunk) for j in range(cps)]
    rs, ks, kds, bs, vs = ({(j, hp): stack(ref[0, rows_of[j], sls[hp]].astype(F32)) for j, hp in units}
                           for ref in (rt_ref, kt_ref, kd_ref, bd_ref, v_ref))
    big = {u: dotg(jnp.concatenate([ks[u], rs[u]], axis=0), jnp.concatenate([bs[u], kds[u]], axis=0), nt)
           for u in units}
    a_b = {u: jnp.where(row > col, big[u][0:c2, 0:c2], 0.0) for u in units}
    a_k = {u: jnp.where(row > col, big[u][0:c2, c2:], 0.0) for u in units}
    a_rb = {u: jnp.where(row >= col, big[u][c2:, 0:c2], 0.0) for u in units}
    a_rk = {u: jnp.where(row >= col, big[u][c2:, c2:], 0.0) for u in units}
    av = {u: dot(jnp.concatenate([a_k[u], a_rk[u]], axis=0), vs[u]) for u in units}
    vk = {u: dotg(vs[u], kds[u], tn) for u in units}
    inv = {u: eye - a_b[u] for u in units}
    pw = {u: dot(a_b[u], a_b[u]) for u in units}
    n_sq = int(math.log2(chunk)) - 1
    for lvl in range(n_sq):
        if lvl + 1 < n_sq:
            both = {u: dot(jnp.concatenate([inv[u], pw[u]], axis=0), pw[u]) for u in units}
            inv = {u: inv[u] + both[u][0:c2] for u in units}
            pw = {u: both[u][c2:] for u in units}
        else:
            inv = {u: inv[u] + dot(inv[u], pw[u]) for u in units}
    hts = [state_ref[0, hp] for hp in pairs]
    for j in range(cps):
        kh = [dotg(jnp.concatenate([ks[j, hp], rs[j, hp]], axis=0), hts[hp], nt) for hp in pairs]
        us = [dot(inv[j, hp], kh[hp][0:c2] + av[j, hp][0:c2]) for hp in pairs]
        ub = [dotg(us[hp], bs[j, hp], tn) for hp in pairs]
        au = [dot(a_rb[j, hp], us[hp]) for hp in pairs]
        for hp in pairs:
            sl = sls[hp]
            pend = pend_ref[0, j, 0:1, sl]
            hts[hp] = (hts[hp] + vk[j, hp] - ub[hp]) * pend
            os_ = kh[hp][c2:] + av[j, hp][c2:] - au[hp]
            o = os_[0:chunk] + os_[chunk:]
            mu = jnp.dot(o, head_mean, precision=HI, preferred_element_type=F32)
            d = o - mu
            var = jnp.dot(d * d, head_mean, precision=HI, preferred_element_type=F32)
            on = d * lax.rsqrt(var + GN_EPS) * ln_ref[0:1, sl] + ln_ref[1:2, sl]
            o_ref[0, rows_of[j], sl] = ((on + bonus_ref[0, rows_of[j], sl]) * g_ref[0, rows_of[j], sl]
                                        ).astype(o_ref.dtype)
    for hp in pairs:
        state_ref[0, hp] = hts[hp]


def rwkv_scan(rt, kt, kd, bd, v, g, bonus, pend, lnx_g, lnx_b, *, prec=None):
    bsz, seq, _ = rt.shape
    chunk = RWKV_CHUNK
    n_chunks = seq // chunk
    ln = jnp.stack([lnx_g, lnx_b] + [jnp.zeros_like(lnx_g)] * 6).astype(F32)
    pend4 = pend.reshape(bsz, n_chunks, 1, WIDTH)
    cps = RWKV_CHUNKS_PER_STEP if n_chunks % RWKV_CHUNKS_PER_STEP == 0 else 1
    spec = pl.BlockSpec((1, cps * chunk, WIDTH), lambda b, c: (b, c, 0))
    return pl.pallas_call(
        functools.partial(_rwkv_scan_kernel, chunk=chunk, cps=cps, prec=prec),
        grid=(bsz, n_chunks // cps),
        in_specs=[spec] * 7 + [
            pl.BlockSpec((1, cps, 1, WIDTH), lambda b, c: (b, c, 0, 0)),
            pl.BlockSpec((8, WIDTH), lambda b, c: (0, 0)),
        ],
        out_specs=spec,
        out_shape=jax.ShapeDtypeStruct((bsz, seq, WIDTH), BF16),
        scratch_shapes=[pltpu.VMEM((1, PAIRS, LANES, LANES), F32)],
        compiler_params=_cparams(("parallel", "arbitrary")),
        name="rwkv_scan",
    )(rt, kt, kd, bd, v, g, bonus, pend4, ln)


def _merge_kernel(x_ref, oa_ref, ob_ref, ga_ref, gb_ref, wa_ref, wb_ref, wo_ref, g2_ref,
                  h_ref, xn_ref, acc_ref):
    j = pl.program_id(1)

    @pl.when(j == 0)
    def _():
        acc_ref[...] = x_ref[...]

    ya = jnp.dot(oa_ref[...].astype(BF16), wa_ref[...], preferred_element_type=F32)
    yb = jnp.dot(ob_ref[...].astype(BF16), wb_ref[...], preferred_element_type=F32)
    y = jax.nn.sigmoid(ga_ref[...].astype(F32)) * ya + jax.nn.sigmoid(gb_ref[...].astype(F32)) * yb
    acc_ref[...] += jnp.dot(y.astype(BF16), wo_ref[...], preferred_element_type=F32)

    @pl.when(j == pl.num_programs(1) - 1)
    def _():
        h = acc_ref[...]
        h_ref[...] = h
        ms = jnp.mean(h * h, axis=-1, keepdims=True)
        xn_ref[...] = _pack_halves(h * lax.rsqrt(ms + RMS_EPS) * g2_ref[...])


def _pack_halves(x):
    half = x.shape[1] // 2
    lo = lax.bitcast_convert_type(x[:, :half].astype(BF16).astype(F32), jnp.int32)
    hi = lax.bitcast_convert_type(x[:, half:].astype(BF16).astype(F32), jnp.int32)
    return lax.bitwise_or(lax.shift_right_logical(lo, jnp.int32(16)), hi)


def _unpack_halves(words):
    lo, hi = _unpack_words(words)
    return jnp.concatenate([lo, hi], axis=1)


def merge_out(x2d, oa, ob, p2d, w_proj_a, w_proj_b, w_out, norm2_g, *, row0=0, tm=512):
    t, d = oa.shape[0], x2d.shape[1]
    r0 = row0 // tm
    tn = WIDTH
    nj = d // tn
    g0 = 0
    return pl.pallas_call(
        _merge_kernel,
        grid=(t // tm, nj),
        in_specs=[
            pl.BlockSpec((tm, d), lambda i, j: (r0 + i, 0)),
            pl.BlockSpec((tm, WIDTH), lambda i, j: (i, 0)),
            pl.BlockSpec((tm, WIDTH), lambda i, j: (i, 0)),
            pl.BlockSpec((tm, tn), lambda i, j: (i, g0 + j)),
            pl.BlockSpec((tm, tn), lambda i, j: (i, g0 + nj + j)),
            pl.BlockSpec((WIDTH, tn), lambda i, j: (0, j)),
            pl.BlockSpec((WIDTH, tn), lambda i, j: (0, j)),
            pl.BlockSpec((tn, d), lambda i, j: (j, 0)),
            pl.BlockSpec((1, d), lambda i, j: (0, 0)),
        ],
        out_specs=[pl.BlockSpec((tm, d), lambda i, j: (i, 0)), pl.BlockSpec((tm, d // 2), lambda i, j: (i, 0))],
        out_shape=[jax.ShapeDtypeStruct((t, d), F32), jax.ShapeDtypeStruct((t, d // 2), jnp.int32)],
        scratch_shapes=[pltpu.VMEM((tm, d), F32)],
        compiler_params=_cparams(("parallel", "arbitrary")),
        name="merge_out",
    )(x2d, oa, ob, p2d, p2d, w_proj_a.astype(BF16), w_proj_b.astype(BF16), w_out.astype(BF16),
      norm2_g.reshape(1, d))


PEER_HEADS = 8
PEER_NKEYS = 128
PEER_TOPK = 16
PEER_HALF = 128


def _topk_rows(s, k):
    n = s.shape[0]
    rows = lax.broadcasted_iota(jnp.int32, s.shape, 0).astype(F32)
    vals, ids = [], []
    for _ in range(k):
        m = jnp.max(s, axis=0, keepdims=True)
        first = jnp.min(jnp.where(s == m, rows, float(n)), axis=0, keepdims=True)
        vals.append(m)
        ids.append(first)
        s = jnp.where(rows == first, -jnp.inf, s)
    return jnp.concatenate(vals, axis=0), jnp.concatenate(ids, axis=0)


def _take_rows(table, ids):
    rows = lax.broadcasted_iota(jnp.int32, table.shape, 0).astype(F32)
    return jnp.sum(jnp.where(rows == ids, table, 0.0), axis=0, keepdims=True)


def _peer_route_kernel(xn_ref, wq_ref, sk_ref, idx_ref, gate_ref, *, prec):
    tt = xn_ref.shape[0]
    k = PEER_TOPK
    xn = _unpack_halves(xn_ref[...]) if xn_ref.dtype == jnp.int32 else xn_ref[...]
    q = jnp.dot(xn.astype(wq_ref.dtype), wq_ref[...], precision=prec, preferred_element_type=F32)
    nt = (((1,), (1,)), ((), ()))
    idx_rows, gate_rows = [], []
    half = k // 2
    for h in range(PEER_HEADS):
        tops = []
        for p in range(2):
            c0 = (h * 2 + p) * PEER_HALF
            s = lax.dot_general(sk_ref[h, p].astype(wq_ref.dtype), q[:, c0:c0 + PEER_HALF].astype(wq_ref.dtype),
                                nt, precision=prec, preferred_element_type=F32)
            tops.append(_topk_rows(s, k))
        (s0, i0), (s1, i1) = tops
        cs = [s0[0:1] + s1] + [s0[i:i + 1] + s1[0:half] for i in range(1, half)] + [s0[half:] + s1[0:1]]
        best_s, pos = _topk_rows(jnp.concatenate(cs, axis=0), k)
        mid = jnp.floor((pos - k) * (1.0 / half))
        end_mid = float(k + (half - 1) * half)
        i_rank = jnp.where(pos < k, 0.0, jnp.where(pos < end_mid, 1.0 + mid, pos - (end_mid - half)))
        j_rank = jnp.where(pos < k, pos, jnp.where(pos < end_mid, (pos - k) - half * mid, 0.0))
        ids = [_take_rows(i0, i_rank[n:n + 1]) * PEER_NKEYS + _take_rows(i1, j_rank[n:n + 1]) for n in range(k)]
        e = jnp.exp(best_s - best_s[0:1])
        gate_rows.append(e / jnp.sum(e, axis=0, keepdims=True))
        idx_rows.append(jnp.concatenate(ids, axis=0).astype(jnp.int32))
    idx_ref[...] = jnp.concatenate(idx_rows, axis=0).T
    gate_ref[...] = jnp.concatenate(gate_rows, axis=0).T


def peer_route(xn2d, peer_wq, peer_subkeys, *, tt=256, prec=None, wdtype=BF16):
    t, dx = xn2d.shape
    d, nq = peer_wq.shape
    n_sel = PEER_HEADS * PEER_TOPK
    return pl.pallas_call(
        functools.partial(_peer_route_kernel, prec=prec),
        grid=(t // tt,),
        in_specs=[
            pl.BlockSpec((tt, dx), lambda i: (i, 0)),
            pl.BlockSpec((d, nq), lambda i: (0, 0)),
            pl.BlockSpec((PEER_HEADS, 2, PEER_NKEYS, PEER_HALF), lambda i: (0, 0, 0, 0)),
        ],
        out_specs=[pl.BlockSpec((tt, n_sel), lambda i: (i, 0))] * 2,
        out_shape=[jax.ShapeDtypeStruct((t, n_sel), jnp.int32), jax.ShapeDtypeStruct((t, n_sel), F32)],
        compiler_params=_cparams(("parallel",)),
        name="peer_route",
    )(xn2d, peer_wq.astype(wdtype), peer_subkeys)


def _final_kernel(h_ref, y_ref, g_ref, *rest):
    o_ref = rest[-1]
    h = h_ref[...] + y_ref[...]
    ms = jnp.mean(h * h, axis=-1, keepdims=True)
    o_ref[...] = h * lax.rsqrt(ms + RMS_EPS) * g_ref[...]


def final_norm(h2d, y2d, g, *, out=None, row0=0, total_rows=None, tm=1024):
    t, d = h2d.shape
    total = t if total_rows is None else total_rows
    r0 = row0 // tm
    spec = pl.BlockSpec((tm, d), lambda i: (i, 0))
    in_specs = [spec, spec, pl.BlockSpec((1, d), lambda i: (0, 0))]
    args = [h2d, y2d, g.reshape(1, d)]
    aliases = {}
    if out is not None:
        in_specs.append(pl.BlockSpec(memory_space=pl.ANY))
        args.append(out)
        aliases = {3: 0}
    return pl.pallas_call(
        _final_kernel,
        grid=(t // tm,),
        in_specs=in_specs,
        out_specs=pl.BlockSpec((tm, d), lambda i: (r0 + i, 0)),
        out_shape=jax.ShapeDtypeStruct((total, d), F32),
        input_output_aliases=aliases,
        compiler_params=_cparams(("parallel",)),
        name="final_norm",
    )(*args)


SC_CORES = 2
SC_SUBCORES = 16
SC_LANES = 16
SC_WORKERS = SC_CORES * SC_SUBCORES
PEER_SEL = PEER_HEADS * PEER_TOPK
PEER_ROWS = 32
PEER_PARTS = PEER_SEL // PEER_ROWS
PEER_NBUF = 4
PEER_GROUP = 64
PEER_BF16_RUN = 4


def _pack_rows_kernel(w_ref, o_ref):
    o_ref[...] = _pack_halves(w_ref[...])


def _pack_rows(w, *, tr=1024):
    e, d = w.shape
    return pl.pallas_call(
        _pack_rows_kernel,
        grid=(e // tr,),
        in_specs=[pl.BlockSpec((tr, d), lambda i: (i, 0))],
        out_specs=pl.BlockSpec((tr, d // 2), lambda i: (i, 0)),
        out_shape=jax.ShapeDtypeStruct((e, d // 2), jnp.int32),
        compiler_params=_cparams(("parallel",)),
        name="pack_rows",
    )(w)


def _unpack_words(w):
    lo = lax.bitcast_convert_type(lax.shift_left(w, jnp.int32(16)), F32)
    hi = lax.bitcast_convert_type(lax.bitwise_and(w, jnp.int32(-65536)), F32)
    return lo, hi


def _packed_dot(a_words, b_words):
    from jax.experimental.pallas import tpu_sc as plsc
    prods = [plsc.bitcast(a, BF16) * plsc.bitcast(b, BF16) for a, b in zip(a_words, b_words)]
    while len(prods) > 1:
        prods = [prods[k] + prods[k + 1] for k in range(0, len(prods), 2)]
    return _unpack_words(plsc.bitcast(prods[0], jnp.int32))


def _sc_mesh():
    from jax.experimental.pallas import tpu_sc as plsc
    return plsc.VectorSubcoreMesh(core_axis_name="c", subcore_axis_name="s",
                                  num_cores=SC_CORES, num_subcores=SC_SUBCORES)


def _sc_loop(n, body, carry):
    from jax.experimental.pallas import tpu_sc as plsc
    return plsc.parallel_loop(0, n, carry=carry)(body)


def _worker_base(tokens_per_worker):
    return (lax.axis_index("s") * SC_CORES + lax.axis_index("c")) * tokens_per_worker


def _gather_compute_loop(table_hbm, idx_v, rows_v, sem, stage_v, out_row, osem, grp, compute):
    n_gathers = PEER_PARTS * grp
    ahead = PEER_NBUF - 1

    def gather(j, b):
        i = j // PEER_PARTS if isinstance(j, int) else lax.shift_right_logical(j, PEER_PARTS.bit_length() - 1)
        h = j % PEER_PARTS if isinstance(j, int) else lax.bitwise_and(j, PEER_PARTS - 1)
        ids = idx_v.at[i, pl.ds(pl.multiple_of(h * PEER_ROWS, PEER_ROWS), PEER_ROWS)]
        return pltpu.make_async_copy(table_hbm.at[ids], rows_v.at[b], sem.at[b])

    def put(i, slot):
        return pltpu.make_async_copy(stage_v.at[slot], out_row(i), osem.at[slot])

    for j in range(ahead):
        gather(j, j).start()

    @pl.loop(0, n_gathers)
    def _(j):
        b = lax.bitwise_and(j, PEER_NBUF - 1)
        h = lax.bitwise_and(j, PEER_PARTS - 1)
        i = lax.shift_right_logical(j, PEER_PARTS.bit_length() - 1)
        slot = lax.bitwise_and(i, 1)

        @pl.when((h == 0) & (i >= 2))
        def _():
            put(i - 2, slot).wait()

        @pl.when(j + ahead < n_gathers)
        def _():
            gather(j + ahead, lax.bitwise_and(j + ahead, PEER_NBUF - 1)).start()

        gather(j, b).wait()
        compute(i, h, b, slot)

        @pl.when(h == PEER_PARTS - 1)
        def _():
            put(i, slot).start()

    put(grp - 2, 0).wait()
    put(grp - 1, 1).wait()


def peer_expert_dots(x_packed, idx, u_packed):
    t, half = x_packed.shape
    n_chunks = half // SC_LANES
    tpw = t // SC_WORKERS
    igrp = min(2 * PEER_GROUP, tpw)
    grp = min(PEER_GROUP, igrp)
    assert t % SC_WORKERS == 0 and tpw % igrp == 0 and igrp % grp == 0 and grp & (grp - 1) == 0
    assert idx.shape == (t, PEER_SEL)
    rows_tog = 8

    def body(x_hbm, idx_hbm, u_hbm, out_hbm, idx_v, x_v, rows_v, ps_v, sem, osem):
        base = _worker_base(tpw)

        def compute(t0, i, h, b, slot):
            @pl.when((h == 0) & (lax.bitwise_and(i, grp - 1) == 0))
            def _():
                pltpu.sync_copy(x_hbm.at[pl.ds(pl.multiple_of(t0 + i, grp), grp)], x_v)

            ix = lax.bitwise_and(i, grp - 1)

            @pl.loop(0, PEER_ROWS // rows_tog)
            def _(rg):
                r0 = rg * rows_tog
                accs = [[None, None] for _ in range(rows_tog)]
                for c0 in range(0, n_chunks, PEER_BF16_RUN):
                    ats = [pl.ds((c0 + k) * SC_LANES, SC_LANES) for k in range(PEER_BF16_RUN)]
                    xw = [x_v[ix, at] for at in ats]
                    for r in range(rows_tog):
                        terms = _packed_dot([rows_v[b, r0 + r, at] for at in ats], xw)
                        for k, term in enumerate(terms):
                            accs[r][k] = term if accs[r][k] is None else accs[r][k] + term
                for r in range(rows_tog):
                    at = pl.ds(pl.multiple_of((h * PEER_ROWS + r0 + r) * SC_LANES, SC_LANES), SC_LANES)
                    ps_v[slot, at] = accs[r][0] + accs[r][1]

        @pl.loop(0, tpw // igrp)
        def _(g):
            t0 = base + g * igrp
            pltpu.sync_copy(idx_hbm.at[pl.ds(t0, igrp)], idx_v)
            _gather_compute_loop(u_hbm, idx_v, rows_v, sem, ps_v, lambda i: out_hbm.at[t0 + i], osem, igrp,
                                 functools.partial(compute, t0))

    return pl.kernel(
        body,
        out_type=jax.ShapeDtypeStruct((t, PEER_SEL * SC_LANES), F32),
        mesh=_sc_mesh(),
        scratch_types=[
            pltpu.VMEM((igrp, PEER_SEL), jnp.int32),
            pltpu.VMEM((grp, half), jnp.int32),
            pltpu.VMEM((PEER_NBUF, PEER_ROWS, half), jnp.int32),
            pltpu.VMEM((2, PEER_SEL * SC_LANES), F32),
            pltpu.SemaphoreType.DMA((PEER_NBUF,)),
            pltpu.SemaphoreType.DMA((2,)),
        ],
        compiler_params=pltpu.CompilerParams(needs_layout_passes=False),
        name="peer_expert_dots",
    )(x_packed, idx, u_packed)


def peer_expert_mix(hgw, idx, v_packed):
    t = hgw.shape[0]
    half = v_packed.shape[1]
    d = 2 * half
    tpw = t // SC_WORKERS
    grp = min(2 * PEER_GROUP, tpw)
    assert t % SC_WORKERS == 0 and tpw % grp == 0 and grp % 2 == 0 and idx.shape == (t, PEER_SEL)
    n_parts = 2
    cpp = half // SC_LANES // n_parts
    from jax.experimental.pallas import tpu_sc as plsc

    def body(hg_hbm, idx_hbm, v_hbm, out_hbm, idx_v, hg_v, rows_v, o_v2, sem, osem):
        base = _worker_base(tpw)

        def compute(i, h, b, slot):
            token = jnp.full((SC_LANES,), i, jnp.int32)
            for part in range(n_parts):
                def rbody(rq, accs):
                    r0 = rq * PEER_BF16_RUN
                    s = [plsc.load_gather(hg_v, [token, jnp.full((SC_LANES,), h * PEER_ROWS + r0 + k, jnp.int32)])
                         for k in range(PEER_BF16_RUN)]
                    new = []
                    for c in range(cpp):
                        at = pl.ds((part * cpp + c) * SC_LANES, SC_LANES)
                        lo, hi = _packed_dot([rows_v[b, r0 + k, at] for k in range(PEER_BF16_RUN)], s)
                        new.append(accs[2 * c] + lo)
                        new.append(accs[2 * c + 1] + hi)
                    return tuple(new)

                accs = _sc_loop(PEER_ROWS // PEER_BF16_RUN, rbody,
                                tuple(jnp.zeros((SC_LANES,), F32) for _ in range(2 * cpp)))
                def store(overwrite):
                    for c in range(cpp):
                        lo_at = pl.ds((part * cpp + c) * SC_LANES, SC_LANES)
                        hi_at = pl.ds(half + (part * cpp + c) * SC_LANES, SC_LANES)
                        if overwrite:
                            o_v2[slot, lo_at] = accs[2 * c]
                            o_v2[slot, hi_at] = accs[2 * c + 1]
                        else:
                            o_v2[slot, lo_at] = o_v2[slot, lo_at] + accs[2 * c]
                            o_v2[slot, hi_at] = o_v2[slot, hi_at] + accs[2 * c + 1]

                pl.when(h == 0)(functools.partial(store, True))
                pl.when(h != 0)(functools.partial(store, False))

        @pl.loop(0, tpw // grp)
        def _(g):
            t0 = base + g * grp
            pltpu.sync_copy(idx_hbm.at[pl.ds(t0, grp)], idx_v)
            pltpu.sync_copy(hg_hbm.at[pl.ds(t0, grp)], hg_v)
            _gather_compute_loop(v_hbm, idx_v, rows_v, sem, o_v2, lambda i: out_hbm.at[t0 + i], osem, grp, compute)

    return pl.kernel(
        body,
        out_type=jax.ShapeDtypeStruct((t, d), F32),
        mesh=_sc_mesh(),
        scratch_types=[
            pltpu.VMEM((grp, PEER_SEL), jnp.int32),
            pltpu.VMEM((grp, PEER_SEL), jnp.int32),
            pltpu.VMEM((PEER_NBUF, PEER_ROWS, half), jnp.int32),
            pltpu.VMEM((2, d), F32),
            pltpu.SemaphoreType.DMA((PEER_NBUF,)),
            pltpu.SemaphoreType.DMA((2,)),
        ],
        compiler_params=pltpu.CompilerParams(needs_layout_passes=False),
        name="peer_expert_mix",
    )(hgw, idx, v_packed)


def _peer_act_kernel(ps_ref, gate_ref, sum_ref, o_ref):
    ps = ps_ref[...]
    sel = sum_ref[...]
    hi = ps.astype(BF16)
    rest = ps - hi.astype(F32)
    mid = rest.astype(BF16)
    lo = (rest - mid.astype(F32)).astype(BF16)
    pre = (jnp.dot(hi, sel, preferred_element_type=F32) + jnp.dot(mid, sel, preferred_element_type=F32)
           + jnp.dot(lo, sel, preferred_element_type=F32))
    hg = 0.5 * pre * (1.0 + lax.erf(pre * (1.0 / math.sqrt(2.0)))) * gate_ref[...]
    bits = lax.bitcast_convert_type(hg.astype(BF16).astype(F32), jnp.int32)
    o_ref[...] = lax.bitwise_or(bits, lax.shift_right_logical(bits, jnp.int32(16)))


def peer_act(ps, gates, *, tm=512):
    t, n = ps.shape
    lane_sum = (jnp.arange(n)[:, None] // SC_LANES == jnp.arange(PEER_SEL)[None, :]).astype(BF16)
    return pl.pallas_call(
        _peer_act_kernel,
        grid=(t // tm,),
        in_specs=[
            pl.BlockSpec((tm, n), lambda i: (i, 0)),
            pl.BlockSpec((tm, PEER_SEL), lambda i: (i, 0)),
            pl.BlockSpec((n, PEER_SEL), lambda i: (0, 0)),
        ],
        out_specs=pl.BlockSpec((tm, PEER_SEL), lambda i: (i, 0)),
        out_shape=jax.ShapeDtypeStruct((t, PEER_SEL), jnp.int32),
        compiler_params=_cparams(("parallel",)),
        name="peer_act",
    )(ps, gates, lane_sum)


BATCH_GROUPS = 8


def kernel(x, norm1_g, w_in, rwkv_mu, w0, w_lora_up, a0, a_lora_up, g_lora_up, k_k, k_a, r_k, lnx_g, lnx_b,
           w_proj_a, w_proj_b, w_out, norm2_g, peer_wq, peer_subkeys, peer_u, peer_v, rel_bias, normf_g):
    bsz, seq, d = x.shape
    depth = norm1_g.shape[0]
    groups = BATCH_GROUPS if bsz % BATCH_GROUPS == 0 else 1
    gb = bsz // groups
    tg = gb * seq
    t = bsz * seq
    src = x.reshape(t, d)
    for l in range(depth):
        w_pad = jnp.concatenate([
            w_in[l][:, :COL_A + COL_B_RAW],
            jnp.zeros((d, COL_B - COL_B_RAW), w_in.dtype),
            w_in[l][:, COL_A + COL_B_RAW:]], axis=1).astype(BF16)
        u_packed = _pack_rows(peer_u[l])
        v_packed = _pack_rows(peer_v[l])
        last = l == depth - 1

        def mix(pending, tie=None):
            row0, h2d, ps, gates, idx = pending
            hgw = peer_act(ps, gates)
            if tie is not None:
                tie, hgw = lax.optimization_barrier((tie, hgw))
            return tie, (row0, h2d, peer_expert_mix(hgw, idx, v_packed))

        outs = []

        def close(mixed):
            row0, h2d, y2d = mixed
            if last:
                outs.append(final_norm(h2d, y2d, normf_g, out=outs[-1] if outs else None, row0=row0, total_rows=t))
            else:
                outs.append(h2d + y2d)

        pending = closing = None
        for g in range(groups):
            pa, pb, pg = norm_proj(src, norm1_g[l], w_pad, row0=g * tg, rows=tg)
            oa = moba_attention(pa.reshape(gb, seq, -1), rel_bias)
            prep = tuple(rwkv_prep(pb.reshape(gb, seq, -1), rwkv_mu[l], w0[l], w_lora_up[l], a0[l], a_lora_up[l], g_lora_up[l],
                                   k_k[l], k_a[l], r_k[l]))
            mixed = None
            if pending is not None:
                (oa, prep), mixed = mix(pending, (oa, prep))
            if closing is not None:
                oa, y2d = lax.optimization_barrier((oa, closing[2]))
                close(closing[:2] + (y2d,))
                closing = None
            ob = rwkv_scan(*prep, lnx_g[l], lnx_b[l])
            h2d, xn2 = merge_out(src, oa.reshape(tg, WIDTH), ob.reshape(tg, WIDTH), pg, w_proj_a[l], w_proj_b[l],
                                 w_out[l], norm2_g[l], row0=g * tg)
            idx, gates = peer_route(xn2, peer_wq[l], peer_subkeys[l])
            if mixed is not None:
                idx, y2d = lax.optimization_barrier((idx, mixed[2]))
                closing = mixed[:2] + (y2d,)
            pending = (g * tg, h2d, peer_expert_dots(xn2, idx, u_packed), gates, idx)
        if closing is not None:
            close(closing)
        close(mix(pending)[1])
        src = outs[-1] if last else jnp.concatenate(outs, axis=0)
    return src.reshape(bsz, seq, d)
```

```python
import functools
import math

import jax
import jax.numpy as jnp
from jax import lax
from jax.experimental import pallas as pl
from jax.experimental.pallas import tpu as pltpu

F32 = jnp.float32
BF16 = jnp.bfloat16
HI = lax.Precision.HIGHEST

LANES = 128
HEAD_DIM = 64
HEADS = 8
PAIRS = HEADS // 2
WIDTH = HEADS * HEAD_DIM
MOBA_BLOCK = 256
MOBA_TOPK = 3
MOBA_LO = 64
REL_BUCKETS = 32
REL_MAX_DIST = 128
DECAY_LORA = 64
AAA_LORA = 64
GATE_LORA = 160
GN_EPS = 64e-5
RMS_EPS = 1e-6
NEG = -1e30
RWKV_CHUNK = 64
RWKV_CHUNKS_PER_STEP = 4
COL_A = 3 * WIDTH
COL_B_RAW = 3 * WIDTH + DECAY_LORA + AAA_LORA + GATE_LORA
COL_B = 4 * WIDTH
COL_G_OFF = COL_A + COL_B
VMEM_LIMIT = 56 * 1024 * 1024


def _cparams(sem):
    return pltpu.CompilerParams(dimension_semantics=sem, vmem_limit_bytes=VMEM_LIMIT)


def _norm_proj_kernel(x_ref, g_ref, w_ref, pa_ref, pb_ref, pg_ref, xn_ref, *, ja, jb):
    j = pl.program_id(1)

    @pl.when(j == 0)
    def _():
        x = x_ref[...]
        ms = jnp.mean(x * x, axis=-1, keepdims=True)
        xn_ref[...] = (x * lax.rsqrt(ms + RMS_EPS) * g_ref[...]).astype(xn_ref.dtype)

    res = jnp.dot(xn_ref[...], w_ref[...], preferred_element_type=F32)

    @pl.when(j < ja)
    def _():
        pa_ref[...] = res.astype(pa_ref.dtype)

    @pl.when((j >= ja) & (j < jb))
    def _():
        pb_ref[...] = res

    @pl.when(j >= jb)
    def _():
        pg_ref[...] = res.astype(pg_ref.dtype)


def norm_proj(x2d, g, w, *, row0=0, rows=None, tm=2048, tn=512):
    d = x2d.shape[1]
    t = x2d.shape[0] if rows is None else rows
    n = w.shape[1]
    r0 = row0 // tm
    ja, jb, jn = COL_A // tn, COL_G_OFF // tn, n // tn
    return pl.pallas_call(
        functools.partial(_norm_proj_kernel, ja=ja, jb=jb),
        grid=(t // tm, jn),
        in_specs=[
            pl.BlockSpec((tm, d), lambda i, j: (r0 + i, 0)),
            pl.BlockSpec((1, d), lambda i, j: (0, 0)),
            pl.BlockSpec((d, tn), lambda i, j: (0, j)),
        ],
        out_specs=[
            pl.BlockSpec((tm, tn), lambda i, j: (i, jnp.minimum(j, ja - 1))),
            pl.BlockSpec((tm, tn), lambda i, j: (i, jnp.clip(j - ja, 0, jb - ja - 1))),
            pl.BlockSpec((tm, tn), lambda i, j: (i, jnp.maximum(j - jb, 0))),
        ],
        out_shape=[jax.ShapeDtypeStruct((t, COL_A), BF16), jax.ShapeDtypeStruct((t, COL_B), F32),
                   jax.ShapeDtypeStruct((t, n - COL_G_OFF), BF16)],
        scratch_shapes=[pltpu.VMEM((tm, d), w.dtype)],
        compiler_params=_cparams(("parallel", "arbitrary")),
        name="norm_proj",
    )(x2d, g.reshape(1, d), w)


def _rel_bucket(dist):
    n = jnp.maximum(dist, 0)
    max_exact = REL_BUCKETS // 2
    nf = jnp.maximum(n, 1).astype(F32)
    large = max_exact + (jnp.log(nf / max_exact) / math.log(REL_MAX_DIST / max_exact)
                         * (REL_BUCKETS - max_exact)).astype(jnp.int32)
    large = jnp.minimum(large, REL_BUCKETS - 1)
    return jnp.where(n < max_exact, n, large)


def _moba_kernel(q_ref, k_ref, v_ref, bown_ref, bprev_ref, bfar_ref, o_ref,
                 kb_ref, vb_ref, kbar_ref, *, n_blocks):
    qb = pl.program_id(2)
    blk = MOBA_BLOCK
    scale = 1.0 / math.sqrt(HEAD_DIM)

    rows2 = 2 * blk
    nt = (((1,), (1,)), ((), ()))

    @pl.when(qb == 0)
    def _():
        kbar_ref[...] = jnp.zeros_like(kbar_ref)
        lane_b = lax.broadcasted_iota(jnp.int32, (blk, LANES), 1)
        for n in range(n_blocks):
            kblk = k_ref[0, n * blk:(n + 1) * blk, :]
            kbar_ref[n:n + 1, :] = jnp.mean(kblk.astype(F32), axis=0, keepdims=True)
            kb_ref[n * blk:(n + 1) * blk, 0:LANES] = kblk.astype(BF16)
            kb_ref[n * blk:(n + 1) * blk, LANES:] = ((lane_b == n) | (lane_b == MOBA_LO + n)).astype(BF16)
        vb_ref[...] = v_ref[0].astype(BF16)

    q2 = q_ref[0].astype(F32)
    first = lax.broadcasted_iota(jnp.int32, (blk, LANES), 1) < HEAD_DIM
    qh = jnp.concatenate([jnp.where(first, q2, 0.0), jnp.where(first, 0.0, q2)], axis=0)
    lane = lax.broadcasted_iota(jnp.int32, (rows2, LANES), 1)
    rowi = lax.broadcasted_iota(jnp.int32, (rows2, LANES), 0)
    gate = lax.dot_general(qh.astype(BF16), kbar_ref[...].astype(BF16), nt, preferred_element_type=F32)
    g = jnp.where(lane < qb, gate, -jnp.inf)
    chosen = lane < 0
    lane_f = lane.astype(F32)
    for _ in range(MOBA_TOPK):
        m = jnp.max(g, axis=1, keepdims=True)
        idx = jnp.min(jnp.where(g == m, lane_f, float(LANES)), axis=1, keepdims=True)
        hit = (lane_f == idx) & (m > -jnp.inf)
        chosen = chosen | hit
        g = jnp.where(hit, -jnp.inf, g)
    nfar = qb - 1
    bfar = jnp.where(rowi < blk, bfar_ref[0, 0:1, 0:1], bfar_ref[1, 0:1, 0:1])
    bhi = bfar.astype(BF16).astype(F32)
    madd = jnp.where(lane < nfar, jnp.where(chosen, bhi, NEG),
                     jnp.where(lane == nfar, jnp.where(chosen, 0.0, NEG),
                               jnp.where((lane >= MOBA_LO) & (lane - MOBA_LO < nfar), bfar - bhi, 0.0)))
    q_aug = jnp.concatenate([(qh * scale).astype(BF16), madd.astype(BF16)], axis=1)

    prev0 = pl.multiple_of(jnp.maximum(nfar, 0) * blk, blk)
    own0 = pl.multiple_of(qb * blk, blk)
    s_prev = (lax.dot_general(q_aug, kb_ref[pl.ds(prev0, blk), :], nt, preferred_element_type=F32)
              + bprev_ref[...].reshape(rows2, blk) + jnp.where(qb > 0, 0.0, NEG))
    s_own = (lax.dot_general(q_aug, kb_ref[pl.ds(own0, blk), :], nt, preferred_element_type=F32)
             + bown_ref[...].reshape(rows2, blk))
    r = lax.broadcasted_iota(jnp.int32, (rows2, blk), 0)
    c = lax.broadcasted_iota(jnp.int32, (rows2, blk), 1)
    s_own = jnp.where(lax.bitwise_and(r, blk - 1) >= c, s_own, NEG)
    s = jnp.concatenate([s_prev, s_own], axis=1)
    m_i = jnp.max(s, axis=1, keepdims=True)
    p = jnp.exp(s - m_i)
    l_i = jnp.sum(p, axis=1, keepdims=True)
    v0 = jnp.concatenate([vb_ref[pl.ds(prev0, blk), :], vb_ref[pl.ds(own0, blk), :]], axis=0)
    acc = jnp.dot(p.astype(BF16), v0, preferred_element_type=F32)

    def body(it, carry):
        m_i, l_i, acc = carry
        k0 = pl.multiple_of(it * rows2, rows2)
        s = lax.dot_general(q_aug, kb_ref[pl.ds(k0, rows2), :], nt, preferred_element_type=F32)
        tail = jnp.where(2 * it + 1 < nfar, 0.0, NEG)
        s = jnp.concatenate([s[:, :blk], s[:, blk:] + tail], axis=1)
        m_new = jnp.maximum(m_i, jnp.max(s, axis=1, keepdims=True))
        alpha = jnp.exp(m_i - m_new)
        p = jnp.exp(s - m_new)
        l_new = alpha * l_i + jnp.sum(p, axis=1, keepdims=True)
        acc_new = alpha * acc + jnp.dot(p.astype(BF16), vb_ref[pl.ds(k0, rows2), :], preferred_element_type=F32)
        return m_new, l_new, acc_new

    m_i, l_i, acc = lax.fori_loop(0, (jnp.maximum(nfar, 0) + 1) // 2, body, (m_i, l_i, acc))
    out = acc / l_i
    o_ref[0] = jnp.where(first, out[:blk], out[blk:]).astype(o_ref.dtype)


def moba_attention(p3d, rel_bias):
    bsz, seq, _ = p3d.shape
    blk = MOBA_BLOCK
    n_blocks = seq // blk
    assert n_blocks <= MOBA_LO and seq % blk == 0
    span = 2 * blk
    by_dist = rel_bias[:, _rel_bucket(jnp.arange(span))].astype(F32)
    shift = jnp.arange(span)

    def toeplitz(c):
        k = jnp.where(shift < blk, shift, shift - span)
        s = by_dist[:, jnp.clip(c - k, 0, span - 1)]
        tiled = jnp.tile(s, (1, blk))[:, :blk * (span - 1)]
        return tiled.reshape(HEADS, blk, span - 1)[:, :, :blk]

    bias_own = toeplitz(0)
    bias_prev = toeplitz(blk)
    bias_far = jnp.broadcast_to(rel_bias[:, REL_BUCKETS - 1].astype(F32)[:, None, None], (HEADS, 8, LANES))
    kern = functools.partial(_moba_kernel, n_blocks=n_blocks)
    return pl.pallas_call(
        kern,
        grid=(bsz, PAIRS, n_blocks),
        in_specs=[
            pl.BlockSpec((1, blk, LANES), lambda b, h, i: (b, i, h)),
            pl.BlockSpec((1, seq, LANES), lambda b, h, i: (b, 0, PAIRS + h)),
            pl.BlockSpec((1, seq, LANES), lambda b, h, i: (b, 0, 2 * PAIRS + h)),
            pl.BlockSpec((2, blk, blk), lambda b, h, i: (h, 0, 0)),
            pl.BlockSpec((2, blk, blk), lambda b, h, i: (h, 0, 0)),
            pl.BlockSpec((2, 8, LANES), lambda b, h, i: (h, 0, 0)),
        ],
        out_specs=pl.BlockSpec((1, blk, LANES), lambda b, h, i: (b, i, h)),
        out_shape=jax.ShapeDtypeStruct((bsz, seq, WIDTH), BF16),
        scratch_shapes=[
            pltpu.VMEM((seq, 2 * LANES), BF16),
            pltpu.VMEM((seq, LANES), BF16),
            pltpu.VMEM((LANES, LANES), F32),
        ],
        compiler_params=_cparams(("parallel", "parallel", "arbitrary")),
        name="moba",
    )(p3d, p3d, p3d, bias_own, bias_prev, bias_far)


def _shifted(x, carry_row):
    rows = lax.broadcasted_iota(jnp.int32, x.shape, 0)
    return jnp.where(rows == 0, carry_row, pltpu.roll(x, 1, axis=0))


def _rwkv_prep_kernel(pr_ref, pk_ref, pv_ref, pl_ref, mu_ref, vec_ref, ww_ref, wa_ref, wg_ref,
                      bd_ref, tri_ref,
                      rt_ref, kt_ref, kd_ref, bd_out_ref, v_ref, g_ref, bonus_ref, pend_ref,
                      carry_ref, *, chunk):
    @pl.when(pl.program_id(1) == 0)
    def _():
        carry_ref[...] = jnp.zeros_like(carry_ref)

    def mix(ref, j):
        x = ref[0]
        mu = mu_ref[0:1, j * WIDTH:(j + 1) * WIDTH]
        prev = _shifted(x, carry_ref[0:1, j * WIDTH:(j + 1) * WIDTH])
        carry_ref[0:1, j * WIDTH:(j + 1) * WIDTH] = x[x.shape[0] - 1:, :]
        return x + mu * (prev - x)

    r = mix(pr_ref, 0)
    k = mix(pk_ref, 1)
    v = mix(pv_ref, 2)
    lo = mix(pl_ref, 3)
    w0, a0, k_k, k_a, r_k = (vec_ref[i:i + 1, :] for i in range(5))
    xwa = lo[:, 0:LANES]
    xg = lo[:, LANES:3 * LANES]
    lw = jnp.dot(jnp.tanh(xwa), ww_ref[...], precision=HI, preferred_element_type=F32)
    la = jnp.dot(xwa, wa_ref[...], precision=HI, preferred_element_type=F32)
    g = jnp.dot(jax.nn.sigmoid(xg), wg_ref[...], precision=HI, preferred_element_type=F32)
    z = -(w0 + lw)
    softplus = jnp.maximum(z, 0.0) + jnp.log(1.0 + jnp.exp(-jnp.abs(z)))
    logw = -jnp.exp(-softplus - 0.5)
    a = jax.nn.sigmoid(a0 + la)
    kk = k * k_k
    ss = jnp.dot(kk * kk, bd_ref[...], precision=HI, preferred_element_type=F32)
    kk = kk / jnp.maximum(jnp.sqrt(ss), 1e-12)
    k2 = k * (1.0 + (a - 1.0) * k_a)
    rk = jnp.dot(r * k2 * r_k, bd_ref[...], precision=HI, preferred_element_type=F32)
    cs = jnp.dot(tri_ref[...], logw, precision=HI, preferred_element_type=F32)
    e_pos = jnp.exp(cs)
    e_neg = jnp.exp(-cs)
    rt_ref[0] = (r * e_pos).astype(rt_ref.dtype)
    kt_ref[0] = (kk * jnp.exp(cs - logw)).astype(kt_ref.dtype)
    kd_ref[0] = (k2 * e_neg).astype(kd_ref.dtype)
    bd_out_ref[0] = (kk * a * e_neg).astype(bd_out_ref.dtype)
    v_ref[0] = v.astype(v_ref.dtype)
    g_ref[0] = g
    bonus_ref[0] = rk * v
    ts = e_pos.shape[0]
    for c in range(ts // chunk):
        pend_ref[0, c:c + 1, :] = e_pos[(c + 1) * chunk - 1:(c + 1) * chunk, :]


def rwkv_prep(p3d, rwkv_mu, w0, w_lora_up, a0, a_lora_up, g_lora_up, k_k, k_a, r_k, *, ts=512):
    bsz, seq, _ = p3d.shape
    chunk = RWKV_CHUNK
    ts = min(ts, seq)
    mu = jnp.pad(rwkv_mu, (0, COL_B - COL_B_RAW)).reshape(1, COL_B)
    vec = jnp.stack([w0, a0, k_k, k_a, r_k.reshape(-1)] + [jnp.zeros_like(w0)] * 3).astype(F32)
    ww = jnp.zeros((LANES, WIDTH), F32).at[:DECAY_LORA].set(w_lora_up)
    wa = jnp.zeros((LANES, WIDTH), F32).at[DECAY_LORA:DECAY_LORA + AAA_LORA].set(a_lora_up)
    wg = jnp.zeros((2 * LANES, WIDTH), F32).at[:GATE_LORA].set(g_lora_up)
    hid = jnp.arange(WIDTH) // HEAD_DIM
    bd = (hid[:, None] == hid[None, :]).astype(F32)
    tix = jnp.arange(ts)
    tri = ((tix[:, None] // chunk == tix[None, :] // chunk) & (tix[None, :] <= tix[:, None])).astype(F32)
    c0 = 0
    big = jax.ShapeDtypeStruct((bsz, seq, WIDTH), F32)
    wspec = lambda shape: pl.BlockSpec(shape, lambda b, i: (0, 0))
    ospec = pl.BlockSpec((1, ts, WIDTH), lambda b, i: (b, i, 0))
    return pl.pallas_call(
        functools.partial(_rwkv_prep_kernel, chunk=chunk),
        grid=(bsz, seq // ts),
        in_specs=[
            pl.BlockSpec((1, ts, WIDTH), lambda b, i: (b, i, c0)),
            pl.BlockSpec((1, ts, WIDTH), lambda b, i: (b, i, c0 + 1)),
            pl.BlockSpec((1, ts, WIDTH), lambda b, i: (b, i, c0 + 2)),
            pl.BlockSpec((1, ts, WIDTH), lambda b, i: (b, i, c0 + 3)),
            wspec((1, COL_B)), wspec((8, WIDTH)), wspec((LANES, WIDTH)), wspec((LANES, WIDTH)),
            wspec((2 * LANES, WIDTH)), wspec((WIDTH, WIDTH)), wspec((ts, ts)),
        ],
        out_specs=[ospec] * 7 + [pl.BlockSpec((1, ts // chunk, WIDTH), lambda b, i: (b, i, 0))],
        out_shape=[jax.ShapeDtypeStruct((bsz, seq, WIDTH), BF16)] * 5 + [big] * 2
        + [jax.ShapeDtypeStruct((bsz, seq // chunk, WIDTH), F32)],
        scratch_shapes=[pltpu.VMEM((8, COL_B), F32)],
        compiler_params=_cparams(("parallel", "arbitrary")),
        name="rwkv_prep",
    )(p3d, p3d, p3d, p3d, mu, vec, ww, wa, wg, bd, tri)


def _rwkv_scan_kernel(rt_ref, kt_ref, kd_ref, bd_ref, v_ref, g_ref, bonus_ref, pend_ref, ln_ref, o_ref,
                      state_ref, *, chunk, cps, prec):
    @pl.when(pl.program_id(1) == 0)
    def _():
        state_ref[...] = jnp.zeros_like(state_ref)

    c2 = 2 * chunk
    lane = lax.broadcasted_iota(jnp.int32, (chunk, LANES), 1)
    first = lane < HEAD_DIM
    row = lax.broadcasted_iota(jnp.int32, (c2, c2), 0)
    col = lax.broadcasted_iota(jnp.int32, (c2, c2), 1)
    eye = (row == col).astype(F32)
    hrow = lax.broadcasted_iota(jnp.int32, (LANES, LANES), 0) // HEAD_DIM
    hcol = lax.broadcasted_iota(jnp.int32, (LANES, LANES), 1) // HEAD_DIM
    head_mean = jnp.where(hrow == hcol, 1.0 / HEAD_DIM, 0.0).astype(F32)
    nt = (((1,), (1,)), ((), ()))
    tn = (((0,), (0,)), ((), ()))
    dot = functools.partial(jnp.dot, precision=prec, preferred_element_type=F32)
    dotg = functools.partial(lax.dot_general, precision=prec, preferred_element_type=F32)

    def stack(x):
        return jnp.concatenate([jnp.where(first, x, 0.0), jnp.where(first, 0.0, x)], axis=0)

    pairs = range(PAIRS)
    units = [(j, hp) for j in range(cps) for hp in pairs]
    sls = [slice(hp * LANES, (hp + 1) * LANES) for hp in pairs]
    rows_of = [slice(j * chunk, (j + 1) * chunk) for j in range(cps)]
    rs, ks, kds, bs, vs = ({(j, hp): stack(ref[0, rows_of[j], sls[hp]].astype(F32)) for j, hp in units}
                           for ref in (rt_ref, kt_ref, kd_ref, bd_ref, v_ref))
    big = {u: dotg(jnp.concatenate([ks[u], rs[u]], axis=0), jnp.concatenate([bs[u], kds[u]], axis=0), nt)
           for u in units}
    a_b = {u: jnp.where(row > col, big[u][0:c2, 0:c2], 0.0) for u in units}
    a_k = {u: jnp.where(row > col, big[u][0:c2, c2:], 0.0) for u in units}
    a_rb = {u: jnp.where(row >= col, big[u][c2:, 0:c2], 0.0) for u in units}
    a_rk = {u: jnp.where(row >= col, big[u][c2:, c2:], 0.0) for u in units}
    av = {u: dot(jnp.concatenate([a_k[u], a_rk[u]], axis=0), vs[u]) for u in units}
    vk = {u: dotg(vs[u], kds[u], tn) for u in units}
    inv = {u: eye - a_b[u] for u in units}
    pw = {u: dot(a_b[u], a_b[u]) for u in units}
    n_sq = int(math.log2(chunk)) - 1
    for lvl in range(n_sq):
        if lvl + 1 < n_sq:
            both = {u: dot(jnp.concatenate([inv[u], pw[u]], axis=0), pw[u]) for u in units}
            inv = {u: inv[u] + both[u][0:c2] for u in units}
            pw = {u: both[u][c2:] for u in units}
        else:
            inv = {u: inv[u] + dot(inv[u], pw[u]) for u in units}
    hts = [state_ref[0, hp] for hp in pairs]
    for j in range(cps):
        kh = [dotg(jnp.concatenate([ks[j, hp], rs[j, hp]], axis=0), hts[hp], nt) for hp in pairs]
        us = [dot(inv[j, hp], kh[hp][0:c2] + av[j, hp][0:c2]) for hp in pairs]
        ub = [dotg(us[hp], bs[j, hp], tn) for hp in pairs]
        au = [dot(a_rb[j, hp], us[hp]) for hp in pairs]
        for hp in pairs:
            sl = sls[hp]
            pend = pend_ref[0, j, 0:1, sl]
            hts[hp] = (hts[hp] + vk[j, hp] - ub[hp]) * pend
            os_ = kh[hp][c2:] + av[j, hp][c2:] - au[hp]
            o = os_[0:chunk] + os_[chunk:]
            mu = jnp.dot(o, head_mean, precision=HI, preferred_element_type=F32)
            d = o - mu
            var = jnp.dot(d * d, head_mean, precision=HI, preferred_element_type=F32)
            on = d * lax.rsqrt(var + GN_EPS) * ln_ref[0:1, sl] + ln_ref[1:2, sl]
            o_ref[0, rows_of[j], sl] = ((on + bonus_ref[0, rows_of[j], sl]) * g_ref[0, rows_of[j], sl]
                                        ).astype(o_ref.dtype)
    for hp in pairs:
        state_ref[0, hp] = hts[hp]


def rwkv_scan(rt, kt, kd, bd, v, g, bonus, pend, lnx_g, lnx_b, *, prec=None):
    bsz, seq, _ = rt.shape
    chunk = RWKV_CHUNK
    n_chunks = seq // chunk
    ln = jnp.stack([lnx_g, lnx_b] + [jnp.zeros_like(lnx_g)] * 6).astype(F32)
    pend4 = pend.reshape(bsz, n_chunks, 1, WIDTH)
    cps = RWKV_CHUNKS_PER_STEP if n_chunks % RWKV_CHUNKS_PER_STEP == 0 else 1
    spec = pl.BlockSpec((1, cps * chunk, WIDTH), lambda b, c: (b, c, 0))
    return pl.pallas_call(
        functools.partial(_rwkv_scan_kernel, chunk=chunk, cps=cps, prec=prec),
        grid=(bsz, n_chunks // cps),
        in_specs=[spec] * 7 + [
            pl.BlockSpec((1, cps, 1, WIDTH), lambda b, c: (b, c, 0, 0)),
            pl.BlockSpec((8, WIDTH), lambda b, c: (0, 0)),
        ],
        out_specs=spec,
        out_shape=jax.ShapeDtypeStruct((bsz, seq, WIDTH), BF16),
        scratch_shapes=[pltpu.VMEM((1, PAIRS, LANES, LANES), F32)],
        compiler_params=_cparams(("parallel", "arbitrary")),
        name="rwkv_scan",
    )(rt, kt, kd, bd, v, g, bonus, pend4, ln)


def _merge_kernel(x_ref, oa_ref, ob_ref, ga_ref, gb_ref, wa_ref, wb_ref, wo_ref, g2_ref,
                  h_ref, xn_ref, acc_ref):
    j = pl.program_id(1)

    @pl.when(j == 0)
    def _():
        acc_ref[...] = x_ref[...]

    ya = jnp.dot(oa_ref[...].astype(BF16), wa_ref[...], preferred_element_type=F32)
    yb = jnp.dot(ob_ref[...].astype(BF16), wb_ref[...], preferred_element_type=F32)
    y = jax.nn.sigmoid(ga_ref[...].astype(F32)) * ya + jax.nn.sigmoid(gb_ref[...].astype(F32)) * yb
    acc_ref[...] += jnp.dot(y.astype(BF16), wo_ref[...], preferred_element_type=F32)

    @pl.when(j == pl.num_programs(1) - 1)
    def _():
        h = acc_ref[...]
        h_ref[...] = h
        ms = jnp.mean(h * h, axis=-1, keepdims=True)
        xn_ref[...] = _pack_halves(h * lax.rsqrt(ms + RMS_EPS) * g2_ref[...])


def _pack_halves(x):
    half = x.shape[1] // 2
    lo = lax.bitcast_convert_type(x[:, :half].astype(BF16).astype(F32), jnp.int32)
    hi = lax.bitcast_convert_type(x[:, half:].astype(BF16).astype(F32), jnp.int32)
    return lax.bitwise_or(lax.shift_right_logical(lo, jnp.int32(16)), hi)


def _unpack_halves(words):
    lo, hi = _unpack_words(words)
    return jnp.concatenate([lo, hi], axis=1)


def merge_out(x2d, oa, ob, p2d, w_proj_a, w_proj_b, w_out, norm2_g, *, row0=0, tm=512):
    t, d = oa.shape[0], x2d.shape[1]
    r0 = row0 // tm
    tn = WIDTH
    nj = d // tn
    g0 = 0
    return pl.pallas_call(
        _merge_kernel,
        grid=(t // tm, nj),
        in_specs=[
            pl.BlockSpec((tm, d), lambda i, j: (r0 + i, 0)),
            pl.BlockSpec((tm, WIDTH), lambda i, j: (i, 0)),
            pl.BlockSpec((tm, WIDTH), lambda i, j: (i, 0)),
            pl.BlockSpec((tm, tn), lambda i, j: (i, g0 + j)),
            pl.BlockSpec((tm, tn), lambda i, j: (i, g0 + nj + j)),
            pl.BlockSpec((WIDTH, tn), lambda i, j: (0, j)),
            pl.BlockSpec((WIDTH, tn), lambda i, j: (0, j)),
            pl.BlockSpec((tn, d), lambda i, j: (j, 0)),
            pl.BlockSpec((1, d), lambda i, j: (0, 0)),
        ],
        out_specs=[pl.BlockSpec((tm, d), lambda i, j: (i, 0)), pl.BlockSpec((tm, d // 2), lambda i, j: (i, 0))],
        out_shape=[jax.ShapeDtypeStruct((t, d), F32), jax.ShapeDtypeStruct((t, d // 2), jnp.int32)],
        scratch_shapes=[pltpu.VMEM((tm, d), F32)],
        compiler_params=_cparams(("parallel", "arbitrary")),
        name="merge_out",
    )(x2d, oa, ob, p2d, p2d, w_proj_a.astype(BF16), w_proj_b.astype(BF16), w_out.astype(BF16),
      norm2_g.reshape(1, d))


PEER_HEADS = 8
PEER_NKEYS = 128
PEER_TOPK = 16
PEER_HALF = 128


def _topk_rows(s, k):
    n = s.shape[0]
    rows = lax.broadcasted_iota(jnp.int32, s.shape, 0).astype(F32)
    vals, ids = [], []
    for _ in range(k):
        m = jnp.max(s, axis=0, keepdims=True)
        first = jnp.min(jnp.where(s == m, rows, float(n)), axis=0, keepdims=True)
        vals.append(m)
        ids.append(first)
        s = jnp.where(rows == first, -jnp.inf, s)
    return jnp.concatenate(vals, axis=0), jnp.concatenate(ids, axis=0)


def _take_rows(table, ids):
    rows = lax.broadcasted_iota(jnp.int32, table.shape, 0).astype(F32)
    return jnp.sum(jnp.where(rows == ids, table, 0.0), axis=0, keepdims=True)


def _peer_route_kernel(xn_ref, wq_ref, sk_ref, idx_ref, gate_ref, *, prec):
    tt = xn_ref.shape[0]
    k = PEER_TOPK
    xn = _unpack_halves(xn_ref[...]) if xn_ref.dtype == jnp.int32 else xn_ref[...]
    q = jnp.dot(xn.astype(wq_ref.dtype), wq_ref[...], precision=prec, preferred_element_type=F32)
    nt = (((1,), (1,)), ((), ()))
    idx_rows, gate_rows = [], []
    half = k // 2
    for h in range(PEER_HEADS):
        tops = []
        for p in range(2):
            c0 = (h * 2 + p) * PEER_HALF
            s = lax.dot_general(sk_ref[h, p].astype(wq_ref.dtype), q[:, c0:c0 + PEER_HALF].astype(wq_ref.dtype),
                                nt, precision=prec, preferred_element_type=F32)
            tops.append(_topk_rows(s, k))
        (s0, i0), (s1, i1) = tops
        cs = [s0[0:1] + s1] + [s0[i:i + 1] + s1[0:half] for i in range(1, half)] + [s0[half:] + s1[0:1]]
        best_s, pos = _topk_rows(jnp.concatenate(cs, axis=0), k)
        mid = jnp.floor((pos - k) * (1.0 / half))
        end_mid = float(k + (half - 1) * half)
        i_rank = jnp.where(pos < k, 0.0, jnp.where(pos < end_mid, 1.0 + mid, pos - (end_mid - half)))
        j_rank = jnp.where(pos < k, pos, jnp.where(pos < end_mid, (pos - k) - half * mid, 0.0))
        ids = [_take_rows(i0, i_rank[n:n + 1]) * PEER_NKEYS + _take_rows(i1, j_rank[n:n + 1]) for n in range(k)]
        e = jnp.exp(best_s - best_s[0:1])
        gate_rows.append(e / jnp.sum(e, axis=0, keepdims=True))
        idx_rows.append(jnp.concatenate(ids, axis=0).astype(jnp.int32))
    idx_ref[...] = jnp.concatenate(idx_rows, axis=0).T
    gate_ref[...] = jnp.concatenate(gate_rows, axis=0).T


def peer_route(xn2d, peer_wq, peer_subkeys, *, tt=256, prec=None, wdtype=BF16):
    t, dx = xn2d.shape
    d, nq = peer_wq.shape
    n_sel = PEER_HEADS * PEER_TOPK
    return pl.pallas_call(
        functools.partial(_peer_route_kernel, prec=prec),
        grid=(t // tt,),
        in_specs=[
            pl.BlockSpec((tt, dx), lambda i: (i, 0)),
            pl.BlockSpec((d, nq), lambda i: (0, 0)),
            pl.BlockSpec((PEER_HEADS, 2, PEER_NKEYS, PEER_HALF), lambda i: (0, 0, 0, 0)),
        ],
        out_specs=[pl.BlockSpec((tt, n_sel), lambda i: (i, 0))] * 2,
        out_shape=[jax.ShapeDtypeStruct((t, n_sel), jnp.int32), jax.ShapeDtypeStruct((t, n_sel), F32)],
        compiler_params=_cparams(("parallel",)),
        name="peer_route",
    )(xn2d, peer_wq.astype(wdtype), peer_subkeys)


def _final_kernel(h_ref, y_ref, g_ref, *rest):
    o_ref = rest[-1]
    h = h_ref[...] + y_ref[...]
    ms = jnp.mean(h * h, axis=-1, keepdims=True)
    o_ref[...] = h * lax.rsqrt(ms + RMS_EPS) * g_ref[...]


def final_norm(h2d, y2d, g, *, out=None, row0=0, total_rows=None, tm=1024):
    t, d = h2d.shape
    total = t if total_rows is None else total_rows
    r0 = row0 // tm
    spec = pl.BlockSpec((tm, d), lambda i: (i, 0))
    in_specs = [spec, spec, pl.BlockSpec((1, d), lambda i: (0, 0))]
    args = [h2d, y2d, g.reshape(1, d)]
    aliases = {}
    if out is not None:
        in_specs.append(pl.BlockSpec(memory_space=pl.ANY))
        args.append(out)
        aliases = {3: 0}
    return pl.pallas_call(
        _final_kernel,
        grid=(t // tm,),
        in_specs=in_specs,
        out_specs=pl.BlockSpec((tm, d), lambda i: (r0 + i, 0)),
        out_shape=jax.ShapeDtypeStruct((total, d), F32),
        input_output_aliases=aliases,
        compiler_params=_cparams(("parallel",)),
        name="final_norm",
    )(*args)


SC_CORES = 2
SC_SUBCORES = 16
SC_LANES = 16
SC_WORKERS = SC_CORES * SC_SUBCORES
PEER_SEL = PEER_HEADS * PEER_TOPK
PEER_ROWS = 32
PEER_PARTS = PEER_SEL // PEER_ROWS
PEER_NBUF = 4
PEER_GROUP = 64
PEER_BF16_RUN = 4
PEER_ROW_PAIR = 4


def _pack_rows_kernel(w_ref, o_ref):
    o_ref[...] = _pack_halves(w_ref[...])


def _pack_rows(w, *, tr=1024):
    e, d = w.shape
    return pl.pallas_call(
        _pack_rows_kernel,
        grid=(e // tr,),
        in_specs=[pl.BlockSpec((tr, d), lambda i: (i, 0))],
        out_specs=pl.BlockSpec((tr, d // 2), lambda i: (i, 0)),
        out_shape=jax.ShapeDtypeStruct((e, d // 2), jnp.int32),
        compiler_params=_cparams(("parallel",)),
        name="pack_rows",
    )(w)


def _unpack_words(w):
    lo = lax.bitcast_convert_type(lax.shift_left(w, jnp.int32(16)), F32)
    hi = lax.bitcast_convert_type(lax.bitwise_and(w, jnp.int32(-65536)), F32)
    return lo, hi


def _packed_dot(a_words, b_words):
    from jax.experimental.pallas import tpu_sc as plsc
    prods = [plsc.bitcast(a, BF16) * plsc.bitcast(b, BF16) for a, b in zip(a_words, b_words)]
    while len(prods) > 1:
        prods = [prods[k] + prods[k + 1] for k in range(0, len(prods), 2)]
    return _unpack_words(plsc.bitcast(prods[0], jnp.int32))


def _sc_mesh():
    from jax.experimental.pallas import tpu_sc as plsc
    return plsc.VectorSubcoreMesh(core_axis_name="c", subcore_axis_name="s",
                                  num_cores=SC_CORES, num_subcores=SC_SUBCORES)


def _sc_loop(n, body, carry):
    from jax.experimental.pallas import tpu_sc as plsc
    return plsc.parallel_loop(0, n, carry=carry)(body)


def _worker_base(tokens_per_worker):
    return (lax.axis_index("s") * SC_CORES + lax.axis_index("c")) * tokens_per_worker


def _gather_compute_loop(table_hbm, idx_v, rows_v, sem, stage_v, out_row, osem, grp, compute):
    n_gathers = PEER_PARTS * grp
    ahead = PEER_NBUF - 1

    def gather(j, b):
        i = j // PEER_PARTS if isinstance(j, int) else lax.shift_right_logical(j, PEER_PARTS.bit_length() - 1)
        h = j % PEER_PARTS if isinstance(j, int) else lax.bitwise_and(j, PEER_PARTS - 1)
        ids = idx_v.at[i, pl.ds(pl.multiple_of(h * PEER_ROWS, PEER_ROWS), PEER_ROWS)]
        return pltpu.make_async_copy(table_hbm.at[ids], rows_v.at[b], sem.at[b])

    def put(i, slot):
        return pltpu.make_async_copy(stage_v.at[slot], out_row(i), osem.at[slot])

    for j in range(ahead):
        gather(j, j).start()

    @pl.loop(0, n_gathers)
    def _(j):
        b = lax.bitwise_and(j, PEER_NBUF - 1)
        h = lax.bitwise_and(j, PEER_PARTS - 1)
        i = lax.shift_right_logical(j, PEER_PARTS.bit_length() - 1)
        slot = lax.bitwise_and(i, 1)

        @pl.when((h == 0) & (i >= 2))
        def _():
            put(i - 2, slot).wait()

        @pl.when(j + ahead < n_gathers)
        def _():
            gather(j + ahead, lax.bitwise_and(j + ahead, PEER_NBUF - 1)).start()

        gather(j, b).wait()
        compute(i, h, b, slot)

        @pl.when(h == PEER_PARTS - 1)
        def _():
            put(i, slot).start()

    put(grp - 2, 0).wait()
    put(grp - 1, 1).wait()


def peer_expert_dots(x_packed, idx, u_packed):
    t, half = x_packed.shape
    n_chunks = half // SC_LANES
    tpw = t // SC_WORKERS
    igrp = min(2 * PEER_GROUP, tpw)
    grp = min(PEER_GROUP, igrp)
    assert t % SC_WORKERS == 0 and tpw % igrp == 0 and igrp % grp == 0 and grp & (grp - 1) == 0
    assert idx.shape == (t, PEER_SEL)
    rows_tog = 2 * PEER_ROW_PAIR

    def body(x_hbm, idx_hbm, u_hbm, out_hbm, idx_v, x_v, rows_v, ps_v, sem, osem):
        base = _worker_base(tpw)

        def compute(t0, i, h, b, slot):
            @pl.when((h == 0) & (lax.bitwise_and(i, grp - 1) == 0))
            def _():
                pltpu.sync_copy(x_hbm.at[pl.ds(pl.multiple_of(t0 + i, grp), grp)], x_v)

            ix = lax.bitwise_and(i, grp - 1)

            @pl.loop(0, PEER_ROWS // rows_tog)
            def _(rg):
                r0 = rg * rows_tog
                accs = [[None, None] for _ in range(rows_tog)]
                for c0 in range(0, n_chunks, PEER_BF16_RUN):
                    ats = [pl.ds((c0 + k) * SC_LANES, SC_LANES) for k in range(PEER_BF16_RUN)]
                    xw = [x_v[ix, at] for at in ats]
                    for r in range(rows_tog):
                        terms = _packed_dot([rows_v[b, r0 + r, at] for at in ats], xw)
                        for k, term in enumerate(terms):
                            accs[r][k] = term if accs[r][k] is None else accs[r][k] + term
                pair = rows_tog // 2
                bits = [lax.bitcast_convert_type(accs[r][0] + accs[r][1], jnp.int32) + jnp.int32(0x8000)
                        for r in range(rows_tog)]
                for r in range(pair):
                    word = lax.bitwise_or(lax.shift_right_logical(bits[r], jnp.int32(16)),
                                          lax.bitwise_and(bits[r + pair], jnp.int32(-65536)))
                    w0 = lax.shift_right_logical(h * PEER_ROWS + r0, 1) + r
                    ps_v[slot, pl.ds(pl.multiple_of(w0 * SC_LANES, SC_LANES), SC_LANES)] = word

        @pl.loop(0, tpw // igrp)
        def _(g):
            t0 = base + g * igrp
            pltpu.sync_copy(idx_hbm.at[pl.ds(t0, igrp)], idx_v)
            _gather_compute_loop(u_hbm, idx_v, rows_v, sem, ps_v, lambda i: out_hbm.at[t0 + i], osem, igrp,
                                 functools.partial(compute, t0))

    return pl.kernel(
        body,
        out_type=jax.ShapeDtypeStruct((t, PEER_SEL * SC_LANES // 2), jnp.int32),
        mesh=_sc_mesh(),
        scratch_types=[
            pltpu.VMEM((igrp, PEER_SEL), jnp.int32),
            pltpu.VMEM((grp, half), jnp.int32),
            pltpu.VMEM((PEER_NBUF, PEER_ROWS, half), jnp.int32),
            pltpu.VMEM((2, PEER_SEL * SC_LANES // 2), jnp.int32),
            pltpu.SemaphoreType.DMA((PEER_NBUF,)),
            pltpu.SemaphoreType.DMA((2,)),
        ],
        compiler_params=pltpu.CompilerParams(needs_layout_passes=False),
        name="peer_expert_dots",
    )(x_packed, idx, u_packed)


def peer_expert_mix(hgw, idx, v_packed):
    t = hgw.shape[0]
    half = v_packed.shape[1]
    d = 2 * half
    tpw = t // SC_WORKERS
    grp = min(2 * PEER_GROUP, tpw)
    assert t % SC_WORKERS == 0 and tpw % grp == 0 and grp % 2 == 0 and idx.shape == (t, PEER_SEL)
    n_parts = 2
    cpp = half // SC_LANES // n_parts
    from jax.experimental.pallas import tpu_sc as plsc

    def body(hg_hbm, idx_hbm, v_hbm, out_hbm, idx_v, hg_v, rows_v, o_v2, sem, osem):
        base = _worker_base(tpw)

        def compute(i, h, b, slot):
            token = jnp.full((SC_LANES,), i, jnp.int32)
            for part in range(n_parts):
                def rbody(rq, accs):
                    r0 = rq * PEER_BF16_RUN
                    s = [plsc.load_gather(hg_v, [token, jnp.full((SC_LANES,), h * PEER_ROWS + r0 + k, jnp.int32)])
                         for k in range(PEER_BF16_RUN)]
                    new = []
                    for c in range(cpp):
                        at = pl.ds((part * cpp + c) * SC_LANES, SC_LANES)
                        lo, hi = _packed_dot([rows_v[b, r0 + k, at] for k in range(PEER_BF16_RUN)], s)
                        new.append(accs[2 * c] + lo)
                        new.append(accs[2 * c + 1] + hi)
                    return tuple(new)

                accs = _sc_loop(PEER_ROWS // PEER_BF16_RUN, rbody,
                                tuple(jnp.zeros((SC_LANES,), F32) for _ in range(2 * cpp)))
                def store(overwrite):
                    for c in range(cpp):
                        lo_at = pl.ds((part * cpp + c) * SC_LANES, SC_LANES)
                        hi_at = pl.ds(half + (part * cpp + c) * SC_LANES, SC_LANES)
                        if overwrite:
                            o_v2[slot, lo_at] = accs[2 * c]
                            o_v2[slot, hi_at] = accs[2 * c + 1]
                        else:
                            o_v2[slot, lo_at] = o_v2[slot, lo_at] + accs[2 * c]
                            o_v2[slot, hi_at] = o_v2[slot, hi_at] + accs[2 * c + 1]

                pl.when(h == 0)(functools.partial(store, True))
                pl.when(h != 0)(functools.partial(store, False))

        @pl.loop(0, tpw // grp)
        def _(g):
            t0 = base + g * grp
            pltpu.sync_copy(idx_hbm.at[pl.ds(t0, grp)], idx_v)
            pltpu.sync_copy(hg_hbm.at[pl.ds(t0, grp)], hg_v)
            _gather_compute_loop(v_hbm, idx_v, rows_v, sem, o_v2, lambda i: out_hbm.at[t0 + i], osem, grp, compute)

    return pl.kernel(
        body,
        out_type=jax.ShapeDtypeStruct((t, d), F32),
        mesh=_sc_mesh(),
        scratch_types=[
            pltpu.VMEM((grp, PEER_SEL), jnp.int32),
            pltpu.VMEM((grp, PEER_SEL), jnp.int32),
            pltpu.VMEM((PEER_NBUF, PEER_ROWS, half), jnp.int32),
            pltpu.VMEM((2, d), F32),
            pltpu.SemaphoreType.DMA((PEER_NBUF,)),
            pltpu.SemaphoreType.DMA((2,)),
        ],
        compiler_params=pltpu.CompilerParams(needs_layout_passes=False),
        name="peer_expert_mix",
    )(hgw, idx, v_packed)


def _peer_act_kernel(ps_ref, gate_ref, lo_ref, hi_ref, o_ref):
    lo, hi = _unpack_words(ps_ref[...])
    pre = (jnp.dot(lo.astype(BF16), lo_ref[...], preferred_element_type=F32)
           + jnp.dot(hi.astype(BF16), hi_ref[...], preferred_element_type=F32))
    hg = 0.5 * pre * (1.0 + lax.erf(pre * (1.0 / math.sqrt(2.0)))) * gate_ref[...]
    bits = lax.bitcast_convert_type(hg.astype(BF16).astype(F32), jnp.int32)
    o_ref[...] = lax.bitwise_or(bits, lax.shift_right_logical(bits, jnp.int32(16)))


def peer_act(ps, gates, *, tm=512):
    t, n = ps.shape
    reg = jnp.arange(n) // SC_LANES
    slot_lo = 2 * PEER_ROW_PAIR * (reg // PEER_ROW_PAIR) + reg % PEER_ROW_PAIR
    place_lo = (slot_lo[:, None] == jnp.arange(PEER_SEL)[None, :]).astype(BF16)
    place_hi = ((slot_lo + PEER_ROW_PAIR)[:, None] == jnp.arange(PEER_SEL)[None, :]).astype(BF16)
    return pl.pallas_call(
        _peer_act_kernel,
        grid=(t // tm,),
        in_specs=[
            pl.BlockSpec((tm, n), lambda i: (i, 0)),
            pl.BlockSpec((tm, PEER_SEL), lambda i: (i, 0)),
            pl.BlockSpec((n, PEER_SEL), lambda i: (0, 0)),
            pl.BlockSpec((n, PEER_SEL), lambda i: (0, 0)),
        ],
        out_specs=pl.BlockSpec((tm, PEER_SEL), lambda i: (i, 0)),
        out_shape=jax.ShapeDtypeStruct((t, PEER_SEL), jnp.int32),
        compiler_params=_cparams(("parallel",)),
        name="peer_act",
    )(ps, gates, place_lo, place_hi)


BATCH_GROUPS = 8


def kernel(x, norm1_g, w_in, rwkv_mu, w0, w_lora_up, a0, a_lora_up, g_lora_up, k_k, k_a, r_k, lnx_g, lnx_b,
           w_proj_a, w_proj_b, w_out, norm2_g, peer_wq, peer_subkeys, peer_u, peer_v, rel_bias, normf_g):
    bsz, seq, d = x.shape
    depth = norm1_g.shape[0]
    groups = BATCH_GROUPS if bsz % BATCH_GROUPS == 0 else 1
    gb = bsz // groups
    tg = gb * seq
    t = bsz * seq
    src = x.reshape(t, d)
    for l in range(depth):
        w_pad = jnp.concatenate([
            w_in[l][:, :COL_A + COL_B_RAW],
            jnp.zeros((d, COL_B - COL_B_RAW), w_in.dtype),
            w_in[l][:, COL_A + COL_B_RAW:]], axis=1).astype(BF16)
        u_packed = _pack_rows(peer_u[l])
        v_packed = _pack_rows(peer_v[l])
        last = l == depth - 1

        def mix(pending, tie=None):
            row0, h2d, ps, gates, idx = pending
            hgw = peer_act(ps, gates)
            if tie is not None:
                tie, hgw = lax.optimization_barrier((tie, hgw))
            return tie, (row0, h2d, peer_expert_mix(hgw, idx, v_packed))

        outs = []

        def close(mixed):
            row0, h2d, y2d = mixed
            if last:
                outs.append(final_norm(h2d, y2d, normf_g, out=outs[-1] if outs else None, row0=row0, total_rows=t))
            else:
                outs.append(h2d + y2d)

        pending = closing = None
        for g in range(groups):
            pa, pb, pg = norm_proj(src, norm1_g[l], w_pad, row0=g * tg, rows=tg)
            oa = moba_attention(pa.reshape(gb, seq, -1), rel_bias)
            prep = tuple(rwkv_prep(pb.reshape(gb, seq, -1), rwkv_mu[l], w0[l], w_lora_up[l], a0[l], a_lora_up[l], g_lora_up[l],
                                   k_k[l], k_a[l], r_k[l]))
            mixed = None
            if pending is not None:
                (oa, prep), mixed = mix(pending, (oa, prep))
            if closing is not None:
                oa, y2d = lax.optimization_barrier((oa, closing[2]))
                close(closing[:2] + (y2d,))
                closing = None
            ob = rwkv_scan(*prep, lnx_g[l], lnx_b[l])
            h2d, xn2 = merge_out(src, oa.reshape(tg, WIDTH), ob.reshape(tg, WIDTH), pg, w_proj_a[l], w_proj_b[l],
                                 w_out[l], norm2_g[l], row0=g * tg)
            idx, gates = peer_route(xn2, peer_wq[l], peer_subkeys[l])
            if mixed is not None:
                idx, y2d = lax.optimization_barrier((idx, mixed[2]))
                closing = mixed[:2] + (y2d,)
            pending = (g * tg, h2d, peer_expert_dots(xn2, idx, u_packed), gates, idx)
        if closing is not None:
            close(closing)
        close(mix(pending)[1])
        src = outs[-1] if last else jnp.concatenate(outs, axis=0)
    return src.reshape(bsz, seq, d)
```

```python
import functools
import math

import jax
import jax.numpy as jnp
from jax import lax
from jax.experimental import pallas as pl
from jax.experimental.pallas import tpu as pltpu

F32 = jnp.float32
BF16 = jnp.bfloat16
HI = lax.Precision.HIGHEST

LANES = 128
HEAD_DIM = 64
HEADS = 8
PAIRS = HEADS // 2
WIDTH = HEADS * HEAD_DIM
MOBA_BLOCK = 256
MOBA_TOPK = 3
MOBA_LO = 64
REL_BUCKETS = 32
REL_MAX_DIST = 128
DECAY_LORA = 64
AAA_LORA = 64
GATE_LORA = 160
GN_EPS = 64e-5
RMS_EPS = 1e-6
NEG = -1e30
RWKV_CHUNK = 64
RWKV_CHUNKS_PER_STEP = 4
COL_A = 3 * WIDTH
COL_B_RAW = 3 * WIDTH + DECAY_LORA + AAA_LORA + GATE_LORA
COL_B = 4 * WIDTH
COL_G_OFF = COL_A + COL_B
VMEM_LIMIT = 56 * 1024 * 1024


def _cparams(sem):
    return pltpu.CompilerParams(dimension_semantics=sem, vmem_limit_bytes=VMEM_LIMIT)


def _norm_proj_kernel(x_ref, g_ref, w_ref, pa_ref, pb_ref, pg_ref, xn_ref, *, ja, jb):
    j = pl.program_id(1)

    @pl.when(j == 0)
    def _():
        x = x_ref[...]
        ms = jnp.mean(x * x, axis=-1, keepdims=True)
        xn_ref[...] = (x * lax.rsqrt(ms + RMS_EPS) * g_ref[...]).astype(xn_ref.dtype)

    res = jnp.dot(xn_ref[...], w_ref[...], preferred_element_type=F32)

    @pl.when(j < ja)
    def _():
        pa_ref[...] = res.astype(pa_ref.dtype)

    @pl.when((j >= ja) & (j < jb))
    def _():
        pb_ref[...] = res

    @pl.when(j >= jb)
    def _():
        pg_ref[...] = res.astype(pg_ref.dtype)


def norm_proj(x2d, g, w, *, row0=0, rows=None, tm=2048, tn=512):
    d = x2d.shape[1]
    t = x2d.shape[0] if rows is None else rows
    n = w.shape[1]
    r0 = row0 // tm
    ja, jb, jn = COL_A // tn, COL_G_OFF // tn, n // tn
    return pl.pallas_call(
        functools.partial(_norm_proj_kernel, ja=ja, jb=jb),
        grid=(t // tm, jn),
        in_specs=[
            pl.BlockSpec((tm, d), lambda i, j: (r0 + i, 0)),
            pl.BlockSpec((1, d), lambda i, j: (0, 0)),
            pl.BlockSpec((d, tn), lambda i, j: (0, j)),
        ],
        out_specs=[
            pl.BlockSpec((tm, tn), lambda i, j: (i, jnp.minimum(j, ja - 1))),
            pl.BlockSpec((tm, tn), lambda i, j: (i, jnp.clip(j - ja, 0, jb - ja - 1))),
            pl.BlockSpec((tm, tn), lambda i, j: (i, jnp.maximum(j - jb, 0))),
        ],
        out_shape=[jax.ShapeDtypeStruct((t, COL_A), BF16), jax.ShapeDtypeStruct((t, COL_B), F32),
                   jax.ShapeDtypeStruct((t, n - COL_G_OFF), BF16)],
        scratch_shapes=[pltpu.VMEM((tm, d), w.dtype)],
        compiler_params=_cparams(("parallel", "arbitrary")),
        name="norm_proj",
    )(x2d, g.reshape(1, d), w)


def _rel_bucket(dist):
    n = jnp.maximum(dist, 0)
    max_exact = REL_BUCKETS // 2
    nf = jnp.maximum(n, 1).astype(F32)
    large = max_exact + (jnp.log(nf / max_exact) / math.log(REL_MAX_DIST / max_exact)
                         * (REL_BUCKETS - max_exact)).astype(jnp.int32)
    large = jnp.minimum(large, REL_BUCKETS - 1)
    return jnp.where(n < max_exact, n, large)


def _moba_kernel(q_ref, k_ref, v_ref, bown_ref, bprev_ref, bfar_ref, o_ref,
                 kb_ref, vb_ref, kbar_ref, *, n_blocks):
    qb = pl.program_id(2)
    blk = MOBA_BLOCK
    scale = 1.0 / math.sqrt(HEAD_DIM)

    rows2 = 2 * blk
    nt = (((1,), (1,)), ((), ()))

    @pl.when(qb == 0)
    def _():
        kbar_ref[...] = jnp.zeros_like(kbar_ref)
        lane_b = lax.broadcasted_iota(jnp.int32, (blk, LANES), 1)
        for n in range(n_blocks):
            kblk = k_ref[0, n * blk:(n + 1) * blk, :]
            kbar_ref[n:n + 1, :] = jnp.mean(kblk.astype(F32), axis=0, keepdims=True)
            kb_ref[n * blk:(n + 1) * blk, 0:LANES] = kblk.astype(BF16)
            kb_ref[n * blk:(n + 1) * blk, LANES:] = ((lane_b == n) | (lane_b == MOBA_LO + n)).astype(BF16)
        vb_ref[...] = v_ref[0].astype(BF16)

    q2 = q_ref[0].astype(F32)
    first = lax.broadcasted_iota(jnp.int32, (blk, LANES), 1) < HEAD_DIM
    qh = jnp.concatenate([jnp.where(first, q2, 0.0), jnp.where(first, 0.0, q2)], axis=0)
    lane = lax.broadcasted_iota(jnp.int32, (rows2, LANES), 1)
    rowi = lax.broadcasted_iota(jnp.int32, (rows2, LANES), 0)
    gate = lax.dot_general(qh.astype(BF16), kbar_ref[...].astype(BF16), nt, preferred_element_type=F32)
    g = jnp.where(lane < qb, gate, -jnp.inf)
    chosen = lane < 0
    lane_f = lane.astype(F32)
    for _ in range(MOBA_TOPK):
        m = jnp.max(g, axis=1, keepdims=True)
        idx = jnp.min(jnp.where(g == m, lane_f, float(LANES)), axis=1, keepdims=True)
        hit = (lane_f == idx) & (m > -jnp.inf)
        chosen = chosen | hit
        g = jnp.where(hit, -jnp.inf, g)
    nfar = qb - 1
    bfar = jnp.where(rowi < blk, bfar_ref[0, 0:1, 0:1], bfar_ref[1, 0:1, 0:1])
    bhi = bfar.astype(BF16).astype(F32)
    madd = jnp.where(lane < nfar, jnp.where(chosen, bhi, NEG),
                     jnp.where(lane == nfar, jnp.where(chosen, 0.0, NEG),
                               jnp.where((lane >= MOBA_LO) & (lane - MOBA_LO < nfar), bfar - bhi, 0.0)))
    q_aug = jnp.concatenate([(qh * scale).astype(BF16), madd.astype(BF16)], axis=1)

    prev0 = pl.multiple_of(jnp.maximum(nfar, 0) * blk, blk)
    own0 = pl.multiple_of(qb * blk, blk)
    s_prev = (lax.dot_general(q_aug, kb_ref[pl.ds(prev0, blk), :], nt, preferred_element_type=F32)
              + bprev_ref[...].reshape(rows2, blk) + jnp.where(qb > 0, 0.0, NEG))
    s_own = (lax.dot_general(q_aug, kb_ref[pl.ds(own0, blk), :], nt, preferred_element_type=F32)
             + bown_ref[...].reshape(rows2, blk))
    r = lax.broadcasted_iota(jnp.int32, (rows2, blk), 0)
    c = lax.broadcasted_iota(jnp.int32, (rows2, blk), 1)
    s_own = jnp.where(lax.bitwise_and(r, blk - 1) >= c, s_own, NEG)
    s = jnp.concatenate([s_prev, s_own], axis=1)
    m_i = jnp.max(s, axis=1, keepdims=True)
    p = jnp.exp(s - m_i)
    l_i = jnp.sum(p, axis=1, keepdims=True)
    v0 = jnp.concatenate([vb_ref[pl.ds(prev0, blk), :], vb_ref[pl.ds(own0, blk), :]], axis=0)
    acc = jnp.dot(p.astype(BF16), v0, preferred_element_type=F32)

    def body(it, carry):
        m_i, l_i, acc = carry
        k0 = pl.multiple_of(it * rows2, rows2)
        s = lax.dot_general(q_aug, kb_ref[pl.ds(k0, rows2), :], nt, preferred_element_type=F32)
        tail = jnp.where(2 * it + 1 < nfar, 0.0, NEG)
        s = jnp.concatenate([s[:, :blk], s[:, blk:] + tail], axis=1)
        m_new = jnp.maximum(m_i, jnp.max(s, axis=1, keepdims=True))
        alpha = jnp.exp(m_i - m_new)
        p = jnp.exp(s - m_new)
        l_new = alpha * l_i + jnp.sum(p, axis=1, keepdims=True)
        acc_new = alpha * acc + jnp.dot(p.astype(BF16), vb_ref[pl.ds(k0, rows2), :], preferred_element_type=F32)
        return m_new, l_new, acc_new

    m_i, l_i, acc = lax.fori_loop(0, (jnp.maximum(nfar, 0) + 1) // 2, body, (m_i, l_i, acc))
    out = acc / l_i
    o_ref[0] = jnp.where(first, out[:blk], out[blk:]).astype(o_ref.dtype)


def moba_attention(p3d, rel_bias):
    bsz, seq, _ = p3d.shape
    blk = MOBA_BLOCK
    n_blocks = seq // blk
    assert n_blocks <= MOBA_LO and seq % blk == 0
    span = 2 * blk
    by_dist = rel_bias[:, _rel_bucket(jnp.arange(span))].astype(F32)
    shift = jnp.arange(span)

    def toeplitz(c):
        k = jnp.where(shift < blk, shift, shift - span)
        s = by_dist[:, jnp.clip(c - k, 0, span - 1)]
        tiled = jnp.tile(s, (1, blk))[:, :blk * (span - 1)]
        return tiled.reshape(HEADS, blk, span - 1)[:, :, :blk]

    bias_own = toeplitz(0)
    bias_prev = toeplitz(blk)
    bias_far = jnp.broadcast_to(rel_bias[:, REL_BUCKETS - 1].astype(F32)[:, None, None], (HEADS, 8, LANES))
    kern = functools.partial(_moba_kernel, n_blocks=n_blocks)
    return pl.pallas_call(
        kern,
        grid=(bsz, PAIRS, n_blocks),
        in_specs=[
            pl.BlockSpec((1, blk, LANES), lambda b, h, i: (b, i, h)),
            pl.BlockSpec((1, seq, LANES), lambda b, h, i: (b, 0, PAIRS + h)),
            pl.BlockSpec((1, seq, LANES), lambda b, h, i: (b, 0, 2 * PAIRS + h)),
            pl.BlockSpec((2, blk, blk), lambda b, h, i: (h, 0, 0)),
            pl.BlockSpec((2, blk, blk), lambda b, h, i: (h, 0, 0)),
            pl.BlockSpec((2, 8, LANES), lambda b, h, i: (h, 0, 0)),
        ],
        out_specs=pl.BlockSpec((1, blk, LANES), lambda b, h, i: (b, i, h)),
        out_shape=jax.ShapeDtypeStruct((bsz, seq, WIDTH), BF16),
        scratch_shapes=[
            pltpu.VMEM((seq, 2 * LANES), BF16),
            pltpu.VMEM((seq, LANES), BF16),
            pltpu.VMEM((LANES, LANES), F32),
        ],
        compiler_params=_cparams(("parallel", "parallel", "arbitrary")),
        name="moba",
    )(p3d, p3d, p3d, bias_own, bias_prev, bias_far)


def _shifted(x, carry_row):
    rows = lax.broadcasted_iota(jnp.int32, x.shape, 0)
    return jnp.where(rows == 0, carry_row, pltpu.roll(x, 1, axis=0))


def _rwkv_prep_kernel(pr_ref, pk_ref, pv_ref, pl_ref, mu_ref, vec_ref, ww_ref, wa_ref, wg_ref,
                      bd_ref, tri_ref,
                      rt_ref, kt_ref, kd_ref, bd_out_ref, v_ref, g_ref, bonus_ref, pend_ref,
                      carry_ref, *, chunk):
    @pl.when(pl.program_id(1) == 0)
    def _():
        carry_ref[...] = jnp.zeros_like(carry_ref)

    def mix(ref, j):
        x = ref[0]
        mu = mu_ref[0:1, j * WIDTH:(j + 1) * WIDTH]
        prev = _shifted(x, carry_ref[0:1, j * WIDTH:(j + 1) * WIDTH])
        carry_ref[0:1, j * WIDTH:(j + 1) * WIDTH] = x[x.shape[0] - 1:, :]
        return x + mu * (prev - x)

    r = mix(pr_ref, 0)
    k = mix(pk_ref, 1)
    v = mix(pv_ref, 2)
    lo = mix(pl_ref, 3)
    w0, a0, k_k, k_a, r_k = (vec_ref[i:i + 1, :] for i in range(5))
    xwa = lo[:, 0:LANES]
    xg = lo[:, LANES:3 * LANES]
    lw = jnp.dot(jnp.tanh(xwa), ww_ref[...], precision=HI, preferred_element_type=F32)
    la = jnp.dot(xwa, wa_ref[...], precision=HI, preferred_element_type=F32)
    g = jnp.dot(jax.nn.sigmoid(xg), wg_ref[...], precision=HI, preferred_element_type=F32)
    z = -(w0 + lw)
    softplus = jnp.maximum(z, 0.0) + jnp.log(1.0 + jnp.exp(-jnp.abs(z)))
    logw = -jnp.exp(-softplus - 0.5)
    a = jax.nn.sigmoid(a0 + la)
    kk = k * k_k
    ss = jnp.dot(kk * kk, bd_ref[...], precision=HI, preferred_element_type=F32)
    kk = kk / jnp.maximum(jnp.sqrt(ss), 1e-12)
    k2 = k * (1.0 + (a - 1.0) * k_a)
    rk = jnp.dot(r * k2 * r_k, bd_ref[...], precision=HI, preferred_element_type=F32)
    cs = jnp.dot(tri_ref[...], logw, precision=HI, preferred_element_type=F32)
    e_pos = jnp.exp(cs)
    e_neg = jnp.exp(-cs)
    rt_ref[0] = (r * e_pos).astype(rt_ref.dtype)
    kt_ref[0] = (kk * jnp.exp(cs - logw)).astype(kt_ref.dtype)
    kd_ref[0] = (k2 * e_neg).astype(kd_ref.dtype)
    bd_out_ref[0] = (kk * a * e_neg).astype(bd_out_ref.dtype)
    v_ref[0] = v.astype(v_ref.dtype)
    g_ref[0] = g
    bonus_ref[0] = rk * v
    ts = e_pos.shape[0]
    for c in range(ts // chunk):
        pend_ref[0, c:c + 1, :] = e_pos[(c + 1) * chunk - 1:(c + 1) * chunk, :]


def rwkv_prep(p3d, rwkv_mu, w0, w_lora_up, a0, a_lora_up, g_lora_up, k_k, k_a, r_k, *, ts=512):
    bsz, seq, _ = p3d.shape
    chunk = RWKV_CHUNK
    ts = min(ts, seq)
    mu = jnp.pad(rwkv_mu, (0, COL_B - COL_B_RAW)).reshape(1, COL_B)
    vec = jnp.stack([w0, a0, k_k, k_a, r_k.reshape(-1)] + [jnp.zeros_like(w0)] * 3).astype(F32)
    ww = jnp.zeros((LANES, WIDTH), F32).at[:DECAY_LORA].set(w_lora_up)
    wa = jnp.zeros((LANES, WIDTH), F32).at[DECAY_LORA:DECAY_LORA + AAA_LORA].set(a_lora_up)
    wg = jnp.zeros((2 * LANES, WIDTH), F32).at[:GATE_LORA].set(g_lora_up)
    hid = jnp.arange(WIDTH) // HEAD_DIM
    bd = (hid[:, None] == hid[None, :]).astype(F32)
    tix = jnp.arange(ts)
    tri = ((tix[:, None] // chunk == tix[None, :] // chunk) & (tix[None, :] <= tix[:, None])).astype(F32)
    c0 = 0
    big = jax.ShapeDtypeStruct((bsz, seq, WIDTH), F32)
    wspec = lambda shape: pl.BlockSpec(shape, lambda b, i: (0, 0))
    ospec = pl.BlockSpec((1, ts, WIDTH), lambda b, i: (b, i, 0))
    return pl.pallas_call(
        functools.partial(_rwkv_prep_kernel, chunk=chunk),
        grid=(bsz, seq // ts),
        in_specs=[
            pl.BlockSpec((1, ts, WIDTH), lambda b, i: (b, i, c0)),
            pl.BlockSpec((1, ts, WIDTH), lambda b, i: (b, i, c0 + 1)),
            pl.BlockSpec((1, ts, WIDTH), lambda b, i: (b, i, c0 + 2)),
            pl.BlockSpec((1, ts, WIDTH), lambda b, i: (b, i, c0 + 3)),
            wspec((1, COL_B)), wspec((8, WIDTH)), wspec((LANES, WIDTH)), wspec((LANES, WIDTH)),
            wspec((2 * LANES, WIDTH)), wspec((WIDTH, WIDTH)), wspec((ts, ts)),
        ],
        out_specs=[ospec] * 7 + [pl.BlockSpec((1, ts // chunk, WIDTH), lambda b, i: (b, i, 0))],
        out_shape=[jax.ShapeDtypeStruct((bsz, seq, WIDTH), BF16)] * 5 + [big] * 2
        + [jax.ShapeDtypeStruct((bsz, seq // chunk, WIDTH), F32)],
        scratch_shapes=[pltpu.VMEM((8, COL_B), F32)],
        compiler_params=_cparams(("parallel", "arbitrary")),
        name="rwkv_prep",
    )(p3d, p3d, p3d, p3d, mu, vec, ww, wa, wg, bd, tri)


def _rwkv_scan_kernel(rt_ref, kt_ref, kd_ref, bd_ref, v_ref, g_ref, bonus_ref, pend_ref, ln_ref, o_ref,
                      state_ref, *, chunk, cps, prec):
    @pl.when(pl.program_id(1) == 0)
    def _():
        state_ref[...] = jnp.zeros_like(state_ref)

    c2 = 2 * chunk
    lane = lax.broadcasted_iota(jnp.int32, (chunk, LANES), 1)
    first = lane < HEAD_DIM
    row = lax.broadcasted_iota(jnp.int32, (c2, c2), 0)
    col = lax.broadcasted_iota(jnp.int32, (c2, c2), 1)
    eye = (row == col).astype(F32)
    hrow = lax.broadcasted_iota(jnp.int32, (LANES, LANES), 0) // HEAD_DIM
    hcol = lax.broadcasted_iota(jnp.int32, (LANES, LANES), 1) // HEAD_DIM
    head_mean = jnp.where(hrow == hcol, 1.0 / HEAD_DIM, 0.0).astype(F32)
    nt = (((1,), (1,)), ((), ()))
    tn = (((0,), (0,)), ((), ()))
    dot = functools.partial(jnp.dot, precision=prec, preferred_element_type=F32)
    dotg = functools.partial(lax.dot_general, precision=prec, preferred_element_type=F32)

    def stack(x):
        return jnp.concatenate([jnp.where(first, x, 0.0), jnp.where(first, 0.0, x)], axis=0)

    pairs = range(PAIRS)
    units = [(j, hp) for j in range(cps) for hp in pairs]
    sls = [slice(hp * LANES, (hp + 1) * LANES) for hp in pairs]
    rows_of = [slice(j * chunk, (j + 1) * chunk) for j in range(cps)]
    rs, ks, kds, bs, vs = ({(j, hp): stack(ref[0, rows_of[j], sls[hp]].astype(F32)) for j, hp in units}
                           for ref in (rt_ref, kt_ref, kd_ref, bd_ref, v_ref))
    big = {u: dotg(jnp.concatenate([ks[u], rs[u]], axis=0), jnp.concatenate([bs[u], kds[u]], axis=0), nt)
           for u in units}
    a_b = {u: jnp.where(row > col, big[u][0:c2, 0:c2], 0.0) for u in units}
    a_k = {u: jnp.where(row > col, big[u][0:c2, c2:], 0.0) for u in units}
    a_rb = {u: jnp.where(row >= col, big[u][c2:, 0:c2], 0.0) for u in units}
    a_rk = {u: jnp.where(row >= col, big[u][c2:, c2:], 0.0) for u in units}
    av = {u: dot(jnp.concatenate([a_k[u], a_rk[u]], axis=0), vs[u]) for u in units}
    vk = {u: dotg(vs[u], kds[u], tn) for u in units}
    inv = {u: eye - a_b[u] for u in units}
    pw = {u: dot(a_b[u], a_b[u]) for u in units}
    n_sq = int(math.log2(chunk)) - 1
    for lvl in range(n_sq):
        if lvl + 1 < n_sq:
            both = {u: dot(jnp.concatenate([inv[u], pw[u]], axis=0), pw[u]) for u in units}
            inv = {u: inv[u] + both[u][0:c2] for u in units}
            pw = {u: both[u][c2:] for u in units}
        else:
            inv = {u: inv[u] + dot(inv[u], pw[u]) for u in units}
    hts = [state_ref[0, hp] for hp in pairs]
    for j in range(cps):
        kh = [dotg(jnp.concatenate([ks[j, hp], rs[j, hp]], axis=0), hts[hp], nt) for hp in pairs]
        us = [dot(inv[j, hp], kh[hp][0:c2] + av[j, hp][0:c2]) for hp in pairs]
        ub = [dotg(us[hp], bs[j, hp], tn) for hp in pairs]
        au = [dot(a_rb[j, hp], us[hp]) for hp in pairs]
        for hp in pairs:
            sl = sls[hp]
            pend = pend_ref[0, j, 0:1, sl]
            hts[hp] = (hts[hp] + vk[j, hp] - ub[hp]) * pend
            os_ = kh[hp][c2:] + av[j, hp][c2:] - au[hp]
            o = os_[0:chunk] + os_[chunk:]
            mu = jnp.dot(o, head_mean, precision=HI, preferred_element_type=F32)
            d = o - mu
            var = jnp.dot(d * d, head_mean, precision=HI, preferred_element_type=F32)
            on = d * lax.rsqrt(var + GN_EPS) * ln_ref[0:1, sl] + ln_ref[1:2, sl]
            o_ref[0, rows_of[j], sl] = ((on + bonus_ref[0, rows_of[j], sl]) * g_ref[0, rows_of[j], sl]
                                        ).astype(o_ref.dtype)
    for hp in pairs:
        state_ref[0, hp] = hts[hp]


def rwkv_scan(rt, kt, kd, bd, v, g, bonus, pend, lnx_g, lnx_b, *, prec=None):
    bsz, seq, _ = rt.shape
    chunk = RWKV_CHUNK
    n_chunks = seq // chunk
    ln = jnp.stack([lnx_g, lnx_b] + [jnp.zeros_like(lnx_g)] * 6).astype(F32)
    pend4 = pend.reshape(bsz, n_chunks, 1, WIDTH)
    cps = RWKV_CHUNKS_PER_STEP if n_chunks % RWKV_CHUNKS_PER_STEP == 0 else 1
    spec = pl.BlockSpec((1, cps * chunk, WIDTH), lambda b, c: (b, c, 0))
    return pl.pallas_call(
        functools.partial(_rwkv_scan_kernel, chunk=chunk, cps=cps, prec=prec),
        grid=(bsz, n_chunks // cps),
        in_specs=[spec] * 7 + [
            pl.BlockSpec((1, cps, 1, WIDTH), lambda b, c: (b, c, 0, 0)),
            pl.BlockSpec((8, WIDTH), lambda b, c: (0, 0)),
        ],
        out_specs=spec,
        out_shape=jax.ShapeDtypeStruct((bsz, seq, WIDTH), BF16),
        scratch_shapes=[pltpu.VMEM((1, PAIRS, LANES, LANES), F32)],
        compiler_params=_cparams(("parallel", "arbitrary")),
        name="rwkv_scan",
    )(rt, kt, kd, bd, v, g, bonus, pend4, ln)


def _merge_kernel(x_ref, oa_ref, ob_ref, ga_ref, gb_ref, wa_ref, wb_ref, wo_ref, g2_ref,
                  h_ref, xn_ref, acc_ref):
    j = pl.program_id(1)

    @pl.when(j == 0)
    def _():
        acc_ref[...] = x_ref[...]

    ya = jnp.dot(oa_ref[...].astype(BF16), wa_ref[...], preferred_element_type=F32)
    yb = jnp.dot(ob_ref[...].astype(BF16), wb_ref[...], preferred_element_type=F32)
    y = jax.nn.sigmoid(ga_ref[...].astype(F32)) * ya + jax.nn.sigmoid(gb_ref[...].astype(F32)) * yb
    acc_ref[...] += jnp.dot(y.astype(BF16), wo_ref[...], preferred_element_type=F32)

    @pl.when(j == pl.num_programs(1) - 1)
    def _():
        h = acc_ref[...]
        h_ref[...] = h
        ms = jnp.mean(h * h, axis=-1, keepdims=True)
        xn_ref[...] = _pack_halves(h * lax.rsqrt(ms + RMS_EPS) * g2_ref[...])


def _pack_halves(x):
    half = x.shape[1] // 2
    lo = lax.bitcast_convert_type(x[:, :half].astype(BF16).astype(F32), jnp.int32)
    hi = lax.bitcast_convert_type(x[:, half:].astype(BF16).astype(F32), jnp.int32)
    return lax.bitwise_or(lax.shift_right_logical(lo, jnp.int32(16)), hi)


def _unpack_halves(words):
    lo, hi = _unpack_words(words)
    return jnp.concatenate([lo, hi], axis=1)


def merge_out(x2d, oa, ob, p2d, w_proj_a, w_proj_b, w_out, norm2_g, *, row0=0, tm=1024):
    t, d = oa.shape[0], x2d.shape[1]
    r0 = row0 // tm
    tn = WIDTH
    nj = d // tn
    g0 = 0
    return pl.pallas_call(
        _merge_kernel,
        grid=(t // tm, nj),
        in_specs=[
            pl.BlockSpec((tm, d), lambda i, j: (r0 + i, 0)),
            pl.BlockSpec((tm, WIDTH), lambda i, j: (i, 0)),
            pl.BlockSpec((tm, WIDTH), lambda i, j: (i, 0)),
            pl.BlockSpec((tm, tn), lambda i, j: (i, g0 + j)),
            pl.BlockSpec((tm, tn), lambda i, j: (i, g0 + nj + j)),
            pl.BlockSpec((WIDTH, tn), lambda i, j: (0, j)),
            pl.BlockSpec((WIDTH, tn), lambda i, j: (0, j)),
            pl.BlockSpec((tn, d), lambda i, j: (j, 0)),
            pl.BlockSpec((1, d), lambda i, j: (0, 0)),
        ],
        out_specs=[pl.BlockSpec((tm, d), lambda i, j: (i, 0)), pl.BlockSpec((tm, d // 2), lambda i, j: (i, 0))],
        out_shape=[jax.ShapeDtypeStruct((t, d), F32), jax.ShapeDtypeStruct((t, d // 2), jnp.int32)],
        scratch_shapes=[pltpu.VMEM((tm, d), F32)],
        compiler_params=_cparams(("parallel", "arbitrary")),
        name="merge_out",
    )(x2d, oa, ob, p2d, p2d, w_proj_a.astype(BF16), w_proj_b.astype(BF16), w_out.astype(BF16),
      norm2_g.reshape(1, d))


PEER_HEADS = 8
PEER_NKEYS = 128
PEER_TOPK = 16
PEER_HALF = 128


def _topk_rows(s, k):
    n = s.shape[0]
    rows = lax.broadcasted_iota(jnp.int32, s.shape, 0).astype(F32)
    vals, ids = [], []
    for _ in range(k):
        m = jnp.max(s, axis=0, keepdims=True)
        first = jnp.min(jnp.where(s == m, rows, float(n)), axis=0, keepdims=True)
        vals.append(m)
        ids.append(first)
        s = jnp.where(rows == first, -jnp.inf, s)
    return jnp.concatenate(vals, axis=0), jnp.concatenate(ids, axis=0)


def _take_rows(table, ids):
    rows = lax.broadcasted_iota(jnp.int32, table.shape, 0).astype(F32)
    return jnp.sum(jnp.where(rows == ids, table, 0.0), axis=0, keepdims=True)


def _peer_route_kernel(xn_ref, wq_ref, sk_ref, idx_ref, gate_ref, *, prec):
    tt = xn_ref.shape[0]
    k = PEER_TOPK
    xn = _unpack_halves(xn_ref[...]) if xn_ref.dtype == jnp.int32 else xn_ref[...]
    q = jnp.dot(xn.astype(wq_ref.dtype), wq_ref[...], precision=prec, preferred_element_type=F32)
    nt = (((1,), (1,)), ((), ()))
    idx_rows, gate_rows = [], []
    half = k // 2
    for h in range(PEER_HEADS):
        tops = []
        for p in range(2):
            c0 = (h * 2 + p) * PEER_HALF
            s = lax.dot_general(sk_ref[h, p].astype(wq_ref.dtype), q[:, c0:c0 + PEER_HALF].astype(wq_ref.dtype),
                                nt, precision=prec, preferred_element_type=F32)
            tops.append(_topk_rows(s, k))
        (s0, i0), (s1, i1) = tops
        cs = [s0[0:1] + s1] + [s0[i:i + 1] + s1[0:half] for i in range(1, half)] + [s0[half:] + s1[0:1]]
        best_s, pos = _topk_rows(jnp.concatenate(cs, axis=0), k)
        mid = jnp.floor((pos - k) * (1.0 / half))
        end_mid = float(k + (half - 1) * half)
        i_rank = jnp.where(pos < k, 0.0, jnp.where(pos < end_mid, 1.0 + mid, pos - (end_mid - half)))
        j_rank = jnp.where(pos < k, pos, jnp.where(pos < end_mid, (pos - k) - half * mid, 0.0))
        ids = [_take_rows(i0, i_rank[n:n + 1]) * PEER_NKEYS + _take_rows(i1, j_rank[n:n + 1]) for n in range(k)]
        e = jnp.exp(best_s - best_s[0:1])
        gate_rows.append(e / jnp.sum(e, axis=0, keepdims=True))
        idx_rows.append(jnp.concatenate(ids, axis=0).astype(jnp.int32))
    idx_ref[...] = jnp.concatenate(idx_rows, axis=0).T
    gate_ref[...] = jnp.concatenate(gate_rows, axis=0).T


def peer_route(xn2d, peer_wq, peer_subkeys, *, tt=256, prec=None, wdtype=BF16):
    t, dx = xn2d.shape
    d, nq = peer_wq.shape
    n_sel = PEER_HEADS * PEER_TOPK
    return pl.pallas_call(
        functools.partial(_peer_route_kernel, prec=prec),
        grid=(t // tt,),
        in_specs=[
            pl.BlockSpec((tt, dx), lambda i: (i, 0)),
            pl.BlockSpec((d, nq), lambda i: (0, 0)),
            pl.BlockSpec((PEER_HEADS, 2, PEER_NKEYS, PEER_HALF), lambda i: (0, 0, 0, 0)),
        ],
        out_specs=[pl.BlockSpec((tt, n_sel), lambda i: (i, 0))] * 2,
        out_shape=[jax.ShapeDtypeStruct((t, n_sel), jnp.int32), jax.ShapeDtypeStruct((t, n_sel), F32)],
        compiler_params=_cparams(("parallel",)),
        name="peer_route",
    )(xn2d, peer_wq.astype(wdtype), peer_subkeys)


def _final_kernel(h_ref, y_ref, g_ref, *rest):
    o_ref = rest[-1]
    h = h_ref[...] + y_ref[...]
    ms = jnp.mean(h * h, axis=-1, keepdims=True)
    o_ref[...] = h * lax.rsqrt(ms + RMS_EPS) * g_ref[...]


def final_norm(h2d, y2d, g, *, out=None, row0=0, total_rows=None, tm=1024):
    t, d = h2d.shape
    total = t if total_rows is None else total_rows
    r0 = row0 // tm
    spec = pl.BlockSpec((tm, d), lambda i: (i, 0))
    in_specs = [spec, spec, pl.BlockSpec((1, d), lambda i: (0, 0))]
    args = [h2d, y2d, g.reshape(1, d)]
    aliases = {}
    if out is not None:
        in_specs.append(pl.BlockSpec(memory_space=pl.ANY))
        args.append(out)
        aliases = {3: 0}
    return pl.pallas_call(
        _final_kernel,
        grid=(t // tm,),
        in_specs=in_specs,
        out_specs=pl.BlockSpec((tm, d), lambda i: (r0 + i, 0)),
        out_shape=jax.ShapeDtypeStruct((total, d), F32),
        input_output_aliases=aliases,
        compiler_params=_cparams(("parallel",)),
        name="final_norm",
    )(*args)


SC_CORES = 2
SC_SUBCORES = 16
SC_LANES = 16
SC_WORKERS = SC_CORES * SC_SUBCORES
PEER_SEL = PEER_HEADS * PEER_TOPK
PEER_ROWS = 32
PEER_PARTS = PEER_SEL // PEER_ROWS
PEER_NBUF = 4
PEER_GROUP = 64
PEER_BF16_RUN = 4
PEER_ROW_PAIR = 4


def _pack_rows_kernel(w_ref, o_ref):
    o_ref[...] = _pack_halves(w_ref[...])


def _pack_rows(w, *, tr=1024):
    e, d = w.shape
    return pl.pallas_call(
        _pack_rows_kernel,
        grid=(e // tr,),
        in_specs=[pl.BlockSpec((tr, d), lambda i: (i, 0))],
        out_specs=pl.BlockSpec((tr, d // 2), lambda i: (i, 0)),
        out_shape=jax.ShapeDtypeStruct((e, d // 2), jnp.int32),
        compiler_params=_cparams(("parallel",)),
        name="pack_rows",
    )(w)


def _unpack_words(w):
    lo = lax.bitcast_convert_type(lax.shift_left(w, jnp.int32(16)), F32)
    hi = lax.bitcast_convert_type(lax.bitwise_and(w, jnp.int32(-65536)), F32)
    return lo, hi


def _packed_dot(a_words, b_words):
    from jax.experimental.pallas import tpu_sc as plsc
    prods = [plsc.bitcast(a, BF16) * plsc.bitcast(b, BF16) for a, b in zip(a_words, b_words)]
    while len(prods) > 1:
        prods = [prods[k] + prods[k + 1] for k in range(0, len(prods), 2)]
    return _unpack_words(plsc.bitcast(prods[0], jnp.int32))


def _sc_mesh():
    from jax.experimental.pallas import tpu_sc as plsc
    return plsc.VectorSubcoreMesh(core_axis_name="c", subcore_axis_name="s",
                                  num_cores=SC_CORES, num_subcores=SC_SUBCORES)


def _sc_loop(n, body, carry):
    from jax.experimental.pallas import tpu_sc as plsc
    return plsc.parallel_loop(0, n, carry=carry)(body)


def _worker_base(tokens_per_worker):
    return (lax.axis_index("s") * SC_CORES + lax.axis_index("c")) * tokens_per_worker


def _gather_compute_loop(table_hbm, idx_v, rows_v, sem, stage_v, out_row, osem, grp, compute):
    n_gathers = PEER_PARTS * grp
    ahead = PEER_NBUF - 1

    def gather(j, b):
        i = j // PEER_PARTS if isinstance(j, int) else lax.shift_right_logical(j, PEER_PARTS.bit_length() - 1)
        h = j % PEER_PARTS if isinstance(j, int) else lax.bitwise_and(j, PEER_PARTS - 1)
        ids = idx_v.at[i, pl.ds(pl.multiple_of(h * PEER_ROWS, PEER_ROWS), PEER_ROWS)]
        return pltpu.make_async_copy(table_hbm.at[ids], rows_v.at[b], sem.at[b])

    def put(i, slot):
        return pltpu.make_async_copy(stage_v.at[slot], out_row(i), osem.at[slot])

    for j in range(ahead):
        gather(j, j).start()

    @pl.loop(0, n_gathers)
    def _(j):
        b = lax.bitwise_and(j, PEER_NBUF - 1)
        h = lax.bitwise_and(j, PEER_PARTS - 1)
        i = lax.shift_right_logical(j, PEER_PARTS.bit_length() - 1)
        slot = lax.bitwise_and(i, 1)

        @pl.when((h == 0) & (i >= 2))
        def _():
            put(i - 2, slot).wait()

        @pl.when(j + ahead < n_gathers)
        def _():
            gather(j + ahead, lax.bitwise_and(j + ahead, PEER_NBUF - 1)).start()

        gather(j, b).wait()
        compute(i, h, b, slot)

        @pl.when(h == PEER_PARTS - 1)
        def _():
            put(i, slot).start()

    put(grp - 2, 0).wait()
    put(grp - 1, 1).wait()


def peer_expert_dots(x_packed, idx, u_packed):
    t, half = x_packed.shape
    n_chunks = half // SC_LANES
    tpw = t // SC_WORKERS
    igrp = min(2 * PEER_GROUP, tpw)
    grp = min(PEER_GROUP, igrp)
    assert t % SC_WORKERS == 0 and tpw % igrp == 0 and igrp % grp == 0 and grp & (grp - 1) == 0
    rows_tog = 2 * PEER_ROW_PAIR
    assert igrp % 2 == 0 and idx.shape == (t, PEER_SEL) and PEER_ROWS % rows_tog == 0

    def body(x_hbm, idx_hbm, u_hbm, out_hbm, idx_v, x_v, rows_v, ps_v, sem, osem):
        base = _worker_base(tpw)

        def compute(t0, i, h, b, slot):
            @pl.when((h == 0) & (lax.bitwise_and(i, grp - 1) == 0))
            def _():
                pltpu.sync_copy(x_hbm.at[pl.ds(pl.multiple_of(t0 + i, grp), grp)], x_v)

            ix = lax.bitwise_and(i, grp - 1)

            @pl.loop(0, PEER_ROWS // rows_tog)
            def _(rg):
                r0 = rg * rows_tog
                accs = [[None, None] for _ in range(rows_tog)]
                for c0 in range(0, n_chunks, PEER_BF16_RUN):
                    ats = [pl.ds((c0 + k) * SC_LANES, SC_LANES) for k in range(PEER_BF16_RUN)]
                    xw = [x_v[ix, at] for at in ats]
                    for r in range(rows_tog):
                        terms = _packed_dot([rows_v[b, r0 + r, at] for at in ats], xw)
                        for k, term in enumerate(terms):
                            accs[r][k] = term if accs[r][k] is None else accs[r][k] + term
                pair = rows_tog // 2
                bits = [lax.bitcast_convert_type(accs[r][0] + accs[r][1], jnp.int32) + jnp.int32(0x8000)
                        for r in range(rows_tog)]
                for r in range(pair):
                    word = lax.bitwise_or(lax.shift_right_logical(bits[r], jnp.int32(16)),
                                          lax.bitwise_and(bits[r + pair], jnp.int32(-65536)))
                    w0 = lax.shift_right_logical(h * PEER_ROWS + r0, 1) + r
                    ps_v[slot, pl.ds(pl.multiple_of(w0 * SC_LANES, SC_LANES), SC_LANES)] = word

        @pl.loop(0, tpw // igrp)
        def _(g):
            t0 = base + g * igrp
            pltpu.sync_copy(idx_hbm.at[pl.ds(t0, igrp)], idx_v)
            _gather_compute_loop(u_hbm, idx_v, rows_v, sem, ps_v, lambda i: out_hbm.at[t0 + i], osem, igrp,
                                 functools.partial(compute, t0))

    return pl.kernel(
        body,
        out_type=jax.ShapeDtypeStruct((t, PEER_SEL * SC_LANES // 2), jnp.int32),
        mesh=_sc_mesh(),
        scratch_types=[
            pltpu.VMEM((igrp, PEER_SEL), jnp.int32),
            pltpu.VMEM((grp, half), jnp.int32),
            pltpu.VMEM((PEER_NBUF, PEER_ROWS, half), jnp.int32),
            pltpu.VMEM((2, PEER_SEL * SC_LANES // 2), jnp.int32),
            pltpu.SemaphoreType.DMA((PEER_NBUF,)),
            pltpu.SemaphoreType.DMA((2,)),
        ],
        compiler_params=pltpu.CompilerParams(needs_layout_passes=False),
        name="peer_expert_dots",
    )(x_packed, idx, u_packed)


def peer_expert_mix(hgw, idx, v_packed):
    t = hgw.shape[0]
    half = v_packed.shape[1]
    d = 2 * half
    tpw = t // SC_WORKERS
    grp = min(2 * PEER_GROUP, tpw)
    assert t % SC_WORKERS == 0 and tpw % grp == 0 and grp % 2 == 0 and idx.shape == (t, PEER_SEL)
    n_parts = 2
    cpp = half // SC_LANES // n_parts
    from jax.experimental.pallas import tpu_sc as plsc

    def body(hg_hbm, idx_hbm, v_hbm, out_hbm, idx_v, hg_v, rows_v, o_v2, sem, osem):
        base = _worker_base(tpw)

        def compute(i, h, b, slot):
            token = jnp.full((SC_LANES,), i, jnp.int32)
            for part in range(n_parts):
                def rbody(rq, accs):
                    r0 = rq * PEER_BF16_RUN
                    s = [plsc.load_gather(hg_v, [token, jnp.full((SC_LANES,), h * PEER_ROWS + r0 + k, jnp.int32)])
                         for k in range(PEER_BF16_RUN)]
                    new = []
                    for c in range(cpp):
                        at = pl.ds((part * cpp + c) * SC_LANES, SC_LANES)
                        lo, hi = _packed_dot([rows_v[b, r0 + k, at] for k in range(PEER_BF16_RUN)], s)
                        new.append(accs[2 * c] + lo)
                        new.append(accs[2 * c + 1] + hi)
                    return tuple(new)

                accs = _sc_loop(PEER_ROWS // PEER_BF16_RUN, rbody,
                                tuple(jnp.zeros((SC_LANES,), F32) for _ in range(2 * cpp)))
                def store(overwrite):
                    for c in range(cpp):
                        lo_at = pl.ds((part * cpp + c) * SC_LANES, SC_LANES)
                        hi_at = pl.ds(half + (part * cpp + c) * SC_LANES, SC_LANES)
                        if overwrite:
                            o_v2[slot, lo_at] = accs[2 * c]
                            o_v2[slot, hi_at] = accs[2 * c + 1]
                        else:
                            o_v2[slot, lo_at] = o_v2[slot, lo_at] + accs[2 * c]
                            o_v2[slot, hi_at] = o_v2[slot, hi_at] + accs[2 * c + 1]

                pl.when(h == 0)(functools.partial(store, True))
                pl.when(h != 0)(functools.partial(store, False))

        @pl.loop(0, tpw // grp)
        def _(g):
            t0 = base + g * grp
            pltpu.sync_copy(idx_hbm.at[pl.ds(t0, grp)], idx_v)
            pltpu.sync_copy(hg_hbm.at[pl.ds(t0, grp)], hg_v)
            _gather_compute_loop(v_hbm, idx_v, rows_v, sem, o_v2, lambda i: out_hbm.at[t0 + i], osem, grp, compute)

    return pl.kernel(
        body,
        out_type=jax.ShapeDtypeStruct((t, d), F32),
        mesh=_sc_mesh(),
        scratch_types=[
            pltpu.VMEM((grp, PEER_SEL), jnp.int32),
            pltpu.VMEM((grp, PEER_SEL), jnp.int32),
            pltpu.VMEM((PEER_NBUF, PEER_ROWS, half), jnp.int32),
            pltpu.VMEM((2, d), F32),
            pltpu.SemaphoreType.DMA((PEER_NBUF,)),
            pltpu.SemaphoreType.DMA((2,)),
        ],
        compiler_params=pltpu.CompilerParams(needs_layout_passes=False),
        name="peer_expert_mix",
    )(hgw, idx, v_packed)


def _peer_act_kernel(ps_ref, gate_ref, lo_ref, hi_ref, o_ref):
    lo, hi = _unpack_words(ps_ref[...])
    pre = (jnp.dot(lo.astype(BF16), lo_ref[...], preferred_element_type=F32)
           + jnp.dot(hi.astype(BF16), hi_ref[...], preferred_element_type=F32))
    hg = 0.5 * pre * (1.0 + lax.erf(pre * (1.0 / math.sqrt(2.0)))) * gate_ref[...]
    bits = lax.bitcast_convert_type(hg.astype(BF16).astype(F32), jnp.int32)
    o_ref[...] = lax.bitwise_or(bits, lax.shift_right_logical(bits, jnp.int32(16)))


def peer_act(ps, gates, *, tm=1024):
    t, n = ps.shape
    reg = jnp.arange(n) // SC_LANES
    slot_lo = 2 * PEER_ROW_PAIR * (reg // PEER_ROW_PAIR) + reg % PEER_ROW_PAIR
    place_lo = (slot_lo[:, None] == jnp.arange(PEER_SEL)[None, :]).astype(BF16)
    place_hi = ((slot_lo + PEER_ROW_PAIR)[:, None] == jnp.arange(PEER_SEL)[None, :]).astype(BF16)
    return pl.pallas_call(
        _peer_act_kernel,
        grid=(t // tm,),
        in_specs=[
            pl.BlockSpec((tm, n), lambda i: (i, 0)),
            pl.BlockSpec((tm, PEER_SEL), lambda i: (i, 0)),
            pl.BlockSpec((n, PEER_SEL), lambda i: (0, 0)),
            pl.BlockSpec((n, PEER_SEL), lambda i: (0, 0)),
        ],
        out_specs=pl.BlockSpec((tm, PEER_SEL), lambda i: (i, 0)),
        out_shape=jax.ShapeDtypeStruct((t, PEER_SEL), jnp.int32),
        compiler_params=_cparams(("parallel",)),
        name="peer_act",
    )(ps, gates, place_lo, place_hi)


BATCH_GROUPS = 8


def kernel(x, norm1_g, w_in, rwkv_mu, w0, w_lora_up, a0, a_lora_up, g_lora_up, k_k, k_a, r_k, lnx_g, lnx_b,
           w_proj_a, w_proj_b, w_out, norm2_g, peer_wq, peer_subkeys, peer_u, peer_v, rel_bias, normf_g):
    bsz, seq, d = x.shape
    depth = norm1_g.shape[0]
    groups = BATCH_GROUPS if bsz % BATCH_GROUPS == 0 else 1
    gb = bsz // groups
    tg = gb * seq
    t = bsz * seq
    src = x.reshape(t, d)
    for l in range(depth):
        w_pad = jnp.concatenate([
            w_in[l][:, :COL_A + COL_B_RAW],
            jnp.zeros((d, COL_B - COL_B_RAW), w_in.dtype),
            w_in[l][:, COL_A + COL_B_RAW:]], axis=1).astype(BF16)
        u_packed = _pack_rows(peer_u[l])
        v_packed = _pack_rows(peer_v[l])
        last = l == depth - 1

        def mix(pending, tie=None):
            row0, h2d, ps, gates, idx = pending
            hgw = peer_act(ps, gates)
            if tie is not None:
                tie, hgw = lax.optimization_barrier((tie, hgw))
            return tie, (row0, h2d, peer_expert_mix(hgw, idx, v_packed))

        outs = []

        def close(mixed):
            row0, h2d, y2d = mixed
            if last:
                outs.append(final_norm(h2d, y2d, normf_g, out=outs[-1] if outs else None, row0=row0, total_rows=t))
            else:
                outs.append(h2d + y2d)

        pending = closing = None
        for g in range(groups):
            pa, pb, pg = norm_proj(src, norm1_g[l], w_pad, row0=g * tg, rows=tg)
            oa = moba_attention(pa.reshape(gb, seq, -1), rel_bias)
            prep = tuple(rwkv_prep(pb.reshape(gb, seq, -1), rwkv_mu[l], w0[l], w_lora_up[l], a0[l], a_lora_up[l], g_lora_up[l],
                                   k_k[l], k_a[l], r_k[l]))
            mixed = None
            if pending is not None:
                (oa, prep), mixed = mix(pending, (oa, prep))
            if closing is not None:
                oa, y2d = lax.optimization_barrier((oa, closing[2]))
                close(closing[:2] + (y2d,))
                closing = None
            ob = rwkv_scan(*prep, lnx_g[l], lnx_b[l])
            h2d, xn2 = merge_out(src, oa.reshape(tg, WIDTH), ob.reshape(tg, WIDTH), pg, w_proj_a[l], w_proj_b[l],
                                 w_out[l], norm2_g[l], row0=g * tg)
            idx, gates = peer_route(xn2, peer_wq[l], peer_subkeys[l])
            if mixed is not None:
                idx, y2d = lax.optimization_barrier((idx, mixed[2]))
                closing = mixed[:2] + (y2d,)
            pending = (g * tg, h2d, peer_expert_dots(xn2, idx, u_packed), gates, idx)
        if closing is not None:
            close(closing)
        close(mix(pending)[1])
        src = outs[-1] if last else jnp.concatenate(outs, axis=0)
    return src.reshape(bsz, seq, d)
```

```python
import functools
import math

import jax
import jax.numpy as jnp
from jax import lax
from jax.experimental import pallas as pl
from jax.experimental.pallas import tpu as pltpu

F32 = jnp.float32
BF16 = jnp.bfloat16
HI = lax.Precision.HIGHEST

LANES = 128
HEAD_DIM = 64
HEADS = 8
PAIRS = HEADS // 2
WIDTH = HEADS * HEAD_DIM
MOBA_BLOCK = 256
MOBA_TOPK = 3
MOBA_LO = 64
REL_BUCKETS = 32
REL_MAX_DIST = 128
DECAY_LORA = 64
AAA_LORA = 64
GATE_LORA = 160
GN_EPS = 64e-5
RMS_EPS = 1e-6
NEG = -1e30
RWKV_CHUNK = 64
RWKV_CHUNKS_PER_STEP = 4
COL_A = 3 * WIDTH
COL_B_RAW = 3 * WIDTH + DECAY_LORA + AAA_LORA + GATE_LORA
COL_B = 4 * WIDTH
COL_G_OFF = COL_A + COL_B
VMEM_LIMIT = 56 * 1024 * 1024


def _cparams(sem):
    return pltpu.CompilerParams(dimension_semantics=sem, vmem_limit_bytes=VMEM_LIMIT)


def _norm_proj_kernel(x_ref, g_ref, w_ref, pa_ref, pb_ref, pg_ref, xn_ref, *, ja, jb):
    j = pl.program_id(1)

    @pl.when(j == 0)
    def _():
        x = x_ref[...]
        ms = jnp.mean(x * x, axis=-1, keepdims=True)
        xn_ref[...] = (x * lax.rsqrt(ms + RMS_EPS) * g_ref[...]).astype(xn_ref.dtype)

    res = jnp.dot(xn_ref[...], w_ref[...], preferred_element_type=F32)

    @pl.when(j < ja)
    def _():
        pa_ref[...] = res.astype(pa_ref.dtype)

    @pl.when((j >= ja) & (j < jb))
    def _():
        pb_ref[...] = res

    @pl.when(j >= jb)
    def _():
        pg_ref[...] = res.astype(pg_ref.dtype)


def norm_proj(x2d, g, w, *, row0=0, rows=None, tm=2048, tn=512):
    d = x2d.shape[1]
    t = x2d.shape[0] if rows is None else rows
    n = w.shape[1]
    r0 = row0 // tm
    ja, jb, jn = COL_A // tn, COL_G_OFF // tn, n // tn
    return pl.pallas_call(
        functools.partial(_norm_proj_kernel, ja=ja, jb=jb),
        grid=(t // tm, jn),
        in_specs=[
            pl.BlockSpec((tm, d), lambda i, j: (r0 + i, 0)),
            pl.BlockSpec((1, d), lambda i, j: (0, 0)),
            pl.BlockSpec((d, tn), lambda i, j: (0, j)),
        ],
        out_specs=[
            pl.BlockSpec((tm, tn), lambda i, j: (i, jnp.minimum(j, ja - 1))),
            pl.BlockSpec((tm, tn), lambda i, j: (i, jnp.clip(j - ja, 0, jb - ja - 1))),
            pl.BlockSpec((tm, tn), lambda i, j: (i, jnp.maximum(j - jb, 0))),
        ],
        out_shape=[jax.ShapeDtypeStruct((t, COL_A), BF16), jax.ShapeDtypeStruct((t, COL_B), F32),
                   jax.ShapeDtypeStruct((t, n - COL_G_OFF), BF16)],
        scratch_shapes=[pltpu.VMEM((tm, d), w.dtype)],
        compiler_params=_cparams(("parallel", "arbitrary")),
        name="norm_proj",
    )(x2d, g.reshape(1, d), w)


def _rel_bucket(dist):
    n = jnp.maximum(dist, 0)
    max_exact = REL_BUCKETS // 2
    nf = jnp.maximum(n, 1).astype(F32)
    large = max_exact + (jnp.log(nf / max_exact) / math.log(REL_MAX_DIST / max_exact)
                         * (REL_BUCKETS - max_exact)).astype(jnp.int32)
    large = jnp.minimum(large, REL_BUCKETS - 1)
    return jnp.where(n < max_exact, n, large)


def _moba_kernel(q_ref, k_ref, v_ref, bown_ref, bprev_ref, bfar_ref, o_ref,
                 kb_ref, vb_ref, kbar_ref, *, n_blocks):
    qb = pl.program_id(2)
    blk = MOBA_BLOCK
    scale = 1.0 / math.sqrt(HEAD_DIM)

    rows2 = 2 * blk
    nt = (((1,), (1,)), ((), ()))

    @pl.when(qb == 0)
    def _():
        kbar_ref[...] = jnp.zeros_like(kbar_ref)
        lane_b = lax.broadcasted_iota(jnp.int32, (blk, LANES), 1)
        for n in range(n_blocks):
            kblk = k_ref[0, n * blk:(n + 1) * blk, :]
            kbar_ref[n:n + 1, :] = jnp.mean(kblk.astype(F32), axis=0, keepdims=True)
            kb_ref[n * blk:(n + 1) * blk, 0:LANES] = kblk.astype(BF16)
            kb_ref[n * blk:(n + 1) * blk, LANES:] = ((lane_b == n) | (lane_b == MOBA_LO + n)).astype(BF16)
        vb_ref[...] = v_ref[0].astype(BF16)

    q2 = q_ref[0].astype(F32)
    first = lax.broadcasted_iota(jnp.int32, (blk, LANES), 1) < HEAD_DIM
    qh = jnp.concatenate([jnp.where(first, q2, 0.0), jnp.where(first, 0.0, q2)], axis=0)
    lane = lax.broadcasted_iota(jnp.int32, (rows2, LANES), 1)
    rowi = lax.broadcasted_iota(jnp.int32, (rows2, LANES), 0)
    gate = lax.dot_general(qh.astype(BF16), kbar_ref[...].astype(BF16), nt, preferred_element_type=F32)
    g = jnp.where(lane < qb, gate, -jnp.inf)
    chosen = lane < 0
    lane_f = lane.astype(F32)
    for _ in range(MOBA_TOPK):
        m = jnp.max(g, axis=1, keepdims=True)
        idx = jnp.min(jnp.where(g == m, lane_f, float(LANES)), axis=1, keepdims=True)
        hit = (lane_f == idx) & (m > -jnp.inf)
        chosen = chosen | hit
        g = jnp.where(hit, -jnp.inf, g)
    nfar = qb - 1
    bfar = jnp.where(rowi < blk, bfar_ref[0, 0:1, 0:1], bfar_ref[1, 0:1, 0:1])
    bhi = bfar.astype(BF16).astype(F32)
    madd = jnp.where(lane < nfar, jnp.where(chosen, bhi, NEG),
                     jnp.where(lane == nfar, jnp.where(chosen, 0.0, NEG),
                               jnp.where((lane >= MOBA_LO) & (lane - MOBA_LO < nfar), bfar - bhi, 0.0)))
    q_aug = jnp.concatenate([(qh * scale).astype(BF16), madd.astype(BF16)], axis=1)

    prev0 = pl.multiple_of(jnp.maximum(nfar, 0) * blk, blk)
    own0 = pl.multiple_of(qb * blk, blk)
    s_prev = (lax.dot_general(q_aug, kb_ref[pl.ds(prev0, blk), :], nt, preferred_element_type=F32)
              + bprev_ref[...].reshape(rows2, blk) + jnp.where(qb > 0, 0.0, NEG))
    s_own = (lax.dot_general(q_aug, kb_ref[pl.ds(own0, blk), :], nt, preferred_element_type=F32)
             + bown_ref[...].reshape(rows2, blk))
    r = lax.broadcasted_iota(jnp.int32, (rows2, blk), 0)
    c = lax.broadcasted_iota(jnp.int32, (rows2, blk), 1)
    s_own = jnp.where(lax.bitwise_and(r, blk - 1) >= c, s_own, NEG)
    s = jnp.concatenate([s_prev, s_own], axis=1)
    m_i = jnp.max(s, axis=1, keepdims=True)
    p = jnp.exp(s - m_i)
    l_i = jnp.sum(p, axis=1, keepdims=True)
    v0 = jnp.concatenate([vb_ref[pl.ds(prev0, blk), :], vb_ref[pl.ds(own0, blk), :]], axis=0)
    acc = jnp.dot(p.astype(BF16), v0, preferred_element_type=F32)

    def body(it, carry):
        m_i, l_i, acc = carry
        k0 = pl.multiple_of(it * rows2, rows2)
        s = lax.dot_general(q_aug, kb_ref[pl.ds(k0, rows2), :], nt, preferred_element_type=F32)
        tail = jnp.where(2 * it + 1 < nfar, 0.0, NEG)
        s = jnp.concatenate([s[:, :blk], s[:, blk:] + tail], axis=1)
        m_new = jnp.maximum(m_i, jnp.max(s, axis=1, keepdims=True))
        alpha = jnp.exp(m_i - m_new)
        p = jnp.exp(s - m_new)
        l_new = alpha * l_i + jnp.sum(p, axis=1, keepdims=True)
        acc_new = alpha * acc + jnp.dot(p.astype(BF16), vb_ref[pl.ds(k0, rows2), :], preferred_element_type=F32)
        return m_new, l_new, acc_new

    m_i, l_i, acc = lax.fori_loop(0, (jnp.maximum(nfar, 0) + 1) // 2, body, (m_i, l_i, acc))
    out = acc / l_i
    o_ref[0] = jnp.where(first, out[:blk], out[blk:]).astype(o_ref.dtype)


def moba_attention(p3d, rel_bias):
    bsz, seq, _ = p3d.shape
    blk = MOBA_BLOCK
    n_blocks = seq // blk
    assert n_blocks <= MOBA_LO and seq % blk == 0
    span = 2 * blk
    by_dist = rel_bias[:, _rel_bucket(jnp.arange(span))].astype(F32)
    shift = jnp.arange(span)

    def toeplitz(c):
        k = jnp.where(shift < blk, shift, shift - span)
        s = by_dist[:, jnp.clip(c - k, 0, span - 1)]
        tiled = jnp.tile(s, (1, blk))[:, :blk * (span - 1)]
        return tiled.reshape(HEADS, blk, span - 1)[:, :, :blk]

    bias_own = toeplitz(0)
    bias_prev = toeplitz(blk)
    bias_far = jnp.broadcast_to(rel_bias[:, REL_BUCKETS - 1].astype(F32)[:, None, None], (HEADS, 8, LANES))
    kern = functools.partial(_moba_kernel, n_blocks=n_blocks)
    return pl.pallas_call(
        kern,
        grid=(bsz, PAIRS, n_blocks),
        in_specs=[
            pl.BlockSpec((1, blk, LANES), lambda b, h, i: (b, i, h)),
            pl.BlockSpec((1, seq, LANES), lambda b, h, i: (b, 0, PAIRS + h)),
            pl.BlockSpec((1, seq, LANES), lambda b, h, i: (b, 0, 2 * PAIRS + h)),
            pl.BlockSpec((2, blk, blk), lambda b, h, i: (h, 0, 0)),
            pl.BlockSpec((2, blk, blk), lambda b, h, i: (h, 0, 0)),
            pl.BlockSpec((2, 8, LANES), lambda b, h, i: (h, 0, 0)),
        ],
        out_specs=pl.BlockSpec((1, blk, LANES), lambda b, h, i: (b, i, h)),
        out_shape=jax.ShapeDtypeStruct((bsz, seq, WIDTH), BF16),
        scratch_shapes=[
            pltpu.VMEM((seq, 2 * LANES), BF16),
            pltpu.VMEM((seq, LANES), BF16),
            pltpu.VMEM((LANES, LANES), F32),
        ],
        compiler_params=_cparams(("parallel", "parallel", "arbitrary")),
        name="moba",
    )(p3d, p3d, p3d, bias_own, bias_prev, bias_far)


def _shifted(x, carry_row):
    rows = lax.broadcasted_iota(jnp.int32, x.shape, 0)
    return jnp.where(rows == 0, carry_row, pltpu.roll(x, 1, axis=0))


def _rwkv_prep_kernel(pr_ref, pk_ref, pv_ref, pl_ref, mu_ref, vec_ref, ww_ref, wa_ref, wg_ref,
                      bd_ref, tri_ref,
                      rt_ref, kt_ref, kd_ref, bd_out_ref, v_ref, g_ref, bonus_ref, pend_ref,
                      carry_ref, *, chunk):
    @pl.when(pl.program_id(1) == 0)
    def _():
        carry_ref[...] = jnp.zeros_like(carry_ref)

    def mix(ref, j):
        x = ref[0]
        mu = mu_ref[0:1, j * WIDTH:(j + 1) * WIDTH]
        prev = _shifted(x, carry_ref[0:1, j * WIDTH:(j + 1) * WIDTH])
        carry_ref[0:1, j * WIDTH:(j + 1) * WIDTH] = x[x.shape[0] - 1:, :]
        return x + mu * (prev - x)

    r = mix(pr_ref, 0)
    k = mix(pk_ref, 1)
    v = mix(pv_ref, 2)
    lo = mix(pl_ref, 3)
    w0, a0, k_k, k_a, r_k = (vec_ref[i:i + 1, :] for i in range(5))
    xwa = lo[:, 0:LANES]
    xg = lo[:, LANES:3 * LANES]
    lw = jnp.dot(jnp.tanh(xwa), ww_ref[...], precision=HI, preferred_element_type=F32)
    la = jnp.dot(xwa, wa_ref[...], precision=HI, preferred_element_type=F32)
    g = jnp.dot(jax.nn.sigmoid(xg), wg_ref[...], precision=HI, preferred_element_type=F32)
    z = -(w0 + lw)
    softplus = jnp.maximum(z, 0.0) + jnp.log(1.0 + jnp.exp(-jnp.abs(z)))
    logw = -jnp.exp(-softplus - 0.5)
    a = jax.nn.sigmoid(a0 + la)
    kk = k * k_k
    ss = jnp.dot(kk * kk, bd_ref[...], precision=HI, preferred_element_type=F32)
    kk = kk / jnp.maximum(jnp.sqrt(ss), 1e-12)
    k2 = k * (1.0 + (a - 1.0) * k_a)
    rk = jnp.dot(r * k2 * r_k, bd_ref[...], precision=HI, preferred_element_type=F32)
    cs = jnp.dot(tri_ref[...], logw, precision=HI, preferred_element_type=F32)
    e_pos = jnp.exp(cs)
    e_neg = jnp.exp(-cs)
    rt_ref[0] = (r * e_pos).astype(rt_ref.dtype)
    kt_ref[0] = (kk * jnp.exp(cs - logw)).astype(kt_ref.dtype)
    kd_ref[0] = (k2 * e_neg).astype(kd_ref.dtype)
    bd_out_ref[0] = (kk * a * e_neg).astype(bd_out_ref.dtype)
    v_ref[0] = v.astype(v_ref.dtype)
    g_ref[0] = g
    bonus_ref[0] = rk * v
    ts = e_pos.shape[0]
    for c in range(ts // chunk):
        pend_ref[0, c:c + 1, :] = e_pos[(c + 1) * chunk - 1:(c + 1) * chunk, :]


def rwkv_prep(p3d, rwkv_mu, w0, w_lora_up, a0, a_lora_up, g_lora_up, k_k, k_a, r_k, *, ts=512):
    bsz, seq, _ = p3d.shape
    chunk = RWKV_CHUNK
    ts = min(ts, seq)
    mu = jnp.pad(rwkv_mu, (0, COL_B - COL_B_RAW)).reshape(1, COL_B)
    vec = jnp.stack([w0, a0, k_k, k_a, r_k.reshape(-1)] + [jnp.zeros_like(w0)] * 3).astype(F32)
    ww = jnp.zeros((LANES, WIDTH), F32).at[:DECAY_LORA].set(w_lora_up)
    wa = jnp.zeros((LANES, WIDTH), F32).at[DECAY_LORA:DECAY_LORA + AAA_LORA].set(a_lora_up)
    wg = jnp.zeros((2 * LANES, WIDTH), F32).at[:GATE_LORA].set(g_lora_up)
    hid = jnp.arange(WIDTH) // HEAD_DIM
    bd = (hid[:, None] == hid[None, :]).astype(F32)
    tix = jnp.arange(ts)
    tri = ((tix[:, None] // chunk == tix[None, :] // chunk) & (tix[None, :] <= tix[:, None])).astype(F32)
    c0 = 0
    big = jax.ShapeDtypeStruct((bsz, seq, WIDTH), F32)
    wspec = lambda shape: pl.BlockSpec(shape, lambda b, i: (0, 0))
    ospec = pl.BlockSpec((1, ts, WIDTH), lambda b, i: (b, i, 0))
    return pl.pallas_call(
        functools.partial(_rwkv_prep_kernel, chunk=chunk),
        grid=(bsz, seq // ts),
        in_specs=[
            pl.BlockSpec((1, ts, WIDTH), lambda b, i: (b, i, c0)),
            pl.BlockSpec((1, ts, WIDTH), lambda b, i: (b, i, c0 + 1)),
            pl.BlockSpec((1, ts, WIDTH), lambda b, i: (b, i, c0 + 2)),
            pl.BlockSpec((1, ts, WIDTH), lambda b, i: (b, i, c0 + 3)),
            wspec((1, COL_B)), wspec((8, WIDTH)), wspec((LANES, WIDTH)), wspec((LANES, WIDTH)),
            wspec((2 * LANES, WIDTH)), wspec((WIDTH, WIDTH)), wspec((ts, ts)),
        ],
        out_specs=[ospec] * 7 + [pl.BlockSpec((1, ts // chunk, WIDTH), lambda b, i: (b, i, 0))],
        out_shape=[jax.ShapeDtypeStruct((bsz, seq, WIDTH), BF16)] * 5 + [big] * 2
        + [jax.ShapeDtypeStruct((bsz, seq // chunk, WIDTH), F32)],
        scratch_shapes=[pltpu.VMEM((8, COL_B), F32)],
        compiler_params=_cparams(("parallel", "arbitrary")),
        name="rwkv_prep",
    )(p3d, p3d, p3d, p3d, mu, vec, ww, wa, wg, bd, tri)


def _rwkv_scan_kernel(rt_ref, kt_ref, kd_ref, bd_ref, v_ref, g_ref, bonus_ref, pend_ref, ln_ref, o_ref,
                      state_ref, *, chunk, cps, prec):
    @pl.when(pl.program_id(1) == 0)
    def _():
        state_ref[...] = jnp.zeros_like(state_ref)

    c2 = 2 * chunk
    lane = lax.broadcasted_iota(jnp.int32, (chunk, LANES), 1)
    first = lane < HEAD_DIM
    row = lax.broadcasted_iota(jnp.int32, (c2, c2), 0)
    col = lax.broadcasted_iota(jnp.int32, (c2, c2), 1)
    eye = (row == col).astype(F32)
    hrow = lax.broadcasted_iota(jnp.int32, (LANES, LANES), 0) // HEAD_DIM
    hcol = lax.broadcasted_iota(jnp.int32, (LANES, LANES), 1) // HEAD_DIM
    head_mean = jnp.where(hrow == hcol, 1.0 / HEAD_DIM, 0.0).astype(F32)
    nt = (((1,), (1,)), ((), ()))
    tn = (((0,), (0,)), ((), ()))
    dot = functools.partial(jnp.dot, precision=prec, preferred_element_type=F32)
    dotg = functools.partial(lax.dot_general, precision=prec, preferred_element_type=F32)

    def stack(x):
        return jnp.concatenate([jnp.where(first, x, 0.0), jnp.where(first, 0.0, x)], axis=0)

    pairs = range(PAIRS)
    units = [(j, hp) for j in range(cps) for hp in pairs]
    sls = [slice(hp * LANES, (hp + 1) * LANES) for hp in pairs]
    rows_of = [slice(j * chunk, (j + 1) * chunk) for j in range(cps)]
    rs, ks, kds, bs, vs = ({(j, hp): stack(ref[0, rows_of[j], sls[hp]].astype(F32)) for j, hp in units}
                           for ref in (rt_ref, kt_ref, kd_ref, bd_ref, v_ref))
    big = {u: dotg(jnp.concatenate([ks[u], rs[u]], axis=0), jnp.concatenate([bs[u], kds[u]], axis=0), nt)
           for u in units}
    a_b = {u: jnp.where(row > col, big[u][0:c2, 0:c2], 0.0) for u in units}
    a_k = {u: jnp.where(row > col, big[u][0:c2, c2:], 0.0) for u in units}
    a_rb = {u: jnp.where(row >= col, big[u][c2:, 0:c2], 0.0) for u in units}
    a_rk = {u: jnp.where(row >= col, big[u][c2:, c2:], 0.0) for u in units}
    av = {u: dot(jnp.concatenate([a_k[u], a_rk[u]], axis=0), vs[u]) for u in units}
    vk = {u: dotg(vs[u], kds[u], tn) for u in units}
    inv = {u: eye - a_b[u] for u in units}
    pw = {u: dot(a_b[u], a_b[u]) for u in units}
    n_sq = int(math.log2(chunk)) - 1
    for lvl in range(n_sq):
        if lvl + 1 < n_sq:
            both = {u: dot(jnp.concatenate([inv[u], pw[u]], axis=0), pw[u]) for u in units}
            inv = {u: inv[u] + both[u][0:c2] for u in units}
            pw = {u: both[u][c2:] for u in units}
        else:
            inv = {u: inv[u] + dot(inv[u], pw[u]) for u in units}
    hts = [state_ref[0, hp] for hp in pairs]
    for j in range(cps):
        kh = [dotg(jnp.concatenate([ks[j, hp], rs[j, hp]], axis=0), hts[hp], nt) for hp in pairs]
        us = [dot(inv[j, hp], kh[hp][0:c2] + av[j, hp][0:c2]) for hp in pairs]
        ub = [dotg(us[hp], bs[j, hp], tn) for hp in pairs]
        au = [dot(a_rb[j, hp], us[hp]) for hp in pairs]
        for hp in pairs:
            sl = sls[hp]
            pend = pend_ref[0, j, 0:1, sl]
            hts[hp] = (hts[hp] + vk[j, hp] - ub[hp]) * pend
            os_ = kh[hp][c2:] + av[j, hp][c2:] - au[hp]
            o = os_[0:chunk] + os_[chunk:]
            mu = jnp.dot(o, head_mean, precision=HI, preferred_element_type=F32)
            d = o - mu
            var = jnp.dot(d * d, head_mean, precision=HI, preferred_element_type=F32)
            on = d * lax.rsqrt(var + GN_EPS) * ln_ref[0:1, sl] + ln_ref[1:2, sl]
            o_ref[0, rows_of[j], sl] = ((on + bonus_ref[0, rows_of[j], sl]) * g_ref[0, rows_of[j], sl]
                                        ).astype(o_ref.dtype)
    for hp in pairs:
        state_ref[0, hp] = hts[hp]


def rwkv_scan(rt, kt, kd, bd, v, g, bonus, pend, lnx_g, lnx_b, *, prec=None):
    bsz, seq, _ = rt.shape
    chunk = RWKV_CHUNK
    n_chunks = seq // chunk
    ln = jnp.stack([lnx_g, lnx_b] + [jnp.zeros_like(lnx_g)] * 6).astype(F32)
    pend4 = pend.reshape(bsz, n_chunks, 1, WIDTH)
    cps = RWKV_CHUNKS_PER_STEP if n_chunks % RWKV_CHUNKS_PER_STEP == 0 else 1
    spec = pl.BlockSpec((1, cps * chunk, WIDTH), lambda b, c: (b, c, 0))
    return pl.pallas_call(
        functools.partial(_rwkv_scan_kernel, chunk=chunk, cps=cps, prec=prec),
        grid=(bsz, n_chunks // cps),
        in_specs=[spec] * 7 + [
            pl.BlockSpec((1, cps, 1, WIDTH), lambda b, c: (b, c, 0, 0)),
            pl.BlockSpec((8, WIDTH), lambda b, c: (0, 0)),
        ],
        out_specs=spec,
        out_shape=jax.ShapeDtypeStruct((bsz, seq, WIDTH), BF16),
        scratch_shapes=[pltpu.VMEM((1, PAIRS, LANES, LANES), F32)],
        compiler_params=_cparams(("parallel", "arbitrary")),
        name="rwkv_scan",
    )(rt, kt, kd, bd, v, g, bonus, pend4, ln)


def _merge_kernel(x_ref, oa_ref, ob_ref, ga_ref, gb_ref, wa_ref, wb_ref, wo_ref, g2_ref,
                  h_ref, xn_ref, acc_ref):
    j = pl.program_id(1)

    @pl.when(j == 0)
    def _():
        acc_ref[...] = x_ref[...]

    ya = jnp.dot(oa_ref[...].astype(BF16), wa_ref[...], preferred_element_type=F32)
    yb = jnp.dot(ob_ref[...].astype(BF16), wb_ref[...], preferred_element_type=F32)
    y = jax.nn.sigmoid(ga_ref[...].astype(F32)) * ya + jax.nn.sigmoid(gb_ref[...].astype(F32)) * yb
    acc_ref[...] += jnp.dot(y.astype(BF16), wo_ref[...], preferred_element_type=F32)

    @pl.when(j == pl.num_programs(1) - 1)
    def _():
        h = acc_ref[...]
        h_ref[...] = h
        ms = jnp.mean(h * h, axis=-1, keepdims=True)
        xn_ref[...] = _pack_halves(h * lax.rsqrt(ms + RMS_EPS) * g2_ref[...])


def _pack_halves(x):
    half = x.shape[1] // 2
    lo = lax.bitcast_convert_type(x[:, :half].astype(BF16).astype(F32), jnp.int32)
    hi = lax.bitcast_convert_type(x[:, half:].astype(BF16).astype(F32), jnp.int32)
    return lax.bitwise_or(lax.shift_right_logical(lo, jnp.int32(16)), hi)


def _unpack_halves(words):
    lo, hi = _unpack_words(words)
    return jnp.concatenate([lo, hi], axis=1)


def merge_out(x2d, oa, ob, p2d, w_proj_a, w_proj_b, w_out, norm2_g, *, row0=0, tm=1024):
    t, d = oa.shape[0], x2d.shape[1]
    r0 = row0 // tm
    tn = WIDTH
    nj = d // tn
    g0 = 0
    return pl.pallas_call(
        _merge_kernel,
        grid=(t // tm, nj),
        in_specs=[
            pl.BlockSpec((tm, d), lambda i, j: (r0 + i, 0)),
            pl.BlockSpec((tm, WIDTH), lambda i, j: (i, 0)),
            pl.BlockSpec((tm, WIDTH), lambda i, j: (i, 0)),
            pl.BlockSpec((tm, tn), lambda i, j: (i, g0 + j)),
            pl.BlockSpec((tm, tn), lambda i, j: (i, g0 + nj + j)),
            pl.BlockSpec((WIDTH, tn), lambda i, j: (0, j)),
            pl.BlockSpec((WIDTH, tn), lambda i, j: (0, j)),
            pl.BlockSpec((tn, d), lambda i, j: (j, 0)),
            pl.BlockSpec((1, d), lambda i, j: (0, 0)),
        ],
        out_specs=[pl.BlockSpec((tm, d), lambda i, j: (i, 0)), pl.BlockSpec((tm, d // 2), lambda i, j: (i, 0))],
        out_shape=[jax.ShapeDtypeStruct((t, d), F32), jax.ShapeDtypeStruct((t, d // 2), jnp.int32)],
        scratch_shapes=[pltpu.VMEM((tm, d), F32)],
        compiler_params=_cparams(("parallel", "arbitrary")),
        name="merge_out",
    )(x2d, oa, ob, p2d, p2d, w_proj_a.astype(BF16), w_proj_b.astype(BF16), w_out.astype(BF16),
      norm2_g.reshape(1, d))


PEER_HEADS = 8
PEER_NKEYS = 128
PEER_TOPK = 16
PEER_HALF = 128


def _topk_rows(s, k):
    n = s.shape[0]
    rows = lax.broadcasted_iota(jnp.int32, s.shape, 0).astype(F32)
    vals, ids = [], []
    for _ in range(k):
        m = jnp.max(s, axis=0, keepdims=True)
        first = jnp.min(jnp.where(s == m, rows, float(n)), axis=0, keepdims=True)
        vals.append(m)
        ids.append(first)
        s = jnp.where(rows == first, -jnp.inf, s)
    return jnp.concatenate(vals, axis=0), jnp.concatenate(ids, axis=0)


def _take_rows(table, ids):
    rows = lax.broadcasted_iota(jnp.int32, table.shape, 0).astype(F32)
    return jnp.sum(jnp.where(rows == ids, table, 0.0), axis=0, keepdims=True)


def _peer_route_kernel(xn_ref, wq_ref, sk_ref, idx_ref, gate_ref, *, prec):
    tt = xn_ref.shape[0]
    k = PEER_TOPK
    xn = _unpack_halves(xn_ref[...]) if xn_ref.dtype == jnp.int32 else xn_ref[...]
    q = jnp.dot(xn.astype(wq_ref.dtype), wq_ref[...], precision=prec, preferred_element_type=F32)
    nt = (((1,), (1,)), ((), ()))
    idx_rows, gate_rows = [], []
    half = k // 2
    for h in range(PEER_HEADS):
        tops = []
        for p in range(2):
            c0 = (h * 2 + p) * PEER_HALF
            s = lax.dot_general(sk_ref[h, p].astype(wq_ref.dtype), q[:, c0:c0 + PEER_HALF].astype(wq_ref.dtype),
                                nt, precision=prec, preferred_element_type=F32)
            tops.append(_topk_rows(s, k))
        (s0, i0), (s1, i1) = tops
        cs = [s0[0:1] + s1] + [s0[i:i + 1] + s1[0:half] for i in range(1, half)] + [s0[half:] + s1[0:1]]
        best_s, pos = _topk_rows(jnp.concatenate(cs, axis=0), k)
        mid = jnp.floor((pos - k) * (1.0 / half))
        end_mid = float(k + (half - 1) * half)
        i_rank = jnp.where(pos < k, 0.0, jnp.where(pos < end_mid, 1.0 + mid, pos - (end_mid - half)))
        j_rank = jnp.where(pos < k, pos, jnp.where(pos < end_mid, (pos - k) - half * mid, 0.0))
        ids = [_take_rows(i0, i_rank[n:n + 1]) * PEER_NKEYS + _take_rows(i1, j_rank[n:n + 1]) for n in range(k)]
        e = jnp.exp(best_s - best_s[0:1])
        gate_rows.append(e / jnp.sum(e, axis=0, keepdims=True))
        idx_rows.append(jnp.concatenate(ids, axis=0).astype(jnp.int32))
    idx_ref[...] = jnp.concatenate(idx_rows, axis=0).T
    gate_ref[...] = jnp.concatenate(gate_rows, axis=0).T


def peer_route(xn2d, peer_wq, peer_subkeys, *, tt=256, prec=None, wdtype=BF16):
    t, dx = xn2d.shape
    d, nq = peer_wq.shape
    n_sel = PEER_HEADS * PEER_TOPK
    return pl.pallas_call(
        functools.partial(_peer_route_kernel, prec=prec),
        grid=(t // tt,),
        in_specs=[
            pl.BlockSpec((tt, dx), lambda i: (i, 0)),
            pl.BlockSpec((d, nq), lambda i: (0, 0)),
            pl.BlockSpec((PEER_HEADS, 2, PEER_NKEYS, PEER_HALF), lambda i: (0, 0, 0, 0)),
        ],
        out_specs=[pl.BlockSpec((tt, n_sel), lambda i: (i, 0))] * 2,
        out_shape=[jax.ShapeDtypeStruct((t, n_sel), jnp.int32), jax.ShapeDtypeStruct((t, n_sel), F32)],
        compiler_params=_cparams(("parallel",)),
        name="peer_route",
    )(xn2d, peer_wq.astype(wdtype), peer_subkeys)


def _final_kernel(h_ref, y_ref, g_ref, *rest):
    o_ref = rest[-1]
    h = h_ref[...] + y_ref[...]
    ms = jnp.mean(h * h, axis=-1, keepdims=True)
    o_ref[...] = h * lax.rsqrt(ms + RMS_EPS) * g_ref[...]


def final_norm(h2d, y2d, g, *, out=None, row0=0, total_rows=None, tm=1024):
    t, d = h2d.shape
    total = t if total_rows is None else total_rows
    r0 = row0 // tm
    spec = pl.BlockSpec((tm, d), lambda i: (i, 0))
    in_specs = [spec, spec, pl.BlockSpec((1, d), lambda i: (0, 0))]
    args = [h2d, y2d, g.reshape(1, d)]
    aliases = {}
    if out is not None:
        in_specs.append(pl.BlockSpec(memory_space=pl.ANY))
        args.append(out)
        aliases = {3: 0}
    return pl.pallas_call(
        _final_kernel,
        grid=(t // tm,),
        in_specs=in_specs,
        out_specs=pl.BlockSpec((tm, d), lambda i: (r0 + i, 0)),
        out_shape=jax.ShapeDtypeStruct((total, d), F32),
        input_output_aliases=aliases,
        compiler_params=_cparams(("parallel",)),
        name="final_norm",
    )(*args)


SC_CORES = 2
SC_SUBCORES = 16
SC_LANES = 16
SC_WORKERS = SC_CORES * SC_SUBCORES
PEER_SEL = PEER_HEADS * PEER_TOPK
PEER_ROWS = 32
PEER_PARTS = PEER_SEL // PEER_ROWS
PEER_NBUF = 4
PEER_GROUP = 64
PEER_BF16_RUN = 4
PEER_ROW_PAIR = 4


def _pack_rows_kernel(w_ref, o_ref):
    o_ref[...] = _pack_halves(w_ref[...])


def _pack_rows(w, *, tr=1024):
    e, d = w.shape
    return pl.pallas_call(
        _pack_rows_kernel,
        grid=(e // tr,),
        in_specs=[pl.BlockSpec((tr, d), lambda i: (i, 0))],
        out_specs=pl.BlockSpec((tr, d // 2), lambda i: (i, 0)),
        out_shape=jax.ShapeDtypeStruct((e, d // 2), jnp.int32),
        compiler_params=_cparams(("parallel",)),
        name="pack_rows",
    )(w)


def _unpack_words(w):
    lo = lax.bitcast_convert_type(lax.shift_left(w, jnp.int32(16)), F32)
    hi = lax.bitcast_convert_type(lax.bitwise_and(w, jnp.int32(-65536)), F32)
    return lo, hi


def _packed_dot(a_words, b_words):
    from jax.experimental.pallas import tpu_sc as plsc
    prods = [plsc.bitcast(a, BF16) * plsc.bitcast(b, BF16) for a, b in zip(a_words, b_words)]
    while len(prods) > 1:
        prods = [prods[k] + prods[k + 1] for k in range(0, len(prods), 2)]
    return _unpack_words(plsc.bitcast(prods[0], jnp.int32))


def _sc_mesh():
    from jax.experimental.pallas import tpu_sc as plsc
    return plsc.VectorSubcoreMesh(core_axis_name="c", subcore_axis_name="s",
                                  num_cores=SC_CORES, num_subcores=SC_SUBCORES)


def _sc_loop(n, body, carry):
    from jax.experimental.pallas import tpu_sc as plsc
    return plsc.parallel_loop(0, n, carry=carry)(body)


def _worker_base(tokens_per_worker):
    return (lax.axis_index("s") * SC_CORES + lax.axis_index("c")) * tokens_per_worker


def _gather_compute_loop(table_hbm, idx_v, rows_v, sem, stage_v, out_row, osem, grp, compute):
    n_gathers = PEER_PARTS * grp
    ahead = PEER_NBUF - 1

    def gather(j, b):
        i = j // PEER_PARTS if isinstance(j, int) else lax.shift_right_logical(j, PEER_PARTS.bit_length() - 1)
        h = j % PEER_PARTS if isinstance(j, int) else lax.bitwise_and(j, PEER_PARTS - 1)
        ids = idx_v.at[i, pl.ds(pl.multiple_of(h * PEER_ROWS, PEER_ROWS), PEER_ROWS)]
        return pltpu.make_async_copy(table_hbm.at[ids], rows_v.at[b], sem.at[b])

    def put(i, slot):
        return pltpu.make_async_copy(stage_v.at[slot], out_row(i), osem.at[slot])

    for j in range(ahead):
        gather(j, j).start()

    @pl.loop(0, n_gathers)
    def _(j):
        b = lax.bitwise_and(j, PEER_NBUF - 1)
        h = lax.bitwise_and(j, PEER_PARTS - 1)
        i = lax.shift_right_logical(j, PEER_PARTS.bit_length() - 1)
        slot = lax.bitwise_and(i, 1)

        @pl.when((h == 0) & (i >= 2))
        def _():
            put(i - 2, slot).wait()

        @pl.when(j + ahead < n_gathers)
        def _():
            gather(j + ahead, lax.bitwise_and(j + ahead, PEER_NBUF - 1)).start()

        gather(j, b).wait()
        compute(i, h, b, slot)

        @pl.when(h == PEER_PARTS - 1)
        def _():
            put(i, slot).start()

    put(grp - 2, 0).wait()
    put(grp - 1, 1).wait()


def peer_expert_dots(x_packed, idx, u_packed):
    t, half = x_packed.shape
    n_chunks = half // SC_LANES
    tpw = t // SC_WORKERS
    igrp = min(2 * PEER_GROUP, tpw)
    grp = min(PEER_GROUP, igrp)
    assert t % SC_WORKERS == 0 and tpw % igrp == 0 and igrp % grp == 0 and grp & (grp - 1) == 0
    rows_tog = 2 * PEER_ROW_PAIR
    assert igrp % 2 == 0 and idx.shape == (t, PEER_SEL) and PEER_ROWS % rows_tog == 0

    def body(x_hbm, idx_hbm, u_hbm, out_hbm, idx_v, x_v, rows_v, ps_v, sem, osem):
        base = _worker_base(tpw)

        def compute(t0, i, h, b, slot):
            @pl.when((h == 0) & (lax.bitwise_and(i, grp - 1) == 0))
            def _():
                pltpu.sync_copy(x_hbm.at[pl.ds(pl.multiple_of(t0 + i, grp), grp)], x_v)

            ix = lax.bitwise_and(i, grp - 1)

            @pl.loop(0, PEER_ROWS // rows_tog)
            def _(rg):
                r0 = rg * rows_tog
                accs = [[None, None] for _ in range(rows_tog)]
                for c0 in range(0, n_chunks, PEER_BF16_RUN):
                    ats = [pl.ds((c0 + k) * SC_LANES, SC_LANES) for k in range(PEER_BF16_RUN)]
                    xw = [x_v[ix, at] for at in ats]
                    for r in range(rows_tog):
                        terms = _packed_dot([rows_v[b, r0 + r, at] for at in ats], xw)
                        for k, term in enumerate(terms):
                            accs[r][k] = term if accs[r][k] is None else accs[r][k] + term
                pair = rows_tog // 2
                bits = [lax.bitcast_convert_type(accs[r][0] + accs[r][1], jnp.int32) + jnp.int32(0x8000)
                        for r in range(rows_tog)]
                for r in range(pair):
                    word = lax.bitwise_or(lax.shift_right_logical(bits[r], jnp.int32(16)),
                                          lax.bitwise_and(bits[r + pair], jnp.int32(-65536)))
                    w0 = lax.shift_right_logical(h * PEER_ROWS + r0, 1) + r
                    ps_v[slot, pl.ds(pl.multiple_of(w0 * SC_LANES, SC_LANES), SC_LANES)] = word

        @pl.loop(0, tpw // igrp)
        def _(g):
            t0 = base + g * igrp
            pltpu.sync_copy(idx_hbm.at[pl.ds(t0, igrp)], idx_v)
            _gather_compute_loop(u_hbm, idx_v, rows_v, sem, ps_v, lambda i: out_hbm.at[t0 + i], osem, igrp,
                                 functools.partial(compute, t0))

    return pl.kernel(
        body,
        out_type=jax.ShapeDtypeStruct((t, PEER_SEL * SC_LANES // 2), jnp.int32),
        mesh=_sc_mesh(),
        scratch_types=[
            pltpu.VMEM((igrp, PEER_SEL), jnp.int32),
            pltpu.VMEM((grp, half), jnp.int32),
            pltpu.VMEM((PEER_NBUF, PEER_ROWS, half), jnp.int32),
            pltpu.VMEM((2, PEER_SEL * SC_LANES // 2), jnp.int32),
            pltpu.SemaphoreType.DMA((PEER_NBUF,)),
            pltpu.SemaphoreType.DMA((2,)),
        ],
        compiler_params=pltpu.CompilerParams(needs_layout_passes=False),
        name="peer_expert_dots",
    )(x_packed, idx, u_packed)


def peer_expert_mix(hgw, idx, v_packed):
    t = hgw.shape[0]
    half = v_packed.shape[1]
    d = 2 * half
    tpw = t // SC_WORKERS
    grp = min(2 * PEER_GROUP, tpw)
    assert t % SC_WORKERS == 0 and tpw % grp == 0 and grp % 2 == 0 and idx.shape == (t, PEER_SEL)
    n_parts = 2
    cpp = half // SC_LANES // n_parts
    from jax.experimental.pallas import tpu_sc as plsc

    def body(hg_hbm, idx_hbm, v_hbm, out_hbm, idx_v, hg_v, rows_v, o_v2, sem, osem):
        base = _worker_base(tpw)

        def compute(i, h, b, slot):
            token = jnp.full((SC_LANES,), i, jnp.int32)
            for part in range(n_parts):
                def rbody(rq, accs):
                    r0 = rq * PEER_BF16_RUN
                    s = [plsc.load_gather(hg_v, [token, jnp.full((SC_LANES,), h * PEER_ROWS + r0 + k, jnp.int32)])
                         for k in range(PEER_BF16_RUN)]
                    new = []
                    for c in range(cpp):
                        at = pl.ds((part * cpp + c) * SC_LANES, SC_LANES)
                        lo, hi = _packed_dot([rows_v[b, r0 + k, at] for k in range(PEER_BF16_RUN)], s)
                        new.append(accs[2 * c] + lo)
                        new.append(accs[2 * c + 1] + hi)
                    return tuple(new)

                accs = _sc_loop(PEER_ROWS // PEER_BF16_RUN, rbody,
                                tuple(jnp.zeros((SC_LANES,), F32) for _ in range(2 * cpp)))
                def store(overwrite):
                    for c in range(cpp):
                        lo_at = pl.ds((part * cpp + c) * SC_LANES, SC_LANES)
                        hi_at = pl.ds(half + (part * cpp + c) * SC_LANES, SC_LANES)
                        if overwrite:
                            o_v2[slot, lo_at] = accs[2 * c]
                            o_v2[slot, hi_at] = accs[2 * c + 1]
                        else:
                            o_v2[slot, lo_at] = o_v2[slot, lo_at] + accs[2 * c]
                            o_v2[slot, hi_at] = o_v2[slot, hi_at] + accs[2 * c + 1]

                pl.when(h == 0)(functools.partial(store, True))
                pl.when(h != 0)(functools.partial(store, False))

        @pl.loop(0, tpw // grp)
        def _(g):
            t0 = base + g * grp
            pltpu.sync_copy(idx_hbm.at[pl.ds(t0, grp)], idx_v)
            pltpu.sync_copy(hg_hbm.at[pl.ds(t0, grp)], hg_v)
            _gather_compute_loop(v_hbm, idx_v, rows_v, sem, o_v2, lambda i: out_hbm.at[t0 + i], osem, grp, compute)

    return pl.kernel(
        body,
        out_type=jax.ShapeDtypeStruct((t, d), F32),
        mesh=_sc_mesh(),
        scratch_types=[
            pltpu.VMEM((grp, PEER_SEL), jnp.int32),
            pltpu.VMEM((grp, PEER_SEL), jnp.int32),
            pltpu.VMEM((PEER_NBUF, PEER_ROWS, half), jnp.int32),
            pltpu.VMEM((2, d), F32),
            pltpu.SemaphoreType.DMA((PEER_NBUF,)),
            pltpu.SemaphoreType.DMA((2,)),
        ],
        compiler_params=pltpu.CompilerParams(needs_layout_passes=False),
        name="peer_expert_mix",
    )(hgw, idx, v_packed)


def _peer_act_kernel(ps_ref, gate_ref, lo_ref, hi_ref, o_ref):
    lo, hi = _unpack_words(ps_ref[...])
    pre = (jnp.dot(lo.astype(BF16), lo_ref[...], preferred_element_type=F32)
           + jnp.dot(hi.astype(BF16), hi_ref[...], preferred_element_type=F32))
    hg = 0.5 * pre * (1.0 + lax.erf(pre * (1.0 / math.sqrt(2.0)))) * gate_ref[...]
    bits = lax.bitcast_convert_type(hg.astype(BF16).astype(F32), jnp.int32)
    o_ref[...] = lax.bitwise_or(bits, lax.shift_right_logical(bits, jnp.int32(16)))


def peer_act(ps, gates, *, tm=1024):
    t, n = ps.shape
    reg = jnp.arange(n) // SC_LANES
    slot_lo = 2 * PEER_ROW_PAIR * (reg // PEER_ROW_PAIR) + reg % PEER_ROW_PAIR
    place_lo = (slot_lo[:, None] == jnp.arange(PEER_SEL)[None, :]).astype(BF16)
    place_hi = ((slot_lo + PEER_ROW_PAIR)[:, None] == jnp.arange(PEER_SEL)[None, :]).astype(BF16)
    return pl.pallas_call(
        _peer_act_kernel,
        grid=(t // tm,),
        in_specs=[
            pl.BlockSpec((tm, n), lambda i: (i, 0)),
            pl.BlockSpec((tm, PEER_SEL), lambda i: (i, 0)),
            pl.BlockSpec((n, PEER_SEL), lambda i: (0, 0)),
            pl.BlockSpec((n, PEER_SEL), lambda i: (0, 0)),
        ],
        out_specs=pl.BlockSpec((tm, PEER_SEL), lambda i: (i, 0)),
        out_shape=jax.ShapeDtypeStruct((t, PEER_SEL), jnp.int32),
        compiler_params=_cparams(("parallel",)),
        name="peer_act",
    )(ps, gates, place_lo, place_hi)


BATCH_GROUPS = 8


def kernel(x, norm1_g, w_in, rwkv_mu, w0, w_lora_up, a0, a_lora_up, g_lora_up, k_k, k_a, r_k, lnx_g, lnx_b,
           w_proj_a, w_proj_b, w_out, norm2_g, peer_wq, peer_subkeys, peer_u, peer_v, rel_bias, normf_g):
    bsz, seq, d = x.shape
    depth = norm1_g.shape[0]
    groups = BATCH_GROUPS if bsz % BATCH_GROUPS == 0 else 1
    gb = bsz // groups
    tg = gb * seq
    t = bsz * seq
    src = x.reshape(t, d)
    for l in range(depth):
        w_pad = jnp.concatenate([
            w_in[l][:, :COL_A + COL_B_RAW],
            jnp.zeros((d, COL_B - COL_B_RAW), w_in.dtype),
            w_in[l][:, COL_A + COL_B_RAW:]], axis=1).astype(BF16)
        u_packed = _pack_rows(peer_u[l])
        v_packed = _pack_rows(peer_v[l])
        last = l == depth - 1

        def mix(pending, tie=None):
            row0, h2d, ps, gates, idx = pending
            hgw = peer_act(ps, gates)
            if tie is not None:
                tie, hgw = lax.optimization_barrier((tie, hgw))
            return tie, (row0, h2d, peer_expert_mix(hgw, idx, v_packed))

        outs = []

        def close(mixed):
            row0, h2d, y2d = mixed
            if last:
                outs.append(final_norm(h2d, y2d, normf_g, out=outs[-1] if outs else None, row0=row0, total_rows=t))
            else:
                outs.append(h2d + y2d)

        pending = closing = None
        for g in range(groups):
            pa, pb, pg = norm_proj(src, norm1_g[l], w_pad, row0=g * tg, rows=tg)
            if closing is not None:
                close(closing)
                pa, outs[-1] = lax.optimization_barrier((pa, outs[-1]))
                closing = None
            oa = moba_attention(pa.reshape(gb, seq, -1), rel_bias)
            prep = tuple(rwkv_prep(pb.reshape(gb, seq, -1), rwkv_mu[l], w0[l], w_lora_up[l], a0[l], a_lora_up[l], g_lora_up[l],
                                   k_k[l], k_a[l], r_k[l]))
            mixed = None
            if pending is not None:
                (oa, prep), mixed = mix(pending, (oa, prep))
            ob = rwkv_scan(*prep, lnx_g[l], lnx_b[l])
            h2d, xn2 = merge_out(src, oa.reshape(tg, WIDTH), ob.reshape(tg, WIDTH), pg, w_proj_a[l], w_proj_b[l],
                                 w_out[l], norm2_g[l], row0=g * tg)
            idx, gates = peer_route(xn2, peer_wq[l], peer_subkeys[l])
            if mixed is not None:
                idx, y2d = lax.optimization_barrier((idx, mixed[2]))
                closing = mixed[:2] + (y2d,)
            pending = (g * tg, h2d, peer_expert_dots(xn2, idx, u_packed), gates, idx)
        if closing is not None:
            close(closing)
        close(mix(pending)[1])
        src = outs[-1] if last else jnp.concatenate(outs, axis=0)
    return src.reshape(bsz, seq, d)
```

```python
import functools
import math

import jax
import jax.numpy as jnp
from jax import lax
from jax.experimental import pallas as pl
from jax.experimental.pallas import tpu as pltpu

F32 = jnp.float32
BF16 = jnp.bfloat16
HI = lax.Precision.HIGHEST

LANES = 128
HEAD_DIM = 64
HEADS = 8
PAIRS = HEADS // 2
WIDTH = HEADS * HEAD_DIM
MOBA_BLOCK = 256
MOBA_TOPK = 3
MOBA_LO = 64
REL_BUCKETS = 32
REL_MAX_DIST = 128
DECAY_LORA = 64
AAA_LORA = 64
GATE_LORA = 160
GN_EPS = 64e-5
RMS_EPS = 1e-6
NEG = -1e30
RWKV_CHUNK = 64
RWKV_CHUNKS_PER_STEP = 4
COL_A = 3 * WIDTH
COL_B_RAW = 3 * WIDTH + DECAY_LORA + AAA_LORA + GATE_LORA
COL_B = 4 * WIDTH
COL_G_OFF = COL_A + COL_B
VMEM_LIMIT = 56 * 1024 * 1024


def _cparams(sem):
    return pltpu.CompilerParams(dimension_semantics=sem, vmem_limit_bytes=VMEM_LIMIT)


def _norm_proj_kernel(x_ref, g_ref, w_ref, pa_ref, pb_ref, pg_ref, xn_ref, *, ja, jb):
    j = pl.program_id(1)

    @pl.when(j == 0)
    def _():
        x = x_ref[...]
        ms = jnp.mean(x * x, axis=-1, keepdims=True)
        xn_ref[...] = (x * lax.rsqrt(ms + RMS_EPS) * g_ref[...]).astype(xn_ref.dtype)

    res = jnp.dot(xn_ref[...], w_ref[...], preferred_element_type=F32)

    @pl.when(j < ja)
    def _():
        pa_ref[...] = res.astype(pa_ref.dtype)

    @pl.when((j >= ja) & (j < jb))
    def _():
        pb_ref[...] = res

    @pl.when(j >= jb)
    def _():
        pg_ref[...] = res.astype(pg_ref.dtype)


def norm_proj(x2d, g, w, *, row0=0, rows=None, tm=2048, tn=512):
    d = x2d.shape[1]
    t = x2d.shape[0] if rows is None else rows
    n = w.shape[1]
    r0 = row0 // tm
    ja, jb, jn = COL_A // tn, COL_G_OFF // tn, n // tn
    return pl.pallas_call(
        functools.partial(_norm_proj_kernel, ja=ja, jb=jb),
        grid=(t // tm, jn),
        in_specs=[
            pl.BlockSpec((tm, d), lambda i, j: (r0 + i, 0)),
            pl.BlockSpec((1, d), lambda i, j: (0, 0)),
            pl.BlockSpec((d, tn), lambda i, j: (0, j)),
        ],
        out_specs=[
            pl.BlockSpec((tm, tn), lambda i, j: (i, jnp.minimum(j, ja - 1))),
            pl.BlockSpec((tm, tn), lambda i, j: (i, jnp.clip(j - ja, 0, jb - ja - 1))),
            pl.BlockSpec((tm, tn), lambda i, j: (i, jnp.maximum(j - jb, 0))),
        ],
        out_shape=[jax.ShapeDtypeStruct((t, COL_A), BF16), jax.ShapeDtypeStruct((t, COL_B), F32),
                   jax.ShapeDtypeStruct((t, n - COL_G_OFF), BF16)],
        scratch_shapes=[pltpu.VMEM((tm, d), w.dtype)],
        compiler_params=_cparams(("parallel", "arbitrary")),
        name="norm_proj",
    )(x2d, g.reshape(1, d), w)


def _rel_bucket(dist):
    n = jnp.maximum(dist, 0)
    max_exact = REL_BUCKETS // 2
    nf = jnp.maximum(n, 1).astype(F32)
    large = max_exact + (jnp.log(nf / max_exact) / math.log(REL_MAX_DIST / max_exact)
                         * (REL_BUCKETS - max_exact)).astype(jnp.int32)
    large = jnp.minimum(large, REL_BUCKETS - 1)
    return jnp.where(n < max_exact, n, large)


def _moba_kernel(q_ref, k_ref, v_ref, bown_ref, bprev_ref, bfar_ref, o_ref,
                 kb_ref, vb_ref, kbar_ref, *, n_blocks):
    qb = pl.program_id(2)
    blk = MOBA_BLOCK
    scale = 1.0 / math.sqrt(HEAD_DIM)

    rows2 = 2 * blk
    nt = (((1,), (1,)), ((), ()))

    @pl.when(qb == 0)
    def _():
        kbar_ref[...] = jnp.zeros_like(kbar_ref)
        lane_b = lax.broadcasted_iota(jnp.int32, (blk, LANES), 1)
        for n in range(n_blocks):
            kblk = k_ref[0, n * blk:(n + 1) * blk, :]
            kbar_ref[n:n + 1, :] = jnp.mean(kblk.astype(F32), axis=0, keepdims=True)
            kb_ref[n * blk:(n + 1) * blk, 0:LANES] = kblk.astype(BF16)
            kb_ref[n * blk:(n + 1) * blk, LANES:] = ((lane_b == n) | (lane_b == MOBA_LO + n)).astype(BF16)
        vb_ref[...] = v_ref[0].astype(BF16)

    q2 = q_ref[0].astype(F32)
    first = lax.broadcasted_iota(jnp.int32, (blk, LANES), 1) < HEAD_DIM
    qh = jnp.concatenate([jnp.where(first, q2, 0.0), jnp.where(first, 0.0, q2)], axis=0)
    lane = lax.broadcasted_iota(jnp.int32, (rows2, LANES), 1)
    rowi = lax.broadcasted_iota(jnp.int32, (rows2, LANES), 0)
    gate = lax.dot_general(qh.astype(BF16), kbar_ref[...].astype(BF16), nt, preferred_element_type=F32)
    g = jnp.where(lane < qb, gate, -jnp.inf)
    chosen = lane < 0
    lane_f = lane.astype(F32)
    for _ in range(MOBA_TOPK):
        m = jnp.max(g, axis=1, keepdims=True)
        idx = jnp.min(jnp.where(g == m, lane_f, float(LANES)), axis=1, keepdims=True)
        hit = (lane_f == idx) & (m > -jnp.inf)
        chosen = chosen | hit
        g = jnp.where(hit, -jnp.inf, g)
    nfar = qb - 1
    bfar = jnp.where(rowi < blk, bfar_ref[0, 0:1, 0:1], bfar_ref[1, 0:1, 0:1])
    bhi = bfar.astype(BF16).astype(F32)
    madd = jnp.where(lane < nfar, jnp.where(chosen, bhi, NEG),
                     jnp.where(lane == nfar, jnp.where(chosen, 0.0, NEG),
                               jnp.where((lane >= MOBA_LO) & (lane - MOBA_LO < nfar), bfar - bhi, 0.0)))
    q_aug = jnp.concatenate([(qh * scale).astype(BF16), madd.astype(BF16)], axis=1)

    prev0 = pl.multiple_of(jnp.maximum(nfar, 0) * blk, blk)
    own0 = pl.multiple_of(qb * blk, blk)
    s_prev = (lax.dot_general(q_aug, kb_ref[pl.ds(prev0, blk), :], nt, preferred_element_type=F32)
              + bprev_ref[...].reshape(rows2, blk) + jnp.where(qb > 0, 0.0, NEG))
    s_own = (lax.dot_general(q_aug, kb_ref[pl.ds(own0, blk), :], nt, preferred_element_type=F32)
             + bown_ref[...].reshape(rows2, blk))
    r = lax.broadcasted_iota(jnp.int32, (rows2, blk), 0)
    c = lax.broadcasted_iota(jnp.int32, (rows2, blk), 1)
    s_own = jnp.where(lax.bitwise_and(r, blk - 1) >= c, s_own, NEG)
    s = jnp.concatenate([s_prev, s_own], axis=1)
    m_i = jnp.max(s, axis=1, keepdims=True)
    p = jnp.exp(s - m_i)
    l_i = jnp.sum(p, axis=1, keepdims=True)
    v0 = jnp.concatenate([vb_ref[pl.ds(prev0, blk), :], vb_ref[pl.ds(own0, blk), :]], axis=0)
    acc = jnp.dot(p.astype(BF16), v0, preferred_element_type=F32)

    def body(it, carry):
        m_i, l_i, acc = carry
        k0 = pl.multiple_of(it * rows2, rows2)
        s = lax.dot_general(q_aug, kb_ref[pl.ds(k0, rows2), :], nt, preferred_element_type=F32)
        tail = jnp.where(2 * it + 1 < nfar, 0.0, NEG)
        s = jnp.concatenate([s[:, :blk], s[:, blk:] + tail], axis=1)
        m_new = jnp.maximum(m_i, jnp.max(s, axis=1, keepdims=True))
        alpha = jnp.exp(m_i - m_new)
        p = jnp.exp(s - m_new)
        l_new = alpha * l_i + jnp.sum(p, axis=1, keepdims=True)
        acc_new = alpha * acc + jnp.dot(p.astype(BF16), vb_ref[pl.ds(k0, rows2), :], preferred_element_type=F32)
        return m_new, l_new, acc_new

    m_i, l_i, acc = lax.fori_loop(0, (jnp.maximum(nfar, 0) + 1) // 2, body, (m_i, l_i, acc))
    out = acc / l_i
    o_ref[0] = jnp.where(first, out[:blk], out[blk:]).astype(o_ref.dtype)


def moba_attention(p3d, rel_bias):
    bsz, seq, _ = p3d.shape
    blk = MOBA_BLOCK
    n_blocks = seq // blk
    assert n_blocks <= MOBA_LO and seq % blk == 0
    span = 2 * blk
    by_dist = rel_bias[:, _rel_bucket(jnp.arange(span))].astype(F32)
    shift = jnp.arange(span)

    def toeplitz(c):
        k = jnp.where(shift < blk, shift, shift - span)
        s = by_dist[:, jnp.clip(c - k, 0, span - 1)]
        tiled = jnp.tile(s, (1, blk))[:, :blk * (span - 1)]
        return tiled.reshape(HEADS, blk, span - 1)[:, :, :blk]

    bias_own = toeplitz(0)
    bias_prev = toeplitz(blk)
    bias_far = jnp.broadcast_to(rel_bias[:, REL_BUCKETS - 1].astype(F32)[:, None, None], (HEADS, 8, LANES))
    kern = functools.partial(_moba_kernel, n_blocks=n_blocks)
    return pl.pallas_call(
        kern,
        grid=(bsz, PAIRS, n_blocks),
        in_specs=[
            pl.BlockSpec((1, blk, LANES), lambda b, h, i: (b, i, h)),
            pl.BlockSpec((1, seq, LANES), lambda b, h, i: (b, 0, PAIRS + h)),
            pl.BlockSpec((1, seq, LANES), lambda b, h, i: (b, 0, 2 * PAIRS + h)),
            pl.BlockSpec((2, blk, blk), lambda b, h, i: (h, 0, 0)),
            pl.BlockSpec((2, blk, blk), lambda b, h, i: (h, 0, 0)),
            pl.BlockSpec((2, 8, LANES), lambda b, h, i: (h, 0, 0)),
        ],
        out_specs=pl.BlockSpec((1, blk, LANES), lambda b, h, i: (b, i, h)),
        out_shape=jax.ShapeDtypeStruct((bsz, seq, WIDTH), BF16),
        scratch_shapes=[
            pltpu.VMEM((seq, 2 * LANES), BF16),
            pltpu.VMEM((seq, LANES), BF16),
            pltpu.VMEM((LANES, LANES), F32),
        ],
        compiler_params=_cparams(("parallel", "parallel", "arbitrary")),
        name="moba",
    )(p3d, p3d, p3d, bias_own, bias_prev, bias_far)


def _shifted(x, carry_row):
    rows = lax.broadcasted_iota(jnp.int32, x.shape, 0)
    return jnp.where(rows == 0, carry_row, pltpu.roll(x, 1, axis=0))


def _rwkv_prep_kernel(pr_ref, pk_ref, pv_ref, pl_ref, mu_ref, vec_ref, ww_ref, wa_ref, wg_ref,
                      bd_ref, tri_ref,
                      rt_ref, kt_ref, kd_ref, bd_out_ref, v_ref, g_ref, bonus_ref, pend_ref,
                      carry_ref, *, chunk):
    @pl.when(pl.program_id(1) == 0)
    def _():
        carry_ref[...] = jnp.zeros_like(carry_ref)

    def mix(ref, j):
        x = ref[0]
        mu = mu_ref[0:1, j * WIDTH:(j + 1) * WIDTH]
        prev = _shifted(x, carry_ref[0:1, j * WIDTH:(j + 1) * WIDTH])
        carry_ref[0:1, j * WIDTH:(j + 1) * WIDTH] = x[x.shape[0] - 1:, :]
        return x + mu * (prev - x)

    r = mix(pr_ref, 0)
    k = mix(pk_ref, 1)
    v = mix(pv_ref, 2)
    lo = mix(pl_ref, 3)
    w0, a0, k_k, k_a, r_k = (vec_ref[i:i + 1, :] for i in range(5))
    xwa = lo[:, 0:LANES]
    xg = lo[:, LANES:3 * LANES]
    lw = jnp.dot(jnp.tanh(xwa), ww_ref[...], precision=HI, preferred_element_type=F32)
    la = jnp.dot(xwa, wa_ref[...], precision=HI, preferred_element_type=F32)
    g = jnp.dot(jax.nn.sigmoid(xg), wg_ref[...], precision=HI, preferred_element_type=F32)
    z = -(w0 + lw)
    softplus = jnp.maximum(z, 0.0) + jnp.log(1.0 + jnp.exp(-jnp.abs(z)))
    logw = -jnp.exp(-softplus - 0.5)
    a = jax.nn.sigmoid(a0 + la)
    kk = k * k_k
    ss = jnp.dot(kk * kk, bd_ref[...], precision=HI, preferred_element_type=F32)
    kk = kk / jnp.maximum(jnp.sqrt(ss), 1e-12)
    k2 = k * (1.0 + (a - 1.0) * k_a)
    rk = jnp.dot(r * k2 * r_k, bd_ref[...], precision=HI, preferred_element_type=F32)
    cs = jnp.dot(tri_ref[...], logw, precision=HI, preferred_element_type=F32)
    e_pos = jnp.exp(cs)
    e_neg = jnp.exp(-cs)
    rt_ref[0] = (r * e_pos).astype(rt_ref.dtype)
    kt_ref[0] = (kk * jnp.exp(cs - logw)).astype(kt_ref.dtype)
    kd_ref[0] = (k2 * e_neg).astype(kd_ref.dtype)
    bd_out_ref[0] = (kk * a * e_neg).astype(bd_out_ref.dtype)
    v_ref[0] = v.astype(v_ref.dtype)
    g_ref[0] = g
    bonus_ref[0] = rk * v
    ts = e_pos.shape[0]
    for c in range(ts // chunk):
        pend_ref[0, c:c + 1, :] = e_pos[(c + 1) * chunk - 1:(c + 1) * chunk, :]


def rwkv_prep(p3d, rwkv_mu, w0, w_lora_up, a0, a_lora_up, g_lora_up, k_k, k_a, r_k, *, ts=512):
    bsz, seq, _ = p3d.shape
    chunk = RWKV_CHUNK
    ts = min(ts, seq)
    mu = jnp.pad(rwkv_mu, (0, COL_B - COL_B_RAW)).reshape(1, COL_B)
    vec = jnp.stack([w0, a0, k_k, k_a, r_k.reshape(-1)] + [jnp.zeros_like(w0)] * 3).astype(F32)
    ww = jnp.zeros((LANES, WIDTH), F32).at[:DECAY_LORA].set(w_lora_up)
    wa = jnp.zeros((LANES, WIDTH), F32).at[DECAY_LORA:DECAY_LORA + AAA_LORA].set(a_lora_up)
    wg = jnp.zeros((2 * LANES, WIDTH), F32).at[:GATE_LORA].set(g_lora_up)
    hid = jnp.arange(WIDTH) // HEAD_DIM
    bd = (hid[:, None] == hid[None, :]).astype(F32)
    tix = jnp.arange(ts)
    tri = ((tix[:, None] // chunk == tix[None, :] // chunk) & (tix[None, :] <= tix[:, None])).astype(F32)
    c0 = 0
    big = jax.ShapeDtypeStruct((bsz, seq, WIDTH), F32)
    wspec = lambda shape: pl.BlockSpec(shape, lambda b, i: (0, 0))
    ospec = pl.BlockSpec((1, ts, WIDTH), lambda b, i: (b, i, 0))
    return pl.pallas_call(
        functools.partial(_rwkv_prep_kernel, chunk=chunk),
        grid=(bsz, seq // ts),
        in_specs=[
            pl.BlockSpec((1, ts, WIDTH), lambda b, i: (b, i, c0)),
            pl.BlockSpec((1, ts, WIDTH), lambda b, i: (b, i, c0 + 1)),
            pl.BlockSpec((1, ts, WIDTH), lambda b, i: (b, i, c0 + 2)),
            pl.BlockSpec((1, ts, WIDTH), lambda b, i: (b, i, c0 + 3)),
            wspec((1, COL_B)), wspec((8, WIDTH)), wspec((LANES, WIDTH)), wspec((LANES, WIDTH)),
            wspec((2 * LANES, WIDTH)), wspec((WIDTH, WIDTH)), wspec((ts, ts)),
        ],
        out_specs=[ospec] * 7 + [pl.BlockSpec((1, ts // chunk, WIDTH), lambda b, i: (b, i, 0))],
        out_shape=[jax.ShapeDtypeStruct((bsz, seq, WIDTH), BF16)] * 5 + [big] * 2
        + [jax.ShapeDtypeStruct((bsz, seq // chunk, WIDTH), F32)],
        scratch_shapes=[pltpu.VMEM((8, COL_B), F32)],
        compiler_params=_cparams(("parallel", "arbitrary")),
        name="rwkv_prep",
    )(p3d, p3d, p3d, p3d, mu, vec, ww, wa, wg, bd, tri)


def _rwkv_scan_kernel(rt_ref, kt_ref, kd_ref, bd_ref, v_ref, g_ref, bonus_ref, pend_ref, ln_ref, o_ref,
                      state_ref, *, chunk, cps, prec):
    @pl.when(pl.program_id(1) == 0)
    def _():
        state_ref[...] = jnp.zeros_like(state_ref)

    c2 = 2 * chunk
    lane = lax.broadcasted_iota(jnp.int32, (chunk, LANES), 1)
    first = lane < HEAD_DIM
    row = lax.broadcasted_iota(jnp.int32, (c2, c2), 0)
    col = lax.broadcasted_iota(jnp.int32, (c2, c2), 1)
    eye = (row == col).astype(F32)
    hrow = lax.broadcasted_iota(jnp.int32, (LANES, LANES), 0) // HEAD_DIM
    hcol = lax.broadcasted_iota(jnp.int32, (LANES, LANES), 1) // HEAD_DIM
    head_mean = jnp.where(hrow == hcol, 1.0 / HEAD_DIM, 0.0).astype(F32)
    nt = (((1,), (1,)), ((), ()))
    tn = (((0,), (0,)), ((), ()))
    dot = functools.partial(jnp.dot, precision=prec, preferred_element_type=F32)
    dotg = functools.partial(lax.dot_general, precision=prec, preferred_element_type=F32)

    def stack(x):
        return jnp.concatenate([jnp.where(first, x, 0.0), jnp.where(first, 0.0, x)], axis=0)

    pairs = range(PAIRS)
    units = [(j, hp) for j in range(cps) for hp in pairs]
    sls = [slice(hp * LANES, (hp + 1) * LANES) for hp in pairs]
    rows_of = [slice(j * chunk, (j + 1) * chunk) for j in range(cps)]
    rs, ks, kds, bs, vs = ({(j, hp): stack(ref[0, rows_of[j], sls[hp]].astype(F32)) for j, hp in units}
                           for ref in (rt_ref, kt_ref, kd_ref, bd_ref, v_ref))
    big = {u: dotg(jnp.concatenate([ks[u], rs[u]], axis=0), jnp.concatenate([bs[u], kds[u]], axis=0), nt)
           for u in units}
    a_b = {u: jnp.where(row > col, big[u][0:c2, 0:c2], 0.0) for u in units}
    a_k = {u: jnp.where(row > col, big[u][0:c2, c2:], 0.0) for u in units}
    a_rb = {u: jnp.where(row >= col, big[u][c2:, 0:c2], 0.0) for u in units}
    a_rk = {u: jnp.where(row >= col, big[u][c2:, c2:], 0.0) for u in units}
    av = {u: dot(jnp.concatenate([a_k[u], a_rk[u]], axis=0), vs[u]) for u in units}
    vk = {u: dotg(vs[u], kds[u], tn) for u in units}
    inv = {u: eye - a_b[u] for u in units}
    pw = {u: dot(a_b[u], a_b[u]) for u in units}
    n_sq = int(math.log2(chunk)) - 1
    for lvl in range(n_sq):
        if lvl + 1 < n_sq:
            both = {u: dot(jnp.concatenate([inv[u], pw[u]], axis=0), pw[u]) for u in units}
            inv = {u: inv[u] + both[u][0:c2] for u in units}
            pw = {u: both[u][c2:] for u in units}
        else:
            inv = {u: inv[u] + dot(inv[u], pw[u]) for u in units}
    hts = [state_ref[0, hp] for hp in pairs]
    for j in range(cps):
        kh = [dotg(jnp.concatenate([ks[j, hp], rs[j, hp]], axis=0), hts[hp], nt) for hp in pairs]
        us = [dot(inv[j, hp], kh[hp][0:c2] + av[j, hp][0:c2]) for hp in pairs]
        ub = [dotg(us[hp], bs[j, hp], tn) for hp in pairs]
        au = [dot(a_rb[j, hp], us[hp]) for hp in pairs]
        for hp in pairs:
            sl = sls[hp]
            pend = pend_ref[0, j, 0:1, sl]
            hts[hp] = (hts[hp] + vk[j, hp] - ub[hp]) * pend
            os_ = kh[hp][c2:] + av[j, hp][c2:] - au[hp]
            o = os_[0:chunk] + os_[chunk:]
            mu = jnp.dot(o, head_mean, precision=HI, preferred_element_type=F32)
            d = o - mu
            var = jnp.dot(d * d, head_mean, precision=HI, preferred_element_type=F32)
            on = d * lax.rsqrt(var + GN_EPS) * ln_ref[0:1, sl] + ln_ref[1:2, sl]
            o_ref[0, rows_of[j], sl] = ((on + bonus_ref[0, rows_of[j], sl]) * g_ref[0, rows_of[j], sl]
                                        ).astype(o_ref.dtype)
    for hp in pairs:
        state_ref[0, hp] = hts[hp]


def rwkv_scan(rt, kt, kd, bd, v, g, bonus, pend, lnx_g, lnx_b, *, prec=None):
    bsz, seq, _ = rt.shape
    chunk = RWKV_CHUNK
    n_chunks = seq // chunk
    ln = jnp.stack([lnx_g, lnx_b] + [jnp.zeros_like(lnx_g)] * 6).astype(F32)
    pend4 = pend.reshape(bsz, n_chunks, 1, WIDTH)
    cps = RWKV_CHUNKS_PER_STEP if n_chunks % RWKV_CHUNKS_PER_STEP == 0 else 1
    spec = pl.BlockSpec((1, cps * chunk, WIDTH), lambda b, c: (b, c, 0))
    return pl.pallas_call(
        functools.partial(_rwkv_scan_kernel, chunk=chunk, cps=cps, prec=prec),
        grid=(bsz, n_chunks // cps),
        in_specs=[spec] * 7 + [
            pl.BlockSpec((1, cps, 1, WIDTH), lambda b, c: (b, c, 0, 0)),
            pl.BlockSpec((8, WIDTH), lambda b, c: (0, 0)),
        ],
        out_specs=spec,
        out_shape=jax.ShapeDtypeStruct((bsz, seq, WIDTH), BF16),
        scratch_shapes=[pltpu.VMEM((1, PAIRS, LANES, LANES), F32)],
        compiler_params=_cparams(("parallel", "arbitrary")),
        name="rwkv_scan",
    )(rt, kt, kd, bd, v, g, bonus, pend4, ln)


def _merge_kernel(x_ref, oa_ref, ob_ref, ga_ref, gb_ref, wa_ref, wb_ref, wo_ref, g2_ref,
                  h_ref, xn_ref, acc_ref):
    j = pl.program_id(1)

    @pl.when(j == 0)
    def _():
        acc_ref[...] = x_ref[...]

    ya = jnp.dot(oa_ref[...].astype(BF16), wa_ref[...], preferred_element_type=F32)
    yb = jnp.dot(ob_ref[...].astype(BF16), wb_ref[...], preferred_element_type=F32)
    y = jax.nn.sigmoid(ga_ref[...].astype(F32)) * ya + jax.nn.sigmoid(gb_ref[...].astype(F32)) * yb
    acc_ref[...] += jnp.dot(y.astype(BF16), wo_ref[...], preferred_element_type=F32)

    @pl.when(j == pl.num_programs(1) - 1)
    def _():
        h = acc_ref[...]
        h_ref[...] = h
        ms = jnp.mean(h * h, axis=-1, keepdims=True)
        xn_ref[...] = _pack_halves(h * lax.rsqrt(ms + RMS_EPS) * g2_ref[...])


def _pack_halves(x):
    half = x.shape[1] // 2
    lo = lax.bitcast_convert_type(x[:, :half].astype(BF16).astype(F32), jnp.int32)
    hi = lax.bitcast_convert_type(x[:, half:].astype(BF16).astype(F32), jnp.int32)
    return lax.bitwise_or(lax.shift_right_logical(lo, jnp.int32(16)), hi)


def _unpack_halves(words):
    lo, hi = _unpack_words(words)
    return jnp.concatenate([lo, hi], axis=1)


def merge_out(x2d, oa, ob, p2d, w_proj_a, w_proj_b, w_out, norm2_g, *, row0=0, tm=1024):
    t, d = oa.shape[0], x2d.shape[1]
    r0 = row0 // tm
    tn = WIDTH
    nj = d // tn
    g0 = 0
    return pl.pallas_call(
        _merge_kernel,
        grid=(t // tm, nj),
        in_specs=[
            pl.BlockSpec((tm, d), lambda i, j: (r0 + i, 0)),
            pl.BlockSpec((tm, WIDTH), lambda i, j: (i, 0)),
            pl.BlockSpec((tm, WIDTH), lambda i, j: (i, 0)),
            pl.BlockSpec((tm, tn), lambda i, j: (i, g0 + j)),
            pl.BlockSpec((tm, tn), lambda i, j: (i, g0 + nj + j)),
            pl.BlockSpec((WIDTH, tn), lambda i, j: (0, j)),
            pl.BlockSpec((WIDTH, tn), lambda i, j: (0, j)),
            pl.BlockSpec((tn, d), lambda i, j: (j, 0)),
            pl.BlockSpec((1, d), lambda i, j: (0, 0)),
        ],
        out_specs=[pl.BlockSpec((tm, d), lambda i, j: (i, 0)), pl.BlockSpec((tm, d // 2), lambda i, j: (i, 0))],
        out_shape=[jax.ShapeDtypeStruct((t, d), F32), jax.ShapeDtypeStruct((t, d // 2), jnp.int32)],
        scratch_shapes=[pltpu.VMEM((tm, d), F32)],
        compiler_params=_cparams(("parallel", "arbitrary")),
        name="merge_out",
    )(x2d, oa, ob, p2d, p2d, w_proj_a.astype(BF16), w_proj_b.astype(BF16), w_out.astype(BF16),
      norm2_g.reshape(1, d))


PEER_HEADS = 8
PEER_NKEYS = 128
PEER_TOPK = 16
PEER_HALF = 128


def _topk_rows(s, k):
    n = s.shape[0]
    rows = lax.broadcasted_iota(jnp.int32, s.shape, 0).astype(F32)
    vals, ids = [], []
    for _ in range(k):
        m = jnp.max(s, axis=0, keepdims=True)
        first = jnp.min(jnp.where(s == m, rows, float(n)), axis=0, keepdims=True)
        vals.append(m)
        ids.append(first)
        s = jnp.where(rows == first, -jnp.inf, s)
    return jnp.concatenate(vals, axis=0), jnp.concatenate(ids, axis=0)


def _take_rows(table, ids):
    rows = lax.broadcasted_iota(jnp.int32, table.shape, 0).astype(F32)
    return jnp.sum(jnp.where(rows == ids, table, 0.0), axis=0, keepdims=True)


def _peer_route_kernel(xn_ref, wq_ref, sk_ref, idx_ref, gate_ref, *, prec):
    tt = xn_ref.shape[0]
    k = PEER_TOPK
    xn = _unpack_halves(xn_ref[...]) if xn_ref.dtype == jnp.int32 else xn_ref[...]
    q = jnp.dot(xn.astype(wq_ref.dtype), wq_ref[...], precision=prec, preferred_element_type=F32)
    nt = (((1,), (1,)), ((), ()))
    idx_rows, gate_rows = [], []
    half = k // 2
    for h in range(PEER_HEADS):
        tops = []
        for p in range(2):
            c0 = (h * 2 + p) * PEER_HALF
            s = lax.dot_general(sk_ref[h, p].astype(wq_ref.dtype), q[:, c0:c0 + PEER_HALF].astype(wq_ref.dtype),
                                nt, precision=prec, preferred_element_type=F32)
            tops.append(_topk_rows(s, k))
        (s0, i0), (s1, i1) = tops
        cs = [s0[0:1] + s1] + [s0[i:i + 1] + s1[0:half] for i in range(1, half)] + [s0[half:] + s1[0:1]]
        best_s, pos = _topk_rows(jnp.concatenate(cs, axis=0), k)
        mid = jnp.floor((pos - k) * (1.0 / half))
        end_mid = float(k + (half - 1) * half)
        i_rank = jnp.where(pos < k, 0.0, jnp.where(pos < end_mid, 1.0 + mid, pos - (end_mid - half)))
        j_rank = jnp.where(pos < k, pos, jnp.where(pos < end_mid, (pos - k) - half * mid, 0.0))
        ids = [_take_rows(i0, i_rank[n:n + 1]) * PEER_NKEYS + _take_rows(i1, j_rank[n:n + 1]) for n in range(k)]
        e = jnp.exp(best_s - best_s[0:1])
        gate_rows.append(e / jnp.sum(e, axis=0, keepdims=True))
        idx_rows.append(jnp.concatenate(ids, axis=0).astype(jnp.int32))
    idx_ref[...] = jnp.concatenate(idx_rows, axis=0).T
    gate_ref[...] = jnp.concatenate(gate_rows, axis=0).T


def peer_route(xn2d, peer_wq, peer_subkeys, *, tt=256, prec=None, wdtype=BF16):
    t, dx = xn2d.shape
    d, nq = peer_wq.shape
    n_sel = PEER_HEADS * PEER_TOPK
    return pl.pallas_call(
        functools.partial(_peer_route_kernel, prec=prec),
        grid=(t // tt,),
        in_specs=[
            pl.BlockSpec((tt, dx), lambda i: (i, 0)),
            pl.BlockSpec((d, nq), lambda i: (0, 0)),
            pl.BlockSpec((PEER_HEADS, 2, PEER_NKEYS, PEER_HALF), lambda i: (0, 0, 0, 0)),
        ],
        out_specs=[pl.BlockSpec((tt, n_sel), lambda i: (i, 0))] * 2,
        out_shape=[jax.ShapeDtypeStruct((t, n_sel), jnp.int32), jax.ShapeDtypeStruct((t, n_sel), F32)],
        compiler_params=_cparams(("parallel",)),
        name="peer_route",
    )(xn2d, peer_wq.astype(wdtype), peer_subkeys)


def _final_kernel(h_ref, y_ref, g_ref, *rest):
    o_ref = rest[-1]
    h = h_ref[...] + y_ref[...]
    ms = jnp.mean(h * h, axis=-1, keepdims=True)
    o_ref[...] = h * lax.rsqrt(ms + RMS_EPS) * g_ref[...]


def final_norm(h2d, y2d, g, *, out=None, row0=0, total_rows=None, tm=1024):
    t, d = h2d.shape
    total = t if total_rows is None else total_rows
    r0 = row0 // tm
    spec = pl.BlockSpec((tm, d), lambda i: (i, 0))
    in_specs = [spec, spec, pl.BlockSpec((1, d), lambda i: (0, 0))]
    args = [h2d, y2d, g.reshape(1, d)]
    aliases = {}
    if out is not None:
        in_specs.append(pl.BlockSpec(memory_space=pl.ANY))
        args.append(out)
        aliases = {3: 0}
    return pl.pallas_call(
        _final_kernel,
        grid=(t // tm,),
        in_specs=in_specs,
        out_specs=pl.BlockSpec((tm, d), lambda i: (r0 + i, 0)),
        out_shape=jax.ShapeDtypeStruct((total, d), F32),
        input_output_aliases=aliases,
        compiler_params=_cparams(("parallel",)),
        name="final_norm",
    )(*args)


SC_CORES = 2
SC_SUBCORES = 16
SC_LANES = 16
SC_WORKERS = SC_CORES * SC_SUBCORES
PEER_SEL = PEER_HEADS * PEER_TOPK
PEER_ROWS = 32
PEER_PARTS = PEER_SEL // PEER_ROWS
PEER_NBUF = 4
PEER_GROUP = 64
PEER_BF16_RUN = 4
PEER_ROW_PAIR = 4


def _pack_rows_kernel(w_ref, o_ref):
    o_ref[...] = _pack_halves(w_ref[...])


def _pack_rows(w, *, tr=1024):
    e, d = w.shape
    return pl.pallas_call(
        _pack_rows_kernel,
        grid=(e // tr,),
        in_specs=[pl.BlockSpec((tr, d), lambda i: (i, 0))],
        out_specs=pl.BlockSpec((tr, d // 2), lambda i: (i, 0)),
        out_shape=jax.ShapeDtypeStruct((e, d // 2), jnp.int32),
        compiler_params=_cparams(("parallel",)),
        name="pack_rows",
    )(w)


def _unpack_words(w):
    lo = lax.bitcast_convert_type(lax.shift_left(w, jnp.int32(16)), F32)
    hi = lax.bitcast_convert_type(lax.bitwise_and(w, jnp.int32(-65536)), F32)
    return lo, hi


def _packed_dot(a_words, b_words):
    from jax.experimental.pallas import tpu_sc as plsc
    prods = [plsc.bitcast(a, BF16) * plsc.bitcast(b, BF16) for a, b in zip(a_words, b_words)]
    while len(prods) > 1:
        prods = [prods[k] + prods[k + 1] for k in range(0, len(prods), 2)]
    return _unpack_words(plsc.bitcast(prods[0], jnp.int32))


def _sc_mesh():
    from jax.experimental.pallas import tpu_sc as plsc
    return plsc.VectorSubcoreMesh(core_axis_name="c", subcore_axis_name="s",
                                  num_cores=SC_CORES, num_subcores=SC_SUBCORES)


def _sc_loop(n, body, carry):
    from jax.experimental.pallas import tpu_sc as plsc
    return plsc.parallel_loop(0, n, carry=carry)(body)


def _worker_base(tokens_per_worker):
    return (lax.axis_index("s") * SC_CORES + lax.axis_index("c")) * tokens_per_worker


def _gather_compute_loop(table_hbm, idx_v, rows_v, sem, stage_v, out_row, osem, grp, compute):
    n_gathers = PEER_PARTS * grp
    ahead = PEER_NBUF - 1

    def gather(j, b):
        i = j // PEER_PARTS if isinstance(j, int) else lax.shift_right_logical(j, PEER_PARTS.bit_length() - 1)
        h = j % PEER_PARTS if isinstance(j, int) else lax.bitwise_and(j, PEER_PARTS - 1)
        ids = idx_v.at[i, pl.ds(pl.multiple_of(h * PEER_ROWS, PEER_ROWS), PEER_ROWS)]
        return pltpu.make_async_copy(table_hbm.at[ids], rows_v.at[b], sem.at[b])

    def put(i, slot):
        return pltpu.make_async_copy(stage_v.at[slot], out_row(i), osem.at[slot])

    for j in range(ahead):
        gather(j, j).start(priority=j % 2)

    @pl.loop(0, n_gathers)
    def _(j):
        b = lax.bitwise_and(j, PEER_NBUF - 1)
        h = lax.bitwise_and(j, PEER_PARTS - 1)
        i = lax.shift_right_logical(j, PEER_PARTS.bit_length() - 1)
        slot = lax.bitwise_and(i, 1)

        @pl.when((h == 0) & (i >= 2))
        def _():
            put(i - 2, slot).wait()

        nxt = j + ahead
        for prio in range(2):
            @pl.when((nxt < n_gathers) & (lax.bitwise_and(nxt, 1) == prio))
            def _():
                gather(nxt, lax.bitwise_and(nxt, PEER_NBUF - 1)).start(priority=prio)

        gather(j, b).wait()
        compute(i, h, b, slot)

        @pl.when(h == PEER_PARTS - 1)
        def _():
            put(i, slot).start()

    put(grp - 2, 0).wait()
    put(grp - 1, 1).wait()


def peer_expert_dots(x_packed, idx, u_packed):
    t, half = x_packed.shape
    n_chunks = half // SC_LANES
    tpw = t // SC_WORKERS
    igrp = min(2 * PEER_GROUP, tpw)
    grp = min(PEER_GROUP, igrp)
    assert t % SC_WORKERS == 0 and tpw % igrp == 0 and igrp % grp == 0 and grp & (grp - 1) == 0
    rows_tog = 2 * PEER_ROW_PAIR
    assert igrp % 2 == 0 and idx.shape == (t, PEER_SEL) and PEER_ROWS % rows_tog == 0

    def body(x_hbm, idx_hbm, u_hbm, out_hbm, idx_v, x_v, rows_v, ps_v, sem, osem):
        base = _worker_base(tpw)

        def compute(t0, i, h, b, slot):
            @pl.when((h == 0) & (lax.bitwise_and(i, grp - 1) == 0))
            def _():
                pltpu.sync_copy(x_hbm.at[pl.ds(pl.multiple_of(t0 + i, grp), grp)], x_v)

            ix = lax.bitwise_and(i, grp - 1)

            @pl.loop(0, PEER_ROWS // rows_tog)
            def _(rg):
                r0 = rg * rows_tog
                accs = [[None, None] for _ in range(rows_tog)]
                for c0 in range(0, n_chunks, PEER_BF16_RUN):
                    ats = [pl.ds((c0 + k) * SC_LANES, SC_LANES) for k in range(PEER_BF16_RUN)]
                    xw = [x_v[ix, at] for at in ats]
                    for r in range(rows_tog):
                        terms = _packed_dot([rows_v[b, r0 + r, at] for at in ats], xw)
                        for k, term in enumerate(terms):
                            accs[r][k] = term if accs[r][k] is None else accs[r][k] + term
                pair = rows_tog // 2
                bits = [lax.bitcast_convert_type(accs[r][0] + accs[r][1], jnp.int32) + jnp.int32(0x8000)
                        for r in range(rows_tog)]
                for r in range(pair):
                    word = lax.bitwise_or(lax.shift_right_logical(bits[r], jnp.int32(16)),
                                          lax.bitwise_and(bits[r + pair], jnp.int32(-65536)))
                    w0 = lax.shift_right_logical(h * PEER_ROWS + r0, 1) + r
                    ps_v[slot, pl.ds(pl.multiple_of(w0 * SC_LANES, SC_LANES), SC_LANES)] = word

        @pl.loop(0, tpw // igrp)
        def _(g):
            t0 = base + g * igrp
            pltpu.sync_copy(idx_hbm.at[pl.ds(t0, igrp)], idx_v)
            _gather_compute_loop(u_hbm, idx_v, rows_v, sem, ps_v, lambda i: out_hbm.at[t0 + i], osem, igrp,
                                 functools.partial(compute, t0))

    return pl.kernel(
        body,
        out_type=jax.ShapeDtypeStruct((t, PEER_SEL * SC_LANES // 2), jnp.int32),
        mesh=_sc_mesh(),
        scratch_types=[
            pltpu.VMEM((igrp, PEER_SEL), jnp.int32),
            pltpu.VMEM((grp, half), jnp.int32),
            pltpu.VMEM((PEER_NBUF, PEER_ROWS, half), jnp.int32),
            pltpu.VMEM((2, PEER_SEL * SC_LANES // 2), jnp.int32),
            pltpu.SemaphoreType.DMA((PEER_NBUF,)),
            pltpu.SemaphoreType.DMA((2,)),
        ],
        compiler_params=pltpu.CompilerParams(needs_layout_passes=False),
        name="peer_expert_dots",
    )(x_packed, idx, u_packed)


def peer_expert_mix(hgw, idx, v_packed):
    t = hgw.shape[0]
    half = v_packed.shape[1]
    d = 2 * half
    tpw = t // SC_WORKERS
    grp = min(2 * PEER_GROUP, tpw)
    assert t % SC_WORKERS == 0 and tpw % grp == 0 and grp % 2 == 0 and idx.shape == (t, PEER_SEL)
    n_parts = 2
    cpp = half // SC_LANES // n_parts
    from jax.experimental.pallas import tpu_sc as plsc

    def body(hg_hbm, idx_hbm, v_hbm, out_hbm, idx_v, hg_v, rows_v, o_v2, sem, osem):
        base = _worker_base(tpw)

        def compute(i, h, b, slot):
            token = jnp.full((SC_LANES,), i, jnp.int32)
            for part in range(n_parts):
                def rbody(rq, accs):
                    r0 = rq * PEER_BF16_RUN
                    s = [plsc.load_gather(hg_v, [token, jnp.full((SC_LANES,), h * PEER_ROWS + r0 + k, jnp.int32)])
                         for k in range(PEER_BF16_RUN)]
                    new = []
                    for c in range(cpp):
                        at = pl.ds((part * cpp + c) * SC_LANES, SC_LANES)
                        lo, hi = _packed_dot([rows_v[b, r0 + k, at] for k in range(PEER_BF16_RUN)], s)
                        new.append(accs[2 * c] + lo)
                        new.append(accs[2 * c + 1] + hi)
                    return tuple(new)

                accs = _sc_loop(PEER_ROWS // PEER_BF16_RUN, rbody,
                                tuple(jnp.zeros((SC_LANES,), F32) for _ in range(2 * cpp)))
                def store(overwrite):
                    for c in range(cpp):
                        lo_at = pl.ds((part * cpp + c) * SC_LANES, SC_LANES)
                        hi_at = pl.ds(half + (part * cpp + c) * SC_LANES, SC_LANES)
                        if overwrite:
                            o_v2[slot, lo_at] = accs[2 * c]
                            o_v2[slot, hi_at] = accs[2 * c + 1]
                        else:
                            o_v2[slot, lo_at] = o_v2[slot, lo_at] + accs[2 * c]
                            o_v2[slot, hi_at] = o_v2[slot, hi_at] + accs[2 * c + 1]

                pl.when(h == 0)(functools.partial(store, True))
                pl.when(h != 0)(functools.partial(store, False))

        @pl.loop(0, tpw // grp)
        def _(g):
            t0 = base + g * grp
            pltpu.sync_copy(idx_hbm.at[pl.ds(t0, grp)], idx_v)
            pltpu.sync_copy(hg_hbm.at[pl.ds(t0, grp)], hg_v)
            _gather_compute_loop(v_hbm, idx_v, rows_v, sem, o_v2, lambda i: out_hbm.at[t0 + i], osem, grp, compute)

    return pl.kernel(
        body,
        out_type=jax.ShapeDtypeStruct((t, d), F32),
        mesh=_sc_mesh(),
        scratch_types=[
            pltpu.VMEM((grp, PEER_SEL), jnp.int32),
            pltpu.VMEM((grp, PEER_SEL), jnp.int32),
            pltpu.VMEM((PEER_NBUF, PEER_ROWS, half), jnp.int32),
            pltpu.VMEM((2, d), F32),
            pltpu.SemaphoreType.DMA((PEER_NBUF,)),
            pltpu.SemaphoreType.DMA((2,)),
        ],
        compiler_params=pltpu.CompilerParams(needs_layout_passes=False),
        name="peer_expert_mix",
    )(hgw, idx, v_packed)


def _peer_act_kernel(ps_ref, gate_ref, lo_ref, hi_ref, o_ref):
    lo, hi = _unpack_words(ps_ref[...])
    pre = (jnp.dot(lo.astype(BF16), lo_ref[...], preferred_element_type=F32)
           + jnp.dot(hi.astype(BF16), hi_ref[...], preferred_element_type=F32))
    hg = 0.5 * pre * (1.0 + lax.erf(pre * (1.0 / math.sqrt(2.0)))) * gate_ref[...]
    bits = lax.bitcast_convert_type(hg.astype(BF16).astype(F32), jnp.int32)
    o_ref[...] = lax.bitwise_or(bits, lax.shift_right_logical(bits, jnp.int32(16)))


def peer_act(ps, gates, *, tm=1024):
    t, n = ps.shape
    reg = jnp.arange(n) // SC_LANES
    slot_lo = 2 * PEER_ROW_PAIR * (reg // PEER_ROW_PAIR) + reg % PEER_ROW_PAIR
    place_lo = (slot_lo[:, None] == jnp.arange(PEER_SEL)[None, :]).astype(BF16)
    place_hi = ((slot_lo + PEER_ROW_PAIR)[:, None] == jnp.arange(PEER_SEL)[None, :]).astype(BF16)
    return pl.pallas_call(
        _peer_act_kernel,
        grid=(t // tm,),
        in_specs=[
            pl.BlockSpec((tm, n), lambda i: (i, 0)),
            pl.BlockSpec((tm, PEER_SEL), lambda i: (i, 0)),
            pl.BlockSpec((n, PEER_SEL), lambda i: (0, 0)),
            pl.BlockSpec((n, PEER_SEL), lambda i: (0, 0)),
        ],
        out_specs=pl.BlockSpec((tm, PEER_SEL), lambda i: (i, 0)),
        out_shape=jax.ShapeDtypeStruct((t, PEER_SEL), jnp.int32),
        compiler_params=_cparams(("parallel",)),
        name="peer_act",
    )(ps, gates, place_lo, place_hi)


BATCH_GROUPS = 8


def kernel(x, norm1_g, w_in, rwkv_mu, w0, w_lora_up, a0, a_lora_up, g_lora_up, k_k, k_a, r_k, lnx_g, lnx_b,
           w_proj_a, w_proj_b, w_out, norm2_g, peer_wq, peer_subkeys, peer_u, peer_v, rel_bias, normf_g):
    bsz, seq, d = x.shape
    depth = norm1_g.shape[0]
    groups = BATCH_GROUPS if bsz % BATCH_GROUPS == 0 else 1
    gb = bsz // groups
    tg = gb * seq
    t = bsz * seq
    src = x.reshape(t, d)
    for l in range(depth):
        w_pad = jnp.concatenate([
            w_in[l][:, :COL_A + COL_B_RAW],
            jnp.zeros((d, COL_B - COL_B_RAW), w_in.dtype),
            w_in[l][:, COL_A + COL_B_RAW:]], axis=1).astype(BF16)
        u_packed = _pack_rows(peer_u[l])
        v_packed = _pack_rows(peer_v[l])
        last = l == depth - 1

        def mix(pending, tie=None):
            row0, h2d, ps, gates, idx = pending
            hgw = peer_act(ps, gates)
            if tie is not None:
                tie, hgw = lax.optimization_barrier((tie, hgw))
            return tie, (row0, h2d, peer_expert_mix(hgw, idx, v_packed))

        outs = []

        def close(mixed):
            row0, h2d, y2d = mixed
            if last:
                outs.append(final_norm(h2d, y2d, normf_g, out=outs[-1] if outs else None, row0=row0, total_rows=t))
            else:
                outs.append(h2d + y2d)

        pending = closing = None
        for g in range(groups):
            pa, pb, pg = norm_proj(src, norm1_g[l], w_pad, row0=g * tg, rows=tg)
            oa = moba_attention(pa.reshape(gb, seq, -1), rel_bias)
            prep = tuple(rwkv_prep(pb.reshape(gb, seq, -1), rwkv_mu[l], w0[l], w_lora_up[l], a0[l], a_lora_up[l], g_lora_up[l],
                                   k_k[l], k_a[l], r_k[l]))
            mixed = None
            if pending is not None:
                (oa, prep), mixed = mix(pending, (oa, prep))
            if closing is not None:
                oa, y2d = lax.optimization_barrier((oa, closing[2]))
                close(closing[:2] + (y2d,))
                closing = None
            ob = rwkv_scan(*prep, lnx_g[l], lnx_b[l])
            h2d, xn2 = merge_out(src, oa.reshape(tg, WIDTH), ob.reshape(tg, WIDTH), pg, w_proj_a[l], w_proj_b[l],
                                 w_out[l], norm2_g[l], row0=g * tg)
            idx, gates = peer_route(xn2, peer_wq[l], peer_subkeys[l])
            if mixed is not None:
                idx, y2d = lax.optimization_barrier((idx, mixed[2]))
                closing = mixed[:2] + (y2d,)
            pending = (g * tg, h2d, peer_expert_dots(xn2, idx, u_packed), gates, idx)
        if closing is not None:
            close(closing)
        close(mix(pending)[1])
        src = outs[-1] if last else jnp.concatenate(outs, axis=0)
    return src.reshape(bsz, seq, d)
```

```python
import functools
import math

import jax
import jax.numpy as jnp
from jax import lax
from jax.experimental import pallas as pl
from jax.experimental.pallas import tpu as pltpu

F32 = jnp.float32
BF16 = jnp.bfloat16
HI = lax.Precision.HIGHEST

LANES = 128
HEAD_DIM = 64
HEADS = 8
PAIRS = HEADS // 2
WIDTH = HEADS * HEAD_DIM
MOBA_BLOCK = 256
MOBA_TOPK = 3
MOBA_LO = 64
REL_BUCKETS = 32
REL_MAX_DIST = 128
DECAY_LORA = 64
AAA_LORA = 64
GATE_LORA = 160
GN_EPS = 64e-5
RMS_EPS = 1e-6
NEG = -1e30
RWKV_CHUNK = 64
RWKV_CHUNKS_PER_STEP = 4
COL_A = 3 * WIDTH
COL_B_RAW = 3 * WIDTH + DECAY_LORA + AAA_LORA + GATE_LORA
COL_B = 4 * WIDTH
COL_G_OFF = COL_A + COL_B
VMEM_LIMIT = 56 * 1024 * 1024


def _cparams(sem):
    return pltpu.CompilerParams(dimension_semantics=sem, vmem_limit_bytes=VMEM_LIMIT)


def _norm_proj_kernel(x_ref, g_ref, w_ref, pa_ref, pb_ref, pg_ref, xn_ref, *, ja, jb):
    j = pl.program_id(1)

    @pl.when(j == 0)
    def _():
        x = x_ref[...]
        ms = jnp.mean(x * x, axis=-1, keepdims=True)
        xn_ref[...] = (x * lax.rsqrt(ms + RMS_EPS) * g_ref[...]).astype(xn_ref.dtype)

    res = jnp.dot(xn_ref[...], w_ref[...], preferred_element_type=F32)

    @pl.when(j < ja)
    def _():
        pa_ref[...] = res.astype(pa_ref.dtype)

    @pl.when((j >= ja) & (j < jb))
    def _():
        pb_ref[...] = res

    @pl.when(j >= jb)
    def _():
        pg_ref[...] = res.astype(pg_ref.dtype)


def norm_proj(x2d, g, w, *, row0=0, rows=None, tm=2048, tn=512):
    d = x2d.shape[1]
    t = x2d.shape[0] if rows is None else rows
    n = w.shape[1]
    r0 = row0 // tm
    ja, jb, jn = COL_A // tn, COL_G_OFF // tn, n // tn
    return pl.pallas_call(
        functools.partial(_norm_proj_kernel, ja=ja, jb=jb),
        grid=(t // tm, jn),
        in_specs=[
            pl.BlockSpec((tm, d), lambda i, j: (r0 + i, 0)),
            pl.BlockSpec((1, d), lambda i, j: (0, 0)),
            pl.BlockSpec((d, tn), lambda i, j: (0, j)),
        ],
        out_specs=[
            pl.BlockSpec((tm, tn), lambda i, j: (i, jnp.minimum(j, ja - 1))),
            pl.BlockSpec((tm, tn), lambda i, j: (i, jnp.clip(j - ja, 0, jb - ja - 1))),
            pl.BlockSpec((tm, tn), lambda i, j: (i, jnp.maximum(j - jb, 0))),
        ],
        out_shape=[jax.ShapeDtypeStruct((t, COL_A), BF16), jax.ShapeDtypeStruct((t, COL_B), F32),
                   jax.ShapeDtypeStruct((t, n - COL_G_OFF), BF16)],
        scratch_shapes=[pltpu.VMEM((tm, d), w.dtype)],
        compiler_params=_cparams(("parallel", "arbitrary")),
        name="norm_proj",
    )(x2d, g.reshape(1, d), w)


def _rel_bucket(dist):
    n = jnp.maximum(dist, 0)
    max_exact = REL_BUCKETS // 2
    nf = jnp.maximum(n, 1).astype(F32)
    large = max_exact + (jnp.log(nf / max_exact) / math.log(REL_MAX_DIST / max_exact)
                         * (REL_BUCKETS - max_exact)).astype(jnp.int32)
    large = jnp.minimum(large, REL_BUCKETS - 1)
    return jnp.where(n < max_exact, n, large)


def _moba_kernel(q_ref, k_ref, v_ref, bown_ref, bprev_ref, bfar_ref, o_ref,
                 kb_ref, vb_ref, kbar_ref, *, n_blocks):
    qb = pl.program_id(2)
    blk = MOBA_BLOCK
    scale = 1.0 / math.sqrt(HEAD_DIM)

    rows2 = 2 * blk
    nt = (((1,), (1,)), ((), ()))

    @pl.when(qb == 0)
    def _():
        kbar_ref[...] = jnp.zeros_like(kbar_ref)
        lane_b = lax.broadcasted_iota(jnp.int32, (blk, LANES), 1)
        for n in range(n_blocks):
            kblk = k_ref[0, n * blk:(n + 1) * blk, :]
            kbar_ref[n:n + 1, :] = jnp.mean(kblk.astype(F32), axis=0, keepdims=True)
            kb_ref[n * blk:(n + 1) * blk, 0:LANES] = kblk.astype(BF16)
            kb_ref[n * blk:(n + 1) * blk, LANES:] = ((lane_b == n) | (lane_b == MOBA_LO + n)).astype(BF16)
        vb_ref[...] = v_ref[0].astype(BF16)

    q2 = q_ref[0].astype(F32)
    first = lax.broadcasted_iota(jnp.int32, (blk, LANES), 1) < HEAD_DIM
    qh = jnp.concatenate([jnp.where(first, q2, 0.0), jnp.where(first, 0.0, q2)], axis=0)
    lane = lax.broadcasted_iota(jnp.int32, (rows2, LANES), 1)
    rowi = lax.broadcasted_iota(jnp.int32, (rows2, LANES), 0)
    nbp = -(-n_blocks // 16) * 16
    gate_t = lax.dot_general(kbar_ref[0:nbp, :].astype(BF16), qh.astype(BF16), nt, preferred_element_type=F32)
    blk_f = lax.broadcasted_iota(jnp.int32, (nbp, rows2), 0).astype(F32)
    g = jnp.where(blk_f < qb.astype(F32), gate_t, -jnp.inf)
    chosen_t = jnp.zeros((nbp, rows2), F32)
    for _ in range(MOBA_TOPK):
        m = jnp.max(g, axis=0, keepdims=True)
        idx = jnp.min(jnp.where(g == m, blk_f, float(nbp)), axis=0, keepdims=True)
        hit = (blk_f == idx) & (m > -jnp.inf)
        chosen_t = jnp.where(hit, 1.0, chosen_t)
        g = jnp.where(hit, -jnp.inf, g)
    ident = (lax.broadcasted_iota(jnp.int32, (nbp, LANES), 0)
             == lax.broadcasted_iota(jnp.int32, (nbp, LANES), 1)).astype(BF16)
    chosen = lax.dot_general(chosen_t.astype(BF16), ident, (((0,), (0,)), ((), ())),
                             preferred_element_type=F32) > 0.5
    nfar = qb - 1
    bfar = jnp.where(rowi < blk, bfar_ref[0, 0:1, 0:1], bfar_ref[1, 0:1, 0:1])
    bhi = bfar.astype(BF16).astype(F32)
    madd = jnp.where(lane < nfar, jnp.where(chosen, bhi, NEG),
                     jnp.where(lane == nfar, jnp.where(chosen, 0.0, NEG),
                               jnp.where((lane >= MOBA_LO) & (lane - MOBA_LO < nfar), bfar - bhi, 0.0)))
    q_aug = jnp.concatenate([(qh * scale).astype(BF16), madd.astype(BF16)], axis=1)

    prev0 = pl.multiple_of(jnp.maximum(nfar, 0) * blk, blk)
    own0 = pl.multiple_of(qb * blk, blk)
    s_prev = (lax.dot_general(q_aug, kb_ref[pl.ds(prev0, blk), :], nt, preferred_element_type=F32)
              + bprev_ref[...].reshape(rows2, blk) + jnp.where(qb > 0, 0.0, NEG))
    s_own = (lax.dot_general(q_aug, kb_ref[pl.ds(own0, blk), :], nt, preferred_element_type=F32)
             + bown_ref[...].reshape(rows2, blk))
    r = lax.broadcasted_iota(jnp.int32, (rows2, blk), 0)
    c = lax.broadcasted_iota(jnp.int32, (rows2, blk), 1)
    s_own = jnp.where(lax.bitwise_and(r, blk - 1) >= c, s_own, NEG)
    s = jnp.concatenate([s_prev, s_own], axis=1)
    m_i = jnp.max(s, axis=1, keepdims=True)
    p = jnp.exp(s - m_i)
    l_i = jnp.sum(p, axis=1, keepdims=True)
    v0 = jnp.concatenate([vb_ref[pl.ds(prev0, blk), :], vb_ref[pl.ds(own0, blk), :]], axis=0)
    acc = jnp.dot(p.astype(BF16), v0, preferred_element_type=F32)

    def body(it, carry):
        m_i, l_i, acc = carry
        k0 = pl.multiple_of(it * rows2, rows2)
        s = lax.dot_general(q_aug, kb_ref[pl.ds(k0, rows2), :], nt, preferred_element_type=F32)
        tail = jnp.where(2 * it + 1 < nfar, 0.0, NEG)
        s = jnp.concatenate([s[:, :blk], s[:, blk:] + tail], axis=1)
        m_new = jnp.maximum(m_i, jnp.max(s, axis=1, keepdims=True))
        alpha = jnp.exp(m_i - m_new)
        p = jnp.exp(s - m_new)
        l_new = alpha * l_i + jnp.sum(p, axis=1, keepdims=True)
        acc_new = alpha * acc + jnp.dot(p.astype(BF16), vb_ref[pl.ds(k0, rows2), :], preferred_element_type=F32)
        return m_new, l_new, acc_new

    m_i, l_i, acc = lax.fori_loop(0, (jnp.maximum(nfar, 0) + 1) // 2, body, (m_i, l_i, acc))
    out = acc / l_i
    o_ref[0] = jnp.where(first, out[:blk], out[blk:]).astype(o_ref.dtype)


def moba_attention(p3d, rel_bias):
    bsz, seq, _ = p3d.shape
    blk = MOBA_BLOCK
    n_blocks = seq // blk
    assert n_blocks <= MOBA_LO and seq % blk == 0
    span = 2 * blk
    by_dist = rel_bias[:, _rel_bucket(jnp.arange(span))].astype(F32)
    shift = jnp.arange(span)

    def toeplitz(c):
        k = jnp.where(shift < blk, shift, shift - span)
        s = by_dist[:, jnp.clip(c - k, 0, span - 1)]
        tiled = jnp.tile(s, (1, blk))[:, :blk * (span - 1)]
        return tiled.reshape(HEADS, blk, span - 1)[:, :, :blk]

    bias_own = toeplitz(0)
    bias_prev = toeplitz(blk)
    bias_far = jnp.broadcast_to(rel_bias[:, REL_BUCKETS - 1].astype(F32)[:, None, None], (HEADS, 8, LANES))
    kern = functools.partial(_moba_kernel, n_blocks=n_blocks)
    return pl.pallas_call(
        kern,
        grid=(bsz, PAIRS, n_blocks),
        in_specs=[
            pl.BlockSpec((1, blk, LANES), lambda b, h, i: (b, i, h)),
            pl.BlockSpec((1, seq, LANES), lambda b, h, i: (b, 0, PAIRS + h)),
            pl.BlockSpec((1, seq, LANES), lambda b, h, i: (b, 0, 2 * PAIRS + h)),
            pl.BlockSpec((2, blk, blk), lambda b, h, i: (h, 0, 0)),
            pl.BlockSpec((2, blk, blk), lambda b, h, i: (h, 0, 0)),
            pl.BlockSpec((2, 8, LANES), lambda b, h, i: (h, 0, 0)),
        ],
        out_specs=pl.BlockSpec((1, blk, LANES), lambda b, h, i: (b, i, h)),
        out_shape=jax.ShapeDtypeStruct((bsz, seq, WIDTH), BF16),
        scratch_shapes=[
            pltpu.VMEM((seq, 2 * LANES), BF16),
            pltpu.VMEM((seq, LANES), BF16),
            pltpu.VMEM((LANES, LANES), F32),
        ],
        compiler_params=_cparams(("parallel", "parallel", "arbitrary")),
        name="moba",
    )(p3d, p3d, p3d, bias_own, bias_prev, bias_far)


def _shifted(x, carry_row):
    rows = lax.broadcasted_iota(jnp.int32, x.shape, 0)
    return jnp.where(rows == 0, carry_row, pltpu.roll(x, 1, axis=0))


def _rwkv_prep_kernel(pr_ref, pk_ref, pv_ref, pl_ref, mu_ref, vec_ref, ww_ref, wa_ref, wg_ref,
                      bd_ref, tri_ref,
                      rt_ref, kt_ref, kd_ref, bd_out_ref, v_ref, g_ref, bonus_ref, pend_ref,
                      carry_ref, *, chunk):
    @pl.when(pl.program_id(1) == 0)
    def _():
        carry_ref[...] = jnp.zeros_like(carry_ref)

    def mix(ref, j):
        x = ref[0]
        mu = mu_ref[0:1, j * WIDTH:(j + 1) * WIDTH]
        prev = _shifted(x, carry_ref[0:1, j * WIDTH:(j + 1) * WIDTH])
        carry_ref[0:1, j * WIDTH:(j + 1) * WIDTH] = x[x.shape[0] - 1:, :]
        return x + mu * (prev - x)

    r = mix(pr_ref, 0)
    k = mix(pk_ref, 1)
    v = mix(pv_ref, 2)
    lo = mix(pl_ref, 3)
    w0, a0, k_k, k_a, r_k = (vec_ref[i:i + 1, :] for i in range(5))
    xwa = lo[:, 0:LANES]
    xg = lo[:, LANES:3 * LANES]
    lw = jnp.dot(jnp.tanh(xwa), ww_ref[...], precision=HI, preferred_element_type=F32)
    la = jnp.dot(xwa, wa_ref[...], precision=HI, preferred_element_type=F32)
    g = jnp.dot(jax.nn.sigmoid(xg), wg_ref[...], precision=HI, preferred_element_type=F32)
    z = -(w0 + lw)
    softplus = jnp.maximum(z, 0.0) + jnp.log(1.0 + jnp.exp(-jnp.abs(z)))
    logw = -jnp.exp(-softplus - 0.5)
    a = jax.nn.sigmoid(a0 + la)
    kk = k * k_k
    ss = jnp.dot(kk * kk, bd_ref[...], precision=HI, preferred_element_type=F32)
    kk = kk / jnp.maximum(jnp.sqrt(ss), 1e-12)
    k2 = k * (1.0 + (a - 1.0) * k_a)
    rk = jnp.dot(r * k2 * r_k, bd_ref[...], precision=HI, preferred_element_type=F32)
    cs = jnp.dot(tri_ref[...], logw, precision=HI, preferred_element_type=F32)
    e_pos = jnp.exp(cs)
    e_neg = jnp.exp(-cs)
    rt_ref[0] = (r * e_pos).astype(rt_ref.dtype)
    kt_ref[0] = (kk * jnp.exp(cs - logw)).astype(kt_ref.dtype)
    kd_ref[0] = (k2 * e_neg).astype(kd_ref.dtype)
    bd_out_ref[0] = (kk * a * e_neg).astype(bd_out_ref.dtype)
    v_ref[0] = v.astype(v_ref.dtype)
    g_ref[0] = g
    bonus_ref[0] = rk * v
    ts = e_pos.shape[0]
    for c in range(ts // chunk):
        pend_ref[0, c:c + 1, :] = e_pos[(c + 1) * chunk - 1:(c + 1) * chunk, :]


def rwkv_prep(p3d, rwkv_mu, w0, w_lora_up, a0, a_lora_up, g_lora_up, k_k, k_a, r_k, *, ts=512):
    bsz, seq, _ = p3d.shape
    chunk = RWKV_CHUNK
    ts = min(ts, seq)
    mu = jnp.pad(rwkv_mu, (0, COL_B - COL_B_RAW)).reshape(1, COL_B)
    vec = jnp.stack([w0, a0, k_k, k_a, r_k.reshape(-1)] + [jnp.zeros_like(w0)] * 3).astype(F32)
    ww = jnp.zeros((LANES, WIDTH), F32).at[:DECAY_LORA].set(w_lora_up)
    wa = jnp.zeros((LANES, WIDTH), F32).at[DECAY_LORA:DECAY_LORA + AAA_LORA].set(a_lora_up)
    wg = jnp.zeros((2 * LANES, WIDTH), F32).at[:GATE_LORA].set(g_lora_up)
    hid = jnp.arange(WIDTH) // HEAD_DIM
    bd = (hid[:, None] == hid[None, :]).astype(F32)
    tix = jnp.arange(ts)
    tri = ((tix[:, None] // chunk == tix[None, :] // chunk) & (tix[None, :] <= tix[:, None])).astype(F32)
    c0 = 0
    big = jax.ShapeDtypeStruct((bsz, seq, WIDTH), F32)
    wspec = lambda shape: pl.BlockSpec(shape, lambda b, i: (0, 0))
    ospec = pl.BlockSpec((1, ts, WIDTH), lambda b, i: (b, i, 0))
    return pl.pallas_call(
        functools.partial(_rwkv_prep_kernel, chunk=chunk),
        grid=(bsz, seq // ts),
        in_specs=[
            pl.BlockSpec((1, ts, WIDTH), lambda b, i: (b, i, c0)),
            pl.BlockSpec((1, ts, WIDTH), lambda b, i: (b, i, c0 + 1)),
            pl.BlockSpec((1, ts, WIDTH), lambda b, i: (b, i, c0 + 2)),
            pl.BlockSpec((1, ts, WIDTH), lambda b, i: (b, i, c0 + 3)),
            wspec((1, COL_B)), wspec((8, WIDTH)), wspec((LANES, WIDTH)), wspec((LANES, WIDTH)),
            wspec((2 * LANES, WIDTH)), wspec((WIDTH, WIDTH)), wspec((ts, ts)),
        ],
        out_specs=[ospec] * 7 + [pl.BlockSpec((1, ts // chunk, WIDTH), lambda b, i: (b, i, 0))],
        out_shape=[jax.ShapeDtypeStruct((bsz, seq, WIDTH), BF16)] * 5 + [big] * 2
        + [jax.ShapeDtypeStruct((bsz, seq // chunk, WIDTH), F32)],
        scratch_shapes=[pltpu.VMEM((8, COL_B), F32)],
        compiler_params=_cparams(("parallel", "arbitrary")),
        name="rwkv_prep",
    )(p3d, p3d, p3d, p3d, mu, vec, ww, wa, wg, bd, tri)


def _rwkv_scan_kernel(rt_ref, kt_ref, kd_ref, bd_ref, v_ref, g_ref, bonus_ref, pend_ref, ln_ref, o_ref,
                      state_ref, *, chunk, cps, prec):
    @pl.when(pl.program_id(1) == 0)
    def _():
        state_ref[...] = jnp.zeros_like(state_ref)

    c2 = 2 * chunk
    lane = lax.broadcasted_iota(jnp.int32, (chunk, LANES), 1)
    first = lane < HEAD_DIM
    row = lax.broadcasted_iota(jnp.int32, (c2, c2), 0)
    col = lax.broadcasted_iota(jnp.int32, (c2, c2), 1)
    eye = (row == col).astype(F32)
    hrow = lax.broadcasted_iota(jnp.int32, (LANES, LANES), 0) // HEAD_DIM
    hcol = lax.broadcasted_iota(jnp.int32, (LANES, LANES), 1) // HEAD_DIM
    head_mean = jnp.where(hrow == hcol, 1.0 / HEAD_DIM, 0.0).astype(F32)
    nt = (((1,), (1,)), ((), ()))
    tn = (((0,), (0,)), ((), ()))
    dot = functools.partial(jnp.dot, precision=prec, preferred_element_type=F32)
    dotg = functools.partial(lax.dot_general, precision=prec, preferred_element_type=F32)

    def stack(x):
        return jnp.concatenate([jnp.where(first, x, 0.0), jnp.where(first, 0.0, x)], axis=0)

    pairs = range(PAIRS)
    units = [(j, hp) for j in range(cps) for hp in pairs]
    sls = [slice(hp * LANES, (hp + 1) * LANES) for hp in pairs]
    rows_of = [slice(j * chunk, (j + 1) * chunk) for j in range(cps)]
    rs, ks, kds, bs, vs = ({(j, hp): stack(ref[0, rows_of[j], sls[hp]].astype(F32)) for j, hp in units}
                           for ref in (rt_ref, kt_ref, kd_ref, bd_ref, v_ref))
    big = {u: dotg(jnp.concatenate([ks[u], rs[u]], axis=0), jnp.concatenate([bs[u], kds[u]], axis=0), nt)
           for u in units}
    a_b = {u: jnp.where(row > col, big[u][0:c2, 0:c2], 0.0) for u in units}
    a_k = {u: jnp.where(row > col, big[u][0:c2, c2:], 0.0) for u in units}
    a_rb = {u: jnp.where(row >= col, big[u][c2:, 0:c2], 0.0) for u in units}
    a_rk = {u: jnp.where(row >= col, big[u][c2:, c2:], 0.0) for u in units}
    av = {u: dot(jnp.concatenate([a_k[u], a_rk[u]], axis=0), vs[u]) for u in units}
    vk = {u: dotg(vs[u], kds[u], tn) for u in units}
    inv = {u: eye - a_b[u] for u in units}
    pw = {u: dot(a_b[u], a_b[u]) for u in units}
    n_sq = int(math.log2(chunk)) - 1
    for lvl in range(n_sq):
        if lvl + 1 < n_sq:
            both = {u: dot(jnp.concatenate([inv[u], pw[u]], axis=0), pw[u]) for u in units}
            inv = {u: inv[u] + both[u][0:c2] for u in units}
            pw = {u: both[u][c2:] for u in units}
        else:
            inv = {u: inv[u] + dot(inv[u], pw[u]) for u in units}
    hts = [state_ref[0, hp] for hp in pairs]
    for j in range(cps):
        kh = [dotg(jnp.concatenate([ks[j, hp], rs[j, hp]], axis=0), hts[hp], nt) for hp in pairs]
        us = [dot(inv[j, hp], kh[hp][0:c2] + av[j, hp][0:c2]) for hp in pairs]
        ub = [dotg(us[hp], bs[j, hp], tn) for hp in pairs]
        au = [dot(a_rb[j, hp], us[hp]) for hp in pairs]
        for hp in pairs:
            sl = sls[hp]
            pend = pend_ref[0, j, 0:1, sl]
            hts[hp] = (hts[hp] + vk[j, hp] - ub[hp]) * pend
            os_ = kh[hp][c2:] + av[j, hp][c2:] - au[hp]
            o = os_[0:chunk] + os_[chunk:]
            mu = jnp.dot(o, head_mean, precision=HI, preferred_element_type=F32)
            d = o - mu
            var = jnp.dot(d * d, head_mean, precision=HI, preferred_element_type=F32)
            on = d * lax.rsqrt(var + GN_EPS) * ln_ref[0:1, sl] + ln_ref[1:2, sl]
            o_ref[0, rows_of[j], sl] = ((on + bonus_ref[0, rows_of[j], sl]) * g_ref[0, rows_of[j], sl]
                                        ).astype(o_ref.dtype)
    for hp in pairs:
        state_ref[0, hp] = hts[hp]


def rwkv_scan(rt, kt, kd, bd, v, g, bonus, pend, lnx_g, lnx_b, *, prec=None):
    bsz, seq, _ = rt.shape
    chunk = RWKV_CHUNK
    n_chunks = seq // chunk
    ln = jnp.stack([lnx_g, lnx_b] + [jnp.zeros_like(lnx_g)] * 6).astype(F32)
    pend4 = pend.reshape(bsz, n_chunks, 1, WIDTH)
    cps = RWKV_CHUNKS_PER_STEP if n_chunks % RWKV_CHUNKS_PER_STEP == 0 else 1
    spec = pl.BlockSpec((1, cps * chunk, WIDTH), lambda b, c: (b, c, 0))
    return pl.pallas_call(
        functools.partial(_rwkv_scan_kernel, chunk=chunk, cps=cps, prec=prec),
        grid=(bsz, n_chunks // cps),
        in_specs=[spec] * 7 + [
            pl.BlockSpec((1, cps, 1, WIDTH), lambda b, c: (b, c, 0, 0)),
            pl.BlockSpec((8, WIDTH), lambda b, c: (0, 0)),
        ],
        out_specs=spec,
        out_shape=jax.ShapeDtypeStruct((bsz, seq, WIDTH), BF16),
        scratch_shapes=[pltpu.VMEM((1, PAIRS, LANES, LANES), F32)],
        compiler_params=_cparams(("parallel", "arbitrary")),
        name="rwkv_scan",
    )(rt, kt, kd, bd, v, g, bonus, pend4, ln)


def _merge_kernel(x_ref, oa_ref, ob_ref, ga_ref, gb_ref, wa_ref, wb_ref, wo_ref, g2_ref,
                  h_ref, xn_ref, acc_ref):
    j = pl.program_id(1)

    @pl.when(j == 0)
    def _():
        acc_ref[...] = x_ref[...]

    ya = jnp.dot(oa_ref[...].astype(BF16), wa_ref[...], preferred_element_type=F32)
    yb = jnp.dot(ob_ref[...].astype(BF16), wb_ref[...], preferred_element_type=F32)
    y = jax.nn.sigmoid(ga_ref[...].astype(F32)) * ya + jax.nn.sigmoid(gb_ref[...].astype(F32)) * yb
    acc_ref[...] += jnp.dot(y.astype(BF16), wo_ref[...], preferred_element_type=F32)

    @pl.when(j == pl.num_programs(1) - 1)
    def _():
        h = acc_ref[...]
        h_ref[...] = h
        ms = jnp.mean(h * h, axis=-1, keepdims=True)
        xn_ref[...] = _pack_halves(h * lax.rsqrt(ms + RMS_EPS) * g2_ref[...])


def _pack_halves(x):
    half = x.shape[1] // 2
    lo = lax.bitcast_convert_type(x[:, :half].astype(BF16).astype(F32), jnp.int32)
    hi = lax.bitcast_convert_type(x[:, half:].astype(BF16).astype(F32), jnp.int32)
    return lax.bitwise_or(lax.shift_right_logical(lo, jnp.int32(16)), hi)


def _unpack_halves(words):
    lo, hi = _unpack_words(words)
    return jnp.concatenate([lo, hi], axis=1)


def merge_out(x2d, oa, ob, p2d, w_proj_a, w_proj_b, w_out, norm2_g, *, row0=0, tm=1024):
    t, d = oa.shape[0], x2d.shape[1]
    r0 = row0 // tm
    tn = WIDTH
    nj = d // tn
    g0 = 0
    return pl.pallas_call(
        _merge_kernel,
        grid=(t // tm, nj),
        in_specs=[
            pl.BlockSpec((tm, d), lambda i, j: (r0 + i, 0)),
            pl.BlockSpec((tm, WIDTH), lambda i, j: (i, 0)),
            pl.BlockSpec((tm, WIDTH), lambda i, j: (i, 0)),
            pl.BlockSpec((tm, tn), lambda i, j: (i, g0 + j)),
            pl.BlockSpec((tm, tn), lambda i, j: (i, g0 + nj + j)),
            pl.BlockSpec((WIDTH, tn), lambda i, j: (0, j)),
            pl.BlockSpec((WIDTH, tn), lambda i, j: (0, j)),
            pl.BlockSpec((tn, d), lambda i, j: (j, 0)),
            pl.BlockSpec((1, d), lambda i, j: (0, 0)),
        ],
        out_specs=[pl.BlockSpec((tm, d), lambda i, j: (i, 0)), pl.BlockSpec((tm, d // 2), lambda i, j: (i, 0))],
        out_shape=[jax.ShapeDtypeStruct((t, d), F32), jax.ShapeDtypeStruct((t, d // 2), jnp.int32)],
        scratch_shapes=[pltpu.VMEM((tm, d), F32)],
        compiler_params=_cparams(("parallel", "arbitrary")),
        name="merge_out",
    )(x2d, oa, ob, p2d, p2d, w_proj_a.astype(BF16), w_proj_b.astype(BF16), w_out.astype(BF16),
      norm2_g.reshape(1, d))


PEER_HEADS = 8
PEER_NKEYS = 128
PEER_TOPK = 16
PEER_HALF = 128


def _topk_rows(s, k):
    n = s.shape[0]
    rows = lax.broadcasted_iota(jnp.int32, s.shape, 0).astype(F32)
    vals, ids = [], []
    for _ in range(k):
        m = jnp.max(s, axis=0, keepdims=True)
        first = jnp.min(jnp.where(s == m, rows, float(n)), axis=0, keepdims=True)
        vals.append(m)
        ids.append(first)
        s = jnp.where(rows == first, -jnp.inf, s)
    return jnp.concatenate(vals, axis=0), jnp.concatenate(ids, axis=0)


def _take_rows(table, ids):
    rows = lax.broadcasted_iota(jnp.int32, table.shape, 0).astype(F32)
    return jnp.sum(jnp.where(rows == ids, table, 0.0), axis=0, keepdims=True)


def _peer_route_kernel(xn_ref, wq_ref, sk_ref, idx_ref, gate_ref, *, prec):
    tt = xn_ref.shape[0]
    k = PEER_TOPK
    xn = _unpack_halves(xn_ref[...]) if xn_ref.dtype == jnp.int32 else xn_ref[...]
    q = jnp.dot(xn.astype(wq_ref.dtype), wq_ref[...], precision=prec, preferred_element_type=F32)
    nt = (((1,), (1,)), ((), ()))
    idx_rows, gate_rows = [], []
    half = k // 2
    for h in range(PEER_HEADS):
        tops = []
        for p in range(2):
            c0 = (h * 2 + p) * PEER_HALF
            s = lax.dot_general(sk_ref[h, p].astype(wq_ref.dtype), q[:, c0:c0 + PEER_HALF].astype(wq_ref.dtype),
                                nt, precision=prec, preferred_element_type=F32)
            tops.append(_topk_rows(s, k))
        (s0, i0), (s1, i1) = tops
        cs = [s0[0:1] + s1] + [s0[i:i + 1] + s1[0:half] for i in range(1, half)] + [s0[half:] + s1[0:1]]
        best_s, pos = _topk_rows(jnp.concatenate(cs, axis=0), k)
        mid = jnp.floor((pos - k) * (1.0 / half))
        end_mid = float(k + (half - 1) * half)
        i_rank = jnp.where(pos < k, 0.0, jnp.where(pos < end_mid, 1.0 + mid, pos - (end_mid - half)))
        j_rank = jnp.where(pos < k, pos, jnp.where(pos < end_mid, (pos - k) - half * mid, 0.0))
        ids = [_take_rows(i0, i_rank[n:n + 1]) * PEER_NKEYS + _take_rows(i1, j_rank[n:n + 1]) for n in range(k)]
        e = jnp.exp(best_s - best_s[0:1])
        gate_rows.append(e / jnp.sum(e, axis=0, keepdims=True))
        idx_rows.append(jnp.concatenate(ids, axis=0).astype(jnp.int32))
    idx_ref[...] = jnp.concatenate(idx_rows, axis=0).T
    gate_ref[...] = jnp.concatenate(gate_rows, axis=0).T


def peer_route(xn2d, peer_wq, peer_subkeys, *, tt=256, prec=None, wdtype=BF16):
    t, dx = xn2d.shape
    d, nq = peer_wq.shape
    n_sel = PEER_HEADS * PEER_TOPK
    return pl.pallas_call(
        functools.partial(_peer_route_kernel, prec=prec),
        grid=(t // tt,),
        in_specs=[
            pl.BlockSpec((tt, dx), lambda i: (i, 0)),
            pl.BlockSpec((d, nq), lambda i: (0, 0)),
            pl.BlockSpec((PEER_HEADS, 2, PEER_NKEYS, PEER_HALF), lambda i: (0, 0, 0, 0)),
        ],
        out_specs=[pl.BlockSpec((tt, n_sel), lambda i: (i, 0))] * 2,
        out_shape=[jax.ShapeDtypeStruct((t, n_sel), jnp.int32), jax.ShapeDtypeStruct((t, n_sel), F32)],
        compiler_params=_cparams(("parallel",)),
        name="peer_route",
    )(xn2d, peer_wq.astype(wdtype), peer_subkeys)


def _final_kernel(h_ref, y_ref, g_ref, *rest):
    o_ref = rest[-1]
    h = h_ref[...] + y_ref[...]
    ms = jnp.mean(h * h, axis=-1, keepdims=True)
    o_ref[...] = h * lax.rsqrt(ms + RMS_EPS) * g_ref[...]


def final_norm(h2d, y2d, g, *, out=None, row0=0, total_rows=None, tm=1024):
    t, d = h2d.shape
    total = t if total_rows is None else total_rows
    r0 = row0 // tm
    spec = pl.BlockSpec((tm, d), lambda i: (i, 0))
    in_specs = [spec, spec, pl.BlockSpec((1, d), lambda i: (0, 0))]
    args = [h2d, y2d, g.reshape(1, d)]
    aliases = {}
    if out is not None:
        in_specs.append(pl.BlockSpec(memory_space=pl.ANY))
        args.append(out)
        aliases = {3: 0}
    return pl.pallas_call(
        _final_kernel,
        grid=(t // tm,),
        in_specs=in_specs,
        out_specs=pl.BlockSpec((tm, d), lambda i: (r0 + i, 0)),
        out_shape=jax.ShapeDtypeStruct((total, d), F32),
        input_output_aliases=aliases,
        compiler_params=_cparams(("parallel",)),
        name="final_norm",
    )(*args)


SC_CORES = 2
SC_SUBCORES = 16
SC_LANES = 16
SC_WORKERS = SC_CORES * SC_SUBCORES
PEER_SEL = PEER_HEADS * PEER_TOPK
PEER_ROWS = 32
PEER_PARTS = PEER_SEL // PEER_ROWS
PEER_NBUF = 4
PEER_GROUP = 64
PEER_BF16_RUN = 4
PEER_ROW_PAIR = 4


def _pack_rows_kernel(w_ref, o_ref):
    o_ref[...] = _pack_halves(w_ref[...])


def _pack_rows(w, *, tr=1024):
    e, d = w.shape
    return pl.pallas_call(
        _pack_rows_kernel,
        grid=(e // tr,),
        in_specs=[pl.BlockSpec((tr, d), lambda i: (i, 0))],
        out_specs=pl.BlockSpec((tr, d // 2), lambda i: (i, 0)),
        out_shape=jax.ShapeDtypeStruct((e, d // 2), jnp.int32),
        compiler_params=_cparams(("parallel",)),
        name="pack_rows",
    )(w)


def _unpack_words(w):
    lo = lax.bitcast_convert_type(lax.shift_left(w, jnp.int32(16)), F32)
    hi = lax.bitcast_convert_type(lax.bitwise_and(w, jnp.int32(-65536)), F32)
    return lo, hi


def _packed_dot(a_words, b_words):
    from jax.experimental.pallas import tpu_sc as plsc
    prods = [plsc.bitcast(a, BF16) * plsc.bitcast(b, BF16) for a, b in zip(a_words, b_words)]
    while len(prods) > 1:
        prods = [prods[k] + prods[k + 1] for k in range(0, len(prods), 2)]
    return _unpack_words(plsc.bitcast(prods[0], jnp.int32))


def _sc_mesh():
    from jax.experimental.pallas import tpu_sc as plsc
    return plsc.VectorSubcoreMesh(core_axis_name="c", subcore_axis_name="s",
                                  num_cores=SC_CORES, num_subcores=SC_SUBCORES)


def _sc_loop(n, body, carry):
    from jax.experimental.pallas import tpu_sc as plsc
    return plsc.parallel_loop(0, n, carry=carry)(body)


def _worker_base(tokens_per_worker):
    return (lax.axis_index("s") * SC_CORES + lax.axis_index("c")) * tokens_per_worker


def _gather_compute_loop(table_hbm, idx_v, rows_v, sem, stage_v, out_row, osem, grp, compute):
    n_gathers = PEER_PARTS * grp
    ahead = PEER_NBUF - 1

    def gather(j, b):
        i = j // PEER_PARTS if isinstance(j, int) else lax.shift_right_logical(j, PEER_PARTS.bit_length() - 1)
        h = j % PEER_PARTS if isinstance(j, int) else lax.bitwise_and(j, PEER_PARTS - 1)
        ids = idx_v.at[i, pl.ds(pl.multiple_of(h * PEER_ROWS, PEER_ROWS), PEER_ROWS)]
        return pltpu.make_async_copy(table_hbm.at[ids], rows_v.at[b], sem.at[b])

    def put(i, slot):
        return pltpu.make_async_copy(stage_v.at[slot], out_row(i), osem.at[slot])

    for j in range(ahead):
        gather(j, j).start()

    @pl.loop(0, n_gathers)
    def _(j):
        b = lax.bitwise_and(j, PEER_NBUF - 1)
        h = lax.bitwise_and(j, PEER_PARTS - 1)
        i = lax.shift_right_logical(j, PEER_PARTS.bit_length() - 1)
        slot = lax.bitwise_and(i, 1)

        @pl.when((h == 0) & (i >= 2))
        def _():
            put(i - 2, slot).wait()

        @pl.when(j + ahead < n_gathers)
        def _():
            gather(j + ahead, lax.bitwise_and(j + ahead, PEER_NBUF - 1)).start()

        gather(j, b).wait()
        compute(i, h, b, slot)

        @pl.when(h == PEER_PARTS - 1)
        def _():
            put(i, slot).start()

    put(grp - 2, 0).wait()
    put(grp - 1, 1).wait()


def peer_expert_dots(x_packed, idx, u_packed):
    t, half = x_packed.shape
    n_chunks = half // SC_LANES
    tpw = t // SC_WORKERS
    igrp = min(2 * PEER_GROUP, tpw)
    grp = min(PEER_GROUP, igrp)
    assert t % SC_WORKERS == 0 and tpw % igrp == 0 and igrp % grp == 0 and grp & (grp - 1) == 0
    rows_tog = 2 * PEER_ROW_PAIR
    assert igrp % 2 == 0 and idx.shape == (t, PEER_SEL) and PEER_ROWS % rows_tog == 0

    def body(x_hbm, idx_hbm, u_hbm, out_hbm, idx_v, x_v, rows_v, ps_v, sem, osem):
        base = _worker_base(tpw)

        def compute(t0, i, h, b, slot):
            @pl.when((h == 0) & (lax.bitwise_and(i, grp - 1) == 0))
            def _():
                pltpu.sync_copy(x_hbm.at[pl.ds(pl.multiple_of(t0 + i, grp), grp)], x_v)

            ix = lax.bitwise_and(i, grp - 1)

            @pl.loop(0, PEER_ROWS // rows_tog)
            def _(rg):
                r0 = rg * rows_tog
                accs = [[None, None] for _ in range(rows_tog)]
                for c0 in range(0, n_chunks, PEER_BF16_RUN):
                    ats = [pl.ds((c0 + k) * SC_LANES, SC_LANES) for k in range(PEER_BF16_RUN)]
                    xw = [x_v[ix, at] for at in ats]
                    for r in range(rows_tog):
                        terms = _packed_dot([rows_v[b, r0 + r, at] for at in ats], xw)
                        for k, term in enumerate(terms):
                            accs[r][k] = term if accs[r][k] is None else accs[r][k] + term
                pair = rows_tog // 2
                bits = [lax.bitcast_convert_type(accs[r][0] + accs[r][1], jnp.int32) + jnp.int32(0x8000)
                        for r in range(rows_tog)]
                for r in range(pair):
                    word = lax.bitwise_or(lax.shift_right_logical(bits[r], jnp.int32(16)),
                                          lax.bitwise_and(bits[r + pair], jnp.int32(-65536)))
                    w0 = lax.shift_right_logical(h * PEER_ROWS + r0, 1) + r
                    ps_v[slot, pl.ds(pl.multiple_of(w0 * SC_LANES, SC_LANES), SC_LANES)] = word

        @pl.loop(0, tpw // igrp)
        def _(g):
            t0 = base + g * igrp
            pltpu.sync_copy(idx_hbm.at[pl.ds(t0, igrp)], idx_v)
            _gather_compute_loop(u_hbm, idx_v, rows_v, sem, ps_v, lambda i: out_hbm.at[t0 + i], osem, igrp,
                                 functools.partial(compute, t0))

    return pl.kernel(
        body,
        out_type=jax.ShapeDtypeStruct((t, PEER_SEL * SC_LANES // 2), jnp.int32),
        mesh=_sc_mesh(),
        scratch_types=[
            pltpu.VMEM((igrp, PEER_SEL), jnp.int32),
            pltpu.VMEM((grp, half), jnp.int32),
            pltpu.VMEM((PEER_NBUF, PEER_ROWS, half), jnp.int32),
            pltpu.VMEM((2, PEER_SEL * SC_LANES // 2), jnp.int32),
            pltpu.SemaphoreType.DMA((PEER_NBUF,)),
            pltpu.SemaphoreType.DMA((2,)),
        ],
        compiler_params=pltpu.CompilerParams(needs_layout_passes=False),
        name="peer_expert_dots",
    )(x_packed, idx, u_packed)


def peer_expert_mix(hgw, idx, v_packed):
    t = hgw.shape[0]
    half = v_packed.shape[1]
    d = 2 * half
    tpw = t // SC_WORKERS
    grp = min(2 * PEER_GROUP, tpw)
    assert t % SC_WORKERS == 0 and tpw % grp == 0 and grp % 2 == 0 and idx.shape == (t, PEER_SEL)
    n_parts = 2
    cpp = half // SC_LANES // n_parts
    from jax.experimental.pallas import tpu_sc as plsc

    def body(hg_hbm, idx_hbm, v_hbm, out_hbm, idx_v, hg_v, rows_v, o_v2, sem, osem):
        base = _worker_base(tpw)

        def compute(i, h, b, slot):
            token = jnp.full((SC_LANES,), i, jnp.int32)
            for part in range(n_parts):
                def rbody(rq, accs):
                    r0 = rq * PEER_BF16_RUN
                    s = [plsc.load_gather(hg_v, [token, jnp.full((SC_LANES,), h * PEER_ROWS + r0 + k, jnp.int32)])
                         for k in range(PEER_BF16_RUN)]
                    new = []
                    for c in range(cpp):
                        at = pl.ds((part * cpp + c) * SC_LANES, SC_LANES)
                        lo, hi = _packed_dot([rows_v[b, r0 + k, at] for k in range(PEER_BF16_RUN)], s)
                        new.append(accs[2 * c] + lo)
                        new.append(accs[2 * c + 1] + hi)
                    return tuple(new)

                accs = _sc_loop(PEER_ROWS // PEER_BF16_RUN, rbody,
                                tuple(jnp.zeros((SC_LANES,), F32) for _ in range(2 * cpp)))
                def store(overwrite):
                    for c in range(cpp):
                        lo_at = pl.ds((part * cpp + c) * SC_LANES, SC_LANES)
                        hi_at = pl.ds(half + (part * cpp + c) * SC_LANES, SC_LANES)
                        if overwrite:
                            o_v2[slot, lo_at] = accs[2 * c]
                            o_v2[slot, hi_at] = accs[2 * c + 1]
                        else:
                            o_v2[slot, lo_at] = o_v2[slot, lo_at] + accs[2 * c]
                            o_v2[slot, hi_at] = o_v2[slot, hi_at] + accs[2 * c + 1]

                pl.when(h == 0)(functools.partial(store, True))
                pl.when(h != 0)(functools.partial(store, False))

        @pl.loop(0, tpw // grp)
        def _(g):
            t0 = base + g * grp
            pltpu.sync_copy(idx_hbm.at[pl.ds(t0, grp)], idx_v)
            pltpu.sync_copy(hg_hbm.at[pl.ds(t0, grp)], hg_v)
            _gather_compute_loop(v_hbm, idx_v, rows_v, sem, o_v2, lambda i: out_hbm.at[t0 + i], osem, grp, compute)

    return pl.kernel(
        body,
        out_type=jax.ShapeDtypeStruct((t, d), F32),
        mesh=_sc_mesh(),
        scratch_types=[
            pltpu.VMEM((grp, PEER_SEL), jnp.int32),
            pltpu.VMEM((grp, PEER_SEL), jnp.int32),
            pltpu.VMEM((PEER_NBUF, PEER_ROWS, half), jnp.int32),
            pltpu.VMEM((2, d), F32),
            pltpu.SemaphoreType.DMA((PEER_NBUF,)),
            pltpu.SemaphoreType.DMA((2,)),
        ],
        compiler_params=pltpu.CompilerParams(needs_layout_passes=False),
        name="peer_expert_mix",
    )(hgw, idx, v_packed)


def _peer_act_kernel(ps_ref, gate_ref, lo_ref, hi_ref, o_ref):
    lo, hi = _unpack_words(ps_ref[...])
    pre = (jnp.dot(lo.astype(BF16), lo_ref[...], preferred_element_type=F32)
           + jnp.dot(hi.astype(BF16), hi_ref[...], preferred_element_type=F32))
    hg = 0.5 * pre * (1.0 + lax.erf(pre * (1.0 / math.sqrt(2.0)))) * gate_ref[...]
    bits = lax.bitcast_convert_type(hg.astype(BF16).astype(F32), jnp.int32)
    o_ref[...] = lax.bitwise_or(bits, lax.shift_right_logical(bits, jnp.int32(16)))


def peer_act(ps, gates, *, tm=1024):
    t, n = ps.shape
    reg = jnp.arange(n) // SC_LANES
    slot_lo = 2 * PEER_ROW_PAIR * (reg // PEER_ROW_PAIR) + reg % PEER_ROW_PAIR
    place_lo = (slot_lo[:, None] == jnp.arange(PEER_SEL)[None, :]).astype(BF16)
    place_hi = ((slot_lo + PEER_ROW_PAIR)[:, None] == jnp.arange(PEER_SEL)[None, :]).astype(BF16)
    return pl.pallas_call(
        _peer_act_kernel,
        grid=(t // tm,),
        in_specs=[
            pl.BlockSpec((tm, n), lambda i: (i, 0)),
            pl.BlockSpec((tm, PEER_SEL), lambda i: (i, 0)),
            pl.BlockSpec((n, PEER_SEL), lambda i: (0, 0)),
            pl.BlockSpec((n, PEER_SEL), lambda i: (0, 0)),
        ],
        out_specs=pl.BlockSpec((tm, PEER_SEL), lambda i: (i, 0)),
        out_shape=jax.ShapeDtypeStruct((t, PEER_SEL), jnp.int32),
        compiler_params=_cparams(("parallel",)),
        name="peer_act",
    )(ps, gates, place_lo, place_hi)


BATCH_GROUPS = 8


def kernel(x, norm1_g, w_in, rwkv_mu, w0, w_lora_up, a0, a_lora_up, g_lora_up, k_k, k_a, r_k, lnx_g, lnx_b,
           w_proj_a, w_proj_b, w_out, norm2_g, peer_wq, peer_subkeys, peer_u, peer_v, rel_bias, normf_g):
    bsz, seq, d = x.shape
    depth = norm1_g.shape[0]
    groups = BATCH_GROUPS if bsz % BATCH_GROUPS == 0 else 1
    gb = bsz // groups
    tg = gb * seq
    t = bsz * seq
    src = x.reshape(t, d)
    for l in range(depth):
        w_pad = jnp.concatenate([
            w_in[l][:, :COL_A + COL_B_RAW],
            jnp.zeros((d, COL_B - COL_B_RAW), w_in.dtype),
            w_in[l][:, COL_A + COL_B_RAW:]], axis=1).astype(BF16)
        u_packed = _pack_rows(peer_u[l])
        v_packed = _pack_rows(peer_v[l])
        last = l == depth - 1

        def mix(pending, tie=None):
            row0, h2d, ps, gates, idx = pending
            hgw = peer_act(ps, gates)
            if tie is not None:
                tie, hgw = lax.optimization_barrier((tie, hgw))
            return tie, (row0, h2d, peer_expert_mix(hgw, idx, v_packed))

        outs = []

        def close(mixed):
            row0, h2d, y2d = mixed
            if last:
                outs.append(final_norm(h2d, y2d, normf_g, out=outs[-1] if outs else None, row0=row0, total_rows=t))
            else:
                outs.append(h2d + y2d)

        pending = closing = None
        for g in range(groups):
            pa, pb, pg = norm_proj(src, norm1_g[l], w_pad, row0=g * tg, rows=tg)
            oa = moba_attention(pa.reshape(gb, seq, -1), rel_bias)
            prep = tuple(rwkv_prep(pb.reshape(gb, seq, -1), rwkv_mu[l], w0[l], w_lora_up[l], a0[l], a_lora_up[l], g_lora_up[l],
                                   k_k[l], k_a[l], r_k[l]))
            mixed = None
            if pending is not None:
                (oa, prep), mixed = mix(pending, (oa, prep))
            if closing is not None:
                oa, y2d = lax.optimization_barrier((oa, closing[2]))
                close(closing[:2] + (y2d,))
                closing = None
            ob = rwkv_scan(*prep, lnx_g[l], lnx_b[l])
            h2d, xn2 = merge_out(src, oa.reshape(tg, WIDTH), ob.reshape(tg, WIDTH), pg, w_proj_a[l], w_proj_b[l],
                                 w_out[l], norm2_g[l], row0=g * tg)
            idx, gates = peer_route(xn2, peer_wq[l], peer_subkeys[l])
            if mixed is not None:
                idx, y2d = lax.optimization_barrier((idx, mixed[2]))
                closing = mixed[:2] + (y2d,)
            pending = (g * tg, h2d, peer_expert_dots(xn2, idx, u_packed), gates, idx)
        if closing is not None:
            close(closing)
        close(mix(pending)[1])
        src = outs[-1] if last else jnp.concatenate(outs, axis=0)
    return src.reshape(bsz, seq, d)
```
